```python
import math
import jax
import jax.numpy as jnp
from jax import lax
import numpy as np

D_MODEL = 1024
BATCH = 8
SEQ = 2048
DEPTH = 1

GDN_HEADS = 8
GDN_HEAD_DIM = 128
GDN_WIDTH = GDN_HEADS * GDN_HEAD_DIM
GDN_CONV = 4
CHUNK = 64
CONV_WIDTH = 1024
SHORT_CONV = 3
MIX_WIDTH = GDN_WIDTH + CONV_WIDTH
EPS = 1e-6

PROJ_SPLITS = (
    3 * GDN_WIDTH,
    GDN_WIDTH,
    GDN_HEADS,
    GDN_HEADS,
    CONV_WIDTH,
    CONV_WIDTH,
    CONV_WIDTH,
    CONV_WIDTH,
)
PROJ_WIDTH = sum(PROJ_SPLITS)

kernel_name = "hybrid_gdn_shortconv_block"


def rmsnorm(x, w):
    xf = x.astype(jnp.float32)
    xf = xf * lax.rsqrt(jnp.mean(xf * xf, axis=-1, keepdims=True) + EPS)
    return (xf * w.astype(jnp.float32)).astype(x.dtype)


def l2norm(x):
    return x * lax.rsqrt(jnp.sum(x * x, axis=-1, keepdims=True) + EPS)


def causal_depthwise_conv(x, w):
    K = w.shape[0]
    L = x.shape[1]
    xp = jnp.pad(x, ((0, 0), (K - 1, 0), (0, 0)))
    return sum(xp[:, j:j + L] * w[j] for j in range(K))


def gated_delta_rule_chunked(q, k, v, g, beta):
    Bsz, L, H, DK = q.shape
    DV = v.shape[-1]
    n = L // CHUNK

    def chunks(t):
        return t.reshape(Bsz, n, CHUNK, H, -1).transpose(0, 3, 1, 2, 4)

    q, k, v = chunks(q), chunks(k), chunks(v)
    g = g.reshape(Bsz, n, CHUNK, H).transpose(0, 3, 1, 2)
    beta = beta.reshape(Bsz, n, CHUNK, H).transpose(0, 3, 1, 2)
    g = jnp.cumsum(g, axis=-1)

    causal = jnp.tril(jnp.ones((CHUNK, CHUNK), dtype=bool))
    strict = jnp.tril(jnp.ones((CHUNK, CHUNK), dtype=bool), k=-1)
    decay = jnp.exp(jnp.where(causal, g[..., :, None] - g[..., None, :], -jnp.inf))

    k_beta = k * beta[..., None]
    v_beta = v * beta[..., None]
    A = jnp.where(strict, jnp.einsum('bhnid,bhnjd->bhnij', k_beta, k) * decay, 0.0)
    eye = jnp.eye(CHUNK, dtype=q.dtype)
    rhs = jnp.concatenate([v_beta, k_beta * jnp.exp(g)[..., None]], axis=-1)
    sol = lax.linalg.triangular_solve(eye + A, rhs, left_side=True, lower=True,
                                      unit_diagonal=True)
    u = sol[..., :DV]
    w = sol[..., DV:]

    attn_intra = jnp.where(causal, jnp.einsum('bhnid,bhnjd->bhnij', q, k) * decay, 0.0)
    g_last = g[..., -1]
    k_state = k * jnp.exp(g_last[..., None] - g)[..., None]
    q_decay = q * jnp.exp(g)[..., None]

    def step(S, inp):
        qd, w_c, u_c, a_c, ks, gl = inp
        v_new = u_c - jnp.einsum('bhck,bhkv->bhcv', w_c, S)
        o = jnp.einsum('bhck,bhkv->bhcv', qd, S) + jnp.einsum('bhij,bhjv->bhiv', a_c, v_new)
        S = S * jnp.exp(gl)[..., None, None] + jnp.einsum('bhck,bhcv->bhkv', ks, v_new)
        return S, o

    xs = tuple(jnp.moveaxis(t, 2, 0) for t in (q_decay, w, u, attn_intra, k_state, g_last))
    S0 = jnp.zeros((Bsz, H, DK, DV), dtype=q.dtype)
    _, o = lax.scan(step, S0, xs)
    return o.transpose(1, 0, 3, 2, 4).reshape(Bsz, L, H, DV)


def hybrid_layer(x, norm_in_w, w_in, conv_qkv_w, A_log, dt_bias, gdn_norm_w,
                 conv_w, conv_b, w_out):
    Bsz, L, _ = x.shape
    h = rmsnorm(x, norm_in_w)
    proj = h @ w_in
    split_at = [int(i) for i in np.cumsum(PROJ_SPLITS)[:-1]]
    qkv, z_g, b_g, a_g, gate_b, gate_c, h_c, z_c = jnp.split(proj, split_at, axis=-1)

    qkv = jax.nn.silu(causal_depthwise_conv(qkv, conv_qkv_w))
    q, k, v = jnp.split(qkv, 3, axis=-1)
    shp = (Bsz, L, GDN_HEADS, GDN_HEAD_DIM)
    q = l2norm(q.reshape(shp).astype(jnp.float32)) * (GDN_HEAD_DIM ** -0.5)
    k = l2norm(k.reshape(shp).astype(jnp.float32))
    v = v.reshape(shp).astype(jnp.float32)
    beta = jax.nn.sigmoid(b_g.astype(jnp.float32))
    g = -jnp.exp(A_log.astype(jnp.float32)) * jax.nn.softplus(
        a_g.astype(jnp.float32) + dt_bias.astype(jnp.float32))
    o = gated_delta_rule_chunked(q, k, v, g, beta).astype(x.dtype)
    o = rmsnorm(o, gdn_norm_w) * jax.nn.silu(z_g.reshape(shp))
    o = o.reshape(Bsz, L, GDN_WIDTH)

    y_c = gate_b * (causal_depthwise_conv(gate_c * h_c, conv_w) + conv_b)
    y_c = y_c * jax.nn.silu(z_c)

    mix = jnp.concatenate([o, y_c], axis=-1)
    return x + mix @ w_out


def _fwd_setup_inputs(seed: int = 0) -> dict:
    key = jax.random.key(seed)
    ks = jax.random.split(key, 12)
    f32 = jnp.float32
    x = jax.random.normal(ks[0], (BATCH, SEQ, D_MODEL), f32)
    norm_in_w = 1.0 + 0.02 * jax.random.normal(ks[1], (DEPTH, D_MODEL), f32)
    w_in = jax.random.normal(ks[2], (DEPTH, D_MODEL, PROJ_WIDTH), f32) * D_MODEL ** -0.5
    conv_qkv_w = jax.random.normal(ks[3], (DEPTH, GDN_CONV, 3 * GDN_WIDTH), f32) * GDN_CONV ** -0.5
    A_log = jnp.log(jax.random.uniform(ks[4], (DEPTH, GDN_HEADS), f32, minval=1.0, maxval=16.0))
    dt = jnp.exp(jax.random.uniform(ks[5], (DEPTH, GDN_HEADS), f32,
                                    minval=math.log(1e-3), maxval=math.log(1e-1)))
    dt_bias = dt + jnp.log(-jnp.expm1(-dt))
    gdn_norm_w = 1.0 + 0.02 * jax.random.normal(ks[6], (DEPTH, GDN_HEAD_DIM), f32)
    conv_w = jax.random.normal(ks[7], (DEPTH, SHORT_CONV, CONV_WIDTH), f32) * SHORT_CONV ** -0.5
    conv_b = 0.01 * jax.random.normal(ks[8], (DEPTH, CONV_WIDTH), f32)
    w_out = jax.random.normal(ks[9], (DEPTH, MIX_WIDTH, D_MODEL), f32) * MIX_WIDTH ** -0.5
    final_norm_w = 1.0 + 0.02 * jax.random.normal(ks[10], (D_MODEL,), f32)
    return {"x": x, "norm_in_w": norm_in_w, "w_in": w_in, "conv_qkv_w": conv_qkv_w,
            "A_log": A_log, "dt_bias": dt_bias, "gdn_norm_w": gdn_norm_w,
            "conv_w": conv_w, "conv_b": conv_b, "w_out": w_out,
            "final_norm_w": final_norm_w}


def _fwd_reference(x, norm_in_w, w_in, conv_qkv_w, A_log, dt_bias, gdn_norm_w,
              conv_w, conv_b, w_out, final_norm_w):
    for layer in range(DEPTH):
        x = hybrid_layer(x, norm_in_w[layer], w_in[layer], conv_qkv_w[layer],
                         A_log[layer], dt_bias[layer], gdn_norm_w[layer],
                         conv_w[layer], conv_b[layer], w_out[layer])
    return rmsnorm(x, final_norm_w)


import jax as _jax
import jax.numpy as _jnp

TWIN_FORMAT = 'train_step'
FWD_PARAMS = ['x', 'norm_in_w', 'w_in', 'conv_qkv_w', 'A_log', 'dt_bias', 'gdn_norm_w', 'conv_w', 'conv_b', 'w_out', 'final_norm_w']
TWIN_WEIGHTS = ['norm_in_w', 'w_in', 'conv_qkv_w', 'A_log', 'dt_bias', 'gdn_norm_w', 'conv_w', 'conv_b', 'w_out', 'final_norm_w']
TWIN_DIFF_INPUT = 'x'
TWIN_INPUTS = ['x', 'norm_in_w', 'w_in', 'conv_qkv_w', 'A_log', 'dt_bias', 'gdn_norm_w', 'conv_w', 'conv_b', 'w_out', 'final_norm_w', 'loss_target', 'm_norm_in_w', 'm_w_in', 'm_conv_qkv_w', 'm_A_log', 'm_dt_bias', 'm_gdn_norm_w', 'm_conv_w', 'm_conv_b', 'm_w_out', 'm_final_norm_w', 'v_norm_in_w', 'v_w_in', 'v_conv_qkv_w', 'v_A_log', 'v_dt_bias', 'v_gdn_norm_w', 'v_conv_w', 'v_conv_b', 'v_w_out', 'v_final_norm_w']
TWIN_OUTPUTS = ['loss', 'grad_x', 'grad_norm_in_w', 'grad_w_in', 'grad_conv_qkv_w', 'grad_A_log', 'grad_dt_bias', 'grad_gdn_norm_w', 'grad_conv_w', 'grad_conv_b', 'grad_w_out', 'grad_final_norm_w', 'delta_norm_in_w', 'delta_w_in', 'delta_conv_qkv_w', 'delta_A_log', 'delta_dt_bias', 'delta_gdn_norm_w', 'delta_conv_w', 'delta_conv_b', 'delta_w_out', 'delta_final_norm_w', 'new_m_norm_in_w', 'new_m_w_in', 'new_m_conv_qkv_w', 'new_m_A_log', 'new_m_dt_bias', 'new_m_gdn_norm_w', 'new_m_conv_w', 'new_m_conv_b', 'new_m_w_out', 'new_m_final_norm_w', 'new_v_norm_in_w', 'new_v_w_in', 'new_v_conv_qkv_w', 'new_v_A_log', 'new_v_dt_bias', 'new_v_gdn_norm_w', 'new_v_conv_w', 'new_v_conv_b', 'new_v_w_out', 'new_v_final_norm_w']
TWIN_LEAF_KINDS = {'loss': 'loss', 'grad_x': 'grad_x', 'grad_norm_in_w': 'grad_w', 'grad_w_in': 'grad_w', 'grad_conv_qkv_w': 'grad_w', 'grad_A_log': 'grad_w', 'grad_dt_bias': 'grad_w', 'grad_gdn_norm_w': 'grad_w', 'grad_conv_w': 'grad_w', 'grad_conv_b': 'grad_w', 'grad_w_out': 'grad_w', 'grad_final_norm_w': 'grad_w', 'delta_norm_in_w': 'delta_w', 'delta_w_in': 'delta_w', 'delta_conv_qkv_w': 'delta_w', 'delta_A_log': 'delta_w', 'delta_dt_bias': 'delta_w', 'delta_gdn_norm_w': 'delta_w', 'delta_conv_w': 'delta_w', 'delta_conv_b': 'delta_w', 'delta_w_out': 'delta_w', 'delta_final_norm_w': 'delta_w', 'new_m_norm_in_w': 'new_m', 'new_m_w_in': 'new_m', 'new_m_conv_qkv_w': 'new_m', 'new_m_A_log': 'new_m', 'new_m_dt_bias': 'new_m', 'new_m_gdn_norm_w': 'new_m', 'new_m_conv_w': 'new_m', 'new_m_conv_b': 'new_m', 'new_m_w_out': 'new_m', 'new_m_final_norm_w': 'new_m', 'new_v_norm_in_w': 'new_v', 'new_v_w_in': 'new_v', 'new_v_conv_qkv_w': 'new_v', 'new_v_A_log': 'new_v', 'new_v_dt_bias': 'new_v', 'new_v_gdn_norm_w': 'new_v', 'new_v_conv_w': 'new_v', 'new_v_conv_b': 'new_v', 'new_v_w_out': 'new_v', 'new_v_final_norm_w': 'new_v'}


def _forward(args):
    return _fwd_reference(*[args[k] for k in FWD_PARAMS])


def _output_shape():
    out = _jax.eval_shape(lambda: _forward(_fwd_setup_inputs(0)))
    return out.shape, out.dtype

N_MICROBATCH = 1
ADAM_LR = 0.001
ADAM_B1 = 0.9
ADAM_B2 = 0.999
ADAM_EPS = 1e-08
ADAM_WD = 0.01
ADAM_STEP = 10
PER_EXAMPLE_BATCH_AXIS = {'x': 0, 'loss_target': 0}
SHARED_INPUTS = []
_WEIGHT_DTYPES = {'norm_in_w': _jnp.float32, 'w_in': _jnp.float32, 'conv_qkv_w': _jnp.float32, 'A_log': _jnp.float32, 'dt_bias': _jnp.float32, 'gdn_norm_w': _jnp.float32, 'conv_w': _jnp.float32, 'conv_b': _jnp.float32, 'w_out': _jnp.float32, 'final_norm_w': _jnp.float32}
MOMENT_SCALE = {'norm_in_w': 1.292583e-01, 'w_in': 4.164209e-02, 'conv_qkv_w': 3.499825e-02, 'A_log': 2.329263e-01, 'dt_bias': 2.255886e-01, 'gdn_norm_w': 1.322453e-01, 'conv_w': 4.723346e-02, 'conv_b': 4.522681e-02, 'w_out': 6.331256e-02, 'final_norm_w': 1.600971e+01}


def _to_microbatches(a, axis):
    t = _jnp.moveaxis(a, axis, 0)
    t = t.reshape((N_MICROBATCH, t.shape[0] // N_MICROBATCH) + t.shape[1:])
    return _jnp.moveaxis(t, 1, axis + 1)


def setup_inputs(seed: int = 0) -> dict:
    inp = _fwd_setup_inputs(seed)
    key = _jax.random.fold_in(_jax.random.key(seed), 7919)
    shape, _ = _output_shape()
    out = dict(inp)
    out["loss_target"] = _jax.random.normal(_jax.random.fold_in(key, 0), shape, _jnp.float32)
    for i, name in enumerate(TWIN_WEIGHTS):
        w = inp[name].astype(_jnp.float32)
        if MOMENT_SCALE is None:
            s = _jnp.sqrt(_jnp.mean(_jnp.square(w)) + 1e-30)
        else:
            s = MOMENT_SCALE[name]
        km, kv = _jax.random.split(_jax.random.fold_in(key, i + 1))
        out[name] = w
        out["m_" + name] = s * _jax.random.normal(km, w.shape, _jnp.float32)
        out["v_" + name] = (s * s) * _jax.random.uniform(kv, w.shape, _jnp.float32, 0.5, 1.5)
    if N_MICROBATCH > 1:
        for name, axis in PER_EXAMPLE_BATCH_AXIS.items():
            out[name] = _to_microbatches(out[name], axis)
    return {'x': out['x'], 'norm_in_w': out['norm_in_w'], 'w_in': out['w_in'], 'conv_qkv_w': out['conv_qkv_w'], 'A_log': out['A_log'], 'dt_bias': out['dt_bias'], 'gdn_norm_w': out['gdn_norm_w'], 'conv_w': out['conv_w'], 'conv_b': out['conv_b'], 'w_out': out['w_out'], 'final_norm_w': out['final_norm_w'], 'loss_target': out['loss_target'], 'm_norm_in_w': out['m_norm_in_w'], 'm_w_in': out['m_w_in'], 'm_conv_qkv_w': out['m_conv_qkv_w'], 'm_A_log': out['m_A_log'], 'm_dt_bias': out['m_dt_bias'], 'm_gdn_norm_w': out['m_gdn_norm_w'], 'm_conv_w': out['m_conv_w'], 'm_conv_b': out['m_conv_b'], 'm_w_out': out['m_w_out'], 'm_final_norm_w': out['m_final_norm_w'], 'v_norm_in_w': out['v_norm_in_w'], 'v_w_in': out['v_w_in'], 'v_conv_qkv_w': out['v_conv_qkv_w'], 'v_A_log': out['v_A_log'], 'v_dt_bias': out['v_dt_bias'], 'v_gdn_norm_w': out['v_gdn_norm_w'], 'v_conv_w': out['v_conv_w'], 'v_conv_b': out['v_conv_b'], 'v_w_out': out['v_w_out'], 'v_final_norm_w': out['v_final_norm_w']}


def _loss(weights, diff, rest, loss_target):
    with _jax.named_scope("forward"):
        args = {**rest, TWIN_DIFF_INPUT: diff, **{k: w.astype(_WEIGHT_DTYPES[k]) for k, w in weights.items()}}
        y = _forward(args)
    with _jax.named_scope("loss_head"):
        err = _jnp.square(y.astype(_jnp.float32) - loss_target)
        return 0.5 * _jnp.sum(_jnp.mean(err, axis=-1)) if err.ndim else 0.5 * err


def _adamw(w, g, m, v):
    m = ADAM_B1 * m + (1.0 - ADAM_B1) * g
    v = ADAM_B2 * v + (1.0 - ADAM_B2) * _jnp.square(g)
    m_hat = m / (1.0 - ADAM_B1 ** ADAM_STEP)
    v_hat = v / (1.0 - ADAM_B2 ** ADAM_STEP)
    delta = -ADAM_LR * (m_hat / (_jnp.sqrt(v_hat) + ADAM_EPS) + ADAM_WD * w)
    return delta, m, v


def reference(x, norm_in_w, w_in, conv_qkv_w, A_log, dt_bias, gdn_norm_w, conv_w, conv_b, w_out, final_norm_w, loss_target, m_norm_in_w, m_w_in, m_conv_qkv_w, m_A_log, m_dt_bias, m_gdn_norm_w, m_conv_w, m_conv_b, m_w_out, m_final_norm_w, v_norm_in_w, v_w_in, v_conv_qkv_w, v_A_log, v_dt_bias, v_gdn_norm_w, v_conv_w, v_conv_b, v_w_out, v_final_norm_w):
    given = dict(x=x, norm_in_w=norm_in_w, w_in=w_in, conv_qkv_w=conv_qkv_w, A_log=A_log, dt_bias=dt_bias, gdn_norm_w=gdn_norm_w, conv_w=conv_w, conv_b=conv_b, w_out=w_out, final_norm_w=final_norm_w, loss_target=loss_target, m_norm_in_w=m_norm_in_w, m_w_in=m_w_in, m_conv_qkv_w=m_conv_qkv_w, m_A_log=m_A_log, m_dt_bias=m_dt_bias, m_gdn_norm_w=m_gdn_norm_w, m_conv_w=m_conv_w, m_conv_b=m_conv_b, m_w_out=m_w_out, m_final_norm_w=m_final_norm_w, v_norm_in_w=v_norm_in_w, v_w_in=v_w_in, v_conv_qkv_w=v_conv_qkv_w, v_A_log=v_A_log, v_dt_bias=v_dt_bias, v_gdn_norm_w=v_gdn_norm_w, v_conv_w=v_conv_w, v_conv_b=v_conv_b, v_w_out=v_w_out, v_final_norm_w=v_final_norm_w)
    weights = {n: given[n] for n in TWIN_WEIGHTS}
    shared = {n: given[n] for n in SHARED_INPUTS}
    per_example = {n: given[n] for n in ['x']}
    grad_fn = _jax.value_and_grad(_loss, argnums=(0, 1))

    def one_microbatch(ex, loss_target):
        ex = dict(ex)
        diff = ex.pop(TWIN_DIFF_INPUT)
        return grad_fn(weights, diff, {**shared, **ex}, loss_target)

    if N_MICROBATCH == 1:
        loss, (grad_w, grad_x) = one_microbatch(per_example, given["loss_target"])
    else:
        def body(carry, xs):
            loss_sum, grad_sum = carry
            l_k, (gw_k, gx_k) = one_microbatch(xs[0], xs[1])
            with _jax.named_scope("update"):
                return (loss_sum + l_k, _jax.tree.map(_jnp.add, grad_sum, gw_k)), gx_k

        init = (_jnp.zeros((), _jnp.float32), _jax.tree.map(_jnp.zeros_like, weights))
        (loss, grad_w), grad_x = _jax.lax.scan(body, init, (per_example, given["loss_target"]))
    with _jax.named_scope("update"):
        delta_w, new_m, new_v = {}, {}, {}
        for n in TWIN_WEIGHTS:
            delta_w[n], new_m[n], new_v[n] = _adamw(weights[n], grad_w[n], given["m_" + n], given["v_" + n])
    return (loss, grad_x, *[grad_w[n] for n in TWIN_WEIGHTS], *[delta_w[n] for n in TWIN_WEIGHTS],
            *[new_m[n] for n in TWIN_WEIGHTS], *[new_v[n] for n in TWIN_WEIGHTS])
```

```python
import functools
import math

import jax
import jax.numpy as jnp
from jax import lax
from jax.experimental import pallas as pl
from jax.experimental.pallas import tpu as pltpu

f32 = jnp.float32
bf16 = jnp.bfloat16

N_DEV = 8
D_MODEL = 1024
HEADS = 8
HEAD_DIM = 128
CHUNK = 64
GDN_WIDTH = HEADS * HEAD_DIM
CONV_WIDTH = 1024
PROJ_WIDTH = 8208
SHARD_W = PROJ_WIDTH // N_DEV
EPS = 1e-6

NAT_SMALL_END = 4112
PAD_COLS = 112
OFF_QKV, OFF_ZG, OFF_BA, OFF_B, OFF_C, OFF_HC, OFF_ZC = 0, 3072, 4096, 4224, 5248, 6272, 7296
PROJ_PAD = 8320
LANE = 128

ADAM_LR, ADAM_B1, ADAM_B2, ADAM_EPS, ADAM_WD, ADAM_STEP = 0.001, 0.9, 0.999, 1e-08, 0.01, 10

VMEM_LIMIT = 56 * 1024 * 1024

MESH = pl.DeviceIdType.MESH
ANY = pl.BlockSpec(memory_space=pl.ANY)


def _pcall(body, **kw):
    return pl.pallas_call(body, **kw)


def _cparams(*sem):
    return pltpu.CompilerParams(dimension_semantics=sem if sem else None, vmem_limit_bytes=VMEM_LIMIT)


def _mm(a, b):
    return jnp.dot(a.astype(bf16), b.astype(bf16), preferred_element_type=f32)


def _mm_nt(a, b):
    return lax.dot_general(a.astype(bf16), b.astype(bf16), (((1,), (1,)), ((), ())), preferred_element_type=f32)


def _mm_tn(a, b):
    return lax.dot_general(a.astype(bf16), b.astype(bf16), (((0,), (0,)), ((), ())), preferred_element_type=f32)


def _rows(shape):
    return lax.broadcasted_iota(jnp.int32, shape, 0)


def _lanes(shape):
    return lax.broadcasted_iota(jnp.int32, shape, 1)


def _shift_down(x, s):
    if s == 0:
        return x
    return jnp.where(_rows(x.shape) >= s, pltpu.roll(x, s, 0), 0.0)


def _shift_up(x, s):
    if s == 0:
        return x
    n = x.shape[0]
    return jnp.where(_rows(x.shape) < n - s, pltpu.roll(x, n - s, 0), 0.0)


def _sigmoid(x):
    return jax.nn.sigmoid(x)


def _softplus(x):
    e = jnp.exp(-jnp.abs(x))
    small = e * (1.0 - e * (0.5 - e * (1.0 / 3.0)))
    return jnp.maximum(x, 0.0) + jnp.where(e < 0.01, small, jnp.log(1.0 + e))


def _mesh_pos():
    return lax.axis_index("x"), lax.axis_index("y"), lax.axis_index("c")


def _flat(px, py, pc):
    return 4 * px + 2 * py + pc


def _all_gather(xs, name):
    n = len(xs)

    def body(*refs):
        x_refs, o_refs = refs[:n], refs[n:2 * n]
        send_sems, recv_sems, local_sems = refs[2 * n:]
        x, y, c = _mesh_pos()
        me, sibling = (x, y, c), (x, y, 1 - c)
        chips = [(1 - x, y), (x, 1 - y), (1 - x, 1 - y)]

        def copy(a, k, block, to, src=None):
            dst = o_refs[a].at[_flat(*block)]
            return pltpu.make_async_remote_copy(
                src_ref=dst if src is None else src, dst_ref=dst,
                send_sem=send_sems.at[a, k], recv_sem=recv_sems.at[a, k], device_id=to, device_id_type=MESH)

        mine, first, passed = [], [], []
        for a in range(n):
            cp = pltpu.make_async_copy(x_refs[a], o_refs[a].at[_flat(*me)], local_sems.at[a])
            cp.start()
            mine.append(cp)
            fa = [copy(a, 0, me, sibling, src=x_refs[a])]
            fa += [copy(a, 1 + j, me, (*chip, c), src=x_refs[a]) for j, chip in enumerate(chips)]
            for cp in fa:
                cp.start()
            first += fa
        for a in range(n):
            for j, chip in enumerate(chips):
                copy(a, 1 + j, (*chip, c), me).wait_recv()
                cp = copy(a, 4 + j, (*chip, c), sibling)
                cp.start()
                passed.append(cp)
        for a in range(n):
            copy(a, 0, sibling, me).wait_recv()
            for j, chip in enumerate(chips):
                copy(a, 4 + j, (*chip, 1 - c), me).wait_recv()
        for cp in first + passed:
            cp.wait_send()
        for cp in mine:
            cp.wait()

    outs = _pcall(
        body, name=name,
        out_shape=[jax.ShapeDtypeStruct((N_DEV,) + a.shape, a.dtype) for a in xs],
        in_specs=[ANY] * n, out_specs=[ANY] * n,
        scratch_shapes=[pltpu.SemaphoreType.DMA((n, 7)), pltpu.SemaphoreType.DMA((n, 7)), pltpu.SemaphoreType.DMA((n,))],
    )(*xs)
    return list(outs)


def _all_to_all(gs, name):
    n = len(gs)

    def body(*refs):
        g_refs, o_refs = refs[:n], refs[n:2 * n]
        send_sems, recv_sems, local_sems = refs[2 * n:]
        x, y, c = _mesh_pos()
        me = _flat(x, y, c)
        peers = []
        for k in range(1, N_DEV):
            kx, ky, kc = (k >> 2) & 1, (k >> 1) & 1, k & 1
            px = (1 - x) if kx else x
            py = (1 - y) if ky else y
            pc = (1 - c) if kc else c
            peers.append((px, py, pc))

        def copy(a, k):
            peer = peers[k - 1]
            return pltpu.make_async_remote_copy(
                src_ref=g_refs[a].at[_flat(*peer)], dst_ref=o_refs[a].at[me],
                send_sem=send_sems.at[a, k - 1], recv_sem=recv_sems.at[a, k - 1], device_id=peer, device_id_type=MESH)

        def arrival(a, k):
            peer = peers[k - 1]
            return pltpu.make_async_remote_copy(
                src_ref=g_refs[a].at[me], dst_ref=o_refs[a].at[_flat(*peer)],
                send_sem=send_sems.at[a, k - 1], recv_sem=recv_sems.at[a, k - 1], device_id=peer, device_id_type=MESH)

        mine, sent = [], []
        for a in range(n):
            cp = pltpu.make_async_copy(g_refs[a].at[me], o_refs[a].at[me], local_sems.at[a])
            cp.start()
            mine.append(cp)
            for k in range(1, N_DEV):
                cp = copy(a, k)
                cp.start()
                sent.append(cp)
        for a in range(n):
            for k in range(1, N_DEV):
                arrival(a, k).wait_recv()
        for cp in sent:
            cp.wait_send()
        for cp in mine:
            cp.wait()

    outs = _pcall(
        body, name=name,
        out_shape=[jax.ShapeDtypeStruct(a.shape, a.dtype) for a in gs],
        in_specs=[ANY] * n, out_specs=[ANY] * n,
        scratch_shapes=[pltpu.SemaphoreType.DMA((n, 7)), pltpu.SemaphoreType.DMA((n, 7)), pltpu.SemaphoreType.DMA((n,))],
    )(*gs)
    return list(outs)


def _in_proj(x, nw, wpad):
    L = x.shape[0]
    tn = 640
    nj = wpad.shape[1] // tn

    def body(x_ref, nw_ref, w_ref, proj_ref, h_ref):
        @pl.when(pl.program_id(0) == 0)
        def _():
            for r in range(0, L, 256):
                xs = x_ref[r:r + 256, :]
                ms = jnp.mean(xs * xs, axis=-1, keepdims=True)
                h_ref[r:r + 256, :] = ((xs * lax.rsqrt(ms + EPS)) * nw_ref[...]).astype(bf16)
        for r in range(0, L, 512):
            proj_ref[r:r + 512, :] = jnp.dot(h_ref[r:r + 512, :], w_ref[...], preferred_element_type=f32)

    return _pcall(
        body, name="in_proj", grid=(nj,),
        in_specs=[pl.BlockSpec((L, D_MODEL), lambda j: (0, 0)), pl.BlockSpec((1, D_MODEL), lambda j: (0, 0)),
                  pl.BlockSpec((D_MODEL, tn), lambda j: (0, j))],
        out_specs=[pl.BlockSpec((L, tn), lambda j: (0, j)), pl.BlockSpec((L, D_MODEL), lambda j: (0, 0))],
        out_shape=[jax.ShapeDtypeStruct((L, wpad.shape[1]), f32), jax.ShapeDtypeStruct((L, D_MODEL), bf16)],
        compiler_params=_cparams("arbitrary"),
    )(x, nw, wpad)


def _conv4(x, cw_ref):
    return (cw_ref[3:4, :] * x + cw_ref[2:3, :] * _shift_down(x, 1) + cw_ref[1:2, :] * _shift_down(x, 2)
            + cw_ref[0:1, :] * _shift_down(x, 3))


def _qkv_act(proj, cw):
    L = proj.shape[0]

    def body(x_ref, cw_ref, o_ref):
        j = pl.program_id(0)
        c = _conv4(x_ref[...], cw_ref)
        a = c * _sigmoid(c)
        rn = lax.rsqrt(jnp.sum(a * a, axis=1, keepdims=True) + EPS)
        scale = jnp.where(j < HEADS, HEAD_DIM ** -0.5, 1.0).astype(f32)
        o_ref[...] = jnp.where(j < 2 * HEADS, (a * rn) * scale, a)

    return _pcall(
        body, name="qkv_act", grid=(3 * HEADS,),
        in_specs=[pl.BlockSpec((L, LANE), lambda j: (0, j)), pl.BlockSpec((4, LANE), lambda j: (0, j))],
        out_specs=pl.BlockSpec((L, LANE), lambda j: (0, j)),
        out_shape=jax.ShapeDtypeStruct((L, 3 * GDN_WIDTH), f32),
        compiler_params=_cparams("parallel"),
    )(proj, cw)


def _scalars(proj, alog_p, dtb_p):
    L = proj.shape[0]
    nc = L // CHUNK

    def body(x_ref, al_ref, dt_ref, sc_ref, gr_ref):
        x = x_ref[...]
        lane = _lanes(x.shape)
        beta = _sigmoid(x)
        g = -jnp.exp(al_ref[...]) * _softplus(x + dt_ref[...])
        gc = jnp.where((lane >= HEADS) & (lane < 2 * HEADS), g, 0.0)
        rc = _rows(x.shape) & (CHUNK - 1)
        for s in (1, 2, 4, 8, 16, 32):
            gc = gc + jnp.where(rc >= s, pltpu.roll(gc, s, 0), 0.0)
        sc_ref[...] = jnp.where(lane < HEADS, beta, gc)
        sel = (_lanes((HEADS, LANE)) == _rows((HEADS, LANE)) + HEADS).astype(f32)
        for c in range(nc):
            gr_ref[c] = lax.dot_general(sel, sc_ref[c * CHUNK:(c + 1) * CHUNK, :], (((1,), (1,)), ((), ())),
                                        preferred_element_type=f32, precision=lax.Precision.HIGHEST)

    return _pcall(
        body, name="scalars", grid=(1,),
        in_specs=[pl.BlockSpec((L, LANE), lambda i: (0, OFF_BA // LANE)), pl.BlockSpec((1, LANE), lambda i: (0, 0)),
                  pl.BlockSpec((1, LANE), lambda i: (0, 0))],
        out_specs=[pl.BlockSpec((L, LANE), lambda i: (0, 0)), pl.BlockSpec((nc, HEADS, CHUNK), lambda i: (0, 0, 0))],
        out_shape=[jax.ShapeDtypeStruct((L, LANE), f32), jax.ShapeDtypeStruct((nc, HEADS, CHUNK), f32)],
        compiler_params=_cparams("arbitrary"),
    )(proj, alog_p, dtb_p)


def _head_scalars(sc, gr_ref, h):
    lane = _lanes(sc.shape)
    beta = jnp.sum(jnp.where(lane == h, sc, 0.0), axis=1, keepdims=True)
    gcc = jnp.sum(jnp.where(lane == HEADS + h, sc, 0.0), axis=1, keepdims=True)
    gcr = gr_ref[0, h:h + 1, :]
    gl = jnp.sum(jnp.where(_lanes(gcr.shape) == CHUNK - 1, gcr, 0.0), axis=1, keepdims=True)
    ii, jj = _rows((CHUNK, CHUNK)), _lanes((CHUNK, CHUNK))
    dmat = jnp.where(ii >= jj, jnp.exp(jnp.minimum(gcc - gcr, 0.0)), 0.0)
    dmat_t = jnp.where(jj >= ii, jnp.exp(jnp.minimum(gcr - gcc, 0.0)), 0.0)
    return beta, gcc, gl, dmat, dmat_t, ii, jj


def _gdn_fwd(qkv, sc, gr):
    L = qkv.shape[0]
    nc = L // CHUNK
    W = GDN_WIDTH

    def body(qkv_ref, sc_ref, gr_ref, o_ref, u_ref, w_ref, vn_ref, t_ref, sp_ref, s_scr):
        @pl.when(pl.program_id(0) == 0)
        def _():
            s_scr[...] = jnp.zeros_like(s_scr)
        sc_v = sc_ref[...]
        for h in range(HEADS):
            cs = slice(h * HEAD_DIM, (h + 1) * HEAD_DIM)
            q = qkv_ref[:, h * HEAD_DIM:(h + 1) * HEAD_DIM]
            k = qkv_ref[:, W + h * HEAD_DIM:W + (h + 1) * HEAD_DIM]
            v = qkv_ref[:, 2 * W + h * HEAD_DIM:2 * W + (h + 1) * HEAD_DIM]
            beta, gcc, gl, dmat, _, ii, jj = _head_scalars(sc_v, gr_ref, h)
            eg, ekl, egl = jnp.exp(gcc), jnp.exp(gl - gcc), jnp.exp(gl)
            kb, vb = k * beta, v * beta
            kbg, qd, ks = kb * eg, q * eg, k * ekl
            a = jnp.where(ii > jj, _mm_nt(kb, k) * dmat, 0.0)
            p = jnp.where(ii >= jj, _mm_nt(q, k) * dmat, 0.0)
            xp = -a
            t = xp
            for _ in range(5):
                xp = _mm(xp, xp)
                t = t + xp + _mm(t, xp)
            uw = _mm(t, jnp.concatenate([vb, kbg], axis=1))
            u = vb + uw[:, :HEAD_DIM]
            w = kbg + uw[:, HEAD_DIM:]
            s = s_scr[h]
            sp_ref[0, cs, :] = s
            ws = _mm(jnp.concatenate([w, qd], axis=0), s)
            vn = u - ws[:CHUNK]
            o_ref[:, cs] = ws[CHUNK:] + _mm(p, vn)
            s_scr[h] = egl * s + _mm_tn(ks, vn)
            u_ref[:, cs] = u
            w_ref[:, cs] = w
            vn_ref[:, cs] = vn
            t_ref[0, h] = t

    row = lambda c: (c, 0)
    act = jax.ShapeDtypeStruct((L, W), f32)
    return _pcall(
        body, name="gdn_fwd", grid=(nc,),
        in_specs=[pl.BlockSpec((CHUNK, 3 * W), row), pl.BlockSpec((CHUNK, LANE), row),
                  pl.BlockSpec((1, HEADS, CHUNK), lambda c: (c, 0, 0))],
        out_specs=[pl.BlockSpec((CHUNK, W), row)] * 4 + [
            pl.BlockSpec((1, HEADS, CHUNK, CHUNK), lambda c: (c, 0, 0, 0)),
            pl.BlockSpec((1, W, HEAD_DIM), lambda c: (c, 0, 0))],
        out_shape=[act, act, act, act, jax.ShapeDtypeStruct((nc, HEADS, CHUNK, CHUNK), f32),
                   jax.ShapeDtypeStruct((nc, W, HEAD_DIM), f32)],
        scratch_shapes=[pltpu.VMEM((HEADS, HEAD_DIM, HEAD_DIM), f32)],
        compiler_params=_cparams("arbitrary"),
    )(qkv, sc, gr)


def _gdn_gate(o, proj, gnw):
    L = o.shape[0]

    def body(o_ref, z_ref, w_ref, m_ref):
        ov, z = o_ref[...], z_ref[...]
        rms = lax.rsqrt(jnp.mean(ov * ov, axis=-1, keepdims=True) + EPS)
        m_ref[...] = (((ov * rms) * w_ref[...]) * (z * _sigmoid(z))).astype(bf16)

    return _pcall(
        body, name="gdn_gate", grid=(HEADS,),
        in_specs=[pl.BlockSpec((L, LANE), lambda j: (0, j)), pl.BlockSpec((L, LANE), lambda j: (0, OFF_ZG // LANE + j)),
                  pl.BlockSpec((1, LANE), lambda j: (0, 0))],
        out_specs=pl.BlockSpec((L, LANE), lambda j: (0, j)),
        out_shape=jax.ShapeDtypeStruct((L, GDN_WIDTH), bf16),
        compiler_params=_cparams("parallel"),
    )(o, proj, gnw)


def _conv3(u, cw_ref):
    return cw_ref[2:3, :] * u + cw_ref[1:2, :] * _shift_down(u, 1) + cw_ref[0:1, :] * _shift_down(u, 2)


def _conv_specs(L):
    blk = lambda off: pl.BlockSpec((L, LANE), lambda j, off=off: (0, off // LANE + j))
    return [blk(OFF_B), blk(OFF_C), blk(OFF_HC), blk(OFF_ZC),
            pl.BlockSpec((3, LANE), lambda j: (0, j)), pl.BlockSpec((1, LANE), lambda j: (0, j))]


def _conv_fwd(proj, cw, cb):
    L = proj.shape[0]

    def body(b_ref, c_ref, h_ref, z_ref, cw_ref, cb_ref, m_ref):
        z = z_ref[...]
        cv = _conv3(c_ref[...] * h_ref[...], cw_ref) + cb_ref[...]
        m_ref[...] = ((b_ref[...] * cv) * (z * _sigmoid(z))).astype(bf16)

    return _pcall(
        body, name="conv_fwd", grid=(CONV_WIDTH // LANE,),
        in_specs=_conv_specs(L),
        out_specs=pl.BlockSpec((L, LANE), lambda j: (0, j)),
        out_shape=jax.ShapeDtypeStruct((L, CONV_WIDTH), bf16),
        compiler_params=_cparams("parallel"),
    )(proj, proj, proj, proj, cw, cb)


def _out_proj_loss(x, mix_a, mix_b, wo, fw, tgt):
    L = x.shape[0]
    tm = min(256, L)

    def body(x_ref, ma_ref, mb_ref, wo_ref, fw_ref, t_ref, dy_ref, dyb_ref, dma_ref, dmb_ref, gfw_ref, loss_ref):
        @pl.when(pl.program_id(0) == 0)
        def _():
            gfw_ref[...] = jnp.zeros_like(gfw_ref)
            loss_ref[...] = jnp.zeros_like(loss_ref)
        y = x_ref[...] + jnp.dot(ma_ref[...], wo_ref[:GDN_WIDTH, :], preferred_element_type=f32) \
            + jnp.dot(mb_ref[...], wo_ref[GDN_WIDTH:, :], preferred_element_type=f32)
        r = lax.rsqrt(jnp.mean(y * y, axis=-1, keepdims=True) + EPS)
        yh = y * r
        fwv = fw_ref[...]
        diff = yh * fwv - t_ref[...]
        loss_ref[...] += jnp.sum(jnp.sum(diff * diff, axis=-1, keepdims=True), axis=0, keepdims=True) * (0.5 / D_MODEL)
        dout = diff * (1.0 / D_MODEL)
        gfw_ref[...] += jnp.sum(dout * yh, axis=0, keepdims=True)
        dyh = dout * fwv
        dy = r * (dyh - yh * jnp.mean(dyh * yh, axis=-1, keepdims=True))
        dy_ref[...] = dy
        dyb = dy.astype(bf16)
        dyb_ref[...] = dyb
        dma_ref[...] = lax.dot_general(dyb, wo_ref[:GDN_WIDTH, :], (((1,), (1,)), ((), ())), preferred_element_type=f32)
        dmb_ref[...] = lax.dot_general(dyb, wo_ref[GDN_WIDTH:, :], (((1,), (1,)), ((), ())), preferred_element_type=f32)

    row = lambda i: (i, 0)
    fix = lambda i: (0, 0)
    act = jax.ShapeDtypeStruct((L, D_MODEL), f32)
    return _pcall(
        body, name="out_proj_loss", grid=(L // tm,),
        in_specs=[pl.BlockSpec((tm, D_MODEL), row), pl.BlockSpec((tm, GDN_WIDTH), row), pl.BlockSpec((tm, CONV_WIDTH), row),
                  pl.BlockSpec((GDN_WIDTH + CONV_WIDTH, D_MODEL), fix), pl.BlockSpec((1, D_MODEL), fix),
                  pl.BlockSpec((tm, D_MODEL), row)],
        out_specs=[pl.BlockSpec((tm, D_MODEL), row), pl.BlockSpec((tm, D_MODEL), row), pl.BlockSpec((tm, GDN_WIDTH), row),
                   pl.BlockSpec((tm, CONV_WIDTH), row), pl.BlockSpec((1, D_MODEL), fix), pl.BlockSpec((1, LANE), fix)],
        out_shape=[act, jax.ShapeDtypeStruct((L, D_MODEL), bf16), act, act,
                   jax.ShapeDtypeStruct((1, D_MODEL), f32), jax.ShapeDtypeStruct((1, LANE), f32)],
        compiler_params=_cparams("arbitrary"),
    )(x, mix_a, mix_b, wo, fw, tgt)


def _tn_matmul(a, b, name):
    L, M = a.shape
    N = b.shape[1]
    tn = 512 if N % 512 == 0 else N

    def body(a_ref, b_ref, o_ref):
        o_ref[...] = lax.dot_general(a_ref[...], b_ref[...], (((0,), (0,)), ((), ())), preferred_element_type=f32)

    return _pcall(
        body, name=name, grid=(N // tn,),
        in_specs=[pl.BlockSpec((L, M), lambda j: (0, 0)), pl.BlockSpec((L, tn), lambda j: (0, j))],
        out_specs=pl.BlockSpec((M, tn), lambda j: (0, j)),
        out_shape=jax.ShapeDtypeStruct((M, N), f32),
        compiler_params=_cparams("parallel"),
    )(a, b)


def _gdn_gate_bwd(o, proj, gnw, dmix_a):
    L = o.shape[0]

    def body(o_ref, z_ref, w_ref, dm_ref, do_ref, dz_ref, gw_ref):
        @pl.when(pl.program_id(0) == 0)
        def _():
            gw_ref[...] = jnp.zeros_like(gw_ref)
        ov, z, dm, wv = o_ref[...], z_ref[...], dm_ref[...], w_ref[...]
        rms = lax.rsqrt(jnp.mean(ov * ov, axis=-1, keepdims=True) + EPS)
        xh = ov * rms
        sg = _sigmoid(z)
        d_on = dm * (z * sg)
        dz_ref[...] = (dm * (xh * wv) * (sg * (1.0 + z * (1.0 - sg)))).astype(bf16)
        gw_ref[...] += jnp.sum(d_on * xh, axis=0, keepdims=True)
        dxh = d_on * wv
        do_ref[...] = rms * (dxh - xh * jnp.mean(dxh * xh, axis=-1, keepdims=True))

    return _pcall(
        body, name="gdn_gate_bwd", grid=(HEADS,),
        in_specs=[pl.BlockSpec((L, LANE), lambda j: (0, j)), pl.BlockSpec((L, LANE), lambda j: (0, OFF_ZG // LANE + j)),
                  pl.BlockSpec((1, LANE), lambda j: (0, 0)), pl.BlockSpec((L, LANE), lambda j: (0, j))],
        out_specs=[pl.BlockSpec((L, LANE), lambda j: (0, j)), pl.BlockSpec((L, LANE), lambda j: (0, j)),
                   pl.BlockSpec((1, LANE), lambda j: (0, 0))],
        out_shape=[jax.ShapeDtypeStruct((L, GDN_WIDTH), f32), jax.ShapeDtypeStruct((L, GDN_WIDTH), bf16),
                   jax.ShapeDtypeStruct((1, LANE), f32)],
        compiler_params=_cparams("arbitrary"),
    )(o, proj, gnw, dmix_a)


def _conv_bwd(proj, cw, cb, dmix_b):
    L = proj.shape[0]

    def body(b_ref, c_ref, h_ref, z_ref, cw_ref, cb_ref, dm_ref, db_ref, dc_ref, dh_ref, dz_ref, gcw_ref, gcb_ref):
        bv, cv_, hv, z, dm = b_ref[...], c_ref[...], h_ref[...], z_ref[...], dm_ref[...]
        u = cv_ * hv
        cv = _conv3(u, cw_ref) + cb_ref[...]
        sg = _sigmoid(z)
        sz = z * sg
        db_ref[...] = (dm * cv * sz).astype(bf16)
        dz_ref[...] = (dm * (bv * cv) * (sg * (1.0 + z * (1.0 - sg)))).astype(bf16)
        dcv = dm * bv * sz
        gcb_ref[...] = jnp.sum(dcv, axis=0, keepdims=True)
        gcw_ref[2:3, :] = jnp.sum(dcv * u, axis=0, keepdims=True)
        gcw_ref[1:2, :] = jnp.sum(dcv * _shift_down(u, 1), axis=0, keepdims=True)
        gcw_ref[0:1, :] = jnp.sum(dcv * _shift_down(u, 2), axis=0, keepdims=True)
        du = cw_ref[2:3, :] * dcv + cw_ref[1:2, :] * _shift_up(dcv, 1) + cw_ref[0:1, :] * _shift_up(dcv, 2)
        dc_ref[...] = (du * hv).astype(bf16)
        dh_ref[...] = (du * cv_).astype(bf16)

    col = pl.BlockSpec((L, LANE), lambda j: (0, j))
    act = jax.ShapeDtypeStruct((L, CONV_WIDTH), bf16)
    return _pcall(
        body, name="conv_bwd", grid=(CONV_WIDTH // LANE,),
        in_specs=_conv_specs(L) + [col],
        out_specs=[col, col, col, col, pl.BlockSpec((3, LANE), lambda j: (0, j)), pl.BlockSpec((1, LANE), lambda j: (0, j))],
        out_shape=[act, act, act, act, jax.ShapeDtypeStruct((3, CONV_WIDTH), f32), jax.ShapeDtypeStruct((1, CONV_WIDTH), f32)],
        compiler_params=_cparams("parallel"),
    )(proj, proj, proj, proj, cw, cb, dmix_b)


def _gdn_bwd(qkv, sc, gr, u_all, w_all, vn_all, t_all, sp_all, do_all):
    L = qkv.shape[0]
    nc = L // CHUNK
    W = GDN_WIDTH

    def body(qkv_ref, sc_ref, gr_ref, u_ref, w_ref, vn_ref, t_ref, sp_ref, do_ref, dqkv_ref, dsc_ref, dgr_ref, ds_scr):
        @pl.when(pl.program_id(0) == 0)
        def _():
            ds_scr[...] = jnp.zeros_like(ds_scr)
        sc_v = sc_ref[...]
        lane = _lanes(sc_v.shape)
        dsc = jnp.zeros(sc_v.shape, f32)
        for h in range(HEADS):
            cs = slice(h * HEAD_DIM, (h + 1) * HEAD_DIM)
            q = qkv_ref[:, h * HEAD_DIM:(h + 1) * HEAD_DIM]
            k = qkv_ref[:, W + h * HEAD_DIM:W + (h + 1) * HEAD_DIM]
            v = qkv_ref[:, 2 * W + h * HEAD_DIM:2 * W + (h + 1) * HEAD_DIM]
            beta, gcc, gl, dmat, dmat_t, ii, jj = _head_scalars(sc_v, gr_ref, h)
            eg, ekl, egl = jnp.exp(gcc), jnp.exp(gl - gcc), jnp.exp(gl)
            kb = k * beta
            kbg, qd, ks = kb * eg, q * eg, k * ekl
            a = jnp.where(ii > jj, _mm_nt(kb, k) * dmat, 0.0)
            p = jnp.where(ii >= jj, _mm_nt(q, k) * dmat, 0.0)
            p_t = jnp.where(jj >= ii, _mm_nt(k, q) * dmat_t, 0.0)
            do, u, w, vn, t = do_ref[:, cs], u_ref[:, cs], w_ref[:, cs], vn_ref[:, cs], t_ref[0, h]
            s, dsn = sp_ref[0, cs, :], ds_scr[h]

            dvn = _mm(p_t, do) + _mm(ks, dsn)
            dp = jnp.where(ii >= jj, _mm_nt(do, vn), 0.0)
            dp_t = jnp.where(jj >= ii, _mm_nt(vn, do), 0.0)
            dodv = jnp.concatenate([do, dvn], axis=0)
            x1 = _mm_nt(dodv, s)
            dqd, dw = x1[:CHUNK], -x1[CHUNK:]
            dks = _mm_nt(vn, dsn)
            dgl = egl * jnp.sum(jnp.sum(s * dsn, axis=1, keepdims=True), axis=0, keepdims=True)
            ds_scr[h] = egl * dsn + _mm_tn(jnp.concatenate([qd, -w], axis=0), dodv)

            duw = jnp.concatenate([dvn, dw], axis=1)
            dvk = duw + _mm_tn(t, duw)
            dvb, dkbg = dvk[:, :HEAD_DIM], dvk[:, HEAD_DIM:]
            uw = jnp.concatenate([u, w], axis=1)
            da = -jnp.where(ii > jj, _mm_nt(dvk, uw), 0.0)
            da_t = -jnp.where(jj > ii, _mm_nt(uw, dvk), 0.0)
            gmat = da * a + dp * p
            r1 = _mm(jnp.concatenate([da * dmat, dp * dmat], axis=0), k)
            dk1 = _mm(jnp.concatenate([da_t * dmat_t, dp_t * dmat_t], axis=1), jnp.concatenate([kb, q], axis=0))
            dkb = r1[:CHUNK] + dkbg * eg
            dq = r1[CHUNK:] + dqd * eg
            dk = dk1 + dks * ekl + dkb * beta
            dbeta = jnp.sum(dkb * k + dvb * v, axis=1, keepdims=True)
            ksum = jnp.sum(dks * ks, axis=1, keepdims=True)
            dgl_tot = dgl + jnp.sum(ksum, axis=0, keepdims=True)
            dgc = jnp.sum(gmat, axis=1, keepdims=True) + jnp.sum(dqd * qd + dkbg * kbg, axis=1, keepdims=True) - ksum
            dgc = dgc + jnp.where(_rows(dgc.shape) == CHUNK - 1, dgl_tot, 0.0)
            dqkv_ref[:, h * HEAD_DIM:(h + 1) * HEAD_DIM] = dq
            dqkv_ref[:, W + h * HEAD_DIM:W + (h + 1) * HEAD_DIM] = dk
            dqkv_ref[:, 2 * W + h * HEAD_DIM:2 * W + (h + 1) * HEAD_DIM] = dvb * beta
            dsc = jnp.where(lane == h, dbeta, jnp.where(lane == HEADS + h, dgc, dsc))
            dgr_ref[0, h:h + 1, :] = jnp.sum(gmat, axis=0, keepdims=True)
        dsc_ref[...] = dsc

    row = lambda c: (nc - 1 - c, 0)
    return _pcall(
        body, name="gdn_bwd", grid=(nc,),
        in_specs=[pl.BlockSpec((CHUNK, 3 * W), row), pl.BlockSpec((CHUNK, LANE), row),
                  pl.BlockSpec((1, HEADS, CHUNK), lambda c: (nc - 1 - c, 0, 0)),
                  pl.BlockSpec((CHUNK, W), row), pl.BlockSpec((CHUNK, W), row), pl.BlockSpec((CHUNK, W), row),
                  pl.BlockSpec((1, HEADS, CHUNK, CHUNK), lambda c: (nc - 1 - c, 0, 0, 0)),
                  pl.BlockSpec((1, W, HEAD_DIM), lambda c: (nc - 1 - c, 0, 0)), pl.BlockSpec((CHUNK, W), row)],
        out_specs=[pl.BlockSpec((CHUNK, 3 * W), row), pl.BlockSpec((CHUNK, LANE), row),
                   pl.BlockSpec((1, HEADS, CHUNK), lambda c: (nc - 1 - c, 0, 0))],
        out_shape=[jax.ShapeDtypeStruct((L, 3 * W), f32), jax.ShapeDtypeStruct((L, LANE), f32),
                   jax.ShapeDtypeStruct((nc, HEADS, CHUNK), f32)],
        scratch_shapes=[pltpu.VMEM((HEADS, HEAD_DIM, HEAD_DIM), f32)],
        compiler_params=_cparams("arbitrary"),
    )(qkv, sc, gr, u_all, w_all, vn_all, t_all, sp_all, do_all)


def _qkv_bwd(proj, cw, dn):
    L = proj.shape[0]

    def body(x_ref, cw_ref, dn_ref, dx_ref, gcw_ref):
        j = pl.program_id(0)
        x, dn_v = x_ref[...], dn_ref[...]
        c = _conv4(x, cw_ref)
        sg = _sigmoid(c)
        a = c * sg
        rn = lax.rsqrt(jnp.sum(a * a, axis=1, keepdims=True) + EPS)
        scale = jnp.where(j < HEADS, HEAD_DIM ** -0.5, 1.0).astype(f32)
        da_n = (scale * rn) * (dn_v - a * ((rn * rn) * jnp.sum(dn_v * a, axis=1, keepdims=True)))
        da = jnp.where(j < 2 * HEADS, da_n, dn_v)
        dc = da * (sg * (1.0 + c * (1.0 - sg)))
        gcw_ref[3:4, :] = jnp.sum(dc * x, axis=0, keepdims=True)
        gcw_ref[2:3, :] = jnp.sum(dc * _shift_down(x, 1), axis=0, keepdims=True)
        gcw_ref[1:2, :] = jnp.sum(dc * _shift_down(x, 2), axis=0, keepdims=True)
        gcw_ref[0:1, :] = jnp.sum(dc * _shift_down(x, 3), axis=0, keepdims=True)
        dx = (cw_ref[3:4, :] * dc + cw_ref[2:3, :] * _shift_up(dc, 1) + cw_ref[1:2, :] * _shift_up(dc, 2)
              + cw_ref[0:1, :] * _shift_up(dc, 3))
        dx_ref[...] = dx.astype(bf16)

    col = pl.BlockSpec((L, LANE), lambda j: (0, j))
    wspec = pl.BlockSpec((4, LANE), lambda j: (0, j))
    return _pcall(
        body, name="qkv_bwd", grid=(3 * HEADS,),
        in_specs=[col, wspec, col], out_specs=[col, wspec],
        out_shape=[jax.ShapeDtypeStruct((L, 3 * GDN_WIDTH), bf16), jax.ShapeDtypeStruct((4, 3 * GDN_WIDTH), f32)],
        compiler_params=_cparams("parallel"),
    )(proj, cw, dn)


def _scalars_bwd(proj, alog_p, dtb_p, dsc, dgr_col):
    L = proj.shape[0]

    def body(x_ref, al_ref, dt_ref, dsc_ref, dgr_ref, dba_ref, gs_ref):
        x, dsc_v = x_ref[...], dsc_ref[...]
        lane = _lanes(x.shape)
        dec = (lane >= HEADS) & (lane < 2 * HEADS)
        dg = jnp.where(dec, dsc_v - dgr_ref[...], 0.0)
        rc = _rows(x.shape) & (CHUNK - 1)
        for s in (1, 2, 4, 8, 16, 32):
            dg = dg + jnp.where(rc + s < CHUNK, pltpu.roll(dg, L - s, 0), 0.0)
        xa = x + dt_ref[...]
        ea = jnp.exp(al_ref[...])
        g = -ea * _softplus(xa)
        da = dg * (-ea) * _sigmoid(xa)
        beta = _sigmoid(x)
        db = dsc_v * beta * (1.0 - beta)
        dba_ref[...] = jnp.where(lane < HEADS, db, jnp.where(dec, da, 0.0)).astype(bf16)
        gs_ref[...] = jnp.zeros_like(gs_ref)
        gs_ref[0:1, :] = jnp.sum(jnp.where(dec, dg * g, 0.0), axis=0, keepdims=True)
        gs_ref[1:2, :] = jnp.sum(jnp.where(dec, da, 0.0), axis=0, keepdims=True)

    full = pl.BlockSpec((L, LANE), lambda i: (0, 0))
    vec = pl.BlockSpec((1, LANE), lambda i: (0, 0))
    return _pcall(
        body, name="scalars_bwd", grid=(1,),
        in_specs=[pl.BlockSpec((L, LANE), lambda i: (0, OFF_BA // LANE)), vec, vec, full, full],
        out_specs=[full, pl.BlockSpec((8, LANE), lambda i: (0, 0))],
        out_shape=[jax.ShapeDtypeStruct((L, LANE), bf16), jax.ShapeDtypeStruct((8, LANE), f32)],
        compiler_params=_cparams("arbitrary"),
    )(proj, alog_p, dtb_p, dsc, dgr_col)


def _input_grad(pieces, offs, wpad, x, nw, dy):
    L = x.shape[0]
    tm = min(256, L)
    npc = len(pieces)

    def body(*refs):
        p_refs = refs[:npc]
        w_hbm, x_ref, nw_ref, dy_ref, gx_ref, gnw_ref, w_vmem, sem = refs[npc:]

        @pl.when(pl.program_id(0) == 0)
        def _():
            cp = pltpu.make_async_copy(w_hbm, w_vmem, sem)
            cp.start()
            cp.wait()
            gnw_ref[...] = jnp.zeros_like(gnw_ref)
        dh = None
        for p_ref, off in zip(p_refs, offs):
            wd = p_ref.shape[1]
            part = lax.dot_general(p_ref[...], w_vmem[:, off:off + wd], (((1,), (1,)), ((), ())), preferred_element_type=f32)
            dh = part if dh is None else dh + part
        xv, nwv = x_ref[...], nw_ref[...]
        r = lax.rsqrt(jnp.mean(xv * xv, axis=-1, keepdims=True) + EPS)
        xh = xv * r
        gnw_ref[...] += jnp.sum(dh * xh, axis=0, keepdims=True)
        dxh = dh * nwv
        gx_ref[...] = dy_ref[...] + r * (dxh - xh * jnp.mean(dxh * xh, axis=-1, keepdims=True))

    row = lambda i: (i, 0)
    fix = lambda i: (0, 0)
    return _pcall(
        body, name="input_grad", grid=(L // tm,),
        in_specs=[pl.BlockSpec((tm, p.shape[1]), row) for p in pieces] + [
            ANY, pl.BlockSpec((tm, D_MODEL), row), pl.BlockSpec((1, D_MODEL), fix), pl.BlockSpec((tm, D_MODEL), row)],
        out_specs=[pl.BlockSpec((tm, D_MODEL), row), pl.BlockSpec((1, D_MODEL), fix)],
        out_shape=[jax.ShapeDtypeStruct((L, D_MODEL), f32), jax.ShapeDtypeStruct((1, D_MODEL), f32)],
        scratch_shapes=[pltpu.VMEM(wpad.shape, bf16), pltpu.SemaphoreType.DMA(())],
        compiler_params=_cparams("arbitrary"),
    )(*pieces, wpad, x, nw, dy)


def _adamw_reduce(parts, w, m, v, name):
    R, C = w.shape
    tr = 128 if R % 128 == 0 else R
    c1 = 1.0 - ADAM_B1 ** ADAM_STEP
    c2 = 1.0 - ADAM_B2 ** ADAM_STEP

    def body(p_ref, w_ref, m_ref, v_ref, g_ref, d_ref, nm_ref, nv_ref):
        g = p_ref[0]
        for s in range(1, N_DEV):
            g = g + p_ref[s]
        nm = ADAM_B1 * m_ref[...] + (1.0 - ADAM_B1) * g
        nv = ADAM_B2 * v_ref[...] + (1.0 - ADAM_B2) * (g * g)
        g_ref[...] = g
        nm_ref[...] = nm
        nv_ref[...] = nv
        d_ref[...] = -ADAM_LR * ((nm / c1) / (jnp.sqrt(nv / c2) + ADAM_EPS) + ADAM_WD * w_ref[...])

    blk = pl.BlockSpec((tr, C), lambda i: (i, 0))
    out = jax.ShapeDtypeStruct((R, C), f32)
    return _pcall(
        body, name=name, grid=(R // tr,),
        in_specs=[pl.BlockSpec((N_DEV, tr, C), lambda i: (0, i, 0)), blk, blk, blk],
        out_specs=[blk] * 4, out_shape=[out] * 4,
        compiler_params=_cparams("parallel"),
    )(parts, w, m, v)


def _pad_lanes(vec8, start):
    return jnp.pad(vec8.reshape(1, -1), ((0, 0), (start, LANE - start - vec8.size)))


def kernel(x, norm_in_w, w_in, conv_qkv_w, A_log, dt_bias, gdn_norm_w, conv_w, conv_b, w_out, final_norm_w, loss_target, m_norm_in_w, m_w_in, m_conv_qkv_w, m_A_log, m_dt_bias, m_gdn_norm_w, m_conv_w, m_conv_b, m_w_out, m_final_norm_w, v_norm_in_w, v_w_in, v_conv_qkv_w, v_A_log, v_dt_bias, v_gdn_norm_w, v_conv_w, v_conv_b, v_w_out, v_final_norm_w):
    L = x.shape[1]
    nc = L // CHUNK
    xs = x[0]
    tgt = loss_target[0]
    fnw = final_norm_w.reshape(1, D_MODEL)

    win_g, wout_g, cqkv_g, cw_g = _all_gather(
        [w_in[0].astype(bf16), w_out[0].astype(bf16), conv_qkv_w[0], conv_w[0]], "gather_weights")
    w_nat = jnp.concatenate([win_g[d] for d in range(N_DEV)], axis=1)
    wpad = jnp.concatenate([w_nat[:, :NAT_SMALL_END], jnp.zeros((D_MODEL, PAD_COLS), bf16), w_nat[:, NAT_SMALL_END:]], axis=1)
    wo = wout_g.reshape(N_DEV * wout_g.shape[1], D_MODEL)
    cqkv = jnp.concatenate([cqkv_g[d] for d in range(N_DEV)], axis=1)
    cw = jnp.concatenate([cw_g[d] for d in range(N_DEV)], axis=1)
    alog_p = _pad_lanes(A_log, HEADS)
    dtb_p = _pad_lanes(dt_bias, HEADS)

    proj, h = _in_proj(xs, norm_in_w, wpad)
    qkv = _qkv_act(proj, cqkv)
    sc, gr = _scalars(proj, alog_p, dtb_p)
    o, u_all, w_all, vn_all, t_all, sp_all = _gdn_fwd(qkv, sc, gr)
    mix_a = _gdn_gate(o, proj, gdn_norm_w)
    mix_b = _conv_fwd(proj, cw, conv_b)
    dy, dyb, dmix_a, dmix_b, g_fnw, loss_v = _out_proj_loss(xs, mix_a, mix_b, wo, fnw, tgt)

    g_wout = jnp.concatenate([_tn_matmul(mix_a, dyb, "grad_w_out_a"), _tn_matmul(mix_b, dyb, "grad_w_out_b")], axis=0)
    do, dzg, g_gnw = _gdn_gate_bwd(o, proj, gdn_norm_w, dmix_a)
    d_b, d_c, d_hc, d_zc, g_cw, g_cb = _conv_bwd(proj, cw, conv_b, dmix_b)
    dqkv_n, dsc, dgr = _gdn_bwd(qkv, sc, gr, u_all, w_all, vn_all, t_all, sp_all, do)
    dqkv, g_cqkv = _qkv_bwd(proj, cqkv, dqkv_n)
    dgr_col = jnp.pad(dgr.transpose(0, 2, 1).reshape(L, HEADS), ((0, 0), (HEADS, LANE - 2 * HEADS)))
    dba, g_sc = _scalars_bwd(proj, alog_p, dtb_p, dsc, dgr_col)
    pieces = [dqkv, dzg, dba, d_b, d_c, d_hc, d_zc]
    offs = [OFF_QKV, OFF_ZG, OFF_BA, OFF_B, OFF_C, OFF_HC, OFF_ZC]
    grad_x, g_nw = _input_grad(pieces, offs, wpad, xs, norm_in_w, dy)
    g_parts = [_tn_matmul(h, p, "grad_w_in_%d" % i) for i, p in enumerate(pieces)]
    g_parts[2] = g_parts[2][:, :2 * HEADS]
    g_win_nat = jnp.concatenate(g_parts, axis=1)
    g_win_blk = jnp.stack([g_win_nat[:, d * SHARD_W:(d + 1) * SHARD_W] for d in range(N_DEV)], axis=0)

    r_win, r_wout, r_cqkv, r_cw = _all_to_all(
        [g_win_blk, g_wout.reshape(N_DEV, -1, D_MODEL), g_cqkv.reshape(4, N_DEV, -1).transpose(1, 0, 2),
         g_cw.reshape(3, N_DEV, -1).transpose(1, 0, 2)], "exchange_grads")
    upd_win = _adamw_reduce(r_win, w_in[0], m_w_in[0], v_w_in[0], "adamw_w_in")
    upd_wout = _adamw_reduce(r_wout, w_out[0], m_w_out[0], v_w_out[0], "adamw_w_out")
    upd_cqkv = _adamw_reduce(r_cqkv, conv_qkv_w[0], m_conv_qkv_w[0], v_conv_qkv_w[0], "adamw_conv_qkv_w")
    upd_cw = _adamw_reduce(r_cw, conv_w[0], m_conv_w[0], v_conv_w[0], "adamw_conv_w")

    def pack(nw_, cb_, fw_, gn_, al_, dt_, ls_):
        return jnp.concatenate([nw_.reshape(1, -1), cb_.reshape(1, -1), fw_.reshape(1, -1), gn_.reshape(1, -1),
                                _pad_lanes(al_, 0), _pad_lanes(dt_, 0), ls_.reshape(1, -1)], axis=1)

    zl = jnp.zeros((1, LANE), f32)
    small_g = pack(g_nw, g_cb, g_fnw, g_gnw, g_sc[0, HEADS:2 * HEADS], g_sc[1, HEADS:2 * HEADS], loss_v)
    (small_all,) = _all_gather([small_g], "gather_small_grads")
    upd_small = _adamw_reduce(
        small_all, pack(norm_in_w, conv_b, final_norm_w, gdn_norm_w, A_log, dt_bias, zl),
        pack(m_norm_in_w, m_conv_b, m_final_norm_w, m_gdn_norm_w, m_A_log, m_dt_bias, zl),
        pack(v_norm_in_w, v_conv_b, v_final_norm_w, v_gdn_norm_w, v_A_log, v_dt_bias, zl), "adamw_small")

    o_nw, o_cb, o_fw, o_gn = 0, D_MODEL, 2 * D_MODEL, 3 * D_MODEL
    o_al, o_dt, o_ls = o_gn + LANE, o_gn + 2 * LANE, o_gn + 3 * LANE

    def unpack(k):
        s = upd_small[k]
        return dict(
            norm_in_w=s[:, o_nw:o_nw + D_MODEL], conv_b=s[:, o_cb:o_cb + D_MODEL], final_norm_w=s[0, o_fw:o_fw + D_MODEL],
            gdn_norm_w=s[:, o_gn:o_gn + LANE], A_log=s[:, o_al:o_al + HEADS], dt_bias=s[:, o_dt:o_dt + HEADS])

    loss = upd_small[0][0, o_ls]
    outs = [loss, grad_x[None]]
    for k in range(4):
        sm = unpack(k)
        outs += [sm["norm_in_w"], upd_win[k][None], upd_cqkv[k][None], sm["A_log"], sm["dt_bias"], sm["gdn_norm_w"],
                 upd_cw[k][None], sm["conv_b"], upd_wout[k][None], sm["final_norm_w"]]
    return tuple(outs)
```

```python
import functools
import math

import jax
import jax.numpy as jnp
from jax import lax
from jax.experimental import pallas as pl
from jax.experimental.pallas import tpu as pltpu

f32 = jnp.float32
bf16 = jnp.bfloat16

N_DEV = 8
D_MODEL = 1024
HEADS = 8
HEAD_DIM = 128
CHUNK = 64
GDN_WIDTH = HEADS * HEAD_DIM
CONV_WIDTH = 1024
PROJ_WIDTH = 8208
SHARD_W = PROJ_WIDTH // N_DEV
EPS = 1e-6

NAT_SMALL_END = 4112
PAD_COLS = 112
OFF_QKV, OFF_ZG, OFF_BA, OFF_B, OFF_C, OFF_HC, OFF_ZC = 0, 3072, 4096, 4224, 5248, 6272, 7296
PROJ_PAD = 8320
LANE = 128

ADAM_LR, ADAM_B1, ADAM_B2, ADAM_EPS, ADAM_WD, ADAM_STEP = 0.001, 0.9, 0.999, 1e-08, 0.01, 10

VMEM_LIMIT = 56 * 1024 * 1024

MESH = pl.DeviceIdType.MESH
ANY = pl.BlockSpec(memory_space=pl.ANY)


def _pcall(body, **kw):
    return pl.pallas_call(body, **kw)


def _cparams(*sem):
    return pltpu.CompilerParams(dimension_semantics=sem if sem else None, vmem_limit_bytes=VMEM_LIMIT)


def _mm(a, b):
    return jnp.dot(a.astype(bf16), b.astype(bf16), preferred_element_type=f32)


def _mm_nt(a, b):
    return lax.dot_general(a.astype(bf16), b.astype(bf16), (((1,), (1,)), ((), ())), preferred_element_type=f32)


def _mm_tn(a, b):
    return lax.dot_general(a.astype(bf16), b.astype(bf16), (((0,), (0,)), ((), ())), preferred_element_type=f32)


def _rows(shape):
    return lax.broadcasted_iota(jnp.int32, shape, 0)


def _lanes(shape):
    return lax.broadcasted_iota(jnp.int32, shape, 1)


def _shift_down(x, s):
    if s == 0:
        return x
    return jnp.where(_rows(x.shape) >= s, pltpu.roll(x, s, 0), 0.0)


def _shift_up(x, s):
    if s == 0:
        return x
    n = x.shape[0]
    return jnp.where(_rows(x.shape) < n - s, pltpu.roll(x, n - s, 0), 0.0)


def _sigmoid(x):
    return jax.nn.sigmoid(x)


def _softplus(x):
    e = jnp.exp(-jnp.abs(x))
    small = e * (1.0 - e * (0.5 - e * (1.0 / 3.0)))
    return jnp.maximum(x, 0.0) + jnp.where(e < 0.01, small, jnp.log(1.0 + e))


def _mesh_pos():
    return lax.axis_index("x"), lax.axis_index("y"), lax.axis_index("c")


def _flat(px, py, pc):
    return 4 * px + 2 * py + pc


def _all_gather(xs, name):
    n = len(xs)

    def body(*refs):
        x_refs, o_refs = refs[:n], refs[n:2 * n]
        send_sems, recv_sems, local_sems = refs[2 * n:]
        x, y, c = _mesh_pos()
        me, sibling = (x, y, c), (x, y, 1 - c)
        chips = [(1 - x, y), (x, 1 - y), (1 - x, 1 - y)]

        def copy(a, k, block, to, src=None):
            dst = o_refs[a].at[_flat(*block)]
            return pltpu.make_async_remote_copy(
                src_ref=dst if src is None else src, dst_ref=dst,
                send_sem=send_sems.at[a, k], recv_sem=recv_sems.at[a, k], device_id=to, device_id_type=MESH)

        mine, first, passed = [], [], []
        for a in range(n):
            cp = pltpu.make_async_copy(x_refs[a], o_refs[a].at[_flat(*me)], local_sems.at[a])
            cp.start()
            mine.append(cp)
            fa = [copy(a, 0, me, sibling, src=x_refs[a])]
            fa += [copy(a, 1 + j, me, (*chip, c), src=x_refs[a]) for j, chip in enumerate(chips)]
            for cp in fa:
                cp.start()
            first += fa
        for a in range(n):
            for j, chip in enumerate(chips):
                copy(a, 1 + j, (*chip, c), me).wait_recv()
                cp = copy(a, 4 + j, (*chip, c), sibling)
                cp.start()
                passed.append(cp)
        for a in range(n):
            copy(a, 0, sibling, me).wait_recv()
            for j, chip in enumerate(chips):
                copy(a, 4 + j, (*chip, 1 - c), me).wait_recv()
        for cp in first + passed:
            cp.wait_send()
        for cp in mine:
            cp.wait()

    outs = _pcall(
        body, name=name,
        out_shape=[jax.ShapeDtypeStruct((N_DEV,) + a.shape, a.dtype) for a in xs],
        in_specs=[ANY] * n, out_specs=[ANY] * n,
        scratch_shapes=[pltpu.SemaphoreType.DMA((n, 7)), pltpu.SemaphoreType.DMA((n, 7)), pltpu.SemaphoreType.DMA((n,))],
    )(*xs)
    return list(outs)


def _all_to_all(gs, name):
    n = len(gs)

    def body(*refs):
        g_refs, o_refs = refs[:n], refs[n:2 * n]
        send_sems, recv_sems, local_sems = refs[2 * n:]
        x, y, c = _mesh_pos()
        me = _flat(x, y, c)
        peers = []
        for k in range(1, N_DEV):
            kx, ky, kc = (k >> 2) & 1, (k >> 1) & 1, k & 1
            px = (1 - x) if kx else x
            py = (1 - y) if ky else y
            pc = (1 - c) if kc else c
            peers.append((px, py, pc))

        def copy(a, k):
            peer = peers[k - 1]
            return pltpu.make_async_remote_copy(
                src_ref=g_refs[a].at[_flat(*peer)], dst_ref=o_refs[a].at[me],
                send_sem=send_sems.at[a, k - 1], recv_sem=recv_sems.at[a, k - 1], device_id=peer, device_id_type=MESH)

        def arrival(a, k):
            peer = peers[k - 1]
            return pltpu.make_async_remote_copy(
                src_ref=g_refs[a].at[me], dst_ref=o_refs[a].at[_flat(*peer)],
                send_sem=send_sems.at[a, k - 1], recv_sem=recv_sems.at[a, k - 1], device_id=peer, device_id_type=MESH)

        mine, sent = [], []
        for a in range(n):
            cp = pltpu.make_async_copy(g_refs[a].at[me], o_refs[a].at[me], local_sems.at[a])
            cp.start()
            mine.append(cp)
            for k in range(1, N_DEV):
                cp = copy(a, k)
                cp.start()
                sent.append(cp)
        for a in range(n):
            for k in range(1, N_DEV):
                arrival(a, k).wait_recv()
        for cp in sent:
            cp.wait_send()
        for cp in mine:
            cp.wait()

    outs = _pcall(
        body, name=name,
        out_shape=[jax.ShapeDtypeStruct(a.shape, a.dtype) for a in gs],
        in_specs=[ANY] * n, out_specs=[ANY] * n,
        scratch_shapes=[pltpu.SemaphoreType.DMA((n, 7)), pltpu.SemaphoreType.DMA((n, 7)), pltpu.SemaphoreType.DMA((n,))],
    )(*gs)
    return list(outs)


def _pair_exchange(gs, name):
    n = len(gs)
    chips = [(0, 0), (0, 1), (1, 0), (1, 1)]

    def body(*refs):
        g_refs, o_refs = refs[:n], refs[n:2 * n]
        send_sems, recv_sems = refs[2 * n:]
        x, y, c = _mesh_pos()
        sibling = (x, y, 1 - c)

        def copy(a, i):
            xp, yp = chips[i]
            return pltpu.make_async_remote_copy(
                src_ref=g_refs[a].at[_flat(xp, yp, 1 - c)], dst_ref=o_refs[a].at[i],
                send_sem=send_sems.at[a, i], recv_sem=recv_sems.at[a, i], device_id=sibling, device_id_type=MESH)

        cps = [copy(a, i) for a in range(n) for i in range(4)]
        for cp in cps:
            cp.start()
        for cp in cps:
            cp.wait()

    outs = _pcall(
        body, name=name,
        out_shape=[jax.ShapeDtypeStruct((4,) + a.shape[1:], a.dtype) for a in gs],
        in_specs=[ANY] * n, out_specs=[ANY] * n,
        scratch_shapes=[pltpu.SemaphoreType.DMA((n, 4)), pltpu.SemaphoreType.DMA((n, 4))],
    )(*gs)
    return list(outs)


def _pair_sum(g, p1, name):
    _, R, C = g.shape
    tr = 256 if R % 256 == 0 else R
    cidx = lax.axis_index("c").astype(jnp.int32).reshape(1)

    def body(c_ref, g_ref, p_ref, o_ref):
        o_ref[...] = (g_ref[...].astype(f32) + p_ref[...].astype(f32)).astype(o_ref.dtype)

    return _pcall(
        body, name=name,
        grid_spec=pltpu.PrefetchScalarGridSpec(
            num_scalar_prefetch=1, grid=(4, R // tr),
            in_specs=[pl.BlockSpec((1, tr, C), lambda i, r, c_ref: (2 * i + c_ref[0], r, 0)),
                      pl.BlockSpec((1, tr, C), lambda i, r, c_ref: (i, r, 0))],
            out_specs=pl.BlockSpec((1, tr, C), lambda i, r, c_ref: (i, r, 0))),
        out_shape=jax.ShapeDtypeStruct((4, R, C), g.dtype),
        compiler_params=_cparams("parallel", "parallel"),
    )(cidx, g, p1)


def _quad_exchange(ss, name):
    n = len(ss)

    def body(*refs):
        s_refs, o_refs = refs[:n], refs[n:2 * n]
        send_sems, recv_sems, local_sems = refs[2 * n:]
        x, y, c = _mesh_pos()
        me = 2 * x + y
        peers = [(1 - x, y), (x, 1 - y), (1 - x, 1 - y)]

        def copy(a, j, arriving):
            px, py = peers[j]
            src_slot, dst_slot = (me, 2 * px + py) if arriving else (2 * px + py, me)
            return pltpu.make_async_remote_copy(
                src_ref=s_refs[a].at[src_slot], dst_ref=o_refs[a].at[dst_slot],
                send_sem=send_sems.at[a, j], recv_sem=recv_sems.at[a, j], device_id=(px, py, c), device_id_type=MESH)

        mine, sent = [], []
        for a in range(n):
            cp = pltpu.make_async_copy(s_refs[a].at[me], o_refs[a].at[me], local_sems.at[a])
            cp.start()
            mine.append(cp)
            for j in range(3):
                cp = copy(a, j, False)
                cp.start()
                sent.append(cp)
        for a in range(n):
            for j in range(3):
                copy(a, j, True).wait_recv()
        for cp in sent:
            cp.wait_send()
        for cp in mine:
            cp.wait()

    outs = _pcall(
        body, name=name,
        out_shape=[jax.ShapeDtypeStruct(a.shape, a.dtype) for a in ss],
        in_specs=[ANY] * n, out_specs=[ANY] * n,
        scratch_shapes=[pltpu.SemaphoreType.DMA((n, 3)), pltpu.SemaphoreType.DMA((n, 3)), pltpu.SemaphoreType.DMA((n,))],
    )(*ss)
    return list(outs)


PIECE_NAT = (0, 3072, 4096, 4112, 5136, 6160, 7184, PROJ_WIDTH)


def _relayout_w_in(win_g):
    rows = win_g.shape[1]

    def body(g_ref, o_ref):
        for r in range(0, rows, 256):
            rs = slice(r, r + 256)
            o_ref[rs, NAT_SMALL_END:NAT_SMALL_END + PAD_COLS] = jnp.zeros((256, PAD_COLS), o_ref.dtype)
            for d in range(N_DEV):
                n0, n1 = d * SHARD_W, (d + 1) * SHARD_W
                cut = min(max(NAT_SMALL_END - n0, 0), SHARD_W)
                if cut > 0:
                    o_ref[rs, n0:n0 + cut] = g_ref[d, rs, 0:cut]
                if cut < SHARD_W:
                    o_ref[rs, n0 + cut + PAD_COLS:n1 + PAD_COLS] = g_ref[d, rs, cut:SHARD_W]

    return _pcall(
        body, name="relayout_w_in",
        out_shape=jax.ShapeDtypeStruct((rows, PROJ_PAD), win_g.dtype),
        in_specs=[pl.BlockSpec(memory_space=pltpu.VMEM)], out_specs=pl.BlockSpec(memory_space=pltpu.VMEM),
        compiler_params=_cparams(),
    )(win_g)


def _grad_blocks(g_parts):
    rows = g_parts[0].shape[0]
    tr = 256
    npc = len(g_parts)

    def body(*refs):
        p_refs, o_ref = refs[:npc], refs[npc]
        for d in range(N_DEV):
            n0, n1 = d * SHARD_W, (d + 1) * SHARD_W
            for i in range(npc):
                lo, hi = max(n0, PIECE_NAT[i]), min(n1, PIECE_NAT[i + 1])
                if lo < hi:
                    o_ref[d, :, lo - n0:hi - n0] = p_refs[i][:, lo - PIECE_NAT[i]:hi - PIECE_NAT[i]].astype(o_ref.dtype)

    return _pcall(
        body, name="grad_blocks", grid=(rows // tr,),
        in_specs=[pl.BlockSpec((tr, p.shape[1]), lambda i: (i, 0)) for p in g_parts],
        out_specs=pl.BlockSpec((N_DEV, tr, SHARD_W), lambda i: (0, i, 0)),
        out_shape=jax.ShapeDtypeStruct((N_DEV, rows, SHARD_W), bf16),
        compiler_params=_cparams("parallel"),
    )(*g_parts)


def _in_proj(x, nw, wpad):
    L = x.shape[0]
    tn = 640
    nj = wpad.shape[1] // tn

    def body(x_ref, nw_ref, w_ref, proj_ref, h_ref):
        @pl.when(pl.program_id(0) == 0)
        def _():
            for r in range(0, L, 256):
                xs = x_ref[r:r + 256, :]
                ms = jnp.mean(xs * xs, axis=-1, keepdims=True)
                h_ref[r:r + 256, :] = ((xs * lax.rsqrt(ms + EPS)) * nw_ref[...]).astype(bf16)
        for r in range(0, L, 512):
            proj_ref[r:r + 512, :] = jnp.dot(h_ref[r:r + 512, :], w_ref[...], preferred_element_type=f32)

    return _pcall(
        body, name="in_proj", grid=(nj,),
        in_specs=[pl.BlockSpec((L, D_MODEL), lambda j: (0, 0)), pl.BlockSpec((1, D_MODEL), lambda j: (0, 0)),
                  pl.BlockSpec((D_MODEL, tn), lambda j: (0, j))],
        out_specs=[pl.BlockSpec((L, tn), lambda j: (0, j)), pl.BlockSpec((L, D_MODEL), lambda j: (0, 0))],
        out_shape=[jax.ShapeDtypeStruct((L, wpad.shape[1]), f32), jax.ShapeDtypeStruct((L, D_MODEL), bf16)],
        compiler_params=_cparams("arbitrary"),
    )(x, nw, wpad)


def _conv4(x, cw_ref):
    return (cw_ref[3:4, :] * x + cw_ref[2:3, :] * _shift_down(x, 1) + cw_ref[1:2, :] * _shift_down(x, 2)
            + cw_ref[0:1, :] * _shift_down(x, 3))


def _qkv_act(proj, cw):
    L = proj.shape[0]

    def body(x_ref, cw_ref, o_ref):
        j = pl.program_id(0)
        c = _conv4(x_ref[...], cw_ref)
        a = c * _sigmoid(c)
        rn = lax.rsqrt(jnp.sum(a * a, axis=1, keepdims=True) + EPS)
        scale = jnp.where(j < HEADS, HEAD_DIM ** -0.5, 1.0).astype(f32)
        o_ref[...] = jnp.where(j < 2 * HEADS, (a * rn) * scale, a)

    return _pcall(
        body, name="qkv_act", grid=(3 * HEADS,),
        in_specs=[pl.BlockSpec((L, LANE), lambda j: (0, j)), pl.BlockSpec((4, LANE), lambda j: (0, j))],
        out_specs=pl.BlockSpec((L, LANE), lambda j: (0, j)),
        out_shape=jax.ShapeDtypeStruct((L, 3 * GDN_WIDTH), f32),
        compiler_params=_cparams("parallel"),
    )(proj, cw)


def _scalars(proj, alog_p, dtb_p):
    L = proj.shape[0]
    nc = L // CHUNK

    def body(x_ref, al_ref, dt_ref, sc_ref, gr_ref):
        x = x_ref[...]
        lane = _lanes(x.shape)
        beta = _sigmoid(x)
        g = -jnp.exp(al_ref[...]) * _softplus(x + dt_ref[...])
        gc = jnp.where((lane >= HEADS) & (lane < 2 * HEADS), g, 0.0)
        rc = _rows(x.shape) & (CHUNK - 1)
        for s in (1, 2, 4, 8, 16, 32):
            gc = gc + jnp.where(rc >= s, pltpu.roll(gc, s, 0), 0.0)
        sc_ref[...] = jnp.where(lane < HEADS, beta, gc)
        sel = (_lanes((HEADS, LANE)) == _rows((HEADS, LANE)) + HEADS).astype(f32)
        for c in range(nc):
            gr_ref[c] = lax.dot_general(sel, sc_ref[c * CHUNK:(c + 1) * CHUNK, :], (((1,), (1,)), ((), ())),
                                        preferred_element_type=f32, precision=lax.Precision.HIGHEST)

    return _pcall(
        body, name="scalars", grid=(1,),
        in_specs=[pl.BlockSpec((L, LANE), lambda i: (0, OFF_BA // LANE)), pl.BlockSpec((1, LANE), lambda i: (0, 0)),
                  pl.BlockSpec((1, LANE), lambda i: (0, 0))],
        out_specs=[pl.BlockSpec((L, LANE), lambda i: (0, 0)), pl.BlockSpec((nc, HEADS, CHUNK), lambda i: (0, 0, 0))],
        out_shape=[jax.ShapeDtypeStruct((L, LANE), f32), jax.ShapeDtypeStruct((nc, HEADS, CHUNK), f32)],
        compiler_params=_cparams("arbitrary"),
    )(proj, alog_p, dtb_p)


def _head_scalars(sc, gr_ref, h):
    lane = _lanes(sc.shape)
    beta = jnp.sum(jnp.where(lane == h, sc, 0.0), axis=1, keepdims=True)
    gcc = jnp.sum(jnp.where(lane == HEADS + h, sc, 0.0), axis=1, keepdims=True)
    gcr = gr_ref[0, h:h + 1, :]
    gl = jnp.sum(jnp.where(_lanes(gcr.shape) == CHUNK - 1, gcr, 0.0), axis=1, keepdims=True)
    ii, jj = _rows((CHUNK, CHUNK)), _lanes((CHUNK, CHUNK))
    dmat = jnp.where(ii >= jj, jnp.exp(jnp.minimum(gcc - gcr, 0.0)), 0.0)
    dmat_t = jnp.where(jj >= ii, jnp.exp(jnp.minimum(gcr - gcc, 0.0)), 0.0)
    return beta, gcc, gl, dmat, dmat_t, ii, jj


def _gdn_fwd(qkv, sc, gr):
    L = qkv.shape[0]
    nc = L // CHUNK
    W = GDN_WIDTH

    def body(qkv_ref, sc_ref, gr_ref, o_ref, u_ref, w_ref, vn_ref, t_ref, sp_ref, s_scr):
        @pl.when(pl.program_id(0) == 0)
        def _():
            s_scr[...] = jnp.zeros_like(s_scr)
        sc_v = sc_ref[...]
        HS = range(HEADS)
        cs = [slice(h * HEAD_DIM, (h + 1) * HEAD_DIM) for h in HS]
        q = [qkv_ref[:, h * HEAD_DIM:(h + 1) * HEAD_DIM] for h in HS]
        k = [qkv_ref[:, W + h * HEAD_DIM:W + (h + 1) * HEAD_DIM] for h in HS]
        v = [qkv_ref[:, 2 * W + h * HEAD_DIM:2 * W + (h + 1) * HEAD_DIM] for h in HS]
        hsc = [_head_scalars(sc_v, gr_ref, h) for h in HS]
        beta, gcc, gl, dmat = ([x[i] for x in hsc] for i in range(4))
        ii, jj = hsc[0][5], hsc[0][6]
        eg = [jnp.exp(gcc[h]) for h in HS]
        kb = [k[h] * beta[h] for h in HS]
        kk = [_mm_nt(kb[h], k[h]) for h in HS]
        qk = [_mm_nt(q[h], k[h]) for h in HS]
        xp = [-jnp.where(ii > jj, kk[h] * dmat[h], 0.0) for h in HS]
        t = xp
        for _ in range(5):
            xp = [_mm(xp[h], xp[h]) for h in HS]
            tx = [_mm(t[h], xp[h]) for h in HS]
            t = [t[h] + xp[h] + tx[h] for h in HS]
        vb = [v[h] * beta[h] for h in HS]
        kbg = [kb[h] * eg[h] for h in HS]
        uw = [_mm(t[h], jnp.concatenate([vb[h], kbg[h]], axis=1)) for h in HS]
        u = [vb[h] + uw[h][:, :HEAD_DIM] for h in HS]
        w = [kbg[h] + uw[h][:, HEAD_DIM:] for h in HS]
        s = [s_scr[h] for h in HS]
        ws = [_mm(jnp.concatenate([w[h], q[h] * eg[h]], axis=0), s[h]) for h in HS]
        vn = [u[h] - ws[h][:CHUNK] for h in HS]
        p = [jnp.where(ii >= jj, qk[h] * dmat[h], 0.0) for h in HS]
        pv = [_mm(p[h], vn[h]) for h in HS]
        kv = [_mm_tn(k[h] * jnp.exp(gl[h] - gcc[h]), vn[h]) for h in HS]
        for h in HS:
            sp_ref[0, cs[h], :] = s[h]
            o_ref[:, cs[h]] = ws[h][CHUNK:] + pv[h]
            s_scr[h] = jnp.exp(gl[h]) * s[h] + kv[h]
            u_ref[:, cs[h]] = u[h]
            w_ref[:, cs[h]] = w[h]
            vn_ref[:, cs[h]] = vn[h]
            t_ref[0, h] = t[h]

    row = lambda c: (c, 0)
    act = jax.ShapeDtypeStruct((L, W), f32)
    return _pcall(
        body, name="gdn_fwd", grid=(nc,),
        in_specs=[pl.BlockSpec((CHUNK, 3 * W), row), pl.BlockSpec((CHUNK, LANE), row),
                  pl.BlockSpec((1, HEADS, CHUNK), lambda c: (c, 0, 0))],
        out_specs=[pl.BlockSpec((CHUNK, W), row)] * 4 + [
            pl.BlockSpec((1, HEADS, CHUNK, CHUNK), lambda c: (c, 0, 0, 0)),
            pl.BlockSpec((1, W, HEAD_DIM), lambda c: (c, 0, 0))],
        out_shape=[act, act, act, act, jax.ShapeDtypeStruct((nc, HEADS, CHUNK, CHUNK), f32),
                   jax.ShapeDtypeStruct((nc, W, HEAD_DIM), f32)],
        scratch_shapes=[pltpu.VMEM((HEADS, HEAD_DIM, HEAD_DIM), f32)],
        compiler_params=_cparams("arbitrary"),
    )(qkv, sc, gr)


def _gdn_gate(o, proj, gnw):
    L = o.shape[0]

    def body(o_ref, z_ref, w_ref, m_ref):
        ov, z = o_ref[...], z_ref[...]
        rms = lax.rsqrt(jnp.mean(ov * ov, axis=-1, keepdims=True) + EPS)
        m_ref[...] = (((ov * rms) * w_ref[...]) * (z * _sigmoid(z))).astype(bf16)

    return _pcall(
        body, name="gdn_gate", grid=(HEADS,),
        in_specs=[pl.BlockSpec((L, LANE), lambda j: (0, j)), pl.BlockSpec((L, LANE), lambda j: (0, OFF_ZG // LANE + j)),
                  pl.BlockSpec((1, LANE), lambda j: (0, 0))],
        out_specs=pl.BlockSpec((L, LANE), lambda j: (0, j)),
        out_shape=jax.ShapeDtypeStruct((L, GDN_WIDTH), bf16),
        compiler_params=_cparams("parallel"),
    )(o, proj, gnw)


def _conv3(u, cw_ref):
    return cw_ref[2:3, :] * u + cw_ref[1:2, :] * _shift_down(u, 1) + cw_ref[0:1, :] * _shift_down(u, 2)


def _conv_specs(L):
    blk = lambda off: pl.BlockSpec((L, LANE), lambda j, off=off: (0, off // LANE + j))
    return [blk(OFF_B), blk(OFF_C), blk(OFF_HC), blk(OFF_ZC),
            pl.BlockSpec((3, LANE), lambda j: (0, j)), pl.BlockSpec((1, LANE), lambda j: (0, j))]


def _conv_fwd(proj, cw, cb):
    L = proj.shape[0]

    def body(b_ref, c_ref, h_ref, z_ref, cw_ref, cb_ref, m_ref):
        z = z_ref[...]
        cv = _conv3(c_ref[...] * h_ref[...], cw_ref) + cb_ref[...]
        m_ref[...] = ((b_ref[...] * cv) * (z * _sigmoid(z))).astype(bf16)

    return _pcall(
        body, name="conv_fwd", grid=(CONV_WIDTH // LANE,),
        in_specs=_conv_specs(L),
        out_specs=pl.BlockSpec((L, LANE), lambda j: (0, j)),
        out_shape=jax.ShapeDtypeStruct((L, CONV_WIDTH), bf16),
        compiler_params=_cparams("parallel"),
    )(proj, proj, proj, proj, cw, cb)


def _out_proj_loss(x, mix_a, mix_b, wo, fw, tgt):
    L = x.shape[0]
    tm = min(256, L)

    def body(x_ref, ma_ref, mb_ref, wo_ref, fw_ref, t_ref, dy_ref, dyb_ref, dma_ref, dmb_ref, gfw_ref, loss_ref):
        @pl.when(pl.program_id(0) == 0)
        def _():
            gfw_ref[...] = jnp.zeros_like(gfw_ref)
            loss_ref[...] = jnp.zeros_like(loss_ref)
        y = x_ref[...] + jnp.dot(ma_ref[...], wo_ref[:GDN_WIDTH, :], preferred_element_type=f32) \
            + jnp.dot(mb_ref[...], wo_ref[GDN_WIDTH:, :], preferred_element_type=f32)
        r = lax.rsqrt(jnp.mean(y * y, axis=-1, keepdims=True) + EPS)
        yh = y * r
        fwv = fw_ref[...]
        diff = yh * fwv - t_ref[...]
        loss_ref[...] += jnp.sum(jnp.sum(diff * diff, axis=-1, keepdims=True), axis=0, keepdims=True) * (0.5 / D_MODEL)
        dout = diff * (1.0 / D_MODEL)
        gfw_ref[...] += jnp.sum(dout * yh, axis=0, keepdims=True)
        dyh = dout * fwv
        dy = r * (dyh - yh * jnp.mean(dyh * yh, axis=-1, keepdims=True))
        dy_ref[...] = dy
        dyb = dy.astype(bf16)
        dyb_ref[...] = dyb
        dma_ref[...] = lax.dot_general(dyb, wo_ref[:GDN_WIDTH, :], (((1,), (1,)), ((), ())), preferred_element_type=f32)
        dmb_ref[...] = lax.dot_general(dyb, wo_ref[GDN_WIDTH:, :], (((1,), (1,)), ((), ())), preferred_element_type=f32)

    row = lambda i: (i, 0)
    fix = lambda i: (0, 0)
    act = jax.ShapeDtypeStruct((L, D_MODEL), f32)
    return _pcall(
        body, name="out_proj_loss", grid=(L // tm,),
        in_specs=[pl.BlockSpec((tm, D_MODEL), row), pl.BlockSpec((tm, GDN_WIDTH), row), pl.BlockSpec((tm, CONV_WIDTH), row),
                  pl.BlockSpec((GDN_WIDTH + CONV_WIDTH, D_MODEL), fix), pl.BlockSpec((1, D_MODEL), fix),
                  pl.BlockSpec((tm, D_MODEL), row)],
        out_specs=[pl.BlockSpec((tm, D_MODEL), row), pl.BlockSpec((tm, D_MODEL), row), pl.BlockSpec((tm, GDN_WIDTH), row),
                   pl.BlockSpec((tm, CONV_WIDTH), row), pl.BlockSpec((1, D_MODEL), fix), pl.BlockSpec((1, LANE), fix)],
        out_shape=[act, jax.ShapeDtypeStruct((L, D_MODEL), bf16), act, act,
                   jax.ShapeDtypeStruct((1, D_MODEL), f32), jax.ShapeDtypeStruct((1, LANE), f32)],
        compiler_params=_cparams("arbitrary"),
    )(x, mix_a, mix_b, wo, fw, tgt)


def _tn_matmul(a, b, name):
    L, M = a.shape
    N = b.shape[1]
    tn = 512 if N % 512 == 0 else N

    def body(a_ref, b_ref, o_ref):
        o_ref[...] = lax.dot_general(a_ref[...], b_ref[...], (((0,), (0,)), ((), ())),
                                     preferred_element_type=f32).astype(o_ref.dtype)

    return _pcall(
        body, name=name, grid=(N // tn,),
        in_specs=[pl.BlockSpec((L, M), lambda j: (0, 0)), pl.BlockSpec((L, tn), lambda j: (0, j))],
        out_specs=pl.BlockSpec((M, tn), lambda j: (0, j)),
        out_shape=jax.ShapeDtypeStruct((M, N), bf16),
        compiler_params=_cparams("parallel"),
    )(a, b)


def _gdn_gate_bwd(o, proj, gnw, dmix_a):
    L = o.shape[0]

    def body(o_ref, z_ref, w_ref, dm_ref, do_ref, dz_ref, gw_ref):
        @pl.when(pl.program_id(0) == 0)
        def _():
            gw_ref[...] = jnp.zeros_like(gw_ref)
        ov, z, dm, wv = o_ref[...], z_ref[...], dm_ref[...], w_ref[...]
        rms = lax.rsqrt(jnp.mean(ov * ov, axis=-1, keepdims=True) + EPS)
        xh = ov * rms
        sg = _sigmoid(z)
        d_on = dm * (z * sg)
        dz_ref[...] = (dm * (xh * wv) * (sg * (1.0 + z * (1.0 - sg)))).astype(bf16)
        gw_ref[...] += jnp.sum(d_on * xh, axis=0, keepdims=True)
        dxh = d_on * wv
        do_ref[...] = rms * (dxh - xh * jnp.mean(dxh * xh, axis=-1, keepdims=True))

    return _pcall(
        body, name="gdn_gate_bwd", grid=(HEADS,),
        in_specs=[pl.BlockSpec((L, LANE), lambda j: (0, j)), pl.BlockSpec((L, LANE), lambda j: (0, OFF_ZG // LANE + j)),
                  pl.BlockSpec((1, LANE), lambda j: (0, 0)), pl.BlockSpec((L, LANE), lambda j: (0, j))],
        out_specs=[pl.BlockSpec((L, LANE), lambda j: (0, j)), pl.BlockSpec((L, LANE), lambda j: (0, j)),
                   pl.BlockSpec((1, LANE), lambda j: (0, 0))],
        out_shape=[jax.ShapeDtypeStruct((L, GDN_WIDTH), f32), jax.ShapeDtypeStruct((L, GDN_WIDTH), bf16),
                   jax.ShapeDtypeStruct((1, LANE), f32)],
        compiler_params=_cparams("arbitrary"),
    )(o, proj, gnw, dmix_a)


def _conv_bwd(proj, cw, cb, dmix_b):
    L = proj.shape[0]

    def body(b_ref, c_ref, h_ref, z_ref, cw_ref, cb_ref, dm_ref, db_ref, dc_ref, dh_ref, dz_ref, gcw_ref, gcb_ref):
        bv, cv_, hv, z, dm = b_ref[...], c_ref[...], h_ref[...], z_ref[...], dm_ref[...]
        u = cv_ * hv
        cv = _conv3(u, cw_ref) + cb_ref[...]
        sg = _sigmoid(z)
        sz = z * sg
        db_ref[...] = (dm * cv * sz).astype(bf16)
        dz_ref[...] = (dm * (bv * cv) * (sg * (1.0 + z * (1.0 - sg)))).astype(bf16)
        dcv = dm * bv * sz
        gcb_ref[...] = jnp.sum(dcv, axis=0, keepdims=True)
        gcw_ref[2:3, :] = jnp.sum(dcv * u, axis=0, keepdims=True)
        gcw_ref[1:2, :] = jnp.sum(dcv * _shift_down(u, 1), axis=0, keepdims=True)
        gcw_ref[0:1, :] = jnp.sum(dcv * _shift_down(u, 2), axis=0, keepdims=True)
        du = cw_ref[2:3, :] * dcv + cw_ref[1:2, :] * _shift_up(dcv, 1) + cw_ref[0:1, :] * _shift_up(dcv, 2)
        dc_ref[...] = (du * hv).astype(bf16)
        dh_ref[...] = (du * cv_).astype(bf16)

    col = pl.BlockSpec((L, LANE), lambda j: (0, j))
    act = jax.ShapeDtypeStruct((L, CONV_WIDTH), bf16)
    return _pcall(
        body, name="conv_bwd", grid=(CONV_WIDTH // LANE,),
        in_specs=_conv_specs(L) + [col],
        out_specs=[col, col, col, col, pl.BlockSpec((3, LANE), lambda j: (0, j)), pl.BlockSpec((1, LANE), lambda j: (0, j))],
        out_shape=[act, act, act, act, jax.ShapeDtypeStruct((3, CONV_WIDTH), f32), jax.ShapeDtypeStruct((1, CONV_WIDTH), f32)],
        compiler_params=_cparams("parallel"),
    )(proj, proj, proj, proj, cw, cb, dmix_b)


def _gdn_bwd(qkv, sc, gr, u_all, w_all, vn_all, t_all, sp_all, do_all):
    L = qkv.shape[0]
    nc = L // CHUNK
    W = GDN_WIDTH

    def body(qkv_ref, sc_ref, gr_ref, u_ref, w_ref, vn_ref, t_ref, sp_ref, do_ref, dqkv_ref, dsc_ref, dgr_ref, ds_scr):
        @pl.when(pl.program_id(0) == 0)
        def _():
            ds_scr[...] = jnp.zeros_like(ds_scr)
        sc_v = sc_ref[...]
        lane = _lanes(sc_v.shape)
        dsc = jnp.zeros(sc_v.shape, f32)
        HS = range(HEADS)
        cs = [slice(h * HEAD_DIM, (h + 1) * HEAD_DIM) for h in HS]
        q = [qkv_ref[:, h * HEAD_DIM:(h + 1) * HEAD_DIM] for h in HS]
        k = [qkv_ref[:, W + h * HEAD_DIM:W + (h + 1) * HEAD_DIM] for h in HS]
        v = [qkv_ref[:, 2 * W + h * HEAD_DIM:2 * W + (h + 1) * HEAD_DIM] for h in HS]
        hsc = [_head_scalars(sc_v, gr_ref, h) for h in HS]
        beta, gcc, gl, dmat, dmat_t = ([x[i] for x in hsc] for i in range(5))
        ii, jj = hsc[0][5], hsc[0][6]
        eg = [jnp.exp(gcc[h]) for h in HS]
        ekl = [jnp.exp(gl[h] - gcc[h]) for h in HS]
        egl = [jnp.exp(gl[h]) for h in HS]
        kb = [k[h] * beta[h] for h in HS]
        ks = [k[h] * ekl[h] for h in HS]
        do = [do_ref[:, cs[h]] for h in HS]
        vn = [vn_ref[:, cs[h]] for h in HS]
        s = [sp_ref[0, cs[h], :] for h in HS]
        dsn = [ds_scr[h] for h in HS]

        kq = [_mm_nt(k[h], q[h]) for h in HS]
        ksd = [_mm(ks[h], dsn[h]) for h in HS]
        p_t = [jnp.where(jj >= ii, kq[h] * dmat_t[h], 0.0) for h in HS]
        ptd = [_mm(p_t[h], do[h]) for h in HS]
        dvn = [ptd[h] + ksd[h] for h in HS]
        dodv = [jnp.concatenate([do[h], dvn[h]], axis=0) for h in HS]
        x1 = [_mm_nt(dodv[h], s[h]) for h in HS]
        dks = [_mm_nt(vn[h], dsn[h]) for h in HS]
        dov = [_mm_nt(do[h], vn[h]) for h in HS]
        vdo = [_mm_nt(vn[h], do[h]) for h in HS]
        kk = [_mm_nt(kb[h], k[h]) for h in HS]
        qk = [_mm_nt(q[h], k[h]) for h in HS]
        w = [w_ref[:, cs[h]] for h in HS]
        qd = [q[h] * eg[h] for h in HS]
        dsq = [_mm_tn(jnp.concatenate([qd[h], -w[h]], axis=0), dodv[h]) for h in HS]
        dgl = [egl[h] * jnp.sum(jnp.sum(s[h] * dsn[h], axis=1, keepdims=True), axis=0, keepdims=True) for h in HS]
        for h in HS:
            ds_scr[h] = egl[h] * dsn[h] + dsq[h]
        dqd = [x1[h][:CHUNK] for h in HS]
        duw = [jnp.concatenate([dvn[h], -x1[h][CHUNK:]], axis=1) for h in HS]
        tdu = [_mm_tn(t_ref[0, h], duw[h]) for h in HS]
        dvk = [duw[h] + tdu[h] for h in HS]
        uw = [jnp.concatenate([u_ref[:, cs[h]], w[h]], axis=1) for h in HS]
        da = [-jnp.where(ii > jj, _mm_nt(dvk[h], uw[h]), 0.0) for h in HS]
        da_t = [-jnp.where(jj > ii, _mm_nt(uw[h], dvk[h]), 0.0) for h in HS]
        dp = [jnp.where(ii >= jj, dov[h], 0.0) for h in HS]
        dp_t = [jnp.where(jj >= ii, vdo[h], 0.0) for h in HS]
        r1 = [_mm(jnp.concatenate([da[h] * dmat[h], dp[h] * dmat[h]], axis=0), k[h]) for h in HS]
        dk1 = [_mm(jnp.concatenate([da_t[h] * dmat_t[h], dp_t[h] * dmat_t[h]], axis=1),
                   jnp.concatenate([kb[h], q[h]], axis=0)) for h in HS]
        dsc = jnp.zeros(sc_v.shape, f32)
        for h in HS:
            a = jnp.where(ii > jj, kk[h] * dmat[h], 0.0)
            p = jnp.where(ii >= jj, qk[h] * dmat[h], 0.0)
            gmat = da[h] * a + dp[h] * p
            dvb, dkbg = dvk[h][:, :HEAD_DIM], dvk[h][:, HEAD_DIM:]
            kbg = kb[h] * eg[h]
            dkb = r1[h][:CHUNK] + dkbg * eg[h]
            dq = r1[h][CHUNK:] + dqd[h] * eg[h]
            dk = dk1[h] + dks[h] * ekl[h] + dkb * beta[h]
            dbeta = jnp.sum(dkb * k[h] + dvb * v[h], axis=1, keepdims=True)
            ksum = jnp.sum(dks[h] * ks[h], axis=1, keepdims=True)
            dgl_tot = dgl[h] + jnp.sum(ksum, axis=0, keepdims=True)
            dgc = jnp.sum(gmat, axis=1, keepdims=True) + jnp.sum(dqd[h] * qd[h] + dkbg * kbg, axis=1, keepdims=True) - ksum
            dgc = dgc + jnp.where(_rows(dgc.shape) == CHUNK - 1, dgl_tot, 0.0)
            dqkv_ref[:, h * HEAD_DIM:(h + 1) * HEAD_DIM] = dq
            dqkv_ref[:, W + h * HEAD_DIM:W + (h + 1) * HEAD_DIM] = dk
            dqkv_ref[:, 2 * W + h * HEAD_DIM:2 * W + (h + 1) * HEAD_DIM] = dvb * beta[h]
            dsc = jnp.where(lane == h, dbeta, jnp.where(lane == HEADS + h, dgc, dsc))
            dgr_ref[0, h:h + 1, :] = jnp.sum(gmat, axis=0, keepdims=True)
        dsc_ref[...] = dsc

    row = lambda c: (nc - 1 - c, 0)
    return _pcall(
        body, name="gdn_bwd", grid=(nc,),
        in_specs=[pl.BlockSpec((CHUNK, 3 * W), row), pl.BlockSpec((CHUNK, LANE), row),
                  pl.BlockSpec((1, HEADS, CHUNK), lambda c: (nc - 1 - c, 0, 0)),
                  pl.BlockSpec((CHUNK, W), row), pl.BlockSpec((CHUNK, W), row), pl.BlockSpec((CHUNK, W), row),
                  pl.BlockSpec((1, HEADS, CHUNK, CHUNK), lambda c: (nc - 1 - c, 0, 0, 0)),
                  pl.BlockSpec((1, W, HEAD_DIM), lambda c: (nc - 1 - c, 0, 0)), pl.BlockSpec((CHUNK, W), row)],
        out_specs=[pl.BlockSpec((CHUNK, 3 * W), row), pl.BlockSpec((CHUNK, LANE), row),
                   pl.BlockSpec((1, HEADS, CHUNK), lambda c: (nc - 1 - c, 0, 0))],
        out_shape=[jax.ShapeDtypeStruct((L, 3 * W), f32), jax.ShapeDtypeStruct((L, LANE), f32),
                   jax.ShapeDtypeStruct((nc, HEADS, CHUNK), f32)],
        scratch_shapes=[pltpu.VMEM((HEADS, HEAD_DIM, HEAD_DIM), f32)],
        compiler_params=_cparams("arbitrary"),
    )(qkv, sc, gr, u_all, w_all, vn_all, t_all, sp_all, do_all)


def _qkv_bwd(proj, cw, dn):
    L = proj.shape[0]

    def body(x_ref, cw_ref, dn_ref, dx_ref, gcw_ref):
        j = pl.program_id(0)
        x, dn_v = x_ref[...], dn_ref[...]
        c = _conv4(x, cw_ref)
        sg = _sigmoid(c)
        a = c * sg
        rn = lax.rsqrt(jnp.sum(a * a, axis=1, keepdims=True) + EPS)
        scale = jnp.where(j < HEADS, HEAD_DIM ** -0.5, 1.0).astype(f32)
        da_n = (scale * rn) * (dn_v - a * ((rn * rn) * jnp.sum(dn_v * a, axis=1, keepdims=True)))
        da = jnp.where(j < 2 * HEADS, da_n, dn_v)
        dc = da * (sg * (1.0 + c * (1.0 - sg)))
        gcw_ref[3:4, :] = jnp.sum(dc * x, axis=0, keepdims=True)
        gcw_ref[2:3, :] = jnp.sum(dc * _shift_down(x, 1), axis=0, keepdims=True)
        gcw_ref[1:2, :] = jnp.sum(dc * _shift_down(x, 2), axis=0, keepdims=True)
        gcw_ref[0:1, :] = jnp.sum(dc * _shift_down(x, 3), axis=0, keepdims=True)
        dx = (cw_ref[3:4, :] * dc + cw_ref[2:3, :] * _shift_up(dc, 1) + cw_ref[1:2, :] * _shift_up(dc, 2)
              + cw_ref[0:1, :] * _shift_up(dc, 3))
        dx_ref[...] = dx.astype(bf16)

    col = pl.BlockSpec((L, LANE), lambda j: (0, j))
    wspec = pl.BlockSpec((4, LANE), lambda j: (0, j))
    return _pcall(
        body, name="qkv_bwd", grid=(3 * HEADS,),
        in_specs=[col, wspec, col], out_specs=[col, wspec],
        out_shape=[jax.ShapeDtypeStruct((L, 3 * GDN_WIDTH), bf16), jax.ShapeDtypeStruct((4, 3 * GDN_WIDTH), f32)],
        compiler_params=_cparams("parallel"),
    )(proj, cw, dn)


def _scalars_bwd(proj, alog_p, dtb_p, dsc, dgr_col):
    L = proj.shape[0]

    def body(x_ref, al_ref, dt_ref, dsc_ref, dgr_ref, dba_ref, gs_ref):
        x, dsc_v = x_ref[...], dsc_ref[...]
        lane = _lanes(x.shape)
        dec = (lane >= HEADS) & (lane < 2 * HEADS)
        dg = jnp.where(dec, dsc_v - dgr_ref[...], 0.0)
        rc = _rows(x.shape) & (CHUNK - 1)
        for s in (1, 2, 4, 8, 16, 32):
            dg = dg + jnp.where(rc + s < CHUNK, pltpu.roll(dg, L - s, 0), 0.0)
        xa = x + dt_ref[...]
        ea = jnp.exp(al_ref[...])
        g = -ea * _softplus(xa)
        da = dg * (-ea) * _sigmoid(xa)
        beta = _sigmoid(x)
        db = dsc_v * beta * (1.0 - beta)
        dba_ref[...] = jnp.where(lane < HEADS, db, jnp.where(dec, da, 0.0)).astype(bf16)
        gs_ref[...] = jnp.zeros_like(gs_ref)
        gs_ref[0:1, :] = jnp.sum(jnp.where(dec, dg * g, 0.0), axis=0, keepdims=True)
        gs_ref[1:2, :] = jnp.sum(jnp.where(dec, da, 0.0), axis=0, keepdims=True)

    full = pl.BlockSpec((L, LANE), lambda i: (0, 0))
    vec = pl.BlockSpec((1, LANE), lambda i: (0, 0))
    return _pcall(
        body, name="scalars_bwd", grid=(1,),
        in_specs=[pl.BlockSpec((L, LANE), lambda i: (0, OFF_BA // LANE)), vec, vec, full, full],
        out_specs=[full, pl.BlockSpec((8, LANE), lambda i: (0, 0))],
        out_shape=[jax.ShapeDtypeStruct((L, LANE), bf16), jax.ShapeDtypeStruct((8, LANE), f32)],
        compiler_params=_cparams("arbitrary"),
    )(proj, alog_p, dtb_p, dsc, dgr_col)


def _input_grad(pieces, offs, wpad, x, nw, dy):
    L = x.shape[0]
    tm = min(256, L)
    npc = len(pieces)

    def body(*refs):
        p_refs = refs[:npc]
        w_hbm, x_ref, nw_ref, dy_ref, gx_ref, gnw_ref, w_vmem, sem = refs[npc:]

        @pl.when(pl.program_id(0) == 0)
        def _():
            cp = pltpu.make_async_copy(w_hbm, w_vmem, sem)
            cp.start()
            cp.wait()
            gnw_ref[...] = jnp.zeros_like(gnw_ref)
        dh = None
        for p_ref, off in zip(p_refs, offs):
            wd = p_ref.shape[1]
            part = lax.dot_general(p_ref[...], w_vmem[:, off:off + wd], (((1,), (1,)), ((), ())), preferred_element_type=f32)
            dh = part if dh is None else dh + part
        xv, nwv = x_ref[...], nw_ref[...]
        r = lax.rsqrt(jnp.mean(xv * xv, axis=-1, keepdims=True) + EPS)
        xh = xv * r
        gnw_ref[...] += jnp.sum(dh * xh, axis=0, keepdims=True)
        dxh = dh * nwv
        gx_ref[...] = dy_ref[...] + r * (dxh - xh * jnp.mean(dxh * xh, axis=-1, keepdims=True))

    row = lambda i: (i, 0)
    fix = lambda i: (0, 0)
    return _pcall(
        body, name="input_grad", grid=(L // tm,),
        in_specs=[pl.BlockSpec((tm, p.shape[1]), row) for p in pieces] + [
            ANY, pl.BlockSpec((tm, D_MODEL), row), pl.BlockSpec((1, D_MODEL), fix), pl.BlockSpec((tm, D_MODEL), row)],
        out_specs=[pl.BlockSpec((tm, D_MODEL), row), pl.BlockSpec((1, D_MODEL), fix)],
        out_shape=[jax.ShapeDtypeStruct((L, D_MODEL), f32), jax.ShapeDtypeStruct((1, D_MODEL), f32)],
        scratch_shapes=[pltpu.VMEM(wpad.shape, bf16), pltpu.SemaphoreType.DMA(())],
        compiler_params=_cparams("arbitrary"),
    )(*pieces, wpad, x, nw, dy)


def _adamw_reduce(parts, w, m, v, name):
    R, C = w.shape
    n_parts = parts.shape[0]
    tr = 128 if R % 128 == 0 else R
    c1 = 1.0 - ADAM_B1 ** ADAM_STEP
    c2 = 1.0 - ADAM_B2 ** ADAM_STEP

    def body(p_ref, w_ref, m_ref, v_ref, g_ref, d_ref, nm_ref, nv_ref):
        g = p_ref[0].astype(f32)
        for s in range(1, n_parts):
            g = g + p_ref[s].astype(f32)
        nm = ADAM_B1 * m_ref[...] + (1.0 - ADAM_B1) * g
        nv = ADAM_B2 * v_ref[...] + (1.0 - ADAM_B2) * (g * g)
        g_ref[...] = g
        nm_ref[...] = nm
        nv_ref[...] = nv
        d_ref[...] = -ADAM_LR * ((nm / c1) / (jnp.sqrt(nv / c2) + ADAM_EPS) + ADAM_WD * w_ref[...])

    blk = pl.BlockSpec((tr, C), lambda i: (i, 0))
    out = jax.ShapeDtypeStruct((R, C), f32)
    return _pcall(
        body, name=name, grid=(R // tr,),
        in_specs=[pl.BlockSpec((n_parts, tr, C), lambda i: (0, i, 0)), blk, blk, blk],
        out_specs=[blk] * 4, out_shape=[out] * 4,
        compiler_params=_cparams("parallel"),
    )(parts, w, m, v)


def _pad_lanes(vec8, start):
    return jnp.pad(vec8.reshape(1, -1), ((0, 0), (start, LANE - start - vec8.size)))


def kernel(x, norm_in_w, w_in, conv_qkv_w, A_log, dt_bias, gdn_norm_w, conv_w, conv_b, w_out, final_norm_w, loss_target, m_norm_in_w, m_w_in, m_conv_qkv_w, m_A_log, m_dt_bias, m_gdn_norm_w, m_conv_w, m_conv_b, m_w_out, m_final_norm_w, v_norm_in_w, v_w_in, v_conv_qkv_w, v_A_log, v_dt_bias, v_gdn_norm_w, v_conv_w, v_conv_b, v_w_out, v_final_norm_w):
    L = x.shape[1]
    nc = L // CHUNK
    xs = x[0]
    tgt = loss_target[0]
    fnw = final_norm_w.reshape(1, D_MODEL)

    win_g, wout_g, cqkv_g, cw_g = _all_gather(
        [w_in[0].astype(bf16), w_out[0].astype(bf16), conv_qkv_w[0], conv_w[0]], "gather_weights")
    wpad = _relayout_w_in(win_g)
    wo =wout_g.reshape(N_DEV * wout_g.shape[1], D_MODEL)
    cqkv = jnp.concatenate([cqkv_g[d] for d in range(N_DEV)], axis=1)
    cw = jnp.concatenate([cw_g[d] for d in range(N_DEV)], axis=1)
    alog_p = _pad_lanes(A_log, HEADS)
    dtb_p = _pad_lanes(dt_bias, HEADS)

    proj, h = _in_proj(xs, norm_in_w, wpad)
    qkv = _qkv_act(proj, cqkv)
    sc, gr = _scalars(proj, alog_p, dtb_p)
    o, u_all, w_all, vn_all, t_all, sp_all = _gdn_fwd(qkv, sc, gr)
    mix_a = _gdn_gate(o, proj, gdn_norm_w)
    mix_b = _conv_fwd(proj, cw, conv_b)
    dy, dyb, dmix_a, dmix_b, g_fnw, loss_v = _out_proj_loss(xs, mix_a, mix_b, wo, fnw, tgt)

    g_wout = jnp.concatenate([_tn_matmul(mix_a, dyb, "grad_w_out_a"), _tn_matmul(mix_b, dyb, "grad_w_out_b")], axis=0)
    do, dzg, g_gnw = _gdn_gate_bwd(o, proj, gdn_norm_w, dmix_a)
    d_b, d_c, d_hc, d_zc, g_cw, g_cb = _conv_bwd(proj, cw, conv_b, dmix_b)
    dqkv_n, dsc, dgr = _gdn_bwd(qkv, sc, gr, u_all, w_all, vn_all, t_all, sp_all, do)
    dqkv, g_cqkv = _qkv_bwd(proj, cqkv, dqkv_n)
    dgr_col = jnp.pad(dgr.transpose(0, 2, 1).reshape(L, HEADS), ((0, 0), (HEADS, LANE - 2 * HEADS)))
    dba, g_sc = _scalars_bwd(proj, alog_p, dtb_p, dsc, dgr_col)
    pieces = [dqkv, dzg, dba, d_b, d_c, d_hc, d_zc]
    offs = [OFF_QKV, OFF_ZG, OFF_BA, OFF_B, OFF_C, OFF_HC, OFF_ZC]
    grad_x, g_nw = _input_grad(pieces, offs, wpad, xs, norm_in_w, dy)
    g_parts = [_tn_matmul(h, p, "grad_w_in_%d" % i) for i, p in enumerate(pieces)]
    g_win_blk = _grad_blocks(g_parts)

    big = [g_win_blk, g_wout.reshape(N_DEV, -1, D_MODEL)]
    p_win, p_wout = _pair_exchange(big, "exchange_grads_pair")
    r_win, r_wout = _quad_exchange(
        [_pair_sum(big[0], p_win, "pair_sum_w_in"), _pair_sum(big[1], p_wout, "pair_sum_w_out")], "exchange_grads_chips")
    r_cqkv, r_cw = _all_to_all(
        [g_cqkv.reshape(4, N_DEV, -1).transpose(1, 0, 2), g_cw.reshape(3, N_DEV, -1).transpose(1, 0, 2)],
        "exchange_small_sharded_grads")
    upd_win = _adamw_reduce(r_win, w_in[0], m_w_in[0], v_w_in[0], "adamw_w_in")
    upd_wout = _adamw_reduce(r_wout, w_out[0], m_w_out[0], v_w_out[0], "adamw_w_out")
    upd_cqkv = _adamw_reduce(r_cqkv, conv_qkv_w[0], m_conv_qkv_w[0], v_conv_qkv_w[0], "adamw_conv_qkv_w")
    upd_cw = _adamw_reduce(r_cw, conv_w[0], m_conv_w[0], v_conv_w[0], "adamw_conv_w")

    def pack(nw_, cb_, fw_, gn_, al_, dt_, ls_):
        return jnp.concatenate([nw_.reshape(1, -1), cb_.reshape(1, -1), fw_.reshape(1, -1), gn_.reshape(1, -1),
                                _pad_lanes(al_, 0), _pad_lanes(dt_, 0), ls_.reshape(1, -1)], axis=1)

    zl = jnp.zeros((1, LANE), f32)
    small_g = pack(g_nw, g_cb, g_fnw, g_gnw, g_sc[0, HEADS:2 * HEADS], g_sc[1, HEADS:2 * HEADS], loss_v)
    (small_all,) = _all_gather([small_g], "gather_small_grads")
    upd_small = _adamw_reduce(
        small_all, pack(norm_in_w, conv_b, final_norm_w, gdn_norm_w, A_log, dt_bias, zl),
        pack(m_norm_in_w, m_conv_b, m_final_norm_w, m_gdn_norm_w, m_A_log, m_dt_bias, zl),
        pack(v_norm_in_w, v_conv_b, v_final_norm_w, v_gdn_norm_w, v_A_log, v_dt_bias, zl), "adamw_small")

    o_nw, o_cb, o_fw, o_gn = 0, D_MODEL, 2 * D_MODEL, 3 * D_MODEL
    o_al, o_dt, o_ls = o_gn + LANE, o_gn + 2 * LANE, o_gn + 3 * LANE

    def unpack(k):
        s = upd_small[k]
        return dict(
            norm_in_w=s[:, o_nw:o_nw + D_MODEL], conv_b=s[:, o_cb:o_cb + D_MODEL], final_norm_w=s[0, o_fw:o_fw + D_MODEL],
            gdn_norm_w=s[:, o_gn:o_gn + LANE], A_log=s[:, o_al:o_al + HEADS], dt_bias=s[:, o_dt:o_dt + HEADS])

    loss = upd_small[0][0, o_ls]
    outs = [loss, grad_x[None]]
    for k in range(4):
        sm = unpack(k)
        outs += [sm["norm_in_w"], upd_win[k][None], upd_cqkv[k][None], sm["A_log"], sm["dt_bias"], sm["gdn_norm_w"],
                 upd_cw[k][None], sm["conv_b"], upd_wout[k][None], sm["final_norm_w"]]
    return tuple(outs)
```

```python
import functools
import math

import jax
import jax.numpy as jnp
from jax import lax
from jax.experimental import pallas as pl
from jax.experimental.pallas import tpu as pltpu

f32 = jnp.float32
bf16 = jnp.bfloat16

N_DEV = 8
D_MODEL = 1024
HEADS = 8
HEAD_DIM = 128
CHUNK = 64
GDN_WIDTH = HEADS * HEAD_DIM
CONV_WIDTH = 1024
PROJ_WIDTH = 8208
SHARD_W = PROJ_WIDTH // N_DEV
EPS = 1e-6

NAT_SMALL_END = 4112
PAD_COLS = 112
OFF_QKV, OFF_ZG, OFF_BA, OFF_B, OFF_C, OFF_HC, OFF_ZC = 0, 3072, 4096, 4224, 5248, 6272, 7296
PROJ_PAD = 8320
LANE = 128

ADAM_LR, ADAM_B1, ADAM_B2, ADAM_EPS, ADAM_WD, ADAM_STEP = 0.001, 0.9, 0.999, 1e-08, 0.01, 10

VMEM_LIMIT = 56 * 1024 * 1024

MESH = pl.DeviceIdType.MESH
ANY = pl.BlockSpec(memory_space=pl.ANY)


def _pcall(body, **kw):
    return pl.pallas_call(body, **kw)


def _cparams(*sem):
    return pltpu.CompilerParams(dimension_semantics=sem if sem else None, vmem_limit_bytes=VMEM_LIMIT)


def _mm(a, b):
    return jnp.dot(a.astype(bf16), b.astype(bf16), preferred_element_type=f32)


def _mm_nt(a, b):
    return lax.dot_general(a.astype(bf16), b.astype(bf16), (((1,), (1,)), ((), ())), preferred_element_type=f32)


def _mm_tn(a, b):
    return lax.dot_general(a.astype(bf16), b.astype(bf16), (((0,), (0,)), ((), ())), preferred_element_type=f32)


def _rows(shape):
    return lax.broadcasted_iota(jnp.int32, shape, 0)


def _lanes(shape):
    return lax.broadcasted_iota(jnp.int32, shape, 1)


def _shift_down(x, s):
    if s == 0:
        return x
    return jnp.where(_rows(x.shape) >= s, pltpu.roll(x, s, 0), 0.0)


def _shift_up(x, s):
    if s == 0:
        return x
    n = x.shape[0]
    return jnp.where(_rows(x.shape) < n - s, pltpu.roll(x, n - s, 0), 0.0)


def _sigmoid(x):
    return jax.nn.sigmoid(x)


def _softplus(x):
    e = jnp.exp(-jnp.abs(x))
    small = e * (1.0 - e * (0.5 - e * (1.0 / 3.0)))
    return jnp.maximum(x, 0.0) + jnp.where(e < 0.01, small, jnp.log(1.0 + e))


def _mesh_pos():
    return lax.axis_index("x"), lax.axis_index("y"), lax.axis_index("c")


def _flat(px, py, pc):
    return 4 * px + 2 * py + pc


def _all_gather(xs, name):
    n = len(xs)

    def body(*refs):
        x_refs, o_refs = refs[:n], refs[n:2 * n]
        send_sems, recv_sems, local_sems = refs[2 * n:]
        x, y, c = _mesh_pos()
        me, sibling = (x, y, c), (x, y, 1 - c)
        chips = [(1 - x, y), (x, 1 - y), (1 - x, 1 - y)]

        def copy(a, k, block, to, src=None):
            dst = o_refs[a].at[_flat(*block)]
            return pltpu.make_async_remote_copy(
                src_ref=dst if src is None else src, dst_ref=dst,
                send_sem=send_sems.at[a, k], recv_sem=recv_sems.at[a, k], device_id=to, device_id_type=MESH)

        mine, first, passed = [], [], []
        for a in range(n):
            cp = pltpu.make_async_copy(x_refs[a], o_refs[a].at[_flat(*me)], local_sems.at[a])
            cp.start()
            mine.append(cp)
            fa = [copy(a, 0, me, sibling, src=x_refs[a])]
            fa += [copy(a, 1 + j, me, (*chip, c), src=x_refs[a]) for j, chip in enumerate(chips)]
            for cp in fa:
                cp.start()
            first += fa
        for a in range(n):
            for j, chip in enumerate(chips):
                copy(a, 1 + j, (*chip, c), me).wait_recv()
                cp = copy(a, 4 + j, (*chip, c), sibling)
                cp.start()
                passed.append(cp)
        for a in range(n):
            copy(a, 0, sibling, me).wait_recv()
            for j, chip in enumerate(chips):
                copy(a, 4 + j, (*chip, 1 - c), me).wait_recv()
        for cp in first + passed:
            cp.wait_send()
        for cp in mine:
            cp.wait()

    outs = _pcall(
        body, name=name,
        out_shape=[jax.ShapeDtypeStruct((N_DEV,) + a.shape, a.dtype) for a in xs],
        in_specs=[ANY] * n, out_specs=[ANY] * n,
        scratch_shapes=[pltpu.SemaphoreType.DMA((n, 7)), pltpu.SemaphoreType.DMA((n, 7)), pltpu.SemaphoreType.DMA((n,))],
    )(*xs)
    return list(outs)


def _all_to_all(gs, name):
    n = len(gs)

    def body(*refs):
        g_refs, o_refs = refs[:n], refs[n:2 * n]
        send_sems, recv_sems, local_sems = refs[2 * n:]
        x, y, c = _mesh_pos()
        me = _flat(x, y, c)
        peers = []
        for k in range(1, N_DEV):
            kx, ky, kc = (k >> 2) & 1, (k >> 1) & 1, k & 1
            px = (1 - x) if kx else x
            py = (1 - y) if ky else y
            pc = (1 - c) if kc else c
            peers.append((px, py, pc))

        def copy(a, k):
            peer = peers[k - 1]
            return pltpu.make_async_remote_copy(
                src_ref=g_refs[a].at[_flat(*peer)], dst_ref=o_refs[a].at[me],
                send_sem=send_sems.at[a, k - 1], recv_sem=recv_sems.at[a, k - 1], device_id=peer, device_id_type=MESH)

        def arrival(a, k):
            peer = peers[k - 1]
            return pltpu.make_async_remote_copy(
                src_ref=g_refs[a].at[me], dst_ref=o_refs[a].at[_flat(*peer)],
                send_sem=send_sems.at[a, k - 1], recv_sem=recv_sems.at[a, k - 1], device_id=peer, device_id_type=MESH)

        mine, sent = [], []
        for a in range(n):
            cp = pltpu.make_async_copy(g_refs[a].at[me], o_refs[a].at[me], local_sems.at[a])
            cp.start()
            mine.append(cp)
            for k in range(1, N_DEV):
                cp = copy(a, k)
                cp.start()
                sent.append(cp)
        for a in range(n):
            for k in range(1, N_DEV):
                arrival(a, k).wait_recv()
        for cp in sent:
            cp.wait_send()
        for cp in mine:
            cp.wait()

    outs = _pcall(
        body, name=name,
        out_shape=[jax.ShapeDtypeStruct(a.shape, a.dtype) for a in gs],
        in_specs=[ANY] * n, out_specs=[ANY] * n,
        scratch_shapes=[pltpu.SemaphoreType.DMA((n, 7)), pltpu.SemaphoreType.DMA((n, 7)), pltpu.SemaphoreType.DMA((n,))],
    )(*gs)
    return list(outs)


def _pair_exchange(gs, name):
    n = len(gs)
    chips = [(0, 0), (0, 1), (1, 0), (1, 1)]

    def body(*refs):
        g_refs, o_refs = refs[:n], refs[n:2 * n]
        send_sems, recv_sems = refs[2 * n:]
        x, y, c = _mesh_pos()
        sibling = (x, y, 1 - c)

        def copy(a, i):
            xp, yp = chips[i]
            return pltpu.make_async_remote_copy(
                src_ref=g_refs[a].at[_flat(xp, yp, 1 - c)], dst_ref=o_refs[a].at[i],
                send_sem=send_sems.at[a, i], recv_sem=recv_sems.at[a, i], device_id=sibling, device_id_type=MESH)

        cps = [copy(a, i) for a in range(n) for i in range(4)]
        for cp in cps:
            cp.start()
        for cp in cps:
            cp.wait()

    outs = _pcall(
        body, name=name,
        out_shape=[jax.ShapeDtypeStruct((4,) + a.shape[1:], a.dtype) for a in gs],
        in_specs=[ANY] * n, out_specs=[ANY] * n,
        scratch_shapes=[pltpu.SemaphoreType.DMA((n, 4)), pltpu.SemaphoreType.DMA((n, 4))],
    )(*gs)
    return list(outs)


def _pair_sum(g, p1, name):
    _, R, C = g.shape
    tr = 256 if R % 256 == 0 else R
    cidx = lax.axis_index("c").astype(jnp.int32).reshape(1)

    def body(c_ref, g_ref, p_ref, o_ref):
        o_ref[...] = (g_ref[...].astype(f32) + p_ref[...].astype(f32)).astype(o_ref.dtype)

    return _pcall(
        body, name=name,
        grid_spec=pltpu.PrefetchScalarGridSpec(
            num_scalar_prefetch=1, grid=(4, R // tr),
            in_specs=[pl.BlockSpec((1, tr, C), lambda i, r, c_ref: (2 * i + c_ref[0], r, 0)),
                      pl.BlockSpec((1, tr, C), lambda i, r, c_ref: (i, r, 0))],
            out_specs=pl.BlockSpec((1, tr, C), lambda i, r, c_ref: (i, r, 0))),
        out_shape=jax.ShapeDtypeStruct((4, R, C), g.dtype),
        compiler_params=_cparams("parallel", "parallel"),
    )(cidx, g, p1)


def _quad_exchange(ss, name):
    n = len(ss)

    def body(*refs):
        s_refs, o_refs = refs[:n], refs[n:2 * n]
        send_sems, recv_sems, local_sems = refs[2 * n:]
        x, y, c = _mesh_pos()
        me = 2 * x + y
        peers = [(1 - x, y), (x, 1 - y), (1 - x, 1 - y)]

        def copy(a, j, arriving):
            px, py = peers[j]
            src_slot, dst_slot = (me, 2 * px + py) if arriving else (2 * px + py, me)
            return pltpu.make_async_remote_copy(
                src_ref=s_refs[a].at[src_slot], dst_ref=o_refs[a].at[dst_slot],
                send_sem=send_sems.at[a, j], recv_sem=recv_sems.at[a, j], device_id=(px, py, c), device_id_type=MESH)

        mine, sent = [], []
        for a in range(n):
            cp = pltpu.make_async_copy(s_refs[a].at[me], o_refs[a].at[me], local_sems.at[a])
            cp.start()
            mine.append(cp)
            for j in range(3):
                cp = copy(a, j, False)
                cp.start()
                sent.append(cp)
        for a in range(n):
            for j in range(3):
                copy(a, j, True).wait_recv()
        for cp in sent:
            cp.wait_send()
        for cp in mine:
            cp.wait()

    outs = _pcall(
        body, name=name,
        out_shape=[jax.ShapeDtypeStruct(a.shape, a.dtype) for a in ss],
        in_specs=[ANY] * n, out_specs=[ANY] * n,
        scratch_shapes=[pltpu.SemaphoreType.DMA((n, 3)), pltpu.SemaphoreType.DMA((n, 3)), pltpu.SemaphoreType.DMA((n,))],
    )(*ss)
    return list(outs)


PIECE_NAT = (0, 3072, 4096, 4112, 5136, 6160, 7184, PROJ_WIDTH)


COL_TILE = 256


def _cast_w_in(w3):
    n = w3.shape[0]

    def body(w_ref, o_ref):
        o_ref[...] = w_ref[:, 0, :].astype(bf16)

    return _pcall(
        body, name="cast_w_in", grid=(D_MODEL // COL_TILE,),
        in_specs=[pl.BlockSpec((n, 1, COL_TILE), lambda j: (0, 0, j))],
        out_specs=pl.BlockSpec((n, COL_TILE), lambda j: (0, j)),
        out_shape=jax.ShapeDtypeStruct((n, D_MODEL), bf16),
        compiler_params=_cparams("parallel"),
    )(w3)


def _relayout_w_in(win_g):
    def body(g_ref, o_ref):
        o_ref[NAT_SMALL_END:NAT_SMALL_END + PAD_COLS, :] = jnp.zeros((PAD_COLS, COL_TILE), o_ref.dtype)
        for d in range(N_DEV):
            n0, n1 = d * SHARD_W, (d + 1) * SHARD_W
            cut = min(max(NAT_SMALL_END - n0, 0), SHARD_W)
            if cut > 0:
                o_ref[n0:n0 + cut, :] = g_ref[d, 0:cut, :]
            if cut < SHARD_W:
                o_ref[n0 + cut + PAD_COLS:n1 + PAD_COLS, :] = g_ref[d, cut:SHARD_W, :]

    return _pcall(
        body, name="relayout_w_in", grid=(D_MODEL // COL_TILE,),
        in_specs=[pl.BlockSpec((N_DEV, SHARD_W, COL_TILE), lambda j: (0, 0, j))],
        out_specs=pl.BlockSpec((PROJ_PAD, COL_TILE), lambda j: (0, j)),
        out_shape=jax.ShapeDtypeStruct((PROJ_PAD, D_MODEL), win_g.dtype),
        compiler_params=_cparams("parallel"),
    )(win_g)


def _grad_blocks(g_parts):
    npc = len(g_parts)

    def body(*refs):
        p_refs, o_ref = refs[:npc], refs[npc]
        for d in range(N_DEV):
            n0, n1 = d * SHARD_W, (d + 1) * SHARD_W
            for i in range(npc):
                lo, hi = max(n0, PIECE_NAT[i]), min(n1, PIECE_NAT[i + 1])
                if lo < hi:
                    o_ref[d, lo - n0:hi - n0, :] = p_refs[i][lo - PIECE_NAT[i]:hi - PIECE_NAT[i], :]

    return _pcall(
        body, name="grad_blocks", grid=(D_MODEL // COL_TILE,),
        in_specs=[pl.BlockSpec((p.shape[0], COL_TILE), lambda j: (0, j)) for p in g_parts],
        out_specs=pl.BlockSpec((N_DEV, SHARD_W, COL_TILE), lambda j: (0, 0, j)),
        out_shape=jax.ShapeDtypeStruct((N_DEV, SHARD_W, D_MODEL), bf16),
        compiler_params=_cparams("parallel"),
    )(*g_parts)


def _in_proj(x, nw, wpad_t):
    L = x.shape[0]
    tn = 640
    nj = wpad_t.shape[0] // tn

    def body(x_ref, nw_ref, w_ref, proj_ref, h_ref):
        @pl.when(pl.program_id(0) == 0)
        def _():
            for r in range(0, L, 256):
                xs = x_ref[r:r + 256, :]
                ms = jnp.mean(xs * xs, axis=-1, keepdims=True)
                h_ref[r:r + 256, :] = ((xs * lax.rsqrt(ms + EPS)) * nw_ref[...]).astype(bf16)
        for r in range(0, L, 512):
            proj_ref[r:r + 512, :] = lax.dot_general(h_ref[r:r + 512, :], w_ref[...], (((1,), (1,)), ((), ())),
                                                     preferred_element_type=f32)

    return _pcall(
        body, name="in_proj", grid=(nj,),
        in_specs=[pl.BlockSpec((L, D_MODEL), lambda j: (0, 0)), pl.BlockSpec((1, D_MODEL), lambda j: (0, 0)),
                  pl.BlockSpec((tn, D_MODEL), lambda j: (j, 0))],
        out_specs=[pl.BlockSpec((L, tn), lambda j: (0, j)), pl.BlockSpec((L, D_MODEL), lambda j: (0, 0))],
        out_shape=[jax.ShapeDtypeStruct((L, wpad_t.shape[0]), f32), jax.ShapeDtypeStruct((L, D_MODEL), bf16)],
        compiler_params=_cparams("arbitrary"),
    )(x, nw, wpad_t)


def _conv4(x, cw_ref):
    return (cw_ref[3:4, :] * x + cw_ref[2:3, :] * _shift_down(x, 1) + cw_ref[1:2, :] * _shift_down(x, 2)
            + cw_ref[0:1, :] * _shift_down(x, 3))


def _qkv_act(proj, cw):
    L = proj.shape[0]

    def body(x_ref, cw_ref, o_ref):
        j = pl.program_id(0)
        c = _conv4(x_ref[...], cw_ref)
        a = c * _sigmoid(c)
        rn = lax.rsqrt(jnp.sum(a * a, axis=1, keepdims=True) + EPS)
        scale = jnp.where(j < HEADS, HEAD_DIM ** -0.5, 1.0).astype(f32)
        o_ref[...] = jnp.where(j < 2 * HEADS, (a * rn) * scale, a)

    return _pcall(
        body, name="qkv_act", grid=(3 * HEADS,),
        in_specs=[pl.BlockSpec((L, LANE), lambda j: (0, j)), pl.BlockSpec((4, LANE), lambda j: (0, j))],
        out_specs=pl.BlockSpec((L, LANE), lambda j: (0, j)),
        out_shape=jax.ShapeDtypeStruct((L, 3 * GDN_WIDTH), f32),
        compiler_params=_cparams("parallel"),
    )(proj, cw)


def _scalars(proj, alog_p, dtb_p):
    L = proj.shape[0]
    nc = L // CHUNK

    def body(x_ref, al_ref, dt_ref, sc_ref, gr_ref):
        x = x_ref[...]
        lane = _lanes(x.shape)
        beta = _sigmoid(x)
        g = -jnp.exp(al_ref[...]) * _softplus(x + dt_ref[...])
        gc = jnp.where((lane >= HEADS) & (lane < 2 * HEADS), g, 0.0)
        rc = _rows(x.shape) & (CHUNK - 1)
        for s in (1, 2, 4, 8, 16, 32):
            gc = gc + jnp.where(rc >= s, pltpu.roll(gc, s, 0), 0.0)
        sc_ref[...] = jnp.where(lane < HEADS, beta, gc)
        sel = (_lanes((HEADS, LANE)) == _rows((HEADS, LANE)) + HEADS).astype(f32)
        for c in range(nc):
            gr_ref[c] = lax.dot_general(sel, sc_ref[c * CHUNK:(c + 1) * CHUNK, :], (((1,), (1,)), ((), ())),
                                        preferred_element_type=f32, precision=lax.Precision.HIGHEST)

    return _pcall(
        body, name="scalars", grid=(1,),
        in_specs=[pl.BlockSpec((L, LANE), lambda i: (0, OFF_BA // LANE)), pl.BlockSpec((1, LANE), lambda i: (0, 0)),
                  pl.BlockSpec((1, LANE), lambda i: (0, 0))],
        out_specs=[pl.BlockSpec((L, LANE), lambda i: (0, 0)), pl.BlockSpec((nc, HEADS, CHUNK), lambda i: (0, 0, 0))],
        out_shape=[jax.ShapeDtypeStruct((L, LANE), f32), jax.ShapeDtypeStruct((nc, HEADS, CHUNK), f32)],
        compiler_params=_cparams("arbitrary"),
    )(proj, alog_p, dtb_p)


def _head_scalars(sc, gr_ref, h):
    lane = _lanes(sc.shape)
    beta = jnp.sum(jnp.where(lane == h, sc, 0.0), axis=1, keepdims=True)
    gcc = jnp.sum(jnp.where(lane == HEADS + h, sc, 0.0), axis=1, keepdims=True)
    gcr = gr_ref[0, h:h + 1, :]
    gl = jnp.sum(jnp.where(_lanes(gcr.shape) == CHUNK - 1, gcr, 0.0), axis=1, keepdims=True)
    ii, jj = _rows((CHUNK, CHUNK)), _lanes((CHUNK, CHUNK))
    dmat = jnp.where(ii >= jj, jnp.exp(jnp.minimum(gcc - gcr, 0.0)), 0.0)
    dmat_t = jnp.where(jj >= ii, jnp.exp(jnp.minimum(gcr - gcc, 0.0)), 0.0)
    return beta, gcc, gl, dmat, dmat_t, ii, jj


def _gdn_fwd(qkv, sc, gr):
    L = qkv.shape[0]
    nc = L // CHUNK
    W = GDN_WIDTH

    def body(qkv_ref, sc_ref, gr_ref, o_ref, u_ref, w_ref, vn_ref, t_ref, sp_ref, s_scr):
        @pl.when(pl.program_id(0) == 0)
        def _():
            s_scr[...] = jnp.zeros_like(s_scr)
        sc_v = sc_ref[...]
        HS = range(HEADS)
        cs = [slice(h * HEAD_DIM, (h + 1) * HEAD_DIM) for h in HS]
        q = [qkv_ref[:, h * HEAD_DIM:(h + 1) * HEAD_DIM] for h in HS]
        k = [qkv_ref[:, W + h * HEAD_DIM:W + (h + 1) * HEAD_DIM] for h in HS]
        v = [qkv_ref[:, 2 * W + h * HEAD_DIM:2 * W + (h + 1) * HEAD_DIM] for h in HS]
        hsc = [_head_scalars(sc_v, gr_ref, h) for h in HS]
        beta, gcc, gl, dmat = ([x[i] for x in hsc] for i in range(4))
        ii, jj = hsc[0][5], hsc[0][6]
        eg = [jnp.exp(gcc[h]) for h in HS]
        kb = [k[h] * beta[h] for h in HS]
        kk = [_mm_nt(kb[h], k[h]) for h in HS]
        qk = [_mm_nt(q[h], k[h]) for h in HS]
        xp = [-jnp.where(ii > jj, kk[h] * dmat[h], 0.0) for h in HS]
        t = xp
        for _ in range(5):
            xp = [_mm(xp[h], xp[h]) for h in HS]
            tx = [_mm(t[h], xp[h]) for h in HS]
            t = [t[h] + xp[h] + tx[h] for h in HS]
        vb = [v[h] * beta[h] for h in HS]
        kbg = [kb[h] * eg[h] for h in HS]
        uw = [_mm(t[h], jnp.concatenate([vb[h], kbg[h]], axis=1)) for h in HS]
        u = [vb[h] + uw[h][:, :HEAD_DIM] for h in HS]
        w = [kbg[h] + uw[h][:, HEAD_DIM:] for h in HS]
        s = [s_scr[h] for h in HS]
        ws = [_mm(jnp.concatenate([w[h], q[h] * eg[h]], axis=0), s[h]) for h in HS]
        vn = [u[h] - ws[h][:CHUNK] for h in HS]
        p = [jnp.where(ii >= jj, qk[h] * dmat[h], 0.0) for h in HS]
        pv = [_mm(p[h], vn[h]) for h in HS]
        kv = [_mm_tn(k[h] * jnp.exp(gl[h] - gcc[h]), vn[h]) for h in HS]
        for h in HS:
            sp_ref[0, cs[h], :] = s[h]
            o_ref[:, cs[h]] = ws[h][CHUNK:] + pv[h]
            s_scr[h] = jnp.exp(gl[h]) * s[h] + kv[h]
            u_ref[:, cs[h]] = u[h]
            w_ref[:, cs[h]] = w[h]
            vn_ref[:, cs[h]] = vn[h]
            t_ref[0, h] = t[h]

    row = lambda c: (c, 0)
    act = jax.ShapeDtypeStruct((L, W), f32)
    return _pcall(
        body, name="gdn_fwd", grid=(nc,),
        in_specs=[pl.BlockSpec((CHUNK, 3 * W), row), pl.BlockSpec((CHUNK, LANE), row),
                  pl.BlockSpec((1, HEADS, CHUNK), lambda c: (c, 0, 0))],
        out_specs=[pl.BlockSpec((CHUNK, W), row)] * 4 + [
            pl.BlockSpec((1, HEADS, CHUNK, CHUNK), lambda c: (c, 0, 0, 0)),
            pl.BlockSpec((1, W, HEAD_DIM), lambda c: (c, 0, 0))],
        out_shape=[act, act, act, act, jax.ShapeDtypeStruct((nc, HEADS, CHUNK, CHUNK), f32),
                   jax.ShapeDtypeStruct((nc, W, HEAD_DIM), f32)],
        scratch_shapes=[pltpu.VMEM((HEADS, HEAD_DIM, HEAD_DIM), f32)],
        compiler_params=_cparams("arbitrary"),
    )(qkv, sc, gr)


def _gdn_gate(o, proj, gnw):
    L = o.shape[0]

    def body(o_ref, z_ref, w_ref, m_ref):
        ov, z = o_ref[...], z_ref[...]
        rms = lax.rsqrt(jnp.mean(ov * ov, axis=-1, keepdims=True) + EPS)
        m_ref[...] = (((ov * rms) * w_ref[...]) * (z * _sigmoid(z))).astype(bf16)

    return _pcall(
        body, name="gdn_gate", grid=(HEADS,),
        in_specs=[pl.BlockSpec((L, LANE), lambda j: (0, j)), pl.BlockSpec((L, LANE), lambda j: (0, OFF_ZG // LANE + j)),
                  pl.BlockSpec((1, LANE), lambda j: (0, 0))],
        out_specs=pl.BlockSpec((L, LANE), lambda j: (0, j)),
        out_shape=jax.ShapeDtypeStruct((L, GDN_WIDTH), bf16),
        compiler_params=_cparams("parallel"),
    )(o, proj, gnw)


def _conv3(u, cw_ref):
    return cw_ref[2:3, :] * u + cw_ref[1:2, :] * _shift_down(u, 1) + cw_ref[0:1, :] * _shift_down(u, 2)


def _conv_specs(L):
    blk = lambda off: pl.BlockSpec((L, LANE), lambda j, off=off: (0, off // LANE + j))
    return [blk(OFF_B), blk(OFF_C), blk(OFF_HC), blk(OFF_ZC),
            pl.BlockSpec((3, LANE), lambda j: (0, j)), pl.BlockSpec((1, LANE), lambda j: (0, j))]


def _conv_fwd(proj, cw, cb):
    L = proj.shape[0]

    def body(b_ref, c_ref, h_ref, z_ref, cw_ref, cb_ref, m_ref):
        z = z_ref[...]
        cv = _conv3(c_ref[...] * h_ref[...], cw_ref) + cb_ref[...]
        m_ref[...] = ((b_ref[...] * cv) * (z * _sigmoid(z))).astype(bf16)

    return _pcall(
        body, name="conv_fwd", grid=(CONV_WIDTH // LANE,),
        in_specs=_conv_specs(L),
        out_specs=pl.BlockSpec((L, LANE), lambda j: (0, j)),
        out_shape=jax.ShapeDtypeStruct((L, CONV_WIDTH), bf16),
        compiler_params=_cparams("parallel"),
    )(proj, proj, proj, proj, cw, cb)


def _out_proj_loss(x, mix_a, mix_b, wo, fw, tgt):
    L = x.shape[0]
    tm = min(256, L)

    def body(x_ref, ma_ref, mb_ref, wo_ref, fw_ref, t_ref, dy_ref, dyb_ref, dma_ref, dmb_ref, gfw_ref, loss_ref):
        @pl.when(pl.program_id(0) == 0)
        def _():
            gfw_ref[...] = jnp.zeros_like(gfw_ref)
            loss_ref[...] = jnp.zeros_like(loss_ref)
        y = x_ref[...] + jnp.dot(ma_ref[...], wo_ref[:GDN_WIDTH, :], preferred_element_type=f32) \
            + jnp.dot(mb_ref[...], wo_ref[GDN_WIDTH:, :], preferred_element_type=f32)
        r = lax.rsqrt(jnp.mean(y * y, axis=-1, keepdims=True) + EPS)
        yh = y * r
        fwv = fw_ref[...]
        diff = yh * fwv - t_ref[...]
        loss_ref[...] += jnp.sum(jnp.sum(diff * diff, axis=-1, keepdims=True), axis=0, keepdims=True) * (0.5 / D_MODEL)
        dout = diff * (1.0 / D_MODEL)
        gfw_ref[...] += jnp.sum(dout * yh, axis=0, keepdims=True)
        dyh = dout * fwv
        dy = r * (dyh - yh * jnp.mean(dyh * yh, axis=-1, keepdims=True))
        dy_ref[...] = dy
        dyb = dy.astype(bf16)
        dyb_ref[...] = dyb
        dma_ref[...] = lax.dot_general(dyb, wo_ref[:GDN_WIDTH, :], (((1,), (1,)), ((), ())), preferred_element_type=f32)
        dmb_ref[...] = lax.dot_general(dyb, wo_ref[GDN_WIDTH:, :], (((1,), (1,)), ((), ())), preferred_element_type=f32)

    row = lambda i: (i, 0)
    fix = lambda i: (0, 0)
    act = jax.ShapeDtypeStruct((L, D_MODEL), f32)
    return _pcall(
        body, name="out_proj_loss", grid=(L // tm,),
        in_specs=[pl.BlockSpec((tm, D_MODEL), row), pl.BlockSpec((tm, GDN_WIDTH), row), pl.BlockSpec((tm, CONV_WIDTH), row),
                  pl.BlockSpec((GDN_WIDTH + CONV_WIDTH, D_MODEL), fix), pl.BlockSpec((1, D_MODEL), fix),
                  pl.BlockSpec((tm, D_MODEL), row)],
        out_specs=[pl.BlockSpec((tm, D_MODEL), row), pl.BlockSpec((tm, D_MODEL), row), pl.BlockSpec((tm, GDN_WIDTH), row),
                   pl.BlockSpec((tm, CONV_WIDTH), row), pl.BlockSpec((1, D_MODEL), fix), pl.BlockSpec((1, LANE), fix)],
        out_shape=[act, jax.ShapeDtypeStruct((L, D_MODEL), bf16), act, act,
                   jax.ShapeDtypeStruct((1, D_MODEL), f32), jax.ShapeDtypeStruct((1, LANE), f32)],
        compiler_params=_cparams("arbitrary"),
    )(x, mix_a, mix_b, wo, fw, tgt)


def _tn_matmul(a, b, name):
    L, M = a.shape
    N = b.shape[1]
    tm = 512 if M % 512 == 0 else M

    def body(a_ref, b_ref, o_ref):
        o_ref[...] = lax.dot_general(a_ref[...], b_ref[...], (((0,), (0,)), ((), ())),
                                     preferred_element_type=f32).astype(o_ref.dtype)

    return _pcall(
        body, name=name, grid=(M // tm,),
        in_specs=[pl.BlockSpec((L, tm), lambda i: (0, i)), pl.BlockSpec((L, N), lambda i: (0, 0))],
        out_specs=pl.BlockSpec((tm, N), lambda i: (i, 0)),
        out_shape=jax.ShapeDtypeStruct((M, N), bf16),
        compiler_params=_cparams("parallel"),
    )(a, b)


def _gdn_gate_bwd(o, proj, gnw, dmix_a):
    L = o.shape[0]

    def body(o_ref, z_ref, w_ref, dm_ref, do_ref, dz_ref, gw_ref):
        @pl.when(pl.program_id(0) == 0)
        def _():
            gw_ref[...] = jnp.zeros_like(gw_ref)
        ov, z, dm, wv = o_ref[...], z_ref[...], dm_ref[...], w_ref[...]
        rms = lax.rsqrt(jnp.mean(ov * ov, axis=-1, keepdims=True) + EPS)
        xh = ov * rms
        sg = _sigmoid(z)
        d_on = dm * (z * sg)
        dz_ref[...] = (dm * (xh * wv) * (sg * (1.0 + z * (1.0 - sg)))).astype(bf16)
        gw_ref[...] += jnp.sum(d_on * xh, axis=0, keepdims=True)
        dxh = d_on * wv
        do_ref[...] = rms * (dxh - xh * jnp.mean(dxh * xh, axis=-1, keepdims=True))

    return _pcall(
        body, name="gdn_gate_bwd", grid=(HEADS,),
        in_specs=[pl.BlockSpec((L, LANE), lambda j: (0, j)), pl.BlockSpec((L, LANE), lambda j: (0, OFF_ZG // LANE + j)),
                  pl.BlockSpec((1, LANE), lambda j: (0, 0)), pl.BlockSpec((L, LANE), lambda j: (0, j))],
        out_specs=[pl.BlockSpec((L, LANE), lambda j: (0, j)), pl.BlockSpec((L, LANE), lambda j: (0, j)),
                   pl.BlockSpec((1, LANE), lambda j: (0, 0))],
        out_shape=[jax.ShapeDtypeStruct((L, GDN_WIDTH), f32), jax.ShapeDtypeStruct((L, GDN_WIDTH), bf16),
                   jax.ShapeDtypeStruct((1, LANE), f32)],
        compiler_params=_cparams("arbitrary"),
    )(o, proj, gnw, dmix_a)


def _conv_bwd(proj, cw, cb, dmix_b):
    L = proj.shape[0]

    def body(b_ref, c_ref, h_ref, z_ref, cw_ref, cb_ref, dm_ref, db_ref, dc_ref, dh_ref, dz_ref, gcw_ref, gcb_ref):
        bv, cv_, hv, z, dm = b_ref[...], c_ref[...], h_ref[...], z_ref[...], dm_ref[...]
        u = cv_ * hv
        cv = _conv3(u, cw_ref) + cb_ref[...]
        sg = _sigmoid(z)
        sz = z * sg
        db_ref[...] = (dm * cv * sz).astype(bf16)
        dz_ref[...] = (dm * (bv * cv) * (sg * (1.0 + z * (1.0 - sg)))).astype(bf16)
        dcv = dm * bv * sz
        gcb_ref[...] = jnp.sum(dcv, axis=0, keepdims=True)
        gcw_ref[2:3, :] = jnp.sum(dcv * u, axis=0, keepdims=True)
        gcw_ref[1:2, :] = jnp.sum(dcv * _shift_down(u, 1), axis=0, keepdims=True)
        gcw_ref[0:1, :] = jnp.sum(dcv * _shift_down(u, 2), axis=0, keepdims=True)
        du = cw_ref[2:3, :] * dcv + cw_ref[1:2, :] * _shift_up(dcv, 1) + cw_ref[0:1, :] * _shift_up(dcv, 2)
        dc_ref[...] = (du * hv).astype(bf16)
        dh_ref[...] = (du * cv_).astype(bf16)

    col = pl.BlockSpec((L, LANE), lambda j: (0, j))
    act = jax.ShapeDtypeStruct((L, CONV_WIDTH), bf16)
    return _pcall(
        body, name="conv_bwd", grid=(CONV_WIDTH // LANE,),
        in_specs=_conv_specs(L) + [col],
        out_specs=[col, col, col, col, pl.BlockSpec((3, LANE), lambda j: (0, j)), pl.BlockSpec((1, LANE), lambda j: (0, j))],
        out_shape=[act, act, act, act, jax.ShapeDtypeStruct((3, CONV_WIDTH), f32), jax.ShapeDtypeStruct((1, CONV_WIDTH), f32)],
        compiler_params=_cparams("parallel"),
    )(proj, proj, proj, proj, cw, cb, dmix_b)


def _gdn_bwd(qkv, sc, gr, u_all, w_all, vn_all, t_all, sp_all, do_all):
    L = qkv.shape[0]
    nc = L // CHUNK
    W = GDN_WIDTH

    def body(qkv_ref, sc_ref, gr_ref, u_ref, w_ref, vn_ref, t_ref, sp_ref, do_ref, dqkv_ref, dsc_ref, dgr_ref, ds_scr):
        @pl.when(pl.program_id(0) == 0)
        def _():
            ds_scr[...] = jnp.zeros_like(ds_scr)
        sc_v = sc_ref[...]
        lane = _lanes(sc_v.shape)
        dsc = jnp.zeros(sc_v.shape, f32)
        HS = range(HEADS)
        cs = [slice(h * HEAD_DIM, (h + 1) * HEAD_DIM) for h in HS]
        q = [qkv_ref[:, h * HEAD_DIM:(h + 1) * HEAD_DIM] for h in HS]
        k = [qkv_ref[:, W + h * HEAD_DIM:W + (h + 1) * HEAD_DIM] for h in HS]
        v = [qkv_ref[:, 2 * W + h * HEAD_DIM:2 * W + (h + 1) * HEAD_DIM] for h in HS]
        hsc = [_head_scalars(sc_v, gr_ref, h) for h in HS]
        beta, gcc, gl, dmat, dmat_t = ([x[i] for x in hsc] for i in range(5))
        ii, jj = hsc[0][5], hsc[0][6]
        eg = [jnp.exp(gcc[h]) for h in HS]
        ekl = [jnp.exp(gl[h] - gcc[h]) for h in HS]
        egl = [jnp.exp(gl[h]) for h in HS]
        kb = [k[h] * beta[h] for h in HS]
        ks = [k[h] * ekl[h] for h in HS]
        do = [do_ref[:, cs[h]] for h in HS]
        vn = [vn_ref[:, cs[h]] for h in HS]
        s = [sp_ref[0, cs[h], :] for h in HS]
        dsn = [ds_scr[h] for h in HS]

        kq = [_mm_nt(k[h], q[h]) for h in HS]
        ksd = [_mm(ks[h], dsn[h]) for h in HS]
        p_t = [jnp.where(jj >= ii, kq[h] * dmat_t[h], 0.0) for h in HS]
        ptd = [_mm(p_t[h], do[h]) for h in HS]
        dvn = [ptd[h] + ksd[h] for h in HS]
        dodv = [jnp.concatenate([do[h], dvn[h]], axis=0) for h in HS]
        x1 = [_mm_nt(dodv[h], s[h]) for h in HS]
        dks = [_mm_nt(vn[h], dsn[h]) for h in HS]
        dov = [_mm_nt(do[h], vn[h]) for h in HS]
        vdo = [_mm_nt(vn[h], do[h]) for h in HS]
        kk = [_mm_nt(kb[h], k[h]) for h in HS]
        qk = [_mm_nt(q[h], k[h]) for h in HS]
        w = [w_ref[:, cs[h]] for h in HS]
        qd = [q[h] * eg[h] for h in HS]
        dsq = [_mm_tn(jnp.concatenate([qd[h], -w[h]], axis=0), dodv[h]) for h in HS]
        dgl = [egl[h] * jnp.sum(jnp.sum(s[h] * dsn[h], axis=1, keepdims=True), axis=0, keepdims=True) for h in HS]
        for h in HS:
            ds_scr[h] = egl[h] * dsn[h] + dsq[h]
        dqd = [x1[h][:CHUNK] for h in HS]
        duw = [jnp.concatenate([dvn[h], -x1[h][CHUNK:]], axis=1) for h in HS]
        tdu = [_mm_tn(t_ref[0, h], duw[h]) for h in HS]
        dvk = [duw[h] + tdu[h] for h in HS]
        uw = [jnp.concatenate([u_ref[:, cs[h]], w[h]], axis=1) for h in HS]
        da = [-jnp.where(ii > jj, _mm_nt(dvk[h], uw[h]), 0.0) for h in HS]
        da_t = [-jnp.where(jj > ii, _mm_nt(uw[h], dvk[h]), 0.0) for h in HS]
        dp = [jnp.where(ii >= jj, dov[h], 0.0) for h in HS]
        dp_t = [jnp.where(jj >= ii, vdo[h], 0.0) for h in HS]
        r1 = [_mm(jnp.concatenate([da[h] * dmat[h], dp[h] * dmat[h]], axis=0), k[h]) for h in HS]
        dk1 = [_mm(jnp.concatenate([da_t[h] * dmat_t[h], dp_t[h] * dmat_t[h]], axis=1),
                   jnp.concatenate([kb[h], q[h]], axis=0)) for h in HS]
        dsc = jnp.zeros(sc_v.shape, f32)
        for h in HS:
            a = jnp.where(ii > jj, kk[h] * dmat[h], 0.0)
            p = jnp.where(ii >= jj, qk[h] * dmat[h], 0.0)
            gmat = da[h] * a + dp[h] * p
            dvb, dkbg = dvk[h][:, :HEAD_DIM], dvk[h][:, HEAD_DIM:]
            kbg = kb[h] * eg[h]
            dkb = r1[h][:CHUNK] + dkbg * eg[h]
            dq = r1[h][CHUNK:] + dqd[h] * eg[h]
            dk = dk1[h] + dks[h] * ekl[h] + dkb * beta[h]
            dbeta = jnp.sum(dkb * k[h] + dvb * v[h], axis=1, keepdims=True)
            ksum = jnp.sum(dks[h] * ks[h], axis=1, keepdims=True)
            dgl_tot = dgl[h] + jnp.sum(ksum, axis=0, keepdims=True)
            dgc = jnp.sum(gmat, axis=1, keepdims=True) + jnp.sum(dqd[h] * qd[h] + dkbg * kbg, axis=1, keepdims=True) - ksum
            dgc = dgc + jnp.where(_rows(dgc.shape) == CHUNK - 1, dgl_tot, 0.0)
            dqkv_ref[:, h * HEAD_DIM:(h + 1) * HEAD_DIM] = dq
            dqkv_ref[:, W + h * HEAD_DIM:W + (h + 1) * HEAD_DIM] = dk
            dqkv_ref[:, 2 * W + h * HEAD_DIM:2 * W + (h + 1) * HEAD_DIM] = dvb * beta[h]
            dsc = jnp.where(lane == h, dbeta, jnp.where(lane == HEADS + h, dgc, dsc))
            dgr_ref[0, h:h + 1, :] = jnp.sum(gmat, axis=0, keepdims=True)
        dsc_ref[...] = dsc

    row = lambda c: (nc - 1 - c, 0)
    return _pcall(
        body, name="gdn_bwd", grid=(nc,),
        in_specs=[pl.BlockSpec((CHUNK, 3 * W), row), pl.BlockSpec((CHUNK, LANE), row),
                  pl.BlockSpec((1, HEADS, CHUNK), lambda c: (nc - 1 - c, 0, 0)),
                  pl.BlockSpec((CHUNK, W), row), pl.BlockSpec((CHUNK, W), row), pl.BlockSpec((CHUNK, W), row),
                  pl.BlockSpec((1, HEADS, CHUNK, CHUNK), lambda c: (nc - 1 - c, 0, 0, 0)),
                  pl.BlockSpec((1, W, HEAD_DIM), lambda c: (nc - 1 - c, 0, 0)), pl.BlockSpec((CHUNK, W), row)],
        out_specs=[pl.BlockSpec((CHUNK, 3 * W), row), pl.BlockSpec((CHUNK, LANE), row),
                   pl.BlockSpec((1, HEADS, CHUNK), lambda c: (nc - 1 - c, 0, 0))],
        out_shape=[jax.ShapeDtypeStruct((L, 3 * W), f32), jax.ShapeDtypeStruct((L, LANE), f32),
                   jax.ShapeDtypeStruct((nc, HEADS, CHUNK), f32)],
        scratch_shapes=[pltpu.VMEM((HEADS, HEAD_DIM, HEAD_DIM), f32)],
        compiler_params=_cparams("arbitrary"),
    )(qkv, sc, gr, u_all, w_all, vn_all, t_all, sp_all, do_all)


def _qkv_bwd(proj, cw, dn):
    L = proj.shape[0]

    def body(x_ref, cw_ref, dn_ref, dx_ref, gcw_ref):
        j = pl.program_id(0)
        x, dn_v = x_ref[...], dn_ref[...]
        c = _conv4(x, cw_ref)
        sg = _sigmoid(c)
        a = c * sg
        rn = lax.rsqrt(jnp.sum(a * a, axis=1, keepdims=True) + EPS)
        scale = jnp.where(j < HEADS, HEAD_DIM ** -0.5, 1.0).astype(f32)
        da_n = (scale * rn) * (dn_v - a * ((rn * rn) * jnp.sum(dn_v * a, axis=1, keepdims=True)))
        da = jnp.where(j < 2 * HEADS, da_n, dn_v)
        dc = da * (sg * (1.0 + c * (1.0 - sg)))
        gcw_ref[3:4, :] = jnp.sum(dc * x, axis=0, keepdims=True)
        gcw_ref[2:3, :] = jnp.sum(dc * _shift_down(x, 1), axis=0, keepdims=True)
        gcw_ref[1:2, :] = jnp.sum(dc * _shift_down(x, 2), axis=0, keepdims=True)
        gcw_ref[0:1, :] = jnp.sum(dc * _shift_down(x, 3), axis=0, keepdims=True)
        dx = (cw_ref[3:4, :] * dc + cw_ref[2:3, :] * _shift_up(dc, 1) + cw_ref[1:2, :] * _shift_up(dc, 2)
              + cw_ref[0:1, :] * _shift_up(dc, 3))
        dx_ref[...] = dx.astype(bf16)

    col = pl.BlockSpec((L, LANE), lambda j: (0, j))
    wspec = pl.BlockSpec((4, LANE), lambda j: (0, j))
    return _pcall(
        body, name="qkv_bwd", grid=(3 * HEADS,),
        in_specs=[col, wspec, col], out_specs=[col, wspec],
        out_shape=[jax.ShapeDtypeStruct((L, 3 * GDN_WIDTH), bf16), jax.ShapeDtypeStruct((4, 3 * GDN_WIDTH), f32)],
        compiler_params=_cparams("parallel"),
    )(proj, cw, dn)


def _scalars_bwd(proj, alog_p, dtb_p, dsc, dgr_col):
    L = proj.shape[0]

    def body(x_ref, al_ref, dt_ref, dsc_ref, dgr_ref, dba_ref, gs_ref):
        x, dsc_v = x_ref[...], dsc_ref[...]
        lane = _lanes(x.shape)
        dec = (lane >= HEADS) & (lane < 2 * HEADS)
        dg = jnp.where(dec, dsc_v - dgr_ref[...], 0.0)
        rc = _rows(x.shape) & (CHUNK - 1)
        for s in (1, 2, 4, 8, 16, 32):
            dg = dg + jnp.where(rc + s < CHUNK, pltpu.roll(dg, L - s, 0), 0.0)
        xa = x + dt_ref[...]
        ea = jnp.exp(al_ref[...])
        g = -ea * _softplus(xa)
        da = dg * (-ea) * _sigmoid(xa)
        beta = _sigmoid(x)
        db = dsc_v * beta * (1.0 - beta)
        dba_ref[...] = jnp.where(lane < HEADS, db, jnp.where(dec, da, 0.0)).astype(bf16)
        gs_ref[...] = jnp.zeros_like(gs_ref)
        gs_ref[0:1, :] = jnp.sum(jnp.where(dec, dg * g, 0.0), axis=0, keepdims=True)
        gs_ref[1:2, :] = jnp.sum(jnp.where(dec, da, 0.0), axis=0, keepdims=True)

    full = pl.BlockSpec((L, LANE), lambda i: (0, 0))
    vec = pl.BlockSpec((1, LANE), lambda i: (0, 0))
    return _pcall(
        body, name="scalars_bwd", grid=(1,),
        in_specs=[pl.BlockSpec((L, LANE), lambda i: (0, OFF_BA // LANE)), vec, vec, full, full],
        out_specs=[full, pl.BlockSpec((8, LANE), lambda i: (0, 0))],
        out_shape=[jax.ShapeDtypeStruct((L, LANE), bf16), jax.ShapeDtypeStruct((8, LANE), f32)],
        compiler_params=_cparams("arbitrary"),
    )(proj, alog_p, dtb_p, dsc, dgr_col)


def _input_grad(pieces, offs, wpad, x, nw, dy):
    L = x.shape[0]
    tm = min(512, L)
    npc = len(pieces)

    def body(*refs):
        p_refs = refs[:npc]
        w_hbm, x_ref, nw_ref, dy_ref, gx_ref, gnw_ref, w_vmem, sem = refs[npc:]

        @pl.when(pl.program_id(0) == 0)
        def _():
            cp = pltpu.make_async_copy(w_hbm, w_vmem, sem)
            cp.start()
            cp.wait()
            gnw_ref[...] = jnp.zeros_like(gnw_ref)
        dh = None
        for p_ref, off in zip(p_refs, offs):
            wd = p_ref.shape[1]
            part = jnp.dot(p_ref[...], w_vmem[off:off + wd, :], preferred_element_type=f32)
            dh = part if dh is None else dh + part
        xv, nwv = x_ref[...], nw_ref[...]
        r = lax.rsqrt(jnp.mean(xv * xv, axis=-1, keepdims=True) + EPS)
        xh = xv * r
        gnw_ref[...] += jnp.sum(dh * xh, axis=0, keepdims=True)
        dxh = dh * nwv
        gx_ref[...] = dy_ref[...] + r * (dxh - xh * jnp.mean(dxh * xh, axis=-1, keepdims=True))

    row = lambda i: (i, 0)
    fix = lambda i: (0, 0)
    return _pcall(
        body, name="input_grad", grid=(L // tm,),
        in_specs=[pl.BlockSpec((tm, p.shape[1]), row) for p in pieces] + [
            ANY, pl.BlockSpec((tm, D_MODEL), row), pl.BlockSpec((1, D_MODEL), fix), pl.BlockSpec((tm, D_MODEL), row)],
        out_specs=[pl.BlockSpec((tm, D_MODEL), row), pl.BlockSpec((1, D_MODEL), fix)],
        out_shape=[jax.ShapeDtypeStruct((L, D_MODEL), f32), jax.ShapeDtypeStruct((1, D_MODEL), f32)],
        scratch_shapes=[pltpu.VMEM(wpad.shape, bf16), pltpu.SemaphoreType.DMA(())],
        compiler_params=_cparams("arbitrary"),
    )(*pieces, wpad, x, nw, dy)


def _adamw_reduce(parts, w, m, v, name):
    R, C = w.shape
    n_parts = parts.shape[0]
    tr = 128 if R % 128 == 0 else R
    c1 = 1.0 - ADAM_B1 ** ADAM_STEP
    c2 = 1.0 - ADAM_B2 ** ADAM_STEP

    def body(p_ref, w_ref, m_ref, v_ref, g_ref, d_ref, nm_ref, nv_ref):
        g = p_ref[0].astype(f32)
        for s in range(1, n_parts):
            g = g + p_ref[s].astype(f32)
        nm = ADAM_B1 * m_ref[...] + (1.0 - ADAM_B1) * g
        nv = ADAM_B2 * v_ref[...] + (1.0 - ADAM_B2) * (g * g)
        g_ref[...] = g
        nm_ref[...] = nm
        nv_ref[...] = nv
        d_ref[...] = -ADAM_LR * ((nm / c1) / (jnp.sqrt(nv / c2) + ADAM_EPS) + ADAM_WD * w_ref[...])

    blk = pl.BlockSpec((tr, C), lambda i: (i, 0))
    out = jax.ShapeDtypeStruct((R, C), f32)
    return _pcall(
        body, name=name, grid=(R // tr,),
        in_specs=[pl.BlockSpec((n_parts, tr, C), lambda i: (0, i, 0)), blk, blk, blk],
        out_specs=[blk] * 4, out_shape=[out] * 4,
        compiler_params=_cparams("parallel"),
    )(parts, w, m, v)


def _adamw_w_in(parts, w3, m3, v3):
    n_parts, n, _ = parts.shape
    c1 = 1.0 - ADAM_B1 ** ADAM_STEP
    c2 = 1.0 - ADAM_B2 ** ADAM_STEP

    def body(p_ref, w_ref, m_ref, v_ref, g_ref, d_ref, nm_ref, nv_ref):
        g = p_ref[0].astype(f32)
        for s in range(1, n_parts):
            g = g + p_ref[s].astype(f32)
        nm = ADAM_B1 * m_ref[:, 0, :] + (1.0 - ADAM_B1) * g
        nv = ADAM_B2 * v_ref[:, 0, :] + (1.0 - ADAM_B2) * (g * g)
        g_ref[:, 0, :] = g
        nm_ref[:, 0, :] = nm
        nv_ref[:, 0, :] = nv
        d_ref[:, 0, :] = -ADAM_LR * ((nm / c1) / (jnp.sqrt(nv / c2) + ADAM_EPS) + ADAM_WD * w_ref[:, 0, :])

    blk = pl.BlockSpec((n, 1, COL_TILE), lambda j: (0, 0, j))
    out = jax.ShapeDtypeStruct((n, 1, D_MODEL), f32)
    return _pcall(
        body, name="adamw_w_in", grid=(D_MODEL // COL_TILE,),
        in_specs=[pl.BlockSpec((n_parts, n, COL_TILE), lambda j: (0, 0, j)), blk, blk, blk],
        out_specs=[blk] * 4, out_shape=[out] * 4,
        compiler_params=_cparams("parallel"),
    )(parts, w3, m3, v3)


def _pad_lanes(vec8, start):
    return jnp.pad(vec8.reshape(1, -1), ((0, 0), (start, LANE - start - vec8.size)))


def kernel(x, norm_in_w, w_in, conv_qkv_w, A_log, dt_bias, gdn_norm_w, conv_w, conv_b, w_out, final_norm_w, loss_target, m_norm_in_w, m_w_in, m_conv_qkv_w, m_A_log, m_dt_bias, m_gdn_norm_w, m_conv_w, m_conv_b, m_w_out, m_final_norm_w, v_norm_in_w, v_w_in, v_conv_qkv_w, v_A_log, v_dt_bias, v_gdn_norm_w, v_conv_w, v_conv_b, v_w_out, v_final_norm_w):
    L = x.shape[1]
    nc = L // CHUNK
    xs = x[0]
    tgt = loss_target[0]
    fnw = final_norm_w.reshape(1, D_MODEL)

    as_rows = lambda a: jnp.transpose(a, (2, 0, 1))
    win_g, wout_g, cqkv_g, cw_g = _all_gather(
        [_cast_w_in(as_rows(w_in)), w_out[0].astype(bf16), conv_qkv_w[0], conv_w[0]], "gather_weights")
    wpad = _relayout_w_in(win_g)
    wo =wout_g.reshape(N_DEV * wout_g.shape[1], D_MODEL)
    cqkv = jnp.concatenate([cqkv_g[d] for d in range(N_DEV)], axis=1)
    cw = jnp.concatenate([cw_g[d] for d in range(N_DEV)], axis=1)
    alog_p = _pad_lanes(A_log, HEADS)
    dtb_p = _pad_lanes(dt_bias, HEADS)

    proj, h = _in_proj(xs, norm_in_w, wpad)
    qkv = _qkv_act(proj, cqkv)
    sc, gr = _scalars(proj, alog_p, dtb_p)
    o, u_all, w_all, vn_all, t_all, sp_all = _gdn_fwd(qkv, sc, gr)
    mix_a = _gdn_gate(o, proj, gdn_norm_w)
    mix_b = _conv_fwd(proj, cw, conv_b)
    dy, dyb, dmix_a, dmix_b, g_fnw, loss_v = _out_proj_loss(xs, mix_a, mix_b, wo, fnw, tgt)

    g_wout = jnp.concatenate([_tn_matmul(mix_a, dyb, "grad_w_out_a"), _tn_matmul(mix_b, dyb, "grad_w_out_b")], axis=0)
    do, dzg, g_gnw = _gdn_gate_bwd(o, proj, gdn_norm_w, dmix_a)
    d_b, d_c, d_hc, d_zc, g_cw, g_cb = _conv_bwd(proj, cw, conv_b, dmix_b)
    dqkv_n, dsc, dgr = _gdn_bwd(qkv, sc, gr, u_all, w_all, vn_all, t_all, sp_all, do)
    dqkv, g_cqkv = _qkv_bwd(proj, cqkv, dqkv_n)
    dgr_col = jnp.pad(dgr.transpose(0, 2, 1).reshape(L, HEADS), ((0, 0), (HEADS, LANE - 2 * HEADS)))
    dba, g_sc = _scalars_bwd(proj, alog_p, dtb_p, dsc, dgr_col)
    pieces = [dqkv, dzg, dba, d_b, d_c, d_hc, d_zc]
    offs = [OFF_QKV, OFF_ZG, OFF_BA, OFF_B, OFF_C, OFF_HC, OFF_ZC]
    grad_x, g_nw = _input_grad(pieces, offs, wpad, xs, norm_in_w, dy)
    g_parts = [_tn_matmul(p, h, "grad_w_in_%d" % i) for i, p in enumerate(pieces)]
    g_win_blk = _grad_blocks(g_parts)

    big = [g_win_blk, g_wout.reshape(N_DEV, -1, D_MODEL)]
    p_win, p_wout = _pair_exchange(big, "exchange_grads_pair")
    r_win, r_wout = _quad_exchange(
        [_pair_sum(big[0], p_win, "pair_sum_w_in"), _pair_sum(big[1], p_wout, "pair_sum_w_out")], "exchange_grads_chips")
    r_cqkv, r_cw = _all_to_all(
        [g_cqkv.reshape(4, N_DEV, -1).transpose(1, 0, 2), g_cw.reshape(3, N_DEV, -1).transpose(1, 0, 2)],
        "exchange_small_sharded_grads")
    upd_win = [jnp.transpose(a, (1, 2, 0)) for a in _adamw_w_in(r_win, as_rows(w_in), as_rows(m_w_in), as_rows(v_w_in))]
    upd_wout = _adamw_reduce(r_wout, w_out[0], m_w_out[0], v_w_out[0], "adamw_w_out")
    upd_cqkv = _adamw_reduce(r_cqkv, conv_qkv_w[0], m_conv_qkv_w[0], v_conv_qkv_w[0], "adamw_conv_qkv_w")
    upd_cw = _adamw_reduce(r_cw, conv_w[0], m_conv_w[0], v_conv_w[0], "adamw_conv_w")

    def pack(nw_, cb_, fw_, gn_, al_, dt_, ls_):
        return jnp.concatenate([nw_.reshape(1, -1), cb_.reshape(1, -1), fw_.reshape(1, -1), gn_.reshape(1, -1),
                                _pad_lanes(al_, 0), _pad_lanes(dt_, 0), ls_.reshape(1, -1)], axis=1)

    zl = jnp.zeros((1, LANE), f32)
    small_g = pack(g_nw, g_cb, g_fnw, g_gnw, g_sc[0, HEADS:2 * HEADS], g_sc[1, HEADS:2 * HEADS], loss_v)
    (small_all,) = _all_gather([small_g], "gather_small_grads")
    upd_small = _adamw_reduce(
        small_all, pack(norm_in_w, conv_b, final_norm_w, gdn_norm_w, A_log, dt_bias, zl),
        pack(m_norm_in_w, m_conv_b, m_final_norm_w, m_gdn_norm_w, m_A_log, m_dt_bias, zl),
        pack(v_norm_in_w, v_conv_b, v_final_norm_w, v_gdn_norm_w, v_A_log, v_dt_bias, zl), "adamw_small")

    o_nw, o_cb, o_fw, o_gn = 0, D_MODEL, 2 * D_MODEL, 3 * D_MODEL
    o_al, o_dt, o_ls = o_gn + LANE, o_gn + 2 * LANE, o_gn + 3 * LANE

    def unpack(k):
        s = upd_small[k]
        return dict(
            norm_in_w=s[:, o_nw:o_nw + D_MODEL], conv_b=s[:, o_cb:o_cb + D_MODEL], final_norm_w=s[0, o_fw:o_fw + D_MODEL],
            gdn_norm_w=s[:, o_gn:o_gn + LANE], A_log=s[:, o_al:o_al + HEADS], dt_bias=s[:, o_dt:o_dt + HEADS])

    loss = upd_small[0][0, o_ls]
    outs = [loss, grad_x[None]]
    for k in range(4):
        sm = unpack(k)
        outs += [sm["norm_in_w"], upd_win[k], upd_cqkv[k][None], sm["A_log"], sm["dt_bias"], sm["gdn_norm_w"],
                 upd_cw[k][None], sm["conv_b"], upd_wout[k][None], sm["final_norm_w"]]
    return tuple(outs)
```

```python
import functools
import math

import jax
import jax.numpy as jnp
from jax import lax
from jax.experimental import pallas as pl
from jax.experimental.pallas import tpu as pltpu

f32 = jnp.float32
bf16 = jnp.bfloat16

N_DEV = 8
D_MODEL = 1024
HEADS = 8
HEAD_DIM = 128
CHUNK = 64
GDN_WIDTH = HEADS * HEAD_DIM
CONV_WIDTH = 1024
PROJ_WIDTH = 8208
SHARD_W = PROJ_WIDTH // N_DEV
EPS = 1e-6

NAT_SMALL_END = 4112
PAD_COLS = 112
OFF_QKV, OFF_ZG, OFF_BA, OFF_B, OFF_C, OFF_HC, OFF_ZC = 0, 3072, 4096, 4224, 5248, 6272, 7296
PROJ_PAD = 8320
LANE = 128

ADAM_LR, ADAM_B1, ADAM_B2, ADAM_EPS, ADAM_WD, ADAM_STEP = 0.001, 0.9, 0.999, 1e-08, 0.01, 10

VMEM_LIMIT = 56 * 1024 * 1024

MESH = pl.DeviceIdType.MESH
ANY = pl.BlockSpec(memory_space=pl.ANY)


def _pcall(body, **kw):
    return pl.pallas_call(body, **kw)


def _cparams(*sem):
    return pltpu.CompilerParams(dimension_semantics=sem if sem else None, vmem_limit_bytes=VMEM_LIMIT)


def _mm(a, b):
    return jnp.dot(a.astype(bf16), b.astype(bf16), preferred_element_type=f32)


def _mm_nt(a, b):
    return lax.dot_general(a.astype(bf16), b.astype(bf16), (((1,), (1,)), ((), ())), preferred_element_type=f32)


def _mm_tn(a, b):
    return lax.dot_general(a.astype(bf16), b.astype(bf16), (((0,), (0,)), ((), ())), preferred_element_type=f32)


def _rows(shape):
    return lax.broadcasted_iota(jnp.int32, shape, 0)


def _lanes(shape):
    return lax.broadcasted_iota(jnp.int32, shape, 1)


def _shift_down(x, s):
    if s == 0:
        return x
    return jnp.where(_rows(x.shape) >= s, pltpu.roll(x, s, 0), 0.0)


def _shift_up(x, s):
    if s == 0:
        return x
    n = x.shape[0]
    return jnp.where(_rows(x.shape) < n - s, pltpu.roll(x, n - s, 0), 0.0)


def _sigmoid(x):
    return jax.nn.sigmoid(x)


def _softplus(x):
    e = jnp.exp(-jnp.abs(x))
    small = e * (1.0 - e * (0.5 - e * (1.0 / 3.0)))
    return jnp.maximum(x, 0.0) + jnp.where(e < 0.01, small, jnp.log(1.0 + e))


def _mesh_pos():
    return lax.axis_index("x"), lax.axis_index("y"), lax.axis_index("c")


def _flat(px, py, pc):
    return 4 * px + 2 * py + pc


def _all_gather(xs, name):
    n = len(xs)

    def body(*refs):
        x_refs, o_refs = refs[:n], refs[n:2 * n]
        send_sems, recv_sems, local_sems = refs[2 * n:]
        x, y, c = _mesh_pos()
        me, sibling = (x, y, c), (x, y, 1 - c)
        chips = [(1 - x, y), (x, 1 - y), (1 - x, 1 - y)]

        def copy(a, k, block, to, src=None):
            dst = o_refs[a].at[_flat(*block)]
            return pltpu.make_async_remote_copy(
                src_ref=dst if src is None else src, dst_ref=dst,
                send_sem=send_sems.at[a, k], recv_sem=recv_sems.at[a, k], device_id=to, device_id_type=MESH)

        mine, first, passed = [], [], []
        for a in range(n):
            cp = pltpu.make_async_copy(x_refs[a], o_refs[a].at[_flat(*me)], local_sems.at[a])
            cp.start()
            mine.append(cp)
            fa = [copy(a, 0, me, sibling, src=x_refs[a])]
            fa += [copy(a, 1 + j, me, (*chip, c), src=x_refs[a]) for j, chip in enumerate(chips)]
            for cp in fa:
                cp.start()
            first += fa
        for a in range(n):
            for j, chip in enumerate(chips):
                copy(a, 1 + j, (*chip, c), me).wait_recv()
                cp = copy(a, 4 + j, (*chip, c), sibling)
                cp.start()
                passed.append(cp)
        for a in range(n):
            copy(a, 0, sibling, me).wait_recv()
            for j, chip in enumerate(chips):
                copy(a, 4 + j, (*chip, 1 - c), me).wait_recv()
        for cp in first + passed:
            cp.wait_send()
        for cp in mine:
            cp.wait()

    outs = _pcall(
        body, name=name,
        out_shape=[jax.ShapeDtypeStruct((N_DEV,) + a.shape, a.dtype) for a in xs],
        in_specs=[ANY] * n, out_specs=[ANY] * n,
        scratch_shapes=[pltpu.SemaphoreType.DMA((n, 7)), pltpu.SemaphoreType.DMA((n, 7)), pltpu.SemaphoreType.DMA((n,))],
    )(*xs)
    return list(outs)


def _all_to_all(gs, name):
    n = len(gs)

    def body(*refs):
        g_refs, o_refs = refs[:n], refs[n:2 * n]
        send_sems, recv_sems, local_sems = refs[2 * n:]
        x, y, c = _mesh_pos()
        me = _flat(x, y, c)
        peers = []
        for k in range(1, N_DEV):
            kx, ky, kc = (k >> 2) & 1, (k >> 1) & 1, k & 1
            px = (1 - x) if kx else x
            py = (1 - y) if ky else y
            pc = (1 - c) if kc else c
            peers.append((px, py, pc))

        def copy(a, k):
            peer = peers[k - 1]
            return pltpu.make_async_remote_copy(
                src_ref=g_refs[a].at[_flat(*peer)], dst_ref=o_refs[a].at[me],
                send_sem=send_sems.at[a, k - 1], recv_sem=recv_sems.at[a, k - 1], device_id=peer, device_id_type=MESH)

        def arrival(a, k):
            peer = peers[k - 1]
            return pltpu.make_async_remote_copy(
                src_ref=g_refs[a].at[me], dst_ref=o_refs[a].at[_flat(*peer)],
                send_sem=send_sems.at[a, k - 1], recv_sem=recv_sems.at[a, k - 1], device_id=peer, device_id_type=MESH)

        mine, sent = [], []
        for a in range(n):
            cp = pltpu.make_async_copy(g_refs[a].at[me], o_refs[a].at[me], local_sems.at[a])
            cp.start()
            mine.append(cp)
            for k in range(1, N_DEV):
                cp = copy(a, k)
                cp.start()
                sent.append(cp)
        for a in range(n):
            for k in range(1, N_DEV):
                arrival(a, k).wait_recv()
        for cp in sent:
            cp.wait_send()
        for cp in mine:
            cp.wait()

    outs = _pcall(
        body, name=name,
        out_shape=[jax.ShapeDtypeStruct(a.shape, a.dtype) for a in gs],
        in_specs=[ANY] * n, out_specs=[ANY] * n,
        scratch_shapes=[pltpu.SemaphoreType.DMA((n, 7)), pltpu.SemaphoreType.DMA((n, 7)), pltpu.SemaphoreType.DMA((n,))],
    )(*gs)
    return list(outs)


def _pair_exchange(gs, name):
    n = len(gs)
    chips = [(0, 0), (0, 1), (1, 0), (1, 1)]

    def body(*refs):
        g_refs, o_refs = refs[:n], refs[n:2 * n]
        send_sems, recv_sems = refs[2 * n:]
        x, y, c = _mesh_pos()
        sibling = (x, y, 1 - c)

        def copy(a, i):
            xp, yp = chips[i]
            return pltpu.make_async_remote_copy(
                src_ref=g_refs[a].at[_flat(xp, yp, 1 - c)], dst_ref=o_refs[a].at[i],
                send_sem=send_sems.at[a, i], recv_sem=recv_sems.at[a, i], device_id=sibling, device_id_type=MESH)

        cps = [copy(a, i) for a in range(n) for i in range(4)]
        for cp in cps:
            cp.start()
        for cp in cps:
            cp.wait()

    outs = _pcall(
        body, name=name,
        out_shape=[jax.ShapeDtypeStruct((4,) + a.shape[1:], a.dtype) for a in gs],
        in_specs=[ANY] * n, out_specs=[ANY] * n,
        scratch_shapes=[pltpu.SemaphoreType.DMA((n, 4)), pltpu.SemaphoreType.DMA((n, 4))],
    )(*gs)
    return list(outs)


def _pair_sum(g, p1, name):
    _, R, C = g.shape
    tr = 256 if R % 256 == 0 else R
    cidx = lax.axis_index("c").astype(jnp.int32).reshape(1)

    def body(c_ref, g_ref, p_ref, o_ref):
        o_ref[...] = (g_ref[...].astype(f32) + p_ref[...].astype(f32)).astype(o_ref.dtype)

    return _pcall(
        body, name=name,
        grid_spec=pltpu.PrefetchScalarGridSpec(
            num_scalar_prefetch=1, grid=(4, R // tr),
            in_specs=[pl.BlockSpec((1, tr, C), lambda i, r, c_ref: (2 * i + c_ref[0], r, 0)),
                      pl.BlockSpec((1, tr, C), lambda i, r, c_ref: (i, r, 0))],
            out_specs=pl.BlockSpec((1, tr, C), lambda i, r, c_ref: (i, r, 0))),
        out_shape=jax.ShapeDtypeStruct((4, R, C), g.dtype),
        compiler_params=_cparams("parallel", "parallel"),
    )(cidx, g, p1)


def _quad_exchange(ss, name):
    n = len(ss)

    def body(*refs):
        s_refs, o_refs = refs[:n], refs[n:2 * n]
        send_sems, recv_sems, local_sems = refs[2 * n:]
        x, y, c = _mesh_pos()
        me = 2 * x + y
        peers = [(1 - x, y), (x, 1 - y), (1 - x, 1 - y)]

        def copy(a, j, arriving):
            px, py = peers[j]
            src_slot, dst_slot = (me, 2 * px + py) if arriving else (2 * px + py, me)
            return pltpu.make_async_remote_copy(
                src_ref=s_refs[a].at[src_slot], dst_ref=o_refs[a].at[dst_slot],
                send_sem=send_sems.at[a, j], recv_sem=recv_sems.at[a, j], device_id=(px, py, c), device_id_type=MESH)

        mine, sent = [], []
        for a in range(n):
            cp = pltpu.make_async_copy(s_refs[a].at[me], o_refs[a].at[me], local_sems.at[a])
            cp.start()
            mine.append(cp)
            for j in range(3):
                cp = copy(a, j, False)
                cp.start()
                sent.append(cp)
        for a in range(n):
            for j in range(3):
                copy(a, j, True).wait_recv()
        for cp in sent:
            cp.wait_send()
        for cp in mine:
            cp.wait()

    outs = _pcall(
        body, name=name,
        out_shape=[jax.ShapeDtypeStruct(a.shape, a.dtype) for a in ss],
        in_specs=[ANY] * n, out_specs=[ANY] * n,
        scratch_shapes=[pltpu.SemaphoreType.DMA((n, 3)), pltpu.SemaphoreType.DMA((n, 3)), pltpu.SemaphoreType.DMA((n,))],
    )(*ss)
    return list(outs)


HBM = pl.BlockSpec(memory_space=pltpu.HBM)
SEM = pl.BlockSpec(memory_space=pltpu.SEMAPHORE)
EFFECT = pltpu.SideEffectType.DATAFLOW_SIDE_EFFECTING


def _peers(x, y, c):
    out = []
    for k in range(1, N_DEV):
        kx, ky, kc = (k >> 2) & 1, (k >> 1) & 1, k & 1
        out.append(((1 - x) if kx else x, (1 - y) if ky else y, (1 - c) if kc else c))
    return out


def _spread_copy(src_ref, land_ref, send_sems, recv_sems, k, scatter):
    x, y, c = _mesh_pos()
    peer = _peers(x, y, c)[k]
    src = src_ref.at[_flat(*peer)] if scatter else src_ref
    return pltpu.make_async_remote_copy(
        src_ref=src, dst_ref=land_ref.at[_flat(x, y, c)], send_sem=send_sems.at[k], recv_sem=recv_sems.at[k],
        device_id=peer, device_id_type=MESH)


def _spread_start(src, after, scatter, name):
    land_shape = src.shape if scatter else (N_DEV,) + src.shape

    def body(src_ref, land_ref, after_ref, send_sems, recv_sems, src_thru, land_thru, token):
        for k in range(N_DEV - 1):
            _spread_copy(src_ref, land_ref, send_sems, recv_sems, k, scatter).start()
        token[...] = jnp.zeros_like(token)

    return _pcall(
        body, name=name,
        out_shape=(pltpu.SemaphoreType.DMA((N_DEV - 1,)), pltpu.SemaphoreType.DMA((N_DEV - 1,)),
                   pltpu.HBM(src.shape, src.dtype), pltpu.HBM(land_shape, src.dtype), jax.ShapeDtypeStruct((8, LANE), f32)),
        in_specs=(HBM, HBM, ANY), out_specs=(SEM, SEM, HBM, HBM, pl.BlockSpec(memory_space=pltpu.VMEM)),
        input_output_aliases={0: 2, 1: 3},
        compiler_params=pltpu.CompilerParams(has_side_effects=EFFECT),
    )(pltpu.with_memory_space_constraint(src, pltpu.HBM),
      pltpu.with_memory_space_constraint(lax.empty(land_shape, src.dtype), pltpu.HBM), after)


def _spread_wait(started, after, scatter, name):
    send_sems, recv_sems, src_thru, land_thru, _ = started

    def body(src_ref, land_ref, send_sems, recv_sems, after_ref, src_dead, got_ref):
        for k in range(N_DEV - 1):
            cp = _spread_copy(src_ref, land_ref, send_sems, recv_sems, k, scatter)
            cp.wait_send()
            cp.wait_recv()

    return _pcall(
        body, name=name,
        out_shape=(pltpu.HBM(src_thru.shape, src_thru.dtype), pltpu.HBM(land_thru.shape, land_thru.dtype)),
        in_specs=(HBM, HBM, SEM, SEM, ANY), out_specs=(HBM, HBM), input_output_aliases={0: 0, 1: 1},
        compiler_params=pltpu.CompilerParams(has_side_effects=EFFECT),
    )(src_thru, land_thru, send_sems, recv_sems, after)[1]


def _own_slot(land, block):
    x, y, c = _mesh_pos()
    zero = jnp.zeros((), jnp.int32)
    return lax.dynamic_update_slice(land, block[None], (_flat(x, y, c).astype(jnp.int32),) + (zero,) * block.ndim)


PIECE_NAT = (0, 3072, 4096, 4112, 5136, 6160, 7184, PROJ_WIDTH)


COL_TILE = 256


def _cast_w_in(w3):
    n = w3.shape[0]

    def body(w_ref, o_ref):
        o_ref[...] = w_ref[:, 0, :].astype(bf16)

    return _pcall(
        body, name="cast_w_in", grid=(D_MODEL // COL_TILE,),
        in_specs=[pl.BlockSpec((n, 1, COL_TILE), lambda j: (0, 0, j))],
        out_specs=pl.BlockSpec((n, COL_TILE), lambda j: (0, j)),
        out_shape=jax.ShapeDtypeStruct((n, D_MODEL), bf16),
        compiler_params=_cparams("parallel"),
    )(w3)


def _relayout_w_in(win_g):
    def body(g_ref, o_ref):
        o_ref[NAT_SMALL_END:NAT_SMALL_END + PAD_COLS, :] = jnp.zeros((PAD_COLS, COL_TILE), o_ref.dtype)
        for d in range(N_DEV):
            n0, n1 = d * SHARD_W, (d + 1) * SHARD_W
            cut = min(max(NAT_SMALL_END - n0, 0), SHARD_W)
            if cut > 0:
                o_ref[n0:n0 + cut, :] = g_ref[d, 0:cut, :]
            if cut < SHARD_W:
                o_ref[n0 + cut + PAD_COLS:n1 + PAD_COLS, :] = g_ref[d, cut:SHARD_W, :]

    return _pcall(
        body, name="relayout_w_in", grid=(D_MODEL // COL_TILE,),
        in_specs=[pl.BlockSpec((N_DEV, SHARD_W, COL_TILE), lambda j: (0, 0, j))],
        out_specs=pl.BlockSpec((PROJ_PAD, COL_TILE), lambda j: (0, j)),
        out_shape=jax.ShapeDtypeStruct((PROJ_PAD, D_MODEL), win_g.dtype),
        compiler_params=_cparams("parallel"),
    )(win_g)


def _grad_blocks(g_parts):
    npc = len(g_parts)

    def body(*refs):
        p_refs, o_ref = refs[:npc], refs[npc]
        for d in range(N_DEV):
            n0, n1 = d * SHARD_W, (d + 1) * SHARD_W
            for i in range(npc):
                lo, hi = max(n0, PIECE_NAT[i]), min(n1, PIECE_NAT[i + 1])
                if lo < hi:
                    o_ref[d, lo - n0:hi - n0, :] = p_refs[i][lo - PIECE_NAT[i]:hi - PIECE_NAT[i], :]

    return _pcall(
        body, name="grad_blocks", grid=(D_MODEL // COL_TILE,),
        in_specs=[pl.BlockSpec((p.shape[0], COL_TILE), lambda j: (0, j)) for p in g_parts],
        out_specs=pl.BlockSpec((N_DEV, SHARD_W, COL_TILE), lambda j: (0, 0, j)),
        out_shape=jax.ShapeDtypeStruct((N_DEV, SHARD_W, D_MODEL), bf16),
        compiler_params=_cparams("parallel"),
    )(*g_parts)


def _in_proj(x, nw, wpad_t):
    L = x.shape[0]
    tn = 640
    nj = wpad_t.shape[0] // tn

    def body(x_ref, nw_ref, w_ref, proj_ref, h_ref):
        @pl.when(pl.program_id(0) == 0)
        def _():
            for r in range(0, L, 256):
                xs = x_ref[r:r + 256, :]
                ms = jnp.mean(xs * xs, axis=-1, keepdims=True)
                h_ref[r:r + 256, :] = ((xs * lax.rsqrt(ms + EPS)) * nw_ref[...]).astype(bf16)
        for r in range(0, L, 512):
            proj_ref[r:r + 512, :] = lax.dot_general(h_ref[r:r + 512, :], w_ref[...], (((1,), (1,)), ((), ())),
                                                     preferred_element_type=f32)

    return _pcall(
        body, name="in_proj", grid=(nj,),
        in_specs=[pl.BlockSpec((L, D_MODEL), lambda j: (0, 0)), pl.BlockSpec((1, D_MODEL), lambda j: (0, 0)),
                  pl.BlockSpec((tn, D_MODEL), lambda j: (j, 0))],
        out_specs=[pl.BlockSpec((L, tn), lambda j: (0, j)), pl.BlockSpec((L, D_MODEL), lambda j: (0, 0))],
        out_shape=[jax.ShapeDtypeStruct((L, wpad_t.shape[0]), f32), jax.ShapeDtypeStruct((L, D_MODEL), bf16)],
        compiler_params=_cparams("arbitrary"),
    )(x, nw, wpad_t)


def _conv4(x, cw_ref):
    return (cw_ref[3:4, :] * x + cw_ref[2:3, :] * _shift_down(x, 1) + cw_ref[1:2, :] * _shift_down(x, 2)
            + cw_ref[0:1, :] * _shift_down(x, 3))


def _qkv_act(proj, cw):
    L = proj.shape[0]

    def body(x_ref, cw_ref, o_ref):
        j = pl.program_id(0)
        c = _conv4(x_ref[...], cw_ref)
        a = c * _sigmoid(c)
        rn = lax.rsqrt(jnp.sum(a * a, axis=1, keepdims=True) + EPS)
        scale = jnp.where(j < HEADS, HEAD_DIM ** -0.5, 1.0).astype(f32)
        o_ref[...] = jnp.where(j < 2 * HEADS, (a * rn) * scale, a)

    return _pcall(
        body, name="qkv_act", grid=(3 * HEADS,),
        in_specs=[pl.BlockSpec((L, LANE), lambda j: (0, j)), pl.BlockSpec((4, LANE), lambda j: (0, j))],
        out_specs=pl.BlockSpec((L, LANE), lambda j: (0, j)),
        out_shape=jax.ShapeDtypeStruct((L, 3 * GDN_WIDTH), f32),
        compiler_params=_cparams("parallel"),
    )(proj, cw)


def _scalars(proj, alog_p, dtb_p):
    L = proj.shape[0]
    nc = L // CHUNK

    def body(x_ref, al_ref, dt_ref, sc_ref, gr_ref):
        x = x_ref[...]
        lane = _lanes(x.shape)
        beta = _sigmoid(x)
        g = -jnp.exp(al_ref[...]) * _softplus(x + dt_ref[...])
        gc = jnp.where((lane >= HEADS) & (lane < 2 * HEADS), g, 0.0)
        rc = _rows(x.shape) & (CHUNK - 1)
        for s in (1, 2, 4, 8, 16, 32):
            gc = gc + jnp.where(rc >= s, pltpu.roll(gc, s, 0), 0.0)
        sc_ref[...] = jnp.where(lane < HEADS, beta, gc)
        sel = (_lanes((HEADS, LANE)) == _rows((HEADS, LANE)) + HEADS).astype(f32)
        for c in range(nc):
            gr_ref[c] = lax.dot_general(sel, sc_ref[c * CHUNK:(c + 1) * CHUNK, :], (((1,), (1,)), ((), ())),
                                        preferred_element_type=f32, precision=lax.Precision.HIGHEST)

    return _pcall(
        body, name="scalars", grid=(1,),
        in_specs=[pl.BlockSpec((L, LANE), lambda i: (0, OFF_BA // LANE)), pl.BlockSpec((1, LANE), lambda i: (0, 0)),
                  pl.BlockSpec((1, LANE), lambda i: (0, 0))],
        out_specs=[pl.BlockSpec((L, LANE), lambda i: (0, 0)), pl.BlockSpec((nc, HEADS, CHUNK), lambda i: (0, 0, 0))],
        out_shape=[jax.ShapeDtypeStruct((L, LANE), f32), jax.ShapeDtypeStruct((nc, HEADS, CHUNK), f32)],
        compiler_params=_cparams("arbitrary"),
    )(proj, alog_p, dtb_p)


def _head_scalars(sc, gr_ref, h):
    lane = _lanes(sc.shape)
    beta = jnp.sum(jnp.where(lane == h, sc, 0.0), axis=1, keepdims=True)
    gcc = jnp.sum(jnp.where(lane == HEADS + h, sc, 0.0), axis=1, keepdims=True)
    gcr = gr_ref[0, h:h + 1, :]
    gl = jnp.sum(jnp.where(_lanes(gcr.shape) == CHUNK - 1, gcr, 0.0), axis=1, keepdims=True)
    ii, jj = _rows((CHUNK, CHUNK)), _lanes((CHUNK, CHUNK))
    dmat = jnp.where(ii >= jj, jnp.exp(jnp.minimum(gcc - gcr, 0.0)), 0.0)
    dmat_t = jnp.where(jj >= ii, jnp.exp(jnp.minimum(gcr - gcc, 0.0)), 0.0)
    return beta, gcc, gl, dmat, dmat_t, ii, jj


def _gdn_fwd(qkv, sc, gr):
    L = qkv.shape[0]
    nc = L // CHUNK
    W = GDN_WIDTH

    def body(qkv_ref, sc_ref, gr_ref, o_ref, u_ref, w_ref, vn_ref, t_ref, sp_ref, s_scr):
        @pl.when(pl.program_id(0) == 0)
        def _():
            s_scr[...] = jnp.zeros_like(s_scr)
        sc_v = sc_ref[...]
        HS = range(HEADS)
        cs = [slice(h * HEAD_DIM, (h + 1) * HEAD_DIM) for h in HS]
        q = [qkv_ref[:, h * HEAD_DIM:(h + 1) * HEAD_DIM] for h in HS]
        k = [qkv_ref[:, W + h * HEAD_DIM:W + (h + 1) * HEAD_DIM] for h in HS]
        v = [qkv_ref[:, 2 * W + h * HEAD_DIM:2 * W + (h + 1) * HEAD_DIM] for h in HS]
        hsc = [_head_scalars(sc_v, gr_ref, h) for h in HS]
        beta, gcc, gl, dmat = ([x[i] for x in hsc] for i in range(4))
        ii, jj = hsc[0][5], hsc[0][6]
        eg = [jnp.exp(gcc[h]) for h in HS]
        kb = [k[h] * beta[h] for h in HS]
        kk = [_mm_nt(kb[h], k[h]) for h in HS]
        qk = [_mm_nt(q[h], k[h]) for h in HS]
        xp = [-jnp.where(ii > jj, kk[h] * dmat[h], 0.0) for h in HS]
        t = xp
        for _ in range(5):
            xp = [_mm(xp[h], xp[h]) for h in HS]
            tx = [_mm(t[h], xp[h]) for h in HS]
            t = [t[h] + xp[h] + tx[h] for h in HS]
        vb = [v[h] * beta[h] for h in HS]
        kbg = [kb[h] * eg[h] for h in HS]
        uw = [_mm(t[h], jnp.concatenate([vb[h], kbg[h]], axis=1)) for h in HS]
        u = [vb[h] + uw[h][:, :HEAD_DIM] for h in HS]
        w = [kbg[h] + uw[h][:, HEAD_DIM:] for h in HS]
        s = [s_scr[h] for h in HS]
        ws = [_mm(jnp.concatenate([w[h], q[h] * eg[h]], axis=0), s[h]) for h in HS]
        vn = [u[h] - ws[h][:CHUNK] for h in HS]
        p = [jnp.where(ii >= jj, qk[h] * dmat[h], 0.0) for h in HS]
        pv = [_mm(p[h], vn[h]) for h in HS]
        kv = [_mm_tn(k[h] * jnp.exp(gl[h] - gcc[h]), vn[h]) for h in HS]
        for h in HS:
            sp_ref[0, cs[h], :] = s[h]
            o_ref[:, cs[h]] = ws[h][CHUNK:] + pv[h]
            s_scr[h] = jnp.exp(gl[h]) * s[h] + kv[h]
            u_ref[:, cs[h]] = u[h]
            w_ref[:, cs[h]] = w[h]
            vn_ref[:, cs[h]] = vn[h]
            t_ref[0, h] = t[h]

    row = lambda c: (c, 0)
    act = jax.ShapeDtypeStruct((L, W), f32)
    return _pcall(
        body, name="gdn_fwd", grid=(nc,),
        in_specs=[pl.BlockSpec((CHUNK, 3 * W), row), pl.BlockSpec((CHUNK, LANE), row),
                  pl.BlockSpec((1, HEADS, CHUNK), lambda c: (c, 0, 0))],
        out_specs=[pl.BlockSpec((CHUNK, W), row)] * 4 + [
            pl.BlockSpec((1, HEADS, CHUNK, CHUNK), lambda c: (c, 0, 0, 0)),
            pl.BlockSpec((1, W, HEAD_DIM), lambda c: (c, 0, 0))],
        out_shape=[act, act, act, act, jax.ShapeDtypeStruct((nc, HEADS, CHUNK, CHUNK), f32),
                   jax.ShapeDtypeStruct((nc, W, HEAD_DIM), f32)],
        scratch_shapes=[pltpu.VMEM((HEADS, HEAD_DIM, HEAD_DIM), f32)],
        compiler_params=_cparams("arbitrary"),
    )(qkv, sc, gr)


def _gdn_gate(o, proj, gnw):
    L = o.shape[0]

    def body(o_ref, z_ref, w_ref, m_ref):
        ov, z = o_ref[...], z_ref[...]
        rms = lax.rsqrt(jnp.mean(ov * ov, axis=-1, keepdims=True) + EPS)
        m_ref[...] = (((ov * rms) * w_ref[...]) * (z * _sigmoid(z))).astype(bf16)

    return _pcall(
        body, name="gdn_gate", grid=(HEADS,),
        in_specs=[pl.BlockSpec((L, LANE), lambda j: (0, j)), pl.BlockSpec((L, LANE), lambda j: (0, OFF_ZG // LANE + j)),
                  pl.BlockSpec((1, LANE), lambda j: (0, 0))],
        out_specs=pl.BlockSpec((L, LANE), lambda j: (0, j)),
        out_shape=jax.ShapeDtypeStruct((L, GDN_WIDTH), bf16),
        compiler_params=_cparams("parallel"),
    )(o, proj, gnw)


def _conv3(u, cw_ref):
    return cw_ref[2:3, :] * u + cw_ref[1:2, :] * _shift_down(u, 1) + cw_ref[0:1, :] * _shift_down(u, 2)


def _conv_specs(L):
    blk = lambda off: pl.BlockSpec((L, LANE), lambda j, off=off: (0, off // LANE + j))
    return [blk(OFF_B), blk(OFF_C), blk(OFF_HC), blk(OFF_ZC),
            pl.BlockSpec((3, LANE), lambda j: (0, j)), pl.BlockSpec((1, LANE), lambda j: (0, j))]


def _conv_fwd(proj, cw, cb):
    L = proj.shape[0]

    def body(b_ref, c_ref, h_ref, z_ref, cw_ref, cb_ref, m_ref):
        z = z_ref[...]
        cv = _conv3(c_ref[...] * h_ref[...], cw_ref) + cb_ref[...]
        m_ref[...] = ((b_ref[...] * cv) * (z * _sigmoid(z))).astype(bf16)

    return _pcall(
        body, name="conv_fwd", grid=(CONV_WIDTH // LANE,),
        in_specs=_conv_specs(L),
        out_specs=pl.BlockSpec((L, LANE), lambda j: (0, j)),
        out_shape=jax.ShapeDtypeStruct((L, CONV_WIDTH), bf16),
        compiler_params=_cparams("parallel"),
    )(proj, proj, proj, proj, cw, cb)


def _out_proj_loss(x, mix_a, mix_b, wo, fw, tgt):
    L = x.shape[0]
    tm = min(256, L)

    def body(x_ref, ma_ref, mb_ref, wo_ref, fw_ref, t_ref, dy_ref, dyb_ref, dma_ref, dmb_ref, gfw_ref, loss_ref):
        @pl.when(pl.program_id(0) == 0)
        def _():
            gfw_ref[...] = jnp.zeros_like(gfw_ref)
            loss_ref[...] = jnp.zeros_like(loss_ref)
        y = x_ref[...] + jnp.dot(ma_ref[...], wo_ref[:GDN_WIDTH, :], preferred_element_type=f32) \
            + jnp.dot(mb_ref[...], wo_ref[GDN_WIDTH:, :], preferred_element_type=f32)
        r = lax.rsqrt(jnp.mean(y * y, axis=-1, keepdims=True) + EPS)
        yh = y * r
        fwv = fw_ref[...]
        diff = yh * fwv - t_ref[...]
        loss_ref[...] += jnp.sum(jnp.sum(diff * diff, axis=-1, keepdims=True), axis=0, keepdims=True) * (0.5 / D_MODEL)
        dout = diff * (1.0 / D_MODEL)
        gfw_ref[...] += jnp.sum(dout * yh, axis=0, keepdims=True)
        dyh = dout * fwv
        dy = r * (dyh - yh * jnp.mean(dyh * yh, axis=-1, keepdims=True))
        dy_ref[...] = dy
        dyb = dy.astype(bf16)
        dyb_ref[...] = dyb
        dma_ref[...] = lax.dot_general(dyb, wo_ref[:GDN_WIDTH, :], (((1,), (1,)), ((), ())), preferred_element_type=f32)
        dmb_ref[...] = lax.dot_general(dyb, wo_ref[GDN_WIDTH:, :], (((1,), (1,)), ((), ())), preferred_element_type=f32)

    row = lambda i: (i, 0)
    fix = lambda i: (0, 0)
    act = jax.ShapeDtypeStruct((L, D_MODEL), f32)
    return _pcall(
        body, name="out_proj_loss", grid=(L // tm,),
        in_specs=[pl.BlockSpec((tm, D_MODEL), row), pl.BlockSpec((tm, GDN_WIDTH), row), pl.BlockSpec((tm, CONV_WIDTH), row),
                  pl.BlockSpec((GDN_WIDTH + CONV_WIDTH, D_MODEL), fix), pl.BlockSpec((1, D_MODEL), fix),
                  pl.BlockSpec((tm, D_MODEL), row)],
        out_specs=[pl.BlockSpec((tm, D_MODEL), row), pl.BlockSpec((tm, D_MODEL), row), pl.BlockSpec((tm, GDN_WIDTH), row),
                   pl.BlockSpec((tm, CONV_WIDTH), row), pl.BlockSpec((1, D_MODEL), fix), pl.BlockSpec((1, LANE), fix)],
        out_shape=[act, jax.ShapeDtypeStruct((L, D_MODEL), bf16), act, act,
                   jax.ShapeDtypeStruct((1, D_MODEL), f32), jax.ShapeDtypeStruct((1, LANE), f32)],
        compiler_params=_cparams("arbitrary"),
    )(x, mix_a, mix_b, wo, fw, tgt)


def _tn_matmul(a, b, name):
    L, M = a.shape
    N = b.shape[1]
    tm = 512 if M % 512 == 0 else M

    def body(a_ref, b_ref, o_ref):
        o_ref[...] = lax.dot_general(a_ref[...], b_ref[...], (((0,), (0,)), ((), ())),
                                     preferred_element_type=f32).astype(o_ref.dtype)

    return _pcall(
        body, name=name, grid=(M // tm,),
        in_specs=[pl.BlockSpec((L, tm), lambda i: (0, i)), pl.BlockSpec((L, N), lambda i: (0, 0))],
        out_specs=pl.BlockSpec((tm, N), lambda i: (i, 0)),
        out_shape=jax.ShapeDtypeStruct((M, N), bf16),
        compiler_params=_cparams("parallel"),
    )(a, b)


def _gdn_gate_bwd(o, proj, gnw, dmix_a):
    L = o.shape[0]

    def body(o_ref, z_ref, w_ref, dm_ref, do_ref, dz_ref, gw_ref):
        @pl.when(pl.program_id(0) == 0)
        def _():
            gw_ref[...] = jnp.zeros_like(gw_ref)
        ov, z, dm, wv = o_ref[...], z_ref[...], dm_ref[...], w_ref[...]
        rms = lax.rsqrt(jnp.mean(ov * ov, axis=-1, keepdims=True) + EPS)
        xh = ov * rms
        sg = _sigmoid(z)
        d_on = dm * (z * sg)
        dz_ref[...] = (dm * (xh * wv) * (sg * (1.0 + z * (1.0 - sg)))).astype(bf16)
        gw_ref[...] += jnp.sum(d_on * xh, axis=0, keepdims=True)
        dxh = d_on * wv
        do_ref[...] = rms * (dxh - xh * jnp.mean(dxh * xh, axis=-1, keepdims=True))

    return _pcall(
        body, name="gdn_gate_bwd", grid=(HEADS,),
        in_specs=[pl.BlockSpec((L, LANE), lambda j: (0, j)), pl.BlockSpec((L, LANE), lambda j: (0, OFF_ZG // LANE + j)),
                  pl.BlockSpec((1, LANE), lambda j: (0, 0)), pl.BlockSpec((L, LANE), lambda j: (0, j))],
        out_specs=[pl.BlockSpec((L, LANE), lambda j: (0, j)), pl.BlockSpec((L, LANE), lambda j: (0, j)),
                   pl.BlockSpec((1, LANE), lambda j: (0, 0))],
        out_shape=[jax.ShapeDtypeStruct((L, GDN_WIDTH), f32), jax.ShapeDtypeStruct((L, GDN_WIDTH), bf16),
                   jax.ShapeDtypeStruct((1, LANE), f32)],
        compiler_params=_cparams("arbitrary"),
    )(o, proj, gnw, dmix_a)


def _conv_bwd(proj, cw, cb, dmix_b):
    L = proj.shape[0]

    def body(b_ref, c_ref, h_ref, z_ref, cw_ref, cb_ref, dm_ref, db_ref, dc_ref, dh_ref, dz_ref, gcw_ref, gcb_ref):
        bv, cv_, hv, z, dm = b_ref[...], c_ref[...], h_ref[...], z_ref[...], dm_ref[...]
        u = cv_ * hv
        cv = _conv3(u, cw_ref) + cb_ref[...]
        sg = _sigmoid(z)
        sz = z * sg
        db_ref[...] = (dm * cv * sz).astype(bf16)
        dz_ref[...] = (dm * (bv * cv) * (sg * (1.0 + z * (1.0 - sg)))).astype(bf16)
        dcv = dm * bv * sz
        gcb_ref[...] = jnp.sum(dcv, axis=0, keepdims=True)
        gcw_ref[2:3, :] = jnp.sum(dcv * u, axis=0, keepdims=True)
        gcw_ref[1:2, :] = jnp.sum(dcv * _shift_down(u, 1), axis=0, keepdims=True)
        gcw_ref[0:1, :] = jnp.sum(dcv * _shift_down(u, 2), axis=0, keepdims=True)
        du = cw_ref[2:3, :] * dcv + cw_ref[1:2, :] * _shift_up(dcv, 1) + cw_ref[0:1, :] * _shift_up(dcv, 2)
        dc_ref[...] = (du * hv).astype(bf16)
        dh_ref[...] = (du * cv_).astype(bf16)

    col = pl.BlockSpec((L, LANE), lambda j: (0, j))
    act = jax.ShapeDtypeStruct((L, CONV_WIDTH), bf16)
    return _pcall(
        body, name="conv_bwd", grid=(CONV_WIDTH // LANE,),
        in_specs=_conv_specs(L) + [col],
        out_specs=[col, col, col, col, pl.BlockSpec((3, LANE), lambda j: (0, j)), pl.BlockSpec((1, LANE), lambda j: (0, j))],
        out_shape=[act, act, act, act, jax.ShapeDtypeStruct((3, CONV_WIDTH), f32), jax.ShapeDtypeStruct((1, CONV_WIDTH), f32)],
        compiler_params=_cparams("parallel"),
    )(proj, proj, proj, proj, cw, cb, dmix_b)


def _gdn_bwd(qkv, sc, gr, u_all, w_all, vn_all, t_all, sp_all, do_all):
    L = qkv.shape[0]
    nc = L // CHUNK
    W = GDN_WIDTH

    def body(qkv_ref, sc_ref, gr_ref, u_ref, w_ref, vn_ref, t_ref, sp_ref, do_ref, dqkv_ref, dsc_ref, dgr_ref, ds_scr):
        @pl.when(pl.program_id(0) == 0)
        def _():
            ds_scr[...] = jnp.zeros_like(ds_scr)
        sc_v = sc_ref[...]
        lane = _lanes(sc_v.shape)
        dsc = jnp.zeros(sc_v.shape, f32)
        HS = range(HEADS)
        cs = [slice(h * HEAD_DIM, (h + 1) * HEAD_DIM) for h in HS]
        q = [qkv_ref[:, h * HEAD_DIM:(h + 1) * HEAD_DIM] for h in HS]
        k = [qkv_ref[:, W + h * HEAD_DIM:W + (h + 1) * HEAD_DIM] for h in HS]
        v = [qkv_ref[:, 2 * W + h * HEAD_DIM:2 * W + (h + 1) * HEAD_DIM] for h in HS]
        hsc = [_head_scalars(sc_v, gr_ref, h) for h in HS]
        beta, gcc, gl, dmat, dmat_t = ([x[i] for x in hsc] for i in range(5))
        ii, jj = hsc[0][5], hsc[0][6]
        eg = [jnp.exp(gcc[h]) for h in HS]
        ekl = [jnp.exp(gl[h] - gcc[h]) for h in HS]
        egl = [jnp.exp(gl[h]) for h in HS]
        kb = [k[h] * beta[h] for h in HS]
        ks = [k[h] * ekl[h] for h in HS]
        do = [do_ref[:, cs[h]] for h in HS]
        vn = [vn_ref[:, cs[h]] for h in HS]
        s = [sp_ref[0, cs[h], :] for h in HS]
        dsn = [ds_scr[h] for h in HS]

        kq = [_mm_nt(k[h], q[h]) for h in HS]
        ksd = [_mm(ks[h], dsn[h]) for h in HS]
        p_t = [jnp.where(jj >= ii, kq[h] * dmat_t[h], 0.0) for h in HS]
        ptd = [_mm(p_t[h], do[h]) for h in HS]
        dvn = [ptd[h] + ksd[h] for h in HS]
        dodv = [jnp.concatenate([do[h], dvn[h]], axis=0) for h in HS]
        x1 = [_mm_nt(dodv[h], s[h]) for h in HS]
        dks = [_mm_nt(vn[h], dsn[h]) for h in HS]
        dov = [_mm_nt(do[h], vn[h]) for h in HS]
        vdo = [_mm_nt(vn[h], do[h]) for h in HS]
        kk = [_mm_nt(kb[h], k[h]) for h in HS]
        qk = [_mm_nt(q[h], k[h]) for h in HS]
        w = [w_ref[:, cs[h]] for h in HS]
        qd = [q[h] * eg[h] for h in HS]
        dsq = [_mm_tn(jnp.concatenate([qd[h], -w[h]], axis=0), dodv[h]) for h in HS]
        dgl = [egl[h] * jnp.sum(jnp.sum(s[h] * dsn[h], axis=1, keepdims=True), axis=0, keepdims=True) for h in HS]
        for h in HS:
            ds_scr[h] = egl[h] * dsn[h] + dsq[h]
        dqd = [x1[h][:CHUNK] for h in HS]
        duw = [jnp.concatenate([dvn[h], -x1[h][CHUNK:]], axis=1) for h in HS]
        tdu = [_mm_tn(t_ref[0, h], duw[h]) for h in HS]
        dvk = [duw[h] + tdu[h] for h in HS]
        uw = [jnp.concatenate([u_ref[:, cs[h]], w[h]], axis=1) for h in HS]
        da = [-jnp.where(ii > jj, _mm_nt(dvk[h], uw[h]), 0.0) for h in HS]
        da_t = [-jnp.where(jj > ii, _mm_nt(uw[h], dvk[h]), 0.0) for h in HS]
        dp = [jnp.where(ii >= jj, dov[h], 0.0) for h in HS]
        dp_t = [jnp.where(jj >= ii, vdo[h], 0.0) for h in HS]
        r1 = [_mm(jnp.concatenate([da[h] * dmat[h], dp[h] * dmat[h]], axis=0), k[h]) for h in HS]
        dk1 = [_mm(jnp.concatenate([da_t[h] * dmat_t[h], dp_t[h] * dmat_t[h]], axis=1),
                   jnp.concatenate([kb[h], q[h]], axis=0)) for h in HS]
        dsc = jnp.zeros(sc_v.shape, f32)
        for h in HS:
            a = jnp.where(ii > jj, kk[h] * dmat[h], 0.0)
            p = jnp.where(ii >= jj, qk[h] * dmat[h], 0.0)
            gmat = da[h] * a + dp[h] * p
            dvb, dkbg = dvk[h][:, :HEAD_DIM], dvk[h][:, HEAD_DIM:]
            kbg = kb[h] * eg[h]
            dkb = r1[h][:CHUNK] + dkbg * eg[h]
            dq = r1[h][CHUNK:] + dqd[h] * eg[h]
            dk = dk1[h] + dks[h] * ekl[h] + dkb * beta[h]
            dbeta = jnp.sum(dkb * k[h] + dvb * v[h], axis=1, keepdims=True)
            ksum = jnp.sum(dks[h] * ks[h], axis=1, keepdims=True)
            dgl_tot = dgl[h] + jnp.sum(ksum, axis=0, keepdims=True)
            dgc = jnp.sum(gmat, axis=1, keepdims=True) + jnp.sum(dqd[h] * qd[h] + dkbg * kbg, axis=1, keepdims=True) - ksum
            dgc = dgc + jnp.where(_rows(dgc.shape) == CHUNK - 1, dgl_tot, 0.0)
            dqkv_ref[:, h * HEAD_DIM:(h + 1) * HEAD_DIM] = dq
            dqkv_ref[:, W + h * HEAD_DIM:W + (h + 1) * HEAD_DIM] = dk
            dqkv_ref[:, 2 * W + h * HEAD_DIM:2 * W + (h + 1) * HEAD_DIM] = dvb * beta[h]
            dsc = jnp.where(lane == h, dbeta, jnp.where(lane == HEADS + h, dgc, dsc))
            dgr_ref[0, h:h + 1, :] = jnp.sum(gmat, axis=0, keepdims=True)
        dsc_ref[...] = dsc

    row = lambda c: (nc - 1 - c, 0)
    return _pcall(
        body, name="gdn_bwd", grid=(nc,),
        in_specs=[pl.BlockSpec((CHUNK, 3 * W), row), pl.BlockSpec((CHUNK, LANE), row),
                  pl.BlockSpec((1, HEADS, CHUNK), lambda c: (nc - 1 - c, 0, 0)),
                  pl.BlockSpec((CHUNK, W), row), pl.BlockSpec((CHUNK, W), row), pl.BlockSpec((CHUNK, W), row),
                  pl.BlockSpec((1, HEADS, CHUNK, CHUNK), lambda c: (nc - 1 - c, 0, 0, 0)),
                  pl.BlockSpec((1, W, HEAD_DIM), lambda c: (nc - 1 - c, 0, 0)), pl.BlockSpec((CHUNK, W), row)],
        out_specs=[pl.BlockSpec((CHUNK, 3 * W), row), pl.BlockSpec((CHUNK, LANE), row),
                   pl.BlockSpec((1, HEADS, CHUNK), lambda c: (nc - 1 - c, 0, 0))],
        out_shape=[jax.ShapeDtypeStruct((L, 3 * W), f32), jax.ShapeDtypeStruct((L, LANE), f32),
                   jax.ShapeDtypeStruct((nc, HEADS, CHUNK), f32)],
        scratch_shapes=[pltpu.VMEM((HEADS, HEAD_DIM, HEAD_DIM), f32)],
        compiler_params=_cparams("arbitrary"),
    )(qkv, sc, gr, u_all, w_all, vn_all, t_all, sp_all, do_all)


def _qkv_bwd(proj, cw, dn):
    L = proj.shape[0]

    def body(x_ref, cw_ref, dn_ref, dx_ref, gcw_ref):
        j = pl.program_id(0)
        x, dn_v = x_ref[...], dn_ref[...]
        c = _conv4(x, cw_ref)
        sg = _sigmoid(c)
        a = c * sg
        rn = lax.rsqrt(jnp.sum(a * a, axis=1, keepdims=True) + EPS)
        scale = jnp.where(j < HEADS, HEAD_DIM ** -0.5, 1.0).astype(f32)
        da_n = (scale * rn) * (dn_v - a * ((rn * rn) * jnp.sum(dn_v * a, axis=1, keepdims=True)))
        da = jnp.where(j < 2 * HEADS, da_n, dn_v)
        dc = da * (sg * (1.0 + c * (1.0 - sg)))
        gcw_ref[3:4, :] = jnp.sum(dc * x, axis=0, keepdims=True)
        gcw_ref[2:3, :] = jnp.sum(dc * _shift_down(x, 1), axis=0, keepdims=True)
        gcw_ref[1:2, :] = jnp.sum(dc * _shift_down(x, 2), axis=0, keepdims=True)
        gcw_ref[0:1, :] = jnp.sum(dc * _shift_down(x, 3), axis=0, keepdims=True)
        dx = (cw_ref[3:4, :] * dc + cw_ref[2:3, :] * _shift_up(dc, 1) + cw_ref[1:2, :] * _shift_up(dc, 2)
              + cw_ref[0:1, :] * _shift_up(dc, 3))
        dx_ref[...] = dx.astype(bf16)

    col = pl.BlockSpec((L, LANE), lambda j: (0, j))
    wspec = pl.BlockSpec((4, LANE), lambda j: (0, j))
    return _pcall(
        body, name="qkv_bwd", grid=(3 * HEADS,),
        in_specs=[col, wspec, col], out_specs=[col, wspec],
        out_shape=[jax.ShapeDtypeStruct((L, 3 * GDN_WIDTH), bf16), jax.ShapeDtypeStruct((4, 3 * GDN_WIDTH), f32)],
        compiler_params=_cparams("parallel"),
    )(proj, cw, dn)


def _scalars_bwd(proj, alog_p, dtb_p, dsc, dgr_col):
    L = proj.shape[0]

    def body(x_ref, al_ref, dt_ref, dsc_ref, dgr_ref, dba_ref, gs_ref):
        x, dsc_v = x_ref[...], dsc_ref[...]
        lane = _lanes(x.shape)
        dec = (lane >= HEADS) & (lane < 2 * HEADS)
        dg = jnp.where(dec, dsc_v - dgr_ref[...], 0.0)
        rc = _rows(x.shape) & (CHUNK - 1)
        for s in (1, 2, 4, 8, 16, 32):
            dg = dg + jnp.where(rc + s < CHUNK, pltpu.roll(dg, L - s, 0), 0.0)
        xa = x + dt_ref[...]
        ea = jnp.exp(al_ref[...])
        g = -ea * _softplus(xa)
        da = dg * (-ea) * _sigmoid(xa)
        beta = _sigmoid(x)
        db = dsc_v * beta * (1.0 - beta)
        dba_ref[...] = jnp.where(lane < HEADS, db, jnp.where(dec, da, 0.0)).astype(bf16)
        g_al = jnp.sum(jnp.where(dec, dg * g, 0.0), axis=0, keepdims=True)
        g_dt = jnp.sum(jnp.where(dec, da, 0.0), axis=0, keepdims=True)
        row8 = _rows(gs_ref.shape)
        gs = jnp.where(row8 == 0, g_al, jnp.where(row8 == 1, g_dt, 0.0))
        gs_ref[...] = pltpu.roll(gs, LANE - HEADS, 1)

    full = pl.BlockSpec((L, LANE), lambda i: (0, 0))
    vec = pl.BlockSpec((1, LANE), lambda i: (0, 0))
    return _pcall(
        body, name="scalars_bwd", grid=(1,),
        in_specs=[pl.BlockSpec((L, LANE), lambda i: (0, OFF_BA // LANE)), vec, vec, full, full],
        out_specs=[full, pl.BlockSpec((8, LANE), lambda i: (0, 0))],
        out_shape=[jax.ShapeDtypeStruct((L, LANE), bf16), jax.ShapeDtypeStruct((8, LANE), f32)],
        compiler_params=_cparams("arbitrary"),
    )(proj, alog_p, dtb_p, dsc, dgr_col)


def _input_grad(pieces, offs, wpad, x, nw, dy):
    L = x.shape[0]
    tm = min(512, L)
    npc = len(pieces)

    def body(*refs):
        p_refs = refs[:npc]
        w_hbm, x_ref, nw_ref, dy_ref, gx_ref, gnw_ref, w_vmem, sem = refs[npc:]

        @pl.when(pl.program_id(0) == 0)
        def _():
            cp = pltpu.make_async_copy(w_hbm, w_vmem, sem)
            cp.start()
            cp.wait()
            gnw_ref[...] = jnp.zeros_like(gnw_ref)
        dh = None
        for p_ref, off in zip(p_refs, offs):
            wd = p_ref.shape[1]
            part = jnp.dot(p_ref[...], w_vmem[off:off + wd, :], preferred_element_type=f32)
            dh = part if dh is None else dh + part
        xv, nwv = x_ref[...], nw_ref[...]
        r = lax.rsqrt(jnp.mean(xv * xv, axis=-1, keepdims=True) + EPS)
        xh = xv * r
        gnw_ref[...] += jnp.sum(dh * xh, axis=0, keepdims=True)
        dxh = dh * nwv
        gx_ref[...] = dy_ref[...] + r * (dxh - xh * jnp.mean(dxh * xh, axis=-1, keepdims=True))

    row = lambda i: (i, 0)
    fix = lambda i: (0, 0)
    return _pcall(
        body, name="input_grad", grid=(L // tm,),
        in_specs=[pl.BlockSpec((tm, p.shape[1]), row) for p in pieces] + [
            ANY, pl.BlockSpec((tm, D_MODEL), row), pl.BlockSpec((1, D_MODEL), fix), pl.BlockSpec((tm, D_MODEL), row)],
        out_specs=[pl.BlockSpec((tm, D_MODEL), row), pl.BlockSpec((1, D_MODEL), fix)],
        out_shape=[jax.ShapeDtypeStruct((L, D_MODEL), f32), jax.ShapeDtypeStruct((1, D_MODEL), f32)],
        scratch_shapes=[pltpu.VMEM(wpad.shape, bf16), pltpu.SemaphoreType.DMA(())],
        compiler_params=_cparams("arbitrary"),
    )(*pieces, wpad, x, nw, dy)


def _adamw_reduce(parts, w, m, v, name):
    R, C = w.shape
    n_parts = parts.shape[0]
    tr = 128 if R % 128 == 0 else R
    c1 = 1.0 - ADAM_B1 ** ADAM_STEP
    c2 = 1.0 - ADAM_B2 ** ADAM_STEP

    def body(p_ref, w_ref, m_ref, v_ref, g_ref, d_ref, nm_ref, nv_ref):
        g = p_ref[0].astype(f32)
        for s in range(1, n_parts):
            g = g + p_ref[s].astype(f32)
        nm = ADAM_B1 * m_ref[...] + (1.0 - ADAM_B1) * g
        nv = ADAM_B2 * v_ref[...] + (1.0 - ADAM_B2) * (g * g)
        g_ref[...] = g
        nm_ref[...] = nm
        nv_ref[...] = nv
        d_ref[...] = -ADAM_LR * ((nm / c1) / (jnp.sqrt(nv / c2) + ADAM_EPS) + ADAM_WD * w_ref[...])

    blk = pl.BlockSpec((tr, C), lambda i: (i, 0))
    out = jax.ShapeDtypeStruct((R, C), f32)
    return _pcall(
        body, name=name, grid=(R // tr,),
        in_specs=[pl.BlockSpec((n_parts, tr, C), lambda i: (0, i, 0)), blk, blk, blk],
        out_specs=[blk] * 4, out_shape=[out] * 4,
        compiler_params=_cparams("parallel"),
    )(parts, w, m, v)


SMALL_SLOTS = ((0, D_MODEL), (D_MODEL, D_MODEL), (2 * D_MODEL, D_MODEL), (3 * D_MODEL, LANE),
               (3 * D_MODEL + LANE, HEADS), (3 * D_MODEL + 2 * LANE, HEADS))
SMALL_LOSS = 3 * D_MODEL + 3 * LANE
SMALL_W = SMALL_LOSS + LANE


def _pack_small(gs):
    def body(nw_ref, cb_ref, fw_ref, gn_ref, sc_ref, ls_ref, o_ref):
        for ref, (start, width) in zip((nw_ref, cb_ref, fw_ref, gn_ref), SMALL_SLOTS[:4]):
            o_ref[:, start:start + width] = ref[...]
        o_ref[:, SMALL_SLOTS[4][0]:SMALL_SLOTS[4][0] + LANE] = sc_ref[0:1, :]
        o_ref[:, SMALL_SLOTS[5][0]:SMALL_SLOTS[5][0] + LANE] = sc_ref[1:2, :]
        o_ref[:, SMALL_LOSS:SMALL_W] = ls_ref[...]

    vm = pl.BlockSpec(memory_space=pltpu.VMEM)
    return _pcall(body, name="pack_small_grads", out_shape=jax.ShapeDtypeStruct((1, SMALL_W), f32),
                  in_specs=[vm] * 6, out_specs=vm)(*gs)


def _adamw_small(parts, ws, ms, vs):
    c1 = 1.0 - ADAM_B1 ** ADAM_STEP
    c2 = 1.0 - ADAM_B2 ** ADAM_STEP
    np_ = len(ws)

    def body(*refs):
        p_ref = refs[0]
        w_refs, m_refs, v_refs = refs[1:1 + np_], refs[1 + np_:1 + 2 * np_], refs[1 + 2 * np_:1 + 3 * np_]
        outs = refs[1 + 3 * np_:]
        g_refs, d_refs, nm_refs, nv_refs = (outs[i * np_:(i + 1) * np_] for i in range(4))
        loss_ref = outs[4 * np_]

        def total(start, width):
            t = p_ref[0, :, start:start + width]
            for s in range(1, N_DEV):
                t = t + p_ref[s, :, start:start + width]
            return t

        for i, (start, width) in enumerate(SMALL_SLOTS):
            g = total(start, width)
            nm = ADAM_B1 * m_refs[i][...] + (1.0 - ADAM_B1) * g
            nv = ADAM_B2 * v_refs[i][...] + (1.0 - ADAM_B2) * (g * g)
            g_refs[i][...] = g
            nm_refs[i][...] = nm
            nv_refs[i][...] = nv
            d_refs[i][...] = -ADAM_LR * ((nm / c1) / (jnp.sqrt(nv / c2) + ADAM_EPS) + ADAM_WD * w_refs[i][...])
        loss_ref[...] = total(SMALL_LOSS, LANE)

    vm = pl.BlockSpec(memory_space=pltpu.VMEM)
    shapes = [jax.ShapeDtypeStruct(w.shape, f32) for w in ws]
    res = _pcall(body, name="adamw_small", out_shape=shapes * 4 + [jax.ShapeDtypeStruct((1, LANE), f32)],
                 in_specs=[vm] * (1 + 3 * np_), out_specs=[vm] * (4 * np_ + 1))(parts, *ws, *ms, *vs)
    return [res[i * np_:(i + 1) * np_] for i in range(4)], res[4 * np_]


def _adamw_w_in(parts, w3, m3, v3):
    n_parts, n, _ = parts.shape
    c1 = 1.0 - ADAM_B1 ** ADAM_STEP
    c2 = 1.0 - ADAM_B2 ** ADAM_STEP

    def body(p_ref, w_ref, m_ref, v_ref, g_ref, d_ref, nm_ref, nv_ref):
        g = p_ref[0].astype(f32)
        for s in range(1, n_parts):
            g = g + p_ref[s].astype(f32)
        nm = ADAM_B1 * m_ref[:, 0, :] + (1.0 - ADAM_B1) * g
        nv = ADAM_B2 * v_ref[:, 0, :] + (1.0 - ADAM_B2) * (g * g)
        g_ref[:, 0, :] = g
        nm_ref[:, 0, :] = nm
        nv_ref[:, 0, :] = nv
        d_ref[:, 0, :] = -ADAM_LR * ((nm / c1) / (jnp.sqrt(nv / c2) + ADAM_EPS) + ADAM_WD * w_ref[:, 0, :])

    blk = pl.BlockSpec((n, 1, COL_TILE), lambda j: (0, 0, j))
    out = jax.ShapeDtypeStruct((n, 1, D_MODEL), f32)
    return _pcall(
        body, name="adamw_w_in", grid=(D_MODEL // COL_TILE,),
        in_specs=[pl.BlockSpec((n_parts, n, COL_TILE), lambda j: (0, 0, j)), blk, blk, blk],
        out_specs=[blk] * 4, out_shape=[out] * 4,
        compiler_params=_cparams("parallel"),
    )(parts, w3, m3, v3)


def _pad_lanes(vec8, start):
    return jnp.pad(vec8.reshape(1, -1), ((0, 0), (start, LANE - start - vec8.size)))


def kernel(x, norm_in_w, w_in, conv_qkv_w, A_log, dt_bias, gdn_norm_w, conv_w, conv_b, w_out, final_norm_w, loss_target, m_norm_in_w, m_w_in, m_conv_qkv_w, m_A_log, m_dt_bias, m_gdn_norm_w, m_conv_w, m_conv_b, m_w_out, m_final_norm_w, v_norm_in_w, v_w_in, v_conv_qkv_w, v_A_log, v_dt_bias, v_gdn_norm_w, v_conv_w, v_conv_b, v_w_out, v_final_norm_w):
    L = x.shape[1]
    nc = L // CHUNK
    xs = x[0]
    tgt = loss_target[0]
    fnw = final_norm_w.reshape(1, D_MODEL)

    as_rows = lambda a: jnp.transpose(a, (2, 0, 1))
    win_g, cqkv_g, cw_g = _all_gather([_cast_w_in(as_rows(w_in)), conv_qkv_w[0], conv_w[0]], "gather_weights")
    wpad = _relayout_w_in(win_g)
    cqkv = jnp.concatenate([cqkv_g[d] for d in range(N_DEV)], axis=1)
    cw = jnp.concatenate([cw_g[d] for d in range(N_DEV)], axis=1)
    alog_p = _pad_lanes(A_log, HEADS)
    dtb_p = _pad_lanes(dt_bias, HEADS)
    wo_own = w_out[0].astype(bf16)
    wo_started = _spread_start(wo_own, wpad, False, "gather_w_out_start")

    proj, h = _in_proj(xs, norm_in_w + wo_started[4][0:1, 0:1], wpad)
    qkv = _qkv_act(proj, cqkv)
    sc, gr = _scalars(proj, alog_p, dtb_p)
    o, u_all, w_all, vn_all, t_all, sp_all = _gdn_fwd(qkv, sc, gr)
    mix_a = _gdn_gate(o, proj, gdn_norm_w)
    mix_b = _conv_fwd(proj, cw, conv_b)
    wo = _own_slot(_spread_wait(wo_started, mix_b, False, "gather_w_out_wait"), wo_own).reshape(-1, D_MODEL)
    dy, dyb, dmix_a, dmix_b, g_fnw, loss_v = _out_proj_loss(xs, mix_a, mix_b, wo, fnw, tgt)

    g_wout = jnp.concatenate([_tn_matmul(mix_a, dyb, "grad_w_out_a"), _tn_matmul(mix_b, dyb, "grad_w_out_b")], axis=0)
    g_wout = g_wout.reshape(N_DEV, -1, D_MODEL)
    g_wout_own = lax.dynamic_index_in_dim(g_wout, _flat(*_mesh_pos()), 0, keepdims=False)
    gwo_started = _spread_start(g_wout, dyb, True, "exchange_grad_w_out_start")
    do, dzg, g_gnw = _gdn_gate_bwd(o, proj, gdn_norm_w + gwo_started[4][0:1, 0:1], dmix_a)
    d_b, d_c, d_hc, d_zc, g_cw, g_cb = _conv_bwd(proj, cw, conv_b, dmix_b)
    dqkv_n, dsc, dgr = _gdn_bwd(qkv, sc, gr, u_all, w_all, vn_all, t_all, sp_all, do)
    dqkv, g_cqkv = _qkv_bwd(proj, cqkv, dqkv_n)
    dgr_col = jnp.pad(dgr.transpose(0, 2, 1).reshape(L, HEADS), ((0, 0), (HEADS, LANE - 2 * HEADS)))
    dba, g_sc = _scalars_bwd(proj, alog_p, dtb_p, dsc, dgr_col)
    pieces = [dqkv, dzg, dba, d_b, d_c, d_hc, d_zc]
    offs = [OFF_QKV, OFF_ZG, OFF_BA, OFF_B, OFF_C, OFF_HC, OFF_ZC]
    grad_x, g_nw = _input_grad(pieces, offs, wpad, xs, norm_in_w, dy)
    g_parts = [_tn_matmul(p, h, "grad_w_in_%d" % i) for i, p in enumerate(pieces)]
    g_win_blk = _grad_blocks(g_parts)

    (p_win,) = _pair_exchange([g_win_blk], "exchange_grads_pair")
    (r_win,) = _quad_exchange([_pair_sum(g_win_blk, p_win, "pair_sum_w_in")], "exchange_grads_chips")
    r_wout = _own_slot(_spread_wait(gwo_started, r_win, True, "exchange_grad_w_out_wait"), g_wout_own)
    r_cqkv, r_cw = _all_to_all(
        [g_cqkv.reshape(4, N_DEV, -1).transpose(1, 0, 2), g_cw.reshape(3, N_DEV, -1).transpose(1, 0, 2)],
        "exchange_small_sharded_grads")
    upd_win = [jnp.transpose(a, (1, 2, 0)) for a in _adamw_w_in(r_win, as_rows(w_in), as_rows(m_w_in), as_rows(v_w_in))]
    upd_wout = _adamw_reduce(r_wout, w_out[0], m_w_out[0], v_w_out[0], "adamw_w_out")
    upd_cqkv = _adamw_reduce(r_cqkv, conv_qkv_w[0], m_conv_qkv_w[0], v_conv_qkv_w[0], "adamw_conv_qkv_w")
    upd_cw = _adamw_reduce(r_cw, conv_w[0], m_conv_w[0], v_conv_w[0], "adamw_conv_w")

    small_g = _pack_small([g_nw, g_cb, g_fnw, g_gnw, g_sc, loss_v])
    (small_all,) = _all_gather([small_g], "gather_small_grads")
    fvec = lambda a: a.reshape(1, D_MODEL)
    upd_small, loss_sum = _adamw_small(
        small_all,
        [norm_in_w, conv_b, fvec(final_norm_w), gdn_norm_w, A_log, dt_bias],
        [m_norm_in_w, m_conv_b, fvec(m_final_norm_w), m_gdn_norm_w, m_A_log, m_dt_bias],
        [v_norm_in_w, v_conv_b, fvec(v_final_norm_w), v_gdn_norm_w, v_A_log, v_dt_bias])

    outs = [loss_sum[0, 0], grad_x[None]]
    for k in range(4):
        nw_k, cb_k, fw_k, gn_k, al_k, dt_k = upd_small[k]
        outs += [nw_k, upd_win[k], upd_cqkv[k][None], al_k, dt_k, gn_k,
                 upd_cw[k][None], cb_k, upd_wout[k][None], fw_k.reshape(D_MODEL)]
    return tuple(outs)
```

```python
import functools
import math

import jax
import jax.numpy as jnp
from jax import lax
from jax.experimental import pallas as pl
from jax.experimental.pallas import tpu as pltpu

f32 = jnp.float32
bf16 = jnp.bfloat16

N_DEV = 8
D_MODEL = 1024
HEADS = 8
HEAD_DIM = 128
CHUNK = 64
GDN_WIDTH = HEADS * HEAD_DIM
CONV_WIDTH = 1024
PROJ_WIDTH = 8208
SHARD_W = PROJ_WIDTH // N_DEV
EPS = 1e-6

NAT_SMALL_END = 4112
PAD_COLS = 112
OFF_QKV, OFF_ZG, OFF_BA, OFF_B, OFF_C, OFF_HC, OFF_ZC = 0, 3072, 4096, 4224, 5248, 6272, 7296
PROJ_PAD = 8320
LANE = 128

ADAM_LR, ADAM_B1, ADAM_B2, ADAM_EPS, ADAM_WD, ADAM_STEP = 0.001, 0.9, 0.999, 1e-08, 0.01, 10

VMEM_LIMIT = 56 * 1024 * 1024

MESH = pl.DeviceIdType.MESH
ANY = pl.BlockSpec(memory_space=pl.ANY)


def _pcall(body, **kw):
    return pl.pallas_call(body, **kw)


def _cparams(*sem):
    return pltpu.CompilerParams(dimension_semantics=sem if sem else None, vmem_limit_bytes=VMEM_LIMIT)


def _mm(a, b):
    return jnp.dot(a.astype(bf16), b.astype(bf16), preferred_element_type=f32)


def _mm_nt(a, b):
    return lax.dot_general(a.astype(bf16), b.astype(bf16), (((1,), (1,)), ((), ())), preferred_element_type=f32)


def _mm_tn(a, b):
    return lax.dot_general(a.astype(bf16), b.astype(bf16), (((0,), (0,)), ((), ())), preferred_element_type=f32)


def _rows(shape):
    return lax.broadcasted_iota(jnp.int32, shape, 0)


def _lanes(shape):
    return lax.broadcasted_iota(jnp.int32, shape, 1)


def _shift_down(x, s):
    if s == 0:
        return x
    return jnp.where(_rows(x.shape) >= s, pltpu.roll(x, s, 0), 0.0)


def _shift_up(x, s):
    if s == 0:
        return x
    n = x.shape[0]
    return jnp.where(_rows(x.shape) < n - s, pltpu.roll(x, n - s, 0), 0.0)


def _sigmoid(x):
    return jax.nn.sigmoid(x)


def _softplus(x):
    e = jnp.exp(-jnp.abs(x))
    small = e * (1.0 - e * (0.5 - e * (1.0 / 3.0)))
    return jnp.maximum(x, 0.0) + jnp.where(e < 0.01, small, jnp.log(1.0 + e))


def _mesh_pos():
    return lax.axis_index("x"), lax.axis_index("y"), lax.axis_index("c")


def _flat(px, py, pc):
    return 4 * px + 2 * py + pc


def _all_gather(xs, name):
    n = len(xs)

    def body(*refs):
        x_refs, o_refs = refs[:n], refs[n:2 * n]
        send_sems, recv_sems, local_sems = refs[2 * n:]
        x, y, c = _mesh_pos()
        me, sibling = (x, y, c), (x, y, 1 - c)
        chips = [(1 - x, y), (x, 1 - y), (1 - x, 1 - y)]

        def copy(a, k, block, to, src=None):
            dst = o_refs[a].at[_flat(*block)]
            return pltpu.make_async_remote_copy(
                src_ref=dst if src is None else src, dst_ref=dst,
                send_sem=send_sems.at[a, k], recv_sem=recv_sems.at[a, k], device_id=to, device_id_type=MESH)

        mine, first, passed = [], [], []
        for a in range(n):
            cp = pltpu.make_async_copy(x_refs[a], o_refs[a].at[_flat(*me)], local_sems.at[a])
            cp.start()
            mine.append(cp)
            fa = [copy(a, 0, me, sibling, src=x_refs[a])]
            fa += [copy(a, 1 + j, me, (*chip, c), src=x_refs[a]) for j, chip in enumerate(chips)]
            for cp in fa:
                cp.start()
            first += fa
        for a in range(n):
            for j, chip in enumerate(chips):
                copy(a, 1 + j, (*chip, c), me).wait_recv()
                cp = copy(a, 4 + j, (*chip, c), sibling)
                cp.start()
                passed.append(cp)
        for a in range(n):
            copy(a, 0, sibling, me).wait_recv()
            for j, chip in enumerate(chips):
                copy(a, 4 + j, (*chip, 1 - c), me).wait_recv()
        for cp in first + passed:
            cp.wait_send()
        for cp in mine:
            cp.wait()

    outs = _pcall(
        body, name=name,
        out_shape=[jax.ShapeDtypeStruct((N_DEV,) + a.shape, a.dtype) for a in xs],
        in_specs=[ANY] * n, out_specs=[ANY] * n,
        scratch_shapes=[pltpu.SemaphoreType.DMA((n, 7)), pltpu.SemaphoreType.DMA((n, 7)), pltpu.SemaphoreType.DMA((n,))],
    )(*xs)
    return list(outs)


def _all_to_all(gs, name):
    n = len(gs)

    def body(*refs):
        g_refs, o_refs = refs[:n], refs[n:2 * n]
        send_sems, recv_sems, local_sems = refs[2 * n:]
        x, y, c = _mesh_pos()
        me = _flat(x, y, c)
        peers = []
        for k in range(1, N_DEV):
            kx, ky, kc = (k >> 2) & 1, (k >> 1) & 1, k & 1
            px = (1 - x) if kx else x
            py = (1 - y) if ky else y
            pc = (1 - c) if kc else c
            peers.append((px, py, pc))

        def copy(a, k):
            peer = peers[k - 1]
            return pltpu.make_async_remote_copy(
                src_ref=g_refs[a].at[_flat(*peer)], dst_ref=o_refs[a].at[me],
                send_sem=send_sems.at[a, k - 1], recv_sem=recv_sems.at[a, k - 1], device_id=peer, device_id_type=MESH)

        def arrival(a, k):
            peer = peers[k - 1]
            return pltpu.make_async_remote_copy(
                src_ref=g_refs[a].at[me], dst_ref=o_refs[a].at[_flat(*peer)],
                send_sem=send_sems.at[a, k - 1], recv_sem=recv_sems.at[a, k - 1], device_id=peer, device_id_type=MESH)

        mine, sent = [], []
        for a in range(n):
            cp = pltpu.make_async_copy(g_refs[a].at[me], o_refs[a].at[me], local_sems.at[a])
            cp.start()
            mine.append(cp)
            for k in range(1, N_DEV):
                cp = copy(a, k)
                cp.start()
                sent.append(cp)
        for a in range(n):
            for k in range(1, N_DEV):
                arrival(a, k).wait_recv()
        for cp in sent:
            cp.wait_send()
        for cp in mine:
            cp.wait()

    outs = _pcall(
        body, name=name,
        out_shape=[jax.ShapeDtypeStruct(a.shape, a.dtype) for a in gs],
        in_specs=[ANY] * n, out_specs=[ANY] * n,
        scratch_shapes=[pltpu.SemaphoreType.DMA((n, 7)), pltpu.SemaphoreType.DMA((n, 7)), pltpu.SemaphoreType.DMA((n,))],
    )(*gs)
    return list(outs)


def _pair_exchange(gs, name):
    n = len(gs)
    chips = [(0, 0), (0, 1), (1, 0), (1, 1)]

    def body(*refs):
        g_refs, o_refs = refs[:n], refs[n:2 * n]
        send_sems, recv_sems = refs[2 * n:]
        x, y, c = _mesh_pos()
        sibling = (x, y, 1 - c)

        def copy(a, i):
            xp, yp = chips[i]
            return pltpu.make_async_remote_copy(
                src_ref=g_refs[a].at[_flat(xp, yp, 1 - c)], dst_ref=o_refs[a].at[i],
                send_sem=send_sems.at[a, i], recv_sem=recv_sems.at[a, i], device_id=sibling, device_id_type=MESH)

        cps = [copy(a, i) for a in range(n) for i in range(4)]
        for cp in cps:
            cp.start()
        for cp in cps:
            cp.wait()

    outs = _pcall(
        body, name=name,
        out_shape=[jax.ShapeDtypeStruct((4,) + a.shape[1:], a.dtype) for a in gs],
        in_specs=[ANY] * n, out_specs=[ANY] * n,
        scratch_shapes=[pltpu.SemaphoreType.DMA((n, 4)), pltpu.SemaphoreType.DMA((n, 4))],
    )(*gs)
    return list(outs)


def _pair_sum(g, p1, name):
    _, R, C = g.shape
    tr = 256 if R % 256 == 0 else R
    cidx = lax.axis_index("c").astype(jnp.int32).reshape(1)

    def body(c_ref, g_ref, p_ref, o_ref):
        o_ref[...] = (g_ref[...].astype(f32) + p_ref[...].astype(f32)).astype(o_ref.dtype)

    return _pcall(
        body, name=name,
        grid_spec=pltpu.PrefetchScalarGridSpec(
            num_scalar_prefetch=1, grid=(4, R // tr),
            in_specs=[pl.BlockSpec((1, tr, C), lambda i, r, c_ref: (2 * i + c_ref[0], r, 0)),
                      pl.BlockSpec((1, tr, C), lambda i, r, c_ref: (i, r, 0))],
            out_specs=pl.BlockSpec((1, tr, C), lambda i, r, c_ref: (i, r, 0))),
        out_shape=jax.ShapeDtypeStruct((4, R, C), g.dtype),
        compiler_params=_cparams("parallel", "parallel"),
    )(cidx, g, p1)


def _quad_exchange(ss, name):
    n = len(ss)

    def body(*refs):
        s_refs, o_refs = refs[:n], refs[n:2 * n]
        send_sems, recv_sems, local_sems = refs[2 * n:]
        x, y, c = _mesh_pos()
        me = 2 * x + y
        peers = [(1 - x, y), (x, 1 - y), (1 - x, 1 - y)]

        def copy(a, j, arriving):
            px, py = peers[j]
            src_slot, dst_slot = (me, 2 * px + py) if arriving else (2 * px + py, me)
            return pltpu.make_async_remote_copy(
                src_ref=s_refs[a].at[src_slot], dst_ref=o_refs[a].at[dst_slot],
                send_sem=send_sems.at[a, j], recv_sem=recv_sems.at[a, j], device_id=(px, py, c), device_id_type=MESH)

        mine, sent = [], []
        for a in range(n):
            cp = pltpu.make_async_copy(s_refs[a].at[me], o_refs[a].at[me], local_sems.at[a])
            cp.start()
            mine.append(cp)
            for j in range(3):
                cp = copy(a, j, False)
                cp.start()
                sent.append(cp)
        for a in range(n):
            for j in range(3):
                copy(a, j, True).wait_recv()
        for cp in sent:
            cp.wait_send()
        for cp in mine:
            cp.wait()

    outs = _pcall(
        body, name=name,
        out_shape=[jax.ShapeDtypeStruct(a.shape, a.dtype) for a in ss],
        in_specs=[ANY] * n, out_specs=[ANY] * n,
        scratch_shapes=[pltpu.SemaphoreType.DMA((n, 3)), pltpu.SemaphoreType.DMA((n, 3)), pltpu.SemaphoreType.DMA((n,))],
    )(*ss)
    return list(outs)


HBM = pl.BlockSpec(memory_space=pltpu.HBM)
SEM = pl.BlockSpec(memory_space=pltpu.SEMAPHORE)
EFFECT = pltpu.SideEffectType.DATAFLOW_SIDE_EFFECTING


def _peers(x, y, c):
    out = []
    for k in range(1, N_DEV):
        kx, ky, kc = (k >> 2) & 1, (k >> 1) & 1, k & 1
        out.append(((1 - x) if kx else x, (1 - y) if ky else y, (1 - c) if kc else c))
    return out


SPREAD_COPIES = {"gather": N_DEV - 1, "scatter": N_DEV - 1, "chips": 3}


def _spread_copy(src_ref, land_ref, send_sems, recv_sems, k, plan):
    x, y, c = _mesh_pos()
    if plan == "chips":
        px, py = [(1 - x, y), (x, 1 - y), (1 - x, 1 - y)][k]
        peer, src, slot = (px, py, c), src_ref.at[2 * px + py], 2 * x + y
    else:
        peer = _peers(x, y, c)[k]
        src, slot = (src_ref.at[_flat(*peer)] if plan == "scatter" else src_ref), _flat(x, y, c)
    return pltpu.make_async_remote_copy(
        src_ref=src, dst_ref=land_ref.at[slot], send_sem=send_sems.at[k], recv_sem=recv_sems.at[k],
        device_id=peer, device_id_type=MESH)


def _spread_start(src, after, plan, name):
    land_shape = (N_DEV,) + src.shape if plan == "gather" else src.shape
    n_copies = SPREAD_COPIES[plan]

    def body(src_ref, land_ref, after_ref, send_sems, recv_sems, src_thru, land_thru, token):
        for k in range(n_copies):
            _spread_copy(src_ref, land_ref, send_sems, recv_sems, k, plan).start()
        token[...] = jnp.zeros_like(token)

    return _pcall(
        body, name=name,
        out_shape=(pltpu.SemaphoreType.DMA((n_copies,)), pltpu.SemaphoreType.DMA((n_copies,)),
                   pltpu.HBM(src.shape, src.dtype), pltpu.HBM(land_shape, src.dtype), jax.ShapeDtypeStruct((8, LANE), f32)),
        in_specs=(HBM, HBM, ANY), out_specs=(SEM, SEM, HBM, HBM, pl.BlockSpec(memory_space=pltpu.VMEM)),
        input_output_aliases={0: 2, 1: 3},
        compiler_params=pltpu.CompilerParams(has_side_effects=EFFECT),
    )(pltpu.with_memory_space_constraint(src, pltpu.HBM),
      pltpu.with_memory_space_constraint(lax.empty(land_shape, src.dtype), pltpu.HBM), after)


def _spread_wait(started, after, plan, name):
    send_sems, recv_sems, src_thru, land_thru, _ = started

    def body(src_ref, land_ref, send_sems, recv_sems, after_ref, src_dead, got_ref):
        for k in range(SPREAD_COPIES[plan]):
            cp = _spread_copy(src_ref, land_ref, send_sems, recv_sems, k, plan)
            cp.wait_send()
            cp.wait_recv()

    return _pcall(
        body, name=name,
        out_shape=(pltpu.HBM(src_thru.shape, src_thru.dtype), pltpu.HBM(land_thru.shape, land_thru.dtype)),
        in_specs=(HBM, HBM, SEM, SEM, ANY), out_specs=(HBM, HBM), input_output_aliases={0: 0, 1: 1},
        compiler_params=pltpu.CompilerParams(has_side_effects=EFFECT),
    )(src_thru, land_thru, send_sems, recv_sems, after)[1]


def _own_slot(land, block, slot):
    zero = jnp.zeros((), jnp.int32)
    return lax.dynamic_update_slice(land, block[None], (slot.astype(jnp.int32),) + (zero,) * block.ndim)


PIECE_NAT = (0, 3072, 4096, 4112, 5136, 6160, 7184, PROJ_WIDTH)


COL_TILE = 256


def _cast_w_in(w3):
    n = w3.shape[0]

    def body(w_ref, o_ref):
        o_ref[...] = w_ref[:, 0, :].astype(bf16)

    return _pcall(
        body, name="cast_w_in", grid=(D_MODEL // COL_TILE,),
        in_specs=[pl.BlockSpec((n, 1, COL_TILE), lambda j: (0, 0, j))],
        out_specs=pl.BlockSpec((n, COL_TILE), lambda j: (0, j)),
        out_shape=jax.ShapeDtypeStruct((n, D_MODEL), bf16),
        compiler_params=_cparams("parallel"),
    )(w3)


def _relayout_w_in(win_g):
    def body(g_ref, o_ref):
        o_ref[NAT_SMALL_END:NAT_SMALL_END + PAD_COLS, :] = jnp.zeros((PAD_COLS, COL_TILE), o_ref.dtype)
        for d in range(N_DEV):
            n0, n1 = d * SHARD_W, (d + 1) * SHARD_W
            cut = min(max(NAT_SMALL_END - n0, 0), SHARD_W)
            if cut > 0:
                o_ref[n0:n0 + cut, :] = g_ref[d, 0:cut, :]
            if cut < SHARD_W:
                o_ref[n0 + cut + PAD_COLS:n1 + PAD_COLS, :] = g_ref[d, cut:SHARD_W, :]

    return _pcall(
        body, name="relayout_w_in", grid=(D_MODEL // COL_TILE,),
        in_specs=[pl.BlockSpec((N_DEV, SHARD_W, COL_TILE), lambda j: (0, 0, j))],
        out_specs=pl.BlockSpec((PROJ_PAD, COL_TILE), lambda j: (0, j)),
        out_shape=jax.ShapeDtypeStruct((PROJ_PAD, D_MODEL), win_g.dtype),
        compiler_params=_cparams("parallel"),
    )(win_g)


def _grad_blocks(g_parts):
    npc = len(g_parts)

    def body(*refs):
        p_refs, o_ref = refs[:npc], refs[npc]
        for d in range(N_DEV):
            n0, n1 = d * SHARD_W, (d + 1) * SHARD_W
            for i in range(npc):
                lo, hi = max(n0, PIECE_NAT[i]), min(n1, PIECE_NAT[i + 1])
                if lo < hi:
                    o_ref[d, lo - n0:hi - n0, :] = p_refs[i][lo - PIECE_NAT[i]:hi - PIECE_NAT[i], :]

    return _pcall(
        body, name="grad_blocks", grid=(D_MODEL // COL_TILE,),
        in_specs=[pl.BlockSpec((p.shape[0], COL_TILE), lambda j: (0, j)) for p in g_parts],
        out_specs=pl.BlockSpec((N_DEV, SHARD_W, COL_TILE), lambda j: (0, 0, j)),
        out_shape=jax.ShapeDtypeStruct((N_DEV, SHARD_W, D_MODEL), bf16),
        compiler_params=_cparams("parallel"),
    )(*g_parts)


def _in_proj(x, nw, wpad_t):
    L = x.shape[0]
    tn = 640
    nj = wpad_t.shape[0] // tn

    def body(x_ref, nw_ref, w_ref, proj_ref, h_ref):
        @pl.when(pl.program_id(0) == 0)
        def _():
            for r in range(0, L, 256):
                xs = x_ref[r:r + 256, :]
                ms = jnp.mean(xs * xs, axis=-1, keepdims=True)
                h_ref[r:r + 256, :] = ((xs * lax.rsqrt(ms + EPS)) * nw_ref[...]).astype(bf16)
        for r in range(0, L, 512):
            proj_ref[r:r + 512, :] = lax.dot_general(h_ref[r:r + 512, :], w_ref[...], (((1,), (1,)), ((), ())),
                                                     preferred_element_type=f32)

    return _pcall(
        body, name="in_proj", grid=(nj,),
        in_specs=[pl.BlockSpec((L, D_MODEL), lambda j: (0, 0)), pl.BlockSpec((1, D_MODEL), lambda j: (0, 0)),
                  pl.BlockSpec((tn, D_MODEL), lambda j: (j, 0))],
        out_specs=[pl.BlockSpec((L, tn), lambda j: (0, j)), pl.BlockSpec((L, D_MODEL), lambda j: (0, 0))],
        out_shape=[jax.ShapeDtypeStruct((L, wpad_t.shape[0]), f32), jax.ShapeDtypeStruct((L, D_MODEL), bf16)],
        compiler_params=_cparams("arbitrary"),
    )(x, nw, wpad_t)


def _conv4(x, cw_ref):
    return (cw_ref[3:4, :] * x + cw_ref[2:3, :] * _shift_down(x, 1) + cw_ref[1:2, :] * _shift_down(x, 2)
            + cw_ref[0:1, :] * _shift_down(x, 3))


def _qkv_act(proj, cw):
    L = proj.shape[0]

    def body(x_ref, cw_ref, o_ref):
        j = pl.program_id(0)
        c = _conv4(x_ref[...], cw_ref)
        a = c * _sigmoid(c)
        rn = lax.rsqrt(jnp.sum(a * a, axis=1, keepdims=True) + EPS)
        scale = jnp.where(j < HEADS, HEAD_DIM ** -0.5, 1.0).astype(f32)
        o_ref[...] = jnp.where(j < 2 * HEADS, (a * rn) * scale, a)

    return _pcall(
        body, name="qkv_act", grid=(3 * HEADS,),
        in_specs=[pl.BlockSpec((L, LANE), lambda j: (0, j)), pl.BlockSpec((4, LANE), lambda j: (0, j))],
        out_specs=pl.BlockSpec((L, LANE), lambda j: (0, j)),
        out_shape=jax.ShapeDtypeStruct((L, 3 * GDN_WIDTH), f32),
        compiler_params=_cparams("parallel"),
    )(proj, cw)


def _scalars(proj, alog_p, dtb_p):
    L = proj.shape[0]
    nc = L // CHUNK

    def body(x_ref, al_ref, dt_ref, sc_ref, gr_ref):
        x = x_ref[...]
        lane = _lanes(x.shape)
        beta = _sigmoid(x)
        g = -jnp.exp(al_ref[...]) * _softplus(x + dt_ref[...])
        gc = jnp.where((lane >= HEADS) & (lane < 2 * HEADS), g, 0.0)
        rc = _rows(x.shape) & (CHUNK - 1)
        for s in (1, 2, 4, 8, 16, 32):
            gc = gc + jnp.where(rc >= s, pltpu.roll(gc, s, 0), 0.0)
        sc_ref[...] = jnp.where(lane < HEADS, beta, gc)
        sel = (_lanes((HEADS, LANE)) == _rows((HEADS, LANE)) + HEADS).astype(f32)
        for c in range(nc):
            gr_ref[c] = lax.dot_general(sel, sc_ref[c * CHUNK:(c + 1) * CHUNK, :], (((1,), (1,)), ((), ())),
                                        preferred_element_type=f32, precision=lax.Precision.HIGHEST)

    return _pcall(
        body, name="scalars", grid=(1,),
        in_specs=[pl.BlockSpec((L, LANE), lambda i: (0, OFF_BA // LANE)), pl.BlockSpec((1, LANE), lambda i: (0, 0)),
                  pl.BlockSpec((1, LANE), lambda i: (0, 0))],
        out_specs=[pl.BlockSpec((L, LANE), lambda i: (0, 0)), pl.BlockSpec((nc, HEADS, CHUNK), lambda i: (0, 0, 0))],
        out_shape=[jax.ShapeDtypeStruct((L, LANE), f32), jax.ShapeDtypeStruct((nc, HEADS, CHUNK), f32)],
        compiler_params=_cparams("arbitrary"),
    )(proj, alog_p, dtb_p)


def _head_scalars(sc, gr_ref, h):
    lane = _lanes(sc.shape)
    beta = jnp.sum(jnp.where(lane == h, sc, 0.0), axis=1, keepdims=True)
    gcc = jnp.sum(jnp.where(lane == HEADS + h, sc, 0.0), axis=1, keepdims=True)
    gcr = gr_ref[0, h:h + 1, :]
    gl = jnp.sum(jnp.where(_lanes(gcr.shape) == CHUNK - 1, gcr, 0.0), axis=1, keepdims=True)
    ii, jj = _rows((CHUNK, CHUNK)), _lanes((CHUNK, CHUNK))
    dmat = jnp.where(ii >= jj, jnp.exp(jnp.minimum(gcc - gcr, 0.0)), 0.0)
    dmat_t = jnp.where(jj >= ii, jnp.exp(jnp.minimum(gcr - gcc, 0.0)), 0.0)
    return beta, gcc, gl, dmat, dmat_t, ii, jj


def _gdn_fwd(qkv, sc, gr):
    L = qkv.shape[0]
    nc = L // CHUNK
    W = GDN_WIDTH

    def body(qkv_ref, sc_ref, gr_ref, o_ref, u_ref, w_ref, vn_ref, t_ref, sp_ref, s_scr):
        @pl.when(pl.program_id(0) == 0)
        def _():
            s_scr[...] = jnp.zeros_like(s_scr)
        sc_v = sc_ref[...]
        HS = range(HEADS)
        cs = [slice(h * HEAD_DIM, (h + 1) * HEAD_DIM) for h in HS]
        q = [qkv_ref[:, h * HEAD_DIM:(h + 1) * HEAD_DIM] for h in HS]
        k = [qkv_ref[:, W + h * HEAD_DIM:W + (h + 1) * HEAD_DIM] for h in HS]
        v = [qkv_ref[:, 2 * W + h * HEAD_DIM:2 * W + (h + 1) * HEAD_DIM] for h in HS]
        hsc = [_head_scalars(sc_v, gr_ref, h) for h in HS]
        beta, gcc, gl, dmat = ([x[i] for x in hsc] for i in range(4))
        ii, jj = hsc[0][5], hsc[0][6]
        eg = [jnp.exp(gcc[h]) for h in HS]
        kb = [k[h] * beta[h] for h in HS]
        kk = [_mm_nt(kb[h], k[h]) for h in HS]
        qk = [_mm_nt(q[h], k[h]) for h in HS]
        xp = [-jnp.where(ii > jj, kk[h] * dmat[h], 0.0) for h in HS]
        t = xp
        for _ in range(5):
            xp = [_mm(xp[h], xp[h]) for h in HS]
            tx = [_mm(t[h], xp[h]) for h in HS]
            t = [t[h] + xp[h] + tx[h] for h in HS]
        vb = [v[h] * beta[h] for h in HS]
        kbg = [kb[h] * eg[h] for h in HS]
        uw = [_mm(t[h], jnp.concatenate([vb[h], kbg[h]], axis=1)) for h in HS]
        u = [vb[h] + uw[h][:, :HEAD_DIM] for h in HS]
        w = [kbg[h] + uw[h][:, HEAD_DIM:] for h in HS]
        s = [s_scr[h] for h in HS]
        ws = [_mm(jnp.concatenate([w[h], q[h] * eg[h]], axis=0), s[h]) for h in HS]
        vn = [u[h] - ws[h][:CHUNK] for h in HS]
        p = [jnp.where(ii >= jj, qk[h] * dmat[h], 0.0) for h in HS]
        pv = [_mm(p[h], vn[h]) for h in HS]
        kv = [_mm_tn(k[h] * jnp.exp(gl[h] - gcc[h]), vn[h]) for h in HS]
        for h in HS:
            sp_ref[0, cs[h], :] = s[h]
            o_ref[:, cs[h]] = ws[h][CHUNK:] + pv[h]
            s_scr[h] = jnp.exp(gl[h]) * s[h] + kv[h]
            u_ref[:, cs[h]] = u[h]
            w_ref[:, cs[h]] = w[h]
            vn_ref[:, cs[h]] = vn[h]
            t_ref[0, h] = t[h]

    row = lambda c: (c, 0)
    act = jax.ShapeDtypeStruct((L, W), f32)
    return _pcall(
        body, name="gdn_fwd", grid=(nc,),
        in_specs=[pl.BlockSpec((CHUNK, 3 * W), row), pl.BlockSpec((CHUNK, LANE), row),
                  pl.BlockSpec((1, HEADS, CHUNK), lambda c: (c, 0, 0))],
        out_specs=[pl.BlockSpec((CHUNK, W), row)] * 4 + [
            pl.BlockSpec((1, HEADS, CHUNK, CHUNK), lambda c: (c, 0, 0, 0)),
            pl.BlockSpec((1, W, HEAD_DIM), lambda c: (c, 0, 0))],
        out_shape=[act, act, act, act, jax.ShapeDtypeStruct((nc, HEADS, CHUNK, CHUNK), f32),
                   jax.ShapeDtypeStruct((nc, W, HEAD_DIM), f32)],
        scratch_shapes=[pltpu.VMEM((HEADS, HEAD_DIM, HEAD_DIM), f32)],
        compiler_params=_cparams("arbitrary"),
    )(qkv, sc, gr)


def _gdn_gate(o, proj, gnw):
    L = o.shape[0]

    def body(o_ref, z_ref, w_ref, m_ref):
        ov, z = o_ref[...], z_ref[...]
        rms = lax.rsqrt(jnp.mean(ov * ov, axis=-1, keepdims=True) + EPS)
        m_ref[...] = (((ov * rms) * w_ref[...]) * (z * _sigmoid(z))).astype(bf16)

    return _pcall(
        body, name="gdn_gate", grid=(HEADS,),
        in_specs=[pl.BlockSpec((L, LANE), lambda j: (0, j)), pl.BlockSpec((L, LANE), lambda j: (0, OFF_ZG // LANE + j)),
                  pl.BlockSpec((1, LANE), lambda j: (0, 0))],
        out_specs=pl.BlockSpec((L, LANE), lambda j: (0, j)),
        out_shape=jax.ShapeDtypeStruct((L, GDN_WIDTH), bf16),
        compiler_params=_cparams("parallel"),
    )(o, proj, gnw)


def _conv3(u, cw_ref):
    return cw_ref[2:3, :] * u + cw_ref[1:2, :] * _shift_down(u, 1) + cw_ref[0:1, :] * _shift_down(u, 2)


def _conv_specs(L):
    blk = lambda off: pl.BlockSpec((L, LANE), lambda j, off=off: (0, off // LANE + j))
    return [blk(OFF_B), blk(OFF_C), blk(OFF_HC), blk(OFF_ZC),
            pl.BlockSpec((3, LANE), lambda j: (0, j)), pl.BlockSpec((1, LANE), lambda j: (0, j))]


def _conv_fwd(proj, cw, cb):
    L = proj.shape[0]

    def body(b_ref, c_ref, h_ref, z_ref, cw_ref, cb_ref, m_ref):
        z = z_ref[...]
        cv = _conv3(c_ref[...] * h_ref[...], cw_ref) + cb_ref[...]
        m_ref[...] = ((b_ref[...] * cv) * (z * _sigmoid(z))).astype(bf16)

    return _pcall(
        body, name="conv_fwd", grid=(CONV_WIDTH // LANE,),
        in_specs=_conv_specs(L),
        out_specs=pl.BlockSpec((L, LANE), lambda j: (0, j)),
        out_shape=jax.ShapeDtypeStruct((L, CONV_WIDTH), bf16),
        compiler_params=_cparams("parallel"),
    )(proj, proj, proj, proj, cw, cb)


def _out_proj_loss(x, mix_a, mix_b, wo, fw, tgt):
    L = x.shape[0]
    tm = min(256, L)

    def body(x_ref, ma_ref, mb_ref, wo_ref, fw_ref, t_ref, dy_ref, dyb_ref, dma_ref, dmb_ref, gfw_ref, loss_ref):
        @pl.when(pl.program_id(0) == 0)
        def _():
            gfw_ref[...] = jnp.zeros_like(gfw_ref)
            loss_ref[...] = jnp.zeros_like(loss_ref)
        y = x_ref[...] + jnp.dot(ma_ref[...], wo_ref[:GDN_WIDTH, :], preferred_element_type=f32) \
            + jnp.dot(mb_ref[...], wo_ref[GDN_WIDTH:, :], preferred_element_type=f32)
        r = lax.rsqrt(jnp.mean(y * y, axis=-1, keepdims=True) + EPS)
        yh = y * r
        fwv = fw_ref[...]
        diff = yh * fwv - t_ref[...]
        loss_ref[...] += jnp.sum(jnp.sum(diff * diff, axis=-1, keepdims=True), axis=0, keepdims=True) * (0.5 / D_MODEL)
        dout = diff * (1.0 / D_MODEL)
        gfw_ref[...] += jnp.sum(dout * yh, axis=0, keepdims=True)
        dyh = dout * fwv
        dy = r * (dyh - yh * jnp.mean(dyh * yh, axis=-1, keepdims=True))
        dy_ref[...] = dy
        dyb = dy.astype(bf16)
        dyb_ref[...] = dyb
        dma_ref[...] = lax.dot_general(dyb, wo_ref[:GDN_WIDTH, :], (((1,), (1,)), ((), ())), preferred_element_type=f32)
        dmb_ref[...] = lax.dot_general(dyb, wo_ref[GDN_WIDTH:, :], (((1,), (1,)), ((), ())), preferred_element_type=f32)

    row = lambda i: (i, 0)
    fix = lambda i: (0, 0)
    act = jax.ShapeDtypeStruct((L, D_MODEL), f32)
    return _pcall(
        body, name="out_proj_loss", grid=(L // tm,),
        in_specs=[pl.BlockSpec((tm, D_MODEL), row), pl.BlockSpec((tm, GDN_WIDTH), row), pl.BlockSpec((tm, CONV_WIDTH), row),
                  pl.BlockSpec((GDN_WIDTH + CONV_WIDTH, D_MODEL), fix), pl.BlockSpec((1, D_MODEL), fix),
                  pl.BlockSpec((tm, D_MODEL), row)],
        out_specs=[pl.BlockSpec((tm, D_MODEL), row), pl.BlockSpec((tm, D_MODEL), row), pl.BlockSpec((tm, GDN_WIDTH), row),
                   pl.BlockSpec((tm, CONV_WIDTH), row), pl.BlockSpec((1, D_MODEL), fix), pl.BlockSpec((1, LANE), fix)],
        out_shape=[act, jax.ShapeDtypeStruct((L, D_MODEL), bf16), act, act,
                   jax.ShapeDtypeStruct((1, D_MODEL), f32), jax.ShapeDtypeStruct((1, LANE), f32)],
        compiler_params=_cparams("arbitrary"),
    )(x, mix_a, mix_b, wo, fw, tgt)


def _tn_matmul(a, b, name):
    L, M = a.shape
    N = b.shape[1]
    tm = 512 if M % 512 == 0 else M

    def body(a_ref, b_ref, o_ref):
        o_ref[...] = lax.dot_general(a_ref[...], b_ref[...], (((0,), (0,)), ((), ())),
                                     preferred_element_type=f32).astype(o_ref.dtype)

    return _pcall(
        body, name=name, grid=(M // tm,),
        in_specs=[pl.BlockSpec((L, tm), lambda i: (0, i)), pl.BlockSpec((L, N), lambda i: (0, 0))],
        out_specs=pl.BlockSpec((tm, N), lambda i: (i, 0)),
        out_shape=jax.ShapeDtypeStruct((M, N), bf16),
        compiler_params=_cparams("parallel"),
    )(a, b)


def _gdn_gate_bwd(o, proj, gnw, dmix_a):
    L = o.shape[0]

    def body(o_ref, z_ref, w_ref, dm_ref, do_ref, dz_ref, gw_ref):
        @pl.when(pl.program_id(0) == 0)
        def _():
            gw_ref[...] = jnp.zeros_like(gw_ref)
        ov, z, dm, wv = o_ref[...], z_ref[...], dm_ref[...], w_ref[...]
        rms = lax.rsqrt(jnp.mean(ov * ov, axis=-1, keepdims=True) + EPS)
        xh = ov * rms
        sg = _sigmoid(z)
        d_on = dm * (z * sg)
        dz_ref[...] = (dm * (xh * wv) * (sg * (1.0 + z * (1.0 - sg)))).astype(bf16)
        gw_ref[...] += jnp.sum(d_on * xh, axis=0, keepdims=True)
        dxh = d_on * wv
        do_ref[...] = rms * (dxh - xh * jnp.mean(dxh * xh, axis=-1, keepdims=True))

    return _pcall(
        body, name="gdn_gate_bwd", grid=(HEADS,),
        in_specs=[pl.BlockSpec((L, LANE), lambda j: (0, j)), pl.BlockSpec((L, LANE), lambda j: (0, OFF_ZG // LANE + j)),
                  pl.BlockSpec((1, LANE), lambda j: (0, 0)), pl.BlockSpec((L, LANE), lambda j: (0, j))],
        out_specs=[pl.BlockSpec((L, LANE), lambda j: (0, j)), pl.BlockSpec((L, LANE), lambda j: (0, j)),
                   pl.BlockSpec((1, LANE), lambda j: (0, 0))],
        out_shape=[jax.ShapeDtypeStruct((L, GDN_WIDTH), f32), jax.ShapeDtypeStruct((L, GDN_WIDTH), bf16),
                   jax.ShapeDtypeStruct((1, LANE), f32)],
        compiler_params=_cparams("arbitrary"),
    )(o, proj, gnw, dmix_a)


def _conv_bwd(proj, cw, cb, dmix_b):
    L = proj.shape[0]

    def body(b_ref, c_ref, h_ref, z_ref, cw_ref, cb_ref, dm_ref, db_ref, dc_ref, dh_ref, dz_ref, gcw_ref, gcb_ref):
        bv, cv_, hv, z, dm = b_ref[...], c_ref[...], h_ref[...], z_ref[...], dm_ref[...]
        u = cv_ * hv
        cv = _conv3(u, cw_ref) + cb_ref[...]
        sg = _sigmoid(z)
        sz = z * sg
        db_ref[...] = (dm * cv * sz).astype(bf16)
        dz_ref[...] = (dm * (bv * cv) * (sg * (1.0 + z * (1.0 - sg)))).astype(bf16)
        dcv = dm * bv * sz
        gcb_ref[...] = jnp.sum(dcv, axis=0, keepdims=True)
        gcw_ref[2:3, :] = jnp.sum(dcv * u, axis=0, keepdims=True)
        gcw_ref[1:2, :] = jnp.sum(dcv * _shift_down(u, 1), axis=0, keepdims=True)
        gcw_ref[0:1, :] = jnp.sum(dcv * _shift_down(u, 2), axis=0, keepdims=True)
        du = cw_ref[2:3, :] * dcv + cw_ref[1:2, :] * _shift_up(dcv, 1) + cw_ref[0:1, :] * _shift_up(dcv, 2)
        dc_ref[...] = (du * hv).astype(bf16)
        dh_ref[...] = (du * cv_).astype(bf16)

    col = pl.BlockSpec((L, LANE), lambda j: (0, j))
    act = jax.ShapeDtypeStruct((L, CONV_WIDTH), bf16)
    return _pcall(
        body, name="conv_bwd", grid=(CONV_WIDTH // LANE,),
        in_specs=_conv_specs(L) + [col],
        out_specs=[col, col, col, col, pl.BlockSpec((3, LANE), lambda j: (0, j)), pl.BlockSpec((1, LANE), lambda j: (0, j))],
        out_shape=[act, act, act, act, jax.ShapeDtypeStruct((3, CONV_WIDTH), f32), jax.ShapeDtypeStruct((1, CONV_WIDTH), f32)],
        compiler_params=_cparams("parallel"),
    )(proj, proj, proj, proj, cw, cb, dmix_b)


def _gdn_bwd(qkv, sc, gr, u_all, w_all, vn_all, t_all, sp_all, do_all):
    L = qkv.shape[0]
    nc = L // CHUNK
    W = GDN_WIDTH

    def body(qkv_ref, sc_ref, gr_ref, u_ref, w_ref, vn_ref, t_ref, sp_ref, do_ref, dqkv_ref, dsc_ref, dgr_ref, ds_scr):
        @pl.when(pl.program_id(0) == 0)
        def _():
            ds_scr[...] = jnp.zeros_like(ds_scr)
        sc_v = sc_ref[...]
        lane = _lanes(sc_v.shape)
        dsc = jnp.zeros(sc_v.shape, f32)
        HS = range(HEADS)
        cs = [slice(h * HEAD_DIM, (h + 1) * HEAD_DIM) for h in HS]
        q = [qkv_ref[:, h * HEAD_DIM:(h + 1) * HEAD_DIM] for h in HS]
        k = [qkv_ref[:, W + h * HEAD_DIM:W + (h + 1) * HEAD_DIM] for h in HS]
        v = [qkv_ref[:, 2 * W + h * HEAD_DIM:2 * W + (h + 1) * HEAD_DIM] for h in HS]
        hsc = [_head_scalars(sc_v, gr_ref, h) for h in HS]
        beta, gcc, gl, dmat, dmat_t = ([x[i] for x in hsc] for i in range(5))
        ii, jj = hsc[0][5], hsc[0][6]
        eg = [jnp.exp(gcc[h]) for h in HS]
        ekl = [jnp.exp(gl[h] - gcc[h]) for h in HS]
        egl = [jnp.exp(gl[h]) for h in HS]
        kb = [k[h] * beta[h] for h in HS]
        ks = [k[h] * ekl[h] for h in HS]
        do = [do_ref[:, cs[h]] for h in HS]
        vn = [vn_ref[:, cs[h]] for h in HS]
        s = [sp_ref[0, cs[h], :] for h in HS]
        dsn = [ds_scr[h] for h in HS]

        kq = [_mm_nt(k[h], q[h]) for h in HS]
        ksd = [_mm(ks[h], dsn[h]) for h in HS]
        p_t = [jnp.where(jj >= ii, kq[h] * dmat_t[h], 0.0) for h in HS]
        ptd = [_mm(p_t[h], do[h]) for h in HS]
        dvn = [ptd[h] + ksd[h] for h in HS]
        dodv = [jnp.concatenate([do[h], dvn[h]], axis=0) for h in HS]
        x1 = [_mm_nt(dodv[h], s[h]) for h in HS]
        dks = [_mm_nt(vn[h], dsn[h]) for h in HS]
        dov = [_mm_nt(do[h], vn[h]) for h in HS]
        vdo = [_mm_nt(vn[h], do[h]) for h in HS]
        kk = [_mm_nt(kb[h], k[h]) for h in HS]
        qk = [_mm_nt(q[h], k[h]) for h in HS]
        w = [w_ref[:, cs[h]] for h in HS]
        qd = [q[h] * eg[h] for h in HS]
        dsq = [_mm_tn(jnp.concatenate([qd[h], -w[h]], axis=0), dodv[h]) for h in HS]
        dgl = [egl[h] * jnp.sum(jnp.sum(s[h] * dsn[h], axis=1, keepdims=True), axis=0, keepdims=True) for h in HS]
        for h in HS:
            ds_scr[h] = egl[h] * dsn[h] + dsq[h]
        dqd = [x1[h][:CHUNK] for h in HS]
        duw = [jnp.concatenate([dvn[h], -x1[h][CHUNK:]], axis=1) for h in HS]
        tdu = [_mm_tn(t_ref[0, h], duw[h]) for h in HS]
        dvk = [duw[h] + tdu[h] for h in HS]
        uw = [jnp.concatenate([u_ref[:, cs[h]], w[h]], axis=1) for h in HS]
        da = [-jnp.where(ii > jj, _mm_nt(dvk[h], uw[h]), 0.0) for h in HS]
        da_t = [-jnp.where(jj > ii, _mm_nt(uw[h], dvk[h]), 0.0) for h in HS]
        dp = [jnp.where(ii >= jj, dov[h], 0.0) for h in HS]
        dp_t = [jnp.where(jj >= ii, vdo[h], 0.0) for h in HS]
        r1 = [_mm(jnp.concatenate([da[h] * dmat[h], dp[h] * dmat[h]], axis=0), k[h]) for h in HS]
        dk1 = [_mm(jnp.concatenate([da_t[h] * dmat_t[h], dp_t[h] * dmat_t[h]], axis=1),
                   jnp.concatenate([kb[h], q[h]], axis=0)) for h in HS]
        dsc = jnp.zeros(sc_v.shape, f32)
        for h in HS:
            a = jnp.where(ii > jj, kk[h] * dmat[h], 0.0)
            p = jnp.where(ii >= jj, qk[h] * dmat[h], 0.0)
            gmat = da[h] * a + dp[h] * p
            dvb, dkbg = dvk[h][:, :HEAD_DIM], dvk[h][:, HEAD_DIM:]
            kbg = kb[h] * eg[h]
            dkb = r1[h][:CHUNK] + dkbg * eg[h]
            dq = r1[h][CHUNK:] + dqd[h] * eg[h]
            dk = dk1[h] + dks[h] * ekl[h] + dkb * beta[h]
            dbeta = jnp.sum(dkb * k[h] + dvb * v[h], axis=1, keepdims=True)
            ksum = jnp.sum(dks[h] * ks[h], axis=1, keepdims=True)
            dgl_tot = dgl[h] + jnp.sum(ksum, axis=0, keepdims=True)
            dgc = jnp.sum(gmat, axis=1, keepdims=True) + jnp.sum(dqd[h] * qd[h] + dkbg * kbg, axis=1, keepdims=True) - ksum
            dgc = dgc + jnp.where(_rows(dgc.shape) == CHUNK - 1, dgl_tot, 0.0)
            dqkv_ref[:, h * HEAD_DIM:(h + 1) * HEAD_DIM] = dq
            dqkv_ref[:, W + h * HEAD_DIM:W + (h + 1) * HEAD_DIM] = dk
            dqkv_ref[:, 2 * W + h * HEAD_DIM:2 * W + (h + 1) * HEAD_DIM] = dvb * beta[h]
            dsc = jnp.where(lane == h, dbeta, jnp.where(lane == HEADS + h, dgc, dsc))
            dgr_ref[0, h:h + 1, :] = jnp.sum(gmat, axis=0, keepdims=True)
        dsc_ref[...] = dsc

    row = lambda c: (nc - 1 - c, 0)
    return _pcall(
        body, name="gdn_bwd", grid=(nc,),
        in_specs=[pl.BlockSpec((CHUNK, 3 * W), row), pl.BlockSpec((CHUNK, LANE), row),
                  pl.BlockSpec((1, HEADS, CHUNK), lambda c: (nc - 1 - c, 0, 0)),
                  pl.BlockSpec((CHUNK, W), row), pl.BlockSpec((CHUNK, W), row), pl.BlockSpec((CHUNK, W), row),
                  pl.BlockSpec((1, HEADS, CHUNK, CHUNK), lambda c: (nc - 1 - c, 0, 0, 0)),
                  pl.BlockSpec((1, W, HEAD_DIM), lambda c: (nc - 1 - c, 0, 0)), pl.BlockSpec((CHUNK, W), row)],
        out_specs=[pl.BlockSpec((CHUNK, 3 * W), row), pl.BlockSpec((CHUNK, LANE), row),
                   pl.BlockSpec((1, HEADS, CHUNK), lambda c: (nc - 1 - c, 0, 0))],
        out_shape=[jax.ShapeDtypeStruct((L, 3 * W), f32), jax.ShapeDtypeStruct((L, LANE), f32),
                   jax.ShapeDtypeStruct((nc, HEADS, CHUNK), f32)],
        scratch_shapes=[pltpu.VMEM((HEADS, HEAD_DIM, HEAD_DIM), f32)],
        compiler_params=_cparams("arbitrary"),
    )(qkv, sc, gr, u_all, w_all, vn_all, t_all, sp_all, do_all)


def _qkv_bwd(proj, cw, dn):
    L = proj.shape[0]

    def body(x_ref, cw_ref, dn_ref, dx_ref, gcw_ref):
        j = pl.program_id(0)
        x, dn_v = x_ref[...], dn_ref[...]
        c = _conv4(x, cw_ref)
        sg = _sigmoid(c)
        a = c * sg
        rn = lax.rsqrt(jnp.sum(a * a, axis=1, keepdims=True) + EPS)
        scale = jnp.where(j < HEADS, HEAD_DIM ** -0.5, 1.0).astype(f32)
        da_n = (scale * rn) * (dn_v - a * ((rn * rn) * jnp.sum(dn_v * a, axis=1, keepdims=True)))
        da = jnp.where(j < 2 * HEADS, da_n, dn_v)
        dc = da * (sg * (1.0 + c * (1.0 - sg)))
        gcw_ref[3:4, :] = jnp.sum(dc * x, axis=0, keepdims=True)
        gcw_ref[2:3, :] = jnp.sum(dc * _shift_down(x, 1), axis=0, keepdims=True)
        gcw_ref[1:2, :] = jnp.sum(dc * _shift_down(x, 2), axis=0, keepdims=True)
        gcw_ref[0:1, :] = jnp.sum(dc * _shift_down(x, 3), axis=0, keepdims=True)
        dx = (cw_ref[3:4, :] * dc + cw_ref[2:3, :] * _shift_up(dc, 1) + cw_ref[1:2, :] * _shift_up(dc, 2)
              + cw_ref[0:1, :] * _shift_up(dc, 3))
        dx_ref[...] = dx.astype(bf16)

    col = pl.BlockSpec((L, LANE), lambda j: (0, j))
    wspec = pl.BlockSpec((4, LANE), lambda j: (0, j))
    return _pcall(
        body, name="qkv_bwd", grid=(3 * HEADS,),
        in_specs=[col, wspec, col], out_specs=[col, wspec],
        out_shape=[jax.ShapeDtypeStruct((L, 3 * GDN_WIDTH), bf16), jax.ShapeDtypeStruct((4, 3 * GDN_WIDTH), f32)],
        compiler_params=_cparams("parallel"),
    )(proj, cw, dn)


def _scalars_bwd(proj, alog_p, dtb_p, dsc, dgr_col):
    L = proj.shape[0]

    def body(x_ref, al_ref, dt_ref, dsc_ref, dgr_ref, dba_ref, gs_ref):
        x, dsc_v = x_ref[...], dsc_ref[...]
        lane = _lanes(x.shape)
        dec = (lane >= HEADS) & (lane < 2 * HEADS)
        dg = jnp.where(dec, dsc_v - dgr_ref[...], 0.0)
        rc = _rows(x.shape) & (CHUNK - 1)
        for s in (1, 2, 4, 8, 16, 32):
            dg = dg + jnp.where(rc + s < CHUNK, pltpu.roll(dg, L - s, 0), 0.0)
        xa = x + dt_ref[...]
        ea = jnp.exp(al_ref[...])
        g = -ea * _softplus(xa)
        da = dg * (-ea) * _sigmoid(xa)
        beta = _sigmoid(x)
        db = dsc_v * beta * (1.0 - beta)
        dba_ref[...] = jnp.where(lane < HEADS, db, jnp.where(dec, da, 0.0)).astype(bf16)
        g_al = jnp.sum(jnp.where(dec, dg * g, 0.0), axis=0, keepdims=True)
        g_dt = jnp.sum(jnp.where(dec, da, 0.0), axis=0, keepdims=True)
        row8 = _rows(gs_ref.shape)
        gs = jnp.where(row8 == 0, g_al, jnp.where(row8 == 1, g_dt, 0.0))
        gs_ref[...] = pltpu.roll(gs, LANE - HEADS, 1)

    full = pl.BlockSpec((L, LANE), lambda i: (0, 0))
    vec = pl.BlockSpec((1, LANE), lambda i: (0, 0))
    return _pcall(
        body, name="scalars_bwd", grid=(1,),
        in_specs=[pl.BlockSpec((L, LANE), lambda i: (0, OFF_BA // LANE)), vec, vec, full, full],
        out_specs=[full, pl.BlockSpec((8, LANE), lambda i: (0, 0))],
        out_shape=[jax.ShapeDtypeStruct((L, LANE), bf16), jax.ShapeDtypeStruct((8, LANE), f32)],
        compiler_params=_cparams("arbitrary"),
    )(proj, alog_p, dtb_p, dsc, dgr_col)


def _input_grad(pieces, offs, wpad, x, nw, dy):
    L = x.shape[0]
    tm = min(512, L)
    npc = len(pieces)

    def body(*refs):
        p_refs = refs[:npc]
        w_hbm, x_ref, nw_ref, dy_ref, gx_ref, gnw_ref, w_vmem, sem = refs[npc:]

        @pl.when(pl.program_id(0) == 0)
        def _():
            cp = pltpu.make_async_copy(w_hbm, w_vmem, sem)
            cp.start()
            cp.wait()
            gnw_ref[...] = jnp.zeros_like(gnw_ref)
        dh = None
        for p_ref, off in zip(p_refs, offs):
            wd = p_ref.shape[1]
            part = jnp.dot(p_ref[...], w_vmem[off:off + wd, :], preferred_element_type=f32)
            dh = part if dh is None else dh + part
        xv, nwv = x_ref[...], nw_ref[...]
        r = lax.rsqrt(jnp.mean(xv * xv, axis=-1, keepdims=True) + EPS)
        xh = xv * r
        gnw_ref[...] += jnp.sum(dh * xh, axis=0, keepdims=True)
        dxh = dh * nwv
        gx_ref[...] = dy_ref[...] + r * (dxh - xh * jnp.mean(dxh * xh, axis=-1, keepdims=True))

    row = lambda i: (i, 0)
    fix = lambda i: (0, 0)
    return _pcall(
        body, name="input_grad", grid=(L // tm,),
        in_specs=[pl.BlockSpec((tm, p.shape[1]), row) for p in pieces] + [
            ANY, pl.BlockSpec((tm, D_MODEL), row), pl.BlockSpec((1, D_MODEL), fix), pl.BlockSpec((tm, D_MODEL), row)],
        out_specs=[pl.BlockSpec((tm, D_MODEL), row), pl.BlockSpec((1, D_MODEL), fix)],
        out_shape=[jax.ShapeDtypeStruct((L, D_MODEL), f32), jax.ShapeDtypeStruct((1, D_MODEL), f32)],
        scratch_shapes=[pltpu.VMEM(wpad.shape, bf16), pltpu.SemaphoreType.DMA(())],
        compiler_params=_cparams("arbitrary"),
    )(*pieces, wpad, x, nw, dy)


def _adamw_reduce(parts, w, m, v, name):
    R, C = w.shape
    n_parts = parts.shape[0]
    tr = 128 if R % 128 == 0 else R
    c1 = 1.0 - ADAM_B1 ** ADAM_STEP
    c2 = 1.0 - ADAM_B2 ** ADAM_STEP

    def body(p_ref, w_ref, m_ref, v_ref, g_ref, d_ref, nm_ref, nv_ref):
        g = p_ref[0].astype(f32)
        for s in range(1, n_parts):
            g = g + p_ref[s].astype(f32)
        nm = ADAM_B1 * m_ref[...] + (1.0 - ADAM_B1) * g
        nv = ADAM_B2 * v_ref[...] + (1.0 - ADAM_B2) * (g * g)
        g_ref[...] = g
        nm_ref[...] = nm
        nv_ref[...] = nv
        d_ref[...] = -ADAM_LR * ((nm / c1) / (jnp.sqrt(nv / c2) + ADAM_EPS) + ADAM_WD * w_ref[...])

    blk = pl.BlockSpec((tr, C), lambda i: (i, 0))
    out = jax.ShapeDtypeStruct((R, C), f32)
    return _pcall(
        body, name=name, grid=(R // tr,),
        in_specs=[pl.BlockSpec((n_parts, tr, C), lambda i: (0, i, 0)), blk, blk, blk],
        out_specs=[blk] * 4, out_shape=[out] * 4,
        compiler_params=_cparams("parallel"),
    )(parts, w, m, v)


SMALL_SLOTS = ((0, D_MODEL), (D_MODEL, D_MODEL), (2 * D_MODEL, D_MODEL), (3 * D_MODEL, LANE),
               (3 * D_MODEL + LANE, HEADS), (3 * D_MODEL + 2 * LANE, HEADS))
SMALL_LOSS = 3 * D_MODEL + 3 * LANE
SMALL_W = SMALL_LOSS + LANE


def _pack_small(gs):
    def body(nw_ref, cb_ref, fw_ref, gn_ref, sc_ref, ls_ref, o_ref):
        for ref, (start, width) in zip((nw_ref, cb_ref, fw_ref, gn_ref), SMALL_SLOTS[:4]):
            o_ref[:, start:start + width] = ref[...]
        o_ref[:, SMALL_SLOTS[4][0]:SMALL_SLOTS[4][0] + LANE] = sc_ref[0:1, :]
        o_ref[:, SMALL_SLOTS[5][0]:SMALL_SLOTS[5][0] + LANE] = sc_ref[1:2, :]
        o_ref[:, SMALL_LOSS:SMALL_W] = ls_ref[...]

    vm = pl.BlockSpec(memory_space=pltpu.VMEM)
    return _pcall(body, name="pack_small_grads", out_shape=jax.ShapeDtypeStruct((1, SMALL_W), f32),
                  in_specs=[vm] * 6, out_specs=vm)(*gs)


def _adamw_small(parts, ws, ms, vs):
    c1 = 1.0 - ADAM_B1 ** ADAM_STEP
    c2 = 1.0 - ADAM_B2 ** ADAM_STEP
    np_ = len(ws)

    def body(*refs):
        p_ref = refs[0]
        w_refs, m_refs, v_refs = refs[1:1 + np_], refs[1 + np_:1 + 2 * np_], refs[1 + 2 * np_:1 + 3 * np_]
        outs = refs[1 + 3 * np_:]
        g_refs, d_refs, nm_refs, nv_refs = (outs[i * np_:(i + 1) * np_] for i in range(4))
        loss_ref = outs[4 * np_]

        def total(start, width):
            t = p_ref[0, :, start:start + width]
            for s in range(1, N_DEV):
                t = t + p_ref[s, :, start:start + width]
            return t

        for i, (start, width) in enumerate(SMALL_SLOTS):
            g = total(start, width)
            nm = ADAM_B1 * m_refs[i][...] + (1.0 - ADAM_B1) * g
            nv = ADAM_B2 * v_refs[i][...] + (1.0 - ADAM_B2) * (g * g)
            g_refs[i][...] = g
            nm_refs[i][...] = nm
            nv_refs[i][...] = nv
            d_refs[i][...] = -ADAM_LR * ((nm / c1) / (jnp.sqrt(nv / c2) + ADAM_EPS) + ADAM_WD * w_refs[i][...])
        loss_ref[...] = total(SMALL_LOSS, LANE)

    vm = pl.BlockSpec(memory_space=pltpu.VMEM)
    shapes = [jax.ShapeDtypeStruct(w.shape, f32) for w in ws]
    res = _pcall(body, name="adamw_small", out_shape=shapes * 4 + [jax.ShapeDtypeStruct((1, LANE), f32)],
                 in_specs=[vm] * (1 + 3 * np_), out_specs=[vm] * (4 * np_ + 1))(parts, *ws, *ms, *vs)
    return [res[i * np_:(i + 1) * np_] for i in range(4)], res[4 * np_]


def _adamw_w_in(parts, w3, m3, v3):
    n_parts, n, _ = parts.shape
    c1 = 1.0 - ADAM_B1 ** ADAM_STEP
    c2 = 1.0 - ADAM_B2 ** ADAM_STEP

    def body(p_ref, w_ref, m_ref, v_ref, g_ref, d_ref, nm_ref, nv_ref):
        g = p_ref[0].astype(f32)
        for s in range(1, n_parts):
            g = g + p_ref[s].astype(f32)
        nm = ADAM_B1 * m_ref[:, 0, :] + (1.0 - ADAM_B1) * g
        nv = ADAM_B2 * v_ref[:, 0, :] + (1.0 - ADAM_B2) * (g * g)
        g_ref[:, 0, :] = g
        nm_ref[:, 0, :] = nm
        nv_ref[:, 0, :] = nv
        d_ref[:, 0, :] = -ADAM_LR * ((nm / c1) / (jnp.sqrt(nv / c2) + ADAM_EPS) + ADAM_WD * w_ref[:, 0, :])

    blk = pl.BlockSpec((n, 1, COL_TILE), lambda j: (0, 0, j))
    out = jax.ShapeDtypeStruct((n, 1, D_MODEL), f32)
    return _pcall(
        body, name="adamw_w_in", grid=(D_MODEL // COL_TILE,),
        in_specs=[pl.BlockSpec((n_parts, n, COL_TILE), lambda j: (0, 0, j)), blk, blk, blk],
        out_specs=[blk] * 4, out_shape=[out] * 4,
        compiler_params=_cparams("parallel"),
    )(parts, w3, m3, v3)


def _pad_lanes(vec8, start):
    return jnp.pad(vec8.reshape(1, -1), ((0, 0), (start, LANE - start - vec8.size)))


def kernel(x, norm_in_w, w_in, conv_qkv_w, A_log, dt_bias, gdn_norm_w, conv_w, conv_b, w_out, final_norm_w, loss_target, m_norm_in_w, m_w_in, m_conv_qkv_w, m_A_log, m_dt_bias, m_gdn_norm_w, m_conv_w, m_conv_b, m_w_out, m_final_norm_w, v_norm_in_w, v_w_in, v_conv_qkv_w, v_A_log, v_dt_bias, v_gdn_norm_w, v_conv_w, v_conv_b, v_w_out, v_final_norm_w):
    L = x.shape[1]
    nc = L // CHUNK
    xs = x[0]
    tgt = loss_target[0]
    fnw = final_norm_w.reshape(1, D_MODEL)

    as_rows = lambda a: jnp.transpose(a, (2, 0, 1))
    win_g, cqkv_g, cw_g = _all_gather([_cast_w_in(as_rows(w_in)), conv_qkv_w[0], conv_w[0]], "gather_weights")
    wpad = _relayout_w_in(win_g)
    cqkv = jnp.concatenate([cqkv_g[d] for d in range(N_DEV)], axis=1)
    cw = jnp.concatenate([cw_g[d] for d in range(N_DEV)], axis=1)
    alog_p = _pad_lanes(A_log, HEADS)
    dtb_p = _pad_lanes(dt_bias, HEADS)
    me_flat, me_chip = _flat(*_mesh_pos()), 2 * lax.axis_index("x") + lax.axis_index("y")
    tok = lambda started: started[4][0:1, 0:1]
    wo_own = w_out[0].astype(bf16)
    wo_started = _spread_start(wo_own, wpad, "gather", "gather_w_out_start")

    proj, h = _in_proj(xs, norm_in_w + tok(wo_started), wpad)
    qkv = _qkv_act(proj, cqkv)
    sc, gr = _scalars(proj, alog_p, dtb_p)
    o, u_all, w_all, vn_all, t_all, sp_all = _gdn_fwd(qkv, sc, gr)
    mix_a = _gdn_gate(o, proj, gdn_norm_w)
    mix_b = _conv_fwd(proj, cw, conv_b)
    wo = _own_slot(_spread_wait(wo_started, mix_b, "gather", "gather_w_out_wait"), wo_own, me_flat).reshape(-1, D_MODEL)
    dy, dyb, dmix_a, dmix_b, g_fnw, loss_v = _out_proj_loss(xs, mix_a, mix_b, wo, fnw, tgt)

    g_wout = jnp.concatenate([_tn_matmul(mix_a, dyb, "grad_w_out_a"), _tn_matmul(mix_b, dyb, "grad_w_out_b")], axis=0)
    g_wout = g_wout.reshape(N_DEV, -1, D_MODEL)
    g_wout_own = lax.dynamic_index_in_dim(g_wout, me_flat, 0, keepdims=False)
    gwo_started = _spread_start(g_wout, dyb, "scatter", "exchange_grad_w_out_start")
    do, dzg, g_gnw = _gdn_gate_bwd(o, proj, gdn_norm_w + tok(gwo_started), dmix_a)
    d_b, d_c, d_hc, d_zc, g_cw, g_cb = _conv_bwd(proj, cw, conv_b, dmix_b)
    dqkv_n, dsc, dgr = _gdn_bwd(qkv, sc, gr, u_all, w_all, vn_all, t_all, sp_all, do)
    dqkv, g_cqkv = _qkv_bwd(proj, cqkv, dqkv_n)
    dgr_col = jnp.pad(dgr.transpose(0, 2, 1).reshape(L, HEADS), ((0, 0), (HEADS, LANE - 2 * HEADS)))
    dba, g_sc = _scalars_bwd(proj, alog_p, dtb_p, dsc, dgr_col)
    pieces = [dqkv, dzg, dba, d_b, d_c, d_hc, d_zc]
    offs = [OFF_QKV, OFF_ZG, OFF_BA, OFF_B, OFF_C, OFF_HC, OFF_ZC]
    g_parts = [_tn_matmul(p, h, "grad_w_in_%d" % i) for i, p in enumerate(pieces)]
    g_win_blk = _grad_blocks(g_parts)

    (p_win,) = _pair_exchange([g_win_blk], "exchange_grads_pair")
    s_win = _pair_sum(g_win_blk, p_win, "pair_sum_w_in")
    s_win_own = lax.dynamic_index_in_dim(s_win, me_chip, 0, keepdims=False)
    gwi_started = _spread_start(s_win, s_win_own, "chips", "exchange_grads_chips_start")
    grad_x, g_nw = _input_grad(pieces, offs, wpad, xs, norm_in_w + tok(gwi_started), dy)

    r_wout = _own_slot(_spread_wait(gwo_started, grad_x, "scatter", "exchange_grad_w_out_wait"), g_wout_own, me_flat)
    r_cqkv, r_cw = _all_to_all(
        [g_cqkv.reshape(4, N_DEV, -1).transpose(1, 0, 2), g_cw.reshape(3, N_DEV, -1).transpose(1, 0, 2)],
        "exchange_small_sharded_grads")
    upd_wout =_adamw_reduce(r_wout, w_out[0], m_w_out[0], v_w_out[0], "adamw_w_out")
    upd_cqkv = _adamw_reduce(r_cqkv, conv_qkv_w[0], m_conv_qkv_w[0], v_conv_qkv_w[0], "adamw_conv_qkv_w")
    upd_cw = _adamw_reduce(r_cw, conv_w[0], m_conv_w[0], v_conv_w[0], "adamw_conv_w")

    small_g = _pack_small([g_nw, g_cb, g_fnw, g_gnw, g_sc, loss_v])
    (small_all,) = _all_gather([small_g], "gather_small_grads")
    fvec = lambda a: a.reshape(1, D_MODEL)
    upd_small, loss_sum = _adamw_small(
        small_all,
        [norm_in_w, conv_b, fvec(final_norm_w), gdn_norm_w, A_log, dt_bias],
        [m_norm_in_w, m_conv_b, fvec(m_final_norm_w), m_gdn_norm_w, m_A_log, m_dt_bias],
        [v_norm_in_w, v_conv_b, fvec(v_final_norm_w), v_gdn_norm_w, v_A_log, v_dt_bias])

    r_win = _own_slot(_spread_wait(gwi_started, loss_sum, "chips", "exchange_grads_chips_wait"), s_win_own, me_chip)
    upd_win = [jnp.transpose(a, (1, 2, 0)) for a in _adamw_w_in(r_win, as_rows(w_in), as_rows(m_w_in), as_rows(v_w_in))]

    outs = [loss_sum[0, 0], grad_x[None]]
    for k in range(4):
        nw_k, cb_k, fw_k, gn_k, al_k, dt_k = upd_small[k]
        outs += [nw_k, upd_win[k], upd_cqkv[k][None], al_k, dt_k, gn_k,
                 upd_cw[k][None], cb_k, upd_wout[k][None], fw_k.reshape(D_MODEL)]
    return tuple(outs)
```

```python
import functools
import math

import jax
import jax.numpy as jnp
from jax import lax
from jax.experimental import pallas as pl
from jax.experimental.pallas import tpu as pltpu

f32 = jnp.float32
bf16 = jnp.bfloat16

N_DEV = 8
D_MODEL = 1024
HEADS = 8
HEAD_DIM = 128
CHUNK = 64
GDN_WIDTH = HEADS * HEAD_DIM
CONV_WIDTH = 1024
PROJ_WIDTH = 8208
SHARD_W = PROJ_WIDTH // N_DEV
EPS = 1e-6

NAT_SMALL_END = 4112
PAD_COLS = 112
OFF_QKV, OFF_ZG, OFF_BA, OFF_B, OFF_C, OFF_HC, OFF_ZC = 0, 3072, 4096, 4224, 5248, 6272, 7296
PROJ_PAD = 8320
LANE = 128

ADAM_LR, ADAM_B1, ADAM_B2, ADAM_EPS, ADAM_WD, ADAM_STEP = 0.001, 0.9, 0.999, 1e-08, 0.01, 10

VMEM_LIMIT = 56 * 1024 * 1024

MESH = pl.DeviceIdType.MESH
ANY = pl.BlockSpec(memory_space=pl.ANY)


def _pcall(body, **kw):
    return pl.pallas_call(body, **kw)


def _cparams(*sem):
    return pltpu.CompilerParams(dimension_semantics=sem if sem else None, vmem_limit_bytes=VMEM_LIMIT)


def _mm(a, b):
    return jnp.dot(a.astype(bf16), b.astype(bf16), preferred_element_type=f32)


def _mm_nt(a, b):
    return lax.dot_general(a.astype(bf16), b.astype(bf16), (((1,), (1,)), ((), ())), preferred_element_type=f32)


def _mm_tn(a, b):
    return lax.dot_general(a.astype(bf16), b.astype(bf16), (((0,), (0,)), ((), ())), preferred_element_type=f32)


def _rows(shape):
    return lax.broadcasted_iota(jnp.int32, shape, 0)


def _lanes(shape):
    return lax.broadcasted_iota(jnp.int32, shape, 1)


def _shift_down(x, s):
    if s == 0:
        return x
    return jnp.where(_rows(x.shape) >= s, pltpu.roll(x, s, 0), 0.0)


def _shift_up(x, s):
    if s == 0:
        return x
    n = x.shape[0]
    return jnp.where(_rows(x.shape) < n - s, pltpu.roll(x, n - s, 0), 0.0)


def _sigmoid(x):
    return jax.nn.sigmoid(x)


def _softplus(x):
    e = jnp.exp(-jnp.abs(x))
    small = e * (1.0 - e * (0.5 - e * (1.0 / 3.0)))
    return jnp.maximum(x, 0.0) + jnp.where(e < 0.01, small, jnp.log(1.0 + e))


def _mesh_pos():
    return lax.axis_index("x"), lax.axis_index("y"), lax.axis_index("c")


def _flat(px, py, pc):
    return 4 * px + 2 * py + pc


def _all_gather(xs, name):
    n = len(xs)

    def body(*refs):
        x_refs, o_refs = refs[:n], refs[n:2 * n]
        send_sems, recv_sems, local_sems = refs[2 * n:]
        x, y, c = _mesh_pos()
        me, sibling = (x, y, c), (x, y, 1 - c)
        chips = [(1 - x, y), (x, 1 - y), (1 - x, 1 - y)]

        def copy(a, k, block, to, src=None):
            dst = o_refs[a].at[_flat(*block)]
            return pltpu.make_async_remote_copy(
                src_ref=dst if src is None else src, dst_ref=dst,
                send_sem=send_sems.at[a, k], recv_sem=recv_sems.at[a, k], device_id=to, device_id_type=MESH)

        mine, first, passed = [], [], []
        for a in range(n):
            cp = pltpu.make_async_copy(x_refs[a], o_refs[a].at[_flat(*me)], local_sems.at[a])
            cp.start()
            mine.append(cp)
            fa = [copy(a, 0, me, sibling, src=x_refs[a])]
            fa += [copy(a, 1 + j, me, (*chip, c), src=x_refs[a]) for j, chip in enumerate(chips)]
            for cp in fa:
                cp.start()
            first += fa
        for a in range(n):
            for j, chip in enumerate(chips):
                copy(a, 1 + j, (*chip, c), me).wait_recv()
                cp = copy(a, 4 + j, (*chip, c), sibling)
                cp.start()
                passed.append(cp)
        for a in range(n):
            copy(a, 0, sibling, me).wait_recv()
            for j, chip in enumerate(chips):
                copy(a, 4 + j, (*chip, 1 - c), me).wait_recv()
        for cp in first + passed:
            cp.wait_send()
        for cp in mine:
            cp.wait()

    outs = _pcall(
        body, name=name,
        out_shape=[jax.ShapeDtypeStruct((N_DEV,) + a.shape, a.dtype) for a in xs],
        in_specs=[ANY] * n, out_specs=[ANY] * n,
        scratch_shapes=[pltpu.SemaphoreType.DMA((n, 7)), pltpu.SemaphoreType.DMA((n, 7)), pltpu.SemaphoreType.DMA((n,))],
    )(*xs)
    return list(outs)


def _all_to_all(gs, name):
    n = len(gs)

    def body(*refs):
        g_refs, o_refs = refs[:n], refs[n:2 * n]
        send_sems, recv_sems, local_sems = refs[2 * n:]
        x, y, c = _mesh_pos()
        me = _flat(x, y, c)
        peers = []
        for k in range(1, N_DEV):
            kx, ky, kc = (k >> 2) & 1, (k >> 1) & 1, k & 1
            px = (1 - x) if kx else x
            py = (1 - y) if ky else y
            pc = (1 - c) if kc else c
            peers.append((px, py, pc))

        def copy(a, k):
            peer = peers[k - 1]
            return pltpu.make_async_remote_copy(
                src_ref=g_refs[a].at[_flat(*peer)], dst_ref=o_refs[a].at[me],
                send_sem=send_sems.at[a, k - 1], recv_sem=recv_sems.at[a, k - 1], device_id=peer, device_id_type=MESH)

        def arrival(a, k):
            peer = peers[k - 1]
            return pltpu.make_async_remote_copy(
                src_ref=g_refs[a].at[me], dst_ref=o_refs[a].at[_flat(*peer)],
                send_sem=send_sems.at[a, k - 1], recv_sem=recv_sems.at[a, k - 1], device_id=peer, device_id_type=MESH)

        mine, sent = [], []
        for a in range(n):
            cp = pltpu.make_async_copy(g_refs[a].at[me], o_refs[a].at[me], local_sems.at[a])
            cp.start()
            mine.append(cp)
            for k in range(1, N_DEV):
                cp = copy(a, k)
                cp.start()
                sent.append(cp)
        for a in range(n):
            for k in range(1, N_DEV):
                arrival(a, k).wait_recv()
        for cp in sent:
            cp.wait_send()
        for cp in mine:
            cp.wait()

    outs = _pcall(
        body, name=name,
        out_shape=[jax.ShapeDtypeStruct(a.shape, a.dtype) for a in gs],
        in_specs=[ANY] * n, out_specs=[ANY] * n,
        scratch_shapes=[pltpu.SemaphoreType.DMA((n, 7)), pltpu.SemaphoreType.DMA((n, 7)), pltpu.SemaphoreType.DMA((n,))],
    )(*gs)
    return list(outs)


def _pair_exchange(gs, name):
    n = len(gs)
    chips = [(0, 0), (0, 1), (1, 0), (1, 1)]

    def body(*refs):
        g_refs, o_refs = refs[:n], refs[n:2 * n]
        send_sems, recv_sems = refs[2 * n:]
        x, y, c = _mesh_pos()
        sibling = (x, y, 1 - c)

        def copy(a, i):
            xp, yp = chips[i]
            return pltpu.make_async_remote_copy(
                src_ref=g_refs[a].at[_flat(xp, yp, 1 - c)], dst_ref=o_refs[a].at[i],
                send_sem=send_sems.at[a, i], recv_sem=recv_sems.at[a, i], device_id=sibling, device_id_type=MESH)

        cps = [copy(a, i) for a in range(n) for i in range(4)]
        for cp in cps:
            cp.start()
        for cp in cps:
            cp.wait()

    outs = _pcall(
        body, name=name,
        out_shape=[jax.ShapeDtypeStruct((4,) + a.shape[1:], a.dtype) for a in gs],
        in_specs=[ANY] * n, out_specs=[ANY] * n,
        scratch_shapes=[pltpu.SemaphoreType.DMA((n, 4)), pltpu.SemaphoreType.DMA((n, 4))],
    )(*gs)
    return list(outs)


def _pair_sum(g, p1, name):
    _, R, C = g.shape
    tr = 256 if R % 256 == 0 else R
    cidx = lax.axis_index("c").astype(jnp.int32).reshape(1)

    def body(c_ref, g_ref, p_ref, o_ref):
        o_ref[...] = (g_ref[...].astype(f32) + p_ref[...].astype(f32)).astype(o_ref.dtype)

    return _pcall(
        body, name=name,
        grid_spec=pltpu.PrefetchScalarGridSpec(
            num_scalar_prefetch=1, grid=(4, R // tr),
            in_specs=[pl.BlockSpec((1, tr, C), lambda i, r, c_ref: (2 * i + c_ref[0], r, 0)),
                      pl.BlockSpec((1, tr, C), lambda i, r, c_ref: (i, r, 0))],
            out_specs=pl.BlockSpec((1, tr, C), lambda i, r, c_ref: (i, r, 0))),
        out_shape=jax.ShapeDtypeStruct((4, R, C), g.dtype),
        compiler_params=_cparams("parallel", "parallel"),
    )(cidx, g, p1)


HBM = pl.BlockSpec(memory_space=pltpu.HBM)
SEM = pl.BlockSpec(memory_space=pltpu.SEMAPHORE)
EFFECT = pltpu.SideEffectType.DATAFLOW_SIDE_EFFECTING


def _peers(x, y, c):
    out = []
    for k in range(1, N_DEV):
        kx, ky, kc = (k >> 2) & 1, (k >> 1) & 1, k & 1
        out.append(((1 - x) if kx else x, (1 - y) if ky else y, (1 - c) if kc else c))
    return out


SPREAD_COPIES = {"gather": N_DEV - 1, "scatter": N_DEV - 1, "chips": 3}


def _spread_copy(src_ref, land_ref, send_sems, recv_sems, k, plan):
    x, y, c = _mesh_pos()
    if plan == "chips":
        px, py = [(1 - x, y), (x, 1 - y), (1 - x, 1 - y)][k]
        peer, src, slot = (px, py, c), src_ref.at[2 * px + py], 2 * x + y
    else:
        peer = _peers(x, y, c)[k]
        src, slot = (src_ref.at[_flat(*peer)] if plan == "scatter" else src_ref), _flat(x, y, c)
    return pltpu.make_async_remote_copy(
        src_ref=src, dst_ref=land_ref.at[slot], send_sem=send_sems.at[k], recv_sem=recv_sems.at[k],
        device_id=peer, device_id_type=MESH)


def _spread_start(src, after, plan, name):
    land_shape = (N_DEV,) + src.shape if plan == "gather" else src.shape
    n_copies = SPREAD_COPIES[plan]

    def body(src_ref, land_ref, after_ref, send_sems, recv_sems, src_thru, land_thru, token):
        for k in range(n_copies):
            _spread_copy(src_ref, land_ref, send_sems, recv_sems, k, plan).start()
        token[...] = jnp.zeros_like(token)

    return _pcall(
        body, name=name,
        out_shape=(pltpu.SemaphoreType.DMA((n_copies,)), pltpu.SemaphoreType.DMA((n_copies,)),
                   pltpu.HBM(src.shape, src.dtype), pltpu.HBM(land_shape, src.dtype), jax.ShapeDtypeStruct((8, LANE), f32)),
        in_specs=(HBM, HBM, ANY), out_specs=(SEM, SEM, HBM, HBM, pl.BlockSpec(memory_space=pltpu.VMEM)),
        input_output_aliases={0: 2, 1: 3},
        compiler_params=pltpu.CompilerParams(has_side_effects=EFFECT),
    )(pltpu.with_memory_space_constraint(src, pltpu.HBM),
      pltpu.with_memory_space_constraint(lax.empty(land_shape, src.dtype), pltpu.HBM), after)


def _spread_wait(started, after, plan, name):
    send_sems, recv_sems, src_thru, land_thru, _ = started

    def body(src_ref, land_ref, send_sems, recv_sems, after_ref, src_dead, got_ref):
        for k in range(SPREAD_COPIES[plan]):
            cp = _spread_copy(src_ref, land_ref, send_sems, recv_sems, k, plan)
            cp.wait_send()
            cp.wait_recv()

    return _pcall(
        body, name=name,
        out_shape=(pltpu.HBM(src_thru.shape, src_thru.dtype), pltpu.HBM(land_thru.shape, land_thru.dtype)),
        in_specs=(HBM, HBM, SEM, SEM, ANY), out_specs=(HBM, HBM), input_output_aliases={0: 0, 1: 1},
        compiler_params=pltpu.CompilerParams(has_side_effects=EFFECT),
    )(src_thru, land_thru, send_sems, recv_sems, after)[1]


def _own_slot(land, block, slot):
    zero = jnp.zeros((), jnp.int32)
    return lax.dynamic_update_slice(land, block[None], (slot.astype(jnp.int32),) + (zero,) * block.ndim)


PIECE_NAT = (0, 3072, 4096, 4112, 5136, 6160, 7184, PROJ_WIDTH)


COL_TILE = 256


def _cast_w_in(w3):
    n = w3.shape[0]

    def body(w_ref, o_ref):
        o_ref[...] = w_ref[:, 0, :].astype(bf16)

    return _pcall(
        body, name="cast_w_in", grid=(D_MODEL // COL_TILE,),
        in_specs=[pl.BlockSpec((n, 1, COL_TILE), lambda j: (0, 0, j))],
        out_specs=pl.BlockSpec((n, COL_TILE), lambda j: (0, j)),
        out_shape=jax.ShapeDtypeStruct((n, D_MODEL), bf16),
        compiler_params=_cparams("parallel"),
    )(w3)


def _relayout_w_in(win_g):
    def body(g_ref, o_ref):
        o_ref[NAT_SMALL_END:NAT_SMALL_END + PAD_COLS, :] = jnp.zeros((PAD_COLS, COL_TILE), o_ref.dtype)
        for d in range(N_DEV):
            n0, n1 = d * SHARD_W, (d + 1) * SHARD_W
            cut = min(max(NAT_SMALL_END - n0, 0), SHARD_W)
            if cut > 0:
                o_ref[n0:n0 + cut, :] = g_ref[d, 0:cut, :]
            if cut < SHARD_W:
                o_ref[n0 + cut + PAD_COLS:n1 + PAD_COLS, :] = g_ref[d, cut:SHARD_W, :]

    return _pcall(
        body, name="relayout_w_in", grid=(D_MODEL // COL_TILE,),
        in_specs=[pl.BlockSpec((N_DEV, SHARD_W, COL_TILE), lambda j: (0, 0, j))],
        out_specs=pl.BlockSpec((PROJ_PAD, COL_TILE), lambda j: (0, j)),
        out_shape=jax.ShapeDtypeStruct((PROJ_PAD, D_MODEL), win_g.dtype),
        compiler_params=_cparams("parallel"),
    )(win_g)


def _grad_blocks(g_parts):
    npc = len(g_parts)

    def body(*refs):
        p_refs, o_ref = refs[:npc], refs[npc]
        for d in range(N_DEV):
            n0, n1 = d * SHARD_W, (d + 1) * SHARD_W
            for i in range(npc):
                lo, hi = max(n0, PIECE_NAT[i]), min(n1, PIECE_NAT[i + 1])
                if lo < hi:
                    o_ref[d, lo - n0:hi - n0, :] = p_refs[i][lo - PIECE_NAT[i]:hi - PIECE_NAT[i], :]

    return _pcall(
        body, name="grad_blocks", grid=(D_MODEL // COL_TILE,),
        in_specs=[pl.BlockSpec((p.shape[0], COL_TILE), lambda j: (0, j)) for p in g_parts],
        out_specs=pl.BlockSpec((N_DEV, SHARD_W, COL_TILE), lambda j: (0, 0, j)),
        out_shape=jax.ShapeDtypeStruct((N_DEV, SHARD_W, D_MODEL), bf16),
        compiler_params=_cparams("parallel"),
    )(*g_parts)


def _in_proj(x, nw, wpad_t):
    L = x.shape[0]
    tn = 640
    nj = wpad_t.shape[0] // tn

    def body(x_ref, nw_ref, w_ref, proj_ref, h_ref):
        @pl.when(pl.program_id(0) == 0)
        def _():
            for r in range(0, L, 256):
                xs = x_ref[r:r + 256, :]
                ms = jnp.mean(xs * xs, axis=-1, keepdims=True)
                h_ref[r:r + 256, :] = ((xs * lax.rsqrt(ms + EPS)) * nw_ref[...]).astype(bf16)
        for r in range(0, L, 512):
            proj_ref[r:r + 512, :] = lax.dot_general(h_ref[r:r + 512, :], w_ref[...], (((1,), (1,)), ((), ())),
                                                     preferred_element_type=f32)

    return _pcall(
        body, name="in_proj", grid=(nj,),
        in_specs=[pl.BlockSpec((L, D_MODEL), lambda j: (0, 0)), pl.BlockSpec((1, D_MODEL), lambda j: (0, 0)),
                  pl.BlockSpec((tn, D_MODEL), lambda j: (j, 0))],
        out_specs=[pl.BlockSpec((L, tn), lambda j: (0, j)), pl.BlockSpec((L, D_MODEL), lambda j: (0, 0))],
        out_shape=[jax.ShapeDtypeStruct((L, wpad_t.shape[0]), f32), jax.ShapeDtypeStruct((L, D_MODEL), bf16)],
        compiler_params=_cparams("arbitrary"),
    )(x, nw, wpad_t)


def _conv4(x, cw_ref):
    return (cw_ref[3:4, :] * x + cw_ref[2:3, :] * _shift_down(x, 1) + cw_ref[1:2, :] * _shift_down(x, 2)
            + cw_ref[0:1, :] * _shift_down(x, 3))


def _qkv_act(proj, cw):
    L = proj.shape[0]

    def body(x_ref, cw_ref, o_ref):
        j = pl.program_id(0)
        c = _conv4(x_ref[...], cw_ref)
        a = c * _sigmoid(c)
        rn = lax.rsqrt(jnp.sum(a * a, axis=1, keepdims=True) + EPS)
        scale = jnp.where(j < HEADS, HEAD_DIM ** -0.5, 1.0).astype(f32)
        o_ref[...] = jnp.where(j < 2 * HEADS, (a * rn) * scale, a)

    return _pcall(
        body, name="qkv_act", grid=(3 * HEADS,),
        in_specs=[pl.BlockSpec((L, LANE), lambda j: (0, j)), pl.BlockSpec((4, LANE), lambda j: (0, j))],
        out_specs=pl.BlockSpec((L, LANE), lambda j: (0, j)),
        out_shape=jax.ShapeDtypeStruct((L, 3 * GDN_WIDTH), f32),
        compiler_params=_cparams("parallel"),
    )(proj, cw)


def _scalars(proj, alog_p, dtb_p):
    L = proj.shape[0]
    nc = L // CHUNK

    def body(x_ref, al_ref, dt_ref, sc_ref, gr_ref):
        x = x_ref[...]
        lane = _lanes(x.shape)
        beta = _sigmoid(x)
        g = -jnp.exp(al_ref[...]) * _softplus(x + dt_ref[...])
        gc = jnp.where((lane >= HEADS) & (lane < 2 * HEADS), g, 0.0)
        rc = _rows(x.shape) & (CHUNK - 1)
        for s in (1, 2, 4, 8, 16, 32):
            gc = gc + jnp.where(rc >= s, pltpu.roll(gc, s, 0), 0.0)
        sc_ref[...] = jnp.where(lane < HEADS, beta, gc)
        sel = (_lanes((HEADS, LANE)) == _rows((HEADS, LANE)) + HEADS).astype(f32)
        for c in range(nc):
            gr_ref[c] = lax.dot_general(sel, sc_ref[c * CHUNK:(c + 1) * CHUNK, :], (((1,), (1,)), ((), ())),
                                        preferred_element_type=f32, precision=lax.Precision.HIGHEST)

    return _pcall(
        body, name="scalars", grid=(1,),
        in_specs=[pl.BlockSpec((L, LANE), lambda i: (0, OFF_BA // LANE)), pl.BlockSpec((1, LANE), lambda i: (0, 0)),
                  pl.BlockSpec((1, LANE), lambda i: (0, 0))],
        out_specs=[pl.BlockSpec((L, LANE), lambda i: (0, 0)), pl.BlockSpec((nc, HEADS, CHUNK), lambda i: (0, 0, 0))],
        out_shape=[jax.ShapeDtypeStruct((L, LANE), f32), jax.ShapeDtypeStruct((nc, HEADS, CHUNK), f32)],
        compiler_params=_cparams("arbitrary"),
    )(proj, alog_p, dtb_p)


def _head_scalars(sc, gr_ref, h):
    lane = _lanes(sc.shape)
    beta = jnp.sum(jnp.where(lane == h, sc, 0.0), axis=1, keepdims=True)
    gcc = jnp.sum(jnp.where(lane == HEADS + h, sc, 0.0), axis=1, keepdims=True)
    gcr = gr_ref[0, h:h + 1, :]
    gl = jnp.sum(jnp.where(_lanes(gcr.shape) == CHUNK - 1, gcr, 0.0), axis=1, keepdims=True)
    ii, jj = _rows((CHUNK, CHUNK)), _lanes((CHUNK, CHUNK))
    dmat = jnp.where(ii >= jj, jnp.exp(jnp.minimum(gcc - gcr, 0.0)), 0.0)
    dmat_t = jnp.where(jj >= ii, jnp.exp(jnp.minimum(gcr - gcc, 0.0)), 0.0)
    return beta, gcc, gl, dmat, dmat_t, ii, jj


def _gdn_fwd(qkv, sc, gr):
    L = qkv.shape[0]
    nc = L // CHUNK
    W = GDN_WIDTH

    def body(qkv_ref, sc_ref, gr_ref, o_ref, u_ref, w_ref, vn_ref, t_ref, sp_ref, s_scr):
        @pl.when(pl.program_id(0) == 0)
        def _():
            s_scr[...] = jnp.zeros_like(s_scr)
        sc_v = sc_ref[...]
        HS = range(HEADS)
        cs = [slice(h * HEAD_DIM, (h + 1) * HEAD_DIM) for h in HS]
        q = [qkv_ref[:, h * HEAD_DIM:(h + 1) * HEAD_DIM] for h in HS]
        k = [qkv_ref[:, W + h * HEAD_DIM:W + (h + 1) * HEAD_DIM] for h in HS]
        v = [qkv_ref[:, 2 * W + h * HEAD_DIM:2 * W + (h + 1) * HEAD_DIM] for h in HS]
        hsc = [_head_scalars(sc_v, gr_ref, h) for h in HS]
        beta, gcc, gl, dmat = ([x[i] for x in hsc] for i in range(4))
        ii, jj = hsc[0][5], hsc[0][6]
        eg = [jnp.exp(gcc[h]) for h in HS]
        kb = [k[h] * beta[h] for h in HS]
        kk = [_mm_nt(kb[h], k[h]) for h in HS]
        qk = [_mm_nt(q[h], k[h]) for h in HS]
        n0 = [-jnp.where(ii > jj, kk[h] * dmat[h], 0.0) for h in HS]
        n1 = [_mm(n0[h], n0[h]) for h in HS]
        n2 = [_mm(n1[h], n1[h]) for h in HS]
        p01 = [n0[h] + n1[h] + _mm(n0[h], n1[h]) for h in HS]
        n3 = [_mm(n2[h], n2[h]) for h in HS]
        n4 = [_mm(n3[h], n3[h]) for h in HS]
        p23 = [n2[h] + n3[h] + _mm(n2[h], n3[h]) for h in HS]
        n5 = [_mm(n4[h], n4[h]) for h in HS]
        p03 = [p01[h] + p23[h] + _mm(p01[h], p23[h]) for h in HS]
        p45 = [n4[h] + n5[h] + _mm(n4[h], n5[h]) for h in HS]
        t = [p03[h] + p45[h] + _mm(p03[h], p45[h]) for h in HS]
        vb = [v[h] * beta[h] for h in HS]
        kbg = [kb[h] * eg[h] for h in HS]
        uw = [_mm(t[h], jnp.concatenate([vb[h], kbg[h]], axis=1)) for h in HS]
        u = [vb[h] + uw[h][:, :HEAD_DIM] for h in HS]
        w = [kbg[h] + uw[h][:, HEAD_DIM:] for h in HS]
        s = [s_scr[h] for h in HS]
        ws = [_mm(jnp.concatenate([w[h], q[h] * eg[h]], axis=0), s[h]) for h in HS]
        vn = [u[h] - ws[h][:CHUNK] for h in HS]
        p = [jnp.where(ii >= jj, qk[h] * dmat[h], 0.0) for h in HS]
        pv = [_mm(p[h], vn[h]) for h in HS]
        kv = [_mm_tn(k[h] * jnp.exp(gl[h] - gcc[h]), vn[h]) for h in HS]
        for h in HS:
            sp_ref[0, cs[h], :] = s[h]
            o_ref[:, cs[h]] = ws[h][CHUNK:] + pv[h]
            s_scr[h] = jnp.exp(gl[h]) * s[h] + kv[h]
            u_ref[:, cs[h]] = u[h]
            w_ref[:, cs[h]] = w[h]
            vn_ref[:, cs[h]] = vn[h]
            t_ref[0, h] = t[h]

    row = lambda c: (c, 0)
    act = jax.ShapeDtypeStruct((L, W), f32)
    return _pcall(
        body, name="gdn_fwd", grid=(nc,),
        in_specs=[pl.BlockSpec((CHUNK, 3 * W), row), pl.BlockSpec((CHUNK, LANE), row),
                  pl.BlockSpec((1, HEADS, CHUNK), lambda c: (c, 0, 0))],
        out_specs=[pl.BlockSpec((CHUNK, W), row)] * 4 + [
            pl.BlockSpec((1, HEADS, CHUNK, CHUNK), lambda c: (c, 0, 0, 0)),
            pl.BlockSpec((1, W, HEAD_DIM), lambda c: (c, 0, 0))],
        out_shape=[act, act, act, act, jax.ShapeDtypeStruct((nc, HEADS, CHUNK, CHUNK), f32),
                   jax.ShapeDtypeStruct((nc, W, HEAD_DIM), f32)],
        scratch_shapes=[pltpu.VMEM((HEADS, HEAD_DIM, HEAD_DIM), f32)],
        compiler_params=_cparams("arbitrary"),
    )(qkv, sc, gr)


def _gdn_gate(o, proj, gnw):
    L = o.shape[0]

    def body(o_ref, z_ref, w_ref, m_ref):
        ov, z = o_ref[...], z_ref[...]
        rms = lax.rsqrt(jnp.mean(ov * ov, axis=-1, keepdims=True) + EPS)
        m_ref[...] = (((ov * rms) * w_ref[...]) * (z * _sigmoid(z))).astype(bf16)

    return _pcall(
        body, name="gdn_gate", grid=(HEADS,),
        in_specs=[pl.BlockSpec((L, LANE), lambda j: (0, j)), pl.BlockSpec((L, LANE), lambda j: (0, OFF_ZG // LANE + j)),
                  pl.BlockSpec((1, LANE), lambda j: (0, 0))],
        out_specs=pl.BlockSpec((L, LANE), lambda j: (0, j)),
        out_shape=jax.ShapeDtypeStruct((L, GDN_WIDTH), bf16),
        compiler_params=_cparams("parallel"),
    )(o, proj, gnw)


def _conv3(u, cw_ref):
    return cw_ref[2:3, :] * u + cw_ref[1:2, :] * _shift_down(u, 1) + cw_ref[0:1, :] * _shift_down(u, 2)


def _conv_specs(L):
    blk = lambda off: pl.BlockSpec((L, LANE), lambda j, off=off: (0, off // LANE + j))
    return [blk(OFF_B), blk(OFF_C), blk(OFF_HC), blk(OFF_ZC),
            pl.BlockSpec((3, LANE), lambda j: (0, j)), pl.BlockSpec((1, LANE), lambda j: (0, j))]


def _conv_fwd(proj, cw, cb):
    L = proj.shape[0]

    def body(b_ref, c_ref, h_ref, z_ref, cw_ref, cb_ref, m_ref):
        z = z_ref[...]
        cv = _conv3(c_ref[...] * h_ref[...], cw_ref) + cb_ref[...]
        m_ref[...] = ((b_ref[...] * cv) * (z * _sigmoid(z))).astype(bf16)

    return _pcall(
        body, name="conv_fwd", grid=(CONV_WIDTH // LANE,),
        in_specs=_conv_specs(L),
        out_specs=pl.BlockSpec((L, LANE), lambda j: (0, j)),
        out_shape=jax.ShapeDtypeStruct((L, CONV_WIDTH), bf16),
        compiler_params=_cparams("parallel"),
    )(proj, proj, proj, proj, cw, cb)


def _out_proj_loss(x, mix_a, mix_b, wo, fw, tgt):
    L = x.shape[0]
    tm = min(256, L)

    def body(x_ref, ma_ref, mb_ref, wo_ref, fw_ref, t_ref, dy_ref, dyb_ref, dma_ref, dmb_ref, gfw_ref, loss_ref):
        @pl.when(pl.program_id(0) == 0)
        def _():
            gfw_ref[...] = jnp.zeros_like(gfw_ref)
            loss_ref[...] = jnp.zeros_like(loss_ref)
        y = x_ref[...] + jnp.dot(ma_ref[...], wo_ref[:GDN_WIDTH, :], preferred_element_type=f32) \
            + jnp.dot(mb_ref[...], wo_ref[GDN_WIDTH:, :], preferred_element_type=f32)
        r = lax.rsqrt(jnp.mean(y * y, axis=-1, keepdims=True) + EPS)
        yh = y * r
        fwv = fw_ref[...]
        diff = yh * fwv - t_ref[...]
        loss_ref[...] += jnp.sum(jnp.sum(diff * diff, axis=-1, keepdims=True), axis=0, keepdims=True) * (0.5 / D_MODEL)
        dout = diff * (1.0 / D_MODEL)
        gfw_ref[...] += jnp.sum(dout * yh, axis=0, keepdims=True)
        dyh = dout * fwv
        dy = r * (dyh - yh * jnp.mean(dyh * yh, axis=-1, keepdims=True))
        dy_ref[...] = dy
        dyb = dy.astype(bf16)
        dyb_ref[...] = dyb
        dma_ref[...] = lax.dot_general(dyb, wo_ref[:GDN_WIDTH, :], (((1,), (1,)), ((), ())), preferred_element_type=f32)
        dmb_ref[...] = lax.dot_general(dyb, wo_ref[GDN_WIDTH:, :], (((1,), (1,)), ((), ())), preferred_element_type=f32)

    row = lambda i: (i, 0)
    fix = lambda i: (0, 0)
    act = jax.ShapeDtypeStruct((L, D_MODEL), f32)
    return _pcall(
        body, name="out_proj_loss", grid=(L // tm,),
        in_specs=[pl.BlockSpec((tm, D_MODEL), row), pl.BlockSpec((tm, GDN_WIDTH), row), pl.BlockSpec((tm, CONV_WIDTH), row),
                  pl.BlockSpec((GDN_WIDTH + CONV_WIDTH, D_MODEL), fix), pl.BlockSpec((1, D_MODEL), fix),
                  pl.BlockSpec((tm, D_MODEL), row)],
        out_specs=[pl.BlockSpec((tm, D_MODEL), row), pl.BlockSpec((tm, D_MODEL), row), pl.BlockSpec((tm, GDN_WIDTH), row),
                   pl.BlockSpec((tm, CONV_WIDTH), row), pl.BlockSpec((1, D_MODEL), fix), pl.BlockSpec((1, LANE), fix)],
        out_shape=[act, jax.ShapeDtypeStruct((L, D_MODEL), bf16), act, act,
                   jax.ShapeDtypeStruct((1, D_MODEL), f32), jax.ShapeDtypeStruct((1, LANE), f32)],
        compiler_params=_cparams("arbitrary"),
    )(x, mix_a, mix_b, wo, fw, tgt)


def _tn_matmul(a, b, name):
    L, M = a.shape
    N = b.shape[1]
    tm = 512 if M % 512 == 0 else M

    def body(a_ref, b_ref, o_ref):
        o_ref[...] = lax.dot_general(a_ref[...], b_ref[...], (((0,), (0,)), ((), ())),
                                     preferred_element_type=f32).astype(o_ref.dtype)

    return _pcall(
        body, name=name, grid=(M // tm,),
        in_specs=[pl.BlockSpec((L, tm), lambda i: (0, i)), pl.BlockSpec((L, N), lambda i: (0, 0))],
        out_specs=pl.BlockSpec((tm, N), lambda i: (i, 0)),
        out_shape=jax.ShapeDtypeStruct((M, N), bf16),
        compiler_params=_cparams("parallel"),
    )(a, b)


def _gdn_gate_bwd(o, proj, gnw, dmix_a):
    L = o.shape[0]

    def body(o_ref, z_ref, w_ref, dm_ref, do_ref, dz_ref, gw_ref):
        @pl.when(pl.program_id(0) == 0)
        def _():
            gw_ref[...] = jnp.zeros_like(gw_ref)
        ov, z, dm, wv = o_ref[...], z_ref[...], dm_ref[...], w_ref[...]
        rms = lax.rsqrt(jnp.mean(ov * ov, axis=-1, keepdims=True) + EPS)
        xh = ov * rms
        sg = _sigmoid(z)
        d_on = dm * (z * sg)
        dz_ref[...] = (dm * (xh * wv) * (sg * (1.0 + z * (1.0 - sg)))).astype(bf16)
        gw_ref[...] += jnp.sum(d_on * xh, axis=0, keepdims=True)
        dxh = d_on * wv
        do_ref[...] = rms * (dxh - xh * jnp.mean(dxh * xh, axis=-1, keepdims=True))

    return _pcall(
        body, name="gdn_gate_bwd", grid=(HEADS,),
        in_specs=[pl.BlockSpec((L, LANE), lambda j: (0, j)), pl.BlockSpec((L, LANE), lambda j: (0, OFF_ZG // LANE + j)),
                  pl.BlockSpec((1, LANE), lambda j: (0, 0)), pl.BlockSpec((L, LANE), lambda j: (0, j))],
        out_specs=[pl.BlockSpec((L, LANE), lambda j: (0, j)), pl.BlockSpec((L, LANE), lambda j: (0, j)),
                   pl.BlockSpec((1, LANE), lambda j: (0, 0))],
        out_shape=[jax.ShapeDtypeStruct((L, GDN_WIDTH), f32), jax.ShapeDtypeStruct((L, GDN_WIDTH), bf16),
                   jax.ShapeDtypeStruct((1, LANE), f32)],
        compiler_params=_cparams("arbitrary"),
    )(o, proj, gnw, dmix_a)


def _conv_bwd(proj, cw, cb, dmix_b):
    L = proj.shape[0]

    def body(b_ref, c_ref, h_ref, z_ref, cw_ref, cb_ref, dm_ref, db_ref, dc_ref, dh_ref, dz_ref, gcw_ref, gcb_ref):
        bv, cv_, hv, z, dm = b_ref[...], c_ref[...], h_ref[...], z_ref[...], dm_ref[...]
        u = cv_ * hv
        cv = _conv3(u, cw_ref) + cb_ref[...]
        sg = _sigmoid(z)
        sz = z * sg
        db_ref[...] = (dm * cv * sz).astype(bf16)
        dz_ref[...] = (dm * (bv * cv) * (sg * (1.0 + z * (1.0 - sg)))).astype(bf16)
        dcv = dm * bv * sz
        gcb_ref[...] = jnp.sum(dcv, axis=0, keepdims=True)
        gcw_ref[2:3, :] = jnp.sum(dcv * u, axis=0, keepdims=True)
        gcw_ref[1:2, :] = jnp.sum(dcv * _shift_down(u, 1), axis=0, keepdims=True)
        gcw_ref[0:1, :] = jnp.sum(dcv * _shift_down(u, 2), axis=0, keepdims=True)
        du = cw_ref[2:3, :] * dcv + cw_ref[1:2, :] * _shift_up(dcv, 1) + cw_ref[0:1, :] * _shift_up(dcv, 2)
        dc_ref[...] = (du * hv).astype(bf16)
        dh_ref[...] = (du * cv_).astype(bf16)

    col = pl.BlockSpec((L, LANE), lambda j: (0, j))
    act = jax.ShapeDtypeStruct((L, CONV_WIDTH), bf16)
    return _pcall(
        body, name="conv_bwd", grid=(CONV_WIDTH // LANE,),
        in_specs=_conv_specs(L) + [col],
        out_specs=[col, col, col, col, pl.BlockSpec((3, LANE), lambda j: (0, j)), pl.BlockSpec((1, LANE), lambda j: (0, j))],
        out_shape=[act, act, act, act, jax.ShapeDtypeStruct((3, CONV_WIDTH), f32), jax.ShapeDtypeStruct((1, CONV_WIDTH), f32)],
        compiler_params=_cparams("parallel"),
    )(proj, proj, proj, proj, cw, cb, dmix_b)


def _gdn_bwd(qkv, sc, gr, u_all, w_all, vn_all, t_all, sp_all, do_all):
    L = qkv.shape[0]
    nc = L // CHUNK
    W = GDN_WIDTH

    def body(qkv_ref, sc_ref, gr_ref, u_ref, w_ref, vn_ref, t_ref, sp_ref, do_ref, dqkv_ref, dsc_ref, dgr_ref, ds_scr):
        @pl.when(pl.program_id(0) == 0)
        def _():
            ds_scr[...] = jnp.zeros_like(ds_scr)
        sc_v = sc_ref[...]
        lane = _lanes(sc_v.shape)
        dsc = jnp.zeros(sc_v.shape, f32)
        HS = range(HEADS)
        cs = [slice(h * HEAD_DIM, (h + 1) * HEAD_DIM) for h in HS]
        q = [qkv_ref[:, h * HEAD_DIM:(h + 1) * HEAD_DIM] for h in HS]
        k = [qkv_ref[:, W + h * HEAD_DIM:W + (h + 1) * HEAD_DIM] for h in HS]
        v = [qkv_ref[:, 2 * W + h * HEAD_DIM:2 * W + (h + 1) * HEAD_DIM] for h in HS]
        hsc = [_head_scalars(sc_v, gr_ref, h) for h in HS]
        beta, gcc, gl, dmat, dmat_t = ([x[i] for x in hsc] for i in range(5))
        ii, jj = hsc[0][5], hsc[0][6]
        eg = [jnp.exp(gcc[h]) for h in HS]
        ekl = [jnp.exp(gl[h] - gcc[h]) for h in HS]
        egl = [jnp.exp(gl[h]) for h in HS]
        kb = [k[h] * beta[h] for h in HS]
        ks = [k[h] * ekl[h] for h in HS]
        do = [do_ref[:, cs[h]] for h in HS]
        vn = [vn_ref[:, cs[h]] for h in HS]
        s = [sp_ref[0, cs[h], :] for h in HS]
        dsn = [ds_scr[h] for h in HS]

        kq = [_mm_nt(k[h], q[h]) for h in HS]
        ksd = [_mm(ks[h], dsn[h]) for h in HS]
        p_t = [jnp.where(jj >= ii, kq[h] * dmat_t[h], 0.0) for h in HS]
        ptd = [_mm(p_t[h], do[h]) for h in HS]
        dvn = [ptd[h] + ksd[h] for h in HS]
        dodv = [jnp.concatenate([do[h], dvn[h]], axis=0) for h in HS]
        x1 = [_mm_nt(dodv[h], s[h]) for h in HS]
        dks = [_mm_nt(vn[h], dsn[h]) for h in HS]
        dov = [_mm_nt(do[h], vn[h]) for h in HS]
        vdo = [_mm_nt(vn[h], do[h]) for h in HS]
        kk = [_mm_nt(kb[h], k[h]) for h in HS]
        qk = [_mm_nt(q[h], k[h]) for h in HS]
        w = [w_ref[:, cs[h]] for h in HS]
        qd = [q[h] * eg[h] for h in HS]
        dsq = [_mm_tn(jnp.concatenate([qd[h], -w[h]], axis=0), dodv[h]) for h in HS]
        dgl = [egl[h] * jnp.sum(jnp.sum(s[h] * dsn[h], axis=1, keepdims=True), axis=0, keepdims=True) for h in HS]
        for h in HS:
            ds_scr[h] = egl[h] * dsn[h] + dsq[h]
        dqd = [x1[h][:CHUNK] for h in HS]
        duw = [jnp.concatenate([dvn[h], -x1[h][CHUNK:]], axis=1) for h in HS]
        tdu = [_mm_tn(t_ref[0, h], duw[h]) for h in HS]
        dvk = [duw[h] + tdu[h] for h in HS]
        uw = [jnp.concatenate([u_ref[:, cs[h]], w[h]], axis=1) for h in HS]
        da = [-jnp.where(ii > jj, _mm_nt(dvk[h], uw[h]), 0.0) for h in HS]
        da_t = [-jnp.where(jj > ii, _mm_nt(uw[h], dvk[h]), 0.0) for h in HS]
        dp = [jnp.where(ii >= jj, dov[h], 0.0) for h in HS]
        dp_t = [jnp.where(jj >= ii, vdo[h], 0.0) for h in HS]
        r1 = [_mm(jnp.concatenate([da[h] * dmat[h], dp[h] * dmat[h]], axis=0), k[h]) for h in HS]
        dk1 = [_mm(jnp.concatenate([da_t[h] * dmat_t[h], dp_t[h] * dmat_t[h]], axis=1),
                   jnp.concatenate([kb[h], q[h]], axis=0)) for h in HS]
        dsc = jnp.zeros(sc_v.shape, f32)
        for h in HS:
            a = jnp.where(ii > jj, kk[h] * dmat[h], 0.0)
            p = jnp.where(ii >= jj, qk[h] * dmat[h], 0.0)
            gmat = da[h] * a + dp[h] * p
            dvb, dkbg = dvk[h][:, :HEAD_DIM], dvk[h][:, HEAD_DIM:]
            kbg = kb[h] * eg[h]
            dkb = r1[h][:CHUNK] + dkbg * eg[h]
            dq = r1[h][CHUNK:] + dqd[h] * eg[h]
            dk = dk1[h] + dks[h] * ekl[h] + dkb * beta[h]
            dbeta = jnp.sum(dkb * k[h] + dvb * v[h], axis=1, keepdims=True)
            ksum = jnp.sum(dks[h] * ks[h], axis=1, keepdims=True)
            dgl_tot = dgl[h] + jnp.sum(ksum, axis=0, keepdims=True)
            dgc = jnp.sum(gmat, axis=1, keepdims=True) + jnp.sum(dqd[h] * qd[h] + dkbg * kbg, axis=1, keepdims=True) - ksum
            dgc = dgc + jnp.where(_rows(dgc.shape) == CHUNK - 1, dgl_tot, 0.0)
            dqkv_ref[:, h * HEAD_DIM:(h + 1) * HEAD_DIM] = dq
            dqkv_ref[:, W + h * HEAD_DIM:W + (h + 1) * HEAD_DIM] = dk
            dqkv_ref[:, 2 * W + h * HEAD_DIM:2 * W + (h + 1) * HEAD_DIM] = dvb * beta[h]
            dsc = jnp.where(lane == h, dbeta, jnp.where(lane == HEADS + h, dgc, dsc))
            dgr_ref[0, h:h + 1, :] = jnp.sum(gmat, axis=0, keepdims=True)
        dsc_ref[...] = dsc

    row = lambda c: (nc - 1 - c, 0)
    return _pcall(
        body, name="gdn_bwd", grid=(nc,),
        in_specs=[pl.BlockSpec((CHUNK, 3 * W), row), pl.BlockSpec((CHUNK, LANE), row),
                  pl.BlockSpec((1, HEADS, CHUNK), lambda c: (nc - 1 - c, 0, 0)),
                  pl.BlockSpec((CHUNK, W), row), pl.BlockSpec((CHUNK, W), row), pl.BlockSpec((CHUNK, W), row),
                  pl.BlockSpec((1, HEADS, CHUNK, CHUNK), lambda c: (nc - 1 - c, 0, 0, 0)),
                  pl.BlockSpec((1, W, HEAD_DIM), lambda c: (nc - 1 - c, 0, 0)), pl.BlockSpec((CHUNK, W), row)],
        out_specs=[pl.BlockSpec((CHUNK, 3 * W), row), pl.BlockSpec((CHUNK, LANE), row),
                   pl.BlockSpec((1, HEADS, CHUNK), lambda c: (nc - 1 - c, 0, 0))],
        out_shape=[jax.ShapeDtypeStruct((L, 3 * W), f32), jax.ShapeDtypeStruct((L, LANE), f32),
                   jax.ShapeDtypeStruct((nc, HEADS, CHUNK), f32)],
        scratch_shapes=[pltpu.VMEM((HEADS, HEAD_DIM, HEAD_DIM), f32)],
        compiler_params=_cparams("arbitrary"),
    )(qkv, sc, gr, u_all, w_all, vn_all, t_all, sp_all, do_all)


def _qkv_bwd(proj, cw, dn):
    L = proj.shape[0]

    def body(x_ref, cw_ref, dn_ref, dx_ref, gcw_ref):
        j = pl.program_id(0)
        x, dn_v = x_ref[...], dn_ref[...]
        c = _conv4(x, cw_ref)
        sg = _sigmoid(c)
        a = c * sg
        rn = lax.rsqrt(jnp.sum(a * a, axis=1, keepdims=True) + EPS)
        scale = jnp.where(j < HEADS, HEAD_DIM ** -0.5, 1.0).astype(f32)
        da_n = (scale * rn) * (dn_v - a * ((rn * rn) * jnp.sum(dn_v * a, axis=1, keepdims=True)))
        da = jnp.where(j < 2 * HEADS, da_n, dn_v)
        dc = da * (sg * (1.0 + c * (1.0 - sg)))
        gcw_ref[3:4, :] = jnp.sum(dc * x, axis=0, keepdims=True)
        gcw_ref[2:3, :] = jnp.sum(dc * _shift_down(x, 1), axis=0, keepdims=True)
        gcw_ref[1:2, :] = jnp.sum(dc * _shift_down(x, 2), axis=0, keepdims=True)
        gcw_ref[0:1, :] = jnp.sum(dc * _shift_down(x, 3), axis=0, keepdims=True)
        dx = (cw_ref[3:4, :] * dc + cw_ref[2:3, :] * _shift_up(dc, 1) + cw_ref[1:2, :] * _shift_up(dc, 2)
              + cw_ref[0:1, :] * _shift_up(dc, 3))
        dx_ref[...] = dx.astype(bf16)

    col = pl.BlockSpec((L, LANE), lambda j: (0, j))
    wspec = pl.BlockSpec((4, LANE), lambda j: (0, j))
    return _pcall(
        body, name="qkv_bwd", grid=(3 * HEADS,),
        in_specs=[col, wspec, col], out_specs=[col, wspec],
        out_shape=[jax.ShapeDtypeStruct((L, 3 * GDN_WIDTH), bf16), jax.ShapeDtypeStruct((4, 3 * GDN_WIDTH), f32)],
        compiler_params=_cparams("parallel"),
    )(proj, cw, dn)


def _scalars_bwd(proj, alog_p, dtb_p, dsc, dgr_col):
    L = proj.shape[0]

    def body(x_ref, al_ref, dt_ref, dsc_ref, dgr_ref, dba_ref, gs_ref):
        x, dsc_v = x_ref[...], dsc_ref[...]
        lane = _lanes(x.shape)
        dec = (lane >= HEADS) & (lane < 2 * HEADS)
        dg = jnp.where(dec, dsc_v - dgr_ref[...], 0.0)
        rc = _rows(x.shape) & (CHUNK - 1)
        for s in (1, 2, 4, 8, 16, 32):
            dg = dg + jnp.where(rc + s < CHUNK, pltpu.roll(dg, L - s, 0), 0.0)
        xa = x + dt_ref[...]
        ea = jnp.exp(al_ref[...])
        g = -ea * _softplus(xa)
        da = dg * (-ea) * _sigmoid(xa)
        beta = _sigmoid(x)
        db = dsc_v * beta * (1.0 - beta)
        dba_ref[...] = jnp.where(lane < HEADS, db, jnp.where(dec, da, 0.0)).astype(bf16)
        g_al = jnp.sum(jnp.where(dec, dg * g, 0.0), axis=0, keepdims=True)
        g_dt = jnp.sum(jnp.where(dec, da, 0.0), axis=0, keepdims=True)
        row8 = _rows(gs_ref.shape)
        gs = jnp.where(row8 == 0, g_al, jnp.where(row8 == 1, g_dt, 0.0))
        gs_ref[...] = pltpu.roll(gs, LANE - HEADS, 1)

    full = pl.BlockSpec((L, LANE), lambda i: (0, 0))
    vec = pl.BlockSpec((1, LANE), lambda i: (0, 0))
    return _pcall(
        body, name="scalars_bwd", grid=(1,),
        in_specs=[pl.BlockSpec((L, LANE), lambda i: (0, OFF_BA // LANE)), vec, vec, full, full],
        out_specs=[full, pl.BlockSpec((8, LANE), lambda i: (0, 0))],
        out_shape=[jax.ShapeDtypeStruct((L, LANE), bf16), jax.ShapeDtypeStruct((8, LANE), f32)],
        compiler_params=_cparams("arbitrary"),
    )(proj, alog_p, dtb_p, dsc, dgr_col)


def _input_grad(pieces, offs, wpad, x, nw, dy):
    L = x.shape[0]
    tm = min(512, L)
    npc = len(pieces)

    def body(*refs):
        p_refs = refs[:npc]
        w_hbm, x_ref, nw_ref, dy_ref, gx_ref, gnw_ref, w_vmem, sem = refs[npc:]

        @pl.when(pl.program_id(0) == 0)
        def _():
            cp = pltpu.make_async_copy(w_hbm, w_vmem, sem)
            cp.start()
            cp.wait()
            gnw_ref[...] = jnp.zeros_like(gnw_ref)
        dh = None
        for p_ref, off in zip(p_refs, offs):
            wd = p_ref.shape[1]
            part = jnp.dot(p_ref[...], w_vmem[off:off + wd, :], preferred_element_type=f32)
            dh = part if dh is None else dh + part
        xv, nwv = x_ref[...], nw_ref[...]
        r = lax.rsqrt(jnp.mean(xv * xv, axis=-1, keepdims=True) + EPS)
        xh = xv * r
        gnw_ref[...] += jnp.sum(dh * xh, axis=0, keepdims=True)
        dxh = dh * nwv
        gx_ref[...] = dy_ref[...] + r * (dxh - xh * jnp.mean(dxh * xh, axis=-1, keepdims=True))

    row = lambda i: (i, 0)
    fix = lambda i: (0, 0)
    return _pcall(
        body, name="input_grad", grid=(L // tm,),
        in_specs=[pl.BlockSpec((tm, p.shape[1]), row) for p in pieces] + [
            ANY, pl.BlockSpec((tm, D_MODEL), row), pl.BlockSpec((1, D_MODEL), fix), pl.BlockSpec((tm, D_MODEL), row)],
        out_specs=[pl.BlockSpec((tm, D_MODEL), row), pl.BlockSpec((1, D_MODEL), fix)],
        out_shape=[jax.ShapeDtypeStruct((L, D_MODEL), f32), jax.ShapeDtypeStruct((1, D_MODEL), f32)],
        scratch_shapes=[pltpu.VMEM(wpad.shape, bf16), pltpu.SemaphoreType.DMA(())],
        compiler_params=_cparams("arbitrary"),
    )(*pieces, wpad, x, nw, dy)


def _adamw_reduce(parts, w, m, v, name):
    R, C = w.shape
    n_parts = parts.shape[0]
    tr = 128 if R % 128 == 0 else R
    c1 = 1.0 - ADAM_B1 ** ADAM_STEP
    c2 = 1.0 - ADAM_B2 ** ADAM_STEP

    def body(p_ref, w_ref, m_ref, v_ref, g_ref, d_ref, nm_ref, nv_ref):
        g = p_ref[0].astype(f32)
        for s in range(1, n_parts):
            g = g + p_ref[s].astype(f32)
        nm = ADAM_B1 * m_ref[...] + (1.0 - ADAM_B1) * g
        nv = ADAM_B2 * v_ref[...] + (1.0 - ADAM_B2) * (g * g)
        g_ref[...] = g
        nm_ref[...] = nm
        nv_ref[...] = nv
        d_ref[...] = -ADAM_LR * ((nm / c1) / (jnp.sqrt(nv / c2) + ADAM_EPS) + ADAM_WD * w_ref[...])

    blk = pl.BlockSpec((tr, C), lambda i: (i, 0))
    out = jax.ShapeDtypeStruct((R, C), f32)
    return _pcall(
        body, name=name, grid=(R // tr,),
        in_specs=[pl.BlockSpec((n_parts, tr, C), lambda i: (0, i, 0)), blk, blk, blk],
        out_specs=[blk] * 4, out_shape=[out] * 4,
        compiler_params=_cparams("parallel"),
    )(parts, w, m, v)


SMALL_SLOTS = ((0, D_MODEL), (D_MODEL, D_MODEL), (2 * D_MODEL, D_MODEL), (3 * D_MODEL, LANE),
               (3 * D_MODEL + LANE, HEADS), (3 * D_MODEL + 2 * LANE, HEADS))
SMALL_LOSS = 3 * D_MODEL + 3 * LANE
SMALL_W = SMALL_LOSS + LANE


def _pack_small(gs, after):
    def body(nw_ref, cb_ref, fw_ref, gn_ref, sc_ref, ls_ref, after_ref, o_ref):
        for ref, (start, width) in zip((nw_ref, cb_ref, fw_ref, gn_ref), SMALL_SLOTS[:4]):
            o_ref[:, start:start + width] = ref[...]
        o_ref[:, SMALL_SLOTS[4][0]:SMALL_SLOTS[4][0] + LANE] = sc_ref[0:1, :]
        o_ref[:, SMALL_SLOTS[5][0]:SMALL_SLOTS[5][0] + LANE] = sc_ref[1:2, :]
        o_ref[:, SMALL_LOSS:SMALL_W] = ls_ref[...]

    vm = pl.BlockSpec(memory_space=pltpu.VMEM)
    return _pcall(body, name="pack_small_grads", out_shape=jax.ShapeDtypeStruct((1, SMALL_W), f32),
                  in_specs=[vm] * 6 + [ANY], out_specs=vm)(*gs, after)


def _adamw_small(parts, ws, ms, vs):
    c1 = 1.0 - ADAM_B1 ** ADAM_STEP
    c2 = 1.0 - ADAM_B2 ** ADAM_STEP
    np_ = len(ws)

    def body(*refs):
        p_ref = refs[0]
        w_refs, m_refs, v_refs = refs[1:1 + np_], refs[1 + np_:1 + 2 * np_], refs[1 + 2 * np_:1 + 3 * np_]
        outs = refs[1 + 3 * np_:]
        g_refs, d_refs, nm_refs, nv_refs = (outs[i * np_:(i + 1) * np_] for i in range(4))
        loss_ref = outs[4 * np_]

        def total(start, width):
            t = p_ref[0, :, start:start + width]
            for s in range(1, N_DEV):
                t = t + p_ref[s, :, start:start + width]
            return t

        for i, (start, width) in enumerate(SMALL_SLOTS):
            g = total(start, width)
            nm = ADAM_B1 * m_refs[i][...] + (1.0 - ADAM_B1) * g
            nv = ADAM_B2 * v_refs[i][...] + (1.0 - ADAM_B2) * (g * g)
            g_refs[i][...] = g
            nm_refs[i][...] = nm
            nv_refs[i][...] = nv
            d_refs[i][...] = -ADAM_LR * ((nm / c1) / (jnp.sqrt(nv / c2) + ADAM_EPS) + ADAM_WD * w_refs[i][...])
        loss_ref[...] = total(SMALL_LOSS, LANE)

    vm = pl.BlockSpec(memory_space=pltpu.VMEM)
    shapes = [jax.ShapeDtypeStruct(w.shape, f32) for w in ws]
    res = _pcall(body, name="adamw_small", out_shape=shapes * 4 + [jax.ShapeDtypeStruct((1, LANE), f32)],
                 in_specs=[vm] * (1 + 3 * np_), out_specs=[vm] * (4 * np_ + 1))(parts, *ws, *ms, *vs)
    return [res[i * np_:(i + 1) * np_] for i in range(4)], res[4 * np_]


def _adamw_w_in(parts, w3, m3, v3):
    n_parts, n, _ = parts.shape
    c1 = 1.0 - ADAM_B1 ** ADAM_STEP
    c2 = 1.0 - ADAM_B2 ** ADAM_STEP

    def body(p_ref, w_ref, m_ref, v_ref, g_ref, d_ref, nm_ref, nv_ref):
        g = p_ref[0].astype(f32)
        for s in range(1, n_parts):
            g = g + p_ref[s].astype(f32)
        nm = ADAM_B1 * m_ref[:, 0, :] + (1.0 - ADAM_B1) * g
        nv = ADAM_B2 * v_ref[:, 0, :] + (1.0 - ADAM_B2) * (g * g)
        g_ref[:, 0, :] = g
        nm_ref[:, 0, :] = nm
        nv_ref[:, 0, :] = nv
        d_ref[:, 0, :] = -ADAM_LR * ((nm / c1) / (jnp.sqrt(nv / c2) + ADAM_EPS) + ADAM_WD * w_ref[:, 0, :])

    tile = 2 * COL_TILE
    blk = pl.BlockSpec((n, 1, tile), lambda j: (0, 0, j))
    out = jax.ShapeDtypeStruct((n, 1, D_MODEL), f32)
    return _pcall(
        body, name="adamw_w_in", grid=(D_MODEL // tile,),
        in_specs=[pl.BlockSpec((n_parts, n, tile), lambda j: (0, 0, j)), blk, blk, blk],
        out_specs=[blk] * 4, out_shape=[out] * 4,
        compiler_params=_cparams("parallel"),
    )(parts, w3, m3, v3)


def _pad_lanes(vec8, start):
    return jnp.pad(vec8.reshape(1, -1), ((0, 0), (start, LANE - start - vec8.size)))


def kernel(x, norm_in_w, w_in, conv_qkv_w, A_log, dt_bias, gdn_norm_w, conv_w, conv_b, w_out, final_norm_w, loss_target, m_norm_in_w, m_w_in, m_conv_qkv_w, m_A_log, m_dt_bias, m_gdn_norm_w, m_conv_w, m_conv_b, m_w_out, m_final_norm_w, v_norm_in_w, v_w_in, v_conv_qkv_w, v_A_log, v_dt_bias, v_gdn_norm_w, v_conv_w, v_conv_b, v_w_out, v_final_norm_w):
    L = x.shape[1]
    nc = L // CHUNK
    xs = x[0]
    tgt = loss_target[0]
    fnw = final_norm_w.reshape(1, D_MODEL)

    as_rows = lambda a: jnp.transpose(a, (2, 0, 1))
    win_g, cqkv_g, cw_g = _all_gather([_cast_w_in(as_rows(w_in)), conv_qkv_w[0], conv_w[0]], "gather_weights")
    wpad = _relayout_w_in(win_g)
    cqkv = jnp.concatenate([cqkv_g[d] for d in range(N_DEV)], axis=1)
    cw = jnp.concatenate([cw_g[d] for d in range(N_DEV)], axis=1)
    alog_p = _pad_lanes(A_log, HEADS)
    dtb_p = _pad_lanes(dt_bias, HEADS)
    me_flat, me_chip = _flat(*_mesh_pos()), 2 * lax.axis_index("x") + lax.axis_index("y")
    tok = lambda started: started[4][0:1, 0:1]
    wo_own = w_out[0].astype(bf16)
    wo_started = _spread_start(wo_own, wpad, "gather", "gather_w_out_start")

    proj, h = _in_proj(xs, norm_in_w + tok(wo_started), wpad)
    qkv = _qkv_act(proj, cqkv)
    sc, gr = _scalars(proj, alog_p, dtb_p)
    o, u_all, w_all, vn_all, t_all, sp_all = _gdn_fwd(qkv, sc, gr)
    mix_a = _gdn_gate(o, proj, gdn_norm_w)
    mix_b = _conv_fwd(proj, cw, conv_b)
    wo = _own_slot(_spread_wait(wo_started, mix_b, "gather", "gather_w_out_wait"), wo_own, me_flat).reshape(-1, D_MODEL)
    dy, dyb, dmix_a, dmix_b, g_fnw, loss_v = _out_proj_loss(xs, mix_a, mix_b, wo, fnw, tgt)

    g_wout = jnp.concatenate([_tn_matmul(mix_a, dyb, "grad_w_out_a"), _tn_matmul(mix_b, dyb, "grad_w_out_b")], axis=0)
    g_wout = g_wout.reshape(N_DEV, -1, D_MODEL)
    g_wout_own = lax.dynamic_index_in_dim(g_wout, me_flat, 0, keepdims=False)
    gwo_started = _spread_start(g_wout, dyb, "scatter", "exchange_grad_w_out_start")
    do, dzg, g_gnw = _gdn_gate_bwd(o, proj, gdn_norm_w + tok(gwo_started), dmix_a)
    d_b, d_c, d_hc, d_zc, g_cw, g_cb = _conv_bwd(proj, cw, conv_b, dmix_b)
    dqkv_n, dsc, dgr = _gdn_bwd(qkv, sc, gr, u_all, w_all, vn_all, t_all, sp_all, do)
    dqkv, g_cqkv = _qkv_bwd(proj, cqkv, dqkv_n)
    dgr_col = jnp.pad(dgr.transpose(0, 2, 1).reshape(L, HEADS), ((0, 0), (HEADS, LANE - 2 * HEADS)))
    dba, g_sc = _scalars_bwd(proj, alog_p, dtb_p, dsc, dgr_col)
    pieces = [dqkv, dzg, dba, d_b, d_c, d_hc, d_zc]
    offs = [OFF_QKV, OFF_ZG, OFF_BA, OFF_B, OFF_C, OFF_HC, OFF_ZC]
    g_parts = [_tn_matmul(p, h, "grad_w_in_%d" % i) for i, p in enumerate(pieces)]
    g_win_blk = _grad_blocks(g_parts)

    (p_win,) = _pair_exchange([g_win_blk], "exchange_grads_pair")
    s_win = _pair_sum(g_win_blk, p_win, "pair_sum_w_in")
    s_win_own = lax.dynamic_index_in_dim(s_win, me_chip, 0, keepdims=False)
    gwi_started = _spread_start(s_win, s_win_own, "chips", "exchange_grads_chips_start")
    grad_x, g_nw = _input_grad(pieces, offs, wpad, xs, norm_in_w + tok(gwi_started), dy)

    r_wout = _own_slot(_spread_wait(gwo_started, grad_x, "scatter", "exchange_grad_w_out_wait"), g_wout_own, me_flat)
    r_cqkv, r_cw = _all_to_all(
        [g_cqkv.reshape(4, N_DEV, -1).transpose(1, 0, 2), g_cw.reshape(3, N_DEV, -1).transpose(1, 0, 2)],
        "exchange_small_sharded_grads")
    upd_wout =_adamw_reduce(r_wout, w_out[0], m_w_out[0], v_w_out[0], "adamw_w_out")
    upd_cqkv = _adamw_reduce(r_cqkv, conv_qkv_w[0], m_conv_qkv_w[0], v_conv_qkv_w[0], "adamw_conv_qkv_w")
    upd_cw = _adamw_reduce(r_cw, conv_w[0], m_conv_w[0], v_conv_w[0], "adamw_conv_w")

    r_win = _own_slot(_spread_wait(gwi_started, upd_cw[0], "chips", "exchange_grads_chips_wait"), s_win_own, me_chip)
    upd_win = [jnp.transpose(a, (1, 2, 0)) for a in _adamw_w_in(r_win, as_rows(w_in), as_rows(m_w_in), as_rows(v_w_in))]

    small_g = _pack_small([g_nw, g_cb, g_fnw, g_gnw, g_sc, loss_v], r_win)
    (small_all,) = _all_gather([small_g], "gather_small_grads")
    fvec = lambda a: a.reshape(1, D_MODEL)
    upd_small, loss_sum = _adamw_small(
        small_all,
        [norm_in_w, conv_b, fvec(final_norm_w), gdn_norm_w, A_log, dt_bias],
        [m_norm_in_w, m_conv_b, fvec(m_final_norm_w), m_gdn_norm_w, m_A_log, m_dt_bias],
        [v_norm_in_w, v_conv_b, fvec(v_final_norm_w), v_gdn_norm_w, v_A_log, v_dt_bias])

    outs = [loss_sum[0, 0], grad_x[None]]
    for k in range(4):
        nw_k, cb_k, fw_k, gn_k, al_k, dt_k = upd_small[k]
        outs += [nw_k, upd_win[k], upd_cqkv[k][None], al_k, dt_k, gn_k,
                 upd_cw[k][None], cb_k, upd_wout[k][None], fw_k.reshape(D_MODEL)]
    return tuple(outs)
```

```python
import functools
import math

import jax
import jax.numpy as jnp
from jax import lax
from jax.experimental import pallas as pl
from jax.experimental.pallas import tpu as pltpu

f32 = jnp.float32
bf16 = jnp.bfloat16

N_DEV = 8
D_MODEL = 1024
HEADS = 8
HEAD_DIM = 128
CHUNK = 64
GDN_CPS = 4
GDN_CPS_BWD = 1
GDN_WIDTH = HEADS * HEAD_DIM
CONV_WIDTH = 1024
PROJ_WIDTH = 8208
SHARD_W = PROJ_WIDTH // N_DEV
EPS = 1e-6

NAT_SMALL_END = 4112
PAD_COLS = 112
OFF_QKV, OFF_ZG, OFF_BA, OFF_B, OFF_C, OFF_HC, OFF_ZC = 0, 3072, 4096, 4224, 5248, 6272, 7296
PROJ_PAD = 8320
LANE = 128

ADAM_LR, ADAM_B1, ADAM_B2, ADAM_EPS, ADAM_WD, ADAM_STEP = 0.001, 0.9, 0.999, 1e-08, 0.01, 10

VMEM_LIMIT = 56 * 1024 * 1024

MESH = pl.DeviceIdType.MESH
ANY = pl.BlockSpec(memory_space=pl.ANY)


def _pcall(body, **kw):
    return pl.pallas_call(body, **kw)


def _cparams(*sem):
    return pltpu.CompilerParams(dimension_semantics=sem if sem else None, vmem_limit_bytes=VMEM_LIMIT)


def _mm(a, b):
    return jnp.dot(a.astype(bf16), b.astype(bf16), preferred_element_type=f32)


def _mm_nt(a, b):
    return lax.dot_general(a.astype(bf16), b.astype(bf16), (((1,), (1,)), ((), ())), preferred_element_type=f32)


def _mm_tn(a, b):
    return lax.dot_general(a.astype(bf16), b.astype(bf16), (((0,), (0,)), ((), ())), preferred_element_type=f32)


def _rows(shape):
    return lax.broadcasted_iota(jnp.int32, shape, 0)


def _lanes(shape):
    return lax.broadcasted_iota(jnp.int32, shape, 1)


def _shift_down(x, s):
    if s == 0:
        return x
    return jnp.where(_rows(x.shape) >= s, pltpu.roll(x, s, 0), 0.0)


def _shift_up(x, s):
    if s == 0:
        return x
    n = x.shape[0]
    return jnp.where(_rows(x.shape) < n - s, pltpu.roll(x, n - s, 0), 0.0)


def _sigmoid(x):
    return jax.nn.sigmoid(x)


def _softplus(x):
    e = jnp.exp(-jnp.abs(x))
    small = e * (1.0 - e * (0.5 - e * (1.0 / 3.0)))
    return jnp.maximum(x, 0.0) + jnp.where(e < 0.01, small, jnp.log(1.0 + e))


def _mesh_pos():
    return lax.axis_index("x"), lax.axis_index("y"), lax.axis_index("c")


def _flat(px, py, pc):
    return 4 * px + 2 * py + pc


def _all_gather(xs, name):
    n = len(xs)

    def body(*refs):
        x_refs, o_refs = refs[:n], refs[n:2 * n]
        send_sems, recv_sems, local_sems = refs[2 * n:]
        x, y, c = _mesh_pos()
        me, sibling = (x, y, c), (x, y, 1 - c)
        chips = [(1 - x, y), (x, 1 - y), (1 - x, 1 - y)]

        def copy(a, k, block, to, src=None):
            dst = o_refs[a].at[_flat(*block)]
            return pltpu.make_async_remote_copy(
                src_ref=dst if src is None else src, dst_ref=dst,
                send_sem=send_sems.at[a, k], recv_sem=recv_sems.at[a, k], device_id=to, device_id_type=MESH)

        mine, first, passed = [], [], []
        for a in range(n):
            cp = pltpu.make_async_copy(x_refs[a], o_refs[a].at[_flat(*me)], local_sems.at[a])
            cp.start()
            mine.append(cp)
            fa = [copy(a, 0, me, sibling, src=x_refs[a])]
            fa += [copy(a, 1 + j, me, (*chip, c), src=x_refs[a]) for j, chip in enumerate(chips)]
            for cp in fa:
                cp.start()
            first += fa
        for a in range(n):
            for j, chip in enumerate(chips):
                copy(a, 1 + j, (*chip, c), me).wait_recv()
                cp = copy(a, 4 + j, (*chip, c), sibling)
                cp.start()
                passed.append(cp)
        for a in range(n):
            copy(a, 0, sibling, me).wait_recv()
            for j, chip in enumerate(chips):
                copy(a, 4 + j, (*chip, 1 - c), me).wait_recv()
        for cp in first + passed:
            cp.wait_send()
        for cp in mine:
            cp.wait()

    outs = _pcall(
        body, name=name,
        out_shape=[jax.ShapeDtypeStruct((N_DEV,) + a.shape, a.dtype) for a in xs],
        in_specs=[ANY] * n, out_specs=[ANY] * n,
        scratch_shapes=[pltpu.SemaphoreType.DMA((n, 7)), pltpu.SemaphoreType.DMA((n, 7)), pltpu.SemaphoreType.DMA((n,))],
    )(*xs)
    return list(outs)


def _all_to_all(gs, name):
    n = len(gs)

    def body(*refs):
        g_refs, o_refs = refs[:n], refs[n:2 * n]
        send_sems, recv_sems, local_sems = refs[2 * n:]
        x, y, c = _mesh_pos()
        me = _flat(x, y, c)
        peers = []
        for k in range(1, N_DEV):
            kx, ky, kc = (k >> 2) & 1, (k >> 1) & 1, k & 1
            px = (1 - x) if kx else x
            py = (1 - y) if ky else y
            pc = (1 - c) if kc else c
            peers.append((px, py, pc))

        def copy(a, k):
            peer = peers[k - 1]
            return pltpu.make_async_remote_copy(
                src_ref=g_refs[a].at[_flat(*peer)], dst_ref=o_refs[a].at[me],
                send_sem=send_sems.at[a, k - 1], recv_sem=recv_sems.at[a, k - 1], device_id=peer, device_id_type=MESH)

        def arrival(a, k):
            peer = peers[k - 1]
            return pltpu.make_async_remote_copy(
                src_ref=g_refs[a].at[me], dst_ref=o_refs[a].at[_flat(*peer)],
                send_sem=send_sems.at[a, k - 1], recv_sem=recv_sems.at[a, k - 1], device_id=peer, device_id_type=MESH)

        mine, sent = [], []
        for a in range(n):
            cp = pltpu.make_async_copy(g_refs[a].at[me], o_refs[a].at[me], local_sems.at[a])
            cp.start()
            mine.append(cp)
            for k in range(1, N_DEV):
                cp = copy(a, k)
                cp.start()
                sent.append(cp)
        for a in range(n):
            for k in range(1, N_DEV):
                arrival(a, k).wait_recv()
        for cp in sent:
            cp.wait_send()
        for cp in mine:
            cp.wait()

    outs = _pcall(
        body, name=name,
        out_shape=[jax.ShapeDtypeStruct(a.shape, a.dtype) for a in gs],
        in_specs=[ANY] * n, out_specs=[ANY] * n,
        scratch_shapes=[pltpu.SemaphoreType.DMA((n, 7)), pltpu.SemaphoreType.DMA((n, 7)), pltpu.SemaphoreType.DMA((n,))],
    )(*gs)
    return list(outs)


def _pair_exchange(gs, name):
    n = len(gs)
    chips = [(0, 0), (0, 1), (1, 0), (1, 1)]

    def body(*refs):
        g_refs, o_refs = refs[:n], refs[n:2 * n]
        send_sems, recv_sems = refs[2 * n:]
        x, y, c = _mesh_pos()
        sibling = (x, y, 1 - c)

        def copy(a, i):
            xp, yp = chips[i]
            return pltpu.make_async_remote_copy(
                src_ref=g_refs[a].at[_flat(xp, yp, 1 - c)], dst_ref=o_refs[a].at[i],
                send_sem=send_sems.at[a, i], recv_sem=recv_sems.at[a, i], device_id=sibling, device_id_type=MESH)

        cps = [copy(a, i) for a in range(n) for i in range(4)]
        for cp in cps:
            cp.start()
        for cp in cps:
            cp.wait()

    outs = _pcall(
        body, name=name,
        out_shape=[jax.ShapeDtypeStruct((4,) + a.shape[1:], a.dtype) for a in gs],
        in_specs=[ANY] * n, out_specs=[ANY] * n,
        scratch_shapes=[pltpu.SemaphoreType.DMA((n, 4)), pltpu.SemaphoreType.DMA((n, 4))],
    )(*gs)
    return list(outs)


def _pair_sum(g, p1, name):
    _, R, C = g.shape
    tr = 256 if R % 256 == 0 else R
    cidx = lax.axis_index("c").astype(jnp.int32).reshape(1)

    def body(c_ref, g_ref, p_ref, o_ref):
        o_ref[...] = (g_ref[...].astype(f32) + p_ref[...].astype(f32)).astype(o_ref.dtype)

    return _pcall(
        body, name=name,
        grid_spec=pltpu.PrefetchScalarGridSpec(
            num_scalar_prefetch=1, grid=(4, R // tr),
            in_specs=[pl.BlockSpec((1, tr, C), lambda i, r, c_ref: (2 * i + c_ref[0], r, 0)),
                      pl.BlockSpec((1, tr, C), lambda i, r, c_ref: (i, r, 0))],
            out_specs=pl.BlockSpec((1, tr, C), lambda i, r, c_ref: (i, r, 0))),
        out_shape=jax.ShapeDtypeStruct((4, R, C), g.dtype),
        compiler_params=_cparams("parallel", "parallel"),
    )(cidx, g, p1)


HBM = pl.BlockSpec(memory_space=pltpu.HBM)
SEM = pl.BlockSpec(memory_space=pltpu.SEMAPHORE)
EFFECT = pltpu.SideEffectType.DATAFLOW_SIDE_EFFECTING


def _peers(x, y, c):
    out = []
    for k in range(1, N_DEV):
        kx, ky, kc = (k >> 2) & 1, (k >> 1) & 1, k & 1
        out.append(((1 - x) if kx else x, (1 - y) if ky else y, (1 - c) if kc else c))
    return out


SPREAD_COPIES = {"gather": N_DEV - 1, "scatter": N_DEV - 1, "chips": 3}


def _spread_copy(src_ref, land_ref, send_sems, recv_sems, k, plan):
    x, y, c = _mesh_pos()
    if plan == "chips":
        px, py = [(1 - x, y), (x, 1 - y), (1 - x, 1 - y)][k]
        peer, src, slot = (px, py, c), src_ref.at[2 * px + py], 2 * x + y
    else:
        peer = _peers(x, y, c)[k]
        src, slot = (src_ref.at[_flat(*peer)] if plan == "scatter" else src_ref), _flat(x, y, c)
    return pltpu.make_async_remote_copy(
        src_ref=src, dst_ref=land_ref.at[slot], send_sem=send_sems.at[k], recv_sem=recv_sems.at[k],
        device_id=peer, device_id_type=MESH)


def _spread_start(src, after, plan, name):
    land_shape = (N_DEV,) + src.shape if plan == "gather" else src.shape
    n_copies = SPREAD_COPIES[plan]

    def body(src_ref, land_ref, after_ref, send_sems, recv_sems, src_thru, land_thru, token):
        for k in range(n_copies):
            _spread_copy(src_ref, land_ref, send_sems, recv_sems, k, plan).start()
        token[...] = jnp.zeros_like(token)

    return _pcall(
        body, name=name,
        out_shape=(pltpu.SemaphoreType.DMA((n_copies,)), pltpu.SemaphoreType.DMA((n_copies,)),
                   pltpu.HBM(src.shape, src.dtype), pltpu.HBM(land_shape, src.dtype), jax.ShapeDtypeStruct((8, LANE), f32)),
        in_specs=(HBM, HBM, ANY), out_specs=(SEM, SEM, HBM, HBM, pl.BlockSpec(memory_space=pltpu.VMEM)),
        input_output_aliases={0: 2, 1: 3},
        compiler_params=pltpu.CompilerParams(has_side_effects=EFFECT),
    )(pltpu.with_memory_space_constraint(src, pltpu.HBM),
      pltpu.with_memory_space_constraint(lax.empty(land_shape, src.dtype), pltpu.HBM), after)


def _spread_wait(started, after, plan, name):
    send_sems, recv_sems, src_thru, land_thru, _ = started

    def body(src_ref, land_ref, send_sems, recv_sems, after_ref, src_dead, got_ref):
        for k in range(SPREAD_COPIES[plan]):
            cp = _spread_copy(src_ref, land_ref, send_sems, recv_sems, k, plan)
            cp.wait_send()
            cp.wait_recv()

    return _pcall(
        body, name=name,
        out_shape=(pltpu.HBM(src_thru.shape, src_thru.dtype), pltpu.HBM(land_thru.shape, land_thru.dtype)),
        in_specs=(HBM, HBM, SEM, SEM, ANY), out_specs=(HBM, HBM), input_output_aliases={0: 0, 1: 1},
        compiler_params=pltpu.CompilerParams(has_side_effects=EFFECT),
    )(src_thru, land_thru, send_sems, recv_sems, after)[1]


def _own_slot(land, block, slot):
    zero = jnp.zeros((), jnp.int32)
    return lax.dynamic_update_slice(land, block[None], (slot.astype(jnp.int32),) + (zero,) * block.ndim)


PIECE_NAT = (0, 3072, 4096, 4112, 5136, 6160, 7184, PROJ_WIDTH)


COL_TILE = 256


def _cast_w_in(w3):
    n = w3.shape[0]

    def body(w_ref, o_ref):
        o_ref[...] = w_ref[:, 0, :].astype(bf16)

    return _pcall(
        body, name="cast_w_in", grid=(D_MODEL // COL_TILE,),
        in_specs=[pl.BlockSpec((n, 1, COL_TILE), lambda j: (0, 0, j))],
        out_specs=pl.BlockSpec((n, COL_TILE), lambda j: (0, j)),
        out_shape=jax.ShapeDtypeStruct((n, D_MODEL), bf16),
        compiler_params=_cparams("parallel"),
    )(w3)


def _relayout_w_in(win_g):
    def body(g_ref, o_ref):
        o_ref[NAT_SMALL_END:NAT_SMALL_END + PAD_COLS, :] = jnp.zeros((PAD_COLS, COL_TILE), o_ref.dtype)
        for d in range(N_DEV):
            n0, n1 = d * SHARD_W, (d + 1) * SHARD_W
            cut = min(max(NAT_SMALL_END - n0, 0), SHARD_W)
            if cut > 0:
                o_ref[n0:n0 + cut, :] = g_ref[d, 0:cut, :]
            if cut < SHARD_W:
                o_ref[n0 + cut + PAD_COLS:n1 + PAD_COLS, :] = g_ref[d, cut:SHARD_W, :]

    return _pcall(
        body, name="relayout_w_in", grid=(D_MODEL // COL_TILE,),
        in_specs=[pl.BlockSpec((N_DEV, SHARD_W, COL_TILE), lambda j: (0, 0, j))],
        out_specs=pl.BlockSpec((PROJ_PAD, COL_TILE), lambda j: (0, j)),
        out_shape=jax.ShapeDtypeStruct((PROJ_PAD, D_MODEL), win_g.dtype),
        compiler_params=_cparams("parallel"),
    )(win_g)


def _grad_blocks(g_parts):
    npc = len(g_parts)

    def body(*refs):
        p_refs, o_ref = refs[:npc], refs[npc]
        for d in range(N_DEV):
            n0, n1 = d * SHARD_W, (d + 1) * SHARD_W
            for i in range(npc):
                lo, hi = max(n0, PIECE_NAT[i]), min(n1, PIECE_NAT[i + 1])
                if lo < hi:
                    o_ref[d, lo - n0:hi - n0, :] = p_refs[i][lo - PIECE_NAT[i]:hi - PIECE_NAT[i], :]

    return _pcall(
        body, name="grad_blocks", grid=(D_MODEL // COL_TILE,),
        in_specs=[pl.BlockSpec((p.shape[0], COL_TILE), lambda j: (0, j)) for p in g_parts],
        out_specs=pl.BlockSpec((N_DEV, SHARD_W, COL_TILE), lambda j: (0, 0, j)),
        out_shape=jax.ShapeDtypeStruct((N_DEV, SHARD_W, D_MODEL), bf16),
        compiler_params=_cparams("parallel"),
    )(*g_parts)


def _in_proj(x, nw, wpad_t):
    L = x.shape[0]
    tn = 640
    nj = wpad_t.shape[0] // tn

    def body(x_ref, nw_ref, w_ref, proj_ref, h_ref):
        @pl.when(pl.program_id(0) == 0)
        def _():
            for r in range(0, L, 256):
                xs = x_ref[r:r + 256, :]
                ms = jnp.mean(xs * xs, axis=-1, keepdims=True)
                h_ref[r:r + 256, :] = ((xs * lax.rsqrt(ms + EPS)) * nw_ref[...]).astype(bf16)
        for r in range(0, L, 512):
            proj_ref[r:r + 512, :] = lax.dot_general(h_ref[r:r + 512, :], w_ref[...], (((1,), (1,)), ((), ())),
                                                     preferred_element_type=f32)

    return _pcall(
        body, name="in_proj", grid=(nj,),
        in_specs=[pl.BlockSpec((L, D_MODEL), lambda j: (0, 0)), pl.BlockSpec((1, D_MODEL), lambda j: (0, 0)),
                  pl.BlockSpec((tn, D_MODEL), lambda j: (j, 0))],
        out_specs=[pl.BlockSpec((L, tn), lambda j: (0, j)), pl.BlockSpec((L, D_MODEL), lambda j: (0, 0))],
        out_shape=[jax.ShapeDtypeStruct((L, wpad_t.shape[0]), f32), jax.ShapeDtypeStruct((L, D_MODEL), bf16)],
        compiler_params=_cparams("arbitrary"),
    )(x, nw, wpad_t)


def _conv4(x, cw_ref):
    return (cw_ref[3:4, :] * x + cw_ref[2:3, :] * _shift_down(x, 1) + cw_ref[1:2, :] * _shift_down(x, 2)
            + cw_ref[0:1, :] * _shift_down(x, 3))


def _qkv_act(proj, cw):
    L = proj.shape[0]

    def body(x_ref, cw_ref, o_ref, c_ref):
        j = pl.program_id(0)
        c = _conv4(x_ref[...], cw_ref)
        c_ref[...] = c
        a = c * _sigmoid(c)
        rn = lax.rsqrt(jnp.sum(a * a, axis=1, keepdims=True) + EPS)
        scale = jnp.where(j < HEADS, HEAD_DIM ** -0.5, 1.0).astype(f32)
        o_ref[...] = jnp.where(j < 2 * HEADS, (a * rn) * scale, a)

    return _pcall(
        body, name="qkv_act", grid=(3 * HEADS,),
        in_specs=[pl.BlockSpec((L, LANE), lambda j: (0, j)), pl.BlockSpec((4, LANE), lambda j: (0, j))],
        out_specs=[pl.BlockSpec((L, LANE), lambda j: (0, j))] * 2,
        out_shape=[jax.ShapeDtypeStruct((L, 3 * GDN_WIDTH), f32)] * 2,
        compiler_params=_cparams("parallel"),
    )(proj, cw)


def _scalars(proj, alog_p, dtb_p):
    L = proj.shape[0]
    nc = L // CHUNK

    def body(x_ref, al_ref, dt_ref, sc_ref, gr_ref):
        x = x_ref[...]
        lane = _lanes(x.shape)
        beta = _sigmoid(x)
        g = -jnp.exp(al_ref[...]) * _softplus(x + dt_ref[...])
        gc = jnp.where((lane >= HEADS) & (lane < 2 * HEADS), g, 0.0)
        rc = _rows(x.shape) & (CHUNK - 1)
        for s in (1, 2, 4, 8, 16, 32):
            gc = gc + jnp.where(rc >= s, pltpu.roll(gc, s, 0), 0.0)
        sc_ref[...] = jnp.where(lane < HEADS, beta, gc)
        sel = (_lanes((HEADS, LANE)) == _rows((HEADS, LANE)) + HEADS).astype(f32)
        for c in range(nc):
            gr_ref[c] = lax.dot_general(sel, sc_ref[c * CHUNK:(c + 1) * CHUNK, :], (((1,), (1,)), ((), ())),
                                        preferred_element_type=f32, precision=lax.Precision.HIGHEST)

    return _pcall(
        body, name="scalars", grid=(1,),
        in_specs=[pl.BlockSpec((L, LANE), lambda i: (0, OFF_BA // LANE)), pl.BlockSpec((1, LANE), lambda i: (0, 0)),
                  pl.BlockSpec((1, LANE), lambda i: (0, 0))],
        out_specs=[pl.BlockSpec((L, LANE), lambda i: (0, 0)), pl.BlockSpec((nc, HEADS, CHUNK), lambda i: (0, 0, 0))],
        out_shape=[jax.ShapeDtypeStruct((L, LANE), f32), jax.ShapeDtypeStruct((nc, HEADS, CHUNK), f32)],
        compiler_params=_cparams("arbitrary"),
    )(proj, alog_p, dtb_p)


def _head_scalars(sc, gr_ref, h, ci=0):
    lane = _lanes(sc.shape)
    beta = jnp.sum(jnp.where(lane == h, sc, 0.0), axis=1, keepdims=True)
    gcc = jnp.sum(jnp.where(lane == HEADS + h, sc, 0.0), axis=1, keepdims=True)
    gcr = gr_ref[ci, h:h + 1, :]
    gl = jnp.sum(jnp.where(_lanes(gcr.shape) == CHUNK - 1, gcr, 0.0), axis=1, keepdims=True)
    ii, jj = _rows((CHUNK, CHUNK)), _lanes((CHUNK, CHUNK))
    dmat = jnp.where(ii >= jj, jnp.exp(jnp.minimum(gcc - gcr, 0.0)), 0.0)
    dmat_t = jnp.where(jj >= ii, jnp.exp(jnp.minimum(gcr - gcc, 0.0)), 0.0)
    return beta, gcc, gl, dmat, dmat_t, ii, jj


def _gdn_fwd(qkv, sc, gr):
    L = qkv.shape[0]
    nc = L // CHUNK
    W = GDN_WIDTH
    cps = GDN_CPS if nc % GDN_CPS == 0 else 1
    rows_per_step = cps * CHUNK

    def body(qkv_ref, sc_ref, gr_ref, o_ref, u_ref, w_ref, vn_ref, t_ref, sp_ref, s_scr):
        @pl.when(pl.program_id(0) == 0)
        def _():
            s_scr[...] = jnp.zeros_like(s_scr)
        HS = range(cps * HEADS)
        hd = [i % HEADS for i in HS]
        rs = [slice((i // HEADS) * CHUNK, (i // HEADS + 1) * CHUNK) for i in HS]
        cs = [slice(hd[i] * HEAD_DIM, (hd[i] + 1) * HEAD_DIM) for i in HS]
        q = [qkv_ref[rs[i], hd[i] * HEAD_DIM:(hd[i] + 1) * HEAD_DIM] for i in HS]
        k = [qkv_ref[rs[i], W + hd[i] * HEAD_DIM:W + (hd[i] + 1) * HEAD_DIM] for i in HS]
        v = [qkv_ref[rs[i], 2 * W + hd[i] * HEAD_DIM:2 * W + (hd[i] + 1) * HEAD_DIM] for i in HS]
        hsc = [_head_scalars(sc_ref[rs[i], :], gr_ref, hd[i], i // HEADS) for i in HS]
        beta, gcc, gl, dmat = ([x[i] for x in hsc] for i in range(4))
        ii, jj = hsc[0][5], hsc[0][6]
        eg = [jnp.exp(gcc[h]) for h in HS]
        kb = [k[h] * beta[h] for h in HS]
        kk = [_mm_nt(kb[h], k[h]) for h in HS]
        qk = [_mm_nt(q[h], k[h]) for h in HS]
        n0 = [-jnp.where(ii > jj, kk[h] * dmat[h], 0.0) for h in HS]
        n1 = [_mm(n0[h], n0[h]) for h in HS]
        n2 = [_mm(n1[h], n1[h]) for h in HS]
        p01 = [n0[h] + n1[h] + _mm(n0[h], n1[h]) for h in HS]
        n3 = [_mm(n2[h], n2[h]) for h in HS]
        n4 = [_mm(n3[h], n3[h]) for h in HS]
        p23 = [n2[h] + n3[h] + _mm(n2[h], n3[h]) for h in HS]
        n5 = [_mm(n4[h], n4[h]) for h in HS]
        p03 = [p01[h] + p23[h] + _mm(p01[h], p23[h]) for h in HS]
        p45 = [n4[h] + n5[h] + _mm(n4[h], n5[h]) for h in HS]
        t = [p03[h] + p45[h] + _mm(p03[h], p45[h]) for h in HS]
        vb = [v[h] * beta[h] for h in HS]
        kbg = [kb[h] * eg[h] for h in HS]
        uw = [_mm(t[h], jnp.concatenate([vb[h], kbg[h]], axis=1)) for h in HS]
        u = [vb[h] + uw[h][:, :HEAD_DIM] for h in HS]
        w = [kbg[h] + uw[h][:, HEAD_DIM:] for h in HS]
        wq = [jnp.concatenate([w[h], q[h] * eg[h]], axis=0) for h in HS]
        p = [jnp.where(ii >= jj, qk[h] * dmat[h], 0.0) for h in HS]
        ks = [k[h] * jnp.exp(gl[h] - gcc[h]) for h in HS]
        s = [s_scr[h] for h in range(HEADS)]
        for ci in range(cps):
            IS = range(ci * HEADS, (ci + 1) * HEADS)
            ws = [_mm(wq[i], s[hd[i]]) for i in IS]
            vn = [u[i] - ws[hd[i]][:CHUNK] for i in IS]
            pv = [_mm(p[i], vn[hd[i]]) for i in IS]
            kv = [_mm_tn(ks[i], vn[hd[i]]) for i in IS]
            for i in IS:
                h = hd[i]
                sp_ref[ci, cs[i], :] = s[h]
                o_ref[rs[i], cs[i]] = ws[h][CHUNK:] + pv[h]
                vn_ref[rs[i], cs[i]] = vn[h]
            s = [jnp.exp(gl[i]) * s[hd[i]] + kv[hd[i]] for i in IS]
        for h in range(HEADS):
            s_scr[h] = s[h]
        for i in HS:
            u_ref[rs[i], cs[i]] = u[i]
            w_ref[rs[i], cs[i]] = w[i]
            t_ref[i // HEADS, hd[i]] = t[i]

    row = lambda c: (c, 0)
    act = jax.ShapeDtypeStruct((L, W), f32)
    return _pcall(
        body, name="gdn_fwd", grid=(nc // cps,),
        in_specs=[pl.BlockSpec((rows_per_step, 3 * W), row), pl.BlockSpec((rows_per_step, LANE), row),
                  pl.BlockSpec((cps, HEADS, CHUNK), lambda c: (c, 0, 0))],
        out_specs=[pl.BlockSpec((rows_per_step, W), row)] * 4 + [
            pl.BlockSpec((cps, HEADS, CHUNK, CHUNK), lambda c: (c, 0, 0, 0)),
            pl.BlockSpec((cps, W, HEAD_DIM), lambda c: (c, 0, 0))],
        out_shape=[act, act, act, act, jax.ShapeDtypeStruct((nc, HEADS, CHUNK, CHUNK), f32),
                   jax.ShapeDtypeStruct((nc, W, HEAD_DIM), f32)],
        scratch_shapes=[pltpu.VMEM((HEADS, HEAD_DIM, HEAD_DIM), f32)],
        compiler_params=_cparams("arbitrary"),
    )(qkv, sc, gr)


def _gdn_gate(o, proj, gnw):
    L = o.shape[0]

    def body(o_ref, z_ref, w_ref, m_ref):
        ov, z = o_ref[...], z_ref[...]
        rms = lax.rsqrt(jnp.mean(ov * ov, axis=-1, keepdims=True) + EPS)
        m_ref[...] = (((ov * rms) * w_ref[...]) * (z * _sigmoid(z))).astype(bf16)

    return _pcall(
        body, name="gdn_gate", grid=(HEADS,),
        in_specs=[pl.BlockSpec((L, LANE), lambda j: (0, j)), pl.BlockSpec((L, LANE), lambda j: (0, OFF_ZG // LANE + j)),
                  pl.BlockSpec((1, LANE), lambda j: (0, 0))],
        out_specs=pl.BlockSpec((L, LANE), lambda j: (0, j)),
        out_shape=jax.ShapeDtypeStruct((L, GDN_WIDTH), bf16),
        compiler_params=_cparams("parallel"),
    )(o, proj, gnw)


def _conv3(u, cw_ref):
    return cw_ref[2:3, :] * u + cw_ref[1:2, :] * _shift_down(u, 1) + cw_ref[0:1, :] * _shift_down(u, 2)


def _conv_specs(L):
    blk = lambda off: pl.BlockSpec((L, LANE), lambda j, off=off: (0, off // LANE + j))
    return [blk(OFF_B), blk(OFF_C), blk(OFF_HC), blk(OFF_ZC),
            pl.BlockSpec((3, LANE), lambda j: (0, j)), pl.BlockSpec((1, LANE), lambda j: (0, j))]


def _conv_fwd(proj, cw, cb):
    L = proj.shape[0]

    def body(b_ref, c_ref, h_ref, z_ref, cw_ref, cb_ref, m_ref, cv_ref):
        z = z_ref[...]
        cv = _conv3(c_ref[...] * h_ref[...], cw_ref) + cb_ref[...]
        cv_ref[...] = cv
        m_ref[...] = ((b_ref[...] * cv) * (z * _sigmoid(z))).astype(bf16)

    col = pl.BlockSpec((L, LANE), lambda j: (0, j))
    return _pcall(
        body, name="conv_fwd", grid=(CONV_WIDTH // LANE,),
        in_specs=_conv_specs(L), out_specs=[col, col],
        out_shape=[jax.ShapeDtypeStruct((L, CONV_WIDTH), bf16), jax.ShapeDtypeStruct((L, CONV_WIDTH), f32)],
        compiler_params=_cparams("parallel"),
    )(proj, proj, proj, proj, cw, cb)


def _out_proj_loss(x, mix_a, mix_b, wo, fw, tgt):
    L = x.shape[0]
    tm = min(256, L)

    def body(x_ref, ma_ref, mb_ref, wo_ref, fw_ref, t_ref, dy_ref, dyb_ref, dma_ref, dmb_ref, gfw_ref, loss_ref):
        @pl.when(pl.program_id(0) == 0)
        def _():
            gfw_ref[...] = jnp.zeros_like(gfw_ref)
            loss_ref[...] = jnp.zeros_like(loss_ref)
        y = x_ref[...] + jnp.dot(ma_ref[...], wo_ref[:GDN_WIDTH, :], preferred_element_type=f32) \
            + jnp.dot(mb_ref[...], wo_ref[GDN_WIDTH:, :], preferred_element_type=f32)
        r = lax.rsqrt(jnp.mean(y * y, axis=-1, keepdims=True) + EPS)
        yh = y * r
        fwv = fw_ref[...]
        diff = yh * fwv - t_ref[...]
        loss_ref[...] += jnp.sum(jnp.sum(diff * diff, axis=-1, keepdims=True), axis=0, keepdims=True) * (0.5 / D_MODEL)
        dout = diff * (1.0 / D_MODEL)
        gfw_ref[...] += jnp.sum(dout * yh, axis=0, keepdims=True)
        dyh = dout * fwv
        dy = r * (dyh - yh * jnp.mean(dyh * yh, axis=-1, keepdims=True))
        dy_ref[...] = dy
        dyb = dy.astype(bf16)
        dyb_ref[...] = dyb
        dma_ref[...] = lax.dot_general(dyb, wo_ref[:GDN_WIDTH, :], (((1,), (1,)), ((), ())), preferred_element_type=f32)
        dmb_ref[...] = lax.dot_general(dyb, wo_ref[GDN_WIDTH:, :], (((1,), (1,)), ((), ())), preferred_element_type=f32)

    row = lambda i: (i, 0)
    fix = lambda i: (0, 0)
    act = jax.ShapeDtypeStruct((L, D_MODEL), f32)
    return _pcall(
        body, name="out_proj_loss", grid=(L // tm,),
        in_specs=[pl.BlockSpec((tm, D_MODEL), row), pl.BlockSpec((tm, GDN_WIDTH), row), pl.BlockSpec((tm, CONV_WIDTH), row),
                  pl.BlockSpec((GDN_WIDTH + CONV_WIDTH, D_MODEL), fix), pl.BlockSpec((1, D_MODEL), fix),
                  pl.BlockSpec((tm, D_MODEL), row)],
        out_specs=[pl.BlockSpec((tm, D_MODEL), row), pl.BlockSpec((tm, D_MODEL), row), pl.BlockSpec((tm, GDN_WIDTH), row),
                   pl.BlockSpec((tm, CONV_WIDTH), row), pl.BlockSpec((1, D_MODEL), fix), pl.BlockSpec((1, LANE), fix)],
        out_shape=[act, jax.ShapeDtypeStruct((L, D_MODEL), bf16), act, act,
                   jax.ShapeDtypeStruct((1, D_MODEL), f32), jax.ShapeDtypeStruct((1, LANE), f32)],
        compiler_params=_cparams("arbitrary"),
    )(x, mix_a, mix_b, wo, fw, tgt)


def _tn_matmul(a, b, name):
    L, M = a.shape
    N = b.shape[1]
    tm = 512 if M % 512 == 0 else M

    def body(a_ref, b_ref, o_ref):
        o_ref[...] = lax.dot_general(a_ref[...], b_ref[...], (((0,), (0,)), ((), ())),
                                     preferred_element_type=f32).astype(o_ref.dtype)

    return _pcall(
        body, name=name, grid=(M // tm,),
        in_specs=[pl.BlockSpec((L, tm), lambda i: (0, i)), pl.BlockSpec((L, N), lambda i: (0, 0))],
        out_specs=pl.BlockSpec((tm, N), lambda i: (i, 0)),
        out_shape=jax.ShapeDtypeStruct((M, N), bf16),
        compiler_params=_cparams("parallel"),
    )(a, b)


def _gdn_gate_bwd(o, proj, gnw, dmix_a):
    L = o.shape[0]

    def body(o_ref, z_ref, w_ref, dm_ref, do_ref, dz_ref, gw_ref):
        @pl.when(pl.program_id(0) == 0)
        def _():
            gw_ref[...] = jnp.zeros_like(gw_ref)
        ov, z, dm, wv = o_ref[...], z_ref[...], dm_ref[...], w_ref[...]
        rms = lax.rsqrt(jnp.mean(ov * ov, axis=-1, keepdims=True) + EPS)
        xh = ov * rms
        sg = _sigmoid(z)
        d_on = dm * (z * sg)
        dz_ref[...] = (dm * (xh * wv) * (sg * (1.0 + z * (1.0 - sg)))).astype(bf16)
        gw_ref[...] += jnp.sum(d_on * xh, axis=0, keepdims=True)
        dxh = d_on * wv
        do_ref[...] = rms * (dxh - xh * jnp.mean(dxh * xh, axis=-1, keepdims=True))

    return _pcall(
        body, name="gdn_gate_bwd", grid=(HEADS,),
        in_specs=[pl.BlockSpec((L, LANE), lambda j: (0, j)), pl.BlockSpec((L, LANE), lambda j: (0, OFF_ZG // LANE + j)),
                  pl.BlockSpec((1, LANE), lambda j: (0, 0)), pl.BlockSpec((L, LANE), lambda j: (0, j))],
        out_specs=[pl.BlockSpec((L, LANE), lambda j: (0, j)), pl.BlockSpec((L, LANE), lambda j: (0, j)),
                   pl.BlockSpec((1, LANE), lambda j: (0, 0))],
        out_shape=[jax.ShapeDtypeStruct((L, GDN_WIDTH), f32), jax.ShapeDtypeStruct((L, GDN_WIDTH), bf16),
                   jax.ShapeDtypeStruct((1, LANE), f32)],
        compiler_params=_cparams("arbitrary"),
    )(o, proj, gnw, dmix_a)


def _conv_bwd(proj, conv_out, cw, cb, dmix_b):
    L = proj.shape[0]

    def body(b_ref, c_ref, h_ref, z_ref, cw_ref, cb_ref, cv_ref, dm_ref, db_ref, dc_ref, dh_ref, dz_ref, gcw_ref, gcb_ref):
        bv, cv_, hv, z, dm, cv = b_ref[...], c_ref[...], h_ref[...], z_ref[...], dm_ref[...], cv_ref[...]
        u = cv_ * hv
        sg = _sigmoid(z)
        sz = z * sg
        db_ref[...] = (dm * cv * sz).astype(bf16)
        dz_ref[...] = (dm * (bv * cv) * (sg * (1.0 + z * (1.0 - sg)))).astype(bf16)
        dcv = dm * bv * sz
        gcb_ref[...] = jnp.sum(dcv, axis=0, keepdims=True)
        dcv1, dcv2 = _shift_up(dcv, 1), _shift_up(dcv, 2)
        gcw_ref[2:3, :] = jnp.sum(dcv * u, axis=0, keepdims=True)
        gcw_ref[1:2, :] = jnp.sum(dcv1 * u, axis=0, keepdims=True)
        gcw_ref[0:1, :] = jnp.sum(dcv2 * u, axis=0, keepdims=True)
        du = cw_ref[2:3, :] * dcv + cw_ref[1:2, :] * dcv1 + cw_ref[0:1, :] * dcv2
        dc_ref[...] = (du * hv).astype(bf16)
        dh_ref[...] = (du * cv_).astype(bf16)

    col = pl.BlockSpec((L, LANE), lambda j: (0, j))
    act = jax.ShapeDtypeStruct((L, CONV_WIDTH), bf16)
    return _pcall(
        body, name="conv_bwd", grid=(CONV_WIDTH // LANE,),
        in_specs=_conv_specs(L) + [col, col],
        out_specs=[col, col, col, col, pl.BlockSpec((3, LANE), lambda j: (0, j)), pl.BlockSpec((1, LANE), lambda j: (0, j))],
        out_shape=[act, act, act, act, jax.ShapeDtypeStruct((3, CONV_WIDTH), f32), jax.ShapeDtypeStruct((1, CONV_WIDTH), f32)],
        compiler_params=_cparams("parallel"),
    )(proj, proj, proj, proj, cw, cb, conv_out, dmix_b)


def _gdn_bwd(qkv, sc, gr, u_all, w_all, vn_all, t_all, sp_all, do_all):
    L = qkv.shape[0]
    nc = L // CHUNK
    W = GDN_WIDTH
    cps = GDN_CPS_BWD if nc % GDN_CPS_BWD == 0 else 1
    rows_per_step = cps * CHUNK
    nsteps = nc // cps

    def body(qkv_ref, sc_ref, gr_ref, u_ref, w_ref, vn_ref, t_ref, sp_ref, do_ref, dqkv_ref, dsc_ref, dgr_ref, ds_scr):
        @pl.when(pl.program_id(0) == 0)
        def _():
            ds_scr[...] = jnp.zeros_like(ds_scr)
        HS = range(cps * HEADS)
        hd = [i % HEADS for i in HS]
        rs = [slice((i // HEADS) * CHUNK, (i // HEADS + 1) * CHUNK) for i in HS]
        cs = [slice(hd[i] * HEAD_DIM, (hd[i] + 1) * HEAD_DIM) for i in HS]
        q = [qkv_ref[rs[i], hd[i] * HEAD_DIM:(hd[i] + 1) * HEAD_DIM] for i in HS]
        k = [qkv_ref[rs[i], W + hd[i] * HEAD_DIM:W + (hd[i] + 1) * HEAD_DIM] for i in HS]
        v = [qkv_ref[rs[i], 2 * W + hd[i] * HEAD_DIM:2 * W + (hd[i] + 1) * HEAD_DIM] for i in HS]
        hsc = [_head_scalars(sc_ref[rs[i], :], gr_ref, hd[i], i // HEADS) for i in HS]
        beta, gcc, gl, dmat, dmat_t = ([x[i] for x in hsc] for i in range(5))
        ii, jj = hsc[0][5], hsc[0][6]
        eg = [jnp.exp(gcc[h]) for h in HS]
        ekl = [jnp.exp(gl[h] - gcc[h]) for h in HS]
        egl = [jnp.exp(gl[h]) for h in HS]
        kb = [k[h] * beta[h] for h in HS]
        ks = [k[h] * ekl[h] for h in HS]
        do = [do_ref[rs[h], cs[h]] for h in HS]
        vn = [vn_ref[rs[h], cs[h]] for h in HS]
        s = [sp_ref[h // HEADS, cs[h], :] for h in HS]
        w = [w_ref[rs[h], cs[h]] for h in HS]
        qd = [q[h] * eg[h] for h in HS]

        kq = [_mm_nt(k[h], q[h]) for h in HS]
        p_t = [jnp.where(jj >= ii, kq[h] * dmat_t[h], 0.0) for h in HS]
        ptd = [_mm(p_t[h], do[h]) for h in HS]
        qw =[jnp.concatenate([qd[h], -w[h]], axis=0) for h in HS]
        dsn, dvn, dodv = [None] * len(HS), [None] * len(HS), [None] * len(HS)
        ds_cur = [ds_scr[h] for h in range(HEADS)]
        for ci in reversed(range(cps)):
            IS = range(ci * HEADS, (ci + 1) * HEADS)
            ksd = [_mm(ks[i], ds_cur[hd[i]]) for i in IS]
            for i in IS:
                dsn[i] = ds_cur[hd[i]]
                dvn[i] = ptd[i] + ksd[hd[i]]
                dodv[i] = jnp.concatenate([do[i], dvn[i]], axis=0)
            dsq = [_mm_tn(qw[i], dodv[i]) for i in IS]
            ds_cur = [egl[i] * ds_cur[hd[i]] + dsq[hd[i]] for i in IS]
        for h in range(HEADS):
            ds_scr[h] = ds_cur[h]
        x1 = [_mm_nt(dodv[h], s[h]) for h in HS]
        dks = [_mm_nt(vn[h], dsn[h]) for h in HS]
        dov = [_mm_nt(do[h], vn[h]) for h in HS]
        vdo = [_mm_nt(vn[h], do[h]) for h in HS]
        kk = [_mm_nt(kb[h], k[h]) for h in HS]
        qk = [_mm_nt(q[h], k[h]) for h in HS]
        dgl = [egl[h] * jnp.sum(jnp.sum(s[h] * dsn[h], axis=1, keepdims=True), axis=0, keepdims=True) for h in HS]
        dqd = [x1[h][:CHUNK] for h in HS]
        duw = [jnp.concatenate([dvn[h], -x1[h][CHUNK:]], axis=1) for h in HS]
        tdu = [_mm_tn(t_ref[h // HEADS, hd[h]], duw[h]) for h in HS]
        dvk = [duw[h] + tdu[h] for h in HS]
        uw = [jnp.concatenate([u_ref[rs[h], cs[h]], w[h]], axis=1) for h in HS]
        da = [-jnp.where(ii > jj, _mm_nt(dvk[h], uw[h]), 0.0) for h in HS]
        da_t = [-jnp.where(jj > ii, _mm_nt(uw[h], dvk[h]), 0.0) for h in HS]
        dp = [jnp.where(ii >= jj, dov[h], 0.0) for h in HS]
        dp_t = [jnp.where(jj >= ii, vdo[h], 0.0) for h in HS]
        r1 = [_mm(jnp.concatenate([da[h] * dmat[h], dp[h] * dmat[h]], axis=0), k[h]) for h in HS]
        dk1 = [_mm(jnp.concatenate([da_t[h] * dmat_t[h], dp_t[h] * dmat_t[h]], axis=1),
                   jnp.concatenate([kb[h], q[h]], axis=0)) for h in HS]
        lane = _lanes((CHUNK, LANE))
        for ci in range(cps):
            dsc = jnp.zeros((CHUNK, LANE), f32)
            for i in range(ci * HEADS, (ci + 1) * HEADS):
                h = hd[i]
                a = jnp.where(ii > jj, kk[i] * dmat[i], 0.0)
                p = jnp.where(ii >= jj, qk[i] * dmat[i], 0.0)
                gmat = da[i] * a + dp[i] * p
                dvb, dkbg = dvk[i][:, :HEAD_DIM], dvk[i][:, HEAD_DIM:]
                kbg = kb[i] * eg[i]
                dkb = r1[i][:CHUNK] + dkbg * eg[i]
                dq = r1[i][CHUNK:] + dqd[i] * eg[i]
                dk = dk1[i] + dks[i] * ekl[i] + dkb * beta[i]
                dbeta = jnp.sum(dkb * k[i] + dvb * v[i], axis=1, keepdims=True)
                ksum = jnp.sum(dks[i] * ks[i], axis=1, keepdims=True)
                dgl_tot = dgl[i] + jnp.sum(ksum, axis=0, keepdims=True)
                dgc = (jnp.sum(gmat, axis=1, keepdims=True) + jnp.sum(dqd[i] * qd[i] + dkbg * kbg, axis=1, keepdims=True)
                       - ksum)
                dgc = dgc + jnp.where(_rows(dgc.shape) == CHUNK - 1, dgl_tot, 0.0)
                dqkv_ref[rs[i], h * HEAD_DIM:(h + 1) * HEAD_DIM] = dq
                dqkv_ref[rs[i], W + h * HEAD_DIM:W + (h + 1) * HEAD_DIM] = dk
                dqkv_ref[rs[i], 2 * W + h * HEAD_DIM:2 * W + (h + 1) * HEAD_DIM] = dvb * beta[i]
                dsc = jnp.where(lane == h, dbeta, jnp.where(lane == HEADS + h, dgc, dsc))
                dgr_ref[ci, h:h + 1, :] = jnp.sum(gmat, axis=0, keepdims=True)
            dsc_ref[ci * CHUNK:(ci + 1) * CHUNK, :] = dsc

    row = lambda c: (nsteps - 1 - c, 0)
    lead3 = lambda c: (nsteps - 1 - c, 0, 0)
    return _pcall(
        body, name="gdn_bwd", grid=(nsteps,),
        in_specs=[pl.BlockSpec((rows_per_step, 3 * W), row), pl.BlockSpec((rows_per_step, LANE), row),
                  pl.BlockSpec((cps, HEADS, CHUNK), lead3),
                  pl.BlockSpec((rows_per_step, W), row), pl.BlockSpec((rows_per_step, W), row),
                  pl.BlockSpec((rows_per_step, W), row),
                  pl.BlockSpec((cps, HEADS, CHUNK, CHUNK), lambda c: (nsteps - 1 - c, 0, 0, 0)),
                  pl.BlockSpec((cps, W, HEAD_DIM), lead3), pl.BlockSpec((rows_per_step, W), row)],
        out_specs=[pl.BlockSpec((rows_per_step, 3 * W), row), pl.BlockSpec((rows_per_step, LANE), row),
                   pl.BlockSpec((cps, HEADS, CHUNK), lead3)],
        out_shape=[jax.ShapeDtypeStruct((L, 3 * W), f32), jax.ShapeDtypeStruct((L, LANE), f32),
                   jax.ShapeDtypeStruct((nc, HEADS, CHUNK), f32)],
        scratch_shapes=[pltpu.VMEM((HEADS, HEAD_DIM, HEAD_DIM), f32)],
        compiler_params=_cparams("arbitrary"),
    )(qkv, sc, gr, u_all, w_all, vn_all, t_all, sp_all, do_all)


def _qkv_bwd(proj, conv_out, cw, dn):
    L = proj.shape[0]

    def body(x_ref, c_ref, cw_ref, dn_ref, dx_ref, gcw_ref):
        j = pl.program_id(0)
        x, c, dn_v = x_ref[...], c_ref[...], dn_ref[...]
        sg = _sigmoid(c)
        a = c * sg
        rn = lax.rsqrt(jnp.sum(a * a, axis=1, keepdims=True) + EPS)
        scale = jnp.where(j < HEADS, HEAD_DIM ** -0.5, 1.0).astype(f32)
        da_n = (scale * rn) * (dn_v - a * ((rn * rn) * jnp.sum(dn_v * a, axis=1, keepdims=True)))
        da = jnp.where(j < 2 * HEADS, da_n, dn_v)
        dc = da * (sg * (1.0 + c * (1.0 - sg)))
        dc1, dc2, dc3 = _shift_up(dc, 1), _shift_up(dc, 2), _shift_up(dc, 3)
        gcw_ref[3:4, :] = jnp.sum(dc * x, axis=0, keepdims=True)
        gcw_ref[2:3, :] = jnp.sum(dc1 * x, axis=0, keepdims=True)
        gcw_ref[1:2, :] = jnp.sum(dc2 * x, axis=0, keepdims=True)
        gcw_ref[0:1, :] = jnp.sum(dc3 * x, axis=0, keepdims=True)
        dx = cw_ref[3:4, :] * dc + cw_ref[2:3, :] * dc1 + cw_ref[1:2, :] * dc2 + cw_ref[0:1, :] * dc3
        dx_ref[...] = dx.astype(bf16)

    col = pl.BlockSpec((L, LANE), lambda j: (0, j))
    wspec = pl.BlockSpec((4, LANE), lambda j: (0, j))
    return _pcall(
        body, name="qkv_bwd", grid=(3 * HEADS,),
        in_specs=[col, col, wspec, col], out_specs=[col, wspec],
        out_shape=[jax.ShapeDtypeStruct((L, 3 * GDN_WIDTH), bf16), jax.ShapeDtypeStruct((4, 3 * GDN_WIDTH), f32)],
        compiler_params=_cparams("parallel"),
    )(proj, conv_out, cw, dn)


def _scalars_bwd(proj, alog_p, dtb_p, dsc, dgr_col):
    L = proj.shape[0]

    def body(x_ref, al_ref, dt_ref, dsc_ref, dgr_ref, dba_ref, gs_ref):
        x, dsc_v = x_ref[...], dsc_ref[...]
        lane = _lanes(x.shape)
        dec = (lane >= HEADS) & (lane < 2 * HEADS)
        dg = jnp.where(dec, dsc_v - dgr_ref[...], 0.0)
        rc = _rows(x.shape) & (CHUNK - 1)
        for s in (1, 2, 4, 8, 16, 32):
            dg = dg + jnp.where(rc + s < CHUNK, pltpu.roll(dg, L - s, 0), 0.0)
        xa = x + dt_ref[...]
        ea = jnp.exp(al_ref[...])
        g = -ea * _softplus(xa)
        da = dg * (-ea) * _sigmoid(xa)
        beta = _sigmoid(x)
        db = dsc_v * beta * (1.0 - beta)
        dba_ref[...] = jnp.where(lane < HEADS, db, jnp.where(dec, da, 0.0)).astype(bf16)
        g_al = jnp.sum(jnp.where(dec, dg * g, 0.0), axis=0, keepdims=True)
        g_dt = jnp.sum(jnp.where(dec, da, 0.0), axis=0, keepdims=True)
        row8 = _rows(gs_ref.shape)
        gs = jnp.where(row8 == 0, g_al, jnp.where(row8 == 1, g_dt, 0.0))
        gs_ref[...] = pltpu.roll(gs, LANE - HEADS, 1)

    full = pl.BlockSpec((L, LANE), lambda i: (0, 0))
    vec = pl.BlockSpec((1, LANE), lambda i: (0, 0))
    return _pcall(
        body, name="scalars_bwd", grid=(1,),
        in_specs=[pl.BlockSpec((L, LANE), lambda i: (0, OFF_BA // LANE)), vec, vec, full, full],
        out_specs=[full, pl.BlockSpec((8, LANE), lambda i: (0, 0))],
        out_shape=[jax.ShapeDtypeStruct((L, LANE), bf16), jax.ShapeDtypeStruct((8, LANE), f32)],
        compiler_params=_cparams("arbitrary"),
    )(proj, alog_p, dtb_p, dsc, dgr_col)


def _input_grad(pieces, offs, wpad, x, nw, dy):
    L = x.shape[0]
    tm = min(512, L)
    npc = len(pieces)

    def body(*refs):
        p_refs = refs[:npc]
        w_hbm, x_ref, nw_ref, dy_ref, gx_ref, gnw_ref, w_vmem, sem = refs[npc:]

        @pl.when(pl.program_id(0) == 0)
        def _():
            cp = pltpu.make_async_copy(w_hbm, w_vmem, sem)
            cp.start()
            cp.wait()
            gnw_ref[...] = jnp.zeros_like(gnw_ref)
        dh = None
        for p_ref, off in zip(p_refs, offs):
            wd = p_ref.shape[1]
            part = jnp.dot(p_ref[...], w_vmem[off:off + wd, :], preferred_element_type=f32)
            dh = part if dh is None else dh + part
        xv, nwv = x_ref[...], nw_ref[...]
        r = lax.rsqrt(jnp.mean(xv * xv, axis=-1, keepdims=True) + EPS)
        xh = xv * r
        gnw_ref[...] += jnp.sum(dh * xh, axis=0, keepdims=True)
        dxh = dh * nwv
        gx_ref[...] = dy_ref[...] + r * (dxh - xh * jnp.mean(dxh * xh, axis=-1, keepdims=True))

    row = lambda i: (i, 0)
    fix = lambda i: (0, 0)
    return _pcall(
        body, name="input_grad", grid=(L // tm,),
        in_specs=[pl.BlockSpec((tm, p.shape[1]), row) for p in pieces] + [
            ANY, pl.BlockSpec((tm, D_MODEL), row), pl.BlockSpec((1, D_MODEL), fix), pl.BlockSpec((tm, D_MODEL), row)],
        out_specs=[pl.BlockSpec((tm, D_MODEL), row), pl.BlockSpec((1, D_MODEL), fix)],
        out_shape=[jax.ShapeDtypeStruct((L, D_MODEL), f32), jax.ShapeDtypeStruct((1, D_MODEL), f32)],
        scratch_shapes=[pltpu.VMEM(wpad.shape, bf16), pltpu.SemaphoreType.DMA(())],
        compiler_params=_cparams("arbitrary"),
    )(*pieces, wpad, x, nw, dy)


def _adamw_reduce(parts, w, m, v, name):
    R, C = w.shape
    n_parts = parts.shape[0]
    tr = 128 if R % 128 == 0 else R
    c1 = 1.0 - ADAM_B1 ** ADAM_STEP
    c2 = 1.0 - ADAM_B2 ** ADAM_STEP

    def body(p_ref, w_ref, m_ref, v_ref, g_ref, d_ref, nm_ref, nv_ref):
        g = p_ref[0].astype(f32)
        for s in range(1, n_parts):
            g = g + p_ref[s].astype(f32)
        nm = ADAM_B1 * m_ref[...] + (1.0 - ADAM_B1) * g
        nv = ADAM_B2 * v_ref[...] + (1.0 - ADAM_B2) * (g * g)
        g_ref[...] = g
        nm_ref[...] = nm
        nv_ref[...] = nv
        d_ref[...] = -ADAM_LR * ((nm / c1) / (jnp.sqrt(nv / c2) + ADAM_EPS) + ADAM_WD * w_ref[...])

    blk = pl.BlockSpec((tr, C), lambda i: (i, 0))
    out = jax.ShapeDtypeStruct((R, C), f32)
    return _pcall(
        body, name=name, grid=(R // tr,),
        in_specs=[pl.BlockSpec((n_parts, tr, C), lambda i: (0, i, 0)), blk, blk, blk],
        out_specs=[blk] * 4, out_shape=[out] * 4,
        compiler_params=_cparams("parallel"),
    )(parts, w, m, v)


SMALL_SLOTS = ((0, D_MODEL), (D_MODEL, D_MODEL), (2 * D_MODEL, D_MODEL), (3 * D_MODEL, LANE),
               (3 * D_MODEL + LANE, HEADS), (3 * D_MODEL + 2 * LANE, HEADS))
SMALL_LOSS = 3 * D_MODEL + 3 * LANE
SMALL_W = SMALL_LOSS + LANE


def _pack_small(gs, after):
    def body(nw_ref, cb_ref, fw_ref, gn_ref, sc_ref, ls_ref, after_ref, o_ref):
        for ref, (start, width) in zip((nw_ref, cb_ref, fw_ref, gn_ref), SMALL_SLOTS[:4]):
            o_ref[:, start:start + width] = ref[...]
        o_ref[:, SMALL_SLOTS[4][0]:SMALL_SLOTS[4][0] + LANE] = sc_ref[0:1, :]
        o_ref[:, SMALL_SLOTS[5][0]:SMALL_SLOTS[5][0] + LANE] = sc_ref[1:2, :]
        o_ref[:, SMALL_LOSS:SMALL_W] = ls_ref[...]

    vm = pl.BlockSpec(memory_space=pltpu.VMEM)
    return _pcall(body, name="pack_small_grads", out_shape=jax.ShapeDtypeStruct((1, SMALL_W), f32),
                  in_specs=[vm] * 6 + [ANY], out_specs=vm)(*gs, after)


def _adamw_small(parts, ws, ms, vs):
    c1 = 1.0 - ADAM_B1 ** ADAM_STEP
    c2 = 1.0 - ADAM_B2 ** ADAM_STEP
    np_ = len(ws)

    def body(*refs):
        p_ref = refs[0]
        w_refs, m_refs, v_refs = refs[1:1 + np_], refs[1 + np_:1 + 2 * np_], refs[1 + 2 * np_:1 + 3 * np_]
        outs = refs[1 + 3 * np_:]
        g_refs, d_refs, nm_refs, nv_refs = (outs[i * np_:(i + 1) * np_] for i in range(4))
        loss_ref = outs[4 * np_]

        def total(start, width):
            t = p_ref[0, :, start:start + width]
            for s in range(1, N_DEV):
                t = t + p_ref[s, :, start:start + width]
            return t

        for i, (start, width) in enumerate(SMALL_SLOTS):
            g = total(start, width)
            nm = ADAM_B1 * m_refs[i][...] + (1.0 - ADAM_B1) * g
            nv = ADAM_B2 * v_refs[i][...] + (1.0 - ADAM_B2) * (g * g)
            g_refs[i][...] = g
            nm_refs[i][...] = nm
            nv_refs[i][...] = nv
            d_refs[i][...] = -ADAM_LR * ((nm / c1) / (jnp.sqrt(nv / c2) + ADAM_EPS) + ADAM_WD * w_refs[i][...])
        loss_ref[...] = total(SMALL_LOSS, LANE)

    vm = pl.BlockSpec(memory_space=pltpu.VMEM)
    shapes = [jax.ShapeDtypeStruct(w.shape, f32) for w in ws]
    res = _pcall(body, name="adamw_small", out_shape=shapes * 4 + [jax.ShapeDtypeStruct((1, LANE), f32)],
                 in_specs=[vm] * (1 + 3 * np_), out_specs=[vm] * (4 * np_ + 1))(parts, *ws, *ms, *vs)
    return [res[i * np_:(i + 1) * np_] for i in range(4)], res[4 * np_]


def _adamw_w_in(parts, w3, m3, v3):
    n_parts, n, _ = parts.shape
    c1 = 1.0 - ADAM_B1 ** ADAM_STEP
    c2 = 1.0 - ADAM_B2 ** ADAM_STEP

    def body(p_ref, w_ref, m_ref, v_ref, g_ref, d_ref, nm_ref, nv_ref):
        g = p_ref[0].astype(f32)
        for s in range(1, n_parts):
            g = g + p_ref[s].astype(f32)
        nm = ADAM_B1 * m_ref[:, 0, :] + (1.0 - ADAM_B1) * g
        nv = ADAM_B2 * v_ref[:, 0, :] + (1.0 - ADAM_B2) * (g * g)
        g_ref[:, 0, :] = g
        nm_ref[:, 0, :] = nm
        nv_ref[:, 0, :] = nv
        d_ref[:, 0, :] = -ADAM_LR * ((nm / c1) / (jnp.sqrt(nv / c2) + ADAM_EPS) + ADAM_WD * w_ref[:, 0, :])

    tile = 2 * COL_TILE
    blk = pl.BlockSpec((n, 1, tile), lambda j: (0, 0, j))
    out = jax.ShapeDtypeStruct((n, 1, D_MODEL), f32)
    return _pcall(
        body, name="adamw_w_in", grid=(D_MODEL // tile,),
        in_specs=[pl.BlockSpec((n_parts, n, tile), lambda j: (0, 0, j)), blk, blk, blk],
        out_specs=[blk] * 4, out_shape=[out] * 4,
        compiler_params=_cparams("parallel"),
    )(parts, w3, m3, v3)


def _pad_lanes(vec8, start):
    return jnp.pad(vec8.reshape(1, -1), ((0, 0), (start, LANE - start - vec8.size)))


def kernel(x, norm_in_w, w_in, conv_qkv_w, A_log, dt_bias, gdn_norm_w, conv_w, conv_b, w_out, final_norm_w, loss_target, m_norm_in_w, m_w_in, m_conv_qkv_w, m_A_log, m_dt_bias, m_gdn_norm_w, m_conv_w, m_conv_b, m_w_out, m_final_norm_w, v_norm_in_w, v_w_in, v_conv_qkv_w, v_A_log, v_dt_bias, v_gdn_norm_w, v_conv_w, v_conv_b, v_w_out, v_final_norm_w):
    L = x.shape[1]
    nc = L // CHUNK
    xs = x[0]
    tgt = loss_target[0]
    fnw = final_norm_w.reshape(1, D_MODEL)

    as_rows = lambda a: jnp.transpose(a, (2, 0, 1))
    win_g, cqkv_g, cw_g = _all_gather([_cast_w_in(as_rows(w_in)), conv_qkv_w[0], conv_w[0]], "gather_weights")
    wpad = _relayout_w_in(win_g)
    cqkv = jnp.concatenate([cqkv_g[d] for d in range(N_DEV)], axis=1)
    cw = jnp.concatenate([cw_g[d] for d in range(N_DEV)], axis=1)
    alog_p = _pad_lanes(A_log, HEADS)
    dtb_p = _pad_lanes(dt_bias, HEADS)
    me_flat, me_chip = _flat(*_mesh_pos()), 2 * lax.axis_index("x") + lax.axis_index("y")
    tok = lambda started: started[4][0:1, 0:1]
    wo_own = w_out[0].astype(bf16)
    wo_started = _spread_start(wo_own, wpad, "gather", "gather_w_out_start")

    proj, h = _in_proj(xs, norm_in_w + tok(wo_started), wpad)
    qkv, qkv_conv = _qkv_act(proj, cqkv)
    sc, gr = _scalars(proj, alog_p, dtb_p)
    o, u_all, w_all, vn_all, t_all, sp_all = _gdn_fwd(qkv, sc, gr)
    mix_a = _gdn_gate(o, proj, gdn_norm_w)
    mix_b, conv_out = _conv_fwd(proj, cw, conv_b)
    wo = _own_slot(_spread_wait(wo_started, mix_b, "gather", "gather_w_out_wait"), wo_own, me_flat).reshape(-1, D_MODEL)
    dy, dyb, dmix_a, dmix_b, g_fnw, loss_v = _out_proj_loss(xs, mix_a, mix_b, wo, fnw, tgt)

    g_wout = jnp.concatenate([_tn_matmul(mix_a, dyb, "grad_w_out_a"), _tn_matmul(mix_b, dyb, "grad_w_out_b")], axis=0)
    g_wout = g_wout.reshape(N_DEV, -1, D_MODEL)
    g_wout_own = lax.dynamic_index_in_dim(g_wout, me_flat, 0, keepdims=False)
    gwo_started = _spread_start(g_wout, dyb, "scatter", "exchange_grad_w_out_start")
    do, dzg, g_gnw = _gdn_gate_bwd(o, proj, gdn_norm_w + tok(gwo_started), dmix_a)
    d_b, d_c, d_hc, d_zc, g_cw, g_cb = _conv_bwd(proj, conv_out, cw, conv_b, dmix_b)
    dqkv_n, dsc, dgr = _gdn_bwd(qkv, sc, gr, u_all, w_all, vn_all, t_all, sp_all, do)
    dqkv, g_cqkv = _qkv_bwd(proj, qkv_conv, cqkv, dqkv_n)
    dgr_col = jnp.pad(dgr.transpose(0, 2, 1).reshape(L, HEADS), ((0, 0), (HEADS, LANE - 2 * HEADS)))
    dba, g_sc = _scalars_bwd(proj, alog_p, dtb_p, dsc, dgr_col)
    pieces = [dqkv, dzg, dba, d_b, d_c, d_hc, d_zc]
    offs = [OFF_QKV, OFF_ZG, OFF_BA, OFF_B, OFF_C, OFF_HC, OFF_ZC]
    g_parts = [_tn_matmul(p, h, "grad_w_in_%d" % i) for i, p in enumerate(pieces)]
    g_win_blk = _grad_blocks(g_parts)

    (p_win,) = _pair_exchange([g_win_blk], "exchange_grads_pair")
    s_win = _pair_sum(g_win_blk, p_win, "pair_sum_w_in")
    s_win_own = lax.dynamic_index_in_dim(s_win, me_chip, 0, keepdims=False)
    r_cqkv, r_cw = _all_to_all(
        [g_cqkv.reshape(4, N_DEV, -1).transpose(1, 0, 2), g_cw.reshape(3, N_DEV, -1).transpose(1, 0, 2)],
        "exchange_small_sharded_grads")
    gwi_started = _spread_start(s_win, r_cw, "chips", "exchange_grads_chips_start")
    grad_x, g_nw = _input_grad(pieces, offs, wpad, xs, norm_in_w + tok(gwi_started), dy)

    r_wout = _own_slot(_spread_wait(gwo_started, grad_x, "scatter", "exchange_grad_w_out_wait"), g_wout_own, me_flat)
    upd_wout =_adamw_reduce(r_wout, w_out[0], m_w_out[0], v_w_out[0], "adamw_w_out")
    upd_cqkv = _adamw_reduce(r_cqkv, conv_qkv_w[0], m_conv_qkv_w[0], v_conv_qkv_w[0], "adamw_conv_qkv_w")
    upd_cw = _adamw_reduce(r_cw, conv_w[0], m_conv_w[0], v_conv_w[0], "adamw_conv_w")

    r_win = _own_slot(_spread_wait(gwi_started, upd_cw[0], "chips", "exchange_grads_chips_wait"), s_win_own, me_chip)
    upd_win = [jnp.transpose(a, (1, 2, 0)) for a in _adamw_w_in(r_win, as_rows(w_in), as_rows(m_w_in), as_rows(v_w_in))]

    small_g = _pack_small([g_nw, g_cb, g_fnw, g_gnw, g_sc, loss_v], r_win)
    (small_all,) = _all_gather([small_g], "gather_small_grads")
    fvec = lambda a: a.reshape(1, D_MODEL)
    upd_small, loss_sum = _adamw_small(
        small_all,
        [norm_in_w, conv_b, fvec(final_norm_w), gdn_norm_w, A_log, dt_bias],
        [m_norm_in_w, m_conv_b, fvec(m_final_norm_w), m_gdn_norm_w, m_A_log, m_dt_bias],
        [v_norm_in_w, v_conv_b, fvec(v_final_norm_w), v_gdn_norm_w, v_A_log, v_dt_bias])

    outs = [loss_sum[0, 0], grad_x[None]]
    for k in range(4):
        nw_k, cb_k, fw_k, gn_k, al_k, dt_k = upd_small[k]
        outs += [nw_k, upd_win[k], upd_cqkv[k][None], al_k, dt_k, gn_k,
                 upd_cw[k][None], cb_k, upd_wout[k][None], fw_k.reshape(D_MODEL)]
    return tuple(outs)
```

```python
import functools
import math

import jax
import jax.numpy as jnp
from jax import lax
from jax.experimental import pallas as pl
from jax.experimental.pallas import tpu as pltpu

f32 = jnp.float32
bf16 = jnp.bfloat16

N_DEV = 8
D_MODEL = 1024
HEADS = 8
HEAD_DIM = 128
CHUNK = 64
GDN_CPS = 4
GDN_CPS_BWD = 1
GDN_WIDTH = HEADS * HEAD_DIM
CONV_WIDTH = 1024
PROJ_WIDTH = 8208
SHARD_W = PROJ_WIDTH // N_DEV
EPS = 1e-6

NAT_SMALL_END = 4112
PAD_COLS = 112
OFF_QKV, OFF_ZG, OFF_BA, OFF_B, OFF_C, OFF_HC, OFF_ZC = 0, 3072, 4096, 4224, 5248, 6272, 7296
PROJ_PAD = 8320
LANE = 128

ADAM_LR, ADAM_B1, ADAM_B2, ADAM_EPS, ADAM_WD, ADAM_STEP = 0.001, 0.9, 0.999, 1e-08, 0.01, 10

VMEM_LIMIT = 56 * 1024 * 1024

MESH = pl.DeviceIdType.MESH
ANY = pl.BlockSpec(memory_space=pl.ANY)


def _pcall(body, **kw):
    return pl.pallas_call(body, **kw)


def _cparams(*sem):
    return pltpu.CompilerParams(dimension_semantics=sem if sem else None, vmem_limit_bytes=VMEM_LIMIT)


def _mm(a, b):
    return jnp.dot(a.astype(bf16), b.astype(bf16), preferred_element_type=f32)


def _mm_nt(a, b):
    return lax.dot_general(a.astype(bf16), b.astype(bf16), (((1,), (1,)), ((), ())), preferred_element_type=f32)


def _mm_tn(a, b):
    return lax.dot_general(a.astype(bf16), b.astype(bf16), (((0,), (0,)), ((), ())), preferred_element_type=f32)


def _rows(shape):
    return lax.broadcasted_iota(jnp.int32, shape, 0)


def _lanes(shape):
    return lax.broadcasted_iota(jnp.int32, shape, 1)


def _shift_down(x, s):
    if s == 0:
        return x
    return jnp.where(_rows(x.shape) >= s, pltpu.roll(x, s, 0), 0.0)


def _shift_up(x, s):
    if s == 0:
        return x
    n = x.shape[0]
    return jnp.where(_rows(x.shape) < n - s, pltpu.roll(x, n - s, 0), 0.0)


def _sigmoid(x):
    return jax.nn.sigmoid(x)


def _softplus(x):
    e = jnp.exp(-jnp.abs(x))
    small = e * (1.0 - e * (0.5 - e * (1.0 / 3.0)))
    return jnp.maximum(x, 0.0) + jnp.where(e < 0.01, small, jnp.log(1.0 + e))


def _mesh_pos():
    return lax.axis_index("x"), lax.axis_index("y"), lax.axis_index("c")


def _flat(px, py, pc):
    return 4 * px + 2 * py + pc


def _all_gather(xs, name):
    n = len(xs)

    def body(*refs):
        x_refs, o_refs = refs[:n], refs[n:2 * n]
        send_sems, recv_sems, local_sems = refs[2 * n:]
        x, y, c = _mesh_pos()
        me, sibling = (x, y, c), (x, y, 1 - c)
        flip = lambda v, bit: v + bit - 2 * v * bit
        nbr_a = (flip(x, 1 - c), flip(y, c))
        nbr_b = (flip(x, c), flip(y, 1 - c))
        diag = (1 - x, 1 - y)

        def copy(a, k, block, to, src=None):
            dst = o_refs[a].at[_flat(*block)]
            return pltpu.make_async_remote_copy(
                src_ref=dst if src is None else src, dst_ref=dst,
                send_sem=send_sems.at[a, k], recv_sem=recv_sems.at[a, k], device_id=to, device_id_type=MESH)

        mine, sent = [], []

        def go(cp):
            cp.start()
            sent.append(cp)

        for a in range(n):
            cp = pltpu.make_async_copy(x_refs[a], o_refs[a].at[_flat(*me)], local_sems.at[a])
            cp.start()
            mine.append(cp)
            go(copy(a, 0, me, sibling, src=x_refs[a]))
            go(copy(a, 1, me, (*nbr_a, c), src=x_refs[a]))
            go(copy(a, 2, me, (*nbr_b, c), src=x_refs[a]))
        for a in range(n):
            copy(a, 1, (*nbr_a, c), me).wait_recv()
            go(copy(a, 3, (*nbr_a, c), (*nbr_b, c)))
            go(copy(a, 4, (*nbr_a, c), sibling))
        for a in range(n):
            copy(a, 2, (*nbr_b, c), me).wait_recv()
            go(copy(a, 5, (*nbr_b, c), sibling))
        for a in range(n):
            copy(a, 3, (*diag, c), me).wait_recv()
            go(copy(a, 6, (*diag, c), sibling))
        for a in range(n):
            copy(a, 0, sibling, me).wait_recv()
            copy(a, 4, (*nbr_b, 1 - c), me).wait_recv()
            copy(a, 5, (*nbr_a, 1 - c), me).wait_recv()
            copy(a, 6, (*diag, 1 - c), me).wait_recv()
        for cp in sent:
            cp.wait_send()
        for cp in mine:
            cp.wait()

    outs = _pcall(
        body, name=name,
        out_shape=[jax.ShapeDtypeStruct((N_DEV,) + a.shape, a.dtype) for a in xs],
        in_specs=[ANY] * n, out_specs=[ANY] * n,
        scratch_shapes=[pltpu.SemaphoreType.DMA((n, 7)), pltpu.SemaphoreType.DMA((n, 7)), pltpu.SemaphoreType.DMA((n,))],
    )(*xs)
    return list(outs)


def _all_to_all(gs, name):
    n = len(gs)

    def body(*refs):
        g_refs, o_refs = refs[:n], refs[n:2 * n]
        send_sems, recv_sems, local_sems = refs[2 * n:]
        x, y, c = _mesh_pos()
        me = _flat(x, y, c)
        peers = []
        for k in range(1, N_DEV):
            kx, ky, kc = (k >> 2) & 1, (k >> 1) & 1, k & 1
            px = (1 - x) if kx else x
            py = (1 - y) if ky else y
            pc = (1 - c) if kc else c
            peers.append((px, py, pc))

        def copy(a, k):
            peer = peers[k - 1]
            return pltpu.make_async_remote_copy(
                src_ref=g_refs[a].at[_flat(*peer)], dst_ref=o_refs[a].at[me],
                send_sem=send_sems.at[a, k - 1], recv_sem=recv_sems.at[a, k - 1], device_id=peer, device_id_type=MESH)

        def arrival(a, k):
            peer = peers[k - 1]
            return pltpu.make_async_remote_copy(
                src_ref=g_refs[a].at[me], dst_ref=o_refs[a].at[_flat(*peer)],
                send_sem=send_sems.at[a, k - 1], recv_sem=recv_sems.at[a, k - 1], device_id=peer, device_id_type=MESH)

        mine, sent = [], []
        for a in range(n):
            cp = pltpu.make_async_copy(g_refs[a].at[me], o_refs[a].at[me], local_sems.at[a])
            cp.start()
            mine.append(cp)
            for k in range(1, N_DEV):
                cp = copy(a, k)
                cp.start()
                sent.append(cp)
        for a in range(n):
            for k in range(1, N_DEV):
                arrival(a, k).wait_recv()
        for cp in sent:
            cp.wait_send()
        for cp in mine:
            cp.wait()

    outs = _pcall(
        body, name=name,
        out_shape=[jax.ShapeDtypeStruct(a.shape, a.dtype) for a in gs],
        in_specs=[ANY] * n, out_specs=[ANY] * n,
        scratch_shapes=[pltpu.SemaphoreType.DMA((n, 7)), pltpu.SemaphoreType.DMA((n, 7)), pltpu.SemaphoreType.DMA((n,))],
    )(*gs)
    return list(outs)


def _pair_exchange(gs, name):
    n = len(gs)
    chips = [(0, 0), (0, 1), (1, 0), (1, 1)]

    def body(*refs):
        g_refs, o_refs = refs[:n], refs[n:2 * n]
        send_sems, recv_sems = refs[2 * n:]
        x, y, c = _mesh_pos()
        sibling = (x, y, 1 - c)

        def copy(a, i):
            xp, yp = chips[i]
            return pltpu.make_async_remote_copy(
                src_ref=g_refs[a].at[_flat(xp, yp, 1 - c)], dst_ref=o_refs[a].at[i],
                send_sem=send_sems.at[a, i], recv_sem=recv_sems.at[a, i], device_id=sibling, device_id_type=MESH)

        cps = [copy(a, i) for a in range(n) for i in range(4)]
        for cp in cps:
            cp.start()
        for cp in cps:
            cp.wait()

    outs = _pcall(
        body, name=name,
        out_shape=[jax.ShapeDtypeStruct((4,) + a.shape[1:], a.dtype) for a in gs],
        in_specs=[ANY] * n, out_specs=[ANY] * n,
        scratch_shapes=[pltpu.SemaphoreType.DMA((n, 4)), pltpu.SemaphoreType.DMA((n, 4))],
    )(*gs)
    return list(outs)


def _pair_sum(g, p1, name):
    _, R, C = g.shape
    tr = 256 if R % 256 == 0 else R
    cidx = lax.axis_index("c").astype(jnp.int32).reshape(1)

    def body(c_ref, g_ref, p_ref, o_ref):
        o_ref[...] = (g_ref[...].astype(f32) + p_ref[...].astype(f32)).astype(o_ref.dtype)

    return _pcall(
        body, name=name,
        grid_spec=pltpu.PrefetchScalarGridSpec(
            num_scalar_prefetch=1, grid=(4, R // tr),
            in_specs=[pl.BlockSpec((1, tr, C), lambda i, r, c_ref: (2 * i + c_ref[0], r, 0)),
                      pl.BlockSpec((1, tr, C), lambda i, r, c_ref: (i, r, 0))],
            out_specs=pl.BlockSpec((1, tr, C), lambda i, r, c_ref: (i, r, 0))),
        out_shape=jax.ShapeDtypeStruct((4, R, C), g.dtype),
        compiler_params=_cparams("parallel", "parallel"),
    )(cidx, g, p1)


HBM = pl.BlockSpec(memory_space=pltpu.HBM)
SEM = pl.BlockSpec(memory_space=pltpu.SEMAPHORE)
EFFECT = pltpu.SideEffectType.DATAFLOW_SIDE_EFFECTING


def _peers(x, y, c):
    out = []
    for k in range(1, N_DEV):
        kx, ky, kc = (k >> 2) & 1, (k >> 1) & 1, k & 1
        out.append(((1 - x) if kx else x, (1 - y) if ky else y, (1 - c) if kc else c))
    return out


SPREAD_COPIES = {"gather": N_DEV - 1, "scatter": N_DEV - 1, "chips": 3}


def _spread_copy(src_ref, land_ref, send_sems, recv_sems, k, plan):
    x, y, c = _mesh_pos()
    if plan == "chips":
        px, py = [(1 - x, y), (x, 1 - y), (1 - x, 1 - y)][k]
        peer, src, slot = (px, py, c), src_ref.at[2 * px + py], 2 * x + y
    else:
        peer = _peers(x, y, c)[k]
        src, slot = (src_ref.at[_flat(*peer)] if plan == "scatter" else src_ref), _flat(x, y, c)
    return pltpu.make_async_remote_copy(
        src_ref=src, dst_ref=land_ref.at[slot], send_sem=send_sems.at[k], recv_sem=recv_sems.at[k],
        device_id=peer, device_id_type=MESH)


def _spread_start(src, after, plan, name):
    land_shape = (N_DEV,) + src.shape if plan == "gather" else src.shape
    n_copies = SPREAD_COPIES[plan]

    def body(src_ref, land_ref, after_ref, send_sems, recv_sems, src_thru, land_thru, token):
        for k in range(n_copies):
            _spread_copy(src_ref, land_ref, send_sems, recv_sems, k, plan).start()
        token[...] = jnp.zeros_like(token)

    return _pcall(
        body, name=name,
        out_shape=(pltpu.SemaphoreType.DMA((n_copies,)), pltpu.SemaphoreType.DMA((n_copies,)),
                   pltpu.HBM(src.shape, src.dtype), pltpu.HBM(land_shape, src.dtype), jax.ShapeDtypeStruct((8, LANE), f32)),
        in_specs=(HBM, HBM, ANY), out_specs=(SEM, SEM, HBM, HBM, pl.BlockSpec(memory_space=pltpu.VMEM)),
        input_output_aliases={0: 2, 1: 3},
        compiler_params=pltpu.CompilerParams(has_side_effects=EFFECT),
    )(pltpu.with_memory_space_constraint(src, pltpu.HBM),
      pltpu.with_memory_space_constraint(lax.empty(land_shape, src.dtype), pltpu.HBM), after)


def _spread_wait(started, after, plan, name):
    send_sems, recv_sems, src_thru, land_thru, _ = started

    def body(src_ref, land_ref, send_sems, recv_sems, after_ref, src_dead, got_ref):
        for k in range(SPREAD_COPIES[plan]):
            cp = _spread_copy(src_ref, land_ref, send_sems, recv_sems, k, plan)
            cp.wait_send()
            cp.wait_recv()

    return _pcall(
        body, name=name,
        out_shape=(pltpu.HBM(src_thru.shape, src_thru.dtype), pltpu.HBM(land_thru.shape, land_thru.dtype)),
        in_specs=(HBM, HBM, SEM, SEM, ANY), out_specs=(HBM, HBM), input_output_aliases={0: 0, 1: 1},
        compiler_params=pltpu.CompilerParams(has_side_effects=EFFECT),
    )(src_thru, land_thru, send_sems, recv_sems, after)[1]


def _own_slot(land, block, slot):
    zero = jnp.zeros((), jnp.int32)
    return lax.dynamic_update_slice(land, block[None], (slot.astype(jnp.int32),) + (zero,) * block.ndim)


PIECE_NAT = (0, 3072, 4096, 4112, 5136, 6160, 7184, PROJ_WIDTH)


COL_TILE = 256


def _cast_w_in(w3):
    n = w3.shape[0]

    def body(w_ref, o_ref):
        o_ref[...] = w_ref[:, 0, :].astype(bf16)

    return _pcall(
        body, name="cast_w_in", grid=(D_MODEL // COL_TILE,),
        in_specs=[pl.BlockSpec((n, 1, COL_TILE), lambda j: (0, 0, j))],
        out_specs=pl.BlockSpec((n, COL_TILE), lambda j: (0, j)),
        out_shape=jax.ShapeDtypeStruct((n, D_MODEL), bf16),
        compiler_params=_cparams("parallel"),
    )(w3)


def _relayout_w_in(win_g):
    def body(g_ref, o_ref):
        o_ref[NAT_SMALL_END:NAT_SMALL_END + PAD_COLS, :] = jnp.zeros((PAD_COLS, COL_TILE), o_ref.dtype)
        for d in range(N_DEV):
            n0, n1 = d * SHARD_W, (d + 1) * SHARD_W
            cut = min(max(NAT_SMALL_END - n0, 0), SHARD_W)
            if cut > 0:
                o_ref[n0:n0 + cut, :] = g_ref[d, 0:cut, :]
            if cut < SHARD_W:
                o_ref[n0 + cut + PAD_COLS:n1 + PAD_COLS, :] = g_ref[d, cut:SHARD_W, :]

    return _pcall(
        body, name="relayout_w_in", grid=(D_MODEL // COL_TILE,),
        in_specs=[pl.BlockSpec((N_DEV, SHARD_W, COL_TILE), lambda j: (0, 0, j))],
        out_specs=pl.BlockSpec((PROJ_PAD, COL_TILE), lambda j: (0, j)),
        out_shape=jax.ShapeDtypeStruct((PROJ_PAD, D_MODEL), win_g.dtype),
        compiler_params=_cparams("parallel"),
    )(win_g)


def _grad_blocks(g_parts):
    npc = len(g_parts)

    def body(*refs):
        p_refs, o_ref = refs[:npc], refs[npc]
        for d in range(N_DEV):
            n0, n1 = d * SHARD_W, (d + 1) * SHARD_W
            for i in range(npc):
                lo, hi = max(n0, PIECE_NAT[i]), min(n1, PIECE_NAT[i + 1])
                if lo < hi:
                    o_ref[d, lo - n0:hi - n0, :] = p_refs[i][lo - PIECE_NAT[i]:hi - PIECE_NAT[i], :]

    return _pcall(
        body, name="grad_blocks", grid=(D_MODEL // COL_TILE,),
        in_specs=[pl.BlockSpec((p.shape[0], COL_TILE), lambda j: (0, j)) for p in g_parts],
        out_specs=pl.BlockSpec((N_DEV, SHARD_W, COL_TILE), lambda j: (0, 0, j)),
        out_shape=jax.ShapeDtypeStruct((N_DEV, SHARD_W, D_MODEL), bf16),
        compiler_params=_cparams("parallel"),
    )(*g_parts)


def _in_proj(x, nw, wpad_t):
    L = x.shape[0]
    tn = 640
    nj = wpad_t.shape[0] // tn

    def body(x_ref, nw_ref, w_ref, proj_ref, h_ref):
        @pl.when(pl.program_id(0) == 0)
        def _():
            for r in range(0, L, 256):
                xs = x_ref[r:r + 256, :]
                ms = jnp.mean(xs * xs, axis=-1, keepdims=True)
                h_ref[r:r + 256, :] = ((xs * lax.rsqrt(ms + EPS)) * nw_ref[...]).astype(bf16)
        for r in range(0, L, 512):
            proj_ref[r:r + 512, :] = lax.dot_general(h_ref[r:r + 512, :], w_ref[...], (((1,), (1,)), ((), ())),
                                                     preferred_element_type=f32)

    return _pcall(
        body, name="in_proj", grid=(nj,),
        in_specs=[pl.BlockSpec((L, D_MODEL), lambda j: (0, 0)), pl.BlockSpec((1, D_MODEL), lambda j: (0, 0)),
                  pl.BlockSpec((tn, D_MODEL), lambda j: (j, 0))],
        out_specs=[pl.BlockSpec((L, tn), lambda j: (0, j)), pl.BlockSpec((L, D_MODEL), lambda j: (0, 0))],
        out_shape=[jax.ShapeDtypeStruct((L, wpad_t.shape[0]), f32), jax.ShapeDtypeStruct((L, D_MODEL), bf16)],
        compiler_params=_cparams("arbitrary"),
    )(x, nw, wpad_t)


def _conv4(x, cw_ref):
    return (cw_ref[3:4, :] * x + cw_ref[2:3, :] * _shift_down(x, 1) + cw_ref[1:2, :] * _shift_down(x, 2)
            + cw_ref[0:1, :] * _shift_down(x, 3))


def _qkv_act(proj, cw):
    L = proj.shape[0]

    def body(x_ref, cw_ref, o_ref):
        j = pl.program_id(0)
        c = _conv4(x_ref[...], cw_ref)
        a = c * _sigmoid(c)
        rn = lax.rsqrt(jnp.sum(a * a, axis=1, keepdims=True) + EPS)
        scale = jnp.where(j < HEADS, HEAD_DIM ** -0.5, 1.0).astype(f32)
        o_ref[...] = jnp.where(j < 2 * HEADS, (a * rn) * scale, a)

    return _pcall(
        body, name="qkv_act", grid=(3 * HEADS,),
        in_specs=[pl.BlockSpec((L, LANE), lambda j: (0, j)), pl.BlockSpec((4, LANE), lambda j: (0, j))],
        out_specs=pl.BlockSpec((L, LANE), lambda j: (0, j)),
        out_shape=jax.ShapeDtypeStruct((L, 3 * GDN_WIDTH), f32),
        compiler_params=_cparams("parallel"),
    )(proj, cw)


def _scalars(proj, alog_p, dtb_p):
    L = proj.shape[0]
    nc = L // CHUNK

    def body(x_ref, al_ref, dt_ref, sc_ref, gr_ref):
        x = x_ref[...]
        lane = _lanes(x.shape)
        beta = _sigmoid(x)
        g = -jnp.exp(al_ref[...]) * _softplus(x + dt_ref[...])
        gc = jnp.where((lane >= HEADS) & (lane < 2 * HEADS), g, 0.0)
        rc = _rows(x.shape) & (CHUNK - 1)
        for s in (1, 2, 4, 8, 16, 32):
            gc = gc + jnp.where(rc >= s, pltpu.roll(gc, s, 0), 0.0)
        sc_ref[...] = jnp.where(lane < HEADS, beta, gc)
        sel = (_lanes((HEADS, LANE)) == _rows((HEADS, LANE)) + HEADS).astype(f32)
        for c in range(nc):
            gr_ref[c] = lax.dot_general(sel, sc_ref[c * CHUNK:(c + 1) * CHUNK, :], (((1,), (1,)), ((), ())),
                                        preferred_element_type=f32, precision=lax.Precision.HIGHEST)

    return _pcall(
        body, name="scalars", grid=(1,),
        in_specs=[pl.BlockSpec((L, LANE), lambda i: (0, OFF_BA // LANE)), pl.BlockSpec((1, LANE), lambda i: (0, 0)),
                  pl.BlockSpec((1, LANE), lambda i: (0, 0))],
        out_specs=[pl.BlockSpec((L, LANE), lambda i: (0, 0)), pl.BlockSpec((nc, HEADS, CHUNK), lambda i: (0, 0, 0))],
        out_shape=[jax.ShapeDtypeStruct((L, LANE), f32), jax.ShapeDtypeStruct((nc, HEADS, CHUNK), f32)],
        compiler_params=_cparams("arbitrary"),
    )(proj, alog_p, dtb_p)


def _head_scalars(sc, gr_ref, h, ci=0):
    lane = _lanes(sc.shape)
    beta = jnp.sum(jnp.where(lane == h, sc, 0.0), axis=1, keepdims=True)
    gcc = jnp.sum(jnp.where(lane == HEADS + h, sc, 0.0), axis=1, keepdims=True)
    gcr = gr_ref[ci, h:h + 1, :]
    gl = jnp.sum(jnp.where(_lanes(gcr.shape) == CHUNK - 1, gcr, 0.0), axis=1, keepdims=True)
    ii, jj = _rows((CHUNK, CHUNK)), _lanes((CHUNK, CHUNK))
    dmat = jnp.where(ii >= jj, jnp.exp(jnp.minimum(gcc - gcr, 0.0)), 0.0)
    dmat_t = jnp.where(jj >= ii, jnp.exp(jnp.minimum(gcr - gcc, 0.0)), 0.0)
    return beta, gcc, gl, dmat, dmat_t, ii, jj


def _gdn_fwd(qkv, sc, gr):
    L = qkv.shape[0]
    nc = L // CHUNK
    W = GDN_WIDTH
    cps = GDN_CPS if nc % GDN_CPS == 0 else 1
    rows_per_step = cps * CHUNK

    def body(qkv_ref, sc_ref, gr_ref, o_ref, u_ref, w_ref, vn_ref, t_ref, sp_ref, s_scr):
        @pl.when(pl.program_id(0) == 0)
        def _():
            s_scr[...] = jnp.zeros_like(s_scr)
        HS = range(cps * HEADS)
        hd = [i % HEADS for i in HS]
        rs = [slice((i // HEADS) * CHUNK, (i // HEADS + 1) * CHUNK) for i in HS]
        cs = [slice(hd[i] * HEAD_DIM, (hd[i] + 1) * HEAD_DIM) for i in HS]
        q = [qkv_ref[rs[i], hd[i] * HEAD_DIM:(hd[i] + 1) * HEAD_DIM] for i in HS]
        k = [qkv_ref[rs[i], W + hd[i] * HEAD_DIM:W + (hd[i] + 1) * HEAD_DIM] for i in HS]
        v = [qkv_ref[rs[i], 2 * W + hd[i] * HEAD_DIM:2 * W + (hd[i] + 1) * HEAD_DIM] for i in HS]
        hsc = [_head_scalars(sc_ref[rs[i], :], gr_ref, hd[i], i // HEADS) for i in HS]
        beta, gcc, gl, dmat = ([x[i] for x in hsc] for i in range(4))
        ii, jj = hsc[0][5], hsc[0][6]
        eg = [jnp.exp(gcc[h]) for h in HS]
        kb = [k[h] * beta[h] for h in HS]
        kk = [_mm_nt(kb[h], k[h]) for h in HS]
        qk = [_mm_nt(q[h], k[h]) for h in HS]
        n0 = [-jnp.where(ii > jj, kk[h] * dmat[h], 0.0) for h in HS]
        n1 = [_mm(n0[h], n0[h]) for h in HS]
        n2 = [_mm(n1[h], n1[h]) for h in HS]
        p01 = [n0[h] + n1[h] + _mm(n0[h], n1[h]) for h in HS]
        n3 = [_mm(n2[h], n2[h]) for h in HS]
        n4 = [_mm(n3[h], n3[h]) for h in HS]
        p23 = [n2[h] + n3[h] + _mm(n2[h], n3[h]) for h in HS]
        n5 = [_mm(n4[h], n4[h]) for h in HS]
        p03 = [p01[h] + p23[h] + _mm(p01[h], p23[h]) for h in HS]
        p45 = [n4[h] + n5[h] + _mm(n4[h], n5[h]) for h in HS]
        t = [p03[h] + p45[h] + _mm(p03[h], p45[h]) for h in HS]
        vb = [v[h] * beta[h] for h in HS]
        kbg = [kb[h] * eg[h] for h in HS]
        uw = [_mm(t[h], jnp.concatenate([vb[h], kbg[h]], axis=1)) for h in HS]
        u = [vb[h] + uw[h][:, :HEAD_DIM] for h in HS]
        w = [kbg[h] + uw[h][:, HEAD_DIM:] for h in HS]
        wq = [jnp.concatenate([w[h], q[h] * eg[h]], axis=0) for h in HS]
        p = [jnp.where(ii >= jj, qk[h] * dmat[h], 0.0) for h in HS]
        ks = [k[h] * jnp.exp(gl[h] - gcc[h]) for h in HS]
        s = [s_scr[h] for h in range(HEADS)]
        for ci in range(cps):
            IS = range(ci * HEADS, (ci + 1) * HEADS)
            ws = [_mm(wq[i], s[hd[i]]) for i in IS]
            vn = [u[i] - ws[hd[i]][:CHUNK] for i in IS]
            pv = [_mm(p[i], vn[hd[i]]) for i in IS]
            kv = [_mm_tn(ks[i], vn[hd[i]]) for i in IS]
            for i in IS:
                h = hd[i]
                sp_ref[ci, cs[i], :] = s[h]
                o_ref[rs[i], cs[i]] = ws[h][CHUNK:] + pv[h]
                vn_ref[rs[i], cs[i]] = vn[h]
            s = [jnp.exp(gl[i]) * s[hd[i]] + kv[hd[i]] for i in IS]
        for h in range(HEADS):
            s_scr[h] = s[h]
        for i in HS:
            u_ref[rs[i], cs[i]] = u[i]
            w_ref[rs[i], cs[i]] = w[i]
            t_ref[i // HEADS, hd[i]] = t[i]

    row = lambda c: (c, 0)
    act = jax.ShapeDtypeStruct((L, W), f32)
    return _pcall(
        body, name="gdn_fwd", grid=(nc // cps,),
        in_specs=[pl.BlockSpec((rows_per_step, 3 * W), row), pl.BlockSpec((rows_per_step, LANE), row),
                  pl.BlockSpec((cps, HEADS, CHUNK), lambda c: (c, 0, 0))],
        out_specs=[pl.BlockSpec((rows_per_step, W), row)] * 4 + [
            pl.BlockSpec((cps, HEADS, CHUNK, CHUNK), lambda c: (c, 0, 0, 0)),
            pl.BlockSpec((cps, W, HEAD_DIM), lambda c: (c, 0, 0))],
        out_shape=[act, act, act, act, jax.ShapeDtypeStruct((nc, HEADS, CHUNK, CHUNK), f32),
                   jax.ShapeDtypeStruct((nc, W, HEAD_DIM), f32)],
        scratch_shapes=[pltpu.VMEM((HEADS, HEAD_DIM, HEAD_DIM), f32)],
        compiler_params=_cparams("arbitrary"),
    )(qkv, sc, gr)


def _gdn_gate(o, proj, gnw):
    L = o.shape[0]

    def body(o_ref, z_ref, w_ref, m_ref):
        ov, z = o_ref[...], z_ref[...]
        rms = lax.rsqrt(jnp.mean(ov * ov, axis=-1, keepdims=True) + EPS)
        m_ref[...] = (((ov * rms) * w_ref[...]) * (z * _sigmoid(z))).astype(bf16)

    return _pcall(
        body, name="gdn_gate", grid=(HEADS,),
        in_specs=[pl.BlockSpec((L, LANE), lambda j: (0, j)), pl.BlockSpec((L, LANE), lambda j: (0, OFF_ZG // LANE + j)),
                  pl.BlockSpec((1, LANE), lambda j: (0, 0))],
        out_specs=pl.BlockSpec((L, LANE), lambda j: (0, j)),
        out_shape=jax.ShapeDtypeStruct((L, GDN_WIDTH), bf16),
        compiler_params=_cparams("parallel"),
    )(o, proj, gnw)


def _conv3(u, cw_ref):
    return cw_ref[2:3, :] * u + cw_ref[1:2, :] * _shift_down(u, 1) + cw_ref[0:1, :] * _shift_down(u, 2)


def _conv_specs(L):
    blk = lambda off: pl.BlockSpec((L, LANE), lambda j, off=off: (0, off // LANE + j))
    return [blk(OFF_B), blk(OFF_C), blk(OFF_HC), blk(OFF_ZC),
            pl.BlockSpec((3, LANE), lambda j: (0, j)), pl.BlockSpec((1, LANE), lambda j: (0, j))]


def _conv_fwd(proj, cw, cb):
    L = proj.shape[0]

    def body(b_ref, c_ref, h_ref, z_ref, cw_ref, cb_ref, m_ref):
        z = z_ref[...]
        cv = _conv3(c_ref[...] * h_ref[...], cw_ref) + cb_ref[...]
        m_ref[...] = ((b_ref[...] * cv) * (z * _sigmoid(z))).astype(bf16)

    return _pcall(
        body, name="conv_fwd", grid=(CONV_WIDTH // LANE,),
        in_specs=_conv_specs(L), out_specs=pl.BlockSpec((L, LANE), lambda j: (0, j)),
        out_shape=jax.ShapeDtypeStruct((L, CONV_WIDTH), bf16),
        compiler_params=_cparams("parallel"),
    )(proj, proj, proj, proj, cw, cb)


def _out_proj_loss(x, mix_a, mix_b, wo, fw, tgt):
    L = x.shape[0]
    tm = min(256, L)

    def body(x_ref, ma_ref, mb_ref, wo_ref, fw_ref, t_ref, dy_ref, dyb_ref, dma_ref, dmb_ref, gfw_ref, loss_ref):
        @pl.when(pl.program_id(0) == 0)
        def _():
            gfw_ref[...] = jnp.zeros_like(gfw_ref)
            loss_ref[...] = jnp.zeros_like(loss_ref)
        y = x_ref[...] + jnp.dot(ma_ref[...], wo_ref[:GDN_WIDTH, :], preferred_element_type=f32) \
            + jnp.dot(mb_ref[...], wo_ref[GDN_WIDTH:, :], preferred_element_type=f32)
        r = lax.rsqrt(jnp.mean(y * y, axis=-1, keepdims=True) + EPS)
        yh = y * r
        fwv = fw_ref[...]
        diff = yh * fwv - t_ref[...]
        loss_ref[...] += jnp.sum(jnp.sum(diff * diff, axis=-1, keepdims=True), axis=0, keepdims=True) * (0.5 / D_MODEL)
        dout = diff * (1.0 / D_MODEL)
        gfw_ref[...] += jnp.sum(dout * yh, axis=0, keepdims=True)
        dyh = dout * fwv
        dy = r * (dyh - yh * jnp.mean(dyh * yh, axis=-1, keepdims=True))
        dy_ref[...] = dy
        dyb = dy.astype(bf16)
        dyb_ref[...] = dyb
        dma_ref[...] = lax.dot_general(dyb, wo_ref[:GDN_WIDTH, :], (((1,), (1,)), ((), ())), preferred_element_type=f32)
        dmb_ref[...] = lax.dot_general(dyb, wo_ref[GDN_WIDTH:, :], (((1,), (1,)), ((), ())), preferred_element_type=f32)

    row = lambda i: (i, 0)
    fix = lambda i: (0, 0)
    act = jax.ShapeDtypeStruct((L, D_MODEL), f32)
    return _pcall(
        body, name="out_proj_loss", grid=(L // tm,),
        in_specs=[pl.BlockSpec((tm, D_MODEL), row), pl.BlockSpec((tm, GDN_WIDTH), row), pl.BlockSpec((tm, CONV_WIDTH), row),
                  pl.BlockSpec((GDN_WIDTH + CONV_WIDTH, D_MODEL), fix), pl.BlockSpec((1, D_MODEL), fix),
                  pl.BlockSpec((tm, D_MODEL), row)],
        out_specs=[pl.BlockSpec((tm, D_MODEL), row), pl.BlockSpec((tm, D_MODEL), row), pl.BlockSpec((tm, GDN_WIDTH), row),
                   pl.BlockSpec((tm, CONV_WIDTH), row), pl.BlockSpec((1, D_MODEL), fix), pl.BlockSpec((1, LANE), fix)],
        out_shape=[act, jax.ShapeDtypeStruct((L, D_MODEL), bf16), act, act,
                   jax.ShapeDtypeStruct((1, D_MODEL), f32), jax.ShapeDtypeStruct((1, LANE), f32)],
        compiler_params=_cparams("arbitrary"),
    )(x, mix_a, mix_b, wo, fw, tgt)


def _tn_matmul(a, b, name):
    L, M = a.shape
    N = b.shape[1]
    tm = 512 if M % 512 == 0 else M

    def body(a_ref, b_ref, o_ref):
        o_ref[...] = lax.dot_general(a_ref[...], b_ref[...], (((0,), (0,)), ((), ())),
                                     preferred_element_type=f32).astype(o_ref.dtype)

    return _pcall(
        body, name=name, grid=(M // tm,),
        in_specs=[pl.BlockSpec((L, tm), lambda i: (0, i)), pl.BlockSpec((L, N), lambda i: (0, 0))],
        out_specs=pl.BlockSpec((tm, N), lambda i: (i, 0)),
        out_shape=jax.ShapeDtypeStruct((M, N), bf16),
        compiler_params=_cparams("parallel"),
    )(a, b)


def _gdn_gate_bwd(o, proj, gnw, dmix_a):
    L = o.shape[0]

    def body(o_ref, z_ref, w_ref, dm_ref, do_ref, dz_ref, gw_ref):
        @pl.when(pl.program_id(0) == 0)
        def _():
            gw_ref[...] = jnp.zeros_like(gw_ref)
        ov, z, dm, wv = o_ref[...], z_ref[...], dm_ref[...], w_ref[...]
        rms = lax.rsqrt(jnp.mean(ov * ov, axis=-1, keepdims=True) + EPS)
        xh = ov * rms
        sg = _sigmoid(z)
        d_on = dm * (z * sg)
        dz_ref[...] = (dm * (xh * wv) * (sg * (1.0 + z * (1.0 - sg)))).astype(bf16)
        gw_ref[...] += jnp.sum(d_on * xh, axis=0, keepdims=True)
        dxh = d_on * wv
        do_ref[...] = rms * (dxh - xh * jnp.mean(dxh * xh, axis=-1, keepdims=True))

    return _pcall(
        body, name="gdn_gate_bwd", grid=(HEADS,),
        in_specs=[pl.BlockSpec((L, LANE), lambda j: (0, j)), pl.BlockSpec((L, LANE), lambda j: (0, OFF_ZG // LANE + j)),
                  pl.BlockSpec((1, LANE), lambda j: (0, 0)), pl.BlockSpec((L, LANE), lambda j: (0, j))],
        out_specs=[pl.BlockSpec((L, LANE), lambda j: (0, j)), pl.BlockSpec((L, LANE), lambda j: (0, j)),
                   pl.BlockSpec((1, LANE), lambda j: (0, 0))],
        out_shape=[jax.ShapeDtypeStruct((L, GDN_WIDTH), f32), jax.ShapeDtypeStruct((L, GDN_WIDTH), bf16),
                   jax.ShapeDtypeStruct((1, LANE), f32)],
        compiler_params=_cparams("arbitrary"),
    )(o, proj, gnw, dmix_a)


def _conv_bwd(proj, cw, cb, dmix_b):
    L = proj.shape[0]

    def body(b_ref, c_ref, h_ref, z_ref, cw_ref, cb_ref, dm_ref, db_ref, dc_ref, dh_ref, dz_ref, gcw_ref, gcb_ref):
        bv, cv_, hv, z, dm = b_ref[...], c_ref[...], h_ref[...], z_ref[...], dm_ref[...]
        u = cv_ * hv
        cv = _conv3(u, cw_ref) + cb_ref[...]
        sg = _sigmoid(z)
        sz = z * sg
        db_ref[...] = (dm * cv * sz).astype(bf16)
        dz_ref[...] = (dm * (bv * cv) * (sg * (1.0 + z * (1.0 - sg)))).astype(bf16)
        dcv = dm * bv * sz
        gcb_ref[...] = jnp.sum(dcv, axis=0, keepdims=True)
        dcv1, dcv2 = _shift_up(dcv, 1), _shift_up(dcv, 2)
        gcw_ref[2:3, :] = jnp.sum(dcv * u, axis=0, keepdims=True)
        gcw_ref[1:2, :] = jnp.sum(dcv1 * u, axis=0, keepdims=True)
        gcw_ref[0:1, :] = jnp.sum(dcv2 * u, axis=0, keepdims=True)
        du = cw_ref[2:3, :] * dcv + cw_ref[1:2, :] * dcv1 + cw_ref[0:1, :] * dcv2
        dc_ref[...] = (du * hv).astype(bf16)
        dh_ref[...] = (du * cv_).astype(bf16)

    col = pl.BlockSpec((L, LANE), lambda j: (0, j))
    act = jax.ShapeDtypeStruct((L, CONV_WIDTH), bf16)
    return _pcall(
        body, name="conv_bwd", grid=(CONV_WIDTH // LANE,),
        in_specs=_conv_specs(L) + [col],
        out_specs=[col, col, col, col, pl.BlockSpec((3, LANE), lambda j: (0, j)), pl.BlockSpec((1, LANE), lambda j: (0, j))],
        out_shape=[act, act, act, act, jax.ShapeDtypeStruct((3, CONV_WIDTH), f32), jax.ShapeDtypeStruct((1, CONV_WIDTH), f32)],
        compiler_params=_cparams("parallel"),
    )(proj, proj, proj, proj, cw, cb, dmix_b)


def _gdn_bwd(qkv, sc, gr, u_all, w_all, vn_all, t_all, sp_all, do_all):
    L = qkv.shape[0]
    nc = L // CHUNK
    W = GDN_WIDTH
    cps = GDN_CPS_BWD if nc % GDN_CPS_BWD == 0 else 1
    rows_per_step = cps * CHUNK
    nsteps = nc // cps

    def body(qkv_ref, sc_ref, gr_ref, u_ref, w_ref, vn_ref, t_ref, sp_ref, do_ref, dqkv_ref, dsc_ref, dgr_ref, ds_scr):
        @pl.when(pl.program_id(0) == 0)
        def _():
            ds_scr[...] = jnp.zeros_like(ds_scr)
        HS = range(cps * HEADS)
        hd = [i % HEADS for i in HS]
        rs = [slice((i // HEADS) * CHUNK, (i // HEADS + 1) * CHUNK) for i in HS]
        cs = [slice(hd[i] * HEAD_DIM, (hd[i] + 1) * HEAD_DIM) for i in HS]
        q = [qkv_ref[rs[i], hd[i] * HEAD_DIM:(hd[i] + 1) * HEAD_DIM] for i in HS]
        k = [qkv_ref[rs[i], W + hd[i] * HEAD_DIM:W + (hd[i] + 1) * HEAD_DIM] for i in HS]
        v = [qkv_ref[rs[i], 2 * W + hd[i] * HEAD_DIM:2 * W + (hd[i] + 1) * HEAD_DIM] for i in HS]
        hsc = [_head_scalars(sc_ref[rs[i], :], gr_ref, hd[i], i // HEADS) for i in HS]
        beta, gcc, gl, dmat, dmat_t = ([x[i] for x in hsc] for i in range(5))
        ii, jj = hsc[0][5], hsc[0][6]
        eg = [jnp.exp(gcc[h]) for h in HS]
        ekl = [jnp.exp(gl[h] - gcc[h]) for h in HS]
        egl = [jnp.exp(gl[h]) for h in HS]
        kb = [k[h] * beta[h] for h in HS]
        ks = [k[h] * ekl[h] for h in HS]
        do = [do_ref[rs[h], cs[h]] for h in HS]
        vn = [vn_ref[rs[h], cs[h]] for h in HS]
        s = [sp_ref[h // HEADS, cs[h], :] for h in HS]
        w = [w_ref[rs[h], cs[h]] for h in HS]
        qd = [q[h] * eg[h] for h in HS]

        kq = [_mm_nt(k[h], q[h]) for h in HS]
        p_t = [jnp.where(jj >= ii, kq[h] * dmat_t[h], 0.0) for h in HS]
        ptd = [_mm(p_t[h], do[h]) for h in HS]
        qw =[jnp.concatenate([qd[h], -w[h]], axis=0) for h in HS]
        dsn, dvn, dodv = [None] * len(HS), [None] * len(HS), [None] * len(HS)
        ds_cur = [ds_scr[h] for h in range(HEADS)]
        for ci in reversed(range(cps)):
            IS = range(ci * HEADS, (ci + 1) * HEADS)
            ksd = [_mm(ks[i], ds_cur[hd[i]]) for i in IS]
            for i in IS:
                dsn[i] = ds_cur[hd[i]]
                dvn[i] = ptd[i] + ksd[hd[i]]
                dodv[i] = jnp.concatenate([do[i], dvn[i]], axis=0)
            dsq = [_mm_tn(qw[i], dodv[i]) for i in IS]
            ds_cur = [egl[i] * ds_cur[hd[i]] + dsq[hd[i]] for i in IS]
        for h in range(HEADS):
            ds_scr[h] = ds_cur[h]
        x1 = [_mm_nt(dodv[h], s[h]) for h in HS]
        dks = [_mm_nt(vn[h], dsn[h]) for h in HS]
        dov = [_mm_nt(do[h], vn[h]) for h in HS]
        vdo = [_mm_nt(vn[h], do[h]) for h in HS]
        kk = [_mm_nt(kb[h], k[h]) for h in HS]
        qk = [_mm_nt(q[h], k[h]) for h in HS]
        dgl = [egl[h] * jnp.sum(jnp.sum(s[h] * dsn[h], axis=1, keepdims=True), axis=0, keepdims=True) for h in HS]
        dqd = [x1[h][:CHUNK] for h in HS]
        duw = [jnp.concatenate([dvn[h], -x1[h][CHUNK:]], axis=1) for h in HS]
        tdu = [_mm_tn(t_ref[h // HEADS, hd[h]], duw[h]) for h in HS]
        dvk = [duw[h] + tdu[h] for h in HS]
        uw = [jnp.concatenate([u_ref[rs[h], cs[h]], w[h]], axis=1) for h in HS]
        da = [-jnp.where(ii > jj, _mm_nt(dvk[h], uw[h]), 0.0) for h in HS]
        da_t = [-jnp.where(jj > ii, _mm_nt(uw[h], dvk[h]), 0.0) for h in HS]
        dp = [jnp.where(ii >= jj, dov[h], 0.0) for h in HS]
        dp_t = [jnp.where(jj >= ii, vdo[h], 0.0) for h in HS]
        r1 = [_mm(jnp.concatenate([da[h] * dmat[h], dp[h] * dmat[h]], axis=0), k[h]) for h in HS]
        dk1 = [_mm(jnp.concatenate([da_t[h] * dmat_t[h], dp_t[h] * dmat_t[h]], axis=1),
                   jnp.concatenate([kb[h], q[h]], axis=0)) for h in HS]
        lane = _lanes((CHUNK, LANE))
        for ci in range(cps):
            dsc = jnp.zeros((CHUNK, LANE), f32)
            for i in range(ci * HEADS, (ci + 1) * HEADS):
                h = hd[i]
                a = jnp.where(ii > jj, kk[i] * dmat[i], 0.0)
                p = jnp.where(ii >= jj, qk[i] * dmat[i], 0.0)
                gmat = da[i] * a + dp[i] * p
                dvb, dkbg = dvk[i][:, :HEAD_DIM], dvk[i][:, HEAD_DIM:]
                kbg = kb[i] * eg[i]
                dkb = r1[i][:CHUNK] + dkbg * eg[i]
                dq = r1[i][CHUNK:] + dqd[i] * eg[i]
                dk = dk1[i] + dks[i] * ekl[i] + dkb * beta[i]
                dbeta = jnp.sum(dkb * k[i] + dvb * v[i], axis=1, keepdims=True)
                ksum = jnp.sum(dks[i] * ks[i], axis=1, keepdims=True)
                dgl_tot = dgl[i] + jnp.sum(ksum, axis=0, keepdims=True)
                dgc = (jnp.sum(gmat, axis=1, keepdims=True) + jnp.sum(dqd[i] * qd[i] + dkbg * kbg, axis=1, keepdims=True)
                       - ksum)
                dgc = dgc + jnp.where(_rows(dgc.shape) == CHUNK - 1, dgl_tot, 0.0)
                dqkv_ref[rs[i], h * HEAD_DIM:(h + 1) * HEAD_DIM] = dq
                dqkv_ref[rs[i], W + h * HEAD_DIM:W + (h + 1) * HEAD_DIM] = dk
                dqkv_ref[rs[i], 2 * W + h * HEAD_DIM:2 * W + (h + 1) * HEAD_DIM] = dvb * beta[i]
                dsc = jnp.where(lane == h, dbeta, jnp.where(lane == HEADS + h, dgc, dsc))
                dgr_ref[ci, h:h + 1, :] = jnp.sum(gmat, axis=0, keepdims=True)
            dsc_ref[ci * CHUNK:(ci + 1) * CHUNK, :] = dsc

    row = lambda c: (nsteps - 1 - c, 0)
    lead3 = lambda c: (nsteps - 1 - c, 0, 0)
    return _pcall(
        body, name="gdn_bwd", grid=(nsteps,),
        in_specs=[pl.BlockSpec((rows_per_step, 3 * W), row), pl.BlockSpec((rows_per_step, LANE), row),
                  pl.BlockSpec((cps, HEADS, CHUNK), lead3),
                  pl.BlockSpec((rows_per_step, W), row), pl.BlockSpec((rows_per_step, W), row),
                  pl.BlockSpec((rows_per_step, W), row),
                  pl.BlockSpec((cps, HEADS, CHUNK, CHUNK), lambda c: (nsteps - 1 - c, 0, 0, 0)),
                  pl.BlockSpec((cps, W, HEAD_DIM), lead3), pl.BlockSpec((rows_per_step, W), row)],
        out_specs=[pl.BlockSpec((rows_per_step, 3 * W), row), pl.BlockSpec((rows_per_step, LANE), row),
                   pl.BlockSpec((cps, HEADS, CHUNK), lead3)],
        out_shape=[jax.ShapeDtypeStruct((L, 3 * W), f32), jax.ShapeDtypeStruct((L, LANE), f32),
                   jax.ShapeDtypeStruct((nc, HEADS, CHUNK), f32)],
        scratch_shapes=[pltpu.VMEM((HEADS, HEAD_DIM, HEAD_DIM), f32)],
        compiler_params=_cparams("arbitrary"),
    )(qkv, sc, gr, u_all, w_all, vn_all, t_all, sp_all, do_all)


def _qkv_bwd(proj, cw, dn):
    L = proj.shape[0]

    def body(x_ref, cw_ref, dn_ref, dx_ref, gcw_ref):
        j = pl.program_id(0)
        x, dn_v = x_ref[...], dn_ref[...]
        c = _conv4(x, cw_ref)
        sg = _sigmoid(c)
        a = c * sg
        rn = lax.rsqrt(jnp.sum(a * a, axis=1, keepdims=True) + EPS)
        scale = jnp.where(j < HEADS, HEAD_DIM ** -0.5, 1.0).astype(f32)
        da_n = (scale * rn) * (dn_v - a * ((rn * rn) * jnp.sum(dn_v * a, axis=1, keepdims=True)))
        da = jnp.where(j < 2 * HEADS, da_n, dn_v)
        dc = da * (sg * (1.0 + c * (1.0 - sg)))
        dc1, dc2, dc3 = _shift_up(dc, 1), _shift_up(dc, 2), _shift_up(dc, 3)
        gcw_ref[3:4, :] = jnp.sum(dc * x, axis=0, keepdims=True)
        gcw_ref[2:3, :] = jnp.sum(dc1 * x, axis=0, keepdims=True)
        gcw_ref[1:2, :] = jnp.sum(dc2 * x, axis=0, keepdims=True)
        gcw_ref[0:1, :] = jnp.sum(dc3 * x, axis=0, keepdims=True)
        dx = cw_ref[3:4, :] * dc + cw_ref[2:3, :] * dc1 + cw_ref[1:2, :] * dc2 + cw_ref[0:1, :] * dc3
        dx_ref[...] = dx.astype(bf16)

    col = pl.BlockSpec((L, LANE), lambda j: (0, j))
    wspec = pl.BlockSpec((4, LANE), lambda j: (0, j))
    return _pcall(
        body, name="qkv_bwd", grid=(3 * HEADS,),
        in_specs=[col, wspec, col], out_specs=[col, wspec],
        out_shape=[jax.ShapeDtypeStruct((L, 3 * GDN_WIDTH), bf16), jax.ShapeDtypeStruct((4, 3 * GDN_WIDTH), f32)],
        compiler_params=_cparams("parallel"),
    )(proj, cw, dn)


def _scalars_bwd(proj, alog_p, dtb_p, dsc, dgr_col):
    L = proj.shape[0]

    def body(x_ref, al_ref, dt_ref, dsc_ref, dgr_ref, dba_ref, gs_ref):
        x, dsc_v = x_ref[...], dsc_ref[...]
        lane = _lanes(x.shape)
        dec = (lane >= HEADS) & (lane < 2 * HEADS)
        dg = jnp.where(dec, dsc_v - dgr_ref[...], 0.0)
        rc = _rows(x.shape) & (CHUNK - 1)
        for s in (1, 2, 4, 8, 16, 32):
            dg = dg + jnp.where(rc + s < CHUNK, pltpu.roll(dg, L - s, 0), 0.0)
        xa = x + dt_ref[...]
        ea = jnp.exp(al_ref[...])
        g = -ea * _softplus(xa)
        da = dg * (-ea) * _sigmoid(xa)
        beta = _sigmoid(x)
        db = dsc_v * beta * (1.0 - beta)
        dba_ref[...] = jnp.where(lane < HEADS, db, jnp.where(dec, da, 0.0)).astype(bf16)
        g_al = jnp.sum(jnp.where(dec, dg * g, 0.0), axis=0, keepdims=True)
        g_dt = jnp.sum(jnp.where(dec, da, 0.0), axis=0, keepdims=True)
        row8 = _rows(gs_ref.shape)
        gs = jnp.where(row8 == 0, g_al, jnp.where(row8 == 1, g_dt, 0.0))
        gs_ref[...] = pltpu.roll(gs, LANE - HEADS, 1)

    full = pl.BlockSpec((L, LANE), lambda i: (0, 0))
    vec = pl.BlockSpec((1, LANE), lambda i: (0, 0))
    return _pcall(
        body, name="scalars_bwd", grid=(1,),
        in_specs=[pl.BlockSpec((L, LANE), lambda i: (0, OFF_BA // LANE)), vec, vec, full, full],
        out_specs=[full, pl.BlockSpec((8, LANE), lambda i: (0, 0))],
        out_shape=[jax.ShapeDtypeStruct((L, LANE), bf16), jax.ShapeDtypeStruct((8, LANE), f32)],
        compiler_params=_cparams("arbitrary"),
    )(proj, alog_p, dtb_p, dsc, dgr_col)


def _input_grad(pieces, offs, wpad, x, nw, dy):
    L = x.shape[0]
    tm = min(512, L)
    npc = len(pieces)

    def body(*refs):
        p_refs = refs[:npc]
        w_hbm, x_ref, nw_ref, dy_ref, gx_ref, gnw_ref, w_vmem, sem = refs[npc:]

        @pl.when(pl.program_id(0) == 0)
        def _():
            cp = pltpu.make_async_copy(w_hbm, w_vmem, sem)
            cp.start()
            cp.wait()
            gnw_ref[...] = jnp.zeros_like(gnw_ref)
        dh = None
        for p_ref, off in zip(p_refs, offs):
            wd = p_ref.shape[1]
            part = jnp.dot(p_ref[...], w_vmem[off:off + wd, :], preferred_element_type=f32)
            dh = part if dh is None else dh + part
        xv, nwv = x_ref[...], nw_ref[...]
        r = lax.rsqrt(jnp.mean(xv * xv, axis=-1, keepdims=True) + EPS)
        xh = xv * r
        gnw_ref[...] += jnp.sum(dh * xh, axis=0, keepdims=True)
        dxh = dh * nwv
        gx_ref[...] = dy_ref[...] + r * (dxh - xh * jnp.mean(dxh * xh, axis=-1, keepdims=True))

    row = lambda i: (i, 0)
    fix = lambda i: (0, 0)
    return _pcall(
        body, name="input_grad", grid=(L // tm,),
        in_specs=[pl.BlockSpec((tm, p.shape[1]), row) for p in pieces] + [
            ANY, pl.BlockSpec((tm, D_MODEL), row), pl.BlockSpec((1, D_MODEL), fix), pl.BlockSpec((tm, D_MODEL), row)],
        out_specs=[pl.BlockSpec((tm, D_MODEL), row), pl.BlockSpec((1, D_MODEL), fix)],
        out_shape=[jax.ShapeDtypeStruct((L, D_MODEL), f32), jax.ShapeDtypeStruct((1, D_MODEL), f32)],
        scratch_shapes=[pltpu.VMEM(wpad.shape, bf16), pltpu.SemaphoreType.DMA(())],
        compiler_params=_cparams("arbitrary"),
    )(*pieces, wpad, x, nw, dy)


def _adamw_reduce(parts, w, m, v, name):
    R, C = w.shape
    n_parts = parts.shape[0]
    tr = 128 if R % 128 == 0 else R
    c1 = 1.0 - ADAM_B1 ** ADAM_STEP
    c2 = 1.0 - ADAM_B2 ** ADAM_STEP

    def body(p_ref, w_ref, m_ref, v_ref, g_ref, d_ref, nm_ref, nv_ref):
        g = p_ref[0].astype(f32)
        for s in range(1, n_parts):
            g = g + p_ref[s].astype(f32)
        nm = ADAM_B1 * m_ref[...] + (1.0 - ADAM_B1) * g
        nv = ADAM_B2 * v_ref[...] + (1.0 - ADAM_B2) * (g * g)
        g_ref[...] = g
        nm_ref[...] = nm
        nv_ref[...] = nv
        d_ref[...] = -ADAM_LR * ((nm / c1) / (jnp.sqrt(nv / c2) + ADAM_EPS) + ADAM_WD * w_ref[...])

    blk = pl.BlockSpec((tr, C), lambda i: (i, 0))
    out = jax.ShapeDtypeStruct((R, C), f32)
    return _pcall(
        body, name=name, grid=(R // tr,),
        in_specs=[pl.BlockSpec((n_parts, tr, C), lambda i: (0, i, 0)), blk, blk, blk],
        out_specs=[blk] * 4, out_shape=[out] * 4,
        compiler_params=_cparams("parallel"),
    )(parts, w, m, v)


SMALL_SLOTS = ((0, D_MODEL), (D_MODEL, D_MODEL), (2 * D_MODEL, D_MODEL), (3 * D_MODEL, LANE),
               (3 * D_MODEL + LANE, HEADS), (3 * D_MODEL + 2 * LANE, HEADS))
SMALL_LOSS = 3 * D_MODEL + 3 * LANE
SMALL_W = SMALL_LOSS + LANE


def _pack_small(gs, after):
    def body(nw_ref, cb_ref, fw_ref, gn_ref, sc_ref, ls_ref, after_ref, o_ref):
        for ref, (start, width) in zip((nw_ref, cb_ref, fw_ref, gn_ref), SMALL_SLOTS[:4]):
            o_ref[:, start:start + width] = ref[...]
        o_ref[:, SMALL_SLOTS[4][0]:SMALL_SLOTS[4][0] + LANE] = sc_ref[0:1, :]
        o_ref[:, SMALL_SLOTS[5][0]:SMALL_SLOTS[5][0] + LANE] = sc_ref[1:2, :]
        o_ref[:, SMALL_LOSS:SMALL_W] = ls_ref[...]

    vm = pl.BlockSpec(memory_space=pltpu.VMEM)
    return _pcall(body, name="pack_small_grads", out_shape=jax.ShapeDtypeStruct((1, SMALL_W), f32),
                  in_specs=[vm] * 6 + [ANY], out_specs=vm)(*gs, after)


def _adamw_small(parts, ws, ms, vs):
    c1 = 1.0 - ADAM_B1 ** ADAM_STEP
    c2 = 1.0 - ADAM_B2 ** ADAM_STEP
    np_ = len(ws)

    def body(*refs):
        p_ref = refs[0]
        w_refs, m_refs, v_refs = refs[1:1 + np_], refs[1 + np_:1 + 2 * np_], refs[1 + 2 * np_:1 + 3 * np_]
        outs = refs[1 + 3 * np_:]
        g_refs, d_refs, nm_refs, nv_refs = (outs[i * np_:(i + 1) * np_] for i in range(4))
        loss_ref = outs[4 * np_]

        def total(start, width):
            t = p_ref[0, :, start:start + width]
            for s in range(1, N_DEV):
                t = t + p_ref[s, :, start:start + width]
            return t

        for i, (start, width) in enumerate(SMALL_SLOTS):
            g = total(start, width)
            nm = ADAM_B1 * m_refs[i][...] + (1.0 - ADAM_B1) * g
            nv = ADAM_B2 * v_refs[i][...] + (1.0 - ADAM_B2) * (g * g)
            g_refs[i][...] = g
            nm_refs[i][...] = nm
            nv_refs[i][...] = nv
            d_refs[i][...] = -ADAM_LR * ((nm / c1) / (jnp.sqrt(nv / c2) + ADAM_EPS) + ADAM_WD * w_refs[i][...])
        loss_ref[...] = total(SMALL_LOSS, LANE)

    vm = pl.BlockSpec(memory_space=pltpu.VMEM)
    shapes = [jax.ShapeDtypeStruct(w.shape, f32) for w in ws]
    res = _pcall(body, name="adamw_small", out_shape=shapes * 4 + [jax.ShapeDtypeStruct((1, LANE), f32)],
                 in_specs=[vm] * (1 + 3 * np_), out_specs=[vm] * (4 * np_ + 1))(parts, *ws, *ms, *vs)
    return [res[i * np_:(i + 1) * np_] for i in range(4)], res[4 * np_]


def _adamw_w_in(parts, w3, m3, v3):
    n_parts, n, _ = parts.shape
    c1 = 1.0 - ADAM_B1 ** ADAM_STEP
    c2 = 1.0 - ADAM_B2 ** ADAM_STEP

    def body(p_ref, w_ref, m_ref, v_ref, g_ref, d_ref, nm_ref, nv_ref):
        g = p_ref[0].astype(f32)
        for s in range(1, n_parts):
            g = g + p_ref[s].astype(f32)
        nm = ADAM_B1 * m_ref[:, 0, :] + (1.0 - ADAM_B1) * g
        nv = ADAM_B2 * v_ref[:, 0, :] + (1.0 - ADAM_B2) * (g * g)
        g_ref[:, 0, :] = g
        nm_ref[:, 0, :] = nm
        nv_ref[:, 0, :] = nv
        d_ref[:, 0, :] = -ADAM_LR * ((nm / c1) / (jnp.sqrt(nv / c2) + ADAM_EPS) + ADAM_WD * w_ref[:, 0, :])

    tile = 2 * COL_TILE
    blk = pl.BlockSpec((n, 1, tile), lambda j: (0, 0, j))
    out = jax.ShapeDtypeStruct((n, 1, D_MODEL), f32)
    return _pcall(
        body, name="adamw_w_in", grid=(D_MODEL // tile,),
        in_specs=[pl.BlockSpec((n_parts, n, tile), lambda j: (0, 0, j)), blk, blk, blk],
        out_specs=[blk] * 4, out_shape=[out] * 4,
        compiler_params=_cparams("parallel"),
    )(parts, w3, m3, v3)


def _pad_lanes(vec8, start):
    return jnp.pad(vec8.reshape(1, -1), ((0, 0), (start, LANE - start - vec8.size)))


def kernel(x, norm_in_w, w_in, conv_qkv_w, A_log, dt_bias, gdn_norm_w, conv_w, conv_b, w_out, final_norm_w, loss_target, m_norm_in_w, m_w_in, m_conv_qkv_w, m_A_log, m_dt_bias, m_gdn_norm_w, m_conv_w, m_conv_b, m_w_out, m_final_norm_w, v_norm_in_w, v_w_in, v_conv_qkv_w, v_A_log, v_dt_bias, v_gdn_norm_w, v_conv_w, v_conv_b, v_w_out, v_final_norm_w):
    L = x.shape[1]
    nc = L // CHUNK
    xs = x[0]
    tgt = loss_target[0]
    fnw = final_norm_w.reshape(1, D_MODEL)

    as_rows = lambda a: jnp.transpose(a, (2, 0, 1))
    win_g, cqkv_g, cw_g = _all_gather([_cast_w_in(as_rows(w_in)), conv_qkv_w[0], conv_w[0]], "gather_weights")
    wpad = _relayout_w_in(win_g)
    cqkv = jnp.concatenate([cqkv_g[d] for d in range(N_DEV)], axis=1)
    cw = jnp.concatenate([cw_g[d] for d in range(N_DEV)], axis=1)
    alog_p = _pad_lanes(A_log, HEADS)
    dtb_p = _pad_lanes(dt_bias, HEADS)
    me_flat, me_chip = _flat(*_mesh_pos()), 2 * lax.axis_index("x") + lax.axis_index("y")
    tok = lambda started: started[4][0:1, 0:1]
    wo_own = w_out[0].astype(bf16)
    wo_started = _spread_start(wo_own, wpad, "gather", "gather_w_out_start")

    proj, h = _in_proj(xs, norm_in_w + tok(wo_started), wpad)
    qkv = _qkv_act(proj, cqkv)
    sc, gr = _scalars(proj, alog_p, dtb_p)
    o, u_all, w_all, vn_all, t_all, sp_all = _gdn_fwd(qkv, sc, gr)
    mix_a = _gdn_gate(o, proj, gdn_norm_w)
    mix_b = _conv_fwd(proj, cw, conv_b)
    wo = _own_slot(_spread_wait(wo_started, mix_b, "gather", "gather_w_out_wait"), wo_own, me_flat).reshape(-1, D_MODEL)
    dy, dyb, dmix_a, dmix_b, g_fnw, loss_v = _out_proj_loss(xs, mix_a, mix_b, wo, fnw, tgt)

    g_wout = jnp.concatenate([_tn_matmul(mix_a, dyb, "grad_w_out_a"), _tn_matmul(mix_b, dyb, "grad_w_out_b")], axis=0)
    g_wout = g_wout.reshape(N_DEV, -1, D_MODEL)
    g_wout_own = lax.dynamic_index_in_dim(g_wout, me_flat, 0, keepdims=False)
    gwo_started = _spread_start(g_wout, dyb, "scatter", "exchange_grad_w_out_start")
    do, dzg, g_gnw = _gdn_gate_bwd(o, proj, gdn_norm_w + tok(gwo_started), dmix_a)
    d_b, d_c, d_hc, d_zc, g_cw, g_cb = _conv_bwd(proj, cw, conv_b, dmix_b)
    dqkv_n, dsc, dgr = _gdn_bwd(qkv, sc, gr, u_all, w_all, vn_all, t_all, sp_all, do)
    dqkv, g_cqkv = _qkv_bwd(proj, cqkv, dqkv_n)
    dgr_col = jnp.pad(dgr.transpose(0, 2, 1).reshape(L, HEADS), ((0, 0), (HEADS, LANE - 2 * HEADS)))
    dba, g_sc = _scalars_bwd(proj, alog_p, dtb_p, dsc, dgr_col)
    pieces = [dqkv, dzg, dba, d_b, d_c, d_hc, d_zc]
    offs = [OFF_QKV, OFF_ZG, OFF_BA, OFF_B, OFF_C, OFF_HC, OFF_ZC]
    g_parts = [_tn_matmul(p, h, "grad_w_in_%d" % i) for i, p in enumerate(pieces)]
    g_win_blk = _grad_blocks(g_parts)

    (p_win,) = _pair_exchange([g_win_blk], "exchange_grads_pair")
    s_win = _pair_sum(g_win_blk, p_win, "pair_sum_w_in")
    s_win_own = lax.dynamic_index_in_dim(s_win, me_chip, 0, keepdims=False)
    r_cqkv, r_cw = _all_to_all(
        [g_cqkv.reshape(4, N_DEV, -1).transpose(1, 0, 2), g_cw.reshape(3, N_DEV, -1).transpose(1, 0, 2)],
        "exchange_small_sharded_grads")
    gwi_started = _spread_start(s_win, r_cw, "chips", "exchange_grads_chips_start")
    grad_x, g_nw = _input_grad(pieces, offs, wpad, xs, norm_in_w + tok(gwi_started), dy)

    r_wout = _own_slot(_spread_wait(gwo_started, grad_x, "scatter", "exchange_grad_w_out_wait"), g_wout_own, me_flat)
    upd_wout =_adamw_reduce(r_wout, w_out[0], m_w_out[0], v_w_out[0], "adamw_w_out")
    upd_cqkv = _adamw_reduce(r_cqkv, conv_qkv_w[0], m_conv_qkv_w[0], v_conv_qkv_w[0], "adamw_conv_qkv_w")
    upd_cw = _adamw_reduce(r_cw, conv_w[0], m_conv_w[0], v_conv_w[0], "adamw_conv_w")

    r_win = _own_slot(_spread_wait(gwi_started, upd_cw[0], "chips", "exchange_grads_chips_wait"), s_win_own, me_chip)
    upd_win = [jnp.transpose(a, (1, 2, 0)) for a in _adamw_w_in(r_win, as_rows(w_in), as_rows(m_w_in), as_rows(v_w_in))]

    small_g = _pack_small([g_nw, g_cb, g_fnw, g_gnw, g_sc, loss_v], r_win)
    (small_all,) = _all_gather([small_g], "gather_small_grads")
    fvec = lambda a: a.reshape(1, D_MODEL)
    upd_small, loss_sum = _adamw_small(
        small_all,
        [norm_in_w, conv_b, fvec(final_norm_w), gdn_norm_w, A_log, dt_bias],
        [m_norm_in_w, m_conv_b, fvec(m_final_norm_w), m_gdn_norm_w, m_A_log, m_dt_bias],
        [v_norm_in_w, v_conv_b, fvec(v_final_norm_w), v_gdn_norm_w, v_A_log, v_dt_bias])

    outs = [loss_sum[0, 0], grad_x[None]]
    for k in range(4):
        nw_k, cb_k, fw_k, gn_k, al_k, dt_k = upd_small[k]
        outs += [nw_k, upd_win[k], upd_cqkv[k][None], al_k, dt_k, gn_k,
                 upd_cw[k][None], cb_k, upd_wout[k][None], fw_k.reshape(D_MODEL)]
    return tuple(outs)
```

```python
import functools
import math

import jax
import jax.numpy as jnp
from jax import lax
from jax.experimental import pallas as pl
from jax.experimental.pallas import tpu as pltpu

f32 = jnp.float32
bf16 = jnp.bfloat16

N_DEV = 8
D_MODEL = 1024
HEADS = 8
HEAD_DIM = 128
CHUNK = 64
GDN_CPS = 4
GDN_CPS_BWD = 1
GDN_WIDTH = HEADS * HEAD_DIM
CONV_WIDTH = 1024
PROJ_WIDTH = 8208
SHARD_W = PROJ_WIDTH // N_DEV
EPS = 1e-6

NAT_SMALL_END = 4112
PAD_COLS = 240
OFF_QKV, OFF_ZG, OFF_BA, OFF_B, OFF_C, OFF_HC, OFF_ZC = 0, 3072, 4096, 4352, 5376, 6400, 7424
PROJ_PAD = 8448
LANE = 128
ELT_W = 256

ADAM_LR, ADAM_B1, ADAM_B2, ADAM_EPS, ADAM_WD, ADAM_STEP = 0.001, 0.9, 0.999, 1e-08, 0.01, 10

VMEM_LIMIT = 56 * 1024 * 1024

MESH = pl.DeviceIdType.MESH
ANY = pl.BlockSpec(memory_space=pl.ANY)


def _pcall(body, **kw):
    return pl.pallas_call(body, **kw)


def _cparams(*sem):
    return pltpu.CompilerParams(dimension_semantics=sem if sem else None, vmem_limit_bytes=VMEM_LIMIT)


def _mm(a, b):
    return jnp.dot(a.astype(bf16), b.astype(bf16), preferred_element_type=f32)


def _mm_nt(a, b):
    return lax.dot_general(a.astype(bf16), b.astype(bf16), (((1,), (1,)), ((), ())), preferred_element_type=f32)


def _mm_tn(a, b):
    return lax.dot_general(a.astype(bf16), b.astype(bf16), (((0,), (0,)), ((), ())), preferred_element_type=f32)


def _rows(shape):
    return lax.broadcasted_iota(jnp.int32, shape, 0)


def _lanes(shape):
    return lax.broadcasted_iota(jnp.int32, shape, 1)


def _shift_down(x, s):
    if s == 0:
        return x
    return jnp.where(_rows(x.shape) >= s, pltpu.roll(x, s, 0), 0.0)


def _shift_up(x, s):
    if s == 0:
        return x
    n = x.shape[0]
    return jnp.where(_rows(x.shape) < n - s, pltpu.roll(x, n - s, 0), 0.0)


def _sigmoid(x):
    return jax.nn.sigmoid(x)


def _softplus(x):
    e = jnp.exp(-jnp.abs(x))
    small = e * (1.0 - e * (0.5 - e * (1.0 / 3.0)))
    return jnp.maximum(x, 0.0) + jnp.where(e < 0.01, small, jnp.log(1.0 + e))


def _mesh_pos():
    return lax.axis_index("x"), lax.axis_index("y"), lax.axis_index("c")


def _flat(px, py, pc):
    return 4 * px + 2 * py + pc


def _all_gather(xs, name):
    n = len(xs)

    def body(*refs):
        x_refs, o_refs = refs[:n], refs[n:2 * n]
        send_sems, recv_sems, local_sems = refs[2 * n:]
        x, y, c = _mesh_pos()
        me, sibling = (x, y, c), (x, y, 1 - c)
        flip = lambda v, bit: v + bit - 2 * v * bit
        nbr_a = (flip(x, 1 - c), flip(y, c))
        nbr_b = (flip(x, c), flip(y, 1 - c))
        diag = (1 - x, 1 - y)

        def copy(a, k, block, to, src=None):
            dst = o_refs[a].at[_flat(*block)]
            return pltpu.make_async_remote_copy(
                src_ref=dst if src is None else src, dst_ref=dst,
                send_sem=send_sems.at[a, k], recv_sem=recv_sems.at[a, k], device_id=to, device_id_type=MESH)

        mine, sent = [], []

        def go(cp):
            cp.start()
            sent.append(cp)

        for a in range(n):
            cp = pltpu.make_async_copy(x_refs[a], o_refs[a].at[_flat(*me)], local_sems.at[a])
            cp.start()
            mine.append(cp)
            go(copy(a, 0, me, sibling, src=x_refs[a]))
            go(copy(a, 1, me, (*nbr_a, c), src=x_refs[a]))
            go(copy(a, 2, me, (*nbr_b, c), src=x_refs[a]))
        for a in range(n):
            copy(a, 1, (*nbr_a, c), me).wait_recv()
            go(copy(a, 3, (*nbr_a, c), (*nbr_b, c)))
            go(copy(a, 4, (*nbr_a, c), sibling))
        for a in range(n):
            copy(a, 2, (*nbr_b, c), me).wait_recv()
            go(copy(a, 5, (*nbr_b, c), sibling))
        for a in range(n):
            copy(a, 3, (*diag, c), me).wait_recv()
            go(copy(a, 6, (*diag, c), sibling))
        for a in range(n):
            copy(a, 0, sibling, me).wait_recv()
            copy(a, 4, (*nbr_b, 1 - c), me).wait_recv()
            copy(a, 5, (*nbr_a, 1 - c), me).wait_recv()
            copy(a, 6, (*diag, 1 - c), me).wait_recv()
        for cp in sent:
            cp.wait_send()
        for cp in mine:
            cp.wait()

    outs = _pcall(
        body, name=name,
        out_shape=[jax.ShapeDtypeStruct((N_DEV,) + a.shape, a.dtype) for a in xs],
        in_specs=[ANY] * n, out_specs=[ANY] * n,
        scratch_shapes=[pltpu.SemaphoreType.DMA((n, 7)), pltpu.SemaphoreType.DMA((n, 7)), pltpu.SemaphoreType.DMA((n,))],
    )(*xs)
    return list(outs)


def _all_to_all(gs, name):
    n = len(gs)

    def body(*refs):
        g_refs, o_refs = refs[:n], refs[n:2 * n]
        send_sems, recv_sems, local_sems = refs[2 * n:]
        x, y, c = _mesh_pos()
        me = _flat(x, y, c)
        peers = []
        for k in range(1, N_DEV):
            kx, ky, kc = (k >> 2) & 1, (k >> 1) & 1, k & 1
            px = (1 - x) if kx else x
            py = (1 - y) if ky else y
            pc = (1 - c) if kc else c
            peers.append((px, py, pc))

        def copy(a, k):
            peer = peers[k - 1]
            return pltpu.make_async_remote_copy(
                src_ref=g_refs[a].at[_flat(*peer)], dst_ref=o_refs[a].at[me],
                send_sem=send_sems.at[a, k - 1], recv_sem=recv_sems.at[a, k - 1], device_id=peer, device_id_type=MESH)

        def arrival(a, k):
            peer = peers[k - 1]
            return pltpu.make_async_remote_copy(
                src_ref=g_refs[a].at[me], dst_ref=o_refs[a].at[_flat(*peer)],
                send_sem=send_sems.at[a, k - 1], recv_sem=recv_sems.at[a, k - 1], device_id=peer, device_id_type=MESH)

        mine, sent = [], []
        for a in range(n):
            cp = pltpu.make_async_copy(g_refs[a].at[me], o_refs[a].at[me], local_sems.at[a])
            cp.start()
            mine.append(cp)
            for k in range(1, N_DEV):
                cp = copy(a, k)
                cp.start()
                sent.append(cp)
        for a in range(n):
            for k in range(1, N_DEV):
                arrival(a, k).wait_recv()
        for cp in sent:
            cp.wait_send()
        for cp in mine:
            cp.wait()

    outs = _pcall(
        body, name=name,
        out_shape=[jax.ShapeDtypeStruct(a.shape, a.dtype) for a in gs],
        in_specs=[ANY] * n, out_specs=[ANY] * n,
        scratch_shapes=[pltpu.SemaphoreType.DMA((n, 7)), pltpu.SemaphoreType.DMA((n, 7)), pltpu.SemaphoreType.DMA((n,))],
    )(*gs)
    return list(outs)


def _pair_exchange(gs, name):
    n = len(gs)
    chips = [(0, 0), (0, 1), (1, 0), (1, 1)]

    def body(*refs):
        g_refs, o_refs = refs[:n], refs[n:2 * n]
        send_sems, recv_sems = refs[2 * n:]
        x, y, c = _mesh_pos()
        sibling = (x, y, 1 - c)

        def copy(a, i):
            xp, yp = chips[i]
            return pltpu.make_async_remote_copy(
                src_ref=g_refs[a].at[_flat(xp, yp, 1 - c)], dst_ref=o_refs[a].at[i],
                send_sem=send_sems.at[a, i], recv_sem=recv_sems.at[a, i], device_id=sibling, device_id_type=MESH)

        cps = [copy(a, i) for a in range(n) for i in range(4)]
        for cp in cps:
            cp.start()
        for cp in cps:
            cp.wait()

    outs = _pcall(
        body, name=name,
        out_shape=[jax.ShapeDtypeStruct((4,) + a.shape[1:], a.dtype) for a in gs],
        in_specs=[ANY] * n, out_specs=[ANY] * n,
        scratch_shapes=[pltpu.SemaphoreType.DMA((n, 4)), pltpu.SemaphoreType.DMA((n, 4))],
    )(*gs)
    return list(outs)


def _pair_sum(g, p1, name):
    _, R, C = g.shape
    tr = 256 if R % 256 == 0 else R
    cidx = lax.axis_index("c").astype(jnp.int32).reshape(1)

    def body(c_ref, g_ref, p_ref, o_ref):
        o_ref[...] = (g_ref[...].astype(f32) + p_ref[...].astype(f32)).astype(o_ref.dtype)

    return _pcall(
        body, name=name,
        grid_spec=pltpu.PrefetchScalarGridSpec(
            num_scalar_prefetch=1, grid=(4, R // tr),
            in_specs=[pl.BlockSpec((1, tr, C), lambda i, r, c_ref: (2 * i + c_ref[0], r, 0)),
                      pl.BlockSpec((1, tr, C), lambda i, r, c_ref: (i, r, 0))],
            out_specs=pl.BlockSpec((1, tr, C), lambda i, r, c_ref: (i, r, 0))),
        out_shape=jax.ShapeDtypeStruct((4, R, C), g.dtype),
        compiler_params=_cparams("parallel", "parallel"),
    )(cidx, g, p1)


HBM = pl.BlockSpec(memory_space=pltpu.HBM)
SEM = pl.BlockSpec(memory_space=pltpu.SEMAPHORE)
EFFECT = pltpu.SideEffectType.DATAFLOW_SIDE_EFFECTING


def _peers(x, y, c):
    out = []
    for k in range(1, N_DEV):
        kx, ky, kc = (k >> 2) & 1, (k >> 1) & 1, k & 1
        out.append(((1 - x) if kx else x, (1 - y) if ky else y, (1 - c) if kc else c))
    return out


SPREAD_COPIES = {"gather": N_DEV - 1, "scatter": N_DEV - 1, "chips": 3}


def _spread_copy(src_ref, land_ref, send_sems, recv_sems, k, plan):
    x, y, c = _mesh_pos()
    if plan == "chips":
        px, py = [(1 - x, y), (x, 1 - y), (1 - x, 1 - y)][k]
        peer, src, slot = (px, py, c), src_ref.at[2 * px + py], 2 * x + y
    else:
        peer = _peers(x, y, c)[k]
        src, slot = (src_ref.at[_flat(*peer)] if plan == "scatter" else src_ref), _flat(x, y, c)
    return pltpu.make_async_remote_copy(
        src_ref=src, dst_ref=land_ref.at[slot], send_sem=send_sems.at[k], recv_sem=recv_sems.at[k],
        device_id=peer, device_id_type=MESH)


def _spread_start(src, after, plan, name):
    land_shape = (N_DEV,) + src.shape if plan == "gather" else src.shape
    n_copies = SPREAD_COPIES[plan]

    def body(src_ref, land_ref, after_ref, send_sems, recv_sems, src_thru, land_thru, token):
        for k in range(n_copies):
            _spread_copy(src_ref, land_ref, send_sems, recv_sems, k, plan).start()
        token[...] = jnp.zeros_like(token)

    return _pcall(
        body, name=name,
        out_shape=(pltpu.SemaphoreType.DMA((n_copies,)), pltpu.SemaphoreType.DMA((n_copies,)),
                   pltpu.HBM(src.shape, src.dtype), pltpu.HBM(land_shape, src.dtype), jax.ShapeDtypeStruct((8, LANE), f32)),
        in_specs=(HBM, HBM, ANY), out_specs=(SEM, SEM, HBM, HBM, pl.BlockSpec(memory_space=pltpu.VMEM)),
        input_output_aliases={0: 2, 1: 3},
        compiler_params=pltpu.CompilerParams(has_side_effects=EFFECT),
    )(pltpu.with_memory_space_constraint(src, pltpu.HBM),
      pltpu.with_memory_space_constraint(lax.empty(land_shape, src.dtype), pltpu.HBM), after)


def _spread_wait(started, after, plan, name):
    send_sems, recv_sems, src_thru, land_thru, _ = started

    def body(src_ref, land_ref, send_sems, recv_sems, after_ref, src_dead, got_ref):
        for k in range(SPREAD_COPIES[plan]):
            cp = _spread_copy(src_ref, land_ref, send_sems, recv_sems, k, plan)
            cp.wait_send()
            cp.wait_recv()

    return _pcall(
        body, name=name,
        out_shape=(pltpu.HBM(src_thru.shape, src_thru.dtype), pltpu.HBM(land_thru.shape, land_thru.dtype)),
        in_specs=(HBM, HBM, SEM, SEM, ANY), out_specs=(HBM, HBM), input_output_aliases={0: 0, 1: 1},
        compiler_params=pltpu.CompilerParams(has_side_effects=EFFECT),
    )(src_thru, land_thru, send_sems, recv_sems, after)[1]


def _own_slot(land, block, slot):
    zero = jnp.zeros((), jnp.int32)
    return lax.dynamic_update_slice(land, block[None], (slot.astype(jnp.int32),) + (zero,) * block.ndim)


PIECE_NAT = (0, 3072, 4096, 4112, 5136, 6160, 7184, PROJ_WIDTH)


COL_TILE = 256


def _cast_w_in(w3):
    n = w3.shape[0]

    def body(w_ref, o_ref):
        o_ref[...] = w_ref[:, 0, :].astype(bf16)

    return _pcall(
        body, name="cast_w_in", grid=(D_MODEL // COL_TILE,),
        in_specs=[pl.BlockSpec((n, 1, COL_TILE), lambda j: (0, 0, j))],
        out_specs=pl.BlockSpec((n, COL_TILE), lambda j: (0, j)),
        out_shape=jax.ShapeDtypeStruct((n, D_MODEL), bf16),
        compiler_params=_cparams("parallel"),
    )(w3)


def _relayout_w_in(win_g):
    def body(g_ref, o_ref):
        o_ref[NAT_SMALL_END:NAT_SMALL_END + PAD_COLS, :] = jnp.zeros((PAD_COLS, COL_TILE), o_ref.dtype)
        for d in range(N_DEV):
            n0, n1 = d * SHARD_W, (d + 1) * SHARD_W
            cut = min(max(NAT_SMALL_END - n0, 0), SHARD_W)
            if cut > 0:
                o_ref[n0:n0 + cut, :] = g_ref[d, 0:cut, :]
            if cut < SHARD_W:
                o_ref[n0 + cut + PAD_COLS:n1 + PAD_COLS, :] = g_ref[d, cut:SHARD_W, :]

    return _pcall(
        body, name="relayout_w_in", grid=(D_MODEL // COL_TILE,),
        in_specs=[pl.BlockSpec((N_DEV, SHARD_W, COL_TILE), lambda j: (0, 0, j))],
        out_specs=pl.BlockSpec((PROJ_PAD, COL_TILE), lambda j: (0, j)),
        out_shape=jax.ShapeDtypeStruct((PROJ_PAD, D_MODEL), win_g.dtype),
        compiler_params=_cparams("parallel"),
    )(win_g)


def _grad_blocks(g_parts):
    npc = len(g_parts)

    def body(*refs):
        p_refs, o_ref = refs[:npc], refs[npc]
        for d in range(N_DEV):
            n0, n1 = d * SHARD_W, (d + 1) * SHARD_W
            for i in range(npc):
                lo, hi = max(n0, PIECE_NAT[i]), min(n1, PIECE_NAT[i + 1])
                if lo < hi:
                    o_ref[d, lo - n0:hi - n0, :] = p_refs[i][lo - PIECE_NAT[i]:hi - PIECE_NAT[i], :]

    return _pcall(
        body, name="grad_blocks", grid=(D_MODEL // COL_TILE,),
        in_specs=[pl.BlockSpec((p.shape[0], COL_TILE), lambda j: (0, j)) for p in g_parts],
        out_specs=pl.BlockSpec((N_DEV, SHARD_W, COL_TILE), lambda j: (0, 0, j)),
        out_shape=jax.ShapeDtypeStruct((N_DEV, SHARD_W, D_MODEL), bf16),
        compiler_params=_cparams("parallel"),
    )(*g_parts)


def _in_proj(x, nw, wpad_t):
    L = x.shape[0]
    tn = 768
    nj = wpad_t.shape[0] // tn

    def body(x_ref, nw_ref, w_ref, proj_ref, h_ref):
        @pl.when(pl.program_id(0) == 0)
        def _():
            for r in range(0, L, 256):
                xs = x_ref[r:r + 256, :]
                ms = jnp.mean(xs * xs, axis=-1, keepdims=True)
                h_ref[r:r + 256, :] = ((xs * lax.rsqrt(ms + EPS)) * nw_ref[...]).astype(bf16)
        for r in range(0, L, 512):
            proj_ref[r:r + 512, :] = lax.dot_general(h_ref[r:r + 512, :], w_ref[...], (((1,), (1,)), ((), ())),
                                                     preferred_element_type=f32)

    return _pcall(
        body, name="in_proj", grid=(nj,),
        in_specs=[pl.BlockSpec((L, D_MODEL), lambda j: (0, 0)), pl.BlockSpec((1, D_MODEL), lambda j: (0, 0)),
                  pl.BlockSpec((tn, D_MODEL), lambda j: (j, 0))],
        out_specs=[pl.BlockSpec((L, tn), lambda j: (0, j)), pl.BlockSpec((L, D_MODEL), lambda j: (0, 0))],
        out_shape=[jax.ShapeDtypeStruct((L, wpad_t.shape[0]), f32), jax.ShapeDtypeStruct((L, D_MODEL), bf16)],
        compiler_params=_cparams("arbitrary"),
    )(x, nw, wpad_t)


HALVES = [slice(i * LANE, (i + 1) * LANE) for i in range(ELT_W // LANE)]
STEPS_PER_GROUP = GDN_WIDTH // ELT_W


def _conv4(x, cw_ref, ls):
    return (cw_ref[3:4, ls] * x + cw_ref[2:3, ls] * _shift_down(x, 1) + cw_ref[1:2, ls] * _shift_down(x, 2)
            + cw_ref[0:1, ls] * _shift_down(x, 3))


def _qkv_act(proj, cw):
    L = proj.shape[0]

    def body(x_ref, cw_ref, o_ref):
        j = pl.program_id(0)
        scale = jnp.where(j < STEPS_PER_GROUP, HEAD_DIM ** -0.5, 1.0).astype(f32)
        for ls in HALVES:
            c = _conv4(x_ref[:, ls], cw_ref, ls)
            a = c * _sigmoid(c)
            rn = lax.rsqrt(jnp.sum(a * a, axis=1, keepdims=True) + EPS)
            o_ref[:, ls] = jnp.where(j < 2 * STEPS_PER_GROUP, (a * rn) * scale, a)

    return _pcall(
        body, name="qkv_act", grid=(3 * STEPS_PER_GROUP,),
        in_specs=[pl.BlockSpec((L, ELT_W), lambda j: (0, j)), pl.BlockSpec((4, ELT_W), lambda j: (0, j))],
        out_specs=pl.BlockSpec((L, ELT_W), lambda j: (0, j)),
        out_shape=jax.ShapeDtypeStruct((L, 3 * GDN_WIDTH), f32),
        compiler_params=_cparams("parallel"),
    )(proj, cw)


def _scalars(proj, alog_p, dtb_p):
    L = proj.shape[0]
    nc = L // CHUNK

    def body(x_ref, al_ref, dt_ref, sc_ref, gr_ref):
        x = x_ref[...]
        lane = _lanes(x.shape)
        beta = _sigmoid(x)
        g = -jnp.exp(al_ref[...]) * _softplus(x + dt_ref[...])
        gc = jnp.where((lane >= HEADS) & (lane < 2 * HEADS), g, 0.0)
        rc = _rows(x.shape) & (CHUNK - 1)
        for s in (1, 2, 4, 8, 16, 32):
            gc = gc + jnp.where(rc >= s, pltpu.roll(gc, s, 0), 0.0)
        sc_ref[...] = jnp.where(lane < HEADS, beta, gc)
        sel = (_lanes((HEADS, LANE)) == _rows((HEADS, LANE)) + HEADS).astype(f32)
        for c in range(nc):
            gr_ref[c] = lax.dot_general(sel, sc_ref[c * CHUNK:(c + 1) * CHUNK, :], (((1,), (1,)), ((), ())),
                                        preferred_element_type=f32, precision=lax.Precision.HIGHEST)

    return _pcall(
        body, name="scalars", grid=(1,),
        in_specs=[pl.BlockSpec((L, LANE), lambda i: (0, OFF_BA // LANE)), pl.BlockSpec((1, LANE), lambda i: (0, 0)),
                  pl.BlockSpec((1, LANE), lambda i: (0, 0))],
        out_specs=[pl.BlockSpec((L, LANE), lambda i: (0, 0)), pl.BlockSpec((nc, HEADS, CHUNK), lambda i: (0, 0, 0))],
        out_shape=[jax.ShapeDtypeStruct((L, LANE), f32), jax.ShapeDtypeStruct((nc, HEADS, CHUNK), f32)],
        compiler_params=_cparams("arbitrary"),
    )(proj, alog_p, dtb_p)


def _head_scalars(sc, gr_ref, h, ci=0):
    lane = _lanes(sc.shape)
    beta = jnp.sum(jnp.where(lane == h, sc, 0.0), axis=1, keepdims=True)
    gcc = jnp.sum(jnp.where(lane == HEADS + h, sc, 0.0), axis=1, keepdims=True)
    gcr = gr_ref[ci, h:h + 1, :]
    gl = jnp.sum(jnp.where(_lanes(gcr.shape) == CHUNK - 1, gcr, 0.0), axis=1, keepdims=True)
    ii, jj = _rows((CHUNK, CHUNK)), _lanes((CHUNK, CHUNK))
    dmat = jnp.where(ii >= jj, jnp.exp(jnp.minimum(gcc - gcr, 0.0)), 0.0)
    dmat_t = jnp.where(jj >= ii, jnp.exp(jnp.minimum(gcr - gcc, 0.0)), 0.0)
    return beta, gcc, gl, dmat, dmat_t, ii, jj


def _gdn_fwd(qkv, sc, gr):
    L = qkv.shape[0]
    nc = L // CHUNK
    W = GDN_WIDTH
    cps = GDN_CPS if nc % GDN_CPS == 0 else 1
    rows_per_step = cps * CHUNK

    def body(qkv_ref, sc_ref, gr_ref, o_ref, u_ref, w_ref, vn_ref, t_ref, sp_ref, s_scr):
        @pl.when(pl.program_id(0) == 0)
        def _():
            s_scr[...] = jnp.zeros_like(s_scr)
        HS = range(cps * HEADS)
        hd = [i % HEADS for i in HS]
        rs = [slice((i // HEADS) * CHUNK, (i // HEADS + 1) * CHUNK) for i in HS]
        cs = [slice(hd[i] * HEAD_DIM, (hd[i] + 1) * HEAD_DIM) for i in HS]
        q = [qkv_ref[rs[i], hd[i] * HEAD_DIM:(hd[i] + 1) * HEAD_DIM] for i in HS]
        k = [qkv_ref[rs[i], W + hd[i] * HEAD_DIM:W + (hd[i] + 1) * HEAD_DIM] for i in HS]
        v = [qkv_ref[rs[i], 2 * W + hd[i] * HEAD_DIM:2 * W + (hd[i] + 1) * HEAD_DIM] for i in HS]
        hsc = [_head_scalars(sc_ref[rs[i], :], gr_ref, hd[i], i // HEADS) for i in HS]
        beta, gcc, gl, dmat = ([x[i] for x in hsc] for i in range(4))
        ii, jj = hsc[0][5], hsc[0][6]
        eg = [jnp.exp(gcc[h]) for h in HS]
        kb = [k[h] * beta[h] for h in HS]
        kk = [_mm_nt(kb[h], k[h]) for h in HS]
        qk = [_mm_nt(q[h], k[h]) for h in HS]
        n0 = [-jnp.where(ii > jj, kk[h] * dmat[h], 0.0) for h in HS]
        n1 = [_mm(n0[h], n0[h]) for h in HS]
        n2 = [_mm(n1[h], n1[h]) for h in HS]
        p01 = [n0[h] + n1[h] + _mm(n0[h], n1[h]) for h in HS]
        n3 = [_mm(n2[h], n2[h]) for h in HS]
        n4 = [_mm(n3[h], n3[h]) for h in HS]
        p23 = [n2[h] + n3[h] + _mm(n2[h], n3[h]) for h in HS]
        n5 = [_mm(n4[h], n4[h]) for h in HS]
        p03 = [p01[h] + p23[h] + _mm(p01[h], p23[h]) for h in HS]
        p45 = [n4[h] + n5[h] + _mm(n4[h], n5[h]) for h in HS]
        t = [p03[h] + p45[h] + _mm(p03[h], p45[h]) for h in HS]
        vb = [v[h] * beta[h] for h in HS]
        kbg = [kb[h] * eg[h] for h in HS]
        uw = [_mm(t[h], jnp.concatenate([vb[h], kbg[h]], axis=1)) for h in HS]
        u = [vb[h] + uw[h][:, :HEAD_DIM] for h in HS]
        w = [kbg[h] + uw[h][:, HEAD_DIM:] for h in HS]
        wq = [jnp.concatenate([w[h], q[h] * eg[h]], axis=0) for h in HS]
        p = [jnp.where(ii >= jj, qk[h] * dmat[h], 0.0) for h in HS]
        ks = [k[h] * jnp.exp(gl[h] - gcc[h]) for h in HS]
        s = [s_scr[h] for h in range(HEADS)]
        for ci in range(cps):
            IS = range(ci * HEADS, (ci + 1) * HEADS)
            ws = [_mm(wq[i], s[hd[i]]) for i in IS]
            vn = [u[i] - ws[hd[i]][:CHUNK] for i in IS]
            pv = [_mm(p[i], vn[hd[i]]) for i in IS]
            kv = [_mm_tn(ks[i], vn[hd[i]]) for i in IS]
            for i in IS:
                h = hd[i]
                sp_ref[ci, cs[i], :] = s[h]
                o_ref[rs[i], cs[i]] = ws[h][CHUNK:] + pv[h]
                vn_ref[rs[i], cs[i]] = vn[h]
            s = [jnp.exp(gl[i]) * s[hd[i]] + kv[hd[i]] for i in IS]
        for h in range(HEADS):
            s_scr[h] = s[h]
        for i in HS:
            u_ref[rs[i], cs[i]] = u[i]
            w_ref[rs[i], cs[i]] = w[i]
            t_ref[i // HEADS, hd[i]] = t[i]

    row = lambda c: (c, 0)
    act = jax.ShapeDtypeStruct((L, W), f32)
    return _pcall(
        body, name="gdn_fwd", grid=(nc // cps,),
        in_specs=[pl.BlockSpec((rows_per_step, 3 * W), row), pl.BlockSpec((rows_per_step, LANE), row),
                  pl.BlockSpec((cps, HEADS, CHUNK), lambda c: (c, 0, 0))],
        out_specs=[pl.BlockSpec((rows_per_step, W), row)] * 4 + [
            pl.BlockSpec((cps, HEADS, CHUNK, CHUNK), lambda c: (c, 0, 0, 0)),
            pl.BlockSpec((cps, W, HEAD_DIM), lambda c: (c, 0, 0))],
        out_shape=[act, act, act, act, jax.ShapeDtypeStruct((nc, HEADS, CHUNK, CHUNK), f32),
                   jax.ShapeDtypeStruct((nc, W, HEAD_DIM), f32)],
        scratch_shapes=[pltpu.VMEM((HEADS, HEAD_DIM, HEAD_DIM), f32)],
        compiler_params=_cparams("arbitrary"),
    )(qkv, sc, gr)


def _gdn_gate(o, proj, gnw):
    L = o.shape[0]

    def body(o_ref, z_ref, w_ref, m_ref):
        for ls in HALVES:
            ov, z = o_ref[:, ls], z_ref[:, ls]
            rms = lax.rsqrt(jnp.mean(ov * ov, axis=-1, keepdims=True) + EPS)
            m_ref[:, ls] = (((ov * rms) * w_ref[...]) * (z * _sigmoid(z))).astype(bf16)

    return _pcall(
        body, name="gdn_gate", grid=(GDN_WIDTH // ELT_W,),
        in_specs=[pl.BlockSpec((L, ELT_W), lambda j: (0, j)), pl.BlockSpec((L, ELT_W), lambda j: (0, OFF_ZG // ELT_W + j)),
                  pl.BlockSpec((1, LANE), lambda j: (0, 0))],
        out_specs=pl.BlockSpec((L, ELT_W), lambda j: (0, j)),
        out_shape=jax.ShapeDtypeStruct((L, GDN_WIDTH), bf16),
        compiler_params=_cparams("parallel"),
    )(o, proj, gnw)


def _conv3(u, cw_ref, ls):
    return cw_ref[2:3, ls] * u + cw_ref[1:2, ls] * _shift_down(u, 1) + cw_ref[0:1, ls] * _shift_down(u, 2)


def _conv_specs(L):
    blk = lambda off: pl.BlockSpec((L, ELT_W), lambda j, off=off: (0, off // ELT_W + j))
    return [blk(OFF_B), blk(OFF_C), blk(OFF_HC), blk(OFF_ZC),
            pl.BlockSpec((3, ELT_W), lambda j: (0, j)), pl.BlockSpec((1, ELT_W), lambda j: (0, j))]


def _conv_fwd(proj, cw, cb):
    L = proj.shape[0]

    def body(b_ref, c_ref, h_ref, z_ref, cw_ref, cb_ref, m_ref):
        for ls in HALVES:
            z = z_ref[:, ls]
            cv = _conv3(c_ref[:, ls] * h_ref[:, ls], cw_ref, ls) + cb_ref[:, ls]
            m_ref[:, ls] = ((b_ref[:, ls] * cv) * (z * _sigmoid(z))).astype(bf16)

    return _pcall(
        body, name="conv_fwd", grid=(CONV_WIDTH // ELT_W,),
        in_specs=_conv_specs(L), out_specs=pl.BlockSpec((L, ELT_W), lambda j: (0, j)),
        out_shape=jax.ShapeDtypeStruct((L, CONV_WIDTH), bf16),
        compiler_params=_cparams("parallel"),
    )(proj, proj, proj, proj, cw, cb)


def _out_proj_loss(x, mix_a, mix_b, wo, fw, tgt):
    L = x.shape[0]
    tm = min(256, L)

    def body(x_ref, ma_ref, mb_ref, wo_ref, fw_ref, t_ref, dy_ref, dyb_ref, dma_ref, dmb_ref, gfw_ref, loss_ref):
        @pl.when(pl.program_id(0) == 0)
        def _():
            gfw_ref[...] = jnp.zeros_like(gfw_ref)
            loss_ref[...] = jnp.zeros_like(loss_ref)
        y = x_ref[...] + jnp.dot(ma_ref[...], wo_ref[:GDN_WIDTH, :], preferred_element_type=f32) \
            + jnp.dot(mb_ref[...], wo_ref[GDN_WIDTH:, :], preferred_element_type=f32)
        r = lax.rsqrt(jnp.mean(y * y, axis=-1, keepdims=True) + EPS)
        yh = y * r
        fwv = fw_ref[...]
        diff = yh * fwv - t_ref[...]
        loss_ref[...] += jnp.sum(jnp.sum(diff * diff, axis=-1, keepdims=True), axis=0, keepdims=True) * (0.5 / D_MODEL)
        dout = diff * (1.0 / D_MODEL)
        gfw_ref[...] += jnp.sum(dout * yh, axis=0, keepdims=True)
        dyh = dout * fwv
        dy = r * (dyh - yh * jnp.mean(dyh * yh, axis=-1, keepdims=True))
        dy_ref[...] = dy
        dyb = dy.astype(bf16)
        dyb_ref[...] = dyb
        dma_ref[...] = lax.dot_general(dyb, wo_ref[:GDN_WIDTH, :], (((1,), (1,)), ((), ())), preferred_element_type=f32)
        dmb_ref[...] = lax.dot_general(dyb, wo_ref[GDN_WIDTH:, :], (((1,), (1,)), ((), ())), preferred_element_type=f32)

    row = lambda i: (i, 0)
    fix = lambda i: (0, 0)
    act = jax.ShapeDtypeStruct((L, D_MODEL), f32)
    return _pcall(
        body, name="out_proj_loss", grid=(L // tm,),
        in_specs=[pl.BlockSpec((tm, D_MODEL), row), pl.BlockSpec((tm, GDN_WIDTH), row), pl.BlockSpec((tm, CONV_WIDTH), row),
                  pl.BlockSpec((GDN_WIDTH + CONV_WIDTH, D_MODEL), fix), pl.BlockSpec((1, D_MODEL), fix),
                  pl.BlockSpec((tm, D_MODEL), row)],
        out_specs=[pl.BlockSpec((tm, D_MODEL), row), pl.BlockSpec((tm, D_MODEL), row), pl.BlockSpec((tm, GDN_WIDTH), row),
                   pl.BlockSpec((tm, CONV_WIDTH), row), pl.BlockSpec((1, D_MODEL), fix), pl.BlockSpec((1, LANE), fix)],
        out_shape=[act, jax.ShapeDtypeStruct((L, D_MODEL), bf16), act, act,
                   jax.ShapeDtypeStruct((1, D_MODEL), f32), jax.ShapeDtypeStruct((1, LANE), f32)],
        compiler_params=_cparams("arbitrary"),
    )(x, mix_a, mix_b, wo, fw, tgt)


def _tn_matmul(a, b, name):
    L, M = a.shape
    N = b.shape[1]
    tm = 512 if M % 512 == 0 else M

    def body(a_ref, b_ref, o_ref):
        o_ref[...] = lax.dot_general(a_ref[...], b_ref[...], (((0,), (0,)), ((), ())),
                                     preferred_element_type=f32).astype(o_ref.dtype)

    return _pcall(
        body, name=name, grid=(M // tm,),
        in_specs=[pl.BlockSpec((L, tm), lambda i: (0, i)), pl.BlockSpec((L, N), lambda i: (0, 0))],
        out_specs=pl.BlockSpec((tm, N), lambda i: (i, 0)),
        out_shape=jax.ShapeDtypeStruct((M, N), bf16),
        compiler_params=_cparams("parallel"),
    )(a, b)


def _gdn_gate_bwd(o, proj, gnw, dmix_a):
    L = o.shape[0]

    def body(o_ref, z_ref, w_ref, dm_ref, do_ref, dz_ref, gw_ref):
        @pl.when(pl.program_id(0) == 0)
        def _():
            gw_ref[...] = jnp.zeros_like(gw_ref)
        wv = w_ref[...]
        for ls in HALVES:
            ov, z, dm = o_ref[:, ls], z_ref[:, ls], dm_ref[:, ls]
            rms = lax.rsqrt(jnp.mean(ov * ov, axis=-1, keepdims=True) + EPS)
            xh = ov * rms
            sg = _sigmoid(z)
            d_on = dm * (z * sg)
            dz_ref[:, ls] = (dm * (xh * wv) * (sg * (1.0 + z * (1.0 - sg)))).astype(bf16)
            gw_ref[...] += jnp.sum(d_on * xh, axis=0, keepdims=True)
            dxh = d_on * wv
            do_ref[:, ls] = rms * (dxh - xh * jnp.mean(dxh * xh, axis=-1, keepdims=True))

    wide = pl.BlockSpec((L, ELT_W), lambda j: (0, j))
    return _pcall(
        body, name="gdn_gate_bwd", grid=(GDN_WIDTH // ELT_W,),
        in_specs=[wide, pl.BlockSpec((L, ELT_W), lambda j: (0, OFF_ZG // ELT_W + j)),
                  pl.BlockSpec((1, LANE), lambda j: (0, 0)), wide],
        out_specs=[wide, wide, pl.BlockSpec((1, LANE), lambda j: (0, 0))],
        out_shape=[jax.ShapeDtypeStruct((L, GDN_WIDTH), f32), jax.ShapeDtypeStruct((L, GDN_WIDTH), bf16),
                   jax.ShapeDtypeStruct((1, LANE), f32)],
        compiler_params=_cparams("arbitrary"),
    )(o, proj, gnw, dmix_a)


def _conv_bwd(proj, cw, cb, dmix_b):
    L = proj.shape[0]

    def body(b_ref, c_ref, h_ref, z_ref, cw_ref, cb_ref, dm_ref, db_ref, dc_ref, dh_ref, dz_ref, gcw_ref, gcb_ref):
        for ls in HALVES:
            bv, cv_, hv, z, dm = b_ref[:, ls], c_ref[:, ls], h_ref[:, ls], z_ref[:, ls], dm_ref[:, ls]
            u = cv_ * hv
            cv = _conv3(u, cw_ref, ls) + cb_ref[:, ls]
            sg = _sigmoid(z)
            sz = z * sg
            db_ref[:, ls] = (dm * cv * sz).astype(bf16)
            dz_ref[:, ls] = (dm * (bv * cv) * (sg * (1.0 + z * (1.0 - sg)))).astype(bf16)
            dcv = dm * bv * sz
            gcb_ref[:, ls] = jnp.sum(dcv, axis=0, keepdims=True)
            dcv1, dcv2 = _shift_up(dcv, 1), _shift_up(dcv, 2)
            gcw_ref[2:3, ls] = jnp.sum(dcv * u, axis=0, keepdims=True)
            gcw_ref[1:2, ls] = jnp.sum(dcv1 * u, axis=0, keepdims=True)
            gcw_ref[0:1, ls] = jnp.sum(dcv2 * u, axis=0, keepdims=True)
            du = cw_ref[2:3, ls] * dcv + cw_ref[1:2, ls] * dcv1 + cw_ref[0:1, ls] * dcv2
            dc_ref[:, ls] = (du * hv).astype(bf16)
            dh_ref[:, ls] = (du * cv_).astype(bf16)

    col = pl.BlockSpec((L, ELT_W), lambda j: (0, j))
    act = jax.ShapeDtypeStruct((L, CONV_WIDTH), bf16)
    return _pcall(
        body, name="conv_bwd", grid=(CONV_WIDTH // ELT_W,),
        in_specs=_conv_specs(L) + [col],
        out_specs=[col, col, col, col, pl.BlockSpec((3, ELT_W), lambda j: (0, j)), pl.BlockSpec((1, ELT_W), lambda j: (0, j))],
        out_shape=[act, act, act, act, jax.ShapeDtypeStruct((3, CONV_WIDTH), f32), jax.ShapeDtypeStruct((1, CONV_WIDTH), f32)],
        compiler_params=_cparams("parallel"),
    )(proj, proj, proj, proj, cw, cb, dmix_b)


def _gdn_bwd(qkv, sc, gr, u_all, w_all, vn_all, t_all, sp_all, do_all):
    L = qkv.shape[0]
    nc = L // CHUNK
    W = GDN_WIDTH
    cps = GDN_CPS_BWD if nc % GDN_CPS_BWD == 0 else 1
    rows_per_step = cps * CHUNK
    nsteps = nc // cps

    def body(qkv_ref, sc_ref, gr_ref, u_ref, w_ref, vn_ref, t_ref, sp_ref, do_ref, dqkv_ref, dsc_ref, dgr_ref, ds_scr):
        @pl.when(pl.program_id(0) == 0)
        def _():
            ds_scr[...] = jnp.zeros_like(ds_scr)
        HS = range(cps * HEADS)
        hd = [i % HEADS for i in HS]
        rs = [slice((i // HEADS) * CHUNK, (i // HEADS + 1) * CHUNK) for i in HS]
        cs = [slice(hd[i] * HEAD_DIM, (hd[i] + 1) * HEAD_DIM) for i in HS]
        q = [qkv_ref[rs[i], hd[i] * HEAD_DIM:(hd[i] + 1) * HEAD_DIM] for i in HS]
        k = [qkv_ref[rs[i], W + hd[i] * HEAD_DIM:W + (hd[i] + 1) * HEAD_DIM] for i in HS]
        v = [qkv_ref[rs[i], 2 * W + hd[i] * HEAD_DIM:2 * W + (hd[i] + 1) * HEAD_DIM] for i in HS]
        hsc = [_head_scalars(sc_ref[rs[i], :], gr_ref, hd[i], i // HEADS) for i in HS]
        beta, gcc, gl, dmat, dmat_t = ([x[i] for x in hsc] for i in range(5))
        ii, jj = hsc[0][5], hsc[0][6]
        eg = [jnp.exp(gcc[h]) for h in HS]
        ekl = [jnp.exp(gl[h] - gcc[h]) for h in HS]
        egl = [jnp.exp(gl[h]) for h in HS]
        kb = [k[h] * beta[h] for h in HS]
        ks = [k[h] * ekl[h] for h in HS]
        do = [do_ref[rs[h], cs[h]] for h in HS]
        vn = [vn_ref[rs[h], cs[h]] for h in HS]
        s = [sp_ref[h // HEADS, cs[h], :] for h in HS]
        w = [w_ref[rs[h], cs[h]] for h in HS]
        qd = [q[h] * eg[h] for h in HS]

        kq = [_mm_nt(k[h], q[h]) for h in HS]
        p_t = [jnp.where(jj >= ii, kq[h] * dmat_t[h], 0.0) for h in HS]
        ptd = [_mm(p_t[h], do[h]) for h in HS]
        qw =[jnp.concatenate([qd[h], -w[h]], axis=0) for h in HS]
        dsn, dvn, dodv = [None] * len(HS), [None] * len(HS), [None] * len(HS)
        ds_cur = [ds_scr[h] for h in range(HEADS)]
        for ci in reversed(range(cps)):
            IS = range(ci * HEADS, (ci + 1) * HEADS)
            ksd = [_mm(ks[i], ds_cur[hd[i]]) for i in IS]
            for i in IS:
                dsn[i] = ds_cur[hd[i]]
                dvn[i] = ptd[i] + ksd[hd[i]]
                dodv[i] = jnp.concatenate([do[i], dvn[i]], axis=0)
            dsq = [_mm_tn(qw[i], dodv[i]) for i in IS]
            ds_cur = [egl[i] * ds_cur[hd[i]] + dsq[hd[i]] for i in IS]
        for h in range(HEADS):
            ds_scr[h] = ds_cur[h]
        x1 = [_mm_nt(dodv[h], s[h]) for h in HS]
        dks = [_mm_nt(vn[h], dsn[h]) for h in HS]
        dov = [_mm_nt(do[h], vn[h]) for h in HS]
        vdo = [_mm_nt(vn[h], do[h]) for h in HS]
        kk = [_mm_nt(kb[h], k[h]) for h in HS]
        qk = [_mm_nt(q[h], k[h]) for h in HS]
        dgl = [egl[h] * jnp.sum(jnp.sum(s[h] * dsn[h], axis=1, keepdims=True), axis=0, keepdims=True) for h in HS]
        dqd = [x1[h][:CHUNK] for h in HS]
        duw = [jnp.concatenate([dvn[h], -x1[h][CHUNK:]], axis=1) for h in HS]
        tdu = [_mm_tn(t_ref[h // HEADS, hd[h]], duw[h]) for h in HS]
        dvk = [duw[h] + tdu[h] for h in HS]
        uw = [jnp.concatenate([u_ref[rs[h], cs[h]], w[h]], axis=1) for h in HS]
        da = [-jnp.where(ii > jj, _mm_nt(dvk[h], uw[h]), 0.0) for h in HS]
        da_t = [-jnp.where(jj > ii, _mm_nt(uw[h], dvk[h]), 0.0) for h in HS]
        dp = [jnp.where(ii >= jj, dov[h], 0.0) for h in HS]
        dp_t = [jnp.where(jj >= ii, vdo[h], 0.0) for h in HS]
        r1 = [_mm(jnp.concatenate([da[h] * dmat[h], dp[h] * dmat[h]], axis=0), k[h]) for h in HS]
        dk1 = [_mm(jnp.concatenate([da_t[h] * dmat_t[h], dp_t[h] * dmat_t[h]], axis=1),
                   jnp.concatenate([kb[h], q[h]], axis=0)) for h in HS]
        lane = _lanes((CHUNK, LANE))
        for ci in range(cps):
            dsc = jnp.zeros((CHUNK, LANE), f32)
            for i in range(ci * HEADS, (ci + 1) * HEADS):
                h = hd[i]
                a = jnp.where(ii > jj, kk[i] * dmat[i], 0.0)
                p = jnp.where(ii >= jj, qk[i] * dmat[i], 0.0)
                gmat = da[i] * a + dp[i] * p
                dvb, dkbg = dvk[i][:, :HEAD_DIM], dvk[i][:, HEAD_DIM:]
                kbg = kb[i] * eg[i]
                dkb = r1[i][:CHUNK] + dkbg * eg[i]
                dq = r1[i][CHUNK:] + dqd[i] * eg[i]
                dk = dk1[i] + dks[i] * ekl[i] + dkb * beta[i]
                dbeta = jnp.sum(dkb * k[i] + dvb * v[i], axis=1, keepdims=True)
                ksum = jnp.sum(dks[i] * ks[i], axis=1, keepdims=True)
                dgl_tot = dgl[i] + jnp.sum(ksum, axis=0, keepdims=True)
                dgc = (jnp.sum(gmat, axis=1, keepdims=True) + jnp.sum(dqd[i] * qd[i] + dkbg * kbg, axis=1, keepdims=True)
                       - ksum)
                dgc = dgc + jnp.where(_rows(dgc.shape) == CHUNK - 1, dgl_tot, 0.0)
                dqkv_ref[rs[i], h * HEAD_DIM:(h + 1) * HEAD_DIM] = dq
                dqkv_ref[rs[i], W + h * HEAD_DIM:W + (h + 1) * HEAD_DIM] = dk
                dqkv_ref[rs[i], 2 * W + h * HEAD_DIM:2 * W + (h + 1) * HEAD_DIM] = dvb * beta[i]
                dsc = jnp.where(lane == h, dbeta, jnp.where(lane == HEADS + h, dgc, dsc))
                dgr_ref[ci, h:h + 1, :] = jnp.sum(gmat, axis=0, keepdims=True)
            dsc_ref[ci * CHUNK:(ci + 1) * CHUNK, :] = dsc

    row = lambda c: (nsteps - 1 - c, 0)
    lead3 = lambda c: (nsteps - 1 - c, 0, 0)
    return _pcall(
        body, name="gdn_bwd", grid=(nsteps,),
        in_specs=[pl.BlockSpec((rows_per_step, 3 * W), row), pl.BlockSpec((rows_per_step, LANE), row),
                  pl.BlockSpec((cps, HEADS, CHUNK), lead3),
                  pl.BlockSpec((rows_per_step, W), row), pl.BlockSpec((rows_per_step, W), row),
                  pl.BlockSpec((rows_per_step, W), row),
                  pl.BlockSpec((cps, HEADS, CHUNK, CHUNK), lambda c: (nsteps - 1 - c, 0, 0, 0)),
                  pl.BlockSpec((cps, W, HEAD_DIM), lead3), pl.BlockSpec((rows_per_step, W), row)],
        out_specs=[pl.BlockSpec((rows_per_step, 3 * W), row), pl.BlockSpec((rows_per_step, LANE), row),
                   pl.BlockSpec((cps, HEADS, CHUNK), lead3)],
        out_shape=[jax.ShapeDtypeStruct((L, 3 * W), f32), jax.ShapeDtypeStruct((L, LANE), f32),
                   jax.ShapeDtypeStruct((nc, HEADS, CHUNK), f32)],
        scratch_shapes=[pltpu.VMEM((HEADS, HEAD_DIM, HEAD_DIM), f32)],
        compiler_params=_cparams("arbitrary"),
    )(qkv, sc, gr, u_all, w_all, vn_all, t_all, sp_all, do_all)


def _qkv_bwd(proj, cw, dn):
    L = proj.shape[0]

    def body(x_ref, cw_ref, dn_ref, dx_ref, gcw_ref):
        j = pl.program_id(0)
        scale = jnp.where(j < STEPS_PER_GROUP, HEAD_DIM ** -0.5, 1.0).astype(f32)
        for ls in HALVES:
            x, dn_v = x_ref[:, ls], dn_ref[:, ls]
            c = _conv4(x, cw_ref, ls)
            sg = _sigmoid(c)
            a = c * sg
            rn = lax.rsqrt(jnp.sum(a * a, axis=1, keepdims=True) + EPS)
            da_n = (scale * rn) * (dn_v - a * ((rn * rn) * jnp.sum(dn_v * a, axis=1, keepdims=True)))
            da = jnp.where(j < 2 * STEPS_PER_GROUP, da_n, dn_v)
            dc = da * (sg * (1.0 + c * (1.0 - sg)))
            dc1, dc2, dc3 = _shift_up(dc, 1), _shift_up(dc, 2), _shift_up(dc, 3)
            gcw_ref[3:4, ls] = jnp.sum(dc * x, axis=0, keepdims=True)
            gcw_ref[2:3, ls] = jnp.sum(dc1 * x, axis=0, keepdims=True)
            gcw_ref[1:2, ls] = jnp.sum(dc2 * x, axis=0, keepdims=True)
            gcw_ref[0:1, ls] = jnp.sum(dc3 * x, axis=0, keepdims=True)
            dx = cw_ref[3:4, ls] * dc + cw_ref[2:3, ls] * dc1 + cw_ref[1:2, ls] * dc2 + cw_ref[0:1, ls] * dc3
            dx_ref[:, ls] = dx.astype(bf16)

    col = pl.BlockSpec((L, ELT_W), lambda j: (0, j))
    wspec = pl.BlockSpec((4, ELT_W), lambda j: (0, j))
    return _pcall(
        body, name="qkv_bwd", grid=(3 * STEPS_PER_GROUP,),
        in_specs=[col, wspec, col], out_specs=[col, wspec],
        out_shape=[jax.ShapeDtypeStruct((L, 3 * GDN_WIDTH), bf16), jax.ShapeDtypeStruct((4, 3 * GDN_WIDTH), f32)],
        compiler_params=_cparams("parallel"),
    )(proj, cw, dn)


def _scalars_bwd(proj, alog_p, dtb_p, dsc, dgr_col):
    L = proj.shape[0]

    def body(x_ref, al_ref, dt_ref, dsc_ref, dgr_ref, dba_ref, gs_ref):
        x, dsc_v = x_ref[...], dsc_ref[...]
        lane = _lanes(x.shape)
        dec = (lane >= HEADS) & (lane < 2 * HEADS)
        dg = jnp.where(dec, dsc_v - dgr_ref[...], 0.0)
        rc = _rows(x.shape) & (CHUNK - 1)
        for s in (1, 2, 4, 8, 16, 32):
            dg = dg + jnp.where(rc + s < CHUNK, pltpu.roll(dg, L - s, 0), 0.0)
        xa = x + dt_ref[...]
        ea = jnp.exp(al_ref[...])
        g = -ea * _softplus(xa)
        da = dg * (-ea) * _sigmoid(xa)
        beta = _sigmoid(x)
        db = dsc_v * beta * (1.0 - beta)
        dba_ref[...] = jnp.where(lane < HEADS, db, jnp.where(dec, da, 0.0)).astype(bf16)
        g_al = jnp.sum(jnp.where(dec, dg * g, 0.0), axis=0, keepdims=True)
        g_dt = jnp.sum(jnp.where(dec, da, 0.0), axis=0, keepdims=True)
        row8 = _rows(gs_ref.shape)
        gs = jnp.where(row8 == 0, g_al, jnp.where(row8 == 1, g_dt, 0.0))
        gs_ref[...] = pltpu.roll(gs, LANE - HEADS, 1)

    full = pl.BlockSpec((L, LANE), lambda i: (0, 0))
    vec = pl.BlockSpec((1, LANE), lambda i: (0, 0))
    return _pcall(
        body, name="scalars_bwd", grid=(1,),
        in_specs=[pl.BlockSpec((L, LANE), lambda i: (0, OFF_BA // LANE)), vec, vec, full, full],
        out_specs=[full, pl.BlockSpec((8, LANE), lambda i: (0, 0))],
        out_shape=[jax.ShapeDtypeStruct((L, LANE), bf16), jax.ShapeDtypeStruct((8, LANE), f32)],
        compiler_params=_cparams("arbitrary"),
    )(proj, alog_p, dtb_p, dsc, dgr_col)


def _input_grad(pieces, offs, wpad, x, nw, dy):
    L = x.shape[0]
    tm = min(512, L)
    npc = len(pieces)

    def body(*refs):
        p_refs = refs[:npc]
        w_hbm, x_ref, nw_ref, dy_ref, gx_ref, gnw_ref, w_vmem, sem = refs[npc:]

        @pl.when(pl.program_id(0) == 0)
        def _():
            cp = pltpu.make_async_copy(w_hbm, w_vmem, sem)
            cp.start()
            cp.wait()
            gnw_ref[...] = jnp.zeros_like(gnw_ref)
        dh = None
        for p_ref, off in zip(p_refs, offs):
            wd = p_ref.shape[1]
            part = jnp.dot(p_ref[...], w_vmem[off:off + wd, :], preferred_element_type=f32)
            dh = part if dh is None else dh + part
        xv, nwv = x_ref[...], nw_ref[...]
        r = lax.rsqrt(jnp.mean(xv * xv, axis=-1, keepdims=True) + EPS)
        xh = xv * r
        gnw_ref[...] += jnp.sum(dh * xh, axis=0, keepdims=True)
        dxh = dh * nwv
        gx_ref[...] = dy_ref[...] + r * (dxh - xh * jnp.mean(dxh * xh, axis=-1, keepdims=True))

    row = lambda i: (i, 0)
    fix = lambda i: (0, 0)
    return _pcall(
        body, name="input_grad", grid=(L // tm,),
        in_specs=[pl.BlockSpec((tm, p.shape[1]), row) for p in pieces] + [
            ANY, pl.BlockSpec((tm, D_MODEL), row), pl.BlockSpec((1, D_MODEL), fix), pl.BlockSpec((tm, D_MODEL), row)],
        out_specs=[pl.BlockSpec((tm, D_MODEL), row), pl.BlockSpec((1, D_MODEL), fix)],
        out_shape=[jax.ShapeDtypeStruct((L, D_MODEL), f32), jax.ShapeDtypeStruct((1, D_MODEL), f32)],
        scratch_shapes=[pltpu.VMEM(wpad.shape, bf16), pltpu.SemaphoreType.DMA(())],
        compiler_params=_cparams("arbitrary"),
    )(*pieces, wpad, x, nw, dy)


def _adamw_reduce(parts, w, m, v, name):
    R, C = w.shape
    n_parts = parts.shape[0]
    tr = 128 if R % 128 == 0 else R
    c1 = 1.0 - ADAM_B1 ** ADAM_STEP
    c2 = 1.0 - ADAM_B2 ** ADAM_STEP

    def body(p_ref, w_ref, m_ref, v_ref, g_ref, d_ref, nm_ref, nv_ref):
        g = p_ref[0].astype(f32)
        for s in range(1, n_parts):
            g = g + p_ref[s].astype(f32)
        nm = ADAM_B1 * m_ref[...] + (1.0 - ADAM_B1) * g
        nv = ADAM_B2 * v_ref[...] + (1.0 - ADAM_B2) * (g * g)
        g_ref[...] = g
        nm_ref[...] = nm
        nv_ref[...] = nv
        d_ref[...] = -ADAM_LR * ((nm / c1) / (jnp.sqrt(nv / c2) + ADAM_EPS) + ADAM_WD * w_ref[...])

    blk = pl.BlockSpec((tr, C), lambda i: (i, 0))
    out = jax.ShapeDtypeStruct((R, C), f32)
    return _pcall(
        body, name=name, grid=(R // tr,),
        in_specs=[pl.BlockSpec((n_parts, tr, C), lambda i: (0, i, 0)), blk, blk, blk],
        out_specs=[blk] * 4, out_shape=[out] * 4,
        compiler_params=_cparams("parallel"),
    )(parts, w, m, v)


SMALL_SLOTS = ((0, D_MODEL), (D_MODEL, D_MODEL), (2 * D_MODEL, D_MODEL), (3 * D_MODEL, LANE),
               (3 * D_MODEL + LANE, HEADS), (3 * D_MODEL + 2 * LANE, HEADS))
SMALL_LOSS = 3 * D_MODEL + 3 * LANE
SMALL_W = SMALL_LOSS + LANE


def _pack_small(gs, after):
    def body(nw_ref, cb_ref, fw_ref, gn_ref, sc_ref, ls_ref, after_ref, o_ref):
        for ref, (start, width) in zip((nw_ref, cb_ref, fw_ref, gn_ref), SMALL_SLOTS[:4]):
            o_ref[:, start:start + width] = ref[...]
        o_ref[:, SMALL_SLOTS[4][0]:SMALL_SLOTS[4][0] + LANE] = sc_ref[0:1, :]
        o_ref[:, SMALL_SLOTS[5][0]:SMALL_SLOTS[5][0] + LANE] = sc_ref[1:2, :]
        o_ref[:, SMALL_LOSS:SMALL_W] = ls_ref[...]

    vm = pl.BlockSpec(memory_space=pltpu.VMEM)
    return _pcall(body, name="pack_small_grads", out_shape=jax.ShapeDtypeStruct((1, SMALL_W), f32),
                  in_specs=[vm] * 6 + [ANY], out_specs=vm)(*gs, after)


def _adamw_small(parts, ws, ms, vs):
    c1 = 1.0 - ADAM_B1 ** ADAM_STEP
    c2 = 1.0 - ADAM_B2 ** ADAM_STEP
    np_ = len(ws)

    def body(*refs):
        p_ref = refs[0]
        w_refs, m_refs, v_refs = refs[1:1 + np_], refs[1 + np_:1 + 2 * np_], refs[1 + 2 * np_:1 + 3 * np_]
        outs = refs[1 + 3 * np_:]
        g_refs, d_refs, nm_refs, nv_refs = (outs[i * np_:(i + 1) * np_] for i in range(4))
        loss_ref = outs[4 * np_]

        def total(start, width):
            t = p_ref[0, :, start:start + width]
            for s in range(1, N_DEV):
                t = t + p_ref[s, :, start:start + width]
            return t

        for i, (start, width) in enumerate(SMALL_SLOTS):
            g = total(start, width)
            nm = ADAM_B1 * m_refs[i][...] + (1.0 - ADAM_B1) * g
            nv = ADAM_B2 * v_refs[i][...] + (1.0 - ADAM_B2) * (g * g)
            g_refs[i][...] = g
            nm_refs[i][...] = nm
            nv_refs[i][...] = nv
            d_refs[i][...] = -ADAM_LR * ((nm / c1) / (jnp.sqrt(nv / c2) + ADAM_EPS) + ADAM_WD * w_refs[i][...])
        loss_ref[...] = total(SMALL_LOSS, LANE)

    vm = pl.BlockSpec(memory_space=pltpu.VMEM)
    shapes = [jax.ShapeDtypeStruct(w.shape, f32) for w in ws]
    res = _pcall(body, name="adamw_small", out_shape=shapes * 4 + [jax.ShapeDtypeStruct((1, LANE), f32)],
                 in_specs=[vm] * (1 + 3 * np_), out_specs=[vm] * (4 * np_ + 1))(parts, *ws, *ms, *vs)
    return [res[i * np_:(i + 1) * np_] for i in range(4)], res[4 * np_]


def _adamw_w_in(parts, w3, m3, v3):
    n_parts, n, _ = parts.shape
    c1 = 1.0 - ADAM_B1 ** ADAM_STEP
    c2 = 1.0 - ADAM_B2 ** ADAM_STEP

    def body(p_ref, w_ref, m_ref, v_ref, g_ref, d_ref, nm_ref, nv_ref):
        g = p_ref[0].astype(f32)
        for s in range(1, n_parts):
            g = g + p_ref[s].astype(f32)
        nm = ADAM_B1 * m_ref[:, 0, :] + (1.0 - ADAM_B1) * g
        nv = ADAM_B2 * v_ref[:, 0, :] + (1.0 - ADAM_B2) * (g * g)
        g_ref[:, 0, :] = g
        nm_ref[:, 0, :] = nm
        nv_ref[:, 0, :] = nv
        d_ref[:, 0, :] = -ADAM_LR * ((nm / c1) / (jnp.sqrt(nv / c2) + ADAM_EPS) + ADAM_WD * w_ref[:, 0, :])

    tile = 2 * COL_TILE
    blk = pl.BlockSpec((n, 1, tile), lambda j: (0, 0, j))
    out = jax.ShapeDtypeStruct((n, 1, D_MODEL), f32)
    return _pcall(
        body, name="adamw_w_in", grid=(D_MODEL // tile,),
        in_specs=[pl.BlockSpec((n_parts, n, tile), lambda j: (0, 0, j)), blk, blk, blk],
        out_specs=[blk] * 4, out_shape=[out] * 4,
        compiler_params=_cparams("parallel"),
    )(parts, w3, m3, v3)


def _pad_lanes(vec8, start):
    return jnp.pad(vec8.reshape(1, -1), ((0, 0), (start, LANE - start - vec8.size)))


def kernel(x, norm_in_w, w_in, conv_qkv_w, A_log, dt_bias, gdn_norm_w, conv_w, conv_b, w_out, final_norm_w, loss_target, m_norm_in_w, m_w_in, m_conv_qkv_w, m_A_log, m_dt_bias, m_gdn_norm_w, m_conv_w, m_conv_b, m_w_out, m_final_norm_w, v_norm_in_w, v_w_in, v_conv_qkv_w, v_A_log, v_dt_bias, v_gdn_norm_w, v_conv_w, v_conv_b, v_w_out, v_final_norm_w):
    L = x.shape[1]
    nc = L // CHUNK
    xs = x[0]
    tgt = loss_target[0]
    fnw = final_norm_w.reshape(1, D_MODEL)

    as_rows = lambda a: jnp.transpose(a, (2, 0, 1))
    win_g, cqkv_g, cw_g = _all_gather([_cast_w_in(as_rows(w_in)), conv_qkv_w[0], conv_w[0]], "gather_weights")
    wpad = _relayout_w_in(win_g)
    cqkv = jnp.concatenate([cqkv_g[d] for d in range(N_DEV)], axis=1)
    cw = jnp.concatenate([cw_g[d] for d in range(N_DEV)], axis=1)
    alog_p = _pad_lanes(A_log, HEADS)
    dtb_p = _pad_lanes(dt_bias, HEADS)
    me_flat, me_chip = _flat(*_mesh_pos()), 2 * lax.axis_index("x") + lax.axis_index("y")
    tok = lambda started: started[4][0:1, 0:1]
    wo_own = w_out[0].astype(bf16)
    wo_started = _spread_start(wo_own, wpad, "gather", "gather_w_out_start")

    proj, h = _in_proj(xs, norm_in_w + tok(wo_started), wpad)
    qkv = _qkv_act(proj, cqkv)
    sc, gr = _scalars(proj, alog_p, dtb_p)
    o, u_all, w_all, vn_all, t_all, sp_all = _gdn_fwd(qkv, sc, gr)
    mix_a = _gdn_gate(o, proj, gdn_norm_w)
    mix_b = _conv_fwd(proj, cw, conv_b)
    wo = _own_slot(_spread_wait(wo_started, mix_b, "gather", "gather_w_out_wait"), wo_own, me_flat).reshape(-1, D_MODEL)
    dy, dyb, dmix_a, dmix_b, g_fnw, loss_v = _out_proj_loss(xs, mix_a, mix_b, wo, fnw, tgt)

    g_wout = jnp.concatenate([_tn_matmul(mix_a, dyb, "grad_w_out_a"), _tn_matmul(mix_b, dyb, "grad_w_out_b")], axis=0)
    g_wout = g_wout.reshape(N_DEV, -1, D_MODEL)
    g_wout_own = lax.dynamic_index_in_dim(g_wout, me_flat, 0, keepdims=False)
    gwo_started = _spread_start(g_wout, dyb, "scatter", "exchange_grad_w_out_start")
    do, dzg, g_gnw = _gdn_gate_bwd(o, proj, gdn_norm_w + tok(gwo_started), dmix_a)
    d_b, d_c, d_hc, d_zc, g_cw, g_cb = _conv_bwd(proj, cw, conv_b, dmix_b)
    dqkv_n, dsc, dgr = _gdn_bwd(qkv, sc, gr, u_all, w_all, vn_all, t_all, sp_all, do)
    dqkv, g_cqkv = _qkv_bwd(proj, cqkv, dqkv_n)
    dgr_col = jnp.pad(dgr.transpose(0, 2, 1).reshape(L, HEADS), ((0, 0), (HEADS, LANE - 2 * HEADS)))
    dba, g_sc = _scalars_bwd(proj, alog_p, dtb_p, dsc, dgr_col)
    pieces = [dqkv, dzg, dba, d_b, d_c, d_hc, d_zc]
    offs = [OFF_QKV, OFF_ZG, OFF_BA, OFF_B, OFF_C, OFF_HC, OFF_ZC]
    g_parts = [_tn_matmul(p, h, "grad_w_in_%d" % i) for i, p in enumerate(pieces)]
    g_win_blk = _grad_blocks(g_parts)

    (p_win,) = _pair_exchange([g_win_blk], "exchange_grads_pair")
    s_win = _pair_sum(g_win_blk, p_win, "pair_sum_w_in")
    s_win_own = lax.dynamic_index_in_dim(s_win, me_chip, 0, keepdims=False)
    r_cqkv, r_cw = _all_to_all(
        [g_cqkv.reshape(4, N_DEV, -1).transpose(1, 0, 2), g_cw.reshape(3, N_DEV, -1).transpose(1, 0, 2)],
        "exchange_small_sharded_grads")
    gwi_started = _spread_start(s_win, r_cw, "chips", "exchange_grads_chips_start")
    grad_x, g_nw = _input_grad(pieces, offs, wpad, xs, norm_in_w + tok(gwi_started), dy)

    r_wout = _own_slot(_spread_wait(gwo_started, grad_x, "scatter", "exchange_grad_w_out_wait"), g_wout_own, me_flat)
    upd_wout =_adamw_reduce(r_wout, w_out[0], m_w_out[0], v_w_out[0], "adamw_w_out")
    upd_cqkv = _adamw_reduce(r_cqkv, conv_qkv_w[0], m_conv_qkv_w[0], v_conv_qkv_w[0], "adamw_conv_qkv_w")
    upd_cw = _adamw_reduce(r_cw, conv_w[0], m_conv_w[0], v_conv_w[0], "adamw_conv_w")

    r_win = _own_slot(_spread_wait(gwi_started, upd_cw[0], "chips", "exchange_grads_chips_wait"), s_win_own, me_chip)
    upd_win = [jnp.transpose(a, (1, 2, 0)) for a in _adamw_w_in(r_win, as_rows(w_in), as_rows(m_w_in), as_rows(v_w_in))]

    small_g = _pack_small([g_nw, g_cb, g_fnw, g_gnw, g_sc, loss_v], r_win)
    (small_all,) = _all_gather([small_g], "gather_small_grads")
    fvec = lambda a: a.reshape(1, D_MODEL)
    upd_small, loss_sum = _adamw_small(
        small_all,
        [norm_in_w, conv_b, fvec(final_norm_w), gdn_norm_w, A_log, dt_bias],
        [m_norm_in_w, m_conv_b, fvec(m_final_norm_w), m_gdn_norm_w, m_A_log, m_dt_bias],
        [v_norm_in_w, v_conv_b, fvec(v_final_norm_w), v_gdn_norm_w, v_A_log, v_dt_bias])

    outs = [loss_sum[0, 0], grad_x[None]]
    for k in range(4):
        nw_k, cb_k, fw_k, gn_k, al_k, dt_k = upd_small[k]
        outs += [nw_k, upd_win[k], upd_cqkv[k][None], al_k, dt_k, gn_k,
                 upd_cw[k][None], cb_k, upd_wout[k][None], fw_k.reshape(D_MODEL)]
    return tuple(outs)
```

```python
import functools
import math

import jax
import jax.numpy as jnp
from jax import lax
from jax.experimental import pallas as pl
from jax.experimental.pallas import tpu as pltpu

f32 = jnp.float32
bf16 = jnp.bfloat16

N_DEV = 8
D_MODEL = 1024
HEADS = 8
HEAD_DIM = 128
CHUNK = 64
GDN_CPS = 4
GDN_CPS_BWD = 1
GDN_WIDTH = HEADS * HEAD_DIM
CONV_WIDTH = 1024
PROJ_WIDTH = 8208
SHARD_W = PROJ_WIDTH // N_DEV
EPS = 1e-6

NAT_SMALL_END = 4112
PAD_COLS = 240
OFF_QKV, OFF_ZG, OFF_BA, OFF_B, OFF_C, OFF_HC, OFF_ZC = 0, 3072, 4096, 4352, 5376, 6400, 7424
PROJ_PAD = 8448
LANE = 128
ELT_W = 256

ADAM_LR, ADAM_B1, ADAM_B2, ADAM_EPS, ADAM_WD, ADAM_STEP = 0.001, 0.9, 0.999, 1e-08, 0.01, 10

VMEM_LIMIT = 56 * 1024 * 1024

MESH = pl.DeviceIdType.MESH
ANY = pl.BlockSpec(memory_space=pl.ANY)


def _pcall(body, **kw):
    return pl.pallas_call(body, **kw)


def _cparams(*sem):
    return pltpu.CompilerParams(dimension_semantics=sem if sem else None, vmem_limit_bytes=VMEM_LIMIT)


def _mm(a, b):
    return jnp.dot(a.astype(bf16), b.astype(bf16), preferred_element_type=f32)


def _mm_nt(a, b):
    return lax.dot_general(a.astype(bf16), b.astype(bf16), (((1,), (1,)), ((), ())), preferred_element_type=f32)


def _mm_tn(a, b):
    return lax.dot_general(a.astype(bf16), b.astype(bf16), (((0,), (0,)), ((), ())), preferred_element_type=f32)


def _rows(shape):
    return lax.broadcasted_iota(jnp.int32, shape, 0)


def _lanes(shape):
    return lax.broadcasted_iota(jnp.int32, shape, 1)


def _shift_down(x, s):
    if s == 0:
        return x
    return jnp.where(_rows(x.shape) >= s, pltpu.roll(x, s, 0), 0.0)


def _shift_up(x, s):
    if s == 0:
        return x
    n = x.shape[0]
    return jnp.where(_rows(x.shape) < n - s, pltpu.roll(x, n - s, 0), 0.0)


def _sigmoid(x):
    return jax.nn.sigmoid(x)


def _softplus(x):
    e = jnp.exp(-jnp.abs(x))
    small = e * (1.0 - e * (0.5 - e * (1.0 / 3.0)))
    return jnp.maximum(x, 0.0) + jnp.where(e < 0.01, small, jnp.log(1.0 + e))


def _mesh_pos():
    return lax.axis_index("x"), lax.axis_index("y"), lax.axis_index("c")


def _flat(px, py, pc):
    return 4 * px + 2 * py + pc


def _all_gather(xs, name):
    n = len(xs)

    def body(*refs):
        x_refs, o_refs = refs[:n], refs[n:2 * n]
        send_sems, recv_sems, local_sems = refs[2 * n:]
        x, y, c = _mesh_pos()
        me, sibling = (x, y, c), (x, y, 1 - c)
        flip = lambda v, bit: v + bit - 2 * v * bit
        nbr_a = (flip(x, 1 - c), flip(y, c))
        nbr_b = (flip(x, c), flip(y, 1 - c))
        diag = (1 - x, 1 - y)

        def copy(a, k, block, to, src=None):
            dst = o_refs[a].at[_flat(*block)]
            return pltpu.make_async_remote_copy(
                src_ref=dst if src is None else src, dst_ref=dst,
                send_sem=send_sems.at[a, k], recv_sem=recv_sems.at[a, k], device_id=to, device_id_type=MESH)

        mine, sent = [], []

        def go(cp):
            cp.start()
            sent.append(cp)

        for a in range(n):
            cp = pltpu.make_async_copy(x_refs[a], o_refs[a].at[_flat(*me)], local_sems.at[a])
            cp.start()
            mine.append(cp)
            go(copy(a, 0, me, sibling, src=x_refs[a]))
            go(copy(a, 1, me, (*nbr_a, c), src=x_refs[a]))
            go(copy(a, 2, me, (*nbr_b, c), src=x_refs[a]))
        for a in range(n):
            copy(a, 1, (*nbr_a, c), me).wait_recv()
            go(copy(a, 3, (*nbr_a, c), (*nbr_b, c)))
            go(copy(a, 4, (*nbr_a, c), sibling))
        for a in range(n):
            copy(a, 2, (*nbr_b, c), me).wait_recv()
            go(copy(a, 5, (*nbr_b, c), sibling))
        for a in range(n):
            copy(a, 3, (*diag, c), me).wait_recv()
            go(copy(a, 6, (*diag, c), sibling))
        for a in range(n):
            copy(a, 0, sibling, me).wait_recv()
            copy(a, 4, (*nbr_b, 1 - c), me).wait_recv()
            copy(a, 5, (*nbr_a, 1 - c), me).wait_recv()
            copy(a, 6, (*diag, 1 - c), me).wait_recv()
        for cp in sent:
            cp.wait_send()
        for cp in mine:
            cp.wait()

    outs = _pcall(
        body, name=name,
        out_shape=[jax.ShapeDtypeStruct((N_DEV,) + a.shape, a.dtype) for a in xs],
        in_specs=[ANY] * n, out_specs=[ANY] * n,
        scratch_shapes=[pltpu.SemaphoreType.DMA((n, 7)), pltpu.SemaphoreType.DMA((n, 7)), pltpu.SemaphoreType.DMA((n,))],
    )(*xs)
    return list(outs)


def _all_to_all(gs, name):
    n = len(gs)

    def body(*refs):
        g_refs, o_refs = refs[:n], refs[n:2 * n]
        send_sems, recv_sems, local_sems = refs[2 * n:]
        x, y, c = _mesh_pos()
        me = _flat(x, y, c)
        peers = []
        for k in range(1, N_DEV):
            kx, ky, kc = (k >> 2) & 1, (k >> 1) & 1, k & 1
            px = (1 - x) if kx else x
            py = (1 - y) if ky else y
            pc = (1 - c) if kc else c
            peers.append((px, py, pc))

        def copy(a, k):
            peer = peers[k - 1]
            return pltpu.make_async_remote_copy(
                src_ref=g_refs[a].at[_flat(*peer)], dst_ref=o_refs[a].at[me],
                send_sem=send_sems.at[a, k - 1], recv_sem=recv_sems.at[a, k - 1], device_id=peer, device_id_type=MESH)

        def arrival(a, k):
            peer = peers[k - 1]
            return pltpu.make_async_remote_copy(
                src_ref=g_refs[a].at[me], dst_ref=o_refs[a].at[_flat(*peer)],
                send_sem=send_sems.at[a, k - 1], recv_sem=recv_sems.at[a, k - 1], device_id=peer, device_id_type=MESH)

        mine, sent = [], []
        for a in range(n):
            cp = pltpu.make_async_copy(g_refs[a].at[me], o_refs[a].at[me], local_sems.at[a])
            cp.start()
            mine.append(cp)
            for k in range(1, N_DEV):
                cp = copy(a, k)
                cp.start()
                sent.append(cp)
        for a in range(n):
            for k in range(1, N_DEV):
                arrival(a, k).wait_recv()
        for cp in sent:
            cp.wait_send()
        for cp in mine:
            cp.wait()

    outs = _pcall(
        body, name=name,
        out_shape=[jax.ShapeDtypeStruct(a.shape, a.dtype) for a in gs],
        in_specs=[ANY] * n, out_specs=[ANY] * n,
        scratch_shapes=[pltpu.SemaphoreType.DMA((n, 7)), pltpu.SemaphoreType.DMA((n, 7)), pltpu.SemaphoreType.DMA((n,))],
    )(*gs)
    return list(outs)


def _pair_exchange(gs, name):
    n = len(gs)
    chips = [(0, 0), (0, 1), (1, 0), (1, 1)]

    def body(*refs):
        g_refs, o_refs = refs[:n], refs[n:2 * n]
        send_sems, recv_sems = refs[2 * n:]
        x, y, c = _mesh_pos()
        sibling = (x, y, 1 - c)

        def copy(a, i):
            xp, yp = chips[i]
            return pltpu.make_async_remote_copy(
                src_ref=g_refs[a].at[_flat(xp, yp, 1 - c)], dst_ref=o_refs[a].at[i],
                send_sem=send_sems.at[a, i], recv_sem=recv_sems.at[a, i], device_id=sibling, device_id_type=MESH)

        cps = [copy(a, i) for a in range(n) for i in range(4)]
        for cp in cps:
            cp.start()
        for cp in cps:
            cp.wait()

    outs = _pcall(
        body, name=name,
        out_shape=[jax.ShapeDtypeStruct((4,) + a.shape[1:], a.dtype) for a in gs],
        in_specs=[ANY] * n, out_specs=[ANY] * n,
        scratch_shapes=[pltpu.SemaphoreType.DMA((n, 4)), pltpu.SemaphoreType.DMA((n, 4))],
    )(*gs)
    return list(outs)


def _pair_sum(g, p1, name):
    _, R, C = g.shape
    tr = 256 if R % 256 == 0 else R
    cidx = lax.axis_index("c").astype(jnp.int32).reshape(1)

    def body(c_ref, g_ref, p_ref, o_ref):
        o_ref[...] = (g_ref[...].astype(f32) + p_ref[...].astype(f32)).astype(o_ref.dtype)

    return _pcall(
        body, name=name,
        grid_spec=pltpu.PrefetchScalarGridSpec(
            num_scalar_prefetch=1, grid=(4, R // tr),
            in_specs=[pl.BlockSpec((1, tr, C), lambda i, r, c_ref: (2 * i + c_ref[0], r, 0)),
                      pl.BlockSpec((1, tr, C), lambda i, r, c_ref: (i, r, 0))],
            out_specs=pl.BlockSpec((1, tr, C), lambda i, r, c_ref: (i, r, 0))),
        out_shape=jax.ShapeDtypeStruct((4, R, C), g.dtype),
        compiler_params=_cparams("parallel", "parallel"),
    )(cidx, g, p1)


HBM = pl.BlockSpec(memory_space=pltpu.HBM)
SEM = pl.BlockSpec(memory_space=pltpu.SEMAPHORE)
EFFECT = pltpu.SideEffectType.DATAFLOW_SIDE_EFFECTING


def _peers(x, y, c):
    out = []
    for k in range(1, N_DEV):
        kx, ky, kc = (k >> 2) & 1, (k >> 1) & 1, k & 1
        out.append(((1 - x) if kx else x, (1 - y) if ky else y, (1 - c) if kc else c))
    return out


SPREAD_COPIES = {"gather": N_DEV - 1, "scatter": N_DEV - 1, "chips": 3}


def _spread_copy(src_ref, land_ref, send_sems, recv_sems, k, plan):
    x, y, c = _mesh_pos()
    if plan == "chips":
        px, py = [(1 - x, y), (x, 1 - y), (1 - x, 1 - y)][k]
        peer, src, slot = (px, py, c), src_ref.at[2 * px + py], 2 * x + y
    else:
        peer = _peers(x, y, c)[k]
        src, slot = (src_ref.at[_flat(*peer)] if plan == "scatter" else src_ref), _flat(x, y, c)
    return pltpu.make_async_remote_copy(
        src_ref=src, dst_ref=land_ref.at[slot], send_sem=send_sems.at[k], recv_sem=recv_sems.at[k],
        device_id=peer, device_id_type=MESH)


def _spread_start(src, after, plan, name):
    land_shape = (N_DEV,) + src.shape if plan == "gather" else src.shape
    n_copies = SPREAD_COPIES[plan]

    def body(src_ref, land_ref, after_ref, send_sems, recv_sems, src_thru, land_thru, token):
        for k in range(n_copies):
            _spread_copy(src_ref, land_ref, send_sems, recv_sems, k, plan).start()
        token[...] = jnp.zeros_like(token)

    return _pcall(
        body, name=name,
        out_shape=(pltpu.SemaphoreType.DMA((n_copies,)), pltpu.SemaphoreType.DMA((n_copies,)),
                   pltpu.HBM(src.shape, src.dtype), pltpu.HBM(land_shape, src.dtype), jax.ShapeDtypeStruct((8, LANE), f32)),
        in_specs=(HBM, HBM, ANY), out_specs=(SEM, SEM, HBM, HBM, pl.BlockSpec(memory_space=pltpu.VMEM)),
        input_output_aliases={0: 2, 1: 3},
        compiler_params=pltpu.CompilerParams(has_side_effects=EFFECT),
    )(pltpu.with_memory_space_constraint(src, pltpu.HBM),
      pltpu.with_memory_space_constraint(lax.empty(land_shape, src.dtype), pltpu.HBM), after)


def _spread_wait(started, after, plan, name):
    send_sems, recv_sems, src_thru, land_thru, _ = started

    def body(src_ref, land_ref, send_sems, recv_sems, after_ref, src_dead, got_ref):
        for k in range(SPREAD_COPIES[plan]):
            cp = _spread_copy(src_ref, land_ref, send_sems, recv_sems, k, plan)
            cp.wait_send()
            cp.wait_recv()

    return _pcall(
        body, name=name,
        out_shape=(pltpu.HBM(src_thru.shape, src_thru.dtype), pltpu.HBM(land_thru.shape, land_thru.dtype)),
        in_specs=(HBM, HBM, SEM, SEM, ANY), out_specs=(HBM, HBM), input_output_aliases={0: 0, 1: 1},
        compiler_params=pltpu.CompilerParams(has_side_effects=EFFECT),
    )(src_thru, land_thru, send_sems, recv_sems, after)[1]


def _own_slot(land, block, slot):
    zero = jnp.zeros((), jnp.int32)
    return lax.dynamic_update_slice(land, block[None], (slot.astype(jnp.int32),) + (zero,) * block.ndim)


PIECE_NAT = (0, 3072, 4096, 4112, 5136, 6160, 7184, PROJ_WIDTH)


COL_TILE = 256


def _cast_w_in(w3):
    n = w3.shape[0]

    def body(w_ref, o_ref):
        o_ref[...] = w_ref[:, 0, :].astype(bf16)

    return _pcall(
        body, name="cast_w_in", grid=(D_MODEL // COL_TILE,),
        in_specs=[pl.BlockSpec((n, 1, COL_TILE), lambda j: (0, 0, j))],
        out_specs=pl.BlockSpec((n, COL_TILE), lambda j: (0, j)),
        out_shape=jax.ShapeDtypeStruct((n, D_MODEL), bf16),
        compiler_params=_cparams("parallel"),
    )(w3)


def _relayout_w_in(win_g):
    def body(g_ref, o_ref):
        o_ref[NAT_SMALL_END:NAT_SMALL_END + PAD_COLS, :] = jnp.zeros((PAD_COLS, COL_TILE), o_ref.dtype)
        for d in range(N_DEV):
            n0, n1 = d * SHARD_W, (d + 1) * SHARD_W
            cut = min(max(NAT_SMALL_END - n0, 0), SHARD_W)
            if cut > 0:
                o_ref[n0:n0 + cut, :] = g_ref[d, 0:cut, :]
            if cut < SHARD_W:
                o_ref[n0 + cut + PAD_COLS:n1 + PAD_COLS, :] = g_ref[d, cut:SHARD_W, :]

    return _pcall(
        body, name="relayout_w_in", grid=(D_MODEL // COL_TILE,),
        in_specs=[pl.BlockSpec((N_DEV, SHARD_W, COL_TILE), lambda j: (0, 0, j))],
        out_specs=pl.BlockSpec((PROJ_PAD, COL_TILE), lambda j: (0, j)),
        out_shape=jax.ShapeDtypeStruct((PROJ_PAD, D_MODEL), win_g.dtype),
        compiler_params=_cparams("parallel"),
    )(win_g)


def _grad_blocks(g_parts):
    npc = len(g_parts)

    def body(*refs):
        p_refs, o_ref = refs[:npc], refs[npc]
        for d in range(N_DEV):
            n0, n1 = d * SHARD_W, (d + 1) * SHARD_W
            for i in range(npc):
                lo, hi = max(n0, PIECE_NAT[i]), min(n1, PIECE_NAT[i + 1])
                if lo < hi:
                    o_ref[d, lo - n0:hi - n0, :] = p_refs[i][lo - PIECE_NAT[i]:hi - PIECE_NAT[i], :]

    return _pcall(
        body, name="grad_blocks", grid=(D_MODEL // COL_TILE,),
        in_specs=[pl.BlockSpec((p.shape[0], COL_TILE), lambda j: (0, j)) for p in g_parts],
        out_specs=pl.BlockSpec((N_DEV, SHARD_W, COL_TILE), lambda j: (0, 0, j)),
        out_shape=jax.ShapeDtypeStruct((N_DEV, SHARD_W, D_MODEL), bf16),
        compiler_params=_cparams("parallel"),
    )(*g_parts)


def _in_proj(x, nw, wpad_t):
    L = x.shape[0]
    tn = 768
    nj = wpad_t.shape[0] // tn

    def body(x_ref, nw_ref, w_ref, proj_ref, h_ref):
        @pl.when(pl.program_id(0) == 0)
        def _():
            for r in range(0, L, 256):
                xs = x_ref[r:r + 256, :]
                ms = jnp.mean(xs * xs, axis=-1, keepdims=True)
                h_ref[r:r + 256, :] = ((xs * lax.rsqrt(ms + EPS)) * nw_ref[...]).astype(bf16)
        for r in range(0, L, 512):
            proj_ref[r:r + 512, :] = lax.dot_general(h_ref[r:r + 512, :], w_ref[...], (((1,), (1,)), ((), ())),
                                                     preferred_element_type=f32)

    return _pcall(
        body, name="in_proj", grid=(nj,),
        in_specs=[pl.BlockSpec((L, D_MODEL), lambda j: (0, 0)), pl.BlockSpec((1, D_MODEL), lambda j: (0, 0)),
                  pl.BlockSpec((tn, D_MODEL), lambda j: (j, 0))],
        out_specs=[pl.BlockSpec((L, tn), lambda j: (0, j)), pl.BlockSpec((L, D_MODEL), lambda j: (0, 0))],
        out_shape=[jax.ShapeDtypeStruct((L, wpad_t.shape[0]), f32), jax.ShapeDtypeStruct((L, D_MODEL), bf16)],
        compiler_params=_cparams("arbitrary"),
    )(x, nw, wpad_t)


HALVES = [slice(i * LANE, (i + 1) * LANE) for i in range(ELT_W // LANE)]
STEPS_PER_GROUP = GDN_WIDTH // ELT_W


def _conv4(x, cw_ref, ls):
    return (cw_ref[3:4, ls] * x + cw_ref[2:3, ls] * _shift_down(x, 1) + cw_ref[1:2, ls] * _shift_down(x, 2)
            + cw_ref[0:1, ls] * _shift_down(x, 3))


def _qkv_act(proj, cw):
    L = proj.shape[0]

    def body(x_ref, cw_ref, o_ref):
        j = pl.program_id(0)
        scale = jnp.where(j < STEPS_PER_GROUP, HEAD_DIM ** -0.5, 1.0).astype(f32)
        for ls in HALVES:
            c = _conv4(x_ref[:, ls], cw_ref, ls)
            a = c * _sigmoid(c)
            rn = lax.rsqrt(jnp.sum(a * a, axis=1, keepdims=True) + EPS)
            o_ref[:, ls] = jnp.where(j < 2 * STEPS_PER_GROUP, (a * rn) * scale, a)

    return _pcall(
        body, name="qkv_act", grid=(3 * STEPS_PER_GROUP,),
        in_specs=[pl.BlockSpec((L, ELT_W), lambda j: (0, j)), pl.BlockSpec((4, ELT_W), lambda j: (0, j))],
        out_specs=pl.BlockSpec((L, ELT_W), lambda j: (0, j)),
        out_shape=jax.ShapeDtypeStruct((L, 3 * GDN_WIDTH), f32),
        compiler_params=_cparams("parallel"),
    )(proj, cw)


def _scalars(proj, alog_p, dtb_p):
    L = proj.shape[0]
    nc = L // CHUNK

    def body(x_ref, al_ref, dt_ref, sc_ref, gr_ref):
        x = x_ref[...]
        lane = _lanes(x.shape)
        beta = _sigmoid(x)
        g = -jnp.exp(al_ref[...]) * _softplus(x + dt_ref[...])
        gc = jnp.where((lane >= HEADS) & (lane < 2 * HEADS), g, 0.0)
        rc = _rows(x.shape) & (CHUNK - 1)
        for s in (1, 2, 4, 8, 16, 32):
            gc = gc + jnp.where(rc >= s, pltpu.roll(gc, s, 0), 0.0)
        sc_ref[...] = jnp.where(lane < HEADS, beta, gc)
        sel = (_lanes((HEADS, LANE)) == _rows((HEADS, LANE)) + HEADS).astype(f32)
        for c in range(nc):
            gr_ref[c] = lax.dot_general(sel, sc_ref[c * CHUNK:(c + 1) * CHUNK, :], (((1,), (1,)), ((), ())),
                                        preferred_element_type=f32, precision=lax.Precision.HIGHEST)

    return _pcall(
        body, name="scalars", grid=(1,),
        in_specs=[pl.BlockSpec((L, LANE), lambda i: (0, OFF_BA // LANE)), pl.BlockSpec((1, LANE), lambda i: (0, 0)),
                  pl.BlockSpec((1, LANE), lambda i: (0, 0))],
        out_specs=[pl.BlockSpec((L, LANE), lambda i: (0, 0)), pl.BlockSpec((nc, HEADS, CHUNK), lambda i: (0, 0, 0))],
        out_shape=[jax.ShapeDtypeStruct((L, LANE), f32), jax.ShapeDtypeStruct((nc, HEADS, CHUNK), f32)],
        compiler_params=_cparams("arbitrary"),
    )(proj, alog_p, dtb_p)


def _head_scalars(sc, gr_ref, h, ci=0):
    lane = _lanes(sc.shape)
    beta = jnp.sum(jnp.where(lane == h, sc, 0.0), axis=1, keepdims=True)
    gcc = jnp.sum(jnp.where(lane == HEADS + h, sc, 0.0), axis=1, keepdims=True)
    gcr = gr_ref[ci, h:h + 1, :]
    gl = jnp.sum(jnp.where(_lanes(gcr.shape) == CHUNK - 1, gcr, 0.0), axis=1, keepdims=True)
    ii, jj = _rows((CHUNK, CHUNK)), _lanes((CHUNK, CHUNK))
    dmat = jnp.where(ii >= jj, jnp.exp(jnp.minimum(gcc - gcr, 0.0)), 0.0)
    dmat_t = jnp.where(jj >= ii, jnp.exp(jnp.minimum(gcr - gcc, 0.0)), 0.0)
    return beta, gcc, gl, dmat, dmat_t, ii, jj


def _gdn_fwd(qkv, sc, gr):
    L = qkv.shape[0]
    nc = L // CHUNK
    W = GDN_WIDTH
    cps = GDN_CPS if nc % GDN_CPS == 0 else 1
    rows_per_step = cps * CHUNK

    def body(qkv_ref, sc_ref, gr_ref, o_ref, u_ref, w_ref, vn_ref, t_ref, sp_ref, s_scr):
        @pl.when(pl.program_id(0) == 0)
        def _():
            s_scr[...] = jnp.zeros_like(s_scr)
        HS = range(cps * HEADS)
        hd = [i % HEADS for i in HS]
        rs = [slice((i // HEADS) * CHUNK, (i // HEADS + 1) * CHUNK) for i in HS]
        cs = [slice(hd[i] * HEAD_DIM, (hd[i] + 1) * HEAD_DIM) for i in HS]
        q = [qkv_ref[rs[i], hd[i] * HEAD_DIM:(hd[i] + 1) * HEAD_DIM] for i in HS]
        k = [qkv_ref[rs[i], W + hd[i] * HEAD_DIM:W + (hd[i] + 1) * HEAD_DIM] for i in HS]
        v = [qkv_ref[rs[i], 2 * W + hd[i] * HEAD_DIM:2 * W + (hd[i] + 1) * HEAD_DIM] for i in HS]
        hsc = [_head_scalars(sc_ref[rs[i], :], gr_ref, hd[i], i // HEADS) for i in HS]
        beta, gcc, gl, dmat = ([x[i] for x in hsc] for i in range(4))
        ii, jj = hsc[0][5], hsc[0][6]
        eg = [jnp.exp(gcc[h]) for h in HS]
        kb = [k[h] * beta[h] for h in HS]
        kk = [_mm_nt(kb[h], k[h]) for h in HS]
        qk = [_mm_nt(q[h], k[h]) for h in HS]
        n0 = [-jnp.where(ii > jj, kk[h] * dmat[h], 0.0) for h in HS]
        n1 = [_mm(n0[h], n0[h]) for h in HS]
        n2 = [_mm(n1[h], n1[h]) for h in HS]
        p01 = [n0[h] + n1[h] + _mm(n0[h], n1[h]) for h in HS]
        n3 = [_mm(n2[h], n2[h]) for h in HS]
        n4 = [_mm(n3[h], n3[h]) for h in HS]
        p23 = [n2[h] + n3[h] + _mm(n2[h], n3[h]) for h in HS]
        n5 = [_mm(n4[h], n4[h]) for h in HS]
        p03 = [p01[h] + p23[h] + _mm(p01[h], p23[h]) for h in HS]
        p45 = [n4[h] + n5[h] + _mm(n4[h], n5[h]) for h in HS]
        t = [p03[h] + p45[h] + _mm(p03[h], p45[h]) for h in HS]
        vb = [v[h] * beta[h] for h in HS]
        kbg = [kb[h] * eg[h] for h in HS]
        uw = [_mm(t[h], jnp.concatenate([vb[h], kbg[h]], axis=1)) for h in HS]
        u = [vb[h] + uw[h][:, :HEAD_DIM] for h in HS]
        w = [kbg[h] + uw[h][:, HEAD_DIM:] for h in HS]
        wq = [jnp.concatenate([w[h], q[h] * eg[h]], axis=0) for h in HS]
        p = [jnp.where(ii >= jj, qk[h] * dmat[h], 0.0) for h in HS]
        ks = [k[h] * jnp.exp(gl[h] - gcc[h]) for h in HS]
        s = [s_scr[h] for h in range(HEADS)]
        for ci in range(cps):
            IS = range(ci * HEADS, (ci + 1) * HEADS)
            ws = [_mm(wq[i], s[hd[i]]) for i in IS]
            vn = [u[i] - ws[hd[i]][:CHUNK] for i in IS]
            pv = [_mm(p[i], vn[hd[i]]) for i in IS]
            kv = [_mm_tn(ks[i], vn[hd[i]]) for i in IS]
            for i in IS:
                h = hd[i]
                sp_ref[ci, cs[i], :] = s[h]
                o_ref[rs[i], cs[i]] = ws[h][CHUNK:] + pv[h]
                vn_ref[rs[i], cs[i]] = vn[h].astype(bf16)
            s = [jnp.exp(gl[i]) * s[hd[i]] + kv[hd[i]] for i in IS]
        for h in range(HEADS):
            s_scr[h] = s[h]
        for i in HS:
            u_ref[rs[i], cs[i]] = u[i].astype(bf16)
            w_ref[rs[i], cs[i]] = w[i].astype(bf16)
            t_ref[i // HEADS, hd[i]] = t[i].astype(bf16)

    row = lambda c: (c, 0)
    act, act16 = jax.ShapeDtypeStruct((L, W), f32), jax.ShapeDtypeStruct((L, W), bf16)
    return _pcall(
        body, name="gdn_fwd", grid=(nc // cps,),
        in_specs=[pl.BlockSpec((rows_per_step, 3 * W), row), pl.BlockSpec((rows_per_step, LANE), row),
                  pl.BlockSpec((cps, HEADS, CHUNK), lambda c: (c, 0, 0))],
        out_specs=[pl.BlockSpec((rows_per_step, W), row)] * 4 + [
            pl.BlockSpec((cps, HEADS, CHUNK, CHUNK), lambda c: (c, 0, 0, 0)),
            pl.BlockSpec((cps, W, HEAD_DIM), lambda c: (c, 0, 0))],
        out_shape=[act, act16, act16, act16, jax.ShapeDtypeStruct((nc, HEADS, CHUNK, CHUNK), bf16),
                   jax.ShapeDtypeStruct((nc, W, HEAD_DIM), f32)],
        scratch_shapes=[pltpu.VMEM((HEADS, HEAD_DIM, HEAD_DIM), f32)],
        compiler_params=_cparams("arbitrary"),
    )(qkv, sc, gr)


def _gdn_gate(o, proj, gnw):
    L = o.shape[0]

    def body(o_ref, z_ref, w_ref, m_ref):
        for ls in HALVES:
            ov, z = o_ref[:, ls], z_ref[:, ls]
            rms = lax.rsqrt(jnp.mean(ov * ov, axis=-1, keepdims=True) + EPS)
            m_ref[:, ls] = (((ov * rms) * w_ref[...]) * (z * _sigmoid(z))).astype(bf16)

    return _pcall(
        body, name="gdn_gate", grid=(GDN_WIDTH // ELT_W,),
        in_specs=[pl.BlockSpec((L, ELT_W), lambda j: (0, j)), pl.BlockSpec((L, ELT_W), lambda j: (0, OFF_ZG // ELT_W + j)),
                  pl.BlockSpec((1, LANE), lambda j: (0, 0))],
        out_specs=pl.BlockSpec((L, ELT_W), lambda j: (0, j)),
        out_shape=jax.ShapeDtypeStruct((L, GDN_WIDTH), bf16),
        compiler_params=_cparams("parallel"),
    )(o, proj, gnw)


def _conv3(u, cw_ref, ls):
    return cw_ref[2:3, ls] * u + cw_ref[1:2, ls] * _shift_down(u, 1) + cw_ref[0:1, ls] * _shift_down(u, 2)


def _conv_specs(L):
    blk = lambda off: pl.BlockSpec((L, ELT_W), lambda j, off=off: (0, off // ELT_W + j))
    return [blk(OFF_B), blk(OFF_C), blk(OFF_HC), blk(OFF_ZC),
            pl.BlockSpec((3, ELT_W), lambda j: (0, j)), pl.BlockSpec((1, ELT_W), lambda j: (0, j))]


def _conv_fwd(proj, cw, cb):
    L = proj.shape[0]

    def body(b_ref, c_ref, h_ref, z_ref, cw_ref, cb_ref, m_ref):
        for ls in HALVES:
            z = z_ref[:, ls]
            cv = _conv3(c_ref[:, ls] * h_ref[:, ls], cw_ref, ls) + cb_ref[:, ls]
            m_ref[:, ls] = ((b_ref[:, ls] * cv) * (z * _sigmoid(z))).astype(bf16)

    return _pcall(
        body, name="conv_fwd", grid=(CONV_WIDTH // ELT_W,),
        in_specs=_conv_specs(L), out_specs=pl.BlockSpec((L, ELT_W), lambda j: (0, j)),
        out_shape=jax.ShapeDtypeStruct((L, CONV_WIDTH), bf16),
        compiler_params=_cparams("parallel"),
    )(proj, proj, proj, proj, cw, cb)


def _out_proj_loss(x, mix_a, mix_b, wo, fw, tgt):
    L = x.shape[0]
    tm = min(512, L)

    def body(x_ref, ma_ref, mb_ref, wo_ref, fw_ref, t_ref, dy_ref, dyb_ref, dma_ref, dmb_ref, gfw_ref, loss_ref):
        @pl.when(pl.program_id(0) == 0)
        def _():
            gfw_ref[...] = jnp.zeros_like(gfw_ref)
            loss_ref[...] = jnp.zeros_like(loss_ref)
        y = x_ref[...] + jnp.dot(ma_ref[...], wo_ref[:GDN_WIDTH, :], preferred_element_type=f32) \
            + jnp.dot(mb_ref[...], wo_ref[GDN_WIDTH:, :], preferred_element_type=f32)
        r = lax.rsqrt(jnp.mean(y * y, axis=-1, keepdims=True) + EPS)
        yh = y * r
        fwv = fw_ref[...]
        diff = yh * fwv - t_ref[...]
        loss_ref[...] += jnp.sum(jnp.sum(diff * diff, axis=-1, keepdims=True), axis=0, keepdims=True) * (0.5 / D_MODEL)
        dout = diff * (1.0 / D_MODEL)
        gfw_ref[...] += jnp.sum(dout * yh, axis=0, keepdims=True)
        dyh = dout * fwv
        dy = r * (dyh - yh * jnp.mean(dyh * yh, axis=-1, keepdims=True))
        dy_ref[...] = dy
        dyb = dy.astype(bf16)
        dyb_ref[...] = dyb
        dma_ref[...] = lax.dot_general(dyb, wo_ref[:GDN_WIDTH, :], (((1,), (1,)), ((), ())), preferred_element_type=f32)
        dmb_ref[...] = lax.dot_general(dyb, wo_ref[GDN_WIDTH:, :], (((1,), (1,)), ((), ())), preferred_element_type=f32)

    row = lambda i: (i, 0)
    fix = lambda i: (0, 0)
    act = jax.ShapeDtypeStruct((L, D_MODEL), f32)
    return _pcall(
        body, name="out_proj_loss", grid=(L // tm,),
        in_specs=[pl.BlockSpec((tm, D_MODEL), row), pl.BlockSpec((tm, GDN_WIDTH), row), pl.BlockSpec((tm, CONV_WIDTH), row),
                  pl.BlockSpec((GDN_WIDTH + CONV_WIDTH, D_MODEL), fix), pl.BlockSpec((1, D_MODEL), fix),
                  pl.BlockSpec((tm, D_MODEL), row)],
        out_specs=[pl.BlockSpec((tm, D_MODEL), row), pl.BlockSpec((tm, D_MODEL), row), pl.BlockSpec((tm, GDN_WIDTH), row),
                   pl.BlockSpec((tm, CONV_WIDTH), row), pl.BlockSpec((1, D_MODEL), fix), pl.BlockSpec((1, LANE), fix)],
        out_shape=[act, jax.ShapeDtypeStruct((L, D_MODEL), bf16), act, act,
                   jax.ShapeDtypeStruct((1, D_MODEL), f32), jax.ShapeDtypeStruct((1, LANE), f32)],
        compiler_params=_cparams("arbitrary"),
    )(x, mix_a, mix_b, wo, fw, tgt)


def _tn_matmul(a, b, name):
    L, M = a.shape
    N = b.shape[1]
    tm = 512 if M % 512 == 0 else M

    def body(a_ref, b_ref, o_ref):
        o_ref[...] = lax.dot_general(a_ref[...], b_ref[...], (((0,), (0,)), ((), ())),
                                     preferred_element_type=f32).astype(o_ref.dtype)

    return _pcall(
        body, name=name, grid=(M // tm,),
        in_specs=[pl.BlockSpec((L, tm), lambda i: (0, i)), pl.BlockSpec((L, N), lambda i: (0, 0))],
        out_specs=pl.BlockSpec((tm, N), lambda i: (i, 0)),
        out_shape=jax.ShapeDtypeStruct((M, N), bf16),
        compiler_params=_cparams("parallel"),
    )(a, b)


def _gdn_gate_bwd(o, proj, gnw, dmix_a):
    L = o.shape[0]

    def body(o_ref, z_ref, w_ref, dm_ref, do_ref, dz_ref, gw_ref):
        @pl.when(pl.program_id(0) == 0)
        def _():
            gw_ref[...] = jnp.zeros_like(gw_ref)
        wv = w_ref[...]
        for ls in HALVES:
            ov, z, dm = o_ref[:, ls], z_ref[:, ls], dm_ref[:, ls]
            rms = lax.rsqrt(jnp.mean(ov * ov, axis=-1, keepdims=True) + EPS)
            xh = ov * rms
            sg = _sigmoid(z)
            d_on = dm * (z * sg)
            dz_ref[:, ls] = (dm * (xh * wv) * (sg * (1.0 + z * (1.0 - sg)))).astype(bf16)
            gw_ref[...] += jnp.sum(d_on * xh, axis=0, keepdims=True)
            dxh = d_on * wv
            do_ref[:, ls] = (rms * (dxh - xh * jnp.mean(dxh * xh, axis=-1, keepdims=True))).astype(bf16)

    wide = pl.BlockSpec((L, ELT_W), lambda j: (0, j))
    return _pcall(
        body, name="gdn_gate_bwd", grid=(GDN_WIDTH // ELT_W,),
        in_specs=[wide, pl.BlockSpec((L, ELT_W), lambda j: (0, OFF_ZG // ELT_W + j)),
                  pl.BlockSpec((1, LANE), lambda j: (0, 0)), wide],
        out_specs=[wide, wide, pl.BlockSpec((1, LANE), lambda j: (0, 0))],
        out_shape=[jax.ShapeDtypeStruct((L, GDN_WIDTH), bf16), jax.ShapeDtypeStruct((L, GDN_WIDTH), bf16),
                   jax.ShapeDtypeStruct((1, LANE), f32)],
        compiler_params=_cparams("arbitrary"),
    )(o, proj, gnw, dmix_a)


def _conv_bwd(proj, cw, cb, dmix_b):
    L = proj.shape[0]

    def body(b_ref, c_ref, h_ref, z_ref, cw_ref, cb_ref, dm_ref, db_ref, dc_ref, dh_ref, dz_ref, gcw_ref, gcb_ref):
        for ls in HALVES:
            bv, cv_, hv, z, dm = b_ref[:, ls], c_ref[:, ls], h_ref[:, ls], z_ref[:, ls], dm_ref[:, ls]
            u = cv_ * hv
            cv = _conv3(u, cw_ref, ls) + cb_ref[:, ls]
            sg = _sigmoid(z)
            sz = z * sg
            db_ref[:, ls] = (dm * cv * sz).astype(bf16)
            dz_ref[:, ls] = (dm * (bv * cv) * (sg * (1.0 + z * (1.0 - sg)))).astype(bf16)
            dcv = dm * bv * sz
            gcb_ref[:, ls] = jnp.sum(dcv, axis=0, keepdims=True)
            dcv1, dcv2 = _shift_up(dcv, 1), _shift_up(dcv, 2)
            gcw_ref[2:3, ls] = jnp.sum(dcv * u, axis=0, keepdims=True)
            gcw_ref[1:2, ls] = jnp.sum(dcv1 * u, axis=0, keepdims=True)
            gcw_ref[0:1, ls] = jnp.sum(dcv2 * u, axis=0, keepdims=True)
            du = cw_ref[2:3, ls] * dcv + cw_ref[1:2, ls] * dcv1 + cw_ref[0:1, ls] * dcv2
            dc_ref[:, ls] = (du * hv).astype(bf16)
            dh_ref[:, ls] = (du * cv_).astype(bf16)

    col = pl.BlockSpec((L, ELT_W), lambda j: (0, j))
    act = jax.ShapeDtypeStruct((L, CONV_WIDTH), bf16)
    return _pcall(
        body, name="conv_bwd", grid=(CONV_WIDTH // ELT_W,),
        in_specs=_conv_specs(L) + [col],
        out_specs=[col, col, col, col, pl.BlockSpec((3, ELT_W), lambda j: (0, j)), pl.BlockSpec((1, ELT_W), lambda j: (0, j))],
        out_shape=[act, act, act, act, jax.ShapeDtypeStruct((3, CONV_WIDTH), f32), jax.ShapeDtypeStruct((1, CONV_WIDTH), f32)],
        compiler_params=_cparams("parallel"),
    )(proj, proj, proj, proj, cw, cb, dmix_b)


def _gdn_bwd(qkv, sc, gr, u_all, w_all, vn_all, t_all, sp_all, do_all):
    L = qkv.shape[0]
    nc = L // CHUNK
    W = GDN_WIDTH
    cps = GDN_CPS_BWD if nc % GDN_CPS_BWD == 0 else 1
    rows_per_step = cps * CHUNK
    nsteps = nc // cps

    def body(qkv_ref, sc_ref, gr_ref, u_ref, w_ref, vn_ref, t_ref, sp_ref, do_ref, dqkv_ref, dsc_ref, dgr_ref, ds_scr):
        @pl.when(pl.program_id(0) == 0)
        def _():
            ds_scr[...] = jnp.zeros_like(ds_scr)
        HS = range(cps * HEADS)
        hd = [i % HEADS for i in HS]
        rs = [slice((i // HEADS) * CHUNK, (i // HEADS + 1) * CHUNK) for i in HS]
        cs = [slice(hd[i] * HEAD_DIM, (hd[i] + 1) * HEAD_DIM) for i in HS]
        q = [qkv_ref[rs[i], hd[i] * HEAD_DIM:(hd[i] + 1) * HEAD_DIM] for i in HS]
        k = [qkv_ref[rs[i], W + hd[i] * HEAD_DIM:W + (hd[i] + 1) * HEAD_DIM] for i in HS]
        v = [qkv_ref[rs[i], 2 * W + hd[i] * HEAD_DIM:2 * W + (hd[i] + 1) * HEAD_DIM] for i in HS]
        hsc = [_head_scalars(sc_ref[rs[i], :], gr_ref, hd[i], i // HEADS) for i in HS]
        beta, gcc, gl, dmat, dmat_t = ([x[i] for x in hsc] for i in range(5))
        ii, jj = hsc[0][5], hsc[0][6]
        eg = [jnp.exp(gcc[h]) for h in HS]
        ekl = [jnp.exp(gl[h] - gcc[h]) for h in HS]
        egl = [jnp.exp(gl[h]) for h in HS]
        kb = [k[h] * beta[h] for h in HS]
        ks = [k[h] * ekl[h] for h in HS]
        do = [do_ref[rs[h], cs[h]] for h in HS]
        vn = [vn_ref[rs[h], cs[h]] for h in HS]
        s = [sp_ref[h // HEADS, cs[h], :] for h in HS]
        w = [w_ref[rs[h], cs[h]] for h in HS]
        qd = [q[h] * eg[h] for h in HS]

        kq = [_mm_nt(k[h], q[h]) for h in HS]
        p_t = [jnp.where(jj >= ii, kq[h] * dmat_t[h], 0.0) for h in HS]
        ptd = [_mm(p_t[h], do[h]) for h in HS]
        qw =[jnp.concatenate([qd[h], -w[h]], axis=0) for h in HS]
        dsn, dvn, dodv = [None] * len(HS), [None] * len(HS), [None] * len(HS)
        ds_cur = [ds_scr[h] for h in range(HEADS)]
        for ci in reversed(range(cps)):
            IS = range(ci * HEADS, (ci + 1) * HEADS)
            ksd = [_mm(ks[i], ds_cur[hd[i]]) for i in IS]
            for i in IS:
                dsn[i] = ds_cur[hd[i]]
                dvn[i] = ptd[i] + ksd[hd[i]]
                dodv[i] = jnp.concatenate([do[i], dvn[i]], axis=0)
            dsq = [_mm_tn(qw[i], dodv[i]) for i in IS]
            ds_cur = [egl[i] * ds_cur[hd[i]] + dsq[hd[i]] for i in IS]
        for h in range(HEADS):
            ds_scr[h] = ds_cur[h]
        x1 = [_mm_nt(dodv[h], s[h]) for h in HS]
        dks = [_mm_nt(vn[h], dsn[h]) for h in HS]
        dov = [_mm_nt(do[h], vn[h]) for h in HS]
        vdo = [_mm_nt(vn[h], do[h]) for h in HS]
        kk = [_mm_nt(kb[h], k[h]) for h in HS]
        qk = [_mm_nt(q[h], k[h]) for h in HS]
        dgl = [egl[h] * jnp.sum(jnp.sum(s[h] * dsn[h], axis=1, keepdims=True), axis=0, keepdims=True) for h in HS]
        dqd = [x1[h][:CHUNK] for h in HS]
        duw = [jnp.concatenate([dvn[h], -x1[h][CHUNK:]], axis=1) for h in HS]
        tdu = [_mm_tn(t_ref[h // HEADS, hd[h]], duw[h]) for h in HS]
        dvk = [duw[h] + tdu[h] for h in HS]
        uw = [jnp.concatenate([u_ref[rs[h], cs[h]], w[h]], axis=1) for h in HS]
        da = [-jnp.where(ii > jj, _mm_nt(dvk[h], uw[h]), 0.0) for h in HS]
        da_t = [-jnp.where(jj > ii, _mm_nt(uw[h], dvk[h]), 0.0) for h in HS]
        dp = [jnp.where(ii >= jj, dov[h], 0.0) for h in HS]
        dp_t = [jnp.where(jj >= ii, vdo[h], 0.0) for h in HS]
        r1 = [_mm(jnp.concatenate([da[h] * dmat[h], dp[h] * dmat[h]], axis=0), k[h]) for h in HS]
        dk1 = [_mm(jnp.concatenate([da_t[h] * dmat_t[h], dp_t[h] * dmat_t[h]], axis=1),
                   jnp.concatenate([kb[h], q[h]], axis=0)) for h in HS]
        lane = _lanes((CHUNK, LANE))
        for ci in range(cps):
            dsc = jnp.zeros((CHUNK, LANE), f32)
            for i in range(ci * HEADS, (ci + 1) * HEADS):
                h = hd[i]
                a = jnp.where(ii > jj, kk[i] * dmat[i], 0.0)
                p = jnp.where(ii >= jj, qk[i] * dmat[i], 0.0)
                gmat = da[i] * a + dp[i] * p
                dvb, dkbg = dvk[i][:, :HEAD_DIM], dvk[i][:, HEAD_DIM:]
                kbg = kb[i] * eg[i]
                dkb = r1[i][:CHUNK] + dkbg * eg[i]
                dq = r1[i][CHUNK:] + dqd[i] * eg[i]
                dk = dk1[i] + dks[i] * ekl[i] + dkb * beta[i]
                dbeta = jnp.sum(dkb * k[i] + dvb * v[i], axis=1, keepdims=True)
                ksum = jnp.sum(dks[i] * ks[i], axis=1, keepdims=True)
                dgl_tot = dgl[i] + jnp.sum(ksum, axis=0, keepdims=True)
                dgc = (jnp.sum(gmat, axis=1, keepdims=True) + jnp.sum(dqd[i] * qd[i] + dkbg * kbg, axis=1, keepdims=True)
                       - ksum)
                dgc = dgc + jnp.where(_rows(dgc.shape) == CHUNK - 1, dgl_tot, 0.0)
                dqkv_ref[rs[i], h * HEAD_DIM:(h + 1) * HEAD_DIM] = dq
                dqkv_ref[rs[i], W + h * HEAD_DIM:W + (h + 1) * HEAD_DIM] = dk
                dqkv_ref[rs[i], 2 * W + h * HEAD_DIM:2 * W + (h + 1) * HEAD_DIM] = dvb * beta[i]
                dsc = jnp.where(lane == h, dbeta, jnp.where(lane == HEADS + h, dgc, dsc))
                dgr_ref[ci, h:h + 1, :] = jnp.sum(gmat, axis=0, keepdims=True)
            dsc_ref[ci * CHUNK:(ci + 1) * CHUNK, :] = dsc

    row = lambda c: (nsteps - 1 - c, 0)
    lead3 = lambda c: (nsteps - 1 - c, 0, 0)
    return _pcall(
        body, name="gdn_bwd", grid=(nsteps,),
        in_specs=[pl.BlockSpec((rows_per_step, 3 * W), row), pl.BlockSpec((rows_per_step, LANE), row),
                  pl.BlockSpec((cps, HEADS, CHUNK), lead3),
                  pl.BlockSpec((rows_per_step, W), row), pl.BlockSpec((rows_per_step, W), row),
                  pl.BlockSpec((rows_per_step, W), row),
                  pl.BlockSpec((cps, HEADS, CHUNK, CHUNK), lambda c: (nsteps - 1 - c, 0, 0, 0)),
                  pl.BlockSpec((cps, W, HEAD_DIM), lead3), pl.BlockSpec((rows_per_step, W), row)],
        out_specs=[pl.BlockSpec((rows_per_step, 3 * W), row), pl.BlockSpec((rows_per_step, LANE), row),
                   pl.BlockSpec((cps, HEADS, CHUNK), lead3)],
        out_shape=[jax.ShapeDtypeStruct((L, 3 * W), f32), jax.ShapeDtypeStruct((L, LANE), f32),
                   jax.ShapeDtypeStruct((nc, HEADS, CHUNK), f32)],
        scratch_shapes=[pltpu.VMEM((HEADS, HEAD_DIM, HEAD_DIM), f32)],
        compiler_params=_cparams("arbitrary"),
    )(qkv, sc, gr, u_all, w_all, vn_all, t_all, sp_all, do_all)


def _qkv_bwd(proj, cw, dn):
    L = proj.shape[0]

    def body(x_ref, cw_ref, dn_ref, dx_ref, gcw_ref):
        j = pl.program_id(0)
        scale = jnp.where(j < STEPS_PER_GROUP, HEAD_DIM ** -0.5, 1.0).astype(f32)
        for ls in HALVES:
            x, dn_v = x_ref[:, ls], dn_ref[:, ls]
            c = _conv4(x, cw_ref, ls)
            sg = _sigmoid(c)
            a = c * sg
            rn = lax.rsqrt(jnp.sum(a * a, axis=1, keepdims=True) + EPS)
            da_n = (scale * rn) * (dn_v - a * ((rn * rn) * jnp.sum(dn_v * a, axis=1, keepdims=True)))
            da = jnp.where(j < 2 * STEPS_PER_GROUP, da_n, dn_v)
            dc = da * (sg * (1.0 + c * (1.0 - sg)))
            dc1, dc2, dc3 = _shift_up(dc, 1), _shift_up(dc, 2), _shift_up(dc, 3)
            gcw_ref[3:4, ls] = jnp.sum(dc * x, axis=0, keepdims=True)
            gcw_ref[2:3, ls] = jnp.sum(dc1 * x, axis=0, keepdims=True)
            gcw_ref[1:2, ls] = jnp.sum(dc2 * x, axis=0, keepdims=True)
            gcw_ref[0:1, ls] = jnp.sum(dc3 * x, axis=0, keepdims=True)
            dx = cw_ref[3:4, ls] * dc + cw_ref[2:3, ls] * dc1 + cw_ref[1:2, ls] * dc2 + cw_ref[0:1, ls] * dc3
            dx_ref[:, ls] = dx.astype(bf16)

    col = pl.BlockSpec((L, ELT_W), lambda j: (0, j))
    wspec = pl.BlockSpec((4, ELT_W), lambda j: (0, j))
    return _pcall(
        body, name="qkv_bwd", grid=(3 * STEPS_PER_GROUP,),
        in_specs=[col, wspec, col], out_specs=[col, wspec],
        out_shape=[jax.ShapeDtypeStruct((L, 3 * GDN_WIDTH), bf16), jax.ShapeDtypeStruct((4, 3 * GDN_WIDTH), f32)],
        compiler_params=_cparams("parallel"),
    )(proj, cw, dn)


def _scalars_bwd(proj, alog_p, dtb_p, dsc, dgr_col):
    L = proj.shape[0]

    def body(x_ref, al_ref, dt_ref, dsc_ref, dgr_ref, dba_ref, gs_ref):
        x, dsc_v = x_ref[...], dsc_ref[...]
        lane = _lanes(x.shape)
        dec = (lane >= HEADS) & (lane < 2 * HEADS)
        dg = jnp.where(dec, dsc_v - dgr_ref[...], 0.0)
        rc = _rows(x.shape) & (CHUNK - 1)
        for s in (1, 2, 4, 8, 16, 32):
            dg = dg + jnp.where(rc + s < CHUNK, pltpu.roll(dg, L - s, 0), 0.0)
        xa = x + dt_ref[...]
        ea = jnp.exp(al_ref[...])
        g = -ea * _softplus(xa)
        da = dg * (-ea) * _sigmoid(xa)
        beta = _sigmoid(x)
        db = dsc_v * beta * (1.0 - beta)
        dba_ref[...] = jnp.where(lane < HEADS, db, jnp.where(dec, da, 0.0)).astype(bf16)
        g_al = jnp.sum(jnp.where(dec, dg * g, 0.0), axis=0, keepdims=True)
        g_dt = jnp.sum(jnp.where(dec, da, 0.0), axis=0, keepdims=True)
        row8 = _rows(gs_ref.shape)
        gs = jnp.where(row8 == 0, g_al, jnp.where(row8 == 1, g_dt, 0.0))
        gs_ref[...] = pltpu.roll(gs, LANE - HEADS, 1)

    full = pl.BlockSpec((L, LANE), lambda i: (0, 0))
    vec = pl.BlockSpec((1, LANE), lambda i: (0, 0))
    return _pcall(
        body, name="scalars_bwd", grid=(1,),
        in_specs=[pl.BlockSpec((L, LANE), lambda i: (0, OFF_BA // LANE)), vec, vec, full, full],
        out_specs=[full, pl.BlockSpec((8, LANE), lambda i: (0, 0))],
        out_shape=[jax.ShapeDtypeStruct((L, LANE), bf16), jax.ShapeDtypeStruct((8, LANE), f32)],
        compiler_params=_cparams("arbitrary"),
    )(proj, alog_p, dtb_p, dsc, dgr_col)


def _input_grad(pieces, offs, wpad, x, nw, dy):
    L = x.shape[0]
    tm = min(512, L)
    npc = len(pieces)

    def body(*refs):
        p_refs = refs[:npc]
        w_hbm, x_ref, nw_ref, dy_ref, gx_ref, gnw_ref, w_vmem, sems = refs[npc:]
        first = pl.program_id(0) == 0
        loads = [pltpu.make_async_copy(w_hbm.at[off:off + p.shape[1], :], w_vmem.at[off:off + p.shape[1], :], sems.at[k])
                 for k, (p, off) in enumerate(zip(p_refs, offs))]

        @pl.when(first)
        def _():
            for cp in loads:
                cp.start()
            gnw_ref[...] = jnp.zeros_like(gnw_ref)
        dh = None
        for k, (p_ref, off) in enumerate(zip(p_refs, offs)):
            wd = p_ref.shape[1]
            pl.when(first)(loads[k].wait)
            part = jnp.dot(p_ref[...], w_vmem[off:off + wd, :], preferred_element_type=f32)
            dh = part if dh is None else dh + part
        xv, nwv = x_ref[...], nw_ref[...]
        r = lax.rsqrt(jnp.mean(xv * xv, axis=-1, keepdims=True) + EPS)
        xh = xv * r
        gnw_ref[...] += jnp.sum(dh * xh, axis=0, keepdims=True)
        dxh = dh * nwv
        gx_ref[...] = dy_ref[...] + r * (dxh - xh * jnp.mean(dxh * xh, axis=-1, keepdims=True))

    row = lambda i: (i, 0)
    fix = lambda i: (0, 0)
    return _pcall(
        body, name="input_grad", grid=(L // tm,),
        in_specs=[pl.BlockSpec((tm, p.shape[1]), row) for p in pieces] + [
            ANY, pl.BlockSpec((tm, D_MODEL), row), pl.BlockSpec((1, D_MODEL), fix), pl.BlockSpec((tm, D_MODEL), row)],
        out_specs=[pl.BlockSpec((tm, D_MODEL), row), pl.BlockSpec((1, D_MODEL), fix)],
        out_shape=[jax.ShapeDtypeStruct((L, D_MODEL), f32), jax.ShapeDtypeStruct((1, D_MODEL), f32)],
        scratch_shapes=[pltpu.VMEM(wpad.shape, bf16), pltpu.SemaphoreType.DMA((npc,))],
        compiler_params=_cparams("arbitrary"),
    )(*pieces, wpad, x, nw, dy)


def _adamw_reduce(parts, w, m, v, name):
    R, C = w.shape
    n_parts = parts.shape[0]
    tr = 128 if R % 128 == 0 else R
    c1 = 1.0 - ADAM_B1 ** ADAM_STEP
    c2 = 1.0 - ADAM_B2 ** ADAM_STEP

    def body(p_ref, w_ref, m_ref, v_ref, g_ref, d_ref, nm_ref, nv_ref):
        g = p_ref[0].astype(f32)
        for s in range(1, n_parts):
            g = g + p_ref[s].astype(f32)
        nm = ADAM_B1 * m_ref[...] + (1.0 - ADAM_B1) * g
        nv = ADAM_B2 * v_ref[...] + (1.0 - ADAM_B2) * (g * g)
        g_ref[...] = g
        nm_ref[...] = nm
        nv_ref[...] = nv
        d_ref[...] = -ADAM_LR * ((nm / c1) / (jnp.sqrt(nv / c2) + ADAM_EPS) + ADAM_WD * w_ref[...])

    blk = pl.BlockSpec((tr, C), lambda i: (i, 0))
    out = jax.ShapeDtypeStruct((R, C), f32)
    return _pcall(
        body, name=name, grid=(R // tr,),
        in_specs=[pl.BlockSpec((n_parts, tr, C), lambda i: (0, i, 0)), blk, blk, blk],
        out_specs=[blk] * 4, out_shape=[out] * 4,
        compiler_params=_cparams("parallel"),
    )(parts, w, m, v)


SMALL_SLOTS = ((0, D_MODEL), (D_MODEL, D_MODEL), (2 * D_MODEL, D_MODEL), (3 * D_MODEL, LANE),
               (3 * D_MODEL + LANE, HEADS), (3 * D_MODEL + 2 * LANE, HEADS))
SMALL_LOSS = 3 * D_MODEL + 3 * LANE
SMALL_W = SMALL_LOSS + LANE


def _pack_small(gs, after):
    def body(nw_ref, cb_ref, fw_ref, gn_ref, sc_ref, ls_ref, after_ref, o_ref):
        for ref, (start, width) in zip((nw_ref, cb_ref, fw_ref, gn_ref), SMALL_SLOTS[:4]):
            o_ref[:, start:start + width] = ref[...]
        o_ref[:, SMALL_SLOTS[4][0]:SMALL_SLOTS[4][0] + LANE] = sc_ref[0:1, :]
        o_ref[:, SMALL_SLOTS[5][0]:SMALL_SLOTS[5][0] + LANE] = sc_ref[1:2, :]
        o_ref[:, SMALL_LOSS:SMALL_W] = ls_ref[...]

    vm = pl.BlockSpec(memory_space=pltpu.VMEM)
    return _pcall(body, name="pack_small_grads", out_shape=jax.ShapeDtypeStruct((1, SMALL_W), f32),
                  in_specs=[vm] * 6 + [ANY], out_specs=vm)(*gs, after)


def _adamw_small(parts, ws, ms, vs):
    c1 = 1.0 - ADAM_B1 ** ADAM_STEP
    c2 = 1.0 - ADAM_B2 ** ADAM_STEP
    np_ = len(ws)

    def body(*refs):
        p_ref = refs[0]
        w_refs, m_refs, v_refs = refs[1:1 + np_], refs[1 + np_:1 + 2 * np_], refs[1 + 2 * np_:1 + 3 * np_]
        outs = refs[1 + 3 * np_:]
        g_refs, d_refs, nm_refs, nv_refs = (outs[i * np_:(i + 1) * np_] for i in range(4))
        loss_ref = outs[4 * np_]

        def total(start, width):
            t = p_ref[0, :, start:start + width]
            for s in range(1, N_DEV):
                t = t + p_ref[s, :, start:start + width]
            return t

        for i, (start, width) in enumerate(SMALL_SLOTS):
            g = total(start, width)
            nm = ADAM_B1 * m_refs[i][...] + (1.0 - ADAM_B1) * g
            nv = ADAM_B2 * v_refs[i][...] + (1.0 - ADAM_B2) * (g * g)
            g_refs[i][...] = g
            nm_refs[i][...] = nm
            nv_refs[i][...] = nv
            d_refs[i][...] = -ADAM_LR * ((nm / c1) / (jnp.sqrt(nv / c2) + ADAM_EPS) + ADAM_WD * w_refs[i][...])
        loss_ref[...] = total(SMALL_LOSS, LANE)

    vm = pl.BlockSpec(memory_space=pltpu.VMEM)
    shapes = [jax.ShapeDtypeStruct(w.shape, f32) for w in ws]
    res = _pcall(body, name="adamw_small", out_shape=shapes * 4 + [jax.ShapeDtypeStruct((1, LANE), f32)],
                 in_specs=[vm] * (1 + 3 * np_), out_specs=[vm] * (4 * np_ + 1))(parts, *ws, *ms, *vs)
    return [res[i * np_:(i + 1) * np_] for i in range(4)], res[4 * np_]


def _adamw_w_in(parts, w3, m3, v3):
    n_parts, n, _ = parts.shape
    c1 = 1.0 - ADAM_B1 ** ADAM_STEP
    c2 = 1.0 - ADAM_B2 ** ADAM_STEP

    def body(p_ref, w_ref, m_ref, v_ref, g_ref, d_ref, nm_ref, nv_ref):
        g = p_ref[0].astype(f32)
        for s in range(1, n_parts):
            g = g + p_ref[s].astype(f32)
        nm = ADAM_B1 * m_ref[:, 0, :] + (1.0 - ADAM_B1) * g
        nv = ADAM_B2 * v_ref[:, 0, :] + (1.0 - ADAM_B2) * (g * g)
        g_ref[:, 0, :] = g
        nm_ref[:, 0, :] = nm
        nv_ref[:, 0, :] = nv
        d_ref[:, 0, :] = -ADAM_LR * ((nm / c1) / (jnp.sqrt(nv / c2) + ADAM_EPS) + ADAM_WD * w_ref[:, 0, :])

    tile = 2 * COL_TILE
    blk = pl.BlockSpec((n, 1, tile), lambda j: (0, 0, j))
    out = jax.ShapeDtypeStruct((n, 1, D_MODEL), f32)
    return _pcall(
        body, name="adamw_w_in", grid=(D_MODEL // tile,),
        in_specs=[pl.BlockSpec((n_parts, n, tile), lambda j: (0, 0, j)), blk, blk, blk],
        out_specs=[blk] * 4, out_shape=[out] * 4,
        compiler_params=_cparams("parallel"),
    )(parts, w3, m3, v3)


def _pad_lanes(vec8, start):
    return jnp.pad(vec8.reshape(1, -1), ((0, 0), (start, LANE - start - vec8.size)))


def kernel(x, norm_in_w, w_in, conv_qkv_w, A_log, dt_bias, gdn_norm_w, conv_w, conv_b, w_out, final_norm_w, loss_target, m_norm_in_w, m_w_in, m_conv_qkv_w, m_A_log, m_dt_bias, m_gdn_norm_w, m_conv_w, m_conv_b, m_w_out, m_final_norm_w, v_norm_in_w, v_w_in, v_conv_qkv_w, v_A_log, v_dt_bias, v_gdn_norm_w, v_conv_w, v_conv_b, v_w_out, v_final_norm_w):
    L = x.shape[1]
    nc = L // CHUNK
    xs = x[0]
    tgt = loss_target[0]
    fnw = final_norm_w.reshape(1, D_MODEL)

    as_rows = lambda a: jnp.transpose(a, (2, 0, 1))
    win_g, cqkv_g, cw_g = _all_gather([_cast_w_in(as_rows(w_in)), conv_qkv_w[0], conv_w[0]], "gather_weights")
    wpad = _relayout_w_in(win_g)
    cqkv = jnp.concatenate([cqkv_g[d] for d in range(N_DEV)], axis=1)
    cw = jnp.concatenate([cw_g[d] for d in range(N_DEV)], axis=1)
    alog_p = _pad_lanes(A_log, HEADS)
    dtb_p = _pad_lanes(dt_bias, HEADS)
    me_flat, me_chip = _flat(*_mesh_pos()), 2 * lax.axis_index("x") + lax.axis_index("y")
    tok = lambda started: started[4][0:1, 0:1]
    wo_own = w_out[0].astype(bf16)
    wo_started = _spread_start(wo_own, wpad, "gather", "gather_w_out_start")

    proj, h = _in_proj(xs, norm_in_w + tok(wo_started), wpad)
    qkv = _qkv_act(proj, cqkv)
    sc, gr = _scalars(proj, alog_p, dtb_p)
    o, u_all, w_all, vn_all, t_all, sp_all = _gdn_fwd(qkv, sc, gr)
    mix_a = _gdn_gate(o, proj, gdn_norm_w)
    mix_b = _conv_fwd(proj, cw, conv_b)
    wo = _own_slot(_spread_wait(wo_started, mix_b, "gather", "gather_w_out_wait"), wo_own, me_flat).reshape(-1, D_MODEL)
    dy, dyb, dmix_a, dmix_b, g_fnw, loss_v = _out_proj_loss(xs, mix_a, mix_b, wo, fnw, tgt)

    g_wout = jnp.concatenate([_tn_matmul(mix_a, dyb, "grad_w_out_a"), _tn_matmul(mix_b, dyb, "grad_w_out_b")], axis=0)
    g_wout = g_wout.reshape(N_DEV, -1, D_MODEL)
    g_wout_own = lax.dynamic_index_in_dim(g_wout, me_flat, 0, keepdims=False)
    gwo_started = _spread_start(g_wout, dyb, "scatter", "exchange_grad_w_out_start")
    do, dzg, g_gnw = _gdn_gate_bwd(o, proj, gdn_norm_w + tok(gwo_started), dmix_a)
    d_b, d_c, d_hc, d_zc, g_cw, g_cb = _conv_bwd(proj, cw, conv_b, dmix_b)
    dqkv_n, dsc, dgr = _gdn_bwd(qkv, sc, gr, u_all, w_all, vn_all, t_all, sp_all, do)
    dqkv, g_cqkv = _qkv_bwd(proj, cqkv, dqkv_n)
    dgr_col = jnp.pad(dgr.transpose(0, 2, 1).reshape(L, HEADS), ((0, 0), (HEADS, LANE - 2 * HEADS)))
    dba, g_sc = _scalars_bwd(proj, alog_p, dtb_p, dsc, dgr_col)
    pieces = [dqkv, dzg, dba, d_b, d_c, d_hc, d_zc]
    offs = [OFF_QKV, OFF_ZG, OFF_BA, OFF_B, OFF_C, OFF_HC, OFF_ZC]
    g_parts = [_tn_matmul(p, h, "grad_w_in_%d" % i) for i, p in enumerate(pieces)]
    g_win_blk = _grad_blocks(g_parts)

    (p_win,) = _pair_exchange([g_win_blk], "exchange_grads_pair")
    s_win = _pair_sum(g_win_blk, p_win, "pair_sum_w_in")
    s_win_own = lax.dynamic_index_in_dim(s_win, me_chip, 0, keepdims=False)
    r_cqkv, r_cw = _all_to_all(
        [g_cqkv.reshape(4, N_DEV, -1).transpose(1, 0, 2), g_cw.reshape(3, N_DEV, -1).transpose(1, 0, 2)],
        "exchange_small_sharded_grads")
    gwi_started = _spread_start(s_win, r_cw, "chips", "exchange_grads_chips_start")
    grad_x, g_nw = _input_grad(pieces, offs, wpad, xs, norm_in_w + tok(gwi_started), dy)

    r_wout = _own_slot(_spread_wait(gwo_started, grad_x, "scatter", "exchange_grad_w_out_wait"), g_wout_own, me_flat)
    upd_wout =_adamw_reduce(r_wout, w_out[0], m_w_out[0], v_w_out[0], "adamw_w_out")
    upd_cqkv = _adamw_reduce(r_cqkv, conv_qkv_w[0], m_conv_qkv_w[0], v_conv_qkv_w[0], "adamw_conv_qkv_w")
    upd_cw = _adamw_reduce(r_cw, conv_w[0], m_conv_w[0], v_conv_w[0], "adamw_conv_w")

    r_win = _own_slot(_spread_wait(gwi_started, upd_cw[0], "chips", "exchange_grads_chips_wait"), s_win_own, me_chip)
    upd_win = [jnp.transpose(a, (1, 2, 0)) for a in _adamw_w_in(r_win, as_rows(w_in), as_rows(m_w_in), as_rows(v_w_in))]

    small_g = _pack_small([g_nw, g_cb, g_fnw, g_gnw, g_sc, loss_v], r_win)
    (small_all,) = _all_gather([small_g], "gather_small_grads")
    fvec = lambda a: a.reshape(1, D_MODEL)
    upd_small, loss_sum = _adamw_small(
        small_all,
        [norm_in_w, conv_b, fvec(final_norm_w), gdn_norm_w, A_log, dt_bias],
        [m_norm_in_w, m_conv_b, fvec(m_final_norm_w), m_gdn_norm_w, m_A_log, m_dt_bias],
        [v_norm_in_w, v_conv_b, fvec(v_final_norm_w), v_gdn_norm_w, v_A_log, v_dt_bias])

    outs = [loss_sum[0, 0], grad_x[None]]
    for k in range(4):
        nw_k, cb_k, fw_k, gn_k, al_k, dt_k = upd_small[k]
        outs += [nw_k, upd_win[k], upd_cqkv[k][None], al_k, dt_k, gn_k,
                 upd_cw[k][None], cb_k, upd_wout[k][None], fw_k.reshape(D_MODEL)]
    return tuple(outs)
```

```python
import functools
import math

import jax
import jax.numpy as jnp
from jax import lax
from jax.experimental import pallas as pl
from jax.experimental.pallas import tpu as pltpu

f32 = jnp.float32
bf16 = jnp.bfloat16

N_DEV = 8
D_MODEL = 1024
HEADS = 8
HEAD_DIM = 128
CHUNK = 64
GDN_CPS = 4
GDN_CPS_BWD = 1
GDN_WIDTH = HEADS * HEAD_DIM
CONV_WIDTH = 1024
PROJ_WIDTH = 8208
SHARD_W = PROJ_WIDTH // N_DEV
EPS = 1e-6

NAT_SMALL_END = 4112
PAD_COLS = 240
OFF_QKV, OFF_ZG, OFF_BA, OFF_B, OFF_C, OFF_HC, OFF_ZC = 0, 3072, 4096, 4352, 5376, 6400, 7424
PROJ_PAD = 8448
LANE = 128
ELT_W = 256

ADAM_LR, ADAM_B1, ADAM_B2, ADAM_EPS, ADAM_WD, ADAM_STEP = 0.001, 0.9, 0.999, 1e-08, 0.01, 10

VMEM_LIMIT = 56 * 1024 * 1024

MESH = pl.DeviceIdType.MESH
ANY = pl.BlockSpec(memory_space=pl.ANY)


def _pcall(body, **kw):
    return pl.pallas_call(body, **kw)


def _cparams(*sem):
    return pltpu.CompilerParams(dimension_semantics=sem if sem else None, vmem_limit_bytes=VMEM_LIMIT)


def _mm(a, b):
    return jnp.dot(a.astype(bf16), b.astype(bf16), preferred_element_type=f32)


def _mm_nt(a, b):
    return lax.dot_general(a.astype(bf16), b.astype(bf16), (((1,), (1,)), ((), ())), preferred_element_type=f32)


def _mm_tn(a, b):
    return lax.dot_general(a.astype(bf16), b.astype(bf16), (((0,), (0,)), ((), ())), preferred_element_type=f32)


def _rows(shape):
    return lax.broadcasted_iota(jnp.int32, shape, 0)


def _lanes(shape):
    return lax.broadcasted_iota(jnp.int32, shape, 1)


def _shift_down(x, s):
    if s == 0:
        return x
    return jnp.where(_rows(x.shape) >= s, pltpu.roll(x, s, 0), 0.0)


def _shift_up(x, s):
    if s == 0:
        return x
    n = x.shape[0]
    return jnp.where(_rows(x.shape) < n - s, pltpu.roll(x, n - s, 0), 0.0)


def _sigmoid(x):
    return jax.nn.sigmoid(x)


def _softplus(x):
    e = jnp.exp(-jnp.abs(x))
    small = e * (1.0 - e * (0.5 - e * (1.0 / 3.0)))
    return jnp.maximum(x, 0.0) + jnp.where(e < 0.01, small, jnp.log(1.0 + e))


def _mesh_pos():
    return lax.axis_index("x"), lax.axis_index("y"), lax.axis_index("c")


def _flat(px, py, pc):
    return 4 * px + 2 * py + pc


def _all_gather(xs, name, pieces=None):
    n = len(xs)
    pieces = pieces or [1] * n
    items = [(a, q) for a in range(n) for q in range(pieces[a])]
    ni = len(items)

    def view(ref, i):
        a, q = items[i]
        if pieces[a] == 1:
            return ref
        wd = xs[a].shape[-1] // pieces[a]
        return ref.at[(slice(None),) * (xs[a].ndim - 1) + (pl.ds(q * wd, wd),)]

    def body(*refs):
        x_refs, o_refs = refs[:n], refs[n:2 * n]
        send_sems, recv_sems, local_sems = refs[2 * n:]
        x, y, c = _mesh_pos()
        me, sibling = (x, y, c), (x, y, 1 - c)
        flip = lambda v, bit: v + bit - 2 * v * bit
        nbr_a = (flip(x, 1 - c), flip(y, c))
        nbr_b = (flip(x, c), flip(y, 1 - c))
        diag = (1 - x, 1 - y)

        def copy(i, k, block, to, own=False):
            a = items[i][0]
            dst = view(o_refs[a].at[_flat(*block)], i)
            return pltpu.make_async_remote_copy(
                src_ref=view(x_refs[a], i) if own else dst, dst_ref=dst,
                send_sem=send_sems.at[i, k], recv_sem=recv_sems.at[i, k], device_id=to, device_id_type=MESH)

        mine, sent = [], []

        def go(cp):
            cp.start()
            sent.append(cp)

        for a in range(n):
            cp = pltpu.make_async_copy(x_refs[a], o_refs[a].at[_flat(*me)], local_sems.at[a])
            cp.start()
            mine.append(cp)
        for a in range(ni):
            go(copy(a, 1, me, (*nbr_a, c), own=True))
            go(copy(a, 2, me, (*nbr_b, c), own=True))
            go(copy(a, 0, me, sibling, own=True))
        for a in range(ni):
            copy(a, 1, (*nbr_a, c), me).wait_recv()
            go(copy(a, 3, (*nbr_a, c), (*nbr_b, c)))
            go(copy(a, 4, (*nbr_a, c), sibling))
        for a in range(ni):
            copy(a, 2, (*nbr_b, c), me).wait_recv()
            go(copy(a, 5, (*nbr_b, c), sibling))
        for a in range(ni):
            copy(a, 3, (*diag, c), me).wait_recv()
            go(copy(a, 6, (*diag, c), sibling))
        for a in range(ni):
            copy(a, 0, sibling, me).wait_recv()
            copy(a, 4, (*nbr_b, 1 - c), me).wait_recv()
            copy(a, 5, (*nbr_a, 1 - c), me).wait_recv()
            copy(a, 6, (*diag, 1 - c), me).wait_recv()
        for cp in sent:
            cp.wait_send()
        for cp in mine:
            cp.wait()

    outs = _pcall(
        body, name=name,
        out_shape=[jax.ShapeDtypeStruct((N_DEV,) + a.shape, a.dtype) for a in xs],
        in_specs=[ANY] * n, out_specs=[ANY] * n,
        scratch_shapes=[pltpu.SemaphoreType.DMA((ni, 7)), pltpu.SemaphoreType.DMA((ni, 7)), pltpu.SemaphoreType.DMA((n,))],
    )(*xs)
    return list(outs)


def _all_to_all(gs, name):
    n = len(gs)

    def body(*refs):
        g_refs, o_refs = refs[:n], refs[n:2 * n]
        send_sems, recv_sems, local_sems = refs[2 * n:]
        x, y, c = _mesh_pos()
        me = _flat(x, y, c)
        peers = []
        for k in range(1, N_DEV):
            kx, ky, kc = (k >> 2) & 1, (k >> 1) & 1, k & 1
            px = (1 - x) if kx else x
            py = (1 - y) if ky else y
            pc = (1 - c) if kc else c
            peers.append((px, py, pc))

        def copy(a, k):
            peer = peers[k - 1]
            return pltpu.make_async_remote_copy(
                src_ref=g_refs[a].at[_flat(*peer)], dst_ref=o_refs[a].at[me],
                send_sem=send_sems.at[a, k - 1], recv_sem=recv_sems.at[a, k - 1], device_id=peer, device_id_type=MESH)

        def arrival(a, k):
            peer = peers[k - 1]
            return pltpu.make_async_remote_copy(
                src_ref=g_refs[a].at[me], dst_ref=o_refs[a].at[_flat(*peer)],
                send_sem=send_sems.at[a, k - 1], recv_sem=recv_sems.at[a, k - 1], device_id=peer, device_id_type=MESH)

        mine, sent = [], []
        for a in range(n):
            cp = pltpu.make_async_copy(g_refs[a].at[me], o_refs[a].at[me], local_sems.at[a])
            cp.start()
            mine.append(cp)
            for k in range(1, N_DEV):
                cp = copy(a, k)
                cp.start()
                sent.append(cp)
        for a in range(n):
            for k in range(1, N_DEV):
                arrival(a, k).wait_recv()
        for cp in sent:
            cp.wait_send()
        for cp in mine:
            cp.wait()

    outs = _pcall(
        body, name=name,
        out_shape=[jax.ShapeDtypeStruct(a.shape, a.dtype) for a in gs],
        in_specs=[ANY] * n, out_specs=[ANY] * n,
        scratch_shapes=[pltpu.SemaphoreType.DMA((n, 7)), pltpu.SemaphoreType.DMA((n, 7)), pltpu.SemaphoreType.DMA((n,))],
    )(*gs)
    return list(outs)


def _pair_exchange(gs, name):
    n = len(gs)
    chips = [(0, 0), (0, 1), (1, 0), (1, 1)]

    def body(*refs):
        g_refs, o_refs = refs[:n], refs[n:2 * n]
        send_sems, recv_sems = refs[2 * n:]
        x, y, c = _mesh_pos()
        sibling = (x, y, 1 - c)

        def copy(a, i):
            xp, yp = chips[i]
            return pltpu.make_async_remote_copy(
                src_ref=g_refs[a].at[_flat(xp, yp, 1 - c)], dst_ref=o_refs[a].at[i],
                send_sem=send_sems.at[a, i], recv_sem=recv_sems.at[a, i], device_id=sibling, device_id_type=MESH)

        cps = [copy(a, i) for a in range(n) for i in range(4)]
        for cp in cps:
            cp.start()
        for cp in cps:
            cp.wait()

    outs = _pcall(
        body, name=name,
        out_shape=[jax.ShapeDtypeStruct((4,) + a.shape[1:], a.dtype) for a in gs],
        in_specs=[ANY] * n, out_specs=[ANY] * n,
        scratch_shapes=[pltpu.SemaphoreType.DMA((n, 4)), pltpu.SemaphoreType.DMA((n, 4))],
    )(*gs)
    return list(outs)


def _pair_sum(g, p1, name):
    _, R, C = g.shape
    tr = 256 if R % 256 == 0 else R
    cidx = lax.axis_index("c").astype(jnp.int32).reshape(1)

    def body(c_ref, g_ref, p_ref, o_ref):
        o_ref[...] = (g_ref[...].astype(f32) + p_ref[...].astype(f32)).astype(o_ref.dtype)

    return _pcall(
        body, name=name,
        grid_spec=pltpu.PrefetchScalarGridSpec(
            num_scalar_prefetch=1, grid=(4, R // tr),
            in_specs=[pl.BlockSpec((1, tr, C), lambda i, r, c_ref: (2 * i + c_ref[0], r, 0)),
                      pl.BlockSpec((1, tr, C), lambda i, r, c_ref: (i, r, 0))],
            out_specs=pl.BlockSpec((1, tr, C), lambda i, r, c_ref: (i, r, 0))),
        out_shape=jax.ShapeDtypeStruct((4, R, C), g.dtype),
        compiler_params=_cparams("parallel", "parallel"),
    )(cidx, g, p1)


HBM = pl.BlockSpec(memory_space=pltpu.HBM)
SEM = pl.BlockSpec(memory_space=pltpu.SEMAPHORE)
EFFECT = pltpu.SideEffectType.DATAFLOW_SIDE_EFFECTING


def _peers(x, y, c):
    out = []
    for k in range(1, N_DEV):
        kx, ky, kc = (k >> 2) & 1, (k >> 1) & 1, k & 1
        out.append(((1 - x) if kx else x, (1 - y) if ky else y, (1 - c) if kc else c))
    return out


SPREAD_COPIES = {"gather": N_DEV - 1, "scatter": N_DEV - 1, "chips": 3}


def _spread_copy(src_ref, land_ref, send_sems, recv_sems, k, plan):
    x, y, c = _mesh_pos()
    if plan == "chips":
        px, py = [(1 - x, y), (x, 1 - y), (1 - x, 1 - y)][k]
        peer, src, slot = (px, py, c), src_ref.at[2 * px + py], 2 * x + y
    else:
        peer = _peers(x, y, c)[k]
        src, slot = (src_ref.at[_flat(*peer)] if plan == "scatter" else src_ref), _flat(x, y, c)
    return pltpu.make_async_remote_copy(
        src_ref=src, dst_ref=land_ref.at[slot], send_sem=send_sems.at[k], recv_sem=recv_sems.at[k],
        device_id=peer, device_id_type=MESH)


def _spread_start(src, after, plan, name):
    land_shape = (N_DEV,) + src.shape if plan == "gather" else src.shape
    n_copies = SPREAD_COPIES[plan]

    def body(src_ref, land_ref, after_ref, send_sems, recv_sems, src_thru, land_thru, token):
        for k in range(n_copies):
            _spread_copy(src_ref, land_ref, send_sems, recv_sems, k, plan).start()
        token[...] = jnp.zeros_like(token)

    return _pcall(
        body, name=name,
        out_shape=(pltpu.SemaphoreType.DMA((n_copies,)), pltpu.SemaphoreType.DMA((n_copies,)),
                   pltpu.HBM(src.shape, src.dtype), pltpu.HBM(land_shape, src.dtype), jax.ShapeDtypeStruct((8, LANE), f32)),
        in_specs=(HBM, HBM, ANY), out_specs=(SEM, SEM, HBM, HBM, pl.BlockSpec(memory_space=pltpu.VMEM)),
        input_output_aliases={0: 2, 1: 3},
        compiler_params=pltpu.CompilerParams(has_side_effects=EFFECT),
    )(pltpu.with_memory_space_constraint(src, pltpu.HBM),
      pltpu.with_memory_space_constraint(lax.empty(land_shape, src.dtype), pltpu.HBM), after)


def _spread_wait(started, after, plan, name):
    send_sems, recv_sems, src_thru, land_thru, _ = started

    def body(src_ref, land_ref, send_sems, recv_sems, after_ref, src_dead, got_ref):
        for k in range(SPREAD_COPIES[plan]):
            cp = _spread_copy(src_ref, land_ref, send_sems, recv_sems, k, plan)
            cp.wait_send()
            cp.wait_recv()

    return _pcall(
        body, name=name,
        out_shape=(pltpu.HBM(src_thru.shape, src_thru.dtype), pltpu.HBM(land_thru.shape, land_thru.dtype)),
        in_specs=(HBM, HBM, SEM, SEM, ANY), out_specs=(HBM, HBM), input_output_aliases={0: 0, 1: 1},
        compiler_params=pltpu.CompilerParams(has_side_effects=EFFECT),
    )(src_thru, land_thru, send_sems, recv_sems, after)[1]


def _own_slot(land, block, slot):
    zero = jnp.zeros((), jnp.int32)
    return lax.dynamic_update_slice(land, block[None], (slot.astype(jnp.int32),) + (zero,) * block.ndim)


PIECE_NAT = (0, 3072, 4096, 4112, 5136, 6160, 7184, PROJ_WIDTH)


COL_TILE = 256


def _cast_w_in(w3):
    n = w3.shape[0]

    def body(w_ref, o_ref):
        o_ref[...] = w_ref[:, 0, :].astype(bf16)

    return _pcall(
        body, name="cast_w_in", grid=(D_MODEL // COL_TILE,),
        in_specs=[pl.BlockSpec((n, 1, COL_TILE), lambda j: (0, 0, j))],
        out_specs=pl.BlockSpec((n, COL_TILE), lambda j: (0, j)),
        out_shape=jax.ShapeDtypeStruct((n, D_MODEL), bf16),
        compiler_params=_cparams("parallel"),
    )(w3)


def _relayout_w_in(win_g):
    def body(g_ref, o_ref):
        o_ref[NAT_SMALL_END:NAT_SMALL_END + PAD_COLS, :] = jnp.zeros((PAD_COLS, COL_TILE), o_ref.dtype)
        for d in range(N_DEV):
            n0, n1 = d * SHARD_W, (d + 1) * SHARD_W
            cut = min(max(NAT_SMALL_END - n0, 0), SHARD_W)
            if cut > 0:
                o_ref[n0:n0 + cut, :] = g_ref[d, 0:cut, :]
            if cut < SHARD_W:
                o_ref[n0 + cut + PAD_COLS:n1 + PAD_COLS, :] = g_ref[d, cut:SHARD_W, :]

    return _pcall(
        body, name="relayout_w_in", grid=(D_MODEL // COL_TILE,),
        in_specs=[pl.BlockSpec((N_DEV, SHARD_W, COL_TILE), lambda j: (0, 0, j))],
        out_specs=pl.BlockSpec((PROJ_PAD, COL_TILE), lambda j: (0, j)),
        out_shape=jax.ShapeDtypeStruct((PROJ_PAD, D_MODEL), win_g.dtype),
        compiler_params=_cparams("parallel"),
    )(win_g)


def _grad_blocks(g_parts):
    npc = len(g_parts)

    def body(*refs):
        p_refs, o_ref = refs[:npc], refs[npc]
        for d in range(N_DEV):
            n0, n1 = d * SHARD_W, (d + 1) * SHARD_W
            for i in range(npc):
                lo, hi = max(n0, PIECE_NAT[i]), min(n1, PIECE_NAT[i + 1])
                if lo < hi:
                    o_ref[d, lo - n0:hi - n0, :] = p_refs[i][lo - PIECE_NAT[i]:hi - PIECE_NAT[i], :]

    return _pcall(
        body, name="grad_blocks", grid=(D_MODEL // COL_TILE,),
        in_specs=[pl.BlockSpec((p.shape[0], COL_TILE), lambda j: (0, j)) for p in g_parts],
        out_specs=pl.BlockSpec((N_DEV, SHARD_W, COL_TILE), lambda j: (0, 0, j)),
        out_shape=jax.ShapeDtypeStruct((N_DEV, SHARD_W, D_MODEL), bf16),
        compiler_params=_cparams("parallel"),
    )(*g_parts)


def _in_proj(x, nw, wpad_t):
    L = x.shape[0]
    tn = 768
    nj = wpad_t.shape[0] // tn

    def body(x_ref, nw_ref, w_ref, proj_ref, h_ref):
        @pl.when(pl.program_id(0) == 0)
        def _():
            for r in range(0, L, 256):
                xs = x_ref[r:r + 256, :]
                ms = jnp.mean(xs * xs, axis=-1, keepdims=True)
                h_ref[r:r + 256, :] = ((xs * lax.rsqrt(ms + EPS)) * nw_ref[...]).astype(bf16)
        for r in range(0, L, 512):
            proj_ref[r:r + 512, :] = lax.dot_general(h_ref[r:r + 512, :], w_ref[...], (((1,), (1,)), ((), ())),
                                                     preferred_element_type=f32)

    return _pcall(
        body, name="in_proj", grid=(nj,),
        in_specs=[pl.BlockSpec((L, D_MODEL), lambda j: (0, 0)), pl.BlockSpec((1, D_MODEL), lambda j: (0, 0)),
                  pl.BlockSpec((tn, D_MODEL), lambda j: (j, 0))],
        out_specs=[pl.BlockSpec((L, tn), lambda j: (0, j)), pl.BlockSpec((L, D_MODEL), lambda j: (0, 0))],
        out_shape=[jax.ShapeDtypeStruct((L, wpad_t.shape[0]), f32), jax.ShapeDtypeStruct((L, D_MODEL), bf16)],
        compiler_params=_cparams("arbitrary"),
    )(x, nw, wpad_t)


HALVES = [slice(i * LANE, (i + 1) * LANE) for i in range(ELT_W // LANE)]
STEPS_PER_GROUP = GDN_WIDTH // ELT_W


def _conv4(x, cw_ref, ls):
    return (cw_ref[3:4, ls] * x + cw_ref[2:3, ls] * _shift_down(x, 1) + cw_ref[1:2, ls] * _shift_down(x, 2)
            + cw_ref[0:1, ls] * _shift_down(x, 3))


def _qkv_act(proj, cw):
    L = proj.shape[0]

    def body(x_ref, cw_ref, o_ref):
        j = pl.program_id(0)
        scale = jnp.where(j < STEPS_PER_GROUP, HEAD_DIM ** -0.5, 1.0).astype(f32)
        for ls in HALVES:
            c = _conv4(x_ref[:, ls], cw_ref, ls)
            a = c * _sigmoid(c)
            rn = lax.rsqrt(jnp.sum(a * a, axis=1, keepdims=True) + EPS)
            o_ref[:, ls] = jnp.where(j < 2 * STEPS_PER_GROUP, (a * rn) * scale, a)

    return _pcall(
        body, name="qkv_act", grid=(3 * STEPS_PER_GROUP,),
        in_specs=[pl.BlockSpec((L, ELT_W), lambda j: (0, j)), pl.BlockSpec((4, ELT_W), lambda j: (0, j))],
        out_specs=pl.BlockSpec((L, ELT_W), lambda j: (0, j)),
        out_shape=jax.ShapeDtypeStruct((L, 3 * GDN_WIDTH), f32),
        compiler_params=_cparams("parallel"),
    )(proj, cw)


def _scalars(proj, alog_p, dtb_p):
    L = proj.shape[0]
    nc = L // CHUNK

    def body(x_ref, al_ref, dt_ref, sc_ref, gr_ref):
        x = x_ref[...]
        lane = _lanes(x.shape)
        beta = _sigmoid(x)
        g = -jnp.exp(al_ref[...]) * _softplus(x + dt_ref[...])
        gc = jnp.where((lane >= HEADS) & (lane < 2 * HEADS), g, 0.0)
        rc = _rows(x.shape) & (CHUNK - 1)
        for s in (1, 2, 4, 8, 16, 32):
            gc = gc + jnp.where(rc >= s, pltpu.roll(gc, s, 0), 0.0)
        sc_ref[...] = jnp.where(lane < HEADS, beta, gc)
        sel = (_lanes((HEADS, LANE)) == _rows((HEADS, LANE)) + HEADS).astype(f32)
        for c in range(nc):
            gr_ref[c] = lax.dot_general(sel, sc_ref[c * CHUNK:(c + 1) * CHUNK, :], (((1,), (1,)), ((), ())),
                                        preferred_element_type=f32, precision=lax.Precision.HIGHEST)

    return _pcall(
        body, name="scalars", grid=(1,),
        in_specs=[pl.BlockSpec((L, LANE), lambda i: (0, OFF_BA // LANE)), pl.BlockSpec((1, LANE), lambda i: (0, 0)),
                  pl.BlockSpec((1, LANE), lambda i: (0, 0))],
        out_specs=[pl.BlockSpec((L, LANE), lambda i: (0, 0)), pl.BlockSpec((nc, HEADS, CHUNK), lambda i: (0, 0, 0))],
        out_shape=[jax.ShapeDtypeStruct((L, LANE), f32), jax.ShapeDtypeStruct((nc, HEADS, CHUNK), f32)],
        compiler_params=_cparams("arbitrary"),
    )(proj, alog_p, dtb_p)


def _head_scalars(sc, gr_ref, h, ci=0):
    lane = _lanes(sc.shape)
    beta = jnp.sum(jnp.where(lane == h, sc, 0.0), axis=1, keepdims=True)
    gcc = jnp.sum(jnp.where(lane == HEADS + h, sc, 0.0), axis=1, keepdims=True)
    gcr = gr_ref[ci, h:h + 1, :]
    gl = jnp.sum(jnp.where(_lanes(gcr.shape) == CHUNK - 1, gcr, 0.0), axis=1, keepdims=True)
    ii, jj = _rows((CHUNK, CHUNK)), _lanes((CHUNK, CHUNK))
    dmat = jnp.where(ii >= jj, jnp.exp(jnp.minimum(gcc - gcr, 0.0)), 0.0)
    dmat_t = jnp.where(jj >= ii, jnp.exp(jnp.minimum(gcr - gcc, 0.0)), 0.0)
    return beta, gcc, gl, dmat, dmat_t, ii, jj


def _gdn_fwd(qkv, sc, gr):
    L = qkv.shape[0]
    nc = L // CHUNK
    W = GDN_WIDTH
    cps = GDN_CPS if nc % GDN_CPS == 0 else 1
    rows_per_step = cps * CHUNK

    def body(qkv_ref, sc_ref, gr_ref, o_ref, u_ref, w_ref, vn_ref, t_ref, sp_ref, s_scr):
        @pl.when(pl.program_id(0) == 0)
        def _():
            s_scr[...] = jnp.zeros_like(s_scr)
        HS = range(cps * HEADS)
        hd = [i % HEADS for i in HS]
        rs = [slice((i // HEADS) * CHUNK, (i // HEADS + 1) * CHUNK) for i in HS]
        cs = [slice(hd[i] * HEAD_DIM, (hd[i] + 1) * HEAD_DIM) for i in HS]
        q = [qkv_ref[rs[i], hd[i] * HEAD_DIM:(hd[i] + 1) * HEAD_DIM] for i in HS]
        k = [qkv_ref[rs[i], W + hd[i] * HEAD_DIM:W + (hd[i] + 1) * HEAD_DIM] for i in HS]
        v = [qkv_ref[rs[i], 2 * W + hd[i] * HEAD_DIM:2 * W + (hd[i] + 1) * HEAD_DIM] for i in HS]
        hsc = [_head_scalars(sc_ref[rs[i], :], gr_ref, hd[i], i // HEADS) for i in HS]
        beta, gcc, gl, dmat = ([x[i] for x in hsc] for i in range(4))
        ii, jj = hsc[0][5], hsc[0][6]
        eg = [jnp.exp(gcc[h]) for h in HS]
        kb = [k[h] * beta[h] for h in HS]
        kk = [_mm_nt(kb[h], k[h]) for h in HS]
        qk = [_mm_nt(q[h], k[h]) for h in HS]
        n0 = [-jnp.where(ii > jj, kk[h] * dmat[h], 0.0) for h in HS]
        n1 = [_mm(n0[h], n0[h]) for h in HS]
        n2 = [_mm(n1[h], n1[h]) for h in HS]
        p01 = [n0[h] + n1[h] + _mm(n0[h], n1[h]) for h in HS]
        n3 = [_mm(n2[h], n2[h]) for h in HS]
        n4 = [_mm(n3[h], n3[h]) for h in HS]
        p23 = [n2[h] + n3[h] + _mm(n2[h], n3[h]) for h in HS]
        n5 = [_mm(n4[h], n4[h]) for h in HS]
        p03 = [p01[h] + p23[h] + _mm(p01[h], p23[h]) for h in HS]
        p45 = [n4[h] + n5[h] + _mm(n4[h], n5[h]) for h in HS]
        t = [p03[h] + p45[h] + _mm(p03[h], p45[h]) for h in HS]
        vb = [v[h] * beta[h] for h in HS]
        kbg = [kb[h] * eg[h] for h in HS]
        uw = [_mm(t[h], jnp.concatenate([vb[h], kbg[h]], axis=1)) for h in HS]
        u = [vb[h] + uw[h][:, :HEAD_DIM] for h in HS]
        w = [kbg[h] + uw[h][:, HEAD_DIM:] for h in HS]
        wq = [jnp.concatenate([w[h], q[h] * eg[h]], axis=0) for h in HS]
        p = [jnp.where(ii >= jj, qk[h] * dmat[h], 0.0) for h in HS]
        ks = [k[h] * jnp.exp(gl[h] - gcc[h]) for h in HS]
        s = [s_scr[h] for h in range(HEADS)]
        for ci in range(cps):
            IS = range(ci * HEADS, (ci + 1) * HEADS)
            ws = [_mm(wq[i], s[hd[i]]) for i in IS]
            vn = [u[i] - ws[hd[i]][:CHUNK] for i in IS]
            pv = [_mm(p[i], vn[hd[i]]) for i in IS]
            kv = [_mm_tn(ks[i], vn[hd[i]]) for i in IS]
            for i in IS:
                h = hd[i]
                sp_ref[ci, cs[i], :] = s[h]
                o_ref[rs[i], cs[i]] = ws[h][CHUNK:] + pv[h]
                vn_ref[rs[i], cs[i]] = vn[h].astype(bf16)
            s = [jnp.exp(gl[i]) * s[hd[i]] + kv[hd[i]] for i in IS]
        for h in range(HEADS):
            s_scr[h] = s[h]
        for i in HS:
            u_ref[rs[i], cs[i]] = u[i].astype(bf16)
            w_ref[rs[i], cs[i]] = w[i].astype(bf16)
            t_ref[i // HEADS, hd[i]] = t[i].astype(bf16)

    row = lambda c: (c, 0)
    act, act16 = jax.ShapeDtypeStruct((L, W), f32), jax.ShapeDtypeStruct((L, W), bf16)
    return _pcall(
        body, name="gdn_fwd", grid=(nc // cps,),
        in_specs=[pl.BlockSpec((rows_per_step, 3 * W), row), pl.BlockSpec((rows_per_step, LANE), row),
                  pl.BlockSpec((cps, HEADS, CHUNK), lambda c: (c, 0, 0))],
        out_specs=[pl.BlockSpec((rows_per_step, W), row)] * 4 + [
            pl.BlockSpec((cps, HEADS, CHUNK, CHUNK), lambda c: (c, 0, 0, 0)),
            pl.BlockSpec((cps, W, HEAD_DIM), lambda c: (c, 0, 0))],
        out_shape=[act, act16, act16, act16, jax.ShapeDtypeStruct((nc, HEADS, CHUNK, CHUNK), bf16),
                   jax.ShapeDtypeStruct((nc, W, HEAD_DIM), f32)],
        scratch_shapes=[pltpu.VMEM((HEADS, HEAD_DIM, HEAD_DIM), f32)],
        compiler_params=_cparams("arbitrary"),
    )(qkv, sc, gr)


def _gdn_gate(o, proj, gnw):
    L = o.shape[0]

    def body(o_ref, z_ref, w_ref, m_ref):
        for ls in HALVES:
            ov, z = o_ref[:, ls], z_ref[:, ls]
            rms = lax.rsqrt(jnp.mean(ov * ov, axis=-1, keepdims=True) + EPS)
            m_ref[:, ls] = (((ov * rms) * w_ref[...]) * (z * _sigmoid(z))).astype(bf16)

    return _pcall(
        body, name="gdn_gate", grid=(GDN_WIDTH // ELT_W,),
        in_specs=[pl.BlockSpec((L, ELT_W), lambda j: (0, j)), pl.BlockSpec((L, ELT_W), lambda j: (0, OFF_ZG // ELT_W + j)),
                  pl.BlockSpec((1, LANE), lambda j: (0, 0))],
        out_specs=pl.BlockSpec((L, ELT_W), lambda j: (0, j)),
        out_shape=jax.ShapeDtypeStruct((L, GDN_WIDTH), bf16),
        compiler_params=_cparams("parallel"),
    )(o, proj, gnw)


def _conv3(u, cw_ref, ls):
    return cw_ref[2:3, ls] * u + cw_ref[1:2, ls] * _shift_down(u, 1) + cw_ref[0:1, ls] * _shift_down(u, 2)


def _conv_specs(L):
    blk = lambda off: pl.BlockSpec((L, ELT_W), lambda j, off=off: (0, off // ELT_W + j))
    return [blk(OFF_B), blk(OFF_C), blk(OFF_HC), blk(OFF_ZC),
            pl.BlockSpec((3, ELT_W), lambda j: (0, j)), pl.BlockSpec((1, ELT_W), lambda j: (0, j))]


def _conv_fwd(proj, cw, cb):
    L = proj.shape[0]

    def body(b_ref, c_ref, h_ref, z_ref, cw_ref, cb_ref, m_ref):
        for ls in HALVES:
            z = z_ref[:, ls]
            cv = _conv3(c_ref[:, ls] * h_ref[:, ls], cw_ref, ls) + cb_ref[:, ls]
            m_ref[:, ls] = ((b_ref[:, ls] * cv) * (z * _sigmoid(z))).astype(bf16)

    return _pcall(
        body, name="conv_fwd", grid=(CONV_WIDTH // ELT_W,),
        in_specs=_conv_specs(L), out_specs=pl.BlockSpec((L, ELT_W), lambda j: (0, j)),
        out_shape=jax.ShapeDtypeStruct((L, CONV_WIDTH), bf16),
        compiler_params=_cparams("parallel"),
    )(proj, proj, proj, proj, cw, cb)


def _out_proj_loss(x, mix_a, mix_b, wo, fw, tgt):
    L = x.shape[0]
    tm = min(512, L)

    def body(x_ref, ma_ref, mb_ref, wo_ref, fw_ref, t_ref, dy_ref, dyb_ref, dma_ref, dmb_ref, gfw_ref, loss_ref):
        @pl.when(pl.program_id(0) == 0)
        def _():
            gfw_ref[...] = jnp.zeros_like(gfw_ref)
            loss_ref[...] = jnp.zeros_like(loss_ref)
        y = x_ref[...] + jnp.dot(ma_ref[...], wo_ref[:GDN_WIDTH, :], preferred_element_type=f32) \
            + jnp.dot(mb_ref[...], wo_ref[GDN_WIDTH:, :], preferred_element_type=f32)
        r = lax.rsqrt(jnp.mean(y * y, axis=-1, keepdims=True) + EPS)
        yh = y * r
        fwv = fw_ref[...]
        diff = yh * fwv - t_ref[...]
        loss_ref[...] += jnp.sum(jnp.sum(diff * diff, axis=-1, keepdims=True), axis=0, keepdims=True) * (0.5 / D_MODEL)
        dout = diff * (1.0 / D_MODEL)
        gfw_ref[...] += jnp.sum(dout * yh, axis=0, keepdims=True)
        dyh = dout * fwv
        dy = r * (dyh - yh * jnp.mean(dyh * yh, axis=-1, keepdims=True))
        dy_ref[...] = dy
        dyb = dy.astype(bf16)
        dyb_ref[...] = dyb
        dma_ref[...] = lax.dot_general(dyb, wo_ref[:GDN_WIDTH, :], (((1,), (1,)), ((), ())), preferred_element_type=f32)
        dmb_ref[...] = lax.dot_general(dyb, wo_ref[GDN_WIDTH:, :], (((1,), (1,)), ((), ())), preferred_element_type=f32)

    row = lambda i: (i, 0)
    fix = lambda i: (0, 0)
    act = jax.ShapeDtypeStruct((L, D_MODEL), f32)
    return _pcall(
        body, name="out_proj_loss", grid=(L // tm,),
        in_specs=[pl.BlockSpec((tm, D_MODEL), row), pl.BlockSpec((tm, GDN_WIDTH), row), pl.BlockSpec((tm, CONV_WIDTH), row),
                  pl.BlockSpec((GDN_WIDTH + CONV_WIDTH, D_MODEL), fix), pl.BlockSpec((1, D_MODEL), fix),
                  pl.BlockSpec((tm, D_MODEL), row)],
        out_specs=[pl.BlockSpec((tm, D_MODEL), row), pl.BlockSpec((tm, D_MODEL), row), pl.BlockSpec((tm, GDN_WIDTH), row),
                   pl.BlockSpec((tm, CONV_WIDTH), row), pl.BlockSpec((1, D_MODEL), fix), pl.BlockSpec((1, LANE), fix)],
        out_shape=[act, jax.ShapeDtypeStruct((L, D_MODEL), bf16), act, act,
                   jax.ShapeDtypeStruct((1, D_MODEL), f32), jax.ShapeDtypeStruct((1, LANE), f32)],
        compiler_params=_cparams("arbitrary"),
    )(x, mix_a, mix_b, wo, fw, tgt)


def _tn_matmul(a, b, name):
    L, M = a.shape
    N = b.shape[1]
    tm = 512 if M % 512 == 0 else M

    def body(a_ref, b_ref, o_ref):
        o_ref[...] = lax.dot_general(a_ref[...], b_ref[...], (((0,), (0,)), ((), ())),
                                     preferred_element_type=f32).astype(o_ref.dtype)

    return _pcall(
        body, name=name, grid=(M // tm,),
        in_specs=[pl.BlockSpec((L, tm), lambda i: (0, i)), pl.BlockSpec((L, N), lambda i: (0, 0))],
        out_specs=pl.BlockSpec((tm, N), lambda i: (i, 0)),
        out_shape=jax.ShapeDtypeStruct((M, N), bf16),
        compiler_params=_cparams("parallel"),
    )(a, b)


def _gdn_gate_bwd(o, proj, gnw, dmix_a):
    L = o.shape[0]

    def body(o_ref, z_ref, w_ref, dm_ref, do_ref, dz_ref, gw_ref):
        @pl.when(pl.program_id(0) == 0)
        def _():
            gw_ref[...] = jnp.zeros_like(gw_ref)
        wv = w_ref[...]
        for ls in HALVES:
            ov, z, dm = o_ref[:, ls], z_ref[:, ls], dm_ref[:, ls]
            rms = lax.rsqrt(jnp.mean(ov * ov, axis=-1, keepdims=True) + EPS)
            xh = ov * rms
            sg = _sigmoid(z)
            d_on = dm * (z * sg)
            dz_ref[:, ls] = (dm * (xh * wv) * (sg * (1.0 + z * (1.0 - sg)))).astype(bf16)
            gw_ref[...] += jnp.sum(d_on * xh, axis=0, keepdims=True)
            dxh = d_on * wv
            do_ref[:, ls] = (rms * (dxh - xh * jnp.mean(dxh * xh, axis=-1, keepdims=True))).astype(bf16)

    wide = pl.BlockSpec((L, ELT_W), lambda j: (0, j))
    return _pcall(
        body, name="gdn_gate_bwd", grid=(GDN_WIDTH // ELT_W,),
        in_specs=[wide, pl.BlockSpec((L, ELT_W), lambda j: (0, OFF_ZG // ELT_W + j)),
                  pl.BlockSpec((1, LANE), lambda j: (0, 0)), wide],
        out_specs=[wide, wide, pl.BlockSpec((1, LANE), lambda j: (0, 0))],
        out_shape=[jax.ShapeDtypeStruct((L, GDN_WIDTH), bf16), jax.ShapeDtypeStruct((L, GDN_WIDTH), bf16),
                   jax.ShapeDtypeStruct((1, LANE), f32)],
        compiler_params=_cparams("arbitrary"),
    )(o, proj, gnw, dmix_a)


def _conv_bwd(proj, cw, cb, dmix_b):
    L = proj.shape[0]

    def body(b_ref, c_ref, h_ref, z_ref, cw_ref, cb_ref, dm_ref, db_ref, dc_ref, dh_ref, dz_ref, gcw_ref, gcb_ref):
        for ls in HALVES:
            bv, cv_, hv, z, dm = b_ref[:, ls], c_ref[:, ls], h_ref[:, ls], z_ref[:, ls], dm_ref[:, ls]
            u = cv_ * hv
            cv = _conv3(u, cw_ref, ls) + cb_ref[:, ls]
            sg = _sigmoid(z)
            sz = z * sg
            db_ref[:, ls] = (dm * cv * sz).astype(bf16)
            dz_ref[:, ls] = (dm * (bv * cv) * (sg * (1.0 + z * (1.0 - sg)))).astype(bf16)
            dcv = dm * bv * sz
            gcb_ref[:, ls] = jnp.sum(dcv, axis=0, keepdims=True)
            dcv1, dcv2 = _shift_up(dcv, 1), _shift_up(dcv, 2)
            gcw_ref[2:3, ls] = jnp.sum(dcv * u, axis=0, keepdims=True)
            gcw_ref[1:2, ls] = jnp.sum(dcv1 * u, axis=0, keepdims=True)
            gcw_ref[0:1, ls] = jnp.sum(dcv2 * u, axis=0, keepdims=True)
            du = cw_ref[2:3, ls] * dcv + cw_ref[1:2, ls] * dcv1 + cw_ref[0:1, ls] * dcv2
            dc_ref[:, ls] = (du * hv).astype(bf16)
            dh_ref[:, ls] = (du * cv_).astype(bf16)

    col = pl.BlockSpec((L, ELT_W), lambda j: (0, j))
    act = jax.ShapeDtypeStruct((L, CONV_WIDTH), bf16)
    return _pcall(
        body, name="conv_bwd", grid=(CONV_WIDTH // ELT_W,),
        in_specs=_conv_specs(L) + [col],
        out_specs=[col, col, col, col, pl.BlockSpec((3, ELT_W), lambda j: (0, j)), pl.BlockSpec((1, ELT_W), lambda j: (0, j))],
        out_shape=[act, act, act, act, jax.ShapeDtypeStruct((3, CONV_WIDTH), f32), jax.ShapeDtypeStruct((1, CONV_WIDTH), f32)],
        compiler_params=_cparams("parallel"),
    )(proj, proj, proj, proj, cw, cb, dmix_b)


def _gdn_bwd(qkv, sc, gr, u_all, w_all, vn_all, t_all, sp_all, do_all):
    L = qkv.shape[0]
    nc = L // CHUNK
    W = GDN_WIDTH
    cps = GDN_CPS_BWD if nc % GDN_CPS_BWD == 0 else 1
    rows_per_step = cps * CHUNK
    nsteps = nc // cps

    def body(qkv_ref, sc_ref, gr_ref, u_ref, w_ref, vn_ref, t_ref, sp_ref, do_ref, dqkv_ref, dsc_ref, dgr_ref, ds_scr):
        @pl.when(pl.program_id(0) == 0)
        def _():
            ds_scr[...] = jnp.zeros_like(ds_scr)
        HS = range(cps * HEADS)
        hd = [i % HEADS for i in HS]
        rs = [slice((i // HEADS) * CHUNK, (i // HEADS + 1) * CHUNK) for i in HS]
        cs = [slice(hd[i] * HEAD_DIM, (hd[i] + 1) * HEAD_DIM) for i in HS]
        q = [qkv_ref[rs[i], hd[i] * HEAD_DIM:(hd[i] + 1) * HEAD_DIM] for i in HS]
        k = [qkv_ref[rs[i], W + hd[i] * HEAD_DIM:W + (hd[i] + 1) * HEAD_DIM] for i in HS]
        v = [qkv_ref[rs[i], 2 * W + hd[i] * HEAD_DIM:2 * W + (hd[i] + 1) * HEAD_DIM] for i in HS]
        hsc = [_head_scalars(sc_ref[rs[i], :], gr_ref, hd[i], i // HEADS) for i in HS]
        beta, gcc, gl, dmat, dmat_t = ([x[i] for x in hsc] for i in range(5))
        ii, jj = hsc[0][5], hsc[0][6]
        eg = [jnp.exp(gcc[h]) for h in HS]
        ekl = [jnp.exp(gl[h] - gcc[h]) for h in HS]
        egl = [jnp.exp(gl[h]) for h in HS]
        kb = [k[h] * beta[h] for h in HS]
        ks = [k[h] * ekl[h] for h in HS]
        do = [do_ref[rs[h], cs[h]] for h in HS]
        vn = [vn_ref[rs[h], cs[h]] for h in HS]
        s = [sp_ref[h // HEADS, cs[h], :] for h in HS]
        w = [w_ref[rs[h], cs[h]] for h in HS]
        qd = [q[h] * eg[h] for h in HS]

        kq = [_mm_nt(k[h], q[h]) for h in HS]
        p_t = [jnp.where(jj >= ii, kq[h] * dmat_t[h], 0.0) for h in HS]
        ptd = [_mm(p_t[h], do[h]) for h in HS]
        qw =[jnp.concatenate([qd[h], -w[h]], axis=0) for h in HS]
        dsn, dvn, dodv = [None] * len(HS), [None] * len(HS), [None] * len(HS)
        ds_cur = [ds_scr[h] for h in range(HEADS)]
        for ci in reversed(range(cps)):
            IS = range(ci * HEADS, (ci + 1) * HEADS)
            ksd = [_mm(ks[i], ds_cur[hd[i]]) for i in IS]
            for i in IS:
                dsn[i] = ds_cur[hd[i]]
                dvn[i] = ptd[i] + ksd[hd[i]]
                dodv[i] = jnp.concatenate([do[i], dvn[i]], axis=0)
            dsq = [_mm_tn(qw[i], dodv[i]) for i in IS]
            ds_cur = [egl[i] * ds_cur[hd[i]] + dsq[hd[i]] for i in IS]
        for h in range(HEADS):
            ds_scr[h] = ds_cur[h]
        x1 = [_mm_nt(dodv[h], s[h]) for h in HS]
        dks = [_mm_nt(vn[h], dsn[h]) for h in HS]
        dov = [_mm_nt(do[h], vn[h]) for h in HS]
        vdo = [_mm_nt(vn[h], do[h]) for h in HS]
        kk = [_mm_nt(kb[h], k[h]) for h in HS]
        qk = [_mm_nt(q[h], k[h]) for h in HS]
        dgl = [egl[h] * jnp.sum(jnp.sum(s[h] * dsn[h], axis=1, keepdims=True), axis=0, keepdims=True) for h in HS]
        dqd = [x1[h][:CHUNK] for h in HS]
        duw = [jnp.concatenate([dvn[h], -x1[h][CHUNK:]], axis=1) for h in HS]
        tdu = [_mm_tn(t_ref[h // HEADS, hd[h]], duw[h]) for h in HS]
        dvk = [duw[h] + tdu[h] for h in HS]
        uw = [jnp.concatenate([u_ref[rs[h], cs[h]], w[h]], axis=1) for h in HS]
        da = [-jnp.where(ii > jj, _mm_nt(dvk[h], uw[h]), 0.0) for h in HS]
        da_t = [-jnp.where(jj > ii, _mm_nt(uw[h], dvk[h]), 0.0) for h in HS]
        dp = [jnp.where(ii >= jj, dov[h], 0.0) for h in HS]
        dp_t = [jnp.where(jj >= ii, vdo[h], 0.0) for h in HS]
        r1 = [_mm(jnp.concatenate([da[h] * dmat[h], dp[h] * dmat[h]], axis=0), k[h]) for h in HS]
        dk1 = [_mm(jnp.concatenate([da_t[h] * dmat_t[h], dp_t[h] * dmat_t[h]], axis=1),
                   jnp.concatenate([kb[h], q[h]], axis=0)) for h in HS]
        lane = _lanes((CHUNK, LANE))
        for ci in range(cps):
            dsc = jnp.zeros((CHUNK, LANE), f32)
            for i in range(ci * HEADS, (ci + 1) * HEADS):
                h = hd[i]
                a = jnp.where(ii > jj, kk[i] * dmat[i], 0.0)
                p = jnp.where(ii >= jj, qk[i] * dmat[i], 0.0)
                gmat = da[i] * a + dp[i] * p
                dvb, dkbg = dvk[i][:, :HEAD_DIM], dvk[i][:, HEAD_DIM:]
                kbg = kb[i] * eg[i]
                dkb = r1[i][:CHUNK] + dkbg * eg[i]
                dq = r1[i][CHUNK:] + dqd[i] * eg[i]
                dk = dk1[i] + dks[i] * ekl[i] + dkb * beta[i]
                dbeta = jnp.sum(dkb * k[i] + dvb * v[i], axis=1, keepdims=True)
                ksum = jnp.sum(dks[i] * ks[i], axis=1, keepdims=True)
                dgl_tot = dgl[i] + jnp.sum(ksum, axis=0, keepdims=True)
                dgc = (jnp.sum(gmat, axis=1, keepdims=True) + jnp.sum(dqd[i] * qd[i] + dkbg * kbg, axis=1, keepdims=True)
                       - ksum)
                dgc = dgc + jnp.where(_rows(dgc.shape) == CHUNK - 1, dgl_tot, 0.0)
                dqkv_ref[rs[i], h * HEAD_DIM:(h + 1) * HEAD_DIM] = dq
                dqkv_ref[rs[i], W + h * HEAD_DIM:W + (h + 1) * HEAD_DIM] = dk
                dqkv_ref[rs[i], 2 * W + h * HEAD_DIM:2 * W + (h + 1) * HEAD_DIM] = dvb * beta[i]
                dsc = jnp.where(lane == h, dbeta, jnp.where(lane == HEADS + h, dgc, dsc))
                dgr_ref[ci, h:h + 1, :] = jnp.sum(gmat, axis=0, keepdims=True)
            dsc_ref[ci * CHUNK:(ci + 1) * CHUNK, :] = dsc

    row = lambda c: (nsteps - 1 - c, 0)
    lead3 = lambda c: (nsteps - 1 - c, 0, 0)
    return _pcall(
        body, name="gdn_bwd", grid=(nsteps,),
        in_specs=[pl.BlockSpec((rows_per_step, 3 * W), row), pl.BlockSpec((rows_per_step, LANE), row),
                  pl.BlockSpec((cps, HEADS, CHUNK), lead3),
                  pl.BlockSpec((rows_per_step, W), row), pl.BlockSpec((rows_per_step, W), row),
                  pl.BlockSpec((rows_per_step, W), row),
                  pl.BlockSpec((cps, HEADS, CHUNK, CHUNK), lambda c: (nsteps - 1 - c, 0, 0, 0)),
                  pl.BlockSpec((cps, W, HEAD_DIM), lead3), pl.BlockSpec((rows_per_step, W), row)],
        out_specs=[pl.BlockSpec((rows_per_step, 3 * W), row), pl.BlockSpec((rows_per_step, LANE), row),
                   pl.BlockSpec((cps, HEADS, CHUNK), lead3)],
        out_shape=[jax.ShapeDtypeStruct((L, 3 * W), f32), jax.ShapeDtypeStruct((L, LANE), f32),
                   jax.ShapeDtypeStruct((nc, HEADS, CHUNK), f32)],
        scratch_shapes=[pltpu.VMEM((HEADS, HEAD_DIM, HEAD_DIM), f32)],
        compiler_params=_cparams("arbitrary"),
    )(qkv, sc, gr, u_all, w_all, vn_all, t_all, sp_all, do_all)


def _qkv_bwd(proj, cw, dn):
    L = proj.shape[0]

    def body(x_ref, cw_ref, dn_ref, dx_ref, gcw_ref):
        j = pl.program_id(0)
        scale = jnp.where(j < STEPS_PER_GROUP, HEAD_DIM ** -0.5, 1.0).astype(f32)
        for ls in HALVES:
            x, dn_v = x_ref[:, ls], dn_ref[:, ls]
            c = _conv4(x, cw_ref, ls)
            sg = _sigmoid(c)
            a = c * sg
            rn = lax.rsqrt(jnp.sum(a * a, axis=1, keepdims=True) + EPS)
            da_n = (scale * rn) * (dn_v - a * ((rn * rn) * jnp.sum(dn_v * a, axis=1, keepdims=True)))
            da = jnp.where(j < 2 * STEPS_PER_GROUP, da_n, dn_v)
            dc = da * (sg * (1.0 + c * (1.0 - sg)))
            dc1, dc2, dc3 = _shift_up(dc, 1), _shift_up(dc, 2), _shift_up(dc, 3)
            gcw_ref[3:4, ls] = jnp.sum(dc * x, axis=0, keepdims=True)
            gcw_ref[2:3, ls] = jnp.sum(dc1 * x, axis=0, keepdims=True)
            gcw_ref[1:2, ls] = jnp.sum(dc2 * x, axis=0, keepdims=True)
            gcw_ref[0:1, ls] = jnp.sum(dc3 * x, axis=0, keepdims=True)
            dx = cw_ref[3:4, ls] * dc + cw_ref[2:3, ls] * dc1 + cw_ref[1:2, ls] * dc2 + cw_ref[0:1, ls] * dc3
            dx_ref[:, ls] = dx.astype(bf16)

    col = pl.BlockSpec((L, ELT_W), lambda j: (0, j))
    wspec = pl.BlockSpec((4, ELT_W), lambda j: (0, j))
    return _pcall(
        body, name="qkv_bwd", grid=(3 * STEPS_PER_GROUP,),
        in_specs=[col, wspec, col], out_specs=[col, wspec],
        out_shape=[jax.ShapeDtypeStruct((L, 3 * GDN_WIDTH), bf16), jax.ShapeDtypeStruct((4, 3 * GDN_WIDTH), f32)],
        compiler_params=_cparams("parallel"),
    )(proj, cw, dn)


def _scalars_bwd(proj, alog_p, dtb_p, dsc, dgr_col):
    L = proj.shape[0]

    def body(x_ref, al_ref, dt_ref, dsc_ref, dgr_ref, dba_ref, gs_ref):
        x, dsc_v = x_ref[...], dsc_ref[...]
        lane = _lanes(x.shape)
        dec = (lane >= HEADS) & (lane < 2 * HEADS)
        dg = jnp.where(dec, dsc_v - dgr_ref[...], 0.0)
        rc = _rows(x.shape) & (CHUNK - 1)
        for s in (1, 2, 4, 8, 16, 32):
            dg = dg + jnp.where(rc + s < CHUNK, pltpu.roll(dg, L - s, 0), 0.0)
        xa = x + dt_ref[...]
        ea = jnp.exp(al_ref[...])
        g = -ea * _softplus(xa)
        da = dg * (-ea) * _sigmoid(xa)
        beta = _sigmoid(x)
        db = dsc_v * beta * (1.0 - beta)
        dba_ref[...] = jnp.where(lane < HEADS, db, jnp.where(dec, da, 0.0)).astype(bf16)
        g_al = jnp.sum(jnp.where(dec, dg * g, 0.0), axis=0, keepdims=True)
        g_dt = jnp.sum(jnp.where(dec, da, 0.0), axis=0, keepdims=True)
        row8 = _rows(gs_ref.shape)
        gs = jnp.where(row8 == 0, g_al, jnp.where(row8 == 1, g_dt, 0.0))
        gs_ref[...] = pltpu.roll(gs, LANE - HEADS, 1)

    full = pl.BlockSpec((L, LANE), lambda i: (0, 0))
    vec = pl.BlockSpec((1, LANE), lambda i: (0, 0))
    return _pcall(
        body, name="scalars_bwd", grid=(1,),
        in_specs=[pl.BlockSpec((L, LANE), lambda i: (0, OFF_BA // LANE)), vec, vec, full, full],
        out_specs=[full, pl.BlockSpec((8, LANE), lambda i: (0, 0))],
        out_shape=[jax.ShapeDtypeStruct((L, LANE), bf16), jax.ShapeDtypeStruct((8, LANE), f32)],
        compiler_params=_cparams("arbitrary"),
    )(proj, alog_p, dtb_p, dsc, dgr_col)


def _input_grad(pieces, offs, wpad, x, nw, dy):
    L = x.shape[0]
    tm = min(512, L)
    npc = len(pieces)

    def body(*refs):
        p_refs = refs[:npc]
        w_hbm, x_ref, nw_ref, dy_ref, gx_ref, gnw_ref, w_vmem, sems = refs[npc:]
        first = pl.program_id(0) == 0
        loads = [pltpu.make_async_copy(w_hbm.at[off:off + p.shape[1], :], w_vmem.at[off:off + p.shape[1], :], sems.at[k])
                 for k, (p, off) in enumerate(zip(p_refs, offs))]

        @pl.when(first)
        def _():
            for cp in loads:
                cp.start()
            gnw_ref[...] = jnp.zeros_like(gnw_ref)
        dh = None
        for k, (p_ref, off) in enumerate(zip(p_refs, offs)):
            wd = p_ref.shape[1]
            pl.when(first)(loads[k].wait)
            part = jnp.dot(p_ref[...], w_vmem[off:off + wd, :], preferred_element_type=f32)
            dh = part if dh is None else dh + part
        xv, nwv = x_ref[...], nw_ref[...]
        r = lax.rsqrt(jnp.mean(xv * xv, axis=-1, keepdims=True) + EPS)
        xh = xv * r
        gnw_ref[...] += jnp.sum(dh * xh, axis=0, keepdims=True)
        dxh = dh * nwv
        gx_ref[...] = dy_ref[...] + r * (dxh - xh * jnp.mean(dxh * xh, axis=-1, keepdims=True))

    row = lambda i: (i, 0)
    fix = lambda i: (0, 0)
    return _pcall(
        body, name="input_grad", grid=(L // tm,),
        in_specs=[pl.BlockSpec((tm, p.shape[1]), row) for p in pieces] + [
            ANY, pl.BlockSpec((tm, D_MODEL), row), pl.BlockSpec((1, D_MODEL), fix), pl.BlockSpec((tm, D_MODEL), row)],
        out_specs=[pl.BlockSpec((tm, D_MODEL), row), pl.BlockSpec((1, D_MODEL), fix)],
        out_shape=[jax.ShapeDtypeStruct((L, D_MODEL), f32), jax.ShapeDtypeStruct((1, D_MODEL), f32)],
        scratch_shapes=[pltpu.VMEM(wpad.shape, bf16), pltpu.SemaphoreType.DMA((npc,))],
        compiler_params=_cparams("arbitrary"),
    )(*pieces, wpad, x, nw, dy)


def _adamw_reduce(parts, w, m, v, name):
    R, C = w.shape
    n_parts = parts.shape[0]
    tr = 128 if R % 128 == 0 else R
    c1 = 1.0 - ADAM_B1 ** ADAM_STEP
    c2 = 1.0 - ADAM_B2 ** ADAM_STEP

    def body(p_ref, w_ref, m_ref, v_ref, g_ref, d_ref, nm_ref, nv_ref):
        g = p_ref[0].astype(f32)
        for s in range(1, n_parts):
            g = g + p_ref[s].astype(f32)
        nm = ADAM_B1 * m_ref[...] + (1.0 - ADAM_B1) * g
        nv = ADAM_B2 * v_ref[...] + (1.0 - ADAM_B2) * (g * g)
        g_ref[...] = g
        nm_ref[...] = nm
        nv_ref[...] = nv
        d_ref[...] = -ADAM_LR * ((nm / c1) / (jnp.sqrt(nv / c2) + ADAM_EPS) + ADAM_WD * w_ref[...])

    blk = pl.BlockSpec((tr, C), lambda i: (i, 0))
    out = jax.ShapeDtypeStruct((R, C), f32)
    return _pcall(
        body, name=name, grid=(R // tr,),
        in_specs=[pl.BlockSpec((n_parts, tr, C), lambda i: (0, i, 0)), blk, blk, blk],
        out_specs=[blk] * 4, out_shape=[out] * 4,
        compiler_params=_cparams("parallel"),
    )(parts, w, m, v)


SMALL_SLOTS = ((0, D_MODEL), (D_MODEL, D_MODEL), (2 * D_MODEL, D_MODEL), (3 * D_MODEL, LANE),
               (3 * D_MODEL + LANE, HEADS), (3 * D_MODEL + 2 * LANE, HEADS))
SMALL_LOSS = 3 * D_MODEL + 3 * LANE
SMALL_W = SMALL_LOSS + LANE


def _pack_small(gs, after):
    def body(nw_ref, cb_ref, fw_ref, gn_ref, sc_ref, ls_ref, after_ref, o_ref):
        for ref, (start, width) in zip((nw_ref, cb_ref, fw_ref, gn_ref), SMALL_SLOTS[:4]):
            o_ref[:, start:start + width] = ref[...]
        o_ref[:, SMALL_SLOTS[4][0]:SMALL_SLOTS[4][0] + LANE] = sc_ref[0:1, :]
        o_ref[:, SMALL_SLOTS[5][0]:SMALL_SLOTS[5][0] + LANE] = sc_ref[1:2, :]
        o_ref[:, SMALL_LOSS:SMALL_W] = ls_ref[...]

    vm = pl.BlockSpec(memory_space=pltpu.VMEM)
    return _pcall(body, name="pack_small_grads", out_shape=jax.ShapeDtypeStruct((1, SMALL_W), f32),
                  in_specs=[vm] * 6 + [ANY], out_specs=vm)(*gs, after)


def _adamw_small(parts, ws, ms, vs):
    c1 = 1.0 - ADAM_B1 ** ADAM_STEP
    c2 = 1.0 - ADAM_B2 ** ADAM_STEP
    np_ = len(ws)

    def body(*refs):
        p_ref = refs[0]
        w_refs, m_refs, v_refs = refs[1:1 + np_], refs[1 + np_:1 + 2 * np_], refs[1 + 2 * np_:1 + 3 * np_]
        outs = refs[1 + 3 * np_:]
        g_refs, d_refs, nm_refs, nv_refs = (outs[i * np_:(i + 1) * np_] for i in range(4))
        loss_ref = outs[4 * np_]

        def total(start, width):
            t = p_ref[0, :, start:start + width]
            for s in range(1, N_DEV):
                t = t + p_ref[s, :, start:start + width]
            return t

        for i, (start, width) in enumerate(SMALL_SLOTS):
            g = total(start, width)
            nm = ADAM_B1 * m_refs[i][...] + (1.0 - ADAM_B1) * g
            nv = ADAM_B2 * v_refs[i][...] + (1.0 - ADAM_B2) * (g * g)
            g_refs[i][...] = g
            nm_refs[i][...] = nm
            nv_refs[i][...] = nv
            d_refs[i][...] = -ADAM_LR * ((nm / c1) / (jnp.sqrt(nv / c2) + ADAM_EPS) + ADAM_WD * w_refs[i][...])
        loss_ref[...] = total(SMALL_LOSS, LANE)

    vm = pl.BlockSpec(memory_space=pltpu.VMEM)
    shapes = [jax.ShapeDtypeStruct(w.shape, f32) for w in ws]
    res = _pcall(body, name="adamw_small", out_shape=shapes * 4 + [jax.ShapeDtypeStruct((1, LANE), f32)],
                 in_specs=[vm] * (1 + 3 * np_), out_specs=[vm] * (4 * np_ + 1))(parts, *ws, *ms, *vs)
    return [res[i * np_:(i + 1) * np_] for i in range(4)], res[4 * np_]


def _adamw_w_in(parts, w3, m3, v3):
    n_parts, n, _ = parts.shape
    c1 = 1.0 - ADAM_B1 ** ADAM_STEP
    c2 = 1.0 - ADAM_B2 ** ADAM_STEP

    def body(p_ref, w_ref, m_ref, v_ref, g_ref, d_ref, nm_ref, nv_ref):
        g = p_ref[0].astype(f32)
        for s in range(1, n_parts):
            g = g + p_ref[s].astype(f32)
        nm = ADAM_B1 * m_ref[:, 0, :] + (1.0 - ADAM_B1) * g
        nv = ADAM_B2 * v_ref[:, 0, :] + (1.0 - ADAM_B2) * (g * g)
        g_ref[:, 0, :] = g
        nm_ref[:, 0, :] = nm
        nv_ref[:, 0, :] = nv
        d_ref[:, 0, :] = -ADAM_LR * ((nm / c1) / (jnp.sqrt(nv / c2) + ADAM_EPS) + ADAM_WD * w_ref[:, 0, :])

    tile = 2 * COL_TILE
    blk = pl.BlockSpec((n, 1, tile), lambda j: (0, 0, j))
    out = jax.ShapeDtypeStruct((n, 1, D_MODEL), f32)
    return _pcall(
        body, name="adamw_w_in", grid=(D_MODEL // tile,),
        in_specs=[pl.BlockSpec((n_parts, n, tile), lambda j: (0, 0, j)), blk, blk, blk],
        out_specs=[blk] * 4, out_shape=[out] * 4,
        compiler_params=_cparams("parallel"),
    )(parts, w3, m3, v3)


def _pad_lanes(vec8, start):
    return jnp.pad(vec8.reshape(1, -1), ((0, 0), (start, LANE - start - vec8.size)))


def kernel(x, norm_in_w, w_in, conv_qkv_w, A_log, dt_bias, gdn_norm_w, conv_w, conv_b, w_out, final_norm_w, loss_target, m_norm_in_w, m_w_in, m_conv_qkv_w, m_A_log, m_dt_bias, m_gdn_norm_w, m_conv_w, m_conv_b, m_w_out, m_final_norm_w, v_norm_in_w, v_w_in, v_conv_qkv_w, v_A_log, v_dt_bias, v_gdn_norm_w, v_conv_w, v_conv_b, v_w_out, v_final_norm_w):
    L = x.shape[1]
    nc = L // CHUNK
    xs = x[0]
    tgt = loss_target[0]
    fnw = final_norm_w.reshape(1, D_MODEL)

    as_rows = lambda a: jnp.transpose(a, (2, 0, 1))
    win_g, cqkv_g, cw_g = _all_gather([_cast_w_in(as_rows(w_in)), conv_qkv_w[0], conv_w[0]], "gather_weights",
                                      pieces=[2, 1, 1])
    wpad = _relayout_w_in(win_g)
    cqkv = jnp.concatenate([cqkv_g[d] for d in range(N_DEV)], axis=1)
    cw = jnp.concatenate([cw_g[d] for d in range(N_DEV)], axis=1)
    alog_p = _pad_lanes(A_log, HEADS)
    dtb_p = _pad_lanes(dt_bias, HEADS)
    me_flat, me_chip = _flat(*_mesh_pos()), 2 * lax.axis_index("x") + lax.axis_index("y")
    tok = lambda started: started[4][0:1, 0:1]
    wo_own = w_out[0].astype(bf16)
    wo_started = _spread_start(wo_own, wpad, "gather", "gather_w_out_start")

    proj, h = _in_proj(xs, norm_in_w + tok(wo_started), wpad)
    qkv = _qkv_act(proj, cqkv)
    sc, gr = _scalars(proj, alog_p, dtb_p)
    o, u_all, w_all, vn_all, t_all, sp_all = _gdn_fwd(qkv, sc, gr)
    mix_a = _gdn_gate(o, proj, gdn_norm_w)
    mix_b = _conv_fwd(proj, cw, conv_b)
    wo = _own_slot(_spread_wait(wo_started, mix_b, "gather", "gather_w_out_wait"), wo_own, me_flat).reshape(-1, D_MODEL)
    dy, dyb, dmix_a, dmix_b, g_fnw, loss_v = _out_proj_loss(xs, mix_a, mix_b, wo, fnw, tgt)

    g_wout = jnp.concatenate([_tn_matmul(mix_a, dyb, "grad_w_out_a"), _tn_matmul(mix_b, dyb, "grad_w_out_b")], axis=0)
    g_wout = g_wout.reshape(N_DEV, -1, D_MODEL)
    g_wout_own = lax.dynamic_index_in_dim(g_wout, me_flat, 0, keepdims=False)
    gwo_started = _spread_start(g_wout, dyb, "scatter", "exchange_grad_w_out_start")
    do, dzg, g_gnw = _gdn_gate_bwd(o, proj, gdn_norm_w + tok(gwo_started), dmix_a)
    d_b, d_c, d_hc, d_zc, g_cw, g_cb = _conv_bwd(proj, cw, conv_b, dmix_b)
    dqkv_n, dsc, dgr = _gdn_bwd(qkv, sc, gr, u_all, w_all, vn_all, t_all, sp_all, do)
    dqkv, g_cqkv = _qkv_bwd(proj, cqkv, dqkv_n)
    dgr_col = jnp.pad(dgr.transpose(0, 2, 1).reshape(L, HEADS), ((0, 0), (HEADS, LANE - 2 * HEADS)))
    dba, g_sc = _scalars_bwd(proj, alog_p, dtb_p, dsc, dgr_col)
    pieces = [dqkv, dzg, dba, d_b, d_c, d_hc, d_zc]
    offs = [OFF_QKV, OFF_ZG, OFF_BA, OFF_B, OFF_C, OFF_HC, OFF_ZC]
    g_parts = [_tn_matmul(p, h, "grad_w_in_%d" % i) for i, p in enumerate(pieces)]
    g_win_blk = _grad_blocks(g_parts)

    (p_win,) = _pair_exchange([g_win_blk], "exchange_grads_pair")
    s_win = _pair_sum(g_win_blk, p_win, "pair_sum_w_in")
    s_win_own = lax.dynamic_index_in_dim(s_win, me_chip, 0, keepdims=False)
    r_cqkv, r_cw = _all_to_all(
        [g_cqkv.reshape(4, N_DEV, -1).transpose(1, 0, 2), g_cw.reshape(3, N_DEV, -1).transpose(1, 0, 2)],
        "exchange_small_sharded_grads")
    gwi_started = _spread_start(s_win, r_cw, "chips", "exchange_grads_chips_start")
    grad_x, g_nw = _input_grad(pieces, offs, wpad, xs, norm_in_w + tok(gwi_started), dy)

    r_wout = _own_slot(_spread_wait(gwo_started, grad_x, "scatter", "exchange_grad_w_out_wait"), g_wout_own, me_flat)
    upd_wout =_adamw_reduce(r_wout, w_out[0], m_w_out[0], v_w_out[0], "adamw_w_out")
    upd_cqkv = _adamw_reduce(r_cqkv, conv_qkv_w[0], m_conv_qkv_w[0], v_conv_qkv_w[0], "adamw_conv_qkv_w")
    upd_cw = _adamw_reduce(r_cw, conv_w[0], m_conv_w[0], v_conv_w[0], "adamw_conv_w")

    r_win = _own_slot(_spread_wait(gwi_started, upd_cw[0], "chips", "exchange_grads_chips_wait"), s_win_own, me_chip)
    upd_win = [jnp.transpose(a, (1, 2, 0)) for a in _adamw_w_in(r_win, as_rows(w_in), as_rows(m_w_in), as_rows(v_w_in))]

    small_g = _pack_small([g_nw, g_cb, g_fnw, g_gnw, g_sc, loss_v], r_win)
    (small_all,) = _all_gather([small_g], "gather_small_grads")
    fvec = lambda a: a.reshape(1, D_MODEL)
    upd_small, loss_sum = _adamw_small(
        small_all,
        [norm_in_w, conv_b, fvec(final_norm_w), gdn_norm_w, A_log, dt_bias],
        [m_norm_in_w, m_conv_b, fvec(m_final_norm_w), m_gdn_norm_w, m_A_log, m_dt_bias],
        [v_norm_in_w, v_conv_b, fvec(v_final_norm_w), v_gdn_norm_w, v_A_log, v_dt_bias])

    outs = [loss_sum[0, 0], grad_x[None]]
    for k in range(4):
        nw_k, cb_k, fw_k, gn_k, al_k, dt_k = upd_small[k]
        outs += [nw_k, upd_win[k], upd_cqkv[k][None], al_k, dt_k, gn_k,
                 upd_cw[k][None], cb_k, upd_wout[k][None], fw_k.reshape(D_MODEL)]
    return tuple(outs)
```

```python
import jax
import jax.numpy as jnp
from jax import lax
from jax.experimental import pallas as pl
from jax.experimental.pallas import tpu as pltpu

f32 = jnp.float32
bf16 = jnp.bfloat16

N_DEV = 8
D_MODEL = 1024
HEADS = 8
HEAD_DIM = 128
CHUNK = 64
GDN_CPS = 4
GDN_CPS_BWD = 1
GDN_WIDTH = HEADS * HEAD_DIM
CONV_WIDTH = 1024
PROJ_WIDTH = 8208
SHARD_W = PROJ_WIDTH // N_DEV
EPS = 1e-6

NAT_SMALL_END = 4112
PAD_COLS = 240
OFF_QKV, OFF_ZG, OFF_BA, OFF_B, OFF_C, OFF_HC, OFF_ZC = 0, 3072, 4096, 4352, 5376, 6400, 7424
PROJ_PAD = 8448
LANE = 128
ELT_W = 256

ADAM_LR, ADAM_B1, ADAM_B2, ADAM_EPS, ADAM_WD, ADAM_STEP = 0.001, 0.9, 0.999, 1e-08, 0.01, 10

V7X_VMEM_BYTES = 64 * 1024 * 1024
VMEM_LIMIT = V7X_VMEM_BYTES - 8 * 1024 * 1024

MESH = pl.DeviceIdType.MESH
ANY = pl.BlockSpec(memory_space=pl.ANY)


def _pcall(body, **kw):
    return pl.pallas_call(body, **kw)


def _cparams(*sem):
    return pltpu.CompilerParams(dimension_semantics=sem if sem else None, vmem_limit_bytes=VMEM_LIMIT)


def _mm(a, b):
    return jnp.dot(a.astype(bf16), b.astype(bf16), preferred_element_type=f32)


def _mm_nt(a, b):
    return lax.dot_general(a.astype(bf16), b.astype(bf16), (((1,), (1,)), ((), ())), preferred_element_type=f32)


def _mm_tn(a, b):
    return lax.dot_general(a.astype(bf16), b.astype(bf16), (((0,), (0,)), ((), ())), preferred_element_type=f32)


def _rows(shape):
    return lax.broadcasted_iota(jnp.int32, shape, 0)


def _lanes(shape):
    return lax.broadcasted_iota(jnp.int32, shape, 1)


def _shift_down(x, s):
    if s == 0:
        return x
    return jnp.where(_rows(x.shape) >= s, pltpu.roll(x, s, 0), 0.0)


def _shift_up(x, s):
    if s == 0:
        return x
    n = x.shape[0]
    return jnp.where(_rows(x.shape) < n - s, pltpu.roll(x, n - s, 0), 0.0)


def _sigmoid(x):
    return jax.nn.sigmoid(x)


def _softplus(x):
    e = jnp.exp(-jnp.abs(x))
    small = e * (1.0 - e * (0.5 - e * (1.0 / 3.0)))
    return jnp.maximum(x, 0.0) + jnp.where(e < 0.01, small, jnp.log(1.0 + e))


def _mesh_pos():
    return lax.axis_index("x"), lax.axis_index("y"), lax.axis_index("c")


def _flat(px, py, pc):
    return 4 * px + 2 * py + pc


def _all_gather(xs, name, pieces=None):
    n = len(xs)
    pieces = pieces or [1] * n
    items = [(a, q) for a in range(n) for q in range(pieces[a])]
    ni = len(items)

    def view(ref, i):
        a, q = items[i]
        if pieces[a] == 1:
            return ref
        wd = xs[a].shape[-1] // pieces[a]
        return ref.at[(slice(None),) * (xs[a].ndim - 1) + (pl.ds(q * wd, wd),)]

    def body(*refs):
        x_refs, o_refs = refs[:n], refs[n:2 * n]
        send_sems, recv_sems, local_sems = refs[2 * n:]
        x, y, c = _mesh_pos()
        me, sibling = (x, y, c), (x, y, 1 - c)
        flip = lambda v, bit: v + bit - 2 * v * bit
        nbr_a = (flip(x, 1 - c), flip(y, c))
        nbr_b = (flip(x, c), flip(y, 1 - c))
        diag = (1 - x, 1 - y)

        def copy(i, k, block, to, own=False):
            a = items[i][0]
            dst = view(o_refs[a].at[_flat(*block)], i)
            return pltpu.make_async_remote_copy(
                src_ref=view(x_refs[a], i) if own else dst, dst_ref=dst,
                send_sem=send_sems.at[i, k], recv_sem=recv_sems.at[i, k], device_id=to, device_id_type=MESH)

        mine, sent = [], []

        def go(cp):
            cp.start()
            sent.append(cp)

        for a in range(n):
            cp = pltpu.make_async_copy(x_refs[a], o_refs[a].at[_flat(*me)], local_sems.at[a])
            cp.start()
            mine.append(cp)
        for a in range(ni):
            go(copy(a, 1, me, (*nbr_a, c), own=True))
            go(copy(a, 2, me, (*nbr_b, c), own=True))
            go(copy(a, 0, me, sibling, own=True))
        for a in range(ni):
            copy(a, 1, (*nbr_a, c), me).wait_recv()
            go(copy(a, 3, (*nbr_a, c), (*nbr_b, c)))
            go(copy(a, 4, (*nbr_a, c), sibling))
        for a in range(ni):
            copy(a, 2, (*nbr_b, c), me).wait_recv()
            go(copy(a, 5, (*nbr_b, c), sibling))
        for a in range(ni):
            copy(a, 3, (*diag, c), me).wait_recv()
            go(copy(a, 6, (*diag, c), sibling))
        for a in range(ni):
            copy(a, 0, sibling, me).wait_recv()
            copy(a, 4, (*nbr_b, 1 - c), me).wait_recv()
            copy(a, 5, (*nbr_a, 1 - c), me).wait_recv()
            copy(a, 6, (*diag, 1 - c), me).wait_recv()
        for cp in sent:
            cp.wait_send()
        for cp in mine:
            cp.wait()

    outs = _pcall(
        body, name=name,
        out_shape=[jax.ShapeDtypeStruct((N_DEV,) + a.shape, a.dtype) for a in xs],
        in_specs=[ANY] * n, out_specs=[ANY] * n,
        scratch_shapes=[pltpu.SemaphoreType.DMA((ni, 7)), pltpu.SemaphoreType.DMA((ni, 7)), pltpu.SemaphoreType.DMA((n,))],
    )(*xs)
    return list(outs)


def _all_to_all(gs, name):
    n = len(gs)

    def body(*refs):
        g_refs, o_refs = refs[:n], refs[n:2 * n]
        send_sems, recv_sems, local_sems = refs[2 * n:]
        x, y, c = _mesh_pos()
        me = _flat(x, y, c)
        peers = []
        for k in range(1, N_DEV):
            kx, ky, kc = (k >> 2) & 1, (k >> 1) & 1, k & 1
            px = (1 - x) if kx else x
            py = (1 - y) if ky else y
            pc = (1 - c) if kc else c
            peers.append((px, py, pc))

        def copy(a, k):
            peer = peers[k - 1]
            return pltpu.make_async_remote_copy(
                src_ref=g_refs[a].at[_flat(*peer)], dst_ref=o_refs[a].at[me],
                send_sem=send_sems.at[a, k - 1], recv_sem=recv_sems.at[a, k - 1], device_id=peer, device_id_type=MESH)

        def arrival(a, k):
            peer = peers[k - 1]
            return pltpu.make_async_remote_copy(
                src_ref=g_refs[a].at[me], dst_ref=o_refs[a].at[_flat(*peer)],
                send_sem=send_sems.at[a, k - 1], recv_sem=recv_sems.at[a, k - 1], device_id=peer, device_id_type=MESH)

        mine, sent = [], []
        for a in range(n):
            cp = pltpu.make_async_copy(g_refs[a].at[me], o_refs[a].at[me], local_sems.at[a])
            cp.start()
            mine.append(cp)
            for k in range(1, N_DEV):
                cp = copy(a, k)
                cp.start()
                sent.append(cp)
        for a in range(n):
            for k in range(1, N_DEV):
                arrival(a, k).wait_recv()
        for cp in sent:
            cp.wait_send()
        for cp in mine:
            cp.wait()

    outs = _pcall(
        body, name=name,
        out_shape=[jax.ShapeDtypeStruct(a.shape, a.dtype) for a in gs],
        in_specs=[ANY] * n, out_specs=[ANY] * n,
        scratch_shapes=[pltpu.SemaphoreType.DMA((n, 7)), pltpu.SemaphoreType.DMA((n, 7)), pltpu.SemaphoreType.DMA((n,))],
    )(*gs)
    return list(outs)


def _pair_exchange(gs, name):
    n = len(gs)
    chips = [(0, 0), (0, 1), (1, 0), (1, 1)]

    def body(*refs):
        g_refs, o_refs = refs[:n], refs[n:2 * n]
        send_sems, recv_sems = refs[2 * n:]
        x, y, c = _mesh_pos()
        sibling = (x, y, 1 - c)

        def copy(a, i):
            xp, yp = chips[i]
            return pltpu.make_async_remote_copy(
                src_ref=g_refs[a].at[_flat(xp, yp, 1 - c)], dst_ref=o_refs[a].at[i],
                send_sem=send_sems.at[a, i], recv_sem=recv_sems.at[a, i], device_id=sibling, device_id_type=MESH)

        cps = [copy(a, i) for a in range(n) for i in range(4)]
        for cp in cps:
            cp.start()
        for cp in cps:
            cp.wait()

    outs = _pcall(
        body, name=name,
        out_shape=[jax.ShapeDtypeStruct((4,) + a.shape[1:], a.dtype) for a in gs],
        in_specs=[ANY] * n, out_specs=[ANY] * n,
        scratch_shapes=[pltpu.SemaphoreType.DMA((n, 4)), pltpu.SemaphoreType.DMA((n, 4))],
    )(*gs)
    return list(outs)


def _pair_sum(g, p1, name):
    _, R, C = g.shape
    tr = 256 if R % 256 == 0 else R
    cidx = lax.axis_index("c").astype(jnp.int32).reshape(1)

    def body(c_ref, g_ref, p_ref, o_ref):
        o_ref[...] = (g_ref[...].astype(f32) + p_ref[...].astype(f32)).astype(o_ref.dtype)

    return _pcall(
        body, name=name,
        grid_spec=pltpu.PrefetchScalarGridSpec(
            num_scalar_prefetch=1, grid=(4, R // tr),
            in_specs=[pl.BlockSpec((1, tr, C), lambda i, r, c_ref: (2 * i + c_ref[0], r, 0)),
                      pl.BlockSpec((1, tr, C), lambda i, r, c_ref: (i, r, 0))],
            out_specs=pl.BlockSpec((1, tr, C), lambda i, r, c_ref: (i, r, 0))),
        out_shape=jax.ShapeDtypeStruct((4, R, C), g.dtype),
        compiler_params=_cparams("parallel", "parallel"),
    )(cidx, g, p1)


HBM = pl.BlockSpec(memory_space=pltpu.HBM)
SEM = pl.BlockSpec(memory_space=pltpu.SEMAPHORE)
EFFECT = pltpu.SideEffectType.DATAFLOW_SIDE_EFFECTING


def _peers(x, y, c):
    out = []
    for k in range(1, N_DEV):
        kx, ky, kc = (k >> 2) & 1, (k >> 1) & 1, k & 1
        out.append(((1 - x) if kx else x, (1 - y) if ky else y, (1 - c) if kc else c))
    return out


SPREAD_COPIES = {"gather": N_DEV - 1, "scatter": N_DEV - 1, "chips": 3}


def _spread_copy(src_ref, land_ref, send_sems, recv_sems, k, plan):
    x, y, c = _mesh_pos()
    if plan == "chips":
        px, py = [(1 - x, y), (x, 1 - y), (1 - x, 1 - y)][k]
        peer, src, slot = (px, py, c), src_ref.at[2 * px + py], 2 * x + y
    else:
        peer = _peers(x, y, c)[k]
        src, slot = (src_ref.at[_flat(*peer)] if plan == "scatter" else src_ref), _flat(x, y, c)
    return pltpu.make_async_remote_copy(
        src_ref=src, dst_ref=land_ref.at[slot], send_sem=send_sems.at[k], recv_sem=recv_sems.at[k],
        device_id=peer, device_id_type=MESH)


def _spread_start(src, after, plan, name):
    land_shape = (N_DEV,) + src.shape if plan == "gather" else src.shape
    n_copies = SPREAD_COPIES[plan]

    def body(src_ref, land_ref, after_ref, send_sems, recv_sems, src_thru, land_thru, token):
        for k in range(n_copies):
            _spread_copy(src_ref, land_ref, send_sems, recv_sems, k, plan).start()
        token[...] = jnp.zeros_like(token)

    return _pcall(
        body, name=name,
        out_shape=(pltpu.SemaphoreType.DMA((n_copies,)), pltpu.SemaphoreType.DMA((n_copies,)),
                   pltpu.HBM(src.shape, src.dtype), pltpu.HBM(land_shape, src.dtype), jax.ShapeDtypeStruct((8, LANE), f32)),
        in_specs=(HBM, HBM, ANY), out_specs=(SEM, SEM, HBM, HBM, pl.BlockSpec(memory_space=pltpu.VMEM)),
        input_output_aliases={0: 2, 1: 3},
        compiler_params=pltpu.CompilerParams(has_side_effects=EFFECT),
    )(pltpu.with_memory_space_constraint(src, pltpu.HBM),
      pltpu.with_memory_space_constraint(lax.empty(land_shape, src.dtype), pltpu.HBM), after)


def _spread_wait(started, after, plan, name):
    send_sems, recv_sems, src_thru, land_thru, _ = started

    def body(src_ref, land_ref, send_sems, recv_sems, after_ref, src_dead, got_ref):
        for k in range(SPREAD_COPIES[plan]):
            cp = _spread_copy(src_ref, land_ref, send_sems, recv_sems, k, plan)
            cp.wait_send()
            cp.wait_recv()

    return _pcall(
        body, name=name,
        out_shape=(pltpu.HBM(src_thru.shape, src_thru.dtype), pltpu.HBM(land_thru.shape, land_thru.dtype)),
        in_specs=(HBM, HBM, SEM, SEM, ANY), out_specs=(HBM, HBM), input_output_aliases={0: 0, 1: 1},
        compiler_params=pltpu.CompilerParams(has_side_effects=EFFECT),
    )(src_thru, land_thru, send_sems, recv_sems, after)[1]


def _own_slot(land, block, slot):
    zero = jnp.zeros((), jnp.int32)
    return lax.dynamic_update_slice(land, block[None], (slot.astype(jnp.int32),) + (zero,) * block.ndim)


PIECE_NAT = (0, 3072, 4096, 4112, 5136, 6160, 7184, PROJ_WIDTH)


COL_TILE = 256


def _cast_w_in(w3):
    n = w3.shape[0]

    def body(w_ref, o_ref):
        o_ref[...] = w_ref[:, 0, :].astype(bf16)

    tile = 2 * COL_TILE
    return _pcall(
        body, name="cast_w_in", grid=(D_MODEL // tile,),
        in_specs=[pl.BlockSpec((n, 1, tile), lambda j: (0, 0, j))],
        out_specs=pl.BlockSpec((n, tile), lambda j: (0, j)),
        out_shape=jax.ShapeDtypeStruct((n, D_MODEL), bf16),
        compiler_params=_cparams("parallel"),
    )(w3)


def _relayout_w_in(win_g):
    def body(g_ref, o_ref):
        o_ref[NAT_SMALL_END:NAT_SMALL_END + PAD_COLS, :] = jnp.zeros((PAD_COLS, COL_TILE), o_ref.dtype)
        for d in range(N_DEV):
            n0, n1 = d * SHARD_W, (d + 1) * SHARD_W
            cut = min(max(NAT_SMALL_END - n0, 0), SHARD_W)
            if cut > 0:
                o_ref[n0:n0 + cut, :] = g_ref[d, 0:cut, :]
            if cut < SHARD_W:
                o_ref[n0 + cut + PAD_COLS:n1 + PAD_COLS, :] = g_ref[d, cut:SHARD_W, :]

    return _pcall(
        body, name="relayout_w_in", grid=(D_MODEL // COL_TILE,),
        in_specs=[pl.BlockSpec((N_DEV, SHARD_W, COL_TILE), lambda j: (0, 0, j))],
        out_specs=pl.BlockSpec((PROJ_PAD, COL_TILE), lambda j: (0, j)),
        out_shape=jax.ShapeDtypeStruct((PROJ_PAD, D_MODEL), win_g.dtype),
        compiler_params=_cparams("parallel"),
    )(win_g)


def _grad_blocks(g_parts):
    npc = len(g_parts)

    def body(*refs):
        p_refs, o_ref = refs[:npc], refs[npc]
        for d in range(N_DEV):
            n0, n1 = d * SHARD_W, (d + 1) * SHARD_W
            for i in range(npc):
                lo, hi = max(n0, PIECE_NAT[i]), min(n1, PIECE_NAT[i + 1])
                if lo < hi:
                    o_ref[d, lo - n0:hi - n0, :] = p_refs[i][lo - PIECE_NAT[i]:hi - PIECE_NAT[i], :]

    return _pcall(
        body, name="grad_blocks", grid=(D_MODEL // COL_TILE,),
        in_specs=[pl.BlockSpec((p.shape[0], COL_TILE), lambda j: (0, j)) for p in g_parts],
        out_specs=pl.BlockSpec((N_DEV, SHARD_W, COL_TILE), lambda j: (0, 0, j)),
        out_shape=jax.ShapeDtypeStruct((N_DEV, SHARD_W, D_MODEL), bf16),
        compiler_params=_cparams("parallel"),
    )(*g_parts)


def _in_proj(x, nw, wpad_t):
    L = x.shape[0]
    tn = 768
    nj = wpad_t.shape[0] // tn

    def body(x_ref, nw_ref, w_ref, proj_ref, h_ref):
        @pl.when(pl.program_id(0) == 0)
        def _():
            for r in range(0, L, 256):
                xs = x_ref[r:r + 256, :]
                ms = jnp.mean(xs * xs, axis=-1, keepdims=True)
                h_ref[r:r + 256, :] = ((xs * lax.rsqrt(ms + EPS)) * nw_ref[...]).astype(bf16)
        for r in range(0, L, 512):
            proj_ref[r:r + 512, :] = lax.dot_general(h_ref[r:r + 512, :], w_ref[...], (((1,), (1,)), ((), ())),
                                                     preferred_element_type=f32)

    return _pcall(
        body, name="in_proj", grid=(nj,),
        in_specs=[pl.BlockSpec((L, D_MODEL), lambda j: (0, 0)), pl.BlockSpec((1, D_MODEL), lambda j: (0, 0)),
                  pl.BlockSpec((tn, D_MODEL), lambda j: (j, 0))],
        out_specs=[pl.BlockSpec((L, tn), lambda j: (0, j)), pl.BlockSpec((L, D_MODEL), lambda j: (0, 0))],
        out_shape=[jax.ShapeDtypeStruct((L, wpad_t.shape[0]), f32), jax.ShapeDtypeStruct((L, D_MODEL), bf16)],
        compiler_params=_cparams("arbitrary"),
    )(x, nw, wpad_t)


HALVES = [slice(i * LANE, (i + 1) * LANE) for i in range(ELT_W // LANE)]
QKV_W = 512
QKV_HEADS = [slice(i * LANE, (i + 1) * LANE) for i in range(QKV_W // LANE)]
STEPS_PER_GROUP = GDN_WIDTH // QKV_W


def _conv4(x, cw_ref, ls):
    return (cw_ref[3:4, ls] * x + cw_ref[2:3, ls] * _shift_down(x, 1) + cw_ref[1:2, ls] * _shift_down(x, 2)
            + cw_ref[0:1, ls] * _shift_down(x, 3))


def _qkv_act(proj, cw):
    L = proj.shape[0]

    def body(x_ref, cw_ref, o_ref):
        j = pl.program_id(0)
        scale = jnp.where(j < STEPS_PER_GROUP, HEAD_DIM ** -0.5, 1.0).astype(f32)
        for ls in QKV_HEADS:
            c = _conv4(x_ref[:, ls], cw_ref, ls)
            a = c * _sigmoid(c)
            rn = lax.rsqrt(jnp.sum(a * a, axis=1, keepdims=True) + EPS)
            o_ref[:, ls] = jnp.where(j < 2 * STEPS_PER_GROUP, (a * rn) * scale, a)

    return _pcall(
        body, name="qkv_act", grid=(3 * STEPS_PER_GROUP,),
        in_specs=[pl.BlockSpec((L, QKV_W), lambda j: (0, j)), pl.BlockSpec((4, QKV_W), lambda j: (0, j))],
        out_specs=pl.BlockSpec((L, QKV_W), lambda j: (0, j)),
        out_shape=jax.ShapeDtypeStruct((L, 3 * GDN_WIDTH), f32),
        compiler_params=_cparams("parallel"),
    )(proj, cw)


def _scalars(proj, alog_p, dtb_p):
    L = proj.shape[0]
    nc = L // CHUNK

    def body(x_ref, al_ref, dt_ref, sc_ref, gr_ref):
        x = x_ref[...]
        lane = _lanes(x.shape)
        beta = _sigmoid(x)
        g = -jnp.exp(al_ref[...]) * _softplus(x + dt_ref[...])
        gc = jnp.where((lane >= HEADS) & (lane < 2 * HEADS), g, 0.0)
        rc = _rows(x.shape) & (CHUNK - 1)
        for s in (1, 2, 4, 8, 16, 32):
            gc = gc + jnp.where(rc >= s, pltpu.roll(gc, s, 0), 0.0)
        sc_ref[...] = jnp.where(lane < HEADS, beta, gc)
        sel = (_lanes((HEADS, LANE)) == _rows((HEADS, LANE)) + HEADS).astype(f32)
        for c in range(nc):
            gr_ref[c] = lax.dot_general(sel, sc_ref[c * CHUNK:(c + 1) * CHUNK, :], (((1,), (1,)), ((), ())),
                                        preferred_element_type=f32, precision=lax.Precision.HIGHEST)

    return _pcall(
        body, name="scalars", grid=(1,),
        in_specs=[pl.BlockSpec((L, LANE), lambda i: (0, OFF_BA // LANE)), pl.BlockSpec((1, LANE), lambda i: (0, 0)),
                  pl.BlockSpec((1, LANE), lambda i: (0, 0))],
        out_specs=[pl.BlockSpec((L, LANE), lambda i: (0, 0)), pl.BlockSpec((nc, HEADS, CHUNK), lambda i: (0, 0, 0))],
        out_shape=[jax.ShapeDtypeStruct((L, LANE), f32), jax.ShapeDtypeStruct((nc, HEADS, CHUNK), f32)],
        compiler_params=_cparams("arbitrary"),
    )(proj, alog_p, dtb_p)


def _head_scalars(sc, gr_ref, h, ci=0):
    lane = _lanes(sc.shape)
    beta = jnp.sum(jnp.where(lane == h, sc, 0.0), axis=1, keepdims=True)
    gcc = jnp.sum(jnp.where(lane == HEADS + h, sc, 0.0), axis=1, keepdims=True)
    gcr = gr_ref[ci, h:h + 1, :]
    gl = jnp.sum(jnp.where(_lanes(gcr.shape) == CHUNK - 1, gcr, 0.0), axis=1, keepdims=True)
    ii, jj = _rows((CHUNK, CHUNK)), _lanes((CHUNK, CHUNK))
    dmat = jnp.where(ii >= jj, jnp.exp(jnp.minimum(gcc - gcr, 0.0)), 0.0)
    dmat_t = jnp.where(jj >= ii, jnp.exp(jnp.minimum(gcr - gcc, 0.0)), 0.0)
    return beta, gcc, gl, dmat, dmat_t, ii, jj


def _gdn_fwd(qkv, sc, gr):
    L = qkv.shape[0]
    nc = L // CHUNK
    W = GDN_WIDTH
    cps = GDN_CPS if nc % GDN_CPS == 0 else 1
    rows_per_step = cps * CHUNK

    def body(qkv_ref, sc_ref, gr_ref, o_ref, u_ref, w_ref, vn_ref, t_ref, sp_ref, s_scr):
        @pl.when(pl.program_id(0) == 0)
        def _():
            s_scr[...] = jnp.zeros_like(s_scr)
        HS = range(cps * HEADS)
        hd = [i % HEADS for i in HS]
        rs = [slice((i // HEADS) * CHUNK, (i // HEADS + 1) * CHUNK) for i in HS]
        cs = [slice(hd[i] * HEAD_DIM, (hd[i] + 1) * HEAD_DIM) for i in HS]
        q = [qkv_ref[rs[i], hd[i] * HEAD_DIM:(hd[i] + 1) * HEAD_DIM] for i in HS]
        k = [qkv_ref[rs[i], W + hd[i] * HEAD_DIM:W + (hd[i] + 1) * HEAD_DIM] for i in HS]
        v = [qkv_ref[rs[i], 2 * W + hd[i] * HEAD_DIM:2 * W + (hd[i] + 1) * HEAD_DIM] for i in HS]
        hsc = [_head_scalars(sc_ref[rs[i], :], gr_ref, hd[i], i // HEADS) for i in HS]
        beta, gcc, gl, dmat = ([x[i] for x in hsc] for i in range(4))
        ii, jj = hsc[0][5], hsc[0][6]
        eg = [jnp.exp(gcc[h]) for h in HS]
        kb = [k[h] * beta[h] for h in HS]
        kk = [_mm_nt(kb[h], k[h]) for h in HS]
        qk = [_mm_nt(q[h], k[h]) for h in HS]
        n0 = [-jnp.where(ii > jj, kk[h] * dmat[h], 0.0) for h in HS]
        n1 = [_mm(n0[h], n0[h]) for h in HS]
        n2 = [_mm(n1[h], n1[h]) for h in HS]
        p01 = [n0[h] + n1[h] + _mm(n0[h], n1[h]) for h in HS]
        n3 = [_mm(n2[h], n2[h]) for h in HS]
        n4 = [_mm(n3[h], n3[h]) for h in HS]
        p23 = [n2[h] + n3[h] + _mm(n2[h], n3[h]) for h in HS]
        n5 = [_mm(n4[h], n4[h]) for h in HS]
        p03 = [p01[h] + p23[h] + _mm(p01[h], p23[h]) for h in HS]
        p45 = [n4[h] + n5[h] + _mm(n4[h], n5[h]) for h in HS]
        t = [p03[h] + p45[h] + _mm(p03[h], p45[h]) for h in HS]
        vb = [v[h] * beta[h] for h in HS]
        kbg = [kb[h] * eg[h] for h in HS]
        uw = [_mm(t[h], jnp.concatenate([vb[h], kbg[h]], axis=1)) for h in HS]
        u = [vb[h] + uw[h][:, :HEAD_DIM] for h in HS]
        w = [kbg[h] + uw[h][:, HEAD_DIM:] for h in HS]
        wq = [jnp.concatenate([w[h], q[h] * eg[h]], axis=0) for h in HS]
        p = [jnp.where(ii >= jj, qk[h] * dmat[h], 0.0) for h in HS]
        ks = [k[h] * jnp.exp(gl[h] - gcc[h]) for h in HS]
        s = [s_scr[h] for h in range(HEADS)]
        for ci in range(cps):
            IS = range(ci * HEADS, (ci + 1) * HEADS)
            ws = [_mm(wq[i], s[hd[i]]) for i in IS]
            vn = [u[i] - ws[hd[i]][:CHUNK] for i in IS]
            pv = [_mm(p[i], vn[hd[i]]) for i in IS]
            kv = [_mm_tn(ks[i], vn[hd[i]]) for i in IS]
            for i in IS:
                h = hd[i]
                sp_ref[ci, cs[i], :] = s[h]
                o_ref[rs[i], cs[i]] = ws[h][CHUNK:] + pv[h]
                vn_ref[rs[i], cs[i]] = vn[h].astype(bf16)
            s = [jnp.exp(gl[i]) * s[hd[i]] + kv[hd[i]] for i in IS]
        for h in range(HEADS):
            s_scr[h] = s[h]
        for i in HS:
            u_ref[rs[i], cs[i]] = u[i].astype(bf16)
            w_ref[rs[i], cs[i]] = w[i].astype(bf16)
            t_ref[i // HEADS, hd[i]] = t[i].astype(bf16)

    row = lambda c: (c, 0)
    act, act16 = jax.ShapeDtypeStruct((L, W), f32), jax.ShapeDtypeStruct((L, W), bf16)
    return _pcall(
        body, name="gdn_fwd", grid=(nc // cps,),
        in_specs=[pl.BlockSpec((rows_per_step, 3 * W), row), pl.BlockSpec((rows_per_step, LANE), row),
                  pl.BlockSpec((cps, HEADS, CHUNK), lambda c: (c, 0, 0))],
        out_specs=[pl.BlockSpec((rows_per_step, W), row)] * 4 + [
            pl.BlockSpec((cps, HEADS, CHUNK, CHUNK), lambda c: (c, 0, 0, 0)),
            pl.BlockSpec((cps, W, HEAD_DIM), lambda c: (c, 0, 0))],
        out_shape=[act, act16, act16, act16, jax.ShapeDtypeStruct((nc, HEADS, CHUNK, CHUNK), bf16),
                   jax.ShapeDtypeStruct((nc, W, HEAD_DIM), f32)],
        scratch_shapes=[pltpu.VMEM((HEADS, HEAD_DIM, HEAD_DIM), f32)],
        compiler_params=_cparams("arbitrary"),
    )(qkv, sc, gr)


def _gdn_gate(o, proj, gnw):
    L = o.shape[0]

    def body(o_ref, z_ref, w_ref, m_ref):
        for ls in HALVES:
            ov, z = o_ref[:, ls], z_ref[:, ls]
            rms = lax.rsqrt(jnp.mean(ov * ov, axis=-1, keepdims=True) + EPS)
            m_ref[:, ls] = (((ov * rms) * w_ref[...]) * (z * _sigmoid(z))).astype(bf16)

    return _pcall(
        body, name="gdn_gate", grid=(GDN_WIDTH // ELT_W,),
        in_specs=[pl.BlockSpec((L, ELT_W), lambda j: (0, j)), pl.BlockSpec((L, ELT_W), lambda j: (0, OFF_ZG // ELT_W + j)),
                  pl.BlockSpec((1, LANE), lambda j: (0, 0))],
        out_specs=pl.BlockSpec((L, ELT_W), lambda j: (0, j)),
        out_shape=jax.ShapeDtypeStruct((L, GDN_WIDTH), bf16),
        compiler_params=_cparams("parallel"),
    )(o, proj, gnw)


def _conv3(u, cw_ref, ls):
    return cw_ref[2:3, ls] * u + cw_ref[1:2, ls] * _shift_down(u, 1) + cw_ref[0:1, ls] * _shift_down(u, 2)


def _conv_specs(L):
    blk = lambda off: pl.BlockSpec((L, ELT_W), lambda j, off=off: (0, off // ELT_W + j))
    return [blk(OFF_B), blk(OFF_C), blk(OFF_HC), blk(OFF_ZC),
            pl.BlockSpec((3, ELT_W), lambda j: (0, j)), pl.BlockSpec((1, ELT_W), lambda j: (0, j))]


def _conv_fwd(proj, cw, cb):
    L = proj.shape[0]

    def body(b_ref, c_ref, h_ref, z_ref, cw_ref, cb_ref, m_ref):
        for ls in HALVES:
            z = z_ref[:, ls]
            cv = _conv3(c_ref[:, ls] * h_ref[:, ls], cw_ref, ls) + cb_ref[:, ls]
            m_ref[:, ls] = ((b_ref[:, ls] * cv) * (z * _sigmoid(z))).astype(bf16)

    return _pcall(
        body, name="conv_fwd", grid=(CONV_WIDTH // ELT_W,),
        in_specs=_conv_specs(L), out_specs=pl.BlockSpec((L, ELT_W), lambda j: (0, j)),
        out_shape=jax.ShapeDtypeStruct((L, CONV_WIDTH), bf16),
        compiler_params=_cparams("parallel"),
    )(proj, proj, proj, proj, cw, cb)


def _out_proj_loss(x, mix_a, mix_b, wo, fw, tgt):
    L = x.shape[0]
    tm = min(512, L)

    def body(x_ref, ma_ref, mb_ref, wo_ref, fw_ref, t_ref, dy_ref, dyb_ref, dma_ref, dmb_ref, gfw_ref, loss_ref):
        @pl.when(pl.program_id(0) == 0)
        def _():
            gfw_ref[...] = jnp.zeros_like(gfw_ref)
            loss_ref[...] = jnp.zeros_like(loss_ref)
        y = x_ref[...] + jnp.dot(ma_ref[...], wo_ref[:GDN_WIDTH, :], preferred_element_type=f32) \
            + jnp.dot(mb_ref[...], wo_ref[GDN_WIDTH:, :], preferred_element_type=f32)
        r = lax.rsqrt(jnp.mean(y * y, axis=-1, keepdims=True) + EPS)
        yh = y * r
        fwv = fw_ref[...]
        diff = yh * fwv - t_ref[...]
        loss_ref[...] += jnp.sum(jnp.sum(diff * diff, axis=-1, keepdims=True), axis=0, keepdims=True) * (0.5 / D_MODEL)
        dout = diff * (1.0 / D_MODEL)
        gfw_ref[...] += jnp.sum(dout * yh, axis=0, keepdims=True)
        dyh = dout * fwv
        dy = r * (dyh - yh * jnp.mean(dyh * yh, axis=-1, keepdims=True))
        dy_ref[...] = dy
        dyb = dy.astype(bf16)
        dyb_ref[...] = dyb
        dma_ref[...] = lax.dot_general(dyb, wo_ref[:GDN_WIDTH, :], (((1,), (1,)), ((), ())), preferred_element_type=f32)
        dmb_ref[...] = lax.dot_general(dyb, wo_ref[GDN_WIDTH:, :], (((1,), (1,)), ((), ())), preferred_element_type=f32)

    row = lambda i: (i, 0)
    fix = lambda i: (0, 0)
    act = jax.ShapeDtypeStruct((L, D_MODEL), f32)
    return _pcall(
        body, name="out_proj_loss", grid=(L // tm,),
        in_specs=[pl.BlockSpec((tm, D_MODEL), row), pl.BlockSpec((tm, GDN_WIDTH), row), pl.BlockSpec((tm, CONV_WIDTH), row),
                  pl.BlockSpec((GDN_WIDTH + CONV_WIDTH, D_MODEL), fix), pl.BlockSpec((1, D_MODEL), fix),
                  pl.BlockSpec((tm, D_MODEL), row)],
        out_specs=[pl.BlockSpec((tm, D_MODEL), row), pl.BlockSpec((tm, D_MODEL), row), pl.BlockSpec((tm, GDN_WIDTH), row),
                   pl.BlockSpec((tm, CONV_WIDTH), row), pl.BlockSpec((1, D_MODEL), fix), pl.BlockSpec((1, LANE), fix)],
        out_shape=[act, jax.ShapeDtypeStruct((L, D_MODEL), bf16), act, act,
                   jax.ShapeDtypeStruct((1, D_MODEL), f32), jax.ShapeDtypeStruct((1, LANE), f32)],
        compiler_params=_cparams("arbitrary"),
    )(x, mix_a, mix_b, wo, fw, tgt)


def _tn_matmul(a, b, name):
    L, M = a.shape
    N = b.shape[1]
    tm = 512 if M % 512 == 0 else M

    def body(a_ref, b_ref, o_ref):
        o_ref[...] = lax.dot_general(a_ref[...], b_ref[...], (((0,), (0,)), ((), ())),
                                     preferred_element_type=f32).astype(o_ref.dtype)

    return _pcall(
        body, name=name, grid=(M // tm,),
        in_specs=[pl.BlockSpec((L, tm), lambda i: (0, i)), pl.BlockSpec((L, N), lambda i: (0, 0))],
        out_specs=pl.BlockSpec((tm, N), lambda i: (i, 0)),
        out_shape=jax.ShapeDtypeStruct((M, N), bf16),
        compiler_params=_cparams("parallel"),
    )(a, b)


def _gdn_gate_bwd(o, proj, gnw, dmix_a):
    L = o.shape[0]

    def body(o_ref, z_ref, w_ref, dm_ref, do_ref, dz_ref, gw_ref):
        @pl.when(pl.program_id(0) == 0)
        def _():
            gw_ref[...] = jnp.zeros_like(gw_ref)
        wv = w_ref[...]
        for ls in HALVES:
            ov, z, dm = o_ref[:, ls], z_ref[:, ls], dm_ref[:, ls]
            rms = lax.rsqrt(jnp.mean(ov * ov, axis=-1, keepdims=True) + EPS)
            xh = ov * rms
            sg = _sigmoid(z)
            d_on = dm * (z * sg)
            dz_ref[:, ls] = (dm * (xh * wv) * (sg * (1.0 + z * (1.0 - sg)))).astype(bf16)
            gw_ref[...] += jnp.sum(d_on * xh, axis=0, keepdims=True)
            dxh = d_on * wv
            do_ref[:, ls] = (rms * (dxh - xh * jnp.mean(dxh * xh, axis=-1, keepdims=True))).astype(bf16)

    wide = pl.BlockSpec((L, ELT_W), lambda j: (0, j))
    return _pcall(
        body, name="gdn_gate_bwd", grid=(GDN_WIDTH // ELT_W,),
        in_specs=[wide, pl.BlockSpec((L, ELT_W), lambda j: (0, OFF_ZG // ELT_W + j)),
                  pl.BlockSpec((1, LANE), lambda j: (0, 0)), wide],
        out_specs=[wide, wide, pl.BlockSpec((1, LANE), lambda j: (0, 0))],
        out_shape=[jax.ShapeDtypeStruct((L, GDN_WIDTH), bf16), jax.ShapeDtypeStruct((L, GDN_WIDTH), bf16),
                   jax.ShapeDtypeStruct((1, LANE), f32)],
        compiler_params=_cparams("arbitrary"),
    )(o, proj, gnw, dmix_a)


def _conv_bwd(proj, cw, cb, dmix_b):
    L = proj.shape[0]

    def body(b_ref, c_ref, h_ref, z_ref, cw_ref, cb_ref, dm_ref, db_ref, dc_ref, dh_ref, dz_ref, gcw_ref, gcb_ref):
        for ls in HALVES:
            bv, cv_, hv, z, dm = b_ref[:, ls], c_ref[:, ls], h_ref[:, ls], z_ref[:, ls], dm_ref[:, ls]
            u = cv_ * hv
            cv = _conv3(u, cw_ref, ls) + cb_ref[:, ls]
            sg = _sigmoid(z)
            sz = z * sg
            db_ref[:, ls] = (dm * cv * sz).astype(bf16)
            dz_ref[:, ls] = (dm * (bv * cv) * (sg * (1.0 + z * (1.0 - sg)))).astype(bf16)
            dcv = dm * bv * sz
            gcb_ref[:, ls] = jnp.sum(dcv, axis=0, keepdims=True)
            dcv1, dcv2 = _shift_up(dcv, 1), _shift_up(dcv, 2)
            gcw_ref[2:3, ls] = jnp.sum(dcv * u, axis=0, keepdims=True)
            gcw_ref[1:2, ls] = jnp.sum(dcv1 * u, axis=0, keepdims=True)
            gcw_ref[0:1, ls] = jnp.sum(dcv2 * u, axis=0, keepdims=True)
            du = cw_ref[2:3, ls] * dcv + cw_ref[1:2, ls] * dcv1 + cw_ref[0:1, ls] * dcv2
            dc_ref[:, ls] = (du * hv).astype(bf16)
            dh_ref[:, ls] = (du * cv_).astype(bf16)

    col = pl.BlockSpec((L, ELT_W), lambda j: (0, j))
    act = jax.ShapeDtypeStruct((L, CONV_WIDTH), bf16)
    return _pcall(
        body, name="conv_bwd", grid=(CONV_WIDTH // ELT_W,),
        in_specs=_conv_specs(L) + [col],
        out_specs=[col, col, col, col, pl.BlockSpec((3, ELT_W), lambda j: (0, j)), pl.BlockSpec((1, ELT_W), lambda j: (0, j))],
        out_shape=[act, act, act, act, jax.ShapeDtypeStruct((3, CONV_WIDTH), f32), jax.ShapeDtypeStruct((1, CONV_WIDTH), f32)],
        compiler_params=_cparams("parallel"),
    )(proj, proj, proj, proj, cw, cb, dmix_b)


def _gdn_bwd(qkv, sc, gr, u_all, w_all, vn_all, t_all, sp_all, do_all):
    L = qkv.shape[0]
    nc = L // CHUNK
    W = GDN_WIDTH
    cps = GDN_CPS_BWD if nc % GDN_CPS_BWD == 0 else 1
    rows_per_step = cps * CHUNK
    nsteps = nc // cps

    def body(qkv_ref, sc_ref, gr_ref, u_ref, w_ref, vn_ref, t_ref, sp_ref, do_ref, dqkv_ref, dsc_ref, dgr_ref, ds_scr):
        @pl.when(pl.program_id(0) == 0)
        def _():
            ds_scr[...] = jnp.zeros_like(ds_scr)
        HS = range(cps * HEADS)
        hd = [i % HEADS for i in HS]
        rs = [slice((i // HEADS) * CHUNK, (i // HEADS + 1) * CHUNK) for i in HS]
        cs = [slice(hd[i] * HEAD_DIM, (hd[i] + 1) * HEAD_DIM) for i in HS]
        q = [qkv_ref[rs[i], hd[i] * HEAD_DIM:(hd[i] + 1) * HEAD_DIM] for i in HS]
        k = [qkv_ref[rs[i], W + hd[i] * HEAD_DIM:W + (hd[i] + 1) * HEAD_DIM] for i in HS]
        v = [qkv_ref[rs[i], 2 * W + hd[i] * HEAD_DIM:2 * W + (hd[i] + 1) * HEAD_DIM] for i in HS]
        hsc = [_head_scalars(sc_ref[rs[i], :], gr_ref, hd[i], i // HEADS) for i in HS]
        beta, gcc, gl, dmat, dmat_t = ([x[i] for x in hsc] for i in range(5))
        ii, jj = hsc[0][5], hsc[0][6]
        eg = [jnp.exp(gcc[h]) for h in HS]
        ekl = [jnp.exp(gl[h] - gcc[h]) for h in HS]
        egl = [jnp.exp(gl[h]) for h in HS]
        kb = [k[h] * beta[h] for h in HS]
        ks = [k[h] * ekl[h] for h in HS]
        do = [do_ref[rs[h], cs[h]] for h in HS]
        vn = [vn_ref[rs[h], cs[h]] for h in HS]
        s = [sp_ref[h // HEADS, cs[h], :] for h in HS]
        w = [w_ref[rs[h], cs[h]] for h in HS]
        qd = [q[h] * eg[h] for h in HS]

        kq = [_mm_nt(k[h], q[h]) for h in HS]
        p_t = [jnp.where(jj >= ii, kq[h] * dmat_t[h], 0.0) for h in HS]
        ptd = [_mm(p_t[h], do[h]) for h in HS]
        qw =[jnp.concatenate([qd[h], -w[h]], axis=0) for h in HS]
        dsn, dvn, dodv = [None] * len(HS), [None] * len(HS), [None] * len(HS)
        ds_cur = [ds_scr[h] for h in range(HEADS)]
        for ci in reversed(range(cps)):
            IS = range(ci * HEADS, (ci + 1) * HEADS)
            ksd = [_mm(ks[i], ds_cur[hd[i]]) for i in IS]
            for i in IS:
                dsn[i] = ds_cur[hd[i]]
                dvn[i] = ptd[i] + ksd[hd[i]]
                dodv[i] = jnp.concatenate([do[i], dvn[i]], axis=0)
            dsq = [_mm_tn(qw[i], dodv[i]) for i in IS]
            ds_cur = [egl[i] * ds_cur[hd[i]] + dsq[hd[i]] for i in IS]
        for h in range(HEADS):
            ds_scr[h] = ds_cur[h]
        x1 = [_mm_nt(dodv[h], s[h]) for h in HS]
        dks = [_mm_nt(vn[h], dsn[h]) for h in HS]
        dov = [_mm_nt(do[h], vn[h]) for h in HS]
        vdo = [_mm_nt(vn[h], do[h]) for h in HS]
        kk = [_mm_nt(kb[h], k[h]) for h in HS]
        qk = [_mm_nt(q[h], k[h]) for h in HS]
        dgl = [egl[h] * jnp.sum(jnp.sum(s[h] * dsn[h], axis=1, keepdims=True), axis=0, keepdims=True) for h in HS]
        dqd = [x1[h][:CHUNK] for h in HS]
        duw = [jnp.concatenate([dvn[h], -x1[h][CHUNK:]], axis=1) for h in HS]
        tdu = [_mm_tn(t_ref[h // HEADS, hd[h]], duw[h]) for h in HS]
        dvk = [duw[h] + tdu[h] for h in HS]
        uw = [jnp.concatenate([u_ref[rs[h], cs[h]], w[h]], axis=1) for h in HS]
        da = [-jnp.where(ii > jj, _mm_nt(dvk[h], uw[h]), 0.0) for h in HS]
        da_t = [-jnp.where(jj > ii, _mm_nt(uw[h], dvk[h]), 0.0) for h in HS]
        dp = [jnp.where(ii >= jj, dov[h], 0.0) for h in HS]
        dp_t = [jnp.where(jj >= ii, vdo[h], 0.0) for h in HS]
        r1 = [_mm(jnp.concatenate([da[h] * dmat[h], dp[h] * dmat[h]], axis=0), k[h]) for h in HS]
        dk1 = [_mm(jnp.concatenate([da_t[h] * dmat_t[h], dp_t[h] * dmat_t[h]], axis=1),
                   jnp.concatenate([kb[h], q[h]], axis=0)) for h in HS]
        lane = _lanes((CHUNK, LANE))
        for ci in range(cps):
            dsc = jnp.zeros((CHUNK, LANE), f32)
            for i in range(ci * HEADS, (ci + 1) * HEADS):
                h = hd[i]
                a = jnp.where(ii > jj, kk[i] * dmat[i], 0.0)
                p = jnp.where(ii >= jj, qk[i] * dmat[i], 0.0)
                gmat = da[i] * a + dp[i] * p
                dvb, dkbg = dvk[i][:, :HEAD_DIM], dvk[i][:, HEAD_DIM:]
                kbg = kb[i] * eg[i]
                dkb = r1[i][:CHUNK] + dkbg * eg[i]
                dq = r1[i][CHUNK:] + dqd[i] * eg[i]
                dk = dk1[i] + dks[i] * ekl[i] + dkb * beta[i]
                dbeta = jnp.sum(dkb * k[i] + dvb * v[i], axis=1, keepdims=True)
                ksum = jnp.sum(dks[i] * ks[i], axis=1, keepdims=True)
                dgl_tot = dgl[i] + jnp.sum(ksum, axis=0, keepdims=True)
                dgc = (jnp.sum(gmat, axis=1, keepdims=True) + jnp.sum(dqd[i] * qd[i] + dkbg * kbg, axis=1, keepdims=True)
                       - ksum)
                dgc = dgc + jnp.where(_rows(dgc.shape) == CHUNK - 1, dgl_tot, 0.0)
                dqkv_ref[rs[i], h * HEAD_DIM:(h + 1) * HEAD_DIM] = dq
                dqkv_ref[rs[i], W + h * HEAD_DIM:W + (h + 1) * HEAD_DIM] = dk
                dqkv_ref[rs[i], 2 * W + h * HEAD_DIM:2 * W + (h + 1) * HEAD_DIM] = dvb * beta[i]
                dsc = jnp.where(lane == h, dbeta, jnp.where(lane == HEADS + h, dgc, dsc))
                dgr_ref[ci, h:h + 1, :] = jnp.sum(gmat, axis=0, keepdims=True)
            dsc_ref[ci * CHUNK:(ci + 1) * CHUNK, :] = dsc

    row = lambda c: (nsteps - 1 - c, 0)
    lead3 = lambda c: (nsteps - 1 - c, 0, 0)
    return _pcall(
        body, name="gdn_bwd", grid=(nsteps,),
        in_specs=[pl.BlockSpec((rows_per_step, 3 * W), row), pl.BlockSpec((rows_per_step, LANE), row),
                  pl.BlockSpec((cps, HEADS, CHUNK), lead3),
                  pl.BlockSpec((rows_per_step, W), row), pl.BlockSpec((rows_per_step, W), row),
                  pl.BlockSpec((rows_per_step, W), row),
                  pl.BlockSpec((cps, HEADS, CHUNK, CHUNK), lambda c: (nsteps - 1 - c, 0, 0, 0)),
                  pl.BlockSpec((cps, W, HEAD_DIM), lead3), pl.BlockSpec((rows_per_step, W), row)],
        out_specs=[pl.BlockSpec((rows_per_step, 3 * W), row), pl.BlockSpec((rows_per_step, LANE), row),
                   pl.BlockSpec((cps, HEADS, CHUNK), lead3)],
        out_shape=[jax.ShapeDtypeStruct((L, 3 * W), f32), jax.ShapeDtypeStruct((L, LANE), f32),
                   jax.ShapeDtypeStruct((nc, HEADS, CHUNK), f32)],
        scratch_shapes=[pltpu.VMEM((HEADS, HEAD_DIM, HEAD_DIM), f32)],
        compiler_params=_cparams("arbitrary"),
    )(qkv, sc, gr, u_all, w_all, vn_all, t_all, sp_all, do_all)


def _qkv_bwd(proj, cw, dn):
    L = proj.shape[0]

    def body(x_ref, cw_ref, dn_ref, dx_ref, gcw_ref):
        j = pl.program_id(0)
        scale = jnp.where(j < STEPS_PER_GROUP, HEAD_DIM ** -0.5, 1.0).astype(f32)
        for ls in QKV_HEADS:
            x, dn_v = x_ref[:, ls], dn_ref[:, ls]
            c = _conv4(x, cw_ref, ls)
            sg = _sigmoid(c)
            a = c * sg
            rn = lax.rsqrt(jnp.sum(a * a, axis=1, keepdims=True) + EPS)
            da_n = (scale * rn) * (dn_v - a * ((rn * rn) * jnp.sum(dn_v * a, axis=1, keepdims=True)))
            da = jnp.where(j < 2 * STEPS_PER_GROUP, da_n, dn_v)
            dc = da * (sg * (1.0 + c * (1.0 - sg)))
            dc1, dc2, dc3 = _shift_up(dc, 1), _shift_up(dc, 2), _shift_up(dc, 3)
            gcw_ref[3:4, ls] = jnp.sum(dc * x, axis=0, keepdims=True)
            gcw_ref[2:3, ls] = jnp.sum(dc1 * x, axis=0, keepdims=True)
            gcw_ref[1:2, ls] = jnp.sum(dc2 * x, axis=0, keepdims=True)
            gcw_ref[0:1, ls] = jnp.sum(dc3 * x, axis=0, keepdims=True)
            dx = cw_ref[3:4, ls] * dc + cw_ref[2:3, ls] * dc1 + cw_ref[1:2, ls] * dc2 + cw_ref[0:1, ls] * dc3
            dx_ref[:, ls] = dx.astype(bf16)

    col = pl.BlockSpec((L, QKV_W), lambda j: (0, j))
    wspec = pl.BlockSpec((4, QKV_W), lambda j: (0, j))
    return _pcall(
        body, name="qkv_bwd", grid=(3 * STEPS_PER_GROUP,),
        in_specs=[col, wspec, col], out_specs=[col, wspec],
        out_shape=[jax.ShapeDtypeStruct((L, 3 * GDN_WIDTH), bf16), jax.ShapeDtypeStruct((4, 3 * GDN_WIDTH), f32)],
        compiler_params=_cparams("parallel"),
    )(proj, cw, dn)


def _scalars_bwd(proj, alog_p, dtb_p, dsc, dgr_col):
    L = proj.shape[0]

    def body(x_ref, al_ref, dt_ref, dsc_ref, dgr_ref, dba_ref, gs_ref):
        x, dsc_v = x_ref[...], dsc_ref[...]
        lane = _lanes(x.shape)
        dec = (lane >= HEADS) & (lane < 2 * HEADS)
        dg = jnp.where(dec, dsc_v - dgr_ref[...], 0.0)
        rc = _rows(x.shape) & (CHUNK - 1)
        for s in (1, 2, 4, 8, 16, 32):
            dg = dg + jnp.where(rc + s < CHUNK, pltpu.roll(dg, L - s, 0), 0.0)
        xa = x + dt_ref[...]
        ea = jnp.exp(al_ref[...])
        g = -ea * _softplus(xa)
        da = dg * (-ea) * _sigmoid(xa)
        beta = _sigmoid(x)
        db = dsc_v * beta * (1.0 - beta)
        dba_ref[...] = jnp.where(lane < HEADS, db, jnp.where(dec, da, 0.0)).astype(bf16)
        g_al = jnp.sum(jnp.where(dec, dg * g, 0.0), axis=0, keepdims=True)
        g_dt = jnp.sum(jnp.where(dec, da, 0.0), axis=0, keepdims=True)
        row8 = _rows(gs_ref.shape)
        gs = jnp.where(row8 == 0, g_al, jnp.where(row8 == 1, g_dt, 0.0))
        gs_ref[...] = pltpu.roll(gs, LANE - HEADS, 1)

    full = pl.BlockSpec((L, LANE), lambda i: (0, 0))
    vec = pl.BlockSpec((1, LANE), lambda i: (0, 0))
    return _pcall(
        body, name="scalars_bwd", grid=(1,),
        in_specs=[pl.BlockSpec((L, LANE), lambda i: (0, OFF_BA // LANE)), vec, vec, full, full],
        out_specs=[full, pl.BlockSpec((8, LANE), lambda i: (0, 0))],
        out_shape=[jax.ShapeDtypeStruct((L, LANE), bf16), jax.ShapeDtypeStruct((8, LANE), f32)],
        compiler_params=_cparams("arbitrary"),
    )(proj, alog_p, dtb_p, dsc, dgr_col)


def _input_grad(pieces, offs, wpad, x, nw, dy):
    L = x.shape[0]
    tm = min(512, L)
    npc = len(pieces)

    def body(*refs):
        p_refs = refs[:npc]
        w_hbm, x_ref, nw_ref, dy_ref, gx_ref, gnw_ref, w_vmem, sems = refs[npc:]
        first = pl.program_id(0) == 0
        loads = [pltpu.make_async_copy(w_hbm.at[off:off + p.shape[1], :], w_vmem.at[off:off + p.shape[1], :], sems.at[k])
                 for k, (p, off) in enumerate(zip(p_refs, offs))]

        @pl.when(first)
        def _():
            for cp in loads:
                cp.start()
            gnw_ref[...] = jnp.zeros_like(gnw_ref)
        dh = None
        for k, (p_ref, off) in enumerate(zip(p_refs, offs)):
            wd = p_ref.shape[1]
            pl.when(first)(loads[k].wait)
            part = jnp.dot(p_ref[...], w_vmem[off:off + wd, :], preferred_element_type=f32)
            dh = part if dh is None else dh + part
        xv, nwv = x_ref[...], nw_ref[...]
        r = lax.rsqrt(jnp.mean(xv * xv, axis=-1, keepdims=True) + EPS)
        xh = xv * r
        gnw_ref[...] += jnp.sum(dh * xh, axis=0, keepdims=True)
        dxh = dh * nwv
        gx_ref[...] = dy_ref[...] + r * (dxh - xh * jnp.mean(dxh * xh, axis=-1, keepdims=True))

    row = lambda i: (i, 0)
    fix = lambda i: (0, 0)
    return _pcall(
        body, name="input_grad", grid=(L // tm,),
        in_specs=[pl.BlockSpec((tm, p.shape[1]), row) for p in pieces] + [
            ANY, pl.BlockSpec((tm, D_MODEL), row), pl.BlockSpec((1, D_MODEL), fix), pl.BlockSpec((tm, D_MODEL), row)],
        out_specs=[pl.BlockSpec((tm, D_MODEL), row), pl.BlockSpec((1, D_MODEL), fix)],
        out_shape=[jax.ShapeDtypeStruct((L, D_MODEL), f32), jax.ShapeDtypeStruct((1, D_MODEL), f32)],
        scratch_shapes=[pltpu.VMEM(wpad.shape, bf16), pltpu.SemaphoreType.DMA((npc,))],
        compiler_params=_cparams("arbitrary"),
    )(*pieces, wpad, x, nw, dy)


def _adamw_reduce(parts, w, m, v, name):
    R, C = w.shape
    n_parts = parts.shape[0]
    tr = 128 if R % 128 == 0 else R
    c1 = 1.0 - ADAM_B1 ** ADAM_STEP
    c2 = 1.0 - ADAM_B2 ** ADAM_STEP

    def body(p_ref, w_ref, m_ref, v_ref, g_ref, d_ref, nm_ref, nv_ref):
        g = p_ref[0].astype(f32)
        for s in range(1, n_parts):
            g = g + p_ref[s].astype(f32)
        nm = ADAM_B1 * m_ref[...] + (1.0 - ADAM_B1) * g
        nv = ADAM_B2 * v_ref[...] + (1.0 - ADAM_B2) * (g * g)
        g_ref[...] = g
        nm_ref[...] = nm
        nv_ref[...] = nv
        d_ref[...] = -ADAM_LR * ((nm / c1) / (jnp.sqrt(nv / c2) + ADAM_EPS) + ADAM_WD * w_ref[...])

    blk = pl.BlockSpec((tr, C), lambda i: (i, 0))
    out = jax.ShapeDtypeStruct((R, C), f32)
    return _pcall(
        body, name=name, grid=(R // tr,),
        in_specs=[pl.BlockSpec((n_parts, tr, C), lambda i: (0, i, 0)), blk, blk, blk],
        out_specs=[blk] * 4, out_shape=[out] * 4,
        compiler_params=_cparams("parallel"),
    )(parts, w, m, v)


SMALL_SLOTS = ((0, D_MODEL), (D_MODEL, D_MODEL), (2 * D_MODEL, D_MODEL), (3 * D_MODEL, LANE),
               (3 * D_MODEL + LANE, HEADS), (3 * D_MODEL + 2 * LANE, HEADS))
SMALL_LOSS = 3 * D_MODEL + 3 * LANE
SMALL_W = SMALL_LOSS + LANE


def _pack_small(gs, after):
    def body(nw_ref, cb_ref, fw_ref, gn_ref, sc_ref, ls_ref, after_ref, o_ref):
        for ref, (start, width) in zip((nw_ref, cb_ref, fw_ref, gn_ref), SMALL_SLOTS[:4]):
            o_ref[:, start:start + width] = ref[...]
        o_ref[:, SMALL_SLOTS[4][0]:SMALL_SLOTS[4][0] + LANE] = sc_ref[0:1, :]
        o_ref[:, SMALL_SLOTS[5][0]:SMALL_SLOTS[5][0] + LANE] = sc_ref[1:2, :]
        o_ref[:, SMALL_LOSS:SMALL_W] = ls_ref[...]

    vm = pl.BlockSpec(memory_space=pltpu.VMEM)
    return _pcall(body, name="pack_small_grads", out_shape=jax.ShapeDtypeStruct((1, SMALL_W), f32),
                  in_specs=[vm] * 6 + [ANY], out_specs=vm)(*gs, after)


def _adamw_small(parts, ws, ms, vs):
    c1 = 1.0 - ADAM_B1 ** ADAM_STEP
    c2 = 1.0 - ADAM_B2 ** ADAM_STEP
    np_ = len(ws)

    def body(*refs):
        p_ref = refs[0]
        w_refs, m_refs, v_refs = refs[1:1 + np_], refs[1 + np_:1 + 2 * np_], refs[1 + 2 * np_:1 + 3 * np_]
        outs = refs[1 + 3 * np_:]
        g_refs, d_refs, nm_refs, nv_refs = (outs[i * np_:(i + 1) * np_] for i in range(4))
        loss_ref = outs[4 * np_]

        def total(start, width):
            t = p_ref[0, :, start:start + width]
            for s in range(1, N_DEV):
                t = t + p_ref[s, :, start:start + width]
            return t

        for i, (start, width) in enumerate(SMALL_SLOTS):
            g = total(start, width)
            nm = ADAM_B1 * m_refs[i][...] + (1.0 - ADAM_B1) * g
            nv = ADAM_B2 * v_refs[i][...] + (1.0 - ADAM_B2) * (g * g)
            g_refs[i][...] = g
            nm_refs[i][...] = nm
            nv_refs[i][...] = nv
            d_refs[i][...] = -ADAM_LR * ((nm / c1) / (jnp.sqrt(nv / c2) + ADAM_EPS) + ADAM_WD * w_refs[i][...])
        loss_ref[...] = total(SMALL_LOSS, LANE)

    vm = pl.BlockSpec(memory_space=pltpu.VMEM)
    shapes = [jax.ShapeDtypeStruct(w.shape, f32) for w in ws]
    res = _pcall(body, name="adamw_small", out_shape=shapes * 4 + [jax.ShapeDtypeStruct((1, LANE), f32)],
                 in_specs=[vm] * (1 + 3 * np_), out_specs=[vm] * (4 * np_ + 1))(parts, *ws, *ms, *vs)
    return [res[i * np_:(i + 1) * np_] for i in range(4)], res[4 * np_]


def _adamw_w_in(parts, w3, m3, v3):
    n_parts, n, _ = parts.shape
    c1 = 1.0 - ADAM_B1 ** ADAM_STEP
    c2 = 1.0 - ADAM_B2 ** ADAM_STEP

    def body(p_ref, w_ref, m_ref, v_ref, g_ref, d_ref, nm_ref, nv_ref):
        g = p_ref[0].astype(f32)
        for s in range(1, n_parts):
            g = g + p_ref[s].astype(f32)
        nm = ADAM_B1 * m_ref[:, 0, :] + (1.0 - ADAM_B1) * g
        nv = ADAM_B2 * v_ref[:, 0, :] + (1.0 - ADAM_B2) * (g * g)
        g_ref[:, 0, :] = g
        nm_ref[:, 0, :] = nm
        nv_ref[:, 0, :] = nv
        d_ref[:, 0, :] = -ADAM_LR * ((nm / c1) / (jnp.sqrt(nv / c2) + ADAM_EPS) + ADAM_WD * w_ref[:, 0, :])

    tile = 2 * COL_TILE
    blk = pl.BlockSpec((n, 1, tile), lambda j: (0, 0, j))
    out = jax.ShapeDtypeStruct((n, 1, D_MODEL), f32)
    return _pcall(
        body, name="adamw_w_in", grid=(D_MODEL // tile,),
        in_specs=[pl.BlockSpec((n_parts, n, tile), lambda j: (0, 0, j)), blk, blk, blk],
        out_specs=[blk] * 4, out_shape=[out] * 4,
        compiler_params=_cparams("parallel"),
    )(parts, w3, m3, v3)


def _pad_lanes(vec8, start):
    return jnp.pad(vec8.reshape(1, -1), ((0, 0), (start, LANE - start - vec8.size)))


def kernel(x, norm_in_w, w_in, conv_qkv_w, A_log, dt_bias, gdn_norm_w, conv_w, conv_b, w_out, final_norm_w, loss_target, m_norm_in_w, m_w_in, m_conv_qkv_w, m_A_log, m_dt_bias, m_gdn_norm_w, m_conv_w, m_conv_b, m_w_out, m_final_norm_w, v_norm_in_w, v_w_in, v_conv_qkv_w, v_A_log, v_dt_bias, v_gdn_norm_w, v_conv_w, v_conv_b, v_w_out, v_final_norm_w):
    L = x.shape[1]
    nc = L // CHUNK
    xs = x[0]
    tgt = loss_target[0]
    fnw = final_norm_w.reshape(1, D_MODEL)

    as_rows = lambda a: jnp.transpose(a, (2, 0, 1))
    win_g, cqkv_g, cw_g = _all_gather([_cast_w_in(as_rows(w_in)), conv_qkv_w[0], conv_w[0]], "gather_weights",
                                      pieces=[4, 1, 1])
    wpad = _relayout_w_in(win_g)
    cqkv = jnp.concatenate([cqkv_g[d] for d in range(N_DEV)], axis=1)
    cw = jnp.concatenate([cw_g[d] for d in range(N_DEV)], axis=1)
    alog_p = _pad_lanes(A_log, HEADS)
    dtb_p = _pad_lanes(dt_bias, HEADS)
    me_flat, me_chip = _flat(*_mesh_pos()), 2 * lax.axis_index("x") + lax.axis_index("y")
    tok = lambda started: started[4][0:1, 0:1]
    wo_own = w_out[0].astype(bf16)
    wo_started = _spread_start(wo_own, wpad, "gather", "gather_w_out_start")

    proj, h = _in_proj(xs, norm_in_w + tok(wo_started), wpad)
    qkv = _qkv_act(proj, cqkv)
    sc, gr = _scalars(proj, alog_p, dtb_p)
    o, u_all, w_all, vn_all, t_all, sp_all = _gdn_fwd(qkv, sc, gr)
    mix_a = _gdn_gate(o, proj, gdn_norm_w)
    mix_b = _conv_fwd(proj, cw, conv_b)
    wo = _own_slot(_spread_wait(wo_started, mix_b, "gather", "gather_w_out_wait"), wo_own, me_flat).reshape(-1, D_MODEL)
    dy, dyb, dmix_a, dmix_b, g_fnw, loss_v = _out_proj_loss(xs, mix_a, mix_b, wo, fnw, tgt)

    g_wout = jnp.concatenate([_tn_matmul(mix_a, dyb, "grad_w_out_a"), _tn_matmul(mix_b, dyb, "grad_w_out_b")], axis=0)
    g_wout = g_wout.reshape(N_DEV, -1, D_MODEL)
    g_wout_own = lax.dynamic_index_in_dim(g_wout, me_flat, 0, keepdims=False)
    gwo_started = _spread_start(g_wout, dyb, "scatter", "exchange_grad_w_out_start")
    do, dzg, g_gnw = _gdn_gate_bwd(o, proj, gdn_norm_w + tok(gwo_started), dmix_a)
    d_b, d_c, d_hc, d_zc, g_cw, g_cb = _conv_bwd(proj, cw, conv_b, dmix_b)
    dqkv_n, dsc, dgr = _gdn_bwd(qkv, sc, gr, u_all, w_all, vn_all, t_all, sp_all, do)
    dqkv, g_cqkv = _qkv_bwd(proj, cqkv, dqkv_n)
    dgr_col = jnp.pad(dgr.transpose(0, 2, 1).reshape(L, HEADS), ((0, 0), (HEADS, LANE - 2 * HEADS)))
    dba, g_sc = _scalars_bwd(proj, alog_p, dtb_p, dsc, dgr_col)
    pieces = [dqkv, dzg, dba, d_b, d_c, d_hc, d_zc]
    offs = [OFF_QKV, OFF_ZG, OFF_BA, OFF_B, OFF_C, OFF_HC, OFF_ZC]
    g_parts = [_tn_matmul(p, h, "grad_w_in_%d" % i) for i, p in enumerate(pieces)]
    g_win_blk = _grad_blocks(g_parts)

    (p_win,) = _pair_exchange([g_win_blk], "exchange_grads_pair")
    s_win = _pair_sum(g_win_blk, p_win, "pair_sum_w_in")
    s_win_own = lax.dynamic_index_in_dim(s_win, me_chip, 0, keepdims=False)
    r_cqkv, r_cw = _all_to_all(
        [g_cqkv.reshape(4, N_DEV, -1).transpose(1, 0, 2), g_cw.reshape(3, N_DEV, -1).transpose(1, 0, 2)],
        "exchange_small_sharded_grads")
    gwi_started = _spread_start(s_win, r_cw, "chips", "exchange_grads_chips_start")
    grad_x, g_nw = _input_grad(pieces, offs, wpad, xs, norm_in_w + tok(gwi_started), dy)

    r_wout = _own_slot(_spread_wait(gwo_started, grad_x, "scatter", "exchange_grad_w_out_wait"), g_wout_own, me_flat)
    upd_wout =_adamw_reduce(r_wout, w_out[0], m_w_out[0], v_w_out[0], "adamw_w_out")
    upd_cqkv = _adamw_reduce(r_cqkv, conv_qkv_w[0], m_conv_qkv_w[0], v_conv_qkv_w[0], "adamw_conv_qkv_w")
    upd_cw = _adamw_reduce(r_cw, conv_w[0], m_conv_w[0], v_conv_w[0], "adamw_conv_w")

    r_win = _own_slot(_spread_wait(gwi_started, upd_cw[0], "chips", "exchange_grads_chips_wait"), s_win_own, me_chip)
    upd_win = [jnp.transpose(a, (1, 2, 0)) for a in _adamw_w_in(r_win, as_rows(w_in), as_rows(m_w_in), as_rows(v_w_in))]

    small_g = _pack_small([g_nw, g_cb, g_fnw, g_gnw, g_sc, loss_v], r_win)
    (small_all,) = _all_gather([small_g], "gather_small_grads")
    fvec = lambda a: a.reshape(1, D_MODEL)
    upd_small, loss_sum = _adamw_small(
        small_all,
        [norm_in_w, conv_b, fvec(final_norm_w), gdn_norm_w, A_log, dt_bias],
        [m_norm_in_w, m_conv_b, fvec(m_final_norm_w), m_gdn_norm_w, m_A_log, m_dt_bias],
        [v_norm_in_w, v_conv_b, fvec(v_final_norm_w), v_gdn_norm_w, v_A_log, v_dt_bias])

    outs = [loss_sum[0, 0], grad_x[None]]
    for k in range(4):
        nw_k, cb_k, fw_k, gn_k, al_k, dt_k = upd_small[k]
        outs += [nw_k, upd_win[k], upd_cqkv[k][None], al_k, dt_k, gn_k,
                 upd_cw[k][None], cb_k, upd_wout[k][None], fw_k.reshape(D_MODEL)]
    return tuple(outs)
```

```python
import jax
import jax.numpy as jnp
from jax import lax
from jax.experimental import pallas as pl
from jax.experimental.pallas import tpu as pltpu

f32 = jnp.float32
bf16 = jnp.bfloat16

N_DEV = 8
D_MODEL = 1024
HEADS = 8
HEAD_DIM = 128
CHUNK = 64
GDN_CPS = 4
GDN_CPS_BWD = 1
GDN_WIDTH = HEADS * HEAD_DIM
CONV_WIDTH = 1024
PROJ_WIDTH = 8208
SHARD_W = PROJ_WIDTH // N_DEV
EPS = 1e-6

LANE = 128
ELT_W = 256

OFF_QKV, OFF_ZG, OFF_CONV, OFF_BA = 0, 3072, 4096, 8192
CONV_BLOCK = 4 * ELT_W
PROJ_PAD = 8448
NAT_BA, NAT_CONV = 4096, 4112


def _padded_col(n):
    if n < NAT_BA:
        return n
    if n < NAT_CONV:
        return OFF_BA + n - NAT_BA
    g, ch = divmod(n - NAT_CONV, CONV_WIDTH)
    j, r = divmod(ch, ELT_W)
    return OFF_CONV + CONV_BLOCK * j + ELT_W * g + r


def _layout_segments(n0, n1):
    cuts = [NAT_BA, NAT_CONV] + [NAT_CONV + ELT_W * k for k in range(1, 4 * CONV_WIDTH // ELT_W)]
    pts = [n0] + [c for c in cuts if n0 < c < n1] + [n1]
    return [(lo, hi - lo, _padded_col(lo)) for lo, hi in zip(pts, pts[1:])]

ADAM_LR, ADAM_B1, ADAM_B2, ADAM_EPS, ADAM_WD, ADAM_STEP = 0.001, 0.9, 0.999, 1e-08, 0.01, 10

V7X_VMEM_BYTES = 64 * 1024 * 1024
VMEM_LIMIT = V7X_VMEM_BYTES - 8 * 1024 * 1024

MESH = pl.DeviceIdType.MESH
ANY = pl.BlockSpec(memory_space=pl.ANY)


def _pcall(body, **kw):
    return pl.pallas_call(body, **kw)


def _cparams(*sem):
    return pltpu.CompilerParams(dimension_semantics=sem if sem else None, vmem_limit_bytes=VMEM_LIMIT)


def _mm(a, b):
    return jnp.dot(a.astype(bf16), b.astype(bf16), preferred_element_type=f32)


def _mm_nt(a, b):
    return lax.dot_general(a.astype(bf16), b.astype(bf16), (((1,), (1,)), ((), ())), preferred_element_type=f32)


def _mm_tn(a, b):
    return lax.dot_general(a.astype(bf16), b.astype(bf16), (((0,), (0,)), ((), ())), preferred_element_type=f32)


def _rows(shape):
    return lax.broadcasted_iota(jnp.int32, shape, 0)


def _lanes(shape):
    return lax.broadcasted_iota(jnp.int32, shape, 1)


def _shift_down(x, s):
    if s == 0:
        return x
    return jnp.where(_rows(x.shape) >= s, pltpu.roll(x, s, 0), 0.0)


def _shift_up(x, s):
    if s == 0:
        return x
    n = x.shape[0]
    return jnp.where(_rows(x.shape) < n - s, pltpu.roll(x, n - s, 0), 0.0)


def _sigmoid(x):
    return jax.nn.sigmoid(x)


def _softplus(x):
    e = jnp.exp(-jnp.abs(x))
    small = e * (1.0 - e * (0.5 - e * (1.0 / 3.0)))
    return jnp.maximum(x, 0.0) + jnp.where(e < 0.01, small, jnp.log(1.0 + e))


def _mesh_pos():
    return lax.axis_index("x"), lax.axis_index("y"), lax.axis_index("c")


def _flat(px, py, pc):
    return 4 * px + 2 * py + pc


def _all_gather(xs, name, pieces=None):
    n = len(xs)
    pieces = pieces or [1] * n
    items = [(a, q) for a in range(n) for q in range(pieces[a])]
    ni = len(items)

    def view(ref, i):
        a, q = items[i]
        if pieces[a] == 1:
            return ref
        wd = xs[a].shape[-1] // pieces[a]
        return ref.at[(slice(None),) * (xs[a].ndim - 1) + (pl.ds(q * wd, wd),)]

    def body(*refs):
        x_refs, o_refs = refs[:n], refs[n:2 * n]
        send_sems, recv_sems, local_sems = refs[2 * n:]
        x, y, c = _mesh_pos()
        me, sibling = (x, y, c), (x, y, 1 - c)
        flip = lambda v, bit: v + bit - 2 * v * bit
        nbr_a = (flip(x, 1 - c), flip(y, c))
        nbr_b = (flip(x, c), flip(y, 1 - c))
        diag = (1 - x, 1 - y)

        def copy(i, k, block, to, own=False):
            a = items[i][0]
            dst = view(o_refs[a].at[_flat(*block)], i)
            return pltpu.make_async_remote_copy(
                src_ref=view(x_refs[a], i) if own else dst, dst_ref=dst,
                send_sem=send_sems.at[i, k], recv_sem=recv_sems.at[i, k], device_id=to, device_id_type=MESH)

        mine, sent = [], []

        def go(cp):
            cp.start()
            sent.append(cp)

        for a in range(n):
            cp = pltpu.make_async_copy(x_refs[a], o_refs[a].at[_flat(*me)], local_sems.at[a])
            cp.start()
            mine.append(cp)
        for a in range(ni):
            go(copy(a, 1, me, (*nbr_a, c), own=True))
            go(copy(a, 2, me, (*nbr_b, c), own=True))
            go(copy(a, 0, me, sibling, own=True))
        for a in range(ni):
            copy(a, 1, (*nbr_a, c), me).wait_recv()
            go(copy(a, 3, (*nbr_a, c), (*nbr_b, c)))
            go(copy(a, 4, (*nbr_a, c), sibling))
        for a in range(ni):
            copy(a, 2, (*nbr_b, c), me).wait_recv()
            go(copy(a, 5, (*nbr_b, c), sibling))
        for a in range(ni):
            copy(a, 3, (*diag, c), me).wait_recv()
            go(copy(a, 6, (*diag, c), sibling))
        for a in range(ni):
            copy(a, 0, sibling, me).wait_recv()
            copy(a, 4, (*nbr_b, 1 - c), me).wait_recv()
            copy(a, 5, (*nbr_a, 1 - c), me).wait_recv()
            copy(a, 6, (*diag, 1 - c), me).wait_recv()
        for cp in sent:
            cp.wait_send()
        for cp in mine:
            cp.wait()

    outs = _pcall(
        body, name=name,
        out_shape=[jax.ShapeDtypeStruct((N_DEV,) + a.shape, a.dtype) for a in xs],
        in_specs=[ANY] * n, out_specs=[ANY] * n,
        scratch_shapes=[pltpu.SemaphoreType.DMA((ni, 7)), pltpu.SemaphoreType.DMA((ni, 7)), pltpu.SemaphoreType.DMA((n,))],
    )(*xs)
    return list(outs)


def _all_to_all(gs, name):
    n = len(gs)

    def body(*refs):
        g_refs, o_refs = refs[:n], refs[n:2 * n]
        send_sems, recv_sems, local_sems = refs[2 * n:]
        x, y, c = _mesh_pos()
        me = _flat(x, y, c)
        peers = []
        for k in range(1, N_DEV):
            kx, ky, kc = (k >> 2) & 1, (k >> 1) & 1, k & 1
            px = (1 - x) if kx else x
            py = (1 - y) if ky else y
            pc = (1 - c) if kc else c
            peers.append((px, py, pc))

        def copy(a, k):
            peer = peers[k - 1]
            return pltpu.make_async_remote_copy(
                src_ref=g_refs[a].at[_flat(*peer)], dst_ref=o_refs[a].at[me],
                send_sem=send_sems.at[a, k - 1], recv_sem=recv_sems.at[a, k - 1], device_id=peer, device_id_type=MESH)

        def arrival(a, k):
            peer = peers[k - 1]
            return pltpu.make_async_remote_copy(
                src_ref=g_refs[a].at[me], dst_ref=o_refs[a].at[_flat(*peer)],
                send_sem=send_sems.at[a, k - 1], recv_sem=recv_sems.at[a, k - 1], device_id=peer, device_id_type=MESH)

        mine, sent = [], []
        for a in range(n):
            cp = pltpu.make_async_copy(g_refs[a].at[me], o_refs[a].at[me], local_sems.at[a])
            cp.start()
            mine.append(cp)
            for k in range(1, N_DEV):
                cp = copy(a, k)
                cp.start()
                sent.append(cp)
        for a in range(n):
            for k in range(1, N_DEV):
                arrival(a, k).wait_recv()
        for cp in sent:
            cp.wait_send()
        for cp in mine:
            cp.wait()

    outs = _pcall(
        body, name=name,
        out_shape=[jax.ShapeDtypeStruct(a.shape, a.dtype) for a in gs],
        in_specs=[ANY] * n, out_specs=[ANY] * n,
        scratch_shapes=[pltpu.SemaphoreType.DMA((n, 7)), pltpu.SemaphoreType.DMA((n, 7)), pltpu.SemaphoreType.DMA((n,))],
    )(*gs)
    return list(outs)


def _pair_exchange(gs, name):
    n = len(gs)
    chips = [(0, 0), (0, 1), (1, 0), (1, 1)]

    def body(*refs):
        g_refs, o_refs = refs[:n], refs[n:2 * n]
        send_sems, recv_sems = refs[2 * n:]
        x, y, c = _mesh_pos()
        sibling = (x, y, 1 - c)

        def copy(a, i):
            xp, yp = chips[i]
            return pltpu.make_async_remote_copy(
                src_ref=g_refs[a].at[_flat(xp, yp, 1 - c)], dst_ref=o_refs[a].at[i],
                send_sem=send_sems.at[a, i], recv_sem=recv_sems.at[a, i], device_id=sibling, device_id_type=MESH)

        cps = [copy(a, i) for a in range(n) for i in range(4)]
        for cp in cps:
            cp.start()
        for cp in cps:
            cp.wait()

    outs = _pcall(
        body, name=name,
        out_shape=[jax.ShapeDtypeStruct((4,) + a.shape[1:], a.dtype) for a in gs],
        in_specs=[ANY] * n, out_specs=[ANY] * n,
        scratch_shapes=[pltpu.SemaphoreType.DMA((n, 4)), pltpu.SemaphoreType.DMA((n, 4))],
    )(*gs)
    return list(outs)


def _pair_sum(g, p1, name):
    _, R, C = g.shape
    tr = 256 if R % 256 == 0 else R
    cidx = lax.axis_index("c").astype(jnp.int32).reshape(1)

    def body(c_ref, g_ref, p_ref, o_ref):
        o_ref[...] = (g_ref[...].astype(f32) + p_ref[...].astype(f32)).astype(o_ref.dtype)

    return _pcall(
        body, name=name,
        grid_spec=pltpu.PrefetchScalarGridSpec(
            num_scalar_prefetch=1, grid=(4, R // tr),
            in_specs=[pl.BlockSpec((1, tr, C), lambda i, r, c_ref: (2 * i + c_ref[0], r, 0)),
                      pl.BlockSpec((1, tr, C), lambda i, r, c_ref: (i, r, 0))],
            out_specs=pl.BlockSpec((1, tr, C), lambda i, r, c_ref: (i, r, 0))),
        out_shape=jax.ShapeDtypeStruct((4, R, C), g.dtype),
        compiler_params=_cparams("parallel", "parallel"),
    )(cidx, g, p1)


HBM = pl.BlockSpec(memory_space=pltpu.HBM)
SEM = pl.BlockSpec(memory_space=pltpu.SEMAPHORE)
EFFECT = pltpu.SideEffectType.DATAFLOW_SIDE_EFFECTING


def _peers(x, y, c):
    out = []
    for k in range(1, N_DEV):
        kx, ky, kc = (k >> 2) & 1, (k >> 1) & 1, k & 1
        out.append(((1 - x) if kx else x, (1 - y) if ky else y, (1 - c) if kc else c))
    return out


SPREAD_COPIES = {"gather": N_DEV - 1, "scatter": N_DEV - 1, "chips": 3}


def _spread_copy(src_ref, land_ref, send_sems, recv_sems, k, plan):
    x, y, c = _mesh_pos()
    if plan == "chips":
        px, py = [(1 - x, y), (x, 1 - y), (1 - x, 1 - y)][k]
        peer, src, slot = (px, py, c), src_ref.at[2 * px + py], 2 * x + y
    else:
        peer = _peers(x, y, c)[k]
        src, slot = (src_ref.at[_flat(*peer)] if plan == "scatter" else src_ref), _flat(x, y, c)
    return pltpu.make_async_remote_copy(
        src_ref=src, dst_ref=land_ref.at[slot], send_sem=send_sems.at[k], recv_sem=recv_sems.at[k],
        device_id=peer, device_id_type=MESH)


def _spread_start(src, after, plan, name):
    land_shape = (N_DEV,) + src.shape if plan == "gather" else src.shape
    n_copies = SPREAD_COPIES[plan]

    def body(src_ref, land_ref, after_ref, send_sems, recv_sems, src_thru, land_thru, token):
        for k in range(n_copies):
            _spread_copy(src_ref, land_ref, send_sems, recv_sems, k, plan).start()
        token[...] = jnp.zeros_like(token)

    return _pcall(
        body, name=name,
        out_shape=(pltpu.SemaphoreType.DMA((n_copies,)), pltpu.SemaphoreType.DMA((n_copies,)),
                   pltpu.HBM(src.shape, src.dtype), pltpu.HBM(land_shape, src.dtype), jax.ShapeDtypeStruct((8, LANE), f32)),
        in_specs=(HBM, HBM, ANY), out_specs=(SEM, SEM, HBM, HBM, pl.BlockSpec(memory_space=pltpu.VMEM)),
        input_output_aliases={0: 2, 1: 3},
        compiler_params=pltpu.CompilerParams(has_side_effects=EFFECT),
    )(pltpu.with_memory_space_constraint(src, pltpu.HBM),
      pltpu.with_memory_space_constraint(lax.empty(land_shape, src.dtype), pltpu.HBM), after)


def _spread_wait(started, after, plan, name):
    send_sems, recv_sems, src_thru, land_thru, _ = started

    def body(src_ref, land_ref, send_sems, recv_sems, after_ref, src_dead, got_ref):
        for k in range(SPREAD_COPIES[plan]):
            cp = _spread_copy(src_ref, land_ref, send_sems, recv_sems, k, plan)
            cp.wait_send()
            cp.wait_recv()

    return _pcall(
        body, name=name,
        out_shape=(pltpu.HBM(src_thru.shape, src_thru.dtype), pltpu.HBM(land_thru.shape, land_thru.dtype)),
        in_specs=(HBM, HBM, SEM, SEM, ANY), out_specs=(HBM, HBM), input_output_aliases={0: 0, 1: 1},
        compiler_params=pltpu.CompilerParams(has_side_effects=EFFECT),
    )(src_thru, land_thru, send_sems, recv_sems, after)[1]


def _own_slot(land, block, slot):
    zero = jnp.zeros((), jnp.int32)
    return lax.dynamic_update_slice(land, block[None], (slot.astype(jnp.int32),) + (zero,) * block.ndim)


COL_TILE = 256


def _cast_w_in(w3):
    n = w3.shape[0]

    def body(w_ref, o_ref):
        o_ref[...] = w_ref[:, 0, :].astype(bf16)

    tile = 2 * COL_TILE
    return _pcall(
        body, name="cast_w_in", grid=(D_MODEL // tile,),
        in_specs=[pl.BlockSpec((n, 1, tile), lambda j: (0, 0, j))],
        out_specs=pl.BlockSpec((n, tile), lambda j: (0, j)),
        out_shape=jax.ShapeDtypeStruct((n, D_MODEL), bf16),
        compiler_params=_cparams("parallel"),
    )(w3)


def _relayout_w_in(win_g):
    def body(g_ref, o_ref):
        used = OFF_BA + NAT_CONV - NAT_BA
        o_ref[used:PROJ_PAD, :] = jnp.zeros((PROJ_PAD - used, COL_TILE), o_ref.dtype)
        for d in range(N_DEV):
            for lo, width, dst in _layout_segments(d * SHARD_W, (d + 1) * SHARD_W):
                src = lo - d * SHARD_W
                o_ref[dst:dst + width, :] = g_ref[d, src:src + width, :]

    return _pcall(
        body, name="relayout_w_in", grid=(D_MODEL // COL_TILE,),
        in_specs=[pl.BlockSpec((N_DEV, SHARD_W, COL_TILE), lambda j: (0, 0, j))],
        out_specs=pl.BlockSpec((PROJ_PAD, COL_TILE), lambda j: (0, j)),
        out_shape=jax.ShapeDtypeStruct((PROJ_PAD, D_MODEL), win_g.dtype),
        compiler_params=_cparams("parallel"),
    )(win_g)


def _grad_blocks(g_t):
    def body(p_ref, o_ref):
        for d in range(N_DEV):
            for lo, width, src in _layout_segments(d * SHARD_W, (d + 1) * SHARD_W):
                dst = lo - d * SHARD_W
                o_ref[d, dst:dst + width, :] = p_ref[src:src + width, :]

    return _pcall(
        body, name="grad_blocks", grid=(D_MODEL // COL_TILE,),
        in_specs=[pl.BlockSpec((PROJ_PAD, COL_TILE), lambda j: (0, j))],
        out_specs=pl.BlockSpec((N_DEV, SHARD_W, COL_TILE), lambda j: (0, 0, j)),
        out_shape=jax.ShapeDtypeStruct((N_DEV, SHARD_W, D_MODEL), bf16),
        compiler_params=_cparams("parallel"),
    )(g_t)


def _in_proj(x, nw, wpad_t):
    L = x.shape[0]
    tn = 768
    nj = wpad_t.shape[0] // tn

    def body(x_ref, nw_ref, w_ref, proj_ref, h_ref):
        @pl.when(pl.program_id(0) == 0)
        def _():
            for r in range(0, L, 256):
                xs = x_ref[r:r + 256, :]
                ms = jnp.mean(xs * xs, axis=-1, keepdims=True)
                h_ref[r:r + 256, :] = ((xs * lax.rsqrt(ms + EPS)) * nw_ref[...]).astype(bf16)
        for r in range(0, L, 512):
            proj_ref[r:r + 512, :] = lax.dot_general(h_ref[r:r + 512, :], w_ref[...], (((1,), (1,)), ((), ())),
                                                     preferred_element_type=f32)

    return _pcall(
        body, name="in_proj", grid=(nj,),
        in_specs=[pl.BlockSpec((L, D_MODEL), lambda j: (0, 0)), pl.BlockSpec((1, D_MODEL), lambda j: (0, 0)),
                  pl.BlockSpec((tn, D_MODEL), lambda j: (j, 0))],
        out_specs=[pl.BlockSpec((L, tn), lambda j: (0, j)), pl.BlockSpec((L, D_MODEL), lambda j: (0, 0))],
        out_shape=[jax.ShapeDtypeStruct((L, wpad_t.shape[0]), f32), jax.ShapeDtypeStruct((L, D_MODEL), bf16)],
        compiler_params=_cparams("arbitrary"),
    )(x, nw, wpad_t)


HALVES = [slice(i * LANE, (i + 1) * LANE) for i in range(ELT_W // LANE)]
QKV_W = 512
QKV_HEADS = [slice(i * LANE, (i + 1) * LANE) for i in range(QKV_W // LANE)]
STEPS_PER_GROUP = GDN_WIDTH // QKV_W


def _conv4(x, cw_ref, ls):
    return (cw_ref[3:4, ls] * x + cw_ref[2:3, ls] * _shift_down(x, 1) + cw_ref[1:2, ls] * _shift_down(x, 2)
            + cw_ref[0:1, ls] * _shift_down(x, 3))


def _qkv_act(proj, cw):
    L = proj.shape[0]

    def body(x_ref, cw_ref, o_ref):
        j = pl.program_id(0)
        scale = jnp.where(j < STEPS_PER_GROUP, HEAD_DIM ** -0.5, 1.0).astype(f32)
        for ls in QKV_HEADS:
            c = _conv4(x_ref[:, ls], cw_ref, ls)
            a = c * _sigmoid(c)
            rn = lax.rsqrt(jnp.sum(a * a, axis=1, keepdims=True) + EPS)
            o_ref[:, ls] = jnp.where(j < 2 * STEPS_PER_GROUP, (a * rn) * scale, a)

    return _pcall(
        body, name="qkv_act", grid=(3 * STEPS_PER_GROUP,),
        in_specs=[pl.BlockSpec((L, QKV_W), lambda j: (0, j)), pl.BlockSpec((4, QKV_W), lambda j: (0, j))],
        out_specs=pl.BlockSpec((L, QKV_W), lambda j: (0, j)),
        out_shape=jax.ShapeDtypeStruct((L, 3 * GDN_WIDTH), f32),
        compiler_params=_cparams("parallel"),
    )(proj, cw)


def _scalars(proj, alog_p, dtb_p):
    L = proj.shape[0]
    nc = L // CHUNK

    def body(x_ref, al_ref, dt_ref, sc_ref, gr_ref):
        x = x_ref[...]
        lane = _lanes(x.shape)
        beta = _sigmoid(x)
        g = -jnp.exp(al_ref[...]) * _softplus(x + dt_ref[...])
        gc = jnp.where((lane >= HEADS) & (lane < 2 * HEADS), g, 0.0)
        rc = _rows(x.shape) & (CHUNK - 1)
        for s in (1, 2, 4, 8, 16, 32):
            gc = gc + jnp.where(rc >= s, pltpu.roll(gc, s, 0), 0.0)
        sc_ref[...] = jnp.where(lane < HEADS, beta, gc)
        sel = (_lanes((HEADS, LANE)) == _rows((HEADS, LANE)) + HEADS).astype(f32)
        for c in range(nc):
            gr_ref[c] = lax.dot_general(sel, sc_ref[c * CHUNK:(c + 1) * CHUNK, :], (((1,), (1,)), ((), ())),
                                        preferred_element_type=f32, precision=lax.Precision.HIGHEST)

    return _pcall(
        body, name="scalars", grid=(1,),
        in_specs=[pl.BlockSpec((L, LANE), lambda i: (0, OFF_BA // LANE)), pl.BlockSpec((1, LANE), lambda i: (0, 0)),
                  pl.BlockSpec((1, LANE), lambda i: (0, 0))],
        out_specs=[pl.BlockSpec((L, LANE), lambda i: (0, 0)), pl.BlockSpec((nc, HEADS, CHUNK), lambda i: (0, 0, 0))],
        out_shape=[jax.ShapeDtypeStruct((L, LANE), f32), jax.ShapeDtypeStruct((nc, HEADS, CHUNK), f32)],
        compiler_params=_cparams("arbitrary"),
    )(proj, alog_p, dtb_p)


def _head_scalars(sc, gr_ref, h, ci=0):
    lane = _lanes(sc.shape)
    beta = jnp.sum(jnp.where(lane == h, sc, 0.0), axis=1, keepdims=True)
    gcc = jnp.sum(jnp.where(lane == HEADS + h, sc, 0.0), axis=1, keepdims=True)
    gcr = gr_ref[ci, h:h + 1, :]
    gl = jnp.sum(jnp.where(_lanes(gcr.shape) == CHUNK - 1, gcr, 0.0), axis=1, keepdims=True)
    ii, jj = _rows((CHUNK, CHUNK)), _lanes((CHUNK, CHUNK))
    dmat = jnp.where(ii >= jj, jnp.exp(jnp.minimum(gcc - gcr, 0.0)), 0.0)
    dmat_t = jnp.where(jj >= ii, jnp.exp(jnp.minimum(gcr - gcc, 0.0)), 0.0)
    return beta, gcc, gl, dmat, dmat_t, ii, jj


def _gdn_fwd(qkv, sc, gr):
    L = qkv.shape[0]
    nc = L // CHUNK
    W = GDN_WIDTH
    cps = GDN_CPS if nc % GDN_CPS == 0 else 1
    rows_per_step = cps * CHUNK

    def body(qkv_ref, sc_ref, gr_ref, o_ref, u_ref, w_ref, vn_ref, t_ref, sp_ref, s_scr):
        @pl.when(pl.program_id(0) == 0)
        def _():
            s_scr[...] = jnp.zeros_like(s_scr)
        HS = range(cps * HEADS)
        hd = [i % HEADS for i in HS]
        rs = [slice((i // HEADS) * CHUNK, (i // HEADS + 1) * CHUNK) for i in HS]
        cs = [slice(hd[i] * HEAD_DIM, (hd[i] + 1) * HEAD_DIM) for i in HS]
        q = [qkv_ref[rs[i], hd[i] * HEAD_DIM:(hd[i] + 1) * HEAD_DIM] for i in HS]
        k = [qkv_ref[rs[i], W + hd[i] * HEAD_DIM:W + (hd[i] + 1) * HEAD_DIM] for i in HS]
        v = [qkv_ref[rs[i], 2 * W + hd[i] * HEAD_DIM:2 * W + (hd[i] + 1) * HEAD_DIM] for i in HS]
        hsc = [_head_scalars(sc_ref[rs[i], :], gr_ref, hd[i], i // HEADS) for i in HS]
        beta, gcc, gl, dmat = ([x[i] for x in hsc] for i in range(4))
        ii, jj = hsc[0][5], hsc[0][6]
        eg = [jnp.exp(gcc[h]) for h in HS]
        kb = [k[h] * beta[h] for h in HS]
        kk = [_mm_nt(kb[h], k[h]) for h in HS]
        qk = [_mm_nt(q[h], k[h]) for h in HS]
        n0 = [-jnp.where(ii > jj, kk[h] * dmat[h], 0.0) for h in HS]
        n1 = [_mm(n0[h], n0[h]) for h in HS]
        n2 = [_mm(n1[h], n1[h]) for h in HS]
        p01 = [n0[h] + n1[h] + _mm(n0[h], n1[h]) for h in HS]
        n3 = [_mm(n2[h], n2[h]) for h in HS]
        n4 = [_mm(n3[h], n3[h]) for h in HS]
        p23 = [n2[h] + n3[h] + _mm(n2[h], n3[h]) for h in HS]
        n5 = [_mm(n4[h], n4[h]) for h in HS]
        p03 = [p01[h] + p23[h] + _mm(p01[h], p23[h]) for h in HS]
        p45 = [n4[h] + n5[h] + _mm(n4[h], n5[h]) for h in HS]
        t = [p03[h] + p45[h] + _mm(p03[h], p45[h]) for h in HS]
        vb = [v[h] * beta[h] for h in HS]
        kbg = [kb[h] * eg[h] for h in HS]
        uw = [_mm(t[h], jnp.concatenate([vb[h], kbg[h]], axis=1)) for h in HS]
        u = [vb[h] + uw[h][:, :HEAD_DIM] for h in HS]
        w = [kbg[h] + uw[h][:, HEAD_DIM:] for h in HS]
        wq = [jnp.concatenate([w[h], q[h] * eg[h]], axis=0) for h in HS]
        p = [jnp.where(ii >= jj, qk[h] * dmat[h], 0.0) for h in HS]
        ks = [k[h] * jnp.exp(gl[h] - gcc[h]) for h in HS]
        s = [s_scr[h] for h in range(HEADS)]
        for ci in range(cps):
            IS = range(ci * HEADS, (ci + 1) * HEADS)
            ws = [_mm(wq[i], s[hd[i]]) for i in IS]
            vn = [u[i] - ws[hd[i]][:CHUNK] for i in IS]
            pv = [_mm(p[i], vn[hd[i]]) for i in IS]
            kv = [_mm_tn(ks[i], vn[hd[i]]) for i in IS]
            for i in IS:
                h = hd[i]
                sp_ref[ci, cs[i], :] = s[h]
                o_ref[rs[i], cs[i]] = ws[h][CHUNK:] + pv[h]
                vn_ref[rs[i], cs[i]] = vn[h].astype(bf16)
            s = [jnp.exp(gl[i]) * s[hd[i]] + kv[hd[i]] for i in IS]
        for h in range(HEADS):
            s_scr[h] = s[h]
        for i in HS:
            u_ref[rs[i], cs[i]] = u[i].astype(bf16)
            w_ref[rs[i], cs[i]] = w[i].astype(bf16)
            t_ref[i // HEADS, hd[i]] = t[i].astype(bf16)

    row = lambda c: (c, 0)
    act, act16 = jax.ShapeDtypeStruct((L, W), f32), jax.ShapeDtypeStruct((L, W), bf16)
    return _pcall(
        body, name="gdn_fwd", grid=(nc // cps,),
        in_specs=[pl.BlockSpec((rows_per_step, 3 * W), row), pl.BlockSpec((rows_per_step, LANE), row),
                  pl.BlockSpec((cps, HEADS, CHUNK), lambda c: (c, 0, 0))],
        out_specs=[pl.BlockSpec((rows_per_step, W), row)] * 4 + [
            pl.BlockSpec((cps, HEADS, CHUNK, CHUNK), lambda c: (c, 0, 0, 0)),
            pl.BlockSpec((cps, W, HEAD_DIM), lambda c: (c, 0, 0))],
        out_shape=[act, act16, act16, act16, jax.ShapeDtypeStruct((nc, HEADS, CHUNK, CHUNK), bf16),
                   jax.ShapeDtypeStruct((nc, W, HEAD_DIM), f32)],
        scratch_shapes=[pltpu.VMEM((HEADS, HEAD_DIM, HEAD_DIM), f32)],
        compiler_params=_cparams("arbitrary"),
    )(qkv, sc, gr)


def _gdn_gate(o, proj, gnw):
    L = o.shape[0]

    def body(o_ref, z_ref, w_ref, m_ref):
        for ls in HALVES:
            ov, z = o_ref[:, ls], z_ref[:, ls]
            rms = lax.rsqrt(jnp.mean(ov * ov, axis=-1, keepdims=True) + EPS)
            m_ref[:, ls] = (((ov * rms) * w_ref[...]) * (z * _sigmoid(z))).astype(bf16)

    return _pcall(
        body, name="gdn_gate", grid=(GDN_WIDTH // ELT_W,),
        in_specs=[pl.BlockSpec((L, ELT_W), lambda j: (0, j)), pl.BlockSpec((L, ELT_W), lambda j: (0, OFF_ZG // ELT_W + j)),
                  pl.BlockSpec((1, LANE), lambda j: (0, 0))],
        out_specs=pl.BlockSpec((L, ELT_W), lambda j: (0, j)),
        out_shape=jax.ShapeDtypeStruct((L, GDN_WIDTH), bf16),
        compiler_params=_cparams("parallel"),
    )(o, proj, gnw)


def _conv3(u, cw_ref, ls):
    return cw_ref[2:3, ls] * u + cw_ref[1:2, ls] * _shift_down(u, 1) + cw_ref[0:1, ls] * _shift_down(u, 2)


def _conv_specs(L):
    return [pl.BlockSpec((L, CONV_BLOCK), lambda j: (0, OFF_CONV // CONV_BLOCK + j)),
            pl.BlockSpec((3, ELT_W), lambda j: (0, j)), pl.BlockSpec((1, ELT_W), lambda j: (0, j))]


def _conv_parts(ls):
    return [slice(g * ELT_W + ls.start, g * ELT_W + ls.stop) for g in range(4)]


def _conv_fwd(proj, cw, cb):
    L = proj.shape[0]

    def body(p_ref, cw_ref, cb_ref, m_ref):
        for ls in HALVES:
            sb, sc_, sh, sz = _conv_parts(ls)
            z = p_ref[:, sz]
            cv = _conv3(p_ref[:, sc_] * p_ref[:, sh], cw_ref, ls) + cb_ref[:, ls]
            m_ref[:, ls] = ((p_ref[:, sb] * cv) * (z * _sigmoid(z))).astype(bf16)

    return _pcall(
        body, name="conv_fwd", grid=(CONV_WIDTH // ELT_W,),
        in_specs=_conv_specs(L), out_specs=pl.BlockSpec((L, ELT_W), lambda j: (0, j)),
        out_shape=jax.ShapeDtypeStruct((L, CONV_WIDTH), bf16),
        compiler_params=_cparams("parallel"),
    )(proj, cw, cb)


def _out_proj_loss(x, mix_a, mix_b, wo, fw, tgt):
    L = x.shape[0]
    tm = min(512, L)

    def body(x_ref, ma_ref, mb_ref, wo_ref, fw_ref, t_ref, dy_ref, dyb_ref, dma_ref, dmb_ref, gfw_ref, loss_ref):
        @pl.when(pl.program_id(0) == 0)
        def _():
            gfw_ref[...] = jnp.zeros_like(gfw_ref)
            loss_ref[...] = jnp.zeros_like(loss_ref)
        y = x_ref[...] + jnp.dot(ma_ref[...], wo_ref[:GDN_WIDTH, :], preferred_element_type=f32) \
            + jnp.dot(mb_ref[...], wo_ref[GDN_WIDTH:, :], preferred_element_type=f32)
        r = lax.rsqrt(jnp.mean(y * y, axis=-1, keepdims=True) + EPS)
        yh = y * r
        fwv = fw_ref[...]
        diff = yh * fwv - t_ref[...]
        loss_ref[...] += jnp.sum(jnp.sum(diff * diff, axis=-1, keepdims=True), axis=0, keepdims=True) * (0.5 / D_MODEL)
        dout = diff * (1.0 / D_MODEL)
        gfw_ref[...] += jnp.sum(dout * yh, axis=0, keepdims=True)
        dyh = dout * fwv
        dy = r * (dyh - yh * jnp.mean(dyh * yh, axis=-1, keepdims=True))
        dy_ref[...] = dy
        dyb = dy.astype(bf16)
        dyb_ref[...] = dyb
        dma_ref[...] = lax.dot_general(dyb, wo_ref[:GDN_WIDTH, :], (((1,), (1,)), ((), ())), preferred_element_type=f32)
        dmb_ref[...] = lax.dot_general(dyb, wo_ref[GDN_WIDTH:, :], (((1,), (1,)), ((), ())), preferred_element_type=f32)

    row = lambda i: (i, 0)
    fix = lambda i: (0, 0)
    act = jax.ShapeDtypeStruct((L, D_MODEL), f32)
    return _pcall(
        body, name="out_proj_loss", grid=(L // tm,),
        in_specs=[pl.BlockSpec((tm, D_MODEL), row), pl.BlockSpec((tm, GDN_WIDTH), row), pl.BlockSpec((tm, CONV_WIDTH), row),
                  pl.BlockSpec((GDN_WIDTH + CONV_WIDTH, D_MODEL), fix), pl.BlockSpec((1, D_MODEL), fix),
                  pl.BlockSpec((tm, D_MODEL), row)],
        out_specs=[pl.BlockSpec((tm, D_MODEL), row), pl.BlockSpec((tm, D_MODEL), row), pl.BlockSpec((tm, GDN_WIDTH), row),
                   pl.BlockSpec((tm, CONV_WIDTH), row), pl.BlockSpec((1, D_MODEL), fix), pl.BlockSpec((1, LANE), fix)],
        out_shape=[act, jax.ShapeDtypeStruct((L, D_MODEL), bf16), act, act,
                   jax.ShapeDtypeStruct((1, D_MODEL), f32), jax.ShapeDtypeStruct((1, LANE), f32)],
        compiler_params=_cparams("arbitrary"),
    )(x, mix_a, mix_b, wo, fw, tgt)


def _tn_matmul(a, b, name):
    L, M = a.shape
    N = b.shape[1]
    tm = 512 if M % 512 == 0 else (768 if M % 768 == 0 else M)

    def body(a_ref, b_ref, o_ref):
        o_ref[...] = lax.dot_general(a_ref[...], b_ref[...], (((0,), (0,)), ((), ())),
                                     preferred_element_type=f32).astype(o_ref.dtype)

    return _pcall(
        body, name=name, grid=(M // tm,),
        in_specs=[pl.BlockSpec((L, tm), lambda i: (0, i)), pl.BlockSpec((L, N), lambda i: (0, 0))],
        out_specs=pl.BlockSpec((tm, N), lambda i: (i, 0)),
        out_shape=jax.ShapeDtypeStruct((M, N), bf16),
        compiler_params=_cparams("parallel"),
    )(a, b)


def _gdn_gate_bwd(o, proj, gnw, dmix_a):
    L = o.shape[0]

    def body(o_ref, z_ref, w_ref, dm_ref, do_ref, dz_ref, gw_ref):
        @pl.when(pl.program_id(0) == 0)
        def _():
            gw_ref[...] = jnp.zeros_like(gw_ref)
        wv = w_ref[...]
        for ls in HALVES:
            ov, z, dm = o_ref[:, ls], z_ref[:, ls], dm_ref[:, ls]
            rms = lax.rsqrt(jnp.mean(ov * ov, axis=-1, keepdims=True) + EPS)
            xh = ov * rms
            sg = _sigmoid(z)
            d_on = dm * (z * sg)
            dz_ref[:, ls] = (dm * (xh * wv) * (sg * (1.0 + z * (1.0 - sg)))).astype(bf16)
            gw_ref[...] += jnp.sum(d_on * xh, axis=0, keepdims=True)
            dxh = d_on * wv
            do_ref[:, ls] = (rms * (dxh - xh * jnp.mean(dxh * xh, axis=-1, keepdims=True))).astype(bf16)

    wide = pl.BlockSpec((L, ELT_W), lambda j: (0, j))
    return _pcall(
        body, name="gdn_gate_bwd", grid=(GDN_WIDTH // ELT_W,),
        in_specs=[wide, pl.BlockSpec((L, ELT_W), lambda j: (0, OFF_ZG // ELT_W + j)),
                  pl.BlockSpec((1, LANE), lambda j: (0, 0)), wide],
        out_specs=[wide, pl.BlockSpec((L, ELT_W), lambda j: (0, OFF_ZG // ELT_W + j)),
                   pl.BlockSpec((1, LANE), lambda j: (0, 0))],
        out_shape=[jax.ShapeDtypeStruct((L, GDN_WIDTH), bf16), jax.ShapeDtypeStruct((L, PROJ_PAD), bf16),
                   jax.ShapeDtypeStruct((1, LANE), f32)],
        compiler_params=_cparams("arbitrary"),
    )(o, proj, gnw, dmix_a)


def _conv_bwd(proj, cw, cb, dmix_b, dproj):
    L = proj.shape[0]

    def body(p_ref, cw_ref, cb_ref, dm_ref, dproj_in, dp_ref, gcw_ref, gcb_ref):
        for ls in HALVES:
            sb, sc_, sh, sz_ = _conv_parts(ls)
            bv, cv_, hv, z, dm = p_ref[:, sb], p_ref[:, sc_], p_ref[:, sh], p_ref[:, sz_], dm_ref[:, ls]
            u = cv_ * hv
            cv = _conv3(u, cw_ref, ls) + cb_ref[:, ls]
            sg = _sigmoid(z)
            sz = z * sg
            dp_ref[:, sb] = (dm * cv * sz).astype(bf16)
            dp_ref[:, sz_] = (dm * (bv * cv) * (sg * (1.0 + z * (1.0 - sg)))).astype(bf16)
            dcv = dm * bv * sz
            gcb_ref[:, ls] = jnp.sum(dcv, axis=0, keepdims=True)
            dcv1, dcv2 = _shift_up(dcv, 1), _shift_up(dcv, 2)
            gcw_ref[2:3, ls] = jnp.sum(dcv * u, axis=0, keepdims=True)
            gcw_ref[1:2, ls] = jnp.sum(dcv1 * u, axis=0, keepdims=True)
            gcw_ref[0:1, ls] = jnp.sum(dcv2 * u, axis=0, keepdims=True)
            du = cw_ref[2:3, ls] * dcv + cw_ref[1:2, ls] * dcv1 + cw_ref[0:1, ls] * dcv2
            dp_ref[:, sc_] = (du * hv).astype(bf16)
            dp_ref[:, sh] = (du * cv_).astype(bf16)

    return _pcall(
        body, name="conv_bwd", grid=(CONV_WIDTH // ELT_W,),
        in_specs=_conv_specs(L) + [pl.BlockSpec((L, ELT_W), lambda j: (0, j)), ANY],
        out_specs=[pl.BlockSpec((L, CONV_BLOCK), lambda j: (0, OFF_CONV // CONV_BLOCK + j)),
                   pl.BlockSpec((3, ELT_W), lambda j: (0, j)), pl.BlockSpec((1, ELT_W), lambda j: (0, j))],
        out_shape=[jax.ShapeDtypeStruct(dproj.shape, dproj.dtype), jax.ShapeDtypeStruct((3, CONV_WIDTH), f32),
                   jax.ShapeDtypeStruct((1, CONV_WIDTH), f32)],
        input_output_aliases={4: 0},
        compiler_params=_cparams("parallel"),
    )(proj, cw, cb, dmix_b, dproj)


def _gdn_bwd(qkv, sc, gr, u_all, w_all, vn_all, t_all, sp_all, do_all):
    L = qkv.shape[0]
    nc = L // CHUNK
    W = GDN_WIDTH
    cps = GDN_CPS_BWD if nc % GDN_CPS_BWD == 0 else 1
    rows_per_step = cps * CHUNK
    nsteps = nc // cps

    def body(qkv_ref, sc_ref, gr_ref, u_ref, w_ref, vn_ref, t_ref, sp_ref, do_ref, dqkv_ref, dsc_ref, dgr_ref, ds_scr):
        @pl.when(pl.program_id(0) == 0)
        def _():
            ds_scr[...] = jnp.zeros_like(ds_scr)
        HS = range(cps * HEADS)
        hd = [i % HEADS for i in HS]
        rs = [slice((i // HEADS) * CHUNK, (i // HEADS + 1) * CHUNK) for i in HS]
        cs = [slice(hd[i] * HEAD_DIM, (hd[i] + 1) * HEAD_DIM) for i in HS]
        q = [qkv_ref[rs[i], hd[i] * HEAD_DIM:(hd[i] + 1) * HEAD_DIM] for i in HS]
        k = [qkv_ref[rs[i], W + hd[i] * HEAD_DIM:W + (hd[i] + 1) * HEAD_DIM] for i in HS]
        v = [qkv_ref[rs[i], 2 * W + hd[i] * HEAD_DIM:2 * W + (hd[i] + 1) * HEAD_DIM] for i in HS]
        hsc = [_head_scalars(sc_ref[rs[i], :], gr_ref, hd[i], i // HEADS) for i in HS]
        beta, gcc, gl, dmat, dmat_t = ([x[i] for x in hsc] for i in range(5))
        ii, jj = hsc[0][5], hsc[0][6]
        eg = [jnp.exp(gcc[h]) for h in HS]
        ekl = [jnp.exp(gl[h] - gcc[h]) for h in HS]
        egl = [jnp.exp(gl[h]) for h in HS]
        kb = [k[h] * beta[h] for h in HS]
        ks = [k[h] * ekl[h] for h in HS]
        do = [do_ref[rs[h], cs[h]] for h in HS]
        vn = [vn_ref[rs[h], cs[h]] for h in HS]
        s = [sp_ref[h // HEADS, cs[h], :] for h in HS]
        w = [w_ref[rs[h], cs[h]] for h in HS]
        qd = [q[h] * eg[h] for h in HS]

        kq = [_mm_nt(k[h], q[h]) for h in HS]
        p_t = [jnp.where(jj >= ii, kq[h] * dmat_t[h], 0.0) for h in HS]
        ptd = [_mm(p_t[h], do[h]) for h in HS]
        qw =[jnp.concatenate([qd[h], -w[h]], axis=0) for h in HS]
        dsn, dvn, dodv = [None] * len(HS), [None] * len(HS), [None] * len(HS)
        ds_cur = [ds_scr[h] for h in range(HEADS)]
        for ci in reversed(range(cps)):
            IS = range(ci * HEADS, (ci + 1) * HEADS)
            ksd = [_mm(ks[i], ds_cur[hd[i]]) for i in IS]
            for i in IS:
                dsn[i] = ds_cur[hd[i]]
                dvn[i] = ptd[i] + ksd[hd[i]]
                dodv[i] = jnp.concatenate([do[i], dvn[i]], axis=0)
            dsq = [_mm_tn(qw[i], dodv[i]) for i in IS]
            ds_cur = [egl[i] * ds_cur[hd[i]] + dsq[hd[i]] for i in IS]
        for h in range(HEADS):
            ds_scr[h] = ds_cur[h]
        x1 = [_mm_nt(dodv[h], s[h]) for h in HS]
        dks = [_mm_nt(vn[h], dsn[h]) for h in HS]
        dov = [_mm_nt(do[h], vn[h]) for h in HS]
        vdo = [_mm_nt(vn[h], do[h]) for h in HS]
        kk = [_mm_nt(kb[h], k[h]) for h in HS]
        qk = [_mm_nt(q[h], k[h]) for h in HS]
        dgl = [egl[h] * jnp.sum(jnp.sum(s[h] * dsn[h], axis=1, keepdims=True), axis=0, keepdims=True) for h in HS]
        dqd = [x1[h][:CHUNK] for h in HS]
        duw = [jnp.concatenate([dvn[h], -x1[h][CHUNK:]], axis=1) for h in HS]
        tdu = [_mm_tn(t_ref[h // HEADS, hd[h]], duw[h]) for h in HS]
        dvk = [duw[h] + tdu[h] for h in HS]
        uw = [jnp.concatenate([u_ref[rs[h], cs[h]], w[h]], axis=1) for h in HS]
        da = [-jnp.where(ii > jj, _mm_nt(dvk[h], uw[h]), 0.0) for h in HS]
        da_t = [-jnp.where(jj > ii, _mm_nt(uw[h], dvk[h]), 0.0) for h in HS]
        dp = [jnp.where(ii >= jj, dov[h], 0.0) for h in HS]
        dp_t = [jnp.where(jj >= ii, vdo[h], 0.0) for h in HS]
        r1 = [_mm(jnp.concatenate([da[h] * dmat[h], dp[h] * dmat[h]], axis=0), k[h]) for h in HS]
        dk1 = [_mm(jnp.concatenate([da_t[h] * dmat_t[h], dp_t[h] * dmat_t[h]], axis=1),
                   jnp.concatenate([kb[h], q[h]], axis=0)) for h in HS]
        lane = _lanes((CHUNK, LANE))
        for ci in range(cps):
            dsc = jnp.zeros((CHUNK, LANE), f32)
            for i in range(ci * HEADS, (ci + 1) * HEADS):
                h = hd[i]
                a = jnp.where(ii > jj, kk[i] * dmat[i], 0.0)
                p = jnp.where(ii >= jj, qk[i] * dmat[i], 0.0)
                gmat = da[i] * a + dp[i] * p
                dvb, dkbg = dvk[i][:, :HEAD_DIM], dvk[i][:, HEAD_DIM:]
                kbg = kb[i] * eg[i]
                dkb = r1[i][:CHUNK] + dkbg * eg[i]
                dq = r1[i][CHUNK:] + dqd[i] * eg[i]
                dk = dk1[i] + dks[i] * ekl[i] + dkb * beta[i]
                dbeta = jnp.sum(dkb * k[i] + dvb * v[i], axis=1, keepdims=True)
                ksum = jnp.sum(dks[i] * ks[i], axis=1, keepdims=True)
                dgl_tot = dgl[i] + jnp.sum(ksum, axis=0, keepdims=True)
                dgc = (jnp.sum(gmat, axis=1, keepdims=True) + jnp.sum(dqd[i] * qd[i] + dkbg * kbg, axis=1, keepdims=True)
                       - ksum)
                dgc = dgc + jnp.where(_rows(dgc.shape) == CHUNK - 1, dgl_tot, 0.0)
                dqkv_ref[rs[i], h * HEAD_DIM:(h + 1) * HEAD_DIM] = dq
                dqkv_ref[rs[i], W + h * HEAD_DIM:W + (h + 1) * HEAD_DIM] = dk
                dqkv_ref[rs[i], 2 * W + h * HEAD_DIM:2 * W + (h + 1) * HEAD_DIM] = dvb * beta[i]
                dsc = jnp.where(lane == h, dbeta, jnp.where(lane == HEADS + h, dgc, dsc))
                dgr_ref[ci, h:h + 1, :] = jnp.sum(gmat, axis=0, keepdims=True)
            dsc_ref[ci * CHUNK:(ci + 1) * CHUNK, :] = dsc

    row = lambda c: (nsteps - 1 - c, 0)
    lead3 = lambda c: (nsteps - 1 - c, 0, 0)
    return _pcall(
        body, name="gdn_bwd", grid=(nsteps,),
        in_specs=[pl.BlockSpec((rows_per_step, 3 * W), row), pl.BlockSpec((rows_per_step, LANE), row),
                  pl.BlockSpec((cps, HEADS, CHUNK), lead3),
                  pl.BlockSpec((rows_per_step, W), row), pl.BlockSpec((rows_per_step, W), row),
                  pl.BlockSpec((rows_per_step, W), row),
                  pl.BlockSpec((cps, HEADS, CHUNK, CHUNK), lambda c: (nsteps - 1 - c, 0, 0, 0)),
                  pl.BlockSpec((cps, W, HEAD_DIM), lead3), pl.BlockSpec((rows_per_step, W), row)],
        out_specs=[pl.BlockSpec((rows_per_step, 3 * W), row), pl.BlockSpec((rows_per_step, LANE), row),
                   pl.BlockSpec((cps, HEADS, CHUNK), lead3)],
        out_shape=[jax.ShapeDtypeStruct((L, 3 * W), f32), jax.ShapeDtypeStruct((L, LANE), f32),
                   jax.ShapeDtypeStruct((nc, HEADS, CHUNK), f32)],
        scratch_shapes=[pltpu.VMEM((HEADS, HEAD_DIM, HEAD_DIM), f32)],
        compiler_params=_cparams("arbitrary"),
    )(qkv, sc, gr, u_all, w_all, vn_all, t_all, sp_all, do_all)


def _qkv_bwd(proj, cw, dn, dproj):
    L = proj.shape[0]

    def body(x_ref, cw_ref, dn_ref, dproj_in, dx_ref, gcw_ref):
        j = pl.program_id(0)
        scale = jnp.where(j < STEPS_PER_GROUP, HEAD_DIM ** -0.5, 1.0).astype(f32)
        for ls in QKV_HEADS:
            x, dn_v = x_ref[:, ls], dn_ref[:, ls]
            c = _conv4(x, cw_ref, ls)
            sg = _sigmoid(c)
            a = c * sg
            rn = lax.rsqrt(jnp.sum(a * a, axis=1, keepdims=True) + EPS)
            da_n = (scale * rn) * (dn_v - a * ((rn * rn) * jnp.sum(dn_v * a, axis=1, keepdims=True)))
            da = jnp.where(j < 2 * STEPS_PER_GROUP, da_n, dn_v)
            dc = da * (sg * (1.0 + c * (1.0 - sg)))
            dc1, dc2, dc3 = _shift_up(dc, 1), _shift_up(dc, 2), _shift_up(dc, 3)
            gcw_ref[3:4, ls] = jnp.sum(dc * x, axis=0, keepdims=True)
            gcw_ref[2:3, ls] = jnp.sum(dc1 * x, axis=0, keepdims=True)
            gcw_ref[1:2, ls] = jnp.sum(dc2 * x, axis=0, keepdims=True)
            gcw_ref[0:1, ls] = jnp.sum(dc3 * x, axis=0, keepdims=True)
            dx = cw_ref[3:4, ls] * dc + cw_ref[2:3, ls] * dc1 + cw_ref[1:2, ls] * dc2 + cw_ref[0:1, ls] * dc3
            dx_ref[:, ls] = dx.astype(bf16)

    col = pl.BlockSpec((L, QKV_W), lambda j: (0, j))
    wspec = pl.BlockSpec((4, QKV_W), lambda j: (0, j))
    return _pcall(
        body, name="qkv_bwd", grid=(3 * STEPS_PER_GROUP,),
        in_specs=[col, wspec, col, ANY], out_specs=[col, wspec],
        out_shape=[jax.ShapeDtypeStruct(dproj.shape, dproj.dtype), jax.ShapeDtypeStruct((4, 3 * GDN_WIDTH), f32)],
        input_output_aliases={3: 0},
        compiler_params=_cparams("parallel"),
    )(proj, cw, dn, dproj)


def _scalars_bwd(proj, alog_p, dtb_p, dsc, dgr_col, dproj):
    L = proj.shape[0]

    def body(x_ref, al_ref, dt_ref, dsc_ref, dgr_ref, dproj_in, dba_ref, gs_ref):
        x, dsc_v = x_ref[...], dsc_ref[...]
        lane = _lanes(x.shape)
        dec = (lane >= HEADS) & (lane < 2 * HEADS)
        dg = jnp.where(dec, dsc_v - dgr_ref[...], 0.0)
        rc = _rows(x.shape) & (CHUNK - 1)
        for s in (1, 2, 4, 8, 16, 32):
            dg = dg + jnp.where(rc + s < CHUNK, pltpu.roll(dg, L - s, 0), 0.0)
        xa = x + dt_ref[...]
        ea = jnp.exp(al_ref[...])
        g = -ea * _softplus(xa)
        da = dg * (-ea) * _sigmoid(xa)
        beta = _sigmoid(x)
        db = dsc_v * beta * (1.0 - beta)
        dba_ref[:, :LANE] = jnp.where(lane < HEADS, db, jnp.where(dec, da, 0.0)).astype(bf16)
        dba_ref[:, LANE:] = jnp.zeros((L, ELT_W - LANE), bf16)
        g_al = jnp.sum(jnp.where(dec, dg * g, 0.0), axis=0, keepdims=True)
        g_dt = jnp.sum(jnp.where(dec, da, 0.0), axis=0, keepdims=True)
        row8 = _rows(gs_ref.shape)
        gs = jnp.where(row8 == 0, g_al, jnp.where(row8 == 1, g_dt, 0.0))
        gs_ref[...] = pltpu.roll(gs, LANE - HEADS, 1)

    full = pl.BlockSpec((L, LANE), lambda i: (0, 0))
    vec = pl.BlockSpec((1, LANE), lambda i: (0, 0))
    return _pcall(
        body, name="scalars_bwd", grid=(1,),
        in_specs=[pl.BlockSpec((L, LANE), lambda i: (0, OFF_BA // LANE)), vec, vec, full, full, ANY],
        out_specs=[pl.BlockSpec((L, ELT_W), lambda i: (0, OFF_BA // ELT_W)), pl.BlockSpec((8, LANE), lambda i: (0, 0))],
        out_shape=[jax.ShapeDtypeStruct(dproj.shape, dproj.dtype), jax.ShapeDtypeStruct((8, LANE), f32)],
        input_output_aliases={5: 0},
        compiler_params=_cparams("arbitrary"),
    )(proj, alog_p, dtb_p, dsc, dgr_col, dproj)


def _input_grad(dproj, wpad, x, nw, dy):
    L = x.shape[0]
    tm = min(512, L)
    cuts = (0, 3072, 5120, 7168, PROJ_PAD)
    nk = len(cuts) - 1

    def body(dp_ref, w_hbm, x_ref, nw_ref, dy_ref, gx_ref, gnw_ref, w_vmem, sems):
        first = pl.program_id(0) == 0
        loads = [pltpu.make_async_copy(w_hbm.at[cuts[k]:cuts[k + 1], :], w_vmem.at[cuts[k]:cuts[k + 1], :], sems.at[k])
                 for k in range(nk)]

        @pl.when(first)
        def _():
            for cp in loads:
                cp.start()
            gnw_ref[...] = jnp.zeros_like(gnw_ref)
        dh = None
        for k in range(nk):
            pl.when(first)(loads[k].wait)
            part = jnp.dot(dp_ref[:, cuts[k]:cuts[k + 1]], w_vmem[cuts[k]:cuts[k + 1], :], preferred_element_type=f32)
            dh = part if dh is None else dh + part
        xv, nwv = x_ref[...], nw_ref[...]
        r = lax.rsqrt(jnp.mean(xv * xv, axis=-1, keepdims=True) + EPS)
        xh = xv * r
        gnw_ref[...] += jnp.sum(dh * xh, axis=0, keepdims=True)
        dxh = dh * nwv
        gx_ref[...] = dy_ref[...] + r * (dxh - xh * jnp.mean(dxh * xh, axis=-1, keepdims=True))

    row = lambda i: (i, 0)
    fix = lambda i: (0, 0)
    return _pcall(
        body, name="input_grad", grid=(L // tm,),
        in_specs=[pl.BlockSpec((tm, PROJ_PAD), row), ANY, pl.BlockSpec((tm, D_MODEL), row),
                  pl.BlockSpec((1, D_MODEL), fix), pl.BlockSpec((tm, D_MODEL), row)],
        out_specs=[pl.BlockSpec((tm, D_MODEL), row), pl.BlockSpec((1, D_MODEL), fix)],
        out_shape=[jax.ShapeDtypeStruct((L, D_MODEL), f32), jax.ShapeDtypeStruct((1, D_MODEL), f32)],
        scratch_shapes=[pltpu.VMEM(wpad.shape, bf16), pltpu.SemaphoreType.DMA((nk,))],
        compiler_params=_cparams("arbitrary"),
    )(dproj, wpad, x, nw, dy)


def _adamw_reduce(parts, w, m, v, name):
    R, C = w.shape
    n_parts = parts.shape[0]
    tr = 128 if R % 128 == 0 else R
    c1 = 1.0 - ADAM_B1 ** ADAM_STEP
    c2 = 1.0 - ADAM_B2 ** ADAM_STEP

    def body(p_ref, w_ref, m_ref, v_ref, g_ref, d_ref, nm_ref, nv_ref):
        g = p_ref[0].astype(f32)
        for s in range(1, n_parts):
            g = g + p_ref[s].astype(f32)
        nm = ADAM_B1 * m_ref[...] + (1.0 - ADAM_B1) * g
        nv = ADAM_B2 * v_ref[...] + (1.0 - ADAM_B2) * (g * g)
        g_ref[...] = g
        nm_ref[...] = nm
        nv_ref[...] = nv
        d_ref[...] = -ADAM_LR * ((nm / c1) / (jnp.sqrt(nv / c2) + ADAM_EPS) + ADAM_WD * w_ref[...])

    blk = pl.BlockSpec((tr, C), lambda i: (i, 0))
    out = jax.ShapeDtypeStruct((R, C), f32)
    return _pcall(
        body, name=name, grid=(R // tr,),
        in_specs=[pl.BlockSpec((n_parts, tr, C), lambda i: (0, i, 0)), blk, blk, blk],
        out_specs=[blk] * 4, out_shape=[out] * 4,
        compiler_params=_cparams("parallel"),
    )(parts, w, m, v)


SMALL_SLOTS = ((0, D_MODEL), (D_MODEL, D_MODEL), (2 * D_MODEL, D_MODEL), (3 * D_MODEL, LANE),
               (3 * D_MODEL + LANE, HEADS), (3 * D_MODEL + 2 * LANE, HEADS))
SMALL_LOSS = 3 * D_MODEL + 3 * LANE
SMALL_W = SMALL_LOSS + LANE


def _pack_small(gs, after):
    def body(nw_ref, cb_ref, fw_ref, gn_ref, sc_ref, ls_ref, after_ref, o_ref):
        for ref, (start, width) in zip((nw_ref, cb_ref, fw_ref, gn_ref), SMALL_SLOTS[:4]):
            o_ref[:, start:start + width] = ref[...]
        o_ref[:, SMALL_SLOTS[4][0]:SMALL_SLOTS[4][0] + LANE] = sc_ref[0:1, :]
        o_ref[:, SMALL_SLOTS[5][0]:SMALL_SLOTS[5][0] + LANE] = sc_ref[1:2, :]
        o_ref[:, SMALL_LOSS:SMALL_W] = ls_ref[...]

    vm = pl.BlockSpec(memory_space=pltpu.VMEM)
    return _pcall(body, name="pack_small_grads", out_shape=jax.ShapeDtypeStruct((1, SMALL_W), f32),
                  in_specs=[vm] * 6 + [ANY], out_specs=vm)(*gs, after)


def _adamw_small(parts, ws, ms, vs):
    c1 = 1.0 - ADAM_B1 ** ADAM_STEP
    c2 = 1.0 - ADAM_B2 ** ADAM_STEP
    np_ = len(ws)

    def body(*refs):
        p_ref = refs[0]
        w_refs, m_refs, v_refs = refs[1:1 + np_], refs[1 + np_:1 + 2 * np_], refs[1 + 2 * np_:1 + 3 * np_]
        outs = refs[1 + 3 * np_:]
        g_refs, d_refs, nm_refs, nv_refs = (outs[i * np_:(i + 1) * np_] for i in range(4))
        loss_ref = outs[4 * np_]

        def total(start, width):
            t = p_ref[0, :, start:start + width]
            for s in range(1, N_DEV):
                t = t + p_ref[s, :, start:start + width]
            return t

        for i, (start, width) in enumerate(SMALL_SLOTS):
            g = total(start, width)
            nm = ADAM_B1 * m_refs[i][...] + (1.0 - ADAM_B1) * g
            nv = ADAM_B2 * v_refs[i][...] + (1.0 - ADAM_B2) * (g * g)
            g_refs[i][...] = g
            nm_refs[i][...] = nm
            nv_refs[i][...] = nv
            d_refs[i][...] = -ADAM_LR * ((nm / c1) / (jnp.sqrt(nv / c2) + ADAM_EPS) + ADAM_WD * w_refs[i][...])
        loss_ref[...] = total(SMALL_LOSS, LANE)

    vm = pl.BlockSpec(memory_space=pltpu.VMEM)
    shapes = [jax.ShapeDtypeStruct(w.shape, f32) for w in ws]
    res = _pcall(body, name="adamw_small", out_shape=shapes * 4 + [jax.ShapeDtypeStruct((1, LANE), f32)],
                 in_specs=[vm] * (1 + 3 * np_), out_specs=[vm] * (4 * np_ + 1))(parts, *ws, *ms, *vs)
    return [res[i * np_:(i + 1) * np_] for i in range(4)], res[4 * np_]


def _adamw_w_in(parts, w3, m3, v3):
    n_parts, n, _ = parts.shape
    c1 = 1.0 - ADAM_B1 ** ADAM_STEP
    c2 = 1.0 - ADAM_B2 ** ADAM_STEP

    def body(p_ref, w_ref, m_ref, v_ref, g_ref, d_ref, nm_ref, nv_ref):
        g = p_ref[0].astype(f32)
        for s in range(1, n_parts):
            g = g + p_ref[s].astype(f32)
        nm = ADAM_B1 * m_ref[:, 0, :] + (1.0 - ADAM_B1) * g
        nv = ADAM_B2 * v_ref[:, 0, :] + (1.0 - ADAM_B2) * (g * g)
        g_ref[:, 0, :] = g
        nm_ref[:, 0, :] = nm
        nv_ref[:, 0, :] = nv
        d_ref[:, 0, :] = -ADAM_LR * ((nm / c1) / (jnp.sqrt(nv / c2) + ADAM_EPS) + ADAM_WD * w_ref[:, 0, :])

    tile = 2 * COL_TILE
    blk = pl.BlockSpec((n, 1, tile), lambda j: (0, 0, j))
    out = jax.ShapeDtypeStruct((n, 1, D_MODEL), f32)
    return _pcall(
        body, name="adamw_w_in", grid=(D_MODEL // tile,),
        in_specs=[pl.BlockSpec((n_parts, n, tile), lambda j: (0, 0, j)), blk, blk, blk],
        out_specs=[blk] * 4, out_shape=[out] * 4,
        compiler_params=_cparams("parallel"),
    )(parts, w3, m3, v3)


def _pad_lanes(vec8, start):
    return jnp.pad(vec8.reshape(1, -1), ((0, 0), (start, LANE - start - vec8.size)))


def kernel(x, norm_in_w, w_in, conv_qkv_w, A_log, dt_bias, gdn_norm_w, conv_w, conv_b, w_out, final_norm_w, loss_target, m_norm_in_w, m_w_in, m_conv_qkv_w, m_A_log, m_dt_bias, m_gdn_norm_w, m_conv_w, m_conv_b, m_w_out, m_final_norm_w, v_norm_in_w, v_w_in, v_conv_qkv_w, v_A_log, v_dt_bias, v_gdn_norm_w, v_conv_w, v_conv_b, v_w_out, v_final_norm_w):
    L = x.shape[1]
    nc = L // CHUNK
    xs = x[0]
    tgt = loss_target[0]
    fnw = final_norm_w.reshape(1, D_MODEL)

    as_rows = lambda a: jnp.transpose(a, (2, 0, 1))
    win_g, cqkv_g, cw_g = _all_gather([_cast_w_in(as_rows(w_in)), conv_qkv_w[0], conv_w[0]], "gather_weights",
                                      pieces=[4, 1, 1])
    wpad = _relayout_w_in(win_g)
    cqkv = jnp.concatenate([cqkv_g[d] for d in range(N_DEV)], axis=1)
    cw = jnp.concatenate([cw_g[d] for d in range(N_DEV)], axis=1)
    alog_p = _pad_lanes(A_log, HEADS)
    dtb_p = _pad_lanes(dt_bias, HEADS)
    me_flat, me_chip = _flat(*_mesh_pos()), 2 * lax.axis_index("x") + lax.axis_index("y")
    tok = lambda started: started[4][0:1, 0:1]
    wo_own = w_out[0].astype(bf16)
    wo_started = _spread_start(wo_own, wpad, "gather", "gather_w_out_start")

    proj, h = _in_proj(xs, norm_in_w + tok(wo_started), wpad)
    qkv = _qkv_act(proj, cqkv)
    sc, gr = _scalars(proj, alog_p, dtb_p)
    o, u_all, w_all, vn_all, t_all, sp_all = _gdn_fwd(qkv, sc, gr)
    mix_a = _gdn_gate(o, proj, gdn_norm_w)
    mix_b = _conv_fwd(proj, cw, conv_b)
    wo = _own_slot(_spread_wait(wo_started, mix_b, "gather", "gather_w_out_wait"), wo_own, me_flat).reshape(-1, D_MODEL)
    dy, dyb, dmix_a, dmix_b, g_fnw, loss_v = _out_proj_loss(xs, mix_a, mix_b, wo, fnw, tgt)

    g_wout = jnp.concatenate([_tn_matmul(mix_a, dyb, "grad_w_out_a"), _tn_matmul(mix_b, dyb, "grad_w_out_b")], axis=0)
    g_wout = g_wout.reshape(N_DEV, -1, D_MODEL)
    g_wout_own = lax.dynamic_index_in_dim(g_wout, me_flat, 0, keepdims=False)
    gwo_started = _spread_start(g_wout, dyb, "scatter", "exchange_grad_w_out_start")
    do, dproj, g_gnw = _gdn_gate_bwd(o, proj, gdn_norm_w + tok(gwo_started), dmix_a)
    dproj, g_cw, g_cb = _conv_bwd(proj, cw, conv_b, dmix_b, dproj)
    dqkv_n, dsc, dgr = _gdn_bwd(qkv, sc, gr, u_all, w_all, vn_all, t_all, sp_all, do)
    dproj, g_cqkv = _qkv_bwd(proj, cqkv, dqkv_n, dproj)
    dgr_col = jnp.pad(dgr.transpose(0, 2, 1).reshape(L, HEADS), ((0, 0), (HEADS, LANE - 2 * HEADS)))
    dproj, g_sc = _scalars_bwd(proj, alog_p, dtb_p, dsc, dgr_col, dproj)
    g_win_blk = _grad_blocks(_tn_matmul(dproj, h, "grad_w_in"))

    (p_win,) = _pair_exchange([g_win_blk], "exchange_grads_pair")
    s_win = _pair_sum(g_win_blk, p_win, "pair_sum_w_in")
    s_win_own = lax.dynamic_index_in_dim(s_win, me_chip, 0, keepdims=False)
    r_cqkv, r_cw = _all_to_all(
        [g_cqkv.reshape(4, N_DEV, -1).transpose(1, 0, 2), g_cw.reshape(3, N_DEV, -1).transpose(1, 0, 2)],
        "exchange_small_sharded_grads")
    gwi_started = _spread_start(s_win, r_cw, "chips", "exchange_grads_chips_start")
    grad_x, g_nw = _input_grad(dproj, wpad, xs, norm_in_w + tok(gwi_started), dy)

    r_wout = _own_slot(_spread_wait(gwo_started, grad_x, "scatter", "exchange_grad_w_out_wait"), g_wout_own, me_flat)
    upd_wout =_adamw_reduce(r_wout, w_out[0], m_w_out[0], v_w_out[0], "adamw_w_out")
    upd_cqkv = _adamw_reduce(r_cqkv, conv_qkv_w[0], m_conv_qkv_w[0], v_conv_qkv_w[0], "adamw_conv_qkv_w")
    upd_cw = _adamw_reduce(r_cw, conv_w[0], m_conv_w[0], v_conv_w[0], "adamw_conv_w")

    r_win = _own_slot(_spread_wait(gwi_started, upd_cw[0], "chips", "exchange_grads_chips_wait"), s_win_own, me_chip)
    upd_win = [jnp.transpose(a, (1, 2, 0)) for a in _adamw_w_in(r_win, as_rows(w_in), as_rows(m_w_in), as_rows(v_w_in))]

    small_g = _pack_small([g_nw, g_cb, g_fnw, g_gnw, g_sc, loss_v], r_win)
    (small_all,) = _all_gather([small_g], "gather_small_grads")
    fvec = lambda a: a.reshape(1, D_MODEL)
    upd_small, loss_sum = _adamw_small(
        small_all,
        [norm_in_w, conv_b, fvec(final_norm_w), gdn_norm_w, A_log, dt_bias],
        [m_norm_in_w, m_conv_b, fvec(m_final_norm_w), m_gdn_norm_w, m_A_log, m_dt_bias],
        [v_norm_in_w, v_conv_b, fvec(v_final_norm_w), v_gdn_norm_w, v_A_log, v_dt_bias])

    outs = [loss_sum[0, 0], grad_x[None]]
    for k in range(4):
        nw_k, cb_k, fw_k, gn_k, al_k, dt_k = upd_small[k]
        outs += [nw_k, upd_win[k], upd_cqkv[k][None], al_k, dt_k, gn_k,
                 upd_cw[k][None], cb_k, upd_wout[k][None], fw_k.reshape(D_MODEL)]
    return tuple(outs)
```

```python
import jax
import jax.numpy as jnp
from jax import lax
from jax.experimental import pallas as pl
from jax.experimental.pallas import tpu as pltpu

f32 = jnp.float32
bf16 = jnp.bfloat16

N_DEV = 8
D_MODEL = 1024
HEADS = 8
HEAD_DIM = 128
CHUNK = 64
GDN_CPS = 4
GDN_CPS_BWD = 1
GDN_WIDTH = HEADS * HEAD_DIM
CONV_WIDTH = 1024
PROJ_WIDTH = 8208
SHARD_W = PROJ_WIDTH // N_DEV
EPS = 1e-6

LANE = 128
ELT_W = 256

OFF_QKV, OFF_ZG, OFF_CONV, OFF_BA = 0, 3072, 4096, 8192
CONV_BLOCK = 4 * ELT_W
PROJ_PAD = 8448
NAT_BA, NAT_CONV = 4096, 4112


def _padded_col(n):
    if n < NAT_BA:
        return n
    if n < NAT_CONV:
        return OFF_BA + n - NAT_BA
    g, ch = divmod(n - NAT_CONV, CONV_WIDTH)
    j, r = divmod(ch, ELT_W)
    return OFF_CONV + CONV_BLOCK * j + ELT_W * g + r


def _layout_segments(n0, n1):
    cuts = [NAT_BA, NAT_CONV] + [NAT_CONV + ELT_W * k for k in range(1, 4 * CONV_WIDTH // ELT_W)]
    pts = [n0] + [c for c in cuts if n0 < c < n1] + [n1]
    return [(lo, hi - lo, _padded_col(lo)) for lo, hi in zip(pts, pts[1:])]

ADAM_LR, ADAM_B1, ADAM_B2, ADAM_EPS, ADAM_WD, ADAM_STEP = 0.001, 0.9, 0.999, 1e-08, 0.01, 10

V7X_VMEM_BYTES = 64 * 1024 * 1024
VMEM_LIMIT = V7X_VMEM_BYTES - 8 * 1024 * 1024

MESH = pl.DeviceIdType.MESH
ANY = pl.BlockSpec(memory_space=pl.ANY)


def _pcall(body, **kw):
    return pl.pallas_call(body, **kw)


def _cparams(*sem):
    return pltpu.CompilerParams(dimension_semantics=sem if sem else None, vmem_limit_bytes=VMEM_LIMIT)


def _mm(a, b):
    return jnp.dot(a.astype(bf16), b.astype(bf16), preferred_element_type=f32)


def _mm_nt(a, b):
    return lax.dot_general(a.astype(bf16), b.astype(bf16), (((1,), (1,)), ((), ())), preferred_element_type=f32)


def _mm_tn(a, b):
    return lax.dot_general(a.astype(bf16), b.astype(bf16), (((0,), (0,)), ((), ())), preferred_element_type=f32)


def _rows(shape):
    return lax.broadcasted_iota(jnp.int32, shape, 0)


def _lanes(shape):
    return lax.broadcasted_iota(jnp.int32, shape, 1)


def _shift_down(x, s):
    if s == 0:
        return x
    return jnp.where(_rows(x.shape) >= s, pltpu.roll(x, s, 0), 0.0)


def _shift_up(x, s):
    if s == 0:
        return x
    n = x.shape[0]
    return jnp.where(_rows(x.shape) < n - s, pltpu.roll(x, n - s, 0), 0.0)


def _sigmoid(x):
    return jax.nn.sigmoid(x)


def _softplus(x):
    e = jnp.exp(-jnp.abs(x))
    small = e * (1.0 - e * (0.5 - e * (1.0 / 3.0)))
    return jnp.maximum(x, 0.0) + jnp.where(e < 0.01, small, jnp.log(1.0 + e))


def _mesh_pos():
    return lax.axis_index("x"), lax.axis_index("y"), lax.axis_index("c")


def _flat(px, py, pc):
    return 4 * px + 2 * py + pc


def _all_gather(xs, name, pieces=None):
    n = len(xs)
    pieces = pieces or [1] * n
    items = [(a, q) for a in range(n) for q in range(pieces[a])]
    ni = len(items)

    def view(ref, i):
        a, q = items[i]
        if pieces[a] == 1:
            return ref
        wd = xs[a].shape[-1] // pieces[a]
        return ref.at[(slice(None),) * (xs[a].ndim - 1) + (pl.ds(q * wd, wd),)]

    def body(*refs):
        x_refs, o_refs = refs[:n], refs[n:2 * n]
        send_sems, recv_sems, local_sems = refs[2 * n:]
        x, y, c = _mesh_pos()
        me, sibling = (x, y, c), (x, y, 1 - c)
        flip = lambda v, bit: v + bit - 2 * v * bit
        nbr_a = (flip(x, 1 - c), flip(y, c))
        nbr_b = (flip(x, c), flip(y, 1 - c))
        diag = (1 - x, 1 - y)

        def copy(i, k, block, to, own=False):
            a = items[i][0]
            dst = view(o_refs[a].at[_flat(*block)], i)
            return pltpu.make_async_remote_copy(
                src_ref=view(x_refs[a], i) if own else dst, dst_ref=dst,
                send_sem=send_sems.at[i, k], recv_sem=recv_sems.at[i, k], device_id=to, device_id_type=MESH)

        mine, sent = [], []

        def go(cp):
            cp.start()
            sent.append(cp)

        for a in range(n):
            cp = pltpu.make_async_copy(x_refs[a], o_refs[a].at[_flat(*me)], local_sems.at[a])
            cp.start()
            mine.append(cp)
        for a in range(ni):
            go(copy(a, 1, me, (*nbr_a, c), own=True))
            go(copy(a, 2, me, (*nbr_b, c), own=True))
            go(copy(a, 0, me, sibling, own=True))
        for a in range(ni):
            copy(a, 1, (*nbr_a, c), me).wait_recv()
            go(copy(a, 3, (*nbr_a, c), (*nbr_b, c)))
            go(copy(a, 4, (*nbr_a, c), sibling))
        for a in range(ni):
            copy(a, 2, (*nbr_b, c), me).wait_recv()
            go(copy(a, 5, (*nbr_b, c), sibling))
        for a in range(ni):
            copy(a, 3, (*diag, c), me).wait_recv()
            go(copy(a, 6, (*diag, c), sibling))
        for a in range(ni):
            copy(a, 0, sibling, me).wait_recv()
            copy(a, 4, (*nbr_b, 1 - c), me).wait_recv()
            copy(a, 5, (*nbr_a, 1 - c), me).wait_recv()
            copy(a, 6, (*diag, 1 - c), me).wait_recv()
        for cp in sent:
            cp.wait_send()
        for cp in mine:
            cp.wait()

    outs = _pcall(
        body, name=name,
        out_shape=[jax.ShapeDtypeStruct((N_DEV,) + a.shape, a.dtype) for a in xs],
        in_specs=[ANY] * n, out_specs=[ANY] * n,
        scratch_shapes=[pltpu.SemaphoreType.DMA((ni, 7)), pltpu.SemaphoreType.DMA((ni, 7)), pltpu.SemaphoreType.DMA((n,))],
    )(*xs)
    return list(outs)


def _all_to_all(gs, name):
    n = len(gs)

    def body(*refs):
        g_refs, o_refs = refs[:n], refs[n:2 * n]
        send_sems, recv_sems, local_sems = refs[2 * n:]
        x, y, c = _mesh_pos()
        me = _flat(x, y, c)
        peers = []
        for k in range(1, N_DEV):
            kx, ky, kc = (k >> 2) & 1, (k >> 1) & 1, k & 1
            px = (1 - x) if kx else x
            py = (1 - y) if ky else y
            pc = (1 - c) if kc else c
            peers.append((px, py, pc))

        def copy(a, k):
            peer = peers[k - 1]
            return pltpu.make_async_remote_copy(
                src_ref=g_refs[a].at[_flat(*peer)], dst_ref=o_refs[a].at[me],
                send_sem=send_sems.at[a, k - 1], recv_sem=recv_sems.at[a, k - 1], device_id=peer, device_id_type=MESH)

        def arrival(a, k):
            peer = peers[k - 1]
            return pltpu.make_async_remote_copy(
                src_ref=g_refs[a].at[me], dst_ref=o_refs[a].at[_flat(*peer)],
                send_sem=send_sems.at[a, k - 1], recv_sem=recv_sems.at[a, k - 1], device_id=peer, device_id_type=MESH)

        mine, sent = [], []
        for a in range(n):
            cp = pltpu.make_async_copy(g_refs[a].at[me], o_refs[a].at[me], local_sems.at[a])
            cp.start()
            mine.append(cp)
            for k in range(1, N_DEV):
                cp = copy(a, k)
                cp.start()
                sent.append(cp)
        for a in range(n):
            for k in range(1, N_DEV):
                arrival(a, k).wait_recv()
        for cp in sent:
            cp.wait_send()
        for cp in mine:
            cp.wait()

    outs = _pcall(
        body, name=name,
        out_shape=[jax.ShapeDtypeStruct(a.shape, a.dtype) for a in gs],
        in_specs=[ANY] * n, out_specs=[ANY] * n,
        scratch_shapes=[pltpu.SemaphoreType.DMA((n, 7)), pltpu.SemaphoreType.DMA((n, 7)), pltpu.SemaphoreType.DMA((n,))],
    )(*gs)
    return list(outs)


def _pair_exchange(gs, name):
    n = len(gs)
    chips = [(0, 0), (0, 1), (1, 0), (1, 1)]

    def body(*refs):
        g_refs, o_refs = refs[:n], refs[n:2 * n]
        send_sems, recv_sems = refs[2 * n:]
        x, y, c = _mesh_pos()
        sibling = (x, y, 1 - c)

        def copy(a, i):
            xp, yp = chips[i]
            return pltpu.make_async_remote_copy(
                src_ref=g_refs[a].at[_flat(xp, yp, 1 - c)], dst_ref=o_refs[a].at[i],
                send_sem=send_sems.at[a, i], recv_sem=recv_sems.at[a, i], device_id=sibling, device_id_type=MESH)

        cps = [copy(a, i) for a in range(n) for i in range(4)]
        for cp in cps:
            cp.start()
        for cp in cps:
            cp.wait()

    outs = _pcall(
        body, name=name,
        out_shape=[jax.ShapeDtypeStruct((4,) + a.shape[1:], a.dtype) for a in gs],
        in_specs=[ANY] * n, out_specs=[ANY] * n,
        scratch_shapes=[pltpu.SemaphoreType.DMA((n, 4)), pltpu.SemaphoreType.DMA((n, 4))],
    )(*gs)
    return list(outs)


def _pair_sum(g, p1, name):
    _, R, C = g.shape
    tr = 256 if R % 256 == 0 else R
    cidx = lax.axis_index("c").astype(jnp.int32).reshape(1)

    def body(c_ref, g_ref, p_ref, o_ref):
        o_ref[...] = (g_ref[...].astype(f32) + p_ref[...].astype(f32)).astype(o_ref.dtype)

    return _pcall(
        body, name=name,
        grid_spec=pltpu.PrefetchScalarGridSpec(
            num_scalar_prefetch=1, grid=(4, R // tr),
            in_specs=[pl.BlockSpec((1, tr, C), lambda i, r, c_ref: (2 * i + c_ref[0], r, 0)),
                      pl.BlockSpec((1, tr, C), lambda i, r, c_ref: (i, r, 0))],
            out_specs=pl.BlockSpec((1, tr, C), lambda i, r, c_ref: (i, r, 0))),
        out_shape=jax.ShapeDtypeStruct((4, R, C), g.dtype),
        compiler_params=_cparams("parallel", "parallel"),
    )(cidx, g, p1)


HBM = pl.BlockSpec(memory_space=pltpu.HBM)
SEM = pl.BlockSpec(memory_space=pltpu.SEMAPHORE)
EFFECT = pltpu.SideEffectType.DATAFLOW_SIDE_EFFECTING


def _peers(x, y, c):
    out = []
    for k in range(1, N_DEV):
        kx, ky, kc = (k >> 2) & 1, (k >> 1) & 1, k & 1
        out.append(((1 - x) if kx else x, (1 - y) if ky else y, (1 - c) if kc else c))
    return out


SPREAD_COPIES = {"gather": N_DEV - 1, "scatter": N_DEV - 1, "chips": 3}


def _spread_copy(src_ref, land_ref, send_sems, recv_sems, k, plan):
    x, y, c = _mesh_pos()
    if plan == "chips":
        px, py = [(1 - x, y), (x, 1 - y), (1 - x, 1 - y)][k]
        peer, src, slot = (px, py, c), src_ref.at[2 * px + py], 2 * x + y
    else:
        peer = _peers(x, y, c)[k]
        src, slot = (src_ref.at[_flat(*peer)] if plan == "scatter" else src_ref), _flat(x, y, c)
    return pltpu.make_async_remote_copy(
        src_ref=src, dst_ref=land_ref.at[slot], send_sem=send_sems.at[k], recv_sem=recv_sems.at[k],
        device_id=peer, device_id_type=MESH)


def _spread_start(src, after, plan, name):
    land_shape = (N_DEV,) + src.shape if plan == "gather" else src.shape
    n_copies = SPREAD_COPIES[plan]

    def body(src_ref, land_ref, after_ref, send_sems, recv_sems, src_thru, land_thru, token):
        for k in range(n_copies):
            _spread_copy(src_ref, land_ref, send_sems, recv_sems, k, plan).start()
        token[...] = jnp.zeros_like(token)

    return _pcall(
        body, name=name,
        out_shape=(pltpu.SemaphoreType.DMA((n_copies,)), pltpu.SemaphoreType.DMA((n_copies,)),
                   pltpu.HBM(src.shape, src.dtype), pltpu.HBM(land_shape, src.dtype), jax.ShapeDtypeStruct((8, LANE), f32)),
        in_specs=(HBM, HBM, ANY), out_specs=(SEM, SEM, HBM, HBM, pl.BlockSpec(memory_space=pltpu.VMEM)),
        input_output_aliases={0: 2, 1: 3},
        compiler_params=pltpu.CompilerParams(has_side_effects=EFFECT),
    )(pltpu.with_memory_space_constraint(src, pltpu.HBM),
      pltpu.with_memory_space_constraint(lax.empty(land_shape, src.dtype), pltpu.HBM), after)


def _spread_wait(started, after, plan, name):
    send_sems, recv_sems, src_thru, land_thru, _ = started

    def body(src_ref, land_ref, send_sems, recv_sems, after_ref, src_dead, got_ref):
        for k in range(SPREAD_COPIES[plan]):
            cp = _spread_copy(src_ref, land_ref, send_sems, recv_sems, k, plan)
            cp.wait_send()
            cp.wait_recv()

    return _pcall(
        body, name=name,
        out_shape=(pltpu.HBM(src_thru.shape, src_thru.dtype), pltpu.HBM(land_thru.shape, land_thru.dtype)),
        in_specs=(HBM, HBM, SEM, SEM, ANY), out_specs=(HBM, HBM), input_output_aliases={0: 0, 1: 1},
        compiler_params=pltpu.CompilerParams(has_side_effects=EFFECT),
    )(src_thru, land_thru, send_sems, recv_sems, after)[1]


def _own_slot(land, block, slot):
    zero = jnp.zeros((), jnp.int32)
    return lax.dynamic_update_slice(land, block[None], (slot.astype(jnp.int32),) + (zero,) * block.ndim)


COL_TILE = 256


def _cast_w_in(w3):
    n = w3.shape[0]

    def body(w_ref, o_ref):
        o_ref[...] = w_ref[:, 0, :].astype(bf16)

    tile = 2 * COL_TILE
    return _pcall(
        body, name="cast_w_in", grid=(D_MODEL // tile,),
        in_specs=[pl.BlockSpec((n, 1, tile), lambda j: (0, 0, j))],
        out_specs=pl.BlockSpec((n, tile), lambda j: (0, j)),
        out_shape=jax.ShapeDtypeStruct((n, D_MODEL), bf16),
        compiler_params=_cparams("parallel"),
    )(w3)


def _relayout_w_in(win_g):
    def body(g_ref, o_ref):
        used = OFF_BA + NAT_CONV - NAT_BA
        o_ref[used:PROJ_PAD, :] = jnp.zeros((PROJ_PAD - used, COL_TILE), o_ref.dtype)
        for d in range(N_DEV):
            for lo, width, dst in _layout_segments(d * SHARD_W, (d + 1) * SHARD_W):
                src = lo - d * SHARD_W
                o_ref[dst:dst + width, :] = g_ref[d, src:src + width, :]

    return _pcall(
        body, name="relayout_w_in", grid=(D_MODEL // COL_TILE,),
        in_specs=[pl.BlockSpec((N_DEV, SHARD_W, COL_TILE), lambda j: (0, 0, j))],
        out_specs=pl.BlockSpec((PROJ_PAD, COL_TILE), lambda j: (0, j)),
        out_shape=jax.ShapeDtypeStruct((PROJ_PAD, D_MODEL), win_g.dtype),
        compiler_params=_cparams("parallel"),
    )(win_g)


def _grad_blocks(g_t):
    def body(p_ref, o_ref):
        for d in range(N_DEV):
            for lo, width, src in _layout_segments(d * SHARD_W, (d + 1) * SHARD_W):
                dst = lo - d * SHARD_W
                o_ref[d, dst:dst + width, :] = p_ref[src:src + width, :]

    return _pcall(
        body, name="grad_blocks", grid=(D_MODEL // COL_TILE,),
        in_specs=[pl.BlockSpec((PROJ_PAD, COL_TILE), lambda j: (0, j))],
        out_specs=pl.BlockSpec((N_DEV, SHARD_W, COL_TILE), lambda j: (0, 0, j)),
        out_shape=jax.ShapeDtypeStruct((N_DEV, SHARD_W, D_MODEL), bf16),
        compiler_params=_cparams("parallel"),
    )(g_t)


def _in_proj(x, nw, wpad_t):
    L = x.shape[0]
    tn = 768
    nj = wpad_t.shape[0] // tn

    def body(x_ref, nw_ref, w_ref, proj_ref, h_ref):
        @pl.when(pl.program_id(0) == 0)
        def _():
            for r in range(0, L, 256):
                xs = x_ref[r:r + 256, :]
                ms = jnp.mean(xs * xs, axis=-1, keepdims=True)
                h_ref[r:r + 256, :] = ((xs * lax.rsqrt(ms + EPS)) * nw_ref[...]).astype(bf16)
        for r in range(0, L, 512):
            proj_ref[r:r + 512, :] = lax.dot_general(h_ref[r:r + 512, :], w_ref[...], (((1,), (1,)), ((), ())),
                                                     preferred_element_type=f32)

    return _pcall(
        body, name="in_proj", grid=(nj,),
        in_specs=[pl.BlockSpec((L, D_MODEL), lambda j: (0, 0)), pl.BlockSpec((1, D_MODEL), lambda j: (0, 0)),
                  pl.BlockSpec((tn, D_MODEL), lambda j: (j, 0))],
        out_specs=[pl.BlockSpec((L, tn), lambda j: (0, j)), pl.BlockSpec((L, D_MODEL), lambda j: (0, 0))],
        out_shape=[jax.ShapeDtypeStruct((L, wpad_t.shape[0]), f32), jax.ShapeDtypeStruct((L, D_MODEL), bf16)],
        compiler_params=_cparams("arbitrary"),
    )(x, nw, wpad_t)


HALVES = [slice(i * LANE, (i + 1) * LANE) for i in range(ELT_W // LANE)]
QKV_W = 512
QKV_HEADS = [slice(i * LANE, (i + 1) * LANE) for i in range(QKV_W // LANE)]
STEPS_PER_GROUP = GDN_WIDTH // QKV_W


def _conv4(x, cw_ref, ls):
    return (cw_ref[3:4, ls] * x + cw_ref[2:3, ls] * _shift_down(x, 1) + cw_ref[1:2, ls] * _shift_down(x, 2)
            + cw_ref[0:1, ls] * _shift_down(x, 3))


def _qkv_act(proj, cw):
    L = proj.shape[0]

    def body(x_ref, cw_ref, o_ref):
        j = pl.program_id(0)
        scale = jnp.where(j < STEPS_PER_GROUP, HEAD_DIM ** -0.5, 1.0).astype(f32)
        for ls in QKV_HEADS:
            c = _conv4(x_ref[:, ls], cw_ref, ls)
            a = c * _sigmoid(c)
            rn = lax.rsqrt(jnp.sum(a * a, axis=1, keepdims=True) + EPS)
            o_ref[:, ls] = jnp.where(j < 2 * STEPS_PER_GROUP, (a * rn) * scale, a)

    return _pcall(
        body, name="qkv_act", grid=(3 * STEPS_PER_GROUP,),
        in_specs=[pl.BlockSpec((L, QKV_W), lambda j: (0, j)), pl.BlockSpec((4, QKV_W), lambda j: (0, j))],
        out_specs=pl.BlockSpec((L, QKV_W), lambda j: (0, j)),
        out_shape=jax.ShapeDtypeStruct((L, 3 * GDN_WIDTH), f32),
        compiler_params=_cparams("parallel"),
    )(proj, cw)


def _scalars(proj, alog_p, dtb_p):
    L = proj.shape[0]
    nc = L // CHUNK

    def body(x_ref, al_ref, dt_ref, sc_ref, gr_ref):
        x = x_ref[...]
        lane = _lanes(x.shape)
        beta = _sigmoid(x)
        g = -jnp.exp(al_ref[...]) * _softplus(x + dt_ref[...])
        gc = jnp.where((lane >= HEADS) & (lane < 2 * HEADS), g, 0.0)
        rc = _rows(x.shape) & (CHUNK - 1)
        for s in (1, 2, 4, 8, 16, 32):
            gc = gc + jnp.where(rc >= s, pltpu.roll(gc, s, 0), 0.0)
        sc_ref[...] = jnp.where(lane < HEADS, beta, gc)
        sel = (_lanes((HEADS, LANE)) == _rows((HEADS, LANE)) + HEADS).astype(f32)
        for c in range(nc):
            gr_ref[c] = lax.dot_general(sel, sc_ref[c * CHUNK:(c + 1) * CHUNK, :], (((1,), (1,)), ((), ())),
                                        preferred_element_type=f32, precision=lax.Precision.HIGHEST)

    return _pcall(
        body, name="scalars", grid=(1,),
        in_specs=[pl.BlockSpec((L, LANE), lambda i: (0, OFF_BA // LANE)), pl.BlockSpec((1, LANE), lambda i: (0, 0)),
                  pl.BlockSpec((1, LANE), lambda i: (0, 0))],
        out_specs=[pl.BlockSpec((L, LANE), lambda i: (0, 0)), pl.BlockSpec((nc, HEADS, CHUNK), lambda i: (0, 0, 0))],
        out_shape=[jax.ShapeDtypeStruct((L, LANE), f32), jax.ShapeDtypeStruct((nc, HEADS, CHUNK), f32)],
        compiler_params=_cparams("arbitrary"),
    )(proj, alog_p, dtb_p)


def _head_scalars(sc, gr_ref, h, ci=0):
    lane = _lanes(sc.shape)
    beta = jnp.sum(jnp.where(lane == h, sc, 0.0), axis=1, keepdims=True)
    gcc = jnp.sum(jnp.where(lane == HEADS + h, sc, 0.0), axis=1, keepdims=True)
    gcr = gr_ref[ci, h:h + 1, :]
    gl = jnp.sum(jnp.where(_lanes(gcr.shape) == CHUNK - 1, gcr, 0.0), axis=1, keepdims=True)
    ii, jj = _rows((CHUNK, CHUNK)), _lanes((CHUNK, CHUNK))
    dmat = jnp.where(ii >= jj, jnp.exp(jnp.minimum(gcc - gcr, 0.0)), 0.0)
    dmat_t = jnp.where(jj >= ii, jnp.exp(jnp.minimum(gcr - gcc, 0.0)), 0.0)
    return beta, gcc, gl, dmat, dmat_t, ii, jj


def _gdn_fwd(qkv, sc, gr):
    L = qkv.shape[0]
    nc = L // CHUNK
    W = GDN_WIDTH
    cps = GDN_CPS if nc % GDN_CPS == 0 else 1
    rows_per_step = cps * CHUNK

    def body(qkv_ref, sc_ref, gr_ref, o_ref, u_ref, w_ref, vn_ref, t_ref, sp_ref, s_scr):
        @pl.when(pl.program_id(0) == 0)
        def _():
            s_scr[...] = jnp.zeros_like(s_scr)
        HS = range(cps * HEADS)
        hd = [i % HEADS for i in HS]
        rs = [slice((i // HEADS) * CHUNK, (i // HEADS + 1) * CHUNK) for i in HS]
        cs = [slice(hd[i] * HEAD_DIM, (hd[i] + 1) * HEAD_DIM) for i in HS]
        q = [qkv_ref[rs[i], hd[i] * HEAD_DIM:(hd[i] + 1) * HEAD_DIM] for i in HS]
        k = [qkv_ref[rs[i], W + hd[i] * HEAD_DIM:W + (hd[i] + 1) * HEAD_DIM] for i in HS]
        v = [qkv_ref[rs[i], 2 * W + hd[i] * HEAD_DIM:2 * W + (hd[i] + 1) * HEAD_DIM] for i in HS]
        hsc = [_head_scalars(sc_ref[rs[i], :], gr_ref, hd[i], i // HEADS) for i in HS]
        beta, gcc, gl, dmat = ([x[i] for x in hsc] for i in range(4))
        ii, jj = hsc[0][5], hsc[0][6]
        eg = [jnp.exp(gcc[h]) for h in HS]
        kb = [k[h] * beta[h] for h in HS]
        kk = [_mm_nt(kb[h], k[h]) for h in HS]
        qk = [_mm_nt(q[h], k[h]) for h in HS]
        n0 = [-jnp.where(ii > jj, kk[h] * dmat[h], 0.0) for h in HS]
        n1 = [_mm(n0[h], n0[h]) for h in HS]
        n2 = [_mm(n1[h], n1[h]) for h in HS]
        p01 = [n0[h] + n1[h] + _mm(n0[h], n1[h]) for h in HS]
        n3 = [_mm(n2[h], n2[h]) for h in HS]
        n4 = [_mm(n3[h], n3[h]) for h in HS]
        p23 = [n2[h] + n3[h] + _mm(n2[h], n3[h]) for h in HS]
        n5 = [_mm(n4[h], n4[h]) for h in HS]
        p03 = [p01[h] + p23[h] + _mm(p01[h], p23[h]) for h in HS]
        p45 = [n4[h] + n5[h] + _mm(n4[h], n5[h]) for h in HS]
        t = [p03[h] + p45[h] + _mm(p03[h], p45[h]) for h in HS]
        vb = [v[h] * beta[h] for h in HS]
        kbg = [kb[h] * eg[h] for h in HS]
        uw = [_mm(t[h], jnp.concatenate([vb[h], kbg[h]], axis=1)) for h in HS]
        u = [vb[h] + uw[h][:, :HEAD_DIM] for h in HS]
        w = [kbg[h] + uw[h][:, HEAD_DIM:] for h in HS]
        wq = [jnp.concatenate([w[h], q[h] * eg[h]], axis=0) for h in HS]
        p = [jnp.where(ii >= jj, qk[h] * dmat[h], 0.0) for h in HS]
        ks = [k[h] * jnp.exp(gl[h] - gcc[h]) for h in HS]
        s = [s_scr[h] for h in range(HEADS)]
        for ci in range(cps):
            IS = range(ci * HEADS, (ci + 1) * HEADS)
            ws = [_mm(wq[i], s[hd[i]]) for i in IS]
            vn = [u[i] - ws[hd[i]][:CHUNK] for i in IS]
            pv = [_mm(p[i], vn[hd[i]]) for i in IS]
            kv = [_mm_tn(ks[i], vn[hd[i]]) for i in IS]
            for i in IS:
                h = hd[i]
                sp_ref[ci, cs[i], :] = s[h]
                o_ref[rs[i], cs[i]] = ws[h][CHUNK:] + pv[h]
                vn_ref[rs[i], cs[i]] = vn[h].astype(bf16)
            s = [jnp.exp(gl[i]) * s[hd[i]] + kv[hd[i]] for i in IS]
        for h in range(HEADS):
            s_scr[h] = s[h]
        for i in HS:
            u_ref[rs[i], cs[i]] = u[i].astype(bf16)
            w_ref[rs[i], cs[i]] = w[i].astype(bf16)
            t_ref[i // HEADS, hd[i]] = t[i].astype(bf16)

    row = lambda c: (c, 0)
    act, act16 = jax.ShapeDtypeStruct((L, W), f32), jax.ShapeDtypeStruct((L, W), bf16)
    return _pcall(
        body, name="gdn_fwd", grid=(nc // cps,),
        in_specs=[pl.BlockSpec((rows_per_step, 3 * W), row), pl.BlockSpec((rows_per_step, LANE), row),
                  pl.BlockSpec((cps, HEADS, CHUNK), lambda c: (c, 0, 0))],
        out_specs=[pl.BlockSpec((rows_per_step, W), row)] * 4 + [
            pl.BlockSpec((cps, HEADS, CHUNK, CHUNK), lambda c: (c, 0, 0, 0)),
            pl.BlockSpec((cps, W, HEAD_DIM), lambda c: (c, 0, 0))],
        out_shape=[act, act16, act16, act16, jax.ShapeDtypeStruct((nc, HEADS, CHUNK, CHUNK), bf16),
                   jax.ShapeDtypeStruct((nc, W, HEAD_DIM), f32)],
        scratch_shapes=[pltpu.VMEM((HEADS, HEAD_DIM, HEAD_DIM), f32)],
        compiler_params=_cparams("arbitrary"),
    )(qkv, sc, gr)


def _gdn_gate(o, proj, gnw):
    L = o.shape[0]

    def body(o_ref, z_ref, w_ref, m_ref):
        for ls in HALVES:
            ov, z = o_ref[:, ls], z_ref[:, ls]
            rms = lax.rsqrt(jnp.mean(ov * ov, axis=-1, keepdims=True) + EPS)
            m_ref[:, ls] = (((ov * rms) * w_ref[...]) * (z * _sigmoid(z))).astype(bf16)

    return _pcall(
        body, name="gdn_gate", grid=(GDN_WIDTH // ELT_W,),
        in_specs=[pl.BlockSpec((L, ELT_W), lambda j: (0, j)), pl.BlockSpec((L, ELT_W), lambda j: (0, OFF_ZG // ELT_W + j)),
                  pl.BlockSpec((1, LANE), lambda j: (0, 0))],
        out_specs=pl.BlockSpec((L, ELT_W), lambda j: (0, j)),
        out_shape=jax.ShapeDtypeStruct((L, GDN_WIDTH + CONV_WIDTH), bf16),
        compiler_params=_cparams("parallel"),
    )(o, proj, gnw)


def _conv3(u, cw_ref, ls):
    return cw_ref[2:3, ls] * u + cw_ref[1:2, ls] * _shift_down(u, 1) + cw_ref[0:1, ls] * _shift_down(u, 2)


def _conv_specs(L):
    return [pl.BlockSpec((L, CONV_BLOCK), lambda j: (0, OFF_CONV // CONV_BLOCK + j)),
            pl.BlockSpec((3, ELT_W), lambda j: (0, j)), pl.BlockSpec((1, ELT_W), lambda j: (0, j))]


def _conv_parts(ls):
    return [slice(g * ELT_W + ls.start, g * ELT_W + ls.stop) for g in range(4)]


def _conv_fwd(proj, cw, cb, mix):
    L = proj.shape[0]

    def body(p_ref, cw_ref, cb_ref, mix_in, m_ref):
        for ls in HALVES:
            sb, sc_, sh, sz = _conv_parts(ls)
            z = p_ref[:, sz]
            cv = _conv3(p_ref[:, sc_] * p_ref[:, sh], cw_ref, ls) + cb_ref[:, ls]
            m_ref[:, ls] = ((p_ref[:, sb] * cv) * (z * _sigmoid(z))).astype(bf16)

    return _pcall(
        body, name="conv_fwd", grid=(CONV_WIDTH // ELT_W,),
        in_specs=_conv_specs(L) + [ANY], out_specs=pl.BlockSpec((L, ELT_W), lambda j: (0, GDN_WIDTH // ELT_W + j)),
        out_shape=jax.ShapeDtypeStruct(mix.shape, mix.dtype), input_output_aliases={3: 0},
        compiler_params=_cparams("parallel"),
    )(proj, cw, cb, mix)


def _out_proj_loss(x, mix, wo, fw, tgt):
    L = x.shape[0]
    tm = min(512, L)
    MW = GDN_WIDTH + CONV_WIDTH

    def body(x_ref, m_ref, wo_ref, fw_ref, t_ref, dy_ref, dyb_ref, dm_ref, gfw_ref, loss_ref):
        @pl.when(pl.program_id(0) == 0)
        def _():
            gfw_ref[...] = jnp.zeros_like(gfw_ref)
            loss_ref[...] = jnp.zeros_like(loss_ref)
        y = x_ref[...] + jnp.dot(m_ref[...], wo_ref[...], preferred_element_type=f32)
        r = lax.rsqrt(jnp.mean(y * y, axis=-1, keepdims=True) + EPS)
        yh = y * r
        fwv = fw_ref[...]
        diff = yh * fwv - t_ref[...]
        loss_ref[...] += jnp.sum(jnp.sum(diff * diff, axis=-1, keepdims=True), axis=0, keepdims=True) * (0.5 / D_MODEL)
        dout = diff * (1.0 / D_MODEL)
        gfw_ref[...] += jnp.sum(dout * yh, axis=0, keepdims=True)
        dyh = dout * fwv
        dy = r * (dyh - yh * jnp.mean(dyh * yh, axis=-1, keepdims=True))
        dy_ref[...] = dy
        dyb = dy.astype(bf16)
        dyb_ref[...] = dyb
        dm_ref[...] = lax.dot_general(dyb, wo_ref[...], (((1,), (1,)), ((), ())), preferred_element_type=f32)

    row = lambda i: (i, 0)
    fix = lambda i: (0, 0)
    act = jax.ShapeDtypeStruct((L, D_MODEL), f32)
    return _pcall(
        body, name="out_proj_loss", grid=(L // tm,),
        in_specs=[pl.BlockSpec((tm, D_MODEL), row), pl.BlockSpec((tm, MW), row), pl.BlockSpec((MW, D_MODEL), fix),
                  pl.BlockSpec((1, D_MODEL), fix), pl.BlockSpec((tm, D_MODEL), row)],
        out_specs=[pl.BlockSpec((tm, D_MODEL), row), pl.BlockSpec((tm, D_MODEL), row), pl.BlockSpec((tm, MW), row),
                   pl.BlockSpec((1, D_MODEL), fix), pl.BlockSpec((1, LANE), fix)],
        out_shape=[act, jax.ShapeDtypeStruct((L, D_MODEL), bf16), jax.ShapeDtypeStruct((L, MW), f32),
                   jax.ShapeDtypeStruct((1, D_MODEL), f32), jax.ShapeDtypeStruct((1, LANE), f32)],
        compiler_params=_cparams("arbitrary"),
    )(x, mix, wo, fw, tgt)


def _tn_matmul(a, b, name):
    L, M = a.shape
    N = b.shape[1]
    tm = 512 if M % 512 == 0 else (768 if M % 768 == 0 else M)

    def body(a_ref, b_ref, o_ref):
        o_ref[...] = lax.dot_general(a_ref[...], b_ref[...], (((0,), (0,)), ((), ())),
                                     preferred_element_type=f32).astype(o_ref.dtype)

    return _pcall(
        body, name=name, grid=(M // tm,),
        in_specs=[pl.BlockSpec((L, tm), lambda i: (0, i)), pl.BlockSpec((L, N), lambda i: (0, 0))],
        out_specs=pl.BlockSpec((tm, N), lambda i: (i, 0)),
        out_shape=jax.ShapeDtypeStruct((M, N), bf16),
        compiler_params=_cparams("parallel"),
    )(a, b)


def _gdn_gate_bwd(o, proj, gnw, dmix_a):
    L = o.shape[0]

    def body(o_ref, z_ref, w_ref, dm_ref, do_ref, dz_ref, gw_ref):
        @pl.when(pl.program_id(0) == 0)
        def _():
            gw_ref[...] = jnp.zeros_like(gw_ref)
        wv = w_ref[...]
        for ls in HALVES:
            ov, z, dm = o_ref[:, ls], z_ref[:, ls], dm_ref[:, ls]
            rms = lax.rsqrt(jnp.mean(ov * ov, axis=-1, keepdims=True) + EPS)
            xh = ov * rms
            sg = _sigmoid(z)
            d_on = dm * (z * sg)
            dz_ref[:, ls] = (dm * (xh * wv) * (sg * (1.0 + z * (1.0 - sg)))).astype(bf16)
            gw_ref[...] += jnp.sum(d_on * xh, axis=0, keepdims=True)
            dxh = d_on * wv
            do_ref[:, ls] = (rms * (dxh - xh * jnp.mean(dxh * xh, axis=-1, keepdims=True))).astype(bf16)

    wide = pl.BlockSpec((L, ELT_W), lambda j: (0, j))
    return _pcall(
        body, name="gdn_gate_bwd", grid=(GDN_WIDTH // ELT_W,),
        in_specs=[wide, pl.BlockSpec((L, ELT_W), lambda j: (0, OFF_ZG // ELT_W + j)),
                  pl.BlockSpec((1, LANE), lambda j: (0, 0)), wide],
        out_specs=[wide, pl.BlockSpec((L, ELT_W), lambda j: (0, OFF_ZG // ELT_W + j)),
                   pl.BlockSpec((1, LANE), lambda j: (0, 0))],
        out_shape=[jax.ShapeDtypeStruct((L, GDN_WIDTH), bf16), jax.ShapeDtypeStruct((L, PROJ_PAD), bf16),
                   jax.ShapeDtypeStruct((1, LANE), f32)],
        compiler_params=_cparams("arbitrary"),
    )(o, proj, gnw, dmix_a)


def _conv_bwd(proj, cw, cb, dmix_b, dproj):
    L = proj.shape[0]

    def body(p_ref, cw_ref, cb_ref, dm_ref, dproj_in, dp_ref, gcw_ref, gcb_ref):
        for ls in HALVES:
            sb, sc_, sh, sz_ = _conv_parts(ls)
            bv, cv_, hv, z, dm = p_ref[:, sb], p_ref[:, sc_], p_ref[:, sh], p_ref[:, sz_], dm_ref[:, ls]
            u = cv_ * hv
            cv = _conv3(u, cw_ref, ls) + cb_ref[:, ls]
            sg = _sigmoid(z)
            sz = z * sg
            dp_ref[:, sb] = (dm * cv * sz).astype(bf16)
            dp_ref[:, sz_] = (dm * (bv * cv) * (sg * (1.0 + z * (1.0 - sg)))).astype(bf16)
            dcv = dm * bv * sz
            gcb_ref[:, ls] = jnp.sum(dcv, axis=0, keepdims=True)
            dcv1, dcv2 = _shift_up(dcv, 1), _shift_up(dcv, 2)
            gcw_ref[2:3, ls] = jnp.sum(dcv * u, axis=0, keepdims=True)
            gcw_ref[1:2, ls] = jnp.sum(dcv1 * u, axis=0, keepdims=True)
            gcw_ref[0:1, ls] = jnp.sum(dcv2 * u, axis=0, keepdims=True)
            du = cw_ref[2:3, ls] * dcv + cw_ref[1:2, ls] * dcv1 + cw_ref[0:1, ls] * dcv2
            dp_ref[:, sc_] = (du * hv).astype(bf16)
            dp_ref[:, sh] = (du * cv_).astype(bf16)

    return _pcall(
        body, name="conv_bwd", grid=(CONV_WIDTH // ELT_W,),
        in_specs=_conv_specs(L) + [pl.BlockSpec((L, ELT_W), lambda j: (0, GDN_WIDTH // ELT_W + j)), ANY],
        out_specs=[pl.BlockSpec((L, CONV_BLOCK), lambda j: (0, OFF_CONV // CONV_BLOCK + j)),
                   pl.BlockSpec((3, ELT_W), lambda j: (0, j)), pl.BlockSpec((1, ELT_W), lambda j: (0, j))],
        out_shape=[jax.ShapeDtypeStruct(dproj.shape, dproj.dtype), jax.ShapeDtypeStruct((3, CONV_WIDTH), f32),
                   jax.ShapeDtypeStruct((1, CONV_WIDTH), f32)],
        input_output_aliases={4: 0},
        compiler_params=_cparams("parallel"),
    )(proj, cw, cb, dmix_b, dproj)


def _gdn_bwd(qkv, sc, gr, u_all, w_all, vn_all, t_all, sp_all, do_all):
    L = qkv.shape[0]
    nc = L // CHUNK
    W = GDN_WIDTH
    cps = GDN_CPS_BWD if nc % GDN_CPS_BWD == 0 else 1
    rows_per_step = cps * CHUNK
    nsteps = nc // cps

    def body(qkv_ref, sc_ref, gr_ref, u_ref, w_ref, vn_ref, t_ref, sp_ref, do_ref, dqkv_ref, dsc_ref, dgr_ref, ds_scr):
        @pl.when(pl.program_id(0) == 0)
        def _():
            ds_scr[...] = jnp.zeros_like(ds_scr)
        HS = range(cps * HEADS)
        hd = [i % HEADS for i in HS]
        rs = [slice((i // HEADS) * CHUNK, (i // HEADS + 1) * CHUNK) for i in HS]
        cs = [slice(hd[i] * HEAD_DIM, (hd[i] + 1) * HEAD_DIM) for i in HS]
        q = [qkv_ref[rs[i], hd[i] * HEAD_DIM:(hd[i] + 1) * HEAD_DIM] for i in HS]
        k = [qkv_ref[rs[i], W + hd[i] * HEAD_DIM:W + (hd[i] + 1) * HEAD_DIM] for i in HS]
        v = [qkv_ref[rs[i], 2 * W + hd[i] * HEAD_DIM:2 * W + (hd[i] + 1) * HEAD_DIM] for i in HS]
        hsc = [_head_scalars(sc_ref[rs[i], :], gr_ref, hd[i], i // HEADS) for i in HS]
        beta, gcc, gl, dmat, dmat_t = ([x[i] for x in hsc] for i in range(5))
        ii, jj = hsc[0][5], hsc[0][6]
        eg = [jnp.exp(gcc[h]) for h in HS]
        ekl = [jnp.exp(gl[h] - gcc[h]) for h in HS]
        egl = [jnp.exp(gl[h]) for h in HS]
        kb = [k[h] * beta[h] for h in HS]
        ks = [k[h] * ekl[h] for h in HS]
        do = [do_ref[rs[h], cs[h]] for h in HS]
        vn = [vn_ref[rs[h], cs[h]] for h in HS]
        s = [sp_ref[h // HEADS, cs[h], :] for h in HS]
        w = [w_ref[rs[h], cs[h]] for h in HS]
        qd = [q[h] * eg[h] for h in HS]

        kq = [_mm_nt(k[h], q[h]) for h in HS]
        p_t = [jnp.where(jj >= ii, kq[h] * dmat_t[h], 0.0) for h in HS]
        ptd = [_mm(p_t[h], do[h]) for h in HS]
        qw =[jnp.concatenate([qd[h], -w[h]], axis=0) for h in HS]
        dsn, dvn, dodv = [None] * len(HS), [None] * len(HS), [None] * len(HS)
        ds_cur = [ds_scr[h] for h in range(HEADS)]
        for ci in reversed(range(cps)):
            IS = range(ci * HEADS, (ci + 1) * HEADS)
            ksd = [_mm(ks[i], ds_cur[hd[i]]) for i in IS]
            for i in IS:
                dsn[i] = ds_cur[hd[i]]
                dvn[i] = ptd[i] + ksd[hd[i]]
                dodv[i] = jnp.concatenate([do[i], dvn[i]], axis=0)
            dsq = [_mm_tn(qw[i], dodv[i]) for i in IS]
            ds_cur = [egl[i] * ds_cur[hd[i]] + dsq[hd[i]] for i in IS]
        for h in range(HEADS):
            ds_scr[h] = ds_cur[h]
        x1 = [_mm_nt(dodv[h], s[h]) for h in HS]
        dks = [_mm_nt(vn[h], dsn[h]) for h in HS]
        dov = [_mm_nt(do[h], vn[h]) for h in HS]
        vdo = [_mm_nt(vn[h], do[h]) for h in HS]
        kk = [_mm_nt(kb[h], k[h]) for h in HS]
        qk = [_mm_nt(q[h], k[h]) for h in HS]
        dgl = [egl[h] * jnp.sum(jnp.sum(s[h] * dsn[h], axis=1, keepdims=True), axis=0, keepdims=True) for h in HS]
        dqd = [x1[h][:CHUNK] for h in HS]
        duw = [jnp.concatenate([dvn[h], -x1[h][CHUNK:]], axis=1) for h in HS]
        tdu = [_mm_tn(t_ref[h // HEADS, hd[h]], duw[h]) for h in HS]
        dvk = [duw[h] + tdu[h] for h in HS]
        uw = [jnp.concatenate([u_ref[rs[h], cs[h]], w[h]], axis=1) for h in HS]
        da = [-jnp.where(ii > jj, _mm_nt(dvk[h], uw[h]), 0.0) for h in HS]
        da_t = [-jnp.where(jj > ii, _mm_nt(uw[h], dvk[h]), 0.0) for h in HS]
        dp = [jnp.where(ii >= jj, dov[h], 0.0) for h in HS]
        dp_t = [jnp.where(jj >= ii, vdo[h], 0.0) for h in HS]
        r1 = [_mm(jnp.concatenate([da[h] * dmat[h], dp[h] * dmat[h]], axis=0), k[h]) for h in HS]
        dk1 = [_mm(jnp.concatenate([da_t[h] * dmat_t[h], dp_t[h] * dmat_t[h]], axis=1),
                   jnp.concatenate([kb[h], q[h]], axis=0)) for h in HS]
        lane = _lanes((CHUNK, LANE))
        for ci in range(cps):
            dsc = jnp.zeros((CHUNK, LANE), f32)
            for i in range(ci * HEADS, (ci + 1) * HEADS):
                h = hd[i]
                a = jnp.where(ii > jj, kk[i] * dmat[i], 0.0)
                p = jnp.where(ii >= jj, qk[i] * dmat[i], 0.0)
                gmat = da[i] * a + dp[i] * p
                dvb, dkbg = dvk[i][:, :HEAD_DIM], dvk[i][:, HEAD_DIM:]
                kbg = kb[i] * eg[i]
                dkb = r1[i][:CHUNK] + dkbg * eg[i]
                dq = r1[i][CHUNK:] + dqd[i] * eg[i]
                dk = dk1[i] + dks[i] * ekl[i] + dkb * beta[i]
                dbeta = jnp.sum(dkb * k[i] + dvb * v[i], axis=1, keepdims=True)
                ksum = jnp.sum(dks[i] * ks[i], axis=1, keepdims=True)
                dgl_tot = dgl[i] + jnp.sum(ksum, axis=0, keepdims=True)
                dgc = (jnp.sum(gmat, axis=1, keepdims=True) + jnp.sum(dqd[i] * qd[i] + dkbg * kbg, axis=1, keepdims=True)
                       - ksum)
                dgc = dgc + jnp.where(_rows(dgc.shape) == CHUNK - 1, dgl_tot, 0.0)
                dqkv_ref[rs[i], h * HEAD_DIM:(h + 1) * HEAD_DIM] = dq
                dqkv_ref[rs[i], W + h * HEAD_DIM:W + (h + 1) * HEAD_DIM] = dk
                dqkv_ref[rs[i], 2 * W + h * HEAD_DIM:2 * W + (h + 1) * HEAD_DIM] = dvb * beta[i]
                dsc = jnp.where(lane == h, dbeta, jnp.where(lane == HEADS + h, dgc, dsc))
                dgr_ref[ci, h:h + 1, :] = jnp.sum(gmat, axis=0, keepdims=True)
            dsc_ref[ci * CHUNK:(ci + 1) * CHUNK, :] = dsc

    row = lambda c: (nsteps - 1 - c, 0)
    lead3 = lambda c: (nsteps - 1 - c, 0, 0)
    return _pcall(
        body, name="gdn_bwd", grid=(nsteps,),
        in_specs=[pl.BlockSpec((rows_per_step, 3 * W), row), pl.BlockSpec((rows_per_step, LANE), row),
                  pl.BlockSpec((cps, HEADS, CHUNK), lead3),
                  pl.BlockSpec((rows_per_step, W), row), pl.BlockSpec((rows_per_step, W), row),
                  pl.BlockSpec((rows_per_step, W), row),
                  pl.BlockSpec((cps, HEADS, CHUNK, CHUNK), lambda c: (nsteps - 1 - c, 0, 0, 0)),
                  pl.BlockSpec((cps, W, HEAD_DIM), lead3), pl.BlockSpec((rows_per_step, W), row)],
        out_specs=[pl.BlockSpec((rows_per_step, 3 * W), row), pl.BlockSpec((rows_per_step, LANE), row),
                   pl.BlockSpec((cps, HEADS, CHUNK), lead3)],
        out_shape=[jax.ShapeDtypeStruct((L, 3 * W), f32), jax.ShapeDtypeStruct((L, LANE), f32),
                   jax.ShapeDtypeStruct((nc, HEADS, CHUNK), f32)],
        scratch_shapes=[pltpu.VMEM((HEADS, HEAD_DIM, HEAD_DIM), f32)],
        compiler_params=_cparams("arbitrary"),
    )(qkv, sc, gr, u_all, w_all, vn_all, t_all, sp_all, do_all)


def _qkv_bwd(proj, cw, dn, dproj):
    L = proj.shape[0]

    def body(x_ref, cw_ref, dn_ref, dproj_in, dx_ref, gcw_ref):
        j = pl.program_id(0)
        steps = GDN_WIDTH // ELT_W
        scale = jnp.where(j < steps, HEAD_DIM ** -0.5, 1.0).astype(f32)
        for ls in HALVES:
            x, dn_v = x_ref[:, ls], dn_ref[:, ls]
            c = _conv4(x, cw_ref, ls)
            sg = _sigmoid(c)
            a = c * sg
            rn = lax.rsqrt(jnp.sum(a * a, axis=1, keepdims=True) + EPS)
            da_n = (scale * rn) * (dn_v - a * ((rn * rn) * jnp.sum(dn_v * a, axis=1, keepdims=True)))
            da = jnp.where(j < 2 * steps, da_n, dn_v)
            dc = da * (sg * (1.0 + c * (1.0 - sg)))
            dc1, dc2, dc3 = _shift_up(dc, 1), _shift_up(dc, 2), _shift_up(dc, 3)
            gcw_ref[3:4, ls] = jnp.sum(dc * x, axis=0, keepdims=True)
            gcw_ref[2:3, ls] = jnp.sum(dc1 * x, axis=0, keepdims=True)
            gcw_ref[1:2, ls] = jnp.sum(dc2 * x, axis=0, keepdims=True)
            gcw_ref[0:1, ls] = jnp.sum(dc3 * x, axis=0, keepdims=True)
            dx = cw_ref[3:4, ls] * dc + cw_ref[2:3, ls] * dc1 + cw_ref[1:2, ls] * dc2 + cw_ref[0:1, ls] * dc3
            dx_ref[:, ls] = dx.astype(bf16)

    col = pl.BlockSpec((L, ELT_W), lambda j: (0, j))
    wspec = pl.BlockSpec((4, ELT_W), lambda j: (0, j))
    return _pcall(
        body, name="qkv_bwd", grid=(3 * GDN_WIDTH // ELT_W,),
        in_specs=[col, wspec, col, ANY], out_specs=[col, wspec],
        out_shape=[jax.ShapeDtypeStruct(dproj.shape, dproj.dtype), jax.ShapeDtypeStruct((4, 3 * GDN_WIDTH), f32)],
        input_output_aliases={3: 0},
        compiler_params=_cparams("parallel"),
    )(proj, cw, dn, dproj)


def _scalars_bwd(proj, alog_p, dtb_p, dsc, dgr_col, dproj):
    L = proj.shape[0]

    def body(x_ref, al_ref, dt_ref, dsc_ref, dgr_ref, dproj_in, dba_ref, gs_ref):
        x, dsc_v = x_ref[...], dsc_ref[...]
        lane = _lanes(x.shape)
        dec = (lane >= HEADS) & (lane < 2 * HEADS)
        dg = jnp.where(dec, dsc_v - dgr_ref[...], 0.0)
        rc = _rows(x.shape) & (CHUNK - 1)
        for s in (1, 2, 4, 8, 16, 32):
            dg = dg + jnp.where(rc + s < CHUNK, pltpu.roll(dg, L - s, 0), 0.0)
        xa = x + dt_ref[...]
        ea = jnp.exp(al_ref[...])
        g = -ea * _softplus(xa)
        da = dg * (-ea) * _sigmoid(xa)
        beta = _sigmoid(x)
        db = dsc_v * beta * (1.0 - beta)
        dba_ref[:, :LANE] = jnp.where(lane < HEADS, db, jnp.where(dec, da, 0.0)).astype(bf16)
        dba_ref[:, LANE:] = jnp.zeros((L, ELT_W - LANE), bf16)
        g_al = jnp.sum(jnp.where(dec, dg * g, 0.0), axis=0, keepdims=True)
        g_dt = jnp.sum(jnp.where(dec, da, 0.0), axis=0, keepdims=True)
        row8 = _rows(gs_ref.shape)
        gs = jnp.where(row8 == 0, g_al, jnp.where(row8 == 1, g_dt, 0.0))
        gs_ref[...] = pltpu.roll(gs, LANE - HEADS, 1)

    full = pl.BlockSpec((L, LANE), lambda i: (0, 0))
    vec = pl.BlockSpec((1, LANE), lambda i: (0, 0))
    return _pcall(
        body, name="scalars_bwd", grid=(1,),
        in_specs=[pl.BlockSpec((L, LANE), lambda i: (0, OFF_BA // LANE)), vec, vec, full, full, ANY],
        out_specs=[pl.BlockSpec((L, ELT_W), lambda i: (0, OFF_BA // ELT_W)), pl.BlockSpec((8, LANE), lambda i: (0, 0))],
        out_shape=[jax.ShapeDtypeStruct(dproj.shape, dproj.dtype), jax.ShapeDtypeStruct((8, LANE), f32)],
        input_output_aliases={5: 0},
        compiler_params=_cparams("arbitrary"),
    )(proj, alog_p, dtb_p, dsc, dgr_col, dproj)


def _input_grad(dproj, wpad, x, nw, dy):
    L = x.shape[0]
    tm = min(512, L)
    cuts = (0, 3072, 5120, 7168, PROJ_PAD)
    nk = len(cuts) - 1

    def body(dp_ref, w_hbm, x_ref, nw_ref, dy_ref, gx_ref, gnw_ref, w_vmem, sems):
        first = pl.program_id(0) == 0
        loads = [pltpu.make_async_copy(w_hbm.at[cuts[k]:cuts[k + 1], :], w_vmem.at[cuts[k]:cuts[k + 1], :], sems.at[k])
                 for k in range(nk)]

        @pl.when(first)
        def _():
            for cp in loads:
                cp.start()
            gnw_ref[...] = jnp.zeros_like(gnw_ref)
        dh = None
        for k in range(nk):
            pl.when(first)(loads[k].wait)
            part = jnp.dot(dp_ref[:, cuts[k]:cuts[k + 1]], w_vmem[cuts[k]:cuts[k + 1], :], preferred_element_type=f32)
            dh = part if dh is None else dh + part
        xv, nwv = x_ref[...], nw_ref[...]
        r = lax.rsqrt(jnp.mean(xv * xv, axis=-1, keepdims=True) + EPS)
        xh = xv * r
        gnw_ref[...] += jnp.sum(dh * xh, axis=0, keepdims=True)
        dxh = dh * nwv
        gx_ref[...] = dy_ref[...] + r * (dxh - xh * jnp.mean(dxh * xh, axis=-1, keepdims=True))

    row = lambda i: (i, 0)
    fix = lambda i: (0, 0)
    return _pcall(
        body, name="input_grad", grid=(L // tm,),
        in_specs=[pl.BlockSpec((tm, PROJ_PAD), row), ANY, pl.BlockSpec((tm, D_MODEL), row),
                  pl.BlockSpec((1, D_MODEL), fix), pl.BlockSpec((tm, D_MODEL), row)],
        out_specs=[pl.BlockSpec((tm, D_MODEL), row), pl.BlockSpec((1, D_MODEL), fix)],
        out_shape=[jax.ShapeDtypeStruct((L, D_MODEL), f32), jax.ShapeDtypeStruct((1, D_MODEL), f32)],
        scratch_shapes=[pltpu.VMEM(wpad.shape, bf16), pltpu.SemaphoreType.DMA((nk,))],
        compiler_params=_cparams("arbitrary"),
    )(dproj, wpad, x, nw, dy)


def _adamw_reduce(parts, w, m, v, name):
    R, C = w.shape
    n_parts = parts.shape[0]
    tr = 128 if R % 128 == 0 else R
    c1 = 1.0 - ADAM_B1 ** ADAM_STEP
    c2 = 1.0 - ADAM_B2 ** ADAM_STEP

    def body(p_ref, w_ref, m_ref, v_ref, g_ref, d_ref, nm_ref, nv_ref):
        g = p_ref[0].astype(f32)
        for s in range(1, n_parts):
            g = g + p_ref[s].astype(f32)
        nm = ADAM_B1 * m_ref[...] + (1.0 - ADAM_B1) * g
        nv = ADAM_B2 * v_ref[...] + (1.0 - ADAM_B2) * (g * g)
        g_ref[...] = g
        nm_ref[...] = nm
        nv_ref[...] = nv
        d_ref[...] = -ADAM_LR * ((nm / c1) / (jnp.sqrt(nv / c2) + ADAM_EPS) + ADAM_WD * w_ref[...])

    blk = pl.BlockSpec((tr, C), lambda i: (i, 0))
    out = jax.ShapeDtypeStruct((R, C), f32)
    return _pcall(
        body, name=name, grid=(R // tr,),
        in_specs=[pl.BlockSpec((n_parts, tr, C), lambda i: (0, i, 0)), blk, blk, blk],
        out_specs=[blk] * 4, out_shape=[out] * 4,
        compiler_params=_cparams("parallel"),
    )(parts, w, m, v)


SMALL_SLOTS = ((0, D_MODEL), (D_MODEL, D_MODEL), (2 * D_MODEL, D_MODEL), (3 * D_MODEL, LANE),
               (3 * D_MODEL + LANE, HEADS), (3 * D_MODEL + 2 * LANE, HEADS))
SMALL_LOSS = 3 * D_MODEL + 3 * LANE
SMALL_W = SMALL_LOSS + LANE


def _pack_small(gs, after):
    def body(nw_ref, cb_ref, fw_ref, gn_ref, sc_ref, ls_ref, after_ref, o_ref):
        for ref, (start, width) in zip((nw_ref, cb_ref, fw_ref, gn_ref), SMALL_SLOTS[:4]):
            o_ref[:, start:start + width] = ref[...]
        o_ref[:, SMALL_SLOTS[4][0]:SMALL_SLOTS[4][0] + LANE] = sc_ref[0:1, :]
        o_ref[:, SMALL_SLOTS[5][0]:SMALL_SLOTS[5][0] + LANE] = sc_ref[1:2, :]
        o_ref[:, SMALL_LOSS:SMALL_W] = ls_ref[...]

    vm = pl.BlockSpec(memory_space=pltpu.VMEM)
    return _pcall(body, name="pack_small_grads", out_shape=jax.ShapeDtypeStruct((1, SMALL_W), f32),
                  in_specs=[vm] * 6 + [ANY], out_specs=vm)(*gs, after)


def _adamw_small(parts, ws, ms, vs):
    c1 = 1.0 - ADAM_B1 ** ADAM_STEP
    c2 = 1.0 - ADAM_B2 ** ADAM_STEP
    np_ = len(ws)

    def body(*refs):
        p_ref = refs[0]
        w_refs, m_refs, v_refs = refs[1:1 + np_], refs[1 + np_:1 + 2 * np_], refs[1 + 2 * np_:1 + 3 * np_]
        outs = refs[1 + 3 * np_:]
        g_refs, d_refs, nm_refs, nv_refs = (outs[i * np_:(i + 1) * np_] for i in range(4))
        loss_ref = outs[4 * np_]

        def total(start, width):
            t = p_ref[0, :, start:start + width]
            for s in range(1, N_DEV):
                t = t + p_ref[s, :, start:start + width]
            return t

        for i, (start, width) in enumerate(SMALL_SLOTS):
            g = total(start, width)
            nm = ADAM_B1 * m_refs[i][...] + (1.0 - ADAM_B1) * g
            nv = ADAM_B2 * v_refs[i][...] + (1.0 - ADAM_B2) * (g * g)
            g_refs[i][...] = g
            nm_refs[i][...] = nm
            nv_refs[i][...] = nv
            d_refs[i][...] = -ADAM_LR * ((nm / c1) / (jnp.sqrt(nv / c2) + ADAM_EPS) + ADAM_WD * w_refs[i][...])
        loss_ref[...] = total(SMALL_LOSS, LANE)

    vm = pl.BlockSpec(memory_space=pltpu.VMEM)
    shapes = [jax.ShapeDtypeStruct(w.shape, f32) for w in ws]
    res = _pcall(body, name="adamw_small", out_shape=shapes * 4 + [jax.ShapeDtypeStruct((1, LANE), f32)],
                 in_specs=[vm] * (1 + 3 * np_), out_specs=[vm] * (4 * np_ + 1))(parts, *ws, *ms, *vs)
    return [res[i * np_:(i + 1) * np_] for i in range(4)], res[4 * np_]


def _adamw_w_in(parts, w3, m3, v3):
    n_parts, n, _ = parts.shape
    c1 = 1.0 - ADAM_B1 ** ADAM_STEP
    c2 = 1.0 - ADAM_B2 ** ADAM_STEP

    def body(p_ref, w_ref, m_ref, v_ref, g_ref, d_ref, nm_ref, nv_ref):
        g = p_ref[0].astype(f32)
        for s in range(1, n_parts):
            g = g + p_ref[s].astype(f32)
        nm = ADAM_B1 * m_ref[:, 0, :] + (1.0 - ADAM_B1) * g
        nv = ADAM_B2 * v_ref[:, 0, :] + (1.0 - ADAM_B2) * (g * g)
        g_ref[:, 0, :] = g
        nm_ref[:, 0, :] = nm
        nv_ref[:, 0, :] = nv
        d_ref[:, 0, :] = -ADAM_LR * ((nm / c1) / (jnp.sqrt(nv / c2) + ADAM_EPS) + ADAM_WD * w_ref[:, 0, :])

    tile = 2 * COL_TILE
    blk = pl.BlockSpec((n, 1, tile), lambda j: (0, 0, j))
    out = jax.ShapeDtypeStruct((n, 1, D_MODEL), f32)
    return _pcall(
        body, name="adamw_w_in", grid=(D_MODEL // tile,),
        in_specs=[pl.BlockSpec((n_parts, n, tile), lambda j: (0, 0, j)), blk, blk, blk],
        out_specs=[blk] * 4, out_shape=[out] * 4,
        compiler_params=_cparams("parallel"),
    )(parts, w3, m3, v3)


def _pad_lanes(vec8, start):
    return jnp.pad(vec8.reshape(1, -1), ((0, 0), (start, LANE - start - vec8.size)))


def kernel(x, norm_in_w, w_in, conv_qkv_w, A_log, dt_bias, gdn_norm_w, conv_w, conv_b, w_out, final_norm_w, loss_target, m_norm_in_w, m_w_in, m_conv_qkv_w, m_A_log, m_dt_bias, m_gdn_norm_w, m_conv_w, m_conv_b, m_w_out, m_final_norm_w, v_norm_in_w, v_w_in, v_conv_qkv_w, v_A_log, v_dt_bias, v_gdn_norm_w, v_conv_w, v_conv_b, v_w_out, v_final_norm_w):
    L = x.shape[1]
    nc = L // CHUNK
    xs = x[0]
    tgt = loss_target[0]
    fnw = final_norm_w.reshape(1, D_MODEL)

    as_rows = lambda a: jnp.transpose(a, (2, 0, 1))
    win_g, cqkv_g, cw_g = _all_gather([_cast_w_in(as_rows(w_in)), conv_qkv_w[0], conv_w[0]], "gather_weights",
                                      pieces=[4, 1, 1])
    wpad = _relayout_w_in(win_g)
    cqkv = jnp.concatenate([cqkv_g[d] for d in range(N_DEV)], axis=1)
    cw = jnp.concatenate([cw_g[d] for d in range(N_DEV)], axis=1)
    alog_p = _pad_lanes(A_log, HEADS)
    dtb_p = _pad_lanes(dt_bias, HEADS)
    me_flat, me_chip = _flat(*_mesh_pos()), 2 * lax.axis_index("x") + lax.axis_index("y")
    tok = lambda started: started[4][0:1, 0:1]
    wo_own = w_out[0].astype(bf16)
    wo_started = _spread_start(wo_own, wpad, "gather", "gather_w_out_start")

    proj, h = _in_proj(xs, norm_in_w + tok(wo_started), wpad)
    qkv = _qkv_act(proj, cqkv)
    sc, gr = _scalars(proj, alog_p, dtb_p)
    o, u_all, w_all, vn_all, t_all, sp_all = _gdn_fwd(qkv, sc, gr)
    mix = _conv_fwd(proj, cw, conv_b, _gdn_gate(o, proj, gdn_norm_w))
    wo = _own_slot(_spread_wait(wo_started, mix, "gather", "gather_w_out_wait"), wo_own, me_flat).reshape(-1, D_MODEL)
    dy, dyb, dmix, g_fnw, loss_v = _out_proj_loss(xs, mix, wo, fnw, tgt)

    g_wout = _tn_matmul(mix, dyb, "grad_w_out")
    g_wout = g_wout.reshape(N_DEV, -1, D_MODEL)
    g_wout_own = lax.dynamic_index_in_dim(g_wout, me_flat, 0, keepdims=False)
    gwo_started = _spread_start(g_wout, dyb, "scatter", "exchange_grad_w_out_start")
    do, dproj, g_gnw = _gdn_gate_bwd(o, proj, gdn_norm_w + tok(gwo_started), dmix)
    dproj, g_cw, g_cb = _conv_bwd(proj, cw, conv_b, dmix, dproj)
    dqkv_n, dsc, dgr = _gdn_bwd(qkv, sc, gr, u_all, w_all, vn_all, t_all, sp_all, do)
    dproj, g_cqkv = _qkv_bwd(proj, cqkv, dqkv_n, dproj)
    dgr_col = jnp.pad(dgr.transpose(0, 2, 1).reshape(L, HEADS), ((0, 0), (HEADS, LANE - 2 * HEADS)))
    dproj, g_sc = _scalars_bwd(proj, alog_p, dtb_p, dsc, dgr_col, dproj)
    g_win_blk = _grad_blocks(_tn_matmul(dproj, h, "grad_w_in"))

    (p_win,) = _pair_exchange([g_win_blk], "exchange_grads_pair")
    s_win = _pair_sum(g_win_blk, p_win, "pair_sum_w_in")
    s_win_own = lax.dynamic_index_in_dim(s_win, me_chip, 0, keepdims=False)
    r_cqkv, r_cw = _all_to_all(
        [g_cqkv.reshape(4, N_DEV, -1).transpose(1, 0, 2), g_cw.reshape(3, N_DEV, -1).transpose(1, 0, 2)],
        "exchange_small_sharded_grads")
    gwi_started = _spread_start(s_win, r_cw, "chips", "exchange_grads_chips_start")
    grad_x, g_nw = _input_grad(dproj, wpad, xs, norm_in_w + tok(gwi_started), dy)

    r_wout = _own_slot(_spread_wait(gwo_started, grad_x, "scatter", "exchange_grad_w_out_wait"), g_wout_own, me_flat)
    upd_wout =_adamw_reduce(r_wout, w_out[0], m_w_out[0], v_w_out[0], "adamw_w_out")
    upd_cqkv = _adamw_reduce(r_cqkv, conv_qkv_w[0], m_conv_qkv_w[0], v_conv_qkv_w[0], "adamw_conv_qkv_w")
    upd_cw = _adamw_reduce(r_cw, conv_w[0], m_conv_w[0], v_conv_w[0], "adamw_conv_w")

    r_win = _own_slot(_spread_wait(gwi_started, upd_cw[0], "chips", "exchange_grads_chips_wait"), s_win_own, me_chip)
    upd_win = [jnp.transpose(a, (1, 2, 0)) for a in _adamw_w_in(r_win, as_rows(w_in), as_rows(m_w_in), as_rows(v_w_in))]

    small_g = _pack_small([g_nw, g_cb, g_fnw, g_gnw, g_sc, loss_v], r_win)
    (small_all,) = _all_gather([small_g], "gather_small_grads")
    fvec = lambda a: a.reshape(1, D_MODEL)
    upd_small, loss_sum = _adamw_small(
        small_all,
        [norm_in_w, conv_b, fvec(final_norm_w), gdn_norm_w, A_log, dt_bias],
        [m_norm_in_w, m_conv_b, fvec(m_final_norm_w), m_gdn_norm_w, m_A_log, m_dt_bias],
        [v_norm_in_w, v_conv_b, fvec(v_final_norm_w), v_gdn_norm_w, v_A_log, v_dt_bias])

    outs = [loss_sum[0, 0], grad_x[None]]
    for k in range(4):
        nw_k, cb_k, fw_k, gn_k, al_k, dt_k = upd_small[k]
        outs += [nw_k, upd_win[k], upd_cqkv[k][None], al_k, dt_k, gn_k,
                 upd_cw[k][None], cb_k, upd_wout[k][None], fw_k.reshape(D_MODEL)]
    return tuple(outs)
```

```python
import jax
import jax.numpy as jnp
from jax import lax
from jax.experimental import pallas as pl
from jax.experimental.pallas import tpu as pltpu

f32 = jnp.float32
bf16 = jnp.bfloat16

N_DEV = 8
D_MODEL = 1024
HEADS = 8
HEAD_DIM = 128
CHUNK = 64
GDN_CPS = 4
GDN_CPS_BWD = 1
GDN_WIDTH = HEADS * HEAD_DIM
CONV_WIDTH = 1024
PROJ_WIDTH = 8208
SHARD_W = PROJ_WIDTH // N_DEV
EPS = 1e-6

LANE = 128
ELT_W = 256

OFF_QKV, OFF_ZG, OFF_CONV, OFF_BA = 0, 3072, 4096, 8192
CONV_BLOCK = 4 * ELT_W
PROJ_PAD = 8448
NAT_BA, NAT_CONV = 4096, 4112


def _padded_col(n):
    if n < NAT_BA:
        return n
    if n < NAT_CONV:
        return OFF_BA + n - NAT_BA
    g, ch = divmod(n - NAT_CONV, CONV_WIDTH)
    j, r = divmod(ch, ELT_W)
    return OFF_CONV + CONV_BLOCK * j + ELT_W * g + r


def _layout_segments(n0, n1):
    cuts = [NAT_BA, NAT_CONV] + [NAT_CONV + ELT_W * k for k in range(1, 4 * CONV_WIDTH // ELT_W)]
    pts = [n0] + [c for c in cuts if n0 < c < n1] + [n1]
    return [(lo, hi - lo, _padded_col(lo)) for lo, hi in zip(pts, pts[1:])]

ADAM_LR, ADAM_B1, ADAM_B2, ADAM_EPS, ADAM_WD, ADAM_STEP = 0.001, 0.9, 0.999, 1e-08, 0.01, 10

V7X_VMEM_BYTES = 64 * 1024 * 1024
VMEM_LIMIT = V7X_VMEM_BYTES - 8 * 1024 * 1024

MESH = pl.DeviceIdType.MESH
ANY = pl.BlockSpec(memory_space=pl.ANY)


def _pcall(body, **kw):
    return pl.pallas_call(body, **kw)


def _cparams(*sem):
    return pltpu.CompilerParams(dimension_semantics=sem if sem else None, vmem_limit_bytes=VMEM_LIMIT)


def _mm(a, b):
    return jnp.dot(a.astype(bf16), b.astype(bf16), preferred_element_type=f32)


def _mm_nt(a, b):
    return lax.dot_general(a.astype(bf16), b.astype(bf16), (((1,), (1,)), ((), ())), preferred_element_type=f32)


def _mm_tn(a, b):
    return lax.dot_general(a.astype(bf16), b.astype(bf16), (((0,), (0,)), ((), ())), preferred_element_type=f32)


def _rows(shape):
    return lax.broadcasted_iota(jnp.int32, shape, 0)


def _lanes(shape):
    return lax.broadcasted_iota(jnp.int32, shape, 1)


def _shift_down(x, s):
    if s == 0:
        return x
    return jnp.where(_rows(x.shape) >= s, pltpu.roll(x, s, 0), 0.0)


def _shift_up(x, s):
    if s == 0:
        return x
    n = x.shape[0]
    return jnp.where(_rows(x.shape) < n - s, pltpu.roll(x, n - s, 0), 0.0)


def _sigmoid(x):
    return jax.nn.sigmoid(x)


def _softplus(x):
    e = jnp.exp(-jnp.abs(x))
    small = e * (1.0 - e * (0.5 - e * (1.0 / 3.0)))
    return jnp.maximum(x, 0.0) + jnp.where(e < 0.01, small, jnp.log(1.0 + e))


def _mesh_pos():
    return lax.axis_index("x"), lax.axis_index("y"), lax.axis_index("c")


def _flat(px, py, pc):
    return 4 * px + 2 * py + pc


def _all_gather(xs, name, pieces=None):
    n = len(xs)
    pieces = pieces or [1] * n
    items = [(a, q) for a in range(n) for q in range(pieces[a])]
    ni = len(items)

    def view(ref, i):
        a, q = items[i]
        if pieces[a] == 1:
            return ref
        wd = xs[a].shape[-1] // pieces[a]
        return ref.at[(slice(None),) * (xs[a].ndim - 1) + (pl.ds(q * wd, wd),)]

    def body(*refs):
        x_refs, o_refs = refs[:n], refs[n:2 * n]
        send_sems, recv_sems, local_sems = refs[2 * n:]
        x, y, c = _mesh_pos()
        me, sibling = (x, y, c), (x, y, 1 - c)
        flip = lambda v, bit: v + bit - 2 * v * bit
        nbr_a = (flip(x, 1 - c), flip(y, c))
        nbr_b = (flip(x, c), flip(y, 1 - c))
        diag = (1 - x, 1 - y)

        def copy(i, k, block, to, own=False):
            a = items[i][0]
            dst = view(o_refs[a].at[_flat(*block)], i)
            return pltpu.make_async_remote_copy(
                src_ref=view(x_refs[a], i) if own else dst, dst_ref=dst,
                send_sem=send_sems.at[i, k], recv_sem=recv_sems.at[i, k], device_id=to, device_id_type=MESH)

        mine, sent = [], []

        def go(cp):
            cp.start()
            sent.append(cp)

        for a in range(n):
            cp = pltpu.make_async_copy(x_refs[a], o_refs[a].at[_flat(*me)], local_sems.at[a])
            cp.start()
            mine.append(cp)
        for a in range(ni):
            go(copy(a, 1, me, (*nbr_a, c), own=True))
            go(copy(a, 2, me, (*nbr_b, c), own=True))
            go(copy(a, 0, me, sibling, own=True))
        for a in range(ni):
            copy(a, 1, (*nbr_a, c), me).wait_recv()
            go(copy(a, 3, (*nbr_a, c), (*nbr_b, c)))
            go(copy(a, 4, (*nbr_a, c), sibling))
        for a in range(ni):
            copy(a, 2, (*nbr_b, c), me).wait_recv()
            go(copy(a, 5, (*nbr_b, c), sibling))
        for a in range(ni):
            copy(a, 3, (*diag, c), me).wait_recv()
            go(copy(a, 6, (*diag, c), sibling))
        for a in range(ni):
            copy(a, 0, sibling, me).wait_recv()
            copy(a, 4, (*nbr_b, 1 - c), me).wait_recv()
            copy(a, 5, (*nbr_a, 1 - c), me).wait_recv()
            copy(a, 6, (*diag, 1 - c), me).wait_recv()
        for cp in sent:
            cp.wait_send()
        for cp in mine:
            cp.wait()

    outs = _pcall(
        body, name=name,
        out_shape=[jax.ShapeDtypeStruct((N_DEV,) + a.shape, a.dtype) for a in xs],
        in_specs=[ANY] * n, out_specs=[ANY] * n,
        scratch_shapes=[pltpu.SemaphoreType.DMA((ni, 7)), pltpu.SemaphoreType.DMA((ni, 7)), pltpu.SemaphoreType.DMA((n,))],
    )(*xs)
    return list(outs)


def _all_to_all(gs, name):
    n = len(gs)

    def body(*refs):
        g_refs, o_refs = refs[:n], refs[n:2 * n]
        send_sems, recv_sems, local_sems = refs[2 * n:]
        x, y, c = _mesh_pos()
        me = _flat(x, y, c)
        peers = []
        for k in range(1, N_DEV):
            kx, ky, kc = (k >> 2) & 1, (k >> 1) & 1, k & 1
            px = (1 - x) if kx else x
            py = (1 - y) if ky else y
            pc = (1 - c) if kc else c
            peers.append((px, py, pc))

        def copy(a, k):
            peer = peers[k - 1]
            return pltpu.make_async_remote_copy(
                src_ref=g_refs[a].at[_flat(*peer)], dst_ref=o_refs[a].at[me],
                send_sem=send_sems.at[a, k - 1], recv_sem=recv_sems.at[a, k - 1], device_id=peer, device_id_type=MESH)

        def arrival(a, k):
            peer = peers[k - 1]
            return pltpu.make_async_remote_copy(
                src_ref=g_refs[a].at[me], dst_ref=o_refs[a].at[_flat(*peer)],
                send_sem=send_sems.at[a, k - 1], recv_sem=recv_sems.at[a, k - 1], device_id=peer, device_id_type=MESH)

        mine, sent = [], []
        for a in range(n):
            cp = pltpu.make_async_copy(g_refs[a].at[me], o_refs[a].at[me], local_sems.at[a])
            cp.start()
            mine.append(cp)
            for k in range(1, N_DEV):
                cp = copy(a, k)
                cp.start()
                sent.append(cp)
        for a in range(n):
            for k in range(1, N_DEV):
                arrival(a, k).wait_recv()
        for cp in sent:
            cp.wait_send()
        for cp in mine:
            cp.wait()

    outs = _pcall(
        body, name=name,
        out_shape=[jax.ShapeDtypeStruct(a.shape, a.dtype) for a in gs],
        in_specs=[ANY] * n, out_specs=[ANY] * n,
        scratch_shapes=[pltpu.SemaphoreType.DMA((n, 7)), pltpu.SemaphoreType.DMA((n, 7)), pltpu.SemaphoreType.DMA((n,))],
    )(*gs)
    return list(outs)


def _pair_exchange(gs, name):
    n = len(gs)
    chips = [(0, 0), (0, 1), (1, 0), (1, 1)]

    def body(*refs):
        g_refs, o_refs = refs[:n], refs[n:2 * n]
        send_sems, recv_sems = refs[2 * n:]
        x, y, c = _mesh_pos()
        sibling = (x, y, 1 - c)

        def copy(a, i):
            xp, yp = chips[i]
            return pltpu.make_async_remote_copy(
                src_ref=g_refs[a].at[_flat(xp, yp, 1 - c)], dst_ref=o_refs[a].at[i],
                send_sem=send_sems.at[a, i], recv_sem=recv_sems.at[a, i], device_id=sibling, device_id_type=MESH)

        cps = [copy(a, i) for a in range(n) for i in range(4)]
        for cp in cps:
            cp.start()
        for cp in cps:
            cp.wait()

    outs = _pcall(
        body, name=name,
        out_shape=[jax.ShapeDtypeStruct((4,) + a.shape[1:], a.dtype) for a in gs],
        in_specs=[ANY] * n, out_specs=[ANY] * n,
        scratch_shapes=[pltpu.SemaphoreType.DMA((n, 4)), pltpu.SemaphoreType.DMA((n, 4))],
    )(*gs)
    return list(outs)


def _pair_sum(g, p1, name):
    _, R, C = g.shape
    tr = 256 if R % 256 == 0 else R
    cidx = lax.axis_index("c").astype(jnp.int32).reshape(1)

    def body(c_ref, g_ref, p_ref, o_ref):
        o_ref[...] = (g_ref[...].astype(f32) + p_ref[...].astype(f32)).astype(o_ref.dtype)

    return _pcall(
        body, name=name,
        grid_spec=pltpu.PrefetchScalarGridSpec(
            num_scalar_prefetch=1, grid=(4, R // tr),
            in_specs=[pl.BlockSpec((1, tr, C), lambda i, r, c_ref: (2 * i + c_ref[0], r, 0)),
                      pl.BlockSpec((1, tr, C), lambda i, r, c_ref: (i, r, 0))],
            out_specs=pl.BlockSpec((1, tr, C), lambda i, r, c_ref: (i, r, 0))),
        out_shape=jax.ShapeDtypeStruct((4, R, C), g.dtype),
        compiler_params=_cparams("parallel", "parallel"),
    )(cidx, g, p1)


HBM = pl.BlockSpec(memory_space=pltpu.HBM)
SEM = pl.BlockSpec(memory_space=pltpu.SEMAPHORE)
EFFECT = pltpu.SideEffectType.DATAFLOW_SIDE_EFFECTING


def _peers(x, y, c):
    out = []
    for k in range(1, N_DEV):
        kx, ky, kc = (k >> 2) & 1, (k >> 1) & 1, k & 1
        out.append(((1 - x) if kx else x, (1 - y) if ky else y, (1 - c) if kc else c))
    return out


SPREAD_COPIES = {"gather": N_DEV - 1, "scatter": N_DEV - 1, "chips": 3}


def _spread_copy(src_ref, land_ref, send_sems, recv_sems, k, plan):
    x, y, c = _mesh_pos()
    if plan == "chips":
        px, py = [(1 - x, y), (x, 1 - y), (1 - x, 1 - y)][k]
        peer, src, slot = (px, py, c), src_ref.at[2 * px + py], 2 * x + y
    else:
        peer = _peers(x, y, c)[k]
        src, slot = (src_ref.at[_flat(*peer)] if plan == "scatter" else src_ref), _flat(x, y, c)
    return pltpu.make_async_remote_copy(
        src_ref=src, dst_ref=land_ref.at[slot], send_sem=send_sems.at[k], recv_sem=recv_sems.at[k],
        device_id=peer, device_id_type=MESH)


def _spread_start(src, after, plan, name):
    land_shape = (N_DEV,) + src.shape if plan == "gather" else src.shape
    n_copies = SPREAD_COPIES[plan]

    def body(src_ref, land_ref, after_ref, send_sems, recv_sems, src_thru, land_thru, token):
        for k in range(n_copies):
            _spread_copy(src_ref, land_ref, send_sems, recv_sems, k, plan).start()
        token[...] = jnp.zeros_like(token)

    return _pcall(
        body, name=name,
        out_shape=(pltpu.SemaphoreType.DMA((n_copies,)), pltpu.SemaphoreType.DMA((n_copies,)),
                   pltpu.HBM(src.shape, src.dtype), pltpu.HBM(land_shape, src.dtype), jax.ShapeDtypeStruct((8, LANE), f32)),
        in_specs=(HBM, HBM, ANY), out_specs=(SEM, SEM, HBM, HBM, pl.BlockSpec(memory_space=pltpu.VMEM)),
        input_output_aliases={0: 2, 1: 3},
        compiler_params=pltpu.CompilerParams(has_side_effects=EFFECT),
    )(pltpu.with_memory_space_constraint(src, pltpu.HBM),
      pltpu.with_memory_space_constraint(lax.empty(land_shape, src.dtype), pltpu.HBM), after)


def _spread_wait(started, after, plan, name):
    send_sems, recv_sems, src_thru, land_thru, _ = started

    def body(src_ref, land_ref, send_sems, recv_sems, after_ref, src_dead, got_ref):
        for k in range(SPREAD_COPIES[plan]):
            cp = _spread_copy(src_ref, land_ref, send_sems, recv_sems, k, plan)
            cp.wait_send()
            cp.wait_recv()

    return _pcall(
        body, name=name,
        out_shape=(pltpu.HBM(src_thru.shape, src_thru.dtype), pltpu.HBM(land_thru.shape, land_thru.dtype)),
        in_specs=(HBM, HBM, SEM, SEM, ANY), out_specs=(HBM, HBM), input_output_aliases={0: 0, 1: 1},
        compiler_params=pltpu.CompilerParams(has_side_effects=EFFECT),
    )(src_thru, land_thru, send_sems, recv_sems, after)


def _spread_finish(started, after, plan, name, slot):
    src, land = _spread_wait(started, after, plan, name)
    block = src if plan == "gather" else lax.dynamic_index_in_dim(src, slot, 0, keepdims=False)
    return _own_slot(land, block, slot)


def _own_slot(land, block, slot):
    zero = jnp.zeros((), jnp.int32)
    return lax.dynamic_update_slice(land, block[None], (slot.astype(jnp.int32),) + (zero,) * block.ndim)


COL_TILE = 256


def _cast_w_in(w3):
    n = w3.shape[0]

    def body(w_ref, o_ref):
        o_ref[...] = w_ref[:, 0, :].astype(bf16)

    tile = 2 * COL_TILE
    return _pcall(
        body, name="cast_w_in", grid=(D_MODEL // tile,),
        in_specs=[pl.BlockSpec((n, 1, tile), lambda j: (0, 0, j))],
        out_specs=pl.BlockSpec((n, tile), lambda j: (0, j)),
        out_shape=jax.ShapeDtypeStruct((n, D_MODEL), bf16),
        compiler_params=_cparams("parallel"),
    )(w3)


def _relayout_w_in(win_g):
    def body(g_ref, o_ref):
        used = OFF_BA + NAT_CONV - NAT_BA
        o_ref[used:PROJ_PAD, :] = jnp.zeros((PROJ_PAD - used, COL_TILE), o_ref.dtype)
        for d in range(N_DEV):
            for lo, width, dst in _layout_segments(d * SHARD_W, (d + 1) * SHARD_W):
                src = lo - d * SHARD_W
                o_ref[dst:dst + width, :] = g_ref[d, src:src + width, :]

    return _pcall(
        body, name="relayout_w_in", grid=(D_MODEL // COL_TILE,),
        in_specs=[pl.BlockSpec((N_DEV, SHARD_W, COL_TILE), lambda j: (0, 0, j))],
        out_specs=pl.BlockSpec((PROJ_PAD, COL_TILE), lambda j: (0, j)),
        out_shape=jax.ShapeDtypeStruct((PROJ_PAD, D_MODEL), win_g.dtype),
        compiler_params=_cparams("parallel"),
    )(win_g)


def _grad_blocks(g_t):
    def body(p_ref, o_ref):
        for d in range(N_DEV):
            for lo, width, src in _layout_segments(d * SHARD_W, (d + 1) * SHARD_W):
                dst = lo - d * SHARD_W
                o_ref[d, dst:dst + width, :] = p_ref[src:src + width, :]

    return _pcall(
        body, name="grad_blocks", grid=(D_MODEL // COL_TILE,),
        in_specs=[pl.BlockSpec((PROJ_PAD, COL_TILE), lambda j: (0, j))],
        out_specs=pl.BlockSpec((N_DEV, SHARD_W, COL_TILE), lambda j: (0, 0, j)),
        out_shape=jax.ShapeDtypeStruct((N_DEV, SHARD_W, D_MODEL), bf16),
        compiler_params=_cparams("parallel"),
    )(g_t)


def _in_proj(x, nw, wpad_t):
    L = x.shape[0]
    tn = 768
    nj = wpad_t.shape[0] // tn

    def body(x_ref, nw_ref, w_ref, proj_ref, h_ref):
        @pl.when(pl.program_id(0) == 0)
        def _():
            for r in range(0, L, 256):
                xs = x_ref[r:r + 256, :]
                ms = jnp.mean(xs * xs, axis=-1, keepdims=True)
                h_ref[r:r + 256, :] = ((xs * lax.rsqrt(ms + EPS)) * nw_ref[...]).astype(bf16)
        for r in range(0, L, 512):
            proj_ref[r:r + 512, :] = lax.dot_general(h_ref[r:r + 512, :], w_ref[...], (((1,), (1,)), ((), ())),
                                                     preferred_element_type=f32)

    return _pcall(
        body, name="in_proj", grid=(nj,),
        in_specs=[pl.BlockSpec((L, D_MODEL), lambda j: (0, 0)), pl.BlockSpec((1, D_MODEL), lambda j: (0, 0)),
                  pl.BlockSpec((tn, D_MODEL), lambda j: (j, 0))],
        out_specs=[pl.BlockSpec((L, tn), lambda j: (0, j)), pl.BlockSpec((L, D_MODEL), lambda j: (0, 0))],
        out_shape=[jax.ShapeDtypeStruct((L, wpad_t.shape[0]), f32), jax.ShapeDtypeStruct((L, D_MODEL), bf16)],
        compiler_params=_cparams("arbitrary"),
    )(x, nw, wpad_t)


HALVES = [slice(i * LANE, (i + 1) * LANE) for i in range(ELT_W // LANE)]
QKV_W = 512
QKV_HEADS = [slice(i * LANE, (i + 1) * LANE) for i in range(QKV_W // LANE)]
STEPS_PER_GROUP = GDN_WIDTH // QKV_W


def _conv4(x, cw_ref, ls):
    return (cw_ref[3:4, ls] * x + cw_ref[2:3, ls] * _shift_down(x, 1) + cw_ref[1:2, ls] * _shift_down(x, 2)
            + cw_ref[0:1, ls] * _shift_down(x, 3))


def _qkv_act(proj, cw):
    L = proj.shape[0]

    def body(x_ref, cw_ref, o_ref):
        j = pl.program_id(0)
        scale = jnp.where(j < STEPS_PER_GROUP, HEAD_DIM ** -0.5, 1.0).astype(f32)
        for ls in QKV_HEADS:
            c = _conv4(x_ref[:, ls], cw_ref, ls)
            a = c * _sigmoid(c)
            rn = lax.rsqrt(jnp.sum(a * a, axis=1, keepdims=True) + EPS)
            o_ref[:, ls] = jnp.where(j < 2 * STEPS_PER_GROUP, (a * rn) * scale, a)

    return _pcall(
        body, name="qkv_act", grid=(3 * STEPS_PER_GROUP,),
        in_specs=[pl.BlockSpec((L, QKV_W), lambda j: (0, j)), pl.BlockSpec((4, QKV_W), lambda j: (0, j))],
        out_specs=pl.BlockSpec((L, QKV_W), lambda j: (0, j)),
        out_shape=jax.ShapeDtypeStruct((L, 3 * GDN_WIDTH), f32),
        compiler_params=_cparams("parallel"),
    )(proj, cw)


def _scalars(proj, alog_p, dtb_p):
    L = proj.shape[0]
    nc = L // CHUNK

    def body(x_ref, al_ref, dt_ref, sc_ref, gr_ref):
        x = x_ref[...]
        lane = _lanes(x.shape)
        beta = _sigmoid(x)
        g = -jnp.exp(al_ref[...]) * _softplus(x + dt_ref[...])
        gc = jnp.where((lane >= HEADS) & (lane < 2 * HEADS), g, 0.0)
        rc = _rows(x.shape) & (CHUNK - 1)
        for s in (1, 2, 4, 8, 16, 32):
            gc = gc + jnp.where(rc >= s, pltpu.roll(gc, s, 0), 0.0)
        sc_ref[...] = jnp.where(lane < HEADS, beta, gc)
        sel = (_lanes((HEADS, LANE)) == _rows((HEADS, LANE)) + HEADS).astype(f32)
        for c in range(nc):
            gr_ref[c] = lax.dot_general(sel, sc_ref[c * CHUNK:(c + 1) * CHUNK, :], (((1,), (1,)), ((), ())),
                                        preferred_element_type=f32, precision=lax.Precision.HIGHEST)

    return _pcall(
        body, name="scalars", grid=(1,),
        in_specs=[pl.BlockSpec((L, LANE), lambda i: (0, OFF_BA // LANE)), pl.BlockSpec((1, LANE), lambda i: (0, 0)),
                  pl.BlockSpec((1, LANE), lambda i: (0, 0))],
        out_specs=[pl.BlockSpec((L, LANE), lambda i: (0, 0)), pl.BlockSpec((nc, HEADS, CHUNK), lambda i: (0, 0, 0))],
        out_shape=[jax.ShapeDtypeStruct((L, LANE), f32), jax.ShapeDtypeStruct((nc, HEADS, CHUNK), f32)],
        compiler_params=_cparams("arbitrary"),
    )(proj, alog_p, dtb_p)


def _head_scalars(sc, gr_ref, h, ci=0):
    lane = _lanes(sc.shape)
    beta = jnp.sum(jnp.where(lane == h, sc, 0.0), axis=1, keepdims=True)
    gcc = jnp.sum(jnp.where(lane == HEADS + h, sc, 0.0), axis=1, keepdims=True)
    gcr = gr_ref[ci, h:h + 1, :]
    gl = jnp.sum(jnp.where(_lanes(gcr.shape) == CHUNK - 1, gcr, 0.0), axis=1, keepdims=True)
    ii, jj = _rows((CHUNK, CHUNK)), _lanes((CHUNK, CHUNK))
    dmat = jnp.where(ii >= jj, jnp.exp(jnp.minimum(gcc - gcr, 0.0)), 0.0)
    dmat_t = jnp.where(jj >= ii, jnp.exp(jnp.minimum(gcr - gcc, 0.0)), 0.0)
    return beta, gcc, gl, dmat, dmat_t, ii, jj


def _gdn_fwd(qkv, sc, gr):
    L = qkv.shape[0]
    nc = L // CHUNK
    W = GDN_WIDTH
    cps = GDN_CPS if nc % GDN_CPS == 0 else 1
    rows_per_step = cps * CHUNK

    def body(qkv_ref, sc_ref, gr_ref, o_ref, u_ref, w_ref, vn_ref, t_ref, sp_ref, s_scr):
        @pl.when(pl.program_id(0) == 0)
        def _():
            s_scr[...] = jnp.zeros_like(s_scr)
        HS = range(cps * HEADS)
        hd = [i % HEADS for i in HS]
        rs = [slice((i // HEADS) * CHUNK, (i // HEADS + 1) * CHUNK) for i in HS]
        cs = [slice(hd[i] * HEAD_DIM, (hd[i] + 1) * HEAD_DIM) for i in HS]
        q = [qkv_ref[rs[i], hd[i] * HEAD_DIM:(hd[i] + 1) * HEAD_DIM] for i in HS]
        k = [qkv_ref[rs[i], W + hd[i] * HEAD_DIM:W + (hd[i] + 1) * HEAD_DIM] for i in HS]
        v = [qkv_ref[rs[i], 2 * W + hd[i] * HEAD_DIM:2 * W + (hd[i] + 1) * HEAD_DIM] for i in HS]
        hsc = [_head_scalars(sc_ref[rs[i], :], gr_ref, hd[i], i // HEADS) for i in HS]
        beta, gcc, gl, dmat = ([x[i] for x in hsc] for i in range(4))
        ii, jj = hsc[0][5], hsc[0][6]
        eg = [jnp.exp(gcc[h]) for h in HS]
        kb = [k[h] * beta[h] for h in HS]
        kk = [_mm_nt(kb[h], k[h]) for h in HS]
        qk = [_mm_nt(q[h], k[h]) for h in HS]
        n0 = [-jnp.where(ii > jj, kk[h] * dmat[h], 0.0) for h in HS]
        n1 = [_mm(n0[h], n0[h]) for h in HS]
        n2 = [_mm(n1[h], n1[h]) for h in HS]
        p01 = [n0[h] + n1[h] + _mm(n0[h], n1[h]) for h in HS]
        n3 = [_mm(n2[h], n2[h]) for h in HS]
        n4 = [_mm(n3[h], n3[h]) for h in HS]
        p23 = [n2[h] + n3[h] + _mm(n2[h], n3[h]) for h in HS]
        n5 = [_mm(n4[h], n4[h]) for h in HS]
        p03 = [p01[h] + p23[h] + _mm(p01[h], p23[h]) for h in HS]
        p45 = [n4[h] + n5[h] + _mm(n4[h], n5[h]) for h in HS]
        t = [p03[h] + p45[h] + _mm(p03[h], p45[h]) for h in HS]
        vb = [v[h] * beta[h] for h in HS]
        kbg = [kb[h] * eg[h] for h in HS]
        uw = [_mm(t[h], jnp.concatenate([vb[h], kbg[h]], axis=1)) for h in HS]
        u = [vb[h] + uw[h][:, :HEAD_DIM] for h in HS]
        w = [kbg[h] + uw[h][:, HEAD_DIM:] for h in HS]
        wq = [jnp.concatenate([w[h], q[h] * eg[h]], axis=0) for h in HS]
        p = [jnp.where(ii >= jj, qk[h] * dmat[h], 0.0) for h in HS]
        ks = [k[h] * jnp.exp(gl[h] - gcc[h]) for h in HS]
        s = [s_scr[h] for h in range(HEADS)]
        for ci in range(cps):
            IS = range(ci * HEADS, (ci + 1) * HEADS)
            ws = [_mm(wq[i], s[hd[i]]) for i in IS]
            vn = [u[i] - ws[hd[i]][:CHUNK] for i in IS]
            pv = [_mm(p[i], vn[hd[i]]) for i in IS]
            kv = [_mm_tn(ks[i], vn[hd[i]]) for i in IS]
            for i in IS:
                h = hd[i]
                sp_ref[ci, cs[i], :] = s[h]
                o_ref[rs[i], cs[i]] = ws[h][CHUNK:] + pv[h]
                vn_ref[rs[i], cs[i]] = vn[h].astype(bf16)
            s = [jnp.exp(gl[i]) * s[hd[i]] + kv[hd[i]] for i in IS]
        for h in range(HEADS):
            s_scr[h] = s[h]
        for i in HS:
            u_ref[rs[i], cs[i]] = u[i].astype(bf16)
            w_ref[rs[i], cs[i]] = w[i].astype(bf16)
            t_ref[i // HEADS, hd[i]] = t[i].astype(bf16)

    row = lambda c: (c, 0)
    act, act16 = jax.ShapeDtypeStruct((L, W), f32), jax.ShapeDtypeStruct((L, W), bf16)
    return _pcall(
        body, name="gdn_fwd", grid=(nc // cps,),
        in_specs=[pl.BlockSpec((rows_per_step, 3 * W), row), pl.BlockSpec((rows_per_step, LANE), row),
                  pl.BlockSpec((cps, HEADS, CHUNK), lambda c: (c, 0, 0))],
        out_specs=[pl.BlockSpec((rows_per_step, W), row)] * 4 + [
            pl.BlockSpec((cps, HEADS, CHUNK, CHUNK), lambda c: (c, 0, 0, 0)),
            pl.BlockSpec((cps, W, HEAD_DIM), lambda c: (c, 0, 0))],
        out_shape=[act, act16, act16, act16, jax.ShapeDtypeStruct((nc, HEADS, CHUNK, CHUNK), bf16),
                   jax.ShapeDtypeStruct((nc, W, HEAD_DIM), f32)],
        scratch_shapes=[pltpu.VMEM((HEADS, HEAD_DIM, HEAD_DIM), f32)],
        compiler_params=_cparams("arbitrary"),
    )(qkv, sc, gr)


def _gdn_gate(o, proj, gnw):
    L = o.shape[0]

    def body(o_ref, z_ref, w_ref, m_ref):
        for ls in HALVES:
            ov, z = o_ref[:, ls], z_ref[:, ls]
            rms = lax.rsqrt(jnp.mean(ov * ov, axis=-1, keepdims=True) + EPS)
            m_ref[:, ls] = (((ov * rms) * w_ref[...]) * (z * _sigmoid(z))).astype(bf16)

    return _pcall(
        body, name="gdn_gate", grid=(GDN_WIDTH // ELT_W,),
        in_specs=[pl.BlockSpec((L, ELT_W), lambda j: (0, j)), pl.BlockSpec((L, ELT_W), lambda j: (0, OFF_ZG // ELT_W + j)),
                  pl.BlockSpec((1, LANE), lambda j: (0, 0))],
        out_specs=pl.BlockSpec((L, ELT_W), lambda j: (0, j)),
        out_shape=jax.ShapeDtypeStruct((L, GDN_WIDTH + CONV_WIDTH), bf16),
        compiler_params=_cparams("parallel"),
    )(o, proj, gnw)


def _conv3(u, cw_ref, ls):
    return cw_ref[2:3, ls] * u + cw_ref[1:2, ls] * _shift_down(u, 1) + cw_ref[0:1, ls] * _shift_down(u, 2)


def _conv_specs(L):
    return [pl.BlockSpec((L, CONV_BLOCK), lambda j: (0, OFF_CONV // CONV_BLOCK + j)),
            pl.BlockSpec((3, ELT_W), lambda j: (0, j)), pl.BlockSpec((1, ELT_W), lambda j: (0, j))]


def _conv_parts(ls):
    return [slice(g * ELT_W + ls.start, g * ELT_W + ls.stop) for g in range(4)]


def _conv_fwd(proj, cw, cb, mix):
    L = proj.shape[0]

    def body(p_ref, cw_ref, cb_ref, mix_in, m_ref):
        for ls in HALVES:
            sb, sc_, sh, sz = _conv_parts(ls)
            z = p_ref[:, sz]
            cv = _conv3(p_ref[:, sc_] * p_ref[:, sh], cw_ref, ls) + cb_ref[:, ls]
            m_ref[:, ls] = ((p_ref[:, sb] * cv) * (z * _sigmoid(z))).astype(bf16)

    return _pcall(
        body, name="conv_fwd", grid=(CONV_WIDTH // ELT_W,),
        in_specs=_conv_specs(L) + [ANY], out_specs=pl.BlockSpec((L, ELT_W), lambda j: (0, GDN_WIDTH // ELT_W + j)),
        out_shape=jax.ShapeDtypeStruct(mix.shape, mix.dtype), input_output_aliases={3: 0},
        compiler_params=_cparams("parallel"),
    )(proj, cw, cb, mix)


def _out_proj_loss(x, mix, wo, fw, tgt):
    L = x.shape[0]
    tm = min(512, L)
    MW = GDN_WIDTH + CONV_WIDTH

    def body(x_ref, m_ref, wo_ref, fw_ref, t_ref, dy_ref, dyb_ref, dm_ref, gfw_ref, loss_ref):
        @pl.when(pl.program_id(0) == 0)
        def _():
            gfw_ref[...] = jnp.zeros_like(gfw_ref)
            loss_ref[...] = jnp.zeros_like(loss_ref)
        y = x_ref[...] + jnp.dot(m_ref[...], wo_ref[...], preferred_element_type=f32)
        r = lax.rsqrt(jnp.mean(y * y, axis=-1, keepdims=True) + EPS)
        yh = y * r
        fwv = fw_ref[...]
        diff = yh * fwv - t_ref[...]
        loss_ref[...] += jnp.sum(jnp.sum(diff * diff, axis=-1, keepdims=True), axis=0, keepdims=True) * (0.5 / D_MODEL)
        dout = diff * (1.0 / D_MODEL)
        gfw_ref[...] += jnp.sum(dout * yh, axis=0, keepdims=True)
        dyh = dout * fwv
        dy = r * (dyh - yh * jnp.mean(dyh * yh, axis=-1, keepdims=True))
        dy_ref[...] = dy
        dyb = dy.astype(bf16)
        dyb_ref[...] = dyb
        dm_ref[...] = lax.dot_general(dyb, wo_ref[...], (((1,), (1,)), ((), ())), preferred_element_type=f32)

    row = lambda i: (i, 0)
    fix = lambda i: (0, 0)
    act = jax.ShapeDtypeStruct((L, D_MODEL), f32)
    return _pcall(
        body, name="out_proj_loss", grid=(L // tm,),
        in_specs=[pl.BlockSpec((tm, D_MODEL), row), pl.BlockSpec((tm, MW), row), pl.BlockSpec((MW, D_MODEL), fix),
                  pl.BlockSpec((1, D_MODEL), fix), pl.BlockSpec((tm, D_MODEL), row)],
        out_specs=[pl.BlockSpec((tm, D_MODEL), row), pl.BlockSpec((tm, D_MODEL), row), pl.BlockSpec((tm, MW), row),
                   pl.BlockSpec((1, D_MODEL), fix), pl.BlockSpec((1, LANE), fix)],
        out_shape=[act, jax.ShapeDtypeStruct((L, D_MODEL), bf16), jax.ShapeDtypeStruct((L, MW), f32),
                   jax.ShapeDtypeStruct((1, D_MODEL), f32), jax.ShapeDtypeStruct((1, LANE), f32)],
        compiler_params=_cparams("arbitrary"),
    )(x, mix, wo, fw, tgt)


def _tn_matmul(a, b, name):
    L, M = a.shape
    N = b.shape[1]
    tm = 512 if M % 512 == 0 else (768 if M % 768 == 0 else M)

    def body(a_ref, b_ref, o_ref):
        o_ref[...] = lax.dot_general(a_ref[...], b_ref[...], (((0,), (0,)), ((), ())),
                                     preferred_element_type=f32).astype(o_ref.dtype)

    return _pcall(
        body, name=name, grid=(M // tm,),
        in_specs=[pl.BlockSpec((L, tm), lambda i: (0, i)), pl.BlockSpec((L, N), lambda i: (0, 0))],
        out_specs=pl.BlockSpec((tm, N), lambda i: (i, 0)),
        out_shape=jax.ShapeDtypeStruct((M, N), bf16),
        compiler_params=_cparams("parallel"),
    )(a, b)


def _gdn_gate_bwd(o, proj, gnw, dmix_a):
    L = o.shape[0]

    def body(o_ref, z_ref, w_ref, dm_ref, do_ref, dz_ref, gw_ref):
        @pl.when(pl.program_id(0) == 0)
        def _():
            gw_ref[...] = jnp.zeros_like(gw_ref)
        wv = w_ref[...]
        for ls in HALVES:
            ov, z, dm = o_ref[:, ls], z_ref[:, ls], dm_ref[:, ls]
            rms = lax.rsqrt(jnp.mean(ov * ov, axis=-1, keepdims=True) + EPS)
            xh = ov * rms
            sg = _sigmoid(z)
            d_on = dm * (z * sg)
            dz_ref[:, ls] = (dm * (xh * wv) * (sg * (1.0 + z * (1.0 - sg)))).astype(bf16)
            gw_ref[...] += jnp.sum(d_on * xh, axis=0, keepdims=True)
            dxh = d_on * wv
            do_ref[:, ls] = (rms * (dxh - xh * jnp.mean(dxh * xh, axis=-1, keepdims=True))).astype(bf16)

    wide = pl.BlockSpec((L, ELT_W), lambda j: (0, j))
    return _pcall(
        body, name="gdn_gate_bwd", grid=(GDN_WIDTH // ELT_W,),
        in_specs=[wide, pl.BlockSpec((L, ELT_W), lambda j: (0, OFF_ZG // ELT_W + j)),
                  pl.BlockSpec((1, LANE), lambda j: (0, 0)), wide],
        out_specs=[wide, pl.BlockSpec((L, ELT_W), lambda j: (0, OFF_ZG // ELT_W + j)),
                   pl.BlockSpec((1, LANE), lambda j: (0, 0))],
        out_shape=[jax.ShapeDtypeStruct((L, GDN_WIDTH), bf16), jax.ShapeDtypeStruct((L, PROJ_PAD), bf16),
                   jax.ShapeDtypeStruct((1, LANE), f32)],
        compiler_params=_cparams("arbitrary"),
    )(o, proj, gnw, dmix_a)


def _conv_bwd(proj, cw, cb, dmix_b, dproj):
    L = proj.shape[0]

    def body(p_ref, cw_ref, cb_ref, dm_ref, dproj_in, dp_ref, gcw_ref, gcb_ref):
        for ls in HALVES:
            sb, sc_, sh, sz_ = _conv_parts(ls)
            bv, cv_, hv, z, dm = p_ref[:, sb], p_ref[:, sc_], p_ref[:, sh], p_ref[:, sz_], dm_ref[:, ls]
            u = cv_ * hv
            cv = _conv3(u, cw_ref, ls) + cb_ref[:, ls]
            sg = _sigmoid(z)
            sz = z * sg
            dp_ref[:, sb] = (dm * cv * sz).astype(bf16)
            dp_ref[:, sz_] = (dm * (bv * cv) * (sg * (1.0 + z * (1.0 - sg)))).astype(bf16)
            dcv = dm * bv * sz
            gcb_ref[:, ls] = jnp.sum(dcv, axis=0, keepdims=True)
            dcv1, dcv2 = _shift_up(dcv, 1), _shift_up(dcv, 2)
            gcw_ref[2:3, ls] = jnp.sum(dcv * u, axis=0, keepdims=True)
            gcw_ref[1:2, ls] = jnp.sum(dcv1 * u, axis=0, keepdims=True)
            gcw_ref[0:1, ls] = jnp.sum(dcv2 * u, axis=0, keepdims=True)
            du = cw_ref[2:3, ls] * dcv + cw_ref[1:2, ls] * dcv1 + cw_ref[0:1, ls] * dcv2
            dp_ref[:, sc_] = (du * hv).astype(bf16)
            dp_ref[:, sh] = (du * cv_).astype(bf16)

    return _pcall(
        body, name="conv_bwd", grid=(CONV_WIDTH // ELT_W,),
        in_specs=_conv_specs(L) + [pl.BlockSpec((L, ELT_W), lambda j: (0, GDN_WIDTH // ELT_W + j)), ANY],
        out_specs=[pl.BlockSpec((L, CONV_BLOCK), lambda j: (0, OFF_CONV // CONV_BLOCK + j)),
                   pl.BlockSpec((3, ELT_W), lambda j: (0, j)), pl.BlockSpec((1, ELT_W), lambda j: (0, j))],
        out_shape=[jax.ShapeDtypeStruct(dproj.shape, dproj.dtype), jax.ShapeDtypeStruct((3, CONV_WIDTH), f32),
                   jax.ShapeDtypeStruct((1, CONV_WIDTH), f32)],
        input_output_aliases={4: 0},
        compiler_params=_cparams("parallel"),
    )(proj, cw, cb, dmix_b, dproj)


def _gdn_bwd(qkv, sc, gr, u_all, w_all, vn_all, t_all, sp_all, do_all):
    L = qkv.shape[0]
    nc = L // CHUNK
    W = GDN_WIDTH
    cps = GDN_CPS_BWD if nc % GDN_CPS_BWD == 0 else 1
    rows_per_step = cps * CHUNK
    nsteps = nc // cps

    def body(qkv_ref, sc_ref, gr_ref, u_ref, w_ref, vn_ref, t_ref, sp_ref, do_ref, dqkv_ref, dsc_ref, dgr_ref, ds_scr):
        @pl.when(pl.program_id(0) == 0)
        def _():
            ds_scr[...] = jnp.zeros_like(ds_scr)
        HS = range(cps * HEADS)
        hd = [i % HEADS for i in HS]
        rs = [slice((i // HEADS) * CHUNK, (i // HEADS + 1) * CHUNK) for i in HS]
        cs = [slice(hd[i] * HEAD_DIM, (hd[i] + 1) * HEAD_DIM) for i in HS]
        q = [qkv_ref[rs[i], hd[i] * HEAD_DIM:(hd[i] + 1) * HEAD_DIM] for i in HS]
        k = [qkv_ref[rs[i], W + hd[i] * HEAD_DIM:W + (hd[i] + 1) * HEAD_DIM] for i in HS]
        v = [qkv_ref[rs[i], 2 * W + hd[i] * HEAD_DIM:2 * W + (hd[i] + 1) * HEAD_DIM] for i in HS]
        hsc = [_head_scalars(sc_ref[rs[i], :], gr_ref, hd[i], i // HEADS) for i in HS]
        beta, gcc, gl, dmat, dmat_t = ([x[i] for x in hsc] for i in range(5))
        ii, jj = hsc[0][5], hsc[0][6]
        eg = [jnp.exp(gcc[h]) for h in HS]
        ekl = [jnp.exp(gl[h] - gcc[h]) for h in HS]
        egl = [jnp.exp(gl[h]) for h in HS]
        kb = [k[h] * beta[h] for h in HS]
        ks = [k[h] * ekl[h] for h in HS]
        do = [do_ref[rs[h], cs[h]] for h in HS]
        vn = [vn_ref[rs[h], cs[h]] for h in HS]
        s = [sp_ref[h // HEADS, cs[h], :] for h in HS]
        w = [w_ref[rs[h], cs[h]] for h in HS]
        qd = [q[h] * eg[h] for h in HS]

        kq = [_mm_nt(k[h], q[h]) for h in HS]
        p_t = [jnp.where(jj >= ii, kq[h] * dmat_t[h], 0.0) for h in HS]
        ptd = [_mm(p_t[h], do[h]) for h in HS]
        qw =[jnp.concatenate([qd[h], -w[h]], axis=0) for h in HS]
        dsn, dvn, dodv = [None] * len(HS), [None] * len(HS), [None] * len(HS)
        ds_cur = [ds_scr[h] for h in range(HEADS)]
        for ci in reversed(range(cps)):
            IS = range(ci * HEADS, (ci + 1) * HEADS)
            ksd = [_mm(ks[i], ds_cur[hd[i]]) for i in IS]
            for i in IS:
                dsn[i] = ds_cur[hd[i]]
                dvn[i] = ptd[i] + ksd[hd[i]]
                dodv[i] = jnp.concatenate([do[i], dvn[i]], axis=0)
            dsq = [_mm_tn(qw[i], dodv[i]) for i in IS]
            ds_cur = [egl[i] * ds_cur[hd[i]] + dsq[hd[i]] for i in IS]
        for h in range(HEADS):
            ds_scr[h] = ds_cur[h]
        x1 = [_mm_nt(dodv[h], s[h]) for h in HS]
        dks = [_mm_nt(vn[h], dsn[h]) for h in HS]
        dov = [_mm_nt(do[h], vn[h]) for h in HS]
        vdo = [_mm_nt(vn[h], do[h]) for h in HS]
        kk = [_mm_nt(kb[h], k[h]) for h in HS]
        qk = [_mm_nt(q[h], k[h]) for h in HS]
        dgl = [egl[h] * jnp.sum(jnp.sum(s[h] * dsn[h], axis=1, keepdims=True), axis=0, keepdims=True) for h in HS]
        dqd = [x1[h][:CHUNK] for h in HS]
        duw = [jnp.concatenate([dvn[h], -x1[h][CHUNK:]], axis=1) for h in HS]
        tdu = [_mm_tn(t_ref[h // HEADS, hd[h]], duw[h]) for h in HS]
        dvk = [duw[h] + tdu[h] for h in HS]
        uw = [jnp.concatenate([u_ref[rs[h], cs[h]], w[h]], axis=1) for h in HS]
        da = [-jnp.where(ii > jj, _mm_nt(dvk[h], uw[h]), 0.0) for h in HS]
        da_t = [-jnp.where(jj > ii, _mm_nt(uw[h], dvk[h]), 0.0) for h in HS]
        dp = [jnp.where(ii >= jj, dov[h], 0.0) for h in HS]
        dp_t = [jnp.where(jj >= ii, vdo[h], 0.0) for h in HS]
        r1 = [_mm(jnp.concatenate([da[h] * dmat[h], dp[h] * dmat[h]], axis=0), k[h]) for h in HS]
        dk1 = [_mm(jnp.concatenate([da_t[h] * dmat_t[h], dp_t[h] * dmat_t[h]], axis=1),
                   jnp.concatenate([kb[h], q[h]], axis=0)) for h in HS]
        lane = _lanes((CHUNK, LANE))
        for ci in range(cps):
            dsc = jnp.zeros((CHUNK, LANE), f32)
            for i in range(ci * HEADS, (ci + 1) * HEADS):
                h = hd[i]
                a = jnp.where(ii > jj, kk[i] * dmat[i], 0.0)
                p = jnp.where(ii >= jj, qk[i] * dmat[i], 0.0)
                gmat = da[i] * a + dp[i] * p
                dvb, dkbg = dvk[i][:, :HEAD_DIM], dvk[i][:, HEAD_DIM:]
                kbg = kb[i] * eg[i]
                dkb = r1[i][:CHUNK] + dkbg * eg[i]
                dq = r1[i][CHUNK:] + dqd[i] * eg[i]
                dk = dk1[i] + dks[i] * ekl[i] + dkb * beta[i]
                dbeta = jnp.sum(dkb * k[i] + dvb * v[i], axis=1, keepdims=True)
                ksum = jnp.sum(dks[i] * ks[i], axis=1, keepdims=True)
                dgl_tot = dgl[i] + jnp.sum(ksum, axis=0, keepdims=True)
                dgc = (jnp.sum(gmat, axis=1, keepdims=True) + jnp.sum(dqd[i] * qd[i] + dkbg * kbg, axis=1, keepdims=True)
                       - ksum)
                dgc = dgc + jnp.where(_rows(dgc.shape) == CHUNK - 1, dgl_tot, 0.0)
                dqkv_ref[rs[i], h * HEAD_DIM:(h + 1) * HEAD_DIM] = dq
                dqkv_ref[rs[i], W + h * HEAD_DIM:W + (h + 1) * HEAD_DIM] = dk
                dqkv_ref[rs[i], 2 * W + h * HEAD_DIM:2 * W + (h + 1) * HEAD_DIM] = dvb * beta[i]
                dsc = jnp.where(lane == h, dbeta, jnp.where(lane == HEADS + h, dgc, dsc))
                dgr_ref[ci, h:h + 1, :] = jnp.sum(gmat, axis=0, keepdims=True)
            dsc_ref[ci * CHUNK:(ci + 1) * CHUNK, :] = dsc

    row = lambda c: (nsteps - 1 - c, 0)
    lead3 = lambda c: (nsteps - 1 - c, 0, 0)
    return _pcall(
        body, name="gdn_bwd", grid=(nsteps,),
        in_specs=[pl.BlockSpec((rows_per_step, 3 * W), row), pl.BlockSpec((rows_per_step, LANE), row),
                  pl.BlockSpec((cps, HEADS, CHUNK), lead3),
                  pl.BlockSpec((rows_per_step, W), row), pl.BlockSpec((rows_per_step, W), row),
                  pl.BlockSpec((rows_per_step, W), row),
                  pl.BlockSpec((cps, HEADS, CHUNK, CHUNK), lambda c: (nsteps - 1 - c, 0, 0, 0)),
                  pl.BlockSpec((cps, W, HEAD_DIM), lead3), pl.BlockSpec((rows_per_step, W), row)],
        out_specs=[pl.BlockSpec((rows_per_step, 3 * W), row), pl.BlockSpec((rows_per_step, LANE), row),
                   pl.BlockSpec((cps, HEADS, CHUNK), lead3)],
        out_shape=[jax.ShapeDtypeStruct((L, 3 * W), f32), jax.ShapeDtypeStruct((L, LANE), f32),
                   jax.ShapeDtypeStruct((nc, HEADS, CHUNK), f32)],
        scratch_shapes=[pltpu.VMEM((HEADS, HEAD_DIM, HEAD_DIM), f32)],
        compiler_params=_cparams("arbitrary"),
    )(qkv, sc, gr, u_all, w_all, vn_all, t_all, sp_all, do_all)


def _qkv_bwd(proj, cw, dn, dproj):
    L = proj.shape[0]

    def body(x_ref, cw_ref, dn_ref, dproj_in, dx_ref, gcw_ref):
        j = pl.program_id(0)
        steps = GDN_WIDTH // ELT_W
        scale = jnp.where(j < steps, HEAD_DIM ** -0.5, 1.0).astype(f32)
        for ls in HALVES:
            x, dn_v = x_ref[:, ls], dn_ref[:, ls]
            c = _conv4(x, cw_ref, ls)
            sg = _sigmoid(c)
            a = c * sg
            rn = lax.rsqrt(jnp.sum(a * a, axis=1, keepdims=True) + EPS)
            da_n = (scale * rn) * (dn_v - a * ((rn * rn) * jnp.sum(dn_v * a, axis=1, keepdims=True)))
            da = jnp.where(j < 2 * steps, da_n, dn_v)
            dc = da * (sg * (1.0 + c * (1.0 - sg)))
            dc1, dc2, dc3 = _shift_up(dc, 1), _shift_up(dc, 2), _shift_up(dc, 3)
            gcw_ref[3:4, ls] = jnp.sum(dc * x, axis=0, keepdims=True)
            gcw_ref[2:3, ls] = jnp.sum(dc1 * x, axis=0, keepdims=True)
            gcw_ref[1:2, ls] = jnp.sum(dc2 * x, axis=0, keepdims=True)
            gcw_ref[0:1, ls] = jnp.sum(dc3 * x, axis=0, keepdims=True)
            dx = cw_ref[3:4, ls] * dc + cw_ref[2:3, ls] * dc1 + cw_ref[1:2, ls] * dc2 + cw_ref[0:1, ls] * dc3
            dx_ref[:, ls] = dx.astype(bf16)

    col = pl.BlockSpec((L, ELT_W), lambda j: (0, j))
    wspec = pl.BlockSpec((4, ELT_W), lambda j: (0, j))
    return _pcall(
        body, name="qkv_bwd", grid=(3 * GDN_WIDTH // ELT_W,),
        in_specs=[col, wspec, col, ANY], out_specs=[col, wspec],
        out_shape=[jax.ShapeDtypeStruct(dproj.shape, dproj.dtype), jax.ShapeDtypeStruct((4, 3 * GDN_WIDTH), f32)],
        input_output_aliases={3: 0},
        compiler_params=_cparams("parallel"),
    )(proj, cw, dn, dproj)


def _scalars_bwd(proj, alog_p, dtb_p, dsc, dgr_col, dproj):
    L = proj.shape[0]

    def body(x_ref, al_ref, dt_ref, dsc_ref, dgr_ref, dproj_in, dba_ref, gs_ref):
        x, dsc_v = x_ref[...], dsc_ref[...]
        lane = _lanes(x.shape)
        dec = (lane >= HEADS) & (lane < 2 * HEADS)
        dg = jnp.where(dec, dsc_v - dgr_ref[...], 0.0)
        rc = _rows(x.shape) & (CHUNK - 1)
        for s in (1, 2, 4, 8, 16, 32):
            dg = dg + jnp.where(rc + s < CHUNK, pltpu.roll(dg, L - s, 0), 0.0)
        xa = x + dt_ref[...]
        ea = jnp.exp(al_ref[...])
        g = -ea * _softplus(xa)
        da = dg * (-ea) * _sigmoid(xa)
        beta = _sigmoid(x)
        db = dsc_v * beta * (1.0 - beta)
        dba_ref[:, :LANE] = jnp.where(lane < HEADS, db, jnp.where(dec, da, 0.0)).astype(bf16)
        dba_ref[:, LANE:] = jnp.zeros((L, ELT_W - LANE), bf16)
        g_al = jnp.sum(jnp.where(dec, dg * g, 0.0), axis=0, keepdims=True)
        g_dt = jnp.sum(jnp.where(dec, da, 0.0), axis=0, keepdims=True)
        row8 = _rows(gs_ref.shape)
        gs = jnp.where(row8 == 0, g_al, jnp.where(row8 == 1, g_dt, 0.0))
        gs_ref[...] = pltpu.roll(gs, LANE - HEADS, 1)

    full = pl.BlockSpec((L, LANE), lambda i: (0, 0))
    vec = pl.BlockSpec((1, LANE), lambda i: (0, 0))
    return _pcall(
        body, name="scalars_bwd", grid=(1,),
        in_specs=[pl.BlockSpec((L, LANE), lambda i: (0, OFF_BA // LANE)), vec, vec, full, full, ANY],
        out_specs=[pl.BlockSpec((L, ELT_W), lambda i: (0, OFF_BA // ELT_W)), pl.BlockSpec((8, LANE), lambda i: (0, 0))],
        out_shape=[jax.ShapeDtypeStruct(dproj.shape, dproj.dtype), jax.ShapeDtypeStruct((8, LANE), f32)],
        input_output_aliases={5: 0},
        compiler_params=_cparams("arbitrary"),
    )(proj, alog_p, dtb_p, dsc, dgr_col, dproj)


def _input_grad(dproj, wpad, x, nw, dy):
    L = x.shape[0]
    tm = min(512, L)
    cuts = (0, 3072, 5120, 7168, PROJ_PAD)
    nk = len(cuts) - 1

    def body(dp_ref, w_hbm, x_ref, nw_ref, dy_ref, gx_ref, gnw_ref, w_vmem, sems):
        first = pl.program_id(0) == 0
        loads = [pltpu.make_async_copy(w_hbm.at[cuts[k]:cuts[k + 1], :], w_vmem.at[cuts[k]:cuts[k + 1], :], sems.at[k])
                 for k in range(nk)]

        @pl.when(first)
        def _():
            for cp in loads:
                cp.start()
            gnw_ref[...] = jnp.zeros_like(gnw_ref)
        dh = None
        for k in range(nk):
            pl.when(first)(loads[k].wait)
            part = jnp.dot(dp_ref[:, cuts[k]:cuts[k + 1]], w_vmem[cuts[k]:cuts[k + 1], :], preferred_element_type=f32)
            dh = part if dh is None else dh + part
        xv, nwv = x_ref[...], nw_ref[...]
        r = lax.rsqrt(jnp.mean(xv * xv, axis=-1, keepdims=True) + EPS)
        xh = xv * r
        gnw_ref[...] += jnp.sum(dh * xh, axis=0, keepdims=True)
        dxh = dh * nwv
        gx_ref[...] = dy_ref[...] + r * (dxh - xh * jnp.mean(dxh * xh, axis=-1, keepdims=True))

    row = lambda i: (i, 0)
    fix = lambda i: (0, 0)
    return _pcall(
        body, name="input_grad", grid=(L // tm,),
        in_specs=[pl.BlockSpec((tm, PROJ_PAD), row), ANY, pl.BlockSpec((tm, D_MODEL), row),
                  pl.BlockSpec((1, D_MODEL), fix), pl.BlockSpec((tm, D_MODEL), row)],
        out_specs=[pl.BlockSpec((tm, D_MODEL), row), pl.BlockSpec((1, D_MODEL), fix)],
        out_shape=[jax.ShapeDtypeStruct((L, D_MODEL), f32), jax.ShapeDtypeStruct((1, D_MODEL), f32)],
        scratch_shapes=[pltpu.VMEM(wpad.shape, bf16), pltpu.SemaphoreType.DMA((nk,))],
        compiler_params=_cparams("arbitrary"),
    )(dproj, wpad, x, nw, dy)


def _adamw_reduce(parts, w, m, v, name):
    R, C = w.shape
    n_parts = parts.shape[0]
    tr = 128 if R % 128 == 0 else R
    c1 = 1.0 - ADAM_B1 ** ADAM_STEP
    c2 = 1.0 - ADAM_B2 ** ADAM_STEP

    def body(p_ref, w_ref, m_ref, v_ref, g_ref, d_ref, nm_ref, nv_ref):
        g = p_ref[0].astype(f32)
        for s in range(1, n_parts):
            g = g + p_ref[s].astype(f32)
        nm = ADAM_B1 * m_ref[...] + (1.0 - ADAM_B1) * g
        nv = ADAM_B2 * v_ref[...] + (1.0 - ADAM_B2) * (g * g)
        g_ref[...] = g
        nm_ref[...] = nm
        nv_ref[...] = nv
        d_ref[...] = -ADAM_LR * ((nm / c1) / (jnp.sqrt(nv / c2) + ADAM_EPS) + ADAM_WD * w_ref[...])

    blk = pl.BlockSpec((tr, C), lambda i: (i, 0))
    out = jax.ShapeDtypeStruct((R, C), f32)
    return _pcall(
        body, name=name, grid=(R // tr,),
        in_specs=[pl.BlockSpec((n_parts, tr, C), lambda i: (0, i, 0)), blk, blk, blk],
        out_specs=[blk] * 4, out_shape=[out] * 4,
        compiler_params=_cparams("parallel"),
    )(parts, w, m, v)


SMALL_SLOTS = ((0, D_MODEL), (D_MODEL, D_MODEL), (2 * D_MODEL, D_MODEL), (3 * D_MODEL, LANE),
               (3 * D_MODEL + LANE, HEADS), (3 * D_MODEL + 2 * LANE, HEADS))
SMALL_LOSS = 3 * D_MODEL + 3 * LANE
SMALL_W = SMALL_LOSS + LANE


def _pack_small(gs, after):
    def body(nw_ref, cb_ref, fw_ref, gn_ref, sc_ref, ls_ref, after_ref, o_ref):
        for ref, (start, width) in zip((nw_ref, cb_ref, fw_ref, gn_ref), SMALL_SLOTS[:4]):
            o_ref[:, start:start + width] = ref[...]
        o_ref[:, SMALL_SLOTS[4][0]:SMALL_SLOTS[4][0] + LANE] = sc_ref[0:1, :]
        o_ref[:, SMALL_SLOTS[5][0]:SMALL_SLOTS[5][0] + LANE] = sc_ref[1:2, :]
        o_ref[:, SMALL_LOSS:SMALL_W] = ls_ref[...]

    vm = pl.BlockSpec(memory_space=pltpu.VMEM)
    return _pcall(body, name="pack_small_grads", out_shape=jax.ShapeDtypeStruct((1, SMALL_W), f32),
                  in_specs=[vm] * 6 + [ANY], out_specs=vm)(*gs, after)


def _adamw_small(parts, ws, ms, vs):
    c1 = 1.0 - ADAM_B1 ** ADAM_STEP
    c2 = 1.0 - ADAM_B2 ** ADAM_STEP
    np_ = len(ws)

    def body(*refs):
        p_ref = refs[0]
        w_refs, m_refs, v_refs = refs[1:1 + np_], refs[1 + np_:1 + 2 * np_], refs[1 + 2 * np_:1 + 3 * np_]
        outs = refs[1 + 3 * np_:]
        g_refs, d_refs, nm_refs, nv_refs = (outs[i * np_:(i + 1) * np_] for i in range(4))
        loss_ref = outs[4 * np_]

        def total(start, width):
            t = p_ref[0, :, start:start + width]
            for s in range(1, N_DEV):
                t = t + p_ref[s, :, start:start + width]
            return t

        for i, (start, width) in enumerate(SMALL_SLOTS):
            g = total(start, width)
            nm = ADAM_B1 * m_refs[i][...] + (1.0 - ADAM_B1) * g
            nv = ADAM_B2 * v_refs[i][...] + (1.0 - ADAM_B2) * (g * g)
            g_refs[i][...] = g
            nm_refs[i][...] = nm
            nv_refs[i][...] = nv
            d_refs[i][...] = -ADAM_LR * ((nm / c1) / (jnp.sqrt(nv / c2) + ADAM_EPS) + ADAM_WD * w_refs[i][...])
        loss_ref[...] = total(SMALL_LOSS, LANE)

    vm = pl.BlockSpec(memory_space=pltpu.VMEM)
    shapes = [jax.ShapeDtypeStruct(w.shape, f32) for w in ws]
    res = _pcall(body, name="adamw_small", out_shape=shapes * 4 + [jax.ShapeDtypeStruct((1, LANE), f32)],
                 in_specs=[vm] * (1 + 3 * np_), out_specs=[vm] * (4 * np_ + 1))(parts, *ws, *ms, *vs)
    return [res[i * np_:(i + 1) * np_] for i in range(4)], res[4 * np_]


def _adamw_w_in(parts, w3, m3, v3):
    n_parts, n, _ = parts.shape
    c1 = 1.0 - ADAM_B1 ** ADAM_STEP
    c2 = 1.0 - ADAM_B2 ** ADAM_STEP

    def body(p_ref, w_ref, m_ref, v_ref, g_ref, d_ref, nm_ref, nv_ref):
        g = p_ref[0].astype(f32)
        for s in range(1, n_parts):
            g = g + p_ref[s].astype(f32)
        nm = ADAM_B1 * m_ref[:, 0, :] + (1.0 - ADAM_B1) * g
        nv = ADAM_B2 * v_ref[:, 0, :] + (1.0 - ADAM_B2) * (g * g)
        g_ref[:, 0, :] = g
        nm_ref[:, 0, :] = nm
        nv_ref[:, 0, :] = nv
        d_ref[:, 0, :] = -ADAM_LR * ((nm / c1) / (jnp.sqrt(nv / c2) + ADAM_EPS) + ADAM_WD * w_ref[:, 0, :])

    tile = 2 * COL_TILE
    blk = pl.BlockSpec((n, 1, tile), lambda j: (0, 0, j))
    out = jax.ShapeDtypeStruct((n, 1, D_MODEL), f32)
    return _pcall(
        body, name="adamw_w_in", grid=(D_MODEL // tile,),
        in_specs=[pl.BlockSpec((n_parts, n, tile), lambda j: (0, 0, j)), blk, blk, blk],
        out_specs=[blk] * 4, out_shape=[out] * 4,
        compiler_params=_cparams("parallel"),
    )(parts, w3, m3, v3)


def _pad_lanes(vec8, start):
    return jnp.pad(vec8.reshape(1, -1), ((0, 0), (start, LANE - start - vec8.size)))


def kernel(x, norm_in_w, w_in, conv_qkv_w, A_log, dt_bias, gdn_norm_w, conv_w, conv_b, w_out, final_norm_w, loss_target, m_norm_in_w, m_w_in, m_conv_qkv_w, m_A_log, m_dt_bias, m_gdn_norm_w, m_conv_w, m_conv_b, m_w_out, m_final_norm_w, v_norm_in_w, v_w_in, v_conv_qkv_w, v_A_log, v_dt_bias, v_gdn_norm_w, v_conv_w, v_conv_b, v_w_out, v_final_norm_w):
    L = x.shape[1]
    nc = L // CHUNK
    xs = x[0]
    tgt = loss_target[0]
    fnw = final_norm_w.reshape(1, D_MODEL)

    as_rows = lambda a: jnp.transpose(a, (2, 0, 1))
    win_g, cqkv_g, cw_g = _all_gather([_cast_w_in(as_rows(w_in)), conv_qkv_w[0], conv_w[0]], "gather_weights",
                                      pieces=[4, 1, 1])
    wpad = _relayout_w_in(win_g)
    cqkv = jnp.concatenate([cqkv_g[d] for d in range(N_DEV)], axis=1)
    cw = jnp.concatenate([cw_g[d] for d in range(N_DEV)], axis=1)
    alog_p = _pad_lanes(A_log, HEADS)
    dtb_p = _pad_lanes(dt_bias, HEADS)
    me_flat, me_chip = _flat(*_mesh_pos()), 2 * lax.axis_index("x") + lax.axis_index("y")
    tok = lambda started: started[4][0:1, 0:1]
    wo_started = _spread_start(w_out[0].astype(bf16), wpad, "gather", "gather_w_out_start")

    proj, h = _in_proj(xs, norm_in_w + tok(wo_started), wpad)
    qkv = _qkv_act(proj, cqkv)
    sc, gr = _scalars(proj, alog_p, dtb_p)
    o, u_all, w_all, vn_all, t_all, sp_all = _gdn_fwd(qkv, sc, gr)
    mix = _conv_fwd(proj, cw, conv_b, _gdn_gate(o, proj, gdn_norm_w))
    wo = _spread_finish(wo_started, mix, "gather", "gather_w_out_wait", me_flat).reshape(-1, D_MODEL)
    dy, dyb, dmix, g_fnw, loss_v = _out_proj_loss(xs, mix, wo, fnw, tgt)

    g_wout = _tn_matmul(mix, dyb, "grad_w_out")
    gwo_started = _spread_start(g_wout.reshape(N_DEV, -1, D_MODEL), dyb, "scatter", "exchange_grad_w_out_start")
    do, dproj, g_gnw = _gdn_gate_bwd(o, proj, gdn_norm_w + tok(gwo_started), dmix)
    dproj, g_cw, g_cb = _conv_bwd(proj, cw, conv_b, dmix, dproj)
    dqkv_n, dsc, dgr = _gdn_bwd(qkv, sc, gr, u_all, w_all, vn_all, t_all, sp_all, do)
    dproj, g_cqkv = _qkv_bwd(proj, cqkv, dqkv_n, dproj)
    dgr_col = jnp.pad(dgr.transpose(0, 2, 1).reshape(L, HEADS), ((0, 0), (HEADS, LANE - 2 * HEADS)))
    dproj, g_sc = _scalars_bwd(proj, alog_p, dtb_p, dsc, dgr_col, dproj)
    g_win_blk = _grad_blocks(_tn_matmul(dproj, h, "grad_w_in"))

    (p_win,) = _pair_exchange([g_win_blk], "exchange_grads_pair")
    s_win = _pair_sum(g_win_blk, p_win, "pair_sum_w_in")
    r_cqkv, r_cw = _all_to_all(
        [g_cqkv.reshape(4, N_DEV, -1).transpose(1, 0, 2), g_cw.reshape(3, N_DEV, -1).transpose(1, 0, 2)],
        "exchange_small_sharded_grads")
    gwi_started = _spread_start(s_win, r_cw, "chips", "exchange_grads_chips_start")
    grad_x, g_nw = _input_grad(dproj, wpad, xs, norm_in_w + tok(gwi_started), dy)

    r_wout = _spread_finish(gwo_started, grad_x, "scatter", "exchange_grad_w_out_wait", me_flat)
    upd_wout =_adamw_reduce(r_wout, w_out[0], m_w_out[0], v_w_out[0], "adamw_w_out")
    upd_cqkv = _adamw_reduce(r_cqkv, conv_qkv_w[0], m_conv_qkv_w[0], v_conv_qkv_w[0], "adamw_conv_qkv_w")
    upd_cw = _adamw_reduce(r_cw, conv_w[0], m_conv_w[0], v_conv_w[0], "adamw_conv_w")

    r_win = _spread_finish(gwi_started, upd_cw[0], "chips", "exchange_grads_chips_wait", me_chip)
    upd_win = [jnp.transpose(a, (1, 2, 0)) for a in _adamw_w_in(r_win, as_rows(w_in), as_rows(m_w_in), as_rows(v_w_in))]

    small_g = _pack_small([g_nw, g_cb, g_fnw, g_gnw, g_sc, loss_v], r_win)
    (small_all,) = _all_gather([small_g], "gather_small_grads")
    fvec = lambda a: a.reshape(1, D_MODEL)
    upd_small, loss_sum = _adamw_small(
        small_all,
        [norm_in_w, conv_b, fvec(final_norm_w), gdn_norm_w, A_log, dt_bias],
        [m_norm_in_w, m_conv_b, fvec(m_final_norm_w), m_gdn_norm_w, m_A_log, m_dt_bias],
        [v_norm_in_w, v_conv_b, fvec(v_final_norm_w), v_gdn_norm_w, v_A_log, v_dt_bias])

    outs = [loss_sum[0, 0], grad_x[None]]
    for k in range(4):
        nw_k, cb_k, fw_k, gn_k, al_k, dt_k = upd_small[k]
        outs += [nw_k, upd_win[k], upd_cqkv[k][None], al_k, dt_k, gn_k,
                 upd_cw[k][None], cb_k, upd_wout[k][None], fw_k.reshape(D_MODEL)]
    return tuple(outs)
```

```python
import jax
import jax.numpy as jnp
from jax import lax
from jax.experimental import pallas as pl
from jax.experimental.pallas import tpu as pltpu

f32 = jnp.float32
bf16 = jnp.bfloat16

N_DEV = 8
D_MODEL = 1024
HEADS = 8
HEAD_DIM = 128
CHUNK = 64
GDN_CPS = 4
GDN_CPS_BWD = 1
GDN_WIDTH = HEADS * HEAD_DIM
CONV_WIDTH = 1024
PROJ_WIDTH = 8208
SHARD_W = PROJ_WIDTH // N_DEV
EPS = 1e-6

LANE = 128
ELT_W = 256

OFF_QKV, OFF_ZG, OFF_CONV, OFF_BA = 0, 3072, 4096, 8192
CONV_BLOCK = 4 * ELT_W
PROJ_PAD = 8448
NAT_BA, NAT_CONV = 4096, 4112


def _padded_col(n):
    if n < NAT_BA:
        return n
    if n < NAT_CONV:
        return OFF_BA + n - NAT_BA
    g, ch = divmod(n - NAT_CONV, CONV_WIDTH)
    j, r = divmod(ch, ELT_W)
    return OFF_CONV + CONV_BLOCK * j + ELT_W * g + r


def _layout_segments(n0, n1):
    cuts = [NAT_BA, NAT_CONV] + [NAT_CONV + ELT_W * k for k in range(1, 4 * CONV_WIDTH // ELT_W)]
    pts = [n0] + [c for c in cuts if n0 < c < n1] + [n1]
    return [(lo, hi - lo, _padded_col(lo)) for lo, hi in zip(pts, pts[1:])]

ADAM_LR, ADAM_B1, ADAM_B2, ADAM_EPS, ADAM_WD, ADAM_STEP = 0.001, 0.9, 0.999, 1e-08, 0.01, 10

V7X_VMEM_BYTES = 64 * 1024 * 1024
VMEM_LIMIT = V7X_VMEM_BYTES - 8 * 1024 * 1024

MESH = pl.DeviceIdType.MESH
ANY = pl.BlockSpec(memory_space=pl.ANY)


def _pcall(body, **kw):
    return pl.pallas_call(body, **kw)


def _cparams(*sem):
    return pltpu.CompilerParams(dimension_semantics=sem if sem else None, vmem_limit_bytes=VMEM_LIMIT)


def _mm(a, b):
    return jnp.dot(a.astype(bf16), b.astype(bf16), preferred_element_type=f32)


def _mm_nt(a, b):
    return lax.dot_general(a.astype(bf16), b.astype(bf16), (((1,), (1,)), ((), ())), preferred_element_type=f32)


def _cat16(parts, axis):
    return jnp.concatenate([p.astype(bf16) for p in parts], axis=axis)


def _mm_tn(a, b):
    return lax.dot_general(a.astype(bf16), b.astype(bf16), (((0,), (0,)), ((), ())), preferred_element_type=f32)


def _rows(shape):
    return lax.broadcasted_iota(jnp.int32, shape, 0)


def _lanes(shape):
    return lax.broadcasted_iota(jnp.int32, shape, 1)


def _shift_down(x, s):
    if s == 0:
        return x
    return jnp.where(_rows(x.shape) >= s, pltpu.roll(x, s, 0), 0.0)


def _shift_up(x, s):
    if s == 0:
        return x
    n = x.shape[0]
    return jnp.where(_rows(x.shape) < n - s, pltpu.roll(x, n - s, 0), 0.0)


def _sigmoid(x):
    return jax.nn.sigmoid(x)


def _softplus(x):
    e = jnp.exp(-jnp.abs(x))
    small = e * (1.0 - e * (0.5 - e * (1.0 / 3.0)))
    return jnp.maximum(x, 0.0) + jnp.where(e < 0.01, small, jnp.log(1.0 + e))


def _mesh_pos():
    return lax.axis_index("x"), lax.axis_index("y"), lax.axis_index("c")


def _flat(px, py, pc):
    return 4 * px + 2 * py + pc


def _all_gather(xs, name, pieces=None):
    n = len(xs)
    pieces = pieces or [1] * n
    items = [(a, q) for a in range(n) for q in range(pieces[a])]
    ni = len(items)

    def view(ref, i):
        a, q = items[i]
        if pieces[a] == 1:
            return ref
        wd = xs[a].shape[-1] // pieces[a]
        return ref.at[(slice(None),) * (xs[a].ndim - 1) + (pl.ds(q * wd, wd),)]

    def body(*refs):
        x_refs, o_refs = refs[:n], refs[n:2 * n]
        send_sems, recv_sems, local_sems = refs[2 * n:]
        x, y, c = _mesh_pos()
        me, sibling = (x, y, c), (x, y, 1 - c)
        flip = lambda v, bit: v + bit - 2 * v * bit
        nbr_a = (flip(x, 1 - c), flip(y, c))
        nbr_b = (flip(x, c), flip(y, 1 - c))
        diag = (1 - x, 1 - y)

        def copy(i, k, block, to, own=False):
            a = items[i][0]
            dst = view(o_refs[a].at[_flat(*block)], i)
            return pltpu.make_async_remote_copy(
                src_ref=view(x_refs[a], i) if own else dst, dst_ref=dst,
                send_sem=send_sems.at[i, k], recv_sem=recv_sems.at[i, k], device_id=to, device_id_type=MESH)

        mine, sent = [], []

        def go(cp):
            cp.start()
            sent.append(cp)

        for a in range(n):
            cp = pltpu.make_async_copy(x_refs[a], o_refs[a].at[_flat(*me)], local_sems.at[a])
            cp.start()
            mine.append(cp)
        for a in range(ni):
            go(copy(a, 1, me, (*nbr_a, c), own=True))
            go(copy(a, 2, me, (*nbr_b, c), own=True))
            go(copy(a, 0, me, sibling, own=True))
        for a in range(ni):
            copy(a, 1, (*nbr_a, c), me).wait_recv()
            go(copy(a, 3, (*nbr_a, c), (*nbr_b, c)))
            go(copy(a, 4, (*nbr_a, c), sibling))
        for a in range(ni):
            copy(a, 2, (*nbr_b, c), me).wait_recv()
            go(copy(a, 5, (*nbr_b, c), sibling))
        for a in range(ni):
            copy(a, 3, (*diag, c), me).wait_recv()
            go(copy(a, 6, (*diag, c), sibling))
        for a in range(ni):
            copy(a, 0, sibling, me).wait_recv()
            copy(a, 4, (*nbr_b, 1 - c), me).wait_recv()
            copy(a, 5, (*nbr_a, 1 - c), me).wait_recv()
            copy(a, 6, (*diag, 1 - c), me).wait_recv()
        for cp in sent:
            cp.wait_send()
        for cp in mine:
            cp.wait()

    outs = _pcall(
        body, name=name,
        out_shape=[jax.ShapeDtypeStruct((N_DEV,) + a.shape, a.dtype) for a in xs],
        in_specs=[ANY] * n, out_specs=[ANY] * n,
        scratch_shapes=[pltpu.SemaphoreType.DMA((ni, 7)), pltpu.SemaphoreType.DMA((ni, 7)), pltpu.SemaphoreType.DMA((n,))],
    )(*xs)
    return list(outs)


def _all_to_all(gs, name):
    n = len(gs)

    def body(*refs):
        g_refs, o_refs = refs[:n], refs[n:2 * n]
        send_sems, recv_sems, local_sems = refs[2 * n:]
        x, y, c = _mesh_pos()
        me = _flat(x, y, c)
        peers = []
        for k in range(1, N_DEV):
            kx, ky, kc = (k >> 2) & 1, (k >> 1) & 1, k & 1
            px = (1 - x) if kx else x
            py = (1 - y) if ky else y
            pc = (1 - c) if kc else c
            peers.append((px, py, pc))

        def copy(a, k):
            peer = peers[k - 1]
            return pltpu.make_async_remote_copy(
                src_ref=g_refs[a].at[_flat(*peer)], dst_ref=o_refs[a].at[me],
                send_sem=send_sems.at[a, k - 1], recv_sem=recv_sems.at[a, k - 1], device_id=peer, device_id_type=MESH)

        def arrival(a, k):
            peer = peers[k - 1]
            return pltpu.make_async_remote_copy(
                src_ref=g_refs[a].at[me], dst_ref=o_refs[a].at[_flat(*peer)],
                send_sem=send_sems.at[a, k - 1], recv_sem=recv_sems.at[a, k - 1], device_id=peer, device_id_type=MESH)

        mine, sent = [], []
        for a in range(n):
            cp = pltpu.make_async_copy(g_refs[a].at[me], o_refs[a].at[me], local_sems.at[a])
            cp.start()
            mine.append(cp)
            for k in range(1, N_DEV):
                cp = copy(a, k)
                cp.start()
                sent.append(cp)
        for a in range(n):
            for k in range(1, N_DEV):
                arrival(a, k).wait_recv()
        for cp in sent:
            cp.wait_send()
        for cp in mine:
            cp.wait()

    outs = _pcall(
        body, name=name,
        out_shape=[jax.ShapeDtypeStruct(a.shape, a.dtype) for a in gs],
        in_specs=[ANY] * n, out_specs=[ANY] * n,
        scratch_shapes=[pltpu.SemaphoreType.DMA((n, 7)), pltpu.SemaphoreType.DMA((n, 7)), pltpu.SemaphoreType.DMA((n,))],
    )(*gs)
    return list(outs)


def _pair_exchange(gs, name):
    n = len(gs)
    chips = [(0, 0), (0, 1), (1, 0), (1, 1)]

    def body(*refs):
        g_refs, o_refs = refs[:n], refs[n:2 * n]
        send_sems, recv_sems = refs[2 * n:]
        x, y, c = _mesh_pos()
        sibling = (x, y, 1 - c)

        def copy(a, i):
            xp, yp = chips[i]
            return pltpu.make_async_remote_copy(
                src_ref=g_refs[a].at[_flat(xp, yp, 1 - c)], dst_ref=o_refs[a].at[i],
                send_sem=send_sems.at[a, i], recv_sem=recv_sems.at[a, i], device_id=sibling, device_id_type=MESH)

        cps = [copy(a, i) for a in range(n) for i in range(4)]
        for cp in cps:
            cp.start()
        for cp in cps:
            cp.wait()

    outs = _pcall(
        body, name=name,
        out_shape=[jax.ShapeDtypeStruct((4,) + a.shape[1:], a.dtype) for a in gs],
        in_specs=[ANY] * n, out_specs=[ANY] * n,
        scratch_shapes=[pltpu.SemaphoreType.DMA((n, 4)), pltpu.SemaphoreType.DMA((n, 4))],
    )(*gs)
    return list(outs)


def _pair_sum(g, p1, name):
    _, R, C = g.shape
    tr = 256 if R % 256 == 0 else R
    cidx = lax.axis_index("c").astype(jnp.int32).reshape(1)

    def body(c_ref, g_ref, p_ref, o_ref):
        o_ref[...] = (g_ref[...].astype(f32) + p_ref[...].astype(f32)).astype(o_ref.dtype)

    return _pcall(
        body, name=name,
        grid_spec=pltpu.PrefetchScalarGridSpec(
            num_scalar_prefetch=1, grid=(4, R // tr),
            in_specs=[pl.BlockSpec((1, tr, C), lambda i, r, c_ref: (2 * i + c_ref[0], r, 0)),
                      pl.BlockSpec((1, tr, C), lambda i, r, c_ref: (i, r, 0))],
            out_specs=pl.BlockSpec((1, tr, C), lambda i, r, c_ref: (i, r, 0))),
        out_shape=jax.ShapeDtypeStruct((4, R, C), g.dtype),
        compiler_params=_cparams("parallel", "parallel"),
    )(cidx, g, p1)


HBM = pl.BlockSpec(memory_space=pltpu.HBM)
SEM = pl.BlockSpec(memory_space=pltpu.SEMAPHORE)
EFFECT = pltpu.SideEffectType.DATAFLOW_SIDE_EFFECTING


def _peers(x, y, c):
    out = []
    for k in range(1, N_DEV):
        kx, ky, kc = (k >> 2) & 1, (k >> 1) & 1, k & 1
        out.append(((1 - x) if kx else x, (1 - y) if ky else y, (1 - c) if kc else c))
    return out


SPREAD_COPIES = {"gather": N_DEV - 1, "scatter": N_DEV - 1, "chips": 3}


def _spread_copy(src_ref, land_ref, send_sems, recv_sems, k, plan):
    x, y, c = _mesh_pos()
    if plan == "chips":
        px, py = [(1 - x, y), (x, 1 - y), (1 - x, 1 - y)][k]
        peer, src, slot = (px, py, c), src_ref.at[2 * px + py], 2 * x + y
    else:
        peer = _peers(x, y, c)[k]
        src, slot = (src_ref.at[_flat(*peer)] if plan == "scatter" else src_ref), _flat(x, y, c)
    return pltpu.make_async_remote_copy(
        src_ref=src, dst_ref=land_ref.at[slot], send_sem=send_sems.at[k], recv_sem=recv_sems.at[k],
        device_id=peer, device_id_type=MESH)


def _spread_start(src, after, plan, name):
    land_shape = (N_DEV,) + src.shape if plan == "gather" else src.shape
    n_copies = SPREAD_COPIES[plan]

    def body(src_ref, land_ref, after_ref, send_sems, recv_sems, src_thru, land_thru, token):
        for k in range(n_copies):
            _spread_copy(src_ref, land_ref, send_sems, recv_sems, k, plan).start()
        token[...] = jnp.zeros_like(token)

    return _pcall(
        body, name=name,
        out_shape=(pltpu.SemaphoreType.DMA((n_copies,)), pltpu.SemaphoreType.DMA((n_copies,)),
                   pltpu.HBM(src.shape, src.dtype), pltpu.HBM(land_shape, src.dtype), jax.ShapeDtypeStruct((8, LANE), f32)),
        in_specs=(HBM, HBM, ANY), out_specs=(SEM, SEM, HBM, HBM, pl.BlockSpec(memory_space=pltpu.VMEM)),
        input_output_aliases={0: 2, 1: 3},
        compiler_params=pltpu.CompilerParams(has_side_effects=EFFECT),
    )(pltpu.with_memory_space_constraint(src, pltpu.HBM),
      pltpu.with_memory_space_constraint(lax.empty(land_shape, src.dtype), pltpu.HBM), after)


def _spread_wait(started, after, plan, name):
    send_sems, recv_sems, src_thru, land_thru, _ = started

    def body(src_ref, land_ref, send_sems, recv_sems, after_ref, src_dead, got_ref):
        for k in range(SPREAD_COPIES[plan]):
            cp = _spread_copy(src_ref, land_ref, send_sems, recv_sems, k, plan)
            cp.wait_send()
            cp.wait_recv()

    return _pcall(
        body, name=name,
        out_shape=(pltpu.HBM(src_thru.shape, src_thru.dtype), pltpu.HBM(land_thru.shape, land_thru.dtype)),
        in_specs=(HBM, HBM, SEM, SEM, ANY), out_specs=(HBM, HBM), input_output_aliases={0: 0, 1: 1},
        compiler_params=pltpu.CompilerParams(has_side_effects=EFFECT),
    )(src_thru, land_thru, send_sems, recv_sems, after)


def _spread_finish(started, after, plan, name, slot):
    src, land = _spread_wait(started, after, plan, name)
    block = src if plan == "gather" else lax.dynamic_index_in_dim(src, slot, 0, keepdims=False)
    return _own_slot(land, block, slot)


def _own_slot(land, block, slot):
    zero = jnp.zeros((), jnp.int32)
    return lax.dynamic_update_slice(land, block[None], (slot.astype(jnp.int32),) + (zero,) * block.ndim)


COL_TILE = 256


def _cast_w_in(w3):
    n = w3.shape[0]

    def body(w_ref, o_ref):
        o_ref[...] = w_ref[:, 0, :].astype(bf16)

    tile = 2 * COL_TILE
    return _pcall(
        body, name="cast_w_in", grid=(D_MODEL // tile,),
        in_specs=[pl.BlockSpec((n, 1, tile), lambda j: (0, 0, j))],
        out_specs=pl.BlockSpec((n, tile), lambda j: (0, j)),
        out_shape=jax.ShapeDtypeStruct((n, D_MODEL), bf16),
        compiler_params=_cparams("parallel"),
    )(w3)


def _relayout_w_in(win_g):
    def body(g_ref, o_ref):
        used = OFF_BA + NAT_CONV - NAT_BA
        o_ref[used:PROJ_PAD, :] = jnp.zeros((PROJ_PAD - used, COL_TILE), o_ref.dtype)
        for d in range(N_DEV):
            for lo, width, dst in _layout_segments(d * SHARD_W, (d + 1) * SHARD_W):
                src = lo - d * SHARD_W
                o_ref[dst:dst + width, :] = g_ref[d, src:src + width, :]

    return _pcall(
        body, name="relayout_w_in", grid=(D_MODEL // COL_TILE,),
        in_specs=[pl.BlockSpec((N_DEV, SHARD_W, COL_TILE), lambda j: (0, 0, j))],
        out_specs=pl.BlockSpec((PROJ_PAD, COL_TILE), lambda j: (0, j)),
        out_shape=jax.ShapeDtypeStruct((PROJ_PAD, D_MODEL), win_g.dtype),
        compiler_params=_cparams("parallel"),
    )(win_g)


def _grad_blocks(g_t):
    def body(p_ref, o_ref):
        for d in range(N_DEV):
            for lo, width, src in _layout_segments(d * SHARD_W, (d + 1) * SHARD_W):
                dst = lo - d * SHARD_W
                o_ref[d, dst:dst + width, :] = p_ref[src:src + width, :]

    return _pcall(
        body, name="grad_blocks", grid=(D_MODEL // COL_TILE,),
        in_specs=[pl.BlockSpec((PROJ_PAD, COL_TILE), lambda j: (0, j))],
        out_specs=pl.BlockSpec((N_DEV, SHARD_W, COL_TILE), lambda j: (0, 0, j)),
        out_shape=jax.ShapeDtypeStruct((N_DEV, SHARD_W, D_MODEL), bf16),
        compiler_params=_cparams("parallel"),
    )(g_t)


def _in_proj(x, nw, wpad_t, after):
    L = x.shape[0]
    tn = 768
    nj = wpad_t.shape[0] // tn

    def body(x_ref, nw_ref, w_ref, after_ref, proj_ref, h_ref):
        @pl.when(pl.program_id(0) == 0)
        def _():
            for r in range(0, L, 256):
                xs = x_ref[r:r + 256, :]
                ms = jnp.mean(xs * xs, axis=-1, keepdims=True)
                h_ref[r:r + 256, :] = ((xs * lax.rsqrt(ms + EPS)) * nw_ref[...]).astype(bf16)
        for r in range(0, L, 512):
            proj_ref[r:r + 512, :] = lax.dot_general(h_ref[r:r + 512, :], w_ref[...], (((1,), (1,)), ((), ())),
                                                     preferred_element_type=f32)

    return _pcall(
        body, name="in_proj", grid=(nj,),
        in_specs=[pl.BlockSpec((L, D_MODEL), lambda j: (0, 0)), pl.BlockSpec((1, D_MODEL), lambda j: (0, 0)),
                  pl.BlockSpec((tn, D_MODEL), lambda j: (j, 0)), ANY],
        out_specs=[pl.BlockSpec((L, tn), lambda j: (0, j)), pl.BlockSpec((L, D_MODEL), lambda j: (0, 0))],
        out_shape=[jax.ShapeDtypeStruct((L, wpad_t.shape[0]), f32), jax.ShapeDtypeStruct((L, D_MODEL), bf16)],
        compiler_params=_cparams("arbitrary"),
    )(x, nw, wpad_t, after)


HALVES = [slice(i * LANE, (i + 1) * LANE) for i in range(ELT_W // LANE)]
QKV_W = 512
QKV_HEADS = [slice(i * LANE, (i + 1) * LANE) for i in range(QKV_W // LANE)]
STEPS_PER_GROUP = GDN_WIDTH // QKV_W


def _conv4(x, cw_ref, ls):
    return (cw_ref[3:4, ls] * x + cw_ref[2:3, ls] * _shift_down(x, 1) + cw_ref[1:2, ls] * _shift_down(x, 2)
            + cw_ref[0:1, ls] * _shift_down(x, 3))


def _qkv_act(proj, cw):
    L = proj.shape[0]

    def body(x_ref, cw_ref, o_ref):
        j = pl.program_id(0)
        scale = jnp.where(j < STEPS_PER_GROUP, HEAD_DIM ** -0.5, 1.0).astype(f32)
        for ls in QKV_HEADS:
            c = _conv4(x_ref[:, ls], cw_ref, ls)
            a = c * _sigmoid(c)
            rn = lax.rsqrt(jnp.sum(a * a, axis=1, keepdims=True) + EPS)
            o_ref[:, ls] = jnp.where(j < 2 * STEPS_PER_GROUP, (a * rn) * scale, a)

    return _pcall(
        body, name="qkv_act", grid=(3 * STEPS_PER_GROUP,),
        in_specs=[pl.BlockSpec((L, QKV_W), lambda j: (0, j)), pl.BlockSpec((4, QKV_W), lambda j: (0, j))],
        out_specs=pl.BlockSpec((L, QKV_W), lambda j: (0, j)),
        out_shape=jax.ShapeDtypeStruct((L, 3 * GDN_WIDTH), f32),
        compiler_params=_cparams("parallel"),
    )(proj, cw)


def _scalars(proj, alog_p, dtb_p):
    L = proj.shape[0]
    nc = L // CHUNK

    def body(x_ref, al_ref, dt_ref, sc_ref, gr_ref):
        x = x_ref[...]
        lane = _lanes(x.shape)
        beta = _sigmoid(x)
        g = -jnp.exp(al_ref[...]) * _softplus(x + dt_ref[...])
        gc = jnp.where((lane >= HEADS) & (lane < 2 * HEADS), g, 0.0)
        rc = _rows(x.shape) & (CHUNK - 1)
        for s in (1, 2, 4, 8, 16, 32):
            gc = gc + jnp.where(rc >= s, pltpu.roll(gc, s, 0), 0.0)
        sc_ref[...] = jnp.where(lane < HEADS, beta, gc)
        sel = (_lanes((HEADS, LANE)) == _rows((HEADS, LANE)) + HEADS).astype(f32)
        for c in range(nc):
            gr_ref[c] = lax.dot_general(sel, sc_ref[c * CHUNK:(c + 1) * CHUNK, :], (((1,), (1,)), ((), ())),
                                        preferred_element_type=f32, precision=lax.Precision.HIGHEST)

    return _pcall(
        body, name="scalars", grid=(1,),
        in_specs=[pl.BlockSpec((L, LANE), lambda i: (0, OFF_BA // LANE)), pl.BlockSpec((1, LANE), lambda i: (0, 0)),
                  pl.BlockSpec((1, LANE), lambda i: (0, 0))],
        out_specs=[pl.BlockSpec((L, LANE), lambda i: (0, 0)), pl.BlockSpec((nc, HEADS, CHUNK), lambda i: (0, 0, 0))],
        out_shape=[jax.ShapeDtypeStruct((L, LANE), f32), jax.ShapeDtypeStruct((nc, HEADS, CHUNK), f32)],
        compiler_params=_cparams("arbitrary"),
    )(proj, alog_p, dtb_p)


def _head_scalars(sc, gr_ref, h, ci=0):
    lane = _lanes(sc.shape)
    beta = jnp.sum(jnp.where(lane == h, sc, 0.0), axis=1, keepdims=True)
    gcc = jnp.sum(jnp.where(lane == HEADS + h, sc, 0.0), axis=1, keepdims=True)
    gcr = gr_ref[ci, h:h + 1, :]
    gl = jnp.sum(jnp.where(_lanes(gcr.shape) == CHUNK - 1, gcr, 0.0), axis=1, keepdims=True)
    ii, jj = _rows((CHUNK, CHUNK)), _lanes((CHUNK, CHUNK))
    dmat = jnp.where(ii >= jj, jnp.exp(jnp.minimum(gcc - gcr, 0.0)), 0.0)
    dmat_t = jnp.where(jj >= ii, jnp.exp(jnp.minimum(gcr - gcc, 0.0)), 0.0)
    return beta, gcc, gl, dmat, dmat_t, ii, jj


def _gdn_fwd(qkv, sc, gr):
    L = qkv.shape[0]
    nc = L // CHUNK
    W = GDN_WIDTH
    cps = GDN_CPS if nc % GDN_CPS == 0 else 1
    rows_per_step = cps * CHUNK

    def body(qkv_ref, sc_ref, gr_ref, o_ref, u_ref, w_ref, vn_ref, t_ref, sp_ref, s_scr):
        @pl.when(pl.program_id(0) == 0)
        def _():
            s_scr[...] = jnp.zeros_like(s_scr)
        HS = range(cps * HEADS)
        hd = [i % HEADS for i in HS]
        rs = [slice((i // HEADS) * CHUNK, (i // HEADS + 1) * CHUNK) for i in HS]
        cs = [slice(hd[i] * HEAD_DIM, (hd[i] + 1) * HEAD_DIM) for i in HS]
        q = [qkv_ref[rs[i], hd[i] * HEAD_DIM:(hd[i] + 1) * HEAD_DIM] for i in HS]
        k = [qkv_ref[rs[i], W + hd[i] * HEAD_DIM:W + (hd[i] + 1) * HEAD_DIM] for i in HS]
        v = [qkv_ref[rs[i], 2 * W + hd[i] * HEAD_DIM:2 * W + (hd[i] + 1) * HEAD_DIM] for i in HS]
        hsc = [_head_scalars(sc_ref[rs[i], :], gr_ref, hd[i], i // HEADS) for i in HS]
        beta, gcc, gl, dmat = ([x[i] for x in hsc] for i in range(4))
        ii, jj = hsc[0][5], hsc[0][6]
        eg = [jnp.exp(gcc[h]) for h in HS]
        kb = [k[h] * beta[h] for h in HS]
        kk = [_mm_nt(kb[h], k[h]) for h in HS]
        qk = [_mm_nt(q[h], k[h]) for h in HS]
        n0 = [-jnp.where(ii > jj, kk[h] * dmat[h], 0.0) for h in HS]
        n1 = [_mm(n0[h], n0[h]) for h in HS]
        n2 = [_mm(n1[h], n1[h]) for h in HS]
        p01 = [n0[h] + n1[h] + _mm(n0[h], n1[h]) for h in HS]
        n3 = [_mm(n2[h], n2[h]) for h in HS]
        n4 = [_mm(n3[h], n3[h]) for h in HS]
        p23 = [n2[h] + n3[h] + _mm(n2[h], n3[h]) for h in HS]
        n5 = [_mm(n4[h], n4[h]) for h in HS]
        p03 = [p01[h] + p23[h] + _mm(p01[h], p23[h]) for h in HS]
        p45 = [n4[h] + n5[h] + _mm(n4[h], n5[h]) for h in HS]
        t = [p03[h] + p45[h] + _mm(p03[h], p45[h]) for h in HS]
        vb = [v[h] * beta[h] for h in HS]
        kbg = [kb[h] * eg[h] for h in HS]
        uw = [_mm(t[h], _cat16([vb[h], kbg[h]], 1)) for h in HS]
        u = [vb[h] + uw[h][:, :HEAD_DIM] for h in HS]
        w = [kbg[h] + uw[h][:, HEAD_DIM:] for h in HS]
        wq = [_cat16([w[h], q[h] * eg[h]], 0) for h in HS]
        p = [jnp.where(ii >= jj, qk[h] * dmat[h], 0.0) for h in HS]
        ks = [k[h] * jnp.exp(gl[h] - gcc[h]) for h in HS]
        s = [s_scr[h] for h in range(HEADS)]
        for ci in range(cps):
            IS = range(ci * HEADS, (ci + 1) * HEADS)
            ws = [_mm(wq[i], s[hd[i]]) for i in IS]
            vn = [u[i] - ws[hd[i]][:CHUNK] for i in IS]
            pv = [_mm(p[i], vn[hd[i]]) for i in IS]
            kv = [_mm_tn(ks[i], vn[hd[i]]) for i in IS]
            for i in IS:
                h = hd[i]
                sp_ref[ci, cs[i], :] = s[h]
                o_ref[rs[i], cs[i]] = ws[h][CHUNK:] + pv[h]
                vn_ref[rs[i], cs[i]] = vn[h].astype(bf16)
            s = [jnp.exp(gl[i]) * s[hd[i]] + kv[hd[i]] for i in IS]
        for h in range(HEADS):
            s_scr[h] = s[h]
        for i in HS:
            u_ref[rs[i], cs[i]] = u[i].astype(bf16)
            w_ref[rs[i], cs[i]] = w[i].astype(bf16)
            t_ref[i // HEADS, hd[i]] = t[i].astype(bf16)

    row = lambda c: (c, 0)
    act, act16 = jax.ShapeDtypeStruct((L, W), f32), jax.ShapeDtypeStruct((L, W), bf16)
    return _pcall(
        body, name="gdn_fwd", grid=(nc // cps,),
        in_specs=[pl.BlockSpec((rows_per_step, 3 * W), row), pl.BlockSpec((rows_per_step, LANE), row),
                  pl.BlockSpec((cps, HEADS, CHUNK), lambda c: (c, 0, 0))],
        out_specs=[pl.BlockSpec((rows_per_step, W), row)] * 4 + [
            pl.BlockSpec((cps, HEADS, CHUNK, CHUNK), lambda c: (c, 0, 0, 0)),
            pl.BlockSpec((cps, W, HEAD_DIM), lambda c: (c, 0, 0))],
        out_shape=[act, act16, act16, act16, jax.ShapeDtypeStruct((nc, HEADS, CHUNK, CHUNK), bf16),
                   jax.ShapeDtypeStruct((nc, W, HEAD_DIM), f32)],
        scratch_shapes=[pltpu.VMEM((HEADS, HEAD_DIM, HEAD_DIM), f32)],
        compiler_params=_cparams("arbitrary"),
    )(qkv, sc, gr)


def _gdn_gate(o, proj, gnw):
    L = o.shape[0]

    def body(o_ref, z_ref, w_ref, m_ref):
        for ls in HALVES:
            ov, z = o_ref[:, ls], z_ref[:, ls]
            rms = lax.rsqrt(jnp.mean(ov * ov, axis=-1, keepdims=True) + EPS)
            m_ref[:, ls] = (((ov * rms) * w_ref[...]) * (z * _sigmoid(z))).astype(bf16)

    return _pcall(
        body, name="gdn_gate", grid=(GDN_WIDTH // ELT_W,),
        in_specs=[pl.BlockSpec((L, ELT_W), lambda j: (0, j)), pl.BlockSpec((L, ELT_W), lambda j: (0, OFF_ZG // ELT_W + j)),
                  pl.BlockSpec((1, LANE), lambda j: (0, 0))],
        out_specs=pl.BlockSpec((L, ELT_W), lambda j: (0, j)),
        out_shape=jax.ShapeDtypeStruct((L, GDN_WIDTH + CONV_WIDTH), bf16),
        compiler_params=_cparams("parallel"),
    )(o, proj, gnw)


def _conv3(u, cw_ref, ls):
    return cw_ref[2:3, ls] * u + cw_ref[1:2, ls] * _shift_down(u, 1) + cw_ref[0:1, ls] * _shift_down(u, 2)


def _conv_specs(L):
    return [pl.BlockSpec((L, CONV_BLOCK), lambda j: (0, OFF_CONV // CONV_BLOCK + j)),
            pl.BlockSpec((3, ELT_W), lambda j: (0, j)), pl.BlockSpec((1, ELT_W), lambda j: (0, j))]


def _conv_parts(ls):
    return [slice(g * ELT_W + ls.start, g * ELT_W + ls.stop) for g in range(4)]


def _conv_fwd(proj, cw, cb, mix):
    L = proj.shape[0]

    def body(p_ref, cw_ref, cb_ref, mix_in, m_ref):
        for ls in HALVES:
            sb, sc_, sh, sz = _conv_parts(ls)
            z = p_ref[:, sz]
            cv = _conv3(p_ref[:, sc_] * p_ref[:, sh], cw_ref, ls) + cb_ref[:, ls]
            m_ref[:, ls] = ((p_ref[:, sb] * cv) * (z * _sigmoid(z))).astype(bf16)

    return _pcall(
        body, name="conv_fwd", grid=(CONV_WIDTH // ELT_W,),
        in_specs=_conv_specs(L) + [ANY], out_specs=pl.BlockSpec((L, ELT_W), lambda j: (0, GDN_WIDTH // ELT_W + j)),
        out_shape=jax.ShapeDtypeStruct(mix.shape, mix.dtype), input_output_aliases={3: 0},
        compiler_params=_cparams("parallel"),
    )(proj, cw, cb, mix)


def _out_proj_loss(x, mix, wo, fw, tgt):
    L = x.shape[0]
    tm = min(512, L)
    MW = GDN_WIDTH + CONV_WIDTH

    def body(x_ref, m_ref, wo_ref, fw_ref, t_ref, dy_ref, dyb_ref, dm_ref, gfw_ref, loss_ref):
        @pl.when(pl.program_id(0) == 0)
        def _():
            gfw_ref[...] = jnp.zeros_like(gfw_ref)
            loss_ref[...] = jnp.zeros_like(loss_ref)
        y = x_ref[...] + jnp.dot(m_ref[...], wo_ref[...], preferred_element_type=f32)
        r = lax.rsqrt(jnp.mean(y * y, axis=-1, keepdims=True) + EPS)
        yh = y * r
        fwv = fw_ref[...]
        diff = yh * fwv - t_ref[...]
        loss_ref[...] += jnp.sum(jnp.sum(diff * diff, axis=-1, keepdims=True), axis=0, keepdims=True) * (0.5 / D_MODEL)
        dout = diff * (1.0 / D_MODEL)
        gfw_ref[...] += jnp.sum(dout * yh, axis=0, keepdims=True)
        dyh = dout * fwv
        dy = r * (dyh - yh * jnp.mean(dyh * yh, axis=-1, keepdims=True))
        dy_ref[...] = dy
        dyb = dy.astype(bf16)
        dyb_ref[...] = dyb
        dm_ref[...] = lax.dot_general(dyb, wo_ref[...], (((1,), (1,)), ((), ())), preferred_element_type=f32)

    row = lambda i: (i, 0)
    fix = lambda i: (0, 0)
    act = jax.ShapeDtypeStruct((L, D_MODEL), f32)
    return _pcall(
        body, name="out_proj_loss", grid=(L // tm,),
        in_specs=[pl.BlockSpec((tm, D_MODEL), row), pl.BlockSpec((tm, MW), row), pl.BlockSpec((MW, D_MODEL), fix),
                  pl.BlockSpec((1, D_MODEL), fix), pl.BlockSpec((tm, D_MODEL), row)],
        out_specs=[pl.BlockSpec((tm, D_MODEL), row), pl.BlockSpec((tm, D_MODEL), row), pl.BlockSpec((tm, MW), row),
                   pl.BlockSpec((1, D_MODEL), fix), pl.BlockSpec((1, LANE), fix)],
        out_shape=[act, jax.ShapeDtypeStruct((L, D_MODEL), bf16), jax.ShapeDtypeStruct((L, MW), f32),
                   jax.ShapeDtypeStruct((1, D_MODEL), f32), jax.ShapeDtypeStruct((1, LANE), f32)],
        compiler_params=_cparams("arbitrary"),
    )(x, mix, wo, fw, tgt)


def _tn_matmul(a, b, name):
    L, M = a.shape
    N = b.shape[1]
    tm = 512 if M % 512 == 0 else (768 if M % 768 == 0 else M)

    def body(a_ref, b_ref, o_ref):
        o_ref[...] = lax.dot_general(a_ref[...], b_ref[...], (((0,), (0,)), ((), ())),
                                     preferred_element_type=f32).astype(o_ref.dtype)

    return _pcall(
        body, name=name, grid=(M // tm,),
        in_specs=[pl.BlockSpec((L, tm), lambda i: (0, i)), pl.BlockSpec((L, N), lambda i: (0, 0))],
        out_specs=pl.BlockSpec((tm, N), lambda i: (i, 0)),
        out_shape=jax.ShapeDtypeStruct((M, N), bf16),
        compiler_params=_cparams("parallel"),
    )(a, b)


def _gdn_gate_bwd(o, proj, gnw, dmix_a, after):
    L = o.shape[0]

    def body(o_ref, z_ref, w_ref, dm_ref, after_ref, do_ref, dz_ref, gw_ref):
        @pl.when(pl.program_id(0) == 0)
        def _():
            gw_ref[...] = jnp.zeros_like(gw_ref)
        wv = w_ref[...]
        for ls in HALVES:
            ov, z, dm = o_ref[:, ls], z_ref[:, ls], dm_ref[:, ls]
            rms = lax.rsqrt(jnp.mean(ov * ov, axis=-1, keepdims=True) + EPS)
            xh = ov * rms
            sg = _sigmoid(z)
            d_on = dm * (z * sg)
            dz_ref[:, ls] = (dm * (xh * wv) * (sg * (1.0 + z * (1.0 - sg)))).astype(bf16)
            gw_ref[...] += jnp.sum(d_on * xh, axis=0, keepdims=True)
            dxh = d_on * wv
            do_ref[:, ls] = (rms * (dxh - xh * jnp.mean(dxh * xh, axis=-1, keepdims=True))).astype(bf16)

    wide = pl.BlockSpec((L, ELT_W), lambda j: (0, j))
    return _pcall(
        body, name="gdn_gate_bwd", grid=(GDN_WIDTH // ELT_W,),
        in_specs=[wide, pl.BlockSpec((L, ELT_W), lambda j: (0, OFF_ZG // ELT_W + j)),
                  pl.BlockSpec((1, LANE), lambda j: (0, 0)), wide, ANY],
        out_specs=[wide, pl.BlockSpec((L, ELT_W), lambda j: (0, OFF_ZG // ELT_W + j)),
                   pl.BlockSpec((1, LANE), lambda j: (0, 0))],
        out_shape=[jax.ShapeDtypeStruct((L, GDN_WIDTH), bf16), jax.ShapeDtypeStruct((L, PROJ_PAD), bf16),
                   jax.ShapeDtypeStruct((1, LANE), f32)],
        compiler_params=_cparams("arbitrary"),
    )(o, proj, gnw, dmix_a, after)


def _conv_bwd(proj, cw, cb, dmix_b, dproj):
    L = proj.shape[0]

    def body(p_ref, cw_ref, cb_ref, dm_ref, dproj_in, dp_ref, gcw_ref, gcb_ref):
        for ls in HALVES:
            sb, sc_, sh, sz_ = _conv_parts(ls)
            bv, cv_, hv, z, dm = p_ref[:, sb], p_ref[:, sc_], p_ref[:, sh], p_ref[:, sz_], dm_ref[:, ls]
            u = cv_ * hv
            cv = _conv3(u, cw_ref, ls) + cb_ref[:, ls]
            sg = _sigmoid(z)
            sz = z * sg
            dp_ref[:, sb] = (dm * cv * sz).astype(bf16)
            dp_ref[:, sz_] = (dm * (bv * cv) * (sg * (1.0 + z * (1.0 - sg)))).astype(bf16)
            dcv = dm * bv * sz
            gcb_ref[:, ls] = jnp.sum(dcv, axis=0, keepdims=True)
            dcv1, dcv2 = _shift_up(dcv, 1), _shift_up(dcv, 2)
            gcw_ref[2:3, ls] = jnp.sum(dcv * u, axis=0, keepdims=True)
            gcw_ref[1:2, ls] = jnp.sum(dcv1 * u, axis=0, keepdims=True)
            gcw_ref[0:1, ls] = jnp.sum(dcv2 * u, axis=0, keepdims=True)
            du = cw_ref[2:3, ls] * dcv + cw_ref[1:2, ls] * dcv1 + cw_ref[0:1, ls] * dcv2
            dp_ref[:, sc_] = (du * hv).astype(bf16)
            dp_ref[:, sh] = (du * cv_).astype(bf16)

    return _pcall(
        body, name="conv_bwd", grid=(CONV_WIDTH // ELT_W,),
        in_specs=_conv_specs(L) + [pl.BlockSpec((L, ELT_W), lambda j: (0, GDN_WIDTH // ELT_W + j)), ANY],
        out_specs=[pl.BlockSpec((L, CONV_BLOCK), lambda j: (0, OFF_CONV // CONV_BLOCK + j)),
                   pl.BlockSpec((3, ELT_W), lambda j: (0, j)), pl.BlockSpec((1, ELT_W), lambda j: (0, j))],
        out_shape=[jax.ShapeDtypeStruct(dproj.shape, dproj.dtype), jax.ShapeDtypeStruct((3, CONV_WIDTH), f32),
                   jax.ShapeDtypeStruct((1, CONV_WIDTH), f32)],
        input_output_aliases={4: 0},
        compiler_params=_cparams("parallel"),
    )(proj, cw, cb, dmix_b, dproj)


def _gdn_bwd(qkv, sc, gr, u_all, w_all, vn_all, t_all, sp_all, do_all):
    L = qkv.shape[0]
    nc = L // CHUNK
    W = GDN_WIDTH
    cps = GDN_CPS_BWD if nc % GDN_CPS_BWD == 0 else 1
    rows_per_step = cps * CHUNK
    nsteps = nc // cps

    def body(qkv_ref, sc_ref, gr_ref, u_ref, w_ref, vn_ref, t_ref, sp_ref, do_ref, dqkv_ref, dsc_ref, dgr_ref, ds_scr):
        @pl.when(pl.program_id(0) == 0)
        def _():
            ds_scr[...] = jnp.zeros_like(ds_scr)
        nh, base = HEADS, 0
        HS = range(cps * nh)
        hl = [i % nh for i in HS]
        hd = [base + hl[i] for i in HS]
        rs = [slice((i // nh) * CHUNK, (i // nh + 1) * CHUNK) for i in HS]
        cs = [slice(hd[i] * HEAD_DIM, (hd[i] + 1) * HEAD_DIM) for i in HS]
        q = [qkv_ref[rs[i], hd[i] * HEAD_DIM:(hd[i] + 1) * HEAD_DIM] for i in HS]
        k = [qkv_ref[rs[i], W + hd[i] * HEAD_DIM:W + (hd[i] + 1) * HEAD_DIM] for i in HS]
        v = [qkv_ref[rs[i], 2 * W + hd[i] * HEAD_DIM:2 * W + (hd[i] + 1) * HEAD_DIM] for i in HS]
        hsc = [_head_scalars(sc_ref[rs[i], :], gr_ref, hd[i], i // nh) for i in HS]
        beta, gcc, gl, dmat, dmat_t = ([x[i] for x in hsc] for i in range(5))
        ii, jj = hsc[0][5], hsc[0][6]
        eg = [jnp.exp(gcc[h]) for h in HS]
        ekl = [jnp.exp(gl[h] - gcc[h]) for h in HS]
        egl = [jnp.exp(gl[h]) for h in HS]
        kb = [k[h] * beta[h] for h in HS]
        ks = [k[h] * ekl[h] for h in HS]
        do = [do_ref[rs[h], cs[h]] for h in HS]
        vn = [vn_ref[rs[h], cs[h]] for h in HS]
        s = [sp_ref[h // nh, cs[h], :] for h in HS]
        w = [w_ref[rs[h], cs[h]] for h in HS]
        qd = [q[h] * eg[h] for h in HS]

        kq = [_mm_nt(k[h], q[h]) for h in HS]
        p_t = [jnp.where(jj >= ii, kq[h] * dmat_t[h], 0.0) for h in HS]
        ptd = [_mm(p_t[h], do[h]) for h in HS]
        qw = [_cat16([qd[h], -w[h]], 0) for h in HS]
        dsn, dvn, dodv = [None] * len(HS), [None] * len(HS), [None] * len(HS)
        ds_cur = [ds_scr[base + h] for h in range(nh)]
        for ci in reversed(range(cps)):
            IS = range(ci * nh, (ci + 1) * nh)
            ksd = [_mm(ks[i], ds_cur[hl[i]]) for i in IS]
            for i in IS:
                dsn[i] = ds_cur[hl[i]]
                dvn[i] = ptd[i] + ksd[hl[i]]
                dodv[i] = _cat16([do[i], dvn[i]], 0)
            dsq = [_mm_tn(qw[i], dodv[i]) for i in IS]
            ds_cur = [egl[i] * ds_cur[hl[i]] + dsq[hl[i]] for i in IS]
        for h in range(nh):
            ds_scr[base + h] = ds_cur[h]
        x1 = [_mm_nt(dodv[h], s[h]) for h in HS]
        dks = [_mm_nt(vn[h], dsn[h]) for h in HS]
        dov = [_mm_nt(do[h], vn[h]) for h in HS]
        vdo = [_mm_nt(vn[h], do[h]) for h in HS]
        kk = [_mm_nt(kb[h], k[h]) for h in HS]
        qk = [_mm_nt(q[h], k[h]) for h in HS]
        dgl = [egl[h] * jnp.sum(jnp.sum(s[h] * dsn[h], axis=1, keepdims=True), axis=0, keepdims=True) for h in HS]
        dqd = [x1[h][:CHUNK] for h in HS]
        duw = [jnp.concatenate([dvn[h], -x1[h][CHUNK:]], axis=1) for h in HS]
        tdu = [_mm_tn(t_ref[h // nh, hd[h]], duw[h]) for h in HS]
        dvk = [duw[h] + tdu[h] for h in HS]
        uw = [jnp.concatenate([u_ref[rs[h], cs[h]], w[h]], axis=1) for h in HS]
        da = [-jnp.where(ii > jj, _mm_nt(dvk[h], uw[h]), 0.0) for h in HS]
        da_t = [-jnp.where(jj > ii, _mm_nt(uw[h], dvk[h]), 0.0) for h in HS]
        dp = [jnp.where(ii >= jj, dov[h], 0.0) for h in HS]
        dp_t = [jnp.where(jj >= ii, vdo[h], 0.0) for h in HS]
        r1 = [_mm(_cat16([da[h] * dmat[h], dp[h] * dmat[h]], 0), k[h]) for h in HS]
        dk1 = [_mm(_cat16([da_t[h] * dmat_t[h], dp_t[h] * dmat_t[h]], 1), _cat16([kb[h], q[h]], 0)) for h in HS]
        lane = _lanes((CHUNK, LANE))
        for ci in range(cps):
            dsc = jnp.zeros((CHUNK, LANE), f32)
            for i in range(ci * nh, (ci + 1) * nh):
                h = hd[i]
                a = jnp.where(ii > jj, kk[i] * dmat[i], 0.0)
                p = jnp.where(ii >= jj, qk[i] * dmat[i], 0.0)
                gmat = da[i] * a + dp[i] * p
                dvb, dkbg = dvk[i][:, :HEAD_DIM], dvk[i][:, HEAD_DIM:]
                kbg = kb[i] * eg[i]
                dkb = r1[i][:CHUNK] + dkbg * eg[i]
                dq = r1[i][CHUNK:] + dqd[i] * eg[i]
                dk = dk1[i] + dks[i] * ekl[i] + dkb * beta[i]
                dbeta = jnp.sum(dkb * k[i] + dvb * v[i], axis=1, keepdims=True)
                ksum = jnp.sum(dks[i] * ks[i], axis=1, keepdims=True)
                dgl_tot = dgl[i] + jnp.sum(ksum, axis=0, keepdims=True)
                dgc = (jnp.sum(gmat, axis=1, keepdims=True) + jnp.sum(dqd[i] * qd[i] + dkbg * kbg, axis=1, keepdims=True)
                       - ksum)
                dgc = dgc + jnp.where(_rows(dgc.shape) == CHUNK - 1, dgl_tot, 0.0)
                dqkv_ref[rs[i], h * HEAD_DIM:(h + 1) * HEAD_DIM] = dq
                dqkv_ref[rs[i], W + h * HEAD_DIM:W + (h + 1) * HEAD_DIM] = dk
                dqkv_ref[rs[i], 2 * W + h * HEAD_DIM:2 * W + (h + 1) * HEAD_DIM] = dvb * beta[i]
                dsc = jnp.where(lane == h, dbeta, jnp.where(lane == HEADS + h, dgc, dsc))
                dgr_ref[ci, h:h + 1, :] = jnp.sum(gmat, axis=0, keepdims=True)
            dsc_ref[ci * CHUNK:(ci + 1) * CHUNK, :] = dsc

    row = lambda c: (nsteps - 1 - c, 0)
    lead3 = lambda c: (nsteps - 1 - c, 0, 0)
    return _pcall(
        body, name="gdn_bwd", grid=(nsteps,),
        in_specs=[pl.BlockSpec((rows_per_step, 3 * W), row), pl.BlockSpec((rows_per_step, LANE), row),
                  pl.BlockSpec((cps, HEADS, CHUNK), lead3),
                  pl.BlockSpec((rows_per_step, W), row), pl.BlockSpec((rows_per_step, W), row),
                  pl.BlockSpec((rows_per_step, W), row),
                  pl.BlockSpec((cps, HEADS, CHUNK, CHUNK), lambda c: (nsteps - 1 - c, 0, 0, 0)),
                  pl.BlockSpec((cps, W, HEAD_DIM), lead3), pl.BlockSpec((rows_per_step, W), row)],
        out_specs=[pl.BlockSpec((rows_per_step, 3 * W), row), pl.BlockSpec((rows_per_step, LANE), row),
                   pl.BlockSpec((cps, HEADS, CHUNK), lead3)],
        out_shape=[jax.ShapeDtypeStruct((L, 3 * W), f32), jax.ShapeDtypeStruct((L, LANE), f32),
                   jax.ShapeDtypeStruct((nc, HEADS, CHUNK), f32)],
        scratch_shapes=[pltpu.VMEM((HEADS, HEAD_DIM, HEAD_DIM), f32)],
        compiler_params=_cparams("arbitrary"),
    )(qkv, sc, gr, u_all, w_all, vn_all, t_all, sp_all, do_all)


def _qkv_bwd(proj, cw, dn, dproj):
    L = proj.shape[0]

    def body(x_ref, cw_ref, dn_ref, dproj_in, dx_ref, gcw_ref):
        j = pl.program_id(0)
        steps = GDN_WIDTH // ELT_W
        scale = jnp.where(j < steps, HEAD_DIM ** -0.5, 1.0).astype(f32)
        for ls in HALVES:
            x, dn_v = x_ref[:, ls], dn_ref[:, ls]
            c = _conv4(x, cw_ref, ls)
            sg = _sigmoid(c)
            a = c * sg
            rn = lax.rsqrt(jnp.sum(a * a, axis=1, keepdims=True) + EPS)
            da_n = (scale * rn) * (dn_v - a * ((rn * rn) * jnp.sum(dn_v * a, axis=1, keepdims=True)))
            da = jnp.where(j < 2 * steps, da_n, dn_v)
            dc = da * (sg * (1.0 + c * (1.0 - sg)))
            dc1, dc2, dc3 = _shift_up(dc, 1), _shift_up(dc, 2), _shift_up(dc, 3)
            gcw_ref[3:4, ls] = jnp.sum(dc * x, axis=0, keepdims=True)
            gcw_ref[2:3, ls] = jnp.sum(dc1 * x, axis=0, keepdims=True)
            gcw_ref[1:2, ls] = jnp.sum(dc2 * x, axis=0, keepdims=True)
            gcw_ref[0:1, ls] = jnp.sum(dc3 * x, axis=0, keepdims=True)
            dx = cw_ref[3:4, ls] * dc + cw_ref[2:3, ls] * dc1 + cw_ref[1:2, ls] * dc2 + cw_ref[0:1, ls] * dc3
            dx_ref[:, ls] = dx.astype(bf16)

    col = pl.BlockSpec((L, ELT_W), lambda j: (0, j))
    wspec = pl.BlockSpec((4, ELT_W), lambda j: (0, j))
    return _pcall(
        body, name="qkv_bwd", grid=(3 * GDN_WIDTH // ELT_W,),
        in_specs=[col, wspec, col, ANY], out_specs=[col, wspec],
        out_shape=[jax.ShapeDtypeStruct(dproj.shape, dproj.dtype), jax.ShapeDtypeStruct((4, 3 * GDN_WIDTH), f32)],
        input_output_aliases={3: 0},
        compiler_params=_cparams("parallel"),
    )(proj, cw, dn, dproj)


def _scalars_bwd(proj, alog_p, dtb_p, dsc, dgr_col, dproj):
    L = proj.shape[0]

    def body(x_ref, al_ref, dt_ref, dsc_ref, dgr_ref, dproj_in, dba_ref, gs_ref):
        x, dsc_v = x_ref[...], dsc_ref[...]
        lane = _lanes(x.shape)
        dec = (lane >= HEADS) & (lane < 2 * HEADS)
        dg = jnp.where(dec, dsc_v - dgr_ref[...], 0.0)
        rc = _rows(x.shape) & (CHUNK - 1)
        for s in (1, 2, 4, 8, 16, 32):
            dg = dg + jnp.where(rc + s < CHUNK, pltpu.roll(dg, L - s, 0), 0.0)
        xa = x + dt_ref[...]
        ea = jnp.exp(al_ref[...])
        g = -ea * _softplus(xa)
        da = dg * (-ea) * _sigmoid(xa)
        beta = _sigmoid(x)
        db = dsc_v * beta * (1.0 - beta)
        dba_ref[:, :LANE] = jnp.where(lane < HEADS, db, jnp.where(dec, da, 0.0)).astype(bf16)
        dba_ref[:, LANE:] = jnp.zeros((L, ELT_W - LANE), bf16)
        g_al = jnp.sum(jnp.where(dec, dg * g, 0.0), axis=0, keepdims=True)
        g_dt = jnp.sum(jnp.where(dec, da, 0.0), axis=0, keepdims=True)
        row8 = _rows(gs_ref.shape)
        gs = jnp.where(row8 == 0, g_al, jnp.where(row8 == 1, g_dt, 0.0))
        gs_ref[...] = pltpu.roll(gs, LANE - HEADS, 1)

    full = pl.BlockSpec((L, LANE), lambda i: (0, 0))
    vec = pl.BlockSpec((1, LANE), lambda i: (0, 0))
    return _pcall(
        body, name="scalars_bwd", grid=(1,),
        in_specs=[pl.BlockSpec((L, LANE), lambda i: (0, OFF_BA // LANE)), vec, vec, full, full, ANY],
        out_specs=[pl.BlockSpec((L, ELT_W), lambda i: (0, OFF_BA // ELT_W)), pl.BlockSpec((8, LANE), lambda i: (0, 0))],
        out_shape=[jax.ShapeDtypeStruct(dproj.shape, dproj.dtype), jax.ShapeDtypeStruct((8, LANE), f32)],
        input_output_aliases={5: 0},
        compiler_params=_cparams("arbitrary"),
    )(proj, alog_p, dtb_p, dsc, dgr_col, dproj)


def _input_grad(dproj, wpad, x, nw, dy, after):
    L = x.shape[0]
    tm = min(512, L)
    cuts = (0, 3072, 5120, 7168, PROJ_PAD)
    nk = len(cuts) - 1

    def body(dp_ref, w_hbm, x_ref, nw_ref, dy_ref, after_ref, gx_ref, gnw_ref, w_vmem, sems):
        first = pl.program_id(0) == 0
        loads = [pltpu.make_async_copy(w_hbm.at[cuts[k]:cuts[k + 1], :], w_vmem.at[cuts[k]:cuts[k + 1], :], sems.at[k])
                 for k in range(nk)]

        @pl.when(first)
        def _():
            for cp in loads:
                cp.start()
            gnw_ref[...] = jnp.zeros_like(gnw_ref)
        dh = None
        for k in range(nk):
            pl.when(first)(loads[k].wait)
            part = jnp.dot(dp_ref[:, cuts[k]:cuts[k + 1]], w_vmem[cuts[k]:cuts[k + 1], :], preferred_element_type=f32)
            dh = part if dh is None else dh + part
        xv, nwv = x_ref[...], nw_ref[...]
        r = lax.rsqrt(jnp.mean(xv * xv, axis=-1, keepdims=True) + EPS)
        xh = xv * r
        gnw_ref[...] += jnp.sum(dh * xh, axis=0, keepdims=True)
        dxh = dh * nwv
        gx_ref[...] = dy_ref[...] + r * (dxh - xh * jnp.mean(dxh * xh, axis=-1, keepdims=True))

    row = lambda i: (i, 0)
    fix = lambda i: (0, 0)
    return _pcall(
        body, name="input_grad", grid=(L // tm,),
        in_specs=[pl.BlockSpec((tm, PROJ_PAD), row), ANY, pl.BlockSpec((tm, D_MODEL), row),
                  pl.BlockSpec((1, D_MODEL), fix), pl.BlockSpec((tm, D_MODEL), row), ANY],
        out_specs=[pl.BlockSpec((tm, D_MODEL), row), pl.BlockSpec((1, D_MODEL), fix)],
        out_shape=[jax.ShapeDtypeStruct((L, D_MODEL), f32), jax.ShapeDtypeStruct((1, D_MODEL), f32)],
        scratch_shapes=[pltpu.VMEM(wpad.shape, bf16), pltpu.SemaphoreType.DMA((nk,))],
        compiler_params=_cparams("arbitrary"),
    )(dproj, wpad, x, nw, dy, after)


def _adamw_reduce(parts, w, m, v, name):
    R, C = w.shape
    n_parts = parts.shape[0]
    tr = 128 if R % 128 == 0 else R
    c1 = 1.0 - ADAM_B1 ** ADAM_STEP
    c2 = 1.0 - ADAM_B2 ** ADAM_STEP

    def body(p_ref, w_ref, m_ref, v_ref, g_ref, d_ref, nm_ref, nv_ref):
        g = p_ref[0].astype(f32)
        for s in range(1, n_parts):
            g = g + p_ref[s].astype(f32)
        nm = ADAM_B1 * m_ref[...] + (1.0 - ADAM_B1) * g
        nv = ADAM_B2 * v_ref[...] + (1.0 - ADAM_B2) * (g * g)
        g_ref[...] = g
        nm_ref[...] = nm
        nv_ref[...] = nv
        d_ref[...] = -ADAM_LR * ((nm / c1) / (jnp.sqrt(nv / c2) + ADAM_EPS) + ADAM_WD * w_ref[...])

    blk = pl.BlockSpec((tr, C), lambda i: (i, 0))
    out = jax.ShapeDtypeStruct((R, C), f32)
    return _pcall(
        body, name=name, grid=(R // tr,),
        in_specs=[pl.BlockSpec((n_parts, tr, C), lambda i: (0, i, 0)), blk, blk, blk],
        out_specs=[blk] * 4, out_shape=[out] * 4,
        compiler_params=_cparams("parallel"),
    )(parts, w, m, v)


SMALL_SLOTS = ((0, D_MODEL), (D_MODEL, D_MODEL), (2 * D_MODEL, D_MODEL), (3 * D_MODEL, LANE),
               (3 * D_MODEL + LANE, HEADS), (3 * D_MODEL + 2 * LANE, HEADS))
SMALL_LOSS = 3 * D_MODEL + 3 * LANE
SMALL_W = SMALL_LOSS + LANE


def _pack_small(gs, after):
    def body(nw_ref, cb_ref, fw_ref, gn_ref, sc_ref, ls_ref, after_ref, o_ref):
        for ref, (start, width) in zip((nw_ref, cb_ref, fw_ref, gn_ref), SMALL_SLOTS[:4]):
            o_ref[:, start:start + width] = ref[...]
        o_ref[:, SMALL_SLOTS[4][0]:SMALL_SLOTS[4][0] + LANE] = sc_ref[0:1, :]
        o_ref[:, SMALL_SLOTS[5][0]:SMALL_SLOTS[5][0] + LANE] = sc_ref[1:2, :]
        o_ref[:, SMALL_LOSS:SMALL_W] = ls_ref[...]

    vm = pl.BlockSpec(memory_space=pltpu.VMEM)
    return _pcall(body, name="pack_small_grads", out_shape=jax.ShapeDtypeStruct((1, SMALL_W), f32),
                  in_specs=[vm] * 6 + [ANY], out_specs=vm)(*gs, after)


def _adamw_small(parts, ws, ms, vs):
    c1 = 1.0 - ADAM_B1 ** ADAM_STEP
    c2 = 1.0 - ADAM_B2 ** ADAM_STEP
    np_ = len(ws)

    def body(*refs):
        p_ref = refs[0]
        w_refs, m_refs, v_refs = refs[1:1 + np_], refs[1 + np_:1 + 2 * np_], refs[1 + 2 * np_:1 + 3 * np_]
        outs = refs[1 + 3 * np_:]
        g_refs, d_refs, nm_refs, nv_refs = (outs[i * np_:(i + 1) * np_] for i in range(4))
        loss_ref = outs[4 * np_]

        def total(start, width):
            t = p_ref[0, :, start:start + width]
            for s in range(1, N_DEV):
                t = t + p_ref[s, :, start:start + width]
            return t

        for i, (start, width) in enumerate(SMALL_SLOTS):
            g = total(start, width)
            nm = ADAM_B1 * m_refs[i][...] + (1.0 - ADAM_B1) * g
            nv = ADAM_B2 * v_refs[i][...] + (1.0 - ADAM_B2) * (g * g)
            g_refs[i][...] = g
            nm_refs[i][...] = nm
            nv_refs[i][...] = nv
            d_refs[i][...] = -ADAM_LR * ((nm / c1) / (jnp.sqrt(nv / c2) + ADAM_EPS) + ADAM_WD * w_refs[i][...])
        loss_ref[...] = total(SMALL_LOSS, LANE)

    vm = pl.BlockSpec(memory_space=pltpu.VMEM)
    shapes = [jax.ShapeDtypeStruct(w.shape, f32) for w in ws]
    res = _pcall(body, name="adamw_small", out_shape=shapes * 4 + [jax.ShapeDtypeStruct((1, LANE), f32)],
                 in_specs=[vm] * (1 + 3 * np_), out_specs=[vm] * (4 * np_ + 1))(parts, *ws, *ms, *vs)
    return [res[i * np_:(i + 1) * np_] for i in range(4)], res[4 * np_]


def _adamw_w_in(parts, w3, m3, v3):
    n_parts, n, _ = parts.shape
    c1 = 1.0 - ADAM_B1 ** ADAM_STEP
    c2 = 1.0 - ADAM_B2 ** ADAM_STEP

    def body(p_ref, w_ref, m_ref, v_ref, g_ref, d_ref, nm_ref, nv_ref):
        g = p_ref[0].astype(f32)
        for s in range(1, n_parts):
            g = g + p_ref[s].astype(f32)
        nm = ADAM_B1 * m_ref[:, 0, :] + (1.0 - ADAM_B1) * g
        nv = ADAM_B2 * v_ref[:, 0, :] + (1.0 - ADAM_B2) * (g * g)
        g_ref[:, 0, :] = g
        nm_ref[:, 0, :] = nm
        nv_ref[:, 0, :] = nv
        d_ref[:, 0, :] = -ADAM_LR * ((nm / c1) / (jnp.sqrt(nv / c2) + ADAM_EPS) + ADAM_WD * w_ref[:, 0, :])

    tile = 2 * COL_TILE
    blk = pl.BlockSpec((n, 1, tile), lambda j: (0, 0, j))
    out = jax.ShapeDtypeStruct((n, 1, D_MODEL), f32)
    return _pcall(
        body, name="adamw_w_in", grid=(D_MODEL // tile,),
        in_specs=[pl.BlockSpec((n_parts, n, tile), lambda j: (0, 0, j)), blk, blk, blk],
        out_specs=[blk] * 4, out_shape=[out] * 4,
        compiler_params=_cparams("parallel"),
    )(parts, w3, m3, v3)


def _pad_lanes(vec8, start):
    return jnp.pad(vec8.reshape(1, -1), ((0, 0), (start, LANE - start - vec8.size)))


def kernel(x, norm_in_w, w_in, conv_qkv_w, A_log, dt_bias, gdn_norm_w, conv_w, conv_b, w_out, final_norm_w, loss_target, m_norm_in_w, m_w_in, m_conv_qkv_w, m_A_log, m_dt_bias, m_gdn_norm_w, m_conv_w, m_conv_b, m_w_out, m_final_norm_w, v_norm_in_w, v_w_in, v_conv_qkv_w, v_A_log, v_dt_bias, v_gdn_norm_w, v_conv_w, v_conv_b, v_w_out, v_final_norm_w):
    L = x.shape[1]
    nc = L // CHUNK
    xs = x[0]
    tgt = loss_target[0]
    fnw = final_norm_w.reshape(1, D_MODEL)

    as_rows = lambda a: jnp.transpose(a, (2, 0, 1))
    win_g, cqkv_g, cw_g = _all_gather([_cast_w_in(as_rows(w_in)), conv_qkv_w[0], conv_w[0]], "gather_weights",
                                      pieces=[4, 1, 1])
    wpad = _relayout_w_in(win_g)
    cqkv = jnp.concatenate([cqkv_g[d] for d in range(N_DEV)], axis=1)
    cw = jnp.concatenate([cw_g[d] for d in range(N_DEV)], axis=1)
    alog_p = _pad_lanes(A_log, HEADS)
    dtb_p = _pad_lanes(dt_bias, HEADS)
    me_flat, me_chip = _flat(*_mesh_pos()), 2 * lax.axis_index("x") + lax.axis_index("y")
    tok = lambda started: started[4]
    wo_started = _spread_start(w_out[0].astype(bf16), wpad, "gather", "gather_w_out_start")

    proj, h = _in_proj(xs, norm_in_w, wpad, tok(wo_started))
    qkv = _qkv_act(proj, cqkv)
    sc, gr = _scalars(proj, alog_p, dtb_p)
    o, u_all, w_all, vn_all, t_all, sp_all = _gdn_fwd(qkv, sc, gr)
    mix = _conv_fwd(proj, cw, conv_b, _gdn_gate(o, proj, gdn_norm_w))
    wo = _spread_finish(wo_started, mix, "gather", "gather_w_out_wait", me_flat).reshape(-1, D_MODEL)
    dy, dyb, dmix, g_fnw, loss_v = _out_proj_loss(xs, mix, wo, fnw, tgt)

    g_wout = _tn_matmul(mix, dyb, "grad_w_out")
    gwo_started = _spread_start(g_wout.reshape(N_DEV, -1, D_MODEL), dyb, "scatter", "exchange_grad_w_out_start")
    do, dproj, g_gnw = _gdn_gate_bwd(o, proj, gdn_norm_w, dmix, tok(gwo_started))
    dproj, g_cw, g_cb = _conv_bwd(proj, cw, conv_b, dmix, dproj)
    dqkv_n, dsc, dgr = _gdn_bwd(qkv, sc, gr, u_all, w_all, vn_all, t_all, sp_all, do)
    dproj, g_cqkv = _qkv_bwd(proj, cqkv, dqkv_n, dproj)
    dgr_col = jnp.pad(dgr.transpose(0, 2, 1).reshape(L, HEADS), ((0, 0), (HEADS, LANE - 2 * HEADS)))
    dproj, g_sc = _scalars_bwd(proj, alog_p, dtb_p, dsc, dgr_col, dproj)
    g_win_blk = _grad_blocks(_tn_matmul(dproj, h, "grad_w_in"))

    (p_win,) = _pair_exchange([g_win_blk], "exchange_grads_pair")
    s_win = _pair_sum(g_win_blk, p_win, "pair_sum_w_in")
    r_cqkv, r_cw = _all_to_all(
        [g_cqkv.reshape(4, N_DEV, -1).transpose(1, 0, 2), g_cw.reshape(3, N_DEV, -1).transpose(1, 0, 2)],
        "exchange_small_sharded_grads")
    gwi_started = _spread_start(s_win, r_cw, "chips", "exchange_grads_chips_start")
    grad_x, g_nw = _input_grad(dproj, wpad, xs, norm_in_w, dy, tok(gwi_started))

    r_wout = _spread_finish(gwo_started, grad_x, "scatter", "exchange_grad_w_out_wait", me_flat)
    upd_wout =_adamw_reduce(r_wout, w_out[0], m_w_out[0], v_w_out[0], "adamw_w_out")
    upd_cqkv = _adamw_reduce(r_cqkv, conv_qkv_w[0], m_conv_qkv_w[0], v_conv_qkv_w[0], "adamw_conv_qkv_w")
    upd_cw = _adamw_reduce(r_cw, conv_w[0], m_conv_w[0], v_conv_w[0], "adamw_conv_w")

    r_win = _spread_finish(gwi_started, upd_cw[0], "chips", "exchange_grads_chips_wait", me_chip)
    upd_win = [jnp.transpose(a, (1, 2, 0)) for a in _adamw_w_in(r_win, as_rows(w_in), as_rows(m_w_in), as_rows(v_w_in))]

    small_g = _pack_small([g_nw, g_cb, g_fnw, g_gnw, g_sc, loss_v], r_win)
    (small_all,) = _all_gather([small_g], "gather_small_grads")
    fvec = lambda a: a.reshape(1, D_MODEL)
    upd_small, loss_sum = _adamw_small(
        small_all,
        [norm_in_w, conv_b, fvec(final_norm_w), gdn_norm_w, A_log, dt_bias],
        [m_norm_in_w, m_conv_b, fvec(m_final_norm_w), m_gdn_norm_w, m_A_log, m_dt_bias],
        [v_norm_in_w, v_conv_b, fvec(v_final_norm_w), v_gdn_norm_w, v_A_log, v_dt_bias])

    outs = [loss_sum[0, 0], grad_x[None]]
    for k in range(4):
        nw_k, cb_k, fw_k, gn_k, al_k, dt_k = upd_small[k]
        outs += [nw_k, upd_win[k], upd_cqkv[k][None], al_k, dt_k, gn_k,
                 upd_cw[k][None], cb_k, upd_wout[k][None], fw_k.reshape(D_MODEL)]
    return tuple(outs)
```

```python
import jax
import jax.numpy as jnp
from jax import lax
from jax.experimental import pallas as pl
from jax.experimental.pallas import tpu as pltpu

f32 = jnp.float32
bf16 = jnp.bfloat16

N_DEV = 8
D_MODEL = 1024
HEADS = 8
HEAD_DIM = 128
CHUNK = 64
GDN_CPS = 4
GDN_CPS_BWD = 1
GDN_WIDTH = HEADS * HEAD_DIM
CONV_WIDTH = 1024
PROJ_WIDTH = 8208
SHARD_W = PROJ_WIDTH // N_DEV
EPS = 1e-6

LANE = 128
ELT_W = 256

OFF_QKV, OFF_ZG, OFF_CONV, OFF_BA = 0, 3072, 4096, 8192
CONV_BLOCK = 4 * ELT_W
PROJ_PAD = 8448
NAT_BA, NAT_CONV = 4096, 4112


def _padded_col(n):
    if n < NAT_BA:
        return n
    if n < NAT_CONV:
        return OFF_BA + n - NAT_BA
    g, ch = divmod(n - NAT_CONV, CONV_WIDTH)
    j, r = divmod(ch, ELT_W)
    return OFF_CONV + CONV_BLOCK * j + ELT_W * g + r


def _layout_segments(n0, n1):
    cuts = [NAT_BA, NAT_CONV] + [NAT_CONV + ELT_W * k for k in range(1, 4 * CONV_WIDTH // ELT_W)]
    pts = [n0] + [c for c in cuts if n0 < c < n1] + [n1]
    return [(lo, hi - lo, _padded_col(lo)) for lo, hi in zip(pts, pts[1:])]

ADAM_LR, ADAM_B1, ADAM_B2, ADAM_EPS, ADAM_WD, ADAM_STEP = 0.001, 0.9, 0.999, 1e-08, 0.01, 10

V7X_VMEM_BYTES = 64 * 1024 * 1024
VMEM_LIMIT = V7X_VMEM_BYTES - 8 * 1024 * 1024

MESH = pl.DeviceIdType.MESH
ANY = pl.BlockSpec(memory_space=pl.ANY)


def _pcall(body, **kw):
    return pl.pallas_call(body, **kw)


def _cparams(*sem):
    return pltpu.CompilerParams(dimension_semantics=sem if sem else None, vmem_limit_bytes=VMEM_LIMIT)


def _mm(a, b):
    return jnp.dot(a.astype(bf16), b.astype(bf16), preferred_element_type=f32)


def _mm_nt(a, b):
    return lax.dot_general(a.astype(bf16), b.astype(bf16), (((1,), (1,)), ((), ())), preferred_element_type=f32)


def _cat16(parts, axis):
    return jnp.concatenate([p.astype(bf16) for p in parts], axis=axis)


def _mm_tn(a, b):
    return lax.dot_general(a.astype(bf16), b.astype(bf16), (((0,), (0,)), ((), ())), preferred_element_type=f32)


def _rows(shape):
    return lax.broadcasted_iota(jnp.int32, shape, 0)


def _lanes(shape):
    return lax.broadcasted_iota(jnp.int32, shape, 1)


def _shift_down(x, s):
    if s == 0:
        return x
    return jnp.where(_rows(x.shape) >= s, pltpu.roll(x, s, 0), 0.0)


def _shift_up(x, s):
    if s == 0:
        return x
    n = x.shape[0]
    return jnp.where(_rows(x.shape) < n - s, pltpu.roll(x, n - s, 0), 0.0)


def _sigmoid(x):
    return jax.nn.sigmoid(x)


def _softplus(x):
    e = jnp.exp(-jnp.abs(x))
    small = e * (1.0 - e * (0.5 - e * (1.0 / 3.0)))
    return jnp.maximum(x, 0.0) + jnp.where(e < 0.01, small, jnp.log(1.0 + e))


def _mesh_pos():
    return lax.axis_index("x"), lax.axis_index("y"), lax.axis_index("c")


def _flat(px, py, pc):
    return 4 * px + 2 * py + pc


def _all_gather(xs, name, pieces=None):
    n = len(xs)
    pieces = pieces or [1] * n
    items = [(a, q) for a in range(n) for q in range(pieces[a])]
    ni = len(items)

    def view(ref, i):
        a, q = items[i]
        if pieces[a] == 1:
            return ref
        wd = xs[a].shape[-1] // pieces[a]
        return ref.at[(slice(None),) * (xs[a].ndim - 1) + (pl.ds(q * wd, wd),)]

    def body(*refs):
        x_refs, o_refs = refs[:n], refs[n:2 * n]
        send_sems, recv_sems, local_sems = refs[2 * n:]
        x, y, c = _mesh_pos()
        me, sibling = (x, y, c), (x, y, 1 - c)
        flip = lambda v, bit: v + bit - 2 * v * bit
        nbr_a = (flip(x, 1 - c), flip(y, c))
        nbr_b = (flip(x, c), flip(y, 1 - c))
        diag = (1 - x, 1 - y)

        def copy(i, k, block, to, own=False):
            a = items[i][0]
            dst = view(o_refs[a].at[_flat(*block)], i)
            return pltpu.make_async_remote_copy(
                src_ref=view(x_refs[a], i) if own else dst, dst_ref=dst,
                send_sem=send_sems.at[i, k], recv_sem=recv_sems.at[i, k], device_id=to, device_id_type=MESH)

        mine, sent = [], []

        def go(cp):
            cp.start()
            sent.append(cp)

        for a in range(n):
            cp = pltpu.make_async_copy(x_refs[a], o_refs[a].at[_flat(*me)], local_sems.at[a])
            cp.start()
            mine.append(cp)
        for a in range(ni):
            go(copy(a, 1, me, (*nbr_a, c), own=True))
            go(copy(a, 2, me, (*nbr_b, c), own=True))
            go(copy(a, 0, me, sibling, own=True))
        for a in range(ni):
            copy(a, 1, (*nbr_a, c), me).wait_recv()
            go(copy(a, 3, (*nbr_a, c), (*nbr_b, c)))
            go(copy(a, 4, (*nbr_a, c), sibling))
        for a in range(ni):
            copy(a, 2, (*nbr_b, c), me).wait_recv()
            go(copy(a, 5, (*nbr_b, c), sibling))
        for a in range(ni):
            copy(a, 3, (*diag, c), me).wait_recv()
            go(copy(a, 6, (*diag, c), sibling))
        for a in range(ni):
            copy(a, 0, sibling, me).wait_recv()
            copy(a, 4, (*nbr_b, 1 - c), me).wait_recv()
            copy(a, 5, (*nbr_a, 1 - c), me).wait_recv()
            copy(a, 6, (*diag, 1 - c), me).wait_recv()
        for cp in sent:
            cp.wait_send()
        for cp in mine:
            cp.wait()

    outs = _pcall(
        body, name=name,
        out_shape=[jax.ShapeDtypeStruct((N_DEV,) + a.shape, a.dtype) for a in xs],
        in_specs=[ANY] * n, out_specs=[ANY] * n,
        scratch_shapes=[pltpu.SemaphoreType.DMA((ni, 7)), pltpu.SemaphoreType.DMA((ni, 7)), pltpu.SemaphoreType.DMA((n,))],
    )(*xs)
    return list(outs)


def _all_to_all(gs, name):
    n = len(gs)

    def body(*refs):
        g_refs, o_refs = refs[:n], refs[n:2 * n]
        send_sems, recv_sems, local_sems = refs[2 * n:]
        x, y, c = _mesh_pos()
        me = _flat(x, y, c)
        peers = []
        for k in range(1, N_DEV):
            kx, ky, kc = (k >> 2) & 1, (k >> 1) & 1, k & 1
            px = (1 - x) if kx else x
            py = (1 - y) if ky else y
            pc = (1 - c) if kc else c
            peers.append((px, py, pc))

        def copy(a, k):
            peer = peers[k - 1]
            return pltpu.make_async_remote_copy(
                src_ref=g_refs[a].at[_flat(*peer)], dst_ref=o_refs[a].at[me],
                send_sem=send_sems.at[a, k - 1], recv_sem=recv_sems.at[a, k - 1], device_id=peer, device_id_type=MESH)

        def arrival(a, k):
            peer = peers[k - 1]
            return pltpu.make_async_remote_copy(
                src_ref=g_refs[a].at[me], dst_ref=o_refs[a].at[_flat(*peer)],
                send_sem=send_sems.at[a, k - 1], recv_sem=recv_sems.at[a, k - 1], device_id=peer, device_id_type=MESH)

        mine, sent = [], []
        for a in range(n):
            cp = pltpu.make_async_copy(g_refs[a].at[me], o_refs[a].at[me], local_sems.at[a])
            cp.start()
            mine.append(cp)
            for k in range(1, N_DEV):
                cp = copy(a, k)
                cp.start()
                sent.append(cp)
        for a in range(n):
            for k in range(1, N_DEV):
                arrival(a, k).wait_recv()
        for cp in sent:
            cp.wait_send()
        for cp in mine:
            cp.wait()

    outs = _pcall(
        body, name=name,
        out_shape=[jax.ShapeDtypeStruct(a.shape, a.dtype) for a in gs],
        in_specs=[ANY] * n, out_specs=[ANY] * n,
        scratch_shapes=[pltpu.SemaphoreType.DMA((n, 7)), pltpu.SemaphoreType.DMA((n, 7)), pltpu.SemaphoreType.DMA((n,))],
    )(*gs)
    return list(outs)


def _pair_exchange(gs, name):
    n = len(gs)
    chips = [(0, 0), (0, 1), (1, 0), (1, 1)]

    def body(*refs):
        g_refs, o_refs = refs[:n], refs[n:2 * n]
        send_sems, recv_sems = refs[2 * n:]
        x, y, c = _mesh_pos()
        sibling = (x, y, 1 - c)

        def copy(a, i):
            xp, yp = chips[i]
            return pltpu.make_async_remote_copy(
                src_ref=g_refs[a].at[_flat(xp, yp, 1 - c)], dst_ref=o_refs[a].at[i],
                send_sem=send_sems.at[a, i], recv_sem=recv_sems.at[a, i], device_id=sibling, device_id_type=MESH)

        cps = [copy(a, i) for a in range(n) for i in range(4)]
        for cp in cps:
            cp.start()
        for cp in cps:
            cp.wait()

    outs = _pcall(
        body, name=name,
        out_shape=[jax.ShapeDtypeStruct((4,) + a.shape[1:], a.dtype) for a in gs],
        in_specs=[ANY] * n, out_specs=[ANY] * n,
        scratch_shapes=[pltpu.SemaphoreType.DMA((n, 4)), pltpu.SemaphoreType.DMA((n, 4))],
    )(*gs)
    return list(outs)


def _pair_sum(g, p1, name):
    _, R, C = g.shape
    tr = 256 if R % 256 == 0 else R
    cidx = lax.axis_index("c").astype(jnp.int32).reshape(1)

    def body(c_ref, g_ref, p_ref, o_ref):
        o_ref[...] = (g_ref[...].astype(f32) + p_ref[...].astype(f32)).astype(o_ref.dtype)

    return _pcall(
        body, name=name,
        grid_spec=pltpu.PrefetchScalarGridSpec(
            num_scalar_prefetch=1, grid=(4, R // tr),
            in_specs=[pl.BlockSpec((1, tr, C), lambda i, r, c_ref: (2 * i + c_ref[0], r, 0)),
                      pl.BlockSpec((1, tr, C), lambda i, r, c_ref: (i, r, 0))],
            out_specs=pl.BlockSpec((1, tr, C), lambda i, r, c_ref: (i, r, 0))),
        out_shape=jax.ShapeDtypeStruct((4, R, C), g.dtype),
        compiler_params=_cparams("parallel", "parallel"),
    )(cidx, g, p1)


def _axis_sum(s, got, name):
    _, R, C = s.shape
    x, y, c = _mesh_pos()
    me, _, b, _ = _axis_chips(x, y, c)
    idx = jnp.stack([2 * me[0] + me[1], 2 * b[0] + b[1]]).astype(jnp.int32)

    def body(idx_ref, s_ref, g_ref, o_ref):
        o_ref[...] = (s_ref[...].astype(f32) + g_ref[...].astype(f32)).astype(o_ref.dtype)

    return _pcall(
        body, name=name,
        grid_spec=pltpu.PrefetchScalarGridSpec(
            num_scalar_prefetch=1, grid=(2,),
            in_specs=[pl.BlockSpec((1, R, C), lambda k, idx_ref: (idx_ref[k], 0, 0)),
                      pl.BlockSpec((1, R, C), lambda k, idx_ref: (k, 0, 0))],
            out_specs=pl.BlockSpec((1, R, C), lambda k, idx_ref: (k, 0, 0))),
        out_shape=jax.ShapeDtypeStruct((2, R, C), s.dtype),
        compiler_params=_cparams("parallel"),
    )(idx, s, got)


HBM = pl.BlockSpec(memory_space=pltpu.HBM)
SEM = pl.BlockSpec(memory_space=pltpu.SEMAPHORE)
EFFECT = pltpu.SideEffectType.DATAFLOW_SIDE_EFFECTING


def _peers(x, y, c):
    out = []
    for k in range(1, N_DEV):
        kx, ky, kc = (k >> 2) & 1, (k >> 1) & 1, k & 1
        out.append(((1 - x) if kx else x, (1 - y) if ky else y, (1 - c) if kc else c))
    return out


SPREAD_COPIES = {"gather": N_DEV - 1, "scatter": N_DEV - 1, "axis_a": 2, "axis_b": 1}
SPREAD_SLOTS = {"axis_a": 2, "axis_b": 1}


def _axis_chips(x, y, c):
    flip = lambda v, bit: v + bit - 2 * v * bit
    return (x, y), (flip(x, 1 - c), flip(y, c)), (flip(x, c), flip(y, 1 - c)), (1 - x, 1 - y)


def _spread_copy(src_ref, land_ref, send_sems, recv_sems, k, plan):
    x, y, c = _mesh_pos()
    if plan in ("axis_a", "axis_b"):
        _, a, b, d = _axis_chips(x, y, c)
        chip = lambda p: 2 * p[0] + p[1]
        peer = (*(a if plan == "axis_a" else b), c)
        src = src_ref.at[chip(a) if k == 0 else chip(d)] if plan == "axis_a" else src_ref.at[1]
        slot = k
    else:
        peer = _peers(x, y, c)[k]
        src, slot = (src_ref.at[_flat(*peer)] if plan == "scatter" else src_ref), _flat(x, y, c)
    return pltpu.make_async_remote_copy(
        src_ref=src, dst_ref=land_ref.at[slot], send_sem=send_sems.at[k], recv_sem=recv_sems.at[k],
        device_id=peer, device_id_type=MESH)


def _spread_start(src, after, plan, name):
    land_shape = (N_DEV,) + src.shape if plan == "gather" else src.shape
    if plan in SPREAD_SLOTS:
        land_shape = (SPREAD_SLOTS[plan],) + src.shape[1:]
    n_copies = SPREAD_COPIES[plan]

    def body(src_ref, land_ref, after_ref, send_sems, recv_sems, src_thru, land_thru, token):
        for k in range(n_copies):
            _spread_copy(src_ref, land_ref, send_sems, recv_sems, k, plan).start()
        token[...] = jnp.zeros_like(token)

    return _pcall(
        body, name=name,
        out_shape=(pltpu.SemaphoreType.DMA((n_copies,)), pltpu.SemaphoreType.DMA((n_copies,)),
                   pltpu.HBM(src.shape, src.dtype), pltpu.HBM(land_shape, src.dtype), jax.ShapeDtypeStruct((8, LANE), f32)),
        in_specs=(HBM, HBM, ANY), out_specs=(SEM, SEM, HBM, HBM, pl.BlockSpec(memory_space=pltpu.VMEM)),
        input_output_aliases={0: 2, 1: 3},
        compiler_params=pltpu.CompilerParams(has_side_effects=EFFECT),
    )(pltpu.with_memory_space_constraint(src, pltpu.HBM),
      pltpu.with_memory_space_constraint(lax.empty(land_shape, src.dtype), pltpu.HBM), after)


def _spread_wait(started, after, plan, name):
    send_sems, recv_sems, src_thru, land_thru, _ = started

    def body(src_ref, land_ref, send_sems, recv_sems, after_ref, src_dead, got_ref):
        for k in range(SPREAD_COPIES[plan]):
            cp = _spread_copy(src_ref, land_ref, send_sems, recv_sems, k, plan)
            cp.wait_send()
            cp.wait_recv()

    return _pcall(
        body, name=name,
        out_shape=(pltpu.HBM(src_thru.shape, src_thru.dtype), pltpu.HBM(land_thru.shape, land_thru.dtype)),
        in_specs=(HBM, HBM, SEM, SEM, ANY), out_specs=(HBM, HBM), input_output_aliases={0: 0, 1: 1},
        compiler_params=pltpu.CompilerParams(has_side_effects=EFFECT),
    )(src_thru, land_thru, send_sems, recv_sems, after)


def _spread_finish(started, after, plan, name, slot):
    src, land = _spread_wait(started, after, plan, name)
    block = src if plan == "gather" else lax.dynamic_index_in_dim(src, slot, 0, keepdims=False)
    return _own_slot(land, block, slot)


def _own_slot(land, block, slot):
    zero = jnp.zeros((), jnp.int32)
    return lax.dynamic_update_slice(land, block[None], (slot.astype(jnp.int32),) + (zero,) * block.ndim)


COL_TILE = 256


def _cast_w_in(w3):
    n = w3.shape[0]

    def body(w_ref, o_ref):
        o_ref[...] = w_ref[:, 0, :].astype(bf16)

    tile = 2 * COL_TILE
    return _pcall(
        body, name="cast_w_in", grid=(D_MODEL // tile,),
        in_specs=[pl.BlockSpec((n, 1, tile), lambda j: (0, 0, j))],
        out_specs=pl.BlockSpec((n, tile), lambda j: (0, j)),
        out_shape=jax.ShapeDtypeStruct((n, D_MODEL), bf16),
        compiler_params=_cparams("parallel"),
    )(w3)


def _relayout_w_in(win_g):
    def body(g_ref, o_ref):
        used = OFF_BA + NAT_CONV - NAT_BA
        o_ref[used:PROJ_PAD, :] = jnp.zeros((PROJ_PAD - used, COL_TILE), o_ref.dtype)
        for d in range(N_DEV):
            for lo, width, dst in _layout_segments(d * SHARD_W, (d + 1) * SHARD_W):
                src = lo - d * SHARD_W
                o_ref[dst:dst + width, :] = g_ref[d, src:src + width, :]

    return _pcall(
        body, name="relayout_w_in", grid=(D_MODEL // COL_TILE,),
        in_specs=[pl.BlockSpec((N_DEV, SHARD_W, COL_TILE), lambda j: (0, 0, j))],
        out_specs=pl.BlockSpec((PROJ_PAD, COL_TILE), lambda j: (0, j)),
        out_shape=jax.ShapeDtypeStruct((PROJ_PAD, D_MODEL), win_g.dtype),
        compiler_params=_cparams("parallel"),
    )(win_g)


def _grad_blocks(g_t):
    def body(p_ref, o_ref):
        for d in range(N_DEV):
            for lo, width, src in _layout_segments(d * SHARD_W, (d + 1) * SHARD_W):
                dst = lo - d * SHARD_W
                o_ref[d, dst:dst + width, :] = p_ref[src:src + width, :]

    return _pcall(
        body, name="grad_blocks", grid=(D_MODEL // COL_TILE,),
        in_specs=[pl.BlockSpec((PROJ_PAD, COL_TILE), lambda j: (0, j))],
        out_specs=pl.BlockSpec((N_DEV, SHARD_W, COL_TILE), lambda j: (0, 0, j)),
        out_shape=jax.ShapeDtypeStruct((N_DEV, SHARD_W, D_MODEL), bf16),
        compiler_params=_cparams("parallel"),
    )(g_t)


def _in_proj(x, nw, wpad_t, after):
    L = x.shape[0]
    tn = 768
    nj = wpad_t.shape[0] // tn

    def body(x_ref, nw_ref, w_ref, after_ref, proj_ref, h_ref):
        @pl.when(pl.program_id(0) == 0)
        def _():
            for r in range(0, L, 256):
                xs = x_ref[r:r + 256, :]
                ms = jnp.mean(xs * xs, axis=-1, keepdims=True)
                h_ref[r:r + 256, :] = ((xs * lax.rsqrt(ms + EPS)) * nw_ref[...]).astype(bf16)
        for r in range(0, L, 512):
            proj_ref[r:r + 512, :] = lax.dot_general(h_ref[r:r + 512, :], w_ref[...], (((1,), (1,)), ((), ())),
                                                     preferred_element_type=f32)

    return _pcall(
        body, name="in_proj", grid=(nj,),
        in_specs=[pl.BlockSpec((L, D_MODEL), lambda j: (0, 0)), pl.BlockSpec((1, D_MODEL), lambda j: (0, 0)),
                  pl.BlockSpec((tn, D_MODEL), lambda j: (j, 0)), ANY],
        out_specs=[pl.BlockSpec((L, tn), lambda j: (0, j)), pl.BlockSpec((L, D_MODEL), lambda j: (0, 0))],
        out_shape=[jax.ShapeDtypeStruct((L, wpad_t.shape[0]), f32), jax.ShapeDtypeStruct((L, D_MODEL), bf16)],
        compiler_params=_cparams("arbitrary"),
    )(x, nw, wpad_t, after)


HALVES = [slice(i * LANE, (i + 1) * LANE) for i in range(ELT_W // LANE)]
QKV_W = 512
QKV_HEADS = [slice(i * LANE, (i + 1) * LANE) for i in range(QKV_W // LANE)]
STEPS_PER_GROUP = GDN_WIDTH // QKV_W


def _conv4(x, cw_ref, ls):
    return (cw_ref[3:4, ls] * x + cw_ref[2:3, ls] * _shift_down(x, 1) + cw_ref[1:2, ls] * _shift_down(x, 2)
            + cw_ref[0:1, ls] * _shift_down(x, 3))


def _qkv_act(proj, cw):
    L = proj.shape[0]

    def body(x_ref, cw_ref, o_ref):
        j = pl.program_id(0)
        scale = jnp.where(j < STEPS_PER_GROUP, HEAD_DIM ** -0.5, 1.0).astype(f32)
        for ls in QKV_HEADS:
            c = _conv4(x_ref[:, ls], cw_ref, ls)
            a = c * _sigmoid(c)
            rn = lax.rsqrt(jnp.sum(a * a, axis=1, keepdims=True) + EPS)
            o_ref[:, ls] = jnp.where(j < 2 * STEPS_PER_GROUP, (a * rn) * scale, a)

    return _pcall(
        body, name="qkv_act", grid=(3 * STEPS_PER_GROUP,),
        in_specs=[pl.BlockSpec((L, QKV_W), lambda j: (0, j)), pl.BlockSpec((4, QKV_W), lambda j: (0, j))],
        out_specs=pl.BlockSpec((L, QKV_W), lambda j: (0, j)),
        out_shape=jax.ShapeDtypeStruct((L, 3 * GDN_WIDTH), f32),
        compiler_params=_cparams("parallel"),
    )(proj, cw)


def _scalars(proj, alog_p, dtb_p):
    L = proj.shape[0]
    nc = L // CHUNK

    def body(x_ref, al_ref, dt_ref, sc_ref, gr_ref):
        x = x_ref[...]
        lane = _lanes(x.shape)
        beta = _sigmoid(x)
        g = -jnp.exp(al_ref[...]) * _softplus(x + dt_ref[...])
        gc = jnp.where((lane >= HEADS) & (lane < 2 * HEADS), g, 0.0)
        rc = _rows(x.shape) & (CHUNK - 1)
        for s in (1, 2, 4, 8, 16, 32):
            gc = gc + jnp.where(rc >= s, pltpu.roll(gc, s, 0), 0.0)
        sc_ref[...] = jnp.where(lane < HEADS, beta, gc)
        sel = (_lanes((HEADS, LANE)) == _rows((HEADS, LANE)) + HEADS).astype(f32)
        for c in range(nc):
            gr_ref[c] = lax.dot_general(sel, sc_ref[c * CHUNK:(c + 1) * CHUNK, :], (((1,), (1,)), ((), ())),
                                        preferred_element_type=f32, precision=lax.Precision.HIGHEST)

    return _pcall(
        body, name="scalars", grid=(1,),
        in_specs=[pl.BlockSpec((L, LANE), lambda i: (0, OFF_BA // LANE)), pl.BlockSpec((1, LANE), lambda i: (0, 0)),
                  pl.BlockSpec((1, LANE), lambda i: (0, 0))],
        out_specs=[pl.BlockSpec((L, LANE), lambda i: (0, 0)), pl.BlockSpec((nc, HEADS, CHUNK), lambda i: (0, 0, 0))],
        out_shape=[jax.ShapeDtypeStruct((L, LANE), f32), jax.ShapeDtypeStruct((nc, HEADS, CHUNK), f32)],
        compiler_params=_cparams("arbitrary"),
    )(proj, alog_p, dtb_p)


def _head_scalars(sc, gr_ref, h, ci=0):
    lane = _lanes(sc.shape)
    beta = jnp.sum(jnp.where(lane == h, sc, 0.0), axis=1, keepdims=True)
    gcc = jnp.sum(jnp.where(lane == HEADS + h, sc, 0.0), axis=1, keepdims=True)
    gcr = gr_ref[ci, h:h + 1, :]
    gl = jnp.sum(jnp.where(_lanes(gcr.shape) == CHUNK - 1, gcr, 0.0), axis=1, keepdims=True)
    ii, jj = _rows((CHUNK, CHUNK)), _lanes((CHUNK, CHUNK))
    dmat = jnp.where(ii >= jj, jnp.exp(jnp.minimum(gcc - gcr, 0.0)), 0.0)
    dmat_t = jnp.where(jj >= ii, jnp.exp(jnp.minimum(gcr - gcc, 0.0)), 0.0)
    return beta, gcc, gl, dmat, dmat_t, ii, jj


def _gdn_fwd(qkv, sc, gr):
    L = qkv.shape[0]
    nc = L // CHUNK
    W = GDN_WIDTH
    cps = GDN_CPS if nc % GDN_CPS == 0 else 1
    rows_per_step = cps * CHUNK

    def body(qkv_ref, sc_ref, gr_ref, o_ref, u_ref, w_ref, vn_ref, t_ref, sp_ref, s_scr):
        @pl.when(pl.program_id(0) == 0)
        def _():
            s_scr[...] = jnp.zeros_like(s_scr)
        HS = range(cps * HEADS)
        hd = [i % HEADS for i in HS]
        rs = [slice((i // HEADS) * CHUNK, (i // HEADS + 1) * CHUNK) for i in HS]
        cs = [slice(hd[i] * HEAD_DIM, (hd[i] + 1) * HEAD_DIM) for i in HS]
        q = [qkv_ref[rs[i], hd[i] * HEAD_DIM:(hd[i] + 1) * HEAD_DIM] for i in HS]
        k = [qkv_ref[rs[i], W + hd[i] * HEAD_DIM:W + (hd[i] + 1) * HEAD_DIM] for i in HS]
        v = [qkv_ref[rs[i], 2 * W + hd[i] * HEAD_DIM:2 * W + (hd[i] + 1) * HEAD_DIM] for i in HS]
        hsc = [_head_scalars(sc_ref[rs[i], :], gr_ref, hd[i], i // HEADS) for i in HS]
        beta, gcc, gl, dmat = ([x[i] for x in hsc] for i in range(4))
        ii, jj = hsc[0][5], hsc[0][6]
        eg = [jnp.exp(gcc[h]) for h in HS]
        kb = [k[h] * beta[h] for h in HS]
        kk = [_mm_nt(kb[h], k[h]) for h in HS]
        qk = [_mm_nt(q[h], k[h]) for h in HS]
        n0 = [-jnp.where(ii > jj, kk[h] * dmat[h], 0.0) for h in HS]
        n1 = [_mm(n0[h], n0[h]) for h in HS]
        n2 = [_mm(n1[h], n1[h]) for h in HS]
        p01 = [n0[h] + n1[h] + _mm(n0[h], n1[h]) for h in HS]
        n3 = [_mm(n2[h], n2[h]) for h in HS]
        n4 = [_mm(n3[h], n3[h]) for h in HS]
        p23 = [n2[h] + n3[h] + _mm(n2[h], n3[h]) for h in HS]
        n5 = [_mm(n4[h], n4[h]) for h in HS]
        p03 = [p01[h] + p23[h] + _mm(p01[h], p23[h]) for h in HS]
        p45 = [n4[h] + n5[h] + _mm(n4[h], n5[h]) for h in HS]
        t = [p03[h] + p45[h] + _mm(p03[h], p45[h]) for h in HS]
        vb = [v[h] * beta[h] for h in HS]
        kbg = [kb[h] * eg[h] for h in HS]
        uw = [_mm(t[h], _cat16([vb[h], kbg[h]], 1)) for h in HS]
        u = [vb[h] + uw[h][:, :HEAD_DIM] for h in HS]
        w = [kbg[h] + uw[h][:, HEAD_DIM:] for h in HS]
        wq = [_cat16([w[h], q[h] * eg[h]], 0) for h in HS]
        p = [jnp.where(ii >= jj, qk[h] * dmat[h], 0.0) for h in HS]
        ks = [k[h] * jnp.exp(gl[h] - gcc[h]) for h in HS]
        s = [s_scr[h] for h in range(HEADS)]
        for ci in range(cps):
            IS = range(ci * HEADS, (ci + 1) * HEADS)
            ws = [_mm(wq[i], s[hd[i]]) for i in IS]
            vn = [u[i] - ws[hd[i]][:CHUNK] for i in IS]
            pv = [_mm(p[i], vn[hd[i]]) for i in IS]
            kv = [_mm_tn(ks[i], vn[hd[i]]) for i in IS]
            for i in IS:
                h = hd[i]
                sp_ref[ci, cs[i], :] = s[h]
                o_ref[rs[i], cs[i]] = ws[h][CHUNK:] + pv[h]
                vn_ref[rs[i], cs[i]] = vn[h].astype(bf16)
            s = [jnp.exp(gl[i]) * s[hd[i]] + kv[hd[i]] for i in IS]
        for h in range(HEADS):
            s_scr[h] = s[h]
        for i in HS:
            u_ref[rs[i], cs[i]] = u[i].astype(bf16)
            w_ref[rs[i], cs[i]] = w[i].astype(bf16)
            t_ref[i // HEADS, hd[i]] = t[i].astype(bf16)

    row = lambda c: (c, 0)
    act, act16 = jax.ShapeDtypeStruct((L, W), f32), jax.ShapeDtypeStruct((L, W), bf16)
    return _pcall(
        body, name="gdn_fwd", grid=(nc // cps,),
        in_specs=[pl.BlockSpec((rows_per_step, 3 * W), row), pl.BlockSpec((rows_per_step, LANE), row),
                  pl.BlockSpec((cps, HEADS, CHUNK), lambda c: (c, 0, 0))],
        out_specs=[pl.BlockSpec((rows_per_step, W), row)] * 4 + [
            pl.BlockSpec((cps, HEADS, CHUNK, CHUNK), lambda c: (c, 0, 0, 0)),
            pl.BlockSpec((cps, W, HEAD_DIM), lambda c: (c, 0, 0))],
        out_shape=[act, act16, act16, act16, jax.ShapeDtypeStruct((nc, HEADS, CHUNK, CHUNK), bf16),
                   jax.ShapeDtypeStruct((nc, W, HEAD_DIM), f32)],
        scratch_shapes=[pltpu.VMEM((HEADS, HEAD_DIM, HEAD_DIM), f32)],
        compiler_params=_cparams("arbitrary"),
    )(qkv, sc, gr)


def _gdn_gate(o, proj, gnw):
    L = o.shape[0]

    def body(o_ref, z_ref, w_ref, m_ref):
        for ls in HALVES:
            ov, z = o_ref[:, ls], z_ref[:, ls]
            rms = lax.rsqrt(jnp.mean(ov * ov, axis=-1, keepdims=True) + EPS)
            m_ref[:, ls] = (((ov * rms) * w_ref[...]) * (z * _sigmoid(z))).astype(bf16)

    return _pcall(
        body, name="gdn_gate", grid=(GDN_WIDTH // ELT_W,),
        in_specs=[pl.BlockSpec((L, ELT_W), lambda j: (0, j)), pl.BlockSpec((L, ELT_W), lambda j: (0, OFF_ZG // ELT_W + j)),
                  pl.BlockSpec((1, LANE), lambda j: (0, 0))],
        out_specs=pl.BlockSpec((L, ELT_W), lambda j: (0, j)),
        out_shape=jax.ShapeDtypeStruct((L, GDN_WIDTH + CONV_WIDTH), bf16),
        compiler_params=_cparams("parallel"),
    )(o, proj, gnw)


def _conv3(u, cw_ref, ls):
    return cw_ref[2:3, ls] * u + cw_ref[1:2, ls] * _shift_down(u, 1) + cw_ref[0:1, ls] * _shift_down(u, 2)


def _conv_specs(L):
    return [pl.BlockSpec((L, CONV_BLOCK), lambda j: (0, OFF_CONV // CONV_BLOCK + j)),
            pl.BlockSpec((3, ELT_W), lambda j: (0, j)), pl.BlockSpec((1, ELT_W), lambda j: (0, j))]


def _conv_parts(ls):
    return [slice(g * ELT_W + ls.start, g * ELT_W + ls.stop) for g in range(4)]


def _conv_fwd(proj, cw, cb, mix):
    L = proj.shape[0]

    def body(p_ref, cw_ref, cb_ref, mix_in, m_ref):
        for ls in HALVES:
            sb, sc_, sh, sz = _conv_parts(ls)
            z = p_ref[:, sz]
            cv = _conv3(p_ref[:, sc_] * p_ref[:, sh], cw_ref, ls) + cb_ref[:, ls]
            m_ref[:, ls] = ((p_ref[:, sb] * cv) * (z * _sigmoid(z))).astype(bf16)

    return _pcall(
        body, name="conv_fwd", grid=(CONV_WIDTH // ELT_W,),
        in_specs=_conv_specs(L) + [ANY], out_specs=pl.BlockSpec((L, ELT_W), lambda j: (0, GDN_WIDTH // ELT_W + j)),
        out_shape=jax.ShapeDtypeStruct(mix.shape, mix.dtype), input_output_aliases={3: 0},
        compiler_params=_cparams("parallel"),
    )(proj, cw, cb, mix)


def _out_proj_loss(x, mix, wo, fw, tgt):
    L = x.shape[0]
    tm = min(512, L)
    MW = GDN_WIDTH + CONV_WIDTH

    def body(x_ref, m_ref, wo_ref, fw_ref, t_ref, dy_ref, dyb_ref, dm_ref, gfw_ref, loss_ref):
        @pl.when(pl.program_id(0) == 0)
        def _():
            gfw_ref[...] = jnp.zeros_like(gfw_ref)
            loss_ref[...] = jnp.zeros_like(loss_ref)
        y = x_ref[...] + jnp.dot(m_ref[...], wo_ref[...], preferred_element_type=f32)
        r = lax.rsqrt(jnp.mean(y * y, axis=-1, keepdims=True) + EPS)
        yh = y * r
        fwv = fw_ref[...]
        diff = yh * fwv - t_ref[...]
        loss_ref[...] += jnp.sum(jnp.sum(diff * diff, axis=-1, keepdims=True), axis=0, keepdims=True) * (0.5 / D_MODEL)
        dout = diff * (1.0 / D_MODEL)
        gfw_ref[...] += jnp.sum(dout * yh, axis=0, keepdims=True)
        dyh = dout * fwv
        dy = r * (dyh - yh * jnp.mean(dyh * yh, axis=-1, keepdims=True))
        dy_ref[...] = dy
        dyb = dy.astype(bf16)
        dyb_ref[...] = dyb
        dm_ref[...] = lax.dot_general(dyb, wo_ref[...], (((1,), (1,)), ((), ())), preferred_element_type=f32)

    row = lambda i: (i, 0)
    fix = lambda i: (0, 0)
    act = jax.ShapeDtypeStruct((L, D_MODEL), f32)
    return _pcall(
        body, name="out_proj_loss", grid=(L // tm,),
        in_specs=[pl.BlockSpec((tm, D_MODEL), row), pl.BlockSpec((tm, MW), row), pl.BlockSpec((MW, D_MODEL), fix),
                  pl.BlockSpec((1, D_MODEL), fix), pl.BlockSpec((tm, D_MODEL), row)],
        out_specs=[pl.BlockSpec((tm, D_MODEL), row), pl.BlockSpec((tm, D_MODEL), row), pl.BlockSpec((tm, MW), row),
                   pl.BlockSpec((1, D_MODEL), fix), pl.BlockSpec((1, LANE), fix)],
        out_shape=[act, jax.ShapeDtypeStruct((L, D_MODEL), bf16), jax.ShapeDtypeStruct((L, MW), f32),
                   jax.ShapeDtypeStruct((1, D_MODEL), f32), jax.ShapeDtypeStruct((1, LANE), f32)],
        compiler_params=_cparams("arbitrary"),
    )(x, mix, wo, fw, tgt)


def _tn_matmul(a, b, name):
    L, M = a.shape
    N = b.shape[1]
    tm = 512 if M % 512 == 0 else (768 if M % 768 == 0 else M)

    def body(a_ref, b_ref, o_ref):
        o_ref[...] = lax.dot_general(a_ref[...], b_ref[...], (((0,), (0,)), ((), ())),
                                     preferred_element_type=f32).astype(o_ref.dtype)

    return _pcall(
        body, name=name, grid=(M // tm,),
        in_specs=[pl.BlockSpec((L, tm), lambda i: (0, i)), pl.BlockSpec((L, N), lambda i: (0, 0))],
        out_specs=pl.BlockSpec((tm, N), lambda i: (i, 0)),
        out_shape=jax.ShapeDtypeStruct((M, N), bf16),
        compiler_params=_cparams("parallel"),
    )(a, b)


def _gdn_gate_bwd(o, proj, gnw, dmix_a, after):
    L = o.shape[0]

    def body(o_ref, z_ref, w_ref, dm_ref, after_ref, do_ref, dz_ref, gw_ref):
        @pl.when(pl.program_id(0) == 0)
        def _():
            gw_ref[...] = jnp.zeros_like(gw_ref)
        wv = w_ref[...]
        for ls in HALVES:
            ov, z, dm = o_ref[:, ls], z_ref[:, ls], dm_ref[:, ls]
            rms = lax.rsqrt(jnp.mean(ov * ov, axis=-1, keepdims=True) + EPS)
            xh = ov * rms
            sg = _sigmoid(z)
            d_on = dm * (z * sg)
            dz_ref[:, ls] = (dm * (xh * wv) * (sg * (1.0 + z * (1.0 - sg)))).astype(bf16)
            gw_ref[...] += jnp.sum(d_on * xh, axis=0, keepdims=True)
            dxh = d_on * wv
            do_ref[:, ls] = (rms * (dxh - xh * jnp.mean(dxh * xh, axis=-1, keepdims=True))).astype(bf16)

    wide = pl.BlockSpec((L, ELT_W), lambda j: (0, j))
    return _pcall(
        body, name="gdn_gate_bwd", grid=(GDN_WIDTH // ELT_W,),
        in_specs=[wide, pl.BlockSpec((L, ELT_W), lambda j: (0, OFF_ZG // ELT_W + j)),
                  pl.BlockSpec((1, LANE), lambda j: (0, 0)), wide, ANY],
        out_specs=[wide, pl.BlockSpec((L, ELT_W), lambda j: (0, OFF_ZG // ELT_W + j)),
                   pl.BlockSpec((1, LANE), lambda j: (0, 0))],
        out_shape=[jax.ShapeDtypeStruct((L, GDN_WIDTH), bf16), jax.ShapeDtypeStruct((L, PROJ_PAD), bf16),
                   jax.ShapeDtypeStruct((1, LANE), f32)],
        compiler_params=_cparams("arbitrary"),
    )(o, proj, gnw, dmix_a, after)


def _conv_bwd(proj, cw, cb, dmix_b, dproj):
    L = proj.shape[0]

    def body(p_ref, cw_ref, cb_ref, dm_ref, dproj_in, dp_ref, gcw_ref, gcb_ref):
        for ls in HALVES:
            sb, sc_, sh, sz_ = _conv_parts(ls)
            bv, cv_, hv, z, dm = p_ref[:, sb], p_ref[:, sc_], p_ref[:, sh], p_ref[:, sz_], dm_ref[:, ls]
            u = cv_ * hv
            cv = _conv3(u, cw_ref, ls) + cb_ref[:, ls]
            sg = _sigmoid(z)
            sz = z * sg
            dp_ref[:, sb] = (dm * cv * sz).astype(bf16)
            dp_ref[:, sz_] = (dm * (bv * cv) * (sg * (1.0 + z * (1.0 - sg)))).astype(bf16)
            dcv = dm * bv * sz
            gcb_ref[:, ls] = jnp.sum(dcv, axis=0, keepdims=True)
            dcv1, dcv2 = _shift_up(dcv, 1), _shift_up(dcv, 2)
            gcw_ref[2:3, ls] = jnp.sum(dcv * u, axis=0, keepdims=True)
            gcw_ref[1:2, ls] = jnp.sum(dcv1 * u, axis=0, keepdims=True)
            gcw_ref[0:1, ls] = jnp.sum(dcv2 * u, axis=0, keepdims=True)
            du = cw_ref[2:3, ls] * dcv + cw_ref[1:2, ls] * dcv1 + cw_ref[0:1, ls] * dcv2
            dp_ref[:, sc_] = (du * hv).astype(bf16)
            dp_ref[:, sh] = (du * cv_).astype(bf16)

    return _pcall(
        body, name="conv_bwd", grid=(CONV_WIDTH // ELT_W,),
        in_specs=_conv_specs(L) + [pl.BlockSpec((L, ELT_W), lambda j: (0, GDN_WIDTH // ELT_W + j)), ANY],
        out_specs=[pl.BlockSpec((L, CONV_BLOCK), lambda j: (0, OFF_CONV // CONV_BLOCK + j)),
                   pl.BlockSpec((3, ELT_W), lambda j: (0, j)), pl.BlockSpec((1, ELT_W), lambda j: (0, j))],
        out_shape=[jax.ShapeDtypeStruct(dproj.shape, dproj.dtype), jax.ShapeDtypeStruct((3, CONV_WIDTH), f32),
                   jax.ShapeDtypeStruct((1, CONV_WIDTH), f32)],
        input_output_aliases={4: 0},
        compiler_params=_cparams("parallel"),
    )(proj, cw, cb, dmix_b, dproj)


def _gdn_bwd(qkv, sc, gr, u_all, w_all, vn_all, t_all, sp_all, do_all):
    L = qkv.shape[0]
    nc = L // CHUNK
    W = GDN_WIDTH
    cps = GDN_CPS_BWD if nc % GDN_CPS_BWD == 0 else 1
    rows_per_step = cps * CHUNK
    nsteps = nc // cps

    def body(qkv_ref, sc_ref, gr_ref, u_ref, w_ref, vn_ref, t_ref, sp_ref, do_ref, dqkv_ref, dsc_ref, dgr_ref, ds_scr):
        @pl.when(pl.program_id(0) == 0)
        def _():
            ds_scr[...] = jnp.zeros_like(ds_scr)
        nh, base = HEADS, 0
        HS = range(cps * nh)
        hl = [i % nh for i in HS]
        hd = [base + hl[i] for i in HS]
        rs = [slice((i // nh) * CHUNK, (i // nh + 1) * CHUNK) for i in HS]
        cs = [slice(hd[i] * HEAD_DIM, (hd[i] + 1) * HEAD_DIM) for i in HS]
        q = [qkv_ref[rs[i], hd[i] * HEAD_DIM:(hd[i] + 1) * HEAD_DIM] for i in HS]
        k = [qkv_ref[rs[i], W + hd[i] * HEAD_DIM:W + (hd[i] + 1) * HEAD_DIM] for i in HS]
        v = [qkv_ref[rs[i], 2 * W + hd[i] * HEAD_DIM:2 * W + (hd[i] + 1) * HEAD_DIM] for i in HS]
        hsc = [_head_scalars(sc_ref[rs[i], :], gr_ref, hd[i], i // nh) for i in HS]
        beta, gcc, gl, dmat, dmat_t = ([x[i] for x in hsc] for i in range(5))
        ii, jj = hsc[0][5], hsc[0][6]
        eg = [jnp.exp(gcc[h]) for h in HS]
        ekl = [jnp.exp(gl[h] - gcc[h]) for h in HS]
        egl = [jnp.exp(gl[h]) for h in HS]
        kb = [k[h] * beta[h] for h in HS]
        ks = [k[h] * ekl[h] for h in HS]
        do = [do_ref[rs[h], cs[h]] for h in HS]
        vn = [vn_ref[rs[h], cs[h]] for h in HS]
        s = [sp_ref[h // nh, cs[h], :] for h in HS]
        w = [w_ref[rs[h], cs[h]] for h in HS]
        qd = [q[h] * eg[h] for h in HS]

        kq = [_mm_nt(k[h], q[h]) for h in HS]
        p_t = [jnp.where(jj >= ii, kq[h] * dmat_t[h], 0.0) for h in HS]
        ptd = [_mm(p_t[h], do[h]) for h in HS]
        qw = [_cat16([qd[h], -w[h]], 0) for h in HS]
        dsn, dvn, dodv = [None] * len(HS), [None] * len(HS), [None] * len(HS)
        ds_cur = [ds_scr[base + h] for h in range(nh)]
        for ci in reversed(range(cps)):
            IS = range(ci * nh, (ci + 1) * nh)
            ksd = [_mm(ks[i], ds_cur[hl[i]]) for i in IS]
            for i in IS:
                dsn[i] = ds_cur[hl[i]]
                dvn[i] = ptd[i] + ksd[hl[i]]
                dodv[i] = _cat16([do[i], dvn[i]], 0)
            dsq = [_mm_tn(qw[i], dodv[i]) for i in IS]
            ds_cur = [egl[i] * ds_cur[hl[i]] + dsq[hl[i]] for i in IS]
        for h in range(nh):
            ds_scr[base + h] = ds_cur[h]
        x1 = [_mm_nt(dodv[h], s[h]) for h in HS]
        dks = [_mm_nt(vn[h], dsn[h]) for h in HS]
        dov = [_mm_nt(do[h], vn[h]) for h in HS]
        vdo = [_mm_nt(vn[h], do[h]) for h in HS]
        kk = [_mm_nt(kb[h], k[h]) for h in HS]
        qk = [_mm_nt(q[h], k[h]) for h in HS]
        dgl = [egl[h] * jnp.sum(jnp.sum(s[h] * dsn[h], axis=1, keepdims=True), axis=0, keepdims=True) for h in HS]
        dqd = [x1[h][:CHUNK] for h in HS]
        duw = [jnp.concatenate([dvn[h], -x1[h][CHUNK:]], axis=1) for h in HS]
        tdu = [_mm_tn(t_ref[h // nh, hd[h]], duw[h]) for h in HS]
        dvk = [duw[h] + tdu[h] for h in HS]
        uw = [jnp.concatenate([u_ref[rs[h], cs[h]], w[h]], axis=1) for h in HS]
        da = [-jnp.where(ii > jj, _mm_nt(dvk[h], uw[h]), 0.0) for h in HS]
        da_t = [-jnp.where(jj > ii, _mm_nt(uw[h], dvk[h]), 0.0) for h in HS]
        dp = [jnp.where(ii >= jj, dov[h], 0.0) for h in HS]
        dp_t = [jnp.where(jj >= ii, vdo[h], 0.0) for h in HS]
        r1 = [_mm(_cat16([da[h] * dmat[h], dp[h] * dmat[h]], 0), k[h]) for h in HS]
        dk1 = [_mm(_cat16([da_t[h] * dmat_t[h], dp_t[h] * dmat_t[h]], 1), _cat16([kb[h], q[h]], 0)) for h in HS]
        lane = _lanes((CHUNK, LANE))
        for ci in range(cps):
            dsc = jnp.zeros((CHUNK, LANE), f32)
            for i in range(ci * nh, (ci + 1) * nh):
                h = hd[i]
                a = jnp.where(ii > jj, kk[i] * dmat[i], 0.0)
                p = jnp.where(ii >= jj, qk[i] * dmat[i], 0.0)
                gmat = da[i] * a + dp[i] * p
                dvb, dkbg = dvk[i][:, :HEAD_DIM], dvk[i][:, HEAD_DIM:]
                kbg = kb[i] * eg[i]
                dkb = r1[i][:CHUNK] + dkbg * eg[i]
                dq = r1[i][CHUNK:] + dqd[i] * eg[i]
                dk = dk1[i] + dks[i] * ekl[i] + dkb * beta[i]
                dbeta = jnp.sum(dkb * k[i] + dvb * v[i], axis=1, keepdims=True)
                ksum = jnp.sum(dks[i] * ks[i], axis=1, keepdims=True)
                dgl_tot = dgl[i] + jnp.sum(ksum, axis=0, keepdims=True)
                dgc = (jnp.sum(gmat, axis=1, keepdims=True) + jnp.sum(dqd[i] * qd[i] + dkbg * kbg, axis=1, keepdims=True)
                       - ksum)
                dgc = dgc + jnp.where(_rows(dgc.shape) == CHUNK - 1, dgl_tot, 0.0)
                dqkv_ref[rs[i], h * HEAD_DIM:(h + 1) * HEAD_DIM] = dq
                dqkv_ref[rs[i], W + h * HEAD_DIM:W + (h + 1) * HEAD_DIM] = dk
                dqkv_ref[rs[i], 2 * W + h * HEAD_DIM:2 * W + (h + 1) * HEAD_DIM] = dvb * beta[i]
                dsc = jnp.where(lane == h, dbeta, jnp.where(lane == HEADS + h, dgc, dsc))
                dgr_ref[ci, h:h + 1, :] = jnp.sum(gmat, axis=0, keepdims=True)
            dsc_ref[ci * CHUNK:(ci + 1) * CHUNK, :] = dsc

    row = lambda c: (nsteps - 1 - c, 0)
    lead3 = lambda c: (nsteps - 1 - c, 0, 0)
    return _pcall(
        body, name="gdn_bwd", grid=(nsteps,),
        in_specs=[pl.BlockSpec((rows_per_step, 3 * W), row), pl.BlockSpec((rows_per_step, LANE), row),
                  pl.BlockSpec((cps, HEADS, CHUNK), lead3),
                  pl.BlockSpec((rows_per_step, W), row), pl.BlockSpec((rows_per_step, W), row),
                  pl.BlockSpec((rows_per_step, W), row),
                  pl.BlockSpec((cps, HEADS, CHUNK, CHUNK), lambda c: (nsteps - 1 - c, 0, 0, 0)),
                  pl.BlockSpec((cps, W, HEAD_DIM), lead3), pl.BlockSpec((rows_per_step, W), row)],
        out_specs=[pl.BlockSpec((rows_per_step, 3 * W), row), pl.BlockSpec((rows_per_step, LANE), row),
                   pl.BlockSpec((cps, HEADS, CHUNK), lead3)],
        out_shape=[jax.ShapeDtypeStruct((L, 3 * W), f32), jax.ShapeDtypeStruct((L, LANE), f32),
                   jax.ShapeDtypeStruct((nc, HEADS, CHUNK), f32)],
        scratch_shapes=[pltpu.VMEM((HEADS, HEAD_DIM, HEAD_DIM), f32)],
        compiler_params=_cparams("arbitrary"),
    )(qkv, sc, gr, u_all, w_all, vn_all, t_all, sp_all, do_all)


def _qkv_bwd(proj, cw, dn, dproj):
    L = proj.shape[0]

    def body(x_ref, cw_ref, dn_ref, dproj_in, dx_ref, gcw_ref):
        j = pl.program_id(0)
        steps = GDN_WIDTH // ELT_W
        scale = jnp.where(j < steps, HEAD_DIM ** -0.5, 1.0).astype(f32)
        for ls in HALVES:
            x, dn_v = x_ref[:, ls], dn_ref[:, ls]
            c = _conv4(x, cw_ref, ls)
            sg = _sigmoid(c)
            a = c * sg
            rn = lax.rsqrt(jnp.sum(a * a, axis=1, keepdims=True) + EPS)
            da_n = (scale * rn) * (dn_v - a * ((rn * rn) * jnp.sum(dn_v * a, axis=1, keepdims=True)))
            da = jnp.where(j < 2 * steps, da_n, dn_v)
            dc = da * (sg * (1.0 + c * (1.0 - sg)))
            dc1, dc2, dc3 = _shift_up(dc, 1), _shift_up(dc, 2), _shift_up(dc, 3)
            gcw_ref[3:4, ls] = jnp.sum(dc * x, axis=0, keepdims=True)
            gcw_ref[2:3, ls] = jnp.sum(dc1 * x, axis=0, keepdims=True)
            gcw_ref[1:2, ls] = jnp.sum(dc2 * x, axis=0, keepdims=True)
            gcw_ref[0:1, ls] = jnp.sum(dc3 * x, axis=0, keepdims=True)
            dx = cw_ref[3:4, ls] * dc + cw_ref[2:3, ls] * dc1 + cw_ref[1:2, ls] * dc2 + cw_ref[0:1, ls] * dc3
            dx_ref[:, ls] = dx.astype(bf16)

    col = pl.BlockSpec((L, ELT_W), lambda j: (0, j))
    wspec = pl.BlockSpec((4, ELT_W), lambda j: (0, j))
    return _pcall(
        body, name="qkv_bwd", grid=(3 * GDN_WIDTH // ELT_W,),
        in_specs=[col, wspec, col, ANY], out_specs=[col, wspec],
        out_shape=[jax.ShapeDtypeStruct(dproj.shape, dproj.dtype), jax.ShapeDtypeStruct((4, 3 * GDN_WIDTH), f32)],
        input_output_aliases={3: 0},
        compiler_params=_cparams("parallel"),
    )(proj, cw, dn, dproj)


def _scalars_bwd(proj, alog_p, dtb_p, dsc, dgr_col, dproj):
    L = proj.shape[0]

    def body(x_ref, al_ref, dt_ref, dsc_ref, dgr_ref, dproj_in, dba_ref, gs_ref):
        x, dsc_v = x_ref[...], dsc_ref[...]
        lane = _lanes(x.shape)
        dec = (lane >= HEADS) & (lane < 2 * HEADS)
        dg = jnp.where(dec, dsc_v - dgr_ref[...], 0.0)
        rc = _rows(x.shape) & (CHUNK - 1)
        for s in (1, 2, 4, 8, 16, 32):
            dg = dg + jnp.where(rc + s < CHUNK, pltpu.roll(dg, L - s, 0), 0.0)
        xa = x + dt_ref[...]
        ea = jnp.exp(al_ref[...])
        g = -ea * _softplus(xa)
        da = dg * (-ea) * _sigmoid(xa)
        beta = _sigmoid(x)
        db = dsc_v * beta * (1.0 - beta)
        dba_ref[:, :LANE] = jnp.where(lane < HEADS, db, jnp.where(dec, da, 0.0)).astype(bf16)
        dba_ref[:, LANE:] = jnp.zeros((L, ELT_W - LANE), bf16)
        g_al = jnp.sum(jnp.where(dec, dg * g, 0.0), axis=0, keepdims=True)
        g_dt = jnp.sum(jnp.where(dec, da, 0.0), axis=0, keepdims=True)
        row8 = _rows(gs_ref.shape)
        gs = jnp.where(row8 == 0, g_al, jnp.where(row8 == 1, g_dt, 0.0))
        gs_ref[...] = pltpu.roll(gs, LANE - HEADS, 1)

    full = pl.BlockSpec((L, LANE), lambda i: (0, 0))
    vec = pl.BlockSpec((1, LANE), lambda i: (0, 0))
    return _pcall(
        body, name="scalars_bwd", grid=(1,),
        in_specs=[pl.BlockSpec((L, LANE), lambda i: (0, OFF_BA // LANE)), vec, vec, full, full, ANY],
        out_specs=[pl.BlockSpec((L, ELT_W), lambda i: (0, OFF_BA // ELT_W)), pl.BlockSpec((8, LANE), lambda i: (0, 0))],
        out_shape=[jax.ShapeDtypeStruct(dproj.shape, dproj.dtype), jax.ShapeDtypeStruct((8, LANE), f32)],
        input_output_aliases={5: 0},
        compiler_params=_cparams("arbitrary"),
    )(proj, alog_p, dtb_p, dsc, dgr_col, dproj)


def _input_grad(dproj, wpad, x, nw, dy, after):
    L = x.shape[0]
    tm = min(512, L)
    cuts = (0, 3072, 5120, 7168, PROJ_PAD)
    nk = len(cuts) - 1

    def body(dp_ref, w_hbm, x_ref, nw_ref, dy_ref, after_ref, gx_ref, gnw_ref, w_vmem, sems):
        first = pl.program_id(0) == 0
        loads = [pltpu.make_async_copy(w_hbm.at[cuts[k]:cuts[k + 1], :], w_vmem.at[cuts[k]:cuts[k + 1], :], sems.at[k])
                 for k in range(nk)]

        @pl.when(first)
        def _():
            for cp in loads:
                cp.start()
            gnw_ref[...] = jnp.zeros_like(gnw_ref)
        dh = None
        for k in range(nk):
            pl.when(first)(loads[k].wait)
            part = jnp.dot(dp_ref[:, cuts[k]:cuts[k + 1]], w_vmem[cuts[k]:cuts[k + 1], :], preferred_element_type=f32)
            dh = part if dh is None else dh + part
        xv, nwv = x_ref[...], nw_ref[...]
        r = lax.rsqrt(jnp.mean(xv * xv, axis=-1, keepdims=True) + EPS)
        xh = xv * r
        gnw_ref[...] += jnp.sum(dh * xh, axis=0, keepdims=True)
        dxh = dh * nwv
        gx_ref[...] = dy_ref[...] + r * (dxh - xh * jnp.mean(dxh * xh, axis=-1, keepdims=True))

    row = lambda i: (i, 0)
    fix = lambda i: (0, 0)
    return _pcall(
        body, name="input_grad", grid=(L // tm,),
        in_specs=[pl.BlockSpec((tm, PROJ_PAD), row), ANY, pl.BlockSpec((tm, D_MODEL), row),
                  pl.BlockSpec((1, D_MODEL), fix), pl.BlockSpec((tm, D_MODEL), row), ANY],
        out_specs=[pl.BlockSpec((tm, D_MODEL), row), pl.BlockSpec((1, D_MODEL), fix)],
        out_shape=[jax.ShapeDtypeStruct((L, D_MODEL), f32), jax.ShapeDtypeStruct((1, D_MODEL), f32)],
        scratch_shapes=[pltpu.VMEM(wpad.shape, bf16), pltpu.SemaphoreType.DMA((nk,))],
        compiler_params=_cparams("arbitrary"),
    )(dproj, wpad, x, nw, dy, after)


def _adamw_reduce(parts, w, m, v, name):
    R, C = w.shape
    n_parts = parts.shape[0]
    tr = 128 if R % 128 == 0 else R
    c1 = 1.0 - ADAM_B1 ** ADAM_STEP
    c2 = 1.0 - ADAM_B2 ** ADAM_STEP

    def body(p_ref, w_ref, m_ref, v_ref, g_ref, d_ref, nm_ref, nv_ref):
        g = p_ref[0].astype(f32)
        for s in range(1, n_parts):
            g = g + p_ref[s].astype(f32)
        nm = ADAM_B1 * m_ref[...] + (1.0 - ADAM_B1) * g
        nv = ADAM_B2 * v_ref[...] + (1.0 - ADAM_B2) * (g * g)
        g_ref[...] = g
        nm_ref[...] = nm
        nv_ref[...] = nv
        d_ref[...] = -ADAM_LR * ((nm / c1) / (jnp.sqrt(nv / c2) + ADAM_EPS) + ADAM_WD * w_ref[...])

    blk = pl.BlockSpec((tr, C), lambda i: (i, 0))
    out = jax.ShapeDtypeStruct((R, C), f32)
    return _pcall(
        body, name=name, grid=(R // tr,),
        in_specs=[pl.BlockSpec((n_parts, tr, C), lambda i: (0, i, 0)), blk, blk, blk],
        out_specs=[blk] * 4, out_shape=[out] * 4,
        compiler_params=_cparams("parallel"),
    )(parts, w, m, v)


SMALL_SLOTS = ((0, D_MODEL), (D_MODEL, D_MODEL), (2 * D_MODEL, D_MODEL), (3 * D_MODEL, LANE),
               (3 * D_MODEL + LANE, HEADS), (3 * D_MODEL + 2 * LANE, HEADS))
SMALL_LOSS = 3 * D_MODEL + 3 * LANE
SMALL_W = SMALL_LOSS + LANE


def _pack_small(gs, after):
    def body(nw_ref, cb_ref, fw_ref, gn_ref, sc_ref, ls_ref, after_ref, o_ref):
        for ref, (start, width) in zip((nw_ref, cb_ref, fw_ref, gn_ref), SMALL_SLOTS[:4]):
            o_ref[:, start:start + width] = ref[...]
        o_ref[:, SMALL_SLOTS[4][0]:SMALL_SLOTS[4][0] + LANE] = sc_ref[0:1, :]
        o_ref[:, SMALL_SLOTS[5][0]:SMALL_SLOTS[5][0] + LANE] = sc_ref[1:2, :]
        o_ref[:, SMALL_LOSS:SMALL_W] = ls_ref[...]

    vm = pl.BlockSpec(memory_space=pltpu.VMEM)
    return _pcall(body, name="pack_small_grads", out_shape=jax.ShapeDtypeStruct((1, SMALL_W), f32),
                  in_specs=[vm] * 6 + [ANY], out_specs=vm)(*gs, after)


def _adamw_small(parts, ws, ms, vs):
    c1 = 1.0 - ADAM_B1 ** ADAM_STEP
    c2 = 1.0 - ADAM_B2 ** ADAM_STEP
    np_ = len(ws)

    def body(*refs):
        p_ref = refs[0]
        w_refs, m_refs, v_refs = refs[1:1 + np_], refs[1 + np_:1 + 2 * np_], refs[1 + 2 * np_:1 + 3 * np_]
        outs = refs[1 + 3 * np_:]
        g_refs, d_refs, nm_refs, nv_refs = (outs[i * np_:(i + 1) * np_] for i in range(4))
        loss_ref = outs[4 * np_]

        def total(start, width):
            t = p_ref[0, :, start:start + width]
            for s in range(1, N_DEV):
                t = t + p_ref[s, :, start:start + width]
            return t

        for i, (start, width) in enumerate(SMALL_SLOTS):
            g = total(start, width)
            nm = ADAM_B1 * m_refs[i][...] + (1.0 - ADAM_B1) * g
            nv = ADAM_B2 * v_refs[i][...] + (1.0 - ADAM_B2) * (g * g)
            g_refs[i][...] = g
            nm_refs[i][...] = nm
            nv_refs[i][...] = nv
            d_refs[i][...] = -ADAM_LR * ((nm / c1) / (jnp.sqrt(nv / c2) + ADAM_EPS) + ADAM_WD * w_refs[i][...])
        loss_ref[...] = total(SMALL_LOSS, LANE)

    vm = pl.BlockSpec(memory_space=pltpu.VMEM)
    shapes = [jax.ShapeDtypeStruct(w.shape, f32) for w in ws]
    res = _pcall(body, name="adamw_small", out_shape=shapes * 4 + [jax.ShapeDtypeStruct((1, LANE), f32)],
                 in_specs=[vm] * (1 + 3 * np_), out_specs=[vm] * (4 * np_ + 1))(parts, *ws, *ms, *vs)
    return [res[i * np_:(i + 1) * np_] for i in range(4)], res[4 * np_]


def _adamw_w_in(part_a, part_b, w3, m3, v3):
    _, n, _ = part_a.shape
    c1 = 1.0 - ADAM_B1 ** ADAM_STEP
    c2 = 1.0 - ADAM_B2 ** ADAM_STEP

    def body(pa_ref, pb_ref, w_ref, m_ref, v_ref, g_ref, d_ref, nm_ref, nv_ref):
        g = pa_ref[0].astype(f32) + pb_ref[0].astype(f32)
        nm = ADAM_B1 * m_ref[:, 0, :] + (1.0 - ADAM_B1) * g
        nv = ADAM_B2 * v_ref[:, 0, :] + (1.0 - ADAM_B2) * (g * g)
        g_ref[:, 0, :] = g
        nm_ref[:, 0, :] = nm
        nv_ref[:, 0, :] = nv
        d_ref[:, 0, :] = -ADAM_LR * ((nm / c1) / (jnp.sqrt(nv / c2) + ADAM_EPS) + ADAM_WD * w_ref[:, 0, :])

    tile = 2 * COL_TILE
    blk = pl.BlockSpec((n, 1, tile), lambda j: (0, 0, j))
    out = jax.ShapeDtypeStruct((n, 1, D_MODEL), f32)
    return _pcall(
        body, name="adamw_w_in", grid=(D_MODEL // tile,),
        in_specs=[pl.BlockSpec((1, n, tile), lambda j: (0, 0, j))] * 2 + [blk, blk, blk],
        out_specs=[blk] * 4, out_shape=[out] * 4,
        compiler_params=_cparams("parallel"),
    )(part_a, part_b, w3, m3, v3)


def _pad_lanes(vec8, start):
    return jnp.pad(vec8.reshape(1, -1), ((0, 0), (start, LANE - start - vec8.size)))


def kernel(x, norm_in_w, w_in, conv_qkv_w, A_log, dt_bias, gdn_norm_w, conv_w, conv_b, w_out, final_norm_w, loss_target, m_norm_in_w, m_w_in, m_conv_qkv_w, m_A_log, m_dt_bias, m_gdn_norm_w, m_conv_w, m_conv_b, m_w_out, m_final_norm_w, v_norm_in_w, v_w_in, v_conv_qkv_w, v_A_log, v_dt_bias, v_gdn_norm_w, v_conv_w, v_conv_b, v_w_out, v_final_norm_w):
    L = x.shape[1]
    nc = L // CHUNK
    xs = x[0]
    tgt = loss_target[0]
    fnw = final_norm_w.reshape(1, D_MODEL)

    as_rows = lambda a: jnp.transpose(a, (2, 0, 1))
    win_g, cqkv_g, cw_g = _all_gather([_cast_w_in(as_rows(w_in)), conv_qkv_w[0], conv_w[0]], "gather_weights",
                                      pieces=[4, 1, 1])
    wpad = _relayout_w_in(win_g)
    cqkv = jnp.concatenate([cqkv_g[d] for d in range(N_DEV)], axis=1)
    cw = jnp.concatenate([cw_g[d] for d in range(N_DEV)], axis=1)
    alog_p = _pad_lanes(A_log, HEADS)
    dtb_p = _pad_lanes(dt_bias, HEADS)
    me_flat = _flat(*_mesh_pos())
    tok = lambda started: started[4]
    wo_started = _spread_start(w_out[0].astype(bf16), wpad, "gather", "gather_w_out_start")

    proj, h = _in_proj(xs, norm_in_w, wpad, tok(wo_started))
    qkv = _qkv_act(proj, cqkv)
    sc, gr = _scalars(proj, alog_p, dtb_p)
    o, u_all, w_all, vn_all, t_all, sp_all = _gdn_fwd(qkv, sc, gr)
    mix = _conv_fwd(proj, cw, conv_b, _gdn_gate(o, proj, gdn_norm_w))
    wo = _spread_finish(wo_started, mix, "gather", "gather_w_out_wait", me_flat).reshape(-1, D_MODEL)
    dy, dyb, dmix, g_fnw, loss_v = _out_proj_loss(xs, mix, wo, fnw, tgt)

    g_wout = _tn_matmul(mix, dyb, "grad_w_out")
    gwo_started = _spread_start(g_wout.reshape(N_DEV, -1, D_MODEL), dyb, "scatter", "exchange_grad_w_out_start")
    do, dproj, g_gnw = _gdn_gate_bwd(o, proj, gdn_norm_w, dmix, tok(gwo_started))
    dproj, g_cw, g_cb = _conv_bwd(proj, cw, conv_b, dmix, dproj)
    dqkv_n, dsc, dgr = _gdn_bwd(qkv, sc, gr, u_all, w_all, vn_all, t_all, sp_all, do)
    dproj, g_cqkv = _qkv_bwd(proj, cqkv, dqkv_n, dproj)
    dgr_col = jnp.pad(dgr.transpose(0, 2, 1).reshape(L, HEADS), ((0, 0), (HEADS, LANE - 2 * HEADS)))
    dproj, g_sc = _scalars_bwd(proj, alog_p, dtb_p, dsc, dgr_col, dproj)
    g_win_blk = _grad_blocks(_tn_matmul(dproj, h, "grad_w_in"))

    (p_win,) = _pair_exchange([g_win_blk], "exchange_grads_pair")
    s_win = _pair_sum(g_win_blk, p_win, "pair_sum_w_in")
    r_cqkv, r_cw = _all_to_all(
        [g_cqkv.reshape(4, N_DEV, -1).transpose(1, 0, 2), g_cw.reshape(3, N_DEV, -1).transpose(1, 0, 2)],
        "exchange_small_sharded_grads")
    gw1_started = _spread_start(s_win, r_cw, "axis_a", "exchange_grads_axis1_start")
    grad_x, g_nw = _input_grad(dproj, wpad, xs, norm_in_w, dy, tok(gw1_started))
    s_thru, got1 = _spread_wait(gw1_started, grad_x, "axis_a", "exchange_grads_axis1_wait")
    t_win = _axis_sum(s_thru, got1, "axis_sum_w_in")
    gw2_started = _spread_start(t_win, got1, "axis_b", "exchange_grads_axis2_start")

    r_wout = _spread_finish(gwo_started, tok(gw2_started), "scatter", "exchange_grad_w_out_wait", me_flat)
    upd_wout =_adamw_reduce(r_wout, w_out[0], m_w_out[0], v_w_out[0], "adamw_w_out")
    upd_cqkv = _adamw_reduce(r_cqkv, conv_qkv_w[0], m_conv_qkv_w[0], v_conv_qkv_w[0], "adamw_conv_qkv_w")
    upd_cw = _adamw_reduce(r_cw, conv_w[0], m_conv_w[0], v_conv_w[0], "adamw_conv_w")

    t_thru, got2 = _spread_wait(gw2_started, upd_cw[0], "axis_b", "exchange_grads_axis2_wait")
    upd_win = [jnp.transpose(a, (1, 2, 0))
               for a in _adamw_w_in(t_thru, got2, as_rows(w_in), as_rows(m_w_in), as_rows(v_w_in))]

    small_g = _pack_small([g_nw, g_cb, g_fnw, g_gnw, g_sc, loss_v], got2)
    (small_all,) = _all_gather([small_g], "gather_small_grads")
    fvec = lambda a: a.reshape(1, D_MODEL)
    upd_small, loss_sum = _adamw_small(
        small_all,
        [norm_in_w, conv_b, fvec(final_norm_w), gdn_norm_w, A_log, dt_bias],
        [m_norm_in_w, m_conv_b, fvec(m_final_norm_w), m_gdn_norm_w, m_A_log, m_dt_bias],
        [v_norm_in_w, v_conv_b, fvec(v_final_norm_w), v_gdn_norm_w, v_A_log, v_dt_bias])

    outs = [loss_sum[0, 0], grad_x[None]]
    for k in range(4):
        nw_k, cb_k, fw_k, gn_k, al_k, dt_k = upd_small[k]
        outs += [nw_k, upd_win[k], upd_cqkv[k][None], al_k, dt_k, gn_k,
                 upd_cw[k][None], cb_k, upd_wout[k][None], fw_k.reshape(D_MODEL)]
    return tuple(outs)
```

```python
import jax
import jax.numpy as jnp
from jax import lax
from jax.experimental import pallas as pl
from jax.experimental.pallas import tpu as pltpu

f32 = jnp.float32
bf16 = jnp.bfloat16

N_DEV = 8
D_MODEL = 1024
HEADS = 8
HEAD_DIM = 128
CHUNK = 64
GDN_CPS = 4
GDN_CPS_BWD = 1
GDN_WIDTH = HEADS * HEAD_DIM
CONV_WIDTH = 1024
PROJ_WIDTH = 8208
SHARD_W = PROJ_WIDTH // N_DEV
EPS = 1e-6

LANE = 128
ELT_W = 256

OFF_QKV, OFF_ZG, OFF_CONV, OFF_BA = 0, 3072, 4096, 8192
CONV_BLOCK = 4 * ELT_W
PROJ_PAD = 8448
NAT_BA, NAT_CONV = 4096, 4112


def _padded_col(n):
    if n < NAT_BA:
        return n
    if n < NAT_CONV:
        return OFF_BA + n - NAT_BA
    g, ch = divmod(n - NAT_CONV, CONV_WIDTH)
    j, r = divmod(ch, ELT_W)
    return OFF_CONV + CONV_BLOCK * j + ELT_W * g + r


def _layout_segments(n0, n1):
    cuts = [NAT_BA, NAT_CONV] + [NAT_CONV + ELT_W * k for k in range(1, 4 * CONV_WIDTH // ELT_W)]
    pts = [n0] + [c for c in cuts if n0 < c < n1] + [n1]
    return [(lo, hi - lo, _padded_col(lo)) for lo, hi in zip(pts, pts[1:])]

ADAM_LR, ADAM_B1, ADAM_B2, ADAM_EPS, ADAM_WD, ADAM_STEP = 0.001, 0.9, 0.999, 1e-08, 0.01, 10

V7X_VMEM_BYTES = 64 * 1024 * 1024
VMEM_LIMIT = V7X_VMEM_BYTES - 8 * 1024 * 1024

MESH = pl.DeviceIdType.MESH
ANY = pl.BlockSpec(memory_space=pl.ANY)


def _pcall(body, **kw):
    return pl.pallas_call(body, **kw)


def _cparams(*sem):
    return pltpu.CompilerParams(dimension_semantics=sem if sem else None, vmem_limit_bytes=VMEM_LIMIT)


def _mm(a, b):
    return jnp.dot(a.astype(bf16), b.astype(bf16), preferred_element_type=f32)


def _mm_nt(a, b):
    return lax.dot_general(a.astype(bf16), b.astype(bf16), (((1,), (1,)), ((), ())), preferred_element_type=f32)


def _cat16(parts, axis):
    return jnp.concatenate([p.astype(bf16) for p in parts], axis=axis)


def _mm_tn(a, b):
    return lax.dot_general(a.astype(bf16), b.astype(bf16), (((0,), (0,)), ((), ())), preferred_element_type=f32)


def _rows(shape):
    return lax.broadcasted_iota(jnp.int32, shape, 0)


def _lanes(shape):
    return lax.broadcasted_iota(jnp.int32, shape, 1)


def _shift_down(x, s):
    if s == 0:
        return x
    return jnp.where(_rows(x.shape) >= s, pltpu.roll(x, s, 0), 0.0)


def _shift_up(x, s):
    if s == 0:
        return x
    n = x.shape[0]
    return jnp.where(_rows(x.shape) < n - s, pltpu.roll(x, n - s, 0), 0.0)


def _sigmoid(x):
    return jax.nn.sigmoid(x)


def _softplus(x):
    e = jnp.exp(-jnp.abs(x))
    small = e * (1.0 - e * (0.5 - e * (1.0 / 3.0)))
    return jnp.maximum(x, 0.0) + jnp.where(e < 0.01, small, jnp.log(1.0 + e))


def _mesh_pos():
    return lax.axis_index("x"), lax.axis_index("y"), lax.axis_index("c")


def _flat(px, py, pc):
    return 4 * px + 2 * py + pc


def _all_gather(xs, name, pieces=None):
    n = len(xs)
    pieces = pieces or [1] * n
    items = [(a, q) for a in range(n) for q in range(pieces[a])]
    ni = len(items)

    def view(ref, i):
        a, q = items[i]
        if pieces[a] == 1:
            return ref
        wd = xs[a].shape[-1] // pieces[a]
        return ref.at[(slice(None),) * (xs[a].ndim - 1) + (pl.ds(q * wd, wd),)]

    def body(*refs):
        x_refs, o_refs = refs[:n], refs[n:2 * n]
        send_sems, recv_sems, local_sems = refs[2 * n:]
        x, y, c = _mesh_pos()
        me, sibling = (x, y, c), (x, y, 1 - c)
        flip = lambda v, bit: v + bit - 2 * v * bit
        nbr_a = (flip(x, 1 - c), flip(y, c))
        nbr_b = (flip(x, c), flip(y, 1 - c))
        diag = (1 - x, 1 - y)

        def copy(i, k, block, to, own=False):
            a = items[i][0]
            dst = view(o_refs[a].at[_flat(*block)], i)
            return pltpu.make_async_remote_copy(
                src_ref=view(x_refs[a], i) if own else dst, dst_ref=dst,
                send_sem=send_sems.at[i, k], recv_sem=recv_sems.at[i, k], device_id=to, device_id_type=MESH)

        mine, sent = [], []

        def go(cp):
            cp.start()
            sent.append(cp)

        for a in range(n):
            cp = pltpu.make_async_copy(x_refs[a], o_refs[a].at[_flat(*me)], local_sems.at[a])
            cp.start()
            mine.append(cp)
        for a in range(ni):
            go(copy(a, 1, me, (*nbr_a, c), own=True))
            go(copy(a, 2, me, (*nbr_b, c), own=True))
            go(copy(a, 0, me, sibling, own=True))
        for a in range(ni):
            copy(a, 1, (*nbr_a, c), me).wait_recv()
            go(copy(a, 3, (*nbr_a, c), (*nbr_b, c)))
            go(copy(a, 4, (*nbr_a, c), sibling))
        for a in range(ni):
            copy(a, 2, (*nbr_b, c), me).wait_recv()
            go(copy(a, 5, (*nbr_b, c), sibling))
        for a in range(ni):
            copy(a, 3, (*diag, c), me).wait_recv()
            go(copy(a, 6, (*diag, c), sibling))
        for a in range(ni):
            copy(a, 0, sibling, me).wait_recv()
            copy(a, 4, (*nbr_b, 1 - c), me).wait_recv()
            copy(a, 5, (*nbr_a, 1 - c), me).wait_recv()
            copy(a, 6, (*diag, 1 - c), me).wait_recv()
        for cp in sent:
            cp.wait_send()
        for cp in mine:
            cp.wait()

    outs = _pcall(
        body, name=name,
        out_shape=[jax.ShapeDtypeStruct((N_DEV,) + a.shape, a.dtype) for a in xs],
        in_specs=[ANY] * n, out_specs=[ANY] * n,
        scratch_shapes=[pltpu.SemaphoreType.DMA((ni, 7)), pltpu.SemaphoreType.DMA((ni, 7)), pltpu.SemaphoreType.DMA((n,))],
    )(*xs)
    return list(outs)


def _pair_exchange(gs, name):
    n = len(gs)
    chips = [(0, 0), (0, 1), (1, 0), (1, 1)]

    def body(*refs):
        g_refs, o_refs = refs[:n], refs[n:2 * n]
        send_sems, recv_sems = refs[2 * n:]
        x, y, c = _mesh_pos()
        sibling = (x, y, 1 - c)

        def copy(a, i):
            xp, yp = chips[i]
            return pltpu.make_async_remote_copy(
                src_ref=g_refs[a].at[_flat(xp, yp, 1 - c)], dst_ref=o_refs[a].at[i],
                send_sem=send_sems.at[a, i], recv_sem=recv_sems.at[a, i], device_id=sibling, device_id_type=MESH)

        cps = [copy(a, i) for a in range(n) for i in range(4)]
        for cp in cps:
            cp.start()
        for cp in cps:
            cp.wait()

    outs = _pcall(
        body, name=name,
        out_shape=[jax.ShapeDtypeStruct((4,) + a.shape[1:], a.dtype) for a in gs],
        in_specs=[ANY] * n, out_specs=[ANY] * n,
        scratch_shapes=[pltpu.SemaphoreType.DMA((n, 4)), pltpu.SemaphoreType.DMA((n, 4))],
    )(*gs)
    return list(outs)


def _pair_sum(g, p1, name):
    _, R, C = g.shape
    tr = 256 if R % 256 == 0 else R
    cidx = lax.axis_index("c").astype(jnp.int32).reshape(1)

    def body(c_ref, g_ref, p_ref, o_ref):
        o_ref[...] = (g_ref[...].astype(f32) + p_ref[...].astype(f32)).astype(o_ref.dtype)

    return _pcall(
        body, name=name,
        grid_spec=pltpu.PrefetchScalarGridSpec(
            num_scalar_prefetch=1, grid=(4, R // tr),
            in_specs=[pl.BlockSpec((1, tr, C), lambda i, r, c_ref: (2 * i + c_ref[0], r, 0)),
                      pl.BlockSpec((1, tr, C), lambda i, r, c_ref: (i, r, 0))],
            out_specs=pl.BlockSpec((1, tr, C), lambda i, r, c_ref: (i, r, 0))),
        out_shape=jax.ShapeDtypeStruct((4, R, C), g.dtype),
        compiler_params=_cparams("parallel", "parallel"),
    )(cidx, g, p1)


def _axis_sum(s, got, name):
    _, R, C = s.shape
    x, y, c = _mesh_pos()
    me, _, b, _ = _axis_chips(x, y, c)
    idx = jnp.stack([2 * me[0] + me[1], 2 * b[0] + b[1]]).astype(jnp.int32)

    def body(idx_ref, s_ref, g_ref, o_ref):
        o_ref[...] = (s_ref[...].astype(f32) + g_ref[...].astype(f32)).astype(o_ref.dtype)

    return _pcall(
        body, name=name,
        grid_spec=pltpu.PrefetchScalarGridSpec(
            num_scalar_prefetch=1, grid=(2,),
            in_specs=[pl.BlockSpec((1, R, C), lambda k, idx_ref: (idx_ref[k], 0, 0)),
                      pl.BlockSpec((1, R, C), lambda k, idx_ref: (k, 0, 0))],
            out_specs=pl.BlockSpec((1, R, C), lambda k, idx_ref: (k, 0, 0))),
        out_shape=jax.ShapeDtypeStruct((2, R, C), s.dtype),
        compiler_params=_cparams("parallel"),
    )(idx, s, got)


HBM = pl.BlockSpec(memory_space=pltpu.HBM)
SEM = pl.BlockSpec(memory_space=pltpu.SEMAPHORE)
EFFECT = pltpu.SideEffectType.DATAFLOW_SIDE_EFFECTING


def _peers(x, y, c):
    out = []
    for k in range(1, N_DEV):
        kx, ky, kc = (k >> 2) & 1, (k >> 1) & 1, k & 1
        out.append(((1 - x) if kx else x, (1 - y) if ky else y, (1 - c) if kc else c))
    return out


SPREAD_COPIES = {"gather": N_DEV - 1, "scatter": N_DEV - 1, "axis_a": 2, "axis_b": 1}
SPREAD_SLOTS = {"axis_a": 2, "axis_b": 1}


def _axis_chips(x, y, c):
    flip = lambda v, bit: v + bit - 2 * v * bit
    return (x, y), (flip(x, 1 - c), flip(y, c)), (flip(x, c), flip(y, 1 - c)), (1 - x, 1 - y)


def _spread_copy(src_ref, land_ref, send_sems, recv_sems, k, plan):
    x, y, c = _mesh_pos()
    if plan in ("axis_a", "axis_b"):
        _, a, b, d = _axis_chips(x, y, c)
        chip = lambda p: 2 * p[0] + p[1]
        peer = (*(a if plan == "axis_a" else b), c)
        src = src_ref.at[chip(a) if k == 0 else chip(d)] if plan == "axis_a" else src_ref.at[1]
        slot = k
    else:
        peer = _peers(x, y, c)[k]
        src, slot = (src_ref.at[_flat(*peer)] if plan == "scatter" else src_ref), _flat(x, y, c)
    return pltpu.make_async_remote_copy(
        src_ref=src, dst_ref=land_ref.at[slot], send_sem=send_sems.at[k], recv_sem=recv_sems.at[k],
        device_id=peer, device_id_type=MESH)


def _spread_start(src, after, plan, name):
    land_shape = (N_DEV,) + src.shape if plan == "gather" else src.shape
    if plan in SPREAD_SLOTS:
        land_shape = (SPREAD_SLOTS[plan],) + src.shape[1:]
    n_copies = SPREAD_COPIES[plan]

    def body(src_ref, land_ref, after_ref, send_sems, recv_sems, src_thru, land_thru, token):
        for k in range(n_copies):
            _spread_copy(src_ref, land_ref, send_sems, recv_sems, k, plan).start()
        token[...] = jnp.zeros_like(token)

    return _pcall(
        body, name=name,
        out_shape=(pltpu.SemaphoreType.DMA((n_copies,)), pltpu.SemaphoreType.DMA((n_copies,)),
                   pltpu.HBM(src.shape, src.dtype), pltpu.HBM(land_shape, src.dtype), jax.ShapeDtypeStruct((8, LANE), f32)),
        in_specs=(HBM, HBM, ANY), out_specs=(SEM, SEM, HBM, HBM, pl.BlockSpec(memory_space=pltpu.VMEM)),
        input_output_aliases={0: 2, 1: 3},
        compiler_params=pltpu.CompilerParams(has_side_effects=EFFECT),
    )(pltpu.with_memory_space_constraint(src, pltpu.HBM),
      pltpu.with_memory_space_constraint(lax.empty(land_shape, src.dtype), pltpu.HBM), after)


def _spread_wait(started, after, plan, name):
    send_sems, recv_sems, src_thru, land_thru, _ = started

    def body(src_ref, land_ref, send_sems, recv_sems, after_ref, src_dead, got_ref):
        for k in range(SPREAD_COPIES[plan]):
            cp = _spread_copy(src_ref, land_ref, send_sems, recv_sems, k, plan)
            cp.wait_send()
            cp.wait_recv()

    return _pcall(
        body, name=name,
        out_shape=(pltpu.HBM(src_thru.shape, src_thru.dtype), pltpu.HBM(land_thru.shape, land_thru.dtype)),
        in_specs=(HBM, HBM, SEM, SEM, ANY), out_specs=(HBM, HBM), input_output_aliases={0: 0, 1: 1},
        compiler_params=pltpu.CompilerParams(has_side_effects=EFFECT),
    )(src_thru, land_thru, send_sems, recv_sems, after)


def _spread_finish(started, after, plan, name, slot):
    src, land = _spread_wait(started, after, plan, name)
    block = src if plan == "gather" else lax.dynamic_index_in_dim(src, slot, 0, keepdims=False)
    return _own_slot(land, block, slot)


def _own_slot(land, block, slot):
    zero = jnp.zeros((), jnp.int32)
    return lax.dynamic_update_slice(land, block[None], (slot.astype(jnp.int32),) + (zero,) * block.ndim)


COL_TILE = 256


def _cast_w_in(w3):
    n = w3.shape[0]

    def body(w_ref, o_ref):
        o_ref[...] = w_ref[:, 0, :].astype(bf16)

    tile = 2 * COL_TILE
    return _pcall(
        body, name="cast_w_in", grid=(D_MODEL // tile,),
        in_specs=[pl.BlockSpec((n, 1, tile), lambda j: (0, 0, j))],
        out_specs=pl.BlockSpec((n, tile), lambda j: (0, j)),
        out_shape=jax.ShapeDtypeStruct((n, D_MODEL), bf16),
        compiler_params=_cparams("parallel"),
    )(w3)


def _relayout_w_in(win_g):
    def body(g_ref, o_ref):
        used = OFF_BA + NAT_CONV - NAT_BA
        o_ref[used:PROJ_PAD, :] = jnp.zeros((PROJ_PAD - used, COL_TILE), o_ref.dtype)
        for d in range(N_DEV):
            for lo, width, dst in _layout_segments(d * SHARD_W, (d + 1) * SHARD_W):
                src = lo - d * SHARD_W
                o_ref[dst:dst + width, :] = g_ref[d, src:src + width, :]

    return _pcall(
        body, name="relayout_w_in", grid=(D_MODEL // COL_TILE,),
        in_specs=[pl.BlockSpec((N_DEV, SHARD_W, COL_TILE), lambda j: (0, 0, j))],
        out_specs=pl.BlockSpec((PROJ_PAD, COL_TILE), lambda j: (0, j)),
        out_shape=jax.ShapeDtypeStruct((PROJ_PAD, D_MODEL), win_g.dtype),
        compiler_params=_cparams("parallel"),
    )(win_g)


def _grad_blocks(g_t):
    def body(p_ref, o_ref):
        for d in range(N_DEV):
            for lo, width, src in _layout_segments(d * SHARD_W, (d + 1) * SHARD_W):
                dst = lo - d * SHARD_W
                o_ref[d, dst:dst + width, :] = p_ref[src:src + width, :]

    return _pcall(
        body, name="grad_blocks", grid=(D_MODEL // COL_TILE,),
        in_specs=[pl.BlockSpec((PROJ_PAD, COL_TILE), lambda j: (0, j))],
        out_specs=pl.BlockSpec((N_DEV, SHARD_W, COL_TILE), lambda j: (0, 0, j)),
        out_shape=jax.ShapeDtypeStruct((N_DEV, SHARD_W, D_MODEL), bf16),
        compiler_params=_cparams("parallel"),
    )(g_t)


def _in_proj(x, nw, wpad_t, after):
    L = x.shape[0]
    tn = 768
    nj = wpad_t.shape[0] // tn

    def body(x_ref, nw_ref, w_ref, after_ref, proj_ref, h_ref):
        @pl.when(pl.program_id(0) == 0)
        def _():
            for r in range(0, L, 256):
                xs = x_ref[r:r + 256, :]
                ms = jnp.mean(xs * xs, axis=-1, keepdims=True)
                h_ref[r:r + 256, :] = ((xs * lax.rsqrt(ms + EPS)) * nw_ref[...]).astype(bf16)
        for r in range(0, L, 512):
            proj_ref[r:r + 512, :] = lax.dot_general(h_ref[r:r + 512, :], w_ref[...], (((1,), (1,)), ((), ())),
                                                     preferred_element_type=f32)

    return _pcall(
        body, name="in_proj", grid=(nj,),
        in_specs=[pl.BlockSpec((L, D_MODEL), lambda j: (0, 0)), pl.BlockSpec((1, D_MODEL), lambda j: (0, 0)),
                  pl.BlockSpec((tn, D_MODEL), lambda j: (j, 0)), ANY],
        out_specs=[pl.BlockSpec((L, tn), lambda j: (0, j)), pl.BlockSpec((L, D_MODEL), lambda j: (0, 0))],
        out_shape=[jax.ShapeDtypeStruct((L, wpad_t.shape[0]), f32), jax.ShapeDtypeStruct((L, D_MODEL), bf16)],
        compiler_params=_cparams("arbitrary"),
    )(x, nw, wpad_t, after)


HALVES = [slice(i * LANE, (i + 1) * LANE) for i in range(ELT_W // LANE)]
QKV_W = 512
QKV_HEADS = [slice(i * LANE, (i + 1) * LANE) for i in range(QKV_W // LANE)]
STEPS_PER_GROUP = GDN_WIDTH // QKV_W


def _conv4(x, cw_ref, ls):
    return (cw_ref[3:4, ls] * x + cw_ref[2:3, ls] * _shift_down(x, 1) + cw_ref[1:2, ls] * _shift_down(x, 2)
            + cw_ref[0:1, ls] * _shift_down(x, 3))


def _qkv_act(proj, cw):
    L = proj.shape[0]

    def body(x_ref, cw_ref, o_ref):
        j = pl.program_id(0)
        scale = jnp.where(j < STEPS_PER_GROUP, HEAD_DIM ** -0.5, 1.0).astype(f32)
        for ls in QKV_HEADS:
            c = _conv4(x_ref[:, ls], cw_ref, ls)
            a = c * _sigmoid(c)
            rn = lax.rsqrt(jnp.sum(a * a, axis=1, keepdims=True) + EPS)
            o_ref[:, ls] = jnp.where(j < 2 * STEPS_PER_GROUP, (a * rn) * scale, a)

    return _pcall(
        body, name="qkv_act", grid=(3 * STEPS_PER_GROUP,),
        in_specs=[pl.BlockSpec((L, QKV_W), lambda j: (0, j)), pl.BlockSpec((4, QKV_W), lambda j: (0, j))],
        out_specs=pl.BlockSpec((L, QKV_W), lambda j: (0, j)),
        out_shape=jax.ShapeDtypeStruct((L, 3 * GDN_WIDTH), f32),
        compiler_params=_cparams("parallel"),
    )(proj, cw)


def _scalars(proj, alog_p, dtb_p):
    L = proj.shape[0]
    nc = L // CHUNK

    def body(x_ref, al_ref, dt_ref, sc_ref, gr_ref):
        x = x_ref[...]
        lane = _lanes(x.shape)
        beta = _sigmoid(x)
        g = -jnp.exp(al_ref[...]) * _softplus(x + dt_ref[...])
        gc = jnp.where((lane >= HEADS) & (lane < 2 * HEADS), g, 0.0)
        rc = _rows(x.shape) & (CHUNK - 1)
        for s in (1, 2, 4, 8, 16, 32):
            gc = gc + jnp.where(rc >= s, pltpu.roll(gc, s, 0), 0.0)
        sc_ref[...] = jnp.where(lane < HEADS, beta, gc)
        sel = (_lanes((HEADS, LANE)) == _rows((HEADS, LANE)) + HEADS).astype(f32)
        for c in range(nc):
            gr_ref[c] = lax.dot_general(sel, sc_ref[c * CHUNK:(c + 1) * CHUNK, :], (((1,), (1,)), ((), ())),
                                        preferred_element_type=f32, precision=lax.Precision.HIGHEST)

    return _pcall(
        body, name="scalars", grid=(1,),
        in_specs=[pl.BlockSpec((L, LANE), lambda i: (0, OFF_BA // LANE)), pl.BlockSpec((1, LANE), lambda i: (0, 0)),
                  pl.BlockSpec((1, LANE), lambda i: (0, 0))],
        out_specs=[pl.BlockSpec((L, LANE), lambda i: (0, 0)), pl.BlockSpec((nc, HEADS, CHUNK), lambda i: (0, 0, 0))],
        out_shape=[jax.ShapeDtypeStruct((L, LANE), f32), jax.ShapeDtypeStruct((nc, HEADS, CHUNK), f32)],
        compiler_params=_cparams("arbitrary"),
    )(proj, alog_p, dtb_p)


def _head_scalars(sc, gr_ref, h, ci=0):
    lane = _lanes(sc.shape)
    beta = jnp.sum(jnp.where(lane == h, sc, 0.0), axis=1, keepdims=True)
    gcc = jnp.sum(jnp.where(lane == HEADS + h, sc, 0.0), axis=1, keepdims=True)
    gcr = gr_ref[ci, h:h + 1, :]
    gl = jnp.sum(jnp.where(_lanes(gcr.shape) == CHUNK - 1, gcr, 0.0), axis=1, keepdims=True)
    ii, jj = _rows((CHUNK, CHUNK)), _lanes((CHUNK, CHUNK))
    dmat = jnp.where(ii >= jj, jnp.exp(jnp.minimum(gcc - gcr, 0.0)), 0.0)
    dmat_t = jnp.where(jj >= ii, jnp.exp(jnp.minimum(gcr - gcc, 0.0)), 0.0)
    return beta, gcc, gl, dmat, dmat_t, ii, jj


def _gdn_fwd(qkv, sc, gr):
    L = qkv.shape[0]
    nc = L // CHUNK
    W = GDN_WIDTH
    cps = GDN_CPS if nc % GDN_CPS == 0 else 1
    rows_per_step = cps * CHUNK

    def body(qkv_ref, sc_ref, gr_ref, o_ref, u_ref, w_ref, vn_ref, t_ref, sp_ref, s_scr):
        @pl.when(pl.program_id(0) == 0)
        def _():
            s_scr[...] = jnp.zeros_like(s_scr)
        HS = range(cps * HEADS)
        hd = [i % HEADS for i in HS]
        rs = [slice((i // HEADS) * CHUNK, (i // HEADS + 1) * CHUNK) for i in HS]
        cs = [slice(hd[i] * HEAD_DIM, (hd[i] + 1) * HEAD_DIM) for i in HS]
        q = [qkv_ref[rs[i], hd[i] * HEAD_DIM:(hd[i] + 1) * HEAD_DIM] for i in HS]
        k = [qkv_ref[rs[i], W + hd[i] * HEAD_DIM:W + (hd[i] + 1) * HEAD_DIM] for i in HS]
        v = [qkv_ref[rs[i], 2 * W + hd[i] * HEAD_DIM:2 * W + (hd[i] + 1) * HEAD_DIM] for i in HS]
        hsc = [_head_scalars(sc_ref[rs[i], :], gr_ref, hd[i], i // HEADS) for i in HS]
        beta, gcc, gl, dmat = ([x[i] for x in hsc] for i in range(4))
        ii, jj = hsc[0][5], hsc[0][6]
        eg = [jnp.exp(gcc[h]) for h in HS]
        kb = [k[h] * beta[h] for h in HS]
        kk = [_mm_nt(kb[h], k[h]) for h in HS]
        qk = [_mm_nt(q[h], k[h]) for h in HS]
        n0 = [-jnp.where(ii > jj, kk[h] * dmat[h], 0.0) for h in HS]
        n1 = [_mm(n0[h], n0[h]) for h in HS]
        n2 = [_mm(n1[h], n1[h]) for h in HS]
        p01 = [n0[h] + n1[h] + _mm(n0[h], n1[h]) for h in HS]
        n3 = [_mm(n2[h], n2[h]) for h in HS]
        n4 = [_mm(n3[h], n3[h]) for h in HS]
        p23 = [n2[h] + n3[h] + _mm(n2[h], n3[h]) for h in HS]
        n5 = [_mm(n4[h], n4[h]) for h in HS]
        p03 = [p01[h] + p23[h] + _mm(p01[h], p23[h]) for h in HS]
        p45 = [n4[h] + n5[h] + _mm(n4[h], n5[h]) for h in HS]
        t = [p03[h] + p45[h] + _mm(p03[h], p45[h]) for h in HS]
        vb = [v[h] * beta[h] for h in HS]
        kbg = [kb[h] * eg[h] for h in HS]
        uw = [_mm(t[h], _cat16([vb[h], kbg[h]], 1)) for h in HS]
        u = [vb[h] + uw[h][:, :HEAD_DIM] for h in HS]
        w = [kbg[h] + uw[h][:, HEAD_DIM:] for h in HS]
        wq = [_cat16([w[h], q[h] * eg[h]], 0) for h in HS]
        p = [jnp.where(ii >= jj, qk[h] * dmat[h], 0.0) for h in HS]
        ks = [k[h] * jnp.exp(gl[h] - gcc[h]) for h in HS]
        s = [s_scr[h] for h in range(HEADS)]
        for ci in range(cps):
            IS = range(ci * HEADS, (ci + 1) * HEADS)
            ws = [_mm(wq[i], s[hd[i]]) for i in IS]
            vn = [u[i] - ws[hd[i]][:CHUNK] for i in IS]
            pv = [_mm(p[i], vn[hd[i]]) for i in IS]
            kv = [_mm_tn(ks[i], vn[hd[i]]) for i in IS]
            for i in IS:
                h = hd[i]
                sp_ref[ci, cs[i], :] = s[h]
                o_ref[rs[i], cs[i]] = ws[h][CHUNK:] + pv[h]
                vn_ref[rs[i], cs[i]] = vn[h].astype(bf16)
            s = [jnp.exp(gl[i]) * s[hd[i]] + kv[hd[i]] for i in IS]
        for h in range(HEADS):
            s_scr[h] = s[h]
        for i in HS:
            u_ref[rs[i], cs[i]] = u[i].astype(bf16)
            w_ref[rs[i], cs[i]] = w[i].astype(bf16)
            t_ref[i // HEADS, hd[i]] = t[i].astype(bf16)

    row = lambda c: (c, 0)
    act, act16 = jax.ShapeDtypeStruct((L, W), f32), jax.ShapeDtypeStruct((L, W), bf16)
    return _pcall(
        body, name="gdn_fwd", grid=(nc // cps,),
        in_specs=[pl.BlockSpec((rows_per_step, 3 * W), row), pl.BlockSpec((rows_per_step, LANE), row),
                  pl.BlockSpec((cps, HEADS, CHUNK), lambda c: (c, 0, 0))],
        out_specs=[pl.BlockSpec((rows_per_step, W), row)] * 4 + [
            pl.BlockSpec((cps, HEADS, CHUNK, CHUNK), lambda c: (c, 0, 0, 0)),
            pl.BlockSpec((cps, W, HEAD_DIM), lambda c: (c, 0, 0))],
        out_shape=[act, act16, act16, act16, jax.ShapeDtypeStruct((nc, HEADS, CHUNK, CHUNK), bf16),
                   jax.ShapeDtypeStruct((nc, W, HEAD_DIM), f32)],
        scratch_shapes=[pltpu.VMEM((HEADS, HEAD_DIM, HEAD_DIM), f32)],
        compiler_params=_cparams("arbitrary"),
    )(qkv, sc, gr)


def _gdn_gate(o, proj, gnw):
    L = o.shape[0]

    def body(o_ref, z_ref, w_ref, m_ref):
        for ls in HALVES:
            ov, z = o_ref[:, ls], z_ref[:, ls]
            rms = lax.rsqrt(jnp.mean(ov * ov, axis=-1, keepdims=True) + EPS)
            m_ref[:, ls] = (((ov * rms) * w_ref[...]) * (z * _sigmoid(z))).astype(bf16)

    return _pcall(
        body, name="gdn_gate", grid=(GDN_WIDTH // ELT_W,),
        in_specs=[pl.BlockSpec((L, ELT_W), lambda j: (0, j)), pl.BlockSpec((L, ELT_W), lambda j: (0, OFF_ZG // ELT_W + j)),
                  pl.BlockSpec((1, LANE), lambda j: (0, 0))],
        out_specs=pl.BlockSpec((L, ELT_W), lambda j: (0, j)),
        out_shape=jax.ShapeDtypeStruct((L, GDN_WIDTH + CONV_WIDTH), bf16),
        compiler_params=_cparams("parallel"),
    )(o, proj, gnw)


def _conv3(u, cw_ref, ls):
    return cw_ref[2:3, ls] * u + cw_ref[1:2, ls] * _shift_down(u, 1) + cw_ref[0:1, ls] * _shift_down(u, 2)


def _conv_specs(L):
    return [pl.BlockSpec((L, CONV_BLOCK), lambda j: (0, OFF_CONV // CONV_BLOCK + j)),
            pl.BlockSpec((3, ELT_W), lambda j: (0, j)), pl.BlockSpec((1, ELT_W), lambda j: (0, j))]


def _conv_parts(ls):
    return [slice(g * ELT_W + ls.start, g * ELT_W + ls.stop) for g in range(4)]


def _conv_fwd(proj, cw, cb, mix):
    L = proj.shape[0]

    def body(p_ref, cw_ref, cb_ref, mix_in, m_ref):
        for ls in HALVES:
            sb, sc_, sh, sz = _conv_parts(ls)
            z = p_ref[:, sz]
            cv = _conv3(p_ref[:, sc_] * p_ref[:, sh], cw_ref, ls) + cb_ref[:, ls]
            m_ref[:, ls] = ((p_ref[:, sb] * cv) * (z * _sigmoid(z))).astype(bf16)

    return _pcall(
        body, name="conv_fwd", grid=(CONV_WIDTH // ELT_W,),
        in_specs=_conv_specs(L) + [ANY], out_specs=pl.BlockSpec((L, ELT_W), lambda j: (0, GDN_WIDTH // ELT_W + j)),
        out_shape=jax.ShapeDtypeStruct(mix.shape, mix.dtype), input_output_aliases={3: 0},
        compiler_params=_cparams("parallel"),
    )(proj, cw, cb, mix)


def _out_proj_loss(x, mix, wo, fw, tgt):
    L = x.shape[0]
    tm = min(512, L)
    MW = GDN_WIDTH + CONV_WIDTH

    def body(x_ref, m_ref, wo_ref, fw_ref, t_ref, dy_ref, dyb_ref, dm_ref, gfw_ref, loss_ref):
        @pl.when(pl.program_id(0) == 0)
        def _():
            gfw_ref[...] = jnp.zeros_like(gfw_ref)
            loss_ref[...] = jnp.zeros_like(loss_ref)
        y = x_ref[...] + jnp.dot(m_ref[...], wo_ref[...], preferred_element_type=f32)
        r = lax.rsqrt(jnp.mean(y * y, axis=-1, keepdims=True) + EPS)
        yh = y * r
        fwv = fw_ref[...]
        diff = yh * fwv - t_ref[...]
        loss_ref[...] += jnp.sum(jnp.sum(diff * diff, axis=-1, keepdims=True), axis=0, keepdims=True) * (0.5 / D_MODEL)
        dout = diff * (1.0 / D_MODEL)
        gfw_ref[...] += jnp.sum(dout * yh, axis=0, keepdims=True)
        dyh = dout * fwv
        dy = r * (dyh - yh * jnp.mean(dyh * yh, axis=-1, keepdims=True))
        dy_ref[...] = dy
        dyb = dy.astype(bf16)
        dyb_ref[...] = dyb
        dm_ref[...] = lax.dot_general(dyb, wo_ref[...], (((1,), (1,)), ((), ())), preferred_element_type=f32)

    row = lambda i: (i, 0)
    fix = lambda i: (0, 0)
    act = jax.ShapeDtypeStruct((L, D_MODEL), f32)
    return _pcall(
        body, name="out_proj_loss", grid=(L // tm,),
        in_specs=[pl.BlockSpec((tm, D_MODEL), row), pl.BlockSpec((tm, MW), row), pl.BlockSpec((MW, D_MODEL), fix),
                  pl.BlockSpec((1, D_MODEL), fix), pl.BlockSpec((tm, D_MODEL), row)],
        out_specs=[pl.BlockSpec((tm, D_MODEL), row), pl.BlockSpec((tm, D_MODEL), row), pl.BlockSpec((tm, MW), row),
                   pl.BlockSpec((1, D_MODEL), fix), pl.BlockSpec((1, LANE), fix)],
        out_shape=[act, jax.ShapeDtypeStruct((L, D_MODEL), bf16), jax.ShapeDtypeStruct((L, MW), f32),
                   jax.ShapeDtypeStruct((1, D_MODEL), f32), jax.ShapeDtypeStruct((1, LANE), f32)],
        compiler_params=_cparams("arbitrary"),
    )(x, mix, wo, fw, tgt)


def _tn_matmul(a, b, name):
    L, M = a.shape
    N = b.shape[1]
    tm = 512 if M % 512 == 0 else (768 if M % 768 == 0 else M)

    def body(a_ref, b_ref, o_ref):
        o_ref[...] = lax.dot_general(a_ref[...], b_ref[...], (((0,), (0,)), ((), ())),
                                     preferred_element_type=f32).astype(o_ref.dtype)

    return _pcall(
        body, name=name, grid=(M // tm,),
        in_specs=[pl.BlockSpec((L, tm), lambda i: (0, i)), pl.BlockSpec((L, N), lambda i: (0, 0))],
        out_specs=pl.BlockSpec((tm, N), lambda i: (i, 0)),
        out_shape=jax.ShapeDtypeStruct((M, N), bf16),
        compiler_params=_cparams("parallel"),
    )(a, b)


def _gdn_gate_bwd(o, proj, gnw, dmix_a, after):
    L = o.shape[0]

    def body(o_ref, z_ref, w_ref, dm_ref, after_ref, do_ref, dz_ref, gw_ref):
        @pl.when(pl.program_id(0) == 0)
        def _():
            gw_ref[...] = jnp.zeros_like(gw_ref)
        wv = w_ref[...]
        for ls in HALVES:
            ov, z, dm = o_ref[:, ls], z_ref[:, ls], dm_ref[:, ls]
            rms = lax.rsqrt(jnp.mean(ov * ov, axis=-1, keepdims=True) + EPS)
            xh = ov * rms
            sg = _sigmoid(z)
            d_on = dm * (z * sg)
            dz_ref[:, ls] = (dm * (xh * wv) * (sg * (1.0 + z * (1.0 - sg)))).astype(bf16)
            gw_ref[...] += jnp.sum(d_on * xh, axis=0, keepdims=True)
            dxh = d_on * wv
            do_ref[:, ls] = (rms * (dxh - xh * jnp.mean(dxh * xh, axis=-1, keepdims=True))).astype(bf16)

    wide = pl.BlockSpec((L, ELT_W), lambda j: (0, j))
    return _pcall(
        body, name="gdn_gate_bwd", grid=(GDN_WIDTH // ELT_W,),
        in_specs=[wide, pl.BlockSpec((L, ELT_W), lambda j: (0, OFF_ZG // ELT_W + j)),
                  pl.BlockSpec((1, LANE), lambda j: (0, 0)), wide, ANY],
        out_specs=[wide, pl.BlockSpec((L, ELT_W), lambda j: (0, OFF_ZG // ELT_W + j)),
                   pl.BlockSpec((1, LANE), lambda j: (0, 0))],
        out_shape=[jax.ShapeDtypeStruct((L, GDN_WIDTH), bf16), jax.ShapeDtypeStruct((L, PROJ_PAD), bf16),
                   jax.ShapeDtypeStruct((1, LANE), f32)],
        compiler_params=_cparams("arbitrary"),
    )(o, proj, gnw, dmix_a, after)


def _conv_bwd(proj, cw, cb, dmix_b, dproj):
    L = proj.shape[0]

    def body(p_ref, cw_ref, cb_ref, dm_ref, dproj_in, dp_ref, gcw_ref, gcb_ref):
        for ls in HALVES:
            sb, sc_, sh, sz_ = _conv_parts(ls)
            bv, cv_, hv, z, dm = p_ref[:, sb], p_ref[:, sc_], p_ref[:, sh], p_ref[:, sz_], dm_ref[:, ls]
            u = cv_ * hv
            cv = _conv3(u, cw_ref, ls) + cb_ref[:, ls]
            sg = _sigmoid(z)
            sz = z * sg
            dp_ref[:, sb] = (dm * cv * sz).astype(bf16)
            dp_ref[:, sz_] = (dm * (bv * cv) * (sg * (1.0 + z * (1.0 - sg)))).astype(bf16)
            dcv = dm * bv * sz
            gcb_ref[:, ls] = jnp.sum(dcv, axis=0, keepdims=True)
            dcv1, dcv2 = _shift_up(dcv, 1), _shift_up(dcv, 2)
            gcw_ref[2:3, ls] = jnp.sum(dcv * u, axis=0, keepdims=True)
            gcw_ref[1:2, ls] = jnp.sum(dcv1 * u, axis=0, keepdims=True)
            gcw_ref[0:1, ls] = jnp.sum(dcv2 * u, axis=0, keepdims=True)
            du = cw_ref[2:3, ls] * dcv + cw_ref[1:2, ls] * dcv1 + cw_ref[0:1, ls] * dcv2
            dp_ref[:, sc_] = (du * hv).astype(bf16)
            dp_ref[:, sh] = (du * cv_).astype(bf16)

    return _pcall(
        body, name="conv_bwd", grid=(CONV_WIDTH // ELT_W,),
        in_specs=_conv_specs(L) + [pl.BlockSpec((L, ELT_W), lambda j: (0, GDN_WIDTH // ELT_W + j)), ANY],
        out_specs=[pl.BlockSpec((L, CONV_BLOCK), lambda j: (0, OFF_CONV // CONV_BLOCK + j)),
                   pl.BlockSpec((3, ELT_W), lambda j: (0, j)), pl.BlockSpec((1, ELT_W), lambda j: (0, j))],
        out_shape=[jax.ShapeDtypeStruct(dproj.shape, dproj.dtype), jax.ShapeDtypeStruct((3, CONV_WIDTH), f32),
                   jax.ShapeDtypeStruct((1, CONV_WIDTH), f32)],
        input_output_aliases={4: 0},
        compiler_params=_cparams("parallel"),
    )(proj, cw, cb, dmix_b, dproj)


def _gdn_bwd(qkv, sc, gr, u_all, w_all, vn_all, t_all, sp_all, do_all):
    L = qkv.shape[0]
    nc = L // CHUNK
    W = GDN_WIDTH
    cps = GDN_CPS_BWD if nc % GDN_CPS_BWD == 0 else 1
    rows_per_step = cps * CHUNK
    nsteps = nc // cps

    def body(qkv_ref, sc_ref, gr_ref, u_ref, w_ref, vn_ref, t_ref, sp_ref, do_ref, dqkv_ref, dsc_ref, dgr_ref, ds_scr):
        @pl.when(pl.program_id(0) == 0)
        def _():
            ds_scr[...] = jnp.zeros_like(ds_scr)
        nh, base = HEADS, 0
        HS = range(cps * nh)
        hl = [i % nh for i in HS]
        hd = [base + hl[i] for i in HS]
        rs = [slice((i // nh) * CHUNK, (i // nh + 1) * CHUNK) for i in HS]
        cs = [slice(hd[i] * HEAD_DIM, (hd[i] + 1) * HEAD_DIM) for i in HS]
        q = [qkv_ref[rs[i], hd[i] * HEAD_DIM:(hd[i] + 1) * HEAD_DIM] for i in HS]
        k = [qkv_ref[rs[i], W + hd[i] * HEAD_DIM:W + (hd[i] + 1) * HEAD_DIM] for i in HS]
        v = [qkv_ref[rs[i], 2 * W + hd[i] * HEAD_DIM:2 * W + (hd[i] + 1) * HEAD_DIM] for i in HS]
        hsc = [_head_scalars(sc_ref[rs[i], :], gr_ref, hd[i], i // nh) for i in HS]
        beta, gcc, gl, dmat, dmat_t = ([x[i] for x in hsc] for i in range(5))
        ii, jj = hsc[0][5], hsc[0][6]
        eg = [jnp.exp(gcc[h]) for h in HS]
        ekl = [jnp.exp(gl[h] - gcc[h]) for h in HS]
        egl = [jnp.exp(gl[h]) for h in HS]
        kb = [k[h] * beta[h] for h in HS]
        ks = [k[h] * ekl[h] for h in HS]
        do = [do_ref[rs[h], cs[h]] for h in HS]
        vn = [vn_ref[rs[h], cs[h]] for h in HS]
        s = [sp_ref[h // nh, cs[h], :] for h in HS]
        w = [w_ref[rs[h], cs[h]] for h in HS]
        qd = [q[h] * eg[h] for h in HS]

        kq = [_mm_nt(k[h], q[h]) for h in HS]
        p_t = [jnp.where(jj >= ii, kq[h] * dmat_t[h], 0.0) for h in HS]
        ptd = [_mm(p_t[h], do[h]) for h in HS]
        qw = [_cat16([qd[h], -w[h]], 0) for h in HS]
        dsn, dvn, dodv = [None] * len(HS), [None] * len(HS), [None] * len(HS)
        ds_cur = [ds_scr[base + h] for h in range(nh)]
        for ci in reversed(range(cps)):
            IS = range(ci * nh, (ci + 1) * nh)
            ksd = [_mm(ks[i], ds_cur[hl[i]]) for i in IS]
            for i in IS:
                dsn[i] = ds_cur[hl[i]]
                dvn[i] = ptd[i] + ksd[hl[i]]
                dodv[i] = _cat16([do[i], dvn[i]], 0)
            dsq = [_mm_tn(qw[i], dodv[i]) for i in IS]
            ds_cur = [egl[i] * ds_cur[hl[i]] + dsq[hl[i]] for i in IS]
        for h in range(nh):
            ds_scr[base + h] = ds_cur[h]
        x1 = [_mm_nt(dodv[h], s[h]) for h in HS]
        dks = [_mm_nt(vn[h], dsn[h]) for h in HS]
        dov = [_mm_nt(do[h], vn[h]) for h in HS]
        vdo = [_mm_nt(vn[h], do[h]) for h in HS]
        kk = [_mm_nt(kb[h], k[h]) for h in HS]
        qk = [_mm_nt(q[h], k[h]) for h in HS]
        dgl = [egl[h] * jnp.sum(jnp.sum(s[h] * dsn[h], axis=1, keepdims=True), axis=0, keepdims=True) for h in HS]
        dqd = [x1[h][:CHUNK] for h in HS]
        duw = [jnp.concatenate([dvn[h], -x1[h][CHUNK:]], axis=1) for h in HS]
        tdu = [_mm_tn(t_ref[h // nh, hd[h]], duw[h]) for h in HS]
        dvk = [duw[h] + tdu[h] for h in HS]
        uw = [jnp.concatenate([u_ref[rs[h], cs[h]], w[h]], axis=1) for h in HS]
        da = [-jnp.where(ii > jj, _mm_nt(dvk[h], uw[h]), 0.0) for h in HS]
        da_t = [-jnp.where(jj > ii, _mm_nt(uw[h], dvk[h]), 0.0) for h in HS]
        dp = [jnp.where(ii >= jj, dov[h], 0.0) for h in HS]
        dp_t = [jnp.where(jj >= ii, vdo[h], 0.0) for h in HS]
        r1 = [_mm(_cat16([da[h] * dmat[h], dp[h] * dmat[h]], 0), k[h]) for h in HS]
        dk1 = [_mm(_cat16([da_t[h] * dmat_t[h], dp_t[h] * dmat_t[h]], 1), _cat16([kb[h], q[h]], 0)) for h in HS]
        lane = _lanes((CHUNK, LANE))
        for ci in range(cps):
            dsc = jnp.zeros((CHUNK, LANE), f32)
            for i in range(ci * nh, (ci + 1) * nh):
                h = hd[i]
                a = jnp.where(ii > jj, kk[i] * dmat[i], 0.0)
                p = jnp.where(ii >= jj, qk[i] * dmat[i], 0.0)
                gmat = da[i] * a + dp[i] * p
                dvb, dkbg = dvk[i][:, :HEAD_DIM], dvk[i][:, HEAD_DIM:]
                kbg = kb[i] * eg[i]
                dkb = r1[i][:CHUNK] + dkbg * eg[i]
                dq = r1[i][CHUNK:] + dqd[i] * eg[i]
                dk = dk1[i] + dks[i] * ekl[i] + dkb * beta[i]
                dbeta = jnp.sum(dkb * k[i] + dvb * v[i], axis=1, keepdims=True)
                ksum = jnp.sum(dks[i] * ks[i], axis=1, keepdims=True)
                dgl_tot = dgl[i] + jnp.sum(ksum, axis=0, keepdims=True)
                dgc = (jnp.sum(gmat, axis=1, keepdims=True) + jnp.sum(dqd[i] * qd[i] + dkbg * kbg, axis=1, keepdims=True)
                       - ksum)
                dgc = dgc + jnp.where(_rows(dgc.shape) == CHUNK - 1, dgl_tot, 0.0)
                dqkv_ref[rs[i], h * HEAD_DIM:(h + 1) * HEAD_DIM] = dq
                dqkv_ref[rs[i], W + h * HEAD_DIM:W + (h + 1) * HEAD_DIM] = dk
                dqkv_ref[rs[i], 2 * W + h * HEAD_DIM:2 * W + (h + 1) * HEAD_DIM] = dvb * beta[i]
                dsc = jnp.where(lane == h, dbeta, jnp.where(lane == HEADS + h, dgc, dsc))
                dgr_ref[ci, h:h + 1, :] = jnp.sum(gmat, axis=0, keepdims=True)
            dsc_ref[ci * CHUNK:(ci + 1) * CHUNK, :] = dsc

    row = lambda c: (nsteps - 1 - c, 0)
    lead3 = lambda c: (nsteps - 1 - c, 0, 0)
    return _pcall(
        body, name="gdn_bwd", grid=(nsteps,),
        in_specs=[pl.BlockSpec((rows_per_step, 3 * W), row), pl.BlockSpec((rows_per_step, LANE), row),
                  pl.BlockSpec((cps, HEADS, CHUNK), lead3),
                  pl.BlockSpec((rows_per_step, W), row), pl.BlockSpec((rows_per_step, W), row),
                  pl.BlockSpec((rows_per_step, W), row),
                  pl.BlockSpec((cps, HEADS, CHUNK, CHUNK), lambda c: (nsteps - 1 - c, 0, 0, 0)),
                  pl.BlockSpec((cps, W, HEAD_DIM), lead3), pl.BlockSpec((rows_per_step, W), row)],
        out_specs=[pl.BlockSpec((rows_per_step, 3 * W), row), pl.BlockSpec((rows_per_step, LANE), row),
                   pl.BlockSpec((cps, HEADS, CHUNK), lead3)],
        out_shape=[jax.ShapeDtypeStruct((L, 3 * W), f32), jax.ShapeDtypeStruct((L, LANE), f32),
                   jax.ShapeDtypeStruct((nc, HEADS, CHUNK), f32)],
        scratch_shapes=[pltpu.VMEM((HEADS, HEAD_DIM, HEAD_DIM), f32)],
        compiler_params=_cparams("arbitrary"),
    )(qkv, sc, gr, u_all, w_all, vn_all, t_all, sp_all, do_all)


def _qkv_bwd(proj, cw, dn, dproj):
    L = proj.shape[0]

    def body(x_ref, cw_ref, dn_ref, dproj_in, dx_ref, gcw_ref):
        j = pl.program_id(0)
        steps = GDN_WIDTH // ELT_W
        scale = jnp.where(j < steps, HEAD_DIM ** -0.5, 1.0).astype(f32)
        for ls in HALVES:
            x, dn_v = x_ref[:, ls], dn_ref[:, ls]
            c = _conv4(x, cw_ref, ls)
            sg = _sigmoid(c)
            a = c * sg
            rn = lax.rsqrt(jnp.sum(a * a, axis=1, keepdims=True) + EPS)
            da_n = (scale * rn) * (dn_v - a * ((rn * rn) * jnp.sum(dn_v * a, axis=1, keepdims=True)))
            da = jnp.where(j < 2 * steps, da_n, dn_v)
            dc = da * (sg * (1.0 + c * (1.0 - sg)))
            dc1, dc2, dc3 = _shift_up(dc, 1), _shift_up(dc, 2), _shift_up(dc, 3)
            gcw_ref[3:4, ls] = jnp.sum(dc * x, axis=0, keepdims=True)
            gcw_ref[2:3, ls] = jnp.sum(dc1 * x, axis=0, keepdims=True)
            gcw_ref[1:2, ls] = jnp.sum(dc2 * x, axis=0, keepdims=True)
            gcw_ref[0:1, ls] = jnp.sum(dc3 * x, axis=0, keepdims=True)
            dx = cw_ref[3:4, ls] * dc + cw_ref[2:3, ls] * dc1 + cw_ref[1:2, ls] * dc2 + cw_ref[0:1, ls] * dc3
            dx_ref[:, ls] = dx.astype(bf16)

    col = pl.BlockSpec((L, ELT_W), lambda j: (0, j))
    wspec = pl.BlockSpec((4, ELT_W), lambda j: (0, j))
    return _pcall(
        body, name="qkv_bwd", grid=(3 * GDN_WIDTH // ELT_W,),
        in_specs=[col, wspec, col, ANY], out_specs=[col, wspec],
        out_shape=[jax.ShapeDtypeStruct(dproj.shape, dproj.dtype), jax.ShapeDtypeStruct((4, 3 * GDN_WIDTH), f32)],
        input_output_aliases={3: 0},
        compiler_params=_cparams("parallel"),
    )(proj, cw, dn, dproj)


def _scalars_bwd(proj, alog_p, dtb_p, dsc, dgr_col, dproj, after):
    L = proj.shape[0]

    def body(x_ref, al_ref, dt_ref, dsc_ref, dgr_ref, dproj_in, after_ref, dba_ref, gs_ref):
        x, dsc_v = x_ref[...], dsc_ref[...]
        lane = _lanes(x.shape)
        dec = (lane >= HEADS) & (lane < 2 * HEADS)
        dg = jnp.where(dec, dsc_v - dgr_ref[...], 0.0)
        rc = _rows(x.shape) & (CHUNK - 1)
        for s in (1, 2, 4, 8, 16, 32):
            dg = dg + jnp.where(rc + s < CHUNK, pltpu.roll(dg, L - s, 0), 0.0)
        xa = x + dt_ref[...]
        ea = jnp.exp(al_ref[...])
        g = -ea * _softplus(xa)
        da = dg * (-ea) * _sigmoid(xa)
        beta = _sigmoid(x)
        db = dsc_v * beta * (1.0 - beta)
        dba_ref[:, :LANE] = jnp.where(lane < HEADS, db, jnp.where(dec, da, 0.0)).astype(bf16)
        dba_ref[:, LANE:] = jnp.zeros((L, ELT_W - LANE), bf16)
        g_al = jnp.sum(jnp.where(dec, dg * g, 0.0), axis=0, keepdims=True)
        g_dt = jnp.sum(jnp.where(dec, da, 0.0), axis=0, keepdims=True)
        row8 = _rows(gs_ref.shape)
        gs = jnp.where(row8 == 0, g_al, jnp.where(row8 == 1, g_dt, 0.0))
        gs_ref[...] = pltpu.roll(gs, LANE - HEADS, 1)

    full = pl.BlockSpec((L, LANE), lambda i: (0, 0))
    vec = pl.BlockSpec((1, LANE), lambda i: (0, 0))
    return _pcall(
        body, name="scalars_bwd", grid=(1,),
        in_specs=[pl.BlockSpec((L, LANE), lambda i: (0, OFF_BA // LANE)), vec, vec, full, full, ANY, ANY],
        out_specs=[pl.BlockSpec((L, ELT_W), lambda i: (0, OFF_BA // ELT_W)), pl.BlockSpec((8, LANE), lambda i: (0, 0))],
        out_shape=[jax.ShapeDtypeStruct(dproj.shape, dproj.dtype), jax.ShapeDtypeStruct((8, LANE), f32)],
        input_output_aliases={5: 0},
        compiler_params=_cparams("arbitrary"),
    )(proj, alog_p, dtb_p, dsc, dgr_col, dproj, after)


def _input_grad(dproj, wpad, x, nw, dy, after):
    L = x.shape[0]
    tm = min(512, L)
    cuts = (0, 3072, 5120, 7168, PROJ_PAD)
    nk = len(cuts) - 1

    def body(dp_ref, w_hbm, x_ref, nw_ref, dy_ref, after_ref, gx_ref, gnw_ref, w_vmem, sems):
        first = pl.program_id(0) == 0
        loads = [pltpu.make_async_copy(w_hbm.at[cuts[k]:cuts[k + 1], :], w_vmem.at[cuts[k]:cuts[k + 1], :], sems.at[k])
                 for k in range(nk)]

        @pl.when(first)
        def _():
            for cp in loads:
                cp.start()
            gnw_ref[...] = jnp.zeros_like(gnw_ref)
        dh = None
        for k in range(nk):
            pl.when(first)(loads[k].wait)
            part = jnp.dot(dp_ref[:, cuts[k]:cuts[k + 1]], w_vmem[cuts[k]:cuts[k + 1], :], preferred_element_type=f32)
            dh = part if dh is None else dh + part
        xv, nwv = x_ref[...], nw_ref[...]
        r = lax.rsqrt(jnp.mean(xv * xv, axis=-1, keepdims=True) + EPS)
        xh = xv * r
        gnw_ref[...] += jnp.sum(dh * xh, axis=0, keepdims=True)
        dxh = dh * nwv
        gx_ref[...] = dy_ref[...] + r * (dxh - xh * jnp.mean(dxh * xh, axis=-1, keepdims=True))

    row = lambda i: (i, 0)
    fix = lambda i: (0, 0)
    return _pcall(
        body, name="input_grad", grid=(L // tm,),
        in_specs=[pl.BlockSpec((tm, PROJ_PAD), row), ANY, pl.BlockSpec((tm, D_MODEL), row),
                  pl.BlockSpec((1, D_MODEL), fix), pl.BlockSpec((tm, D_MODEL), row), ANY],
        out_specs=[pl.BlockSpec((tm, D_MODEL), row), pl.BlockSpec((1, D_MODEL), fix)],
        out_shape=[jax.ShapeDtypeStruct((L, D_MODEL), f32), jax.ShapeDtypeStruct((1, D_MODEL), f32)],
        scratch_shapes=[pltpu.VMEM(wpad.shape, bf16), pltpu.SemaphoreType.DMA((nk,))],
        compiler_params=_cparams("arbitrary"),
    )(dproj, wpad, x, nw, dy, after)


def _adamw_reduce(parts, w, m, v, name):
    R, C = w.shape
    n_parts = parts.shape[0]
    tr = 128 if R % 128 == 0 else R
    c1 = 1.0 - ADAM_B1 ** ADAM_STEP
    c2 = 1.0 - ADAM_B2 ** ADAM_STEP

    def body(p_ref, w_ref, m_ref, v_ref, g_ref, d_ref, nm_ref, nv_ref):
        g = p_ref[0].astype(f32)
        for s in range(1, n_parts):
            g = g + p_ref[s].astype(f32)
        nm = ADAM_B1 * m_ref[...] + (1.0 - ADAM_B1) * g
        nv = ADAM_B2 * v_ref[...] + (1.0 - ADAM_B2) * (g * g)
        g_ref[...] = g
        nm_ref[...] = nm
        nv_ref[...] = nv
        d_ref[...] = -ADAM_LR * ((nm / c1) / (jnp.sqrt(nv / c2) + ADAM_EPS) + ADAM_WD * w_ref[...])

    blk = pl.BlockSpec((tr, C), lambda i: (i, 0))
    out = jax.ShapeDtypeStruct((R, C), f32)
    return _pcall(
        body, name=name, grid=(R // tr,),
        in_specs=[pl.BlockSpec((n_parts, tr, C), lambda i: (0, i, 0)), blk, blk, blk],
        out_specs=[blk] * 4, out_shape=[out] * 4,
        compiler_params=_cparams("parallel"),
    )(parts, w, m, v)


SMALL_SLOTS = ((0, D_MODEL), (D_MODEL, D_MODEL), (2 * D_MODEL, D_MODEL), (3 * D_MODEL, LANE),
               (3 * D_MODEL + LANE, HEADS), (3 * D_MODEL + 2 * LANE, HEADS))
SMALL_LOSS = 3 * D_MODEL + 3 * LANE
SMALL_W = SMALL_LOSS + LANE


def _pack_small(gs, after):
    def body(nw_ref, cb_ref, fw_ref, gn_ref, sc_ref, ls_ref, after_ref, o_ref):
        for ref, (start, width) in zip((nw_ref, cb_ref, fw_ref, gn_ref), SMALL_SLOTS[:4]):
            o_ref[:, start:start + width] = ref[...]
        o_ref[:, SMALL_SLOTS[4][0]:SMALL_SLOTS[4][0] + LANE] = sc_ref[0:1, :]
        o_ref[:, SMALL_SLOTS[5][0]:SMALL_SLOTS[5][0] + LANE] = sc_ref[1:2, :]
        o_ref[:, SMALL_LOSS:SMALL_W] = ls_ref[...]

    vm = pl.BlockSpec(memory_space=pltpu.VMEM)
    return _pcall(body, name="pack_small_grads", out_shape=jax.ShapeDtypeStruct((1, SMALL_W), f32),
                  in_specs=[vm] * 6 + [ANY], out_specs=vm)(*gs, after)


def _adamw_small(parts, ws, ms, vs):
    c1 = 1.0 - ADAM_B1 ** ADAM_STEP
    c2 = 1.0 - ADAM_B2 ** ADAM_STEP
    np_ = len(ws)

    def body(*refs):
        p_ref = refs[0]
        w_refs, m_refs, v_refs = refs[1:1 + np_], refs[1 + np_:1 + 2 * np_], refs[1 + 2 * np_:1 + 3 * np_]
        outs = refs[1 + 3 * np_:]
        g_refs, d_refs, nm_refs, nv_refs = (outs[i * np_:(i + 1) * np_] for i in range(4))
        loss_ref = outs[4 * np_]

        def total(start, width):
            t = p_ref[0, :, start:start + width]
            for s in range(1, N_DEV):
                t = t + p_ref[s, :, start:start + width]
            return t

        for i, (start, width) in enumerate(SMALL_SLOTS):
            g = total(start, width)
            nm = ADAM_B1 * m_refs[i][...] + (1.0 - ADAM_B1) * g
            nv = ADAM_B2 * v_refs[i][...] + (1.0 - ADAM_B2) * (g * g)
            g_refs[i][...] = g
            nm_refs[i][...] = nm
            nv_refs[i][...] = nv
            d_refs[i][...] = -ADAM_LR * ((nm / c1) / (jnp.sqrt(nv / c2) + ADAM_EPS) + ADAM_WD * w_refs[i][...])
        loss_ref[...] = total(SMALL_LOSS, LANE)

    vm = pl.BlockSpec(memory_space=pltpu.VMEM)
    shapes = [jax.ShapeDtypeStruct(w.shape, f32) for w in ws]
    res = _pcall(body, name="adamw_small", out_shape=shapes * 4 + [jax.ShapeDtypeStruct((1, LANE), f32)],
                 in_specs=[vm] * (1 + 3 * np_), out_specs=[vm] * (4 * np_ + 1))(parts, *ws, *ms, *vs)
    return [res[i * np_:(i + 1) * np_] for i in range(4)], res[4 * np_]


def _adamw_w_in(part_a, part_b, w3, m3, v3, after):
    _, n, _ = part_a.shape
    c1 = 1.0 - ADAM_B1 ** ADAM_STEP
    c2 = 1.0 - ADAM_B2 ** ADAM_STEP

    def body(pa_ref, pb_ref, w_ref, m_ref, v_ref, after_ref, g_ref, d_ref, nm_ref, nv_ref):
        g = pa_ref[0].astype(f32) + pb_ref[0].astype(f32)
        nm = ADAM_B1 * m_ref[:, 0, :] + (1.0 - ADAM_B1) * g
        nv = ADAM_B2 * v_ref[:, 0, :] + (1.0 - ADAM_B2) * (g * g)
        g_ref[:, 0, :] = g
        nm_ref[:, 0, :] = nm
        nv_ref[:, 0, :] = nv
        d_ref[:, 0, :] = -ADAM_LR * ((nm / c1) / (jnp.sqrt(nv / c2) + ADAM_EPS) + ADAM_WD * w_ref[:, 0, :])

    tile = 2 * COL_TILE
    blk = pl.BlockSpec((n, 1, tile), lambda j: (0, 0, j))
    out = jax.ShapeDtypeStruct((n, 1, D_MODEL), f32)
    return _pcall(
        body, name="adamw_w_in", grid=(D_MODEL // tile,),
        in_specs=[pl.BlockSpec((1, n, tile), lambda j: (0, 0, j))] * 2 + [blk, blk, blk, ANY],
        out_specs=[blk] * 4, out_shape=[out] * 4,
        compiler_params=_cparams("parallel"),
    )(part_a, part_b, w3, m3, v3, after)


def _pad_lanes(vec8, start):
    return jnp.pad(vec8.reshape(1, -1), ((0, 0), (start, LANE - start - vec8.size)))


def kernel(x, norm_in_w, w_in, conv_qkv_w, A_log, dt_bias, gdn_norm_w, conv_w, conv_b, w_out, final_norm_w, loss_target, m_norm_in_w, m_w_in, m_conv_qkv_w, m_A_log, m_dt_bias, m_gdn_norm_w, m_conv_w, m_conv_b, m_w_out, m_final_norm_w, v_norm_in_w, v_w_in, v_conv_qkv_w, v_A_log, v_dt_bias, v_gdn_norm_w, v_conv_w, v_conv_b, v_w_out, v_final_norm_w):
    L = x.shape[1]
    nc = L // CHUNK
    xs = x[0]
    tgt = loss_target[0]
    fnw = final_norm_w.reshape(1, D_MODEL)

    as_rows = lambda a: jnp.transpose(a, (2, 0, 1))
    win_g, cqkv_g, cw_g = _all_gather([_cast_w_in(as_rows(w_in)), conv_qkv_w[0], conv_w[0]], "gather_weights",
                                      pieces=[4, 1, 1])
    wpad = _relayout_w_in(win_g)
    cqkv = jnp.concatenate([cqkv_g[d] for d in range(N_DEV)], axis=1)
    cw = jnp.concatenate([cw_g[d] for d in range(N_DEV)], axis=1)
    alog_p = _pad_lanes(A_log, HEADS)
    dtb_p = _pad_lanes(dt_bias, HEADS)
    me_flat = _flat(*_mesh_pos())
    tok = lambda started: started[4]
    wo_started = _spread_start(w_out[0].astype(bf16), wpad, "gather", "gather_w_out_start")

    proj, h = _in_proj(xs, norm_in_w, wpad, tok(wo_started))
    qkv = _qkv_act(proj, cqkv)
    sc, gr = _scalars(proj, alog_p, dtb_p)
    o, u_all, w_all, vn_all, t_all, sp_all = _gdn_fwd(qkv, sc, gr)
    mix = _conv_fwd(proj, cw, conv_b, _gdn_gate(o, proj, gdn_norm_w))
    wo = _spread_finish(wo_started, mix, "gather", "gather_w_out_wait", me_flat).reshape(-1, D_MODEL)
    dy, dyb, dmix, g_fnw, loss_v = _out_proj_loss(xs, mix, wo, fnw, tgt)

    g_wout = _tn_matmul(mix, dyb, "grad_w_out")
    gwo_started = _spread_start(g_wout.reshape(N_DEV, -1, D_MODEL), dyb, "scatter", "exchange_grad_w_out_start")
    do, dproj, g_gnw = _gdn_gate_bwd(o, proj, gdn_norm_w, dmix, tok(gwo_started))
    dproj, g_cw, g_cb = _conv_bwd(proj, cw, conv_b, dmix, dproj)
    dqkv_n, dsc, dgr = _gdn_bwd(qkv, sc, gr, u_all, w_all, vn_all, t_all, sp_all, do)
    dproj, g_cqkv = _qkv_bwd(proj, cqkv, dqkv_n, dproj)
    g_cqkv_blk = g_cqkv.reshape(4, N_DEV, -1).transpose(1, 0, 2)
    g_cw_blk = jnp.pad(g_cw.reshape(3, N_DEV, -1).transpose(1, 0, 2),
                       ((0, 0), (0, 1), (0, g_cqkv_blk.shape[2] - g_cw.shape[1] // N_DEV)))
    gsm_started = _spread_start(jnp.concatenate([g_cqkv_blk, g_cw_blk], axis=1), g_cqkv, "scatter",
                                "exchange_small_sharded_grads_start")
    dgr_col = jnp.pad(dgr.transpose(0, 2, 1).reshape(L, HEADS), ((0, 0), (HEADS, LANE - 2 * HEADS)))
    dproj, g_sc = _scalars_bwd(proj, alog_p, dtb_p, dsc, dgr_col, dproj, tok(gsm_started))
    g_win_blk = _grad_blocks(_tn_matmul(dproj, h, "grad_w_in"))

    (p_win,) = _pair_exchange([g_win_blk], "exchange_grads_pair")
    r_small = _spread_finish(gsm_started, p_win, "scatter", "exchange_small_sharded_grads_wait", me_flat)
    r_cqkv, r_cw = r_small[:, :4, :], r_small[:, 4:7, :g_cw.shape[1] // N_DEV]
    s_win = _pair_sum(g_win_blk, p_win, "pair_sum_w_in")
    gw1_started = _spread_start(s_win, r_small, "axis_a", "exchange_grads_axis1_start")
    grad_x, g_nw = _input_grad(dproj, wpad, xs, norm_in_w, dy, tok(gw1_started))
    s_thru, got1 = _spread_wait(gw1_started, grad_x, "axis_a", "exchange_grads_axis1_wait")
    t_win = _axis_sum(s_thru, got1, "axis_sum_w_in")
    gw2_started = _spread_start(t_win, got1, "axis_b", "exchange_grads_axis2_start")

    r_wout = _spread_finish(gwo_started, tok(gw2_started), "scatter", "exchange_grad_w_out_wait", me_flat)
    upd_wout =_adamw_reduce(r_wout, w_out[0], m_w_out[0], v_w_out[0], "adamw_w_out")
    upd_cqkv = _adamw_reduce(r_cqkv, conv_qkv_w[0], m_conv_qkv_w[0], v_conv_qkv_w[0], "adamw_conv_qkv_w")
    upd_cw = _adamw_reduce(r_cw, conv_w[0], m_conv_w[0], v_conv_w[0], "adamw_conv_w")

    t_thru, got2 = _spread_wait(gw2_started, upd_cw[0], "axis_b", "exchange_grads_axis2_wait")

    small_g = _pack_small([g_nw, g_cb, g_fnw, g_gnw, g_sc, loss_v], got2)
    gsg_started = _spread_start(small_g, got2, "gather", "gather_small_grads_start")
    upd_win_t = _adamw_w_in(t_thru, got2, as_rows(w_in), as_rows(m_w_in), as_rows(v_w_in), tok(gsg_started))
    upd_win = [jnp.transpose(a, (1, 2, 0)) for a in upd_win_t]
    small_all = _spread_finish(gsg_started, upd_win_t[0], "gather", "gather_small_grads_wait", me_flat)
    fvec = lambda a: a.reshape(1, D_MODEL)
    upd_small, loss_sum = _adamw_small(
        small_all,
        [norm_in_w, conv_b, fvec(final_norm_w), gdn_norm_w, A_log, dt_bias],
        [m_norm_in_w, m_conv_b, fvec(m_final_norm_w), m_gdn_norm_w, m_A_log, m_dt_bias],
        [v_norm_in_w, v_conv_b, fvec(v_final_norm_w), v_gdn_norm_w, v_A_log, v_dt_bias])

    outs = [loss_sum[0, 0], grad_x[None]]
    for k in range(4):
        nw_k, cb_k, fw_k, gn_k, al_k, dt_k = upd_small[k]
        outs += [nw_k, upd_win[k], upd_cqkv[k][None], al_k, dt_k, gn_k,
                 upd_cw[k][None], cb_k, upd_wout[k][None], fw_k.reshape(D_MODEL)]
    return tuple(outs)
```

```python
import jax
import jax.numpy as jnp
from jax import lax
from jax.experimental import pallas as pl
from jax.experimental.pallas import tpu as pltpu

f32 = jnp.float32
bf16 = jnp.bfloat16

N_DEV = 8
D_MODEL = 1024
HEADS = 8
HEAD_DIM = 128
CHUNK = 64
GDN_CPS = 4
GDN_CPS_BWD = 1
GDN_WIDTH = HEADS * HEAD_DIM
CONV_WIDTH = 1024
PROJ_WIDTH = 8208
SHARD_W = PROJ_WIDTH // N_DEV
EPS = 1e-6

LANE = 128
ELT_W = 256

OFF_QKV, OFF_ZG, OFF_CONV, OFF_BA = 0, 3072, 4096, 8192
CONV_BLOCK = 4 * ELT_W
PROJ_PAD = 8448
NAT_BA, NAT_CONV = 4096, 4112


def _padded_col(n):
    if n < NAT_BA:
        return n
    if n < NAT_CONV:
        return OFF_BA + n - NAT_BA
    g, ch = divmod(n - NAT_CONV, CONV_WIDTH)
    j, r = divmod(ch, ELT_W)
    return OFF_CONV + CONV_BLOCK * j + ELT_W * g + r


def _layout_segments(n0, n1):
    cuts = [NAT_BA, NAT_CONV] + [NAT_CONV + ELT_W * k for k in range(1, 4 * CONV_WIDTH // ELT_W)]
    pts = [n0] + [c for c in cuts if n0 < c < n1] + [n1]
    return [(lo, hi - lo, _padded_col(lo)) for lo, hi in zip(pts, pts[1:])]

ADAM_LR, ADAM_B1, ADAM_B2, ADAM_EPS, ADAM_WD, ADAM_STEP = 0.001, 0.9, 0.999, 1e-08, 0.01, 10

V7X_VMEM_BYTES = 64 * 1024 * 1024
VMEM_LIMIT = V7X_VMEM_BYTES - 8 * 1024 * 1024

MESH = pl.DeviceIdType.MESH
ANY = pl.BlockSpec(memory_space=pl.ANY)


def _pcall(body, **kw):
    return pl.pallas_call(body, **kw)


def _cparams(*sem):
    return pltpu.CompilerParams(dimension_semantics=sem if sem else None, vmem_limit_bytes=VMEM_LIMIT)


def _mm(a, b):
    return jnp.dot(a.astype(bf16), b.astype(bf16), preferred_element_type=f32)


def _mm_nt(a, b):
    return lax.dot_general(a.astype(bf16), b.astype(bf16), (((1,), (1,)), ((), ())), preferred_element_type=f32)


def _cat16(parts, axis):
    return jnp.concatenate([p.astype(bf16) for p in parts], axis=axis)


def _mm_tn(a, b):
    return lax.dot_general(a.astype(bf16), b.astype(bf16), (((0,), (0,)), ((), ())), preferred_element_type=f32)


def _rows(shape):
    return lax.broadcasted_iota(jnp.int32, shape, 0)


def _lanes(shape):
    return lax.broadcasted_iota(jnp.int32, shape, 1)


def _shift_down(x, s):
    if s == 0:
        return x
    return jnp.where(_rows(x.shape) >= s, pltpu.roll(x, s, 0), 0.0)


def _shift_up(x, s):
    if s == 0:
        return x
    n = x.shape[0]
    return jnp.where(_rows(x.shape) < n - s, pltpu.roll(x, n - s, 0), 0.0)


def _sigmoid(x):
    return jax.nn.sigmoid(x)


def _softplus(x):
    e = jnp.exp(-jnp.abs(x))
    small = e * (1.0 - e * (0.5 - e * (1.0 / 3.0)))
    return jnp.maximum(x, 0.0) + jnp.where(e < 0.01, small, jnp.log(1.0 + e))


def _mesh_pos():
    return lax.axis_index("x"), lax.axis_index("y"), lax.axis_index("c")


def _flat(px, py, pc):
    return 4 * px + 2 * py + pc


def _all_gather(xs, name, pieces=None):
    n = len(xs)
    pieces = pieces or [1] * n
    items = [(a, q) for a in range(n) for q in range(pieces[a])]
    ni = len(items)

    def view(ref, i):
        a, q = items[i]
        if pieces[a] == 1:
            return ref
        wd = xs[a].shape[-1] // pieces[a]
        return ref.at[(slice(None),) * (xs[a].ndim - 1) + (pl.ds(q * wd, wd),)]

    def body(*refs):
        x_refs, o_refs = refs[:n], refs[n:2 * n]
        send_sems, recv_sems, local_sems = refs[2 * n:]
        x, y, c = _mesh_pos()
        me, sibling = (x, y, c), (x, y, 1 - c)
        flip = lambda v, bit: v + bit - 2 * v * bit
        nbr_a = (flip(x, 1 - c), flip(y, c))
        nbr_b = (flip(x, c), flip(y, 1 - c))
        diag = (1 - x, 1 - y)

        def copy(i, k, block, to, own=False):
            a = items[i][0]
            dst = view(o_refs[a].at[_flat(*block)], i)
            return pltpu.make_async_remote_copy(
                src_ref=view(x_refs[a], i) if own else dst, dst_ref=dst,
                send_sem=send_sems.at[i, k], recv_sem=recv_sems.at[i, k], device_id=to, device_id_type=MESH)

        mine, sent = [], []

        def go(cp):
            cp.start()
            sent.append(cp)

        for a in range(n):
            cp = pltpu.make_async_copy(x_refs[a], o_refs[a].at[_flat(*me)], local_sems.at[a])
            cp.start()
            mine.append(cp)
        for a in range(ni):
            go(copy(a, 1, me, (*nbr_a, c), own=True))
            go(copy(a, 2, me, (*nbr_b, c), own=True))
            go(copy(a, 0, me, sibling, own=True))
        for a in range(ni):
            copy(a, 1, (*nbr_a, c), me).wait_recv()
            go(copy(a, 3, (*nbr_a, c), (*nbr_b, c)))
            go(copy(a, 4, (*nbr_a, c), sibling))
        for a in range(ni):
            copy(a, 2, (*nbr_b, c), me).wait_recv()
            go(copy(a, 5, (*nbr_b, c), sibling))
        for a in range(ni):
            copy(a, 3, (*diag, c), me).wait_recv()
            go(copy(a, 6, (*diag, c), sibling))
        for a in range(ni):
            copy(a, 0, sibling, me).wait_recv()
            copy(a, 4, (*nbr_b, 1 - c), me).wait_recv()
            copy(a, 5, (*nbr_a, 1 - c), me).wait_recv()
            copy(a, 6, (*diag, 1 - c), me).wait_recv()
        for cp in sent:
            cp.wait_send()
        for cp in mine:
            cp.wait()

    outs = _pcall(
        body, name=name,
        out_shape=[jax.ShapeDtypeStruct((N_DEV,) + a.shape, a.dtype) for a in xs],
        in_specs=[ANY] * n, out_specs=[ANY] * n,
        scratch_shapes=[pltpu.SemaphoreType.DMA((ni, 7)), pltpu.SemaphoreType.DMA((ni, 7)), pltpu.SemaphoreType.DMA((n,))],
    )(*xs)
    return list(outs)


def _pair_exchange(gs, name):
    n = len(gs)
    chips = [(0, 0), (0, 1), (1, 0), (1, 1)]

    def body(*refs):
        g_refs, o_refs = refs[:n], refs[n:2 * n]
        send_sems, recv_sems = refs[2 * n:]
        x, y, c = _mesh_pos()
        sibling = (x, y, 1 - c)

        def copy(a, i):
            xp, yp = chips[i]
            return pltpu.make_async_remote_copy(
                src_ref=g_refs[a].at[_flat(xp, yp, 1 - c)], dst_ref=o_refs[a].at[i],
                send_sem=send_sems.at[a, i], recv_sem=recv_sems.at[a, i], device_id=sibling, device_id_type=MESH)

        cps = [copy(a, i) for a in range(n) for i in range(4)]
        for cp in cps:
            cp.start()
        for cp in cps:
            cp.wait()

    outs = _pcall(
        body, name=name,
        out_shape=[jax.ShapeDtypeStruct((4,) + a.shape[1:], a.dtype) for a in gs],
        in_specs=[ANY] * n, out_specs=[ANY] * n,
        scratch_shapes=[pltpu.SemaphoreType.DMA((n, 4)), pltpu.SemaphoreType.DMA((n, 4))],
    )(*gs)
    return list(outs)


def _pair_sum(g, p1, name):
    _, R, C = g.shape
    tr = 256 if R % 256 == 0 else R
    cidx = lax.axis_index("c").astype(jnp.int32).reshape(1)

    def body(c_ref, g_ref, p_ref, o_ref):
        o_ref[...] = (g_ref[...].astype(f32) + p_ref[...].astype(f32)).astype(o_ref.dtype)

    return _pcall(
        body, name=name,
        grid_spec=pltpu.PrefetchScalarGridSpec(
            num_scalar_prefetch=1, grid=(4, R // tr),
            in_specs=[pl.BlockSpec((1, tr, C), lambda i, r, c_ref: (2 * i + c_ref[0], r, 0)),
                      pl.BlockSpec((1, tr, C), lambda i, r, c_ref: (i, r, 0))],
            out_specs=pl.BlockSpec((1, tr, C), lambda i, r, c_ref: (i, r, 0))),
        out_shape=jax.ShapeDtypeStruct((4, R, C), g.dtype),
        compiler_params=_cparams("parallel", "parallel"),
    )(cidx, g, p1)


def _axis_sum(s, got, name):
    _, R, C = s.shape
    x, y, c = _mesh_pos()
    me, _, b, _ = _axis_chips(x, y, c)
    idx = jnp.stack([2 * me[0] + me[1], 2 * b[0] + b[1]]).astype(jnp.int32)

    def body(idx_ref, s_ref, g_ref, o_ref):
        o_ref[...] = (s_ref[...].astype(f32) + g_ref[...].astype(f32)).astype(o_ref.dtype)

    return _pcall(
        body, name=name,
        grid_spec=pltpu.PrefetchScalarGridSpec(
            num_scalar_prefetch=1, grid=(2,),
            in_specs=[pl.BlockSpec((1, R, C), lambda k, idx_ref: (idx_ref[k], 0, 0)),
                      pl.BlockSpec((1, R, C), lambda k, idx_ref: (k, 0, 0))],
            out_specs=pl.BlockSpec((1, R, C), lambda k, idx_ref: (k, 0, 0))),
        out_shape=jax.ShapeDtypeStruct((2, R, C), s.dtype),
        compiler_params=_cparams("parallel"),
    )(idx, s, got)


HBM = pl.BlockSpec(memory_space=pltpu.HBM)
SEM = pl.BlockSpec(memory_space=pltpu.SEMAPHORE)
EFFECT = pltpu.SideEffectType.DATAFLOW_SIDE_EFFECTING


def _peers(x, y, c):
    out = []
    for k in range(1, N_DEV):
        kx, ky, kc = (k >> 2) & 1, (k >> 1) & 1, k & 1
        out.append(((1 - x) if kx else x, (1 - y) if ky else y, (1 - c) if kc else c))
    return out


SPREAD_COPIES = {"gather": N_DEV - 1, "scatter": N_DEV - 1, "axis_a": 2, "axis_b": 1}
SPREAD_SLOTS = {"axis_a": 2, "axis_b": 1}


def _axis_chips(x, y, c):
    flip = lambda v, bit: v + bit - 2 * v * bit
    return (x, y), (flip(x, 1 - c), flip(y, c)), (flip(x, c), flip(y, 1 - c)), (1 - x, 1 - y)


def _spread_copy(src_ref, land_ref, send_sems, recv_sems, k, plan):
    x, y, c = _mesh_pos()
    if plan in ("axis_a", "axis_b"):
        _, a, b, d = _axis_chips(x, y, c)
        chip = lambda p: 2 * p[0] + p[1]
        peer = (*(a if plan == "axis_a" else b), c)
        src = src_ref.at[chip(a) if k == 0 else chip(d)] if plan == "axis_a" else src_ref.at[1]
        slot = k
    else:
        peer = _peers(x, y, c)[k]
        src, slot = (src_ref.at[_flat(*peer)] if plan == "scatter" else src_ref), _flat(x, y, c)
    return pltpu.make_async_remote_copy(
        src_ref=src, dst_ref=land_ref.at[slot], send_sem=send_sems.at[k], recv_sem=recv_sems.at[k],
        device_id=peer, device_id_type=MESH)


def _own_copy(src_ref, land_ref, send_sems, plan):
    me = _flat(*_mesh_pos())
    return pltpu.make_async_copy(src_ref.at[me] if plan == "scatter" else src_ref, land_ref.at[me],
                                 send_sems.at[SPREAD_COPIES[plan]])


def _spread_start(src, after, plan, name):
    land_shape = (N_DEV,) + src.shape if plan == "gather" else src.shape
    if plan in SPREAD_SLOTS:
        land_shape = (SPREAD_SLOTS[plan],) + src.shape[1:]
    n_copies = SPREAD_COPIES[plan]

    def body(src_ref, land_ref, after_ref, send_sems, recv_sems, src_thru, land_thru, token):
        for k in range(n_copies):
            _spread_copy(src_ref, land_ref, send_sems, recv_sems, k, plan).start()
        if plan not in SPREAD_SLOTS:
            _own_copy(src_ref, land_ref, send_sems, plan).start()
        token[...] = jnp.zeros_like(token)

    return _pcall(
        body, name=name,
        out_shape=(pltpu.SemaphoreType.DMA((n_copies + (plan not in SPREAD_SLOTS),)), pltpu.SemaphoreType.DMA((n_copies,)),
                   pltpu.HBM(src.shape, src.dtype), pltpu.HBM(land_shape, src.dtype), jax.ShapeDtypeStruct((8, LANE), f32)),
        in_specs=(HBM, HBM, ANY), out_specs=(SEM, SEM, HBM, HBM, pl.BlockSpec(memory_space=pltpu.VMEM)),
        input_output_aliases={0: 2, 1: 3},
        compiler_params=pltpu.CompilerParams(has_side_effects=EFFECT),
    )(pltpu.with_memory_space_constraint(src, pltpu.HBM),
      pltpu.with_memory_space_constraint(lax.empty(land_shape, src.dtype), pltpu.HBM), after)


def _spread_wait(started, after, plan, name):
    send_sems, recv_sems, src_thru, land_thru, _ = started

    def body(src_ref, land_ref, send_sems, recv_sems, after_ref, src_dead, got_ref):
        for k in range(SPREAD_COPIES[plan]):
            cp = _spread_copy(src_ref, land_ref, send_sems, recv_sems, k, plan)
            cp.wait_send()
            cp.wait_recv()
        if plan not in SPREAD_SLOTS:
            _own_copy(src_ref, land_ref, send_sems, plan).wait()

    return _pcall(
        body, name=name,
        out_shape=(pltpu.HBM(src_thru.shape, src_thru.dtype), pltpu.HBM(land_thru.shape, land_thru.dtype)),
        in_specs=(HBM, HBM, SEM, SEM, ANY), out_specs=(HBM, HBM), input_output_aliases={0: 0, 1: 1},
        compiler_params=pltpu.CompilerParams(has_side_effects=EFFECT),
    )(src_thru, land_thru, send_sems, recv_sems, after)


COL_TILE = 256


def _cast_w_in(w3):
    n = w3.shape[0]

    def body(w_ref, o_ref):
        o_ref[...] = w_ref[:, 0, :].astype(bf16)

    tile = 2 * COL_TILE
    return _pcall(
        body, name="cast_w_in", grid=(D_MODEL // tile,),
        in_specs=[pl.BlockSpec((n, 1, tile), lambda j: (0, 0, j))],
        out_specs=pl.BlockSpec((n, tile), lambda j: (0, j)),
        out_shape=jax.ShapeDtypeStruct((n, D_MODEL), bf16),
        compiler_params=_cparams("parallel"),
    )(w3)


def _relayout_w_in(win_g):
    def body(g_ref, o_ref):
        used = OFF_BA + NAT_CONV - NAT_BA
        o_ref[used:PROJ_PAD, :] = jnp.zeros((PROJ_PAD - used, COL_TILE), o_ref.dtype)
        for d in range(N_DEV):
            for lo, width, dst in _layout_segments(d * SHARD_W, (d + 1) * SHARD_W):
                src = lo - d * SHARD_W
                o_ref[dst:dst + width, :] = g_ref[d, src:src + width, :]

    return _pcall(
        body, name="relayout_w_in", grid=(D_MODEL // COL_TILE,),
        in_specs=[pl.BlockSpec((N_DEV, SHARD_W, COL_TILE), lambda j: (0, 0, j))],
        out_specs=pl.BlockSpec((PROJ_PAD, COL_TILE), lambda j: (0, j)),
        out_shape=jax.ShapeDtypeStruct((PROJ_PAD, D_MODEL), win_g.dtype),
        compiler_params=_cparams("parallel"),
    )(win_g)


def _grad_blocks(g_t):
    def body(p_ref, o_ref):
        for d in range(N_DEV):
            for lo, width, src in _layout_segments(d * SHARD_W, (d + 1) * SHARD_W):
                dst = lo - d * SHARD_W
                o_ref[d, dst:dst + width, :] = p_ref[src:src + width, :]

    return _pcall(
        body, name="grad_blocks", grid=(D_MODEL // COL_TILE,),
        in_specs=[pl.BlockSpec((PROJ_PAD, COL_TILE), lambda j: (0, j))],
        out_specs=pl.BlockSpec((N_DEV, SHARD_W, COL_TILE), lambda j: (0, 0, j)),
        out_shape=jax.ShapeDtypeStruct((N_DEV, SHARD_W, D_MODEL), bf16),
        compiler_params=_cparams("parallel"),
    )(g_t)


def _in_proj(x, nw, wpad_t, after):
    L = x.shape[0]
    tn = 768
    nj = wpad_t.shape[0] // tn

    def body(x_ref, nw_ref, w_ref, after_ref, proj_ref, h_ref):
        @pl.when(pl.program_id(0) == 0)
        def _():
            for r in range(0, L, 256):
                xs = x_ref[r:r + 256, :]
                ms = jnp.mean(xs * xs, axis=-1, keepdims=True)
                h_ref[r:r + 256, :] = ((xs * lax.rsqrt(ms + EPS)) * nw_ref[...]).astype(bf16)
        for r in range(0, L, 512):
            proj_ref[r:r + 512, :] = lax.dot_general(h_ref[r:r + 512, :], w_ref[...], (((1,), (1,)), ((), ())),
                                                     preferred_element_type=f32)

    return _pcall(
        body, name="in_proj", grid=(nj,),
        in_specs=[pl.BlockSpec((L, D_MODEL), lambda j: (0, 0)), pl.BlockSpec((1, D_MODEL), lambda j: (0, 0)),
                  pl.BlockSpec((tn, D_MODEL), lambda j: (j, 0)), ANY],
        out_specs=[pl.BlockSpec((L, tn), lambda j: (0, j)), pl.BlockSpec((L, D_MODEL), lambda j: (0, 0))],
        out_shape=[jax.ShapeDtypeStruct((L, wpad_t.shape[0]), f32), jax.ShapeDtypeStruct((L, D_MODEL), bf16)],
        compiler_params=_cparams("arbitrary"),
    )(x, nw, wpad_t, after)


HALVES = [slice(i * LANE, (i + 1) * LANE) for i in range(ELT_W // LANE)]
QKV_W = 512
QKV_HEADS = [slice(i * LANE, (i + 1) * LANE) for i in range(QKV_W // LANE)]
STEPS_PER_GROUP = GDN_WIDTH // QKV_W


def _conv4(x, cw_ref, ls):
    return (cw_ref[3:4, ls] * x + cw_ref[2:3, ls] * _shift_down(x, 1) + cw_ref[1:2, ls] * _shift_down(x, 2)
            + cw_ref[0:1, ls] * _shift_down(x, 3))


def _qkv_act(proj, cw):
    L = proj.shape[0]

    def body(x_ref, cw_ref, o_ref):
        j = pl.program_id(0)
        scale = jnp.where(j < STEPS_PER_GROUP, HEAD_DIM ** -0.5, 1.0).astype(f32)
        for ls in QKV_HEADS:
            c = _conv4(x_ref[:, ls], cw_ref, ls)
            a = c * _sigmoid(c)
            rn = lax.rsqrt(jnp.sum(a * a, axis=1, keepdims=True) + EPS)
            o_ref[:, ls] = jnp.where(j < 2 * STEPS_PER_GROUP, (a * rn) * scale, a)

    return _pcall(
        body, name="qkv_act", grid=(3 * STEPS_PER_GROUP,),
        in_specs=[pl.BlockSpec((L, QKV_W), lambda j: (0, j)), pl.BlockSpec((4, QKV_W), lambda j: (0, j))],
        out_specs=pl.BlockSpec((L, QKV_W), lambda j: (0, j)),
        out_shape=jax.ShapeDtypeStruct((L, 3 * GDN_WIDTH), f32),
        compiler_params=_cparams("parallel"),
    )(proj, cw)


def _scalars(proj, alog_p, dtb_p):
    L = proj.shape[0]
    nc = L // CHUNK

    def body(x_ref, al_ref, dt_ref, sc_ref, gr_ref):
        x = x_ref[...]
        lane = _lanes(x.shape)
        beta = _sigmoid(x)
        g = -jnp.exp(al_ref[...]) * _softplus(x + dt_ref[...])
        gc = jnp.where((lane >= HEADS) & (lane < 2 * HEADS), g, 0.0)
        rc = _rows(x.shape) & (CHUNK - 1)
        for s in (1, 2, 4, 8, 16, 32):
            gc = gc + jnp.where(rc >= s, pltpu.roll(gc, s, 0), 0.0)
        sc_ref[...] = jnp.where(lane < HEADS, beta, gc)
        sel = (_lanes((HEADS, LANE)) == _rows((HEADS, LANE)) + HEADS).astype(f32)
        for c in range(nc):
            gr_ref[c] = lax.dot_general(sel, sc_ref[c * CHUNK:(c + 1) * CHUNK, :], (((1,), (1,)), ((), ())),
                                        preferred_element_type=f32, precision=lax.Precision.HIGHEST)

    return _pcall(
        body, name="scalars", grid=(1,),
        in_specs=[pl.BlockSpec((L, LANE), lambda i: (0, OFF_BA // LANE)), pl.BlockSpec((1, LANE), lambda i: (0, 0)),
                  pl.BlockSpec((1, LANE), lambda i: (0, 0))],
        out_specs=[pl.BlockSpec((L, LANE), lambda i: (0, 0)), pl.BlockSpec((nc, HEADS, CHUNK), lambda i: (0, 0, 0))],
        out_shape=[jax.ShapeDtypeStruct((L, LANE), f32), jax.ShapeDtypeStruct((nc, HEADS, CHUNK), f32)],
        compiler_params=_cparams("arbitrary"),
    )(proj, alog_p, dtb_p)


def _head_scalars(sc, gr_ref, h, ci=0):
    lane = _lanes(sc.shape)
    beta = jnp.sum(jnp.where(lane == h, sc, 0.0), axis=1, keepdims=True)
    gcc = jnp.sum(jnp.where(lane == HEADS + h, sc, 0.0), axis=1, keepdims=True)
    gcr = gr_ref[ci, h:h + 1, :]
    gl = jnp.sum(jnp.where(_lanes(gcr.shape) == CHUNK - 1, gcr, 0.0), axis=1, keepdims=True)
    ii, jj = _rows((CHUNK, CHUNK)), _lanes((CHUNK, CHUNK))
    dmat = jnp.where(ii >= jj, jnp.exp(jnp.minimum(gcc - gcr, 0.0)), 0.0)
    dmat_t = jnp.where(jj >= ii, jnp.exp(jnp.minimum(gcr - gcc, 0.0)), 0.0)
    return beta, gcc, gl, dmat, dmat_t, ii, jj


def _gdn_fwd(qkv, sc, gr):
    L = qkv.shape[0]
    nc = L // CHUNK
    W = GDN_WIDTH
    cps = GDN_CPS if nc % GDN_CPS == 0 else 1
    rows_per_step = cps * CHUNK

    def body(qkv_ref, sc_ref, gr_ref, o_ref, u_ref, w_ref, vn_ref, t_ref, sp_ref, s_scr):
        @pl.when(pl.program_id(0) == 0)
        def _():
            s_scr[...] = jnp.zeros_like(s_scr)
        HS = range(cps * HEADS)
        hd = [i % HEADS for i in HS]
        rs = [slice((i // HEADS) * CHUNK, (i // HEADS + 1) * CHUNK) for i in HS]
        cs = [slice(hd[i] * HEAD_DIM, (hd[i] + 1) * HEAD_DIM) for i in HS]
        q = [qkv_ref[rs[i], hd[i] * HEAD_DIM:(hd[i] + 1) * HEAD_DIM] for i in HS]
        k = [qkv_ref[rs[i], W + hd[i] * HEAD_DIM:W + (hd[i] + 1) * HEAD_DIM] for i in HS]
        v = [qkv_ref[rs[i], 2 * W + hd[i] * HEAD_DIM:2 * W + (hd[i] + 1) * HEAD_DIM] for i in HS]
        hsc = [_head_scalars(sc_ref[rs[i], :], gr_ref, hd[i], i // HEADS) for i in HS]
        beta, gcc, gl, dmat = ([x[i] for x in hsc] for i in range(4))
        ii, jj = hsc[0][5], hsc[0][6]
        eg = [jnp.exp(gcc[h]) for h in HS]
        kb = [k[h] * beta[h] for h in HS]
        kk = [_mm_nt(kb[h], k[h]) for h in HS]
        qk = [_mm_nt(q[h], k[h]) for h in HS]
        n0 = [-jnp.where(ii > jj, kk[h] * dmat[h], 0.0) for h in HS]
        n1 = [_mm(n0[h], n0[h]) for h in HS]
        n2 = [_mm(n1[h], n1[h]) for h in HS]
        p01 = [n0[h] + n1[h] + _mm(n0[h], n1[h]) for h in HS]
        n3 = [_mm(n2[h], n2[h]) for h in HS]
        n4 = [_mm(n3[h], n3[h]) for h in HS]
        p23 = [n2[h] + n3[h] + _mm(n2[h], n3[h]) for h in HS]
        n5 = [_mm(n4[h], n4[h]) for h in HS]
        p03 = [p01[h] + p23[h] + _mm(p01[h], p23[h]) for h in HS]
        p45 = [n4[h] + n5[h] + _mm(n4[h], n5[h]) for h in HS]
        t = [p03[h] + p45[h] + _mm(p03[h], p45[h]) for h in HS]
        vb = [v[h] * beta[h] for h in HS]
        kbg = [kb[h] * eg[h] for h in HS]
        uw = [_mm(t[h], _cat16([vb[h], kbg[h]], 1)) for h in HS]
        u = [vb[h] + uw[h][:, :HEAD_DIM] for h in HS]
        w = [kbg[h] + uw[h][:, HEAD_DIM:] for h in HS]
        wq = [_cat16([w[h], q[h] * eg[h]], 0) for h in HS]
        p = [jnp.where(ii >= jj, qk[h] * dmat[h], 0.0) for h in HS]
        ks = [k[h] * jnp.exp(gl[h] - gcc[h]) for h in HS]
        s = [s_scr[h] for h in range(HEADS)]
        for ci in range(cps):
            IS = range(ci * HEADS, (ci + 1) * HEADS)
            ws = [_mm(wq[i], s[hd[i]]) for i in IS]
            vn = [u[i] - ws[hd[i]][:CHUNK] for i in IS]
            pv = [_mm(p[i], vn[hd[i]]) for i in IS]
            kv = [_mm_tn(ks[i], vn[hd[i]]) for i in IS]
            for i in IS:
                h = hd[i]
                sp_ref[ci, cs[i], :] = s[h]
                o_ref[rs[i], cs[i]] = ws[h][CHUNK:] + pv[h]
                vn_ref[rs[i], cs[i]] = vn[h].astype(bf16)
            s = [jnp.exp(gl[i]) * s[hd[i]] + kv[hd[i]] for i in IS]
        for h in range(HEADS):
            s_scr[h] = s[h]
        for i in HS:
            u_ref[rs[i], cs[i]] = u[i].astype(bf16)
            w_ref[rs[i], cs[i]] = w[i].astype(bf16)
            t_ref[i // HEADS, hd[i]] = t[i].astype(bf16)

    row = lambda c: (c, 0)
    act, act16 = jax.ShapeDtypeStruct((L, W), f32), jax.ShapeDtypeStruct((L, W), bf16)
    return _pcall(
        body, name="gdn_fwd", grid=(nc // cps,),
        in_specs=[pl.BlockSpec((rows_per_step, 3 * W), row), pl.BlockSpec((rows_per_step, LANE), row),
                  pl.BlockSpec((cps, HEADS, CHUNK), lambda c: (c, 0, 0))],
        out_specs=[pl.BlockSpec((rows_per_step, W), row)] * 4 + [
            pl.BlockSpec((cps, HEADS, CHUNK, CHUNK), lambda c: (c, 0, 0, 0)),
            pl.BlockSpec((cps, W, HEAD_DIM), lambda c: (c, 0, 0))],
        out_shape=[act, act16, act16, act16, jax.ShapeDtypeStruct((nc, HEADS, CHUNK, CHUNK), bf16),
                   jax.ShapeDtypeStruct((nc, W, HEAD_DIM), f32)],
        scratch_shapes=[pltpu.VMEM((HEADS, HEAD_DIM, HEAD_DIM), f32)],
        compiler_params=_cparams("arbitrary"),
    )(qkv, sc, gr)


def _gdn_gate(o, proj, gnw):
    L = o.shape[0]

    def body(o_ref, z_ref, w_ref, m_ref):
        for ls in HALVES:
            ov, z = o_ref[:, ls], z_ref[:, ls]
            rms = lax.rsqrt(jnp.mean(ov * ov, axis=-1, keepdims=True) + EPS)
            m_ref[:, ls] = (((ov * rms) * w_ref[...]) * (z * _sigmoid(z))).astype(bf16)

    return _pcall(
        body, name="gdn_gate", grid=(GDN_WIDTH // ELT_W,),
        in_specs=[pl.BlockSpec((L, ELT_W), lambda j: (0, j)), pl.BlockSpec((L, ELT_W), lambda j: (0, OFF_ZG // ELT_W + j)),
                  pl.BlockSpec((1, LANE), lambda j: (0, 0))],
        out_specs=pl.BlockSpec((L, ELT_W), lambda j: (0, j)),
        out_shape=jax.ShapeDtypeStruct((L, GDN_WIDTH + CONV_WIDTH), bf16),
        compiler_params=_cparams("parallel"),
    )(o, proj, gnw)


def _conv3(u, cw_ref, ls):
    return cw_ref[2:3, ls] * u + cw_ref[1:2, ls] * _shift_down(u, 1) + cw_ref[0:1, ls] * _shift_down(u, 2)


def _conv_specs(L):
    return [pl.BlockSpec((L, CONV_BLOCK), lambda j: (0, OFF_CONV // CONV_BLOCK + j)),
            pl.BlockSpec((3, ELT_W), lambda j: (0, j)), pl.BlockSpec((1, ELT_W), lambda j: (0, j))]


def _conv_parts(ls):
    return [slice(g * ELT_W + ls.start, g * ELT_W + ls.stop) for g in range(4)]


def _conv_fwd(proj, cw, cb, mix):
    L = proj.shape[0]

    def body(p_ref, cw_ref, cb_ref, mix_in, m_ref):
        for ls in HALVES:
            sb, sc_, sh, sz = _conv_parts(ls)
            z = p_ref[:, sz]
            cv = _conv3(p_ref[:, sc_] * p_ref[:, sh], cw_ref, ls) + cb_ref[:, ls]
            m_ref[:, ls] = ((p_ref[:, sb] * cv) * (z * _sigmoid(z))).astype(bf16)

    return _pcall(
        body, name="conv_fwd", grid=(CONV_WIDTH // ELT_W,),
        in_specs=_conv_specs(L) + [ANY], out_specs=pl.BlockSpec((L, ELT_W), lambda j: (0, GDN_WIDTH // ELT_W + j)),
        out_shape=jax.ShapeDtypeStruct(mix.shape, mix.dtype), input_output_aliases={3: 0},
        compiler_params=_cparams("parallel"),
    )(proj, cw, cb, mix)


def _out_proj_loss(x, mix, wo, fw, tgt):
    L = x.shape[0]
    tm = min(512, L)
    MW = GDN_WIDTH + CONV_WIDTH

    def body(x_ref, m_ref, wo_ref, fw_ref, t_ref, dy_ref, dyb_ref, dm_ref, gfw_ref, loss_ref):
        @pl.when(pl.program_id(0) == 0)
        def _():
            gfw_ref[...] = jnp.zeros_like(gfw_ref)
            loss_ref[...] = jnp.zeros_like(loss_ref)
        y = x_ref[...] + jnp.dot(m_ref[...], wo_ref[...], preferred_element_type=f32)
        r = lax.rsqrt(jnp.mean(y * y, axis=-1, keepdims=True) + EPS)
        yh = y * r
        fwv = fw_ref[...]
        diff = yh * fwv - t_ref[...]
        loss_ref[...] += jnp.sum(jnp.sum(diff * diff, axis=-1, keepdims=True), axis=0, keepdims=True) * (0.5 / D_MODEL)
        dout = diff * (1.0 / D_MODEL)
        gfw_ref[...] += jnp.sum(dout * yh, axis=0, keepdims=True)
        dyh = dout * fwv
        dy = r * (dyh - yh * jnp.mean(dyh * yh, axis=-1, keepdims=True))
        dy_ref[...] = dy
        dyb = dy.astype(bf16)
        dyb_ref[...] = dyb
        dm_ref[...] = lax.dot_general(dyb, wo_ref[...], (((1,), (1,)), ((), ())), preferred_element_type=f32)

    row = lambda i: (i, 0)
    fix = lambda i: (0, 0)
    act = jax.ShapeDtypeStruct((L, D_MODEL), f32)
    return _pcall(
        body, name="out_proj_loss", grid=(L // tm,),
        in_specs=[pl.BlockSpec((tm, D_MODEL), row), pl.BlockSpec((tm, MW), row), pl.BlockSpec((MW, D_MODEL), fix),
                  pl.BlockSpec((1, D_MODEL), fix), pl.BlockSpec((tm, D_MODEL), row)],
        out_specs=[pl.BlockSpec((tm, D_MODEL), row), pl.BlockSpec((tm, D_MODEL), row), pl.BlockSpec((tm, MW), row),
                   pl.BlockSpec((1, D_MODEL), fix), pl.BlockSpec((1, LANE), fix)],
        out_shape=[act, jax.ShapeDtypeStruct((L, D_MODEL), bf16), jax.ShapeDtypeStruct((L, MW), f32),
                   jax.ShapeDtypeStruct((1, D_MODEL), f32), jax.ShapeDtypeStruct((1, LANE), f32)],
        compiler_params=_cparams("arbitrary"),
    )(x, mix, wo, fw, tgt)


def _tn_matmul(a, b, name):
    L, M = a.shape
    N = b.shape[1]
    tm = 512 if M % 512 == 0 else (768 if M % 768 == 0 else M)

    def body(a_ref, b_ref, o_ref):
        o_ref[...] = lax.dot_general(a_ref[...], b_ref[...], (((0,), (0,)), ((), ())),
                                     preferred_element_type=f32).astype(o_ref.dtype)

    return _pcall(
        body, name=name, grid=(M // tm,),
        in_specs=[pl.BlockSpec((L, tm), lambda i: (0, i)), pl.BlockSpec((L, N), lambda i: (0, 0))],
        out_specs=pl.BlockSpec((tm, N), lambda i: (i, 0)),
        out_shape=jax.ShapeDtypeStruct((M, N), bf16),
        compiler_params=_cparams("parallel"),
    )(a, b)


def _gdn_gate_bwd(o, proj, gnw, dmix_a, after):
    L = o.shape[0]

    def body(o_ref, z_ref, w_ref, dm_ref, after_ref, do_ref, dz_ref, gw_ref):
        @pl.when(pl.program_id(0) == 0)
        def _():
            gw_ref[...] = jnp.zeros_like(gw_ref)
        wv = w_ref[...]
        for ls in HALVES:
            ov, z, dm = o_ref[:, ls], z_ref[:, ls], dm_ref[:, ls]
            rms = lax.rsqrt(jnp.mean(ov * ov, axis=-1, keepdims=True) + EPS)
            xh = ov * rms
            sg = _sigmoid(z)
            d_on = dm * (z * sg)
            dz_ref[:, ls] = (dm * (xh * wv) * (sg * (1.0 + z * (1.0 - sg)))).astype(bf16)
            gw_ref[...] += jnp.sum(d_on * xh, axis=0, keepdims=True)
            dxh = d_on * wv
            do_ref[:, ls] = (rms * (dxh - xh * jnp.mean(dxh * xh, axis=-1, keepdims=True))).astype(bf16)

    wide = pl.BlockSpec((L, ELT_W), lambda j: (0, j))
    return _pcall(
        body, name="gdn_gate_bwd", grid=(GDN_WIDTH // ELT_W,),
        in_specs=[wide, pl.BlockSpec((L, ELT_W), lambda j: (0, OFF_ZG // ELT_W + j)),
                  pl.BlockSpec((1, LANE), lambda j: (0, 0)), wide, ANY],
        out_specs=[wide, pl.BlockSpec((L, ELT_W), lambda j: (0, OFF_ZG // ELT_W + j)),
                   pl.BlockSpec((1, LANE), lambda j: (0, 0))],
        out_shape=[jax.ShapeDtypeStruct((L, GDN_WIDTH), bf16), jax.ShapeDtypeStruct((L, PROJ_PAD), bf16),
                   jax.ShapeDtypeStruct((1, LANE), f32)],
        compiler_params=_cparams("arbitrary"),
    )(o, proj, gnw, dmix_a, after)


def _conv_bwd(proj, cw, cb, dmix_b, dproj):
    L = proj.shape[0]

    def body(p_ref, cw_ref, cb_ref, dm_ref, dproj_in, dp_ref, gcw_ref, gcb_ref):
        for ls in HALVES:
            sb, sc_, sh, sz_ = _conv_parts(ls)
            bv, cv_, hv, z, dm = p_ref[:, sb], p_ref[:, sc_], p_ref[:, sh], p_ref[:, sz_], dm_ref[:, ls]
            u = cv_ * hv
            cv = _conv3(u, cw_ref, ls) + cb_ref[:, ls]
            sg = _sigmoid(z)
            sz = z * sg
            dp_ref[:, sb] = (dm * cv * sz).astype(bf16)
            dp_ref[:, sz_] = (dm * (bv * cv) * (sg * (1.0 + z * (1.0 - sg)))).astype(bf16)
            dcv = dm * bv * sz
            gcb_ref[:, ls] = jnp.sum(dcv, axis=0, keepdims=True)
            dcv1, dcv2 = _shift_up(dcv, 1), _shift_up(dcv, 2)
            gcw_ref[2:3, ls] = jnp.sum(dcv * u, axis=0, keepdims=True)
            gcw_ref[1:2, ls] = jnp.sum(dcv1 * u, axis=0, keepdims=True)
            gcw_ref[0:1, ls] = jnp.sum(dcv2 * u, axis=0, keepdims=True)
            du = cw_ref[2:3, ls] * dcv + cw_ref[1:2, ls] * dcv1 + cw_ref[0:1, ls] * dcv2
            dp_ref[:, sc_] = (du * hv).astype(bf16)
            dp_ref[:, sh] = (du * cv_).astype(bf16)

    return _pcall(
        body, name="conv_bwd", grid=(CONV_WIDTH // ELT_W,),
        in_specs=_conv_specs(L) + [pl.BlockSpec((L, ELT_W), lambda j: (0, GDN_WIDTH // ELT_W + j)), ANY],
        out_specs=[pl.BlockSpec((L, CONV_BLOCK), lambda j: (0, OFF_CONV // CONV_BLOCK + j)),
                   pl.BlockSpec((3, ELT_W), lambda j: (0, j)), pl.BlockSpec((1, ELT_W), lambda j: (0, j))],
        out_shape=[jax.ShapeDtypeStruct(dproj.shape, dproj.dtype), jax.ShapeDtypeStruct((3, CONV_WIDTH), f32),
                   jax.ShapeDtypeStruct((1, CONV_WIDTH), f32)],
        input_output_aliases={4: 0},
        compiler_params=_cparams("parallel"),
    )(proj, cw, cb, dmix_b, dproj)


def _gdn_bwd(qkv, sc, gr, u_all, w_all, vn_all, t_all, sp_all, do_all):
    L = qkv.shape[0]
    nc = L // CHUNK
    W = GDN_WIDTH
    cps = GDN_CPS_BWD if nc % GDN_CPS_BWD == 0 else 1
    rows_per_step = cps * CHUNK
    nsteps = nc // cps

    def body(qkv_ref, sc_ref, gr_ref, u_ref, w_ref, vn_ref, t_ref, sp_ref, do_ref, dqkv_ref, dsc_ref, dgr_ref, ds_scr):
        @pl.when(pl.program_id(0) == 0)
        def _():
            ds_scr[...] = jnp.zeros_like(ds_scr)
        nh, base = HEADS, 0
        HS = range(cps * nh)
        hl = [i % nh for i in HS]
        hd = [base + hl[i] for i in HS]
        rs = [slice((i // nh) * CHUNK, (i // nh + 1) * CHUNK) for i in HS]
        cs = [slice(hd[i] * HEAD_DIM, (hd[i] + 1) * HEAD_DIM) for i in HS]
        q = [qkv_ref[rs[i], hd[i] * HEAD_DIM:(hd[i] + 1) * HEAD_DIM] for i in HS]
        k = [qkv_ref[rs[i], W + hd[i] * HEAD_DIM:W + (hd[i] + 1) * HEAD_DIM] for i in HS]
        v = [qkv_ref[rs[i], 2 * W + hd[i] * HEAD_DIM:2 * W + (hd[i] + 1) * HEAD_DIM] for i in HS]
        hsc = [_head_scalars(sc_ref[rs[i], :], gr_ref, hd[i], i // nh) for i in HS]
        beta, gcc, gl, dmat, dmat_t = ([x[i] for x in hsc] for i in range(5))
        ii, jj = hsc[0][5], hsc[0][6]
        eg = [jnp.exp(gcc[h]) for h in HS]
        ekl = [jnp.exp(gl[h] - gcc[h]) for h in HS]
        egl = [jnp.exp(gl[h]) for h in HS]
        kb = [k[h] * beta[h] for h in HS]
        ks = [k[h] * ekl[h] for h in HS]
        do = [do_ref[rs[h], cs[h]] for h in HS]
        vn = [vn_ref[rs[h], cs[h]] for h in HS]
        s = [sp_ref[h // nh, cs[h], :] for h in HS]
        w = [w_ref[rs[h], cs[h]] for h in HS]
        qd = [q[h] * eg[h] for h in HS]

        kq = [_mm_nt(k[h], q[h]) for h in HS]
        p_t = [jnp.where(jj >= ii, kq[h] * dmat_t[h], 0.0) for h in HS]
        ptd = [_mm(p_t[h], do[h]) for h in HS]
        qw = [_cat16([qd[h], -w[h]], 0) for h in HS]
        dsn, dvn, dodv = [None] * len(HS), [None] * len(HS), [None] * len(HS)
        ds_cur = [ds_scr[base + h] for h in range(nh)]
        for ci in reversed(range(cps)):
            IS = range(ci * nh, (ci + 1) * nh)
            ksd = [_mm(ks[i], ds_cur[hl[i]]) for i in IS]
            for i in IS:
                dsn[i] = ds_cur[hl[i]]
                dvn[i] = ptd[i] + ksd[hl[i]]
                dodv[i] = _cat16([do[i], dvn[i]], 0)
            dsq = [_mm_tn(qw[i], dodv[i]) for i in IS]
            ds_cur = [egl[i] * ds_cur[hl[i]] + dsq[hl[i]] for i in IS]
        for h in range(nh):
            ds_scr[base + h] = ds_cur[h]
        x1 = [_mm_nt(dodv[h], s[h]) for h in HS]
        dks = [_mm_nt(vn[h], dsn[h]) for h in HS]
        dov = [_mm_nt(do[h], vn[h]) for h in HS]
        vdo = [_mm_nt(vn[h], do[h]) for h in HS]
        kk = [_mm_nt(kb[h], k[h]) for h in HS]
        qk = [_mm_nt(q[h], k[h]) for h in HS]
        dgl = [egl[h] * jnp.sum(jnp.sum(s[h] * dsn[h], axis=1, keepdims=True), axis=0, keepdims=True) for h in HS]
        dqd = [x1[h][:CHUNK] for h in HS]
        duw = [jnp.concatenate([dvn[h], -x1[h][CHUNK:]], axis=1) for h in HS]
        tdu = [_mm_tn(t_ref[h // nh, hd[h]], duw[h]) for h in HS]
        dvk = [duw[h] + tdu[h] for h in HS]
        uw = [jnp.concatenate([u_ref[rs[h], cs[h]], w[h]], axis=1) for h in HS]
        da = [-jnp.where(ii > jj, _mm_nt(dvk[h], uw[h]), 0.0) for h in HS]
        da_t = [-jnp.where(jj > ii, _mm_nt(uw[h], dvk[h]), 0.0) for h in HS]
        dp = [jnp.where(ii >= jj, dov[h], 0.0) for h in HS]
        dp_t = [jnp.where(jj >= ii, vdo[h], 0.0) for h in HS]
        r1 = [_mm(_cat16([da[h] * dmat[h], dp[h] * dmat[h]], 0), k[h]) for h in HS]
        dk1 = [_mm(_cat16([da_t[h] * dmat_t[h], dp_t[h] * dmat_t[h]], 1), _cat16([kb[h], q[h]], 0)) for h in HS]
        lane = _lanes((CHUNK, LANE))
        for ci in range(cps):
            dsc = jnp.zeros((CHUNK, LANE), f32)
            for i in range(ci * nh, (ci + 1) * nh):
                h = hd[i]
                a = jnp.where(ii > jj, kk[i] * dmat[i], 0.0)
                p = jnp.where(ii >= jj, qk[i] * dmat[i], 0.0)
                gmat = da[i] * a + dp[i] * p
                dvb, dkbg = dvk[i][:, :HEAD_DIM], dvk[i][:, HEAD_DIM:]
                kbg = kb[i] * eg[i]
                dkb = r1[i][:CHUNK] + dkbg * eg[i]
                dq = r1[i][CHUNK:] + dqd[i] * eg[i]
                dk = dk1[i] + dks[i] * ekl[i] + dkb * beta[i]
                dbeta = jnp.sum(dkb * k[i] + dvb * v[i], axis=1, keepdims=True)
                ksum = jnp.sum(dks[i] * ks[i], axis=1, keepdims=True)
                dgl_tot = dgl[i] + jnp.sum(ksum, axis=0, keepdims=True)
                dgc = (jnp.sum(gmat, axis=1, keepdims=True) + jnp.sum(dqd[i] * qd[i] + dkbg * kbg, axis=1, keepdims=True)
                       - ksum)
                dgc = dgc + jnp.where(_rows(dgc.shape) == CHUNK - 1, dgl_tot, 0.0)
                dqkv_ref[rs[i], h * HEAD_DIM:(h + 1) * HEAD_DIM] = dq
                dqkv_ref[rs[i], W + h * HEAD_DIM:W + (h + 1) * HEAD_DIM] = dk
                dqkv_ref[rs[i], 2 * W + h * HEAD_DIM:2 * W + (h + 1) * HEAD_DIM] = dvb * beta[i]
                dsc = jnp.where(lane == h, dbeta, jnp.where(lane == HEADS + h, dgc, dsc))
                dgr_ref[ci, h:h + 1, :] = jnp.sum(gmat, axis=0, keepdims=True)
            dsc_ref[ci * CHUNK:(ci + 1) * CHUNK, :] = dsc

    row = lambda c: (nsteps - 1 - c, 0)
    lead3 = lambda c: (nsteps - 1 - c, 0, 0)
    return _pcall(
        body, name="gdn_bwd", grid=(nsteps,),
        in_specs=[pl.BlockSpec((rows_per_step, 3 * W), row), pl.BlockSpec((rows_per_step, LANE), row),
                  pl.BlockSpec((cps, HEADS, CHUNK), lead3),
                  pl.BlockSpec((rows_per_step, W), row), pl.BlockSpec((rows_per_step, W), row),
                  pl.BlockSpec((rows_per_step, W), row),
                  pl.BlockSpec((cps, HEADS, CHUNK, CHUNK), lambda c: (nsteps - 1 - c, 0, 0, 0)),
                  pl.BlockSpec((cps, W, HEAD_DIM), lead3), pl.BlockSpec((rows_per_step, W), row)],
        out_specs=[pl.BlockSpec((rows_per_step, 3 * W), row), pl.BlockSpec((rows_per_step, LANE), row),
                   pl.BlockSpec((cps, HEADS, CHUNK), lead3)],
        out_shape=[jax.ShapeDtypeStruct((L, 3 * W), f32), jax.ShapeDtypeStruct((L, LANE), f32),
                   jax.ShapeDtypeStruct((nc, HEADS, CHUNK), f32)],
        scratch_shapes=[pltpu.VMEM((HEADS, HEAD_DIM, HEAD_DIM), f32)],
        compiler_params=_cparams("arbitrary"),
    )(qkv, sc, gr, u_all, w_all, vn_all, t_all, sp_all, do_all)


def _qkv_bwd(proj, cw, dn, dproj):
    L = proj.shape[0]

    def body(x_ref, cw_ref, dn_ref, dproj_in, dx_ref, gcw_ref):
        j = pl.program_id(0)
        steps = GDN_WIDTH // ELT_W
        scale = jnp.where(j < steps, HEAD_DIM ** -0.5, 1.0).astype(f32)
        for ls in HALVES:
            x, dn_v = x_ref[:, ls], dn_ref[:, ls]
            c = _conv4(x, cw_ref, ls)
            sg = _sigmoid(c)
            a = c * sg
            rn = lax.rsqrt(jnp.sum(a * a, axis=1, keepdims=True) + EPS)
            da_n = (scale * rn) * (dn_v - a * ((rn * rn) * jnp.sum(dn_v * a, axis=1, keepdims=True)))
            da = jnp.where(j < 2 * steps, da_n, dn_v)
            dc = da * (sg * (1.0 + c * (1.0 - sg)))
            dc1, dc2, dc3 = _shift_up(dc, 1), _shift_up(dc, 2), _shift_up(dc, 3)
            gcw_ref[3:4, ls] = jnp.sum(dc * x, axis=0, keepdims=True)
            gcw_ref[2:3, ls] = jnp.sum(dc1 * x, axis=0, keepdims=True)
            gcw_ref[1:2, ls] = jnp.sum(dc2 * x, axis=0, keepdims=True)
            gcw_ref[0:1, ls] = jnp.sum(dc3 * x, axis=0, keepdims=True)
            dx = cw_ref[3:4, ls] * dc + cw_ref[2:3, ls] * dc1 + cw_ref[1:2, ls] * dc2 + cw_ref[0:1, ls] * dc3
            dx_ref[:, ls] = dx.astype(bf16)

    col = pl.BlockSpec((L, ELT_W), lambda j: (0, j))
    wspec = pl.BlockSpec((4, ELT_W), lambda j: (0, j))
    return _pcall(
        body, name="qkv_bwd", grid=(3 * GDN_WIDTH // ELT_W,),
        in_specs=[col, wspec, col, ANY], out_specs=[col, wspec],
        out_shape=[jax.ShapeDtypeStruct(dproj.shape, dproj.dtype), jax.ShapeDtypeStruct((4, 3 * GDN_WIDTH), f32)],
        input_output_aliases={3: 0},
        compiler_params=_cparams("parallel"),
    )(proj, cw, dn, dproj)


def _scalars_bwd(proj, alog_p, dtb_p, dsc, dgr_col, dproj, after):
    L = proj.shape[0]

    def body(x_ref, al_ref, dt_ref, dsc_ref, dgr_ref, dproj_in, after_ref, dba_ref, gs_ref):
        x, dsc_v = x_ref[...], dsc_ref[...]
        lane = _lanes(x.shape)
        dec = (lane >= HEADS) & (lane < 2 * HEADS)
        dg = jnp.where(dec, dsc_v - dgr_ref[...], 0.0)
        rc = _rows(x.shape) & (CHUNK - 1)
        for s in (1, 2, 4, 8, 16, 32):
            dg = dg + jnp.where(rc + s < CHUNK, pltpu.roll(dg, L - s, 0), 0.0)
        xa = x + dt_ref[...]
        ea = jnp.exp(al_ref[...])
        g = -ea * _softplus(xa)
        da = dg * (-ea) * _sigmoid(xa)
        beta = _sigmoid(x)
        db = dsc_v * beta * (1.0 - beta)
        dba_ref[:, :LANE] = jnp.where(lane < HEADS, db, jnp.where(dec, da, 0.0)).astype(bf16)
        dba_ref[:, LANE:] = jnp.zeros((L, ELT_W - LANE), bf16)
        g_al = jnp.sum(jnp.where(dec, dg * g, 0.0), axis=0, keepdims=True)
        g_dt = jnp.sum(jnp.where(dec, da, 0.0), axis=0, keepdims=True)
        row8 = _rows(gs_ref.shape)
        gs = jnp.where(row8 == 0, g_al, jnp.where(row8 == 1, g_dt, 0.0))
        gs_ref[...] = pltpu.roll(gs, LANE - HEADS, 1)

    full = pl.BlockSpec((L, LANE), lambda i: (0, 0))
    vec = pl.BlockSpec((1, LANE), lambda i: (0, 0))
    return _pcall(
        body, name="scalars_bwd", grid=(1,),
        in_specs=[pl.BlockSpec((L, LANE), lambda i: (0, OFF_BA // LANE)), vec, vec, full, full, ANY, ANY],
        out_specs=[pl.BlockSpec((L, ELT_W), lambda i: (0, OFF_BA // ELT_W)), pl.BlockSpec((8, LANE), lambda i: (0, 0))],
        out_shape=[jax.ShapeDtypeStruct(dproj.shape, dproj.dtype), jax.ShapeDtypeStruct((8, LANE), f32)],
        input_output_aliases={5: 0},
        compiler_params=_cparams("arbitrary"),
    )(proj, alog_p, dtb_p, dsc, dgr_col, dproj, after)


def _input_grad(dproj, wpad, x, nw, dy, after):
    L = x.shape[0]
    tm = min(512, L)
    cuts = (0, 3072, 5120, 7168, PROJ_PAD)
    nk = len(cuts) - 1

    def body(dp_ref, w_hbm, x_ref, nw_ref, dy_ref, after_ref, gx_ref, gnw_ref, w_vmem, sems):
        first = pl.program_id(0) == 0
        loads = [pltpu.make_async_copy(w_hbm.at[cuts[k]:cuts[k + 1], :], w_vmem.at[cuts[k]:cuts[k + 1], :], sems.at[k])
                 for k in range(nk)]

        @pl.when(first)
        def _():
            for cp in loads:
                cp.start()
            gnw_ref[...] = jnp.zeros_like(gnw_ref)
        dh = None
        for k in range(nk):
            pl.when(first)(loads[k].wait)
            part = jnp.dot(dp_ref[:, cuts[k]:cuts[k + 1]], w_vmem[cuts[k]:cuts[k + 1], :], preferred_element_type=f32)
            dh = part if dh is None else dh + part
        xv, nwv = x_ref[...], nw_ref[...]
        r = lax.rsqrt(jnp.mean(xv * xv, axis=-1, keepdims=True) + EPS)
        xh = xv * r
        gnw_ref[...] += jnp.sum(dh * xh, axis=0, keepdims=True)
        dxh = dh * nwv
        gx_ref[...] = dy_ref[...] + r * (dxh - xh * jnp.mean(dxh * xh, axis=-1, keepdims=True))

    row = lambda i: (i, 0)
    fix = lambda i: (0, 0)
    return _pcall(
        body, name="input_grad", grid=(L // tm,),
        in_specs=[pl.BlockSpec((tm, PROJ_PAD), row), ANY, pl.BlockSpec((tm, D_MODEL), row),
                  pl.BlockSpec((1, D_MODEL), fix), pl.BlockSpec((tm, D_MODEL), row), ANY],
        out_specs=[pl.BlockSpec((tm, D_MODEL), row), pl.BlockSpec((1, D_MODEL), fix)],
        out_shape=[jax.ShapeDtypeStruct((L, D_MODEL), f32), jax.ShapeDtypeStruct((1, D_MODEL), f32)],
        scratch_shapes=[pltpu.VMEM(wpad.shape, bf16), pltpu.SemaphoreType.DMA((nk,))],
        compiler_params=_cparams("arbitrary"),
    )(dproj, wpad, x, nw, dy, after)


def _adamw_reduce(parts, w, m, v, name):
    R, C = w.shape
    n_parts = parts.shape[0]
    tr = 128 if R % 128 == 0 else R
    c1 = 1.0 - ADAM_B1 ** ADAM_STEP
    c2 = 1.0 - ADAM_B2 ** ADAM_STEP

    def body(p_ref, w_ref, m_ref, v_ref, g_ref, d_ref, nm_ref, nv_ref):
        g = p_ref[0].astype(f32)
        for s in range(1, n_parts):
            g = g + p_ref[s].astype(f32)
        nm = ADAM_B1 * m_ref[...] + (1.0 - ADAM_B1) * g
        nv = ADAM_B2 * v_ref[...] + (1.0 - ADAM_B2) * (g * g)
        g_ref[...] = g
        nm_ref[...] = nm
        nv_ref[...] = nv
        d_ref[...] = -ADAM_LR * ((nm / c1) / (jnp.sqrt(nv / c2) + ADAM_EPS) + ADAM_WD * w_ref[...])

    blk = pl.BlockSpec((tr, C), lambda i: (i, 0))
    out = jax.ShapeDtypeStruct((R, C), f32)
    return _pcall(
        body, name=name, grid=(R // tr,),
        in_specs=[pl.BlockSpec((n_parts, tr, C), lambda i: (0, i, 0)), blk, blk, blk],
        out_specs=[blk] * 4, out_shape=[out] * 4,
        compiler_params=_cparams("parallel"),
    )(parts, w, m, v)


SMALL_SLOTS = ((0, D_MODEL), (D_MODEL, D_MODEL), (2 * D_MODEL, D_MODEL), (3 * D_MODEL, LANE),
               (3 * D_MODEL + LANE, HEADS), (3 * D_MODEL + 2 * LANE, HEADS))
SMALL_LOSS = 3 * D_MODEL + 3 * LANE
SMALL_W = SMALL_LOSS + LANE


def _pack_small(gs, after):
    def body(nw_ref, cb_ref, fw_ref, gn_ref, sc_ref, ls_ref, after_ref, o_ref):
        for ref, (start, width) in zip((nw_ref, cb_ref, fw_ref, gn_ref), SMALL_SLOTS[:4]):
            o_ref[:, start:start + width] = ref[...]
        o_ref[:, SMALL_SLOTS[4][0]:SMALL_SLOTS[4][0] + LANE] = sc_ref[0:1, :]
        o_ref[:, SMALL_SLOTS[5][0]:SMALL_SLOTS[5][0] + LANE] = sc_ref[1:2, :]
        o_ref[:, SMALL_LOSS:SMALL_W] = ls_ref[...]

    vm = pl.BlockSpec(memory_space=pltpu.VMEM)
    return _pcall(body, name="pack_small_grads", out_shape=jax.ShapeDtypeStruct((1, SMALL_W), f32),
                  in_specs=[vm] * 6 + [ANY], out_specs=vm)(*gs, after)


def _adamw_small(parts, ws, ms, vs):
    c1 = 1.0 - ADAM_B1 ** ADAM_STEP
    c2 = 1.0 - ADAM_B2 ** ADAM_STEP
    np_ = len(ws)

    def body(*refs):
        p_ref = refs[0]
        w_refs, m_refs, v_refs = refs[1:1 + np_], refs[1 + np_:1 + 2 * np_], refs[1 + 2 * np_:1 + 3 * np_]
        outs = refs[1 + 3 * np_:]
        g_refs, d_refs, nm_refs, nv_refs = (outs[i * np_:(i + 1) * np_] for i in range(4))
        loss_ref = outs[4 * np_]

        def total(start, width):
            t = p_ref[0, :, start:start + width]
            for s in range(1, N_DEV):
                t = t + p_ref[s, :, start:start + width]
            return t

        for i, (start, width) in enumerate(SMALL_SLOTS):
            g = total(start, width)
            nm = ADAM_B1 * m_refs[i][...] + (1.0 - ADAM_B1) * g
            nv = ADAM_B2 * v_refs[i][...] + (1.0 - ADAM_B2) * (g * g)
            g_refs[i][...] = g
            nm_refs[i][...] = nm
            nv_refs[i][...] = nv
            d_refs[i][...] = -ADAM_LR * ((nm / c1) / (jnp.sqrt(nv / c2) + ADAM_EPS) + ADAM_WD * w_refs[i][...])
        loss_ref[...] = total(SMALL_LOSS, LANE)

    vm = pl.BlockSpec(memory_space=pltpu.VMEM)
    shapes = [jax.ShapeDtypeStruct(w.shape, f32) for w in ws]
    res = _pcall(body, name="adamw_small", out_shape=shapes * 4 + [jax.ShapeDtypeStruct((1, LANE), f32)],
                 in_specs=[vm] * (1 + 3 * np_), out_specs=[vm] * (4 * np_ + 1))(parts, *ws, *ms, *vs)
    return [res[i * np_:(i + 1) * np_] for i in range(4)], res[4 * np_]


def _adamw_w_in(part_a, part_b, w3, m3, v3, after):
    _, n, _ = part_a.shape
    c1 = 1.0 - ADAM_B1 ** ADAM_STEP
    c2 = 1.0 - ADAM_B2 ** ADAM_STEP

    def body(pa_ref, pb_ref, w_ref, m_ref, v_ref, after_ref, g_ref, d_ref, nm_ref, nv_ref):
        g = pa_ref[0].astype(f32) + pb_ref[0].astype(f32)
        nm = ADAM_B1 * m_ref[:, 0, :] + (1.0 - ADAM_B1) * g
        nv = ADAM_B2 * v_ref[:, 0, :] + (1.0 - ADAM_B2) * (g * g)
        g_ref[:, 0, :] = g
        nm_ref[:, 0, :] = nm
        nv_ref[:, 0, :] = nv
        d_ref[:, 0, :] = -ADAM_LR * ((nm / c1) / (jnp.sqrt(nv / c2) + ADAM_EPS) + ADAM_WD * w_ref[:, 0, :])

    tile = 2 * COL_TILE
    blk = pl.BlockSpec((n, 1, tile), lambda j: (0, 0, j))
    out = jax.ShapeDtypeStruct((n, 1, D_MODEL), f32)
    return _pcall(
        body, name="adamw_w_in", grid=(D_MODEL // tile,),
        in_specs=[pl.BlockSpec((1, n, tile), lambda j: (0, 0, j))] * 2 + [blk, blk, blk, ANY],
        out_specs=[blk] * 4, out_shape=[out] * 4,
        compiler_params=_cparams("parallel"),
    )(part_a, part_b, w3, m3, v3, after)


def _pad_lanes(vec8, start):
    return jnp.pad(vec8.reshape(1, -1), ((0, 0), (start, LANE - start - vec8.size)))


def kernel(x, norm_in_w, w_in, conv_qkv_w, A_log, dt_bias, gdn_norm_w, conv_w, conv_b, w_out, final_norm_w, loss_target, m_norm_in_w, m_w_in, m_conv_qkv_w, m_A_log, m_dt_bias, m_gdn_norm_w, m_conv_w, m_conv_b, m_w_out, m_final_norm_w, v_norm_in_w, v_w_in, v_conv_qkv_w, v_A_log, v_dt_bias, v_gdn_norm_w, v_conv_w, v_conv_b, v_w_out, v_final_norm_w):
    L = x.shape[1]
    nc = L // CHUNK
    xs = x[0]
    tgt = loss_target[0]
    fnw = final_norm_w.reshape(1, D_MODEL)

    as_rows = lambda a: jnp.transpose(a, (2, 0, 1))
    win_g, cqkv_g, cw_g = _all_gather([_cast_w_in(as_rows(w_in)), conv_qkv_w[0], conv_w[0]], "gather_weights",
                                      pieces=[4, 1, 1])
    wpad = _relayout_w_in(win_g)
    cqkv = jnp.concatenate([cqkv_g[d] for d in range(N_DEV)], axis=1)
    cw = jnp.concatenate([cw_g[d] for d in range(N_DEV)], axis=1)
    alog_p = _pad_lanes(A_log, HEADS)
    dtb_p = _pad_lanes(dt_bias, HEADS)
    tok = lambda started: started[4]
    wo_started = _spread_start(w_out[0].astype(bf16), wpad, "gather", "gather_w_out_start")

    proj, h = _in_proj(xs, norm_in_w, wpad, tok(wo_started))
    qkv = _qkv_act(proj, cqkv)
    sc, gr = _scalars(proj, alog_p, dtb_p)
    o, u_all, w_all, vn_all, t_all, sp_all = _gdn_fwd(qkv, sc, gr)
    mix = _conv_fwd(proj, cw, conv_b, _gdn_gate(o, proj, gdn_norm_w))
    wo = _spread_wait(wo_started, mix, "gather", "gather_w_out_wait")[1].reshape(-1, D_MODEL)
    dy, dyb, dmix, g_fnw, loss_v = _out_proj_loss(xs, mix, wo, fnw, tgt)

    g_wout = _tn_matmul(mix, dyb, "grad_w_out")
    gwo_started = _spread_start(g_wout.reshape(N_DEV, -1, D_MODEL), dyb, "scatter", "exchange_grad_w_out_start")
    do, dproj, g_gnw = _gdn_gate_bwd(o, proj, gdn_norm_w, dmix, tok(gwo_started))
    dproj, g_cw, g_cb = _conv_bwd(proj, cw, conv_b, dmix, dproj)
    dqkv_n, dsc, dgr = _gdn_bwd(qkv, sc, gr, u_all, w_all, vn_all, t_all, sp_all, do)
    dproj, g_cqkv = _qkv_bwd(proj, cqkv, dqkv_n, dproj)
    g_cqkv_blk = g_cqkv.reshape(4, N_DEV, -1).transpose(1, 0, 2)
    g_cw_blk = jnp.pad(g_cw.reshape(3, N_DEV, -1).transpose(1, 0, 2),
                       ((0, 0), (0, 1), (0, g_cqkv_blk.shape[2] - g_cw.shape[1] // N_DEV)))
    gsm_started = _spread_start(jnp.concatenate([g_cqkv_blk, g_cw_blk], axis=1), g_cqkv, "scatter",
                                "exchange_small_sharded_grads_start")
    dgr_col = jnp.pad(dgr.transpose(0, 2, 1).reshape(L, HEADS), ((0, 0), (HEADS, LANE - 2 * HEADS)))
    dproj, g_sc = _scalars_bwd(proj, alog_p, dtb_p, dsc, dgr_col, dproj, tok(gsm_started))
    g_win_blk = _grad_blocks(_tn_matmul(dproj, h, "grad_w_in"))

    (p_win,) = _pair_exchange([g_win_blk], "exchange_grads_pair")
    r_small = _spread_wait(gsm_started, p_win, "scatter", "exchange_small_sharded_grads_wait")[1]
    r_cqkv, r_cw = r_small[:, :4, :], r_small[:, 4:7, :g_cw.shape[1] // N_DEV]
    s_win = _pair_sum(g_win_blk, p_win, "pair_sum_w_in")
    gw1_started = _spread_start(s_win, r_small, "axis_a", "exchange_grads_axis1_start")
    grad_x, g_nw = _input_grad(dproj, wpad, xs, norm_in_w, dy, tok(gw1_started))
    s_thru, got1 = _spread_wait(gw1_started, grad_x, "axis_a", "exchange_grads_axis1_wait")
    t_win = _axis_sum(s_thru, got1, "axis_sum_w_in")
    gw2_started = _spread_start(t_win, got1, "axis_b", "exchange_grads_axis2_start")

    r_wout = _spread_wait(gwo_started, tok(gw2_started), "scatter", "exchange_grad_w_out_wait")[1]
    upd_wout =_adamw_reduce(r_wout, w_out[0], m_w_out[0], v_w_out[0], "adamw_w_out")
    upd_cqkv = _adamw_reduce(r_cqkv, conv_qkv_w[0], m_conv_qkv_w[0], v_conv_qkv_w[0], "adamw_conv_qkv_w")
    upd_cw = _adamw_reduce(r_cw, conv_w[0], m_conv_w[0], v_conv_w[0], "adamw_conv_w")

    t_thru, got2 = _spread_wait(gw2_started, upd_cw[0], "axis_b", "exchange_grads_axis2_wait")

    small_g = _pack_small([g_nw, g_cb, g_fnw, g_gnw, g_sc, loss_v], got2)
    gsg_started = _spread_start(small_g, got2, "gather", "gather_small_grads_start")
    upd_win_t = _adamw_w_in(t_thru, got2, as_rows(w_in), as_rows(m_w_in), as_rows(v_w_in), tok(gsg_started))
    upd_win = [jnp.transpose(a, (1, 2, 0)) for a in upd_win_t]
    small_all = _spread_wait(gsg_started, upd_win_t[0], "gather", "gather_small_grads_wait")[1]
    fvec = lambda a: a.reshape(1, D_MODEL)
    upd_small, loss_sum = _adamw_small(
        small_all,
        [norm_in_w, conv_b, fvec(final_norm_w), gdn_norm_w, A_log, dt_bias],
        [m_norm_in_w, m_conv_b, fvec(m_final_norm_w), m_gdn_norm_w, m_A_log, m_dt_bias],
        [v_norm_in_w, v_conv_b, fvec(v_final_norm_w), v_gdn_norm_w, v_A_log, v_dt_bias])

    outs = [loss_sum[0, 0], grad_x[None]]
    for k in range(4):
        nw_k, cb_k, fw_k, gn_k, al_k, dt_k = upd_small[k]
        outs += [nw_k, upd_win[k], upd_cqkv[k][None], al_k, dt_k, gn_k,
                 upd_cw[k][None], cb_k, upd_wout[k][None], fw_k.reshape(D_MODEL)]
    return tuple(outs)
```

```python
import jax
import jax.numpy as jnp
from jax import lax
from jax.experimental import pallas as pl
from jax.experimental.pallas import tpu as pltpu

f32 = jnp.float32
bf16 = jnp.bfloat16

N_DEV = 8
D_MODEL = 1024
HEADS = 8
HEAD_DIM = 128
CHUNK = 64
GDN_CPS = 4
GDN_CPS_BWD = 1
GDN_WIDTH = HEADS * HEAD_DIM
CONV_WIDTH = 1024
PROJ_WIDTH = 8208
SHARD_W = PROJ_WIDTH // N_DEV
EPS = 1e-6

LANE = 128
ELT_W = 256

OFF_QKV, OFF_ZG, OFF_CONV, OFF_BA = 0, 3072, 4096, 8192
CONV_BLOCK = 4 * ELT_W
PROJ_PAD = 8448
NAT_BA, NAT_CONV = 4096, 4112


def _padded_col(n):
    if n < NAT_BA:
        return n
    if n < NAT_CONV:
        return OFF_BA + n - NAT_BA
    g, ch = divmod(n - NAT_CONV, CONV_WIDTH)
    j, r = divmod(ch, ELT_W)
    return OFF_CONV + CONV_BLOCK * j + ELT_W * g + r


def _layout_segments(n0, n1):
    cuts = [NAT_BA, NAT_CONV] + [NAT_CONV + ELT_W * k for k in range(1, 4 * CONV_WIDTH // ELT_W)]
    pts = [n0] + [c for c in cuts if n0 < c < n1] + [n1]
    return [(lo, hi - lo, _padded_col(lo)) for lo, hi in zip(pts, pts[1:])]

ADAM_LR, ADAM_B1, ADAM_B2, ADAM_EPS, ADAM_WD, ADAM_STEP = 0.001, 0.9, 0.999, 1e-08, 0.01, 10

V7X_VMEM_BYTES = 64 * 1024 * 1024
VMEM_LIMIT = V7X_VMEM_BYTES - 8 * 1024 * 1024

MESH = pl.DeviceIdType.MESH
ANY = pl.BlockSpec(memory_space=pl.ANY)


def _pcall(body, **kw):
    return pl.pallas_call(body, **kw)


def _cparams(*sem):
    return pltpu.CompilerParams(dimension_semantics=sem if sem else None, vmem_limit_bytes=VMEM_LIMIT)


def _mm(a, b):
    return jnp.dot(a.astype(bf16), b.astype(bf16), preferred_element_type=f32)


def _mm_nt(a, b):
    return lax.dot_general(a.astype(bf16), b.astype(bf16), (((1,), (1,)), ((), ())), preferred_element_type=f32)


def _cat16(parts, axis):
    return jnp.concatenate([p.astype(bf16) for p in parts], axis=axis)


def _mm_tn(a, b):
    return lax.dot_general(a.astype(bf16), b.astype(bf16), (((0,), (0,)), ((), ())), preferred_element_type=f32)


def _rows(shape):
    return lax.broadcasted_iota(jnp.int32, shape, 0)


def _lanes(shape):
    return lax.broadcasted_iota(jnp.int32, shape, 1)


def _shift_down(x, s):
    if s == 0:
        return x
    return jnp.where(_rows(x.shape) >= s, pltpu.roll(x, s, 0), 0.0)


def _shift_up(x, s):
    if s == 0:
        return x
    n = x.shape[0]
    return jnp.where(_rows(x.shape) < n - s, pltpu.roll(x, n - s, 0), 0.0)


def _sigmoid(x):
    return jax.nn.sigmoid(x)


def _softplus(x):
    e = jnp.exp(-jnp.abs(x))
    small = e * (1.0 - e * (0.5 - e * (1.0 / 3.0)))
    return jnp.maximum(x, 0.0) + jnp.where(e < 0.01, small, jnp.log(1.0 + e))


def _mesh_pos():
    return lax.axis_index("x"), lax.axis_index("y"), lax.axis_index("c")


def _flat(px, py, pc):
    return 4 * px + 2 * py + pc


def _all_gather(xs, name, pieces=None):
    n = len(xs)
    pieces = pieces or [1] * n
    items = [(a, q) for a in range(n) for q in range(pieces[a])]
    ni = len(items)

    def view(ref, i):
        a, q = items[i]
        if pieces[a] == 1:
            return ref
        wd = xs[a].shape[-1] // pieces[a]
        return ref.at[(slice(None),) * (xs[a].ndim - 1) + (pl.ds(q * wd, wd),)]

    def body(*refs):
        x_refs, o_refs = refs[:n], refs[n:2 * n]
        send_sems, recv_sems, local_sems = refs[2 * n:]
        x, y, c = _mesh_pos()
        me, sibling = (x, y, c), (x, y, 1 - c)
        flip = lambda v, bit: v + bit - 2 * v * bit
        nbr_a = (flip(x, 1 - c), flip(y, c))
        nbr_b = (flip(x, c), flip(y, 1 - c))
        diag = (1 - x, 1 - y)

        def copy(i, k, block, to, own=False):
            a = items[i][0]
            dst = view(o_refs[a].at[_flat(*block)], i)
            return pltpu.make_async_remote_copy(
                src_ref=view(x_refs[a], i) if own else dst, dst_ref=dst,
                send_sem=send_sems.at[i, k], recv_sem=recv_sems.at[i, k], device_id=to, device_id_type=MESH)

        mine, sent = [], []

        def go(cp):
            cp.start()
            sent.append(cp)

        for a in range(n):
            cp = pltpu.make_async_copy(x_refs[a], o_refs[a].at[_flat(*me)], local_sems.at[a])
            cp.start()
            mine.append(cp)
        for a in range(ni):
            go(copy(a, 1, me, (*nbr_a, c), own=True))
            go(copy(a, 2, me, (*nbr_b, c), own=True))
            go(copy(a, 0, me, sibling, own=True))
        for a in range(ni):
            copy(a, 1, (*nbr_a, c), me).wait_recv()
            go(copy(a, 3, (*nbr_a, c), (*nbr_b, c)))
            go(copy(a, 4, (*nbr_a, c), sibling))
        for a in range(ni):
            copy(a, 2, (*nbr_b, c), me).wait_recv()
            go(copy(a, 5, (*nbr_b, c), sibling))
        for a in range(ni):
            copy(a, 3, (*diag, c), me).wait_recv()
            go(copy(a, 6, (*diag, c), sibling))
        for a in range(ni):
            copy(a, 0, sibling, me).wait_recv()
            copy(a, 4, (*nbr_b, 1 - c), me).wait_recv()
            copy(a, 5, (*nbr_a, 1 - c), me).wait_recv()
            copy(a, 6, (*diag, 1 - c), me).wait_recv()
        for cp in sent:
            cp.wait_send()
        for cp in mine:
            cp.wait()

    outs = _pcall(
        body, name=name,
        out_shape=[jax.ShapeDtypeStruct((N_DEV,) + a.shape, a.dtype) for a in xs],
        in_specs=[ANY] * n, out_specs=[ANY] * n,
        scratch_shapes=[pltpu.SemaphoreType.DMA((ni, 7)), pltpu.SemaphoreType.DMA((ni, 7)), pltpu.SemaphoreType.DMA((n,))],
    )(*xs)
    return list(outs)


def _pair_exchange(gs, name):
    n = len(gs)
    chips = [(0, 0), (0, 1), (1, 0), (1, 1)]

    def body(*refs):
        g_refs, o_refs = refs[:n], refs[n:2 * n]
        send_sems, recv_sems = refs[2 * n:]
        x, y, c = _mesh_pos()
        sibling = (x, y, 1 - c)

        def copy(a, i):
            xp, yp = chips[i]
            return pltpu.make_async_remote_copy(
                src_ref=g_refs[a].at[_flat(xp, yp, 1 - c)], dst_ref=o_refs[a].at[i],
                send_sem=send_sems.at[a, i], recv_sem=recv_sems.at[a, i], device_id=sibling, device_id_type=MESH)

        cps = [copy(a, i) for a in range(n) for i in range(4)]
        for cp in cps:
            cp.start()
        for cp in cps:
            cp.wait()

    outs = _pcall(
        body, name=name,
        out_shape=[jax.ShapeDtypeStruct((4,) + a.shape[1:], a.dtype) for a in gs],
        in_specs=[ANY] * n, out_specs=[ANY] * n,
        scratch_shapes=[pltpu.SemaphoreType.DMA((n, 4)), pltpu.SemaphoreType.DMA((n, 4))],
    )(*gs)
    return list(outs)


def _pair_sum(g, p1, name):
    _, R, C = g.shape
    tr = 256 if R % 256 == 0 else R
    cidx = lax.axis_index("c").astype(jnp.int32).reshape(1)

    def body(c_ref, g_ref, p_ref, o_ref):
        o_ref[...] = (g_ref[...].astype(f32) + p_ref[...].astype(f32)).astype(o_ref.dtype)

    return _pcall(
        body, name=name,
        grid_spec=pltpu.PrefetchScalarGridSpec(
            num_scalar_prefetch=1, grid=(4, R // tr),
            in_specs=[pl.BlockSpec((1, tr, C), lambda i, r, c_ref: (2 * i + c_ref[0], r, 0)),
                      pl.BlockSpec((1, tr, C), lambda i, r, c_ref: (i, r, 0))],
            out_specs=pl.BlockSpec((1, tr, C), lambda i, r, c_ref: (i, r, 0))),
        out_shape=jax.ShapeDtypeStruct((4, R, C), g.dtype),
        compiler_params=_cparams("parallel", "parallel"),
    )(cidx, g, p1)


def _axis_sum(s, got, half, name):
    _, R, C = got.shape
    x, y, c = _mesh_pos()
    me, _, b, _ = _axis_chips(x, y, c)
    idx = jnp.stack([2 * me[0] + me[1], 2 * b[0] + b[1]]).astype(jnp.int32)

    def body(idx_ref, s_ref, g_ref, o_ref):
        o_ref[...] = (s_ref[...].astype(f32) + g_ref[...].astype(f32)).astype(o_ref.dtype)

    return _pcall(
        body, name=name,
        grid_spec=pltpu.PrefetchScalarGridSpec(
            num_scalar_prefetch=1, grid=(2,),
            in_specs=[pl.BlockSpec((1, R, C), lambda k, idx_ref: (idx_ref[k], 0, half)),
                      pl.BlockSpec((1, R, C), lambda k, idx_ref: (k, 0, 0))],
            out_specs=pl.BlockSpec((1, R, C), lambda k, idx_ref: (k, 0, 0))),
        out_shape=jax.ShapeDtypeStruct((2, R, C), s.dtype),
        compiler_params=_cparams("parallel"),
    )(idx, s, got)


HBM = pl.BlockSpec(memory_space=pltpu.HBM)
SEM = pl.BlockSpec(memory_space=pltpu.SEMAPHORE)
EFFECT = pltpu.SideEffectType.DATAFLOW_SIDE_EFFECTING


def _peers(x, y, c):
    out = []
    for k in range(1, N_DEV):
        kx, ky, kc = (k >> 2) & 1, (k >> 1) & 1, k & 1
        out.append(((1 - x) if kx else x, (1 - y) if ky else y, (1 - c) if kc else c))
    return out


SPREAD_COPIES = {"gather": N_DEV - 1, "scatter": N_DEV - 1, "axis_a": 2, "axis_b": 1}
SPREAD_SLOTS = {"axis_a": 2, "axis_b": 1}
HALF_W = D_MODEL // 2


def _axis_chips(x, y, c):
    flip = lambda v, bit: v + bit - 2 * v * bit
    return (x, y), (flip(x, 1 - c), flip(y, c)), (flip(x, c), flip(y, 1 - c)), (1 - x, 1 - y)


def _spread_copy(src_ref, land_ref, send_sems, recv_sems, k, plan, half=None):
    x, y, c = _mesh_pos()
    if plan in ("axis_a", "axis_b"):
        _, a, b, d = _axis_chips(x, y, c)
        chip = lambda p: 2 * p[0] + p[1]
        peer = (*(a if plan == "axis_a" else b), c)
        if plan == "axis_a":
            src = src_ref.at[chip(a) if k == 0 else chip(d), :, pl.ds(half * HALF_W, HALF_W)]
        else:
            src = src_ref.at[1]
        slot = k
    else:
        peer = _peers(x, y, c)[k]
        src, slot = (src_ref.at[_flat(*peer)] if plan == "scatter" else src_ref), _flat(x, y, c)
    return pltpu.make_async_remote_copy(
        src_ref=src, dst_ref=land_ref.at[slot], send_sem=send_sems.at[k], recv_sem=recv_sems.at[k],
        device_id=peer, device_id_type=MESH)


def _own_copy(src_ref, land_ref, send_sems, plan):
    me = _flat(*_mesh_pos())
    return pltpu.make_async_copy(src_ref.at[me] if plan == "scatter" else src_ref, land_ref.at[me],
                                 send_sems.at[SPREAD_COPIES[plan]])


def _spread_start(src, after, plan, name, half=None):
    land_shape = (N_DEV,) + src.shape if plan == "gather" else src.shape
    if plan in SPREAD_SLOTS:
        land_shape = (SPREAD_SLOTS[plan], src.shape[1], HALF_W)
    n_copies = SPREAD_COPIES[plan]

    def body(src_ref, land_ref, after_ref, send_sems, recv_sems, src_thru, land_thru, token):
        for k in range(n_copies):
            _spread_copy(src_ref, land_ref, send_sems, recv_sems, k, plan, half).start()
        if plan not in SPREAD_SLOTS:
            _own_copy(src_ref, land_ref, send_sems, plan).start()
        token[...] = jnp.zeros_like(token)

    return _pcall(
        body, name=name,
        out_shape=(pltpu.SemaphoreType.DMA((n_copies + (plan not in SPREAD_SLOTS),)), pltpu.SemaphoreType.DMA((n_copies,)),
                   pltpu.HBM(src.shape, src.dtype), pltpu.HBM(land_shape, src.dtype), jax.ShapeDtypeStruct((8, LANE), f32)),
        in_specs=(HBM, HBM, ANY), out_specs=(SEM, SEM, HBM, HBM, pl.BlockSpec(memory_space=pltpu.VMEM)),
        input_output_aliases={0: 2, 1: 3},
        compiler_params=pltpu.CompilerParams(has_side_effects=EFFECT),
    )(pltpu.with_memory_space_constraint(src, pltpu.HBM),
      pltpu.with_memory_space_constraint(lax.empty(land_shape, src.dtype), pltpu.HBM), after)


def _spread_wait(started, after, plan, name, half=None):
    send_sems, recv_sems, src_thru, land_thru, _ = started

    def body(src_ref, land_ref, send_sems, recv_sems, after_ref, src_dead, got_ref):
        for k in range(SPREAD_COPIES[plan]):
            cp = _spread_copy(src_ref, land_ref, send_sems, recv_sems, k, plan, half)
            cp.wait_send()
            cp.wait_recv()
        if plan not in SPREAD_SLOTS:
            _own_copy(src_ref, land_ref, send_sems, plan).wait()

    return _pcall(
        body, name=name,
        out_shape=(pltpu.HBM(src_thru.shape, src_thru.dtype), pltpu.HBM(land_thru.shape, land_thru.dtype)),
        in_specs=(HBM, HBM, SEM, SEM, ANY), out_specs=(HBM, HBM), input_output_aliases={0: 0, 1: 1},
        compiler_params=pltpu.CompilerParams(has_side_effects=EFFECT),
    )(src_thru, land_thru, send_sems, recv_sems, after)


COL_TILE = 256


def _cast_w_in(w3):
    n = w3.shape[0]

    def body(w_ref, o_ref):
        o_ref[...] = w_ref[:, 0, :].astype(bf16)

    tile = 2 * COL_TILE
    return _pcall(
        body, name="cast_w_in", grid=(D_MODEL // tile,),
        in_specs=[pl.BlockSpec((n, 1, tile), lambda j: (0, 0, j))],
        out_specs=pl.BlockSpec((n, tile), lambda j: (0, j)),
        out_shape=jax.ShapeDtypeStruct((n, D_MODEL), bf16),
        compiler_params=_cparams("parallel"),
    )(w3)


def _relayout_w_in(win_g):
    def body(g_ref, o_ref):
        used = OFF_BA + NAT_CONV - NAT_BA
        o_ref[used:PROJ_PAD, :] = jnp.zeros((PROJ_PAD - used, COL_TILE), o_ref.dtype)
        for d in range(N_DEV):
            for lo, width, dst in _layout_segments(d * SHARD_W, (d + 1) * SHARD_W):
                src = lo - d * SHARD_W
                o_ref[dst:dst + width, :] = g_ref[d, src:src + width, :]

    return _pcall(
        body, name="relayout_w_in", grid=(D_MODEL // COL_TILE,),
        in_specs=[pl.BlockSpec((N_DEV, SHARD_W, COL_TILE), lambda j: (0, 0, j))],
        out_specs=pl.BlockSpec((PROJ_PAD, COL_TILE), lambda j: (0, j)),
        out_shape=jax.ShapeDtypeStruct((PROJ_PAD, D_MODEL), win_g.dtype),
        compiler_params=_cparams("parallel"),
    )(win_g)


def _grad_blocks(g_t):
    def body(p_ref, o_ref):
        for d in range(N_DEV):
            for lo, width, src in _layout_segments(d * SHARD_W, (d + 1) * SHARD_W):
                dst = lo - d * SHARD_W
                o_ref[d, dst:dst + width, :] = p_ref[src:src + width, :]

    return _pcall(
        body, name="grad_blocks", grid=(D_MODEL // COL_TILE,),
        in_specs=[pl.BlockSpec((PROJ_PAD, COL_TILE), lambda j: (0, j))],
        out_specs=pl.BlockSpec((N_DEV, SHARD_W, COL_TILE), lambda j: (0, 0, j)),
        out_shape=jax.ShapeDtypeStruct((N_DEV, SHARD_W, D_MODEL), bf16),
        compiler_params=_cparams("parallel"),
    )(g_t)


def _in_proj(x, nw, wpad_t, after):
    L = x.shape[0]
    tn = 768
    nj = wpad_t.shape[0] // tn

    def body(x_ref, nw_ref, w_ref, after_ref, proj_ref, h_ref):
        @pl.when(pl.program_id(0) == 0)
        def _():
            for r in range(0, L, 256):
                xs = x_ref[r:r + 256, :]
                ms = jnp.mean(xs * xs, axis=-1, keepdims=True)
                h_ref[r:r + 256, :] = ((xs * lax.rsqrt(ms + EPS)) * nw_ref[...]).astype(bf16)
        for r in range(0, L, 512):
            proj_ref[r:r + 512, :] = lax.dot_general(h_ref[r:r + 512, :], w_ref[...], (((1,), (1,)), ((), ())),
                                                     preferred_element_type=f32)

    return _pcall(
        body, name="in_proj", grid=(nj,),
        in_specs=[pl.BlockSpec((L, D_MODEL), lambda j: (0, 0)), pl.BlockSpec((1, D_MODEL), lambda j: (0, 0)),
                  pl.BlockSpec((tn, D_MODEL), lambda j: (j, 0)), ANY],
        out_specs=[pl.BlockSpec((L, tn), lambda j: (0, j)), pl.BlockSpec((L, D_MODEL), lambda j: (0, 0))],
        out_shape=[jax.ShapeDtypeStruct((L, wpad_t.shape[0]), f32), jax.ShapeDtypeStruct((L, D_MODEL), bf16)],
        compiler_params=_cparams("arbitrary"),
    )(x, nw, wpad_t, after)


HALVES = [slice(i * LANE, (i + 1) * LANE) for i in range(ELT_W // LANE)]
QKV_W = 512
QKV_HEADS = [slice(i * LANE, (i + 1) * LANE) for i in range(QKV_W // LANE)]
STEPS_PER_GROUP = GDN_WIDTH // QKV_W


def _conv4(x, cw_ref, ls):
    return (cw_ref[3:4, ls] * x + cw_ref[2:3, ls] * _shift_down(x, 1) + cw_ref[1:2, ls] * _shift_down(x, 2)
            + cw_ref[0:1, ls] * _shift_down(x, 3))


def _qkv_act(proj, cw):
    L = proj.shape[0]

    def body(x_ref, cw_ref, o_ref):
        j = pl.program_id(0)
        scale = jnp.where(j < STEPS_PER_GROUP, HEAD_DIM ** -0.5, 1.0).astype(f32)
        for ls in QKV_HEADS:
            c = _conv4(x_ref[:, ls], cw_ref, ls)
            a = c * _sigmoid(c)
            rn = lax.rsqrt(jnp.sum(a * a, axis=1, keepdims=True) + EPS)
            o_ref[:, ls] = jnp.where(j < 2 * STEPS_PER_GROUP, (a * rn) * scale, a)

    return _pcall(
        body, name="qkv_act", grid=(3 * STEPS_PER_GROUP,),
        in_specs=[pl.BlockSpec((L, QKV_W), lambda j: (0, j)), pl.BlockSpec((4, QKV_W), lambda j: (0, j))],
        out_specs=pl.BlockSpec((L, QKV_W), lambda j: (0, j)),
        out_shape=jax.ShapeDtypeStruct((L, 3 * GDN_WIDTH), f32),
        compiler_params=_cparams("parallel"),
    )(proj, cw)


def _scalars(proj, alog_p, dtb_p):
    L = proj.shape[0]
    nc = L // CHUNK

    def body(x_ref, al_ref, dt_ref, sc_ref, gr_ref):
        x = x_ref[...]
        lane = _lanes(x.shape)
        beta = _sigmoid(x)
        g = -jnp.exp(al_ref[...]) * _softplus(x + dt_ref[...])
        gc = jnp.where((lane >= HEADS) & (lane < 2 * HEADS), g, 0.0)
        rc = _rows(x.shape) & (CHUNK - 1)
        for s in (1, 2, 4, 8, 16, 32):
            gc = gc + jnp.where(rc >= s, pltpu.roll(gc, s, 0), 0.0)
        sc_ref[...] = jnp.where(lane < HEADS, beta, gc)
        sel = (_lanes((HEADS, LANE)) == _rows((HEADS, LANE)) + HEADS).astype(f32)
        for c in range(nc):
            gr_ref[c] = lax.dot_general(sel, sc_ref[c * CHUNK:(c + 1) * CHUNK, :], (((1,), (1,)), ((), ())),
                                        preferred_element_type=f32, precision=lax.Precision.HIGHEST)

    return _pcall(
        body, name="scalars", grid=(1,),
        in_specs=[pl.BlockSpec((L, LANE), lambda i: (0, OFF_BA // LANE)), pl.BlockSpec((1, LANE), lambda i: (0, 0)),
                  pl.BlockSpec((1, LANE), lambda i: (0, 0))],
        out_specs=[pl.BlockSpec((L, LANE), lambda i: (0, 0)), pl.BlockSpec((nc, HEADS, CHUNK), lambda i: (0, 0, 0))],
        out_shape=[jax.ShapeDtypeStruct((L, LANE), f32), jax.ShapeDtypeStruct((nc, HEADS, CHUNK), f32)],
        compiler_params=_cparams("arbitrary"),
    )(proj, alog_p, dtb_p)


def _head_scalars(sc, gr_ref, h, ci=0):
    lane = _lanes(sc.shape)
    beta = jnp.sum(jnp.where(lane == h, sc, 0.0), axis=1, keepdims=True)
    gcc = jnp.sum(jnp.where(lane == HEADS + h, sc, 0.0), axis=1, keepdims=True)
    gcr = gr_ref[ci, h:h + 1, :]
    gl = jnp.sum(jnp.where(_lanes(gcr.shape) == CHUNK - 1, gcr, 0.0), axis=1, keepdims=True)
    ii, jj = _rows((CHUNK, CHUNK)), _lanes((CHUNK, CHUNK))
    dmat = jnp.where(ii >= jj, jnp.exp(jnp.minimum(gcc - gcr, 0.0)), 0.0)
    dmat_t = jnp.where(jj >= ii, jnp.exp(jnp.minimum(gcr - gcc, 0.0)), 0.0)
    return beta, gcc, gl, dmat, dmat_t, ii, jj


def _gdn_fwd(qkv, sc, gr):
    L = qkv.shape[0]
    nc = L // CHUNK
    W = GDN_WIDTH
    cps = GDN_CPS if nc % GDN_CPS == 0 else 1
    rows_per_step = cps * CHUNK

    def body(qkv_ref, sc_ref, gr_ref, o_ref, u_ref, w_ref, vn_ref, t_ref, sp_ref, s_scr):
        @pl.when(pl.program_id(0) == 0)
        def _():
            s_scr[...] = jnp.zeros_like(s_scr)
        HS = range(cps * HEADS)
        hd = [i % HEADS for i in HS]
        rs = [slice((i // HEADS) * CHUNK, (i // HEADS + 1) * CHUNK) for i in HS]
        cs = [slice(hd[i] * HEAD_DIM, (hd[i] + 1) * HEAD_DIM) for i in HS]
        q = [qkv_ref[rs[i], hd[i] * HEAD_DIM:(hd[i] + 1) * HEAD_DIM] for i in HS]
        k = [qkv_ref[rs[i], W + hd[i] * HEAD_DIM:W + (hd[i] + 1) * HEAD_DIM] for i in HS]
        v = [qkv_ref[rs[i], 2 * W + hd[i] * HEAD_DIM:2 * W + (hd[i] + 1) * HEAD_DIM] for i in HS]
        hsc = [_head_scalars(sc_ref[rs[i], :], gr_ref, hd[i], i // HEADS) for i in HS]
        beta, gcc, gl, dmat = ([x[i] for x in hsc] for i in range(4))
        ii, jj = hsc[0][5], hsc[0][6]
        eg = [jnp.exp(gcc[h]) for h in HS]
        kb = [k[h] * beta[h] for h in HS]
        kk = [_mm_nt(kb[h], k[h]) for h in HS]
        qk = [_mm_nt(q[h], k[h]) for h in HS]
        n0 = [-jnp.where(ii > jj, kk[h] * dmat[h], 0.0) for h in HS]
        n1 = [_mm(n0[h], n0[h]) for h in HS]
        n2 = [_mm(n1[h], n1[h]) for h in HS]
        p01 = [n0[h] + n1[h] + _mm(n0[h], n1[h]) for h in HS]
        n3 = [_mm(n2[h], n2[h]) for h in HS]
        n4 = [_mm(n3[h], n3[h]) for h in HS]
        p23 = [n2[h] + n3[h] + _mm(n2[h], n3[h]) for h in HS]
        n5 = [_mm(n4[h], n4[h]) for h in HS]
        p03 = [p01[h] + p23[h] + _mm(p01[h], p23[h]) for h in HS]
        p45 = [n4[h] + n5[h] + _mm(n4[h], n5[h]) for h in HS]
        t = [p03[h] + p45[h] + _mm(p03[h], p45[h]) for h in HS]
        vb = [v[h] * beta[h] for h in HS]
        kbg = [kb[h] * eg[h] for h in HS]
        uw = [_mm(t[h], _cat16([vb[h], kbg[h]], 1)) for h in HS]
        u = [vb[h] + uw[h][:, :HEAD_DIM] for h in HS]
        w = [kbg[h] + uw[h][:, HEAD_DIM:] for h in HS]
        wq = [_cat16([w[h], q[h] * eg[h]], 0) for h in HS]
        p = [jnp.where(ii >= jj, qk[h] * dmat[h], 0.0) for h in HS]
        ks = [k[h] * jnp.exp(gl[h] - gcc[h]) for h in HS]
        s = [s_scr[h] for h in range(HEADS)]
        for ci in range(cps):
            IS = range(ci * HEADS, (ci + 1) * HEADS)
            ws = [_mm(wq[i], s[hd[i]]) for i in IS]
            vn = [u[i] - ws[hd[i]][:CHUNK] for i in IS]
            pv = [_mm(p[i], vn[hd[i]]) for i in IS]
            kv = [_mm_tn(ks[i], vn[hd[i]]) for i in IS]
            for i in IS:
                h = hd[i]
                sp_ref[ci, cs[i], :] = s[h]
                o_ref[rs[i], cs[i]] = ws[h][CHUNK:] + pv[h]
                vn_ref[rs[i], cs[i]] = vn[h].astype(bf16)
            s = [jnp.exp(gl[i]) * s[hd[i]] + kv[hd[i]] for i in IS]
        for h in range(HEADS):
            s_scr[h] = s[h]
        for i in HS:
            u_ref[rs[i], cs[i]] = u[i].astype(bf16)
            w_ref[rs[i], cs[i]] = w[i].astype(bf16)
            t_ref[i // HEADS, hd[i]] = t[i].astype(bf16)

    row = lambda c: (c, 0)
    act, act16 = jax.ShapeDtypeStruct((L, W), f32), jax.ShapeDtypeStruct((L, W), bf16)
    return _pcall(
        body, name="gdn_fwd", grid=(nc // cps,),
        in_specs=[pl.BlockSpec((rows_per_step, 3 * W), row), pl.BlockSpec((rows_per_step, LANE), row),
                  pl.BlockSpec((cps, HEADS, CHUNK), lambda c: (c, 0, 0))],
        out_specs=[pl.BlockSpec((rows_per_step, W), row)] * 4 + [
            pl.BlockSpec((cps, HEADS, CHUNK, CHUNK), lambda c: (c, 0, 0, 0)),
            pl.BlockSpec((cps, W, HEAD_DIM), lambda c: (c, 0, 0))],
        out_shape=[act, act16, act16, act16, jax.ShapeDtypeStruct((nc, HEADS, CHUNK, CHUNK), bf16),
                   jax.ShapeDtypeStruct((nc, W, HEAD_DIM), f32)],
        scratch_shapes=[pltpu.VMEM((HEADS, HEAD_DIM, HEAD_DIM), f32)],
        compiler_params=_cparams("arbitrary"),
    )(qkv, sc, gr)


def _gdn_gate(o, proj, gnw):
    L = o.shape[0]

    def body(o_ref, z_ref, w_ref, m_ref):
        for ls in HALVES:
            ov, z = o_ref[:, ls], z_ref[:, ls]
            rms = lax.rsqrt(jnp.mean(ov * ov, axis=-1, keepdims=True) + EPS)
            m_ref[:, ls] = (((ov * rms) * w_ref[...]) * (z * _sigmoid(z))).astype(bf16)

    return _pcall(
        body, name="gdn_gate", grid=(GDN_WIDTH // ELT_W,),
        in_specs=[pl.BlockSpec((L, ELT_W), lambda j: (0, j)), pl.BlockSpec((L, ELT_W), lambda j: (0, OFF_ZG // ELT_W + j)),
                  pl.BlockSpec((1, LANE), lambda j: (0, 0))],
        out_specs=pl.BlockSpec((L, ELT_W), lambda j: (0, j)),
        out_shape=jax.ShapeDtypeStruct((L, GDN_WIDTH + CONV_WIDTH), bf16),
        compiler_params=_cparams("parallel"),
    )(o, proj, gnw)


def _conv3(u, cw_ref, ls):
    return cw_ref[2:3, ls] * u + cw_ref[1:2, ls] * _shift_down(u, 1) + cw_ref[0:1, ls] * _shift_down(u, 2)


def _conv_specs(L):
    return [pl.BlockSpec((L, CONV_BLOCK), lambda j: (0, OFF_CONV // CONV_BLOCK + j)),
            pl.BlockSpec((3, ELT_W), lambda j: (0, j)), pl.BlockSpec((1, ELT_W), lambda j: (0, j))]


def _conv_parts(ls):
    return [slice(g * ELT_W + ls.start, g * ELT_W + ls.stop) for g in range(4)]


def _conv_fwd(proj, cw, cb, mix):
    L = proj.shape[0]

    def body(p_ref, cw_ref, cb_ref, mix_in, m_ref):
        for ls in HALVES:
            sb, sc_, sh, sz = _conv_parts(ls)
            z = p_ref[:, sz]
            cv = _conv3(p_ref[:, sc_] * p_ref[:, sh], cw_ref, ls) + cb_ref[:, ls]
            m_ref[:, ls] = ((p_ref[:, sb] * cv) * (z * _sigmoid(z))).astype(bf16)

    return _pcall(
        body, name="conv_fwd", grid=(CONV_WIDTH // ELT_W,),
        in_specs=_conv_specs(L) + [ANY], out_specs=pl.BlockSpec((L, ELT_W), lambda j: (0, GDN_WIDTH // ELT_W + j)),
        out_shape=jax.ShapeDtypeStruct(mix.shape, mix.dtype), input_output_aliases={3: 0},
        compiler_params=_cparams("parallel"),
    )(proj, cw, cb, mix)


def _out_proj_loss(x, mix, wo, fw, tgt):
    L = x.shape[0]
    tm = min(512, L)
    MW = GDN_WIDTH + CONV_WIDTH

    def body(x_ref, m_ref, wo_ref, fw_ref, t_ref, dy_ref, dyb_ref, dm_ref, gfw_ref, loss_ref):
        @pl.when(pl.program_id(0) == 0)
        def _():
            gfw_ref[...] = jnp.zeros_like(gfw_ref)
            loss_ref[...] = jnp.zeros_like(loss_ref)
        y = x_ref[...] + jnp.dot(m_ref[...], wo_ref[...], preferred_element_type=f32)
        r = lax.rsqrt(jnp.mean(y * y, axis=-1, keepdims=True) + EPS)
        yh = y * r
        fwv = fw_ref[...]
        diff = yh * fwv - t_ref[...]
        loss_ref[...] += jnp.sum(jnp.sum(diff * diff, axis=-1, keepdims=True), axis=0, keepdims=True) * (0.5 / D_MODEL)
        dout = diff * (1.0 / D_MODEL)
        gfw_ref[...] += jnp.sum(dout * yh, axis=0, keepdims=True)
        dyh = dout * fwv
        dy = r * (dyh - yh * jnp.mean(dyh * yh, axis=-1, keepdims=True))
        dy_ref[...] = dy
        dyb = dy.astype(bf16)
        dyb_ref[...] = dyb
        dm_ref[...] = lax.dot_general(dyb, wo_ref[...], (((1,), (1,)), ((), ())), preferred_element_type=f32)

    row = lambda i: (i, 0)
    fix = lambda i: (0, 0)
    act = jax.ShapeDtypeStruct((L, D_MODEL), f32)
    return _pcall(
        body, name="out_proj_loss", grid=(L // tm,),
        in_specs=[pl.BlockSpec((tm, D_MODEL), row), pl.BlockSpec((tm, MW), row), pl.BlockSpec((MW, D_MODEL), fix),
                  pl.BlockSpec((1, D_MODEL), fix), pl.BlockSpec((tm, D_MODEL), row)],
        out_specs=[pl.BlockSpec((tm, D_MODEL), row), pl.BlockSpec((tm, D_MODEL), row), pl.BlockSpec((tm, MW), row),
                   pl.BlockSpec((1, D_MODEL), fix), pl.BlockSpec((1, LANE), fix)],
        out_shape=[act, jax.ShapeDtypeStruct((L, D_MODEL), bf16), jax.ShapeDtypeStruct((L, MW), f32),
                   jax.ShapeDtypeStruct((1, D_MODEL), f32), jax.ShapeDtypeStruct((1, LANE), f32)],
        compiler_params=_cparams("arbitrary"),
    )(x, mix, wo, fw, tgt)


def _tn_matmul(a, b, name):
    L, M = a.shape
    N = b.shape[1]
    tm = 512 if M % 512 == 0 else (768 if M % 768 == 0 else M)

    def body(a_ref, b_ref, o_ref):
        o_ref[...] = lax.dot_general(a_ref[...], b_ref[...], (((0,), (0,)), ((), ())),
                                     preferred_element_type=f32).astype(o_ref.dtype)

    return _pcall(
        body, name=name, grid=(M // tm,),
        in_specs=[pl.BlockSpec((L, tm), lambda i: (0, i)), pl.BlockSpec((L, N), lambda i: (0, 0))],
        out_specs=pl.BlockSpec((tm, N), lambda i: (i, 0)),
        out_shape=jax.ShapeDtypeStruct((M, N), bf16),
        compiler_params=_cparams("parallel"),
    )(a, b)


def _gdn_gate_bwd(o, proj, gnw, dmix_a, after):
    L = o.shape[0]

    def body(o_ref, z_ref, w_ref, dm_ref, after_ref, do_ref, dz_ref, gw_ref):
        @pl.when(pl.program_id(0) == 0)
        def _():
            gw_ref[...] = jnp.zeros_like(gw_ref)
        wv = w_ref[...]
        for ls in HALVES:
            ov, z, dm = o_ref[:, ls], z_ref[:, ls], dm_ref[:, ls]
            rms = lax.rsqrt(jnp.mean(ov * ov, axis=-1, keepdims=True) + EPS)
            xh = ov * rms
            sg = _sigmoid(z)
            d_on = dm * (z * sg)
            dz_ref[:, ls] = (dm * (xh * wv) * (sg * (1.0 + z * (1.0 - sg)))).astype(bf16)
            gw_ref[...] += jnp.sum(d_on * xh, axis=0, keepdims=True)
            dxh = d_on * wv
            do_ref[:, ls] = (rms * (dxh - xh * jnp.mean(dxh * xh, axis=-1, keepdims=True))).astype(bf16)

    wide = pl.BlockSpec((L, ELT_W), lambda j: (0, j))
    return _pcall(
        body, name="gdn_gate_bwd", grid=(GDN_WIDTH // ELT_W,),
        in_specs=[wide, pl.BlockSpec((L, ELT_W), lambda j: (0, OFF_ZG // ELT_W + j)),
                  pl.BlockSpec((1, LANE), lambda j: (0, 0)), wide, ANY],
        out_specs=[wide, pl.BlockSpec((L, ELT_W), lambda j: (0, OFF_ZG // ELT_W + j)),
                   pl.BlockSpec((1, LANE), lambda j: (0, 0))],
        out_shape=[jax.ShapeDtypeStruct((L, GDN_WIDTH), bf16), jax.ShapeDtypeStruct((L, PROJ_PAD), bf16),
                   jax.ShapeDtypeStruct((1, LANE), f32)],
        compiler_params=_cparams("arbitrary"),
    )(o, proj, gnw, dmix_a, after)


def _conv_bwd(proj, cw, cb, dmix_b, dproj):
    L = proj.shape[0]

    def body(p_ref, cw_ref, cb_ref, dm_ref, dproj_in, dp_ref, gcw_ref, gcb_ref):
        for ls in HALVES:
            sb, sc_, sh, sz_ = _conv_parts(ls)
            bv, cv_, hv, z, dm = p_ref[:, sb], p_ref[:, sc_], p_ref[:, sh], p_ref[:, sz_], dm_ref[:, ls]
            u = cv_ * hv
            cv = _conv3(u, cw_ref, ls) + cb_ref[:, ls]
            sg = _sigmoid(z)
            sz = z * sg
            dp_ref[:, sb] = (dm * cv * sz).astype(bf16)
            dp_ref[:, sz_] = (dm * (bv * cv) * (sg * (1.0 + z * (1.0 - sg)))).astype(bf16)
            dcv = dm * bv * sz
            gcb_ref[:, ls] = jnp.sum(dcv, axis=0, keepdims=True)
            dcv1, dcv2 = _shift_up(dcv, 1), _shift_up(dcv, 2)
            gcw_ref[2:3, ls] = jnp.sum(dcv * u, axis=0, keepdims=True)
            gcw_ref[1:2, ls] = jnp.sum(dcv1 * u, axis=0, keepdims=True)
            gcw_ref[0:1, ls] = jnp.sum(dcv2 * u, axis=0, keepdims=True)
            du = cw_ref[2:3, ls] * dcv + cw_ref[1:2, ls] * dcv1 + cw_ref[0:1, ls] * dcv2
            dp_ref[:, sc_] = (du * hv).astype(bf16)
            dp_ref[:, sh] = (du * cv_).astype(bf16)

    return _pcall(
        body, name="conv_bwd", grid=(CONV_WIDTH // ELT_W,),
        in_specs=_conv_specs(L) + [pl.BlockSpec((L, ELT_W), lambda j: (0, GDN_WIDTH // ELT_W + j)), ANY],
        out_specs=[pl.BlockSpec((L, CONV_BLOCK), lambda j: (0, OFF_CONV // CONV_BLOCK + j)),
                   pl.BlockSpec((3, ELT_W), lambda j: (0, j)), pl.BlockSpec((1, ELT_W), lambda j: (0, j))],
        out_shape=[jax.ShapeDtypeStruct(dproj.shape, dproj.dtype), jax.ShapeDtypeStruct((3, CONV_WIDTH), f32),
                   jax.ShapeDtypeStruct((1, CONV_WIDTH), f32)],
        input_output_aliases={4: 0},
        compiler_params=_cparams("parallel"),
    )(proj, cw, cb, dmix_b, dproj)


def _gdn_bwd(qkv, sc, gr, u_all, w_all, vn_all, t_all, sp_all, do_all):
    L = qkv.shape[0]
    nc = L // CHUNK
    W = GDN_WIDTH
    cps = GDN_CPS_BWD if nc % GDN_CPS_BWD == 0 else 1
    rows_per_step = cps * CHUNK
    nsteps = nc // cps

    def body(qkv_ref, sc_ref, gr_ref, u_ref, w_ref, vn_ref, t_ref, sp_ref, do_ref, dqkv_ref, dsc_ref, dgr_ref, ds_scr):
        @pl.when(pl.program_id(0) == 0)
        def _():
            ds_scr[...] = jnp.zeros_like(ds_scr)
        nh, base = HEADS, 0
        HS = range(cps * nh)
        hl = [i % nh for i in HS]
        hd = [base + hl[i] for i in HS]
        rs = [slice((i // nh) * CHUNK, (i // nh + 1) * CHUNK) for i in HS]
        cs = [slice(hd[i] * HEAD_DIM, (hd[i] + 1) * HEAD_DIM) for i in HS]
        q = [qkv_ref[rs[i], hd[i] * HEAD_DIM:(hd[i] + 1) * HEAD_DIM] for i in HS]
        k = [qkv_ref[rs[i], W + hd[i] * HEAD_DIM:W + (hd[i] + 1) * HEAD_DIM] for i in HS]
        v = [qkv_ref[rs[i], 2 * W + hd[i] * HEAD_DIM:2 * W + (hd[i] + 1) * HEAD_DIM] for i in HS]
        hsc = [_head_scalars(sc_ref[rs[i], :], gr_ref, hd[i], i // nh) for i in HS]
        beta, gcc, gl, dmat, dmat_t = ([x[i] for x in hsc] for i in range(5))
        ii, jj = hsc[0][5], hsc[0][6]
        eg = [jnp.exp(gcc[h]) for h in HS]
        ekl = [jnp.exp(gl[h] - gcc[h]) for h in HS]
        egl = [jnp.exp(gl[h]) for h in HS]
        kb = [k[h] * beta[h] for h in HS]
        ks = [k[h] * ekl[h] for h in HS]
        do = [do_ref[rs[h], cs[h]] for h in HS]
        vn = [vn_ref[rs[h], cs[h]] for h in HS]
        s = [sp_ref[h // nh, cs[h], :] for h in HS]
        w = [w_ref[rs[h], cs[h]] for h in HS]
        qd = [q[h] * eg[h] for h in HS]

        kq = [_mm_nt(k[h], q[h]) for h in HS]
        p_t = [jnp.where(jj >= ii, kq[h] * dmat_t[h], 0.0) for h in HS]
        ptd = [_mm(p_t[h], do[h]) for h in HS]
        qw = [_cat16([qd[h], -w[h]], 0) for h in HS]
        dsn, dvn, dodv = [None] * len(HS), [None] * len(HS), [None] * len(HS)
        ds_cur = [ds_scr[base + h] for h in range(nh)]
        for ci in reversed(range(cps)):
            IS = range(ci * nh, (ci + 1) * nh)
            ksd = [_mm(ks[i], ds_cur[hl[i]]) for i in IS]
            for i in IS:
                dsn[i] = ds_cur[hl[i]]
                dvn[i] = ptd[i] + ksd[hl[i]]
                dodv[i] = _cat16([do[i], dvn[i]], 0)
            dsq = [_mm_tn(qw[i], dodv[i]) for i in IS]
            ds_cur = [egl[i] * ds_cur[hl[i]] + dsq[hl[i]] for i in IS]
        for h in range(nh):
            ds_scr[base + h] = ds_cur[h]
        x1 = [_mm_nt(dodv[h], s[h]) for h in HS]
        dks = [_mm_nt(vn[h], dsn[h]) for h in HS]
        dov = [_mm_nt(do[h], vn[h]) for h in HS]
        vdo = [_mm_nt(vn[h], do[h]) for h in HS]
        kk = [_mm_nt(kb[h], k[h]) for h in HS]
        qk = [_mm_nt(q[h], k[h]) for h in HS]
        dgl = [egl[h] * jnp.sum(jnp.sum(s[h] * dsn[h], axis=1, keepdims=True), axis=0, keepdims=True) for h in HS]
        dqd = [x1[h][:CHUNK] for h in HS]
        duw = [jnp.concatenate([dvn[h], -x1[h][CHUNK:]], axis=1) for h in HS]
        tdu = [_mm_tn(t_ref[h // nh, hd[h]], duw[h]) for h in HS]
        dvk = [duw[h] + tdu[h] for h in HS]
        uw = [jnp.concatenate([u_ref[rs[h], cs[h]], w[h]], axis=1) for h in HS]
        da = [-jnp.where(ii > jj, _mm_nt(dvk[h], uw[h]), 0.0) for h in HS]
        da_t = [-jnp.where(jj > ii, _mm_nt(uw[h], dvk[h]), 0.0) for h in HS]
        dp = [jnp.where(ii >= jj, dov[h], 0.0) for h in HS]
        dp_t = [jnp.where(jj >= ii, vdo[h], 0.0) for h in HS]
        r1 = [_mm(_cat16([da[h] * dmat[h], dp[h] * dmat[h]], 0), k[h]) for h in HS]
        dk1 = [_mm(_cat16([da_t[h] * dmat_t[h], dp_t[h] * dmat_t[h]], 1), _cat16([kb[h], q[h]], 0)) for h in HS]
        lane = _lanes((CHUNK, LANE))
        for ci in range(cps):
            dsc = jnp.zeros((CHUNK, LANE), f32)
            for i in range(ci * nh, (ci + 1) * nh):
                h = hd[i]
                a = jnp.where(ii > jj, kk[i] * dmat[i], 0.0)
                p = jnp.where(ii >= jj, qk[i] * dmat[i], 0.0)
                gmat = da[i] * a + dp[i] * p
                dvb, dkbg = dvk[i][:, :HEAD_DIM], dvk[i][:, HEAD_DIM:]
                kbg = kb[i] * eg[i]
                dkb = r1[i][:CHUNK] + dkbg * eg[i]
                dq = r1[i][CHUNK:] + dqd[i] * eg[i]
                dk = dk1[i] + dks[i] * ekl[i] + dkb * beta[i]
                dbeta = jnp.sum(dkb * k[i] + dvb * v[i], axis=1, keepdims=True)
                ksum = jnp.sum(dks[i] * ks[i], axis=1, keepdims=True)
                dgl_tot = dgl[i] + jnp.sum(ksum, axis=0, keepdims=True)
                dgc = (jnp.sum(gmat, axis=1, keepdims=True) + jnp.sum(dqd[i] * qd[i] + dkbg * kbg, axis=1, keepdims=True)
                       - ksum)
                dgc = dgc + jnp.where(_rows(dgc.shape) == CHUNK - 1, dgl_tot, 0.0)
                dqkv_ref[rs[i], h * HEAD_DIM:(h + 1) * HEAD_DIM] = dq
                dqkv_ref[rs[i], W + h * HEAD_DIM:W + (h + 1) * HEAD_DIM] = dk
                dqkv_ref[rs[i], 2 * W + h * HEAD_DIM:2 * W + (h + 1) * HEAD_DIM] = dvb * beta[i]
                dsc = jnp.where(lane == h, dbeta, jnp.where(lane == HEADS + h, dgc, dsc))
                dgr_ref[ci, h:h + 1, :] = jnp.sum(gmat, axis=0, keepdims=True)
            dsc_ref[ci * CHUNK:(ci + 1) * CHUNK, :] = dsc

    row = lambda c: (nsteps - 1 - c, 0)
    lead3 = lambda c: (nsteps - 1 - c, 0, 0)
    return _pcall(
        body, name="gdn_bwd", grid=(nsteps,),
        in_specs=[pl.BlockSpec((rows_per_step, 3 * W), row), pl.BlockSpec((rows_per_step, LANE), row),
                  pl.BlockSpec((cps, HEADS, CHUNK), lead3),
                  pl.BlockSpec((rows_per_step, W), row), pl.BlockSpec((rows_per_step, W), row),
                  pl.BlockSpec((rows_per_step, W), row),
                  pl.BlockSpec((cps, HEADS, CHUNK, CHUNK), lambda c: (nsteps - 1 - c, 0, 0, 0)),
                  pl.BlockSpec((cps, W, HEAD_DIM), lead3), pl.BlockSpec((rows_per_step, W), row)],
        out_specs=[pl.BlockSpec((rows_per_step, 3 * W), row), pl.BlockSpec((rows_per_step, LANE), row),
                   pl.BlockSpec((cps, HEADS, CHUNK), lead3)],
        out_shape=[jax.ShapeDtypeStruct((L, 3 * W), f32), jax.ShapeDtypeStruct((L, LANE), f32),
                   jax.ShapeDtypeStruct((nc, HEADS, CHUNK), f32)],
        scratch_shapes=[pltpu.VMEM((HEADS, HEAD_DIM, HEAD_DIM), f32)],
        compiler_params=_cparams("arbitrary"),
    )(qkv, sc, gr, u_all, w_all, vn_all, t_all, sp_all, do_all)


def _qkv_bwd(proj, cw, dn, dproj):
    L = proj.shape[0]

    def body(x_ref, cw_ref, dn_ref, dproj_in, dx_ref, gcw_ref):
        j = pl.program_id(0)
        steps = GDN_WIDTH // ELT_W
        scale = jnp.where(j < steps, HEAD_DIM ** -0.5, 1.0).astype(f32)
        for ls in HALVES:
            x, dn_v = x_ref[:, ls], dn_ref[:, ls]
            c = _conv4(x, cw_ref, ls)
            sg = _sigmoid(c)
            a = c * sg
            rn = lax.rsqrt(jnp.sum(a * a, axis=1, keepdims=True) + EPS)
            da_n = (scale * rn) * (dn_v - a * ((rn * rn) * jnp.sum(dn_v * a, axis=1, keepdims=True)))
            da = jnp.where(j < 2 * steps, da_n, dn_v)
            dc = da * (sg * (1.0 + c * (1.0 - sg)))
            dc1, dc2, dc3 = _shift_up(dc, 1), _shift_up(dc, 2), _shift_up(dc, 3)
            gcw_ref[3:4, ls] = jnp.sum(dc * x, axis=0, keepdims=True)
            gcw_ref[2:3, ls] = jnp.sum(dc1 * x, axis=0, keepdims=True)
            gcw_ref[1:2, ls] = jnp.sum(dc2 * x, axis=0, keepdims=True)
            gcw_ref[0:1, ls] = jnp.sum(dc3 * x, axis=0, keepdims=True)
            dx = cw_ref[3:4, ls] * dc + cw_ref[2:3, ls] * dc1 + cw_ref[1:2, ls] * dc2 + cw_ref[0:1, ls] * dc3
            dx_ref[:, ls] = dx.astype(bf16)

    col = pl.BlockSpec((L, ELT_W), lambda j: (0, j))
    wspec = pl.BlockSpec((4, ELT_W), lambda j: (0, j))
    return _pcall(
        body, name="qkv_bwd", grid=(3 * GDN_WIDTH // ELT_W,),
        in_specs=[col, wspec, col, ANY], out_specs=[col, wspec],
        out_shape=[jax.ShapeDtypeStruct(dproj.shape, dproj.dtype), jax.ShapeDtypeStruct((4, 3 * GDN_WIDTH), f32)],
        input_output_aliases={3: 0},
        compiler_params=_cparams("parallel"),
    )(proj, cw, dn, dproj)


def _scalars_bwd(proj, alog_p, dtb_p, dsc, dgr_col, dproj, after):
    L = proj.shape[0]

    def body(x_ref, al_ref, dt_ref, dsc_ref, dgr_ref, dproj_in, after_ref, dba_ref, gs_ref):
        x, dsc_v = x_ref[...], dsc_ref[...]
        lane = _lanes(x.shape)
        dec = (lane >= HEADS) & (lane < 2 * HEADS)
        dg = jnp.where(dec, dsc_v - dgr_ref[...], 0.0)
        rc = _rows(x.shape) & (CHUNK - 1)
        for s in (1, 2, 4, 8, 16, 32):
            dg = dg + jnp.where(rc + s < CHUNK, pltpu.roll(dg, L - s, 0), 0.0)
        xa = x + dt_ref[...]
        ea = jnp.exp(al_ref[...])
        g = -ea * _softplus(xa)
        da = dg * (-ea) * _sigmoid(xa)
        beta = _sigmoid(x)
        db = dsc_v * beta * (1.0 - beta)
        dba_ref[:, :LANE] = jnp.where(lane < HEADS, db, jnp.where(dec, da, 0.0)).astype(bf16)
        dba_ref[:, LANE:] = jnp.zeros((L, ELT_W - LANE), bf16)
        g_al = jnp.sum(jnp.where(dec, dg * g, 0.0), axis=0, keepdims=True)
        g_dt = jnp.sum(jnp.where(dec, da, 0.0), axis=0, keepdims=True)
        row8 = _rows(gs_ref.shape)
        gs = jnp.where(row8 == 0, g_al, jnp.where(row8 == 1, g_dt, 0.0))
        gs_ref[...] = pltpu.roll(gs, LANE - HEADS, 1)

    full = pl.BlockSpec((L, LANE), lambda i: (0, 0))
    vec = pl.BlockSpec((1, LANE), lambda i: (0, 0))
    return _pcall(
        body, name="scalars_bwd", grid=(1,),
        in_specs=[pl.BlockSpec((L, LANE), lambda i: (0, OFF_BA // LANE)), vec, vec, full, full, ANY, ANY],
        out_specs=[pl.BlockSpec((L, ELT_W), lambda i: (0, OFF_BA // ELT_W)), pl.BlockSpec((8, LANE), lambda i: (0, 0))],
        out_shape=[jax.ShapeDtypeStruct(dproj.shape, dproj.dtype), jax.ShapeDtypeStruct((8, LANE), f32)],
        input_output_aliases={5: 0},
        compiler_params=_cparams("arbitrary"),
    )(proj, alog_p, dtb_p, dsc, dgr_col, dproj, after)


def _input_grad(dproj, wpad, x, nw, dy, after, part, gx_so_far, gnw_so_far):
    L = x.shape[0]
    tm = min(512, L // 2)
    steps = L // tm // 2
    cuts = (0, 3072, 5120, 7168, PROJ_PAD)
    nk = len(cuts) - 1

    def body(dp_ref, w_hbm, x_ref, nw_ref, dy_ref, after_ref, gx_in, gnw_in, gx_ref, gnw_ref, w_vmem, sems):
        first = pl.program_id(0) == 0
        loads = [pltpu.make_async_copy(w_hbm.at[cuts[k]:cuts[k + 1], :], w_vmem.at[cuts[k]:cuts[k + 1], :], sems.at[k])
                 for k in range(nk)]

        @pl.when(first)
        def _():
            for cp in loads:
                cp.start()
            gnw_ref[...] = gnw_in[...]
        dh = None
        for k in range(nk):
            pl.when(first)(loads[k].wait)
            part = jnp.dot(dp_ref[:, cuts[k]:cuts[k + 1]], w_vmem[cuts[k]:cuts[k + 1], :], preferred_element_type=f32)
            dh = part if dh is None else dh + part
        xv, nwv = x_ref[...], nw_ref[...]
        r = lax.rsqrt(jnp.mean(xv * xv, axis=-1, keepdims=True) + EPS)
        xh = xv * r
        gnw_ref[...] += jnp.sum(dh * xh, axis=0, keepdims=True)
        dxh = dh * nwv
        gx_ref[...] = dy_ref[...] + r * (dxh - xh * jnp.mean(dxh * xh, axis=-1, keepdims=True))

    row = lambda i: (i + part * steps, 0)
    fix = lambda i: (0, 0)
    return _pcall(
        body, name=f"input_grad_{part}", grid=(steps,),
        in_specs=[pl.BlockSpec((tm, PROJ_PAD), row), ANY, pl.BlockSpec((tm, D_MODEL), row),
                  pl.BlockSpec((1, D_MODEL), fix), pl.BlockSpec((tm, D_MODEL), row), ANY, ANY,
                  pl.BlockSpec((1, D_MODEL), fix)],
        out_specs=[pl.BlockSpec((tm, D_MODEL), row), pl.BlockSpec((1, D_MODEL), fix)],
        out_shape=[jax.ShapeDtypeStruct((L, D_MODEL), f32), jax.ShapeDtypeStruct((1, D_MODEL), f32)],
        input_output_aliases={6: 0},
        scratch_shapes=[pltpu.VMEM(wpad.shape, bf16), pltpu.SemaphoreType.DMA((nk,))],
        compiler_params=_cparams("arbitrary"),
    )(dproj, wpad, x, nw, dy, after, gx_so_far, gnw_so_far)


def _adamw_reduce(parts, w, m, v, name):
    R, C = w.shape
    n_parts = parts.shape[0]
    tr = 128 if R % 128 == 0 else R
    c1 = 1.0 - ADAM_B1 ** ADAM_STEP
    c2 = 1.0 - ADAM_B2 ** ADAM_STEP

    def body(p_ref, w_ref, m_ref, v_ref, g_ref, d_ref, nm_ref, nv_ref):
        g = p_ref[0].astype(f32)
        for s in range(1, n_parts):
            g = g + p_ref[s].astype(f32)
        nm = ADAM_B1 * m_ref[...] + (1.0 - ADAM_B1) * g
        nv = ADAM_B2 * v_ref[...] + (1.0 - ADAM_B2) * (g * g)
        g_ref[...] = g
        nm_ref[...] = nm
        nv_ref[...] = nv
        d_ref[...] = -ADAM_LR * ((nm / c1) / (jnp.sqrt(nv / c2) + ADAM_EPS) + ADAM_WD * w_ref[...])

    blk = pl.BlockSpec((tr, C), lambda i: (i, 0))
    out = jax.ShapeDtypeStruct((R, C), f32)
    return _pcall(
        body, name=name, grid=(R // tr,),
        in_specs=[pl.BlockSpec((n_parts, tr, C), lambda i: (0, i, 0)), blk, blk, blk],
        out_specs=[blk] * 4, out_shape=[out] * 4,
        compiler_params=_cparams("parallel"),
    )(parts, w, m, v)


SMALL_SLOTS = ((0, D_MODEL), (D_MODEL, D_MODEL), (2 * D_MODEL, D_MODEL), (3 * D_MODEL, LANE),
               (3 * D_MODEL + LANE, HEADS), (3 * D_MODEL + 2 * LANE, HEADS))
SMALL_LOSS = 3 * D_MODEL + 3 * LANE
SMALL_W = SMALL_LOSS + LANE


def _pack_small(gs, after):
    def body(nw_ref, cb_ref, fw_ref, gn_ref, sc_ref, ls_ref, after_ref, o_ref):
        for ref, (start, width) in zip((nw_ref, cb_ref, fw_ref, gn_ref), SMALL_SLOTS[:4]):
            o_ref[:, start:start + width] = ref[...]
        o_ref[:, SMALL_SLOTS[4][0]:SMALL_SLOTS[4][0] + LANE] = sc_ref[0:1, :]
        o_ref[:, SMALL_SLOTS[5][0]:SMALL_SLOTS[5][0] + LANE] = sc_ref[1:2, :]
        o_ref[:, SMALL_LOSS:SMALL_W] = ls_ref[...]

    vm = pl.BlockSpec(memory_space=pltpu.VMEM)
    return _pcall(body, name="pack_small_grads", out_shape=jax.ShapeDtypeStruct((1, SMALL_W), f32),
                  in_specs=[vm] * 6 + [ANY], out_specs=vm)(*gs, after)


def _adamw_small(parts, ws, ms, vs):
    c1 = 1.0 - ADAM_B1 ** ADAM_STEP
    c2 = 1.0 - ADAM_B2 ** ADAM_STEP
    np_ = len(ws)

    def body(*refs):
        p_ref = refs[0]
        w_refs, m_refs, v_refs = refs[1:1 + np_], refs[1 + np_:1 + 2 * np_], refs[1 + 2 * np_:1 + 3 * np_]
        outs = refs[1 + 3 * np_:]
        g_refs, d_refs, nm_refs, nv_refs = (outs[i * np_:(i + 1) * np_] for i in range(4))
        loss_ref = outs[4 * np_]

        def total(start, width):
            t = p_ref[0, :, start:start + width]
            for s in range(1, N_DEV):
                t = t + p_ref[s, :, start:start + width]
            return t

        for i, (start, width) in enumerate(SMALL_SLOTS):
            g = total(start, width)
            nm = ADAM_B1 * m_refs[i][...] + (1.0 - ADAM_B1) * g
            nv = ADAM_B2 * v_refs[i][...] + (1.0 - ADAM_B2) * (g * g)
            g_refs[i][...] = g
            nm_refs[i][...] = nm
            nv_refs[i][...] = nv
            d_refs[i][...] = -ADAM_LR * ((nm / c1) / (jnp.sqrt(nv / c2) + ADAM_EPS) + ADAM_WD * w_refs[i][...])
        loss_ref[...] = total(SMALL_LOSS, LANE)

    vm = pl.BlockSpec(memory_space=pltpu.VMEM)
    shapes = [jax.ShapeDtypeStruct(w.shape, f32) for w in ws]
    res = _pcall(body, name="adamw_small", out_shape=shapes * 4 + [jax.ShapeDtypeStruct((1, LANE), f32)],
                 in_specs=[vm] * (1 + 3 * np_), out_specs=[vm] * (4 * np_ + 1))(parts, *ws, *ms, *vs)
    return [res[i * np_:(i + 1) * np_] for i in range(4)], res[4 * np_]


def _adamw_w_in(parts_a, parts_b, w3, m3, v3, after):
    _, n, _ = parts_a[0].shape
    c1 = 1.0 - ADAM_B1 ** ADAM_STEP
    c2 = 1.0 - ADAM_B2 ** ADAM_STEP

    def body(pa0_ref, pa1_ref, pb0_ref, pb1_ref, w_ref, m_ref, v_ref, after_ref, g_ref, d_ref, nm_ref, nv_ref):
        g = jnp.where(pl.program_id(0) == 0, pa0_ref[0].astype(f32) + pb0_ref[0].astype(f32),
                      pa1_ref[0].astype(f32) + pb1_ref[0].astype(f32))
        nm = ADAM_B1 * m_ref[:, 0, :] + (1.0 - ADAM_B1) * g
        nv = ADAM_B2 * v_ref[:, 0, :] + (1.0 - ADAM_B2) * (g * g)
        g_ref[:, 0, :] = g
        nm_ref[:, 0, :] = nm
        nv_ref[:, 0, :] = nv
        d_ref[:, 0, :] = -ADAM_LR * ((nm / c1) / (jnp.sqrt(nv / c2) + ADAM_EPS) + ADAM_WD * w_ref[:, 0, :])

    blk = pl.BlockSpec((n, 1, HALF_W), lambda j: (0, 0, j))
    part = pl.BlockSpec((1, n, HALF_W), lambda j: (0, 0, 0))
    out = jax.ShapeDtypeStruct((n, 1, D_MODEL), f32)
    return _pcall(
        body, name="adamw_w_in", grid=(D_MODEL // HALF_W,),
        in_specs=[part] * 4 + [blk, blk, blk, ANY],
        out_specs=[blk] * 4, out_shape=[out] * 4,
        compiler_params=_cparams("parallel"),
    )(*parts_a, *parts_b, w3, m3, v3, after)


def _pad_lanes(vec8, start):
    return jnp.pad(vec8.reshape(1, -1), ((0, 0), (start, LANE - start - vec8.size)))


def kernel(x, norm_in_w, w_in, conv_qkv_w, A_log, dt_bias, gdn_norm_w, conv_w, conv_b, w_out, final_norm_w, loss_target, m_norm_in_w, m_w_in, m_conv_qkv_w, m_A_log, m_dt_bias, m_gdn_norm_w, m_conv_w, m_conv_b, m_w_out, m_final_norm_w, v_norm_in_w, v_w_in, v_conv_qkv_w, v_A_log, v_dt_bias, v_gdn_norm_w, v_conv_w, v_conv_b, v_w_out, v_final_norm_w):
    L = x.shape[1]
    nc = L // CHUNK
    xs = x[0]
    tgt = loss_target[0]
    fnw = final_norm_w.reshape(1, D_MODEL)

    as_rows = lambda a: jnp.transpose(a, (2, 0, 1))
    win_g, cqkv_g, cw_g = _all_gather([_cast_w_in(as_rows(w_in)), conv_qkv_w[0], conv_w[0]], "gather_weights",
                                      pieces=[4, 1, 1])
    wpad = _relayout_w_in(win_g)
    cqkv = jnp.concatenate([cqkv_g[d] for d in range(N_DEV)], axis=1)
    cw = jnp.concatenate([cw_g[d] for d in range(N_DEV)], axis=1)
    alog_p = _pad_lanes(A_log, HEADS)
    dtb_p = _pad_lanes(dt_bias, HEADS)
    tok = lambda started: started[4]
    wo_started = _spread_start(w_out[0].astype(bf16), wpad, "gather", "gather_w_out_start")

    proj, h = _in_proj(xs, norm_in_w, wpad, tok(wo_started))
    qkv = _qkv_act(proj, cqkv)
    sc, gr = _scalars(proj, alog_p, dtb_p)
    o, u_all, w_all, vn_all, t_all, sp_all = _gdn_fwd(qkv, sc, gr)
    mix = _conv_fwd(proj, cw, conv_b, _gdn_gate(o, proj, gdn_norm_w))
    wo = _spread_wait(wo_started, mix, "gather", "gather_w_out_wait")[1].reshape(-1, D_MODEL)
    dy, dyb, dmix, g_fnw, loss_v = _out_proj_loss(xs, mix, wo, fnw, tgt)

    g_wout = _tn_matmul(mix, dyb, "grad_w_out")
    gwo_started = _spread_start(g_wout.reshape(N_DEV, -1, D_MODEL), dyb, "scatter", "exchange_grad_w_out_start")
    do, dproj, g_gnw = _gdn_gate_bwd(o, proj, gdn_norm_w, dmix, tok(gwo_started))
    dproj, g_cw, g_cb = _conv_bwd(proj, cw, conv_b, dmix, dproj)
    dqkv_n, dsc, dgr = _gdn_bwd(qkv, sc, gr, u_all, w_all, vn_all, t_all, sp_all, do)
    dproj, g_cqkv = _qkv_bwd(proj, cqkv, dqkv_n, dproj)
    g_cqkv_blk = g_cqkv.reshape(4, N_DEV, -1).transpose(1, 0, 2)
    g_cw_blk = jnp.pad(g_cw.reshape(3, N_DEV, -1).transpose(1, 0, 2),
                       ((0, 0), (0, 1), (0, g_cqkv_blk.shape[2] - g_cw.shape[1] // N_DEV)))
    gsm_started = _spread_start(jnp.concatenate([g_cqkv_blk, g_cw_blk], axis=1), g_cqkv, "scatter",
                                "exchange_small_sharded_grads_start")
    dgr_col = jnp.pad(dgr.transpose(0, 2, 1).reshape(L, HEADS), ((0, 0), (HEADS, LANE - 2 * HEADS)))
    dproj, g_sc = _scalars_bwd(proj, alog_p, dtb_p, dsc, dgr_col, dproj, tok(gsm_started))
    g_win_blk = _grad_blocks(_tn_matmul(dproj, h, "grad_w_in"))

    (p_win,) = _pair_exchange([g_win_blk], "exchange_grads_pair")
    r_small = _spread_wait(gsm_started, p_win, "scatter", "exchange_small_sharded_grads_wait")[1]
    r_cqkv, r_cw = r_small[:, :4, :], r_small[:, 4:7, :g_cw.shape[1] // N_DEV]
    s_win = _pair_sum(g_win_blk, p_win, "pair_sum_w_in")
    a0 = _spread_start(s_win, r_small, "axis_a", "exchange_grads_axis1_start_0", half=0)
    a1 = _spread_start(a0[2], tok(a0), "axis_a", "exchange_grads_axis1_start_1", half=1)
    gx_half, gnw_half = _input_grad(dproj, wpad, xs, norm_in_w, dy, tok(a1), 0,
                                    lax.empty((L, D_MODEL), f32), jnp.zeros((1, D_MODEL), f32))
    s_thru, got1 = _spread_wait((a0[0], a0[1], a1[2], a0[3], None), gx_half, "axis_a",
                                "exchange_grads_axis1_wait_0", half=0)
    t0 = _axis_sum(s_thru, got1, 0, "axis_sum_w_in_0")
    b0 = _spread_start(t0, got1, "axis_b", "exchange_grads_axis2_start_0")
    grad_x, g_nw = _input_grad(dproj, wpad, xs, norm_in_w, dy, tok(b0), 1, gx_half, gnw_half)
    s_thru, got1 = _spread_wait((a1[0], a1[1], s_thru, a1[3], None), grad_x, "axis_a",
                                "exchange_grads_axis1_wait_1", half=1)
    t1 = _axis_sum(s_thru, got1, 1, "axis_sum_w_in_1")
    b1 = _spread_start(t1, got1, "axis_b", "exchange_grads_axis2_start_1")

    r_wout = _spread_wait(gwo_started, tok(b1), "scatter", "exchange_grad_w_out_wait")[1]
    upd_wout =_adamw_reduce(r_wout, w_out[0], m_w_out[0], v_w_out[0], "adamw_w_out")
    upd_cqkv = _adamw_reduce(r_cqkv, conv_qkv_w[0], m_conv_qkv_w[0], v_conv_qkv_w[0], "adamw_conv_qkv_w")
    upd_cw = _adamw_reduce(r_cw, conv_w[0], m_conv_w[0], v_conv_w[0], "adamw_conv_w")

    small_g = _pack_small([g_nw, g_cb, g_fnw, g_gnw, g_sc, loss_v], upd_cw[0])
    gsg_started = _spread_start(small_g, upd_cw[0], "gather", "gather_small_grads_start")

    t0_thru, got2_0 = _spread_wait(b0, tok(gsg_started), "axis_b", "exchange_grads_axis2_wait_0")
    t1_thru, got2_1 = _spread_wait(b1, got2_0, "axis_b", "exchange_grads_axis2_wait_1")
    upd_win_t = _adamw_w_in([t0_thru, t1_thru], [got2_0, got2_1], as_rows(w_in), as_rows(m_w_in), as_rows(v_w_in),
                            got2_1)
    upd_win = [jnp.transpose(a, (1, 2, 0)) for a in upd_win_t]
    small_all = _spread_wait(gsg_started, upd_win_t[0], "gather", "gather_small_grads_wait")[1]
    fvec = lambda a: a.reshape(1, D_MODEL)
    upd_small, loss_sum = _adamw_small(
        small_all,
        [norm_in_w, conv_b, fvec(final_norm_w), gdn_norm_w, A_log, dt_bias],
        [m_norm_in_w, m_conv_b, fvec(m_final_norm_w), m_gdn_norm_w, m_A_log, m_dt_bias],
        [v_norm_in_w, v_conv_b, fvec(v_final_norm_w), v_gdn_norm_w, v_A_log, v_dt_bias])

    outs = [loss_sum[0, 0], grad_x[None]]
    for k in range(4):
        nw_k, cb_k, fw_k, gn_k, al_k, dt_k = upd_small[k]
        outs += [nw_k, upd_win[k], upd_cqkv[k][None], al_k, dt_k, gn_k,
                 upd_cw[k][None], cb_k, upd_wout[k][None], fw_k.reshape(D_MODEL)]
    return tuple(outs)
```

```python
import jax
import jax.numpy as jnp
from jax import lax
from jax.experimental import pallas as pl
from jax.experimental.pallas import tpu as pltpu

f32 = jnp.float32
bf16 = jnp.bfloat16

N_DEV = 8
D_MODEL = 1024
HEADS = 8
HEAD_DIM = 128
CHUNK = 64
GDN_CPS = 4
GDN_CPS_BWD = 1
GDN_WIDTH = HEADS * HEAD_DIM
CONV_WIDTH = 1024
PROJ_WIDTH = 8208
SHARD_W = PROJ_WIDTH // N_DEV
EPS = 1e-6

LANE = 128
ELT_W = 256

OFF_QKV, OFF_ZG, OFF_CONV, OFF_BA = 0, 3072, 4096, 8192
CONV_BLOCK = 4 * ELT_W
PROJ_PAD = 8448
NAT_BA, NAT_CONV = 4096, 4112


def _padded_col(n):
    if n < NAT_BA:
        return n
    if n < NAT_CONV:
        return OFF_BA + n - NAT_BA
    g, ch = divmod(n - NAT_CONV, CONV_WIDTH)
    j, r = divmod(ch, ELT_W)
    return OFF_CONV + CONV_BLOCK * j + ELT_W * g + r


def _layout_segments(n0, n1):
    cuts = [NAT_BA, NAT_CONV] + [NAT_CONV + ELT_W * k for k in range(1, 4 * CONV_WIDTH // ELT_W)]
    pts = [n0] + [c for c in cuts if n0 < c < n1] + [n1]
    return [(lo, hi - lo, _padded_col(lo)) for lo, hi in zip(pts, pts[1:])]

ADAM_LR, ADAM_B1, ADAM_B2, ADAM_EPS, ADAM_WD, ADAM_STEP = 0.001, 0.9, 0.999, 1e-08, 0.01, 10

V7X_VMEM_BYTES = 64 * 1024 * 1024
VMEM_LIMIT = V7X_VMEM_BYTES - 8 * 1024 * 1024

MESH = pl.DeviceIdType.MESH
ANY = pl.BlockSpec(memory_space=pl.ANY)


def _pcall(body, **kw):
    return pl.pallas_call(body, **kw)


def _cparams(*sem):
    return pltpu.CompilerParams(dimension_semantics=sem if sem else None, vmem_limit_bytes=VMEM_LIMIT)


def _mm(a, b):
    return jnp.dot(a.astype(bf16), b.astype(bf16), preferred_element_type=f32)


def _mm_nt(a, b):
    return lax.dot_general(a.astype(bf16), b.astype(bf16), (((1,), (1,)), ((), ())), preferred_element_type=f32)


def _cat16(parts, axis):
    return jnp.concatenate([p.astype(bf16) for p in parts], axis=axis)


def _mm_tn(a, b):
    return lax.dot_general(a.astype(bf16), b.astype(bf16), (((0,), (0,)), ((), ())), preferred_element_type=f32)


def _rows(shape):
    return lax.broadcasted_iota(jnp.int32, shape, 0)


def _lanes(shape):
    return lax.broadcasted_iota(jnp.int32, shape, 1)


def _shift_down(x, s):
    if s == 0:
        return x
    return jnp.where(_rows(x.shape) >= s, pltpu.roll(x, s, 0), 0.0)


def _shift_up(x, s):
    if s == 0:
        return x
    n = x.shape[0]
    return jnp.where(_rows(x.shape) < n - s, pltpu.roll(x, n - s, 0), 0.0)


def _sigmoid(x):
    return jax.nn.sigmoid(x)


def _softplus(x):
    e = jnp.exp(-jnp.abs(x))
    small = e * (1.0 - e * (0.5 - e * (1.0 / 3.0)))
    return jnp.maximum(x, 0.0) + jnp.where(e < 0.01, small, jnp.log(1.0 + e))


def _mesh_pos():
    return lax.axis_index("x"), lax.axis_index("y"), lax.axis_index("c")


def _flat(px, py, pc):
    return 4 * px + 2 * py + pc


def _all_gather(xs, name, pieces=None):
    n = len(xs)
    pieces = pieces or [1] * n
    items = [(a, q) for a in range(n) for q in range(pieces[a])]
    ni = len(items)

    def view(ref, i):
        a, q = items[i]
        if pieces[a] == 1:
            return ref
        wd = xs[a].shape[-1] // pieces[a]
        return ref.at[(slice(None),) * (xs[a].ndim - 1) + (pl.ds(q * wd, wd),)]

    def body(*refs):
        x_refs, o_refs = refs[:n], refs[n:2 * n]
        send_sems, recv_sems, local_sems = refs[2 * n:]
        x, y, c = _mesh_pos()
        me, sibling = (x, y, c), (x, y, 1 - c)
        flip = lambda v, bit: v + bit - 2 * v * bit
        nbr_a = (flip(x, 1 - c), flip(y, c))
        nbr_b = (flip(x, c), flip(y, 1 - c))
        diag = (1 - x, 1 - y)

        def copy(i, k, block, to, own=False):
            a = items[i][0]
            dst = view(o_refs[a].at[_flat(*block)], i)
            return pltpu.make_async_remote_copy(
                src_ref=view(x_refs[a], i) if own else dst, dst_ref=dst,
                send_sem=send_sems.at[i, k], recv_sem=recv_sems.at[i, k], device_id=to, device_id_type=MESH)

        mine, sent = [], []

        def go(cp):
            cp.start()
            sent.append(cp)

        for a in range(n):
            cp = pltpu.make_async_copy(x_refs[a], o_refs[a].at[_flat(*me)], local_sems.at[a])
            cp.start()
            mine.append(cp)
        for a in range(ni):
            go(copy(a, 1, me, (*nbr_a, c), own=True))
            go(copy(a, 2, me, (*nbr_b, c), own=True))
            go(copy(a, 0, me, sibling, own=True))
        for a in range(ni):
            copy(a, 1, (*nbr_a, c), me).wait_recv()
            go(copy(a, 3, (*nbr_a, c), (*nbr_b, c)))
            go(copy(a, 4, (*nbr_a, c), sibling))
        for a in range(ni):
            copy(a, 2, (*nbr_b, c), me).wait_recv()
            go(copy(a, 5, (*nbr_b, c), sibling))
        for a in range(ni):
            copy(a, 3, (*diag, c), me).wait_recv()
            go(copy(a, 6, (*diag, c), sibling))
        for a in range(ni):
            copy(a, 0, sibling, me).wait_recv()
            copy(a, 4, (*nbr_b, 1 - c), me).wait_recv()
            copy(a, 5, (*nbr_a, 1 - c), me).wait_recv()
            copy(a, 6, (*diag, 1 - c), me).wait_recv()
        for cp in sent:
            cp.wait_send()
        for cp in mine:
            cp.wait()

    outs = _pcall(
        body, name=name,
        out_shape=[jax.ShapeDtypeStruct((N_DEV,) + a.shape, a.dtype) for a in xs],
        in_specs=[ANY] * n, out_specs=[ANY] * n,
        scratch_shapes=[pltpu.SemaphoreType.DMA((ni, 7)), pltpu.SemaphoreType.DMA((ni, 7)), pltpu.SemaphoreType.DMA((n,))],
    )(*xs)
    return list(outs)


def _pair_exchange(gs, name):
    n = len(gs)
    chips = [(0, 0), (0, 1), (1, 0), (1, 1)]

    def body(*refs):
        g_refs, o_refs = refs[:n], refs[n:2 * n]
        send_sems, recv_sems = refs[2 * n:]
        x, y, c = _mesh_pos()
        sibling = (x, y, 1 - c)

        def copy(a, i):
            xp, yp = chips[i]
            return pltpu.make_async_remote_copy(
                src_ref=g_refs[a].at[_flat(xp, yp, 1 - c)], dst_ref=o_refs[a].at[i],
                send_sem=send_sems.at[a, i], recv_sem=recv_sems.at[a, i], device_id=sibling, device_id_type=MESH)

        cps = [copy(a, i) for a in range(n) for i in range(4)]
        for cp in cps:
            cp.start()
        for cp in cps:
            cp.wait()

    outs = _pcall(
        body, name=name,
        out_shape=[jax.ShapeDtypeStruct((4,) + a.shape[1:], a.dtype) for a in gs],
        in_specs=[ANY] * n, out_specs=[ANY] * n,
        scratch_shapes=[pltpu.SemaphoreType.DMA((n, 4)), pltpu.SemaphoreType.DMA((n, 4))],
    )(*gs)
    return list(outs)


def _pair_sum(g, p1, name):
    _, R, C = g.shape
    tr = 256 if R % 256 == 0 else R
    cidx = lax.axis_index("c").astype(jnp.int32).reshape(1)

    def body(c_ref, g_ref, p_ref, o_ref):
        o_ref[...] = (g_ref[...].astype(f32) + p_ref[...].astype(f32)).astype(o_ref.dtype)

    return _pcall(
        body, name=name,
        grid_spec=pltpu.PrefetchScalarGridSpec(
            num_scalar_prefetch=1, grid=(4, R // tr),
            in_specs=[pl.BlockSpec((1, tr, C), lambda i, r, c_ref: (2 * i + c_ref[0], r, 0)),
                      pl.BlockSpec((1, tr, C), lambda i, r, c_ref: (i, r, 0))],
            out_specs=pl.BlockSpec((1, tr, C), lambda i, r, c_ref: (i, r, 0))),
        out_shape=jax.ShapeDtypeStruct((4, R, C), g.dtype),
        compiler_params=_cparams("parallel", "parallel"),
    )(cidx, g, p1)


def _axis_sum(s, got, name):
    _, R, C = s.shape
    x, y, c = _mesh_pos()
    me, _, b, _ = _axis_chips(x, y, c)
    idx = jnp.stack([2 * me[0] + me[1], 2 * b[0] + b[1]]).astype(jnp.int32)

    def body(idx_ref, s_ref, g_ref, o_ref):
        o_ref[...] = (s_ref[...].astype(f32) + g_ref[...].astype(f32)).astype(o_ref.dtype)

    return _pcall(
        body, name=name,
        grid_spec=pltpu.PrefetchScalarGridSpec(
            num_scalar_prefetch=1, grid=(2,),
            in_specs=[pl.BlockSpec((1, R, C), lambda k, idx_ref: (idx_ref[k], 0, 0)),
                      pl.BlockSpec((1, R, C), lambda k, idx_ref: (k, 0, 0))],
            out_specs=pl.BlockSpec((1, R, C), lambda k, idx_ref: (k, 0, 0))),
        out_shape=jax.ShapeDtypeStruct((2, R, C), s.dtype),
        compiler_params=_cparams("parallel"),
    )(idx, s, got)


HBM = pl.BlockSpec(memory_space=pltpu.HBM)
SEM = pl.BlockSpec(memory_space=pltpu.SEMAPHORE)
EFFECT = pltpu.SideEffectType.DATAFLOW_SIDE_EFFECTING


def _peers(x, y, c):
    out = []
    for k in range(1, N_DEV):
        kx, ky, kc = (k >> 2) & 1, (k >> 1) & 1, k & 1
        out.append(((1 - x) if kx else x, (1 - y) if ky else y, (1 - c) if kc else c))
    return out


SPREAD_COPIES = {"gather": N_DEV - 1, "scatter": N_DEV - 1, "axis_a": 2, "axis_b": 2}
SPREAD_SLOTS = {"axis_a": 2, "axis_b": 1}


def _axis_chips(x, y, c):
    flip = lambda v, bit: v + bit - 2 * v * bit
    return (x, y), (flip(x, 1 - c), flip(y, c)), (flip(x, c), flip(y, 1 - c)), (1 - x, 1 - y)


def _spread_copy(src_ref, land_ref, send_sems, recv_sems, k, plan):
    x, y, c = _mesh_pos()
    if plan in ("axis_a", "axis_b"):
        _, a, b, d = _axis_chips(x, y, c)
        chip = lambda p: 2 * p[0] + p[1]
        peer = (*(a if plan == "axis_a" else b), c)
        if plan == "axis_a":
            src, dst = src_ref.at[chip(a) if k == 0 else chip(d)], land_ref.at[k]
        else:
            cols = pl.ds(k * (src_ref.shape[2] // 2), src_ref.shape[2] // 2)
            src, dst = src_ref.at[1, :, cols], land_ref.at[0, :, cols]
    else:
        peer = _peers(x, y, c)[k]
        src = src_ref.at[_flat(*peer)] if plan == "scatter" else src_ref
        dst = land_ref.at[_flat(x, y, c)]
    return pltpu.make_async_remote_copy(
        src_ref=src, dst_ref=dst, send_sem=send_sems.at[k], recv_sem=recv_sems.at[k],
        device_id=peer, device_id_type=MESH)


def _own_copy(src_ref, land_ref, send_sems, plan):
    me = _flat(*_mesh_pos())
    return pltpu.make_async_copy(src_ref.at[me] if plan == "scatter" else src_ref, land_ref.at[me],
                                 send_sems.at[SPREAD_COPIES[plan]])


def _spread_start(src, after, plan, name):
    land_shape = (N_DEV,) + src.shape if plan == "gather" else src.shape
    if plan in SPREAD_SLOTS:
        land_shape = (SPREAD_SLOTS[plan],) + src.shape[1:]
    n_copies = SPREAD_COPIES[plan]

    def body(src_ref, land_ref, after_ref, send_sems, recv_sems, src_thru, land_thru, token):
        for k in range(n_copies):
            _spread_copy(src_ref, land_ref, send_sems, recv_sems, k, plan).start()
        if plan not in SPREAD_SLOTS:
            _own_copy(src_ref, land_ref, send_sems, plan).start()
        token[...] = jnp.zeros_like(token)

    return _pcall(
        body, name=name,
        out_shape=(pltpu.SemaphoreType.DMA((n_copies + (plan not in SPREAD_SLOTS),)), pltpu.SemaphoreType.DMA((n_copies,)),
                   pltpu.HBM(src.shape, src.dtype), pltpu.HBM(land_shape, src.dtype), jax.ShapeDtypeStruct((8, LANE), f32)),
        in_specs=(HBM, HBM, ANY), out_specs=(SEM, SEM, HBM, HBM, pl.BlockSpec(memory_space=pltpu.VMEM)),
        input_output_aliases={0: 2, 1: 3},
        compiler_params=pltpu.CompilerParams(has_side_effects=EFFECT),
    )(pltpu.with_memory_space_constraint(src, pltpu.HBM),
      pltpu.with_memory_space_constraint(lax.empty(land_shape, src.dtype), pltpu.HBM), after)


def _spread_wait(started, after, plan, name):
    send_sems, recv_sems, src_thru, land_thru, _ = started

    def body(src_ref, land_ref, send_sems, recv_sems, after_ref, src_dead, got_ref):
        for k in range(SPREAD_COPIES[plan]):
            cp = _spread_copy(src_ref, land_ref, send_sems, recv_sems, k, plan)
            cp.wait_send()
            cp.wait_recv()
        if plan not in SPREAD_SLOTS:
            _own_copy(src_ref, land_ref, send_sems, plan).wait()

    return _pcall(
        body, name=name,
        out_shape=(pltpu.HBM(src_thru.shape, src_thru.dtype), pltpu.HBM(land_thru.shape, land_thru.dtype)),
        in_specs=(HBM, HBM, SEM, SEM, ANY), out_specs=(HBM, HBM), input_output_aliases={0: 0, 1: 1},
        compiler_params=pltpu.CompilerParams(has_side_effects=EFFECT),
    )(src_thru, land_thru, send_sems, recv_sems, after)


COL_TILE = 256


def _cast_w_in(w3):
    n = w3.shape[0]

    def body(w_ref, o_ref):
        o_ref[...] = w_ref[:, 0, :].astype(bf16)

    tile = 2 * COL_TILE
    return _pcall(
        body, name="cast_w_in", grid=(D_MODEL // tile,),
        in_specs=[pl.BlockSpec((n, 1, tile), lambda j: (0, 0, j))],
        out_specs=pl.BlockSpec((n, tile), lambda j: (0, j)),
        out_shape=jax.ShapeDtypeStruct((n, D_MODEL), bf16),
        compiler_params=_cparams("parallel"),
    )(w3)


def _relayout_w_in(win_g):
    def body(g_ref, o_ref):
        used = OFF_BA + NAT_CONV - NAT_BA
        o_ref[used:PROJ_PAD, :] = jnp.zeros((PROJ_PAD - used, COL_TILE), o_ref.dtype)
        for d in range(N_DEV):
            for lo, width, dst in _layout_segments(d * SHARD_W, (d + 1) * SHARD_W):
                src = lo - d * SHARD_W
                o_ref[dst:dst + width, :] = g_ref[d, src:src + width, :]

    return _pcall(
        body, name="relayout_w_in", grid=(D_MODEL // COL_TILE,),
        in_specs=[pl.BlockSpec((N_DEV, SHARD_W, COL_TILE), lambda j: (0, 0, j))],
        out_specs=pl.BlockSpec((PROJ_PAD, COL_TILE), lambda j: (0, j)),
        out_shape=jax.ShapeDtypeStruct((PROJ_PAD, D_MODEL), win_g.dtype),
        compiler_params=_cparams("parallel"),
    )(win_g)


def _grad_blocks(g_t):
    def body(p_ref, o_ref):
        for d in range(N_DEV):
            for lo, width, src in _layout_segments(d * SHARD_W, (d + 1) * SHARD_W):
                dst = lo - d * SHARD_W
                o_ref[d, dst:dst + width, :] = p_ref[src:src + width, :]

    return _pcall(
        body, name="grad_blocks", grid=(D_MODEL // COL_TILE,),
        in_specs=[pl.BlockSpec((PROJ_PAD, COL_TILE), lambda j: (0, j))],
        out_specs=pl.BlockSpec((N_DEV, SHARD_W, COL_TILE), lambda j: (0, 0, j)),
        out_shape=jax.ShapeDtypeStruct((N_DEV, SHARD_W, D_MODEL), bf16),
        compiler_params=_cparams("parallel"),
    )(g_t)


def _in_proj(x, nw, wpad_t, after):
    L = x.shape[0]
    tn = 768
    nj = wpad_t.shape[0] // tn

    def body(x_ref, nw_ref, w_ref, after_ref, proj_ref, h_ref):
        @pl.when(pl.program_id(0) == 0)
        def _():
            for r in range(0, L, 256):
                xs = x_ref[r:r + 256, :]
                ms = jnp.mean(xs * xs, axis=-1, keepdims=True)
                h_ref[r:r + 256, :] = ((xs * lax.rsqrt(ms + EPS)) * nw_ref[...]).astype(bf16)
        for r in range(0, L, 512):
            proj_ref[r:r + 512, :] = lax.dot_general(h_ref[r:r + 512, :], w_ref[...], (((1,), (1,)), ((), ())),
                                                     preferred_element_type=f32)

    return _pcall(
        body, name="in_proj", grid=(nj,),
        in_specs=[pl.BlockSpec((L, D_MODEL), lambda j: (0, 0)), pl.BlockSpec((1, D_MODEL), lambda j: (0, 0)),
                  pl.BlockSpec((tn, D_MODEL), lambda j: (j, 0)), ANY],
        out_specs=[pl.BlockSpec((L, tn), lambda j: (0, j)), pl.BlockSpec((L, D_MODEL), lambda j: (0, 0))],
        out_shape=[jax.ShapeDtypeStruct((L, wpad_t.shape[0]), f32), jax.ShapeDtypeStruct((L, D_MODEL), bf16)],
        compiler_params=_cparams("arbitrary"),
    )(x, nw, wpad_t, after)


HALVES = [slice(i * LANE, (i + 1) * LANE) for i in range(ELT_W // LANE)]
QKV_W = 512
QKV_HEADS = [slice(i * LANE, (i + 1) * LANE) for i in range(QKV_W // LANE)]
STEPS_PER_GROUP = GDN_WIDTH // QKV_W


def _conv4(x, cw_ref, ls):
    return (cw_ref[3:4, ls] * x + cw_ref[2:3, ls] * _shift_down(x, 1) + cw_ref[1:2, ls] * _shift_down(x, 2)
            + cw_ref[0:1, ls] * _shift_down(x, 3))


def _qkv_act(proj, cw):
    L = proj.shape[0]

    def body(x_ref, cw_ref, o_ref):
        j = pl.program_id(0)
        scale = jnp.where(j < STEPS_PER_GROUP, HEAD_DIM ** -0.5, 1.0).astype(f32)
        for ls in QKV_HEADS:
            c = _conv4(x_ref[:, ls], cw_ref, ls)
            a = c * _sigmoid(c)
            rn = lax.rsqrt(jnp.sum(a * a, axis=1, keepdims=True) + EPS)
            o_ref[:, ls] = jnp.where(j < 2 * STEPS_PER_GROUP, (a * rn) * scale, a)

    return _pcall(
        body, name="qkv_act", grid=(3 * STEPS_PER_GROUP,),
        in_specs=[pl.BlockSpec((L, QKV_W), lambda j: (0, j)), pl.BlockSpec((4, QKV_W), lambda j: (0, j))],
        out_specs=pl.BlockSpec((L, QKV_W), lambda j: (0, j)),
        out_shape=jax.ShapeDtypeStruct((L, 3 * GDN_WIDTH), f32),
        compiler_params=_cparams("parallel"),
    )(proj, cw)


def _scalars(proj, alog_p, dtb_p):
    L = proj.shape[0]
    nc = L // CHUNK

    def body(x_ref, al_ref, dt_ref, sc_ref, gr_ref):
        x = x_ref[...]
        lane = _lanes(x.shape)
        beta = _sigmoid(x)
        g = -jnp.exp(al_ref[...]) * _softplus(x + dt_ref[...])
        gc = jnp.where((lane >= HEADS) & (lane < 2 * HEADS), g, 0.0)
        rc = _rows(x.shape) & (CHUNK - 1)
        for s in (1, 2, 4, 8, 16, 32):
            gc = gc + jnp.where(rc >= s, pltpu.roll(gc, s, 0), 0.0)
        sc_ref[...] = jnp.where(lane < HEADS, beta, gc)
        sel = (_lanes((HEADS, LANE)) == _rows((HEADS, LANE)) + HEADS).astype(f32)
        for c in range(nc):
            gr_ref[c] = lax.dot_general(sel, sc_ref[c * CHUNK:(c + 1) * CHUNK, :], (((1,), (1,)), ((), ())),
                                        preferred_element_type=f32, precision=lax.Precision.HIGHEST)

    return _pcall(
        body, name="scalars", grid=(1,),
        in_specs=[pl.BlockSpec((L, LANE), lambda i: (0, OFF_BA // LANE)), pl.BlockSpec((1, LANE), lambda i: (0, 0)),
                  pl.BlockSpec((1, LANE), lambda i: (0, 0))],
        out_specs=[pl.BlockSpec((L, LANE), lambda i: (0, 0)), pl.BlockSpec((nc, HEADS, CHUNK), lambda i: (0, 0, 0))],
        out_shape=[jax.ShapeDtypeStruct((L, LANE), f32), jax.ShapeDtypeStruct((nc, HEADS, CHUNK), f32)],
        compiler_params=_cparams("arbitrary"),
    )(proj, alog_p, dtb_p)


def _head_scalars(sc, gr_ref, h, ci=0):
    lane = _lanes(sc.shape)
    beta = jnp.sum(jnp.where(lane == h, sc, 0.0), axis=1, keepdims=True)
    gcc = jnp.sum(jnp.where(lane == HEADS + h, sc, 0.0), axis=1, keepdims=True)
    gcr = gr_ref[ci, h:h + 1, :]
    gl = jnp.sum(jnp.where(_lanes(gcr.shape) == CHUNK - 1, gcr, 0.0), axis=1, keepdims=True)
    ii, jj = _rows((CHUNK, CHUNK)), _lanes((CHUNK, CHUNK))
    dmat = jnp.where(ii >= jj, jnp.exp(jnp.minimum(gcc - gcr, 0.0)), 0.0)
    dmat_t = jnp.where(jj >= ii, jnp.exp(jnp.minimum(gcr - gcc, 0.0)), 0.0)
    return beta, gcc, gl, dmat, dmat_t, ii, jj


def _gdn_fwd(qkv, sc, gr):
    L = qkv.shape[0]
    nc = L // CHUNK
    W = GDN_WIDTH
    cps = GDN_CPS if nc % GDN_CPS == 0 else 1
    rows_per_step = cps * CHUNK

    def body(qkv_ref, sc_ref, gr_ref, o_ref, u_ref, w_ref, vn_ref, t_ref, sp_ref, s_scr):
        @pl.when(pl.program_id(0) == 0)
        def _():
            s_scr[...] = jnp.zeros_like(s_scr)
        HS = range(cps * HEADS)
        hd = [i % HEADS for i in HS]
        rs = [slice((i // HEADS) * CHUNK, (i // HEADS + 1) * CHUNK) for i in HS]
        cs = [slice(hd[i] * HEAD_DIM, (hd[i] + 1) * HEAD_DIM) for i in HS]
        q = [qkv_ref[rs[i], hd[i] * HEAD_DIM:(hd[i] + 1) * HEAD_DIM] for i in HS]
        k = [qkv_ref[rs[i], W + hd[i] * HEAD_DIM:W + (hd[i] + 1) * HEAD_DIM] for i in HS]
        v = [qkv_ref[rs[i], 2 * W + hd[i] * HEAD_DIM:2 * W + (hd[i] + 1) * HEAD_DIM] for i in HS]
        hsc = [_head_scalars(sc_ref[rs[i], :], gr_ref, hd[i], i // HEADS) for i in HS]
        beta, gcc, gl, dmat = ([x[i] for x in hsc] for i in range(4))
        ii, jj = hsc[0][5], hsc[0][6]
        eg = [jnp.exp(gcc[h]) for h in HS]
        kb = [k[h] * beta[h] for h in HS]
        kk = [_mm_nt(kb[h], k[h]) for h in HS]
        qk = [_mm_nt(q[h], k[h]) for h in HS]
        n0 = [-jnp.where(ii > jj, kk[h] * dmat[h], 0.0) for h in HS]
        n1 = [_mm(n0[h], n0[h]) for h in HS]
        n2 = [_mm(n1[h], n1[h]) for h in HS]
        p01 = [n0[h] + n1[h] + _mm(n0[h], n1[h]) for h in HS]
        n3 = [_mm(n2[h], n2[h]) for h in HS]
        n4 = [_mm(n3[h], n3[h]) for h in HS]
        p23 = [n2[h] + n3[h] + _mm(n2[h], n3[h]) for h in HS]
        n5 = [_mm(n4[h], n4[h]) for h in HS]
        p03 = [p01[h] + p23[h] + _mm(p01[h], p23[h]) for h in HS]
        p45 = [n4[h] + n5[h] + _mm(n4[h], n5[h]) for h in HS]
        t = [p03[h] + p45[h] + _mm(p03[h], p45[h]) for h in HS]
        vb = [v[h] * beta[h] for h in HS]
        kbg = [kb[h] * eg[h] for h in HS]
        uw = [_mm(t[h], _cat16([vb[h], kbg[h]], 1)) for h in HS]
        u = [vb[h] + uw[h][:, :HEAD_DIM] for h in HS]
        w = [kbg[h] + uw[h][:, HEAD_DIM:] for h in HS]
        wq = [_cat16([w[h], q[h] * eg[h]], 0) for h in HS]
        p = [jnp.where(ii >= jj, qk[h] * dmat[h], 0.0) for h in HS]
        ks = [k[h] * jnp.exp(gl[h] - gcc[h]) for h in HS]
        s = [s_scr[h] for h in range(HEADS)]
        for ci in range(cps):
            IS = range(ci * HEADS, (ci + 1) * HEADS)
            ws = [_mm(wq[i], s[hd[i]]) for i in IS]
            vn = [u[i] - ws[hd[i]][:CHUNK] for i in IS]
            pv = [_mm(p[i], vn[hd[i]]) for i in IS]
            kv = [_mm_tn(ks[i], vn[hd[i]]) for i in IS]
            for i in IS:
                h = hd[i]
                sp_ref[ci, cs[i], :] = s[h]
                o_ref[rs[i], cs[i]] = ws[h][CHUNK:] + pv[h]
                vn_ref[rs[i], cs[i]] = vn[h].astype(bf16)
            s = [jnp.exp(gl[i]) * s[hd[i]] + kv[hd[i]] for i in IS]
        for h in range(HEADS):
            s_scr[h] = s[h]
        for i in HS:
            u_ref[rs[i], cs[i]] = u[i].astype(bf16)
            w_ref[rs[i], cs[i]] = w[i].astype(bf16)
            t_ref[i // HEADS, hd[i]] = t[i].astype(bf16)

    row = lambda c: (c, 0)
    act, act16 = jax.ShapeDtypeStruct((L, W), f32), jax.ShapeDtypeStruct((L, W), bf16)
    return _pcall(
        body, name="gdn_fwd", grid=(nc // cps,),
        in_specs=[pl.BlockSpec((rows_per_step, 3 * W), row), pl.BlockSpec((rows_per_step, LANE), row),
                  pl.BlockSpec((cps, HEADS, CHUNK), lambda c: (c, 0, 0))],
        out_specs=[pl.BlockSpec((rows_per_step, W), row)] * 4 + [
            pl.BlockSpec((cps, HEADS, CHUNK, CHUNK), lambda c: (c, 0, 0, 0)),
            pl.BlockSpec((cps, W, HEAD_DIM), lambda c: (c, 0, 0))],
        out_shape=[act, act16, act16, act16, jax.ShapeDtypeStruct((nc, HEADS, CHUNK, CHUNK), bf16),
                   jax.ShapeDtypeStruct((nc, W, HEAD_DIM), f32)],
        scratch_shapes=[pltpu.VMEM((HEADS, HEAD_DIM, HEAD_DIM), f32)],
        compiler_params=_cparams("arbitrary"),
    )(qkv, sc, gr)


def _gdn_gate(o, proj, gnw):
    L = o.shape[0]

    def body(o_ref, z_ref, w_ref, m_ref):
        for ls in HALVES:
            ov, z = o_ref[:, ls], z_ref[:, ls]
            rms = lax.rsqrt(jnp.mean(ov * ov, axis=-1, keepdims=True) + EPS)
            m_ref[:, ls] = (((ov * rms) * w_ref[...]) * (z * _sigmoid(z))).astype(bf16)

    return _pcall(
        body, name="gdn_gate", grid=(GDN_WIDTH // ELT_W,),
        in_specs=[pl.BlockSpec((L, ELT_W), lambda j: (0, j)), pl.BlockSpec((L, ELT_W), lambda j: (0, OFF_ZG // ELT_W + j)),
                  pl.BlockSpec((1, LANE), lambda j: (0, 0))],
        out_specs=pl.BlockSpec((L, ELT_W), lambda j: (0, j)),
        out_shape=jax.ShapeDtypeStruct((L, GDN_WIDTH + CONV_WIDTH), bf16),
        compiler_params=_cparams("parallel"),
    )(o, proj, gnw)


def _conv3(u, cw_ref, ls):
    return cw_ref[2:3, ls] * u + cw_ref[1:2, ls] * _shift_down(u, 1) + cw_ref[0:1, ls] * _shift_down(u, 2)


def _conv_specs(L):
    return [pl.BlockSpec((L, CONV_BLOCK), lambda j: (0, OFF_CONV // CONV_BLOCK + j)),
            pl.BlockSpec((3, ELT_W), lambda j: (0, j)), pl.BlockSpec((1, ELT_W), lambda j: (0, j))]


def _conv_parts(ls):
    return [slice(g * ELT_W + ls.start, g * ELT_W + ls.stop) for g in range(4)]


def _conv_fwd(proj, cw, cb, mix):
    L = proj.shape[0]

    def body(p_ref, cw_ref, cb_ref, mix_in, m_ref):
        for ls in HALVES:
            sb, sc_, sh, sz = _conv_parts(ls)
            z = p_ref[:, sz]
            cv = _conv3(p_ref[:, sc_] * p_ref[:, sh], cw_ref, ls) + cb_ref[:, ls]
            m_ref[:, ls] = ((p_ref[:, sb] * cv) * (z * _sigmoid(z))).astype(bf16)

    return _pcall(
        body, name="conv_fwd", grid=(CONV_WIDTH // ELT_W,),
        in_specs=_conv_specs(L) + [ANY], out_specs=pl.BlockSpec((L, ELT_W), lambda j: (0, GDN_WIDTH // ELT_W + j)),
        out_shape=jax.ShapeDtypeStruct(mix.shape, mix.dtype), input_output_aliases={3: 0},
        compiler_params=_cparams("parallel"),
    )(proj, cw, cb, mix)


def _out_proj_loss(x, mix, wo, fw, tgt):
    L = x.shape[0]
    tm = min(512, L)
    MW = GDN_WIDTH + CONV_WIDTH

    def body(x_ref, m_ref, wo_ref, fw_ref, t_ref, dy_ref, dyb_ref, dm_ref, gfw_ref, loss_ref):
        @pl.when(pl.program_id(0) == 0)
        def _():
            gfw_ref[...] = jnp.zeros_like(gfw_ref)
            loss_ref[...] = jnp.zeros_like(loss_ref)
        y = x_ref[...] + jnp.dot(m_ref[...], wo_ref[...], preferred_element_type=f32)
        r = lax.rsqrt(jnp.mean(y * y, axis=-1, keepdims=True) + EPS)
        yh = y * r
        fwv = fw_ref[...]
        diff = yh * fwv - t_ref[...]
        loss_ref[...] += jnp.sum(jnp.sum(diff * diff, axis=-1, keepdims=True), axis=0, keepdims=True) * (0.5 / D_MODEL)
        dout = diff * (1.0 / D_MODEL)
        gfw_ref[...] += jnp.sum(dout * yh, axis=0, keepdims=True)
        dyh = dout * fwv
        dy = r * (dyh - yh * jnp.mean(dyh * yh, axis=-1, keepdims=True))
        dy_ref[...] = dy
        dyb = dy.astype(bf16)
        dyb_ref[...] = dyb
        dm_ref[...] = lax.dot_general(dyb, wo_ref[...], (((1,), (1,)), ((), ())), preferred_element_type=f32)

    row = lambda i: (i, 0)
    fix = lambda i: (0, 0)
    act = jax.ShapeDtypeStruct((L, D_MODEL), f32)
    return _pcall(
        body, name="out_proj_loss", grid=(L // tm,),
        in_specs=[pl.BlockSpec((tm, D_MODEL), row), pl.BlockSpec((tm, MW), row), pl.BlockSpec((MW, D_MODEL), fix),
                  pl.BlockSpec((1, D_MODEL), fix), pl.BlockSpec((tm, D_MODEL), row)],
        out_specs=[pl.BlockSpec((tm, D_MODEL), row), pl.BlockSpec((tm, D_MODEL), row), pl.BlockSpec((tm, MW), row),
                   pl.BlockSpec((1, D_MODEL), fix), pl.BlockSpec((1, LANE), fix)],
        out_shape=[act, jax.ShapeDtypeStruct((L, D_MODEL), bf16), jax.ShapeDtypeStruct((L, MW), f32),
                   jax.ShapeDtypeStruct((1, D_MODEL), f32), jax.ShapeDtypeStruct((1, LANE), f32)],
        compiler_params=_cparams("arbitrary"),
    )(x, mix, wo, fw, tgt)


def _tn_matmul(a, b, name):
    L, M = a.shape
    N = b.shape[1]
    tm = 512 if M % 512 == 0 else (768 if M % 768 == 0 else M)

    def body(a_ref, b_ref, o_ref):
        o_ref[...] = lax.dot_general(a_ref[...], b_ref[...], (((0,), (0,)), ((), ())),
                                     preferred_element_type=f32).astype(o_ref.dtype)

    return _pcall(
        body, name=name, grid=(M // tm,),
        in_specs=[pl.BlockSpec((L, tm), lambda i: (0, i)), pl.BlockSpec((L, N), lambda i: (0, 0))],
        out_specs=pl.BlockSpec((tm, N), lambda i: (i, 0)),
        out_shape=jax.ShapeDtypeStruct((M, N), bf16),
        compiler_params=_cparams("parallel"),
    )(a, b)


def _gdn_gate_bwd(o, proj, gnw, dmix_a, after):
    L = o.shape[0]

    def body(o_ref, z_ref, w_ref, dm_ref, after_ref, do_ref, dz_ref, gw_ref):
        @pl.when(pl.program_id(0) == 0)
        def _():
            gw_ref[...] = jnp.zeros_like(gw_ref)
        wv = w_ref[...]
        for ls in HALVES:
            ov, z, dm = o_ref[:, ls], z_ref[:, ls], dm_ref[:, ls]
            rms = lax.rsqrt(jnp.mean(ov * ov, axis=-1, keepdims=True) + EPS)
            xh = ov * rms
            sg = _sigmoid(z)
            d_on = dm * (z * sg)
            dz_ref[:, ls] = (dm * (xh * wv) * (sg * (1.0 + z * (1.0 - sg)))).astype(bf16)
            gw_ref[...] += jnp.sum(d_on * xh, axis=0, keepdims=True)
            dxh = d_on * wv
            do_ref[:, ls] = (rms * (dxh - xh * jnp.mean(dxh * xh, axis=-1, keepdims=True))).astype(bf16)

    wide = pl.BlockSpec((L, ELT_W), lambda j: (0, j))
    return _pcall(
        body, name="gdn_gate_bwd", grid=(GDN_WIDTH // ELT_W,),
        in_specs=[wide, pl.BlockSpec((L, ELT_W), lambda j: (0, OFF_ZG // ELT_W + j)),
                  pl.BlockSpec((1, LANE), lambda j: (0, 0)), wide, ANY],
        out_specs=[wide, pl.BlockSpec((L, ELT_W), lambda j: (0, OFF_ZG // ELT_W + j)),
                   pl.BlockSpec((1, LANE), lambda j: (0, 0))],
        out_shape=[jax.ShapeDtypeStruct((L, GDN_WIDTH), bf16), jax.ShapeDtypeStruct((L, PROJ_PAD), bf16),
                   jax.ShapeDtypeStruct((1, LANE), f32)],
        compiler_params=_cparams("arbitrary"),
    )(o, proj, gnw, dmix_a, after)


def _conv_bwd(proj, cw, cb, dmix_b, dproj):
    L = proj.shape[0]

    def body(p_ref, cw_ref, cb_ref, dm_ref, dproj_in, dp_ref, gcw_ref, gcb_ref):
        for ls in HALVES:
            sb, sc_, sh, sz_ = _conv_parts(ls)
            bv, cv_, hv, z, dm = p_ref[:, sb], p_ref[:, sc_], p_ref[:, sh], p_ref[:, sz_], dm_ref[:, ls]
            u = cv_ * hv
            cv = _conv3(u, cw_ref, ls) + cb_ref[:, ls]
            sg = _sigmoid(z)
            sz = z * sg
            dp_ref[:, sb] = (dm * cv * sz).astype(bf16)
            dp_ref[:, sz_] = (dm * (bv * cv) * (sg * (1.0 + z * (1.0 - sg)))).astype(bf16)
            dcv = dm * bv * sz
            gcb_ref[:, ls] = jnp.sum(dcv, axis=0, keepdims=True)
            dcv1, dcv2 = _shift_up(dcv, 1), _shift_up(dcv, 2)
            gcw_ref[2:3, ls] = jnp.sum(dcv * u, axis=0, keepdims=True)
            gcw_ref[1:2, ls] = jnp.sum(dcv1 * u, axis=0, keepdims=True)
            gcw_ref[0:1, ls] = jnp.sum(dcv2 * u, axis=0, keepdims=True)
            du = cw_ref[2:3, ls] * dcv + cw_ref[1:2, ls] * dcv1 + cw_ref[0:1, ls] * dcv2
            dp_ref[:, sc_] = (du * hv).astype(bf16)
            dp_ref[:, sh] = (du * cv_).astype(bf16)

    return _pcall(
        body, name="conv_bwd", grid=(CONV_WIDTH // ELT_W,),
        in_specs=_conv_specs(L) + [pl.BlockSpec((L, ELT_W), lambda j: (0, GDN_WIDTH // ELT_W + j)), ANY],
        out_specs=[pl.BlockSpec((L, CONV_BLOCK), lambda j: (0, OFF_CONV // CONV_BLOCK + j)),
                   pl.BlockSpec((3, ELT_W), lambda j: (0, j)), pl.BlockSpec((1, ELT_W), lambda j: (0, j))],
        out_shape=[jax.ShapeDtypeStruct(dproj.shape, dproj.dtype), jax.ShapeDtypeStruct((3, CONV_WIDTH), f32),
                   jax.ShapeDtypeStruct((1, CONV_WIDTH), f32)],
        input_output_aliases={4: 0},
        compiler_params=_cparams("parallel"),
    )(proj, cw, cb, dmix_b, dproj)


def _gdn_bwd(qkv, sc, gr, u_all, w_all, vn_all, t_all, sp_all, do_all):
    L = qkv.shape[0]
    nc = L // CHUNK
    W = GDN_WIDTH
    cps = GDN_CPS_BWD if nc % GDN_CPS_BWD == 0 else 1
    rows_per_step = cps * CHUNK
    nsteps = nc // cps

    def body(qkv_ref, sc_ref, gr_ref, u_ref, w_ref, vn_ref, t_ref, sp_ref, do_ref, dqkv_ref, dsc_ref, dgr_ref, ds_scr):
        @pl.when(pl.program_id(0) == 0)
        def _():
            ds_scr[...] = jnp.zeros_like(ds_scr)
        nh, base = HEADS, 0
        HS = range(cps * nh)
        hl = [i % nh for i in HS]
        hd = [base + hl[i] for i in HS]
        rs = [slice((i // nh) * CHUNK, (i // nh + 1) * CHUNK) for i in HS]
        cs = [slice(hd[i] * HEAD_DIM, (hd[i] + 1) * HEAD_DIM) for i in HS]
        q = [qkv_ref[rs[i], hd[i] * HEAD_DIM:(hd[i] + 1) * HEAD_DIM] for i in HS]
        k = [qkv_ref[rs[i], W + hd[i] * HEAD_DIM:W + (hd[i] + 1) * HEAD_DIM] for i in HS]
        v = [qkv_ref[rs[i], 2 * W + hd[i] * HEAD_DIM:2 * W + (hd[i] + 1) * HEAD_DIM] for i in HS]
        hsc = [_head_scalars(sc_ref[rs[i], :], gr_ref, hd[i], i // nh) for i in HS]
        beta, gcc, gl, dmat, dmat_t = ([x[i] for x in hsc] for i in range(5))
        ii, jj = hsc[0][5], hsc[0][6]
        eg = [jnp.exp(gcc[h]) for h in HS]
        ekl = [jnp.exp(gl[h] - gcc[h]) for h in HS]
        egl = [jnp.exp(gl[h]) for h in HS]
        kb = [k[h] * beta[h] for h in HS]
        ks = [k[h] * ekl[h] for h in HS]
        do = [do_ref[rs[h], cs[h]] for h in HS]
        vn = [vn_ref[rs[h], cs[h]] for h in HS]
        s = [sp_ref[h // nh, cs[h], :] for h in HS]
        w = [w_ref[rs[h], cs[h]] for h in HS]
        qd = [q[h] * eg[h] for h in HS]

        kq = [_mm_nt(k[h], q[h]) for h in HS]
        p_t = [jnp.where(jj >= ii, kq[h] * dmat_t[h], 0.0) for h in HS]
        ptd = [_mm(p_t[h], do[h]) for h in HS]
        qw = [_cat16([qd[h], -w[h]], 0) for h in HS]
        dsn, dvn, dodv = [None] * len(HS), [None] * len(HS), [None] * len(HS)
        ds_cur = [ds_scr[base + h] for h in range(nh)]
        for ci in reversed(range(cps)):
            IS = range(ci * nh, (ci + 1) * nh)
            ksd = [_mm(ks[i], ds_cur[hl[i]]) for i in IS]
            for i in IS:
                dsn[i] = ds_cur[hl[i]]
                dvn[i] = ptd[i] + ksd[hl[i]]
                dodv[i] = _cat16([do[i], dvn[i]], 0)
            dsq = [_mm_tn(qw[i], dodv[i]) for i in IS]
            ds_cur = [egl[i] * ds_cur[hl[i]] + dsq[hl[i]] for i in IS]
        for h in range(nh):
            ds_scr[base + h] = ds_cur[h]
        x1 = [_mm_nt(dodv[h], s[h]) for h in HS]
        dks = [_mm_nt(vn[h], dsn[h]) for h in HS]
        dov = [_mm_nt(do[h], vn[h]) for h in HS]
        vdo = [_mm_nt(vn[h], do[h]) for h in HS]
        kk = [_mm_nt(kb[h], k[h]) for h in HS]
        qk = [_mm_nt(q[h], k[h]) for h in HS]
        dgl = [egl[h] * jnp.sum(jnp.sum(s[h] * dsn[h], axis=1, keepdims=True), axis=0, keepdims=True) for h in HS]
        dqd = [x1[h][:CHUNK] for h in HS]
        duw = [jnp.concatenate([dvn[h], -x1[h][CHUNK:]], axis=1) for h in HS]
        tdu = [_mm_tn(t_ref[h // nh, hd[h]], duw[h]) for h in HS]
        dvk = [duw[h] + tdu[h] for h in HS]
        uw = [jnp.concatenate([u_ref[rs[h], cs[h]], w[h]], axis=1) for h in HS]
        da = [-jnp.where(ii > jj, _mm_nt(dvk[h], uw[h]), 0.0) for h in HS]
        da_t = [-jnp.where(jj > ii, _mm_nt(uw[h], dvk[h]), 0.0) for h in HS]
        dp = [jnp.where(ii >= jj, dov[h], 0.0) for h in HS]
        dp_t = [jnp.where(jj >= ii, vdo[h], 0.0) for h in HS]
        r1 = [_mm(_cat16([da[h] * dmat[h], dp[h] * dmat[h]], 0), k[h]) for h in HS]
        dk1 = [_mm(_cat16([da_t[h] * dmat_t[h], dp_t[h] * dmat_t[h]], 1), _cat16([kb[h], q[h]], 0)) for h in HS]
        lane = _lanes((CHUNK, LANE))
        for ci in range(cps):
            dsc = jnp.zeros((CHUNK, LANE), f32)
            for i in range(ci * nh, (ci + 1) * nh):
                h = hd[i]
                a = jnp.where(ii > jj, kk[i] * dmat[i], 0.0)
                p = jnp.where(ii >= jj, qk[i] * dmat[i], 0.0)
                gmat = da[i] * a + dp[i] * p
                dvb, dkbg = dvk[i][:, :HEAD_DIM], dvk[i][:, HEAD_DIM:]
                kbg = kb[i] * eg[i]
                dkb = r1[i][:CHUNK] + dkbg * eg[i]
                dq = r1[i][CHUNK:] + dqd[i] * eg[i]
                dk = dk1[i] + dks[i] * ekl[i] + dkb * beta[i]
                dbeta = jnp.sum(dkb * k[i] + dvb * v[i], axis=1, keepdims=True)
                ksum = jnp.sum(dks[i] * ks[i], axis=1, keepdims=True)
                dgl_tot = dgl[i] + jnp.sum(ksum, axis=0, keepdims=True)
                dgc = (jnp.sum(gmat, axis=1, keepdims=True) + jnp.sum(dqd[i] * qd[i] + dkbg * kbg, axis=1, keepdims=True)
                       - ksum)
                dgc = dgc + jnp.where(_rows(dgc.shape) == CHUNK - 1, dgl_tot, 0.0)
                dqkv_ref[rs[i], h * HEAD_DIM:(h + 1) * HEAD_DIM] = dq
                dqkv_ref[rs[i], W + h * HEAD_DIM:W + (h + 1) * HEAD_DIM] = dk
                dqkv_ref[rs[i], 2 * W + h * HEAD_DIM:2 * W + (h + 1) * HEAD_DIM] = dvb * beta[i]
                dsc = jnp.where(lane == h, dbeta, jnp.where(lane == HEADS + h, dgc, dsc))
                dgr_ref[ci, h:h + 1, :] = jnp.sum(gmat, axis=0, keepdims=True)
            dsc_ref[ci * CHUNK:(ci + 1) * CHUNK, :] = dsc

    row = lambda c: (nsteps - 1 - c, 0)
    lead3 = lambda c: (nsteps - 1 - c, 0, 0)
    return _pcall(
        body, name="gdn_bwd", grid=(nsteps,),
        in_specs=[pl.BlockSpec((rows_per_step, 3 * W), row), pl.BlockSpec((rows_per_step, LANE), row),
                  pl.BlockSpec((cps, HEADS, CHUNK), lead3),
                  pl.BlockSpec((rows_per_step, W), row), pl.BlockSpec((rows_per_step, W), row),
                  pl.BlockSpec((rows_per_step, W), row),
                  pl.BlockSpec((cps, HEADS, CHUNK, CHUNK), lambda c: (nsteps - 1 - c, 0, 0, 0)),
                  pl.BlockSpec((cps, W, HEAD_DIM), lead3), pl.BlockSpec((rows_per_step, W), row)],
        out_specs=[pl.BlockSpec((rows_per_step, 3 * W), row), pl.BlockSpec((rows_per_step, LANE), row),
                   pl.BlockSpec((cps, HEADS, CHUNK), lead3)],
        out_shape=[jax.ShapeDtypeStruct((L, 3 * W), f32), jax.ShapeDtypeStruct((L, LANE), f32),
                   jax.ShapeDtypeStruct((nc, HEADS, CHUNK), f32)],
        scratch_shapes=[pltpu.VMEM((HEADS, HEAD_DIM, HEAD_DIM), f32)],
        compiler_params=_cparams("arbitrary"),
    )(qkv, sc, gr, u_all, w_all, vn_all, t_all, sp_all, do_all)


def _qkv_bwd(proj, cw, dn, dproj):
    L = proj.shape[0]

    def body(x_ref, cw_ref, dn_ref, dproj_in, dx_ref, gcw_ref):
        j = pl.program_id(0)
        steps = GDN_WIDTH // ELT_W
        scale = jnp.where(j < steps, HEAD_DIM ** -0.5, 1.0).astype(f32)
        for ls in HALVES:
            x, dn_v = x_ref[:, ls], dn_ref[:, ls]
            c = _conv4(x, cw_ref, ls)
            sg = _sigmoid(c)
            a = c * sg
            rn = lax.rsqrt(jnp.sum(a * a, axis=1, keepdims=True) + EPS)
            da_n = (scale * rn) * (dn_v - a * ((rn * rn) * jnp.sum(dn_v * a, axis=1, keepdims=True)))
            da = jnp.where(j < 2 * steps, da_n, dn_v)
            dc = da * (sg * (1.0 + c * (1.0 - sg)))
            dc1, dc2, dc3 = _shift_up(dc, 1), _shift_up(dc, 2), _shift_up(dc, 3)
            gcw_ref[3:4, ls] = jnp.sum(dc * x, axis=0, keepdims=True)
            gcw_ref[2:3, ls] = jnp.sum(dc1 * x, axis=0, keepdims=True)
            gcw_ref[1:2, ls] = jnp.sum(dc2 * x, axis=0, keepdims=True)
            gcw_ref[0:1, ls] = jnp.sum(dc3 * x, axis=0, keepdims=True)
            dx = cw_ref[3:4, ls] * dc + cw_ref[2:3, ls] * dc1 + cw_ref[1:2, ls] * dc2 + cw_ref[0:1, ls] * dc3
            dx_ref[:, ls] = dx.astype(bf16)

    col = pl.BlockSpec((L, ELT_W), lambda j: (0, j))
    wspec = pl.BlockSpec((4, ELT_W), lambda j: (0, j))
    return _pcall(
        body, name="qkv_bwd", grid=(3 * GDN_WIDTH // ELT_W,),
        in_specs=[col, wspec, col, ANY], out_specs=[col, wspec],
        out_shape=[jax.ShapeDtypeStruct(dproj.shape, dproj.dtype), jax.ShapeDtypeStruct((4, 3 * GDN_WIDTH), f32)],
        input_output_aliases={3: 0},
        compiler_params=_cparams("parallel"),
    )(proj, cw, dn, dproj)


def _scalars_bwd(proj, alog_p, dtb_p, dsc, dgr_col, dproj, after):
    L = proj.shape[0]

    def body(x_ref, al_ref, dt_ref, dsc_ref, dgr_ref, dproj_in, after_ref, dba_ref, gs_ref):
        x, dsc_v = x_ref[...], dsc_ref[...]
        lane = _lanes(x.shape)
        dec = (lane >= HEADS) & (lane < 2 * HEADS)
        dg = jnp.where(dec, dsc_v - dgr_ref[...], 0.0)
        rc = _rows(x.shape) & (CHUNK - 1)
        for s in (1, 2, 4, 8, 16, 32):
            dg = dg + jnp.where(rc + s < CHUNK, pltpu.roll(dg, L - s, 0), 0.0)
        xa = x + dt_ref[...]
        ea = jnp.exp(al_ref[...])
        g = -ea * _softplus(xa)
        da = dg * (-ea) * _sigmoid(xa)
        beta = _sigmoid(x)
        db = dsc_v * beta * (1.0 - beta)
        dba_ref[:, :LANE] = jnp.where(lane < HEADS, db, jnp.where(dec, da, 0.0)).astype(bf16)
        dba_ref[:, LANE:] = jnp.zeros((L, ELT_W - LANE), bf16)
        g_al = jnp.sum(jnp.where(dec, dg * g, 0.0), axis=0, keepdims=True)
        g_dt = jnp.sum(jnp.where(dec, da, 0.0), axis=0, keepdims=True)
        row8 = _rows(gs_ref.shape)
        gs = jnp.where(row8 == 0, g_al, jnp.where(row8 == 1, g_dt, 0.0))
        gs_ref[...] = pltpu.roll(gs, LANE - HEADS, 1)

    full = pl.BlockSpec((L, LANE), lambda i: (0, 0))
    vec = pl.BlockSpec((1, LANE), lambda i: (0, 0))
    return _pcall(
        body, name="scalars_bwd", grid=(1,),
        in_specs=[pl.BlockSpec((L, LANE), lambda i: (0, OFF_BA // LANE)), vec, vec, full, full, ANY, ANY],
        out_specs=[pl.BlockSpec((L, ELT_W), lambda i: (0, OFF_BA // ELT_W)), pl.BlockSpec((8, LANE), lambda i: (0, 0))],
        out_shape=[jax.ShapeDtypeStruct(dproj.shape, dproj.dtype), jax.ShapeDtypeStruct((8, LANE), f32)],
        input_output_aliases={5: 0},
        compiler_params=_cparams("arbitrary"),
    )(proj, alog_p, dtb_p, dsc, dgr_col, dproj, after)


def _input_grad(dproj, wpad, x, nw, dy, after):
    L = x.shape[0]
    tm = min(512, L)
    cuts = (0, 3072, 5120, 7168, PROJ_PAD)
    nk = len(cuts) - 1

    def body(dp_ref, w_hbm, x_ref, nw_ref, dy_ref, after_ref, gx_ref, gnw_ref, w_vmem, sems):
        first = pl.program_id(0) == 0
        loads = [pltpu.make_async_copy(w_hbm.at[cuts[k]:cuts[k + 1], :], w_vmem.at[cuts[k]:cuts[k + 1], :], sems.at[k])
                 for k in range(nk)]

        @pl.when(first)
        def _():
            for cp in loads:
                cp.start()
            gnw_ref[...] = jnp.zeros_like(gnw_ref)
        dh = None
        for k in range(nk):
            pl.when(first)(loads[k].wait)
            part = jnp.dot(dp_ref[:, cuts[k]:cuts[k + 1]], w_vmem[cuts[k]:cuts[k + 1], :], preferred_element_type=f32)
            dh = part if dh is None else dh + part
        xv, nwv = x_ref[...], nw_ref[...]
        r = lax.rsqrt(jnp.mean(xv * xv, axis=-1, keepdims=True) + EPS)
        xh = xv * r
        gnw_ref[...] += jnp.sum(dh * xh, axis=0, keepdims=True)
        dxh = dh * nwv
        gx_ref[...] = dy_ref[...] + r * (dxh - xh * jnp.mean(dxh * xh, axis=-1, keepdims=True))

    row = lambda i: (i, 0)
    fix = lambda i: (0, 0)
    return _pcall(
        body, name="input_grad", grid=(L // tm,),
        in_specs=[pl.BlockSpec((tm, PROJ_PAD), row), ANY, pl.BlockSpec((tm, D_MODEL), row),
                  pl.BlockSpec((1, D_MODEL), fix), pl.BlockSpec((tm, D_MODEL), row), ANY],
        out_specs=[pl.BlockSpec((tm, D_MODEL), row), pl.BlockSpec((1, D_MODEL), fix)],
        out_shape=[jax.ShapeDtypeStruct((L, D_MODEL), f32), jax.ShapeDtypeStruct((1, D_MODEL), f32)],
        scratch_shapes=[pltpu.VMEM(wpad.shape, bf16), pltpu.SemaphoreType.DMA((nk,))],
        compiler_params=_cparams("arbitrary"),
    )(dproj, wpad, x, nw, dy, after)


def _adamw_reduce(parts, w, m, v, name):
    R, C = w.shape
    n_parts = parts.shape[0]
    tr = 128 if R % 128 == 0 else R
    c1 = 1.0 - ADAM_B1 ** ADAM_STEP
    c2 = 1.0 - ADAM_B2 ** ADAM_STEP

    def body(p_ref, w_ref, m_ref, v_ref, g_ref, d_ref, nm_ref, nv_ref):
        g = p_ref[0].astype(f32)
        for s in range(1, n_parts):
            g = g + p_ref[s].astype(f32)
        nm = ADAM_B1 * m_ref[...] + (1.0 - ADAM_B1) * g
        nv = ADAM_B2 * v_ref[...] + (1.0 - ADAM_B2) * (g * g)
        g_ref[...] = g
        nm_ref[...] = nm
        nv_ref[...] = nv
        d_ref[...] = -ADAM_LR * ((nm / c1) / (jnp.sqrt(nv / c2) + ADAM_EPS) + ADAM_WD * w_ref[...])

    blk = pl.BlockSpec((tr, C), lambda i: (i, 0))
    out = jax.ShapeDtypeStruct((R, C), f32)
    return _pcall(
        body, name=name, grid=(R // tr,),
        in_specs=[pl.BlockSpec((n_parts, tr, C), lambda i: (0, i, 0)), blk, blk, blk],
        out_specs=[blk] * 4, out_shape=[out] * 4,
        compiler_params=_cparams("parallel"),
    )(parts, w, m, v)


SMALL_SLOTS = ((0, D_MODEL), (D_MODEL, D_MODEL), (2 * D_MODEL, D_MODEL), (3 * D_MODEL, LANE),
               (3 * D_MODEL + LANE, HEADS), (3 * D_MODEL + 2 * LANE, HEADS))
SMALL_LOSS = 3 * D_MODEL + 3 * LANE
SMALL_W = SMALL_LOSS + LANE


def _pack_small(gs, after):
    def body(nw_ref, cb_ref, fw_ref, gn_ref, sc_ref, ls_ref, after_ref, o_ref):
        for ref, (start, width) in zip((nw_ref, cb_ref, fw_ref, gn_ref), SMALL_SLOTS[:4]):
            o_ref[:, start:start + width] = ref[...]
        o_ref[:, SMALL_SLOTS[4][0]:SMALL_SLOTS[4][0] + LANE] = sc_ref[0:1, :]
        o_ref[:, SMALL_SLOTS[5][0]:SMALL_SLOTS[5][0] + LANE] = sc_ref[1:2, :]
        o_ref[:, SMALL_LOSS:SMALL_W] = ls_ref[...]

    vm = pl.BlockSpec(memory_space=pltpu.VMEM)
    return _pcall(body, name="pack_small_grads", out_shape=jax.ShapeDtypeStruct((1, SMALL_W), f32),
                  in_specs=[vm] * 6 + [ANY], out_specs=vm)(*gs, after)


def _adamw_small(parts, ws, ms, vs):
    c1 = 1.0 - ADAM_B1 ** ADAM_STEP
    c2 = 1.0 - ADAM_B2 ** ADAM_STEP
    np_ = len(ws)

    def body(*refs):
        p_ref = refs[0]
        w_refs, m_refs, v_refs = refs[1:1 + np_], refs[1 + np_:1 + 2 * np_], refs[1 + 2 * np_:1 + 3 * np_]
        outs = refs[1 + 3 * np_:]
        g_refs, d_refs, nm_refs, nv_refs = (outs[i * np_:(i + 1) * np_] for i in range(4))
        loss_ref = outs[4 * np_]

        def total(start, width):
            t = p_ref[0, :, start:start + width]
            for s in range(1, N_DEV):
                t = t + p_ref[s, :, start:start + width]
            return t

        for i, (start, width) in enumerate(SMALL_SLOTS):
            g = total(start, width)
            nm = ADAM_B1 * m_refs[i][...] + (1.0 - ADAM_B1) * g
            nv = ADAM_B2 * v_refs[i][...] + (1.0 - ADAM_B2) * (g * g)
            g_refs[i][...] = g
            nm_refs[i][...] = nm
            nv_refs[i][...] = nv
            d_refs[i][...] = -ADAM_LR * ((nm / c1) / (jnp.sqrt(nv / c2) + ADAM_EPS) + ADAM_WD * w_refs[i][...])
        loss_ref[...] = total(SMALL_LOSS, LANE)

    vm = pl.BlockSpec(memory_space=pltpu.VMEM)
    shapes = [jax.ShapeDtypeStruct(w.shape, f32) for w in ws]
    res = _pcall(body, name="adamw_small", out_shape=shapes * 4 + [jax.ShapeDtypeStruct((1, LANE), f32)],
                 in_specs=[vm] * (1 + 3 * np_), out_specs=[vm] * (4 * np_ + 1))(parts, *ws, *ms, *vs)
    return [res[i * np_:(i + 1) * np_] for i in range(4)], res[4 * np_]


def _adamw_w_in(part_a, part_b, w3, m3, v3, after):
    _, n, _ = part_a.shape
    c1 = 1.0 - ADAM_B1 ** ADAM_STEP
    c2 = 1.0 - ADAM_B2 ** ADAM_STEP

    def body(pa_ref, pb_ref, w_ref, m_ref, v_ref, after_ref, g_ref, d_ref, nm_ref, nv_ref):
        g = pa_ref[0].astype(f32) + pb_ref[0].astype(f32)
        nm = ADAM_B1 * m_ref[:, 0, :] + (1.0 - ADAM_B1) * g
        nv = ADAM_B2 * v_ref[:, 0, :] + (1.0 - ADAM_B2) * (g * g)
        g_ref[:, 0, :] = g
        nm_ref[:, 0, :] = nm
        nv_ref[:, 0, :] = nv
        d_ref[:, 0, :] = -ADAM_LR * ((nm / c1) / (jnp.sqrt(nv / c2) + ADAM_EPS) + ADAM_WD * w_ref[:, 0, :])

    tile = 2 * COL_TILE
    blk = pl.BlockSpec((n, 1, tile), lambda j: (0, 0, j))
    out = jax.ShapeDtypeStruct((n, 1, D_MODEL), f32)
    return _pcall(
        body, name="adamw_w_in", grid=(D_MODEL // tile,),
        in_specs=[pl.BlockSpec((1, n, tile), lambda j: (0, 0, j))] * 2 + [blk, blk, blk, ANY],
        out_specs=[blk] * 4, out_shape=[out] * 4,
        compiler_params=_cparams("parallel"),
    )(part_a, part_b, w3, m3, v3, after)


def _pad_lanes(vec8, start):
    return jnp.pad(vec8.reshape(1, -1), ((0, 0), (start, LANE - start - vec8.size)))


def kernel(x, norm_in_w, w_in, conv_qkv_w, A_log, dt_bias, gdn_norm_w, conv_w, conv_b, w_out, final_norm_w, loss_target, m_norm_in_w, m_w_in, m_conv_qkv_w, m_A_log, m_dt_bias, m_gdn_norm_w, m_conv_w, m_conv_b, m_w_out, m_final_norm_w, v_norm_in_w, v_w_in, v_conv_qkv_w, v_A_log, v_dt_bias, v_gdn_norm_w, v_conv_w, v_conv_b, v_w_out, v_final_norm_w):
    L = x.shape[1]
    nc = L // CHUNK
    xs = x[0]
    tgt = loss_target[0]
    fnw = final_norm_w.reshape(1, D_MODEL)

    as_rows = lambda a: jnp.transpose(a, (2, 0, 1))
    win_g, cqkv_g, cw_g = _all_gather([_cast_w_in(as_rows(w_in)), conv_qkv_w[0], conv_w[0]], "gather_weights",
                                      pieces=[4, 1, 1])
    wpad = _relayout_w_in(win_g)
    cqkv = jnp.concatenate([cqkv_g[d] for d in range(N_DEV)], axis=1)
    cw = jnp.concatenate([cw_g[d] for d in range(N_DEV)], axis=1)
    alog_p = _pad_lanes(A_log, HEADS)
    dtb_p = _pad_lanes(dt_bias, HEADS)
    tok = lambda started: started[4]
    wo_started = _spread_start(w_out[0].astype(bf16), wpad, "gather", "gather_w_out_start")

    proj, h = _in_proj(xs, norm_in_w, wpad, tok(wo_started))
    qkv = _qkv_act(proj, cqkv)
    sc, gr = _scalars(proj, alog_p, dtb_p)
    o, u_all, w_all, vn_all, t_all, sp_all = _gdn_fwd(qkv, sc, gr)
    mix = _conv_fwd(proj, cw, conv_b, _gdn_gate(o, proj, gdn_norm_w))
    wo = _spread_wait(wo_started, mix, "gather", "gather_w_out_wait")[1].reshape(-1, D_MODEL)
    dy, dyb, dmix, g_fnw, loss_v = _out_proj_loss(xs, mix, wo, fnw, tgt)

    g_wout = _tn_matmul(mix, dyb, "grad_w_out")
    gwo_started = _spread_start(g_wout.reshape(N_DEV, -1, D_MODEL), dyb, "scatter", "exchange_grad_w_out_start")
    do, dproj, g_gnw = _gdn_gate_bwd(o, proj, gdn_norm_w, dmix, tok(gwo_started))
    dproj, g_cw, g_cb = _conv_bwd(proj, cw, conv_b, dmix, dproj)
    dqkv_n, dsc, dgr = _gdn_bwd(qkv, sc, gr, u_all, w_all, vn_all, t_all, sp_all, do)
    dproj, g_cqkv = _qkv_bwd(proj, cqkv, dqkv_n, dproj)
    g_cqkv_blk = g_cqkv.reshape(4, N_DEV, -1).transpose(1, 0, 2)
    g_cw_blk = jnp.pad(g_cw.reshape(3, N_DEV, -1).transpose(1, 0, 2),
                       ((0, 0), (0, 1), (0, g_cqkv_blk.shape[2] - g_cw.shape[1] // N_DEV)))
    gsm_started = _spread_start(jnp.concatenate([g_cqkv_blk, g_cw_blk], axis=1), g_cqkv, "scatter",
                                "exchange_small_sharded_grads_start")
    dgr_col = jnp.pad(dgr.transpose(0, 2, 1).reshape(L, HEADS), ((0, 0), (HEADS, LANE - 2 * HEADS)))
    dproj, g_sc = _scalars_bwd(proj, alog_p, dtb_p, dsc, dgr_col, dproj, tok(gsm_started))
    g_win_blk = _grad_blocks(_tn_matmul(dproj, h, "grad_w_in"))

    (p_win,) = _pair_exchange([g_win_blk], "exchange_grads_pair")
    r_small = _spread_wait(gsm_started, p_win, "scatter", "exchange_small_sharded_grads_wait")[1]
    r_cqkv, r_cw = r_small[:, :4, :], r_small[:, 4:7, :g_cw.shape[1] // N_DEV]
    s_win = _pair_sum(g_win_blk, p_win, "pair_sum_w_in")
    gw1_started = _spread_start(s_win, r_small, "axis_a", "exchange_grads_axis1_start")
    grad_x, g_nw = _input_grad(dproj, wpad, xs, norm_in_w, dy, tok(gw1_started))
    s_thru, got1 = _spread_wait(gw1_started, grad_x, "axis_a", "exchange_grads_axis1_wait")
    t_win = _axis_sum(s_thru, got1, "axis_sum_w_in")
    gw2_started = _spread_start(t_win, got1, "axis_b", "exchange_grads_axis2_start")

    r_wout = _spread_wait(gwo_started, tok(gw2_started), "scatter", "exchange_grad_w_out_wait")[1]
    upd_wout =_adamw_reduce(r_wout, w_out[0], m_w_out[0], v_w_out[0], "adamw_w_out")
    upd_cqkv = _adamw_reduce(r_cqkv, conv_qkv_w[0], m_conv_qkv_w[0], v_conv_qkv_w[0], "adamw_conv_qkv_w")
    upd_cw = _adamw_reduce(r_cw, conv_w[0], m_conv_w[0], v_conv_w[0], "adamw_conv_w")

    small_g = _pack_small([g_nw, g_cb, g_fnw, g_gnw, g_sc, loss_v], upd_cw[0])
    gsg_started = _spread_start(small_g, upd_cw[0], "gather", "gather_small_grads_start")

    t_thru, got2 = _spread_wait(gw2_started, tok(gsg_started), "axis_b", "exchange_grads_axis2_wait")
    upd_win_t = _adamw_w_in(t_thru, got2, as_rows(w_in), as_rows(m_w_in), as_rows(v_w_in), got2)
    upd_win = [jnp.transpose(a, (1, 2, 0)) for a in upd_win_t]
    small_all = _spread_wait(gsg_started, upd_win_t[0], "gather", "gather_small_grads_wait")[1]
    fvec = lambda a: a.reshape(1, D_MODEL)
    upd_small, loss_sum = _adamw_small(
        small_all,
        [norm_in_w, conv_b, fvec(final_norm_w), gdn_norm_w, A_log, dt_bias],
        [m_norm_in_w, m_conv_b, fvec(m_final_norm_w), m_gdn_norm_w, m_A_log, m_dt_bias],
        [v_norm_in_w, v_conv_b, fvec(v_final_norm_w), v_gdn_norm_w, v_A_log, v_dt_bias])

    outs = [loss_sum[0, 0], grad_x[None]]
    for k in range(4):
        nw_k, cb_k, fw_k, gn_k, al_k, dt_k = upd_small[k]
        outs += [nw_k, upd_win[k], upd_cqkv[k][None], al_k, dt_k, gn_k,
                 upd_cw[k][None], cb_k, upd_wout[k][None], fw_k.reshape(D_MODEL)]
    return tuple(outs)
```

```python
import jax
import jax.numpy as jnp
from jax import lax
from jax.experimental import pallas as pl
from jax.experimental.pallas import tpu as pltpu

f32 = jnp.float32
bf16 = jnp.bfloat16

N_DEV = 8
D_MODEL = 1024
HEADS = 8
HEAD_DIM = 128
CHUNK = 64
GDN_CPS = 4
GDN_CPS_BWD = 1
GDN_WIDTH = HEADS * HEAD_DIM
CONV_WIDTH = 1024
PROJ_WIDTH = 8208
SHARD_W = PROJ_WIDTH // N_DEV
EPS = 1e-6

LANE = 128
ELT_W = 256

OFF_QKV, OFF_ZG, OFF_CONV, OFF_BA = 0, 3072, 4096, 8192
CONV_BLOCK = 4 * ELT_W
PROJ_PAD = 8448
NAT_BA, NAT_CONV = 4096, 4112


def _padded_col(n):
    if n < NAT_BA:
        return n
    if n < NAT_CONV:
        return OFF_BA + n - NAT_BA
    g, ch = divmod(n - NAT_CONV, CONV_WIDTH)
    j, r = divmod(ch, ELT_W)
    return OFF_CONV + CONV_BLOCK * j + ELT_W * g + r


def _layout_segments(n0, n1):
    cuts = [NAT_BA, NAT_CONV] + [NAT_CONV + ELT_W * k for k in range(1, 4 * CONV_WIDTH // ELT_W)]
    pts = [n0] + [c for c in cuts if n0 < c < n1] + [n1]
    return [(lo, hi - lo, _padded_col(lo)) for lo, hi in zip(pts, pts[1:])]

ADAM_LR, ADAM_B1, ADAM_B2, ADAM_EPS, ADAM_WD, ADAM_STEP = 0.001, 0.9, 0.999, 1e-08, 0.01, 10

V7X_VMEM_BYTES = 64 * 1024 * 1024
VMEM_LIMIT = V7X_VMEM_BYTES - 8 * 1024 * 1024

MESH = pl.DeviceIdType.MESH
ANY = pl.BlockSpec(memory_space=pl.ANY)


def _pcall(body, **kw):
    return pl.pallas_call(body, **kw)


def _cparams(*sem):
    return pltpu.CompilerParams(dimension_semantics=sem if sem else None, vmem_limit_bytes=VMEM_LIMIT)


def _mm(a, b):
    return jnp.dot(a.astype(bf16), b.astype(bf16), preferred_element_type=f32)


def _mm_nt(a, b):
    return lax.dot_general(a.astype(bf16), b.astype(bf16), (((1,), (1,)), ((), ())), preferred_element_type=f32)


def _cat16(parts, axis):
    return jnp.concatenate([p.astype(bf16) for p in parts], axis=axis)


def _mm_tn(a, b):
    return lax.dot_general(a.astype(bf16), b.astype(bf16), (((0,), (0,)), ((), ())), preferred_element_type=f32)


def _rows(shape):
    return lax.broadcasted_iota(jnp.int32, shape, 0)


def _lanes(shape):
    return lax.broadcasted_iota(jnp.int32, shape, 1)


def _shift_down(x, s):
    if s == 0:
        return x
    return jnp.where(_rows(x.shape) >= s, pltpu.roll(x, s, 0), 0.0)


def _shift_up(x, s):
    if s == 0:
        return x
    n = x.shape[0]
    return jnp.where(_rows(x.shape) < n - s, pltpu.roll(x, n - s, 0), 0.0)


def _sigmoid(x):
    return jax.nn.sigmoid(x)


def _softplus(x):
    e = jnp.exp(-jnp.abs(x))
    small = e * (1.0 - e * (0.5 - e * (1.0 / 3.0)))
    return jnp.maximum(x, 0.0) + jnp.where(e < 0.01, small, jnp.log(1.0 + e))


def _mesh_pos():
    return lax.axis_index("x"), lax.axis_index("y"), lax.axis_index("c")


def _flat(px, py, pc):
    return 4 * px + 2 * py + pc


def _all_gather(xs, name, pieces=None):
    n = len(xs)
    pieces = pieces or [1] * n
    items = [(a, q) for a in range(n) for q in range(pieces[a])]
    ni = len(items)

    def view(ref, i):
        a, q = items[i]
        if pieces[a] == 1:
            return ref
        wd = xs[a].shape[-1] // pieces[a]
        return ref.at[(slice(None),) * (xs[a].ndim - 1) + (pl.ds(q * wd, wd),)]

    def body(*refs):
        x_refs, o_refs = refs[:n], refs[n:2 * n]
        send_sems, recv_sems, local_sems = refs[2 * n:]
        x, y, c = _mesh_pos()
        me, sibling = (x, y, c), (x, y, 1 - c)
        flip = lambda v, bit: v + bit - 2 * v * bit
        nbr_a = (flip(x, 1 - c), flip(y, c))
        nbr_b = (flip(x, c), flip(y, 1 - c))
        diag = (1 - x, 1 - y)

        def copy(i, k, block, to, own=False):
            a = items[i][0]
            dst = view(o_refs[a].at[_flat(*block)], i)
            return pltpu.make_async_remote_copy(
                src_ref=view(x_refs[a], i) if own else dst, dst_ref=dst,
                send_sem=send_sems.at[i, k], recv_sem=recv_sems.at[i, k], device_id=to, device_id_type=MESH)

        mine, sent = [], []

        def go(cp):
            cp.start()
            sent.append(cp)

        for a in range(n):
            cp = pltpu.make_async_copy(x_refs[a], o_refs[a].at[_flat(*me)], local_sems.at[a])
            cp.start()
            mine.append(cp)
        for a in range(ni):
            go(copy(a, 1, me, (*nbr_a, c), own=True))
            go(copy(a, 2, me, (*nbr_b, c), own=True))
            go(copy(a, 0, me, sibling, own=True))
        for a in range(ni):
            copy(a, 1, (*nbr_a, c), me).wait_recv()
            go(copy(a, 3, (*nbr_a, c), (*nbr_b, c)))
            go(copy(a, 4, (*nbr_a, c), sibling))
        for a in range(ni):
            copy(a, 2, (*nbr_b, c), me).wait_recv()
            go(copy(a, 5, (*nbr_b, c), sibling))
        for a in range(ni):
            copy(a, 3, (*diag, c), me).wait_recv()
            go(copy(a, 6, (*diag, c), sibling))
        for a in range(ni):
            copy(a, 0, sibling, me).wait_recv()
            copy(a, 4, (*nbr_b, 1 - c), me).wait_recv()
            copy(a, 5, (*nbr_a, 1 - c), me).wait_recv()
            copy(a, 6, (*diag, 1 - c), me).wait_recv()
        for cp in sent:
            cp.wait_send()
        for cp in mine:
            cp.wait()

    outs = _pcall(
        body, name=name,
        out_shape=[jax.ShapeDtypeStruct((N_DEV,) + a.shape, a.dtype) for a in xs],
        in_specs=[ANY] * n, out_specs=[ANY] * n,
        scratch_shapes=[pltpu.SemaphoreType.DMA((ni, 7)), pltpu.SemaphoreType.DMA((ni, 7)), pltpu.SemaphoreType.DMA((n,))],
    )(*xs)
    return list(outs)


def _pair_exchange(gs, name):
    n = len(gs)
    chips = [(0, 0), (0, 1), (1, 0), (1, 1)]

    def body(*refs):
        g_refs, o_refs = refs[:n], refs[n:2 * n]
        send_sems, recv_sems = refs[2 * n:]
        x, y, c = _mesh_pos()
        sibling = (x, y, 1 - c)

        def copy(a, i):
            xp, yp = chips[i]
            return pltpu.make_async_remote_copy(
                src_ref=g_refs[a].at[_flat(xp, yp, 1 - c)], dst_ref=o_refs[a].at[i],
                send_sem=send_sems.at[a, i], recv_sem=recv_sems.at[a, i], device_id=sibling, device_id_type=MESH)

        cps = [copy(a, i) for a in range(n) for i in range(4)]
        for cp in cps:
            cp.start()
        for cp in cps:
            cp.wait()

    outs = _pcall(
        body, name=name,
        out_shape=[jax.ShapeDtypeStruct((4,) + a.shape[1:], a.dtype) for a in gs],
        in_specs=[ANY] * n, out_specs=[ANY] * n,
        scratch_shapes=[pltpu.SemaphoreType.DMA((n, 4)), pltpu.SemaphoreType.DMA((n, 4))],
    )(*gs)
    return list(outs)


def _pair_sum(g, p1, name):
    _, R, C = g.shape
    tr = 256 if R % 256 == 0 else R
    cidx = lax.axis_index("c").astype(jnp.int32).reshape(1)

    def body(c_ref, g_ref, p_ref, o_ref):
        o_ref[...] = (g_ref[...].astype(f32) + p_ref[...].astype(f32)).astype(o_ref.dtype)

    return _pcall(
        body, name=name,
        grid_spec=pltpu.PrefetchScalarGridSpec(
            num_scalar_prefetch=1, grid=(4, R // tr),
            in_specs=[pl.BlockSpec((1, tr, C), lambda i, r, c_ref: (2 * i + c_ref[0], r, 0)),
                      pl.BlockSpec((1, tr, C), lambda i, r, c_ref: (i, r, 0))],
            out_specs=pl.BlockSpec((1, tr, C), lambda i, r, c_ref: (i, r, 0))),
        out_shape=jax.ShapeDtypeStruct((4, R, C), g.dtype),
        compiler_params=_cparams("parallel", "parallel"),
    )(cidx, g, p1)


def _axis_sum(s, got, name):
    _, R, C = s.shape
    x, y, c = _mesh_pos()
    me, _, b, _ = _axis_chips(x, y, c)
    idx = jnp.stack([2 * me[0] + me[1], 2 * b[0] + b[1]]).astype(jnp.int32)

    def body(idx_ref, s_ref, g_ref, o_ref):
        o_ref[...] = (s_ref[...].astype(f32) + g_ref[...].astype(f32)).astype(o_ref.dtype)

    return _pcall(
        body, name=name,
        grid_spec=pltpu.PrefetchScalarGridSpec(
            num_scalar_prefetch=1, grid=(2,),
            in_specs=[pl.BlockSpec((1, R, C), lambda k, idx_ref: (idx_ref[k], 0, 0)),
                      pl.BlockSpec((1, R, C), lambda k, idx_ref: (k, 0, 0))],
            out_specs=pl.BlockSpec((1, R, C), lambda k, idx_ref: (k, 0, 0))),
        out_shape=jax.ShapeDtypeStruct((2, R, C), s.dtype),
        compiler_params=_cparams("parallel"),
    )(idx, s, got)


HBM = pl.BlockSpec(memory_space=pltpu.HBM)
SEM = pl.BlockSpec(memory_space=pltpu.SEMAPHORE)
EFFECT = pltpu.SideEffectType.DATAFLOW_SIDE_EFFECTING


def _peers(x, y, c):
    out = []
    for k in range(1, N_DEV):
        kx, ky, kc = (k >> 2) & 1, (k >> 1) & 1, k & 1
        out.append(((1 - x) if kx else x, (1 - y) if ky else y, (1 - c) if kc else c))
    return out


SPREAD_COPIES = {"gather": N_DEV - 1, "scatter": N_DEV - 1, "axis_a": 2, "axis_b": 1}
SPREAD_SLOTS = {"axis_a": 2, "axis_b": 1}


def _axis_chips(x, y, c):
    flip = lambda v, bit: v + bit - 2 * v * bit
    return (x, y), (flip(x, 1 - c), flip(y, c)), (flip(x, c), flip(y, 1 - c)), (1 - x, 1 - y)


def _spread_copy(src_ref, land_ref, send_sems, recv_sems, k, plan):
    x, y, c = _mesh_pos()
    if plan in ("axis_a", "axis_b"):
        _, a, b, d = _axis_chips(x, y, c)
        chip = lambda p: 2 * p[0] + p[1]
        peer = (*(a if plan == "axis_a" else b), c)
        src = src_ref.at[chip(a) if k == 0 else chip(d)] if plan == "axis_a" else src_ref.at[1]
        slot = k
    else:
        peer = _peers(x, y, c)[k]
        src, slot = (src_ref.at[_flat(*peer)] if plan == "scatter" else src_ref), _flat(x, y, c)
    return pltpu.make_async_remote_copy(
        src_ref=src, dst_ref=land_ref.at[slot], send_sem=send_sems.at[k], recv_sem=recv_sems.at[k],
        device_id=peer, device_id_type=MESH)


def _own_copy(src_ref, land_ref, send_sems, plan):
    me = _flat(*_mesh_pos())
    return pltpu.make_async_copy(src_ref.at[me] if plan == "scatter" else src_ref, land_ref.at[me],
                                 send_sems.at[SPREAD_COPIES[plan]])


def _spread_start(src, after, plan, name):
    land_shape = (N_DEV,) + src.shape if plan == "gather" else src.shape
    if plan in SPREAD_SLOTS:
        land_shape = (SPREAD_SLOTS[plan],) + src.shape[1:]
    n_copies = SPREAD_COPIES[plan]

    def body(src_ref, land_ref, after_ref, send_sems, recv_sems, src_thru, land_thru, token):
        for k in range(n_copies):
            _spread_copy(src_ref, land_ref, send_sems, recv_sems, k, plan).start()
        if plan not in SPREAD_SLOTS:
            _own_copy(src_ref, land_ref, send_sems, plan).start()
        token[...] = jnp.zeros_like(token)

    return _pcall(
        body, name=name,
        out_shape=(pltpu.SemaphoreType.DMA((n_copies + (plan not in SPREAD_SLOTS),)), pltpu.SemaphoreType.DMA((n_copies,)),
                   pltpu.HBM(src.shape, src.dtype), pltpu.HBM(land_shape, src.dtype), jax.ShapeDtypeStruct((8, LANE), f32)),
        in_specs=(HBM, HBM, ANY), out_specs=(SEM, SEM, HBM, HBM, pl.BlockSpec(memory_space=pltpu.VMEM)),
        input_output_aliases={0: 2, 1: 3},
        compiler_params=pltpu.CompilerParams(has_side_effects=EFFECT),
    )(pltpu.with_memory_space_constraint(src, pltpu.HBM),
      pltpu.with_memory_space_constraint(lax.empty(land_shape, src.dtype), pltpu.HBM), after)


def _spread_wait(started, after, plan, name):
    send_sems, recv_sems, src_thru, land_thru, _ = started

    def body(src_ref, land_ref, send_sems, recv_sems, after_ref, src_dead, got_ref):
        for k in range(SPREAD_COPIES[plan]):
            cp = _spread_copy(src_ref, land_ref, send_sems, recv_sems, k, plan)
            cp.wait_send()
            cp.wait_recv()
        if plan not in SPREAD_SLOTS:
            _own_copy(src_ref, land_ref, send_sems, plan).wait()

    return _pcall(
        body, name=name,
        out_shape=(pltpu.HBM(src_thru.shape, src_thru.dtype), pltpu.HBM(land_thru.shape, land_thru.dtype)),
        in_specs=(HBM, HBM, SEM, SEM, ANY), out_specs=(HBM, HBM), input_output_aliases={0: 0, 1: 1},
        compiler_params=pltpu.CompilerParams(has_side_effects=EFFECT),
    )(src_thru, land_thru, send_sems, recv_sems, after)


COL_TILE = 256


def _cast_w_in(w3):
    n = w3.shape[0]

    def body(w_ref, o_ref):
        o_ref[...] = w_ref[:, 0, :].astype(bf16)

    tile = 2 * COL_TILE
    return _pcall(
        body, name="cast_w_in", grid=(D_MODEL // tile,),
        in_specs=[pl.BlockSpec((n, 1, tile), lambda j: (0, 0, j))],
        out_specs=pl.BlockSpec((n, tile), lambda j: (0, j)),
        out_shape=jax.ShapeDtypeStruct((n, D_MODEL), bf16),
        compiler_params=_cparams("parallel"),
    )(w3)


def _relayout_w_in(win_g):
    def body(g_ref, o_ref):
        used = OFF_BA + NAT_CONV - NAT_BA
        o_ref[used:PROJ_PAD, :] = jnp.zeros((PROJ_PAD - used, COL_TILE), o_ref.dtype)
        for d in range(N_DEV):
            for lo, width, dst in _layout_segments(d * SHARD_W, (d + 1) * SHARD_W):
                src = lo - d * SHARD_W
                o_ref[dst:dst + width, :] = g_ref[d, src:src + width, :]

    return _pcall(
        body, name="relayout_w_in", grid=(D_MODEL // COL_TILE,),
        in_specs=[pl.BlockSpec((N_DEV, SHARD_W, COL_TILE), lambda j: (0, 0, j))],
        out_specs=pl.BlockSpec((PROJ_PAD, COL_TILE), lambda j: (0, j)),
        out_shape=jax.ShapeDtypeStruct((PROJ_PAD, D_MODEL), win_g.dtype),
        compiler_params=_cparams("parallel"),
    )(win_g)


def _grad_blocks(g_t):
    def body(p_ref, o_ref):
        for d in range(N_DEV):
            for lo, width, src in _layout_segments(d * SHARD_W, (d + 1) * SHARD_W):
                dst = lo - d * SHARD_W
                o_ref[d, dst:dst + width, :] = p_ref[src:src + width, :]

    return _pcall(
        body, name="grad_blocks", grid=(D_MODEL // COL_TILE,),
        in_specs=[pl.BlockSpec((PROJ_PAD, COL_TILE), lambda j: (0, j))],
        out_specs=pl.BlockSpec((N_DEV, SHARD_W, COL_TILE), lambda j: (0, 0, j)),
        out_shape=jax.ShapeDtypeStruct((N_DEV, SHARD_W, D_MODEL), bf16),
        compiler_params=_cparams("parallel"),
    )(g_t)


def _in_proj(x, nw, wpad_t, after):
    L = x.shape[0]
    tn = 768
    nj = wpad_t.shape[0] // tn

    def body(x_ref, nw_ref, w_ref, after_ref, proj_ref, h_ref):
        first = pl.program_id(0) == 0

        def project(r, n, hv):
            proj_ref[r:r + n, :] = lax.dot_general(hv, w_ref[...], (((1,), (1,)), ((), ())), preferred_element_type=f32)

        @pl.when(first)
        def _():
            for r in range(0, L, 256):
                xs = x_ref[r:r + 256, :]
                ms = jnp.mean(xs * xs, axis=-1, keepdims=True)
                hv = ((xs * lax.rsqrt(ms + EPS)) * nw_ref[...]).astype(bf16)
                h_ref[r:r + 256, :] = hv
                project(r, 256, hv)

        @pl.when(jnp.logical_not(first))
        def _():
            for r in range(0, L, 512):
                project(r, 512, h_ref[r:r + 512, :])

    return _pcall(
        body, name="in_proj", grid=(nj,),
        in_specs=[pl.BlockSpec((L, D_MODEL), lambda j: (0, 0)), pl.BlockSpec((1, D_MODEL), lambda j: (0, 0)),
                  pl.BlockSpec((tn, D_MODEL), lambda j: (j, 0)), ANY],
        out_specs=[pl.BlockSpec((L, tn), lambda j: (0, j)), pl.BlockSpec((L, D_MODEL), lambda j: (0, 0))],
        out_shape=[jax.ShapeDtypeStruct((L, wpad_t.shape[0]), f32), jax.ShapeDtypeStruct((L, D_MODEL), bf16)],
        compiler_params=_cparams("arbitrary"),
    )(x, nw, wpad_t, after)


HALVES = [slice(i * LANE, (i + 1) * LANE) for i in range(ELT_W // LANE)]
QKV_W = 512
QKV_HEADS = [slice(i * LANE, (i + 1) * LANE) for i in range(QKV_W // LANE)]
STEPS_PER_GROUP = GDN_WIDTH // QKV_W


def _conv4(x, cw_ref, ls):
    return (cw_ref[3:4, ls] * x + cw_ref[2:3, ls] * _shift_down(x, 1) + cw_ref[1:2, ls] * _shift_down(x, 2)
            + cw_ref[0:1, ls] * _shift_down(x, 3))


def _qkv_act(proj, cw):
    L = proj.shape[0]

    def body(x_ref, cw_ref, o_ref):
        j = pl.program_id(0)
        scale = jnp.where(j < STEPS_PER_GROUP, HEAD_DIM ** -0.5, 1.0).astype(f32)
        for ls in QKV_HEADS:
            c = _conv4(x_ref[:, ls], cw_ref, ls)
            a = c * _sigmoid(c)
            rn = lax.rsqrt(jnp.sum(a * a, axis=1, keepdims=True) + EPS)
            o_ref[:, ls] = jnp.where(j < 2 * STEPS_PER_GROUP, (a * rn) * scale, a)

    return _pcall(
        body, name="qkv_act", grid=(3 * STEPS_PER_GROUP,),
        in_specs=[pl.BlockSpec((L, QKV_W), lambda j: (0, j)), pl.BlockSpec((4, QKV_W), lambda j: (0, j))],
        out_specs=pl.BlockSpec((L, QKV_W), lambda j: (0, j)),
        out_shape=jax.ShapeDtypeStruct((L, 3 * GDN_WIDTH), f32),
        compiler_params=_cparams("parallel"),
    )(proj, cw)


def _scalars(proj, alog_p, dtb_p):
    L = proj.shape[0]
    nc = L // CHUNK

    def body(x_ref, al_ref, dt_ref, sc_ref, gr_ref):
        x = x_ref[...]
        lane = _lanes(x.shape)
        beta = _sigmoid(x)
        g = -jnp.exp(al_ref[...]) * _softplus(x + dt_ref[...])
        gc = jnp.where((lane >= HEADS) & (lane < 2 * HEADS), g, 0.0)
        rc = _rows(x.shape) & (CHUNK - 1)
        for s in (1, 2, 4, 8, 16, 32):
            gc = gc + jnp.where(rc >= s, pltpu.roll(gc, s, 0), 0.0)
        sc_ref[...] = jnp.where(lane < HEADS, beta, gc)
        sel = (_lanes((HEADS, LANE)) == _rows((HEADS, LANE)) + HEADS).astype(f32)
        for c in range(nc):
            gr_ref[c] = lax.dot_general(sel, sc_ref[c * CHUNK:(c + 1) * CHUNK, :], (((1,), (1,)), ((), ())),
                                        preferred_element_type=f32, precision=lax.Precision.HIGHEST)

    return _pcall(
        body, name="scalars", grid=(1,),
        in_specs=[pl.BlockSpec((L, LANE), lambda i: (0, OFF_BA // LANE)), pl.BlockSpec((1, LANE), lambda i: (0, 0)),
                  pl.BlockSpec((1, LANE), lambda i: (0, 0))],
        out_specs=[pl.BlockSpec((L, LANE), lambda i: (0, 0)), pl.BlockSpec((nc, HEADS, CHUNK), lambda i: (0, 0, 0))],
        out_shape=[jax.ShapeDtypeStruct((L, LANE), f32), jax.ShapeDtypeStruct((nc, HEADS, CHUNK), f32)],
        compiler_params=_cparams("arbitrary"),
    )(proj, alog_p, dtb_p)


def _head_scalars(sc, gr_ref, h, ci=0):
    lane = _lanes(sc.shape)
    beta = jnp.sum(jnp.where(lane == h, sc, 0.0), axis=1, keepdims=True)
    gcc = jnp.sum(jnp.where(lane == HEADS + h, sc, 0.0), axis=1, keepdims=True)
    gcr = gr_ref[ci, h:h + 1, :]
    gl = jnp.sum(jnp.where(_lanes(gcr.shape) == CHUNK - 1, gcr, 0.0), axis=1, keepdims=True)
    ii, jj = _rows((CHUNK, CHUNK)), _lanes((CHUNK, CHUNK))
    dmat = jnp.where(ii >= jj, jnp.exp(jnp.minimum(gcc - gcr, 0.0)), 0.0)
    dmat_t = jnp.where(jj >= ii, jnp.exp(jnp.minimum(gcr - gcc, 0.0)), 0.0)
    return beta, gcc, gl, dmat, dmat_t, ii, jj


def _gdn_fwd(qkv, sc, gr):
    L = qkv.shape[0]
    nc = L // CHUNK
    W = GDN_WIDTH
    cps = GDN_CPS if nc % GDN_CPS == 0 else 1
    rows_per_step = cps * CHUNK

    def body(qkv_ref, sc_ref, gr_ref, o_ref, u_ref, w_ref, vn_ref, t_ref, sp_ref, s_scr):
        @pl.when(pl.program_id(0) == 0)
        def _():
            s_scr[...] = jnp.zeros_like(s_scr)
        HS = range(cps * HEADS)
        hd = [i % HEADS for i in HS]
        rs = [slice((i // HEADS) * CHUNK, (i // HEADS + 1) * CHUNK) for i in HS]
        cs = [slice(hd[i] * HEAD_DIM, (hd[i] + 1) * HEAD_DIM) for i in HS]
        q = [qkv_ref[rs[i], hd[i] * HEAD_DIM:(hd[i] + 1) * HEAD_DIM] for i in HS]
        k = [qkv_ref[rs[i], W + hd[i] * HEAD_DIM:W + (hd[i] + 1) * HEAD_DIM] for i in HS]
        v = [qkv_ref[rs[i], 2 * W + hd[i] * HEAD_DIM:2 * W + (hd[i] + 1) * HEAD_DIM] for i in HS]
        hsc = [_head_scalars(sc_ref[rs[i], :], gr_ref, hd[i], i // HEADS) for i in HS]
        beta, gcc, gl, dmat = ([x[i] for x in hsc] for i in range(4))
        ii, jj = hsc[0][5], hsc[0][6]
        eg = [jnp.exp(gcc[h]) for h in HS]
        kb = [k[h] * beta[h] for h in HS]
        kk = [_mm_nt(kb[h], k[h]) for h in HS]
        qk = [_mm_nt(q[h], k[h]) for h in HS]
        n0 = [-jnp.where(ii > jj, kk[h] * dmat[h], 0.0) for h in HS]
        n1 = [_mm(n0[h], n0[h]) for h in HS]
        n2 = [_mm(n1[h], n1[h]) for h in HS]
        p01 = [n0[h] + n1[h] + _mm(n0[h], n1[h]) for h in HS]
        n3 = [_mm(n2[h], n2[h]) for h in HS]
        n4 = [_mm(n3[h], n3[h]) for h in HS]
        p23 = [n2[h] + n3[h] + _mm(n2[h], n3[h]) for h in HS]
        n5 = [_mm(n4[h], n4[h]) for h in HS]
        p03 = [p01[h] + p23[h] + _mm(p01[h], p23[h]) for h in HS]
        p45 = [n4[h] + n5[h] + _mm(n4[h], n5[h]) for h in HS]
        t = [p03[h] + p45[h] + _mm(p03[h], p45[h]) for h in HS]
        vb = [v[h] * beta[h] for h in HS]
        kbg = [kb[h] * eg[h] for h in HS]
        uw = [_mm(t[h], _cat16([vb[h], kbg[h]], 1)) for h in HS]
        u = [vb[h] + uw[h][:, :HEAD_DIM] for h in HS]
        w = [kbg[h] + uw[h][:, HEAD_DIM:] for h in HS]
        wq = [_cat16([w[h], q[h] * eg[h]], 0) for h in HS]
        p = [jnp.where(ii >= jj, qk[h] * dmat[h], 0.0) for h in HS]
        ks = [k[h] * jnp.exp(gl[h] - gcc[h]) for h in HS]
        s = [s_scr[h] for h in range(HEADS)]
        for ci in range(cps):
            IS = range(ci * HEADS, (ci + 1) * HEADS)
            ws = [_mm(wq[i], s[hd[i]]) for i in IS]
            vn = [u[i] - ws[hd[i]][:CHUNK] for i in IS]
            pv = [_mm(p[i], vn[hd[i]]) for i in IS]
            kv = [_mm_tn(ks[i], vn[hd[i]]) for i in IS]
            for i in IS:
                h = hd[i]
                sp_ref[ci, cs[i], :] = s[h]
                o_ref[rs[i], cs[i]] = ws[h][CHUNK:] + pv[h]
                vn_ref[rs[i], cs[i]] = vn[h].astype(bf16)
            s = [jnp.exp(gl[i]) * s[hd[i]] + kv[hd[i]] for i in IS]
        for h in range(HEADS):
            s_scr[h] = s[h]
        for i in HS:
            u_ref[rs[i], cs[i]] = u[i].astype(bf16)
            w_ref[rs[i], cs[i]] = w[i].astype(bf16)
            t_ref[i // HEADS, hd[i]] = t[i].astype(bf16)

    row = lambda c: (c, 0)
    act, act16 = jax.ShapeDtypeStruct((L, W), f32), jax.ShapeDtypeStruct((L, W), bf16)
    return _pcall(
        body, name="gdn_fwd", grid=(nc // cps,),
        in_specs=[pl.BlockSpec((rows_per_step, 3 * W), row), pl.BlockSpec((rows_per_step, LANE), row),
                  pl.BlockSpec((cps, HEADS, CHUNK), lambda c: (c, 0, 0))],
        out_specs=[pl.BlockSpec((rows_per_step, W), row)] * 4 + [
            pl.BlockSpec((cps, HEADS, CHUNK, CHUNK), lambda c: (c, 0, 0, 0)),
            pl.BlockSpec((cps, W, HEAD_DIM), lambda c: (c, 0, 0))],
        out_shape=[act, act16, act16, act16, jax.ShapeDtypeStruct((nc, HEADS, CHUNK, CHUNK), bf16),
                   jax.ShapeDtypeStruct((nc, W, HEAD_DIM), f32)],
        scratch_shapes=[pltpu.VMEM((HEADS, HEAD_DIM, HEAD_DIM), f32)],
        compiler_params=_cparams("arbitrary"),
    )(qkv, sc, gr)


def _gdn_gate(o, proj, gnw):
    L = o.shape[0]

    def body(o_ref, z_ref, w_ref, m_ref):
        for ls in HALVES:
            ov, z = o_ref[:, ls], z_ref[:, ls]
            rms = lax.rsqrt(jnp.mean(ov * ov, axis=-1, keepdims=True) + EPS)
            m_ref[:, ls] = (((ov * rms) * w_ref[...]) * (z * _sigmoid(z))).astype(bf16)

    return _pcall(
        body, name="gdn_gate", grid=(GDN_WIDTH // ELT_W,),
        in_specs=[pl.BlockSpec((L, ELT_W), lambda j: (0, j)), pl.BlockSpec((L, ELT_W), lambda j: (0, OFF_ZG // ELT_W + j)),
                  pl.BlockSpec((1, LANE), lambda j: (0, 0))],
        out_specs=pl.BlockSpec((L, ELT_W), lambda j: (0, j)),
        out_shape=jax.ShapeDtypeStruct((L, GDN_WIDTH + CONV_WIDTH), bf16),
        compiler_params=_cparams("parallel"),
    )(o, proj, gnw)


def _conv3(u, cw_ref, ls):
    return cw_ref[2:3, ls] * u + cw_ref[1:2, ls] * _shift_down(u, 1) + cw_ref[0:1, ls] * _shift_down(u, 2)


def _conv_specs(L):
    return [pl.BlockSpec((L, CONV_BLOCK), lambda j: (0, OFF_CONV // CONV_BLOCK + j)),
            pl.BlockSpec((3, ELT_W), lambda j: (0, j)), pl.BlockSpec((1, ELT_W), lambda j: (0, j))]


def _conv_parts(ls):
    return [slice(g * ELT_W + ls.start, g * ELT_W + ls.stop) for g in range(4)]


def _conv_fwd(proj, cw, cb, mix):
    L = proj.shape[0]

    def body(p_ref, cw_ref, cb_ref, mix_in, m_ref):
        for ls in HALVES:
            sb, sc_, sh, sz = _conv_parts(ls)
            z = p_ref[:, sz]
            cv = _conv3(p_ref[:, sc_] * p_ref[:, sh], cw_ref, ls) + cb_ref[:, ls]
            m_ref[:, ls] = ((p_ref[:, sb] * cv) * (z * _sigmoid(z))).astype(bf16)

    return _pcall(
        body, name="conv_fwd", grid=(CONV_WIDTH // ELT_W,),
        in_specs=_conv_specs(L) + [ANY], out_specs=pl.BlockSpec((L, ELT_W), lambda j: (0, GDN_WIDTH // ELT_W + j)),
        out_shape=jax.ShapeDtypeStruct(mix.shape, mix.dtype), input_output_aliases={3: 0},
        compiler_params=_cparams("parallel"),
    )(proj, cw, cb, mix)


def _out_proj_loss(x, mix, wo, fw, tgt):
    L = x.shape[0]
    tm = min(512, L)
    MW = GDN_WIDTH + CONV_WIDTH

    def body(x_ref, m_ref, wo_ref, fw_ref, t_ref, dy_ref, dyb_ref, dm_ref, gfw_ref, loss_ref):
        @pl.when(pl.program_id(0) == 0)
        def _():
            gfw_ref[...] = jnp.zeros_like(gfw_ref)
            loss_ref[...] = jnp.zeros_like(loss_ref)
        y = x_ref[...] + jnp.dot(m_ref[...], wo_ref[...], preferred_element_type=f32)
        r = lax.rsqrt(jnp.mean(y * y, axis=-1, keepdims=True) + EPS)
        yh = y * r
        fwv = fw_ref[...]
        diff = yh * fwv - t_ref[...]
        loss_ref[...] += jnp.sum(jnp.sum(diff * diff, axis=-1, keepdims=True), axis=0, keepdims=True) * (0.5 / D_MODEL)
        dout = diff * (1.0 / D_MODEL)
        gfw_ref[...] += jnp.sum(dout * yh, axis=0, keepdims=True)
        dyh = dout * fwv
        dy = r * (dyh - yh * jnp.mean(dyh * yh, axis=-1, keepdims=True))
        dy_ref[...] = dy
        dyb = dy.astype(bf16)
        dyb_ref[...] = dyb
        dm_ref[...] = lax.dot_general(dyb, wo_ref[...], (((1,), (1,)), ((), ())), preferred_element_type=f32)

    row = lambda i: (i, 0)
    fix = lambda i: (0, 0)
    act = jax.ShapeDtypeStruct((L, D_MODEL), f32)
    return _pcall(
        body, name="out_proj_loss", grid=(L // tm,),
        in_specs=[pl.BlockSpec((tm, D_MODEL), row), pl.BlockSpec((tm, MW), row), pl.BlockSpec((MW, D_MODEL), fix),
                  pl.BlockSpec((1, D_MODEL), fix), pl.BlockSpec((tm, D_MODEL), row)],
        out_specs=[pl.BlockSpec((tm, D_MODEL), row), pl.BlockSpec((tm, D_MODEL), row), pl.BlockSpec((tm, MW), row),
                   pl.BlockSpec((1, D_MODEL), fix), pl.BlockSpec((1, LANE), fix)],
        out_shape=[act, jax.ShapeDtypeStruct((L, D_MODEL), bf16), jax.ShapeDtypeStruct((L, MW), f32),
                   jax.ShapeDtypeStruct((1, D_MODEL), f32), jax.ShapeDtypeStruct((1, LANE), f32)],
        compiler_params=_cparams("arbitrary"),
    )(x, mix, wo, fw, tgt)


def _tn_matmul(a, b, name):
    L, M = a.shape
    N = b.shape[1]
    tm = 512 if M % 512 == 0 else (768 if M % 768 == 0 else M)

    def body(a_ref, b_ref, o_ref):
        o_ref[...] = lax.dot_general(a_ref[...], b_ref[...], (((0,), (0,)), ((), ())),
                                     preferred_element_type=f32).astype(o_ref.dtype)

    return _pcall(
        body, name=name, grid=(M // tm,),
        in_specs=[pl.BlockSpec((L, tm), lambda i: (0, i)), pl.BlockSpec((L, N), lambda i: (0, 0))],
        out_specs=pl.BlockSpec((tm, N), lambda i: (i, 0)),
        out_shape=jax.ShapeDtypeStruct((M, N), bf16),
        compiler_params=_cparams("parallel"),
    )(a, b)


def _gdn_gate_bwd(o, proj, gnw, dmix_a, after):
    L = o.shape[0]

    def body(o_ref, z_ref, w_ref, dm_ref, after_ref, do_ref, dz_ref, gw_ref):
        @pl.when(pl.program_id(0) == 0)
        def _():
            gw_ref[...] = jnp.zeros_like(gw_ref)
        wv = w_ref[...]
        for ls in HALVES:
            ov, z, dm = o_ref[:, ls], z_ref[:, ls], dm_ref[:, ls]
            rms = lax.rsqrt(jnp.mean(ov * ov, axis=-1, keepdims=True) + EPS)
            xh = ov * rms
            sg = _sigmoid(z)
            d_on = dm * (z * sg)
            dz_ref[:, ls] = (dm * (xh * wv) * (sg * (1.0 + z * (1.0 - sg)))).astype(bf16)
            gw_ref[...] += jnp.sum(d_on * xh, axis=0, keepdims=True)
            dxh = d_on * wv
            do_ref[:, ls] = (rms * (dxh - xh * jnp.mean(dxh * xh, axis=-1, keepdims=True))).astype(bf16)

    wide = pl.BlockSpec((L, ELT_W), lambda j: (0, j))
    return _pcall(
        body, name="gdn_gate_bwd", grid=(GDN_WIDTH // ELT_W,),
        in_specs=[wide, pl.BlockSpec((L, ELT_W), lambda j: (0, OFF_ZG // ELT_W + j)),
                  pl.BlockSpec((1, LANE), lambda j: (0, 0)), wide, ANY],
        out_specs=[wide, pl.BlockSpec((L, ELT_W), lambda j: (0, OFF_ZG // ELT_W + j)),
                   pl.BlockSpec((1, LANE), lambda j: (0, 0))],
        out_shape=[jax.ShapeDtypeStruct((L, GDN_WIDTH), bf16), jax.ShapeDtypeStruct((L, PROJ_PAD), bf16),
                   jax.ShapeDtypeStruct((1, LANE), f32)],
        compiler_params=_cparams("arbitrary"),
    )(o, proj, gnw, dmix_a, after)


def _conv_bwd(proj, cw, cb, dmix_b, dproj):
    L = proj.shape[0]

    def body(p_ref, cw_ref, cb_ref, dm_ref, dproj_in, dp_ref, gcw_ref, gcb_ref):
        for ls in HALVES:
            sb, sc_, sh, sz_ = _conv_parts(ls)
            bv, cv_, hv, z, dm = p_ref[:, sb], p_ref[:, sc_], p_ref[:, sh], p_ref[:, sz_], dm_ref[:, ls]
            u = cv_ * hv
            cv = _conv3(u, cw_ref, ls) + cb_ref[:, ls]
            sg = _sigmoid(z)
            sz = z * sg
            dp_ref[:, sb] = (dm * cv * sz).astype(bf16)
            dp_ref[:, sz_] = (dm * (bv * cv) * (sg * (1.0 + z * (1.0 - sg)))).astype(bf16)
            dcv = dm * bv * sz
            gcb_ref[:, ls] = jnp.sum(dcv, axis=0, keepdims=True)
            dcv1, dcv2 = _shift_up(dcv, 1), _shift_up(dcv, 2)
            gcw_ref[2:3, ls] = jnp.sum(dcv * u, axis=0, keepdims=True)
            gcw_ref[1:2, ls] = jnp.sum(dcv1 * u, axis=0, keepdims=True)
            gcw_ref[0:1, ls] = jnp.sum(dcv2 * u, axis=0, keepdims=True)
            du = cw_ref[2:3, ls] * dcv + cw_ref[1:2, ls] * dcv1 + cw_ref[0:1, ls] * dcv2
            dp_ref[:, sc_] = (du * hv).astype(bf16)
            dp_ref[:, sh] = (du * cv_).astype(bf16)

    return _pcall(
        body, name="conv_bwd", grid=(CONV_WIDTH // ELT_W,),
        in_specs=_conv_specs(L) + [pl.BlockSpec((L, ELT_W), lambda j: (0, GDN_WIDTH // ELT_W + j)), ANY],
        out_specs=[pl.BlockSpec((L, CONV_BLOCK), lambda j: (0, OFF_CONV // CONV_BLOCK + j)),
                   pl.BlockSpec((3, ELT_W), lambda j: (0, j)), pl.BlockSpec((1, ELT_W), lambda j: (0, j))],
        out_shape=[jax.ShapeDtypeStruct(dproj.shape, dproj.dtype), jax.ShapeDtypeStruct((3, CONV_WIDTH), f32),
                   jax.ShapeDtypeStruct((1, CONV_WIDTH), f32)],
        input_output_aliases={4: 0},
        compiler_params=_cparams("parallel"),
    )(proj, cw, cb, dmix_b, dproj)


def _gdn_bwd(qkv, sc, gr, u_all, w_all, vn_all, t_all, sp_all, do_all):
    L = qkv.shape[0]
    nc = L // CHUNK
    W = GDN_WIDTH
    cps = GDN_CPS_BWD if nc % GDN_CPS_BWD == 0 else 1
    rows_per_step = cps * CHUNK
    nsteps = nc // cps

    def body(qkv_ref, sc_ref, gr_ref, u_ref, w_ref, vn_ref, t_ref, sp_ref, do_ref, dqkv_ref, dsc_ref, dgr_ref, ds_scr):
        @pl.when(pl.program_id(0) == 0)
        def _():
            ds_scr[...] = jnp.zeros_like(ds_scr)
        nh, base = HEADS, 0
        HS = range(cps * nh)
        hl = [i % nh for i in HS]
        hd = [base + hl[i] for i in HS]
        rs = [slice((i // nh) * CHUNK, (i // nh + 1) * CHUNK) for i in HS]
        cs = [slice(hd[i] * HEAD_DIM, (hd[i] + 1) * HEAD_DIM) for i in HS]
        q = [qkv_ref[rs[i], hd[i] * HEAD_DIM:(hd[i] + 1) * HEAD_DIM] for i in HS]
        k = [qkv_ref[rs[i], W + hd[i] * HEAD_DIM:W + (hd[i] + 1) * HEAD_DIM] for i in HS]
        v = [qkv_ref[rs[i], 2 * W + hd[i] * HEAD_DIM:2 * W + (hd[i] + 1) * HEAD_DIM] for i in HS]
        hsc = [_head_scalars(sc_ref[rs[i], :], gr_ref, hd[i], i // nh) for i in HS]
        beta, gcc, gl, dmat, dmat_t = ([x[i] for x in hsc] for i in range(5))
        ii, jj = hsc[0][5], hsc[0][6]
        eg = [jnp.exp(gcc[h]) for h in HS]
        ekl = [jnp.exp(gl[h] - gcc[h]) for h in HS]
        egl = [jnp.exp(gl[h]) for h in HS]
        kb = [k[h] * beta[h] for h in HS]
        ks = [k[h] * ekl[h] for h in HS]
        do = [do_ref[rs[h], cs[h]] for h in HS]
        vn = [vn_ref[rs[h], cs[h]] for h in HS]
        s = [sp_ref[h // nh, cs[h], :] for h in HS]
        w = [w_ref[rs[h], cs[h]] for h in HS]
        qd = [q[h] * eg[h] for h in HS]

        kq = [_mm_nt(k[h], q[h]) for h in HS]
        p_t = [jnp.where(jj >= ii, kq[h] * dmat_t[h], 0.0) for h in HS]
        ptd = [_mm(p_t[h], do[h]) for h in HS]
        qw = [_cat16([qd[h], -w[h]], 0) for h in HS]
        dsn, dvn, dodv = [None] * len(HS), [None] * len(HS), [None] * len(HS)
        ds_cur = [ds_scr[base + h] for h in range(nh)]
        for ci in reversed(range(cps)):
            IS = range(ci * nh, (ci + 1) * nh)
            ksd = [_mm(ks[i], ds_cur[hl[i]]) for i in IS]
            for i in IS:
                dsn[i] = ds_cur[hl[i]]
                dvn[i] = ptd[i] + ksd[hl[i]]
                dodv[i] = _cat16([do[i], dvn[i]], 0)
            dsq = [_mm_tn(qw[i], dodv[i]) for i in IS]
            ds_cur = [egl[i] * ds_cur[hl[i]] + dsq[hl[i]] for i in IS]
        for h in range(nh):
            ds_scr[base + h] = ds_cur[h]
        x1 = [_mm_nt(dodv[h], s[h]) for h in HS]
        dks = [_mm_nt(vn[h], dsn[h]) for h in HS]
        dov = [_mm_nt(do[h], vn[h]) for h in HS]
        vdo = [_mm_nt(vn[h], do[h]) for h in HS]
        kk = [_mm_nt(kb[h], k[h]) for h in HS]
        qk = [_mm_nt(q[h], k[h]) for h in HS]
        dgl = [egl[h] * jnp.sum(jnp.sum(s[h] * dsn[h], axis=1, keepdims=True), axis=0, keepdims=True) for h in HS]
        dqd = [x1[h][:CHUNK] for h in HS]
        duw = [jnp.concatenate([dvn[h], -x1[h][CHUNK:]], axis=1) for h in HS]
        tdu = [_mm_tn(t_ref[h // nh, hd[h]], duw[h]) for h in HS]
        dvk = [duw[h] + tdu[h] for h in HS]
        uw = [jnp.concatenate([u_ref[rs[h], cs[h]], w[h]], axis=1) for h in HS]
        da = [-jnp.where(ii > jj, _mm_nt(dvk[h], uw[h]), 0.0) for h in HS]
        da_t = [-jnp.where(jj > ii, _mm_nt(uw[h], dvk[h]), 0.0) for h in HS]
        dp = [jnp.where(ii >= jj, dov[h], 0.0) for h in HS]
        dp_t = [jnp.where(jj >= ii, vdo[h], 0.0) for h in HS]
        r1 = [_mm(_cat16([da[h] * dmat[h], dp[h] * dmat[h]], 0), k[h]) for h in HS]
        dk1 = [_mm(_cat16([da_t[h] * dmat_t[h], dp_t[h] * dmat_t[h]], 1), _cat16([kb[h], q[h]], 0)) for h in HS]
        lane = _lanes((CHUNK, LANE))
        for ci in range(cps):
            dsc = jnp.zeros((CHUNK, LANE), f32)
            for i in range(ci * nh, (ci + 1) * nh):
                h = hd[i]
                a = jnp.where(ii > jj, kk[i] * dmat[i], 0.0)
                p = jnp.where(ii >= jj, qk[i] * dmat[i], 0.0)
                gmat = da[i] * a + dp[i] * p
                dvb, dkbg = dvk[i][:, :HEAD_DIM], dvk[i][:, HEAD_DIM:]
                kbg = kb[i] * eg[i]
                dkb = r1[i][:CHUNK] + dkbg * eg[i]
                dq = r1[i][CHUNK:] + dqd[i] * eg[i]
                dk = dk1[i] + dks[i] * ekl[i] + dkb * beta[i]
                dbeta = jnp.sum(dkb * k[i] + dvb * v[i], axis=1, keepdims=True)
                ksum = jnp.sum(dks[i] * ks[i], axis=1, keepdims=True)
                dgl_tot = dgl[i] + jnp.sum(ksum, axis=0, keepdims=True)
                dgc = (jnp.sum(gmat, axis=1, keepdims=True) + jnp.sum(dqd[i] * qd[i] + dkbg * kbg, axis=1, keepdims=True)
                       - ksum)
                dgc = dgc + jnp.where(_rows(dgc.shape) == CHUNK - 1, dgl_tot, 0.0)
                dqkv_ref[rs[i], h * HEAD_DIM:(h + 1) * HEAD_DIM] = dq
                dqkv_ref[rs[i], W + h * HEAD_DIM:W + (h + 1) * HEAD_DIM] = dk
                dqkv_ref[rs[i], 2 * W + h * HEAD_DIM:2 * W + (h + 1) * HEAD_DIM] = dvb * beta[i]
                dsc = jnp.where(lane == h, dbeta, jnp.where(lane == HEADS + h, dgc, dsc))
                dgr_ref[ci, h:h + 1, :] = jnp.sum(gmat, axis=0, keepdims=True)
            dsc_ref[ci * CHUNK:(ci + 1) * CHUNK, :] = dsc

    row = lambda c: (nsteps - 1 - c, 0)
    lead3 = lambda c: (nsteps - 1 - c, 0, 0)
    return _pcall(
        body, name="gdn_bwd", grid=(nsteps,),
        in_specs=[pl.BlockSpec((rows_per_step, 3 * W), row), pl.BlockSpec((rows_per_step, LANE), row),
                  pl.BlockSpec((cps, HEADS, CHUNK), lead3),
                  pl.BlockSpec((rows_per_step, W), row), pl.BlockSpec((rows_per_step, W), row),
                  pl.BlockSpec((rows_per_step, W), row),
                  pl.BlockSpec((cps, HEADS, CHUNK, CHUNK), lambda c: (nsteps - 1 - c, 0, 0, 0)),
                  pl.BlockSpec((cps, W, HEAD_DIM), lead3), pl.BlockSpec((rows_per_step, W), row)],
        out_specs=[pl.BlockSpec((rows_per_step, 3 * W), row), pl.BlockSpec((rows_per_step, LANE), row),
                   pl.BlockSpec((cps, HEADS, CHUNK), lead3)],
        out_shape=[jax.ShapeDtypeStruct((L, 3 * W), f32), jax.ShapeDtypeStruct((L, LANE), f32),
                   jax.ShapeDtypeStruct((nc, HEADS, CHUNK), f32)],
        scratch_shapes=[pltpu.VMEM((HEADS, HEAD_DIM, HEAD_DIM), f32)],
        compiler_params=_cparams("arbitrary"),
    )(qkv, sc, gr, u_all, w_all, vn_all, t_all, sp_all, do_all)


def _qkv_bwd(proj, cw, dn, dproj):
    L = proj.shape[0]

    def body(x_ref, cw_ref, dn_ref, dproj_in, dx_ref, gcw_ref):
        j = pl.program_id(0)
        steps = GDN_WIDTH // ELT_W
        scale = jnp.where(j < steps, HEAD_DIM ** -0.5, 1.0).astype(f32)
        for ls in HALVES:
            x, dn_v = x_ref[:, ls], dn_ref[:, ls]
            c = _conv4(x, cw_ref, ls)
            sg = _sigmoid(c)
            a = c * sg
            rn = lax.rsqrt(jnp.sum(a * a, axis=1, keepdims=True) + EPS)
            da_n = (scale * rn) * (dn_v - a * ((rn * rn) * jnp.sum(dn_v * a, axis=1, keepdims=True)))
            da = jnp.where(j < 2 * steps, da_n, dn_v)
            dc = da * (sg * (1.0 + c * (1.0 - sg)))
            dc1, dc2, dc3 = _shift_up(dc, 1), _shift_up(dc, 2), _shift_up(dc, 3)
            gcw_ref[3:4, ls] = jnp.sum(dc * x, axis=0, keepdims=True)
            gcw_ref[2:3, ls] = jnp.sum(dc1 * x, axis=0, keepdims=True)
            gcw_ref[1:2, ls] = jnp.sum(dc2 * x, axis=0, keepdims=True)
            gcw_ref[0:1, ls] = jnp.sum(dc3 * x, axis=0, keepdims=True)
            dx = cw_ref[3:4, ls] * dc + cw_ref[2:3, ls] * dc1 + cw_ref[1:2, ls] * dc2 + cw_ref[0:1, ls] * dc3
            dx_ref[:, ls] = dx.astype(bf16)

    col = pl.BlockSpec((L, ELT_W), lambda j: (0, j))
    wspec = pl.BlockSpec((4, ELT_W), lambda j: (0, j))
    return _pcall(
        body, name="qkv_bwd", grid=(3 * GDN_WIDTH // ELT_W,),
        in_specs=[col, wspec, col, ANY], out_specs=[col, wspec],
        out_shape=[jax.ShapeDtypeStruct(dproj.shape, dproj.dtype), jax.ShapeDtypeStruct((4, 3 * GDN_WIDTH), f32)],
        input_output_aliases={3: 0},
        compiler_params=_cparams("parallel"),
    )(proj, cw, dn, dproj)


def _scalars_bwd(proj, alog_p, dtb_p, dsc, dgr_col, dproj, after):
    L = proj.shape[0]

    def body(x_ref, al_ref, dt_ref, dsc_ref, dgr_ref, dproj_in, after_ref, dba_ref, gs_ref):
        x, dsc_v = x_ref[...], dsc_ref[...]
        lane = _lanes(x.shape)
        dec = (lane >= HEADS) & (lane < 2 * HEADS)
        dg = jnp.where(dec, dsc_v - dgr_ref[...], 0.0)
        rc = _rows(x.shape) & (CHUNK - 1)
        for s in (1, 2, 4, 8, 16, 32):
            dg = dg + jnp.where(rc + s < CHUNK, pltpu.roll(dg, L - s, 0), 0.0)
        xa = x + dt_ref[...]
        ea = jnp.exp(al_ref[...])
        g = -ea * _softplus(xa)
        da = dg * (-ea) * _sigmoid(xa)
        beta = _sigmoid(x)
        db = dsc_v * beta * (1.0 - beta)
        dba_ref[:, :LANE] = jnp.where(lane < HEADS, db, jnp.where(dec, da, 0.0)).astype(bf16)
        dba_ref[:, LANE:] = jnp.zeros((L, ELT_W - LANE), bf16)
        g_al = jnp.sum(jnp.where(dec, dg * g, 0.0), axis=0, keepdims=True)
        g_dt = jnp.sum(jnp.where(dec, da, 0.0), axis=0, keepdims=True)
        row8 = _rows(gs_ref.shape)
        gs = jnp.where(row8 == 0, g_al, jnp.where(row8 == 1, g_dt, 0.0))
        gs_ref[...] = pltpu.roll(gs, LANE - HEADS, 1)

    full = pl.BlockSpec((L, LANE), lambda i: (0, 0))
    vec = pl.BlockSpec((1, LANE), lambda i: (0, 0))
    return _pcall(
        body, name="scalars_bwd", grid=(1,),
        in_specs=[pl.BlockSpec((L, LANE), lambda i: (0, OFF_BA // LANE)), vec, vec, full, full, ANY, ANY],
        out_specs=[pl.BlockSpec((L, ELT_W), lambda i: (0, OFF_BA // ELT_W)), pl.BlockSpec((8, LANE), lambda i: (0, 0))],
        out_shape=[jax.ShapeDtypeStruct(dproj.shape, dproj.dtype), jax.ShapeDtypeStruct((8, LANE), f32)],
        input_output_aliases={5: 0},
        compiler_params=_cparams("arbitrary"),
    )(proj, alog_p, dtb_p, dsc, dgr_col, dproj, after)


def _input_grad(dproj, wpad, x, nw, dy, after):
    L = x.shape[0]
    tm = min(512, L)
    cuts = (0, 1024, 3072, 5120, 7168, PROJ_PAD)
    nk = len(cuts) - 1

    def body(dp_ref, w_hbm, x_ref, nw_ref, dy_ref, after_ref, gx_ref, gnw_ref, w_vmem, sems):
        first = pl.program_id(0) == 0
        loads = [pltpu.make_async_copy(w_hbm.at[cuts[k]:cuts[k + 1], :], w_vmem.at[cuts[k]:cuts[k + 1], :], sems.at[k])
                 for k in range(nk)]

        @pl.when(first)
        def _():
            for cp in loads:
                cp.start()
            gnw_ref[...] = jnp.zeros_like(gnw_ref)
        dh = None
        for k in range(nk):
            pl.when(first)(loads[k].wait)
            part = jnp.dot(dp_ref[:, cuts[k]:cuts[k + 1]], w_vmem[cuts[k]:cuts[k + 1], :], preferred_element_type=f32)
            dh = part if dh is None else dh + part
        xv, nwv = x_ref[...], nw_ref[...]
        r = lax.rsqrt(jnp.mean(xv * xv, axis=-1, keepdims=True) + EPS)
        xh = xv * r
        gnw_ref[...] += jnp.sum(dh * xh, axis=0, keepdims=True)
        dxh = dh * nwv
        gx_ref[...] = dy_ref[...] + r * (dxh - xh * jnp.mean(dxh * xh, axis=-1, keepdims=True))

    row = lambda i: (i, 0)
    fix = lambda i: (0, 0)
    return _pcall(
        body, name="input_grad", grid=(L // tm,),
        in_specs=[pl.BlockSpec((tm, PROJ_PAD), row), ANY, pl.BlockSpec((tm, D_MODEL), row),
                  pl.BlockSpec((1, D_MODEL), fix), pl.BlockSpec((tm, D_MODEL), row), ANY],
        out_specs=[pl.BlockSpec((tm, D_MODEL), row), pl.BlockSpec((1, D_MODEL), fix)],
        out_shape=[jax.ShapeDtypeStruct((L, D_MODEL), f32), jax.ShapeDtypeStruct((1, D_MODEL), f32)],
        scratch_shapes=[pltpu.VMEM(wpad.shape, bf16), pltpu.SemaphoreType.DMA((nk,))],
        compiler_params=_cparams("arbitrary"),
    )(dproj, wpad, x, nw, dy, after)


def _adamw_reduce(parts, w, m, v, name):
    R, C = w.shape
    n_parts = parts.shape[0]
    tr = 128 if R % 128 == 0 else R
    c1 = 1.0 - ADAM_B1 ** ADAM_STEP
    c2 = 1.0 - ADAM_B2 ** ADAM_STEP

    def body(p_ref, w_ref, m_ref, v_ref, g_ref, d_ref, nm_ref, nv_ref):
        g = p_ref[0].astype(f32)
        for s in range(1, n_parts):
            g = g + p_ref[s].astype(f32)
        nm = ADAM_B1 * m_ref[...] + (1.0 - ADAM_B1) * g
        nv = ADAM_B2 * v_ref[...] + (1.0 - ADAM_B2) * (g * g)
        g_ref[...] = g
        nm_ref[...] = nm
        nv_ref[...] = nv
        d_ref[...] = -ADAM_LR * ((nm / c1) / (jnp.sqrt(nv / c2) + ADAM_EPS) + ADAM_WD * w_ref[...])

    blk = pl.BlockSpec((tr, C), lambda i: (i, 0))
    out = jax.ShapeDtypeStruct((R, C), f32)
    return _pcall(
        body, name=name, grid=(R // tr,),
        in_specs=[pl.BlockSpec((n_parts, tr, C), lambda i: (0, i, 0)), blk, blk, blk],
        out_specs=[blk] * 4, out_shape=[out] * 4,
        compiler_params=_cparams("parallel"),
    )(parts, w, m, v)


SMALL_SLOTS = ((0, D_MODEL), (D_MODEL, D_MODEL), (2 * D_MODEL, D_MODEL), (3 * D_MODEL, LANE),
               (3 * D_MODEL + LANE, HEADS), (3 * D_MODEL + 2 * LANE, HEADS))
SMALL_LOSS = 3 * D_MODEL + 3 * LANE
SMALL_W = SMALL_LOSS + LANE


def _pack_small(gs, after):
    def body(nw_ref, cb_ref, fw_ref, gn_ref, sc_ref, ls_ref, after_ref, o_ref):
        for ref, (start, width) in zip((nw_ref, cb_ref, fw_ref, gn_ref), SMALL_SLOTS[:4]):
            o_ref[:, start:start + width] = ref[...]
        o_ref[:, SMALL_SLOTS[4][0]:SMALL_SLOTS[4][0] + LANE] = sc_ref[0:1, :]
        o_ref[:, SMALL_SLOTS[5][0]:SMALL_SLOTS[5][0] + LANE] = sc_ref[1:2, :]
        o_ref[:, SMALL_LOSS:SMALL_W] = ls_ref[...]

    vm = pl.BlockSpec(memory_space=pltpu.VMEM)
    return _pcall(body, name="pack_small_grads", out_shape=jax.ShapeDtypeStruct((1, SMALL_W), f32),
                  in_specs=[vm] * 6 + [ANY], out_specs=vm)(*gs, after)


def _adamw_small(parts, ws, ms, vs):
    c1 = 1.0 - ADAM_B1 ** ADAM_STEP
    c2 = 1.0 - ADAM_B2 ** ADAM_STEP
    np_ = len(ws)

    def body(*refs):
        p_ref = refs[0]
        w_refs, m_refs, v_refs = refs[1:1 + np_], refs[1 + np_:1 + 2 * np_], refs[1 + 2 * np_:1 + 3 * np_]
        outs = refs[1 + 3 * np_:]
        g_refs, d_refs, nm_refs, nv_refs = (outs[i * np_:(i + 1) * np_] for i in range(4))
        loss_ref = outs[4 * np_]

        def total(start, width):
            t = p_ref[0, :, start:start + width]
            for s in range(1, N_DEV):
                t = t + p_ref[s, :, start:start + width]
            return t

        for i, (start, width) in enumerate(SMALL_SLOTS):
            g = total(start, width)
            nm = ADAM_B1 * m_refs[i][...] + (1.0 - ADAM_B1) * g
            nv = ADAM_B2 * v_refs[i][...] + (1.0 - ADAM_B2) * (g * g)
            g_refs[i][...] = g
            nm_refs[i][...] = nm
            nv_refs[i][...] = nv
            d_refs[i][...] = -ADAM_LR * ((nm / c1) / (jnp.sqrt(nv / c2) + ADAM_EPS) + ADAM_WD * w_refs[i][...])
        loss_ref[...] = total(SMALL_LOSS, LANE)

    vm = pl.BlockSpec(memory_space=pltpu.VMEM)
    shapes = [jax.ShapeDtypeStruct(w.shape, f32) for w in ws]
    res = _pcall(body, name="adamw_small", out_shape=shapes * 4 + [jax.ShapeDtypeStruct((1, LANE), f32)],
                 in_specs=[vm] * (1 + 3 * np_), out_specs=[vm] * (4 * np_ + 1))(parts, *ws, *ms, *vs)
    return [res[i * np_:(i + 1) * np_] for i in range(4)], res[4 * np_]


def _adamw_w_in(part_a, part_b, w3, m3, v3, after):
    _, n, _ = part_a.shape
    c1 = 1.0 - ADAM_B1 ** ADAM_STEP
    c2 = 1.0 - ADAM_B2 ** ADAM_STEP

    def body(pa_ref, pb_ref, w_ref, m_ref, v_ref, after_ref, g_ref, d_ref, nm_ref, nv_ref):
        g = pa_ref[0].astype(f32) + pb_ref[0].astype(f32)
        nm = ADAM_B1 * m_ref[:, 0, :] + (1.0 - ADAM_B1) * g
        nv = ADAM_B2 * v_ref[:, 0, :] + (1.0 - ADAM_B2) * (g * g)
        g_ref[:, 0, :] = g
        nm_ref[:, 0, :] = nm
        nv_ref[:, 0, :] = nv
        d_ref[:, 0, :] = -ADAM_LR * ((nm / c1) / (jnp.sqrt(nv / c2) + ADAM_EPS) + ADAM_WD * w_ref[:, 0, :])

    tile = 2 * COL_TILE
    blk = pl.BlockSpec((n, 1, tile), lambda j: (0, 0, j))
    out = jax.ShapeDtypeStruct((n, 1, D_MODEL), f32)
    return _pcall(
        body, name="adamw_w_in", grid=(D_MODEL // tile,),
        in_specs=[pl.BlockSpec((1, n, tile), lambda j: (0, 0, j))] * 2 + [blk, blk, blk, ANY],
        out_specs=[blk] * 4, out_shape=[out] * 4,
        compiler_params=_cparams("parallel"),
    )(part_a, part_b, w3, m3, v3, after)


def _pad_lanes(vec8, start):
    return jnp.pad(vec8.reshape(1, -1), ((0, 0), (start, LANE - start - vec8.size)))


def kernel(x, norm_in_w, w_in, conv_qkv_w, A_log, dt_bias, gdn_norm_w, conv_w, conv_b, w_out, final_norm_w, loss_target, m_norm_in_w, m_w_in, m_conv_qkv_w, m_A_log, m_dt_bias, m_gdn_norm_w, m_conv_w, m_conv_b, m_w_out, m_final_norm_w, v_norm_in_w, v_w_in, v_conv_qkv_w, v_A_log, v_dt_bias, v_gdn_norm_w, v_conv_w, v_conv_b, v_w_out, v_final_norm_w):
    L = x.shape[1]
    nc = L // CHUNK
    xs = x[0]
    tgt = loss_target[0]
    fnw = final_norm_w.reshape(1, D_MODEL)

    as_rows = lambda a: jnp.transpose(a, (2, 0, 1))
    win_g, cqkv_g, cw_g = _all_gather([_cast_w_in(as_rows(w_in)), conv_qkv_w[0], conv_w[0]], "gather_weights",
                                      pieces=[4, 1, 1])
    wpad = _relayout_w_in(win_g)
    cqkv = jnp.concatenate([cqkv_g[d] for d in range(N_DEV)], axis=1)
    cw = jnp.concatenate([cw_g[d] for d in range(N_DEV)], axis=1)
    alog_p = _pad_lanes(A_log, HEADS)
    dtb_p = _pad_lanes(dt_bias, HEADS)
    tok = lambda started: started[4]
    wo_started = _spread_start(w_out[0].astype(bf16), wpad, "gather", "gather_w_out_start")

    proj, h = _in_proj(xs, norm_in_w, wpad, tok(wo_started))
    qkv = _qkv_act(proj, cqkv)
    sc, gr = _scalars(proj, alog_p, dtb_p)
    o, u_all, w_all, vn_all, t_all, sp_all = _gdn_fwd(qkv, sc, gr)
    mix = _conv_fwd(proj, cw, conv_b, _gdn_gate(o, proj, gdn_norm_w))
    wo = _spread_wait(wo_started, mix, "gather", "gather_w_out_wait")[1].reshape(-1, D_MODEL)
    dy, dyb, dmix, g_fnw, loss_v = _out_proj_loss(xs, mix, wo, fnw, tgt)

    g_wout = _tn_matmul(mix, dyb, "grad_w_out")
    gwo_started = _spread_start(g_wout.reshape(N_DEV, -1, D_MODEL), dyb, "scatter", "exchange_grad_w_out_start")
    do, dproj, g_gnw = _gdn_gate_bwd(o, proj, gdn_norm_w, dmix, tok(gwo_started))
    dproj, g_cw, g_cb = _conv_bwd(proj, cw, conv_b, dmix, dproj)
    dqkv_n, dsc, dgr = _gdn_bwd(qkv, sc, gr, u_all, w_all, vn_all, t_all, sp_all, do)
    dproj, g_cqkv = _qkv_bwd(proj, cqkv, dqkv_n, dproj)
    g_cqkv_blk = g_cqkv.reshape(4, N_DEV, -1).transpose(1, 0, 2)
    g_cw_blk = jnp.pad(g_cw.reshape(3, N_DEV, -1).transpose(1, 0, 2),
                       ((0, 0), (0, 1), (0, g_cqkv_blk.shape[2] - g_cw.shape[1] // N_DEV)))
    gsm_started = _spread_start(jnp.concatenate([g_cqkv_blk, g_cw_blk], axis=1), g_cqkv, "scatter",
                                "exchange_small_sharded_grads_start")
    dgr_col = jnp.pad(dgr.transpose(0, 2, 1).reshape(L, HEADS), ((0, 0), (HEADS, LANE - 2 * HEADS)))
    dproj, g_sc = _scalars_bwd(proj, alog_p, dtb_p, dsc, dgr_col, dproj, tok(gsm_started))
    g_win_blk = _grad_blocks(_tn_matmul(dproj, h, "grad_w_in"))

    (p_win,) = _pair_exchange([g_win_blk], "exchange_grads_pair")
    r_small = _spread_wait(gsm_started, p_win, "scatter", "exchange_small_sharded_grads_wait")[1]
    r_cqkv, r_cw = r_small[:, :4, :], r_small[:, 4:7, :g_cw.shape[1] // N_DEV]
    s_win = _pair_sum(g_win_blk, p_win, "pair_sum_w_in")
    gw1_started = _spread_start(s_win, r_small, "axis_a", "exchange_grads_axis1_start")
    grad_x, g_nw = _input_grad(dproj, wpad, xs, norm_in_w, dy, tok(gw1_started))
    s_thru, got1 = _spread_wait(gw1_started, grad_x, "axis_a", "exchange_grads_axis1_wait")
    t_win = _axis_sum(s_thru, got1, "axis_sum_w_in")
    gw2_started = _spread_start(t_win, got1, "axis_b", "exchange_grads_axis2_start")

    r_wout = _spread_wait(gwo_started, tok(gw2_started), "scatter", "exchange_grad_w_out_wait")[1]
    upd_wout =_adamw_reduce(r_wout, w_out[0], m_w_out[0], v_w_out[0], "adamw_w_out")
    upd_cqkv = _adamw_reduce(r_cqkv, conv_qkv_w[0], m_conv_qkv_w[0], v_conv_qkv_w[0], "adamw_conv_qkv_w")
    upd_cw = _adamw_reduce(r_cw, conv_w[0], m_conv_w[0], v_conv_w[0], "adamw_conv_w")

    t_thru, got2 = _spread_wait(gw2_started, upd_cw[0], "axis_b", "exchange_grads_axis2_wait")

    small_g = _pack_small([g_nw, g_cb, g_fnw, g_gnw, g_sc, loss_v], got2)
    gsg_started = _spread_start(small_g, got2, "gather", "gather_small_grads_start")
    upd_win_t = _adamw_w_in(t_thru, got2, as_rows(w_in), as_rows(m_w_in), as_rows(v_w_in), tok(gsg_started))
    upd_win = [jnp.transpose(a, (1, 2, 0)) for a in upd_win_t]
    small_all = _spread_wait(gsg_started, upd_win_t[0], "gather", "gather_small_grads_wait")[1]
    fvec = lambda a: a.reshape(1, D_MODEL)
    upd_small, loss_sum = _adamw_small(
        small_all,
        [norm_in_w, conv_b, fvec(final_norm_w), gdn_norm_w, A_log, dt_bias],
        [m_norm_in_w, m_conv_b, fvec(m_final_norm_w), m_gdn_norm_w, m_A_log, m_dt_bias],
        [v_norm_in_w, v_conv_b, fvec(v_final_norm_w), v_gdn_norm_w, v_A_log, v_dt_bias])

    outs = [loss_sum[0, 0], grad_x[None]]
    for k in range(4):
        nw_k, cb_k, fw_k, gn_k, al_k, dt_k = upd_small[k]
        outs += [nw_k, upd_win[k], upd_cqkv[k][None], al_k, dt_k, gn_k,
                 upd_cw[k][None], cb_k, upd_wout[k][None], fw_k.reshape(D_MODEL)]
    return tuple(outs)
```

```python
import jax
import jax.numpy as jnp
from jax import lax
from jax.experimental import pallas as pl
from jax.experimental.pallas import tpu as pltpu

f32 = jnp.float32
bf16 = jnp.bfloat16

N_DEV = 8
D_MODEL = 1024
HEADS = 8
HEAD_DIM = 128
CHUNK = 64
GDN_CPS = 4
GDN_CPS_BWD = 1
GDN_WIDTH = HEADS * HEAD_DIM
CONV_WIDTH = 1024
PROJ_WIDTH = 8208
SHARD_W = PROJ_WIDTH // N_DEV
EPS = 1e-6

LANE = 128
ELT_W = 256

OFF_QKV, OFF_ZG, OFF_CONV, OFF_BA = 0, 3072, 4096, 8192
CONV_BLOCK = 4 * ELT_W
PROJ_PAD = 8448
NAT_BA, NAT_CONV = 4096, 4112


def _padded_col(n):
    if n < NAT_BA:
        return n
    if n < NAT_CONV:
        return OFF_BA + n - NAT_BA
    g, ch = divmod(n - NAT_CONV, CONV_WIDTH)
    j, r = divmod(ch, ELT_W)
    return OFF_CONV + CONV_BLOCK * j + ELT_W * g + r


def _layout_segments(n0, n1):
    cuts = [NAT_BA, NAT_CONV] + [NAT_CONV + ELT_W * k for k in range(1, 4 * CONV_WIDTH // ELT_W)]
    pts = [n0] + [c for c in cuts if n0 < c < n1] + [n1]
    return [(lo, hi - lo, _padded_col(lo)) for lo, hi in zip(pts, pts[1:])]

ADAM_LR, ADAM_B1, ADAM_B2, ADAM_EPS, ADAM_WD, ADAM_STEP = 0.001, 0.9, 0.999, 1e-08, 0.01, 10

V7X_VMEM_BYTES = 64 * 1024 * 1024
VMEM_LIMIT = V7X_VMEM_BYTES - 8 * 1024 * 1024

MESH = pl.DeviceIdType.MESH
ANY = pl.BlockSpec(memory_space=pl.ANY)


def _pcall(body, **kw):
    return pl.pallas_call(body, **kw)


def _cparams(*sem):
    return pltpu.CompilerParams(dimension_semantics=sem if sem else None, vmem_limit_bytes=VMEM_LIMIT)


def _mm(a, b):
    return jnp.dot(a.astype(bf16), b.astype(bf16), preferred_element_type=f32)


def _mm_nt(a, b):
    return lax.dot_general(a.astype(bf16), b.astype(bf16), (((1,), (1,)), ((), ())), preferred_element_type=f32)


def _cat16(parts, axis):
    return jnp.concatenate([p.astype(bf16) for p in parts], axis=axis)


def _mm_tn(a, b):
    return lax.dot_general(a.astype(bf16), b.astype(bf16), (((0,), (0,)), ((), ())), preferred_element_type=f32)


def _rows(shape):
    return lax.broadcasted_iota(jnp.int32, shape, 0)


def _lanes(shape):
    return lax.broadcasted_iota(jnp.int32, shape, 1)


def _shift_down(x, s):
    if s == 0:
        return x
    return jnp.where(_rows(x.shape) >= s, pltpu.roll(x, s, 0), 0.0)


def _shift_up(x, s):
    if s == 0:
        return x
    n = x.shape[0]
    return jnp.where(_rows(x.shape) < n - s, pltpu.roll(x, n - s, 0), 0.0)


def _sigmoid(x):
    return jax.nn.sigmoid(x)


def _softplus(x):
    e = jnp.exp(-jnp.abs(x))
    small = e * (1.0 - e * (0.5 - e * (1.0 / 3.0)))
    return jnp.maximum(x, 0.0) + jnp.where(e < 0.01, small, jnp.log(1.0 + e))


def _mesh_pos():
    return lax.axis_index("x"), lax.axis_index("y"), lax.axis_index("c")


def _flat(px, py, pc):
    return 4 * px + 2 * py + pc


def _all_gather(xs, name, pieces=None):
    n = len(xs)
    pieces = pieces or [1] * n
    items = [(a, q) for a in range(n) for q in range(pieces[a])]
    ni = len(items)

    def view(ref, i):
        a, q = items[i]
        if pieces[a] == 1:
            return ref
        wd = xs[a].shape[-1] // pieces[a]
        return ref.at[(slice(None),) * (xs[a].ndim - 1) + (pl.ds(q * wd, wd),)]

    def body(*refs):
        x_refs, o_refs = refs[:n], refs[n:2 * n]
        send_sems, recv_sems, local_sems = refs[2 * n:]
        x, y, c = _mesh_pos()
        me, sibling = (x, y, c), (x, y, 1 - c)
        flip = lambda v, bit: v + bit - 2 * v * bit
        nbr_a = (flip(x, 1 - c), flip(y, c))
        nbr_b = (flip(x, c), flip(y, 1 - c))
        diag = (1 - x, 1 - y)

        def copy(i, k, block, to, own=False):
            a = items[i][0]
            dst = view(o_refs[a].at[_flat(*block)], i)
            return pltpu.make_async_remote_copy(
                src_ref=view(x_refs[a], i) if own else dst, dst_ref=dst,
                send_sem=send_sems.at[i, k], recv_sem=recv_sems.at[i, k], device_id=to, device_id_type=MESH)

        mine, sent = [], []

        def go(cp):
            cp.start()
            sent.append(cp)

        for a in range(n):
            cp = pltpu.make_async_copy(x_refs[a], o_refs[a].at[_flat(*me)], local_sems.at[a])
            cp.start()
            mine.append(cp)
        for a in range(ni):
            go(copy(a, 1, me, (*nbr_a, c), own=True))
            go(copy(a, 2, me, (*nbr_b, c), own=True))
            go(copy(a, 0, me, sibling, own=True))
        for a in range(ni):
            copy(a, 1, (*nbr_a, c), me).wait_recv()
            go(copy(a, 3, (*nbr_a, c), (*nbr_b, c)))
            go(copy(a, 4, (*nbr_a, c), sibling))
        for a in range(ni):
            copy(a, 2, (*nbr_b, c), me).wait_recv()
            go(copy(a, 5, (*nbr_b, c), sibling))
        for a in range(ni):
            copy(a, 3, (*diag, c), me).wait_recv()
            go(copy(a, 6, (*diag, c), sibling))
        for a in range(ni):
            copy(a, 0, sibling, me).wait_recv()
            copy(a, 4, (*nbr_b, 1 - c), me).wait_recv()
            copy(a, 5, (*nbr_a, 1 - c), me).wait_recv()
            copy(a, 6, (*diag, 1 - c), me).wait_recv()
        for cp in sent:
            cp.wait_send()
        for cp in mine:
            cp.wait()

    outs = _pcall(
        body, name=name,
        out_shape=[jax.ShapeDtypeStruct((N_DEV,) + a.shape, a.dtype) for a in xs],
        in_specs=[ANY] * n, out_specs=[ANY] * n,
        scratch_shapes=[pltpu.SemaphoreType.DMA((ni, 7)), pltpu.SemaphoreType.DMA((ni, 7)), pltpu.SemaphoreType.DMA((n,))],
    )(*xs)
    return list(outs)


def _pair_exchange(gs, name):
    n = len(gs)
    chips = [(0, 0), (0, 1), (1, 0), (1, 1)]

    def body(*refs):
        g_refs, o_refs = refs[:n], refs[n:2 * n]
        send_sems, recv_sems = refs[2 * n:]
        x, y, c = _mesh_pos()
        sibling = (x, y, 1 - c)

        def copy(a, i):
            xp, yp = chips[i]
            return pltpu.make_async_remote_copy(
                src_ref=g_refs[a].at[_flat(xp, yp, 1 - c)], dst_ref=o_refs[a].at[i],
                send_sem=send_sems.at[a, i], recv_sem=recv_sems.at[a, i], device_id=sibling, device_id_type=MESH)

        cps = [copy(a, i) for a in range(n) for i in range(4)]
        for cp in cps:
            cp.start()
        for cp in cps:
            cp.wait()

    outs = _pcall(
        body, name=name,
        out_shape=[jax.ShapeDtypeStruct((4,) + a.shape[1:], a.dtype) for a in gs],
        in_specs=[ANY] * n, out_specs=[ANY] * n,
        scratch_shapes=[pltpu.SemaphoreType.DMA((n, 4)), pltpu.SemaphoreType.DMA((n, 4))],
    )(*gs)
    return list(outs)


def _pair_sum(g, p1, name):
    _, R, C = g.shape
    tr = 256 if R % 256 == 0 else R
    cidx = lax.axis_index("c").astype(jnp.int32).reshape(1)

    def body(c_ref, g_ref, p_ref, o_ref):
        o_ref[...] = (g_ref[...].astype(f32) + p_ref[...].astype(f32)).astype(o_ref.dtype)

    return _pcall(
        body, name=name,
        grid_spec=pltpu.PrefetchScalarGridSpec(
            num_scalar_prefetch=1, grid=(4, R // tr),
            in_specs=[pl.BlockSpec((1, tr, C), lambda i, r, c_ref: (2 * i + c_ref[0], r, 0)),
                      pl.BlockSpec((1, tr, C), lambda i, r, c_ref: (i, r, 0))],
            out_specs=pl.BlockSpec((1, tr, C), lambda i, r, c_ref: (i, r, 0))),
        out_shape=jax.ShapeDtypeStruct((4, R, C), g.dtype),
        compiler_params=_cparams("parallel", "parallel"),
    )(cidx, g, p1)


def _axis_sum(s, got, name):
    _, R, C = s.shape
    x, y, c = _mesh_pos()
    me, _, b, _ = _axis_chips(x, y, c)
    idx = jnp.stack([2 * me[0] + me[1], 2 * b[0] + b[1]]).astype(jnp.int32)

    def body(idx_ref, s_ref, g_ref, o_ref):
        o_ref[...] = (s_ref[...].astype(f32) + g_ref[...].astype(f32)).astype(o_ref.dtype)

    return _pcall(
        body, name=name,
        grid_spec=pltpu.PrefetchScalarGridSpec(
            num_scalar_prefetch=1, grid=(2,),
            in_specs=[pl.BlockSpec((1, R, C), lambda k, idx_ref: (idx_ref[k], 0, 0)),
                      pl.BlockSpec((1, R, C), lambda k, idx_ref: (k, 0, 0))],
            out_specs=pl.BlockSpec((1, R, C), lambda k, idx_ref: (k, 0, 0))),
        out_shape=jax.ShapeDtypeStruct((2, R, C), s.dtype),
        compiler_params=_cparams("parallel"),
    )(idx, s, got)


HBM = pl.BlockSpec(memory_space=pltpu.HBM)
SEM = pl.BlockSpec(memory_space=pltpu.SEMAPHORE)
EFFECT = pltpu.SideEffectType.DATAFLOW_SIDE_EFFECTING


def _peers(x, y, c):
    out = []
    for k in range(1, N_DEV):
        kx, ky, kc = (k >> 2) & 1, (k >> 1) & 1, k & 1
        out.append(((1 - x) if kx else x, (1 - y) if ky else y, (1 - c) if kc else c))
    return out


SPREAD_COPIES = {"gather": N_DEV - 1, "scatter": N_DEV - 1, "axis_a": 2, "axis_b": 1}
SPREAD_SLOTS = {"axis_a": 2, "axis_b": 1}


def _axis_chips(x, y, c):
    flip = lambda v, bit: v + bit - 2 * v * bit
    return (x, y), (flip(x, 1 - c), flip(y, c)), (flip(x, c), flip(y, 1 - c)), (1 - x, 1 - y)


def _spread_copy(src_ref, land_ref, send_sems, recv_sems, k, plan):
    x, y, c = _mesh_pos()
    if plan in ("axis_a", "axis_b"):
        _, a, b, d = _axis_chips(x, y, c)
        chip = lambda p: 2 * p[0] + p[1]
        peer = (*(a if plan == "axis_a" else b), c)
        src = src_ref.at[chip(a) if k == 0 else chip(d)] if plan == "axis_a" else src_ref.at[1]
        slot = k
    else:
        peer = _peers(x, y, c)[k]
        src, slot = (src_ref.at[_flat(*peer)] if plan == "scatter" else src_ref), _flat(x, y, c)
    return pltpu.make_async_remote_copy(
        src_ref=src, dst_ref=land_ref.at[slot], send_sem=send_sems.at[k], recv_sem=recv_sems.at[k],
        device_id=peer, device_id_type=MESH)


def _own_copy(src_ref, land_ref, send_sems, plan):
    me = _flat(*_mesh_pos())
    return pltpu.make_async_copy(src_ref.at[me] if plan == "scatter" else src_ref, land_ref.at[me],
                                 send_sems.at[SPREAD_COPIES[plan]])


def _spread_start(src, after, plan, name):
    land_shape = (N_DEV,) + src.shape if plan == "gather" else src.shape
    if plan in SPREAD_SLOTS:
        land_shape = (SPREAD_SLOTS[plan],) + src.shape[1:]
    n_copies = SPREAD_COPIES[plan]

    def body(src_ref, land_ref, after_ref, send_sems, recv_sems, src_thru, land_thru, token):
        for k in range(n_copies):
            _spread_copy(src_ref, land_ref, send_sems, recv_sems, k, plan).start()
        if plan not in SPREAD_SLOTS:
            _own_copy(src_ref, land_ref, send_sems, plan).start()
        token[...] = jnp.zeros_like(token)

    return _pcall(
        body, name=name,
        out_shape=(pltpu.SemaphoreType.DMA((n_copies + (plan not in SPREAD_SLOTS),)), pltpu.SemaphoreType.DMA((n_copies,)),
                   pltpu.HBM(src.shape, src.dtype), pltpu.HBM(land_shape, src.dtype), jax.ShapeDtypeStruct((8, LANE), f32)),
        in_specs=(HBM, HBM, ANY), out_specs=(SEM, SEM, HBM, HBM, pl.BlockSpec(memory_space=pltpu.VMEM)),
        input_output_aliases={0: 2, 1: 3},
        compiler_params=pltpu.CompilerParams(has_side_effects=EFFECT),
    )(pltpu.with_memory_space_constraint(src, pltpu.HBM),
      pltpu.with_memory_space_constraint(lax.empty(land_shape, src.dtype), pltpu.HBM), after)


def _spread_wait(started, after, plan, name):
    send_sems, recv_sems, src_thru, land_thru, _ = started

    def body(src_ref, land_ref, send_sems, recv_sems, after_ref, src_dead, got_ref):
        for k in range(SPREAD_COPIES[plan]):
            cp = _spread_copy(src_ref, land_ref, send_sems, recv_sems, k, plan)
            cp.wait_send()
            cp.wait_recv()
        if plan not in SPREAD_SLOTS:
            _own_copy(src_ref, land_ref, send_sems, plan).wait()

    return _pcall(
        body, name=name,
        out_shape=(pltpu.HBM(src_thru.shape, src_thru.dtype), pltpu.HBM(land_thru.shape, land_thru.dtype)),
        in_specs=(HBM, HBM, SEM, SEM, ANY), out_specs=(HBM, HBM), input_output_aliases={0: 0, 1: 1},
        compiler_params=pltpu.CompilerParams(has_side_effects=EFFECT),
    )(src_thru, land_thru, send_sems, recv_sems, after)


COL_TILE = 256


def _cast_w_in(w3):
    n = w3.shape[0]

    def body(w_ref, o_ref):
        o_ref[...] = w_ref[:, 0, :].astype(bf16)

    tile = 2 * COL_TILE
    return _pcall(
        body, name="cast_w_in", grid=(D_MODEL // tile,),
        in_specs=[pl.BlockSpec((n, 1, tile), lambda j: (0, 0, j))],
        out_specs=pl.BlockSpec((n, tile), lambda j: (0, j)),
        out_shape=jax.ShapeDtypeStruct((n, D_MODEL), bf16),
        compiler_params=_cparams("parallel"),
    )(w3)


def _relayout_w_in(win_g):
    def body(g_ref, o_ref):
        used = OFF_BA + NAT_CONV - NAT_BA
        o_ref[used:PROJ_PAD, :] = jnp.zeros((PROJ_PAD - used, COL_TILE), o_ref.dtype)
        for d in range(N_DEV):
            for lo, width, dst in _layout_segments(d * SHARD_W, (d + 1) * SHARD_W):
                src = lo - d * SHARD_W
                o_ref[dst:dst + width, :] = g_ref[d, src:src + width, :]

    return _pcall(
        body, name="relayout_w_in", grid=(D_MODEL // COL_TILE,),
        in_specs=[pl.BlockSpec((N_DEV, SHARD_W, COL_TILE), lambda j: (0, 0, j))],
        out_specs=pl.BlockSpec((PROJ_PAD, COL_TILE), lambda j: (0, j)),
        out_shape=jax.ShapeDtypeStruct((PROJ_PAD, D_MODEL), win_g.dtype),
        compiler_params=_cparams("parallel"),
    )(win_g)


def _grad_blocks(g_t):
    def body(p_ref, o_ref):
        for d in range(N_DEV):
            for lo, width, src in _layout_segments(d * SHARD_W, (d + 1) * SHARD_W):
                dst = lo - d * SHARD_W
                o_ref[d, dst:dst + width, :] = p_ref[src:src + width, :]

    return _pcall(
        body, name="grad_blocks", grid=(D_MODEL // COL_TILE,),
        in_specs=[pl.BlockSpec((PROJ_PAD, COL_TILE), lambda j: (0, j))],
        out_specs=pl.BlockSpec((N_DEV, SHARD_W, COL_TILE), lambda j: (0, 0, j)),
        out_shape=jax.ShapeDtypeStruct((N_DEV, SHARD_W, D_MODEL), bf16),
        compiler_params=_cparams("parallel"),
    )(g_t)


def _in_proj(x, nw, wpad_t, after):
    L = x.shape[0]
    tn = 768
    nj = wpad_t.shape[0] // tn

    rows = min(256, L)

    def body(x_hbm, nw_ref, w_ref, after_ref, proj_ref, h_ref, x_vmem, sems):
        first = pl.program_id(0) == 0

        def project(r, n, hv):
            proj_ref[r:r + n, :] = lax.dot_general(hv, w_ref[...], (((1,), (1,)), ((), ())), preferred_element_type=f32)

        @pl.when(first)
        def _():
            loads = [pltpu.make_async_copy(x_hbm.at[r:r + rows, :], x_vmem.at[r:r + rows, :], sems.at[r // rows])
                     for r in range(0, L, rows)]
            for cp in loads:
                cp.start()
            for r in range(0, L, rows):
                loads[r // rows].wait()
                xs = x_vmem[r:r + rows, :]
                ms = jnp.mean(xs * xs, axis=-1, keepdims=True)
                hv = ((xs * lax.rsqrt(ms + EPS)) * nw_ref[...]).astype(bf16)
                h_ref[r:r + rows, :] = hv
                project(r, rows, hv)

        @pl.when(jnp.logical_not(first))
        def _():
            for r in range(0, L, min(512, L)):
                project(r, min(512, L), h_ref[r:r + min(512, L), :])

    return _pcall(
        body, name="in_proj", grid=(nj,),
        in_specs=[ANY, pl.BlockSpec((1, D_MODEL), lambda j: (0, 0)), pl.BlockSpec((tn, D_MODEL), lambda j: (j, 0)), ANY],
        out_specs=[pl.BlockSpec((L, tn), lambda j: (0, j)), pl.BlockSpec((L, D_MODEL), lambda j: (0, 0))],
        out_shape=[jax.ShapeDtypeStruct((L, wpad_t.shape[0]), f32), jax.ShapeDtypeStruct((L, D_MODEL), bf16)],
        scratch_shapes=[pltpu.VMEM((L, D_MODEL), f32), pltpu.SemaphoreType.DMA((L // rows,))],
        compiler_params=_cparams("arbitrary"),
    )(x, nw, wpad_t, after)


HALVES = [slice(i * LANE, (i + 1) * LANE) for i in range(ELT_W // LANE)]
QKV_W = 512
QKV_HEADS = [slice(i * LANE, (i + 1) * LANE) for i in range(QKV_W // LANE)]
STEPS_PER_GROUP = GDN_WIDTH // QKV_W


def _conv4(x, cw_ref, ls):
    return (cw_ref[3:4, ls] * x + cw_ref[2:3, ls] * _shift_down(x, 1) + cw_ref[1:2, ls] * _shift_down(x, 2)
            + cw_ref[0:1, ls] * _shift_down(x, 3))


def _qkv_act(proj, cw):
    L = proj.shape[0]

    def body(x_ref, cw_ref, o_ref):
        j = pl.program_id(0)
        scale = jnp.where(j < STEPS_PER_GROUP, HEAD_DIM ** -0.5, 1.0).astype(f32)
        for ls in QKV_HEADS:
            c = _conv4(x_ref[:, ls], cw_ref, ls)
            a = c * _sigmoid(c)
            rn = lax.rsqrt(jnp.sum(a * a, axis=1, keepdims=True) + EPS)
            o_ref[:, ls] = jnp.where(j < 2 * STEPS_PER_GROUP, (a * rn) * scale, a)

    return _pcall(
        body, name="qkv_act", grid=(3 * STEPS_PER_GROUP,),
        in_specs=[pl.BlockSpec((L, QKV_W), lambda j: (0, j)), pl.BlockSpec((4, QKV_W), lambda j: (0, j))],
        out_specs=pl.BlockSpec((L, QKV_W), lambda j: (0, j)),
        out_shape=jax.ShapeDtypeStruct((L, 3 * GDN_WIDTH), f32),
        compiler_params=_cparams("parallel"),
    )(proj, cw)


def _scalars(proj, alog_p, dtb_p):
    L = proj.shape[0]
    nc = L // CHUNK

    def body(x_ref, al_ref, dt_ref, sc_ref, gr_ref):
        x = x_ref[...]
        lane = _lanes(x.shape)
        beta = _sigmoid(x)
        g = -jnp.exp(al_ref[...]) * _softplus(x + dt_ref[...])
        gc = jnp.where((lane >= HEADS) & (lane < 2 * HEADS), g, 0.0)
        rc = _rows(x.shape) & (CHUNK - 1)
        for s in (1, 2, 4, 8, 16, 32):
            gc = gc + jnp.where(rc >= s, pltpu.roll(gc, s, 0), 0.0)
        sc_ref[...] = jnp.where(lane < HEADS, beta, gc)
        sel = (_lanes((HEADS, LANE)) == _rows((HEADS, LANE)) + HEADS).astype(f32)
        for c in range(nc):
            gr_ref[c] = lax.dot_general(sel, sc_ref[c * CHUNK:(c + 1) * CHUNK, :], (((1,), (1,)), ((), ())),
                                        preferred_element_type=f32, precision=lax.Precision.HIGHEST)

    return _pcall(
        body, name="scalars", grid=(1,),
        in_specs=[pl.BlockSpec((L, LANE), lambda i: (0, OFF_BA // LANE)), pl.BlockSpec((1, LANE), lambda i: (0, 0)),
                  pl.BlockSpec((1, LANE), lambda i: (0, 0))],
        out_specs=[pl.BlockSpec((L, LANE), lambda i: (0, 0)), pl.BlockSpec((nc, HEADS, CHUNK), lambda i: (0, 0, 0))],
        out_shape=[jax.ShapeDtypeStruct((L, LANE), f32), jax.ShapeDtypeStruct((nc, HEADS, CHUNK), f32)],
        compiler_params=_cparams("arbitrary"),
    )(proj, alog_p, dtb_p)


def _head_scalars(sc, gr_ref, h, ci=0):
    lane = _lanes(sc.shape)
    beta = jnp.sum(jnp.where(lane == h, sc, 0.0), axis=1, keepdims=True)
    gcc = jnp.sum(jnp.where(lane == HEADS + h, sc, 0.0), axis=1, keepdims=True)
    gcr = gr_ref[ci, h:h + 1, :]
    gl = jnp.sum(jnp.where(_lanes(gcr.shape) == CHUNK - 1, gcr, 0.0), axis=1, keepdims=True)
    ii, jj = _rows((CHUNK, CHUNK)), _lanes((CHUNK, CHUNK))
    dmat = jnp.where(ii >= jj, jnp.exp(jnp.minimum(gcc - gcr, 0.0)), 0.0)
    dmat_t = jnp.where(jj >= ii, jnp.exp(jnp.minimum(gcr - gcc, 0.0)), 0.0)
    return beta, gcc, gl, dmat, dmat_t, ii, jj


def _gdn_fwd(qkv, sc, gr):
    L = qkv.shape[0]
    nc = L // CHUNK
    W = GDN_WIDTH
    cps = GDN_CPS if nc % GDN_CPS == 0 else 1
    rows_per_step = cps * CHUNK

    def body(qkv_ref, sc_ref, gr_ref, o_ref, u_ref, w_ref, vn_ref, t_ref, sp_ref, s_scr):
        @pl.when(pl.program_id(0) == 0)
        def _():
            s_scr[...] = jnp.zeros_like(s_scr)
        HS = range(cps * HEADS)
        hd = [i % HEADS for i in HS]
        rs = [slice((i // HEADS) * CHUNK, (i // HEADS + 1) * CHUNK) for i in HS]
        cs = [slice(hd[i] * HEAD_DIM, (hd[i] + 1) * HEAD_DIM) for i in HS]
        q = [qkv_ref[rs[i], hd[i] * HEAD_DIM:(hd[i] + 1) * HEAD_DIM] for i in HS]
        k = [qkv_ref[rs[i], W + hd[i] * HEAD_DIM:W + (hd[i] + 1) * HEAD_DIM] for i in HS]
        v = [qkv_ref[rs[i], 2 * W + hd[i] * HEAD_DIM:2 * W + (hd[i] + 1) * HEAD_DIM] for i in HS]
        hsc = [_head_scalars(sc_ref[rs[i], :], gr_ref, hd[i], i // HEADS) for i in HS]
        beta, gcc, gl, dmat = ([x[i] for x in hsc] for i in range(4))
        ii, jj = hsc[0][5], hsc[0][6]
        eg = [jnp.exp(gcc[h]) for h in HS]
        kb = [k[h] * beta[h] for h in HS]
        kk = [_mm_nt(kb[h], k[h]) for h in HS]
        qk = [_mm_nt(q[h], k[h]) for h in HS]
        n0 = [-jnp.where(ii > jj, kk[h] * dmat[h], 0.0) for h in HS]
        n1 = [_mm(n0[h], n0[h]) for h in HS]
        n2 = [_mm(n1[h], n1[h]) for h in HS]
        p01 = [n0[h] + n1[h] + _mm(n0[h], n1[h]) for h in HS]
        n3 = [_mm(n2[h], n2[h]) for h in HS]
        n4 = [_mm(n3[h], n3[h]) for h in HS]
        p23 = [n2[h] + n3[h] + _mm(n2[h], n3[h]) for h in HS]
        n5 = [_mm(n4[h], n4[h]) for h in HS]
        p03 = [p01[h] + p23[h] + _mm(p01[h], p23[h]) for h in HS]
        p45 = [n4[h] + n5[h] + _mm(n4[h], n5[h]) for h in HS]
        t = [p03[h] + p45[h] + _mm(p03[h], p45[h]) for h in HS]
        vb = [v[h] * beta[h] for h in HS]
        kbg = [kb[h] * eg[h] for h in HS]
        uw = [_mm(t[h], _cat16([vb[h], kbg[h]], 1)) for h in HS]
        u = [vb[h] + uw[h][:, :HEAD_DIM] for h in HS]
        w = [kbg[h] + uw[h][:, HEAD_DIM:] for h in HS]
        wq = [_cat16([w[h], q[h] * eg[h]], 0) for h in HS]
        p = [jnp.where(ii >= jj, qk[h] * dmat[h], 0.0) for h in HS]
        ks = [k[h] * jnp.exp(gl[h] - gcc[h]) for h in HS]
        s = [s_scr[h] for h in range(HEADS)]
        for ci in range(cps):
            IS = range(ci * HEADS, (ci + 1) * HEADS)
            ws = [_mm(wq[i], s[hd[i]]) for i in IS]
            vn = [u[i] - ws[hd[i]][:CHUNK] for i in IS]
            pv = [_mm(p[i], vn[hd[i]]) for i in IS]
            kv = [_mm_tn(ks[i], vn[hd[i]]) for i in IS]
            for i in IS:
                h = hd[i]
                sp_ref[ci, cs[i], :] = s[h]
                o_ref[rs[i], cs[i]] = ws[h][CHUNK:] + pv[h]
                vn_ref[rs[i], cs[i]] = vn[h].astype(bf16)
            s = [jnp.exp(gl[i]) * s[hd[i]] + kv[hd[i]] for i in IS]
        for h in range(HEADS):
            s_scr[h] = s[h]
        for i in HS:
            u_ref[rs[i], cs[i]] = u[i].astype(bf16)
            w_ref[rs[i], cs[i]] = w[i].astype(bf16)
            t_ref[i // HEADS, hd[i]] = t[i].astype(bf16)

    row = lambda c: (c, 0)
    act, act16 = jax.ShapeDtypeStruct((L, W), f32), jax.ShapeDtypeStruct((L, W), bf16)
    return _pcall(
        body, name="gdn_fwd", grid=(nc // cps,),
        in_specs=[pl.BlockSpec((rows_per_step, 3 * W), row), pl.BlockSpec((rows_per_step, LANE), row),
                  pl.BlockSpec((cps, HEADS, CHUNK), lambda c: (c, 0, 0))],
        out_specs=[pl.BlockSpec((rows_per_step, W), row)] * 4 + [
            pl.BlockSpec((cps, HEADS, CHUNK, CHUNK), lambda c: (c, 0, 0, 0)),
            pl.BlockSpec((cps, W, HEAD_DIM), lambda c: (c, 0, 0))],
        out_shape=[act, act16, act16, act16, jax.ShapeDtypeStruct((nc, HEADS, CHUNK, CHUNK), bf16),
                   jax.ShapeDtypeStruct((nc, W, HEAD_DIM), f32)],
        scratch_shapes=[pltpu.VMEM((HEADS, HEAD_DIM, HEAD_DIM), f32)],
        compiler_params=_cparams("arbitrary"),
    )(qkv, sc, gr)


def _gdn_gate(o, proj, gnw):
    L = o.shape[0]

    def body(o_ref, z_ref, w_ref, m_ref):
        for ls in HALVES:
            ov, z = o_ref[:, ls], z_ref[:, ls]
            rms = lax.rsqrt(jnp.mean(ov * ov, axis=-1, keepdims=True) + EPS)
            m_ref[:, ls] = (((ov * rms) * w_ref[...]) * (z * _sigmoid(z))).astype(bf16)

    return _pcall(
        body, name="gdn_gate", grid=(GDN_WIDTH // ELT_W,),
        in_specs=[pl.BlockSpec((L, ELT_W), lambda j: (0, j)), pl.BlockSpec((L, ELT_W), lambda j: (0, OFF_ZG // ELT_W + j)),
                  pl.BlockSpec((1, LANE), lambda j: (0, 0))],
        out_specs=pl.BlockSpec((L, ELT_W), lambda j: (0, j)),
        out_shape=jax.ShapeDtypeStruct((L, GDN_WIDTH + CONV_WIDTH), bf16),
        compiler_params=_cparams("parallel"),
    )(o, proj, gnw)


def _conv3(u, cw_ref, ls):
    return cw_ref[2:3, ls] * u + cw_ref[1:2, ls] * _shift_down(u, 1) + cw_ref[0:1, ls] * _shift_down(u, 2)


def _conv_specs(L):
    return [pl.BlockSpec((L, CONV_BLOCK), lambda j: (0, OFF_CONV // CONV_BLOCK + j)),
            pl.BlockSpec((3, ELT_W), lambda j: (0, j)), pl.BlockSpec((1, ELT_W), lambda j: (0, j))]


def _conv_parts(ls):
    return [slice(g * ELT_W + ls.start, g * ELT_W + ls.stop) for g in range(4)]


def _conv_fwd(proj, cw, cb, mix):
    L = proj.shape[0]

    def body(p_ref, cw_ref, cb_ref, mix_in, m_ref):
        for ls in HALVES:
            sb, sc_, sh, sz = _conv_parts(ls)
            z = p_ref[:, sz]
            cv = _conv3(p_ref[:, sc_] * p_ref[:, sh], cw_ref, ls) + cb_ref[:, ls]
            m_ref[:, ls] = ((p_ref[:, sb] * cv) * (z * _sigmoid(z))).astype(bf16)

    return _pcall(
        body, name="conv_fwd", grid=(CONV_WIDTH // ELT_W,),
        in_specs=_conv_specs(L) + [ANY], out_specs=pl.BlockSpec((L, ELT_W), lambda j: (0, GDN_WIDTH // ELT_W + j)),
        out_shape=jax.ShapeDtypeStruct(mix.shape, mix.dtype), input_output_aliases={3: 0},
        compiler_params=_cparams("parallel"),
    )(proj, cw, cb, mix)


def _out_proj_loss(x, mix, wo, fw, tgt):
    L = x.shape[0]
    tm = min(512, L)
    MW = GDN_WIDTH + CONV_WIDTH

    def body(x_ref, m_ref, wo_ref, fw_ref, t_ref, dy_ref, dyb_ref, dm_ref, gfw_ref, loss_ref):
        @pl.when(pl.program_id(0) == 0)
        def _():
            gfw_ref[...] = jnp.zeros_like(gfw_ref)
            loss_ref[...] = jnp.zeros_like(loss_ref)
        y = x_ref[...] + jnp.dot(m_ref[...], wo_ref[...], preferred_element_type=f32)
        r = lax.rsqrt(jnp.mean(y * y, axis=-1, keepdims=True) + EPS)
        yh = y * r
        fwv = fw_ref[...]
        diff = yh * fwv - t_ref[...]
        loss_ref[...] += jnp.sum(jnp.sum(diff * diff, axis=-1, keepdims=True), axis=0, keepdims=True) * (0.5 / D_MODEL)
        dout = diff * (1.0 / D_MODEL)
        gfw_ref[...] += jnp.sum(dout * yh, axis=0, keepdims=True)
        dyh = dout * fwv
        dy = r * (dyh - yh * jnp.mean(dyh * yh, axis=-1, keepdims=True))
        dy_ref[...] = dy
        dyb = dy.astype(bf16)
        dyb_ref[...] = dyb
        dm_ref[...] = lax.dot_general(dyb, wo_ref[...], (((1,), (1,)), ((), ())), preferred_element_type=f32)

    row = lambda i: (i, 0)
    fix = lambda i: (0, 0)
    act = jax.ShapeDtypeStruct((L, D_MODEL), f32)
    return _pcall(
        body, name="out_proj_loss", grid=(L // tm,),
        in_specs=[pl.BlockSpec((tm, D_MODEL), row), pl.BlockSpec((tm, MW), row), pl.BlockSpec((MW, D_MODEL), fix),
                  pl.BlockSpec((1, D_MODEL), fix), pl.BlockSpec((tm, D_MODEL), row)],
        out_specs=[pl.BlockSpec((tm, D_MODEL), row), pl.BlockSpec((tm, D_MODEL), row), pl.BlockSpec((tm, MW), row),
                   pl.BlockSpec((1, D_MODEL), fix), pl.BlockSpec((1, LANE), fix)],
        out_shape=[act, jax.ShapeDtypeStruct((L, D_MODEL), bf16), jax.ShapeDtypeStruct((L, MW), f32),
                   jax.ShapeDtypeStruct((1, D_MODEL), f32), jax.ShapeDtypeStruct((1, LANE), f32)],
        compiler_params=_cparams("arbitrary"),
    )(x, mix, wo, fw, tgt)


def _tn_matmul(a, b, name):
    L, M = a.shape
    N = b.shape[1]
    tm = 512 if M % 512 == 0 else (768 if M % 768 == 0 else M)

    def body(a_ref, b_ref, o_ref):
        o_ref[...] = lax.dot_general(a_ref[...], b_ref[...], (((0,), (0,)), ((), ())),
                                     preferred_element_type=f32).astype(o_ref.dtype)

    return _pcall(
        body, name=name, grid=(M // tm,),
        in_specs=[pl.BlockSpec((L, tm), lambda i: (0, i)), pl.BlockSpec((L, N), lambda i: (0, 0))],
        out_specs=pl.BlockSpec((tm, N), lambda i: (i, 0)),
        out_shape=jax.ShapeDtypeStruct((M, N), bf16),
        compiler_params=_cparams("parallel"),
    )(a, b)


def _gdn_gate_bwd(o, proj, gnw, dmix_a, after):
    L = o.shape[0]

    def body(o_ref, z_ref, w_ref, dm_ref, after_ref, do_ref, dz_ref, gw_ref):
        @pl.when(pl.program_id(0) == 0)
        def _():
            gw_ref[...] = jnp.zeros_like(gw_ref)
        wv = w_ref[...]
        for ls in HALVES:
            ov, z, dm = o_ref[:, ls], z_ref[:, ls], dm_ref[:, ls]
            rms = lax.rsqrt(jnp.mean(ov * ov, axis=-1, keepdims=True) + EPS)
            xh = ov * rms
            sg = _sigmoid(z)
            d_on = dm * (z * sg)
            dz_ref[:, ls] = (dm * (xh * wv) * (sg * (1.0 + z * (1.0 - sg)))).astype(bf16)
            gw_ref[...] += jnp.sum(d_on * xh, axis=0, keepdims=True)
            dxh = d_on * wv
            do_ref[:, ls] = (rms * (dxh - xh * jnp.mean(dxh * xh, axis=-1, keepdims=True))).astype(bf16)

    wide = pl.BlockSpec((L, ELT_W), lambda j: (0, j))
    return _pcall(
        body, name="gdn_gate_bwd", grid=(GDN_WIDTH // ELT_W,),
        in_specs=[wide, pl.BlockSpec((L, ELT_W), lambda j: (0, OFF_ZG // ELT_W + j)),
                  pl.BlockSpec((1, LANE), lambda j: (0, 0)), wide, ANY],
        out_specs=[wide, pl.BlockSpec((L, ELT_W), lambda j: (0, OFF_ZG // ELT_W + j)),
                   pl.BlockSpec((1, LANE), lambda j: (0, 0))],
        out_shape=[jax.ShapeDtypeStruct((L, GDN_WIDTH), bf16), jax.ShapeDtypeStruct((L, PROJ_PAD), bf16),
                   jax.ShapeDtypeStruct((1, LANE), f32)],
        compiler_params=_cparams("arbitrary"),
    )(o, proj, gnw, dmix_a, after)


def _conv_bwd(proj, cw, cb, dmix_b, dproj):
    L = proj.shape[0]

    def body(p_ref, cw_ref, cb_ref, dm_ref, dproj_in, dp_ref, gcw_ref, gcb_ref):
        for ls in HALVES:
            sb, sc_, sh, sz_ = _conv_parts(ls)
            bv, cv_, hv, z, dm = p_ref[:, sb], p_ref[:, sc_], p_ref[:, sh], p_ref[:, sz_], dm_ref[:, ls]
            u = cv_ * hv
            cv = _conv3(u, cw_ref, ls) + cb_ref[:, ls]
            sg = _sigmoid(z)
            sz = z * sg
            dp_ref[:, sb] = (dm * cv * sz).astype(bf16)
            dp_ref[:, sz_] = (dm * (bv * cv) * (sg * (1.0 + z * (1.0 - sg)))).astype(bf16)
            dcv = dm * bv * sz
            gcb_ref[:, ls] = jnp.sum(dcv, axis=0, keepdims=True)
            dcv1, dcv2 = _shift_up(dcv, 1), _shift_up(dcv, 2)
            gcw_ref[2:3, ls] = jnp.sum(dcv * u, axis=0, keepdims=True)
            gcw_ref[1:2, ls] = jnp.sum(dcv1 * u, axis=0, keepdims=True)
            gcw_ref[0:1, ls] = jnp.sum(dcv2 * u, axis=0, keepdims=True)
            du = cw_ref[2:3, ls] * dcv + cw_ref[1:2, ls] * dcv1 + cw_ref[0:1, ls] * dcv2
            dp_ref[:, sc_] = (du * hv).astype(bf16)
            dp_ref[:, sh] = (du * cv_).astype(bf16)

    return _pcall(
        body, name="conv_bwd", grid=(CONV_WIDTH // ELT_W,),
        in_specs=_conv_specs(L) + [pl.BlockSpec((L, ELT_W), lambda j: (0, GDN_WIDTH // ELT_W + j)), ANY],
        out_specs=[pl.BlockSpec((L, CONV_BLOCK), lambda j: (0, OFF_CONV // CONV_BLOCK + j)),
                   pl.BlockSpec((3, ELT_W), lambda j: (0, j)), pl.BlockSpec((1, ELT_W), lambda j: (0, j))],
        out_shape=[jax.ShapeDtypeStruct(dproj.shape, dproj.dtype), jax.ShapeDtypeStruct((3, CONV_WIDTH), f32),
                   jax.ShapeDtypeStruct((1, CONV_WIDTH), f32)],
        input_output_aliases={4: 0},
        compiler_params=_cparams("parallel"),
    )(proj, cw, cb, dmix_b, dproj)


def _gdn_bwd(qkv, sc, gr, u_all, w_all, vn_all, t_all, sp_all, do_all):
    L = qkv.shape[0]
    nc = L // CHUNK
    W = GDN_WIDTH
    cps = GDN_CPS_BWD if nc % GDN_CPS_BWD == 0 else 1
    rows_per_step = cps * CHUNK
    nsteps = nc // cps

    def body(qkv_ref, sc_ref, gr_ref, u_ref, w_ref, vn_ref, t_ref, sp_ref, do_ref, dqkv_ref, dsc_ref, dgr_ref, ds_scr):
        @pl.when(pl.program_id(0) == 0)
        def _():
            ds_scr[...] = jnp.zeros_like(ds_scr)
        nh, base = HEADS, 0
        HS = range(cps * nh)
        hl = [i % nh for i in HS]
        hd = [base + hl[i] for i in HS]
        rs = [slice((i // nh) * CHUNK, (i // nh + 1) * CHUNK) for i in HS]
        cs = [slice(hd[i] * HEAD_DIM, (hd[i] + 1) * HEAD_DIM) for i in HS]
        q = [qkv_ref[rs[i], hd[i] * HEAD_DIM:(hd[i] + 1) * HEAD_DIM] for i in HS]
        k = [qkv_ref[rs[i], W + hd[i] * HEAD_DIM:W + (hd[i] + 1) * HEAD_DIM] for i in HS]
        v = [qkv_ref[rs[i], 2 * W + hd[i] * HEAD_DIM:2 * W + (hd[i] + 1) * HEAD_DIM] for i in HS]
        hsc = [_head_scalars(sc_ref[rs[i], :], gr_ref, hd[i], i // nh) for i in HS]
        beta, gcc, gl, dmat, dmat_t = ([x[i] for x in hsc] for i in range(5))
        ii, jj = hsc[0][5], hsc[0][6]
        eg = [jnp.exp(gcc[h]) for h in HS]
        ekl = [jnp.exp(gl[h] - gcc[h]) for h in HS]
        egl = [jnp.exp(gl[h]) for h in HS]
        kb = [k[h] * beta[h] for h in HS]
        ks = [k[h] * ekl[h] for h in HS]
        do = [do_ref[rs[h], cs[h]] for h in HS]
        vn = [vn_ref[rs[h], cs[h]] for h in HS]
        s = [sp_ref[h // nh, cs[h], :] for h in HS]
        w = [w_ref[rs[h], cs[h]] for h in HS]
        qd = [q[h] * eg[h] for h in HS]

        kq = [_mm_nt(k[h], q[h]) for h in HS]
        p_t = [jnp.where(jj >= ii, kq[h] * dmat_t[h], 0.0) for h in HS]
        ptd = [_mm(p_t[h], do[h]) for h in HS]
        qw = [_cat16([qd[h], -w[h]], 0) for h in HS]
        dsn, dvn, dodv = [None] * len(HS), [None] * len(HS), [None] * len(HS)
        ds_cur = [ds_scr[base + h] for h in range(nh)]
        for ci in reversed(range(cps)):
            IS = range(ci * nh, (ci + 1) * nh)
            ksd = [_mm(ks[i], ds_cur[hl[i]]) for i in IS]
            for i in IS:
                dsn[i] = ds_cur[hl[i]]
                dvn[i] = ptd[i] + ksd[hl[i]]
                dodv[i] = _cat16([do[i], dvn[i]], 0)
            dsq = [_mm_tn(qw[i], dodv[i]) for i in IS]
            ds_cur = [egl[i] * ds_cur[hl[i]] + dsq[hl[i]] for i in IS]
        for h in range(nh):
            ds_scr[base + h] = ds_cur[h]
        x1 = [_mm_nt(dodv[h], s[h]) for h in HS]
        dks = [_mm_nt(vn[h], dsn[h]) for h in HS]
        dov = [_mm_nt(do[h], vn[h]) for h in HS]
        vdo = [_mm_nt(vn[h], do[h]) for h in HS]
        kk = [_mm_nt(kb[h], k[h]) for h in HS]
        qk = [_mm_nt(q[h], k[h]) for h in HS]
        dgl = [egl[h] * jnp.sum(jnp.sum(s[h] * dsn[h], axis=1, keepdims=True), axis=0, keepdims=True) for h in HS]
        dqd = [x1[h][:CHUNK] for h in HS]
        duw = [jnp.concatenate([dvn[h], -x1[h][CHUNK:]], axis=1) for h in HS]
        tdu = [_mm_tn(t_ref[h // nh, hd[h]], duw[h]) for h in HS]
        dvk = [duw[h] + tdu[h] for h in HS]
        uw = [jnp.concatenate([u_ref[rs[h], cs[h]], w[h]], axis=1) for h in HS]
        da = [-jnp.where(ii > jj, _mm_nt(dvk[h], uw[h]), 0.0) for h in HS]
        da_t = [-jnp.where(jj > ii, _mm_nt(uw[h], dvk[h]), 0.0) for h in HS]
        dp = [jnp.where(ii >= jj, dov[h], 0.0) for h in HS]
        dp_t = [jnp.where(jj >= ii, vdo[h], 0.0) for h in HS]
        r1 = [_mm(_cat16([da[h] * dmat[h], dp[h] * dmat[h]], 0), k[h]) for h in HS]
        dk1 = [_mm(_cat16([da_t[h] * dmat_t[h], dp_t[h] * dmat_t[h]], 1), _cat16([kb[h], q[h]], 0)) for h in HS]
        lane = _lanes((CHUNK, LANE))
        for ci in range(cps):
            dsc = jnp.zeros((CHUNK, LANE), f32)
            for i in range(ci * nh, (ci + 1) * nh):
                h = hd[i]
                a = jnp.where(ii > jj, kk[i] * dmat[i], 0.0)
                p = jnp.where(ii >= jj, qk[i] * dmat[i], 0.0)
                gmat = da[i] * a + dp[i] * p
                dvb, dkbg = dvk[i][:, :HEAD_DIM], dvk[i][:, HEAD_DIM:]
                kbg = kb[i] * eg[i]
                dkb = r1[i][:CHUNK] + dkbg * eg[i]
                dq = r1[i][CHUNK:] + dqd[i] * eg[i]
                dk = dk1[i] + dks[i] * ekl[i] + dkb * beta[i]
                dbeta = jnp.sum(dkb * k[i] + dvb * v[i], axis=1, keepdims=True)
                ksum = jnp.sum(dks[i] * ks[i], axis=1, keepdims=True)
                dgl_tot = dgl[i] + jnp.sum(ksum, axis=0, keepdims=True)
                dgc = (jnp.sum(gmat, axis=1, keepdims=True) + jnp.sum(dqd[i] * qd[i] + dkbg * kbg, axis=1, keepdims=True)
                       - ksum)
                dgc = dgc + jnp.where(_rows(dgc.shape) == CHUNK - 1, dgl_tot, 0.0)
                dqkv_ref[rs[i], h * HEAD_DIM:(h + 1) * HEAD_DIM] = dq
                dqkv_ref[rs[i], W + h * HEAD_DIM:W + (h + 1) * HEAD_DIM] = dk
                dqkv_ref[rs[i], 2 * W + h * HEAD_DIM:2 * W + (h + 1) * HEAD_DIM] = dvb * beta[i]
                dsc = jnp.where(lane == h, dbeta, jnp.where(lane == HEADS + h, dgc, dsc))
                dgr_ref[ci, h:h + 1, :] = jnp.sum(gmat, axis=0, keepdims=True)
            dsc_ref[ci * CHUNK:(ci + 1) * CHUNK, :] = dsc

    row = lambda c: (nsteps - 1 - c, 0)
    lead3 = lambda c: (nsteps - 1 - c, 0, 0)
    return _pcall(
        body, name="gdn_bwd", grid=(nsteps,),
        in_specs=[pl.BlockSpec((rows_per_step, 3 * W), row), pl.BlockSpec((rows_per_step, LANE), row),
                  pl.BlockSpec((cps, HEADS, CHUNK), lead3),
                  pl.BlockSpec((rows_per_step, W), row), pl.BlockSpec((rows_per_step, W), row),
                  pl.BlockSpec((rows_per_step, W), row),
                  pl.BlockSpec((cps, HEADS, CHUNK, CHUNK), lambda c: (nsteps - 1 - c, 0, 0, 0)),
                  pl.BlockSpec((cps, W, HEAD_DIM), lead3), pl.BlockSpec((rows_per_step, W), row)],
        out_specs=[pl.BlockSpec((rows_per_step, 3 * W), row), pl.BlockSpec((rows_per_step, LANE), row),
                   pl.BlockSpec((cps, HEADS, CHUNK), lead3)],
        out_shape=[jax.ShapeDtypeStruct((L, 3 * W), f32), jax.ShapeDtypeStruct((L, LANE), f32),
                   jax.ShapeDtypeStruct((nc, HEADS, CHUNK), f32)],
        scratch_shapes=[pltpu.VMEM((HEADS, HEAD_DIM, HEAD_DIM), f32)],
        compiler_params=_cparams("arbitrary"),
    )(qkv, sc, gr, u_all, w_all, vn_all, t_all, sp_all, do_all)


def _qkv_bwd(proj, cw, dn, dproj):
    L = proj.shape[0]

    def body(x_ref, cw_ref, dn_ref, dproj_in, dx_ref, gcw_ref):
        j = pl.program_id(0)
        steps = GDN_WIDTH // ELT_W
        scale = jnp.where(j < steps, HEAD_DIM ** -0.5, 1.0).astype(f32)
        for ls in HALVES:
            x, dn_v = x_ref[:, ls], dn_ref[:, ls]
            c = _conv4(x, cw_ref, ls)
            sg = _sigmoid(c)
            a = c * sg
            rn = lax.rsqrt(jnp.sum(a * a, axis=1, keepdims=True) + EPS)
            da_n = (scale * rn) * (dn_v - a * ((rn * rn) * jnp.sum(dn_v * a, axis=1, keepdims=True)))
            da = jnp.where(j < 2 * steps, da_n, dn_v)
            dc = da * (sg * (1.0 + c * (1.0 - sg)))
            dc1, dc2, dc3 = _shift_up(dc, 1), _shift_up(dc, 2), _shift_up(dc, 3)
            gcw_ref[3:4, ls] = jnp.sum(dc * x, axis=0, keepdims=True)
            gcw_ref[2:3, ls] = jnp.sum(dc1 * x, axis=0, keepdims=True)
            gcw_ref[1:2, ls] = jnp.sum(dc2 * x, axis=0, keepdims=True)
            gcw_ref[0:1, ls] = jnp.sum(dc3 * x, axis=0, keepdims=True)
            dx = cw_ref[3:4, ls] * dc + cw_ref[2:3, ls] * dc1 + cw_ref[1:2, ls] * dc2 + cw_ref[0:1, ls] * dc3
            dx_ref[:, ls] = dx.astype(bf16)

    col = pl.BlockSpec((L, ELT_W), lambda j: (0, j))
    wspec = pl.BlockSpec((4, ELT_W), lambda j: (0, j))
    return _pcall(
        body, name="qkv_bwd", grid=(3 * GDN_WIDTH // ELT_W,),
        in_specs=[col, wspec, col, ANY], out_specs=[col, wspec],
        out_shape=[jax.ShapeDtypeStruct(dproj.shape, dproj.dtype), jax.ShapeDtypeStruct((4, 3 * GDN_WIDTH), f32)],
        input_output_aliases={3: 0},
        compiler_params=_cparams("parallel"),
    )(proj, cw, dn, dproj)


def _scalars_bwd(proj, alog_p, dtb_p, dsc, dgr_col, dproj, after):
    L = proj.shape[0]

    def body(x_ref, al_ref, dt_ref, dsc_ref, dgr_ref, dproj_in, after_ref, dba_ref, gs_ref):
        x, dsc_v = x_ref[...], dsc_ref[...]
        lane = _lanes(x.shape)
        dec = (lane >= HEADS) & (lane < 2 * HEADS)
        dg = jnp.where(dec, dsc_v - dgr_ref[...], 0.0)
        rc = _rows(x.shape) & (CHUNK - 1)
        for s in (1, 2, 4, 8, 16, 32):
            dg = dg + jnp.where(rc + s < CHUNK, pltpu.roll(dg, L - s, 0), 0.0)
        xa = x + dt_ref[...]
        ea = jnp.exp(al_ref[...])
        g = -ea * _softplus(xa)
        da = dg * (-ea) * _sigmoid(xa)
        beta = _sigmoid(x)
        db = dsc_v * beta * (1.0 - beta)
        dba_ref[:, :LANE] = jnp.where(lane < HEADS, db, jnp.where(dec, da, 0.0)).astype(bf16)
        dba_ref[:, LANE:] = jnp.zeros((L, ELT_W - LANE), bf16)
        g_al = jnp.sum(jnp.where(dec, dg * g, 0.0), axis=0, keepdims=True)
        g_dt = jnp.sum(jnp.where(dec, da, 0.0), axis=0, keepdims=True)
        row8 = _rows(gs_ref.shape)
        gs = jnp.where(row8 == 0, g_al, jnp.where(row8 == 1, g_dt, 0.0))
        gs_ref[...] = pltpu.roll(gs, LANE - HEADS, 1)

    full = pl.BlockSpec((L, LANE), lambda i: (0, 0))
    vec = pl.BlockSpec((1, LANE), lambda i: (0, 0))
    return _pcall(
        body, name="scalars_bwd", grid=(1,),
        in_specs=[pl.BlockSpec((L, LANE), lambda i: (0, OFF_BA // LANE)), vec, vec, full, full, ANY, ANY],
        out_specs=[pl.BlockSpec((L, ELT_W), lambda i: (0, OFF_BA // ELT_W)), pl.BlockSpec((8, LANE), lambda i: (0, 0))],
        out_shape=[jax.ShapeDtypeStruct(dproj.shape, dproj.dtype), jax.ShapeDtypeStruct((8, LANE), f32)],
        input_output_aliases={5: 0},
        compiler_params=_cparams("arbitrary"),
    )(proj, alog_p, dtb_p, dsc, dgr_col, dproj, after)


def _input_grad(dproj, wpad, x, nw, dy, after):
    L = x.shape[0]
    tm = min(512, L)
    cuts = (0, 1024, 3072, 5120, 7168, PROJ_PAD)
    nk = len(cuts) - 1

    def body(dp_ref, w_hbm, x_ref, nw_ref, dy_ref, after_ref, gx_ref, gnw_ref, w_vmem, sems):
        first = pl.program_id(0) == 0
        loads = [pltpu.make_async_copy(w_hbm.at[cuts[k]:cuts[k + 1], :], w_vmem.at[cuts[k]:cuts[k + 1], :], sems.at[k])
                 for k in range(nk)]

        @pl.when(first)
        def _():
            for cp in loads:
                cp.start()
            gnw_ref[...] = jnp.zeros_like(gnw_ref)
        dh = None
        for k in range(nk):
            pl.when(first)(loads[k].wait)
            part = jnp.dot(dp_ref[:, cuts[k]:cuts[k + 1]], w_vmem[cuts[k]:cuts[k + 1], :], preferred_element_type=f32)
            dh = part if dh is None else dh + part
        xv, nwv = x_ref[...], nw_ref[...]
        r = lax.rsqrt(jnp.mean(xv * xv, axis=-1, keepdims=True) + EPS)
        xh = xv * r
        gnw_ref[...] += jnp.sum(dh * xh, axis=0, keepdims=True)
        dxh = dh * nwv
        gx_ref[...] = dy_ref[...] + r * (dxh - xh * jnp.mean(dxh * xh, axis=-1, keepdims=True))

    row = lambda i: (i, 0)
    fix = lambda i: (0, 0)
    return _pcall(
        body, name="input_grad", grid=(L // tm,),
        in_specs=[pl.BlockSpec((tm, PROJ_PAD), row), ANY, pl.BlockSpec((tm, D_MODEL), row),
                  pl.BlockSpec((1, D_MODEL), fix), pl.BlockSpec((tm, D_MODEL), row), ANY],
        out_specs=[pl.BlockSpec((tm, D_MODEL), row), pl.BlockSpec((1, D_MODEL), fix)],
        out_shape=[jax.ShapeDtypeStruct((L, D_MODEL), f32), jax.ShapeDtypeStruct((1, D_MODEL), f32)],
        scratch_shapes=[pltpu.VMEM(wpad.shape, bf16), pltpu.SemaphoreType.DMA((nk,))],
        compiler_params=_cparams("arbitrary"),
    )(dproj, wpad, x, nw, dy, after)


def _adamw_reduce(parts, w, m, v, name):
    R, C = w.shape
    n_parts = parts.shape[0]
    tr = 128 if R % 128 == 0 else R
    c1 = 1.0 - ADAM_B1 ** ADAM_STEP
    c2 = 1.0 - ADAM_B2 ** ADAM_STEP

    def body(p_ref, w_ref, m_ref, v_ref, g_ref, d_ref, nm_ref, nv_ref):
        g = p_ref[0].astype(f32)
        for s in range(1, n_parts):
            g = g + p_ref[s].astype(f32)
        nm = ADAM_B1 * m_ref[...] + (1.0 - ADAM_B1) * g
        nv = ADAM_B2 * v_ref[...] + (1.0 - ADAM_B2) * (g * g)
        g_ref[...] = g
        nm_ref[...] = nm
        nv_ref[...] = nv
        d_ref[...] = -ADAM_LR * ((nm / c1) / (jnp.sqrt(nv / c2) + ADAM_EPS) + ADAM_WD * w_ref[...])

    blk = pl.BlockSpec((tr, C), lambda i: (i, 0))
    out = jax.ShapeDtypeStruct((R, C), f32)
    return _pcall(
        body, name=name, grid=(R // tr,),
        in_specs=[pl.BlockSpec((n_parts, tr, C), lambda i: (0, i, 0)), blk, blk, blk],
        out_specs=[blk] * 4, out_shape=[out] * 4,
        compiler_params=_cparams("parallel"),
    )(parts, w, m, v)


SMALL_SLOTS = ((0, D_MODEL), (D_MODEL, D_MODEL), (2 * D_MODEL, D_MODEL), (3 * D_MODEL, LANE),
               (3 * D_MODEL + LANE, HEADS), (3 * D_MODEL + 2 * LANE, HEADS))
SMALL_LOSS = 3 * D_MODEL + 3 * LANE
SMALL_W = SMALL_LOSS + LANE


def _pack_small(gs, after):
    def body(nw_ref, cb_ref, fw_ref, gn_ref, sc_ref, ls_ref, after_ref, o_ref):
        for ref, (start, width) in zip((nw_ref, cb_ref, fw_ref, gn_ref), SMALL_SLOTS[:4]):
            o_ref[:, start:start + width] = ref[...]
        o_ref[:, SMALL_SLOTS[4][0]:SMALL_SLOTS[4][0] + LANE] = sc_ref[0:1, :]
        o_ref[:, SMALL_SLOTS[5][0]:SMALL_SLOTS[5][0] + LANE] = sc_ref[1:2, :]
        o_ref[:, SMALL_LOSS:SMALL_W] = ls_ref[...]

    vm = pl.BlockSpec(memory_space=pltpu.VMEM)
    return _pcall(body, name="pack_small_grads", out_shape=jax.ShapeDtypeStruct((1, SMALL_W), f32),
                  in_specs=[vm] * 6 + [ANY], out_specs=vm)(*gs, after)


def _adamw_small(parts, ws, ms, vs):
    c1 = 1.0 - ADAM_B1 ** ADAM_STEP
    c2 = 1.0 - ADAM_B2 ** ADAM_STEP
    np_ = len(ws)

    def body(*refs):
        p_ref = refs[0]
        w_refs, m_refs, v_refs = refs[1:1 + np_], refs[1 + np_:1 + 2 * np_], refs[1 + 2 * np_:1 + 3 * np_]
        outs = refs[1 + 3 * np_:]
        g_refs, d_refs, nm_refs, nv_refs = (outs[i * np_:(i + 1) * np_] for i in range(4))
        loss_ref = outs[4 * np_]

        def total(start, width):
            t = p_ref[0, :, start:start + width]
            for s in range(1, N_DEV):
                t = t + p_ref[s, :, start:start + width]
            return t

        for i, (start, width) in enumerate(SMALL_SLOTS):
            g = total(start, width)
            nm = ADAM_B1 * m_refs[i][...] + (1.0 - ADAM_B1) * g
            nv = ADAM_B2 * v_refs[i][...] + (1.0 - ADAM_B2) * (g * g)
            g_refs[i][...] = g
            nm_refs[i][...] = nm
            nv_refs[i][...] = nv
            d_refs[i][...] = -ADAM_LR * ((nm / c1) / (jnp.sqrt(nv / c2) + ADAM_EPS) + ADAM_WD * w_refs[i][...])
        loss_ref[...] = total(SMALL_LOSS, LANE)

    vm = pl.BlockSpec(memory_space=pltpu.VMEM)
    shapes = [jax.ShapeDtypeStruct(w.shape, f32) for w in ws]
    res = _pcall(body, name="adamw_small", out_shape=shapes * 4 + [jax.ShapeDtypeStruct((1, LANE), f32)],
                 in_specs=[vm] * (1 + 3 * np_), out_specs=[vm] * (4 * np_ + 1))(parts, *ws, *ms, *vs)
    return [res[i * np_:(i + 1) * np_] for i in range(4)], res[4 * np_]


def _adamw_w_in(part_a, part_b, w3, m3, v3, after):
    _, n, _ = part_a.shape
    c1 = 1.0 - ADAM_B1 ** ADAM_STEP
    c2 = 1.0 - ADAM_B2 ** ADAM_STEP

    def body(pa_ref, pb_ref, w_ref, m_ref, v_ref, after_ref, g_ref, d_ref, nm_ref, nv_ref):
        g = pa_ref[0].astype(f32) + pb_ref[0].astype(f32)
        nm = ADAM_B1 * m_ref[:, 0, :] + (1.0 - ADAM_B1) * g
        nv = ADAM_B2 * v_ref[:, 0, :] + (1.0 - ADAM_B2) * (g * g)
        g_ref[:, 0, :] = g
        nm_ref[:, 0, :] = nm
        nv_ref[:, 0, :] = nv
        d_ref[:, 0, :] = -ADAM_LR * ((nm / c1) / (jnp.sqrt(nv / c2) + ADAM_EPS) + ADAM_WD * w_ref[:, 0, :])

    tile = 2 * COL_TILE
    blk = pl.BlockSpec((n, 1, tile), lambda j: (0, 0, j))
    out = jax.ShapeDtypeStruct((n, 1, D_MODEL), f32)
    return _pcall(
        body, name="adamw_w_in", grid=(D_MODEL // tile,),
        in_specs=[pl.BlockSpec((1, n, tile), lambda j: (0, 0, j))] * 2 + [blk, blk, blk, ANY],
        out_specs=[blk] * 4, out_shape=[out] * 4,
        compiler_params=_cparams("parallel"),
    )(part_a, part_b, w3, m3, v3, after)


def _pad_lanes(vec8, start):
    return jnp.pad(vec8.reshape(1, -1), ((0, 0), (start, LANE - start - vec8.size)))


def kernel(x, norm_in_w, w_in, conv_qkv_w, A_log, dt_bias, gdn_norm_w, conv_w, conv_b, w_out, final_norm_w, loss_target, m_norm_in_w, m_w_in, m_conv_qkv_w, m_A_log, m_dt_bias, m_gdn_norm_w, m_conv_w, m_conv_b, m_w_out, m_final_norm_w, v_norm_in_w, v_w_in, v_conv_qkv_w, v_A_log, v_dt_bias, v_gdn_norm_w, v_conv_w, v_conv_b, v_w_out, v_final_norm_w):
    L = x.shape[1]
    nc = L // CHUNK
    xs = x[0]
    tgt = loss_target[0]
    fnw = final_norm_w.reshape(1, D_MODEL)

    as_rows = lambda a: jnp.transpose(a, (2, 0, 1))
    win_g, cqkv_g, cw_g = _all_gather([_cast_w_in(as_rows(w_in)), conv_qkv_w[0], conv_w[0]], "gather_weights",
                                      pieces=[4, 1, 1])
    wpad = _relayout_w_in(win_g)
    cqkv = jnp.concatenate([cqkv_g[d] for d in range(N_DEV)], axis=1)
    cw = jnp.concatenate([cw_g[d] for d in range(N_DEV)], axis=1)
    alog_p = _pad_lanes(A_log, HEADS)
    dtb_p = _pad_lanes(dt_bias, HEADS)
    tok = lambda started: started[4]
    wo_started = _spread_start(w_out[0].astype(bf16), wpad, "gather", "gather_w_out_start")

    proj, h = _in_proj(xs, norm_in_w, wpad, tok(wo_started))
    qkv = _qkv_act(proj, cqkv)
    sc, gr = _scalars(proj, alog_p, dtb_p)
    o, u_all, w_all, vn_all, t_all, sp_all = _gdn_fwd(qkv, sc, gr)
    mix = _conv_fwd(proj, cw, conv_b, _gdn_gate(o, proj, gdn_norm_w))
    wo = _spread_wait(wo_started, mix, "gather", "gather_w_out_wait")[1].reshape(-1, D_MODEL)
    dy, dyb, dmix, g_fnw, loss_v = _out_proj_loss(xs, mix, wo, fnw, tgt)

    g_wout = _tn_matmul(mix, dyb, "grad_w_out")
    gwo_started = _spread_start(g_wout.reshape(N_DEV, -1, D_MODEL), dyb, "scatter", "exchange_grad_w_out_start")
    do, dproj, g_gnw = _gdn_gate_bwd(o, proj, gdn_norm_w, dmix, tok(gwo_started))
    dproj, g_cw, g_cb = _conv_bwd(proj, cw, conv_b, dmix, dproj)
    dqkv_n, dsc, dgr = _gdn_bwd(qkv, sc, gr, u_all, w_all, vn_all, t_all, sp_all, do)
    dproj, g_cqkv = _qkv_bwd(proj, cqkv, dqkv_n, dproj)
    g_cqkv_blk = g_cqkv.reshape(4, N_DEV, -1).transpose(1, 0, 2)
    g_cw_blk = jnp.pad(g_cw.reshape(3, N_DEV, -1).transpose(1, 0, 2),
                       ((0, 0), (0, 1), (0, g_cqkv_blk.shape[2] - g_cw.shape[1] // N_DEV)))
    gsm_started = _spread_start(jnp.concatenate([g_cqkv_blk, g_cw_blk], axis=1), g_cqkv, "scatter",
                                "exchange_small_sharded_grads_start")
    dgr_col = jnp.pad(dgr.transpose(0, 2, 1).reshape(L, HEADS), ((0, 0), (HEADS, LANE - 2 * HEADS)))
    dproj, g_sc = _scalars_bwd(proj, alog_p, dtb_p, dsc, dgr_col, dproj, tok(gsm_started))
    g_win_blk = _grad_blocks(_tn_matmul(dproj, h, "grad_w_in"))

    (p_win,) = _pair_exchange([g_win_blk], "exchange_grads_pair")
    r_small = _spread_wait(gsm_started, p_win, "scatter", "exchange_small_sharded_grads_wait")[1]
    r_cqkv, r_cw = r_small[:, :4, :], r_small[:, 4:7, :g_cw.shape[1] // N_DEV]
    s_win = _pair_sum(g_win_blk, p_win, "pair_sum_w_in")
    gw1_started = _spread_start(s_win, r_small, "axis_a", "exchange_grads_axis1_start")
    grad_x, g_nw = _input_grad(dproj, wpad, xs, norm_in_w, dy, tok(gw1_started))
    s_thru, got1 = _spread_wait(gw1_started, grad_x, "axis_a", "exchange_grads_axis1_wait")
    t_win = _axis_sum(s_thru, got1, "axis_sum_w_in")
    gw2_started = _spread_start(t_win, got1, "axis_b", "exchange_grads_axis2_start")

    r_wout = _spread_wait(gwo_started, tok(gw2_started), "scatter", "exchange_grad_w_out_wait")[1]
    upd_wout =_adamw_reduce(r_wout, w_out[0], m_w_out[0], v_w_out[0], "adamw_w_out")
    upd_cqkv = _adamw_reduce(r_cqkv, conv_qkv_w[0], m_conv_qkv_w[0], v_conv_qkv_w[0], "adamw_conv_qkv_w")
    upd_cw = _adamw_reduce(r_cw, conv_w[0], m_conv_w[0], v_conv_w[0], "adamw_conv_w")

    t_thru, got2 = _spread_wait(gw2_started, upd_cw[0], "axis_b", "exchange_grads_axis2_wait")

    small_g = _pack_small([g_nw, g_cb, g_fnw, g_gnw, g_sc, loss_v], got2)
    gsg_started = _spread_start(small_g, got2, "gather", "gather_small_grads_start")
    upd_win_t = _adamw_w_in(t_thru, got2, as_rows(w_in), as_rows(m_w_in), as_rows(v_w_in), tok(gsg_started))
    upd_win = [jnp.transpose(a, (1, 2, 0)) for a in upd_win_t]
    small_all = _spread_wait(gsg_started, upd_win_t[0], "gather", "gather_small_grads_wait")[1]
    fvec = lambda a: a.reshape(1, D_MODEL)
    upd_small, loss_sum = _adamw_small(
        small_all,
        [norm_in_w, conv_b, fvec(final_norm_w), gdn_norm_w, A_log, dt_bias],
        [m_norm_in_w, m_conv_b, fvec(m_final_norm_w), m_gdn_norm_w, m_A_log, m_dt_bias],
        [v_norm_in_w, v_conv_b, fvec(v_final_norm_w), v_gdn_norm_w, v_A_log, v_dt_bias])

    outs = [loss_sum[0, 0], grad_x[None]]
    for k in range(4):
        nw_k, cb_k, fw_k, gn_k, al_k, dt_k = upd_small[k]
        outs += [nw_k, upd_win[k], upd_cqkv[k][None], al_k, dt_k, gn_k,
                 upd_cw[k][None], cb_k, upd_wout[k][None], fw_k.reshape(D_MODEL)]
    return tuple(outs)
```

```python
import jax
import jax.numpy as jnp
from jax import lax
from jax.experimental import pallas as pl
from jax.experimental.pallas import tpu as pltpu

f32 = jnp.float32
bf16 = jnp.bfloat16

N_DEV = 8
D_MODEL = 1024
HEADS = 8
HEAD_DIM = 128
CHUNK = 64
GDN_CPS = 4
GDN_CPS_BWD = 1
GDN_WIDTH = HEADS * HEAD_DIM
CONV_WIDTH = 1024
PROJ_WIDTH = 8208
SHARD_W = PROJ_WIDTH // N_DEV
EPS = 1e-6

LANE = 128
ELT_W = 256

OFF_QKV, OFF_ZG, OFF_CONV, OFF_BA = 0, 3072, 4096, 8192
CONV_BLOCK = 4 * ELT_W
PROJ_PAD = 8448
NAT_BA, NAT_CONV = 4096, 4112


def _padded_col(n):
    if n < NAT_BA:
        return n
    if n < NAT_CONV:
        return OFF_BA + n - NAT_BA
    g, ch = divmod(n - NAT_CONV, CONV_WIDTH)
    j, r = divmod(ch, ELT_W)
    return OFF_CONV + CONV_BLOCK * j + ELT_W * g + r


def _layout_segments(n0, n1):
    cuts = [NAT_BA, NAT_CONV] + [NAT_CONV + ELT_W * k for k in range(1, 4 * CONV_WIDTH // ELT_W)]
    pts = [n0] + [c for c in cuts if n0 < c < n1] + [n1]
    return [(lo, hi - lo, _padded_col(lo)) for lo, hi in zip(pts, pts[1:])]

ADAM_LR, ADAM_B1, ADAM_B2, ADAM_EPS, ADAM_WD, ADAM_STEP = 0.001, 0.9, 0.999, 1e-08, 0.01, 10

V7X_VMEM_BYTES = 64 * 1024 * 1024
VMEM_LIMIT = V7X_VMEM_BYTES - 8 * 1024 * 1024

MESH = pl.DeviceIdType.MESH
ANY = pl.BlockSpec(memory_space=pl.ANY)


def _pcall(body, **kw):
    return pl.pallas_call(body, **kw)


def _cparams(*sem):
    return pltpu.CompilerParams(dimension_semantics=sem if sem else None, vmem_limit_bytes=VMEM_LIMIT)


def _mm(a, b):
    return jnp.dot(a.astype(bf16), b.astype(bf16), preferred_element_type=f32)


def _mm_nt(a, b):
    return lax.dot_general(a.astype(bf16), b.astype(bf16), (((1,), (1,)), ((), ())), preferred_element_type=f32)


def _cat16(parts, axis):
    return jnp.concatenate([p.astype(bf16) for p in parts], axis=axis)


def _mm_tn(a, b):
    return lax.dot_general(a.astype(bf16), b.astype(bf16), (((0,), (0,)), ((), ())), preferred_element_type=f32)


def _rows(shape):
    return lax.broadcasted_iota(jnp.int32, shape, 0)


def _lanes(shape):
    return lax.broadcasted_iota(jnp.int32, shape, 1)


def _shift_down(x, s):
    if s == 0:
        return x
    return jnp.where(_rows(x.shape) >= s, pltpu.roll(x, s, 0), 0.0)


def _shift_up(x, s):
    if s == 0:
        return x
    n = x.shape[0]
    return jnp.where(_rows(x.shape) < n - s, pltpu.roll(x, n - s, 0), 0.0)


def _sigmoid(x):
    return jax.nn.sigmoid(x)


def _softplus(x):
    e = jnp.exp(-jnp.abs(x))
    small = e * (1.0 - e * (0.5 - e * (1.0 / 3.0)))
    return jnp.maximum(x, 0.0) + jnp.where(e < 0.01, small, jnp.log(1.0 + e))


def _mesh_pos():
    return lax.axis_index("x"), lax.axis_index("y"), lax.axis_index("c")


def _flat(px, py, pc):
    return 4 * px + 2 * py + pc


def _all_gather(xs, name, pieces=None):
    n = len(xs)
    pieces = pieces or [1] * n
    items = [(a, q) for a in range(n) for q in range(pieces[a])]
    ni = len(items)

    def view(ref, i):
        a, q = items[i]
        if pieces[a] == 1:
            return ref
        wd = xs[a].shape[-1] // pieces[a]
        return ref.at[(slice(None),) * (xs[a].ndim - 1) + (pl.ds(q * wd, wd),)]

    def body(*refs):
        x_refs, o_refs = refs[:n], refs[n:2 * n]
        send_sems, recv_sems, local_sems = refs[2 * n:]
        x, y, c = _mesh_pos()
        me, sibling = (x, y, c), (x, y, 1 - c)
        flip = lambda v, bit: v + bit - 2 * v * bit
        nbr_a = (flip(x, 1 - c), flip(y, c))
        nbr_b = (flip(x, c), flip(y, 1 - c))
        diag = (1 - x, 1 - y)

        def copy(i, k, block, to, own=False):
            a = items[i][0]
            dst = view(o_refs[a].at[_flat(*block)], i)
            return pltpu.make_async_remote_copy(
                src_ref=view(x_refs[a], i) if own else dst, dst_ref=dst,
                send_sem=send_sems.at[i, k], recv_sem=recv_sems.at[i, k], device_id=to, device_id_type=MESH)

        mine, sent = [], []

        def go(cp):
            cp.start()
            sent.append(cp)

        for a in range(n):
            cp = pltpu.make_async_copy(x_refs[a], o_refs[a].at[_flat(*me)], local_sems.at[a])
            cp.start()
            mine.append(cp)
        for a in range(ni):
            go(copy(a, 1, me, (*nbr_a, c), own=True))
            go(copy(a, 2, me, (*nbr_b, c), own=True))
            go(copy(a, 0, me, sibling, own=True))
        for a in range(ni):
            copy(a, 1, (*nbr_a, c), me).wait_recv()
            go(copy(a, 3, (*nbr_a, c), (*nbr_b, c)))
            go(copy(a, 4, (*nbr_a, c), sibling))
        for a in range(ni):
            copy(a, 2, (*nbr_b, c), me).wait_recv()
            go(copy(a, 5, (*nbr_b, c), sibling))
        for a in range(ni):
            copy(a, 3, (*diag, c), me).wait_recv()
            go(copy(a, 6, (*diag, c), sibling))
        for a in range(ni):
            copy(a, 0, sibling, me).wait_recv()
            copy(a, 4, (*nbr_b, 1 - c), me).wait_recv()
            copy(a, 5, (*nbr_a, 1 - c), me).wait_recv()
            copy(a, 6, (*diag, 1 - c), me).wait_recv()
        for cp in sent:
            cp.wait_send()
        for cp in mine:
            cp.wait()

    outs = _pcall(
        body, name=name,
        out_shape=[jax.ShapeDtypeStruct((N_DEV,) + a.shape, a.dtype) for a in xs],
        in_specs=[ANY] * n, out_specs=[ANY] * n,
        scratch_shapes=[pltpu.SemaphoreType.DMA((ni, 7)), pltpu.SemaphoreType.DMA((ni, 7)), pltpu.SemaphoreType.DMA((n,))],
    )(*xs)
    return list(outs)


def _pair_exchange(gs, name):
    n = len(gs)
    chips = [(0, 0), (0, 1), (1, 0), (1, 1)]

    def body(*refs):
        g_refs, o_refs = refs[:n], refs[n:2 * n]
        send_sems, recv_sems = refs[2 * n:]
        x, y, c = _mesh_pos()
        sibling = (x, y, 1 - c)

        def copy(a, i):
            xp, yp = chips[i]
            return pltpu.make_async_remote_copy(
                src_ref=g_refs[a].at[_flat(xp, yp, 1 - c)], dst_ref=o_refs[a].at[i],
                send_sem=send_sems.at[a, i], recv_sem=recv_sems.at[a, i], device_id=sibling, device_id_type=MESH)

        cps = [copy(a, i) for a in range(n) for i in range(4)]
        for cp in cps:
            cp.start()
        for cp in cps:
            cp.wait()

    outs = _pcall(
        body, name=name,
        out_shape=[jax.ShapeDtypeStruct((4,) + a.shape[1:], a.dtype) for a in gs],
        in_specs=[ANY] * n, out_specs=[ANY] * n,
        scratch_shapes=[pltpu.SemaphoreType.DMA((n, 4)), pltpu.SemaphoreType.DMA((n, 4))],
    )(*gs)
    return list(outs)


def _pair_sum(g, p1, name):
    _, R, C = g.shape
    tr = 256 if R % 256 == 0 else R
    cidx = lax.axis_index("c").astype(jnp.int32).reshape(1)

    def body(c_ref, g_ref, p_ref, o_ref):
        o_ref[...] = (g_ref[...].astype(f32) + p_ref[...].astype(f32)).astype(o_ref.dtype)

    return _pcall(
        body, name=name,
        grid_spec=pltpu.PrefetchScalarGridSpec(
            num_scalar_prefetch=1, grid=(4, R // tr),
            in_specs=[pl.BlockSpec((1, tr, C), lambda i, r, c_ref: (2 * i + c_ref[0], r, 0)),
                      pl.BlockSpec((1, tr, C), lambda i, r, c_ref: (i, r, 0))],
            out_specs=pl.BlockSpec((1, tr, C), lambda i, r, c_ref: (i, r, 0))),
        out_shape=jax.ShapeDtypeStruct((4, R, C), g.dtype),
        compiler_params=_cparams("parallel", "parallel"),
    )(cidx, g, p1)


def _axis_sum(s, got, name):
    _, R, C = s.shape
    x, y, c = _mesh_pos()
    me, _, b, _ = _axis_chips(x, y, c)
    idx = jnp.stack([2 * me[0] + me[1], 2 * b[0] + b[1]]).astype(jnp.int32)

    def body(idx_ref, s_ref, g_ref, o_ref):
        o_ref[...] = (s_ref[...].astype(f32) + g_ref[...].astype(f32)).astype(o_ref.dtype)

    return _pcall(
        body, name=name,
        grid_spec=pltpu.PrefetchScalarGridSpec(
            num_scalar_prefetch=1, grid=(2,),
            in_specs=[pl.BlockSpec((1, R, C), lambda k, idx_ref: (idx_ref[k], 0, 0)),
                      pl.BlockSpec((1, R, C), lambda k, idx_ref: (k, 0, 0))],
            out_specs=pl.BlockSpec((1, R, C), lambda k, idx_ref: (k, 0, 0))),
        out_shape=jax.ShapeDtypeStruct((2, R, C), s.dtype),
        compiler_params=_cparams("parallel"),
    )(idx, s, got)


HBM = pl.BlockSpec(memory_space=pltpu.HBM)
SEM = pl.BlockSpec(memory_space=pltpu.SEMAPHORE)
EFFECT = pltpu.SideEffectType.DATAFLOW_SIDE_EFFECTING


def _peers(x, y, c):
    out = []
    for k in range(1, N_DEV):
        kx, ky, kc = (k >> 2) & 1, (k >> 1) & 1, k & 1
        out.append(((1 - x) if kx else x, (1 - y) if ky else y, (1 - c) if kc else c))
    return out


SPREAD_COPIES = {"gather": N_DEV - 1, "scatter": N_DEV - 1, "axis_a": 2, "axis_b": 1}
SPREAD_SLOTS = {"axis_a": 2, "axis_b": 1}


def _axis_chips(x, y, c):
    flip = lambda v, bit: v + bit - 2 * v * bit
    return (x, y), (flip(x, 1 - c), flip(y, c)), (flip(x, c), flip(y, 1 - c)), (1 - x, 1 - y)


def _spread_copy(src_ref, land_ref, send_sems, recv_sems, k, plan):
    x, y, c = _mesh_pos()
    if plan in ("axis_a", "axis_b"):
        _, a, b, d = _axis_chips(x, y, c)
        chip = lambda p: 2 * p[0] + p[1]
        peer = (*(a if plan == "axis_a" else b), c)
        src = src_ref.at[chip(a) if k == 0 else chip(d)] if plan == "axis_a" else src_ref.at[1]
        slot = k
    else:
        peer = _peers(x, y, c)[k]
        src, slot = (src_ref.at[_flat(*peer)] if plan == "scatter" else src_ref), _flat(x, y, c)
    return pltpu.make_async_remote_copy(
        src_ref=src, dst_ref=land_ref.at[slot], send_sem=send_sems.at[k], recv_sem=recv_sems.at[k],
        device_id=peer, device_id_type=MESH)


def _own_copy(src_ref, land_ref, send_sems, plan):
    me = _flat(*_mesh_pos())
    return pltpu.make_async_copy(src_ref.at[me] if plan == "scatter" else src_ref, land_ref.at[me],
                                 send_sems.at[SPREAD_COPIES[plan]])


def _spread_start(src, after, plan, name):
    land_shape = (N_DEV,) + src.shape if plan == "gather" else src.shape
    if plan in SPREAD_SLOTS:
        land_shape = (SPREAD_SLOTS[plan],) + src.shape[1:]
    n_copies = SPREAD_COPIES[plan]

    def body(src_ref, land_ref, after_ref, send_sems, recv_sems, src_thru, land_thru, token):
        for k in range(n_copies):
            _spread_copy(src_ref, land_ref, send_sems, recv_sems, k, plan).start()
        if plan not in SPREAD_SLOTS:
            _own_copy(src_ref, land_ref, send_sems, plan).start()
        token[...] = jnp.zeros_like(token)

    return _pcall(
        body, name=name,
        out_shape=(pltpu.SemaphoreType.DMA((n_copies + (plan not in SPREAD_SLOTS),)), pltpu.SemaphoreType.DMA((n_copies,)),
                   pltpu.HBM(src.shape, src.dtype), pltpu.HBM(land_shape, src.dtype), jax.ShapeDtypeStruct((8, LANE), f32)),
        in_specs=(HBM, HBM, ANY), out_specs=(SEM, SEM, HBM, HBM, pl.BlockSpec(memory_space=pltpu.VMEM)),
        input_output_aliases={0: 2, 1: 3},
        compiler_params=pltpu.CompilerParams(has_side_effects=EFFECT),
    )(pltpu.with_memory_space_constraint(src, pltpu.HBM),
      pltpu.with_memory_space_constraint(lax.empty(land_shape, src.dtype), pltpu.HBM), after)


def _spread_wait(started, after, plan, name):
    send_sems, recv_sems, src_thru, land_thru, _ = started

    def body(src_ref, land_ref, send_sems, recv_sems, after_ref, src_dead, got_ref):
        for k in range(SPREAD_COPIES[plan]):
            cp = _spread_copy(src_ref, land_ref, send_sems, recv_sems, k, plan)
            cp.wait_send()
            cp.wait_recv()
        if plan not in SPREAD_SLOTS:
            _own_copy(src_ref, land_ref, send_sems, plan).wait()

    return _pcall(
        body, name=name,
        out_shape=(pltpu.HBM(src_thru.shape, src_thru.dtype), pltpu.HBM(land_thru.shape, land_thru.dtype)),
        in_specs=(HBM, HBM, SEM, SEM, ANY), out_specs=(HBM, HBM), input_output_aliases={0: 0, 1: 1},
        compiler_params=pltpu.CompilerParams(has_side_effects=EFFECT),
    )(src_thru, land_thru, send_sems, recv_sems, after)


COL_TILE = 256


def _cast_w_in(w3):
    n = w3.shape[0]

    def body(w_ref, o_ref):
        o_ref[...] = w_ref[:, 0, :].astype(bf16)

    return _pcall(
        body, name="cast_w_in", grid=(1,),
        in_specs=[pl.BlockSpec((n, 1, D_MODEL), lambda j: (0, 0, 0))],
        out_specs=pl.BlockSpec((n, D_MODEL), lambda j: (0, 0)),
        out_shape=jax.ShapeDtypeStruct((n, D_MODEL), bf16),
        compiler_params=_cparams("arbitrary"),
    )(w3)


def _relayout_w_in(win_g):
    def body(g_ref, o_ref):
        used = OFF_BA + NAT_CONV - NAT_BA
        o_ref[used:PROJ_PAD, :] = jnp.zeros((PROJ_PAD - used, COL_TILE), o_ref.dtype)
        for d in range(N_DEV):
            for lo, width, dst in _layout_segments(d * SHARD_W, (d + 1) * SHARD_W):
                src = lo - d * SHARD_W
                o_ref[dst:dst + width, :] = g_ref[d, src:src + width, :]

    return _pcall(
        body, name="relayout_w_in", grid=(D_MODEL // COL_TILE,),
        in_specs=[pl.BlockSpec((N_DEV, SHARD_W, COL_TILE), lambda j: (0, 0, j))],
        out_specs=pl.BlockSpec((PROJ_PAD, COL_TILE), lambda j: (0, j)),
        out_shape=jax.ShapeDtypeStruct((PROJ_PAD, D_MODEL), win_g.dtype),
        compiler_params=_cparams("parallel"),
    )(win_g)


def _grad_blocks(g_t):
    def body(p_ref, o_ref):
        for d in range(N_DEV):
            for lo, width, src in _layout_segments(d * SHARD_W, (d + 1) * SHARD_W):
                dst = lo - d * SHARD_W
                o_ref[d, dst:dst + width, :] = p_ref[src:src + width, :]

    return _pcall(
        body, name="grad_blocks", grid=(D_MODEL // COL_TILE,),
        in_specs=[pl.BlockSpec((PROJ_PAD, COL_TILE), lambda j: (0, j))],
        out_specs=pl.BlockSpec((N_DEV, SHARD_W, COL_TILE), lambda j: (0, 0, j)),
        out_shape=jax.ShapeDtypeStruct((N_DEV, SHARD_W, D_MODEL), bf16),
        compiler_params=_cparams("parallel"),
    )(g_t)


def _in_proj(x, nw, wpad_t, after):
    L = x.shape[0]
    tn = 768
    nj = wpad_t.shape[0] // tn

    def body(x_ref, nw_ref, w_ref, after_ref, proj_ref, h_ref):
        first = pl.program_id(0) == 0

        def project(r, n, hv):
            proj_ref[r:r + n, :] = lax.dot_general(hv, w_ref[...], (((1,), (1,)), ((), ())), preferred_element_type=f32)

        @pl.when(first)
        def _():
            for r in range(0, L, 256):
                xs = x_ref[r:r + 256, :]
                ms = jnp.mean(xs * xs, axis=-1, keepdims=True)
                hv = ((xs * lax.rsqrt(ms + EPS)) * nw_ref[...]).astype(bf16)
                h_ref[r:r + 256, :] = hv
                project(r, 256, hv)

        @pl.when(jnp.logical_not(first))
        def _():
            for r in range(0, L, 512):
                project(r, 512, h_ref[r:r + 512, :])

    return _pcall(
        body, name="in_proj", grid=(nj,),
        in_specs=[pl.BlockSpec((L, D_MODEL), lambda j: (0, 0)), pl.BlockSpec((1, D_MODEL), lambda j: (0, 0)),
                  pl.BlockSpec((tn, D_MODEL), lambda j: (j, 0)), ANY],
        out_specs=[pl.BlockSpec((L, tn), lambda j: (0, j)), pl.BlockSpec((L, D_MODEL), lambda j: (0, 0))],
        out_shape=[jax.ShapeDtypeStruct((L, wpad_t.shape[0]), f32), jax.ShapeDtypeStruct((L, D_MODEL), bf16)],
        compiler_params=_cparams("arbitrary"),
    )(x, nw, wpad_t, after)


HALVES = [slice(i * LANE, (i + 1) * LANE) for i in range(ELT_W // LANE)]
QKV_W = 512
QKV_HEADS = [slice(i * LANE, (i + 1) * LANE) for i in range(QKV_W // LANE)]
STEPS_PER_GROUP = GDN_WIDTH // QKV_W


def _conv4(x, cw_ref, ls):
    return (cw_ref[3:4, ls] * x + cw_ref[2:3, ls] * _shift_down(x, 1) + cw_ref[1:2, ls] * _shift_down(x, 2)
            + cw_ref[0:1, ls] * _shift_down(x, 3))


def _qkv_act(proj, cw):
    L = proj.shape[0]

    def body(x_ref, cw_ref, o_ref):
        j = pl.program_id(0)
        scale = jnp.where(j < STEPS_PER_GROUP, HEAD_DIM ** -0.5, 1.0).astype(f32)
        for ls in QKV_HEADS:
            c = _conv4(x_ref[:, ls], cw_ref, ls)
            a = c * _sigmoid(c)
            rn = lax.rsqrt(jnp.sum(a * a, axis=1, keepdims=True) + EPS)
            o_ref[:, ls] = jnp.where(j < 2 * STEPS_PER_GROUP, (a * rn) * scale, a)

    return _pcall(
        body, name="qkv_act", grid=(3 * STEPS_PER_GROUP,),
        in_specs=[pl.BlockSpec((L, QKV_W), lambda j: (0, j)), pl.BlockSpec((4, QKV_W), lambda j: (0, j))],
        out_specs=pl.BlockSpec((L, QKV_W), lambda j: (0, j)),
        out_shape=jax.ShapeDtypeStruct((L, 3 * GDN_WIDTH), f32),
        compiler_params=_cparams("parallel"),
    )(proj, cw)


def _scalars(proj, alog_p, dtb_p):
    L = proj.shape[0]
    nc = L // CHUNK

    def body(x_ref, al_ref, dt_ref, sc_ref, gr_ref):
        x = x_ref[...]
        lane = _lanes(x.shape)
        beta = _sigmoid(x)
        g = -jnp.exp(al_ref[...]) * _softplus(x + dt_ref[...])
        gc = jnp.where((lane >= HEADS) & (lane < 2 * HEADS), g, 0.0)
        rc = _rows(x.shape) & (CHUNK - 1)
        for s in (1, 2, 4, 8, 16, 32):
            gc = gc + jnp.where(rc >= s, pltpu.roll(gc, s, 0), 0.0)
        sc_ref[...] = jnp.where(lane < HEADS, beta, gc)
        sel = (_lanes((HEADS, LANE)) == _rows((HEADS, LANE)) + HEADS).astype(f32)
        for c in range(nc):
            gr_ref[c] = lax.dot_general(sel, sc_ref[c * CHUNK:(c + 1) * CHUNK, :], (((1,), (1,)), ((), ())),
                                        preferred_element_type=f32, precision=lax.Precision.HIGHEST)

    return _pcall(
        body, name="scalars", grid=(1,),
        in_specs=[pl.BlockSpec((L, LANE), lambda i: (0, OFF_BA // LANE)), pl.BlockSpec((1, LANE), lambda i: (0, 0)),
                  pl.BlockSpec((1, LANE), lambda i: (0, 0))],
        out_specs=[pl.BlockSpec((L, LANE), lambda i: (0, 0)), pl.BlockSpec((nc, HEADS, CHUNK), lambda i: (0, 0, 0))],
        out_shape=[jax.ShapeDtypeStruct((L, LANE), f32), jax.ShapeDtypeStruct((nc, HEADS, CHUNK), f32)],
        compiler_params=_cparams("arbitrary"),
    )(proj, alog_p, dtb_p)


def _head_scalars(sc, gr_ref, h, ci=0):
    lane = _lanes(sc.shape)
    beta = jnp.sum(jnp.where(lane == h, sc, 0.0), axis=1, keepdims=True)
    gcc = jnp.sum(jnp.where(lane == HEADS + h, sc, 0.0), axis=1, keepdims=True)
    gcr = gr_ref[ci, h:h + 1, :]
    gl = jnp.sum(jnp.where(_lanes(gcr.shape) == CHUNK - 1, gcr, 0.0), axis=1, keepdims=True)
    ii, jj = _rows((CHUNK, CHUNK)), _lanes((CHUNK, CHUNK))
    dmat = jnp.where(ii >= jj, jnp.exp(jnp.minimum(gcc - gcr, 0.0)), 0.0)
    dmat_t = jnp.where(jj >= ii, jnp.exp(jnp.minimum(gcr - gcc, 0.0)), 0.0)
    return beta, gcc, gl, dmat, dmat_t, ii, jj


def _gdn_fwd(qkv, sc, gr):
    L = qkv.shape[0]
    nc = L // CHUNK
    W = GDN_WIDTH
    cps = GDN_CPS if nc % GDN_CPS == 0 else 1
    rows_per_step = cps * CHUNK

    def body(qkv_ref, sc_ref, gr_ref, o_ref, u_ref, w_ref, vn_ref, t_ref, sp_ref, s_scr):
        @pl.when(pl.program_id(0) == 0)
        def _():
            s_scr[...] = jnp.zeros_like(s_scr)
        HS = range(cps * HEADS)
        hd = [i % HEADS for i in HS]
        rs = [slice((i // HEADS) * CHUNK, (i // HEADS + 1) * CHUNK) for i in HS]
        cs = [slice(hd[i] * HEAD_DIM, (hd[i] + 1) * HEAD_DIM) for i in HS]
        q = [qkv_ref[rs[i], hd[i] * HEAD_DIM:(hd[i] + 1) * HEAD_DIM] for i in HS]
        k = [qkv_ref[rs[i], W + hd[i] * HEAD_DIM:W + (hd[i] + 1) * HEAD_DIM] for i in HS]
        v = [qkv_ref[rs[i], 2 * W + hd[i] * HEAD_DIM:2 * W + (hd[i] + 1) * HEAD_DIM] for i in HS]
        hsc = [_head_scalars(sc_ref[rs[i], :], gr_ref, hd[i], i // HEADS) for i in HS]
        beta, gcc, gl, dmat = ([x[i] for x in hsc] for i in range(4))
        ii, jj = hsc[0][5], hsc[0][6]
        eg = [jnp.exp(gcc[h]) for h in HS]
        kb = [k[h] * beta[h] for h in HS]
        kk = [_mm_nt(kb[h], k[h]) for h in HS]
        qk = [_mm_nt(q[h], k[h]) for h in HS]
        n0 = [-jnp.where(ii > jj, kk[h] * dmat[h], 0.0) for h in HS]
        n1 = [_mm(n0[h], n0[h]) for h in HS]
        n2 = [_mm(n1[h], n1[h]) for h in HS]
        p01 = [n0[h] + n1[h] + _mm(n0[h], n1[h]) for h in HS]
        n3 = [_mm(n2[h], n2[h]) for h in HS]
        n4 = [_mm(n3[h], n3[h]) for h in HS]
        p23 = [n2[h] + n3[h] + _mm(n2[h], n3[h]) for h in HS]
        n5 = [_mm(n4[h], n4[h]) for h in HS]
        p03 = [p01[h] + p23[h] + _mm(p01[h], p23[h]) for h in HS]
        p45 = [n4[h] + n5[h] + _mm(n4[h], n5[h]) for h in HS]
        t = [p03[h] + p45[h] + _mm(p03[h], p45[h]) for h in HS]
        vb = [v[h] * beta[h] for h in HS]
        kbg = [kb[h] * eg[h] for h in HS]
        uw = [_mm(t[h], _cat16([vb[h], kbg[h]], 1)) for h in HS]
        u = [vb[h] + uw[h][:, :HEAD_DIM] for h in HS]
        w = [kbg[h] + uw[h][:, HEAD_DIM:] for h in HS]
        wq = [_cat16([w[h], q[h] * eg[h]], 0) for h in HS]
        p = [jnp.where(ii >= jj, qk[h] * dmat[h], 0.0) for h in HS]
        ks = [k[h] * jnp.exp(gl[h] - gcc[h]) for h in HS]
        s = [s_scr[h] for h in range(HEADS)]
        for ci in range(cps):
            IS = range(ci * HEADS, (ci + 1) * HEADS)
            ws = [_mm(wq[i], s[hd[i]]) for i in IS]
            vn = [u[i] - ws[hd[i]][:CHUNK] for i in IS]
            pv = [_mm(p[i], vn[hd[i]]) for i in IS]
            kv = [_mm_tn(ks[i], vn[hd[i]]) for i in IS]
            for i in IS:
                h = hd[i]
                sp_ref[ci, cs[i], :] = s[h]
                o_ref[rs[i], cs[i]] = ws[h][CHUNK:] + pv[h]
                vn_ref[rs[i], cs[i]] = vn[h].astype(bf16)
            s = [jnp.exp(gl[i]) * s[hd[i]] + kv[hd[i]] for i in IS]
        for h in range(HEADS):
            s_scr[h] = s[h]
        for i in HS:
            u_ref[rs[i], cs[i]] = u[i].astype(bf16)
            w_ref[rs[i], cs[i]] = w[i].astype(bf16)
            t_ref[i // HEADS, hd[i]] = t[i].astype(bf16)

    row = lambda c: (c, 0)
    act, act16 = jax.ShapeDtypeStruct((L, W), f32), jax.ShapeDtypeStruct((L, W), bf16)
    return _pcall(
        body, name="gdn_fwd", grid=(nc // cps,),
        in_specs=[pl.BlockSpec((rows_per_step, 3 * W), row), pl.BlockSpec((rows_per_step, LANE), row),
                  pl.BlockSpec((cps, HEADS, CHUNK), lambda c: (c, 0, 0))],
        out_specs=[pl.BlockSpec((rows_per_step, W), row)] * 4 + [
            pl.BlockSpec((cps, HEADS, CHUNK, CHUNK), lambda c: (c, 0, 0, 0)),
            pl.BlockSpec((cps, W, HEAD_DIM), lambda c: (c, 0, 0))],
        out_shape=[act, act16, act16, act16, jax.ShapeDtypeStruct((nc, HEADS, CHUNK, CHUNK), bf16),
                   jax.ShapeDtypeStruct((nc, W, HEAD_DIM), f32)],
        scratch_shapes=[pltpu.VMEM((HEADS, HEAD_DIM, HEAD_DIM), f32)],
        compiler_params=_cparams("arbitrary"),
    )(qkv, sc, gr)


def _gdn_gate(o, proj, gnw):
    L = o.shape[0]

    def body(o_ref, z_ref, w_ref, m_ref):
        for ls in HALVES:
            ov, z = o_ref[:, ls], z_ref[:, ls]
            rms = lax.rsqrt(jnp.mean(ov * ov, axis=-1, keepdims=True) + EPS)
            m_ref[:, ls] = (((ov * rms) * w_ref[...]) * (z * _sigmoid(z))).astype(bf16)

    return _pcall(
        body, name="gdn_gate", grid=(GDN_WIDTH // ELT_W,),
        in_specs=[pl.BlockSpec((L, ELT_W), lambda j: (0, j)), pl.BlockSpec((L, ELT_W), lambda j: (0, OFF_ZG // ELT_W + j)),
                  pl.BlockSpec((1, LANE), lambda j: (0, 0))],
        out_specs=pl.BlockSpec((L, ELT_W), lambda j: (0, j)),
        out_shape=jax.ShapeDtypeStruct((L, GDN_WIDTH + CONV_WIDTH), bf16),
        compiler_params=_cparams("parallel"),
    )(o, proj, gnw)


def _conv3(u, cw_ref, ls):
    return cw_ref[2:3, ls] * u + cw_ref[1:2, ls] * _shift_down(u, 1) + cw_ref[0:1, ls] * _shift_down(u, 2)


def _conv_specs(L):
    return [pl.BlockSpec((L, CONV_BLOCK), lambda j: (0, OFF_CONV // CONV_BLOCK + j)),
            pl.BlockSpec((3, ELT_W), lambda j: (0, j)), pl.BlockSpec((1, ELT_W), lambda j: (0, j))]


def _conv_parts(ls):
    return [slice(g * ELT_W + ls.start, g * ELT_W + ls.stop) for g in range(4)]


def _conv_fwd(proj, cw, cb, mix):
    L = proj.shape[0]

    def body(p_ref, cw_ref, cb_ref, mix_in, m_ref):
        for ls in HALVES:
            sb, sc_, sh, sz = _conv_parts(ls)
            z = p_ref[:, sz]
            cv = _conv3(p_ref[:, sc_] * p_ref[:, sh], cw_ref, ls) + cb_ref[:, ls]
            m_ref[:, ls] = ((p_ref[:, sb] * cv) * (z * _sigmoid(z))).astype(bf16)

    return _pcall(
        body, name="conv_fwd", grid=(CONV_WIDTH // ELT_W,),
        in_specs=_conv_specs(L) + [ANY], out_specs=pl.BlockSpec((L, ELT_W), lambda j: (0, GDN_WIDTH // ELT_W + j)),
        out_shape=jax.ShapeDtypeStruct(mix.shape, mix.dtype), input_output_aliases={3: 0},
        compiler_params=_cparams("parallel"),
    )(proj, cw, cb, mix)


def _out_proj_loss(x, mix, wo, fw, tgt):
    L = x.shape[0]
    tm = min(512, L)
    MW = GDN_WIDTH + CONV_WIDTH

    def body(x_ref, m_ref, wo_ref, fw_ref, t_ref, dy_ref, dyb_ref, dm_ref, gfw_ref, loss_ref):
        @pl.when(pl.program_id(0) == 0)
        def _():
            gfw_ref[...] = jnp.zeros_like(gfw_ref)
            loss_ref[...] = jnp.zeros_like(loss_ref)
        y = x_ref[...] + jnp.dot(m_ref[...], wo_ref[...], preferred_element_type=f32)
        r = lax.rsqrt(jnp.mean(y * y, axis=-1, keepdims=True) + EPS)
        yh = y * r
        fwv = fw_ref[...]
        diff = yh * fwv - t_ref[...]
        loss_ref[...] += jnp.sum(jnp.sum(diff * diff, axis=-1, keepdims=True), axis=0, keepdims=True) * (0.5 / D_MODEL)
        dout = diff * (1.0 / D_MODEL)
        gfw_ref[...] += jnp.sum(dout * yh, axis=0, keepdims=True)
        dyh = dout * fwv
        dy = r * (dyh - yh * jnp.mean(dyh * yh, axis=-1, keepdims=True))
        dy_ref[...] = dy
        dyb = dy.astype(bf16)
        dyb_ref[...] = dyb
        dm_ref[...] = lax.dot_general(dyb, wo_ref[...], (((1,), (1,)), ((), ())), preferred_element_type=f32)

    row = lambda i: (i, 0)
    fix = lambda i: (0, 0)
    act = jax.ShapeDtypeStruct((L, D_MODEL), f32)
    return _pcall(
        body, name="out_proj_loss", grid=(L // tm,),
        in_specs=[pl.BlockSpec((tm, D_MODEL), row), pl.BlockSpec((tm, MW), row), pl.BlockSpec((MW, D_MODEL), fix),
                  pl.BlockSpec((1, D_MODEL), fix), pl.BlockSpec((tm, D_MODEL), row)],
        out_specs=[pl.BlockSpec((tm, D_MODEL), row), pl.BlockSpec((tm, D_MODEL), row), pl.BlockSpec((tm, MW), row),
                   pl.BlockSpec((1, D_MODEL), fix), pl.BlockSpec((1, LANE), fix)],
        out_shape=[act, jax.ShapeDtypeStruct((L, D_MODEL), bf16), jax.ShapeDtypeStruct((L, MW), f32),
                   jax.ShapeDtypeStruct((1, D_MODEL), f32), jax.ShapeDtypeStruct((1, LANE), f32)],
        compiler_params=_cparams("arbitrary"),
    )(x, mix, wo, fw, tgt)


def _tn_matmul(a, b, name):
    L, M = a.shape
    N = b.shape[1]
    tm = 512 if M % 512 == 0 else (768 if M % 768 == 0 else M)

    def body(a_ref, b_ref, o_ref):
        o_ref[...] = lax.dot_general(a_ref[...], b_ref[...], (((0,), (0,)), ((), ())),
                                     preferred_element_type=f32).astype(o_ref.dtype)

    return _pcall(
        body, name=name, grid=(M // tm,),
        in_specs=[pl.BlockSpec((L, tm), lambda i: (0, i)), pl.BlockSpec((L, N), lambda i: (0, 0))],
        out_specs=pl.BlockSpec((tm, N), lambda i: (i, 0)),
        out_shape=jax.ShapeDtypeStruct((M, N), bf16),
        compiler_params=_cparams("parallel"),
    )(a, b)


def _gdn_gate_bwd(o, proj, gnw, dmix_a, after):
    L = o.shape[0]

    def body(o_ref, z_ref, w_ref, dm_ref, after_ref, do_ref, dz_ref, gw_ref):
        @pl.when(pl.program_id(0) == 0)
        def _():
            gw_ref[...] = jnp.zeros_like(gw_ref)
        wv = w_ref[...]
        for ls in HALVES:
            ov, z, dm = o_ref[:, ls], z_ref[:, ls], dm_ref[:, ls]
            rms = lax.rsqrt(jnp.mean(ov * ov, axis=-1, keepdims=True) + EPS)
            xh = ov * rms
            sg = _sigmoid(z)
            d_on = dm * (z * sg)
            dz_ref[:, ls] = (dm * (xh * wv) * (sg * (1.0 + z * (1.0 - sg)))).astype(bf16)
            gw_ref[...] += jnp.sum(d_on * xh, axis=0, keepdims=True)
            dxh = d_on * wv
            do_ref[:, ls] = (rms * (dxh - xh * jnp.mean(dxh * xh, axis=-1, keepdims=True))).astype(bf16)

    wide = pl.BlockSpec((L, ELT_W), lambda j: (0, j))
    return _pcall(
        body, name="gdn_gate_bwd", grid=(GDN_WIDTH // ELT_W,),
        in_specs=[wide, pl.BlockSpec((L, ELT_W), lambda j: (0, OFF_ZG // ELT_W + j)),
                  pl.BlockSpec((1, LANE), lambda j: (0, 0)), wide, ANY],
        out_specs=[wide, pl.BlockSpec((L, ELT_W), lambda j: (0, OFF_ZG // ELT_W + j)),
                   pl.BlockSpec((1, LANE), lambda j: (0, 0))],
        out_shape=[jax.ShapeDtypeStruct((L, GDN_WIDTH), bf16), jax.ShapeDtypeStruct((L, PROJ_PAD), bf16),
                   jax.ShapeDtypeStruct((1, LANE), f32)],
        compiler_params=_cparams("arbitrary"),
    )(o, proj, gnw, dmix_a, after)


def _conv_bwd(proj, cw, cb, dmix_b, dproj):
    L = proj.shape[0]

    def body(p_ref, cw_ref, cb_ref, dm_ref, dproj_in, dp_ref, gcw_ref, gcb_ref):
        for ls in HALVES:
            sb, sc_, sh, sz_ = _conv_parts(ls)
            bv, cv_, hv, z, dm = p_ref[:, sb], p_ref[:, sc_], p_ref[:, sh], p_ref[:, sz_], dm_ref[:, ls]
            u = cv_ * hv
            cv = _conv3(u, cw_ref, ls) + cb_ref[:, ls]
            sg = _sigmoid(z)
            sz = z * sg
            dp_ref[:, sb] = (dm * cv * sz).astype(bf16)
            dp_ref[:, sz_] = (dm * (bv * cv) * (sg * (1.0 + z * (1.0 - sg)))).astype(bf16)
            dcv = dm * bv * sz
            gcb_ref[:, ls] = jnp.sum(dcv, axis=0, keepdims=True)
            dcv1, dcv2 = _shift_up(dcv, 1), _shift_up(dcv, 2)
            gcw_ref[2:3, ls] = jnp.sum(dcv * u, axis=0, keepdims=True)
            gcw_ref[1:2, ls] = jnp.sum(dcv1 * u, axis=0, keepdims=True)
            gcw_ref[0:1, ls] = jnp.sum(dcv2 * u, axis=0, keepdims=True)
            du = cw_ref[2:3, ls] * dcv + cw_ref[1:2, ls] * dcv1 + cw_ref[0:1, ls] * dcv2
            dp_ref[:, sc_] = (du * hv).astype(bf16)
            dp_ref[:, sh] = (du * cv_).astype(bf16)

    return _pcall(
        body, name="conv_bwd", grid=(CONV_WIDTH // ELT_W,),
        in_specs=_conv_specs(L) + [pl.BlockSpec((L, ELT_W), lambda j: (0, GDN_WIDTH // ELT_W + j)), ANY],
        out_specs=[pl.BlockSpec((L, CONV_BLOCK), lambda j: (0, OFF_CONV // CONV_BLOCK + j)),
                   pl.BlockSpec((3, ELT_W), lambda j: (0, j)), pl.BlockSpec((1, ELT_W), lambda j: (0, j))],
        out_shape=[jax.ShapeDtypeStruct(dproj.shape, dproj.dtype), jax.ShapeDtypeStruct((3, CONV_WIDTH), f32),
                   jax.ShapeDtypeStruct((1, CONV_WIDTH), f32)],
        input_output_aliases={4: 0},
        compiler_params=_cparams("parallel"),
    )(proj, cw, cb, dmix_b, dproj)


def _gdn_bwd(qkv, sc, gr, u_all, w_all, vn_all, t_all, sp_all, do_all):
    L = qkv.shape[0]
    nc = L // CHUNK
    W = GDN_WIDTH
    cps = GDN_CPS_BWD if nc % GDN_CPS_BWD == 0 else 1
    rows_per_step = cps * CHUNK
    nsteps = nc // cps

    def body(qkv_ref, sc_ref, gr_ref, u_ref, w_ref, vn_ref, t_ref, sp_ref, do_ref, dqkv_ref, dsc_ref, dgr_ref, ds_scr):
        @pl.when(pl.program_id(0) == 0)
        def _():
            ds_scr[...] = jnp.zeros_like(ds_scr)
        nh, base = HEADS, 0
        HS = range(cps * nh)
        hl = [i % nh for i in HS]
        hd = [base + hl[i] for i in HS]
        rs = [slice((i // nh) * CHUNK, (i // nh + 1) * CHUNK) for i in HS]
        cs = [slice(hd[i] * HEAD_DIM, (hd[i] + 1) * HEAD_DIM) for i in HS]
        q = [qkv_ref[rs[i], hd[i] * HEAD_DIM:(hd[i] + 1) * HEAD_DIM] for i in HS]
        k = [qkv_ref[rs[i], W + hd[i] * HEAD_DIM:W + (hd[i] + 1) * HEAD_DIM] for i in HS]
        v = [qkv_ref[rs[i], 2 * W + hd[i] * HEAD_DIM:2 * W + (hd[i] + 1) * HEAD_DIM] for i in HS]
        hsc = [_head_scalars(sc_ref[rs[i], :], gr_ref, hd[i], i // nh) for i in HS]
        beta, gcc, gl, dmat, dmat_t = ([x[i] for x in hsc] for i in range(5))
        ii, jj = hsc[0][5], hsc[0][6]
        eg = [jnp.exp(gcc[h]) for h in HS]
        ekl = [jnp.exp(gl[h] - gcc[h]) for h in HS]
        egl = [jnp.exp(gl[h]) for h in HS]
        kb = [k[h] * beta[h] for h in HS]
        ks = [k[h] * ekl[h] for h in HS]
        do = [do_ref[rs[h], cs[h]] for h in HS]
        vn = [vn_ref[rs[h], cs[h]] for h in HS]
        s = [sp_ref[h // nh, cs[h], :] for h in HS]
        w = [w_ref[rs[h], cs[h]] for h in HS]
        qd = [q[h] * eg[h] for h in HS]

        kq = [_mm_nt(k[h], q[h]) for h in HS]
        p_t = [jnp.where(jj >= ii, kq[h] * dmat_t[h], 0.0) for h in HS]
        ptd = [_mm(p_t[h], do[h]) for h in HS]
        qw = [_cat16([qd[h], -w[h]], 0) for h in HS]
        dsn, dvn, dodv = [None] * len(HS), [None] * len(HS), [None] * len(HS)
        ds_cur = [ds_scr[base + h] for h in range(nh)]
        for ci in reversed(range(cps)):
            IS = range(ci * nh, (ci + 1) * nh)
            ksd = [_mm(ks[i], ds_cur[hl[i]]) for i in IS]
            for i in IS:
                dsn[i] = ds_cur[hl[i]]
                dvn[i] = ptd[i] + ksd[hl[i]]
                dodv[i] = _cat16([do[i], dvn[i]], 0)
            dsq = [_mm_tn(qw[i], dodv[i]) for i in IS]
            ds_cur = [egl[i] * ds_cur[hl[i]] + dsq[hl[i]] for i in IS]
        for h in range(nh):
            ds_scr[base + h] = ds_cur[h]
        x1 = [_mm_nt(dodv[h], s[h]) for h in HS]
        dks = [_mm_nt(vn[h], dsn[h]) for h in HS]
        dov = [_mm_nt(do[h], vn[h]) for h in HS]
        vdo = [_mm_nt(vn[h], do[h]) for h in HS]
        kk = [_mm_nt(kb[h], k[h]) for h in HS]
        qk = [_mm_nt(q[h], k[h]) for h in HS]
        dgl = [egl[h] * jnp.sum(jnp.sum(s[h] * dsn[h], axis=1, keepdims=True), axis=0, keepdims=True) for h in HS]
        dqd = [x1[h][:CHUNK] for h in HS]
        duw = [jnp.concatenate([dvn[h], -x1[h][CHUNK:]], axis=1) for h in HS]
        tdu = [_mm_tn(t_ref[h // nh, hd[h]], duw[h]) for h in HS]
        dvk = [duw[h] + tdu[h] for h in HS]
        uw = [jnp.concatenate([u_ref[rs[h], cs[h]], w[h]], axis=1) for h in HS]
        da = [-jnp.where(ii > jj, _mm_nt(dvk[h], uw[h]), 0.0) for h in HS]
        da_t = [-jnp.where(jj > ii, _mm_nt(uw[h], dvk[h]), 0.0) for h in HS]
        dp = [jnp.where(ii >= jj, dov[h], 0.0) for h in HS]
        dp_t = [jnp.where(jj >= ii, vdo[h], 0.0) for h in HS]
        r1 = [_mm(_cat16([da[h] * dmat[h], dp[h] * dmat[h]], 0), k[h]) for h in HS]
        dk1 = [_mm(_cat16([da_t[h] * dmat_t[h], dp_t[h] * dmat_t[h]], 1), _cat16([kb[h], q[h]], 0)) for h in HS]
        lane = _lanes((CHUNK, LANE))
        for ci in range(cps):
            dsc = jnp.zeros((CHUNK, LANE), f32)
            for i in range(ci * nh, (ci + 1) * nh):
                h = hd[i]
                a = jnp.where(ii > jj, kk[i] * dmat[i], 0.0)
                p = jnp.where(ii >= jj, qk[i] * dmat[i], 0.0)
                gmat = da[i] * a + dp[i] * p
                dvb, dkbg = dvk[i][:, :HEAD_DIM], dvk[i][:, HEAD_DIM:]
                kbg = kb[i] * eg[i]
                dkb = r1[i][:CHUNK] + dkbg * eg[i]
                dq = r1[i][CHUNK:] + dqd[i] * eg[i]
                dk = dk1[i] + dks[i] * ekl[i] + dkb * beta[i]
                dbeta = jnp.sum(dkb * k[i] + dvb * v[i], axis=1, keepdims=True)
                ksum = jnp.sum(dks[i] * ks[i], axis=1, keepdims=True)
                dgl_tot = dgl[i] + jnp.sum(ksum, axis=0, keepdims=True)
                dgc = (jnp.sum(gmat, axis=1, keepdims=True) + jnp.sum(dqd[i] * qd[i] + dkbg * kbg, axis=1, keepdims=True)
                       - ksum)
                dgc = dgc + jnp.where(_rows(dgc.shape) == CHUNK - 1, dgl_tot, 0.0)
                dqkv_ref[rs[i], h * HEAD_DIM:(h + 1) * HEAD_DIM] = dq
                dqkv_ref[rs[i], W + h * HEAD_DIM:W + (h + 1) * HEAD_DIM] = dk
                dqkv_ref[rs[i], 2 * W + h * HEAD_DIM:2 * W + (h + 1) * HEAD_DIM] = dvb * beta[i]
                dsc = jnp.where(lane == h, dbeta, jnp.where(lane == HEADS + h, dgc, dsc))
                dgr_ref[ci, h:h + 1, :] = jnp.sum(gmat, axis=0, keepdims=True)
            dsc_ref[ci * CHUNK:(ci + 1) * CHUNK, :] = dsc

    row = lambda c: (nsteps - 1 - c, 0)
    lead3 = lambda c: (nsteps - 1 - c, 0, 0)
    return _pcall(
        body, name="gdn_bwd", grid=(nsteps,),
        in_specs=[pl.BlockSpec((rows_per_step, 3 * W), row), pl.BlockSpec((rows_per_step, LANE), row),
                  pl.BlockSpec((cps, HEADS, CHUNK), lead3),
                  pl.BlockSpec((rows_per_step, W), row), pl.BlockSpec((rows_per_step, W), row),
                  pl.BlockSpec((rows_per_step, W), row),
                  pl.BlockSpec((cps, HEADS, CHUNK, CHUNK), lambda c: (nsteps - 1 - c, 0, 0, 0)),
                  pl.BlockSpec((cps, W, HEAD_DIM), lead3), pl.BlockSpec((rows_per_step, W), row)],
        out_specs=[pl.BlockSpec((rows_per_step, 3 * W), row), pl.BlockSpec((rows_per_step, LANE), row),
                   pl.BlockSpec((cps, HEADS, CHUNK), lead3)],
        out_shape=[jax.ShapeDtypeStruct((L, 3 * W), f32), jax.ShapeDtypeStruct((L, LANE), f32),
                   jax.ShapeDtypeStruct((nc, HEADS, CHUNK), f32)],
        scratch_shapes=[pltpu.VMEM((HEADS, HEAD_DIM, HEAD_DIM), f32)],
        compiler_params=_cparams("arbitrary"),
    )(qkv, sc, gr, u_all, w_all, vn_all, t_all, sp_all, do_all)


def _qkv_bwd(proj, cw, dn, dproj):
    L = proj.shape[0]

    def body(x_ref, cw_ref, dn_ref, dproj_in, dx_ref, gcw_ref):
        j = pl.program_id(0)
        steps = GDN_WIDTH // ELT_W
        scale = jnp.where(j < steps, HEAD_DIM ** -0.5, 1.0).astype(f32)
        for ls in HALVES:
            x, dn_v = x_ref[:, ls], dn_ref[:, ls]
            c = _conv4(x, cw_ref, ls)
            sg = _sigmoid(c)
            a = c * sg
            rn = lax.rsqrt(jnp.sum(a * a, axis=1, keepdims=True) + EPS)
            da_n = (scale * rn) * (dn_v - a * ((rn * rn) * jnp.sum(dn_v * a, axis=1, keepdims=True)))
            da = jnp.where(j < 2 * steps, da_n, dn_v)
            dc = da * (sg * (1.0 + c * (1.0 - sg)))
            dc1, dc2, dc3 = _shift_up(dc, 1), _shift_up(dc, 2), _shift_up(dc, 3)
            gcw_ref[3:4, ls] = jnp.sum(dc * x, axis=0, keepdims=True)
            gcw_ref[2:3, ls] = jnp.sum(dc1 * x, axis=0, keepdims=True)
            gcw_ref[1:2, ls] = jnp.sum(dc2 * x, axis=0, keepdims=True)
            gcw_ref[0:1, ls] = jnp.sum(dc3 * x, axis=0, keepdims=True)
            dx = cw_ref[3:4, ls] * dc + cw_ref[2:3, ls] * dc1 + cw_ref[1:2, ls] * dc2 + cw_ref[0:1, ls] * dc3
            dx_ref[:, ls] = dx.astype(bf16)

    col = pl.BlockSpec((L, ELT_W), lambda j: (0, j))
    wspec = pl.BlockSpec((4, ELT_W), lambda j: (0, j))
    return _pcall(
        body, name="qkv_bwd", grid=(3 * GDN_WIDTH // ELT_W,),
        in_specs=[col, wspec, col, ANY], out_specs=[col, wspec],
        out_shape=[jax.ShapeDtypeStruct(dproj.shape, dproj.dtype), jax.ShapeDtypeStruct((4, 3 * GDN_WIDTH), f32)],
        input_output_aliases={3: 0},
        compiler_params=_cparams("parallel"),
    )(proj, cw, dn, dproj)


def _scalars_bwd(proj, alog_p, dtb_p, dsc, dgr_col, dproj, after):
    L = proj.shape[0]

    def body(x_ref, al_ref, dt_ref, dsc_ref, dgr_ref, dproj_in, after_ref, dba_ref, gs_ref):
        x, dsc_v = x_ref[...], dsc_ref[...]
        lane = _lanes(x.shape)
        dec = (lane >= HEADS) & (lane < 2 * HEADS)
        dg = jnp.where(dec, dsc_v - dgr_ref[...], 0.0)
        rc = _rows(x.shape) & (CHUNK - 1)
        for s in (1, 2, 4, 8, 16, 32):
            dg = dg + jnp.where(rc + s < CHUNK, pltpu.roll(dg, L - s, 0), 0.0)
        xa = x + dt_ref[...]
        ea = jnp.exp(al_ref[...])
        g = -ea * _softplus(xa)
        da = dg * (-ea) * _sigmoid(xa)
        beta = _sigmoid(x)
        db = dsc_v * beta * (1.0 - beta)
        dba_ref[:, :LANE] = jnp.where(lane < HEADS, db, jnp.where(dec, da, 0.0)).astype(bf16)
        dba_ref[:, LANE:] = jnp.zeros((L, ELT_W - LANE), bf16)
        g_al = jnp.sum(jnp.where(dec, dg * g, 0.0), axis=0, keepdims=True)
        g_dt = jnp.sum(jnp.where(dec, da, 0.0), axis=0, keepdims=True)
        row8 = _rows(gs_ref.shape)
        gs = jnp.where(row8 == 0, g_al, jnp.where(row8 == 1, g_dt, 0.0))
        gs_ref[...] = pltpu.roll(gs, LANE - HEADS, 1)

    full = pl.BlockSpec((L, LANE), lambda i: (0, 0))
    vec = pl.BlockSpec((1, LANE), lambda i: (0, 0))
    return _pcall(
        body, name="scalars_bwd", grid=(1,),
        in_specs=[pl.BlockSpec((L, LANE), lambda i: (0, OFF_BA // LANE)), vec, vec, full, full, ANY, ANY],
        out_specs=[pl.BlockSpec((L, ELT_W), lambda i: (0, OFF_BA // ELT_W)), pl.BlockSpec((8, LANE), lambda i: (0, 0))],
        out_shape=[jax.ShapeDtypeStruct(dproj.shape, dproj.dtype), jax.ShapeDtypeStruct((8, LANE), f32)],
        input_output_aliases={5: 0},
        compiler_params=_cparams("arbitrary"),
    )(proj, alog_p, dtb_p, dsc, dgr_col, dproj, after)


def _input_grad(dproj, wpad, x, nw, dy, after):
    L = x.shape[0]
    tm = min(512, L)
    cuts = (0, 1024, 3072, 5120, 7168, PROJ_PAD)
    nk = len(cuts) - 1

    def body(dp_ref, w_hbm, x_ref, nw_ref, dy_ref, after_ref, gx_ref, gnw_ref, w_vmem, sems):
        first = pl.program_id(0) == 0
        loads = [pltpu.make_async_copy(w_hbm.at[cuts[k]:cuts[k + 1], :], w_vmem.at[cuts[k]:cuts[k + 1], :], sems.at[k])
                 for k in range(nk)]

        @pl.when(first)
        def _():
            for cp in loads:
                cp.start()
            gnw_ref[...] = jnp.zeros_like(gnw_ref)
        dh = None
        for k in range(nk):
            pl.when(first)(loads[k].wait)
            part = jnp.dot(dp_ref[:, cuts[k]:cuts[k + 1]], w_vmem[cuts[k]:cuts[k + 1], :], preferred_element_type=f32)
            dh = part if dh is None else dh + part
        xv, nwv = x_ref[...], nw_ref[...]
        r = lax.rsqrt(jnp.mean(xv * xv, axis=-1, keepdims=True) + EPS)
        xh = xv * r
        gnw_ref[...] += jnp.sum(dh * xh, axis=0, keepdims=True)
        dxh = dh * nwv
        gx_ref[...] = dy_ref[...] + r * (dxh - xh * jnp.mean(dxh * xh, axis=-1, keepdims=True))

    row = lambda i: (i, 0)
    fix = lambda i: (0, 0)
    return _pcall(
        body, name="input_grad", grid=(L // tm,),
        in_specs=[pl.BlockSpec((tm, PROJ_PAD), row), ANY, pl.BlockSpec((tm, D_MODEL), row),
                  pl.BlockSpec((1, D_MODEL), fix), pl.BlockSpec((tm, D_MODEL), row), ANY],
        out_specs=[pl.BlockSpec((tm, D_MODEL), row), pl.BlockSpec((1, D_MODEL), fix)],
        out_shape=[jax.ShapeDtypeStruct((L, D_MODEL), f32), jax.ShapeDtypeStruct((1, D_MODEL), f32)],
        scratch_shapes=[pltpu.VMEM(wpad.shape, bf16), pltpu.SemaphoreType.DMA((nk,))],
        compiler_params=_cparams("arbitrary"),
    )(dproj, wpad, x, nw, dy, after)


def _adamw_reduce(parts, w, m, v, name):
    R, C = w.shape
    n_parts = parts.shape[0]
    tr = 128 if R % 128 == 0 else R
    c1 = 1.0 - ADAM_B1 ** ADAM_STEP
    c2 = 1.0 - ADAM_B2 ** ADAM_STEP

    def body(p_ref, w_ref, m_ref, v_ref, g_ref, d_ref, nm_ref, nv_ref):
        g = p_ref[0].astype(f32)
        for s in range(1, n_parts):
            g = g + p_ref[s].astype(f32)
        nm = ADAM_B1 * m_ref[...] + (1.0 - ADAM_B1) * g
        nv = ADAM_B2 * v_ref[...] + (1.0 - ADAM_B2) * (g * g)
        g_ref[...] = g
        nm_ref[...] = nm
        nv_ref[...] = nv
        d_ref[...] = -ADAM_LR * ((nm / c1) / (jnp.sqrt(nv / c2) + ADAM_EPS) + ADAM_WD * w_ref[...])

    blk = pl.BlockSpec((tr, C), lambda i: (i, 0))
    out = jax.ShapeDtypeStruct((R, C), f32)
    return _pcall(
        body, name=name, grid=(R // tr,),
        in_specs=[pl.BlockSpec((n_parts, tr, C), lambda i: (0, i, 0)), blk, blk, blk],
        out_specs=[blk] * 4, out_shape=[out] * 4,
        compiler_params=_cparams("parallel"),
    )(parts, w, m, v)


SMALL_SLOTS = ((0, D_MODEL), (D_MODEL, D_MODEL), (2 * D_MODEL, D_MODEL), (3 * D_MODEL, LANE),
               (3 * D_MODEL + LANE, HEADS), (3 * D_MODEL + 2 * LANE, HEADS))
SMALL_LOSS = 3 * D_MODEL + 3 * LANE
SMALL_W = SMALL_LOSS + LANE


def _pack_small(gs, after):
    def body(nw_ref, cb_ref, fw_ref, gn_ref, sc_ref, ls_ref, after_ref, o_ref):
        for ref, (start, width) in zip((nw_ref, cb_ref, fw_ref, gn_ref), SMALL_SLOTS[:4]):
            o_ref[:, start:start + width] = ref[...]
        o_ref[:, SMALL_SLOTS[4][0]:SMALL_SLOTS[4][0] + LANE] = sc_ref[0:1, :]
        o_ref[:, SMALL_SLOTS[5][0]:SMALL_SLOTS[5][0] + LANE] = sc_ref[1:2, :]
        o_ref[:, SMALL_LOSS:SMALL_W] = ls_ref[...]

    vm = pl.BlockSpec(memory_space=pltpu.VMEM)
    return _pcall(body, name="pack_small_grads", out_shape=jax.ShapeDtypeStruct((1, SMALL_W), f32),
                  in_specs=[vm] * 6 + [ANY], out_specs=vm)(*gs, after)


def _adamw_small(parts, ws, ms, vs):
    c1 = 1.0 - ADAM_B1 ** ADAM_STEP
    c2 = 1.0 - ADAM_B2 ** ADAM_STEP
    np_ = len(ws)

    def body(*refs):
        p_ref = refs[0]
        w_refs, m_refs, v_refs = refs[1:1 + np_], refs[1 + np_:1 + 2 * np_], refs[1 + 2 * np_:1 + 3 * np_]
        outs = refs[1 + 3 * np_:]
        g_refs, d_refs, nm_refs, nv_refs = (outs[i * np_:(i + 1) * np_] for i in range(4))
        loss_ref = outs[4 * np_]

        def total(start, width):
            t = p_ref[0, :, start:start + width]
            for s in range(1, N_DEV):
                t = t + p_ref[s, :, start:start + width]
            return t

        for i, (start, width) in enumerate(SMALL_SLOTS):
            g = total(start, width)
            nm = ADAM_B1 * m_refs[i][...] + (1.0 - ADAM_B1) * g
            nv = ADAM_B2 * v_refs[i][...] + (1.0 - ADAM_B2) * (g * g)
            g_refs[i][...] = g
            nm_refs[i][...] = nm
            nv_refs[i][...] = nv
            d_refs[i][...] = -ADAM_LR * ((nm / c1) / (jnp.sqrt(nv / c2) + ADAM_EPS) + ADAM_WD * w_refs[i][...])
        loss_ref[...] = total(SMALL_LOSS, LANE)

    vm = pl.BlockSpec(memory_space=pltpu.VMEM)
    shapes = [jax.ShapeDtypeStruct(w.shape, f32) for w in ws]
    res = _pcall(body, name="adamw_small", out_shape=shapes * 4 + [jax.ShapeDtypeStruct((1, LANE), f32)],
                 in_specs=[vm] * (1 + 3 * np_), out_specs=[vm] * (4 * np_ + 1))(parts, *ws, *ms, *vs)
    return [res[i * np_:(i + 1) * np_] for i in range(4)], res[4 * np_]


def _adamw_w_in(part_a, part_b, w3, m3, v3, after):
    _, n, _ = part_a.shape
    c1 = 1.0 - ADAM_B1 ** ADAM_STEP
    c2 = 1.0 - ADAM_B2 ** ADAM_STEP
    rows = 256
    main = n // rows * rows
    tail = n - main

    def step(g, w_ref, m_ref, v_ref, g_ref, d_ref, nm_ref, nv_ref):
        nm = ADAM_B1 * m_ref[:, 0, :] + (1.0 - ADAM_B1) * g
        nv = ADAM_B2 * v_ref[:, 0, :] + (1.0 - ADAM_B2) * (g * g)
        g_ref[:, 0, :] = g
        nm_ref[:, 0, :] = nm
        nv_ref[:, 0, :] = nv
        d_ref[:, 0, :] = -ADAM_LR * ((nm / c1) / (jnp.sqrt(nv / c2) + ADAM_EPS) + ADAM_WD * w_ref[:, 0, :])

    def body(pa_ref, pb_ref, w_ref, m_ref, v_ref, after_ref, *outs):
        step(pa_ref[0].astype(f32) + pb_ref[0].astype(f32), w_ref, m_ref, v_ref, *outs)

    def tail_body(pa_ref, pb_ref, w_ref, m_ref, v_ref, g_in, d_in, nm_in, nv_in, *outs):
        step(pa_ref[...].astype(f32) + pb_ref[...].astype(f32), w_ref, m_ref, v_ref, *outs)

    out = jax.ShapeDtypeStruct((n, 1, D_MODEL), f32)
    blk = pl.BlockSpec((rows, 1, D_MODEL), lambda i: (i, 0, 0))
    part = pl.BlockSpec((1, rows, D_MODEL), lambda i: (0, i, 0))
    res = _pcall(
        body, name="adamw_w_in", grid=(main // rows,),
        in_specs=[part, part, blk, blk, blk, ANY], out_specs=[blk] * 4, out_shape=[out] * 4,
        compiler_params=_cparams("parallel"),
    )(part_a, part_b, w3, m3, v3, after)
    last = pl.BlockSpec((tail, 1, D_MODEL), lambda i: (main // tail, 0, 0))
    whole = pl.BlockSpec((tail, D_MODEL), lambda i: (0, 0))
    return _pcall(
        tail_body, name="adamw_w_in_tail", grid=(1,),
        in_specs=[whole, whole, last, last, last] + [ANY] * 4, out_specs=[last] * 4, out_shape=[out] * 4,
        input_output_aliases={5: 0, 6: 1, 7: 2, 8: 3},
        compiler_params=_cparams("arbitrary"),
    )(part_a[0, main:], part_b[0, main:], w3, m3, v3, *res)


def _pad_lanes(vec8, start):
    return jnp.pad(vec8.reshape(1, -1), ((0, 0), (start, LANE - start - vec8.size)))


def kernel(x, norm_in_w, w_in, conv_qkv_w, A_log, dt_bias, gdn_norm_w, conv_w, conv_b, w_out, final_norm_w, loss_target, m_norm_in_w, m_w_in, m_conv_qkv_w, m_A_log, m_dt_bias, m_gdn_norm_w, m_conv_w, m_conv_b, m_w_out, m_final_norm_w, v_norm_in_w, v_w_in, v_conv_qkv_w, v_A_log, v_dt_bias, v_gdn_norm_w, v_conv_w, v_conv_b, v_w_out, v_final_norm_w):
    L = x.shape[1]
    nc = L // CHUNK
    xs = x[0]
    tgt = loss_target[0]
    fnw = final_norm_w.reshape(1, D_MODEL)

    as_rows = lambda a: jnp.transpose(a, (2, 0, 1))
    win_g, cqkv_g, cw_g = _all_gather([_cast_w_in(as_rows(w_in)), conv_qkv_w[0], conv_w[0]], "gather_weights",
                                      pieces=[4, 1, 1])
    wpad = _relayout_w_in(win_g)
    cqkv = jnp.concatenate([cqkv_g[d] for d in range(N_DEV)], axis=1)
    cw = jnp.concatenate([cw_g[d] for d in range(N_DEV)], axis=1)
    alog_p = _pad_lanes(A_log, HEADS)
    dtb_p = _pad_lanes(dt_bias, HEADS)
    tok = lambda started: started[4]
    wo_started = _spread_start(w_out[0].astype(bf16), wpad, "gather", "gather_w_out_start")

    proj, h = _in_proj(xs, norm_in_w, wpad, tok(wo_started))
    qkv = _qkv_act(proj, cqkv)
    sc, gr = _scalars(proj, alog_p, dtb_p)
    o, u_all, w_all, vn_all, t_all, sp_all = _gdn_fwd(qkv, sc, gr)
    mix = _conv_fwd(proj, cw, conv_b, _gdn_gate(o, proj, gdn_norm_w))
    wo = _spread_wait(wo_started, mix, "gather", "gather_w_out_wait")[1].reshape(-1, D_MODEL)
    dy, dyb, dmix, g_fnw, loss_v = _out_proj_loss(xs, mix, wo, fnw, tgt)

    g_wout = _tn_matmul(mix, dyb, "grad_w_out")
    gwo_started = _spread_start(g_wout.reshape(N_DEV, -1, D_MODEL), dyb, "scatter", "exchange_grad_w_out_start")
    do, dproj, g_gnw = _gdn_gate_bwd(o, proj, gdn_norm_w, dmix, tok(gwo_started))
    dproj, g_cw, g_cb = _conv_bwd(proj, cw, conv_b, dmix, dproj)
    dqkv_n, dsc, dgr = _gdn_bwd(qkv, sc, gr, u_all, w_all, vn_all, t_all, sp_all, do)
    dproj, g_cqkv = _qkv_bwd(proj, cqkv, dqkv_n, dproj)
    g_cqkv_blk = g_cqkv.reshape(4, N_DEV, -1).transpose(1, 0, 2)
    g_cw_blk = jnp.pad(g_cw.reshape(3, N_DEV, -1).transpose(1, 0, 2),
                       ((0, 0), (0, 1), (0, g_cqkv_blk.shape[2] - g_cw.shape[1] // N_DEV)))
    gsm_started = _spread_start(jnp.concatenate([g_cqkv_blk, g_cw_blk], axis=1), g_cqkv, "scatter",
                                "exchange_small_sharded_grads_start")
    dgr_col = jnp.pad(dgr.transpose(0, 2, 1).reshape(L, HEADS), ((0, 0), (HEADS, LANE - 2 * HEADS)))
    dproj, g_sc = _scalars_bwd(proj, alog_p, dtb_p, dsc, dgr_col, dproj, tok(gsm_started))
    g_win_blk = _grad_blocks(_tn_matmul(dproj, h, "grad_w_in"))

    (p_win,) = _pair_exchange([g_win_blk], "exchange_grads_pair")
    r_small = _spread_wait(gsm_started, p_win, "scatter", "exchange_small_sharded_grads_wait")[1]
    r_cqkv, r_cw = r_small[:, :4, :], r_small[:, 4:7, :g_cw.shape[1] // N_DEV]
    s_win = _pair_sum(g_win_blk, p_win, "pair_sum_w_in")
    gw1_started = _spread_start(s_win, r_small, "axis_a", "exchange_grads_axis1_start")
    grad_x, g_nw = _input_grad(dproj, wpad, xs, norm_in_w, dy, tok(gw1_started))
    s_thru, got1 = _spread_wait(gw1_started, grad_x, "axis_a", "exchange_grads_axis1_wait")
    t_win = _axis_sum(s_thru, got1, "axis_sum_w_in")
    gw2_started = _spread_start(t_win, got1, "axis_b", "exchange_grads_axis2_start")

    r_wout = _spread_wait(gwo_started, tok(gw2_started), "scatter", "exchange_grad_w_out_wait")[1]
    upd_wout =_adamw_reduce(r_wout, w_out[0], m_w_out[0], v_w_out[0], "adamw_w_out")
    upd_cqkv = _adamw_reduce(r_cqkv, conv_qkv_w[0], m_conv_qkv_w[0], v_conv_qkv_w[0], "adamw_conv_qkv_w")
    upd_cw = _adamw_reduce(r_cw, conv_w[0], m_conv_w[0], v_conv_w[0], "adamw_conv_w")

    t_thru, got2 = _spread_wait(gw2_started, upd_cw[0], "axis_b", "exchange_grads_axis2_wait")

    small_g = _pack_small([g_nw, g_cb, g_fnw, g_gnw, g_sc, loss_v], got2)
    gsg_started = _spread_start(small_g, got2, "gather", "gather_small_grads_start")
    upd_win_t = _adamw_w_in(t_thru, got2, as_rows(w_in), as_rows(m_w_in), as_rows(v_w_in), tok(gsg_started))
    upd_win = [jnp.transpose(a, (1, 2, 0)) for a in upd_win_t]
    small_all = _spread_wait(gsg_started, upd_win_t[0], "gather", "gather_small_grads_wait")[1]
    fvec = lambda a: a.reshape(1, D_MODEL)
    upd_small, loss_sum = _adamw_small(
        small_all,
        [norm_in_w, conv_b, fvec(final_norm_w), gdn_norm_w, A_log, dt_bias],
        [m_norm_in_w, m_conv_b, fvec(m_final_norm_w), m_gdn_norm_w, m_A_log, m_dt_bias],
        [v_norm_in_w, v_conv_b, fvec(v_final_norm_w), v_gdn_norm_w, v_A_log, v_dt_bias])

    outs = [loss_sum[0, 0], grad_x[None]]
    for k in range(4):
        nw_k, cb_k, fw_k, gn_k, al_k, dt_k = upd_small[k]
        outs += [nw_k, upd_win[k], upd_cqkv[k][None], al_k, dt_k, gn_k,
                 upd_cw[k][None], cb_k, upd_wout[k][None], fw_k.reshape(D_MODEL)]
    return tuple(outs)
```

```python
import jax
import jax.numpy as jnp
from jax import lax
from jax.experimental import pallas as pl
from jax.experimental.pallas import tpu as pltpu

f32 = jnp.float32
bf16 = jnp.bfloat16

N_DEV = 8
D_MODEL = 1024
HEADS = 8
HEAD_DIM = 128
CHUNK = 64
GDN_CPS = 4
GDN_CPS_BWD = 1
GDN_WIDTH = HEADS * HEAD_DIM
CONV_WIDTH = 1024
PROJ_WIDTH = 8208
SHARD_W = PROJ_WIDTH // N_DEV
EPS = 1e-6

LANE = 128
ELT_W = 256

OFF_QKV, OFF_ZG, OFF_CONV, OFF_BA = 0, 3072, 4096, 8192
CONV_BLOCK = 4 * ELT_W
PROJ_PAD = 8448
NAT_BA, NAT_CONV = 4096, 4112


def _padded_col(n):
    if n < NAT_BA:
        return n
    if n < NAT_CONV:
        return OFF_BA + n - NAT_BA
    g, ch = divmod(n - NAT_CONV, CONV_WIDTH)
    j, r = divmod(ch, ELT_W)
    return OFF_CONV + CONV_BLOCK * j + ELT_W * g + r


def _layout_segments(n0, n1):
    cuts = [NAT_BA, NAT_CONV] + [NAT_CONV + ELT_W * k for k in range(1, 4 * CONV_WIDTH // ELT_W)]
    pts = [n0] + [c for c in cuts if n0 < c < n1] + [n1]
    return [(lo, hi - lo, _padded_col(lo)) for lo, hi in zip(pts, pts[1:])]

ADAM_LR, ADAM_B1, ADAM_B2, ADAM_EPS, ADAM_WD, ADAM_STEP = 0.001, 0.9, 0.999, 1e-08, 0.01, 10

V7X_VMEM_BYTES = 64 * 1024 * 1024
VMEM_LIMIT = V7X_VMEM_BYTES - 8 * 1024 * 1024

MESH = pl.DeviceIdType.MESH
ANY = pl.BlockSpec(memory_space=pl.ANY)


def _pcall(body, **kw):
    return pl.pallas_call(body, **kw)


def _cparams(*sem):
    return pltpu.CompilerParams(dimension_semantics=sem if sem else None, vmem_limit_bytes=VMEM_LIMIT)


def _mm(a, b):
    return jnp.dot(a.astype(bf16), b.astype(bf16), preferred_element_type=f32)


def _mm_nt(a, b):
    return lax.dot_general(a.astype(bf16), b.astype(bf16), (((1,), (1,)), ((), ())), preferred_element_type=f32)


def _cat16(parts, axis):
    return jnp.concatenate([p.astype(bf16) for p in parts], axis=axis)


def _mm_tn(a, b):
    return lax.dot_general(a.astype(bf16), b.astype(bf16), (((0,), (0,)), ((), ())), preferred_element_type=f32)


def _rows(shape):
    return lax.broadcasted_iota(jnp.int32, shape, 0)


def _lanes(shape):
    return lax.broadcasted_iota(jnp.int32, shape, 1)


def _shift_down(x, s):
    if s == 0:
        return x
    return jnp.where(_rows(x.shape) >= s, pltpu.roll(x, s, 0), 0.0)


def _shift_up(x, s):
    if s == 0:
        return x
    n = x.shape[0]
    return jnp.where(_rows(x.shape) < n - s, pltpu.roll(x, n - s, 0), 0.0)


def _sigmoid(x):
    return jax.nn.sigmoid(x)


def _softplus(x):
    e = jnp.exp(-jnp.abs(x))
    small = e * (1.0 - e * (0.5 - e * (1.0 / 3.0)))
    return jnp.maximum(x, 0.0) + jnp.where(e < 0.01, small, jnp.log(1.0 + e))


def _mesh_pos():
    return lax.axis_index("x"), lax.axis_index("y"), lax.axis_index("c")


def _flat(px, py, pc):
    return 4 * px + 2 * py + pc


def _all_gather(xs, name, pieces=None):
    n = len(xs)
    pieces = pieces or [1] * n
    items = [(a, q) for a in range(n) for q in range(pieces[a])]
    ni = len(items)

    def view(ref, i):
        a, q = items[i]
        if pieces[a] == 1:
            return ref
        wd = xs[a].shape[-1] // pieces[a]
        return ref.at[(slice(None),) * (xs[a].ndim - 1) + (pl.ds(q * wd, wd),)]

    def body(*refs):
        x_refs, o_refs = refs[:n], refs[n:2 * n]
        send_sems, recv_sems, local_sems = refs[2 * n:]
        x, y, c = _mesh_pos()
        me, sibling = (x, y, c), (x, y, 1 - c)
        flip = lambda v, bit: v + bit - 2 * v * bit
        nbr_a = (flip(x, 1 - c), flip(y, c))
        nbr_b = (flip(x, c), flip(y, 1 - c))
        diag = (1 - x, 1 - y)

        def copy(i, k, block, to, own=False):
            a = items[i][0]
            dst = view(o_refs[a].at[_flat(*block)], i)
            return pltpu.make_async_remote_copy(
                src_ref=view(x_refs[a], i) if own else dst, dst_ref=dst,
                send_sem=send_sems.at[i, k], recv_sem=recv_sems.at[i, k], device_id=to, device_id_type=MESH)

        mine, sent = [], []

        def go(cp):
            cp.start()
            sent.append(cp)

        for a in range(n):
            cp = pltpu.make_async_copy(x_refs[a], o_refs[a].at[_flat(*me)], local_sems.at[a])
            cp.start()
            mine.append(cp)
        for a in range(ni):
            go(copy(a, 1, me, (*nbr_a, c), own=True))
            go(copy(a, 2, me, (*nbr_b, c), own=True))
            go(copy(a, 0, me, sibling, own=True))
        for a in range(ni):
            copy(a, 1, (*nbr_a, c), me).wait_recv()
            go(copy(a, 3, (*nbr_a, c), (*nbr_b, c)))
            go(copy(a, 4, (*nbr_a, c), sibling))
        for a in range(ni):
            copy(a, 2, (*nbr_b, c), me).wait_recv()
            go(copy(a, 5, (*nbr_b, c), sibling))
        for a in range(ni):
            copy(a, 3, (*diag, c), me).wait_recv()
            go(copy(a, 6, (*diag, c), sibling))
        for a in range(ni):
            copy(a, 0, sibling, me).wait_recv()
            copy(a, 4, (*nbr_b, 1 - c), me).wait_recv()
            copy(a, 5, (*nbr_a, 1 - c), me).wait_recv()
            copy(a, 6, (*diag, 1 - c), me).wait_recv()
        for cp in sent:
            cp.wait_send()
        for cp in mine:
            cp.wait()

    outs = _pcall(
        body, name=name,
        out_shape=[jax.ShapeDtypeStruct((N_DEV,) + a.shape, a.dtype) for a in xs],
        in_specs=[ANY] * n, out_specs=[ANY] * n,
        scratch_shapes=[pltpu.SemaphoreType.DMA((ni, 7)), pltpu.SemaphoreType.DMA((ni, 7)), pltpu.SemaphoreType.DMA((n,))],
    )(*xs)
    return list(outs)


def _pair_exchange(gs, name):
    n = len(gs)
    chips = [(0, 0), (0, 1), (1, 0), (1, 1)]

    def body(*refs):
        g_refs, o_refs = refs[:n], refs[n:2 * n]
        send_sems, recv_sems = refs[2 * n:]
        x, y, c = _mesh_pos()
        sibling = (x, y, 1 - c)

        def copy(a, i):
            xp, yp = chips[i]
            return pltpu.make_async_remote_copy(
                src_ref=g_refs[a].at[_flat(xp, yp, 1 - c)], dst_ref=o_refs[a].at[i],
                send_sem=send_sems.at[a, i], recv_sem=recv_sems.at[a, i], device_id=sibling, device_id_type=MESH)

        cps = [copy(a, i) for a in range(n) for i in range(4)]
        for cp in cps:
            cp.start()
        for cp in cps:
            cp.wait()

    outs = _pcall(
        body, name=name,
        out_shape=[jax.ShapeDtypeStruct((4,) + a.shape[1:], a.dtype) for a in gs],
        in_specs=[ANY] * n, out_specs=[ANY] * n,
        scratch_shapes=[pltpu.SemaphoreType.DMA((n, 4)), pltpu.SemaphoreType.DMA((n, 4))],
    )(*gs)
    return list(outs)


def _pair_sum(g, p1, name):
    _, R, C = g.shape
    tr = 256 if R % 256 == 0 else R
    cidx = lax.axis_index("c").astype(jnp.int32).reshape(1)

    def body(c_ref, g_ref, p_ref, o_ref):
        o_ref[...] = (g_ref[...].astype(f32) + p_ref[...].astype(f32)).astype(o_ref.dtype)

    return _pcall(
        body, name=name,
        grid_spec=pltpu.PrefetchScalarGridSpec(
            num_scalar_prefetch=1, grid=(4, R // tr),
            in_specs=[pl.BlockSpec((1, tr, C), lambda i, r, c_ref: (2 * i + c_ref[0], r, 0)),
                      pl.BlockSpec((1, tr, C), lambda i, r, c_ref: (i, r, 0))],
            out_specs=pl.BlockSpec((1, tr, C), lambda i, r, c_ref: (i, r, 0))),
        out_shape=jax.ShapeDtypeStruct((4, R, C), g.dtype),
        compiler_params=_cparams("parallel", "parallel"),
    )(cidx, g, p1)


def _axis_sum(s, got, name):
    _, R, C = s.shape
    x, y, c = _mesh_pos()
    me, _, b, _ = _axis_chips(x, y, c)
    idx = jnp.stack([2 * me[0] + me[1], 2 * b[0] + b[1]]).astype(jnp.int32)

    def body(idx_ref, s_ref, g_ref, o_ref):
        o_ref[...] = (s_ref[...].astype(f32) + g_ref[...].astype(f32)).astype(o_ref.dtype)

    return _pcall(
        body, name=name,
        grid_spec=pltpu.PrefetchScalarGridSpec(
            num_scalar_prefetch=1, grid=(2,),
            in_specs=[pl.BlockSpec((1, R, C), lambda k, idx_ref: (idx_ref[k], 0, 0)),
                      pl.BlockSpec((1, R, C), lambda k, idx_ref: (k, 0, 0))],
            out_specs=pl.BlockSpec((1, R, C), lambda k, idx_ref: (k, 0, 0))),
        out_shape=jax.ShapeDtypeStruct((2, R, C), s.dtype),
        compiler_params=_cparams("parallel"),
    )(idx, s, got)


HBM = pl.BlockSpec(memory_space=pltpu.HBM)
SEM = pl.BlockSpec(memory_space=pltpu.SEMAPHORE)
EFFECT = pltpu.SideEffectType.DATAFLOW_SIDE_EFFECTING


def _peers(x, y, c):
    out = []
    for k in range(1, N_DEV):
        kx, ky, kc = (k >> 2) & 1, (k >> 1) & 1, k & 1
        out.append(((1 - x) if kx else x, (1 - y) if ky else y, (1 - c) if kc else c))
    return out


SPREAD_COPIES = {"gather": N_DEV - 1, "scatter": N_DEV - 1, "axis_a": 2, "axis_b": 1}
SPREAD_SLOTS = {"axis_a": 2, "axis_b": 1}


def _axis_chips(x, y, c):
    flip = lambda v, bit: v + bit - 2 * v * bit
    return (x, y), (flip(x, 1 - c), flip(y, c)), (flip(x, c), flip(y, 1 - c)), (1 - x, 1 - y)


def _spread_copy(src_ref, land_ref, send_sems, recv_sems, k, plan):
    x, y, c = _mesh_pos()
    if plan in ("axis_a", "axis_b"):
        _, a, b, d = _axis_chips(x, y, c)
        chip = lambda p: 2 * p[0] + p[1]
        peer = (*(a if plan == "axis_a" else b), c)
        src = src_ref.at[chip(a) if k == 0 else chip(d)] if plan == "axis_a" else src_ref.at[1]
        slot = k
    else:
        peer = _peers(x, y, c)[k]
        src, slot = (src_ref.at[_flat(*peer)] if plan == "scatter" else src_ref), _flat(x, y, c)
    return pltpu.make_async_remote_copy(
        src_ref=src, dst_ref=land_ref.at[slot], send_sem=send_sems.at[k], recv_sem=recv_sems.at[k],
        device_id=peer, device_id_type=MESH)


def _own_copy(src_ref, land_ref, send_sems, plan):
    me = _flat(*_mesh_pos())
    return pltpu.make_async_copy(src_ref.at[me] if plan == "scatter" else src_ref, land_ref.at[me],
                                 send_sems.at[SPREAD_COPIES[plan]])


def _spread_start(src, after, plan, name):
    land_shape = (N_DEV,) + src.shape if plan == "gather" else src.shape
    if plan in SPREAD_SLOTS:
        land_shape = (SPREAD_SLOTS[plan],) + src.shape[1:]
    n_copies = SPREAD_COPIES[plan]

    def body(src_ref, land_ref, after_ref, send_sems, recv_sems, src_thru, land_thru, token):
        for k in range(n_copies):
            _spread_copy(src_ref, land_ref, send_sems, recv_sems, k, plan).start()
        if plan not in SPREAD_SLOTS:
            _own_copy(src_ref, land_ref, send_sems, plan).start()
        token[...] = jnp.zeros_like(token)

    return _pcall(
        body, name=name,
        out_shape=(pltpu.SemaphoreType.DMA((n_copies + (plan not in SPREAD_SLOTS),)), pltpu.SemaphoreType.DMA((n_copies,)),
                   pltpu.HBM(src.shape, src.dtype), pltpu.HBM(land_shape, src.dtype), jax.ShapeDtypeStruct((8, LANE), f32)),
        in_specs=(HBM, HBM, ANY), out_specs=(SEM, SEM, HBM, HBM, pl.BlockSpec(memory_space=pltpu.VMEM)),
        input_output_aliases={0: 2, 1: 3},
        compiler_params=pltpu.CompilerParams(has_side_effects=EFFECT),
    )(pltpu.with_memory_space_constraint(src, pltpu.HBM),
      pltpu.with_memory_space_constraint(lax.empty(land_shape, src.dtype), pltpu.HBM), after)


def _spread_wait(started, after, plan, name):
    send_sems, recv_sems, src_thru, land_thru, _ = started

    def body(src_ref, land_ref, send_sems, recv_sems, after_ref, src_dead, got_ref):
        for k in range(SPREAD_COPIES[plan]):
            cp = _spread_copy(src_ref, land_ref, send_sems, recv_sems, k, plan)
            cp.wait_send()
            cp.wait_recv()
        if plan not in SPREAD_SLOTS:
            _own_copy(src_ref, land_ref, send_sems, plan).wait()

    return _pcall(
        body, name=name,
        out_shape=(pltpu.HBM(src_thru.shape, src_thru.dtype), pltpu.HBM(land_thru.shape, land_thru.dtype)),
        in_specs=(HBM, HBM, SEM, SEM, ANY), out_specs=(HBM, HBM), input_output_aliases={0: 0, 1: 1},
        compiler_params=pltpu.CompilerParams(has_side_effects=EFFECT),
    )(src_thru, land_thru, send_sems, recv_sems, after)


COL_TILE = 256


def _cast_w_in(w3):
    n = w3.shape[0]

    def body(w_ref, o_ref):
        o_ref[...] = w_ref[:, 0, :].astype(bf16)

    tile = 2 * COL_TILE
    return _pcall(
        body, name="cast_w_in", grid=(D_MODEL // tile,),
        in_specs=[pl.BlockSpec((n, 1, tile), lambda j: (0, 0, j))],
        out_specs=pl.BlockSpec((n, tile), lambda j: (0, j)),
        out_shape=jax.ShapeDtypeStruct((n, D_MODEL), bf16),
        compiler_params=_cparams("parallel"),
    )(w3)


def _relayout_w_in(win_g):
    def body(g_ref, o_ref):
        used = OFF_BA + NAT_CONV - NAT_BA
        o_ref[used:PROJ_PAD, :] = jnp.zeros((PROJ_PAD - used, COL_TILE), o_ref.dtype)
        for d in range(N_DEV):
            for lo, width, dst in _layout_segments(d * SHARD_W, (d + 1) * SHARD_W):
                src = lo - d * SHARD_W
                o_ref[dst:dst + width, :] = g_ref[d, src:src + width, :]

    return _pcall(
        body, name="relayout_w_in", grid=(D_MODEL // COL_TILE,),
        in_specs=[pl.BlockSpec((N_DEV, SHARD_W, COL_TILE), lambda j: (0, 0, j))],
        out_specs=pl.BlockSpec((PROJ_PAD, COL_TILE), lambda j: (0, j)),
        out_shape=jax.ShapeDtypeStruct((PROJ_PAD, D_MODEL), win_g.dtype),
        compiler_params=_cparams("parallel"),
    )(win_g)


def _grad_blocks(g_t):
    def body(p_ref, o_ref):
        for d in range(N_DEV):
            for lo, width, src in _layout_segments(d * SHARD_W, (d + 1) * SHARD_W):
                dst = lo - d * SHARD_W
                o_ref[d, dst:dst + width, :] = p_ref[src:src + width, :]

    return _pcall(
        body, name="grad_blocks", grid=(D_MODEL // COL_TILE,),
        in_specs=[pl.BlockSpec((PROJ_PAD, COL_TILE), lambda j: (0, j))],
        out_specs=pl.BlockSpec((N_DEV, SHARD_W, COL_TILE), lambda j: (0, 0, j)),
        out_shape=jax.ShapeDtypeStruct((N_DEV, SHARD_W, D_MODEL), bf16),
        compiler_params=_cparams("parallel"),
    )(g_t)


def _in_proj(x, nw, wpad_t, after):
    L = x.shape[0]
    tn = 768
    nj = wpad_t.shape[0] // tn

    def body(x_ref, nw_ref, w_ref, after_ref, proj_ref, h_ref):
        first = pl.program_id(0) == 0

        def project(r, n, hv):
            proj_ref[r:r + n, :] = lax.dot_general(hv, w_ref[...], (((1,), (1,)), ((), ())), preferred_element_type=f32)

        @pl.when(first)
        def _():
            for r in range(0, L, 256):
                xs = x_ref[r:r + 256, :]
                ms = jnp.mean(xs * xs, axis=-1, keepdims=True)
                hv = ((xs * lax.rsqrt(ms + EPS)) * nw_ref[...]).astype(bf16)
                h_ref[r:r + 256, :] = hv
                project(r, 256, hv)

        @pl.when(jnp.logical_not(first))
        def _():
            for r in range(0, L, 512):
                project(r, 512, h_ref[r:r + 512, :])

    return _pcall(
        body, name="in_proj", grid=(nj,),
        in_specs=[pl.BlockSpec((L, D_MODEL), lambda j: (0, 0)), pl.BlockSpec((1, D_MODEL), lambda j: (0, 0)),
                  pl.BlockSpec((tn, D_MODEL), lambda j: (j, 0)), ANY],
        out_specs=[pl.BlockSpec((L, tn), lambda j: (0, j)), pl.BlockSpec((L, D_MODEL), lambda j: (0, 0))],
        out_shape=[jax.ShapeDtypeStruct((L, wpad_t.shape[0]), f32), jax.ShapeDtypeStruct((L, D_MODEL), bf16)],
        compiler_params=_cparams("arbitrary"),
    )(x, nw, wpad_t, after)


HALVES = [slice(i * LANE, (i + 1) * LANE) for i in range(ELT_W // LANE)]
QKV_W = 512
QKV_HEADS = [slice(i * LANE, (i + 1) * LANE) for i in range(QKV_W // LANE)]
STEPS_PER_GROUP = GDN_WIDTH // QKV_W


def _conv4(x, cw_ref, ls):
    return (cw_ref[3:4, ls] * x + cw_ref[2:3, ls] * _shift_down(x, 1) + cw_ref[1:2, ls] * _shift_down(x, 2)
            + cw_ref[0:1, ls] * _shift_down(x, 3))


def _qkv_act(proj, cw):
    L = proj.shape[0]

    def body(x_ref, cw_ref, o_ref):
        j = pl.program_id(0)
        scale = jnp.where(j < STEPS_PER_GROUP, HEAD_DIM ** -0.5, 1.0).astype(f32)
        for ls in QKV_HEADS:
            c = _conv4(x_ref[:, ls], cw_ref, ls)
            a = c * _sigmoid(c)
            rn = lax.rsqrt(jnp.sum(a * a, axis=1, keepdims=True) + EPS)
            o_ref[:, ls] = jnp.where(j < 2 * STEPS_PER_GROUP, (a * rn) * scale, a)

    return _pcall(
        body, name="qkv_act", grid=(3 * STEPS_PER_GROUP,),
        in_specs=[pl.BlockSpec((L, QKV_W), lambda j: (0, j)), pl.BlockSpec((4, QKV_W), lambda j: (0, j))],
        out_specs=pl.BlockSpec((L, QKV_W), lambda j: (0, j)),
        out_shape=jax.ShapeDtypeStruct((L, 3 * GDN_WIDTH), f32),
        compiler_params=_cparams("parallel"),
    )(proj, cw)


def _scalars(proj, alog_p, dtb_p):
    L = proj.shape[0]
    nc = L // CHUNK

    def body(x_ref, al_ref, dt_ref, sc_ref, gr_ref):
        x = x_ref[...]
        lane = _lanes(x.shape)
        beta = _sigmoid(x)
        g = -jnp.exp(al_ref[...]) * _softplus(x + dt_ref[...])
        gc = jnp.where((lane >= HEADS) & (lane < 2 * HEADS), g, 0.0)
        rc = _rows(x.shape) & (CHUNK - 1)
        for s in (1, 2, 4, 8, 16, 32):
            gc = gc + jnp.where(rc >= s, pltpu.roll(gc, s, 0), 0.0)
        sc_ref[...] = jnp.where(lane < HEADS, beta, gc)
        sel = (_lanes((HEADS, LANE)) == _rows((HEADS, LANE)) + HEADS).astype(f32)
        for c in range(nc):
            gr_ref[c] = lax.dot_general(sel, sc_ref[c * CHUNK:(c + 1) * CHUNK, :], (((1,), (1,)), ((), ())),
                                        preferred_element_type=f32, precision=lax.Precision.HIGHEST)

    return _pcall(
        body, name="scalars", grid=(1,),
        in_specs=[pl.BlockSpec((L, LANE), lambda i: (0, OFF_BA // LANE)), pl.BlockSpec((1, LANE), lambda i: (0, 0)),
                  pl.BlockSpec((1, LANE), lambda i: (0, 0))],
        out_specs=[pl.BlockSpec((L, LANE), lambda i: (0, 0)), pl.BlockSpec((nc, HEADS, CHUNK), lambda i: (0, 0, 0))],
        out_shape=[jax.ShapeDtypeStruct((L, LANE), f32), jax.ShapeDtypeStruct((nc, HEADS, CHUNK), f32)],
        compiler_params=_cparams("arbitrary"),
    )(proj, alog_p, dtb_p)


def _head_scalars(sc, gr_ref, h, ci=0):
    lane = _lanes(sc.shape)
    beta = jnp.sum(jnp.where(lane == h, sc, 0.0), axis=1, keepdims=True)
    gcc = jnp.sum(jnp.where(lane == HEADS + h, sc, 0.0), axis=1, keepdims=True)
    gcr = gr_ref[ci, h:h + 1, :]
    gl = jnp.sum(jnp.where(_lanes(gcr.shape) == CHUNK - 1, gcr, 0.0), axis=1, keepdims=True)
    ii, jj = _rows((CHUNK, CHUNK)), _lanes((CHUNK, CHUNK))
    dmat = jnp.where(ii >= jj, jnp.exp(jnp.minimum(gcc - gcr, 0.0)), 0.0)
    dmat_t = jnp.where(jj >= ii, jnp.exp(jnp.minimum(gcr - gcc, 0.0)), 0.0)
    return beta, gcc, gl, dmat, dmat_t, ii, jj


def _gdn_fwd(qkv, sc, gr):
    L = qkv.shape[0]
    nc = L // CHUNK
    W = GDN_WIDTH
    cps = GDN_CPS if nc % GDN_CPS == 0 else 1
    rows_per_step = cps * CHUNK

    def body(qkv_ref, sc_ref, gr_ref, o_ref, u_ref, w_ref, vn_ref, t_ref, sp_ref, s_scr):
        @pl.when(pl.program_id(0) == 0)
        def _():
            s_scr[...] = jnp.zeros_like(s_scr)
        HS = range(cps * HEADS)
        hd = [i % HEADS for i in HS]
        rs = [slice((i // HEADS) * CHUNK, (i // HEADS + 1) * CHUNK) for i in HS]
        cs = [slice(hd[i] * HEAD_DIM, (hd[i] + 1) * HEAD_DIM) for i in HS]
        q = [qkv_ref[rs[i], hd[i] * HEAD_DIM:(hd[i] + 1) * HEAD_DIM] for i in HS]
        k = [qkv_ref[rs[i], W + hd[i] * HEAD_DIM:W + (hd[i] + 1) * HEAD_DIM] for i in HS]
        v = [qkv_ref[rs[i], 2 * W + hd[i] * HEAD_DIM:2 * W + (hd[i] + 1) * HEAD_DIM] for i in HS]
        hsc = [_head_scalars(sc_ref[rs[i], :], gr_ref, hd[i], i // HEADS) for i in HS]
        beta, gcc, gl, dmat = ([x[i] for x in hsc] for i in range(4))
        ii, jj = hsc[0][5], hsc[0][6]
        eg = [jnp.exp(gcc[h]) for h in HS]
        kb = [k[h] * beta[h] for h in HS]
        kk = [_mm_nt(kb[h], k[h]) for h in HS]
        qk = [_mm_nt(q[h], k[h]) for h in HS]
        n0 = [-jnp.where(ii > jj, kk[h] * dmat[h], 0.0) for h in HS]
        n1 = [_mm(n0[h], n0[h]) for h in HS]
        n2 = [_mm(n1[h], n1[h]) for h in HS]
        p01 = [n0[h] + n1[h] + _mm(n0[h], n1[h]) for h in HS]
        n3 = [_mm(n2[h], n2[h]) for h in HS]
        n4 = [_mm(n3[h], n3[h]) for h in HS]
        p23 = [n2[h] + n3[h] + _mm(n2[h], n3[h]) for h in HS]
        n5 = [_mm(n4[h], n4[h]) for h in HS]
        p03 = [p01[h] + p23[h] + _mm(p01[h], p23[h]) for h in HS]
        p45 = [n4[h] + n5[h] + _mm(n4[h], n5[h]) for h in HS]
        t = [p03[h] + p45[h] + _mm(p03[h], p45[h]) for h in HS]
        vb = [v[h] * beta[h] for h in HS]
        kbg = [kb[h] * eg[h] for h in HS]
        uw = [_mm(t[h], _cat16([vb[h], kbg[h]], 1)) for h in HS]
        u = [vb[h] + uw[h][:, :HEAD_DIM] for h in HS]
        w = [kbg[h] + uw[h][:, HEAD_DIM:] for h in HS]
        wq = [_cat16([w[h], q[h] * eg[h]], 0) for h in HS]
        p = [jnp.where(ii >= jj, qk[h] * dmat[h], 0.0) for h in HS]
        ks = [k[h] * jnp.exp(gl[h] - gcc[h]) for h in HS]
        s = [s_scr[h] for h in range(HEADS)]
        for ci in range(cps):
            IS = range(ci * HEADS, (ci + 1) * HEADS)
            ws = [_mm(wq[i], s[hd[i]]) for i in IS]
            vn = [u[i] - ws[hd[i]][:CHUNK] for i in IS]
            pv = [_mm(p[i], vn[hd[i]]) for i in IS]
            kv = [_mm_tn(ks[i], vn[hd[i]]) for i in IS]
            for i in IS:
                h = hd[i]
                sp_ref[ci, cs[i], :] = s[h]
                o_ref[rs[i], cs[i]] = ws[h][CHUNK:] + pv[h]
                vn_ref[rs[i], cs[i]] = vn[h].astype(bf16)
            s = [jnp.exp(gl[i]) * s[hd[i]] + kv[hd[i]] for i in IS]
        for h in range(HEADS):
            s_scr[h] = s[h]
        for i in HS:
            u_ref[rs[i], cs[i]] = u[i].astype(bf16)
            w_ref[rs[i], cs[i]] = w[i].astype(bf16)
            t_ref[i // HEADS, hd[i]] = t[i].astype(bf16)

    row = lambda c: (c, 0)
    act, act16 = jax.ShapeDtypeStruct((L, W), f32), jax.ShapeDtypeStruct((L, W), bf16)
    return _pcall(
        body, name="gdn_fwd", grid=(nc // cps,),
        in_specs=[pl.BlockSpec((rows_per_step, 3 * W), row), pl.BlockSpec((rows_per_step, LANE), row),
                  pl.BlockSpec((cps, HEADS, CHUNK), lambda c: (c, 0, 0))],
        out_specs=[pl.BlockSpec((rows_per_step, W), row)] * 4 + [
            pl.BlockSpec((cps, HEADS, CHUNK, CHUNK), lambda c: (c, 0, 0, 0)),
            pl.BlockSpec((cps, W, HEAD_DIM), lambda c: (c, 0, 0))],
        out_shape=[act, act16, act16, act16, jax.ShapeDtypeStruct((nc, HEADS, CHUNK, CHUNK), bf16),
                   jax.ShapeDtypeStruct((nc, W, HEAD_DIM), f32)],
        scratch_shapes=[pltpu.VMEM((HEADS, HEAD_DIM, HEAD_DIM), f32)],
        compiler_params=_cparams("arbitrary"),
    )(qkv, sc, gr)


def _gdn_gate(o, proj, gnw):
    L = o.shape[0]

    def body(o_ref, z_ref, w_ref, m_ref):
        for ls in HALVES:
            ov, z = o_ref[:, ls], z_ref[:, ls]
            rms = lax.rsqrt(jnp.mean(ov * ov, axis=-1, keepdims=True) + EPS)
            m_ref[:, ls] = (((ov * rms) * w_ref[...]) * (z * _sigmoid(z))).astype(bf16)

    return _pcall(
        body, name="gdn_gate", grid=(GDN_WIDTH // ELT_W,),
        in_specs=[pl.BlockSpec((L, ELT_W), lambda j: (0, j)), pl.BlockSpec((L, ELT_W), lambda j: (0, OFF_ZG // ELT_W + j)),
                  pl.BlockSpec((1, LANE), lambda j: (0, 0))],
        out_specs=pl.BlockSpec((L, ELT_W), lambda j: (0, j)),
        out_shape=jax.ShapeDtypeStruct((L, GDN_WIDTH + CONV_WIDTH), bf16),
        compiler_params=_cparams("parallel"),
    )(o, proj, gnw)


def _conv3(u, cw_ref, ls):
    return cw_ref[2:3, ls] * u + cw_ref[1:2, ls] * _shift_down(u, 1) + cw_ref[0:1, ls] * _shift_down(u, 2)


def _conv_specs(L):
    return [pl.BlockSpec((L, CONV_BLOCK), lambda j: (0, OFF_CONV // CONV_BLOCK + j)),
            pl.BlockSpec((3, ELT_W), lambda j: (0, j)), pl.BlockSpec((1, ELT_W), lambda j: (0, j))]


def _conv_parts(ls):
    return [slice(g * ELT_W + ls.start, g * ELT_W + ls.stop) for g in range(4)]


def _conv_fwd(proj, cw, cb, mix):
    L = proj.shape[0]

    def body(p_ref, cw_ref, cb_ref, mix_in, m_ref):
        for ls in HALVES:
            sb, sc_, sh, sz = _conv_parts(ls)
            z = p_ref[:, sz]
            cv = _conv3(p_ref[:, sc_] * p_ref[:, sh], cw_ref, ls) + cb_ref[:, ls]
            m_ref[:, ls] = ((p_ref[:, sb] * cv) * (z * _sigmoid(z))).astype(bf16)

    return _pcall(
        body, name="conv_fwd", grid=(CONV_WIDTH // ELT_W,),
        in_specs=_conv_specs(L) + [ANY], out_specs=pl.BlockSpec((L, ELT_W), lambda j: (0, GDN_WIDTH // ELT_W + j)),
        out_shape=jax.ShapeDtypeStruct(mix.shape, mix.dtype), input_output_aliases={3: 0},
        compiler_params=_cparams("parallel"),
    )(proj, cw, cb, mix)


def _out_proj_loss(x, mix, wo, fw, tgt):
    L = x.shape[0]
    tm = min(512, L)
    MW = GDN_WIDTH + CONV_WIDTH

    def body(x_ref, m_ref, wo_ref, fw_ref, t_ref, dy_ref, dyb_ref, dm_ref, gfw_ref, loss_ref):
        @pl.when(pl.program_id(0) == 0)
        def _():
            gfw_ref[...] = jnp.zeros_like(gfw_ref)
            loss_ref[...] = jnp.zeros_like(loss_ref)
        y = x_ref[...] + jnp.dot(m_ref[...], wo_ref[...], preferred_element_type=f32)
        r = lax.rsqrt(jnp.mean(y * y, axis=-1, keepdims=True) + EPS)
        yh = y * r
        fwv = fw_ref[...]
        diff = yh * fwv - t_ref[...]
        loss_ref[...] += jnp.sum(jnp.sum(diff * diff, axis=-1, keepdims=True), axis=0, keepdims=True) * (0.5 / D_MODEL)
        dout = diff * (1.0 / D_MODEL)
        gfw_ref[...] += jnp.sum(dout * yh, axis=0, keepdims=True)
        dyh = dout * fwv
        dy = r * (dyh - yh * jnp.mean(dyh * yh, axis=-1, keepdims=True))
        dy_ref[...] = dy
        dyb = dy.astype(bf16)
        dyb_ref[...] = dyb
        dm_ref[...] = lax.dot_general(dyb, wo_ref[...], (((1,), (1,)), ((), ())), preferred_element_type=f32)

    row = lambda i: (i, 0)
    fix = lambda i: (0, 0)
    act = jax.ShapeDtypeStruct((L, D_MODEL), f32)
    return _pcall(
        body, name="out_proj_loss", grid=(L // tm,),
        in_specs=[pl.BlockSpec((tm, D_MODEL), row), pl.BlockSpec((tm, MW), row), pl.BlockSpec((MW, D_MODEL), fix),
                  pl.BlockSpec((1, D_MODEL), fix), pl.BlockSpec((tm, D_MODEL), row)],
        out_specs=[pl.BlockSpec((tm, D_MODEL), row), pl.BlockSpec((tm, D_MODEL), row), pl.BlockSpec((tm, MW), row),
                   pl.BlockSpec((1, D_MODEL), fix), pl.BlockSpec((1, LANE), fix)],
        out_shape=[act, jax.ShapeDtypeStruct((L, D_MODEL), bf16), jax.ShapeDtypeStruct((L, MW), f32),
                   jax.ShapeDtypeStruct((1, D_MODEL), f32), jax.ShapeDtypeStruct((1, LANE), f32)],
        compiler_params=_cparams("arbitrary"),
    )(x, mix, wo, fw, tgt)


def _tn_matmul(a, b, name):
    L, M = a.shape
    N = b.shape[1]
    tm = 512 if M % 512 == 0 else (768 if M % 768 == 0 else M)

    def body(a_ref, b_ref, o_ref):
        o_ref[...] = lax.dot_general(a_ref[...], b_ref[...], (((0,), (0,)), ((), ())),
                                     preferred_element_type=f32).astype(o_ref.dtype)

    return _pcall(
        body, name=name, grid=(M // tm,),
        in_specs=[pl.BlockSpec((L, tm), lambda i: (0, i)), pl.BlockSpec((L, N), lambda i: (0, 0))],
        out_specs=pl.BlockSpec((tm, N), lambda i: (i, 0)),
        out_shape=jax.ShapeDtypeStruct((M, N), bf16),
        compiler_params=_cparams("parallel"),
    )(a, b)


def _gdn_gate_bwd(o, proj, gnw, dmix_a, after):
    L = o.shape[0]

    def body(o_ref, z_ref, w_ref, dm_ref, after_ref, do_ref, dz_ref, gw_ref):
        @pl.when(pl.program_id(0) == 0)
        def _():
            gw_ref[...] = jnp.zeros_like(gw_ref)
        wv = w_ref[...]
        for ls in HALVES:
            ov, z, dm = o_ref[:, ls], z_ref[:, ls], dm_ref[:, ls]
            rms = lax.rsqrt(jnp.mean(ov * ov, axis=-1, keepdims=True) + EPS)
            xh = ov * rms
            sg = _sigmoid(z)
            d_on = dm * (z * sg)
            dz_ref[:, ls] = (dm * (xh * wv) * (sg * (1.0 + z * (1.0 - sg)))).astype(bf16)
            gw_ref[...] += jnp.sum(d_on * xh, axis=0, keepdims=True)
            dxh = d_on * wv
            do_ref[:, ls] = (rms * (dxh - xh * jnp.mean(dxh * xh, axis=-1, keepdims=True))).astype(bf16)

    wide = pl.BlockSpec((L, ELT_W), lambda j: (0, j))
    return _pcall(
        body, name="gdn_gate_bwd", grid=(GDN_WIDTH // ELT_W,),
        in_specs=[wide, pl.BlockSpec((L, ELT_W), lambda j: (0, OFF_ZG // ELT_W + j)),
                  pl.BlockSpec((1, LANE), lambda j: (0, 0)), wide, ANY],
        out_specs=[wide, pl.BlockSpec((L, ELT_W), lambda j: (0, OFF_ZG // ELT_W + j)),
                   pl.BlockSpec((1, LANE), lambda j: (0, 0))],
        out_shape=[jax.ShapeDtypeStruct((L, GDN_WIDTH), bf16), jax.ShapeDtypeStruct((L, PROJ_PAD), bf16),
                   jax.ShapeDtypeStruct((1, LANE), f32)],
        compiler_params=_cparams("arbitrary"),
    )(o, proj, gnw, dmix_a, after)


def _conv_bwd(proj, cw, cb, dmix_b, dproj):
    L = proj.shape[0]

    def body(p_ref, cw_ref, cb_ref, dm_ref, dproj_in, dp_ref, gcw_ref, gcb_ref):
        for ls in HALVES:
            sb, sc_, sh, sz_ = _conv_parts(ls)
            bv, cv_, hv, z, dm = p_ref[:, sb], p_ref[:, sc_], p_ref[:, sh], p_ref[:, sz_], dm_ref[:, ls]
            u = cv_ * hv
            cv = _conv3(u, cw_ref, ls) + cb_ref[:, ls]
            sg = _sigmoid(z)
            sz = z * sg
            dp_ref[:, sb] = (dm * cv * sz).astype(bf16)
            dp_ref[:, sz_] = (dm * (bv * cv) * (sg * (1.0 + z * (1.0 - sg)))).astype(bf16)
            dcv = dm * bv * sz
            gcb_ref[:, ls] = jnp.sum(dcv, axis=0, keepdims=True)
            dcv1, dcv2 = _shift_up(dcv, 1), _shift_up(dcv, 2)
            gcw_ref[2:3, ls] = jnp.sum(dcv * u, axis=0, keepdims=True)
            gcw_ref[1:2, ls] = jnp.sum(dcv1 * u, axis=0, keepdims=True)
            gcw_ref[0:1, ls] = jnp.sum(dcv2 * u, axis=0, keepdims=True)
            du = cw_ref[2:3, ls] * dcv + cw_ref[1:2, ls] * dcv1 + cw_ref[0:1, ls] * dcv2
            dp_ref[:, sc_] = (du * hv).astype(bf16)
            dp_ref[:, sh] = (du * cv_).astype(bf16)

    return _pcall(
        body, name="conv_bwd", grid=(CONV_WIDTH // ELT_W,),
        in_specs=_conv_specs(L) + [pl.BlockSpec((L, ELT_W), lambda j: (0, GDN_WIDTH // ELT_W + j)), ANY],
        out_specs=[pl.BlockSpec((L, CONV_BLOCK), lambda j: (0, OFF_CONV // CONV_BLOCK + j)),
                   pl.BlockSpec((3, ELT_W), lambda j: (0, j)), pl.BlockSpec((1, ELT_W), lambda j: (0, j))],
        out_shape=[jax.ShapeDtypeStruct(dproj.shape, dproj.dtype), jax.ShapeDtypeStruct((3, CONV_WIDTH), f32),
                   jax.ShapeDtypeStruct((1, CONV_WIDTH), f32)],
        input_output_aliases={4: 0},
        compiler_params=_cparams("parallel"),
    )(proj, cw, cb, dmix_b, dproj)


def _gdn_bwd(qkv, sc, gr, u_all, w_all, vn_all, t_all, sp_all, do_all):
    L = qkv.shape[0]
    nc = L // CHUNK
    W = GDN_WIDTH
    cps = GDN_CPS_BWD if nc % GDN_CPS_BWD == 0 else 1
    rows_per_step = cps * CHUNK
    nsteps = nc // cps

    def body(qkv_ref, sc_ref, gr_ref, u_ref, w_ref, vn_ref, t_ref, sp_ref, do_ref, dqkv_ref, dsc_ref, dgr_ref, ds_scr):
        @pl.when(pl.program_id(0) == 0)
        def _():
            ds_scr[...] = jnp.zeros_like(ds_scr)
        nh, base = HEADS, 0
        HS = range(cps * nh)
        hl = [i % nh for i in HS]
        hd = [base + hl[i] for i in HS]
        rs = [slice((i // nh) * CHUNK, (i // nh + 1) * CHUNK) for i in HS]
        cs = [slice(hd[i] * HEAD_DIM, (hd[i] + 1) * HEAD_DIM) for i in HS]
        q = [qkv_ref[rs[i], hd[i] * HEAD_DIM:(hd[i] + 1) * HEAD_DIM] for i in HS]
        k = [qkv_ref[rs[i], W + hd[i] * HEAD_DIM:W + (hd[i] + 1) * HEAD_DIM] for i in HS]
        v = [qkv_ref[rs[i], 2 * W + hd[i] * HEAD_DIM:2 * W + (hd[i] + 1) * HEAD_DIM] for i in HS]
        hsc = [_head_scalars(sc_ref[rs[i], :], gr_ref, hd[i], i // nh) for i in HS]
        beta, gcc, gl, dmat, dmat_t = ([x[i] for x in hsc] for i in range(5))
        ii, jj = hsc[0][5], hsc[0][6]
        eg = [jnp.exp(gcc[h]) for h in HS]
        ekl = [jnp.exp(gl[h] - gcc[h]) for h in HS]
        egl = [jnp.exp(gl[h]) for h in HS]
        kb = [k[h] * beta[h] for h in HS]
        ks = [k[h] * ekl[h] for h in HS]
        do = [do_ref[rs[h], cs[h]] for h in HS]
        vn = [vn_ref[rs[h], cs[h]] for h in HS]
        s = [sp_ref[h // nh, cs[h], :] for h in HS]
        w = [w_ref[rs[h], cs[h]] for h in HS]
        qd = [q[h] * eg[h] for h in HS]

        kq = [_mm_nt(k[h], q[h]) for h in HS]
        p_t = [jnp.where(jj >= ii, kq[h] * dmat_t[h], 0.0) for h in HS]
        ptd = [_mm(p_t[h], do[h]) for h in HS]
        qw = [_cat16([qd[h], -w[h]], 0) for h in HS]
        dsn, dvn, dodv = [None] * len(HS), [None] * len(HS), [None] * len(HS)
        ds_cur = [ds_scr[base + h] for h in range(nh)]
        for ci in reversed(range(cps)):
            IS = range(ci * nh, (ci + 1) * nh)
            ksd = [_mm(ks[i], ds_cur[hl[i]]) for i in IS]
            for i in IS:
                dsn[i] = ds_cur[hl[i]]
                dvn[i] = ptd[i] + ksd[hl[i]]
                dodv[i] = _cat16([do[i], dvn[i]], 0)
            dsq = [_mm_tn(qw[i], dodv[i]) for i in IS]
            ds_cur = [egl[i] * ds_cur[hl[i]] + dsq[hl[i]] for i in IS]
        for h in range(nh):
            ds_scr[base + h] = ds_cur[h]
        x1 = [_mm_nt(dodv[h], s[h]) for h in HS]
        dks = [_mm_nt(vn[h], dsn[h]) for h in HS]
        dov = [_mm_nt(do[h], vn[h]) for h in HS]
        vdo = [_mm_nt(vn[h], do[h]) for h in HS]
        kk = [_mm_nt(kb[h], k[h]) for h in HS]
        qk = [_mm_nt(q[h], k[h]) for h in HS]
        dgl = [egl[h] * jnp.sum(jnp.sum(s[h] * dsn[h], axis=1, keepdims=True), axis=0, keepdims=True) for h in HS]
        dqd = [x1[h][:CHUNK] for h in HS]
        duw = [jnp.concatenate([dvn[h], -x1[h][CHUNK:]], axis=1) for h in HS]
        tdu = [_mm_tn(t_ref[h // nh, hd[h]], duw[h]) for h in HS]
        dvk = [duw[h] + tdu[h] for h in HS]
        uw = [jnp.concatenate([u_ref[rs[h], cs[h]], w[h]], axis=1) for h in HS]
        da = [-jnp.where(ii > jj, _mm_nt(dvk[h], uw[h]), 0.0) for h in HS]
        da_t = [-jnp.where(jj > ii, _mm_nt(uw[h], dvk[h]), 0.0) for h in HS]
        dp = [jnp.where(ii >= jj, dov[h], 0.0) for h in HS]
        dp_t = [jnp.where(jj >= ii, vdo[h], 0.0) for h in HS]
        r1 = [_mm(_cat16([da[h] * dmat[h], dp[h] * dmat[h]], 0), k[h]) for h in HS]
        dk1 = [_mm(_cat16([da_t[h] * dmat_t[h], dp_t[h] * dmat_t[h]], 1), _cat16([kb[h], q[h]], 0)) for h in HS]
        lane = _lanes((CHUNK, LANE))
        for ci in range(cps):
            dsc = jnp.zeros((CHUNK, LANE), f32)
            for i in range(ci * nh, (ci + 1) * nh):
                h = hd[i]
                a = jnp.where(ii > jj, kk[i] * dmat[i], 0.0)
                p = jnp.where(ii >= jj, qk[i] * dmat[i], 0.0)
                gmat = da[i] * a + dp[i] * p
                dvb, dkbg = dvk[i][:, :HEAD_DIM], dvk[i][:, HEAD_DIM:]
                kbg = kb[i] * eg[i]
                dkb = r1[i][:CHUNK] + dkbg * eg[i]
                dq = r1[i][CHUNK:] + dqd[i] * eg[i]
                dk = dk1[i] + dks[i] * ekl[i] + dkb * beta[i]
                dbeta = jnp.sum(dkb * k[i] + dvb * v[i], axis=1, keepdims=True)
                ksum = jnp.sum(dks[i] * ks[i], axis=1, keepdims=True)
                dgl_tot = dgl[i] + jnp.sum(ksum, axis=0, keepdims=True)
                dgc = (jnp.sum(gmat, axis=1, keepdims=True) + jnp.sum(dqd[i] * qd[i] + dkbg * kbg, axis=1, keepdims=True)
                       - ksum)
                dgc = dgc + jnp.where(_rows(dgc.shape) == CHUNK - 1, dgl_tot, 0.0)
                dqkv_ref[rs[i], h * HEAD_DIM:(h + 1) * HEAD_DIM] = dq
                dqkv_ref[rs[i], W + h * HEAD_DIM:W + (h + 1) * HEAD_DIM] = dk
                dqkv_ref[rs[i], 2 * W + h * HEAD_DIM:2 * W + (h + 1) * HEAD_DIM] = dvb * beta[i]
                dsc = jnp.where(lane == h, dbeta, jnp.where(lane == HEADS + h, dgc, dsc))
                dgr_ref[ci, h:h + 1, :] = jnp.sum(gmat, axis=0, keepdims=True)
            dsc_ref[ci * CHUNK:(ci + 1) * CHUNK, :] = dsc

    row = lambda c: (nsteps - 1 - c, 0)
    lead3 = lambda c: (nsteps - 1 - c, 0, 0)
    return _pcall(
        body, name="gdn_bwd", grid=(nsteps,),
        in_specs=[pl.BlockSpec((rows_per_step, 3 * W), row), pl.BlockSpec((rows_per_step, LANE), row),
                  pl.BlockSpec((cps, HEADS, CHUNK), lead3),
                  pl.BlockSpec((rows_per_step, W), row), pl.BlockSpec((rows_per_step, W), row),
                  pl.BlockSpec((rows_per_step, W), row),
                  pl.BlockSpec((cps, HEADS, CHUNK, CHUNK), lambda c: (nsteps - 1 - c, 0, 0, 0)),
                  pl.BlockSpec((cps, W, HEAD_DIM), lead3), pl.BlockSpec((rows_per_step, W), row)],
        out_specs=[pl.BlockSpec((rows_per_step, 3 * W), row), pl.BlockSpec((rows_per_step, LANE), row),
                   pl.BlockSpec((cps, HEADS, CHUNK), lead3)],
        out_shape=[jax.ShapeDtypeStruct((L, 3 * W), f32), jax.ShapeDtypeStruct((L, LANE), f32),
                   jax.ShapeDtypeStruct((nc, HEADS, CHUNK), f32)],
        scratch_shapes=[pltpu.VMEM((HEADS, HEAD_DIM, HEAD_DIM), f32)],
        compiler_params=_cparams("arbitrary"),
    )(qkv, sc, gr, u_all, w_all, vn_all, t_all, sp_all, do_all)


def _qkv_bwd(proj, cw, dn, dproj):
    L = proj.shape[0]

    def body(x_ref, cw_ref, dn_ref, dproj_in, dx_ref, gcw_ref):
        j = pl.program_id(0)
        steps = GDN_WIDTH // ELT_W
        scale = jnp.where(j < steps, HEAD_DIM ** -0.5, 1.0).astype(f32)
        for ls in HALVES:
            x, dn_v = x_ref[:, ls], dn_ref[:, ls]
            c = _conv4(x, cw_ref, ls)
            sg = _sigmoid(c)
            a = c * sg
            rn = lax.rsqrt(jnp.sum(a * a, axis=1, keepdims=True) + EPS)
            da_n = (scale * rn) * (dn_v - a * ((rn * rn) * jnp.sum(dn_v * a, axis=1, keepdims=True)))
            da = jnp.where(j < 2 * steps, da_n, dn_v)
            dc = da * (sg * (1.0 + c * (1.0 - sg)))
            dc1, dc2, dc3 = _shift_up(dc, 1), _shift_up(dc, 2), _shift_up(dc, 3)
            gcw_ref[3:4, ls] = jnp.sum(dc * x, axis=0, keepdims=True)
            gcw_ref[2:3, ls] = jnp.sum(dc1 * x, axis=0, keepdims=True)
            gcw_ref[1:2, ls] = jnp.sum(dc2 * x, axis=0, keepdims=True)
            gcw_ref[0:1, ls] = jnp.sum(dc3 * x, axis=0, keepdims=True)
            dx = cw_ref[3:4, ls] * dc + cw_ref[2:3, ls] * dc1 + cw_ref[1:2, ls] * dc2 + cw_ref[0:1, ls] * dc3
            dx_ref[:, ls] = dx.astype(bf16)

    col = pl.BlockSpec((L, ELT_W), lambda j: (0, j))
    wspec = pl.BlockSpec((4, ELT_W), lambda j: (0, j))
    return _pcall(
        body, name="qkv_bwd", grid=(3 * GDN_WIDTH // ELT_W,),
        in_specs=[col, wspec, col, ANY], out_specs=[col, wspec],
        out_shape=[jax.ShapeDtypeStruct(dproj.shape, dproj.dtype), jax.ShapeDtypeStruct((4, 3 * GDN_WIDTH), f32)],
        input_output_aliases={3: 0},
        compiler_params=_cparams("parallel"),
    )(proj, cw, dn, dproj)


def _scalars_bwd(proj, alog_p, dtb_p, dsc, dgr_col, dproj, after):
    L = proj.shape[0]

    def body(x_ref, al_ref, dt_ref, dsc_ref, dgr_ref, dproj_in, after_ref, dba_ref, gs_ref):
        x, dsc_v = x_ref[...], dsc_ref[...]
        lane = _lanes(x.shape)
        dec = (lane >= HEADS) & (lane < 2 * HEADS)
        dg = jnp.where(dec, dsc_v - dgr_ref[...], 0.0)
        rc = _rows(x.shape) & (CHUNK - 1)
        for s in (1, 2, 4, 8, 16, 32):
            dg = dg + jnp.where(rc + s < CHUNK, pltpu.roll(dg, L - s, 0), 0.0)
        xa = x + dt_ref[...]
        ea = jnp.exp(al_ref[...])
        g = -ea * _softplus(xa)
        da = dg * (-ea) * _sigmoid(xa)
        beta = _sigmoid(x)
        db = dsc_v * beta * (1.0 - beta)
        dba_ref[:, :LANE] = jnp.where(lane < HEADS, db, jnp.where(dec, da, 0.0)).astype(bf16)
        dba_ref[:, LANE:] = jnp.zeros((L, ELT_W - LANE), bf16)
        g_al = jnp.sum(jnp.where(dec, dg * g, 0.0), axis=0, keepdims=True)
        g_dt = jnp.sum(jnp.where(dec, da, 0.0), axis=0, keepdims=True)
        row8 = _rows(gs_ref.shape)
        gs = jnp.where(row8 == 0, g_al, jnp.where(row8 == 1, g_dt, 0.0))
        gs_ref[...] = pltpu.roll(gs, LANE - HEADS, 1)

    full = pl.BlockSpec((L, LANE), lambda i: (0, 0))
    vec = pl.BlockSpec((1, LANE), lambda i: (0, 0))
    return _pcall(
        body, name="scalars_bwd", grid=(1,),
        in_specs=[pl.BlockSpec((L, LANE), lambda i: (0, OFF_BA // LANE)), vec, vec, full, full, ANY, ANY],
        out_specs=[pl.BlockSpec((L, ELT_W), lambda i: (0, OFF_BA // ELT_W)), pl.BlockSpec((8, LANE), lambda i: (0, 0))],
        out_shape=[jax.ShapeDtypeStruct(dproj.shape, dproj.dtype), jax.ShapeDtypeStruct((8, LANE), f32)],
        input_output_aliases={5: 0},
        compiler_params=_cparams("arbitrary"),
    )(proj, alog_p, dtb_p, dsc, dgr_col, dproj, after)


def _input_grad(dproj, wpad, x, nw, dy, after):
    L = x.shape[0]
    tm = min(512, L)
    cuts = (0, 1024, 3072, 5120, 7168, PROJ_PAD)
    nk = len(cuts) - 1

    def body(dp_ref, w_hbm, x_ref, nw_ref, dy_ref, after_ref, gx_ref, gnw_ref, w_vmem, sems):
        first = pl.program_id(0) == 0
        loads = [pltpu.make_async_copy(w_hbm.at[cuts[k]:cuts[k + 1], :], w_vmem.at[cuts[k]:cuts[k + 1], :], sems.at[k])
                 for k in range(nk)]

        @pl.when(first)
        def _():
            for cp in loads:
                cp.start()
            gnw_ref[...] = jnp.zeros_like(gnw_ref)
        dh = None
        for k in range(nk):
            pl.when(first)(loads[k].wait)
            part = jnp.dot(dp_ref[:, cuts[k]:cuts[k + 1]], w_vmem[cuts[k]:cuts[k + 1], :], preferred_element_type=f32)
            dh = part if dh is None else dh + part
        xv, nwv = x_ref[...], nw_ref[...]
        r = lax.rsqrt(jnp.mean(xv * xv, axis=-1, keepdims=True) + EPS)
        xh = xv * r
        gnw_ref[...] += jnp.sum(dh * xh, axis=0, keepdims=True)
        dxh = dh * nwv
        gx_ref[...] = dy_ref[...] + r * (dxh - xh * jnp.mean(dxh * xh, axis=-1, keepdims=True))

    row = lambda i: (i, 0)
    fix = lambda i: (0, 0)
    return _pcall(
        body, name="input_grad", grid=(L // tm,),
        in_specs=[pl.BlockSpec((tm, PROJ_PAD), row), ANY, pl.BlockSpec((tm, D_MODEL), row),
                  pl.BlockSpec((1, D_MODEL), fix), pl.BlockSpec((tm, D_MODEL), row), ANY],
        out_specs=[pl.BlockSpec((tm, D_MODEL), row), pl.BlockSpec((1, D_MODEL), fix)],
        out_shape=[jax.ShapeDtypeStruct((L, D_MODEL), f32), jax.ShapeDtypeStruct((1, D_MODEL), f32)],
        scratch_shapes=[pltpu.VMEM(wpad.shape, bf16), pltpu.SemaphoreType.DMA((nk,))],
        compiler_params=_cparams("arbitrary"),
    )(dproj, wpad, x, nw, dy, after)


def _adamw_reduce(parts, w, m, v, name):
    R, C = w.shape
    n_parts = parts.shape[0]
    tr = 128 if R % 128 == 0 else R
    c1 = 1.0 - ADAM_B1 ** ADAM_STEP
    c2 = 1.0 - ADAM_B2 ** ADAM_STEP

    def body(p_ref, w_ref, m_ref, v_ref, g_ref, d_ref, nm_ref, nv_ref):
        g = p_ref[0].astype(f32)
        for s in range(1, n_parts):
            g = g + p_ref[s].astype(f32)
        nm = ADAM_B1 * m_ref[...] + (1.0 - ADAM_B1) * g
        nv = ADAM_B2 * v_ref[...] + (1.0 - ADAM_B2) * (g * g)
        g_ref[...] = g
        nm_ref[...] = nm
        nv_ref[...] = nv
        d_ref[...] = -ADAM_LR * ((nm / c1) / (jnp.sqrt(nv / c2) + ADAM_EPS) + ADAM_WD * w_ref[...])

    blk = pl.BlockSpec((tr, C), lambda i: (i, 0))
    out = jax.ShapeDtypeStruct((R, C), f32)
    return _pcall(
        body, name=name, grid=(R // tr,),
        in_specs=[pl.BlockSpec((n_parts, tr, C), lambda i: (0, i, 0)), blk, blk, blk],
        out_specs=[blk] * 4, out_shape=[out] * 4,
        compiler_params=_cparams("parallel"),
    )(parts, w, m, v)


SMALL_SLOTS = ((0, D_MODEL), (D_MODEL, D_MODEL), (2 * D_MODEL, D_MODEL), (3 * D_MODEL, LANE),
               (3 * D_MODEL + LANE, HEADS), (3 * D_MODEL + 2 * LANE, HEADS))
SMALL_LOSS = 3 * D_MODEL + 3 * LANE
SMALL_W = SMALL_LOSS + LANE


def _pack_small(gs, after):
    def body(nw_ref, cb_ref, fw_ref, gn_ref, sc_ref, ls_ref, after_ref, o_ref):
        for ref, (start, width) in zip((nw_ref, cb_ref, fw_ref, gn_ref), SMALL_SLOTS[:4]):
            o_ref[:, start:start + width] = ref[...]
        o_ref[:, SMALL_SLOTS[4][0]:SMALL_SLOTS[4][0] + LANE] = sc_ref[0:1, :]
        o_ref[:, SMALL_SLOTS[5][0]:SMALL_SLOTS[5][0] + LANE] = sc_ref[1:2, :]
        o_ref[:, SMALL_LOSS:SMALL_W] = ls_ref[...]

    vm = pl.BlockSpec(memory_space=pltpu.VMEM)
    return _pcall(body, name="pack_small_grads", out_shape=jax.ShapeDtypeStruct((1, SMALL_W), f32),
                  in_specs=[vm] * 6 + [ANY], out_specs=vm)(*gs, after)


def _adamw_small(parts, ws, ms, vs):
    c1 = 1.0 - ADAM_B1 ** ADAM_STEP
    c2 = 1.0 - ADAM_B2 ** ADAM_STEP
    np_ = len(ws)

    def body(*refs):
        p_ref = refs[0]
        w_refs, m_refs, v_refs = refs[1:1 + np_], refs[1 + np_:1 + 2 * np_], refs[1 + 2 * np_:1 + 3 * np_]
        outs = refs[1 + 3 * np_:]
        g_refs, d_refs, nm_refs, nv_refs = (outs[i * np_:(i + 1) * np_] for i in range(4))
        loss_ref = outs[4 * np_]

        def total(start, width):
            t = p_ref[0, :, start:start + width]
            for s in range(1, N_DEV):
                t = t + p_ref[s, :, start:start + width]
            return t

        for i, (start, width) in enumerate(SMALL_SLOTS):
            g = total(start, width)
            nm = ADAM_B1 * m_refs[i][...] + (1.0 - ADAM_B1) * g
            nv = ADAM_B2 * v_refs[i][...] + (1.0 - ADAM_B2) * (g * g)
            g_refs[i][...] = g
            nm_refs[i][...] = nm
            nv_refs[i][...] = nv
            d_refs[i][...] = -ADAM_LR * ((nm / c1) / (jnp.sqrt(nv / c2) + ADAM_EPS) + ADAM_WD * w_refs[i][...])
        loss_ref[...] = total(SMALL_LOSS, LANE)

    vm = pl.BlockSpec(memory_space=pltpu.VMEM)
    shapes = [jax.ShapeDtypeStruct(w.shape, f32) for w in ws]
    res = _pcall(body, name="adamw_small", out_shape=shapes * 4 + [jax.ShapeDtypeStruct((1, LANE), f32)],
                 in_specs=[vm] * (1 + 3 * np_), out_specs=[vm] * (4 * np_ + 1))(parts, *ws, *ms, *vs)
    return [res[i * np_:(i + 1) * np_] for i in range(4)], res[4 * np_]


def _adamw_w_in(part_a, part_b, w3, m3, v3, after):
    _, n, _ = part_a.shape
    c1 = 1.0 - ADAM_B1 ** ADAM_STEP
    c2 = 1.0 - ADAM_B2 ** ADAM_STEP
    rows = 256
    main = n // rows * rows
    tail = n - main

    def step(g, w_ref, m_ref, v_ref, g_ref, d_ref, nm_ref, nv_ref):
        nm = ADAM_B1 * m_ref[:, 0, :] + (1.0 - ADAM_B1) * g
        nv = ADAM_B2 * v_ref[:, 0, :] + (1.0 - ADAM_B2) * (g * g)
        g_ref[:, 0, :] = g
        nm_ref[:, 0, :] = nm
        nv_ref[:, 0, :] = nv
        d_ref[:, 0, :] = -ADAM_LR * ((nm / c1) / (jnp.sqrt(nv / c2) + ADAM_EPS) + ADAM_WD * w_ref[:, 0, :])

    def body(pa_ref, pb_ref, w_ref, m_ref, v_ref, after_ref, *outs):
        step(pa_ref[0].astype(f32) + pb_ref[0].astype(f32), w_ref, m_ref, v_ref, *outs)

    def tail_body(pa_ref, pb_ref, w_ref, m_ref, v_ref, g_in, d_in, nm_in, nv_in, *outs):
        step(pa_ref[0, 0:tail, :].astype(f32) + pb_ref[0, 0:tail, :].astype(f32), w_ref, m_ref, v_ref, *outs)

    out = jax.ShapeDtypeStruct((n, 1, D_MODEL), f32)
    blk = pl.BlockSpec((rows, 1, D_MODEL), lambda i: (i, 0, 0))
    part = pl.BlockSpec((1, rows, D_MODEL), lambda i: (0, i, 0))
    res = _pcall(
        body, name="adamw_w_in", grid=(main // rows,),
        in_specs=[part, part, blk, blk, blk, ANY], out_specs=[blk] * 4, out_shape=[out] * 4,
        compiler_params=_cparams("parallel"),
    )(part_a, part_b, w3, m3, v3, after)
    last = pl.BlockSpec((tail, 1, D_MODEL), lambda i: (main // tail, 0, 0))
    packed = 16
    last_part = pl.BlockSpec((1, packed, D_MODEL), lambda i: (0, main // packed, 0))
    return _pcall(
        tail_body, name="adamw_w_in_tail", grid=(1,),
        in_specs=[last_part, last_part, last, last, last] + [ANY] * 4, out_specs=[last] * 4, out_shape=[out] * 4,
        input_output_aliases={5: 0, 6: 1, 7: 2, 8: 3},
        compiler_params=_cparams("arbitrary"),
    )(part_a, part_b, w3, m3, v3, *res)


def _pad_lanes(vec8, start):
    return jnp.pad(vec8.reshape(1, -1), ((0, 0), (start, LANE - start - vec8.size)))


def kernel(x, norm_in_w, w_in, conv_qkv_w, A_log, dt_bias, gdn_norm_w, conv_w, conv_b, w_out, final_norm_w, loss_target, m_norm_in_w, m_w_in, m_conv_qkv_w, m_A_log, m_dt_bias, m_gdn_norm_w, m_conv_w, m_conv_b, m_w_out, m_final_norm_w, v_norm_in_w, v_w_in, v_conv_qkv_w, v_A_log, v_dt_bias, v_gdn_norm_w, v_conv_w, v_conv_b, v_w_out, v_final_norm_w):
    L = x.shape[1]
    nc = L // CHUNK
    xs = x[0]
    tgt = loss_target[0]
    fnw = final_norm_w.reshape(1, D_MODEL)

    as_rows = lambda a: jnp.transpose(a, (2, 0, 1))
    win_g, cqkv_g, cw_g = _all_gather([_cast_w_in(as_rows(w_in)), conv_qkv_w[0], conv_w[0]], "gather_weights",
                                      pieces=[4, 1, 1])
    wpad = _relayout_w_in(win_g)
    cqkv = jnp.concatenate([cqkv_g[d] for d in range(N_DEV)], axis=1)
    cw = jnp.concatenate([cw_g[d] for d in range(N_DEV)], axis=1)
    alog_p = _pad_lanes(A_log, HEADS)
    dtb_p = _pad_lanes(dt_bias, HEADS)
    tok = lambda started: started[4]
    wo_started = _spread_start(w_out[0].astype(bf16), wpad, "gather", "gather_w_out_start")

    proj, h = _in_proj(xs, norm_in_w, wpad, tok(wo_started))
    qkv = _qkv_act(proj, cqkv)
    sc, gr = _scalars(proj, alog_p, dtb_p)
    o, u_all, w_all, vn_all, t_all, sp_all = _gdn_fwd(qkv, sc, gr)
    mix = _conv_fwd(proj, cw, conv_b, _gdn_gate(o, proj, gdn_norm_w))
    wo = _spread_wait(wo_started, mix, "gather", "gather_w_out_wait")[1].reshape(-1, D_MODEL)
    dy, dyb, dmix, g_fnw, loss_v = _out_proj_loss(xs, mix, wo, fnw, tgt)

    g_wout = _tn_matmul(mix, dyb, "grad_w_out")
    gwo_started = _spread_start(g_wout.reshape(N_DEV, -1, D_MODEL), dyb, "scatter", "exchange_grad_w_out_start")
    do, dproj, g_gnw = _gdn_gate_bwd(o, proj, gdn_norm_w, dmix, tok(gwo_started))
    dproj, g_cw, g_cb = _conv_bwd(proj, cw, conv_b, dmix, dproj)
    dqkv_n, dsc, dgr = _gdn_bwd(qkv, sc, gr, u_all, w_all, vn_all, t_all, sp_all, do)
    dproj, g_cqkv = _qkv_bwd(proj, cqkv, dqkv_n, dproj)
    g_cqkv_blk = g_cqkv.reshape(4, N_DEV, -1).transpose(1, 0, 2)
    g_cw_blk = jnp.pad(g_cw.reshape(3, N_DEV, -1).transpose(1, 0, 2),
                       ((0, 0), (0, 1), (0, g_cqkv_blk.shape[2] - g_cw.shape[1] // N_DEV)))
    gsm_started = _spread_start(jnp.concatenate([g_cqkv_blk, g_cw_blk], axis=1), g_cqkv, "scatter",
                                "exchange_small_sharded_grads_start")
    dgr_col = jnp.pad(dgr.transpose(0, 2, 1).reshape(L, HEADS), ((0, 0), (HEADS, LANE - 2 * HEADS)))
    dproj, g_sc = _scalars_bwd(proj, alog_p, dtb_p, dsc, dgr_col, dproj, tok(gsm_started))
    g_win_blk = _grad_blocks(_tn_matmul(dproj, h, "grad_w_in"))

    (p_win,) = _pair_exchange([g_win_blk], "exchange_grads_pair")
    r_small = _spread_wait(gsm_started, p_win, "scatter", "exchange_small_sharded_grads_wait")[1]
    r_cqkv, r_cw = r_small[:, :4, :], r_small[:, 4:7, :g_cw.shape[1] // N_DEV]
    s_win = _pair_sum(g_win_blk, p_win, "pair_sum_w_in")
    gw1_started = _spread_start(s_win, r_small, "axis_a", "exchange_grads_axis1_start")
    grad_x, g_nw = _input_grad(dproj, wpad, xs, norm_in_w, dy, tok(gw1_started))
    s_thru, got1 = _spread_wait(gw1_started, grad_x, "axis_a", "exchange_grads_axis1_wait")
    t_win = _axis_sum(s_thru, got1, "axis_sum_w_in")
    gw2_started = _spread_start(t_win, got1, "axis_b", "exchange_grads_axis2_start")

    r_wout = _spread_wait(gwo_started, tok(gw2_started), "scatter", "exchange_grad_w_out_wait")[1]
    upd_wout =_adamw_reduce(r_wout, w_out[0], m_w_out[0], v_w_out[0], "adamw_w_out")
    upd_cqkv = _adamw_reduce(r_cqkv, conv_qkv_w[0], m_conv_qkv_w[0], v_conv_qkv_w[0], "adamw_conv_qkv_w")
    upd_cw = _adamw_reduce(r_cw, conv_w[0], m_conv_w[0], v_conv_w[0], "adamw_conv_w")

    t_thru, got2 = _spread_wait(gw2_started, upd_cw[0], "axis_b", "exchange_grads_axis2_wait")

    small_g = _pack_small([g_nw, g_cb, g_fnw, g_gnw, g_sc, loss_v], got2)
    gsg_started = _spread_start(small_g, got2, "gather", "gather_small_grads_start")
    upd_win_t = _adamw_w_in(t_thru, got2, as_rows(w_in), as_rows(m_w_in), as_rows(v_w_in), tok(gsg_started))
    upd_win = [jnp.transpose(a, (1, 2, 0)) for a in upd_win_t]
    small_all = _spread_wait(gsg_started, upd_win_t[0], "gather", "gather_small_grads_wait")[1]
    fvec = lambda a: a.reshape(1, D_MODEL)
    upd_small, loss_sum = _adamw_small(
        small_all,
        [norm_in_w, conv_b, fvec(final_norm_w), gdn_norm_w, A_log, dt_bias],
        [m_norm_in_w, m_conv_b, fvec(m_final_norm_w), m_gdn_norm_w, m_A_log, m_dt_bias],
        [v_norm_in_w, v_conv_b, fvec(v_final_norm_w), v_gdn_norm_w, v_A_log, v_dt_bias])

    outs = [loss_sum[0, 0], grad_x[None]]
    for k in range(4):
        nw_k, cb_k, fw_k, gn_k, al_k, dt_k = upd_small[k]
        outs += [nw_k, upd_win[k], upd_cqkv[k][None], al_k, dt_k, gn_k,
                 upd_cw[k][None], cb_k, upd_wout[k][None], fw_k.reshape(D_MODEL)]
    return tuple(outs)
```

```python
import jax
import jax.numpy as jnp
from jax import lax
from jax.experimental import pallas as pl
from jax.experimental.pallas import tpu as pltpu

f32 = jnp.float32
bf16 = jnp.bfloat16

N_DEV = 8
D_MODEL = 1024
HEADS = 8
HEAD_DIM = 128
CHUNK = 64
GDN_CPS = 4
GDN_CPS_BWD = 1
GDN_WIDTH = HEADS * HEAD_DIM
CONV_WIDTH = 1024
PROJ_WIDTH = 8208
SHARD_W = PROJ_WIDTH // N_DEV
EPS = 1e-6

LANE = 128
ELT_W = 256

OFF_QKV, OFF_ZG, OFF_CONV, OFF_BA = 0, 3072, 4096, 8192
CONV_BLOCK = 4 * ELT_W
PROJ_PAD = 8448
NAT_BA, NAT_CONV = 4096, 4112


def _padded_col(n):
    if n < NAT_BA:
        return n
    if n < NAT_CONV:
        return OFF_BA + n - NAT_BA
    g, ch = divmod(n - NAT_CONV, CONV_WIDTH)
    j, r = divmod(ch, ELT_W)
    return OFF_CONV + CONV_BLOCK * j + ELT_W * g + r


def _layout_segments(n0, n1):
    cuts = [NAT_BA, NAT_CONV] + [NAT_CONV + ELT_W * k for k in range(1, 4 * CONV_WIDTH // ELT_W)]
    pts = [n0] + [c for c in cuts if n0 < c < n1] + [n1]
    return [(lo, hi - lo, _padded_col(lo)) for lo, hi in zip(pts, pts[1:])]

ADAM_LR, ADAM_B1, ADAM_B2, ADAM_EPS, ADAM_WD, ADAM_STEP = 0.001, 0.9, 0.999, 1e-08, 0.01, 10

V7X_VMEM_BYTES = 64 * 1024 * 1024
VMEM_LIMIT = V7X_VMEM_BYTES - 8 * 1024 * 1024

MESH = pl.DeviceIdType.MESH
ANY = pl.BlockSpec(memory_space=pl.ANY)


def _pcall(body, **kw):
    return pl.pallas_call(body, **kw)


def _cparams(*sem):
    return pltpu.CompilerParams(dimension_semantics=sem if sem else None, vmem_limit_bytes=VMEM_LIMIT)


def _mm(a, b):
    return jnp.dot(a.astype(bf16), b.astype(bf16), preferred_element_type=f32)


def _mm_nt(a, b):
    return lax.dot_general(a.astype(bf16), b.astype(bf16), (((1,), (1,)), ((), ())), preferred_element_type=f32)


def _cat16(parts, axis):
    return jnp.concatenate([p.astype(bf16) for p in parts], axis=axis)


def _mm_tn(a, b):
    return lax.dot_general(a.astype(bf16), b.astype(bf16), (((0,), (0,)), ((), ())), preferred_element_type=f32)


def _rows(shape):
    return lax.broadcasted_iota(jnp.int32, shape, 0)


def _lanes(shape):
    return lax.broadcasted_iota(jnp.int32, shape, 1)


def _shift_down(x, s):
    if s == 0:
        return x
    return jnp.where(_rows(x.shape) >= s, pltpu.roll(x, s, 0), 0.0)


def _shift_up(x, s):
    if s == 0:
        return x
    n = x.shape[0]
    return jnp.where(_rows(x.shape) < n - s, pltpu.roll(x, n - s, 0), 0.0)


def _sigmoid(x):
    return jax.nn.sigmoid(x)


def _softplus(x):
    e = jnp.exp(-jnp.abs(x))
    small = e * (1.0 - e * (0.5 - e * (1.0 / 3.0)))
    return jnp.maximum(x, 0.0) + jnp.where(e < 0.01, small, jnp.log(1.0 + e))


def _mesh_pos():
    return lax.axis_index("x"), lax.axis_index("y"), lax.axis_index("c")


def _flat(px, py, pc):
    return 4 * px + 2 * py + pc


def _all_gather(xs, name, pieces=None):
    n = len(xs)
    pieces = pieces or [1] * n
    items = [(a, q) for a in range(n) for q in range(pieces[a])]
    ni = len(items)

    def view(ref, i):
        a, q = items[i]
        if pieces[a] == 1:
            return ref
        wd = xs[a].shape[-1] // pieces[a]
        return ref.at[(slice(None),) * (xs[a].ndim - 1) + (pl.ds(q * wd, wd),)]

    def body(*refs):
        x_refs, o_refs = refs[:n], refs[n:2 * n]
        send_sems, recv_sems, local_sems = refs[2 * n:]
        x, y, c = _mesh_pos()
        me, sibling = (x, y, c), (x, y, 1 - c)
        flip = lambda v, bit: v + bit - 2 * v * bit
        nbr_a = (flip(x, 1 - c), flip(y, c))
        nbr_b = (flip(x, c), flip(y, 1 - c))
        diag = (1 - x, 1 - y)

        def copy(i, k, block, to, own=False):
            a = items[i][0]
            dst = view(o_refs[a].at[_flat(*block)], i)
            return pltpu.make_async_remote_copy(
                src_ref=view(x_refs[a], i) if own else dst, dst_ref=dst,
                send_sem=send_sems.at[i, k], recv_sem=recv_sems.at[i, k], device_id=to, device_id_type=MESH)

        mine, sent = [], []

        def go(cp):
            cp.start()
            sent.append(cp)

        for a in range(n):
            cp = pltpu.make_async_copy(x_refs[a], o_refs[a].at[_flat(*me)], local_sems.at[a])
            cp.start()
            mine.append(cp)
        for a in range(ni):
            go(copy(a, 1, me, (*nbr_a, c), own=True))
            go(copy(a, 2, me, (*nbr_b, c), own=True))
            go(copy(a, 0, me, sibling, own=True))
        for a in range(ni):
            copy(a, 1, (*nbr_a, c), me).wait_recv()
            go(copy(a, 3, (*nbr_a, c), (*nbr_b, c)))
            go(copy(a, 4, (*nbr_a, c), sibling))
        for a in range(ni):
            copy(a, 2, (*nbr_b, c), me).wait_recv()
            go(copy(a, 5, (*nbr_b, c), sibling))
        for a in range(ni):
            copy(a, 3, (*diag, c), me).wait_recv()
            go(copy(a, 6, (*diag, c), sibling))
        for a in range(ni):
            copy(a, 0, sibling, me).wait_recv()
            copy(a, 4, (*nbr_b, 1 - c), me).wait_recv()
            copy(a, 5, (*nbr_a, 1 - c), me).wait_recv()
            copy(a, 6, (*diag, 1 - c), me).wait_recv()
        for cp in sent:
            cp.wait_send()
        for cp in mine:
            cp.wait()

    outs = _pcall(
        body, name=name,
        out_shape=[jax.ShapeDtypeStruct((N_DEV,) + a.shape, a.dtype) for a in xs],
        in_specs=[ANY] * n, out_specs=[ANY] * n,
        scratch_shapes=[pltpu.SemaphoreType.DMA((ni, 7)), pltpu.SemaphoreType.DMA((ni, 7)), pltpu.SemaphoreType.DMA((n,))],
    )(*xs)
    return list(outs)


def _pair_exchange(gs, name):
    n = len(gs)
    chips = [(0, 0), (0, 1), (1, 0), (1, 1)]

    def body(*refs):
        g_refs, o_refs = refs[:n], refs[n:2 * n]
        send_sems, recv_sems = refs[2 * n:]
        x, y, c = _mesh_pos()
        sibling = (x, y, 1 - c)

        def copy(a, i):
            xp, yp = chips[i]
            return pltpu.make_async_remote_copy(
                src_ref=g_refs[a].at[_flat(xp, yp, 1 - c)], dst_ref=o_refs[a].at[i],
                send_sem=send_sems.at[a, i], recv_sem=recv_sems.at[a, i], device_id=sibling, device_id_type=MESH)

        cps = [copy(a, i) for a in range(n) for i in range(4)]
        for cp in cps:
            cp.start()
        for cp in cps:
            cp.wait()

    outs = _pcall(
        body, name=name,
        out_shape=[jax.ShapeDtypeStruct((4,) + a.shape[1:], a.dtype) for a in gs],
        in_specs=[ANY] * n, out_specs=[ANY] * n,
        scratch_shapes=[pltpu.SemaphoreType.DMA((n, 4)), pltpu.SemaphoreType.DMA((n, 4))],
    )(*gs)
    return list(outs)


def _pair_sum(g, p1, name):
    _, R, C = g.shape
    tr = 256 if R % 256 == 0 else R
    cidx = lax.axis_index("c").astype(jnp.int32).reshape(1)

    def body(c_ref, g_ref, p_ref, o_ref):
        o_ref[...] = (g_ref[...].astype(f32) + p_ref[...].astype(f32)).astype(o_ref.dtype)

    return _pcall(
        body, name=name,
        grid_spec=pltpu.PrefetchScalarGridSpec(
            num_scalar_prefetch=1, grid=(4, R // tr),
            in_specs=[pl.BlockSpec((1, tr, C), lambda i, r, c_ref: (2 * i + c_ref[0], r, 0)),
                      pl.BlockSpec((1, tr, C), lambda i, r, c_ref: (i, r, 0))],
            out_specs=pl.BlockSpec((1, tr, C), lambda i, r, c_ref: (i, r, 0))),
        out_shape=jax.ShapeDtypeStruct((4, R, C), g.dtype),
        compiler_params=_cparams("parallel", "parallel"),
    )(cidx, g, p1)


def _axis_sum(s, got, name):
    _, R, C = s.shape
    x, y, c = _mesh_pos()
    me, _, b, _ = _axis_chips(x, y, c)
    idx = jnp.stack([2 * me[0] + me[1], 2 * b[0] + b[1]]).astype(jnp.int32)

    def body(idx_ref, s_ref, g_ref, o_ref):
        o_ref[...] = (s_ref[...].astype(f32) + g_ref[...].astype(f32)).astype(o_ref.dtype)

    return _pcall(
        body, name=name,
        grid_spec=pltpu.PrefetchScalarGridSpec(
            num_scalar_prefetch=1, grid=(2,),
            in_specs=[pl.BlockSpec((1, R, C), lambda k, idx_ref: (idx_ref[k], 0, 0)),
                      pl.BlockSpec((1, R, C), lambda k, idx_ref: (k, 0, 0))],
            out_specs=pl.BlockSpec((1, R, C), lambda k, idx_ref: (k, 0, 0))),
        out_shape=jax.ShapeDtypeStruct((2, R, C), s.dtype),
        compiler_params=_cparams("parallel"),
    )(idx, s, got)


HBM = pl.BlockSpec(memory_space=pltpu.HBM)
SEM = pl.BlockSpec(memory_space=pltpu.SEMAPHORE)
EFFECT = pltpu.SideEffectType.DATAFLOW_SIDE_EFFECTING


def _peers(x, y, c):
    out = []
    for k in range(1, N_DEV):
        kx, ky, kc = (k >> 2) & 1, (k >> 1) & 1, k & 1
        out.append(((1 - x) if kx else x, (1 - y) if ky else y, (1 - c) if kc else c))
    return out


SPREAD_COPIES = {"gather": N_DEV - 1, "scatter": N_DEV - 1, "axis_a": 2, "axis_b": 1}
SPREAD_SLOTS = {"axis_a": 2, "axis_b": 1}


def _axis_chips(x, y, c):
    flip = lambda v, bit: v + bit - 2 * v * bit
    return (x, y), (flip(x, 1 - c), flip(y, c)), (flip(x, c), flip(y, 1 - c)), (1 - x, 1 - y)


def _spread_copy(src_ref, land_ref, send_sems, recv_sems, k, plan):
    x, y, c = _mesh_pos()
    if plan in ("axis_a", "axis_b"):
        _, a, b, d = _axis_chips(x, y, c)
        chip = lambda p: 2 * p[0] + p[1]
        peer = (*(a if plan == "axis_a" else b), c)
        src = src_ref.at[chip(a) if k == 0 else chip(d)] if plan == "axis_a" else src_ref.at[1]
        slot = k
    else:
        peer = _peers(x, y, c)[k]
        src, slot = (src_ref.at[_flat(*peer)] if plan == "scatter" else src_ref), _flat(x, y, c)
    return pltpu.make_async_remote_copy(
        src_ref=src, dst_ref=land_ref.at[slot], send_sem=send_sems.at[k], recv_sem=recv_sems.at[k],
        device_id=peer, device_id_type=MESH)


def _own_copy(src_ref, land_ref, send_sems, plan):
    me = _flat(*_mesh_pos())
    return pltpu.make_async_copy(src_ref.at[me] if plan == "scatter" else src_ref, land_ref.at[me],
                                 send_sems.at[SPREAD_COPIES[plan]])


def _spread_start(src, after, plan, name):
    land_shape = (N_DEV,) + src.shape if plan == "gather" else src.shape
    if plan in SPREAD_SLOTS:
        land_shape = (SPREAD_SLOTS[plan],) + src.shape[1:]
    n_copies = SPREAD_COPIES[plan]

    def body(src_ref, land_ref, after_ref, send_sems, recv_sems, src_thru, land_thru, token):
        for k in range(n_copies):
            _spread_copy(src_ref, land_ref, send_sems, recv_sems, k, plan).start()
        if plan not in SPREAD_SLOTS:
            _own_copy(src_ref, land_ref, send_sems, plan).start()
        token[...] = jnp.zeros_like(token)

    return _pcall(
        body, name=name,
        out_shape=(pltpu.SemaphoreType.DMA((n_copies + (plan not in SPREAD_SLOTS),)), pltpu.SemaphoreType.DMA((n_copies,)),
                   pltpu.HBM(src.shape, src.dtype), pltpu.HBM(land_shape, src.dtype), jax.ShapeDtypeStruct((8, LANE), f32)),
        in_specs=(HBM, HBM, ANY), out_specs=(SEM, SEM, HBM, HBM, pl.BlockSpec(memory_space=pltpu.VMEM)),
        input_output_aliases={0: 2, 1: 3},
        compiler_params=pltpu.CompilerParams(has_side_effects=EFFECT),
    )(pltpu.with_memory_space_constraint(src, pltpu.HBM),
      pltpu.with_memory_space_constraint(lax.empty(land_shape, src.dtype), pltpu.HBM), after)


def _spread_wait(started, after, plan, name):
    send_sems, recv_sems, src_thru, land_thru, _ = started

    def body(src_ref, land_ref, send_sems, recv_sems, after_ref, src_dead, got_ref):
        for k in range(SPREAD_COPIES[plan]):
            cp = _spread_copy(src_ref, land_ref, send_sems, recv_sems, k, plan)
            cp.wait_send()
            cp.wait_recv()
        if plan not in SPREAD_SLOTS:
            _own_copy(src_ref, land_ref, send_sems, plan).wait()

    return _pcall(
        body, name=name,
        out_shape=(pltpu.HBM(src_thru.shape, src_thru.dtype), pltpu.HBM(land_thru.shape, land_thru.dtype)),
        in_specs=(HBM, HBM, SEM, SEM, ANY), out_specs=(HBM, HBM), input_output_aliases={0: 0, 1: 1},
        compiler_params=pltpu.CompilerParams(has_side_effects=EFFECT),
    )(src_thru, land_thru, send_sems, recv_sems, after)


COL_TILE = 256


def _cast_w_in(w3):
    n = w3.shape[0]

    def body(w_ref, o_ref):
        o_ref[...] = w_ref[:, 0, :].astype(bf16)

    tile = 2 * COL_TILE
    return _pcall(
        body, name="cast_w_in", grid=(D_MODEL // tile,),
        in_specs=[pl.BlockSpec((n, 1, tile), lambda j: (0, 0, j))],
        out_specs=pl.BlockSpec((n, tile), lambda j: (0, j)),
        out_shape=jax.ShapeDtypeStruct((n, D_MODEL), bf16),
        compiler_params=_cparams("parallel"),
    )(w3)


def _relayout_w_in(win_g):
    def body(g_ref, o_ref):
        used = OFF_BA + NAT_CONV - NAT_BA
        o_ref[used:PROJ_PAD, :] = jnp.zeros((PROJ_PAD - used, COL_TILE), o_ref.dtype)
        for d in range(N_DEV):
            for lo, width, dst in _layout_segments(d * SHARD_W, (d + 1) * SHARD_W):
                src = lo - d * SHARD_W
                o_ref[dst:dst + width, :] = g_ref[d, src:src + width, :]

    return _pcall(
        body, name="relayout_w_in", grid=(D_MODEL // COL_TILE,),
        in_specs=[pl.BlockSpec((N_DEV, SHARD_W, COL_TILE), lambda j: (0, 0, j))],
        out_specs=pl.BlockSpec((PROJ_PAD, COL_TILE), lambda j: (0, j)),
        out_shape=jax.ShapeDtypeStruct((PROJ_PAD, D_MODEL), win_g.dtype),
        compiler_params=_cparams("parallel"),
    )(win_g)


def _grad_blocks(g_t):
    def body(p_ref, o_ref):
        for d in range(N_DEV):
            for lo, width, src in _layout_segments(d * SHARD_W, (d + 1) * SHARD_W):
                dst = lo - d * SHARD_W
                o_ref[d, dst:dst + width, :] = p_ref[src:src + width, :]

    return _pcall(
        body, name="grad_blocks", grid=(D_MODEL // COL_TILE,),
        in_specs=[pl.BlockSpec((PROJ_PAD, COL_TILE), lambda j: (0, j))],
        out_specs=pl.BlockSpec((N_DEV, SHARD_W, COL_TILE), lambda j: (0, 0, j)),
        out_shape=jax.ShapeDtypeStruct((N_DEV, SHARD_W, D_MODEL), bf16),
        compiler_params=_cparams("parallel"),
    )(g_t)


def _in_proj(x, nw, wpad_t, after):
    L = x.shape[0]
    tn = 768
    nj = wpad_t.shape[0] // tn

    def body(x_ref, nw_ref, w_ref, after_ref, proj_ref, h_ref):
        first = pl.program_id(0) == 0

        def project(r, n, hv):
            proj_ref[r:r + n, :] = lax.dot_general(hv, w_ref[...], (((1,), (1,)), ((), ())), preferred_element_type=f32)

        @pl.when(first)
        def _():
            for r in range(0, L, 256):
                xs = x_ref[r:r + 256, :]
                ms = jnp.mean(xs * xs, axis=-1, keepdims=True)
                hv = ((xs * lax.rsqrt(ms + EPS)) * nw_ref[...]).astype(bf16)
                h_ref[r:r + 256, :] = hv
                project(r, 256, hv)

        @pl.when(jnp.logical_not(first))
        def _():
            for r in range(0, L, 512):
                project(r, 512, h_ref[r:r + 512, :])

    return _pcall(
        body, name="in_proj", grid=(nj,),
        in_specs=[pl.BlockSpec((L, D_MODEL), lambda j: (0, 0)), pl.BlockSpec((1, D_MODEL), lambda j: (0, 0)),
                  pl.BlockSpec((tn, D_MODEL), lambda j: (j, 0)), ANY],
        out_specs=[pl.BlockSpec((L, tn), lambda j: (0, j)), pl.BlockSpec((L, D_MODEL), lambda j: (0, 0))],
        out_shape=[jax.ShapeDtypeStruct((L, wpad_t.shape[0]), f32), jax.ShapeDtypeStruct((L, D_MODEL), bf16)],
        compiler_params=_cparams("arbitrary"),
    )(x, nw, wpad_t, after)


HALVES = [slice(i * LANE, (i + 1) * LANE) for i in range(ELT_W // LANE)]
QKV_W = 512
QKV_HEADS = [slice(i * LANE, (i + 1) * LANE) for i in range(QKV_W // LANE)]
STEPS_PER_GROUP = GDN_WIDTH // QKV_W


def _conv4(x, cw_ref, ls):
    return (cw_ref[3:4, ls] * x + cw_ref[2:3, ls] * _shift_down(x, 1) + cw_ref[1:2, ls] * _shift_down(x, 2)
            + cw_ref[0:1, ls] * _shift_down(x, 3))


def _qkv_act(proj, cw):
    L = proj.shape[0]

    def body(x_ref, cw_ref, o_ref):
        j = pl.program_id(0)
        scale = jnp.where(j < STEPS_PER_GROUP, HEAD_DIM ** -0.5, 1.0).astype(f32)
        for ls in QKV_HEADS:
            c = _conv4(x_ref[:, ls], cw_ref, ls)
            a = c * _sigmoid(c)
            rn = lax.rsqrt(jnp.sum(a * a, axis=1, keepdims=True) + EPS)
            o_ref[:, ls] = jnp.where(j < 2 * STEPS_PER_GROUP, (a * rn) * scale, a)

    return _pcall(
        body, name="qkv_act", grid=(3 * STEPS_PER_GROUP,),
        in_specs=[pl.BlockSpec((L, QKV_W), lambda j: (0, j)), pl.BlockSpec((4, QKV_W), lambda j: (0, j))],
        out_specs=pl.BlockSpec((L, QKV_W), lambda j: (0, j)),
        out_shape=jax.ShapeDtypeStruct((L, 3 * GDN_WIDTH), f32),
        compiler_params=_cparams("parallel"),
    )(proj, cw)


def _scalars(proj, alog_p, dtb_p):
    L = proj.shape[0]
    nc = L // CHUNK

    def body(x_ref, al_ref, dt_ref, sc_ref, gr_ref):
        x = x_ref[...]
        lane = _lanes(x.shape)
        beta = _sigmoid(x)
        g = -jnp.exp(al_ref[...]) * _softplus(x + dt_ref[...])
        gc = jnp.where((lane >= HEADS) & (lane < 2 * HEADS), g, 0.0)
        rc = _rows(x.shape) & (CHUNK - 1)
        for s in (1, 2, 4, 8, 16, 32):
            gc = gc + jnp.where(rc >= s, pltpu.roll(gc, s, 0), 0.0)
        sc_ref[...] = jnp.where(lane < HEADS, beta, gc)
        sel = (_lanes((HEADS, LANE)) == _rows((HEADS, LANE)) + HEADS).astype(f32)
        for c in range(nc):
            gr_ref[c] = lax.dot_general(sel, sc_ref[c * CHUNK:(c + 1) * CHUNK, :], (((1,), (1,)), ((), ())),
                                        preferred_element_type=f32, precision=lax.Precision.HIGHEST)

    return _pcall(
        body, name="scalars", grid=(1,),
        in_specs=[pl.BlockSpec((L, LANE), lambda i: (0, OFF_BA // LANE)), pl.BlockSpec((1, LANE), lambda i: (0, 0)),
                  pl.BlockSpec((1, LANE), lambda i: (0, 0))],
        out_specs=[pl.BlockSpec((L, LANE), lambda i: (0, 0)), pl.BlockSpec((nc, HEADS, CHUNK), lambda i: (0, 0, 0))],
        out_shape=[jax.ShapeDtypeStruct((L, LANE), f32), jax.ShapeDtypeStruct((nc, HEADS, CHUNK), f32)],
        compiler_params=_cparams("arbitrary"),
    )(proj, alog_p, dtb_p)


def _head_scalars(sc, gr_ref, h, ci=0):
    lane = _lanes(sc.shape)
    beta = jnp.sum(jnp.where(lane == h, sc, 0.0), axis=1, keepdims=True)
    gcc = jnp.sum(jnp.where(lane == HEADS + h, sc, 0.0), axis=1, keepdims=True)
    gcr = gr_ref[ci, h:h + 1, :]
    gl = jnp.sum(jnp.where(_lanes(gcr.shape) == CHUNK - 1, gcr, 0.0), axis=1, keepdims=True)
    ii, jj = _rows((CHUNK, CHUNK)), _lanes((CHUNK, CHUNK))
    dmat = jnp.where(ii >= jj, jnp.exp(jnp.minimum(gcc - gcr, 0.0)), 0.0)
    dmat_t = jnp.where(jj >= ii, jnp.exp(jnp.minimum(gcr - gcc, 0.0)), 0.0)
    return beta, gcc, gl, dmat, dmat_t, ii, jj


def _gdn_fwd(qkv, sc, gr):
    L = qkv.shape[0]
    nc = L // CHUNK
    W = GDN_WIDTH
    cps = GDN_CPS if nc % GDN_CPS == 0 else 1
    rows_per_step = cps * CHUNK

    def body(qkv_ref, sc_ref, gr_ref, o_ref, u_ref, w_ref, vn_ref, t_ref, sp_ref, s_scr):
        @pl.when(pl.program_id(0) == 0)
        def _():
            s_scr[...] = jnp.zeros_like(s_scr)
        HS = range(cps * HEADS)
        hd = [i % HEADS for i in HS]
        rs = [slice((i // HEADS) * CHUNK, (i // HEADS + 1) * CHUNK) for i in HS]
        cs = [slice(hd[i] * HEAD_DIM, (hd[i] + 1) * HEAD_DIM) for i in HS]
        q = [qkv_ref[rs[i], hd[i] * HEAD_DIM:(hd[i] + 1) * HEAD_DIM] for i in HS]
        k = [qkv_ref[rs[i], W + hd[i] * HEAD_DIM:W + (hd[i] + 1) * HEAD_DIM] for i in HS]
        v = [qkv_ref[rs[i], 2 * W + hd[i] * HEAD_DIM:2 * W + (hd[i] + 1) * HEAD_DIM] for i in HS]
        hsc = [_head_scalars(sc_ref[rs[i], :], gr_ref, hd[i], i // HEADS) for i in HS]
        beta, gcc, gl, dmat = ([x[i] for x in hsc] for i in range(4))
        ii, jj = hsc[0][5], hsc[0][6]
        eg = [jnp.exp(gcc[h]) for h in HS]
        kb = [k[h] * beta[h] for h in HS]
        kk = [_mm_nt(kb[h], k[h]) for h in HS]
        qk = [_mm_nt(q[h], k[h]) for h in HS]
        n0 = [-jnp.where(ii > jj, kk[h] * dmat[h], 0.0) for h in HS]
        n1 = [_mm(n0[h], n0[h]) for h in HS]
        n2 = [_mm(n1[h], n1[h]) for h in HS]
        p01 = [n0[h] + n1[h] + _mm(n0[h], n1[h]) for h in HS]
        n3 = [_mm(n2[h], n2[h]) for h in HS]
        n4 = [_mm(n3[h], n3[h]) for h in HS]
        p23 = [n2[h] + n3[h] + _mm(n2[h], n3[h]) for h in HS]
        n5 = [_mm(n4[h], n4[h]) for h in HS]
        p03 = [p01[h] + p23[h] + _mm(p01[h], p23[h]) for h in HS]
        p45 = [n4[h] + n5[h] + _mm(n4[h], n5[h]) for h in HS]
        t = [p03[h] + p45[h] + _mm(p03[h], p45[h]) for h in HS]
        vb = [v[h] * beta[h] for h in HS]
        kbg = [kb[h] * eg[h] for h in HS]
        uw = [_mm(t[h], _cat16([vb[h], kbg[h]], 1)) for h in HS]
        u = [vb[h] + uw[h][:, :HEAD_DIM] for h in HS]
        w = [kbg[h] + uw[h][:, HEAD_DIM:] for h in HS]
        wq = [_cat16([w[h], q[h] * eg[h]], 0) for h in HS]
        p = [jnp.where(ii >= jj, qk[h] * dmat[h], 0.0) for h in HS]
        ks = [k[h] * jnp.exp(gl[h] - gcc[h]) for h in HS]
        s = [s_scr[h] for h in range(HEADS)]
        for ci in range(cps):
            IS = range(ci * HEADS, (ci + 1) * HEADS)
            ws = [_mm(wq[i], s[hd[i]]) for i in IS]
            vn = [u[i] - ws[hd[i]][:CHUNK] for i in IS]
            pv = [_mm(p[i], vn[hd[i]]) for i in IS]
            kv = [_mm_tn(ks[i], vn[hd[i]]) for i in IS]
            for i in IS:
                h = hd[i]
                sp_ref[ci, cs[i], :] = s[h]
                o_ref[rs[i], cs[i]] = ws[h][CHUNK:] + pv[h]
                vn_ref[rs[i], cs[i]] = vn[h].astype(bf16)
            s = [jnp.exp(gl[i]) * s[hd[i]] + kv[hd[i]] for i in IS]
        for h in range(HEADS):
            s_scr[h] = s[h]
        for i in HS:
            u_ref[rs[i], cs[i]] = u[i].astype(bf16)
            w_ref[rs[i], cs[i]] = w[i].astype(bf16)
            t_ref[i // HEADS, hd[i]] = t[i].astype(bf16)

    row = lambda c: (c, 0)
    act, act16 = jax.ShapeDtypeStruct((L, W), f32), jax.ShapeDtypeStruct((L, W), bf16)
    return _pcall(
        body, name="gdn_fwd", grid=(nc // cps,),
        in_specs=[pl.BlockSpec((rows_per_step, 3 * W), row), pl.BlockSpec((rows_per_step, LANE), row),
                  pl.BlockSpec((cps, HEADS, CHUNK), lambda c: (c, 0, 0))],
        out_specs=[pl.BlockSpec((rows_per_step, W), row)] * 4 + [
            pl.BlockSpec((cps, HEADS, CHUNK, CHUNK), lambda c: (c, 0, 0, 0)),
            pl.BlockSpec((cps, W, HEAD_DIM), lambda c: (c, 0, 0))],
        out_shape=[act, act16, act16, act16, jax.ShapeDtypeStruct((nc, HEADS, CHUNK, CHUNK), bf16),
                   jax.ShapeDtypeStruct((nc, W, HEAD_DIM), f32)],
        scratch_shapes=[pltpu.VMEM((HEADS, HEAD_DIM, HEAD_DIM), f32)],
        compiler_params=_cparams("arbitrary"),
    )(qkv, sc, gr)


def _gdn_gate(o, proj, gnw):
    L = o.shape[0]

    def body(o_ref, z_ref, w_ref, m_ref):
        for ls in HALVES:
            ov, z = o_ref[:, ls], z_ref[:, ls]
            rms = lax.rsqrt(jnp.mean(ov * ov, axis=-1, keepdims=True) + EPS)
            m_ref[:, ls] = (((ov * rms) * w_ref[...]) * (z * _sigmoid(z))).astype(bf16)

    return _pcall(
        body, name="gdn_gate", grid=(GDN_WIDTH // ELT_W,),
        in_specs=[pl.BlockSpec((L, ELT_W), lambda j: (0, j)), pl.BlockSpec((L, ELT_W), lambda j: (0, OFF_ZG // ELT_W + j)),
                  pl.BlockSpec((1, LANE), lambda j: (0, 0))],
        out_specs=pl.BlockSpec((L, ELT_W), lambda j: (0, j)),
        out_shape=jax.ShapeDtypeStruct((L, GDN_WIDTH + CONV_WIDTH), bf16),
        compiler_params=_cparams("parallel"),
    )(o, proj, gnw)


def _conv3(u, cw_ref, ls):
    return cw_ref[2:3, ls] * u + cw_ref[1:2, ls] * _shift_down(u, 1) + cw_ref[0:1, ls] * _shift_down(u, 2)


def _conv_specs(L):
    return [pl.BlockSpec((L, CONV_BLOCK), lambda j: (0, OFF_CONV // CONV_BLOCK + j)),
            pl.BlockSpec((3, ELT_W), lambda j: (0, j)), pl.BlockSpec((1, ELT_W), lambda j: (0, j))]


def _conv_parts(ls):
    return [slice(g * ELT_W + ls.start, g * ELT_W + ls.stop) for g in range(4)]


def _conv_fwd(proj, cw, cb, mix):
    L = proj.shape[0]

    def body(p_ref, cw_ref, cb_ref, mix_in, m_ref):
        for ls in HALVES:
            sb, sc_, sh, sz = _conv_parts(ls)
            z = p_ref[:, sz]
            cv = _conv3(p_ref[:, sc_] * p_ref[:, sh], cw_ref, ls) + cb_ref[:, ls]
            m_ref[:, ls] = ((p_ref[:, sb] * cv) * (z * _sigmoid(z))).astype(bf16)

    return _pcall(
        body, name="conv_fwd", grid=(CONV_WIDTH // ELT_W,),
        in_specs=_conv_specs(L) + [ANY], out_specs=pl.BlockSpec((L, ELT_W), lambda j: (0, GDN_WIDTH // ELT_W + j)),
        out_shape=jax.ShapeDtypeStruct(mix.shape, mix.dtype), input_output_aliases={3: 0},
        compiler_params=_cparams("parallel"),
    )(proj, cw, cb, mix)


def _out_proj_loss(x, mix, wo, fw, tgt):
    L = x.shape[0]
    tm = min(512, L)
    MW = GDN_WIDTH + CONV_WIDTH

    def body(x_ref, m_ref, wo_ref, fw_ref, t_ref, dy_ref, dyb_ref, dm_ref, gfw_ref, loss_ref):
        @pl.when(pl.program_id(0) == 0)
        def _():
            gfw_ref[...] = jnp.zeros_like(gfw_ref)
            loss_ref[...] = jnp.zeros_like(loss_ref)
        y = x_ref[...] + jnp.dot(m_ref[...], wo_ref[...], preferred_element_type=f32)
        r = lax.rsqrt(jnp.mean(y * y, axis=-1, keepdims=True) + EPS)
        yh = y * r
        fwv = fw_ref[...]
        diff = yh * fwv - t_ref[...]
        loss_ref[...] += jnp.sum(jnp.sum(diff * diff, axis=-1, keepdims=True), axis=0, keepdims=True) * (0.5 / D_MODEL)
        dout = diff * (1.0 / D_MODEL)
        gfw_ref[...] += jnp.sum(dout * yh, axis=0, keepdims=True)
        dyh = dout * fwv
        dy = r * (dyh - yh * jnp.mean(dyh * yh, axis=-1, keepdims=True))
        dy_ref[...] = dy
        dyb = dy.astype(bf16)
        dyb_ref[...] = dyb
        dm_ref[...] = lax.dot_general(dyb, wo_ref[...], (((1,), (1,)), ((), ())), preferred_element_type=f32)

    row = lambda i: (i, 0)
    fix = lambda i: (0, 0)
    act = jax.ShapeDtypeStruct((L, D_MODEL), f32)
    return _pcall(
        body, name="out_proj_loss", grid=(L // tm,),
        in_specs=[pl.BlockSpec((tm, D_MODEL), row), pl.BlockSpec((tm, MW), row), pl.BlockSpec((MW, D_MODEL), fix),
                  pl.BlockSpec((1, D_MODEL), fix), pl.BlockSpec((tm, D_MODEL), row)],
        out_specs=[pl.BlockSpec((tm, D_MODEL), row), pl.BlockSpec((tm, D_MODEL), row), pl.BlockSpec((tm, MW), row),
                   pl.BlockSpec((1, D_MODEL), fix), pl.BlockSpec((1, LANE), fix)],
        out_shape=[act, jax.ShapeDtypeStruct((L, D_MODEL), bf16), jax.ShapeDtypeStruct((L, MW), f32),
                   jax.ShapeDtypeStruct((1, D_MODEL), f32), jax.ShapeDtypeStruct((1, LANE), f32)],
        compiler_params=_cparams("arbitrary"),
    )(x, mix, wo, fw, tgt)


def _tn_matmul(a, b, name):
    L, M = a.shape
    N = b.shape[1]
    tm = 512 if M % 512 == 0 else (768 if M % 768 == 0 else M)

    def body(a_ref, b_ref, o_ref):
        o_ref[...] = lax.dot_general(a_ref[...], b_ref[...], (((0,), (0,)), ((), ())),
                                     preferred_element_type=f32).astype(o_ref.dtype)

    return _pcall(
        body, name=name, grid=(M // tm,),
        in_specs=[pl.BlockSpec((L, tm), lambda i: (0, i)), pl.BlockSpec((L, N), lambda i: (0, 0))],
        out_specs=pl.BlockSpec((tm, N), lambda i: (i, 0)),
        out_shape=jax.ShapeDtypeStruct((M, N), bf16),
        compiler_params=_cparams("parallel"),
    )(a, b)


def _gdn_gate_bwd(o, proj, gnw, dmix_a, after):
    L = o.shape[0]

    def body(o_ref, z_ref, w_ref, dm_ref, after_ref, do_ref, dz_ref, gw_ref):
        @pl.when(pl.program_id(0) == 0)
        def _():
            gw_ref[...] = jnp.zeros_like(gw_ref)
        wv = w_ref[...]
        for ls in HALVES:
            ov, z, dm = o_ref[:, ls], z_ref[:, ls], dm_ref[:, ls]
            rms = lax.rsqrt(jnp.mean(ov * ov, axis=-1, keepdims=True) + EPS)
            xh = ov * rms
            sg = _sigmoid(z)
            d_on = dm * (z * sg)
            dz_ref[:, ls] = (dm * (xh * wv) * (sg * (1.0 + z * (1.0 - sg)))).astype(bf16)
            gw_ref[...] += jnp.sum(d_on * xh, axis=0, keepdims=True)
            dxh = d_on * wv
            do_ref[:, ls] = (rms * (dxh - xh * jnp.mean(dxh * xh, axis=-1, keepdims=True))).astype(bf16)

    wide = pl.BlockSpec((L, ELT_W), lambda j: (0, j))
    return _pcall(
        body, name="gdn_gate_bwd", grid=(GDN_WIDTH // ELT_W,),
        in_specs=[wide, pl.BlockSpec((L, ELT_W), lambda j: (0, OFF_ZG // ELT_W + j)),
                  pl.BlockSpec((1, LANE), lambda j: (0, 0)), wide, ANY],
        out_specs=[wide, pl.BlockSpec((L, ELT_W), lambda j: (0, OFF_ZG // ELT_W + j)),
                   pl.BlockSpec((1, LANE), lambda j: (0, 0))],
        out_shape=[jax.ShapeDtypeStruct((L, GDN_WIDTH), bf16), jax.ShapeDtypeStruct((L, PROJ_PAD), bf16),
                   jax.ShapeDtypeStruct((1, LANE), f32)],
        compiler_params=_cparams("arbitrary"),
    )(o, proj, gnw, dmix_a, after)


def _conv_bwd(proj, cw, cb, dmix_b, dproj):
    L = proj.shape[0]

    def body(p_ref, cw_ref, cb_ref, dm_ref, dproj_in, dp_ref, gcw_ref, gcb_ref):
        for ls in HALVES:
            sb, sc_, sh, sz_ = _conv_parts(ls)
            bv, cv_, hv, z, dm = p_ref[:, sb], p_ref[:, sc_], p_ref[:, sh], p_ref[:, sz_], dm_ref[:, ls]
            u = cv_ * hv
            cv = _conv3(u, cw_ref, ls) + cb_ref[:, ls]
            sg = _sigmoid(z)
            sz = z * sg
            dp_ref[:, sb] = (dm * cv * sz).astype(bf16)
            dp_ref[:, sz_] = (dm * (bv * cv) * (sg * (1.0 + z * (1.0 - sg)))).astype(bf16)
            dcv = dm * bv * sz
            gcb_ref[:, ls] = jnp.sum(dcv, axis=0, keepdims=True)
            dcv1, dcv2 = _shift_up(dcv, 1), _shift_up(dcv, 2)
            gcw_ref[2:3, ls] = jnp.sum(dcv * u, axis=0, keepdims=True)
            gcw_ref[1:2, ls] = jnp.sum(dcv1 * u, axis=0, keepdims=True)
            gcw_ref[0:1, ls] = jnp.sum(dcv2 * u, axis=0, keepdims=True)
            du = cw_ref[2:3, ls] * dcv + cw_ref[1:2, ls] * dcv1 + cw_ref[0:1, ls] * dcv2
            dp_ref[:, sc_] = (du * hv).astype(bf16)
            dp_ref[:, sh] = (du * cv_).astype(bf16)

    return _pcall(
        body, name="conv_bwd", grid=(CONV_WIDTH // ELT_W,),
        in_specs=_conv_specs(L) + [pl.BlockSpec((L, ELT_W), lambda j: (0, GDN_WIDTH // ELT_W + j)), ANY],
        out_specs=[pl.BlockSpec((L, CONV_BLOCK), lambda j: (0, OFF_CONV // CONV_BLOCK + j)),
                   pl.BlockSpec((3, ELT_W), lambda j: (0, j)), pl.BlockSpec((1, ELT_W), lambda j: (0, j))],
        out_shape=[jax.ShapeDtypeStruct(dproj.shape, dproj.dtype), jax.ShapeDtypeStruct((3, CONV_WIDTH), f32),
                   jax.ShapeDtypeStruct((1, CONV_WIDTH), f32)],
        input_output_aliases={4: 0},
        compiler_params=_cparams("parallel"),
    )(proj, cw, cb, dmix_b, dproj)


def _gdn_bwd(qkv, sc, gr, u_all, w_all, vn_all, t_all, sp_all, do_all):
    L = qkv.shape[0]
    nc = L // CHUNK
    W = GDN_WIDTH
    cps = GDN_CPS_BWD if nc % GDN_CPS_BWD == 0 else 1
    rows_per_step = cps * CHUNK
    nsteps = nc // cps

    def body(qkv_ref, sc_ref, gr_ref, u_ref, w_ref, vn_ref, t_ref, sp_ref, do_ref, dqkv_ref, dsc_ref, dgr_ref, ds_scr):
        @pl.when(pl.program_id(0) == 0)
        def _():
            ds_scr[...] = jnp.zeros_like(ds_scr)
        nh, base = HEADS, 0
        HS = range(cps * nh)
        hl = [i % nh for i in HS]
        hd = [base + hl[i] for i in HS]
        rs = [slice((i // nh) * CHUNK, (i // nh + 1) * CHUNK) for i in HS]
        cs = [slice(hd[i] * HEAD_DIM, (hd[i] + 1) * HEAD_DIM) for i in HS]
        q = [qkv_ref[rs[i], hd[i] * HEAD_DIM:(hd[i] + 1) * HEAD_DIM] for i in HS]
        k = [qkv_ref[rs[i], W + hd[i] * HEAD_DIM:W + (hd[i] + 1) * HEAD_DIM] for i in HS]
        v = [qkv_ref[rs[i], 2 * W + hd[i] * HEAD_DIM:2 * W + (hd[i] + 1) * HEAD_DIM] for i in HS]
        hsc = [_head_scalars(sc_ref[rs[i], :], gr_ref, hd[i], i // nh) for i in HS]
        beta, gcc, gl, dmat, dmat_t = ([x[i] for x in hsc] for i in range(5))
        ii, jj = hsc[0][5], hsc[0][6]
        eg = [jnp.exp(gcc[h]) for h in HS]
        ekl = [jnp.exp(gl[h] - gcc[h]) for h in HS]
        egl = [jnp.exp(gl[h]) for h in HS]
        kb = [k[h] * beta[h] for h in HS]
        ks = [k[h] * ekl[h] for h in HS]
        do = [do_ref[rs[h], cs[h]] for h in HS]
        vn = [vn_ref[rs[h], cs[h]] for h in HS]
        s = [sp_ref[h // nh, cs[h], :] for h in HS]
        w = [w_ref[rs[h], cs[h]] for h in HS]
        qd = [q[h] * eg[h] for h in HS]

        kq = [_mm_nt(k[h], q[h]) for h in HS]
        p_t = [jnp.where(jj >= ii, kq[h] * dmat_t[h], 0.0) for h in HS]
        ptd = [_mm(p_t[h], do[h]) for h in HS]
        qw = [_cat16([qd[h], -w[h]], 0) for h in HS]
        dsn, dvn, dodv = [None] * len(HS), [None] * len(HS), [None] * len(HS)
        ds_cur = [ds_scr[base + h] for h in range(nh)]
        for ci in reversed(range(cps)):
            IS = range(ci * nh, (ci + 1) * nh)
            ksd = [_mm(ks[i], ds_cur[hl[i]]) for i in IS]
            for i in IS:
                dsn[i] = ds_cur[hl[i]]
                dvn[i] = ptd[i] + ksd[hl[i]]
                dodv[i] = _cat16([do[i], dvn[i]], 0)
            dsq = [_mm_tn(qw[i], dodv[i]) for i in IS]
            ds_cur = [egl[i] * ds_cur[hl[i]] + dsq[hl[i]] for i in IS]
        for h in range(nh):
            ds_scr[base + h] = ds_cur[h]
        x1 = [_mm_nt(dodv[h], s[h]) for h in HS]
        dks = [_mm_nt(vn[h], dsn[h]) for h in HS]
        dov = [_mm_nt(do[h], vn[h]) for h in HS]
        vdo = [_mm_nt(vn[h], do[h]) for h in HS]
        kk = [_mm_nt(kb[h], k[h]) for h in HS]
        qk = [_mm_nt(q[h], k[h]) for h in HS]
        dgl = [egl[h] * jnp.sum(jnp.sum(s[h] * dsn[h], axis=1, keepdims=True), axis=0, keepdims=True) for h in HS]
        dqd = [x1[h][:CHUNK] for h in HS]
        duw = [jnp.concatenate([dvn[h], -x1[h][CHUNK:]], axis=1) for h in HS]
        tdu = [_mm_tn(t_ref[h // nh, hd[h]], duw[h]) for h in HS]
        dvk = [duw[h] + tdu[h] for h in HS]
        uw = [jnp.concatenate([u_ref[rs[h], cs[h]], w[h]], axis=1) for h in HS]
        da = [-jnp.where(ii > jj, _mm_nt(dvk[h], uw[h]), 0.0) for h in HS]
        da_t = [-jnp.where(jj > ii, _mm_nt(uw[h], dvk[h]), 0.0) for h in HS]
        dp = [jnp.where(ii >= jj, dov[h], 0.0) for h in HS]
        dp_t = [jnp.where(jj >= ii, vdo[h], 0.0) for h in HS]
        r1 = [_mm(_cat16([da[h] * dmat[h], dp[h] * dmat[h]], 0), k[h]) for h in HS]
        dk1 = [_mm(_cat16([da_t[h] * dmat_t[h], dp_t[h] * dmat_t[h]], 1), _cat16([kb[h], q[h]], 0)) for h in HS]
        lane = _lanes((CHUNK, LANE))
        for ci in range(cps):
            dsc = jnp.zeros((CHUNK, LANE), f32)
            for i in range(ci * nh, (ci + 1) * nh):
                h = hd[i]
                a = jnp.where(ii > jj, kk[i] * dmat[i], 0.0)
                p = jnp.where(ii >= jj, qk[i] * dmat[i], 0.0)
                gmat = da[i] * a + dp[i] * p
                dvb, dkbg = dvk[i][:, :HEAD_DIM], dvk[i][:, HEAD_DIM:]
                kbg = kb[i] * eg[i]
                dkb = r1[i][:CHUNK] + dkbg * eg[i]
                dq = r1[i][CHUNK:] + dqd[i] * eg[i]
                dk = dk1[i] + dks[i] * ekl[i] + dkb * beta[i]
                dbeta = jnp.sum(dkb * k[i] + dvb * v[i], axis=1, keepdims=True)
                ksum = jnp.sum(dks[i] * ks[i], axis=1, keepdims=True)
                dgl_tot = dgl[i] + jnp.sum(ksum, axis=0, keepdims=True)
                dgc = (jnp.sum(gmat, axis=1, keepdims=True) + jnp.sum(dqd[i] * qd[i] + dkbg * kbg, axis=1, keepdims=True)
                       - ksum)
                dgc = dgc + jnp.where(_rows(dgc.shape) == CHUNK - 1, dgl_tot, 0.0)
                dqkv_ref[rs[i], h * HEAD_DIM:(h + 1) * HEAD_DIM] = dq
                dqkv_ref[rs[i], W + h * HEAD_DIM:W + (h + 1) * HEAD_DIM] = dk
                dqkv_ref[rs[i], 2 * W + h * HEAD_DIM:2 * W + (h + 1) * HEAD_DIM] = dvb * beta[i]
                dsc = jnp.where(lane == h, dbeta, jnp.where(lane == HEADS + h, dgc, dsc))
                dgr_ref[ci, h:h + 1, :] = jnp.sum(gmat, axis=0, keepdims=True)
            dsc_ref[ci * CHUNK:(ci + 1) * CHUNK, :] = dsc

    row = lambda c: (nsteps - 1 - c, 0)
    lead3 = lambda c: (nsteps - 1 - c, 0, 0)
    return _pcall(
        body, name="gdn_bwd", grid=(nsteps,),
        in_specs=[pl.BlockSpec((rows_per_step, 3 * W), row), pl.BlockSpec((rows_per_step, LANE), row),
                  pl.BlockSpec((cps, HEADS, CHUNK), lead3),
                  pl.BlockSpec((rows_per_step, W), row), pl.BlockSpec((rows_per_step, W), row),
                  pl.BlockSpec((rows_per_step, W), row),
                  pl.BlockSpec((cps, HEADS, CHUNK, CHUNK), lambda c: (nsteps - 1 - c, 0, 0, 0)),
                  pl.BlockSpec((cps, W, HEAD_DIM), lead3), pl.BlockSpec((rows_per_step, W), row)],
        out_specs=[pl.BlockSpec((rows_per_step, 3 * W), row), pl.BlockSpec((rows_per_step, LANE), row),
                   pl.BlockSpec((cps, HEADS, CHUNK), lead3)],
        out_shape=[jax.ShapeDtypeStruct((L, 3 * W), f32), jax.ShapeDtypeStruct((L, LANE), f32),
                   jax.ShapeDtypeStruct((nc, HEADS, CHUNK), f32)],
        scratch_shapes=[pltpu.VMEM((HEADS, HEAD_DIM, HEAD_DIM), f32)],
        compiler_params=_cparams("arbitrary"),
    )(qkv, sc, gr, u_all, w_all, vn_all, t_all, sp_all, do_all)


def _qkv_bwd(proj, cw, dn, dproj):
    L = proj.shape[0]

    def body(x_ref, cw_ref, dn_ref, dproj_in, dx_ref, gcw_ref):
        j = pl.program_id(0)
        steps = GDN_WIDTH // ELT_W
        scale = jnp.where(j < steps, HEAD_DIM ** -0.5, 1.0).astype(f32)
        for ls in HALVES:
            x, dn_v = x_ref[:, ls], dn_ref[:, ls]
            c = _conv4(x, cw_ref, ls)
            sg = _sigmoid(c)
            a = c * sg
            rn = lax.rsqrt(jnp.sum(a * a, axis=1, keepdims=True) + EPS)
            da_n = (scale * rn) * (dn_v - a * ((rn * rn) * jnp.sum(dn_v * a, axis=1, keepdims=True)))
            da = jnp.where(j < 2 * steps, da_n, dn_v)
            dc = da * (sg * (1.0 + c * (1.0 - sg)))
            dc1, dc2, dc3 = _shift_up(dc, 1), _shift_up(dc, 2), _shift_up(dc, 3)
            gcw_ref[3:4, ls] = jnp.sum(dc * x, axis=0, keepdims=True)
            gcw_ref[2:3, ls] = jnp.sum(dc1 * x, axis=0, keepdims=True)
            gcw_ref[1:2, ls] = jnp.sum(dc2 * x, axis=0, keepdims=True)
            gcw_ref[0:1, ls] = jnp.sum(dc3 * x, axis=0, keepdims=True)
            dx = cw_ref[3:4, ls] * dc + cw_ref[2:3, ls] * dc1 + cw_ref[1:2, ls] * dc2 + cw_ref[0:1, ls] * dc3
            dx_ref[:, ls] = dx.astype(bf16)

    col = pl.BlockSpec((L, ELT_W), lambda j: (0, j))
    wspec = pl.BlockSpec((4, ELT_W), lambda j: (0, j))
    return _pcall(
        body, name="qkv_bwd", grid=(3 * GDN_WIDTH // ELT_W,),
        in_specs=[col, wspec, col, ANY], out_specs=[col, wspec],
        out_shape=[jax.ShapeDtypeStruct(dproj.shape, dproj.dtype), jax.ShapeDtypeStruct((4, 3 * GDN_WIDTH), f32)],
        input_output_aliases={3: 0},
        compiler_params=_cparams("parallel"),
    )(proj, cw, dn, dproj)


def _scalars_bwd(proj, alog_p, dtb_p, dsc, dgr_col, dproj, after):
    L = proj.shape[0]

    def body(x_ref, al_ref, dt_ref, dsc_ref, dgr_ref, dproj_in, after_ref, dba_ref, gs_ref):
        x, dsc_v = x_ref[...], dsc_ref[...]
        lane = _lanes(x.shape)
        dec = (lane >= HEADS) & (lane < 2 * HEADS)
        dg = jnp.where(dec, dsc_v - dgr_ref[...], 0.0)
        rc = _rows(x.shape) & (CHUNK - 1)
        for s in (1, 2, 4, 8, 16, 32):
            dg = dg + jnp.where(rc + s < CHUNK, pltpu.roll(dg, L - s, 0), 0.0)
        xa = x + dt_ref[...]
        ea = jnp.exp(al_ref[...])
        g = -ea * _softplus(xa)
        da = dg * (-ea) * _sigmoid(xa)
        beta = _sigmoid(x)
        db = dsc_v * beta * (1.0 - beta)
        dba_ref[:, :LANE] = jnp.where(lane < HEADS, db, jnp.where(dec, da, 0.0)).astype(bf16)
        dba_ref[:, LANE:] = jnp.zeros((L, ELT_W - LANE), bf16)
        g_al = jnp.sum(jnp.where(dec, dg * g, 0.0), axis=0, keepdims=True)
        g_dt = jnp.sum(jnp.where(dec, da, 0.0), axis=0, keepdims=True)
        row8 = _rows(gs_ref.shape)
        gs = jnp.where(row8 == 0, g_al, jnp.where(row8 == 1, g_dt, 0.0))
        gs_ref[...] = pltpu.roll(gs, LANE - HEADS, 1)

    full = pl.BlockSpec((L, LANE), lambda i: (0, 0))
    vec = pl.BlockSpec((1, LANE), lambda i: (0, 0))
    return _pcall(
        body, name="scalars_bwd", grid=(1,),
        in_specs=[pl.BlockSpec((L, LANE), lambda i: (0, OFF_BA // LANE)), vec, vec, full, full, ANY, ANY],
        out_specs=[pl.BlockSpec((L, ELT_W), lambda i: (0, OFF_BA // ELT_W)), pl.BlockSpec((8, LANE), lambda i: (0, 0))],
        out_shape=[jax.ShapeDtypeStruct(dproj.shape, dproj.dtype), jax.ShapeDtypeStruct((8, LANE), f32)],
        input_output_aliases={5: 0},
        compiler_params=_cparams("arbitrary"),
    )(proj, alog_p, dtb_p, dsc, dgr_col, dproj, after)


def _input_grad(dproj, wpad, x, nw, dy, after):
    L = x.shape[0]
    tm = min(512, L)
    cuts = (0, 1024, 3072, 5120, 7168, PROJ_PAD)
    nk = len(cuts) - 1

    def body(dp_ref, w_hbm, x_ref, nw_ref, dy_ref, after_ref, gx_ref, gnw_ref, w_vmem, sems):
        first = pl.program_id(0) == 0
        loads = [pltpu.make_async_copy(w_hbm.at[cuts[k]:cuts[k + 1], :], w_vmem.at[cuts[k]:cuts[k + 1], :], sems.at[k])
                 for k in range(nk)]

        @pl.when(first)
        def _():
            for cp in loads:
                cp.start()
            gnw_ref[...] = jnp.zeros_like(gnw_ref)
        dh = None
        for k in range(nk):
            pl.when(first)(loads[k].wait)
            part = jnp.dot(dp_ref[:, cuts[k]:cuts[k + 1]], w_vmem[cuts[k]:cuts[k + 1], :], preferred_element_type=f32)
            dh = part if dh is None else dh + part
        xv, nwv = x_ref[...], nw_ref[...]
        r = lax.rsqrt(jnp.mean(xv * xv, axis=-1, keepdims=True) + EPS)
        xh = xv * r
        gnw_ref[...] += jnp.sum(dh * xh, axis=0, keepdims=True)
        dxh = dh * nwv
        gx_ref[...] = dy_ref[...] + r * (dxh - xh * jnp.mean(dxh * xh, axis=-1, keepdims=True))

    row = lambda i: (i, 0)
    fix = lambda i: (0, 0)
    return _pcall(
        body, name="input_grad", grid=(L // tm,),
        in_specs=[pl.BlockSpec((tm, PROJ_PAD), row), ANY, pl.BlockSpec((tm, D_MODEL), row),
                  pl.BlockSpec((1, D_MODEL), fix), pl.BlockSpec((tm, D_MODEL), row), ANY],
        out_specs=[pl.BlockSpec((tm, D_MODEL), row), pl.BlockSpec((1, D_MODEL), fix)],
        out_shape=[jax.ShapeDtypeStruct((L, D_MODEL), f32), jax.ShapeDtypeStruct((1, D_MODEL), f32)],
        scratch_shapes=[pltpu.VMEM(wpad.shape, bf16), pltpu.SemaphoreType.DMA((nk,))],
        compiler_params=_cparams("arbitrary"),
    )(dproj, wpad, x, nw, dy, after)


def _adamw_reduce(parts, w, m, v, name):
    R, C = w.shape[0], w.shape[-1]
    n_parts = parts.shape[0]
    tr = 128 if R % 128 == 0 else R
    c1 = 1.0 - ADAM_B1 ** ADAM_STEP
    c2 = 1.0 - ADAM_B2 ** ADAM_STEP
    at = (slice(None), 0, slice(None)) if w.ndim == 3 else Ellipsis

    def body(p_ref, w_ref, m_ref, v_ref, g_ref, d_ref, nm_ref, nv_ref):
        g = p_ref[0].astype(f32)
        for s in range(1, n_parts):
            g = g + p_ref[s].astype(f32)
        nm = ADAM_B1 * m_ref[at] + (1.0 - ADAM_B1) * g
        nv = ADAM_B2 * v_ref[at] + (1.0 - ADAM_B2) * (g * g)
        g_ref[at] = g
        nm_ref[at] = nm
        nv_ref[at] = nv
        d_ref[at] = -ADAM_LR * ((nm / c1) / (jnp.sqrt(nv / c2) + ADAM_EPS) + ADAM_WD * w_ref[at])

    blk = pl.BlockSpec((tr, 1, C), lambda i: (i, 0, 0)) if w.ndim == 3 else pl.BlockSpec((tr, C), lambda i: (i, 0))
    out = jax.ShapeDtypeStruct(w.shape, f32)
    return _pcall(
        body, name=name, grid=(R // tr,),
        in_specs=[pl.BlockSpec((n_parts, tr, C), lambda i: (0, i, 0)), blk, blk, blk],
        out_specs=[blk] * 4, out_shape=[out] * 4,
        compiler_params=_cparams("parallel"),
    )(parts, w, m, v)


SMALL_SLOTS = ((0, D_MODEL), (D_MODEL, D_MODEL), (2 * D_MODEL, D_MODEL), (3 * D_MODEL, LANE),
               (3 * D_MODEL + LANE, HEADS), (3 * D_MODEL + 2 * LANE, HEADS))
SMALL_LOSS = 3 * D_MODEL + 3 * LANE
SMALL_W = SMALL_LOSS + LANE


def _pack_small(gs, after):
    def body(nw_ref, cb_ref, fw_ref, gn_ref, sc_ref, ls_ref, after_ref, o_ref):
        for ref, (start, width) in zip((nw_ref, cb_ref, fw_ref, gn_ref), SMALL_SLOTS[:4]):
            o_ref[:, start:start + width] = ref[...]
        o_ref[:, SMALL_SLOTS[4][0]:SMALL_SLOTS[4][0] + LANE] = sc_ref[0:1, :]
        o_ref[:, SMALL_SLOTS[5][0]:SMALL_SLOTS[5][0] + LANE] = sc_ref[1:2, :]
        o_ref[:, SMALL_LOSS:SMALL_W] = ls_ref[...]

    vm = pl.BlockSpec(memory_space=pltpu.VMEM)
    return _pcall(body, name="pack_small_grads", out_shape=jax.ShapeDtypeStruct((1, SMALL_W), f32),
                  in_specs=[vm] * 6 + [ANY], out_specs=vm)(*gs, after)


def _adamw_small(parts, ws, ms, vs):
    c1 = 1.0 - ADAM_B1 ** ADAM_STEP
    c2 = 1.0 - ADAM_B2 ** ADAM_STEP
    np_ = len(ws)

    def body(*refs):
        p_ref = refs[0]
        w_refs, m_refs, v_refs = refs[1:1 + np_], refs[1 + np_:1 + 2 * np_], refs[1 + 2 * np_:1 + 3 * np_]
        outs = refs[1 + 3 * np_:]
        g_refs, d_refs, nm_refs, nv_refs = (outs[i * np_:(i + 1) * np_] for i in range(4))
        loss_ref = outs[4 * np_]

        def total(start, width):
            t = p_ref[0, :, start:start + width]
            for s in range(1, N_DEV):
                t = t + p_ref[s, :, start:start + width]
            return t

        for i, (start, width) in enumerate(SMALL_SLOTS):
            g = total(start, width)
            nm = ADAM_B1 * m_refs[i][...] + (1.0 - ADAM_B1) * g
            nv = ADAM_B2 * v_refs[i][...] + (1.0 - ADAM_B2) * (g * g)
            g_refs[i][...] = g
            nm_refs[i][...] = nm
            nv_refs[i][...] = nv
            d_refs[i][...] = -ADAM_LR * ((nm / c1) / (jnp.sqrt(nv / c2) + ADAM_EPS) + ADAM_WD * w_refs[i][...])
        loss_ref[...] = total(SMALL_LOSS, LANE)

    vm = pl.BlockSpec(memory_space=pltpu.VMEM)
    shapes = [jax.ShapeDtypeStruct(w.shape, f32) for w in ws]
    res = _pcall(body, name="adamw_small", out_shape=shapes * 4 + [jax.ShapeDtypeStruct((1, LANE), f32)],
                 in_specs=[vm] * (1 + 3 * np_), out_specs=[vm] * (4 * np_ + 1))(parts, *ws, *ms, *vs)
    return [res[i * np_:(i + 1) * np_] for i in range(4)], res[4 * np_]


def _adamw_w_in(part_a, part_b, w3, m3, v3, after):
    _, n, _ = part_a.shape
    c1 = 1.0 - ADAM_B1 ** ADAM_STEP
    c2 = 1.0 - ADAM_B2 ** ADAM_STEP

    def body(pa_ref, pb_ref, w_ref, m_ref, v_ref, after_ref, g_ref, d_ref, nm_ref, nv_ref):
        g = pa_ref[0].astype(f32) + pb_ref[0].astype(f32)
        nm = ADAM_B1 * m_ref[:, 0, :] + (1.0 - ADAM_B1) * g
        nv = ADAM_B2 * v_ref[:, 0, :] + (1.0 - ADAM_B2) * (g * g)
        g_ref[:, 0, :] = g
        nm_ref[:, 0, :] = nm
        nv_ref[:, 0, :] = nv
        d_ref[:, 0, :] = -ADAM_LR * ((nm / c1) / (jnp.sqrt(nv / c2) + ADAM_EPS) + ADAM_WD * w_ref[:, 0, :])

    tile = 2 * COL_TILE
    blk = pl.BlockSpec((n, 1, tile), lambda j: (0, 0, j))
    out = jax.ShapeDtypeStruct((n, 1, D_MODEL), f32)
    return _pcall(
        body, name="adamw_w_in", grid=(D_MODEL // tile,),
        in_specs=[pl.BlockSpec((1, n, tile), lambda j: (0, 0, j))] * 2 + [blk, blk, blk, ANY],
        out_specs=[blk] * 4, out_shape=[out] * 4,
        compiler_params=_cparams("parallel"),
    )(part_a, part_b, w3, m3, v3, after)


def _pad_lanes(vec8, start):
    return jnp.pad(vec8.reshape(1, -1), ((0, 0), (start, LANE - start - vec8.size)))


def kernel(x, norm_in_w, w_in, conv_qkv_w, A_log, dt_bias, gdn_norm_w, conv_w, conv_b, w_out, final_norm_w, loss_target, m_norm_in_w, m_w_in, m_conv_qkv_w, m_A_log, m_dt_bias, m_gdn_norm_w, m_conv_w, m_conv_b, m_w_out, m_final_norm_w, v_norm_in_w, v_w_in, v_conv_qkv_w, v_A_log, v_dt_bias, v_gdn_norm_w, v_conv_w, v_conv_b, v_w_out, v_final_norm_w):
    L = x.shape[1]
    nc = L // CHUNK
    xs = x[0]
    tgt = loss_target[0]
    fnw = final_norm_w.reshape(1, D_MODEL)

    as_rows = lambda a: jnp.transpose(a, (2, 0, 1))
    as_taps = lambda a: jnp.transpose(a, (1, 0, 2))
    win_g, cqkv_g, cw_g = _all_gather([_cast_w_in(as_rows(w_in)), conv_qkv_w[0], as_taps(conv_w)], "gather_weights",
                                      pieces=[4, 1, 1])
    wpad = _relayout_w_in(win_g)
    cqkv = jnp.concatenate([cqkv_g[d] for d in range(N_DEV)], axis=1)
    cw = jnp.concatenate([cw_g[d][:, 0, :] for d in range(N_DEV)], axis=1)
    alog_p = _pad_lanes(A_log, HEADS)
    dtb_p = _pad_lanes(dt_bias, HEADS)
    tok = lambda started: started[4]
    wo_started = _spread_start(w_out[0].astype(bf16), wpad, "gather", "gather_w_out_start")

    proj, h = _in_proj(xs, norm_in_w, wpad, tok(wo_started))
    qkv = _qkv_act(proj, cqkv)
    sc, gr = _scalars(proj, alog_p, dtb_p)
    o, u_all, w_all, vn_all, t_all, sp_all = _gdn_fwd(qkv, sc, gr)
    mix = _conv_fwd(proj, cw, conv_b, _gdn_gate(o, proj, gdn_norm_w))
    wo = _spread_wait(wo_started, mix, "gather", "gather_w_out_wait")[1].reshape(-1, D_MODEL)
    dy, dyb, dmix, g_fnw, loss_v = _out_proj_loss(xs, mix, wo, fnw, tgt)

    g_wout = _tn_matmul(mix, dyb, "grad_w_out")
    gwo_started = _spread_start(g_wout.reshape(N_DEV, -1, D_MODEL), dyb, "scatter", "exchange_grad_w_out_start")
    do, dproj, g_gnw = _gdn_gate_bwd(o, proj, gdn_norm_w, dmix, tok(gwo_started))
    dproj, g_cw, g_cb = _conv_bwd(proj, cw, conv_b, dmix, dproj)
    dqkv_n, dsc, dgr = _gdn_bwd(qkv, sc, gr, u_all, w_all, vn_all, t_all, sp_all, do)
    dproj, g_cqkv = _qkv_bwd(proj, cqkv, dqkv_n, dproj)
    g_cqkv_blk = g_cqkv.reshape(4, N_DEV, -1).transpose(1, 0, 2)
    g_cw_blk = jnp.pad(g_cw.reshape(3, N_DEV, -1).transpose(1, 0, 2),
                       ((0, 0), (0, 1), (0, g_cqkv_blk.shape[2] - g_cw.shape[1] // N_DEV)))
    gsm_started = _spread_start(jnp.concatenate([g_cqkv_blk, g_cw_blk], axis=1), g_cqkv, "scatter",
                                "exchange_small_sharded_grads_start")
    dgr_col = jnp.pad(dgr.transpose(0, 2, 1).reshape(L, HEADS), ((0, 0), (HEADS, LANE - 2 * HEADS)))
    dproj, g_sc = _scalars_bwd(proj, alog_p, dtb_p, dsc, dgr_col, dproj, tok(gsm_started))
    g_win_blk = _grad_blocks(_tn_matmul(dproj, h, "grad_w_in"))

    (p_win,) = _pair_exchange([g_win_blk], "exchange_grads_pair")
    r_small = _spread_wait(gsm_started, p_win, "scatter", "exchange_small_sharded_grads_wait")[1]
    r_cqkv, r_cw = r_small[:, :4, :], r_small[:, 4:7, :g_cw.shape[1] // N_DEV]
    s_win = _pair_sum(g_win_blk, p_win, "pair_sum_w_in")
    gw1_started = _spread_start(s_win, r_small, "axis_a", "exchange_grads_axis1_start")
    grad_x, g_nw = _input_grad(dproj, wpad, xs, norm_in_w, dy, tok(gw1_started))
    s_thru, got1 = _spread_wait(gw1_started, grad_x, "axis_a", "exchange_grads_axis1_wait")
    t_win = _axis_sum(s_thru, got1, "axis_sum_w_in")
    gw2_started = _spread_start(t_win, got1, "axis_b", "exchange_grads_axis2_start")

    r_wout = _spread_wait(gwo_started, tok(gw2_started), "scatter", "exchange_grad_w_out_wait")[1]
    upd_wout =_adamw_reduce(r_wout, w_out[0], m_w_out[0], v_w_out[0], "adamw_w_out")
    upd_cqkv = _adamw_reduce(r_cqkv, conv_qkv_w[0], m_conv_qkv_w[0], v_conv_qkv_w[0], "adamw_conv_qkv_w")
    upd_cw = _adamw_reduce(r_cw, as_taps(conv_w), as_taps(m_conv_w), as_taps(v_conv_w), "adamw_conv_w")

    t_thru, got2 = _spread_wait(gw2_started, upd_cw[0], "axis_b", "exchange_grads_axis2_wait")

    small_g = _pack_small([g_nw, g_cb, g_fnw, g_gnw, g_sc, loss_v], got2)
    gsg_started = _spread_start(small_g, got2, "gather", "gather_small_grads_start")
    upd_win_t = _adamw_w_in(t_thru, got2, as_rows(w_in), as_rows(m_w_in), as_rows(v_w_in), tok(gsg_started))
    upd_win = [jnp.transpose(a, (1, 2, 0)) for a in upd_win_t]
    small_all = _spread_wait(gsg_started, upd_win_t[0], "gather", "gather_small_grads_wait")[1]
    fvec = lambda a: a.reshape(1, D_MODEL)
    upd_small, loss_sum = _adamw_small(
        small_all,
        [norm_in_w, conv_b, fvec(final_norm_w), gdn_norm_w, A_log, dt_bias],
        [m_norm_in_w, m_conv_b, fvec(m_final_norm_w), m_gdn_norm_w, m_A_log, m_dt_bias],
        [v_norm_in_w, v_conv_b, fvec(v_final_norm_w), v_gdn_norm_w, v_A_log, v_dt_bias])

    outs = [loss_sum[0, 0], grad_x[None]]
    for k in range(4):
        nw_k, cb_k, fw_k, gn_k, al_k, dt_k = upd_small[k]
        outs += [nw_k, upd_win[k], upd_cqkv[k][None], al_k, dt_k, gn_k,
                 as_taps(upd_cw[k]), cb_k, upd_wout[k][None], fw_k.reshape(D_MODEL)]
    return tuple(outs)
```

```python
import jax
import jax.numpy as jnp
from jax import lax
from jax.experimental import pallas as pl
from jax.experimental.pallas import tpu as pltpu

f32 = jnp.float32
bf16 = jnp.bfloat16

N_DEV = 8
D_MODEL = 1024
HEADS = 8
HEAD_DIM = 128
CHUNK = 64
GDN_CPS = 4
GDN_CPS_BWD = 1
GDN_WIDTH = HEADS * HEAD_DIM
CONV_WIDTH = 1024
PROJ_WIDTH = 8208
SHARD_W = PROJ_WIDTH // N_DEV
EPS = 1e-6

LANE = 128
ELT_W = 256

OFF_QKV, OFF_ZG, OFF_CONV, OFF_BA = 0, 3072, 4096, 8192
CONV_BLOCK = 4 * ELT_W
PROJ_PAD = 8448
NAT_BA, NAT_CONV = 4096, 4112


def _padded_col(n):
    if n < NAT_BA:
        return n
    if n < NAT_CONV:
        return OFF_BA + n - NAT_BA
    g, ch = divmod(n - NAT_CONV, CONV_WIDTH)
    j, r = divmod(ch, ELT_W)
    return OFF_CONV + CONV_BLOCK * j + ELT_W * g + r


def _layout_segments(n0, n1):
    cuts = [NAT_BA, NAT_CONV] + [NAT_CONV + ELT_W * k for k in range(1, 4 * CONV_WIDTH // ELT_W)]
    pts = [n0] + [c for c in cuts if n0 < c < n1] + [n1]
    return [(lo, hi - lo, _padded_col(lo)) for lo, hi in zip(pts, pts[1:])]

ADAM_LR, ADAM_B1, ADAM_B2, ADAM_EPS, ADAM_WD, ADAM_STEP = 0.001, 0.9, 0.999, 1e-08, 0.01, 10

V7X_VMEM_BYTES = 64 * 1024 * 1024
VMEM_LIMIT = V7X_VMEM_BYTES - 8 * 1024 * 1024

MESH = pl.DeviceIdType.MESH
ANY = pl.BlockSpec(memory_space=pl.ANY)


def _pcall(body, **kw):
    return pl.pallas_call(body, **kw)


def _cparams(*sem):
    return pltpu.CompilerParams(dimension_semantics=sem if sem else None, vmem_limit_bytes=VMEM_LIMIT)


def _mm(a, b):
    return jnp.dot(a.astype(bf16), b.astype(bf16), preferred_element_type=f32)


def _mm_nt(a, b):
    return lax.dot_general(a.astype(bf16), b.astype(bf16), (((1,), (1,)), ((), ())), preferred_element_type=f32)


def _cat16(parts, axis):
    return jnp.concatenate([p.astype(bf16) for p in parts], axis=axis)


def _mm_tn(a, b):
    return lax.dot_general(a.astype(bf16), b.astype(bf16), (((0,), (0,)), ((), ())), preferred_element_type=f32)


def _rows(shape):
    return lax.broadcasted_iota(jnp.int32, shape, 0)


def _lanes(shape):
    return lax.broadcasted_iota(jnp.int32, shape, 1)


def _shift_down(x, s):
    if s == 0:
        return x
    return jnp.where(_rows(x.shape) >= s, pltpu.roll(x, s, 0), 0.0)


def _shift_up(x, s):
    if s == 0:
        return x
    n = x.shape[0]
    return jnp.where(_rows(x.shape) < n - s, pltpu.roll(x, n - s, 0), 0.0)


def _sigmoid(x):
    return jax.nn.sigmoid(x)


def _softplus(x):
    e = jnp.exp(-jnp.abs(x))
    small = e * (1.0 - e * (0.5 - e * (1.0 / 3.0)))
    return jnp.maximum(x, 0.0) + jnp.where(e < 0.01, small, jnp.log(1.0 + e))


def _mesh_pos():
    return lax.axis_index("x"), lax.axis_index("y"), lax.axis_index("c")


def _flat(px, py, pc):
    return 4 * px + 2 * py + pc


def _all_gather(xs, name, pieces=None):
    n = len(xs)
    pieces = pieces or [1] * n
    items = [(a, q) for a in range(n) for q in range(pieces[a])]
    ni = len(items)

    def view(ref, i):
        a, q = items[i]
        if pieces[a] == 1:
            return ref
        wd = xs[a].shape[-1] // pieces[a]
        return ref.at[(slice(None),) * (xs[a].ndim - 1) + (pl.ds(q * wd, wd),)]

    def body(*refs):
        x_refs, o_refs = refs[:n], refs[n:2 * n]
        send_sems, recv_sems, local_sems = refs[2 * n:]
        x, y, c = _mesh_pos()
        me, sibling = (x, y, c), (x, y, 1 - c)
        flip = lambda v, bit: v + bit - 2 * v * bit
        nbr_a = (flip(x, 1 - c), flip(y, c))
        nbr_b = (flip(x, c), flip(y, 1 - c))
        diag = (1 - x, 1 - y)

        def copy(i, k, block, to, own=False):
            a = items[i][0]
            dst = view(o_refs[a].at[_flat(*block)], i)
            return pltpu.make_async_remote_copy(
                src_ref=view(x_refs[a], i) if own else dst, dst_ref=dst,
                send_sem=send_sems.at[i, k], recv_sem=recv_sems.at[i, k], device_id=to, device_id_type=MESH)

        mine, sent = [], []

        def go(cp):
            cp.start()
            sent.append(cp)

        for a in range(n):
            cp = pltpu.make_async_copy(x_refs[a], o_refs[a].at[_flat(*me)], local_sems.at[a])
            cp.start()
            mine.append(cp)
        for a in range(ni):
            go(copy(a, 1, me, (*nbr_a, c), own=True))
            go(copy(a, 2, me, (*nbr_b, c), own=True))
            go(copy(a, 0, me, sibling, own=True))
        for a in range(ni):
            copy(a, 1, (*nbr_a, c), me).wait_recv()
            go(copy(a, 3, (*nbr_a, c), (*nbr_b, c)))
            go(copy(a, 4, (*nbr_a, c), sibling))
        for a in range(ni):
            copy(a, 2, (*nbr_b, c), me).wait_recv()
            go(copy(a, 5, (*nbr_b, c), sibling))
        for a in range(ni):
            copy(a, 3, (*diag, c), me).wait_recv()
            go(copy(a, 6, (*diag, c), sibling))
        for a in range(ni):
            copy(a, 0, sibling, me).wait_recv()
            copy(a, 4, (*nbr_b, 1 - c), me).wait_recv()
            copy(a, 5, (*nbr_a, 1 - c), me).wait_recv()
            copy(a, 6, (*diag, 1 - c), me).wait_recv()
        for cp in sent:
            cp.wait_send()
        for cp in mine:
            cp.wait()

    outs = _pcall(
        body, name=name,
        out_shape=[jax.ShapeDtypeStruct((N_DEV,) + a.shape, a.dtype) for a in xs],
        in_specs=[ANY] * n, out_specs=[ANY] * n,
        scratch_shapes=[pltpu.SemaphoreType.DMA((ni, 7)), pltpu.SemaphoreType.DMA((ni, 7)), pltpu.SemaphoreType.DMA((n,))],
    )(*xs)
    return list(outs)


def _pair_exchange(gs, name):
    n = len(gs)
    chips = [(0, 0), (0, 1), (1, 0), (1, 1)]

    def body(*refs):
        g_refs, o_refs = refs[:n], refs[n:2 * n]
        send_sems, recv_sems = refs[2 * n:]
        x, y, c = _mesh_pos()
        sibling = (x, y, 1 - c)

        def copy(a, i):
            xp, yp = chips[i]
            return pltpu.make_async_remote_copy(
                src_ref=g_refs[a].at[_flat(xp, yp, 1 - c)], dst_ref=o_refs[a].at[i],
                send_sem=send_sems.at[a, i], recv_sem=recv_sems.at[a, i], device_id=sibling, device_id_type=MESH)

        cps = [copy(a, i) for a in range(n) for i in range(4)]
        for cp in cps:
            cp.start()
        for cp in cps:
            cp.wait()

    outs = _pcall(
        body, name=name,
        out_shape=[jax.ShapeDtypeStruct((4,) + a.shape[1:], a.dtype) for a in gs],
        in_specs=[ANY] * n, out_specs=[ANY] * n,
        scratch_shapes=[pltpu.SemaphoreType.DMA((n, 4)), pltpu.SemaphoreType.DMA((n, 4))],
    )(*gs)
    return list(outs)


def _pair_sum(g, p1, name):
    _, R, C = g.shape
    tr = 256 if R % 256 == 0 else R
    cidx = lax.axis_index("c").astype(jnp.int32).reshape(1)

    def body(c_ref, g_ref, p_ref, o_ref):
        o_ref[...] = (g_ref[...].astype(f32) + p_ref[...].astype(f32)).astype(o_ref.dtype)

    return _pcall(
        body, name=name,
        grid_spec=pltpu.PrefetchScalarGridSpec(
            num_scalar_prefetch=1, grid=(4, R // tr),
            in_specs=[pl.BlockSpec((1, tr, C), lambda i, r, c_ref: (2 * i + c_ref[0], r, 0)),
                      pl.BlockSpec((1, tr, C), lambda i, r, c_ref: (i, r, 0))],
            out_specs=pl.BlockSpec((1, tr, C), lambda i, r, c_ref: (i, r, 0))),
        out_shape=jax.ShapeDtypeStruct((4, R, C), g.dtype),
        compiler_params=_cparams("parallel", "parallel"),
    )(cidx, g, p1)


def _axis_sum(s, got, name):
    _, R, C = s.shape
    x, y, c = _mesh_pos()
    me, _, b, _ = _axis_chips(x, y, c)
    idx = jnp.stack([2 * me[0] + me[1], 2 * b[0] + b[1]]).astype(jnp.int32)

    def body(idx_ref, s_ref, g_ref, o_ref):
        o_ref[...] = (s_ref[...].astype(f32) + g_ref[...].astype(f32)).astype(o_ref.dtype)

    return _pcall(
        body, name=name,
        grid_spec=pltpu.PrefetchScalarGridSpec(
            num_scalar_prefetch=1, grid=(2,),
            in_specs=[pl.BlockSpec((1, R, C), lambda k, idx_ref: (idx_ref[k], 0, 0)),
                      pl.BlockSpec((1, R, C), lambda k, idx_ref: (k, 0, 0))],
            out_specs=pl.BlockSpec((1, R, C), lambda k, idx_ref: (k, 0, 0))),
        out_shape=jax.ShapeDtypeStruct((2, R, C), s.dtype),
        compiler_params=_cparams("parallel"),
    )(idx, s, got)


HBM = pl.BlockSpec(memory_space=pltpu.HBM)
SEM = pl.BlockSpec(memory_space=pltpu.SEMAPHORE)
EFFECT = pltpu.SideEffectType.DATAFLOW_SIDE_EFFECTING


def _peers(x, y, c):
    out = []
    for k in range(1, N_DEV):
        kx, ky, kc = (k >> 2) & 1, (k >> 1) & 1, k & 1
        out.append(((1 - x) if kx else x, (1 - y) if ky else y, (1 - c) if kc else c))
    return out


SPREAD_COPIES = {"gather": N_DEV - 1, "scatter": N_DEV - 1, "axis_a": 2, "axis_b": 1}
SPREAD_SLOTS = {"axis_a": 2, "axis_b": 1}


def _axis_chips(x, y, c):
    flip = lambda v, bit: v + bit - 2 * v * bit
    return (x, y), (flip(x, 1 - c), flip(y, c)), (flip(x, c), flip(y, 1 - c)), (1 - x, 1 - y)


def _spread_copy(src_ref, land_ref, send_sems, recv_sems, k, plan):
    x, y, c = _mesh_pos()
    if plan in ("axis_a", "axis_b"):
        _, a, b, d = _axis_chips(x, y, c)
        chip = lambda p: 2 * p[0] + p[1]
        peer = (*(a if plan == "axis_a" else b), c)
        src = src_ref.at[chip(a) if k == 0 else chip(d)] if plan == "axis_a" else src_ref.at[1]
        slot = k
    else:
        peer = _peers(x, y, c)[k]
        src, slot = (src_ref.at[_flat(*peer)] if plan == "scatter" else src_ref), _flat(x, y, c)
    return pltpu.make_async_remote_copy(
        src_ref=src, dst_ref=land_ref.at[slot], send_sem=send_sems.at[k], recv_sem=recv_sems.at[k],
        device_id=peer, device_id_type=MESH)


def _own_copy(src_ref, land_ref, send_sems, plan):
    me = _flat(*_mesh_pos())
    return pltpu.make_async_copy(src_ref.at[me] if plan == "scatter" else src_ref, land_ref.at[me],
                                 send_sems.at[SPREAD_COPIES[plan]])


def _spread_start(src, after, plan, name):
    land_shape = (N_DEV,) + src.shape if plan == "gather" else src.shape
    if plan in SPREAD_SLOTS:
        land_shape = (SPREAD_SLOTS[plan],) + src.shape[1:]
    n_copies = SPREAD_COPIES[plan]

    def body(src_ref, land_ref, after_ref, send_sems, recv_sems, src_thru, land_thru, token):
        for k in range(n_copies):
            _spread_copy(src_ref, land_ref, send_sems, recv_sems, k, plan).start()
        if plan not in SPREAD_SLOTS:
            _own_copy(src_ref, land_ref, send_sems, plan).start()
        token[...] = jnp.zeros_like(token)

    return _pcall(
        body, name=name,
        out_shape=(pltpu.SemaphoreType.DMA((n_copies + (plan not in SPREAD_SLOTS),)), pltpu.SemaphoreType.DMA((n_copies,)),
                   pltpu.HBM(src.shape, src.dtype), pltpu.HBM(land_shape, src.dtype), jax.ShapeDtypeStruct((8, LANE), f32)),
        in_specs=(HBM, HBM, ANY), out_specs=(SEM, SEM, HBM, HBM, pl.BlockSpec(memory_space=pltpu.VMEM)),
        input_output_aliases={0: 2, 1: 3},
        compiler_params=pltpu.CompilerParams(has_side_effects=EFFECT),
    )(pltpu.with_memory_space_constraint(src, pltpu.HBM),
      pltpu.with_memory_space_constraint(lax.empty(land_shape, src.dtype), pltpu.HBM), after)


def _spread_wait(started, after, plan, name):
    send_sems, recv_sems, src_thru, land_thru, _ = started

    def body(src_ref, land_ref, send_sems, recv_sems, after_ref, src_dead, got_ref):
        for k in range(SPREAD_COPIES[plan]):
            cp = _spread_copy(src_ref, land_ref, send_sems, recv_sems, k, plan)
            cp.wait_send()
            cp.wait_recv()
        if plan not in SPREAD_SLOTS:
            _own_copy(src_ref, land_ref, send_sems, plan).wait()

    return _pcall(
        body, name=name,
        out_shape=(pltpu.HBM(src_thru.shape, src_thru.dtype), pltpu.HBM(land_thru.shape, land_thru.dtype)),
        in_specs=(HBM, HBM, SEM, SEM, ANY), out_specs=(HBM, HBM), input_output_aliases={0: 0, 1: 1},
        compiler_params=pltpu.CompilerParams(has_side_effects=EFFECT),
    )(src_thru, land_thru, send_sems, recv_sems, after)


COL_TILE = 256


def _cast_weights(w3, wo):
    n = w3.shape[0]

    def body(w_ref, wo_ref, o_ref, oo_ref):
        o_ref[...] = w_ref[:, 0, :].astype(bf16)
        oo_ref[...] = wo_ref[...].astype(bf16)

    tile = 2 * COL_TILE
    return _pcall(
        body, name="cast_weights", grid=(D_MODEL // tile,),
        in_specs=[pl.BlockSpec((n, 1, tile), lambda j: (0, 0, j)), pl.BlockSpec((wo.shape[0], tile), lambda j: (0, j))],
        out_specs=[pl.BlockSpec((n, tile), lambda j: (0, j)), pl.BlockSpec((wo.shape[0], tile), lambda j: (0, j))],
        out_shape=[jax.ShapeDtypeStruct((n, D_MODEL), bf16), jax.ShapeDtypeStruct(wo.shape, bf16)],
        compiler_params=_cparams("parallel"),
    )(w3, wo)


def _relayout_w_in(win_g):
    def body(g_ref, o_ref):
        used = OFF_BA + NAT_CONV - NAT_BA
        o_ref[used:PROJ_PAD, :] = jnp.zeros((PROJ_PAD - used, COL_TILE), o_ref.dtype)
        for d in range(N_DEV):
            for lo, width, dst in _layout_segments(d * SHARD_W, (d + 1) * SHARD_W):
                src = lo - d * SHARD_W
                o_ref[dst:dst + width, :] = g_ref[d, src:src + width, :]

    return _pcall(
        body, name="relayout_w_in", grid=(D_MODEL // COL_TILE,),
        in_specs=[pl.BlockSpec((N_DEV, SHARD_W, COL_TILE), lambda j: (0, 0, j))],
        out_specs=pl.BlockSpec((PROJ_PAD, COL_TILE), lambda j: (0, j)),
        out_shape=jax.ShapeDtypeStruct((PROJ_PAD, D_MODEL), win_g.dtype),
        compiler_params=_cparams("parallel"),
    )(win_g)


def _grad_blocks(g_t):
    def body(p_ref, o_ref):
        for d in range(N_DEV):
            for lo, width, src in _layout_segments(d * SHARD_W, (d + 1) * SHARD_W):
                dst = lo - d * SHARD_W
                o_ref[d, dst:dst + width, :] = p_ref[src:src + width, :]

    return _pcall(
        body, name="grad_blocks", grid=(D_MODEL // COL_TILE,),
        in_specs=[pl.BlockSpec((PROJ_PAD, COL_TILE), lambda j: (0, j))],
        out_specs=pl.BlockSpec((N_DEV, SHARD_W, COL_TILE), lambda j: (0, 0, j)),
        out_shape=jax.ShapeDtypeStruct((N_DEV, SHARD_W, D_MODEL), bf16),
        compiler_params=_cparams("parallel"),
    )(g_t)


def _in_proj(x, nw, wpad_t, after):
    L = x.shape[0]
    tn = 768
    nj = wpad_t.shape[0] // tn

    def body(x_ref, nw_ref, w_ref, after_ref, proj_ref, h_ref):
        first = pl.program_id(0) == 0

        def project(r, n, hv):
            proj_ref[r:r + n, :] = lax.dot_general(hv, w_ref[...], (((1,), (1,)), ((), ())), preferred_element_type=f32)

        @pl.when(first)
        def _():
            for r in range(0, L, 256):
                xs = x_ref[r:r + 256, :]
                ms = jnp.mean(xs * xs, axis=-1, keepdims=True)
                hv = ((xs * lax.rsqrt(ms + EPS)) * nw_ref[...]).astype(bf16)
                h_ref[r:r + 256, :] = hv
                project(r, 256, hv)

        @pl.when(jnp.logical_not(first))
        def _():
            for r in range(0, L, 512):
                project(r, 512, h_ref[r:r + 512, :])

    return _pcall(
        body, name="in_proj", grid=(nj,),
        in_specs=[pl.BlockSpec((L, D_MODEL), lambda j: (0, 0)), pl.BlockSpec((1, D_MODEL), lambda j: (0, 0)),
                  pl.BlockSpec((tn, D_MODEL), lambda j: (j, 0)), ANY],
        out_specs=[pl.BlockSpec((L, tn), lambda j: (0, j)), pl.BlockSpec((L, D_MODEL), lambda j: (0, 0))],
        out_shape=[jax.ShapeDtypeStruct((L, wpad_t.shape[0]), f32), jax.ShapeDtypeStruct((L, D_MODEL), bf16)],
        compiler_params=_cparams("arbitrary"),
    )(x, nw, wpad_t, after)


HALVES = [slice(i * LANE, (i + 1) * LANE) for i in range(ELT_W // LANE)]
QKV_W = 512
QKV_HEADS = [slice(i * LANE, (i + 1) * LANE) for i in range(QKV_W // LANE)]
STEPS_PER_GROUP = GDN_WIDTH // QKV_W


def _conv4(x, cw_ref, ls):
    return (cw_ref[3:4, ls] * x + cw_ref[2:3, ls] * _shift_down(x, 1) + cw_ref[1:2, ls] * _shift_down(x, 2)
            + cw_ref[0:1, ls] * _shift_down(x, 3))


def _qkv_act(proj, cw):
    L = proj.shape[0]

    def body(x_ref, cw_ref, o_ref):
        j = pl.program_id(0)
        scale = jnp.where(j < STEPS_PER_GROUP, HEAD_DIM ** -0.5, 1.0).astype(f32)
        for ls in QKV_HEADS:
            c = _conv4(x_ref[:, ls], cw_ref, ls)
            a = c * _sigmoid(c)
            rn = lax.rsqrt(jnp.sum(a * a, axis=1, keepdims=True) + EPS)
            o_ref[:, ls] = jnp.where(j < 2 * STEPS_PER_GROUP, (a * rn) * scale, a)

    return _pcall(
        body, name="qkv_act", grid=(3 * STEPS_PER_GROUP,),
        in_specs=[pl.BlockSpec((L, QKV_W), lambda j: (0, j)), pl.BlockSpec((4, QKV_W), lambda j: (0, j))],
        out_specs=pl.BlockSpec((L, QKV_W), lambda j: (0, j)),
        out_shape=jax.ShapeDtypeStruct((L, 3 * GDN_WIDTH), f32),
        compiler_params=_cparams("parallel"),
    )(proj, cw)


def _scalars(proj, alog_p, dtb_p):
    L = proj.shape[0]
    nc = L // CHUNK

    def body(x_ref, al_ref, dt_ref, sc_ref, gr_ref):
        x = x_ref[...]
        lane = _lanes(x.shape)
        beta = _sigmoid(x)
        g = -jnp.exp(al_ref[...]) * _softplus(x + dt_ref[...])
        gc = jnp.where((lane >= HEADS) & (lane < 2 * HEADS), g, 0.0)
        rc = _rows(x.shape) & (CHUNK - 1)
        for s in (1, 2, 4, 8, 16, 32):
            gc = gc + jnp.where(rc >= s, pltpu.roll(gc, s, 0), 0.0)
        sc_ref[...] = jnp.where(lane < HEADS, beta, gc)
        sel = (_lanes((HEADS, LANE)) == _rows((HEADS, LANE)) + HEADS).astype(f32)
        for c in range(nc):
            gr_ref[c] = lax.dot_general(sel, sc_ref[c * CHUNK:(c + 1) * CHUNK, :], (((1,), (1,)), ((), ())),
                                        preferred_element_type=f32, precision=lax.Precision.HIGHEST)

    return _pcall(
        body, name="scalars", grid=(1,),
        in_specs=[pl.BlockSpec((L, LANE), lambda i: (0, OFF_BA // LANE)), pl.BlockSpec((1, LANE), lambda i: (0, 0)),
                  pl.BlockSpec((1, LANE), lambda i: (0, 0))],
        out_specs=[pl.BlockSpec((L, LANE), lambda i: (0, 0)), pl.BlockSpec((nc, HEADS, CHUNK), lambda i: (0, 0, 0))],
        out_shape=[jax.ShapeDtypeStruct((L, LANE), f32), jax.ShapeDtypeStruct((nc, HEADS, CHUNK), f32)],
        compiler_params=_cparams("arbitrary"),
    )(proj, alog_p, dtb_p)


def _head_scalars(sc, gr_ref, h, ci=0):
    lane = _lanes(sc.shape)
    beta = jnp.sum(jnp.where(lane == h, sc, 0.0), axis=1, keepdims=True)
    gcc = jnp.sum(jnp.where(lane == HEADS + h, sc, 0.0), axis=1, keepdims=True)
    gcr = gr_ref[ci, h:h + 1, :]
    gl = jnp.sum(jnp.where(_lanes(gcr.shape) == CHUNK - 1, gcr, 0.0), axis=1, keepdims=True)
    ii, jj = _rows((CHUNK, CHUNK)), _lanes((CHUNK, CHUNK))
    dmat = jnp.where(ii >= jj, jnp.exp(jnp.minimum(gcc - gcr, 0.0)), 0.0)
    dmat_t = jnp.where(jj >= ii, jnp.exp(jnp.minimum(gcr - gcc, 0.0)), 0.0)
    return beta, gcc, gl, dmat, dmat_t, ii, jj


def _gdn_fwd(qkv, sc, gr):
    L = qkv.shape[0]
    nc = L // CHUNK
    W = GDN_WIDTH
    cps = GDN_CPS if nc % GDN_CPS == 0 else 1
    rows_per_step = cps * CHUNK

    def body(qkv_ref, sc_ref, gr_ref, o_ref, u_ref, w_ref, vn_ref, t_ref, sp_ref, s_scr):
        @pl.when(pl.program_id(0) == 0)
        def _():
            s_scr[...] = jnp.zeros_like(s_scr)
        HS = range(cps * HEADS)
        hd = [i % HEADS for i in HS]
        rs = [slice((i // HEADS) * CHUNK, (i // HEADS + 1) * CHUNK) for i in HS]
        cs = [slice(hd[i] * HEAD_DIM, (hd[i] + 1) * HEAD_DIM) for i in HS]
        q = [qkv_ref[rs[i], hd[i] * HEAD_DIM:(hd[i] + 1) * HEAD_DIM] for i in HS]
        k = [qkv_ref[rs[i], W + hd[i] * HEAD_DIM:W + (hd[i] + 1) * HEAD_DIM] for i in HS]
        v = [qkv_ref[rs[i], 2 * W + hd[i] * HEAD_DIM:2 * W + (hd[i] + 1) * HEAD_DIM] for i in HS]
        hsc = [_head_scalars(sc_ref[rs[i], :], gr_ref, hd[i], i // HEADS) for i in HS]
        beta, gcc, gl, dmat = ([x[i] for x in hsc] for i in range(4))
        ii, jj = hsc[0][5], hsc[0][6]
        eg = [jnp.exp(gcc[h]) for h in HS]
        kb = [k[h] * beta[h] for h in HS]
        kk = [_mm_nt(kb[h], k[h]) for h in HS]
        qk = [_mm_nt(q[h], k[h]) for h in HS]
        n0 = [-jnp.where(ii > jj, kk[h] * dmat[h], 0.0) for h in HS]
        n1 = [_mm(n0[h], n0[h]) for h in HS]
        n2 = [_mm(n1[h], n1[h]) for h in HS]
        p01 = [n0[h] + n1[h] + _mm(n0[h], n1[h]) for h in HS]
        n3 = [_mm(n2[h], n2[h]) for h in HS]
        n4 = [_mm(n3[h], n3[h]) for h in HS]
        p23 = [n2[h] + n3[h] + _mm(n2[h], n3[h]) for h in HS]
        n5 = [_mm(n4[h], n4[h]) for h in HS]
        p03 = [p01[h] + p23[h] + _mm(p01[h], p23[h]) for h in HS]
        p45 = [n4[h] + n5[h] + _mm(n4[h], n5[h]) for h in HS]
        t = [p03[h] + p45[h] + _mm(p03[h], p45[h]) for h in HS]
        vb = [v[h] * beta[h] for h in HS]
        kbg = [kb[h] * eg[h] for h in HS]
        uw = [_mm(t[h], _cat16([vb[h], kbg[h]], 1)) for h in HS]
        u = [vb[h] + uw[h][:, :HEAD_DIM] for h in HS]
        w = [kbg[h] + uw[h][:, HEAD_DIM:] for h in HS]
        wq = [_cat16([w[h], q[h] * eg[h]], 0) for h in HS]
        p = [jnp.where(ii >= jj, qk[h] * dmat[h], 0.0) for h in HS]
        ks = [k[h] * jnp.exp(gl[h] - gcc[h]) for h in HS]
        s = [s_scr[h] for h in range(HEADS)]
        for ci in range(cps):
            IS = range(ci * HEADS, (ci + 1) * HEADS)
            ws = [_mm(wq[i], s[hd[i]]) for i in IS]
            vn = [u[i] - ws[hd[i]][:CHUNK] for i in IS]
            pv = [_mm(p[i], vn[hd[i]]) for i in IS]
            kv = [_mm_tn(ks[i], vn[hd[i]]) for i in IS]
            for i in IS:
                h = hd[i]
                sp_ref[ci, cs[i], :] = s[h]
                o_ref[rs[i], cs[i]] = ws[h][CHUNK:] + pv[h]
                vn_ref[rs[i], cs[i]] = vn[h].astype(bf16)
            s = [jnp.exp(gl[i]) * s[hd[i]] + kv[hd[i]] for i in IS]
        for h in range(HEADS):
            s_scr[h] = s[h]
        for i in HS:
            u_ref[rs[i], cs[i]] = u[i].astype(bf16)
            w_ref[rs[i], cs[i]] = w[i].astype(bf16)
            t_ref[i // HEADS, hd[i]] = t[i].astype(bf16)

    row = lambda c: (c, 0)
    act, act16 = jax.ShapeDtypeStruct((L, W), f32), jax.ShapeDtypeStruct((L, W), bf16)
    return _pcall(
        body, name="gdn_fwd", grid=(nc // cps,),
        in_specs=[pl.BlockSpec((rows_per_step, 3 * W), row), pl.BlockSpec((rows_per_step, LANE), row),
                  pl.BlockSpec((cps, HEADS, CHUNK), lambda c: (c, 0, 0))],
        out_specs=[pl.BlockSpec((rows_per_step, W), row)] * 4 + [
            pl.BlockSpec((cps, HEADS, CHUNK, CHUNK), lambda c: (c, 0, 0, 0)),
            pl.BlockSpec((cps, W, HEAD_DIM), lambda c: (c, 0, 0))],
        out_shape=[act, act16, act16, act16, jax.ShapeDtypeStruct((nc, HEADS, CHUNK, CHUNK), bf16),
                   jax.ShapeDtypeStruct((nc, W, HEAD_DIM), f32)],
        scratch_shapes=[pltpu.VMEM((HEADS, HEAD_DIM, HEAD_DIM), f32)],
        compiler_params=_cparams("arbitrary"),
    )(qkv, sc, gr)


def _gdn_gate(o, proj, gnw):
    L = o.shape[0]

    def body(o_ref, z_ref, w_ref, m_ref):
        for ls in HALVES:
            ov, z = o_ref[:, ls], z_ref[:, ls]
            rms = lax.rsqrt(jnp.mean(ov * ov, axis=-1, keepdims=True) + EPS)
            m_ref[:, ls] = (((ov * rms) * w_ref[...]) * (z * _sigmoid(z))).astype(bf16)

    return _pcall(
        body, name="gdn_gate", grid=(GDN_WIDTH // ELT_W,),
        in_specs=[pl.BlockSpec((L, ELT_W), lambda j: (0, j)), pl.BlockSpec((L, ELT_W), lambda j: (0, OFF_ZG // ELT_W + j)),
                  pl.BlockSpec((1, LANE), lambda j: (0, 0))],
        out_specs=pl.BlockSpec((L, ELT_W), lambda j: (0, j)),
        out_shape=jax.ShapeDtypeStruct((L, GDN_WIDTH + CONV_WIDTH), bf16),
        compiler_params=_cparams("parallel"),
    )(o, proj, gnw)


def _conv3(u, cw_ref, ls):
    return cw_ref[2:3, ls] * u + cw_ref[1:2, ls] * _shift_down(u, 1) + cw_ref[0:1, ls] * _shift_down(u, 2)


def _conv_specs(L):
    return [pl.BlockSpec((L, CONV_BLOCK), lambda j: (0, OFF_CONV // CONV_BLOCK + j)),
            pl.BlockSpec((3, ELT_W), lambda j: (0, j)), pl.BlockSpec((1, ELT_W), lambda j: (0, j))]


def _conv_parts(ls):
    return [slice(g * ELT_W + ls.start, g * ELT_W + ls.stop) for g in range(4)]


def _conv_fwd(proj, cw, cb, mix):
    L = proj.shape[0]

    def body(p_ref, cw_ref, cb_ref, mix_in, m_ref):
        for ls in HALVES:
            sb, sc_, sh, sz = _conv_parts(ls)
            z = p_ref[:, sz]
            cv = _conv3(p_ref[:, sc_] * p_ref[:, sh], cw_ref, ls) + cb_ref[:, ls]
            m_ref[:, ls] = ((p_ref[:, sb] * cv) * (z * _sigmoid(z))).astype(bf16)

    return _pcall(
        body, name="conv_fwd", grid=(CONV_WIDTH // ELT_W,),
        in_specs=_conv_specs(L) + [ANY], out_specs=pl.BlockSpec((L, ELT_W), lambda j: (0, GDN_WIDTH // ELT_W + j)),
        out_shape=jax.ShapeDtypeStruct(mix.shape, mix.dtype), input_output_aliases={3: 0},
        compiler_params=_cparams("parallel"),
    )(proj, cw, cb, mix)


def _out_proj_loss(x, mix, wo, fw, tgt):
    L = x.shape[0]
    tm = min(512, L)
    MW = GDN_WIDTH + CONV_WIDTH

    def body(x_ref, m_ref, wo_ref, fw_ref, t_ref, dy_ref, dyb_ref, dm_ref, gfw_ref, loss_ref):
        @pl.when(pl.program_id(0) == 0)
        def _():
            gfw_ref[...] = jnp.zeros_like(gfw_ref)
            loss_ref[...] = jnp.zeros_like(loss_ref)
        y = x_ref[...] + jnp.dot(m_ref[...], wo_ref[...], preferred_element_type=f32)
        r = lax.rsqrt(jnp.mean(y * y, axis=-1, keepdims=True) + EPS)
        yh = y * r
        fwv = fw_ref[...]
        diff = yh * fwv - t_ref[...]
        loss_ref[...] += jnp.sum(jnp.sum(diff * diff, axis=-1, keepdims=True), axis=0, keepdims=True) * (0.5 / D_MODEL)
        dout = diff * (1.0 / D_MODEL)
        gfw_ref[...] += jnp.sum(dout * yh, axis=0, keepdims=True)
        dyh = dout * fwv
        dy = r * (dyh - yh * jnp.mean(dyh * yh, axis=-1, keepdims=True))
        dy_ref[...] = dy
        dyb = dy.astype(bf16)
        dyb_ref[...] = dyb
        dm_ref[...] = lax.dot_general(dyb, wo_ref[...], (((1,), (1,)), ((), ())), preferred_element_type=f32)

    row = lambda i: (i, 0)
    fix = lambda i: (0, 0)
    act = jax.ShapeDtypeStruct((L, D_MODEL), f32)
    return _pcall(
        body, name="out_proj_loss", grid=(L // tm,),
        in_specs=[pl.BlockSpec((tm, D_MODEL), row), pl.BlockSpec((tm, MW), row), pl.BlockSpec((MW, D_MODEL), fix),
                  pl.BlockSpec((1, D_MODEL), fix), pl.BlockSpec((tm, D_MODEL), row)],
        out_specs=[pl.BlockSpec((tm, D_MODEL), row), pl.BlockSpec((tm, D_MODEL), row), pl.BlockSpec((tm, MW), row),
                   pl.BlockSpec((1, D_MODEL), fix), pl.BlockSpec((1, LANE), fix)],
        out_shape=[act, jax.ShapeDtypeStruct((L, D_MODEL), bf16), jax.ShapeDtypeStruct((L, MW), f32),
                   jax.ShapeDtypeStruct((1, D_MODEL), f32), jax.ShapeDtypeStruct((1, LANE), f32)],
        compiler_params=_cparams("arbitrary"),
    )(x, mix, wo, fw, tgt)


def _tn_matmul(a, b, name):
    L, M = a.shape
    N = b.shape[1]
    tm = 512 if M % 512 == 0 else (768 if M % 768 == 0 else M)

    def body(a_ref, b_ref, o_ref):
        o_ref[...] = lax.dot_general(a_ref[...], b_ref[...], (((0,), (0,)), ((), ())),
                                     preferred_element_type=f32).astype(o_ref.dtype)

    return _pcall(
        body, name=name, grid=(M // tm,),
        in_specs=[pl.BlockSpec((L, tm), lambda i: (0, i)), pl.BlockSpec((L, N), lambda i: (0, 0))],
        out_specs=pl.BlockSpec((tm, N), lambda i: (i, 0)),
        out_shape=jax.ShapeDtypeStruct((M, N), bf16),
        compiler_params=_cparams("parallel"),
    )(a, b)


def _gdn_gate_bwd(o, proj, gnw, dmix_a, after):
    L = o.shape[0]

    def body(o_ref, z_ref, w_ref, dm_ref, after_ref, do_ref, dz_ref, gw_ref):
        @pl.when(pl.program_id(0) == 0)
        def _():
            gw_ref[...] = jnp.zeros_like(gw_ref)
        wv = w_ref[...]
        for ls in HALVES:
            ov, z, dm = o_ref[:, ls], z_ref[:, ls], dm_ref[:, ls]
            rms = lax.rsqrt(jnp.mean(ov * ov, axis=-1, keepdims=True) + EPS)
            xh = ov * rms
            sg = _sigmoid(z)
            d_on = dm * (z * sg)
            dz_ref[:, ls] = (dm * (xh * wv) * (sg * (1.0 + z * (1.0 - sg)))).astype(bf16)
            gw_ref[...] += jnp.sum(d_on * xh, axis=0, keepdims=True)
            dxh = d_on * wv
            do_ref[:, ls] = (rms * (dxh - xh * jnp.mean(dxh * xh, axis=-1, keepdims=True))).astype(bf16)

    wide = pl.BlockSpec((L, ELT_W), lambda j: (0, j))
    return _pcall(
        body, name="gdn_gate_bwd", grid=(GDN_WIDTH // ELT_W,),
        in_specs=[wide, pl.BlockSpec((L, ELT_W), lambda j: (0, OFF_ZG // ELT_W + j)),
                  pl.BlockSpec((1, LANE), lambda j: (0, 0)), wide, ANY],
        out_specs=[wide, pl.BlockSpec((L, ELT_W), lambda j: (0, OFF_ZG // ELT_W + j)),
                   pl.BlockSpec((1, LANE), lambda j: (0, 0))],
        out_shape=[jax.ShapeDtypeStruct((L, GDN_WIDTH), bf16), jax.ShapeDtypeStruct((L, PROJ_PAD), bf16),
                   jax.ShapeDtypeStruct((1, LANE), f32)],
        compiler_params=_cparams("arbitrary"),
    )(o, proj, gnw, dmix_a, after)


def _conv_bwd(proj, cw, cb, dmix_b, dproj):
    L = proj.shape[0]

    def body(p_ref, cw_ref, cb_ref, dm_ref, dproj_in, dp_ref, gcw_ref, gcb_ref):
        for ls in HALVES:
            sb, sc_, sh, sz_ = _conv_parts(ls)
            bv, cv_, hv, z, dm = p_ref[:, sb], p_ref[:, sc_], p_ref[:, sh], p_ref[:, sz_], dm_ref[:, ls]
            u = cv_ * hv
            cv = _conv3(u, cw_ref, ls) + cb_ref[:, ls]
            sg = _sigmoid(z)
            sz = z * sg
            dp_ref[:, sb] = (dm * cv * sz).astype(bf16)
            dp_ref[:, sz_] = (dm * (bv * cv) * (sg * (1.0 + z * (1.0 - sg)))).astype(bf16)
            dcv = dm * bv * sz
            gcb_ref[:, ls] = jnp.sum(dcv, axis=0, keepdims=True)
            dcv1, dcv2 = _shift_up(dcv, 1), _shift_up(dcv, 2)
            gcw_ref[2:3, ls] = jnp.sum(dcv * u, axis=0, keepdims=True)
            gcw_ref[1:2, ls] = jnp.sum(dcv1 * u, axis=0, keepdims=True)
            gcw_ref[0:1, ls] = jnp.sum(dcv2 * u, axis=0, keepdims=True)
            du = cw_ref[2:3, ls] * dcv + cw_ref[1:2, ls] * dcv1 + cw_ref[0:1, ls] * dcv2
            dp_ref[:, sc_] = (du * hv).astype(bf16)
            dp_ref[:, sh] = (du * cv_).astype(bf16)

    return _pcall(
        body, name="conv_bwd", grid=(CONV_WIDTH // ELT_W,),
        in_specs=_conv_specs(L) + [pl.BlockSpec((L, ELT_W), lambda j: (0, GDN_WIDTH // ELT_W + j)), ANY],
        out_specs=[pl.BlockSpec((L, CONV_BLOCK), lambda j: (0, OFF_CONV // CONV_BLOCK + j)),
                   pl.BlockSpec((3, ELT_W), lambda j: (0, j)), pl.BlockSpec((1, ELT_W), lambda j: (0, j))],
        out_shape=[jax.ShapeDtypeStruct(dproj.shape, dproj.dtype), jax.ShapeDtypeStruct((3, CONV_WIDTH), f32),
                   jax.ShapeDtypeStruct((1, CONV_WIDTH), f32)],
        input_output_aliases={4: 0},
        compiler_params=_cparams("parallel"),
    )(proj, cw, cb, dmix_b, dproj)


def _gdn_bwd(qkv, sc, gr, u_all, w_all, vn_all, t_all, sp_all, do_all):
    L = qkv.shape[0]
    nc = L // CHUNK
    W = GDN_WIDTH
    cps = GDN_CPS_BWD if nc % GDN_CPS_BWD == 0 else 1
    rows_per_step = cps * CHUNK
    nsteps = nc // cps

    def body(qkv_ref, sc_ref, gr_ref, u_ref, w_ref, vn_ref, t_ref, sp_ref, do_ref, dqkv_ref, dsc_ref, dgr_ref, ds_scr):
        @pl.when(pl.program_id(0) == 0)
        def _():
            ds_scr[...] = jnp.zeros_like(ds_scr)
        nh, base = HEADS, 0
        HS = range(cps * nh)
        hl = [i % nh for i in HS]
        hd = [base + hl[i] for i in HS]
        rs = [slice((i // nh) * CHUNK, (i // nh + 1) * CHUNK) for i in HS]
        cs = [slice(hd[i] * HEAD_DIM, (hd[i] + 1) * HEAD_DIM) for i in HS]
        q = [qkv_ref[rs[i], hd[i] * HEAD_DIM:(hd[i] + 1) * HEAD_DIM] for i in HS]
        k = [qkv_ref[rs[i], W + hd[i] * HEAD_DIM:W + (hd[i] + 1) * HEAD_DIM] for i in HS]
        v = [qkv_ref[rs[i], 2 * W + hd[i] * HEAD_DIM:2 * W + (hd[i] + 1) * HEAD_DIM] for i in HS]
        hsc = [_head_scalars(sc_ref[rs[i], :], gr_ref, hd[i], i // nh) for i in HS]
        beta, gcc, gl, dmat, dmat_t = ([x[i] for x in hsc] for i in range(5))
        ii, jj = hsc[0][5], hsc[0][6]
        eg = [jnp.exp(gcc[h]) for h in HS]
        ekl = [jnp.exp(gl[h] - gcc[h]) for h in HS]
        egl = [jnp.exp(gl[h]) for h in HS]
        kb = [k[h] * beta[h] for h in HS]
        ks = [k[h] * ekl[h] for h in HS]
        do = [do_ref[rs[h], cs[h]] for h in HS]
        vn = [vn_ref[rs[h], cs[h]] for h in HS]
        s = [sp_ref[h // nh, cs[h], :] for h in HS]
        w = [w_ref[rs[h], cs[h]] for h in HS]
        qd = [q[h] * eg[h] for h in HS]

        kq = [_mm_nt(k[h], q[h]) for h in HS]
        p_t = [jnp.where(jj >= ii, kq[h] * dmat_t[h], 0.0) for h in HS]
        ptd = [_mm(p_t[h], do[h]) for h in HS]
        qw = [_cat16([qd[h], -w[h]], 0) for h in HS]
        dsn, dvn, dodv = [None] * len(HS), [None] * len(HS), [None] * len(HS)
        ds_cur = [ds_scr[base + h] for h in range(nh)]
        for ci in reversed(range(cps)):
            IS = range(ci * nh, (ci + 1) * nh)
            ksd = [_mm(ks[i], ds_cur[hl[i]]) for i in IS]
            for i in IS:
                dsn[i] = ds_cur[hl[i]]
                dvn[i] = ptd[i] + ksd[hl[i]]
                dodv[i] = _cat16([do[i], dvn[i]], 0)
            dsq = [_mm_tn(qw[i], dodv[i]) for i in IS]
            ds_cur = [egl[i] * ds_cur[hl[i]] + dsq[hl[i]] for i in IS]
        for h in range(nh):
            ds_scr[base + h] = ds_cur[h]
        x1 = [_mm_nt(dodv[h], s[h]) for h in HS]
        dks = [_mm_nt(vn[h], dsn[h]) for h in HS]
        dov = [_mm_nt(do[h], vn[h]) for h in HS]
        vdo = [_mm_nt(vn[h], do[h]) for h in HS]
        kk = [_mm_nt(kb[h], k[h]) for h in HS]
        qk = [_mm_nt(q[h], k[h]) for h in HS]
        dgl = [egl[h] * jnp.sum(jnp.sum(s[h] * dsn[h], axis=1, keepdims=True), axis=0, keepdims=True) for h in HS]
        dqd = [x1[h][:CHUNK] for h in HS]
        duw = [jnp.concatenate([dvn[h], -x1[h][CHUNK:]], axis=1) for h in HS]
        tdu = [_mm_tn(t_ref[h // nh, hd[h]], duw[h]) for h in HS]
        dvk = [duw[h] + tdu[h] for h in HS]
        uw = [jnp.concatenate([u_ref[rs[h], cs[h]], w[h]], axis=1) for h in HS]
        da = [-jnp.where(ii > jj, _mm_nt(dvk[h], uw[h]), 0.0) for h in HS]
        da_t = [-jnp.where(jj > ii, _mm_nt(uw[h], dvk[h]), 0.0) for h in HS]
        dp = [jnp.where(ii >= jj, dov[h], 0.0) for h in HS]
        dp_t = [jnp.where(jj >= ii, vdo[h], 0.0) for h in HS]
        r1 = [_mm(_cat16([da[h] * dmat[h], dp[h] * dmat[h]], 0), k[h]) for h in HS]
        dk1 = [_mm(_cat16([da_t[h] * dmat_t[h], dp_t[h] * dmat_t[h]], 1), _cat16([kb[h], q[h]], 0)) for h in HS]
        lane = _lanes((CHUNK, LANE))
        for ci in range(cps):
            dsc = jnp.zeros((CHUNK, LANE), f32)
            for i in range(ci * nh, (ci + 1) * nh):
                h = hd[i]
                a = jnp.where(ii > jj, kk[i] * dmat[i], 0.0)
                p = jnp.where(ii >= jj, qk[i] * dmat[i], 0.0)
                gmat = da[i] * a + dp[i] * p
                dvb, dkbg = dvk[i][:, :HEAD_DIM], dvk[i][:, HEAD_DIM:]
                kbg = kb[i] * eg[i]
                dkb = r1[i][:CHUNK] + dkbg * eg[i]
                dq = r1[i][CHUNK:] + dqd[i] * eg[i]
                dk = dk1[i] + dks[i] * ekl[i] + dkb * beta[i]
                dbeta = jnp.sum(dkb * k[i] + dvb * v[i], axis=1, keepdims=True)
                ksum = jnp.sum(dks[i] * ks[i], axis=1, keepdims=True)
                dgl_tot = dgl[i] + jnp.sum(ksum, axis=0, keepdims=True)
                dgc = (jnp.sum(gmat, axis=1, keepdims=True) + jnp.sum(dqd[i] * qd[i] + dkbg * kbg, axis=1, keepdims=True)
                       - ksum)
                dgc = dgc + jnp.where(_rows(dgc.shape) == CHUNK - 1, dgl_tot, 0.0)
                dqkv_ref[rs[i], h * HEAD_DIM:(h + 1) * HEAD_DIM] = dq
                dqkv_ref[rs[i], W + h * HEAD_DIM:W + (h + 1) * HEAD_DIM] = dk
                dqkv_ref[rs[i], 2 * W + h * HEAD_DIM:2 * W + (h + 1) * HEAD_DIM] = dvb * beta[i]
                dsc = jnp.where(lane == h, dbeta, jnp.where(lane == HEADS + h, dgc, dsc))
                dgr_ref[ci, h:h + 1, :] = jnp.sum(gmat, axis=0, keepdims=True)
            dsc_ref[ci * CHUNK:(ci + 1) * CHUNK, :] = dsc

    row = lambda c: (nsteps - 1 - c, 0)
    lead3 = lambda c: (nsteps - 1 - c, 0, 0)
    return _pcall(
        body, name="gdn_bwd", grid=(nsteps,),
        in_specs=[pl.BlockSpec((rows_per_step, 3 * W), row), pl.BlockSpec((rows_per_step, LANE), row),
                  pl.BlockSpec((cps, HEADS, CHUNK), lead3),
                  pl.BlockSpec((rows_per_step, W), row), pl.BlockSpec((rows_per_step, W), row),
                  pl.BlockSpec((rows_per_step, W), row),
                  pl.BlockSpec((cps, HEADS, CHUNK, CHUNK), lambda c: (nsteps - 1 - c, 0, 0, 0)),
                  pl.BlockSpec((cps, W, HEAD_DIM), lead3), pl.BlockSpec((rows_per_step, W), row)],
        out_specs=[pl.BlockSpec((rows_per_step, 3 * W), row), pl.BlockSpec((rows_per_step, LANE), row),
                   pl.BlockSpec((cps, HEADS, CHUNK), lead3)],
        out_shape=[jax.ShapeDtypeStruct((L, 3 * W), f32), jax.ShapeDtypeStruct((L, LANE), f32),
                   jax.ShapeDtypeStruct((nc, HEADS, CHUNK), f32)],
        scratch_shapes=[pltpu.VMEM((HEADS, HEAD_DIM, HEAD_DIM), f32)],
        compiler_params=_cparams("arbitrary"),
    )(qkv, sc, gr, u_all, w_all, vn_all, t_all, sp_all, do_all)


def _qkv_bwd(proj, cw, dn, dproj):
    L = proj.shape[0]

    def body(x_ref, cw_ref, dn_ref, dproj_in, dx_ref, gcw_ref):
        j = pl.program_id(0)
        steps = GDN_WIDTH // ELT_W
        scale = jnp.where(j < steps, HEAD_DIM ** -0.5, 1.0).astype(f32)
        for ls in HALVES:
            x, dn_v = x_ref[:, ls], dn_ref[:, ls]
            c = _conv4(x, cw_ref, ls)
            sg = _sigmoid(c)
            a = c * sg
            rn = lax.rsqrt(jnp.sum(a * a, axis=1, keepdims=True) + EPS)
            da_n = (scale * rn) * (dn_v - a * ((rn * rn) * jnp.sum(dn_v * a, axis=1, keepdims=True)))
            da = jnp.where(j < 2 * steps, da_n, dn_v)
            dc = da * (sg * (1.0 + c * (1.0 - sg)))
            dc1, dc2, dc3 = _shift_up(dc, 1), _shift_up(dc, 2), _shift_up(dc, 3)
            gcw_ref[3:4, ls] = jnp.sum(dc * x, axis=0, keepdims=True)
            gcw_ref[2:3, ls] = jnp.sum(dc1 * x, axis=0, keepdims=True)
            gcw_ref[1:2, ls] = jnp.sum(dc2 * x, axis=0, keepdims=True)
            gcw_ref[0:1, ls] = jnp.sum(dc3 * x, axis=0, keepdims=True)
            dx = cw_ref[3:4, ls] * dc + cw_ref[2:3, ls] * dc1 + cw_ref[1:2, ls] * dc2 + cw_ref[0:1, ls] * dc3
            dx_ref[:, ls] = dx.astype(bf16)

    col = pl.BlockSpec((L, ELT_W), lambda j: (0, j))
    wspec = pl.BlockSpec((4, ELT_W), lambda j: (0, j))
    return _pcall(
        body, name="qkv_bwd", grid=(3 * GDN_WIDTH // ELT_W,),
        in_specs=[col, wspec, col, ANY], out_specs=[col, wspec],
        out_shape=[jax.ShapeDtypeStruct(dproj.shape, dproj.dtype), jax.ShapeDtypeStruct((4, 3 * GDN_WIDTH), f32)],
        input_output_aliases={3: 0},
        compiler_params=_cparams("parallel"),
    )(proj, cw, dn, dproj)


def _scalars_bwd(proj, alog_p, dtb_p, dsc, dgr_col, dproj, after):
    L = proj.shape[0]

    def body(x_ref, al_ref, dt_ref, dsc_ref, dgr_ref, dproj_in, after_ref, dba_ref, gs_ref):
        x, dsc_v = x_ref[...], dsc_ref[...]
        lane = _lanes(x.shape)
        dec = (lane >= HEADS) & (lane < 2 * HEADS)
        dg = jnp.where(dec, dsc_v - dgr_ref[...], 0.0)
        rc = _rows(x.shape) & (CHUNK - 1)
        for s in (1, 2, 4, 8, 16, 32):
            dg = dg + jnp.where(rc + s < CHUNK, pltpu.roll(dg, L - s, 0), 0.0)
        xa = x + dt_ref[...]
        ea = jnp.exp(al_ref[...])
        g = -ea * _softplus(xa)
        da = dg * (-ea) * _sigmoid(xa)
        beta = _sigmoid(x)
        db = dsc_v * beta * (1.0 - beta)
        dba_ref[:, :LANE] = jnp.where(lane < HEADS, db, jnp.where(dec, da, 0.0)).astype(bf16)
        dba_ref[:, LANE:] = jnp.zeros((L, ELT_W - LANE), bf16)
        g_al = jnp.sum(jnp.where(dec, dg * g, 0.0), axis=0, keepdims=True)
        g_dt = jnp.sum(jnp.where(dec, da, 0.0), axis=0, keepdims=True)
        row8 = _rows(gs_ref.shape)
        gs = jnp.where(row8 == 0, g_al, jnp.where(row8 == 1, g_dt, 0.0))
        gs_ref[...] = pltpu.roll(gs, LANE - HEADS, 1)

    full = pl.BlockSpec((L, LANE), lambda i: (0, 0))
    vec = pl.BlockSpec((1, LANE), lambda i: (0, 0))
    return _pcall(
        body, name="scalars_bwd", grid=(1,),
        in_specs=[pl.BlockSpec((L, LANE), lambda i: (0, OFF_BA // LANE)), vec, vec, full, full, ANY, ANY],
        out_specs=[pl.BlockSpec((L, ELT_W), lambda i: (0, OFF_BA // ELT_W)), pl.BlockSpec((8, LANE), lambda i: (0, 0))],
        out_shape=[jax.ShapeDtypeStruct(dproj.shape, dproj.dtype), jax.ShapeDtypeStruct((8, LANE), f32)],
        input_output_aliases={5: 0},
        compiler_params=_cparams("arbitrary"),
    )(proj, alog_p, dtb_p, dsc, dgr_col, dproj, after)


def _input_grad(dproj, wpad, x, nw, dy, after):
    L = x.shape[0]
    tm = min(512, L)
    cuts = (0, 1024, 3072, 5120, 7168, PROJ_PAD)
    nk = len(cuts) - 1

    def body(dp_ref, w_hbm, x_ref, nw_ref, dy_ref, after_ref, gx_ref, gnw_ref, w_vmem, sems):
        first = pl.program_id(0) == 0
        loads = [pltpu.make_async_copy(w_hbm.at[cuts[k]:cuts[k + 1], :], w_vmem.at[cuts[k]:cuts[k + 1], :], sems.at[k])
                 for k in range(nk)]

        @pl.when(first)
        def _():
            for cp in loads:
                cp.start()
            gnw_ref[...] = jnp.zeros_like(gnw_ref)
        dh = None
        for k in range(nk):
            pl.when(first)(loads[k].wait)
            part = jnp.dot(dp_ref[:, cuts[k]:cuts[k + 1]], w_vmem[cuts[k]:cuts[k + 1], :], preferred_element_type=f32)
            dh = part if dh is None else dh + part
        xv, nwv = x_ref[...], nw_ref[...]
        r = lax.rsqrt(jnp.mean(xv * xv, axis=-1, keepdims=True) + EPS)
        xh = xv * r
        gnw_ref[...] += jnp.sum(dh * xh, axis=0, keepdims=True)
        dxh = dh * nwv
        gx_ref[...] = dy_ref[...] + r * (dxh - xh * jnp.mean(dxh * xh, axis=-1, keepdims=True))

    row = lambda i: (i, 0)
    fix = lambda i: (0, 0)
    return _pcall(
        body, name="input_grad", grid=(L // tm,),
        in_specs=[pl.BlockSpec((tm, PROJ_PAD), row), ANY, pl.BlockSpec((tm, D_MODEL), row),
                  pl.BlockSpec((1, D_MODEL), fix), pl.BlockSpec((tm, D_MODEL), row), ANY],
        out_specs=[pl.BlockSpec((tm, D_MODEL), row), pl.BlockSpec((1, D_MODEL), fix)],
        out_shape=[jax.ShapeDtypeStruct((L, D_MODEL), f32), jax.ShapeDtypeStruct((1, D_MODEL), f32)],
        scratch_shapes=[pltpu.VMEM(wpad.shape, bf16), pltpu.SemaphoreType.DMA((nk,))],
        compiler_params=_cparams("arbitrary"),
    )(dproj, wpad, x, nw, dy, after)


def _adamw_reduce(parts, w, m, v, name, first_row=None):
    R, C = w.shape[0], w.shape[-1]
    n_parts = parts.shape[0]
    tr = 128 if R % 128 == 0 else R
    c1 = 1.0 - ADAM_B1 ** ADAM_STEP
    c2 = 1.0 - ADAM_B2 ** ADAM_STEP
    at = (slice(None), 0, slice(None)) if w.ndim == 3 else Ellipsis
    window = (slice(None), slice(None)) if first_row is None else (slice(first_row, first_row + R), slice(0, C))

    def body(p_ref, w_ref, m_ref, v_ref, g_ref, d_ref, nm_ref, nv_ref):
        g = p_ref[(0,) + window].astype(f32)
        for s in range(1, n_parts):
            g = g + p_ref[(s,) + window].astype(f32)
        nm = ADAM_B1 * m_ref[at] + (1.0 - ADAM_B1) * g
        nv = ADAM_B2 * v_ref[at] + (1.0 - ADAM_B2) * (g * g)
        g_ref[at] = g
        nm_ref[at] = nm
        nv_ref[at] = nv
        d_ref[at] = -ADAM_LR * ((nm / c1) / (jnp.sqrt(nv / c2) + ADAM_EPS) + ADAM_WD * w_ref[at])

    blk = pl.BlockSpec((tr, 1, C), lambda i: (i, 0, 0)) if w.ndim == 3 else pl.BlockSpec((tr, C), lambda i: (i, 0))
    out = jax.ShapeDtypeStruct(w.shape, f32)
    if first_row is None:
        p_spec = pl.BlockSpec((n_parts, tr, C), lambda i: (0, i, 0))
    else:
        assert tr == R
        p_spec = pl.BlockSpec(parts.shape, lambda i: (0, 0, 0))
    return _pcall(
        body, name=name, grid=(R // tr,),
        in_specs=[p_spec, blk, blk, blk],
        out_specs=[blk] * 4, out_shape=[out] * 4,
        compiler_params=_cparams("parallel"),
    )(parts, w, m, v)


SMALL_SLOTS = ((0, D_MODEL), (D_MODEL, D_MODEL), (2 * D_MODEL, D_MODEL), (3 * D_MODEL, LANE),
               (3 * D_MODEL + LANE, HEADS), (3 * D_MODEL + 2 * LANE, HEADS))
SMALL_LOSS = 3 * D_MODEL + 3 * LANE
SMALL_W = SMALL_LOSS + LANE


def _pack_small(gs, after):
    def body(nw_ref, cb_ref, fw_ref, gn_ref, sc_ref, ls_ref, after_ref, o_ref):
        for ref, (start, width) in zip((nw_ref, cb_ref, fw_ref, gn_ref), SMALL_SLOTS[:4]):
            o_ref[:, start:start + width] = ref[...]
        o_ref[:, SMALL_SLOTS[4][0]:SMALL_SLOTS[4][0] + LANE] = sc_ref[0:1, :]
        o_ref[:, SMALL_SLOTS[5][0]:SMALL_SLOTS[5][0] + LANE] = sc_ref[1:2, :]
        o_ref[:, SMALL_LOSS:SMALL_W] = ls_ref[...]

    vm = pl.BlockSpec(memory_space=pltpu.VMEM)
    return _pcall(body, name="pack_small_grads", out_shape=jax.ShapeDtypeStruct((1, SMALL_W), f32),
                  in_specs=[vm] * 6 + [ANY], out_specs=vm)(*gs, after)


def _adamw_small(parts, ws, ms, vs):
    c1 = 1.0 - ADAM_B1 ** ADAM_STEP
    c2 = 1.0 - ADAM_B2 ** ADAM_STEP
    np_ = len(ws)

    def body(*refs):
        p_ref = refs[0]
        w_refs, m_refs, v_refs = refs[1:1 + np_], refs[1 + np_:1 + 2 * np_], refs[1 + 2 * np_:1 + 3 * np_]
        outs = refs[1 + 3 * np_:]
        g_refs, d_refs, nm_refs, nv_refs = (outs[i * np_:(i + 1) * np_] for i in range(4))
        loss_ref = outs[4 * np_]

        def total(start, width):
            t = p_ref[0, :, start:start + width]
            for s in range(1, N_DEV):
                t = t + p_ref[s, :, start:start + width]
            return t

        for i, (start, width) in enumerate(SMALL_SLOTS):
            g = total(start, width)
            nm = ADAM_B1 * m_refs[i][...] + (1.0 - ADAM_B1) * g
            nv = ADAM_B2 * v_refs[i][...] + (1.0 - ADAM_B2) * (g * g)
            g_refs[i][...] = g
            nm_refs[i][...] = nm
            nv_refs[i][...] = nv
            d_refs[i][...] = -ADAM_LR * ((nm / c1) / (jnp.sqrt(nv / c2) + ADAM_EPS) + ADAM_WD * w_refs[i][...])
        loss_ref[...] = total(SMALL_LOSS, LANE)

    vm = pl.BlockSpec(memory_space=pltpu.VMEM)
    shapes = [jax.ShapeDtypeStruct(w.shape, f32) for w in ws]
    res = _pcall(body, name="adamw_small", out_shape=shapes * 4 + [jax.ShapeDtypeStruct((1, LANE), f32)],
                 in_specs=[vm] * (1 + 3 * np_), out_specs=[vm] * (4 * np_ + 1))(parts, *ws, *ms, *vs)
    return [res[i * np_:(i + 1) * np_] for i in range(4)], res[4 * np_]


def _adamw_w_in(part_a, part_b, w3, m3, v3, after):
    _, n, _ = part_a.shape
    c1 = 1.0 - ADAM_B1 ** ADAM_STEP
    c2 = 1.0 - ADAM_B2 ** ADAM_STEP

    def body(pa_ref, pb_ref, w_ref, m_ref, v_ref, after_ref, g_ref, d_ref, nm_ref, nv_ref):
        g = pa_ref[0].astype(f32) + pb_ref[0].astype(f32)
        nm = ADAM_B1 * m_ref[:, 0, :] + (1.0 - ADAM_B1) * g
        nv = ADAM_B2 * v_ref[:, 0, :] + (1.0 - ADAM_B2) * (g * g)
        g_ref[:, 0, :] = g
        nm_ref[:, 0, :] = nm
        nv_ref[:, 0, :] = nv
        d_ref[:, 0, :] = -ADAM_LR * ((nm / c1) / (jnp.sqrt(nv / c2) + ADAM_EPS) + ADAM_WD * w_ref[:, 0, :])

    tile = 2 * COL_TILE
    blk = pl.BlockSpec((n, 1, tile), lambda j: (0, 0, j))
    out = jax.ShapeDtypeStruct((n, 1, D_MODEL), f32)
    return _pcall(
        body, name="adamw_w_in", grid=(D_MODEL // tile,),
        in_specs=[pl.BlockSpec((1, n, tile), lambda j: (0, 0, j))] * 2 + [blk, blk, blk, ANY],
        out_specs=[blk] * 4, out_shape=[out] * 4,
        compiler_params=_cparams("parallel"),
    )(part_a, part_b, w3, m3, v3, after)


def _pad_lanes(vec8, start):
    return jnp.pad(vec8.reshape(1, -1), ((0, 0), (start, LANE - start - vec8.size)))


def kernel(x, norm_in_w, w_in, conv_qkv_w, A_log, dt_bias, gdn_norm_w, conv_w, conv_b, w_out, final_norm_w, loss_target, m_norm_in_w, m_w_in, m_conv_qkv_w, m_A_log, m_dt_bias, m_gdn_norm_w, m_conv_w, m_conv_b, m_w_out, m_final_norm_w, v_norm_in_w, v_w_in, v_conv_qkv_w, v_A_log, v_dt_bias, v_gdn_norm_w, v_conv_w, v_conv_b, v_w_out, v_final_norm_w):
    L = x.shape[1]
    nc = L // CHUNK
    xs = x[0]
    tgt = loss_target[0]
    fnw = final_norm_w.reshape(1, D_MODEL)

    as_rows = lambda a: jnp.transpose(a, (2, 0, 1))
    as_taps = lambda a: jnp.transpose(a, (1, 0, 2))
    win_blk, wo_blk = _cast_weights(as_rows(w_in), w_out[0])
    win_g, cqkv_g, cw_g = _all_gather([win_blk, conv_qkv_w[0], as_taps(conv_w)], "gather_weights",
                                      pieces=[4, 1, 1])
    wpad = _relayout_w_in(win_g)
    cqkv = jnp.concatenate([cqkv_g[d] for d in range(N_DEV)], axis=1)
    cw = jnp.concatenate([cw_g[d][:, 0, :] for d in range(N_DEV)], axis=1)
    alog_p = _pad_lanes(A_log, HEADS)
    dtb_p = _pad_lanes(dt_bias, HEADS)
    tok = lambda started: started[4]
    wo_started = _spread_start(wo_blk, wpad, "gather", "gather_w_out_start")

    proj, h = _in_proj(xs, norm_in_w, wpad, tok(wo_started))
    qkv = _qkv_act(proj, cqkv)
    sc, gr = _scalars(proj, alog_p, dtb_p)
    o, u_all, w_all, vn_all, t_all, sp_all = _gdn_fwd(qkv, sc, gr)
    mix = _conv_fwd(proj, cw, conv_b, _gdn_gate(o, proj, gdn_norm_w))
    wo = _spread_wait(wo_started, mix, "gather", "gather_w_out_wait")[1].reshape(-1, D_MODEL)
    dy, dyb, dmix, g_fnw, loss_v = _out_proj_loss(xs, mix, wo, fnw, tgt)

    g_wout = _tn_matmul(mix, dyb, "grad_w_out")
    gwo_started = _spread_start(g_wout.reshape(N_DEV, -1, D_MODEL), dyb, "scatter", "exchange_grad_w_out_start")
    do, dproj, g_gnw = _gdn_gate_bwd(o, proj, gdn_norm_w, dmix, tok(gwo_started))
    dproj, g_cw, g_cb = _conv_bwd(proj, cw, conv_b, dmix, dproj)
    dqkv_n, dsc, dgr = _gdn_bwd(qkv, sc, gr, u_all, w_all, vn_all, t_all, sp_all, do)
    dproj, g_cqkv = _qkv_bwd(proj, cqkv, dqkv_n, dproj)
    g_cqkv_blk = g_cqkv.reshape(4, N_DEV, -1).transpose(1, 0, 2)
    g_cw_blk = jnp.pad(g_cw.reshape(3, N_DEV, -1).transpose(1, 0, 2),
                       ((0, 0), (0, 1), (0, g_cqkv_blk.shape[2] - g_cw.shape[1] // N_DEV)))
    gsm_started = _spread_start(jnp.concatenate([g_cqkv_blk, g_cw_blk], axis=1), g_cqkv, "scatter",
                                "exchange_small_sharded_grads_start")
    dgr_col = jnp.pad(dgr.transpose(0, 2, 1).reshape(L, HEADS), ((0, 0), (HEADS, LANE - 2 * HEADS)))
    dproj, g_sc = _scalars_bwd(proj, alog_p, dtb_p, dsc, dgr_col, dproj, tok(gsm_started))
    g_win_blk = _grad_blocks(_tn_matmul(dproj, h, "grad_w_in"))

    (p_win,) = _pair_exchange([g_win_blk], "exchange_grads_pair")
    r_small = _spread_wait(gsm_started, p_win, "scatter", "exchange_small_sharded_grads_wait")[1]
    s_win = _pair_sum(g_win_blk, p_win, "pair_sum_w_in")
    gw1_started = _spread_start(s_win, r_small, "axis_a", "exchange_grads_axis1_start")
    grad_x, g_nw = _input_grad(dproj, wpad, xs, norm_in_w, dy, tok(gw1_started))
    s_thru, got1 = _spread_wait(gw1_started, grad_x, "axis_a", "exchange_grads_axis1_wait")
    t_win = _axis_sum(s_thru, got1, "axis_sum_w_in")
    gw2_started = _spread_start(t_win, got1, "axis_b", "exchange_grads_axis2_start")

    r_wout = _spread_wait(gwo_started, tok(gw2_started), "scatter", "exchange_grad_w_out_wait")[1]
    upd_wout =_adamw_reduce(r_wout, w_out[0], m_w_out[0], v_w_out[0], "adamw_w_out")
    upd_cqkv = _adamw_reduce(r_small, conv_qkv_w[0], m_conv_qkv_w[0], v_conv_qkv_w[0], "adamw_conv_qkv_w", first_row=0)
    upd_cw = _adamw_reduce(r_small, as_taps(conv_w), as_taps(m_conv_w), as_taps(v_conv_w), "adamw_conv_w", first_row=4)

    t_thru, got2 = _spread_wait(gw2_started, upd_cw[0], "axis_b", "exchange_grads_axis2_wait")

    small_g = _pack_small([g_nw, g_cb, g_fnw, g_gnw, g_sc, loss_v], got2)
    gsg_started = _spread_start(small_g, got2, "gather", "gather_small_grads_start")
    upd_win_t = _adamw_w_in(t_thru, got2, as_rows(w_in), as_rows(m_w_in), as_rows(v_w_in), tok(gsg_started))
    upd_win = [jnp.transpose(a, (1, 2, 0)) for a in upd_win_t]
    small_all = _spread_wait(gsg_started, upd_win_t[0], "gather", "gather_small_grads_wait")[1]
    fvec = lambda a: a.reshape(1, D_MODEL)
    upd_small, loss_sum = _adamw_small(
        small_all,
        [norm_in_w, conv_b, fvec(final_norm_w), gdn_norm_w, A_log, dt_bias],
        [m_norm_in_w, m_conv_b, fvec(m_final_norm_w), m_gdn_norm_w, m_A_log, m_dt_bias],
        [v_norm_in_w, v_conv_b, fvec(v_final_norm_w), v_gdn_norm_w, v_A_log, v_dt_bias])

    outs = [loss_sum[0, 0], grad_x[None]]
    for k in range(4):
        nw_k, cb_k, fw_k, gn_k, al_k, dt_k = upd_small[k]
        outs += [nw_k, upd_win[k], upd_cqkv[k][None], al_k, dt_k, gn_k,
                 as_taps(upd_cw[k]), cb_k, upd_wout[k][None], fw_k.reshape(D_MODEL)]
    return tuple(outs)
```

```python
import jax
import jax.numpy as jnp
from jax import lax
from jax.experimental import pallas as pl
from jax.experimental.pallas import tpu as pltpu

f32 = jnp.float32
bf16 = jnp.bfloat16

N_DEV = 8
D_MODEL = 1024
HEADS = 8
HEAD_DIM = 128
CHUNK = 64
GDN_CPS = 4
GDN_CPS_BWD = 1
GDN_WIDTH = HEADS * HEAD_DIM
CONV_WIDTH = 1024
PROJ_WIDTH = 8208
SHARD_W = PROJ_WIDTH // N_DEV
EPS = 1e-6

LANE = 128
ELT_W = 256

OFF_QKV, OFF_ZG, OFF_CONV, OFF_BA = 0, 3072, 4096, 8192
CONV_BLOCK = 4 * ELT_W
PROJ_PAD = 8448
NAT_BA, NAT_CONV = 4096, 4112


def _padded_col(n):
    if n < NAT_BA:
        return n
    if n < NAT_CONV:
        return OFF_BA + n - NAT_BA
    g, ch = divmod(n - NAT_CONV, CONV_WIDTH)
    j, r = divmod(ch, ELT_W)
    return OFF_CONV + CONV_BLOCK * j + ELT_W * g + r


def _layout_segments(n0, n1):
    cuts = [NAT_BA, NAT_CONV] + [NAT_CONV + ELT_W * k for k in range(1, 4 * CONV_WIDTH // ELT_W)]
    pts = [n0] + [c for c in cuts if n0 < c < n1] + [n1]
    return [(lo, hi - lo, _padded_col(lo)) for lo, hi in zip(pts, pts[1:])]

ADAM_LR, ADAM_B1, ADAM_B2, ADAM_EPS, ADAM_WD, ADAM_STEP = 0.001, 0.9, 0.999, 1e-08, 0.01, 10

V7X_VMEM_BYTES = 64 * 1024 * 1024
VMEM_LIMIT = V7X_VMEM_BYTES - 8 * 1024 * 1024

MESH = pl.DeviceIdType.MESH
ANY = pl.BlockSpec(memory_space=pl.ANY)


def _pcall(body, **kw):
    return pl.pallas_call(body, **kw)


def _cparams(*sem):
    return pltpu.CompilerParams(dimension_semantics=sem if sem else None, vmem_limit_bytes=VMEM_LIMIT)


def _mm(a, b):
    return jnp.dot(a.astype(bf16), b.astype(bf16), preferred_element_type=f32)


def _mm_nt(a, b):
    return lax.dot_general(a.astype(bf16), b.astype(bf16), (((1,), (1,)), ((), ())), preferred_element_type=f32)


def _cat16(parts, axis):
    return jnp.concatenate([p.astype(bf16) for p in parts], axis=axis)


def _mm_tn(a, b):
    return lax.dot_general(a.astype(bf16), b.astype(bf16), (((0,), (0,)), ((), ())), preferred_element_type=f32)


def _rows(shape):
    return lax.broadcasted_iota(jnp.int32, shape, 0)


def _lanes(shape):
    return lax.broadcasted_iota(jnp.int32, shape, 1)


def _shift_down(x, s):
    if s == 0:
        return x
    return jnp.where(_rows(x.shape) >= s, pltpu.roll(x, s, 0), 0.0)


def _shift_up(x, s):
    if s == 0:
        return x
    n = x.shape[0]
    return jnp.where(_rows(x.shape) < n - s, pltpu.roll(x, n - s, 0), 0.0)


def _sigmoid(x):
    return jax.nn.sigmoid(x)


def _softplus(x):
    e = jnp.exp(-jnp.abs(x))
    small = e * (1.0 - e * (0.5 - e * (1.0 / 3.0)))
    return jnp.maximum(x, 0.0) + jnp.where(e < 0.01, small, jnp.log(1.0 + e))


def _mesh_pos():
    return lax.axis_index("x"), lax.axis_index("y"), lax.axis_index("c")


def _flat(px, py, pc):
    return 4 * px + 2 * py + pc


def _all_gather(xs, name, pieces=None):
    n = len(xs)
    pieces = pieces or [1] * n
    items = [(a, q) for a in range(n) for q in range(pieces[a])]
    ni = len(items)

    def view(ref, i):
        a, q = items[i]
        if pieces[a] == 1:
            return ref
        wd = xs[a].shape[-1] // pieces[a]
        return ref.at[(slice(None),) * (xs[a].ndim - 1) + (pl.ds(q * wd, wd),)]

    def body(*refs):
        x_refs, o_refs = refs[:n], refs[n:2 * n]
        send_sems, recv_sems, local_sems = refs[2 * n:]
        x, y, c = _mesh_pos()
        me, sibling = (x, y, c), (x, y, 1 - c)
        flip = lambda v, bit: v + bit - 2 * v * bit
        nbr_a = (flip(x, 1 - c), flip(y, c))
        nbr_b = (flip(x, c), flip(y, 1 - c))
        diag = (1 - x, 1 - y)

        def copy(i, k, block, to, own=False):
            a = items[i][0]
            dst = view(o_refs[a].at[_flat(*block)], i)
            return pltpu.make_async_remote_copy(
                src_ref=view(x_refs[a], i) if own else dst, dst_ref=dst,
                send_sem=send_sems.at[i, k], recv_sem=recv_sems.at[i, k], device_id=to, device_id_type=MESH)

        mine, sent = [], []

        def go(cp):
            cp.start()
            sent.append(cp)

        for a in range(n):
            cp = pltpu.make_async_copy(x_refs[a], o_refs[a].at[_flat(*me)], local_sems.at[a])
            cp.start()
            mine.append(cp)
        for a in range(ni):
            go(copy(a, 1, me, (*nbr_a, c), own=True))
            go(copy(a, 2, me, (*nbr_b, c), own=True))
            go(copy(a, 0, me, sibling, own=True))
        for a in range(ni):
            copy(a, 1, (*nbr_a, c), me).wait_recv()
            go(copy(a, 3, (*nbr_a, c), (*nbr_b, c)))
            go(copy(a, 4, (*nbr_a, c), sibling))
        for a in range(ni):
            copy(a, 2, (*nbr_b, c), me).wait_recv()
            go(copy(a, 5, (*nbr_b, c), sibling))
        for a in range(ni):
            copy(a, 3, (*diag, c), me).wait_recv()
            go(copy(a, 6, (*diag, c), sibling))
        for a in range(ni):
            copy(a, 0, sibling, me).wait_recv()
            copy(a, 4, (*nbr_b, 1 - c), me).wait_recv()
            copy(a, 5, (*nbr_a, 1 - c), me).wait_recv()
            copy(a, 6, (*diag, 1 - c), me).wait_recv()
        for cp in sent:
            cp.wait_send()
        for cp in mine:
            cp.wait()

    outs = _pcall(
        body, name=name,
        out_shape=[jax.ShapeDtypeStruct((N_DEV,) + a.shape, a.dtype) for a in xs],
        in_specs=[ANY] * n, out_specs=[ANY] * n,
        scratch_shapes=[pltpu.SemaphoreType.DMA((ni, 7)), pltpu.SemaphoreType.DMA((ni, 7)), pltpu.SemaphoreType.DMA((n,))],
    )(*xs)
    return list(outs)


def _pair_exchange(gs, name):
    n = len(gs)
    chips = [(0, 0), (0, 1), (1, 0), (1, 1)]

    def body(*refs):
        g_refs, o_refs = refs[:n], refs[n:2 * n]
        send_sems, recv_sems = refs[2 * n:]
        x, y, c = _mesh_pos()
        sibling = (x, y, 1 - c)

        def copy(a, i):
            xp, yp = chips[i]
            return pltpu.make_async_remote_copy(
                src_ref=g_refs[a].at[_flat(xp, yp, 1 - c)], dst_ref=o_refs[a].at[i],
                send_sem=send_sems.at[a, i], recv_sem=recv_sems.at[a, i], device_id=sibling, device_id_type=MESH)

        cps = [copy(a, i) for a in range(n) for i in range(4)]
        for cp in cps:
            cp.start()
        for cp in cps:
            cp.wait()

    outs = _pcall(
        body, name=name,
        out_shape=[jax.ShapeDtypeStruct((4,) + a.shape[1:], a.dtype) for a in gs],
        in_specs=[ANY] * n, out_specs=[ANY] * n,
        scratch_shapes=[pltpu.SemaphoreType.DMA((n, 4)), pltpu.SemaphoreType.DMA((n, 4))],
    )(*gs)
    return list(outs)


def _pair_sum(g, p1, name):
    _, R, C = g.shape
    tr = 256 if R % 256 == 0 else R
    cidx = lax.axis_index("c").astype(jnp.int32).reshape(1)

    def body(c_ref, g_ref, p_ref, o_ref):
        o_ref[...] = (g_ref[...].astype(f32) + p_ref[...].astype(f32)).astype(o_ref.dtype)

    return _pcall(
        body, name=name,
        grid_spec=pltpu.PrefetchScalarGridSpec(
            num_scalar_prefetch=1, grid=(4, R // tr),
            in_specs=[pl.BlockSpec((1, tr, C), lambda i, r, c_ref: (2 * i + c_ref[0], r, 0)),
                      pl.BlockSpec((1, tr, C), lambda i, r, c_ref: (i, r, 0))],
            out_specs=pl.BlockSpec((1, tr, C), lambda i, r, c_ref: (i, r, 0))),
        out_shape=jax.ShapeDtypeStruct((4, R, C), g.dtype),
        compiler_params=_cparams("parallel", "parallel"),
    )(cidx, g, p1)


def _axis_sum(s, got, name):
    _, R, C = s.shape
    x, y, c = _mesh_pos()
    me, _, b, _ = _axis_chips(x, y, c)
    idx = jnp.stack([2 * me[0] + me[1], 2 * b[0] + b[1]]).astype(jnp.int32)

    def body(idx_ref, s_ref, g_ref, o_ref):
        o_ref[...] = (s_ref[...].astype(f32) + g_ref[...].astype(f32)).astype(o_ref.dtype)

    return _pcall(
        body, name=name,
        grid_spec=pltpu.PrefetchScalarGridSpec(
            num_scalar_prefetch=1, grid=(2,),
            in_specs=[pl.BlockSpec((1, R, C), lambda k, idx_ref: (idx_ref[k], 0, 0)),
                      pl.BlockSpec((1, R, C), lambda k, idx_ref: (k, 0, 0))],
            out_specs=pl.BlockSpec((1, R, C), lambda k, idx_ref: (k, 0, 0))),
        out_shape=jax.ShapeDtypeStruct((2, R, C), s.dtype),
        compiler_params=_cparams("parallel"),
    )(idx, s, got)


HBM = pl.BlockSpec(memory_space=pltpu.HBM)
SEM = pl.BlockSpec(memory_space=pltpu.SEMAPHORE)
EFFECT = pltpu.SideEffectType.DATAFLOW_SIDE_EFFECTING


def _peers(x, y, c):
    out = []
    for k in range(1, N_DEV):
        kx, ky, kc = (k >> 2) & 1, (k >> 1) & 1, k & 1
        out.append(((1 - x) if kx else x, (1 - y) if ky else y, (1 - c) if kc else c))
    return out


SPREAD_COPIES = {"gather": N_DEV - 1, "scatter": N_DEV - 1, "axis_a": 2, "axis_b": 1}
SPREAD_SLOTS = {"axis_a": 2, "axis_b": 1}


def _axis_chips(x, y, c):
    flip = lambda v, bit: v + bit - 2 * v * bit
    return (x, y), (flip(x, 1 - c), flip(y, c)), (flip(x, c), flip(y, 1 - c)), (1 - x, 1 - y)


def _spread_copy(src_ref, land_ref, send_sems, recv_sems, k, plan):
    x, y, c = _mesh_pos()
    if plan in ("axis_a", "axis_b"):
        _, a, b, d = _axis_chips(x, y, c)
        chip = lambda p: 2 * p[0] + p[1]
        peer = (*(a if plan == "axis_a" else b), c)
        src = src_ref.at[chip(a) if k == 0 else chip(d)] if plan == "axis_a" else src_ref.at[1]
        slot = k
    else:
        peer = _peers(x, y, c)[k]
        src, slot = (src_ref.at[_flat(*peer)] if plan == "scatter" else src_ref), _flat(x, y, c)
    return pltpu.make_async_remote_copy(
        src_ref=src, dst_ref=land_ref.at[slot], send_sem=send_sems.at[k], recv_sem=recv_sems.at[k],
        device_id=peer, device_id_type=MESH)


def _own_copy(src_ref, land_ref, send_sems, plan):
    me = _flat(*_mesh_pos())
    return pltpu.make_async_copy(src_ref.at[me] if plan == "scatter" else src_ref, land_ref.at[me],
                                 send_sems.at[SPREAD_COPIES[plan]])


def _spread_start(src, after, plan, name):
    land_shape = (N_DEV,) + src.shape if plan == "gather" else src.shape
    if plan in SPREAD_SLOTS:
        land_shape = (SPREAD_SLOTS[plan],) + src.shape[1:]
    n_copies = SPREAD_COPIES[plan]

    def body(src_ref, land_ref, after_ref, send_sems, recv_sems, src_thru, land_thru, token):
        for k in range(n_copies):
            _spread_copy(src_ref, land_ref, send_sems, recv_sems, k, plan).start()
        if plan not in SPREAD_SLOTS:
            _own_copy(src_ref, land_ref, send_sems, plan).start()
        token[...] = jnp.zeros_like(token)

    return _pcall(
        body, name=name,
        out_shape=(pltpu.SemaphoreType.DMA((n_copies + (plan not in SPREAD_SLOTS),)), pltpu.SemaphoreType.DMA((n_copies,)),
                   pltpu.HBM(src.shape, src.dtype), pltpu.HBM(land_shape, src.dtype), jax.ShapeDtypeStruct((8, LANE), f32)),
        in_specs=(HBM, HBM, ANY), out_specs=(SEM, SEM, HBM, HBM, pl.BlockSpec(memory_space=pltpu.VMEM)),
        input_output_aliases={0: 2, 1: 3},
        compiler_params=pltpu.CompilerParams(has_side_effects=EFFECT),
    )(pltpu.with_memory_space_constraint(src, pltpu.HBM),
      pltpu.with_memory_space_constraint(lax.empty(land_shape, src.dtype), pltpu.HBM), after)


def _spread_wait(started, after, plan, name):
    send_sems, recv_sems, src_thru, land_thru, _ = started

    def body(src_ref, land_ref, send_sems, recv_sems, after_ref, src_dead, got_ref):
        for k in range(SPREAD_COPIES[plan]):
            cp = _spread_copy(src_ref, land_ref, send_sems, recv_sems, k, plan)
            cp.wait_send()
            cp.wait_recv()
        if plan not in SPREAD_SLOTS:
            _own_copy(src_ref, land_ref, send_sems, plan).wait()

    return _pcall(
        body, name=name,
        out_shape=(pltpu.HBM(src_thru.shape, src_thru.dtype), pltpu.HBM(land_thru.shape, land_thru.dtype)),
        in_specs=(HBM, HBM, SEM, SEM, ANY), out_specs=(HBM, HBM), input_output_aliases={0: 0, 1: 1},
        compiler_params=pltpu.CompilerParams(has_side_effects=EFFECT),
    )(src_thru, land_thru, send_sems, recv_sems, after)


COL_TILE = 256


def _cast_weights(w3, wo):
    n = w3.shape[0]

    def body(w_ref, wo_ref, o_ref, oo_ref):
        o_ref[...] = w_ref[:, 0, :].astype(bf16)
        oo_ref[...] = wo_ref[...].astype(bf16)

    tile = 2 * COL_TILE
    return _pcall(
        body, name="cast_weights", grid=(D_MODEL // tile,),
        in_specs=[pl.BlockSpec((n, 1, tile), lambda j: (0, 0, j)), pl.BlockSpec((wo.shape[0], tile), lambda j: (0, j))],
        out_specs=[pl.BlockSpec((n, tile), lambda j: (0, j)), pl.BlockSpec((wo.shape[0], tile), lambda j: (0, j))],
        out_shape=[jax.ShapeDtypeStruct((n, D_MODEL), bf16), jax.ShapeDtypeStruct(wo.shape, bf16)],
        compiler_params=_cparams("parallel"),
    )(w3, wo)


def _relayout_w_in(win_g):
    def body(g_ref, o_ref):
        used = OFF_BA + NAT_CONV - NAT_BA
        o_ref[used:PROJ_PAD, :] = jnp.zeros((PROJ_PAD - used, COL_TILE), o_ref.dtype)
        for d in range(N_DEV):
            for lo, width, dst in _layout_segments(d * SHARD_W, (d + 1) * SHARD_W):
                src = lo - d * SHARD_W
                o_ref[dst:dst + width, :] = g_ref[d, src:src + width, :]

    return _pcall(
        body, name="relayout_w_in", grid=(D_MODEL // COL_TILE,),
        in_specs=[pl.BlockSpec((N_DEV, SHARD_W, COL_TILE), lambda j: (0, 0, j))],
        out_specs=pl.BlockSpec((PROJ_PAD, COL_TILE), lambda j: (0, j)),
        out_shape=jax.ShapeDtypeStruct((PROJ_PAD, D_MODEL), win_g.dtype),
        compiler_params=_cparams("parallel"),
    )(win_g)


def _grad_blocks(g_t):
    def body(p_ref, o_ref):
        for d in range(N_DEV):
            for lo, width, src in _layout_segments(d * SHARD_W, (d + 1) * SHARD_W):
                dst = lo - d * SHARD_W
                o_ref[d, dst:dst + width, :] = p_ref[src:src + width, :]

    return _pcall(
        body, name="grad_blocks", grid=(D_MODEL // COL_TILE,),
        in_specs=[pl.BlockSpec((PROJ_PAD, COL_TILE), lambda j: (0, j))],
        out_specs=pl.BlockSpec((N_DEV, SHARD_W, COL_TILE), lambda j: (0, 0, j)),
        out_shape=jax.ShapeDtypeStruct((N_DEV, SHARD_W, D_MODEL), bf16),
        compiler_params=_cparams("parallel"),
    )(g_t)


def _in_proj(x, nw, wpad_t, after):
    L = x.shape[0]
    tn = 768
    nj = wpad_t.shape[0] // tn

    def body(x_ref, nw_ref, w_ref, after_ref, proj_ref, h_ref):
        first = pl.program_id(0) == 0

        def project(r, n, hv):
            proj_ref[r:r + n, :] = lax.dot_general(hv, w_ref[...], (((1,), (1,)), ((), ())), preferred_element_type=f32)

        @pl.when(first)
        def _():
            for r in range(0, L, 256):
                xs = x_ref[r:r + 256, :]
                ms = jnp.mean(xs * xs, axis=-1, keepdims=True)
                hv = ((xs * lax.rsqrt(ms + EPS)) * nw_ref[...]).astype(bf16)
                h_ref[r:r + 256, :] = hv
                project(r, 256, hv)

        @pl.when(jnp.logical_not(first))
        def _():
            for r in range(0, L, 512):
                project(r, 512, h_ref[r:r + 512, :])

    return _pcall(
        body, name="in_proj", grid=(nj,),
        in_specs=[pl.BlockSpec((L, D_MODEL), lambda j: (0, 0)), pl.BlockSpec((1, D_MODEL), lambda j: (0, 0)),
                  pl.BlockSpec((tn, D_MODEL), lambda j: (j, 0)), ANY],
        out_specs=[pl.BlockSpec((L, tn), lambda j: (0, j)), pl.BlockSpec((L, D_MODEL), lambda j: (0, 0))],
        out_shape=[jax.ShapeDtypeStruct((L, wpad_t.shape[0]), f32), jax.ShapeDtypeStruct((L, D_MODEL), bf16)],
        compiler_params=_cparams("arbitrary"),
    )(x, nw, wpad_t, after)


HALVES = [slice(i * LANE, (i + 1) * LANE) for i in range(ELT_W // LANE)]
QKV_W = 512
QKV_HEADS = [slice(i * LANE, (i + 1) * LANE) for i in range(QKV_W // LANE)]
STEPS_PER_GROUP = GDN_WIDTH // QKV_W


def _conv4(x, cw_ref, ls):
    return (cw_ref[3:4, ls] * x + cw_ref[2:3, ls] * _shift_down(x, 1) + cw_ref[1:2, ls] * _shift_down(x, 2)
            + cw_ref[0:1, ls] * _shift_down(x, 3))


def _qkv_act(proj, cw):
    L = proj.shape[0]

    def body(x_ref, cw_ref, o_ref):
        j = pl.program_id(0)
        scale = jnp.where(j < STEPS_PER_GROUP, HEAD_DIM ** -0.5, 1.0).astype(f32)
        for ls in QKV_HEADS:
            c = _conv4(x_ref[:, ls], cw_ref, ls)
            a = c * _sigmoid(c)
            rn = lax.rsqrt(jnp.sum(a * a, axis=1, keepdims=True) + EPS)
            o_ref[:, ls] = jnp.where(j < 2 * STEPS_PER_GROUP, (a * rn) * scale, a)

    return _pcall(
        body, name="qkv_act", grid=(3 * STEPS_PER_GROUP,),
        in_specs=[pl.BlockSpec((L, QKV_W), lambda j: (0, j)), pl.BlockSpec((4, QKV_W), lambda j: (0, j))],
        out_specs=pl.BlockSpec((L, QKV_W), lambda j: (0, j)),
        out_shape=jax.ShapeDtypeStruct((L, 3 * GDN_WIDTH), f32),
        compiler_params=_cparams("parallel"),
    )(proj, cw)


def _scalars(proj, alog_p, dtb_p):
    L = proj.shape[0]
    nc = L // CHUNK

    def body(x_ref, al_ref, dt_ref, sc_ref, gr_ref):
        x = x_ref[...]
        lane = _lanes(x.shape)
        beta = _sigmoid(x)
        g = -jnp.exp(al_ref[...]) * _softplus(x + dt_ref[...])
        gc = jnp.where((lane >= HEADS) & (lane < 2 * HEADS), g, 0.0)
        rc = _rows(x.shape) & (CHUNK - 1)
        for s in (1, 2, 4, 8, 16, 32):
            gc = gc + jnp.where(rc >= s, pltpu.roll(gc, s, 0), 0.0)
        sc_ref[...] = jnp.where(lane < HEADS, beta, gc)
        sel = (_lanes((HEADS, LANE)) == _rows((HEADS, LANE)) + HEADS).astype(f32)
        for c in range(nc):
            gr_ref[c] = lax.dot_general(sel, sc_ref[c * CHUNK:(c + 1) * CHUNK, :], (((1,), (1,)), ((), ())),
                                        preferred_element_type=f32, precision=lax.Precision.HIGHEST)

    return _pcall(
        body, name="scalars", grid=(1,),
        in_specs=[pl.BlockSpec((L, LANE), lambda i: (0, OFF_BA // LANE)), pl.BlockSpec((1, LANE), lambda i: (0, 0)),
                  pl.BlockSpec((1, LANE), lambda i: (0, 0))],
        out_specs=[pl.BlockSpec((L, LANE), lambda i: (0, 0)), pl.BlockSpec((nc, HEADS, CHUNK), lambda i: (0, 0, 0))],
        out_shape=[jax.ShapeDtypeStruct((L, LANE), f32), jax.ShapeDtypeStruct((nc, HEADS, CHUNK), f32)],
        compiler_params=_cparams("arbitrary"),
    )(proj, alog_p, dtb_p)


def _head_scalars(sc, gr_ref, h, ci=0):
    lane = _lanes(sc.shape)
    beta = jnp.sum(jnp.where(lane == h, sc, 0.0), axis=1, keepdims=True)
    gcc = jnp.sum(jnp.where(lane == HEADS + h, sc, 0.0), axis=1, keepdims=True)
    gcr = gr_ref[ci, h:h + 1, :]
    gl = jnp.sum(jnp.where(_lanes(gcr.shape) == CHUNK - 1, gcr, 0.0), axis=1, keepdims=True)
    ii, jj = _rows((CHUNK, CHUNK)), _lanes((CHUNK, CHUNK))
    dmat = jnp.where(ii >= jj, jnp.exp(jnp.minimum(gcc - gcr, 0.0)), 0.0)
    dmat_t = jnp.where(jj >= ii, jnp.exp(jnp.minimum(gcr - gcc, 0.0)), 0.0)
    return beta, gcc, gl, dmat, dmat_t, ii, jj


def _gdn_fwd(qkv, sc, gr):
    L = qkv.shape[0]
    nc = L // CHUNK
    W = GDN_WIDTH
    cps = GDN_CPS if nc % GDN_CPS == 0 else 1
    rows_per_step = cps * CHUNK

    def body(qkv_ref, sc_ref, gr_ref, o_ref, u_ref, w_ref, vn_ref, t_ref, sp_ref, s_scr):
        @pl.when(pl.program_id(0) == 0)
        def _():
            s_scr[...] = jnp.zeros_like(s_scr)
        HS = range(cps * HEADS)
        hd = [i % HEADS for i in HS]
        rs = [slice((i // HEADS) * CHUNK, (i // HEADS + 1) * CHUNK) for i in HS]
        cs = [slice(hd[i] * HEAD_DIM, (hd[i] + 1) * HEAD_DIM) for i in HS]
        q = [qkv_ref[rs[i], hd[i] * HEAD_DIM:(hd[i] + 1) * HEAD_DIM] for i in HS]
        k = [qkv_ref[rs[i], W + hd[i] * HEAD_DIM:W + (hd[i] + 1) * HEAD_DIM] for i in HS]
        v = [qkv_ref[rs[i], 2 * W + hd[i] * HEAD_DIM:2 * W + (hd[i] + 1) * HEAD_DIM] for i in HS]
        hsc = [_head_scalars(sc_ref[rs[i], :], gr_ref, hd[i], i // HEADS) for i in HS]
        beta, gcc, gl, dmat = ([x[i] for x in hsc] for i in range(4))
        ii, jj = hsc[0][5], hsc[0][6]
        eg = [jnp.exp(gcc[h]) for h in HS]
        kb = [k[h] * beta[h] for h in HS]
        kk = [_mm_nt(kb[h], k[h]) for h in HS]
        qk = [_mm_nt(q[h], k[h]) for h in HS]
        n0 = [-jnp.where(ii > jj, kk[h] * dmat[h], 0.0) for h in HS]
        n1 = [_mm(n0[h], n0[h]) for h in HS]
        n2 = [_mm(n1[h], n1[h]) for h in HS]
        p01 = [n0[h] + n1[h] + _mm(n0[h], n1[h]) for h in HS]
        n3 = [_mm(n2[h], n2[h]) for h in HS]
        n4 = [_mm(n3[h], n3[h]) for h in HS]
        p23 = [n2[h] + n3[h] + _mm(n2[h], n3[h]) for h in HS]
        n5 = [_mm(n4[h], n4[h]) for h in HS]
        p03 = [p01[h] + p23[h] + _mm(p01[h], p23[h]) for h in HS]
        p45 = [n4[h] + n5[h] + _mm(n4[h], n5[h]) for h in HS]
        t = [p03[h] + p45[h] + _mm(p03[h], p45[h]) for h in HS]
        vb = [v[h] * beta[h] for h in HS]
        kbg = [kb[h] * eg[h] for h in HS]
        uw = [_mm(t[h], _cat16([vb[h], kbg[h]], 1)) for h in HS]
        u = [vb[h] + uw[h][:, :HEAD_DIM] for h in HS]
        w = [kbg[h] + uw[h][:, HEAD_DIM:] for h in HS]
        wq = [_cat16([w[h], q[h] * eg[h]], 0) for h in HS]
        p = [jnp.where(ii >= jj, qk[h] * dmat[h], 0.0) for h in HS]
        ks = [k[h] * jnp.exp(gl[h] - gcc[h]) for h in HS]
        s = [s_scr[h] for h in range(HEADS)]
        for ci in range(cps):
            IS = range(ci * HEADS, (ci + 1) * HEADS)
            ws = [_mm(wq[i], s[hd[i]]) for i in IS]
            vn = [u[i] - ws[hd[i]][:CHUNK] for i in IS]
            pv = [_mm(p[i], vn[hd[i]]) for i in IS]
            kv = [_mm_tn(ks[i], vn[hd[i]]) for i in IS]
            for i in IS:
                h = hd[i]
                sp_ref[ci, cs[i], :] = s[h]
                o_ref[rs[i], cs[i]] = ws[h][CHUNK:] + pv[h]
                vn_ref[rs[i], cs[i]] = vn[h].astype(bf16)
            s = [jnp.exp(gl[i]) * s[hd[i]] + kv[hd[i]] for i in IS]
        for h in range(HEADS):
            s_scr[h] = s[h]
        for i in HS:
            u_ref[rs[i], cs[i]] = u[i].astype(bf16)
            w_ref[rs[i], cs[i]] = w[i].astype(bf16)
            t_ref[i // HEADS, hd[i]] = t[i].astype(bf16)

    row = lambda c: (c, 0)
    act, act16 = jax.ShapeDtypeStruct((L, W), f32), jax.ShapeDtypeStruct((L, W), bf16)
    return _pcall(
        body, name="gdn_fwd", grid=(nc // cps,),
        in_specs=[pl.BlockSpec((rows_per_step, 3 * W), row), pl.BlockSpec((rows_per_step, LANE), row),
                  pl.BlockSpec((cps, HEADS, CHUNK), lambda c: (c, 0, 0))],
        out_specs=[pl.BlockSpec((rows_per_step, W), row)] * 4 + [
            pl.BlockSpec((cps, HEADS, CHUNK, CHUNK), lambda c: (c, 0, 0, 0)),
            pl.BlockSpec((cps, W, HEAD_DIM), lambda c: (c, 0, 0))],
        out_shape=[act, act16, act16, act16, jax.ShapeDtypeStruct((nc, HEADS, CHUNK, CHUNK), bf16),
                   jax.ShapeDtypeStruct((nc, W, HEAD_DIM), f32)],
        scratch_shapes=[pltpu.VMEM((HEADS, HEAD_DIM, HEAD_DIM), f32)],
        compiler_params=_cparams("arbitrary"),
    )(qkv, sc, gr)


def _gdn_gate(o, proj, gnw):
    L = o.shape[0]

    def body(o_ref, z_ref, w_ref, m_ref):
        for ls in HALVES:
            ov, z = o_ref[:, ls], z_ref[:, ls]
            rms = lax.rsqrt(jnp.mean(ov * ov, axis=-1, keepdims=True) + EPS)
            m_ref[:, ls] = (((ov * rms) * w_ref[...]) * (z * _sigmoid(z))).astype(bf16)

    return _pcall(
        body, name="gdn_gate", grid=(GDN_WIDTH // ELT_W,),
        in_specs=[pl.BlockSpec((L, ELT_W), lambda j: (0, j)), pl.BlockSpec((L, ELT_W), lambda j: (0, OFF_ZG // ELT_W + j)),
                  pl.BlockSpec((1, LANE), lambda j: (0, 0))],
        out_specs=pl.BlockSpec((L, ELT_W), lambda j: (0, j)),
        out_shape=jax.ShapeDtypeStruct((L, GDN_WIDTH + CONV_WIDTH), bf16),
        compiler_params=_cparams("parallel"),
    )(o, proj, gnw)


def _conv3(u, cw_ref, ls):
    return cw_ref[2:3, ls] * u + cw_ref[1:2, ls] * _shift_down(u, 1) + cw_ref[0:1, ls] * _shift_down(u, 2)


def _conv_specs(L):
    return [pl.BlockSpec((L, CONV_BLOCK), lambda j: (0, OFF_CONV // CONV_BLOCK + j)),
            pl.BlockSpec((3, ELT_W), lambda j: (0, j)), pl.BlockSpec((1, ELT_W), lambda j: (0, j))]


def _conv_parts(ls):
    return [slice(g * ELT_W + ls.start, g * ELT_W + ls.stop) for g in range(4)]


def _conv_fwd(proj, cw, cb, mix):
    L = proj.shape[0]

    def body(p_ref, cw_ref, cb_ref, mix_in, m_ref):
        for ls in HALVES:
            sb, sc_, sh, sz = _conv_parts(ls)
            z = p_ref[:, sz]
            cv = _conv3(p_ref[:, sc_] * p_ref[:, sh], cw_ref, ls) + cb_ref[:, ls]
            m_ref[:, ls] = ((p_ref[:, sb] * cv) * (z * _sigmoid(z))).astype(bf16)

    return _pcall(
        body, name="conv_fwd", grid=(CONV_WIDTH // ELT_W,),
        in_specs=_conv_specs(L) + [ANY], out_specs=pl.BlockSpec((L, ELT_W), lambda j: (0, GDN_WIDTH // ELT_W + j)),
        out_shape=jax.ShapeDtypeStruct(mix.shape, mix.dtype), input_output_aliases={3: 0},
        compiler_params=_cparams("parallel"),
    )(proj, cw, cb, mix)


def _out_proj_loss(x, mix, wo, fw, tgt):
    L = x.shape[0]
    tm = min(512, L)
    MW = GDN_WIDTH + CONV_WIDTH

    def body(x_ref, m_ref, wo_ref, fw_ref, t_ref, dy_ref, dyb_ref, dm_ref, gfw_ref, loss_ref):
        @pl.when(pl.program_id(0) == 0)
        def _():
            gfw_ref[...] = jnp.zeros_like(gfw_ref)
            loss_ref[...] = jnp.zeros_like(loss_ref)
        y = x_ref[...] + jnp.dot(m_ref[...], wo_ref[...], preferred_element_type=f32)
        r = lax.rsqrt(jnp.mean(y * y, axis=-1, keepdims=True) + EPS)
        yh = y * r
        fwv = fw_ref[...]
        diff = yh * fwv - t_ref[...]
        loss_ref[...] += jnp.sum(jnp.sum(diff * diff, axis=-1, keepdims=True), axis=0, keepdims=True) * (0.5 / D_MODEL)
        dout = diff * (1.0 / D_MODEL)
        gfw_ref[...] += jnp.sum(dout * yh, axis=0, keepdims=True)
        dyh = dout * fwv
        dy = r * (dyh - yh * jnp.mean(dyh * yh, axis=-1, keepdims=True))
        dy_ref[...] = dy
        dyb = dy.astype(bf16)
        dyb_ref[...] = dyb
        dm_ref[...] = lax.dot_general(dyb, wo_ref[...], (((1,), (1,)), ((), ())), preferred_element_type=f32)

    row = lambda i: (i, 0)
    fix = lambda i: (0, 0)
    act = jax.ShapeDtypeStruct((L, D_MODEL), f32)
    return _pcall(
        body, name="out_proj_loss", grid=(L // tm,),
        in_specs=[pl.BlockSpec((tm, D_MODEL), row), pl.BlockSpec((tm, MW), row), pl.BlockSpec((MW, D_MODEL), fix),
                  pl.BlockSpec((1, D_MODEL), fix), pl.BlockSpec((tm, D_MODEL), row)],
        out_specs=[pl.BlockSpec((tm, D_MODEL), row), pl.BlockSpec((tm, D_MODEL), row), pl.BlockSpec((tm, MW), row),
                   pl.BlockSpec((1, D_MODEL), fix), pl.BlockSpec((1, LANE), fix)],
        out_shape=[act, jax.ShapeDtypeStruct((L, D_MODEL), bf16), jax.ShapeDtypeStruct((L, MW), f32),
                   jax.ShapeDtypeStruct((1, D_MODEL), f32), jax.ShapeDtypeStruct((1, LANE), f32)],
        compiler_params=_cparams("arbitrary"),
    )(x, mix, wo, fw, tgt)


def _tn_matmul(a, b, name):
    L, M = a.shape
    N = b.shape[1]
    tm = 512 if M % 512 == 0 else (768 if M % 768 == 0 else M)

    def body(a_ref, b_ref, o_ref):
        o_ref[...] = lax.dot_general(a_ref[...], b_ref[...], (((0,), (0,)), ((), ())),
                                     preferred_element_type=f32).astype(o_ref.dtype)

    return _pcall(
        body, name=name, grid=(M // tm,),
        in_specs=[pl.BlockSpec((L, tm), lambda i: (0, i)), pl.BlockSpec((L, N), lambda i: (0, 0))],
        out_specs=pl.BlockSpec((tm, N), lambda i: (i, 0)),
        out_shape=jax.ShapeDtypeStruct((M, N), bf16),
        compiler_params=_cparams("parallel"),
    )(a, b)


def _gdn_gate_bwd(o, proj, gnw, dmix_a, after):
    L = o.shape[0]

    def body(o_ref, z_ref, w_ref, dm_ref, after_ref, do_ref, dz_ref, gw_ref):
        @pl.when(pl.program_id(0) == 0)
        def _():
            gw_ref[...] = jnp.zeros_like(gw_ref)
        wv = w_ref[...]
        for ls in HALVES:
            ov, z, dm = o_ref[:, ls], z_ref[:, ls], dm_ref[:, ls]
            rms = lax.rsqrt(jnp.mean(ov * ov, axis=-1, keepdims=True) + EPS)
            xh = ov * rms
            sg = _sigmoid(z)
            d_on = dm * (z * sg)
            dz_ref[:, ls] = (dm * (xh * wv) * (sg * (1.0 + z * (1.0 - sg)))).astype(bf16)
            gw_ref[...] += jnp.sum(d_on * xh, axis=0, keepdims=True)
            dxh = d_on * wv
            do_ref[:, ls] = (rms * (dxh - xh * jnp.mean(dxh * xh, axis=-1, keepdims=True))).astype(bf16)

    wide = pl.BlockSpec((L, ELT_W), lambda j: (0, j))
    return _pcall(
        body, name="gdn_gate_bwd", grid=(GDN_WIDTH // ELT_W,),
        in_specs=[wide, pl.BlockSpec((L, ELT_W), lambda j: (0, OFF_ZG // ELT_W + j)),
                  pl.BlockSpec((1, LANE), lambda j: (0, 0)), wide, ANY],
        out_specs=[wide, pl.BlockSpec((L, ELT_W), lambda j: (0, OFF_ZG // ELT_W + j)),
                   pl.BlockSpec((1, LANE), lambda j: (0, 0))],
        out_shape=[jax.ShapeDtypeStruct((L, GDN_WIDTH), bf16), jax.ShapeDtypeStruct((L, PROJ_PAD), bf16),
                   jax.ShapeDtypeStruct((1, LANE), f32)],
        compiler_params=_cparams("arbitrary"),
    )(o, proj, gnw, dmix_a, after)


def _conv_bwd(proj, cw, cb, dmix_b, dproj):
    L = proj.shape[0]

    def body(p_ref, cw_ref, cb_ref, dm_ref, dproj_in, dp_ref, gcw_ref, gcb_ref):
        for ls in HALVES:
            sb, sc_, sh, sz_ = _conv_parts(ls)
            bv, cv_, hv, z, dm = p_ref[:, sb], p_ref[:, sc_], p_ref[:, sh], p_ref[:, sz_], dm_ref[:, ls]
            u = cv_ * hv
            cv = _conv3(u, cw_ref, ls) + cb_ref[:, ls]
            sg = _sigmoid(z)
            sz = z * sg
            dp_ref[:, sb] = (dm * cv * sz).astype(bf16)
            dp_ref[:, sz_] = (dm * (bv * cv) * (sg * (1.0 + z * (1.0 - sg)))).astype(bf16)
            dcv = dm * bv * sz
            gcb_ref[:, ls] = jnp.sum(dcv, axis=0, keepdims=True)
            dcv1, dcv2 = _shift_up(dcv, 1), _shift_up(dcv, 2)
            gcw_ref[2:3, ls] = jnp.sum(dcv * u, axis=0, keepdims=True)
            gcw_ref[1:2, ls] = jnp.sum(dcv1 * u, axis=0, keepdims=True)
            gcw_ref[0:1, ls] = jnp.sum(dcv2 * u, axis=0, keepdims=True)
            du = cw_ref[2:3, ls] * dcv + cw_ref[1:2, ls] * dcv1 + cw_ref[0:1, ls] * dcv2
            dp_ref[:, sc_] = (du * hv).astype(bf16)
            dp_ref[:, sh] = (du * cv_).astype(bf16)

    return _pcall(
        body, name="conv_bwd", grid=(CONV_WIDTH // ELT_W,),
        in_specs=_conv_specs(L) + [pl.BlockSpec((L, ELT_W), lambda j: (0, GDN_WIDTH // ELT_W + j)), ANY],
        out_specs=[pl.BlockSpec((L, CONV_BLOCK), lambda j: (0, OFF_CONV // CONV_BLOCK + j)),
                   pl.BlockSpec((3, ELT_W), lambda j: (0, j)), pl.BlockSpec((1, ELT_W), lambda j: (0, j))],
        out_shape=[jax.ShapeDtypeStruct(dproj.shape, dproj.dtype), jax.ShapeDtypeStruct((3, CONV_WIDTH), f32),
                   jax.ShapeDtypeStruct((1, CONV_WIDTH), f32)],
        input_output_aliases={4: 0},
        compiler_params=_cparams("parallel"),
    )(proj, cw, cb, dmix_b, dproj)


def _gdn_bwd(qkv, sc, gr, u_all, w_all, vn_all, t_all, sp_all, do_all):
    L = qkv.shape[0]
    nc = L // CHUNK
    W = GDN_WIDTH
    cps = GDN_CPS_BWD if nc % GDN_CPS_BWD == 0 else 1
    rows_per_step = cps * CHUNK
    nsteps = nc // cps

    def body(qkv_ref, sc_ref, gr_ref, u_ref, w_ref, vn_ref, t_ref, sp_ref, do_ref, dqkv_ref, dsc_ref, dgr_ref, ds_scr):
        @pl.when(pl.program_id(0) == 0)
        def _():
            ds_scr[...] = jnp.zeros_like(ds_scr)
        nh, base = HEADS, 0
        HS = range(cps * nh)
        hl = [i % nh for i in HS]
        hd = [base + hl[i] for i in HS]
        rs = [slice((i // nh) * CHUNK, (i // nh + 1) * CHUNK) for i in HS]
        cs = [slice(hd[i] * HEAD_DIM, (hd[i] + 1) * HEAD_DIM) for i in HS]
        q = [qkv_ref[rs[i], hd[i] * HEAD_DIM:(hd[i] + 1) * HEAD_DIM] for i in HS]
        k = [qkv_ref[rs[i], W + hd[i] * HEAD_DIM:W + (hd[i] + 1) * HEAD_DIM] for i in HS]
        v = [qkv_ref[rs[i], 2 * W + hd[i] * HEAD_DIM:2 * W + (hd[i] + 1) * HEAD_DIM] for i in HS]
        hsc = [_head_scalars(sc_ref[rs[i], :], gr_ref, hd[i], i // nh) for i in HS]
        beta, gcc, gl, dmat, dmat_t = ([x[i] for x in hsc] for i in range(5))
        ii, jj = hsc[0][5], hsc[0][6]
        eg = [jnp.exp(gcc[h]) for h in HS]
        ekl = [jnp.exp(gl[h] - gcc[h]) for h in HS]
        egl = [jnp.exp(gl[h]) for h in HS]
        kb = [k[h] * beta[h] for h in HS]
        ks = [k[h] * ekl[h] for h in HS]
        do = [do_ref[rs[h], cs[h]] for h in HS]
        vn = [vn_ref[rs[h], cs[h]] for h in HS]
        s = [sp_ref[h // nh, cs[h], :] for h in HS]
        w = [w_ref[rs[h], cs[h]] for h in HS]
        qd = [q[h] * eg[h] for h in HS]

        kq = [_mm_nt(k[h], q[h]) for h in HS]
        p_t = [jnp.where(jj >= ii, kq[h] * dmat_t[h], 0.0) for h in HS]
        ptd = [_mm(p_t[h], do[h]) for h in HS]
        qw = [_cat16([qd[h], -w[h]], 0) for h in HS]
        dsn, dvn, dodv = [None] * len(HS), [None] * len(HS), [None] * len(HS)
        ds_cur = [ds_scr[base + h] for h in range(nh)]
        for ci in reversed(range(cps)):
            IS = range(ci * nh, (ci + 1) * nh)
            ksd = [_mm(ks[i], ds_cur[hl[i]]) for i in IS]
            for i in IS:
                dsn[i] = ds_cur[hl[i]]
                dvn[i] = ptd[i] + ksd[hl[i]]
                dodv[i] = _cat16([do[i], dvn[i]], 0)
            dsq = [_mm_tn(qw[i], dodv[i]) for i in IS]
            ds_cur = [egl[i] * ds_cur[hl[i]] + dsq[hl[i]] for i in IS]
        for h in range(nh):
            ds_scr[base + h] = ds_cur[h]
        x1 = [_mm_nt(dodv[h], s[h]) for h in HS]
        dks = [_mm_nt(vn[h], dsn[h]) for h in HS]
        dov = [_mm_nt(do[h], vn[h]) for h in HS]
        vdo = [_mm_nt(vn[h], do[h]) for h in HS]
        kk = [_mm_nt(kb[h], k[h]) for h in HS]
        qk = [_mm_nt(q[h], k[h]) for h in HS]
        dgl = [egl[h] * jnp.sum(jnp.sum(s[h] * dsn[h], axis=1, keepdims=True), axis=0, keepdims=True) for h in HS]
        dqd = [x1[h][:CHUNK] for h in HS]
        duw = [jnp.concatenate([dvn[h], -x1[h][CHUNK:]], axis=1) for h in HS]
        tdu = [_mm_tn(t_ref[h // nh, hd[h]], duw[h]) for h in HS]
        dvk = [duw[h] + tdu[h] for h in HS]
        uw = [jnp.concatenate([u_ref[rs[h], cs[h]], w[h]], axis=1) for h in HS]
        da = [-jnp.where(ii > jj, _mm_nt(dvk[h], uw[h]), 0.0) for h in HS]
        da_t = [-jnp.where(jj > ii, _mm_nt(uw[h], dvk[h]), 0.0) for h in HS]
        dp = [jnp.where(ii >= jj, dov[h], 0.0) for h in HS]
        dp_t = [jnp.where(jj >= ii, vdo[h], 0.0) for h in HS]
        r1 = [_mm(_cat16([da[h] * dmat[h], dp[h] * dmat[h]], 0), k[h]) for h in HS]
        dk1 = [_mm(_cat16([da_t[h] * dmat_t[h], dp_t[h] * dmat_t[h]], 1), _cat16([kb[h], q[h]], 0)) for h in HS]
        lane = _lanes((CHUNK, LANE))
        for ci in range(cps):
            dsc = jnp.zeros((CHUNK, LANE), f32)
            for i in range(ci * nh, (ci + 1) * nh):
                h = hd[i]
                a = jnp.where(ii > jj, kk[i] * dmat[i], 0.0)
                p = jnp.where(ii >= jj, qk[i] * dmat[i], 0.0)
                gmat = da[i] * a + dp[i] * p
                dvb, dkbg = dvk[i][:, :HEAD_DIM], dvk[i][:, HEAD_DIM:]
                kbg = kb[i] * eg[i]
                dkb = r1[i][:CHUNK] + dkbg * eg[i]
                dq = r1[i][CHUNK:] + dqd[i] * eg[i]
                dk = dk1[i] + dks[i] * ekl[i] + dkb * beta[i]
                dbeta = jnp.sum(dkb * k[i] + dvb * v[i], axis=1, keepdims=True)
                ksum = jnp.sum(dks[i] * ks[i], axis=1, keepdims=True)
                dgl_tot = dgl[i] + jnp.sum(ksum, axis=0, keepdims=True)
                dgc = (jnp.sum(gmat, axis=1, keepdims=True) + jnp.sum(dqd[i] * qd[i] + dkbg * kbg, axis=1, keepdims=True)
                       - ksum)
                dgc = dgc + jnp.where(_rows(dgc.shape) == CHUNK - 1, dgl_tot, 0.0)
                dqkv_ref[rs[i], h * HEAD_DIM:(h + 1) * HEAD_DIM] = dq
                dqkv_ref[rs[i], W + h * HEAD_DIM:W + (h + 1) * HEAD_DIM] = dk
                dqkv_ref[rs[i], 2 * W + h * HEAD_DIM:2 * W + (h + 1) * HEAD_DIM] = dvb * beta[i]
                dsc = jnp.where(lane == h, dbeta, jnp.where(lane == HEADS + h, dgc, dsc))
                dgr_ref[ci, h:h + 1, :] = jnp.sum(gmat, axis=0, keepdims=True)
            dsc_ref[ci * CHUNK:(ci + 1) * CHUNK, :] = dsc

    row = lambda c: (nsteps - 1 - c, 0)
    lead3 = lambda c: (nsteps - 1 - c, 0, 0)
    return _pcall(
        body, name="gdn_bwd", grid=(nsteps,),
        in_specs=[pl.BlockSpec((rows_per_step, 3 * W), row), pl.BlockSpec((rows_per_step, LANE), row),
                  pl.BlockSpec((cps, HEADS, CHUNK), lead3),
                  pl.BlockSpec((rows_per_step, W), row), pl.BlockSpec((rows_per_step, W), row),
                  pl.BlockSpec((rows_per_step, W), row),
                  pl.BlockSpec((cps, HEADS, CHUNK, CHUNK), lambda c: (nsteps - 1 - c, 0, 0, 0)),
                  pl.BlockSpec((cps, W, HEAD_DIM), lead3), pl.BlockSpec((rows_per_step, W), row)],
        out_specs=[pl.BlockSpec((rows_per_step, 3 * W), row), pl.BlockSpec((rows_per_step, LANE), row),
                   pl.BlockSpec((cps, HEADS, CHUNK), lead3)],
        out_shape=[jax.ShapeDtypeStruct((L, 3 * W), f32), jax.ShapeDtypeStruct((L, LANE), f32),
                   jax.ShapeDtypeStruct((nc, HEADS, CHUNK), f32)],
        scratch_shapes=[pltpu.VMEM((HEADS, HEAD_DIM, HEAD_DIM), f32)],
        compiler_params=_cparams("arbitrary"),
    )(qkv, sc, gr, u_all, w_all, vn_all, t_all, sp_all, do_all)


def _qkv_bwd(proj, cw, dn, dproj):
    L = proj.shape[0]

    def body(x_ref, cw_ref, dn_ref, dproj_in, dx_ref, gcw_ref):
        j = pl.program_id(0)
        steps = GDN_WIDTH // ELT_W
        scale = jnp.where(j < steps, HEAD_DIM ** -0.5, 1.0).astype(f32)
        for ls in HALVES:
            x, dn_v = x_ref[:, ls], dn_ref[:, ls]
            c = _conv4(x, cw_ref, ls)
            sg = _sigmoid(c)
            a = c * sg
            rn = lax.rsqrt(jnp.sum(a * a, axis=1, keepdims=True) + EPS)
            da_n = (scale * rn) * (dn_v - a * ((rn * rn) * jnp.sum(dn_v * a, axis=1, keepdims=True)))
            da = jnp.where(j < 2 * steps, da_n, dn_v)
            dc = da * (sg * (1.0 + c * (1.0 - sg)))
            dc1, dc2, dc3 = _shift_up(dc, 1), _shift_up(dc, 2), _shift_up(dc, 3)
            gcw_ref[3:4, ls] = jnp.sum(dc * x, axis=0, keepdims=True)
            gcw_ref[2:3, ls] = jnp.sum(dc1 * x, axis=0, keepdims=True)
            gcw_ref[1:2, ls] = jnp.sum(dc2 * x, axis=0, keepdims=True)
            gcw_ref[0:1, ls] = jnp.sum(dc3 * x, axis=0, keepdims=True)
            dx = cw_ref[3:4, ls] * dc + cw_ref[2:3, ls] * dc1 + cw_ref[1:2, ls] * dc2 + cw_ref[0:1, ls] * dc3
            dx_ref[:, ls] = dx.astype(bf16)

    col = pl.BlockSpec((L, ELT_W), lambda j: (0, j))
    wspec = pl.BlockSpec((4, ELT_W), lambda j: (0, j))
    return _pcall(
        body, name="qkv_bwd", grid=(3 * GDN_WIDTH // ELT_W,),
        in_specs=[col, wspec, col, ANY], out_specs=[col, wspec],
        out_shape=[jax.ShapeDtypeStruct(dproj.shape, dproj.dtype), jax.ShapeDtypeStruct((4, 3 * GDN_WIDTH), f32)],
        input_output_aliases={3: 0},
        compiler_params=_cparams("parallel"),
    )(proj, cw, dn, dproj)


def _scalars_bwd(proj, alog_p, dtb_p, dsc, dgr_col, dproj, after):
    L = proj.shape[0]

    def body(x_ref, al_ref, dt_ref, dsc_ref, dgr_ref, dproj_in, after_ref, dba_ref, gs_ref):
        x, dsc_v = x_ref[...], dsc_ref[...]
        lane = _lanes(x.shape)
        dec = (lane >= HEADS) & (lane < 2 * HEADS)
        dg = jnp.where(dec, dsc_v - dgr_ref[...], 0.0)
        rc = _rows(x.shape) & (CHUNK - 1)
        for s in (1, 2, 4, 8, 16, 32):
            dg = dg + jnp.where(rc + s < CHUNK, pltpu.roll(dg, L - s, 0), 0.0)
        xa = x + dt_ref[...]
        ea = jnp.exp(al_ref[...])
        g = -ea * _softplus(xa)
        da = dg * (-ea) * _sigmoid(xa)
        beta = _sigmoid(x)
        db = dsc_v * beta * (1.0 - beta)
        dba_ref[:, :LANE] = jnp.where(lane < HEADS, db, jnp.where(dec, da, 0.0)).astype(bf16)
        dba_ref[:, LANE:] = jnp.zeros((L, ELT_W - LANE), bf16)
        g_al = jnp.sum(jnp.where(dec, dg * g, 0.0), axis=0, keepdims=True)
        g_dt = jnp.sum(jnp.where(dec, da, 0.0), axis=0, keepdims=True)
        row8 = _rows(gs_ref.shape)
        gs = jnp.where(row8 == 0, g_al, jnp.where(row8 == 1, g_dt, 0.0))
        gs_ref[...] = pltpu.roll(gs, LANE - HEADS, 1)

    full = pl.BlockSpec((L, LANE), lambda i: (0, 0))
    vec = pl.BlockSpec((1, LANE), lambda i: (0, 0))
    return _pcall(
        body, name="scalars_bwd", grid=(1,),
        in_specs=[pl.BlockSpec((L, LANE), lambda i: (0, OFF_BA // LANE)), vec, vec, full, full, ANY, ANY],
        out_specs=[pl.BlockSpec((L, ELT_W), lambda i: (0, OFF_BA // ELT_W)), pl.BlockSpec((8, LANE), lambda i: (0, 0))],
        out_shape=[jax.ShapeDtypeStruct(dproj.shape, dproj.dtype), jax.ShapeDtypeStruct((8, LANE), f32)],
        input_output_aliases={5: 0},
        compiler_params=_cparams("arbitrary"),
    )(proj, alog_p, dtb_p, dsc, dgr_col, dproj, after)


def _input_grad(dproj, wpad, x, nw, dy, after):
    L = x.shape[0]
    tm = min(512, L)
    cuts = (0, 512, 1024, 2048, 3072, 5120, 7168, PROJ_PAD)
    nk = len(cuts) - 1

    def body(dp_ref, w_hbm, x_ref, nw_ref, dy_ref, after_ref, gx_ref, gnw_ref, w_vmem, sems):
        first = pl.program_id(0) == 0
        loads = [pltpu.make_async_copy(w_hbm.at[cuts[k]:cuts[k + 1], :], w_vmem.at[cuts[k]:cuts[k + 1], :], sems.at[k])
                 for k in range(nk)]

        @pl.when(first)
        def _():
            for cp in loads:
                cp.start()
            gnw_ref[...] = jnp.zeros_like(gnw_ref)
        dh = None
        for k in range(nk):
            pl.when(first)(loads[k].wait)
            part = jnp.dot(dp_ref[:, cuts[k]:cuts[k + 1]], w_vmem[cuts[k]:cuts[k + 1], :], preferred_element_type=f32)
            dh = part if dh is None else dh + part
        xv, nwv = x_ref[...], nw_ref[...]
        r = lax.rsqrt(jnp.mean(xv * xv, axis=-1, keepdims=True) + EPS)
        xh = xv * r
        gnw_ref[...] += jnp.sum(dh * xh, axis=0, keepdims=True)
        dxh = dh * nwv
        gx_ref[...] = dy_ref[...] + r * (dxh - xh * jnp.mean(dxh * xh, axis=-1, keepdims=True))

    row = lambda i: (i, 0)
    fix = lambda i: (0, 0)
    return _pcall(
        body, name="input_grad", grid=(L // tm,),
        in_specs=[pl.BlockSpec((tm, PROJ_PAD), row), ANY, pl.BlockSpec((tm, D_MODEL), row),
                  pl.BlockSpec((1, D_MODEL), fix), pl.BlockSpec((tm, D_MODEL), row), ANY],
        out_specs=[pl.BlockSpec((tm, D_MODEL), row), pl.BlockSpec((1, D_MODEL), fix)],
        out_shape=[jax.ShapeDtypeStruct((L, D_MODEL), f32), jax.ShapeDtypeStruct((1, D_MODEL), f32)],
        scratch_shapes=[pltpu.VMEM(wpad.shape, bf16), pltpu.SemaphoreType.DMA((nk,))],
        compiler_params=_cparams("arbitrary"),
    )(dproj, wpad, x, nw, dy, after)


def _adamw_reduce(parts, w, m, v, name, first_row=None):
    R, C = w.shape[0], w.shape[-1]
    n_parts = parts.shape[0]
    tr = 128 if R % 128 == 0 else R
    c1 = 1.0 - ADAM_B1 ** ADAM_STEP
    c2 = 1.0 - ADAM_B2 ** ADAM_STEP
    at = (slice(None), 0, slice(None)) if w.ndim == 3 else Ellipsis
    window = (slice(None), slice(None)) if first_row is None else (slice(first_row, first_row + R), slice(0, C))

    def body(p_ref, w_ref, m_ref, v_ref, g_ref, d_ref, nm_ref, nv_ref):
        g = p_ref[(0,) + window].astype(f32)
        for s in range(1, n_parts):
            g = g + p_ref[(s,) + window].astype(f32)
        nm = ADAM_B1 * m_ref[at] + (1.0 - ADAM_B1) * g
        nv = ADAM_B2 * v_ref[at] + (1.0 - ADAM_B2) * (g * g)
        g_ref[at] = g
        nm_ref[at] = nm
        nv_ref[at] = nv
        d_ref[at] = -ADAM_LR * ((nm / c1) / (jnp.sqrt(nv / c2) + ADAM_EPS) + ADAM_WD * w_ref[at])

    blk = pl.BlockSpec((tr, 1, C), lambda i: (i, 0, 0)) if w.ndim == 3 else pl.BlockSpec((tr, C), lambda i: (i, 0))
    out = jax.ShapeDtypeStruct(w.shape, f32)
    if first_row is None:
        p_spec = pl.BlockSpec((n_parts, tr, C), lambda i: (0, i, 0))
    else:
        assert tr == R
        p_spec = pl.BlockSpec(parts.shape, lambda i: (0, 0, 0))
    return _pcall(
        body, name=name, grid=(R // tr,),
        in_specs=[p_spec, blk, blk, blk],
        out_specs=[blk] * 4, out_shape=[out] * 4,
        compiler_params=_cparams("parallel"),
    )(parts, w, m, v)


SMALL_SLOTS = ((0, D_MODEL), (D_MODEL, D_MODEL), (2 * D_MODEL, D_MODEL), (3 * D_MODEL, LANE),
               (3 * D_MODEL + LANE, HEADS), (3 * D_MODEL + 2 * LANE, HEADS))
SMALL_LOSS = 3 * D_MODEL + 3 * LANE
SMALL_W = SMALL_LOSS + LANE


def _pack_small(gs, after):
    def body(nw_ref, cb_ref, fw_ref, gn_ref, sc_ref, ls_ref, after_ref, o_ref):
        for ref, (start, width) in zip((nw_ref, cb_ref, fw_ref, gn_ref), SMALL_SLOTS[:4]):
            o_ref[:, start:start + width] = ref[...]
        o_ref[:, SMALL_SLOTS[4][0]:SMALL_SLOTS[4][0] + LANE] = sc_ref[0:1, :]
        o_ref[:, SMALL_SLOTS[5][0]:SMALL_SLOTS[5][0] + LANE] = sc_ref[1:2, :]
        o_ref[:, SMALL_LOSS:SMALL_W] = ls_ref[...]

    vm = pl.BlockSpec(memory_space=pltpu.VMEM)
    return _pcall(body, name="pack_small_grads", out_shape=jax.ShapeDtypeStruct((1, SMALL_W), f32),
                  in_specs=[vm] * 6 + [ANY], out_specs=vm)(*gs, after)


def _adamw_small(parts, ws, ms, vs):
    c1 = 1.0 - ADAM_B1 ** ADAM_STEP
    c2 = 1.0 - ADAM_B2 ** ADAM_STEP
    np_ = len(ws)

    def body(*refs):
        p_ref = refs[0]
        w_refs, m_refs, v_refs = refs[1:1 + np_], refs[1 + np_:1 + 2 * np_], refs[1 + 2 * np_:1 + 3 * np_]
        outs = refs[1 + 3 * np_:]
        g_refs, d_refs, nm_refs, nv_refs = (outs[i * np_:(i + 1) * np_] for i in range(4))
        loss_ref = outs[4 * np_]

        def total(start, width):
            t = p_ref[0, :, start:start + width]
            for s in range(1, N_DEV):
                t = t + p_ref[s, :, start:start + width]
            return t

        for i, (start, width) in enumerate(SMALL_SLOTS):
            g = total(start, width)
            nm = ADAM_B1 * m_refs[i][...] + (1.0 - ADAM_B1) * g
            nv = ADAM_B2 * v_refs[i][...] + (1.0 - ADAM_B2) * (g * g)
            g_refs[i][...] = g
            nm_refs[i][...] = nm
            nv_refs[i][...] = nv
            d_refs[i][...] = -ADAM_LR * ((nm / c1) / (jnp.sqrt(nv / c2) + ADAM_EPS) + ADAM_WD * w_refs[i][...])
        loss_ref[...] = total(SMALL_LOSS, LANE)

    vm = pl.BlockSpec(memory_space=pltpu.VMEM)
    shapes = [jax.ShapeDtypeStruct(w.shape, f32) for w in ws]
    res = _pcall(body, name="adamw_small", out_shape=shapes * 4 + [jax.ShapeDtypeStruct((1, LANE), f32)],
                 in_specs=[vm] * (1 + 3 * np_), out_specs=[vm] * (4 * np_ + 1))(parts, *ws, *ms, *vs)
    return [res[i * np_:(i + 1) * np_] for i in range(4)], res[4 * np_]


def _adamw_w_in(part_a, part_b, w3, m3, v3, after):
    _, n, _ = part_a.shape
    c1 = 1.0 - ADAM_B1 ** ADAM_STEP
    c2 = 1.0 - ADAM_B2 ** ADAM_STEP

    def body(pa_ref, pb_ref, w_ref, m_ref, v_ref, after_ref, g_ref, d_ref, nm_ref, nv_ref):
        g = pa_ref[0].astype(f32) + pb_ref[0].astype(f32)
        nm = ADAM_B1 * m_ref[:, 0, :] + (1.0 - ADAM_B1) * g
        nv = ADAM_B2 * v_ref[:, 0, :] + (1.0 - ADAM_B2) * (g * g)
        g_ref[:, 0, :] = g
        nm_ref[:, 0, :] = nm
        nv_ref[:, 0, :] = nv
        d_ref[:, 0, :] = -ADAM_LR * ((nm / c1) / (jnp.sqrt(nv / c2) + ADAM_EPS) + ADAM_WD * w_ref[:, 0, :])

    tile = 2 * COL_TILE
    blk = pl.BlockSpec((n, 1, tile), lambda j: (0, 0, j))
    out = jax.ShapeDtypeStruct((n, 1, D_MODEL), f32)
    return _pcall(
        body, name="adamw_w_in", grid=(D_MODEL // tile,),
        in_specs=[pl.BlockSpec((1, n, tile), lambda j: (0, 0, j))] * 2 + [blk, blk, blk, ANY],
        out_specs=[blk] * 4, out_shape=[out] * 4,
        compiler_params=_cparams("parallel"),
    )(part_a, part_b, w3, m3, v3, after)


def _pad_lanes(vec8, start):
    return jnp.pad(vec8.reshape(1, -1), ((0, 0), (start, LANE - start - vec8.size)))


def kernel(x, norm_in_w, w_in, conv_qkv_w, A_log, dt_bias, gdn_norm_w, conv_w, conv_b, w_out, final_norm_w, loss_target, m_norm_in_w, m_w_in, m_conv_qkv_w, m_A_log, m_dt_bias, m_gdn_norm_w, m_conv_w, m_conv_b, m_w_out, m_final_norm_w, v_norm_in_w, v_w_in, v_conv_qkv_w, v_A_log, v_dt_bias, v_gdn_norm_w, v_conv_w, v_conv_b, v_w_out, v_final_norm_w):
    L = x.shape[1]
    nc = L // CHUNK
    xs = x[0]
    tgt = loss_target[0]
    fnw = final_norm_w.reshape(1, D_MODEL)

    as_rows = lambda a: jnp.transpose(a, (2, 0, 1))
    as_taps = lambda a: jnp.transpose(a, (1, 0, 2))
    win_blk, wo_blk = _cast_weights(as_rows(w_in), w_out[0])
    win_g, cqkv_g, cw_g = _all_gather([win_blk, conv_qkv_w[0], as_taps(conv_w)], "gather_weights",
                                      pieces=[8, 1, 1])
    wpad = _relayout_w_in(win_g)
    cqkv = jnp.concatenate([cqkv_g[d] for d in range(N_DEV)], axis=1)
    cw = jnp.concatenate([cw_g[d][:, 0, :] for d in range(N_DEV)], axis=1)
    alog_p = _pad_lanes(A_log, HEADS)
    dtb_p = _pad_lanes(dt_bias, HEADS)
    tok = lambda started: started[4]
    wo_started = _spread_start(wo_blk, wpad, "gather", "gather_w_out_start")

    proj, h = _in_proj(xs, norm_in_w, wpad, tok(wo_started))
    qkv = _qkv_act(proj, cqkv)
    sc, gr = _scalars(proj, alog_p, dtb_p)
    o, u_all, w_all, vn_all, t_all, sp_all = _gdn_fwd(qkv, sc, gr)
    mix = _conv_fwd(proj, cw, conv_b, _gdn_gate(o, proj, gdn_norm_w))
    wo = _spread_wait(wo_started, mix, "gather", "gather_w_out_wait")[1].reshape(-1, D_MODEL)
    dy, dyb, dmix, g_fnw, loss_v = _out_proj_loss(xs, mix, wo, fnw, tgt)

    g_wout = _tn_matmul(mix, dyb, "grad_w_out")
    gwo_started = _spread_start(g_wout.reshape(N_DEV, -1, D_MODEL), dyb, "scatter", "exchange_grad_w_out_start")
    do, dproj, g_gnw = _gdn_gate_bwd(o, proj, gdn_norm_w, dmix, tok(gwo_started))
    dproj, g_cw, g_cb = _conv_bwd(proj, cw, conv_b, dmix, dproj)
    dqkv_n, dsc, dgr = _gdn_bwd(qkv, sc, gr, u_all, w_all, vn_all, t_all, sp_all, do)
    dproj, g_cqkv = _qkv_bwd(proj, cqkv, dqkv_n, dproj)
    g_cqkv_blk = g_cqkv.reshape(4, N_DEV, -1).transpose(1, 0, 2)
    g_cw_blk = jnp.pad(g_cw.reshape(3, N_DEV, -1).transpose(1, 0, 2),
                       ((0, 0), (0, 1), (0, g_cqkv_blk.shape[2] - g_cw.shape[1] // N_DEV)))
    gsm_started = _spread_start(jnp.concatenate([g_cqkv_blk, g_cw_blk], axis=1), g_cqkv, "scatter",
                                "exchange_small_sharded_grads_start")
    dgr_col = jnp.pad(dgr.transpose(0, 2, 1).reshape(L, HEADS), ((0, 0), (HEADS, LANE - 2 * HEADS)))
    dproj, g_sc = _scalars_bwd(proj, alog_p, dtb_p, dsc, dgr_col, dproj, tok(gsm_started))
    g_win_blk = _grad_blocks(_tn_matmul(dproj, h, "grad_w_in"))

    (p_win,) = _pair_exchange([g_win_blk], "exchange_grads_pair")
    r_small = _spread_wait(gsm_started, p_win, "scatter", "exchange_small_sharded_grads_wait")[1]
    s_win = _pair_sum(g_win_blk, p_win, "pair_sum_w_in")
    gw1_started = _spread_start(s_win, r_small, "axis_a", "exchange_grads_axis1_start")
    grad_x, g_nw = _input_grad(dproj, wpad, xs, norm_in_w, dy, tok(gw1_started))
    s_thru, got1 = _spread_wait(gw1_started, grad_x, "axis_a", "exchange_grads_axis1_wait")
    t_win = _axis_sum(s_thru, got1, "axis_sum_w_in")
    gw2_started = _spread_start(t_win, got1, "axis_b", "exchange_grads_axis2_start")

    r_wout = _spread_wait(gwo_started, tok(gw2_started), "scatter", "exchange_grad_w_out_wait")[1]
    upd_wout =_adamw_reduce(r_wout, w_out[0], m_w_out[0], v_w_out[0], "adamw_w_out")
    upd_cqkv = _adamw_reduce(r_small, conv_qkv_w[0], m_conv_qkv_w[0], v_conv_qkv_w[0], "adamw_conv_qkv_w", first_row=0)
    upd_cw = _adamw_reduce(r_small, as_taps(conv_w), as_taps(m_conv_w), as_taps(v_conv_w), "adamw_conv_w", first_row=4)

    t_thru, got2 = _spread_wait(gw2_started, upd_cw[0], "axis_b", "exchange_grads_axis2_wait")

    small_g = _pack_small([g_nw, g_cb, g_fnw, g_gnw, g_sc, loss_v], got2)
    gsg_started = _spread_start(small_g, got2, "gather", "gather_small_grads_start")
    upd_win_t = _adamw_w_in(t_thru, got2, as_rows(w_in), as_rows(m_w_in), as_rows(v_w_in), tok(gsg_started))
    upd_win = [jnp.transpose(a, (1, 2, 0)) for a in upd_win_t]
    small_all = _spread_wait(gsg_started, upd_win_t[0], "gather", "gather_small_grads_wait")[1]
    fvec = lambda a: a.reshape(1, D_MODEL)
    upd_small, loss_sum = _adamw_small(
        small_all,
        [norm_in_w, conv_b, fvec(final_norm_w), gdn_norm_w, A_log, dt_bias],
        [m_norm_in_w, m_conv_b, fvec(m_final_norm_w), m_gdn_norm_w, m_A_log, m_dt_bias],
        [v_norm_in_w, v_conv_b, fvec(v_final_norm_w), v_gdn_norm_w, v_A_log, v_dt_bias])

    outs = [loss_sum[0, 0], grad_x[None]]
    for k in range(4):
        nw_k, cb_k, fw_k, gn_k, al_k, dt_k = upd_small[k]
        outs += [nw_k, upd_win[k], upd_cqkv[k][None], al_k, dt_k, gn_k,
                 as_taps(upd_cw[k]), cb_k, upd_wout[k][None], fw_k.reshape(D_MODEL)]
    return tuple(outs)
```

```python
import jax
import jax.numpy as jnp
from jax import lax
from jax.experimental import pallas as pl
from jax.experimental.pallas import tpu as pltpu

f32 = jnp.float32
bf16 = jnp.bfloat16

N_DEV = 8
D_MODEL = 1024
HEADS = 8
HEAD_DIM = 128
CHUNK = 64
GDN_CPS = 4
GDN_CPS_BWD = 1
GDN_WIDTH = HEADS * HEAD_DIM
CONV_WIDTH = 1024
PROJ_WIDTH = 8208
SHARD_W = PROJ_WIDTH // N_DEV
EPS = 1e-6

LANE = 128
ELT_W = 256

OFF_QKV, OFF_ZG, OFF_CONV, OFF_BA = 0, 3072, 4096, 8192
CONV_BLOCK = 4 * ELT_W
PROJ_PAD = 8448
NAT_BA, NAT_CONV = 4096, 4112


def _padded_col(n):
    if n < NAT_BA:
        return n
    if n < NAT_CONV:
        return OFF_BA + n - NAT_BA
    g, ch = divmod(n - NAT_CONV, CONV_WIDTH)
    j, r = divmod(ch, ELT_W)
    return OFF_CONV + CONV_BLOCK * j + ELT_W * g + r


def _layout_segments(n0, n1):
    cuts = [NAT_BA, NAT_CONV] + [NAT_CONV + ELT_W * k for k in range(1, 4 * CONV_WIDTH // ELT_W)]
    pts = [n0] + [c for c in cuts if n0 < c < n1] + [n1]
    return [(lo, hi - lo, _padded_col(lo)) for lo, hi in zip(pts, pts[1:])]

ADAM_LR, ADAM_B1, ADAM_B2, ADAM_EPS, ADAM_WD, ADAM_STEP = 0.001, 0.9, 0.999, 1e-08, 0.01, 10

V7X_VMEM_BYTES = 64 * 1024 * 1024
VMEM_LIMIT = V7X_VMEM_BYTES - 8 * 1024 * 1024

MESH = pl.DeviceIdType.MESH
ANY = pl.BlockSpec(memory_space=pl.ANY)


def _pcall(body, **kw):
    return pl.pallas_call(body, **kw)


def _cparams(*sem):
    return pltpu.CompilerParams(dimension_semantics=sem if sem else None, vmem_limit_bytes=VMEM_LIMIT)


def _mm(a, b):
    return jnp.dot(a.astype(bf16), b.astype(bf16), preferred_element_type=f32)


def _mm_nt(a, b):
    return lax.dot_general(a.astype(bf16), b.astype(bf16), (((1,), (1,)), ((), ())), preferred_element_type=f32)


def _cat16(parts, axis):
    return jnp.concatenate([p.astype(bf16) for p in parts], axis=axis)


def _mm_tn(a, b):
    return lax.dot_general(a.astype(bf16), b.astype(bf16), (((0,), (0,)), ((), ())), preferred_element_type=f32)


def _rows(shape):
    return lax.broadcasted_iota(jnp.int32, shape, 0)


def _lanes(shape):
    return lax.broadcasted_iota(jnp.int32, shape, 1)


def _shift_down(x, s):
    if s == 0:
        return x
    return jnp.where(_rows(x.shape) >= s, pltpu.roll(x, s, 0), 0.0)


def _shift_up(x, s):
    if s == 0:
        return x
    n = x.shape[0]
    return jnp.where(_rows(x.shape) < n - s, pltpu.roll(x, n - s, 0), 0.0)


def _sigmoid(x):
    return jax.nn.sigmoid(x)


def _softplus(x):
    e = jnp.exp(-jnp.abs(x))
    small = e * (1.0 - e * (0.5 - e * (1.0 / 3.0)))
    return jnp.maximum(x, 0.0) + jnp.where(e < 0.01, small, jnp.log(1.0 + e))


def _mesh_pos():
    return lax.axis_index("x"), lax.axis_index("y"), lax.axis_index("c")


def _flat(px, py, pc):
    return 4 * px + 2 * py + pc


def _all_gather(xs, name, pieces=None):
    n = len(xs)
    pieces = pieces or [1] * n
    items = [(a, q) for a in range(n) for q in range(pieces[a])]
    ni = len(items)

    def view(ref, i):
        a, q = items[i]
        if pieces[a] == 1:
            return ref
        wd = xs[a].shape[-1] // pieces[a]
        return ref.at[(slice(None),) * (xs[a].ndim - 1) + (pl.ds(q * wd, wd),)]

    def body(*refs):
        x_refs, o_refs = refs[:n], refs[n:2 * n]
        send_sems, recv_sems, local_sems = refs[2 * n:]
        x, y, c = _mesh_pos()
        me, sibling = (x, y, c), (x, y, 1 - c)
        flip = lambda v, bit: v + bit - 2 * v * bit
        nbr_a = (flip(x, 1 - c), flip(y, c))
        nbr_b = (flip(x, c), flip(y, 1 - c))
        diag = (1 - x, 1 - y)

        def copy(i, k, block, to, own=False):
            a = items[i][0]
            dst = view(o_refs[a].at[_flat(*block)], i)
            return pltpu.make_async_remote_copy(
                src_ref=view(x_refs[a], i) if own else dst, dst_ref=dst,
                send_sem=send_sems.at[i, k], recv_sem=recv_sems.at[i, k], device_id=to, device_id_type=MESH)

        mine, sent = [], []

        def go(cp):
            cp.start()
            sent.append(cp)

        for a in range(n):
            cp = pltpu.make_async_copy(x_refs[a], o_refs[a].at[_flat(*me)], local_sems.at[a])
            cp.start()
            mine.append(cp)
        for a in range(ni):
            go(copy(a, 1, me, (*nbr_a, c), own=True))
            go(copy(a, 2, me, (*nbr_b, c), own=True))
            go(copy(a, 0, me, sibling, own=True))
        for a in range(ni):
            copy(a, 1, (*nbr_a, c), me).wait_recv()
            go(copy(a, 3, (*nbr_a, c), (*nbr_b, c)))
            go(copy(a, 4, (*nbr_a, c), sibling))
        for a in range(ni):
            copy(a, 2, (*nbr_b, c), me).wait_recv()
            go(copy(a, 5, (*nbr_b, c), sibling))
        for a in range(ni):
            copy(a, 3, (*diag, c), me).wait_recv()
            go(copy(a, 6, (*diag, c), sibling))
        for a in range(ni):
            copy(a, 0, sibling, me).wait_recv()
            copy(a, 4, (*nbr_b, 1 - c), me).wait_recv()
            copy(a, 5, (*nbr_a, 1 - c), me).wait_recv()
            copy(a, 6, (*diag, 1 - c), me).wait_recv()
        for cp in sent:
            cp.wait_send()
        for cp in mine:
            cp.wait()

    outs = _pcall(
        body, name=name,
        out_shape=[jax.ShapeDtypeStruct((N_DEV,) + a.shape, a.dtype) for a in xs],
        in_specs=[ANY] * n, out_specs=[ANY] * n,
        scratch_shapes=[pltpu.SemaphoreType.DMA((ni, 7)), pltpu.SemaphoreType.DMA((ni, 7)), pltpu.SemaphoreType.DMA((n,))],
    )(*xs)
    return list(outs)


def _pair_exchange(gs, name):
    n = len(gs)
    chips = [(0, 0), (0, 1), (1, 0), (1, 1)]

    def body(*refs):
        g_refs, o_refs = refs[:n], refs[n:2 * n]
        send_sems, recv_sems = refs[2 * n:]
        x, y, c = _mesh_pos()
        sibling = (x, y, 1 - c)

        def copy(a, i):
            xp, yp = chips[i]
            return pltpu.make_async_remote_copy(
                src_ref=g_refs[a].at[_flat(xp, yp, 1 - c)], dst_ref=o_refs[a].at[i],
                send_sem=send_sems.at[a, i], recv_sem=recv_sems.at[a, i], device_id=sibling, device_id_type=MESH)

        cps = [copy(a, i) for a in range(n) for i in range(4)]
        for cp in cps:
            cp.start()
        for cp in cps:
            cp.wait()

    outs = _pcall(
        body, name=name,
        out_shape=[jax.ShapeDtypeStruct((4,) + a.shape[1:], a.dtype) for a in gs],
        in_specs=[ANY] * n, out_specs=[ANY] * n,
        scratch_shapes=[pltpu.SemaphoreType.DMA((n, 4)), pltpu.SemaphoreType.DMA((n, 4))],
    )(*gs)
    return list(outs)


def _pair_sum(g, p1, name):
    _, R, C = g.shape
    tr = 256 if R % 256 == 0 else R
    cidx = lax.axis_index("c").astype(jnp.int32).reshape(1)

    def body(c_ref, g_ref, p_ref, o_ref):
        o_ref[...] = (g_ref[...].astype(f32) + p_ref[...].astype(f32)).astype(o_ref.dtype)

    return _pcall(
        body, name=name,
        grid_spec=pltpu.PrefetchScalarGridSpec(
            num_scalar_prefetch=1, grid=(4, R // tr),
            in_specs=[pl.BlockSpec((1, tr, C), lambda i, r, c_ref: (2 * i + c_ref[0], r, 0)),
                      pl.BlockSpec((1, tr, C), lambda i, r, c_ref: (i, r, 0))],
            out_specs=pl.BlockSpec((1, tr, C), lambda i, r, c_ref: (i, r, 0))),
        out_shape=jax.ShapeDtypeStruct((4, R, C), g.dtype),
        compiler_params=_cparams("parallel", "parallel"),
    )(cidx, g, p1)


def _axis_sum(s, got, name):
    _, R, C = s.shape
    x, y, c = _mesh_pos()
    me, _, b, _ = _axis_chips(x, y, c)
    idx = jnp.stack([2 * me[0] + me[1], 2 * b[0] + b[1]]).astype(jnp.int32)

    def body(idx_ref, s_ref, g_ref, o_ref):
        o_ref[...] = (s_ref[...].astype(f32) + g_ref[...].astype(f32)).astype(o_ref.dtype)

    return _pcall(
        body, name=name,
        grid_spec=pltpu.PrefetchScalarGridSpec(
            num_scalar_prefetch=1, grid=(2,),
            in_specs=[pl.BlockSpec((1, R, C), lambda k, idx_ref: (idx_ref[k], 0, 0)),
                      pl.BlockSpec((1, R, C), lambda k, idx_ref: (k, 0, 0))],
            out_specs=pl.BlockSpec((1, R, C), lambda k, idx_ref: (k, 0, 0))),
        out_shape=jax.ShapeDtypeStruct((2, R, C), s.dtype),
        compiler_params=_cparams("parallel"),
    )(idx, s, got)


HBM = pl.BlockSpec(memory_space=pltpu.HBM)
SEM = pl.BlockSpec(memory_space=pltpu.SEMAPHORE)
EFFECT = pltpu.SideEffectType.DATAFLOW_SIDE_EFFECTING


def _peers(x, y, c):
    out = []
    for k in range(1, N_DEV):
        kx, ky, kc = (k >> 2) & 1, (k >> 1) & 1, k & 1
        out.append(((1 - x) if kx else x, (1 - y) if ky else y, (1 - c) if kc else c))
    return out


SPREAD_COPIES = {"gather": N_DEV - 1, "scatter": N_DEV - 1, "axis_a": 2, "axis_b": 1}
SPREAD_SLOTS = {"axis_a": 2, "axis_b": 1}


def _axis_chips(x, y, c):
    flip = lambda v, bit: v + bit - 2 * v * bit
    return (x, y), (flip(x, 1 - c), flip(y, c)), (flip(x, c), flip(y, 1 - c)), (1 - x, 1 - y)


def _spread_copy(src_ref, land_ref, send_sems, recv_sems, k, plan):
    x, y, c = _mesh_pos()
    if plan in ("axis_a", "axis_b"):
        _, a, b, d = _axis_chips(x, y, c)
        chip = lambda p: 2 * p[0] + p[1]
        peer = (*(a if plan == "axis_a" else b), c)
        src = src_ref.at[chip(a) if k == 0 else chip(d)] if plan == "axis_a" else src_ref.at[1]
        slot = k
    else:
        peer = _peers(x, y, c)[k]
        src, slot = (src_ref.at[_flat(*peer)] if plan == "scatter" else src_ref), _flat(x, y, c)
    return pltpu.make_async_remote_copy(
        src_ref=src, dst_ref=land_ref.at[slot], send_sem=send_sems.at[k], recv_sem=recv_sems.at[k],
        device_id=peer, device_id_type=MESH)


def _own_copy(src_ref, land_ref, send_sems, plan):
    me = _flat(*_mesh_pos())
    return pltpu.make_async_copy(src_ref.at[me] if plan == "scatter" else src_ref, land_ref.at[me],
                                 send_sems.at[SPREAD_COPIES[plan]])


def _spread_start(src, after, plan, name):
    land_shape = (N_DEV,) + src.shape if plan == "gather" else src.shape
    if plan in SPREAD_SLOTS:
        land_shape = (SPREAD_SLOTS[plan],) + src.shape[1:]
    n_copies = SPREAD_COPIES[plan]

    def body(src_ref, land_ref, after_ref, send_sems, recv_sems, src_thru, land_thru, token):
        for k in range(n_copies):
            _spread_copy(src_ref, land_ref, send_sems, recv_sems, k, plan).start()
        if plan not in SPREAD_SLOTS:
            _own_copy(src_ref, land_ref, send_sems, plan).start()
        token[...] = jnp.zeros_like(token)

    return _pcall(
        body, name=name,
        out_shape=(pltpu.SemaphoreType.DMA((n_copies + (plan not in SPREAD_SLOTS),)), pltpu.SemaphoreType.DMA((n_copies,)),
                   pltpu.HBM(src.shape, src.dtype), pltpu.HBM(land_shape, src.dtype), jax.ShapeDtypeStruct((8, LANE), f32)),
        in_specs=(HBM, HBM, ANY), out_specs=(SEM, SEM, HBM, HBM, pl.BlockSpec(memory_space=pltpu.VMEM)),
        input_output_aliases={0: 2, 1: 3},
        compiler_params=pltpu.CompilerParams(has_side_effects=EFFECT),
    )(pltpu.with_memory_space_constraint(src, pltpu.HBM),
      pltpu.with_memory_space_constraint(lax.empty(land_shape, src.dtype), pltpu.HBM), after)


def _spread_wait(started, after, plan, name):
    send_sems, recv_sems, src_thru, land_thru, _ = started

    def body(src_ref, land_ref, send_sems, recv_sems, after_ref, src_dead, got_ref):
        for k in range(SPREAD_COPIES[plan]):
            cp = _spread_copy(src_ref, land_ref, send_sems, recv_sems, k, plan)
            cp.wait_send()
            cp.wait_recv()
        if plan not in SPREAD_SLOTS:
            _own_copy(src_ref, land_ref, send_sems, plan).wait()

    return _pcall(
        body, name=name,
        out_shape=(pltpu.HBM(src_thru.shape, src_thru.dtype), pltpu.HBM(land_thru.shape, land_thru.dtype)),
        in_specs=(HBM, HBM, SEM, SEM, ANY), out_specs=(HBM, HBM), input_output_aliases={0: 0, 1: 1},
        compiler_params=pltpu.CompilerParams(has_side_effects=EFFECT),
    )(src_thru, land_thru, send_sems, recv_sems, after)


COL_TILE = 256


def _cast_weights(w3, wo):
    n = w3.shape[0]

    def body(w_ref, wo_ref, o_ref, oo_ref):
        o_ref[...] = w_ref[:, 0, :].astype(bf16)
        oo_ref[...] = wo_ref[...].astype(bf16)

    tile = 2 * COL_TILE
    return _pcall(
        body, name="cast_weights", grid=(D_MODEL // tile,),
        in_specs=[pl.BlockSpec((n, 1, tile), lambda j: (0, 0, j)), pl.BlockSpec((wo.shape[0], tile), lambda j: (0, j))],
        out_specs=[pl.BlockSpec((n, tile), lambda j: (0, j)), pl.BlockSpec((wo.shape[0], tile), lambda j: (0, j))],
        out_shape=[jax.ShapeDtypeStruct((n, D_MODEL), bf16), jax.ShapeDtypeStruct(wo.shape, bf16)],
        compiler_params=_cparams("parallel"),
    )(w3, wo)


def _relayout_w_in(win_g):
    def body(g_ref, o_ref):
        used = OFF_BA + NAT_CONV - NAT_BA
        o_ref[used:PROJ_PAD, :] = jnp.zeros((PROJ_PAD - used, COL_TILE), o_ref.dtype)
        for d in range(N_DEV):
            for lo, width, dst in _layout_segments(d * SHARD_W, (d + 1) * SHARD_W):
                src = lo - d * SHARD_W
                o_ref[dst:dst + width, :] = g_ref[d, src:src + width, :]

    return _pcall(
        body, name="relayout_w_in", grid=(D_MODEL // COL_TILE,),
        in_specs=[pl.BlockSpec((N_DEV, SHARD_W, COL_TILE), lambda j: (0, 0, j))],
        out_specs=pl.BlockSpec((PROJ_PAD, COL_TILE), lambda j: (0, j)),
        out_shape=jax.ShapeDtypeStruct((PROJ_PAD, D_MODEL), win_g.dtype),
        compiler_params=_cparams("parallel"),
    )(win_g)


def _grad_blocks(g_t):
    def body(p_ref, o_ref):
        for d in range(N_DEV):
            for lo, width, src in _layout_segments(d * SHARD_W, (d + 1) * SHARD_W):
                dst = lo - d * SHARD_W
                o_ref[d, dst:dst + width, :] = p_ref[src:src + width, :]

    return _pcall(
        body, name="grad_blocks", grid=(D_MODEL // COL_TILE,),
        in_specs=[pl.BlockSpec((PROJ_PAD, COL_TILE), lambda j: (0, j))],
        out_specs=pl.BlockSpec((N_DEV, SHARD_W, COL_TILE), lambda j: (0, 0, j)),
        out_shape=jax.ShapeDtypeStruct((N_DEV, SHARD_W, D_MODEL), bf16),
        compiler_params=_cparams("parallel"),
    )(g_t)


def _in_proj(x, nw, wpad_t, after):
    L = x.shape[0]
    tn = 768
    nj = wpad_t.shape[0] // tn

    def body(x_ref, nw_ref, w_ref, after_ref, proj_ref, h_ref):
        first = pl.program_id(0) == 0

        def project(r, n, hv):
            proj_ref[r:r + n, :] = lax.dot_general(hv, w_ref[...], (((1,), (1,)), ((), ())), preferred_element_type=f32)

        @pl.when(first)
        def _():
            for r in range(0, L, 256):
                xs = x_ref[r:r + 256, :]
                ms = jnp.mean(xs * xs, axis=-1, keepdims=True)
                hv = ((xs * lax.rsqrt(ms + EPS)) * nw_ref[...]).astype(bf16)
                h_ref[r:r + 256, :] = hv
                project(r, 256, hv)

        @pl.when(jnp.logical_not(first))
        def _():
            for r in range(0, L, 512):
                project(r, 512, h_ref[r:r + 512, :])

    return _pcall(
        body, name="in_proj", grid=(nj,),
        in_specs=[pl.BlockSpec((L, D_MODEL), lambda j: (0, 0)), pl.BlockSpec((1, D_MODEL), lambda j: (0, 0)),
                  pl.BlockSpec((tn, D_MODEL), lambda j: (j, 0)), ANY],
        out_specs=[pl.BlockSpec((L, tn), lambda j: (0, j)), pl.BlockSpec((L, D_MODEL), lambda j: (0, 0))],
        out_shape=[jax.ShapeDtypeStruct((L, wpad_t.shape[0]), f32), jax.ShapeDtypeStruct((L, D_MODEL), bf16)],
        compiler_params=_cparams("arbitrary"),
    )(x, nw, wpad_t, after)


HALVES = [slice(i * LANE, (i + 1) * LANE) for i in range(ELT_W // LANE)]
QKV_W = 512
QKV_HEADS = [slice(i * LANE, (i + 1) * LANE) for i in range(QKV_W // LANE)]
STEPS_PER_GROUP = GDN_WIDTH // QKV_W


def _conv4(x, cw_ref, ls):
    return (cw_ref[3:4, ls] * x + cw_ref[2:3, ls] * _shift_down(x, 1) + cw_ref[1:2, ls] * _shift_down(x, 2)
            + cw_ref[0:1, ls] * _shift_down(x, 3))


def _qkv_act(proj, cw):
    L = proj.shape[0]

    def body(x_ref, cw_ref, o_ref):
        j = pl.program_id(0)
        scale = jnp.where(j < STEPS_PER_GROUP, HEAD_DIM ** -0.5, 1.0).astype(f32)
        for ls in QKV_HEADS:
            c = _conv4(x_ref[:, ls], cw_ref, ls)
            a = c * _sigmoid(c)
            rn = lax.rsqrt(jnp.sum(a * a, axis=1, keepdims=True) + EPS)
            o_ref[:, ls] = jnp.where(j < 2 * STEPS_PER_GROUP, (a * rn) * scale, a)

    return _pcall(
        body, name="qkv_act", grid=(3 * STEPS_PER_GROUP,),
        in_specs=[pl.BlockSpec((L, QKV_W), lambda j: (0, j)), pl.BlockSpec((4, QKV_W), lambda j: (0, j))],
        out_specs=pl.BlockSpec((L, QKV_W), lambda j: (0, j)),
        out_shape=jax.ShapeDtypeStruct((L, 3 * GDN_WIDTH), f32),
        compiler_params=_cparams("parallel"),
    )(proj, cw)


def _scalars(proj, alog_p, dtb_p):
    L = proj.shape[0]
    nc = L // CHUNK

    def body(x_ref, al_ref, dt_ref, sc_ref, gr_ref):
        x = x_ref[...]
        lane = _lanes(x.shape)
        beta = _sigmoid(x)
        g = -jnp.exp(al_ref[...]) * _softplus(x + dt_ref[...])
        gc = jnp.where((lane >= HEADS) & (lane < 2 * HEADS), g, 0.0)
        rc = _rows(x.shape) & (CHUNK - 1)
        for s in (1, 2, 4, 8, 16, 32):
            gc = gc + jnp.where(rc >= s, pltpu.roll(gc, s, 0), 0.0)
        sc_ref[...] = jnp.where(lane < HEADS, beta, gc)
        sel = (_lanes((HEADS, LANE)) == _rows((HEADS, LANE)) + HEADS).astype(f32)
        for c in range(nc):
            gr_ref[c] = lax.dot_general(sel, sc_ref[c * CHUNK:(c + 1) * CHUNK, :], (((1,), (1,)), ((), ())),
                                        preferred_element_type=f32, precision=lax.Precision.HIGHEST)

    return _pcall(
        body, name="scalars", grid=(1,),
        in_specs=[pl.BlockSpec((L, LANE), lambda i: (0, OFF_BA // LANE)), pl.BlockSpec((1, LANE), lambda i: (0, 0)),
                  pl.BlockSpec((1, LANE), lambda i: (0, 0))],
        out_specs=[pl.BlockSpec((L, LANE), lambda i: (0, 0)), pl.BlockSpec((nc, HEADS, CHUNK), lambda i: (0, 0, 0))],
        out_shape=[jax.ShapeDtypeStruct((L, LANE), f32), jax.ShapeDtypeStruct((nc, HEADS, CHUNK), f32)],
        compiler_params=_cparams("arbitrary"),
    )(proj, alog_p, dtb_p)


def _head_scalars(sc, gr_ref, h, ci=0):
    lane = _lanes(sc.shape)
    beta = jnp.sum(jnp.where(lane == h, sc, 0.0), axis=1, keepdims=True)
    gcc = jnp.sum(jnp.where(lane == HEADS + h, sc, 0.0), axis=1, keepdims=True)
    gcr = gr_ref[ci, h:h + 1, :]
    gl = jnp.sum(jnp.where(_lanes(gcr.shape) == CHUNK - 1, gcr, 0.0), axis=1, keepdims=True)
    ii, jj = _rows((CHUNK, CHUNK)), _lanes((CHUNK, CHUNK))
    dmat = jnp.where(ii >= jj, jnp.exp(jnp.minimum(gcc - gcr, 0.0)), 0.0)
    dmat_t = jnp.where(jj >= ii, jnp.exp(jnp.minimum(gcr - gcc, 0.0)), 0.0)
    return beta, gcc, gl, dmat, dmat_t, ii, jj


def _gdn_fwd(qkv, sc, gr):
    L = qkv.shape[0]
    nc = L // CHUNK
    W = GDN_WIDTH
    cps = GDN_CPS if nc % GDN_CPS == 0 else 1
    rows_per_step = cps * CHUNK

    def body(qkv_ref, sc_ref, gr_ref, o_ref, u_ref, w_ref, vn_ref, t_ref, sp_ref, s_scr):
        @pl.when(pl.program_id(0) == 0)
        def _():
            s_scr[...] = jnp.zeros_like(s_scr)
        HS = range(cps * HEADS)
        hd = [i % HEADS for i in HS]
        rs = [slice((i // HEADS) * CHUNK, (i // HEADS + 1) * CHUNK) for i in HS]
        cs = [slice(hd[i] * HEAD_DIM, (hd[i] + 1) * HEAD_DIM) for i in HS]
        q = [qkv_ref[rs[i], hd[i] * HEAD_DIM:(hd[i] + 1) * HEAD_DIM] for i in HS]
        k = [qkv_ref[rs[i], W + hd[i] * HEAD_DIM:W + (hd[i] + 1) * HEAD_DIM] for i in HS]
        v = [qkv_ref[rs[i], 2 * W + hd[i] * HEAD_DIM:2 * W + (hd[i] + 1) * HEAD_DIM] for i in HS]
        hsc = [_head_scalars(sc_ref[rs[i], :], gr_ref, hd[i], i // HEADS) for i in HS]
        beta, gcc, gl, dmat = ([x[i] for x in hsc] for i in range(4))
        ii, jj = hsc[0][5], hsc[0][6]
        eg = [jnp.exp(gcc[h]) for h in HS]
        kb = [k[h] * beta[h] for h in HS]
        kk = [_mm_nt(kb[h], k[h]) for h in HS]
        qk = [_mm_nt(q[h], k[h]) for h in HS]
        n0 = [-jnp.where(ii > jj, kk[h] * dmat[h], 0.0) for h in HS]
        n1 = [_mm(n0[h], n0[h]) for h in HS]
        n2 = [_mm(n1[h], n1[h]) for h in HS]
        p01 = [n0[h] + n1[h] + _mm(n0[h], n1[h]) for h in HS]
        n3 = [_mm(n2[h], n2[h]) for h in HS]
        n4 = [_mm(n3[h], n3[h]) for h in HS]
        p23 = [n2[h] + n3[h] + _mm(n2[h], n3[h]) for h in HS]
        n5 = [_mm(n4[h], n4[h]) for h in HS]
        p03 = [p01[h] + p23[h] + _mm(p01[h], p23[h]) for h in HS]
        p45 = [n4[h] + n5[h] + _mm(n4[h], n5[h]) for h in HS]
        t = [p03[h] + p45[h] + _mm(p03[h], p45[h]) for h in HS]
        vb = [v[h] * beta[h] for h in HS]
        kbg = [kb[h] * eg[h] for h in HS]
        uw = [_mm(t[h], _cat16([vb[h], kbg[h]], 1)) for h in HS]
        u = [vb[h] + uw[h][:, :HEAD_DIM] for h in HS]
        w = [kbg[h] + uw[h][:, HEAD_DIM:] for h in HS]
        wq = [_cat16([w[h], q[h] * eg[h]], 0) for h in HS]
        p = [jnp.where(ii >= jj, qk[h] * dmat[h], 0.0) for h in HS]
        ks = [k[h] * jnp.exp(gl[h] - gcc[h]) for h in HS]
        s = [s_scr[h] for h in range(HEADS)]
        for ci in range(cps):
            IS = range(ci * HEADS, (ci + 1) * HEADS)
            ws = [_mm(wq[i], s[hd[i]]) for i in IS]
            vn = [u[i] - ws[hd[i]][:CHUNK] for i in IS]
            pv = [_mm(p[i], vn[hd[i]]) for i in IS]
            kv = [_mm_tn(ks[i], vn[hd[i]]) for i in IS]
            for i in IS:
                h = hd[i]
                sp_ref[ci, cs[i], :] = s[h]
                o_ref[rs[i], cs[i]] = ws[h][CHUNK:] + pv[h]
                vn_ref[rs[i], cs[i]] = vn[h].astype(bf16)
            s = [jnp.exp(gl[i]) * s[hd[i]] + kv[hd[i]] for i in IS]
        for h in range(HEADS):
            s_scr[h] = s[h]
        for i in HS:
            u_ref[rs[i], cs[i]] = u[i].astype(bf16)
            w_ref[rs[i], cs[i]] = w[i].astype(bf16)
            t_ref[i // HEADS, hd[i]] = t[i].astype(bf16)

    row = lambda c: (c, 0)
    act, act16 = jax.ShapeDtypeStruct((L, W), f32), jax.ShapeDtypeStruct((L, W), bf16)
    return _pcall(
        body, name="gdn_fwd", grid=(nc // cps,),
        in_specs=[pl.BlockSpec((rows_per_step, 3 * W), row), pl.BlockSpec((rows_per_step, LANE), row),
                  pl.BlockSpec((cps, HEADS, CHUNK), lambda c: (c, 0, 0))],
        out_specs=[pl.BlockSpec((rows_per_step, W), row)] * 4 + [
            pl.BlockSpec((cps, HEADS, CHUNK, CHUNK), lambda c: (c, 0, 0, 0)),
            pl.BlockSpec((cps, W, HEAD_DIM), lambda c: (c, 0, 0))],
        out_shape=[act, act16, act16, act16, jax.ShapeDtypeStruct((nc, HEADS, CHUNK, CHUNK), bf16),
                   jax.ShapeDtypeStruct((nc, W, HEAD_DIM), f32)],
        scratch_shapes=[pltpu.VMEM((HEADS, HEAD_DIM, HEAD_DIM), f32)],
        compiler_params=_cparams("arbitrary"),
    )(qkv, sc, gr)


def _gdn_gate(o, proj, gnw):
    L = o.shape[0]

    def body(o_ref, z_ref, w_ref, m_ref):
        for ls in HALVES:
            ov, z = o_ref[:, ls], z_ref[:, ls]
            rms = lax.rsqrt(jnp.mean(ov * ov, axis=-1, keepdims=True) + EPS)
            m_ref[:, ls] = (((ov * rms) * w_ref[...]) * (z * _sigmoid(z))).astype(bf16)

    return _pcall(
        body, name="gdn_gate", grid=(GDN_WIDTH // ELT_W,),
        in_specs=[pl.BlockSpec((L, ELT_W), lambda j: (0, j)), pl.BlockSpec((L, ELT_W), lambda j: (0, OFF_ZG // ELT_W + j)),
                  pl.BlockSpec((1, LANE), lambda j: (0, 0))],
        out_specs=pl.BlockSpec((L, ELT_W), lambda j: (0, j)),
        out_shape=jax.ShapeDtypeStruct((L, GDN_WIDTH + CONV_WIDTH), bf16),
        compiler_params=_cparams("parallel"),
    )(o, proj, gnw)


def _conv3(u, cw_ref, ls):
    return cw_ref[2:3, ls] * u + cw_ref[1:2, ls] * _shift_down(u, 1) + cw_ref[0:1, ls] * _shift_down(u, 2)


def _conv_specs(L):
    return [pl.BlockSpec((L, CONV_BLOCK), lambda j: (0, OFF_CONV // CONV_BLOCK + j)),
            pl.BlockSpec((3, ELT_W), lambda j: (0, j)), pl.BlockSpec((1, ELT_W), lambda j: (0, j))]


def _conv_parts(ls):
    return [slice(g * ELT_W + ls.start, g * ELT_W + ls.stop) for g in range(4)]


def _conv_fwd(proj, cw, cb, mix):
    L = proj.shape[0]

    def body(p_ref, cw_ref, cb_ref, mix_in, m_ref):
        for ls in HALVES:
            sb, sc_, sh, sz = _conv_parts(ls)
            z = p_ref[:, sz]
            cv = _conv3(p_ref[:, sc_] * p_ref[:, sh], cw_ref, ls) + cb_ref[:, ls]
            m_ref[:, ls] = ((p_ref[:, sb] * cv) * (z * _sigmoid(z))).astype(bf16)

    return _pcall(
        body, name="conv_fwd", grid=(CONV_WIDTH // ELT_W,),
        in_specs=_conv_specs(L) + [ANY], out_specs=pl.BlockSpec((L, ELT_W), lambda j: (0, GDN_WIDTH // ELT_W + j)),
        out_shape=jax.ShapeDtypeStruct(mix.shape, mix.dtype), input_output_aliases={3: 0},
        compiler_params=_cparams("parallel"),
    )(proj, cw, cb, mix)


def _out_proj_loss(x, mix, wo, fw, tgt):
    L = x.shape[0]
    tm = min(512, L)
    MW = GDN_WIDTH + CONV_WIDTH

    cuts = (0, 256, 512, 1024, MW)
    nk = len(cuts) - 1

    def body(x_ref, m_ref, wo_hbm, fw_ref, t_ref, dy_ref, dyb_ref, dm_ref, gfw_ref, loss_ref, wo_ref, sems):
        first = pl.program_id(0) == 0
        loads = [pltpu.make_async_copy(wo_hbm.at[cuts[k]:cuts[k + 1], :], wo_ref.at[cuts[k]:cuts[k + 1], :], sems.at[k])
                 for k in range(nk)]

        @pl.when(first)
        def _():
            for cp in loads:
                cp.start()
            gfw_ref[...] = jnp.zeros_like(gfw_ref)
            loss_ref[...] = jnp.zeros_like(loss_ref)
        y = x_ref[...]
        for k in range(nk):
            pl.when(first)(loads[k].wait)
            y = y + jnp.dot(m_ref[:, cuts[k]:cuts[k + 1]], wo_ref[cuts[k]:cuts[k + 1], :], preferred_element_type=f32)
        r = lax.rsqrt(jnp.mean(y * y, axis=-1, keepdims=True) + EPS)
        yh = y * r
        fwv = fw_ref[...]
        diff = yh * fwv - t_ref[...]
        loss_ref[...] += jnp.sum(jnp.sum(diff * diff, axis=-1, keepdims=True), axis=0, keepdims=True) * (0.5 / D_MODEL)
        dout = diff * (1.0 / D_MODEL)
        gfw_ref[...] += jnp.sum(dout * yh, axis=0, keepdims=True)
        dyh = dout * fwv
        dy = r * (dyh - yh * jnp.mean(dyh * yh, axis=-1, keepdims=True))
        dy_ref[...] = dy
        dyb = dy.astype(bf16)
        dyb_ref[...] = dyb
        dm_ref[...] = lax.dot_general(dyb, wo_ref[...], (((1,), (1,)), ((), ())), preferred_element_type=f32)

    row = lambda i: (i, 0)
    fix = lambda i: (0, 0)
    act = jax.ShapeDtypeStruct((L, D_MODEL), f32)
    return _pcall(
        body, name="out_proj_loss", grid=(L // tm,),
        in_specs=[pl.BlockSpec((tm, D_MODEL), row), pl.BlockSpec((tm, MW), row), ANY,
                  pl.BlockSpec((1, D_MODEL), fix), pl.BlockSpec((tm, D_MODEL), row)],
        out_specs=[pl.BlockSpec((tm, D_MODEL), row), pl.BlockSpec((tm, D_MODEL), row), pl.BlockSpec((tm, MW), row),
                   pl.BlockSpec((1, D_MODEL), fix), pl.BlockSpec((1, LANE), fix)],
        out_shape=[act, jax.ShapeDtypeStruct((L, D_MODEL), bf16), jax.ShapeDtypeStruct((L, MW), f32),
                   jax.ShapeDtypeStruct((1, D_MODEL), f32), jax.ShapeDtypeStruct((1, LANE), f32)],
        scratch_shapes=[pltpu.VMEM((MW, D_MODEL), bf16), pltpu.SemaphoreType.DMA((nk,))],
        compiler_params=_cparams("arbitrary"),
    )(x, mix, wo, fw, tgt)


def _tn_matmul(a, b, name):
    L, M = a.shape
    N = b.shape[1]
    tm = 512 if M % 512 == 0 else (768 if M % 768 == 0 else M)

    def body(a_ref, b_ref, o_ref):
        o_ref[...] = lax.dot_general(a_ref[...], b_ref[...], (((0,), (0,)), ((), ())),
                                     preferred_element_type=f32).astype(o_ref.dtype)

    return _pcall(
        body, name=name, grid=(M // tm,),
        in_specs=[pl.BlockSpec((L, tm), lambda i: (0, i)), pl.BlockSpec((L, N), lambda i: (0, 0))],
        out_specs=pl.BlockSpec((tm, N), lambda i: (i, 0)),
        out_shape=jax.ShapeDtypeStruct((M, N), bf16),
        compiler_params=_cparams("parallel"),
    )(a, b)


def _gdn_gate_bwd(o, proj, gnw, dmix_a, after):
    L = o.shape[0]

    def body(o_ref, z_ref, w_ref, dm_ref, after_ref, do_ref, dz_ref, gw_ref):
        @pl.when(pl.program_id(0) == 0)
        def _():
            gw_ref[...] = jnp.zeros_like(gw_ref)
        wv = w_ref[...]
        for ls in HALVES:
            ov, z, dm = o_ref[:, ls], z_ref[:, ls], dm_ref[:, ls]
            rms = lax.rsqrt(jnp.mean(ov * ov, axis=-1, keepdims=True) + EPS)
            xh = ov * rms
            sg = _sigmoid(z)
            d_on = dm * (z * sg)
            dz_ref[:, ls] = (dm * (xh * wv) * (sg * (1.0 + z * (1.0 - sg)))).astype(bf16)
            gw_ref[...] += jnp.sum(d_on * xh, axis=0, keepdims=True)
            dxh = d_on * wv
            do_ref[:, ls] = (rms * (dxh - xh * jnp.mean(dxh * xh, axis=-1, keepdims=True))).astype(bf16)

    wide = pl.BlockSpec((L, ELT_W), lambda j: (0, j))
    return _pcall(
        body, name="gdn_gate_bwd", grid=(GDN_WIDTH // ELT_W,),
        in_specs=[wide, pl.BlockSpec((L, ELT_W), lambda j: (0, OFF_ZG // ELT_W + j)),
                  pl.BlockSpec((1, LANE), lambda j: (0, 0)), wide, ANY],
        out_specs=[wide, pl.BlockSpec((L, ELT_W), lambda j: (0, OFF_ZG // ELT_W + j)),
                   pl.BlockSpec((1, LANE), lambda j: (0, 0))],
        out_shape=[jax.ShapeDtypeStruct((L, GDN_WIDTH), bf16), jax.ShapeDtypeStruct((L, PROJ_PAD), bf16),
                   jax.ShapeDtypeStruct((1, LANE), f32)],
        compiler_params=_cparams("arbitrary"),
    )(o, proj, gnw, dmix_a, after)


def _conv_bwd(proj, cw, cb, dmix_b, dproj):
    L = proj.shape[0]

    def body(p_ref, cw_ref, cb_ref, dm_ref, dproj_in, dp_ref, gcw_ref, gcb_ref):
        for ls in HALVES:
            sb, sc_, sh, sz_ = _conv_parts(ls)
            bv, cv_, hv, z, dm = p_ref[:, sb], p_ref[:, sc_], p_ref[:, sh], p_ref[:, sz_], dm_ref[:, ls]
            u = cv_ * hv
            cv = _conv3(u, cw_ref, ls) + cb_ref[:, ls]
            sg = _sigmoid(z)
            sz = z * sg
            dp_ref[:, sb] = (dm * cv * sz).astype(bf16)
            dp_ref[:, sz_] = (dm * (bv * cv) * (sg * (1.0 + z * (1.0 - sg)))).astype(bf16)
            dcv = dm * bv * sz
            gcb_ref[:, ls] = jnp.sum(dcv, axis=0, keepdims=True)
            dcv1, dcv2 = _shift_up(dcv, 1), _shift_up(dcv, 2)
            gcw_ref[2:3, ls] = jnp.sum(dcv * u, axis=0, keepdims=True)
            gcw_ref[1:2, ls] = jnp.sum(dcv1 * u, axis=0, keepdims=True)
            gcw_ref[0:1, ls] = jnp.sum(dcv2 * u, axis=0, keepdims=True)
            du = cw_ref[2:3, ls] * dcv + cw_ref[1:2, ls] * dcv1 + cw_ref[0:1, ls] * dcv2
            dp_ref[:, sc_] = (du * hv).astype(bf16)
            dp_ref[:, sh] = (du * cv_).astype(bf16)

    return _pcall(
        body, name="conv_bwd", grid=(CONV_WIDTH // ELT_W,),
        in_specs=_conv_specs(L) + [pl.BlockSpec((L, ELT_W), lambda j: (0, GDN_WIDTH // ELT_W + j)), ANY],
        out_specs=[pl.BlockSpec((L, CONV_BLOCK), lambda j: (0, OFF_CONV // CONV_BLOCK + j)),
                   pl.BlockSpec((3, ELT_W), lambda j: (0, j)), pl.BlockSpec((1, ELT_W), lambda j: (0, j))],
        out_shape=[jax.ShapeDtypeStruct(dproj.shape, dproj.dtype), jax.ShapeDtypeStruct((3, CONV_WIDTH), f32),
                   jax.ShapeDtypeStruct((1, CONV_WIDTH), f32)],
        input_output_aliases={4: 0},
        compiler_params=_cparams("parallel"),
    )(proj, cw, cb, dmix_b, dproj)


def _gdn_bwd(qkv, sc, gr, u_all, w_all, vn_all, t_all, sp_all, do_all):
    L = qkv.shape[0]
    nc = L // CHUNK
    W = GDN_WIDTH
    cps = GDN_CPS_BWD if nc % GDN_CPS_BWD == 0 else 1
    rows_per_step = cps * CHUNK
    nsteps = nc // cps

    def body(qkv_ref, sc_ref, gr_ref, u_ref, w_ref, vn_ref, t_ref, sp_ref, do_ref, dqkv_ref, dsc_ref, dgr_ref, ds_scr):
        @pl.when(pl.program_id(0) == 0)
        def _():
            ds_scr[...] = jnp.zeros_like(ds_scr)
        nh, base = HEADS, 0
        HS = range(cps * nh)
        hl = [i % nh for i in HS]
        hd = [base + hl[i] for i in HS]
        rs = [slice((i // nh) * CHUNK, (i // nh + 1) * CHUNK) for i in HS]
        cs = [slice(hd[i] * HEAD_DIM, (hd[i] + 1) * HEAD_DIM) for i in HS]
        q = [qkv_ref[rs[i], hd[i] * HEAD_DIM:(hd[i] + 1) * HEAD_DIM] for i in HS]
        k = [qkv_ref[rs[i], W + hd[i] * HEAD_DIM:W + (hd[i] + 1) * HEAD_DIM] for i in HS]
        v = [qkv_ref[rs[i], 2 * W + hd[i] * HEAD_DIM:2 * W + (hd[i] + 1) * HEAD_DIM] for i in HS]
        hsc = [_head_scalars(sc_ref[rs[i], :], gr_ref, hd[i], i // nh) for i in HS]
        beta, gcc, gl, dmat, dmat_t = ([x[i] for x in hsc] for i in range(5))
        ii, jj = hsc[0][5], hsc[0][6]
        eg = [jnp.exp(gcc[h]) for h in HS]
        ekl = [jnp.exp(gl[h] - gcc[h]) for h in HS]
        egl = [jnp.exp(gl[h]) for h in HS]
        kb = [k[h] * beta[h] for h in HS]
        ks = [k[h] * ekl[h] for h in HS]
        do = [do_ref[rs[h], cs[h]] for h in HS]
        vn = [vn_ref[rs[h], cs[h]] for h in HS]
        s = [sp_ref[h // nh, cs[h], :] for h in HS]
        w = [w_ref[rs[h], cs[h]] for h in HS]
        qd = [q[h] * eg[h] for h in HS]

        kq = [_mm_nt(k[h], q[h]) for h in HS]
        p_t = [jnp.where(jj >= ii, kq[h] * dmat_t[h], 0.0) for h in HS]
        ptd = [_mm(p_t[h], do[h]) for h in HS]
        qw = [_cat16([qd[h], -w[h]], 0) for h in HS]
        dsn, dvn, dodv = [None] * len(HS), [None] * len(HS), [None] * len(HS)
        ds_cur = [ds_scr[base + h] for h in range(nh)]
        for ci in reversed(range(cps)):
            IS = range(ci * nh, (ci + 1) * nh)
            ksd = [_mm(ks[i], ds_cur[hl[i]]) for i in IS]
            for i in IS:
                dsn[i] = ds_cur[hl[i]]
                dvn[i] = ptd[i] + ksd[hl[i]]
                dodv[i] = _cat16([do[i], dvn[i]], 0)
            dsq = [_mm_tn(qw[i], dodv[i]) for i in IS]
            ds_cur = [egl[i] * ds_cur[hl[i]] + dsq[hl[i]] for i in IS]
        for h in range(nh):
            ds_scr[base + h] = ds_cur[h]
        x1 = [_mm_nt(dodv[h], s[h]) for h in HS]
        dks = [_mm_nt(vn[h], dsn[h]) for h in HS]
        dov = [_mm_nt(do[h], vn[h]) for h in HS]
        vdo = [_mm_nt(vn[h], do[h]) for h in HS]
        kk = [_mm_nt(kb[h], k[h]) for h in HS]
        qk = [_mm_nt(q[h], k[h]) for h in HS]
        dgl = [egl[h] * jnp.sum(jnp.sum(s[h] * dsn[h], axis=1, keepdims=True), axis=0, keepdims=True) for h in HS]
        dqd = [x1[h][:CHUNK] for h in HS]
        duw = [jnp.concatenate([dvn[h], -x1[h][CHUNK:]], axis=1) for h in HS]
        tdu = [_mm_tn(t_ref[h // nh, hd[h]], duw[h]) for h in HS]
        dvk = [duw[h] + tdu[h] for h in HS]
        uw = [jnp.concatenate([u_ref[rs[h], cs[h]], w[h]], axis=1) for h in HS]
        da = [-jnp.where(ii > jj, _mm_nt(dvk[h], uw[h]), 0.0) for h in HS]
        da_t = [-jnp.where(jj > ii, _mm_nt(uw[h], dvk[h]), 0.0) for h in HS]
        dp = [jnp.where(ii >= jj, dov[h], 0.0) for h in HS]
        dp_t = [jnp.where(jj >= ii, vdo[h], 0.0) for h in HS]
        r1 = [_mm(_cat16([da[h] * dmat[h], dp[h] * dmat[h]], 0), k[h]) for h in HS]
        dk1 = [_mm(_cat16([da_t[h] * dmat_t[h], dp_t[h] * dmat_t[h]], 1), _cat16([kb[h], q[h]], 0)) for h in HS]
        lane = _lanes((CHUNK, LANE))
        for ci in range(cps):
            dsc = jnp.zeros((CHUNK, LANE), f32)
            for i in range(ci * nh, (ci + 1) * nh):
                h = hd[i]
                a = jnp.where(ii > jj, kk[i] * dmat[i], 0.0)
                p = jnp.where(ii >= jj, qk[i] * dmat[i], 0.0)
                gmat = da[i] * a + dp[i] * p
                dvb, dkbg = dvk[i][:, :HEAD_DIM], dvk[i][:, HEAD_DIM:]
                kbg = kb[i] * eg[i]
                dkb = r1[i][:CHUNK] + dkbg * eg[i]
                dq = r1[i][CHUNK:] + dqd[i] * eg[i]
                dk = dk1[i] + dks[i] * ekl[i] + dkb * beta[i]
                dbeta = jnp.sum(dkb * k[i] + dvb * v[i], axis=1, keepdims=True)
                ksum = jnp.sum(dks[i] * ks[i], axis=1, keepdims=True)
                dgl_tot = dgl[i] + jnp.sum(ksum, axis=0, keepdims=True)
                dgc = (jnp.sum(gmat, axis=1, keepdims=True) + jnp.sum(dqd[i] * qd[i] + dkbg * kbg, axis=1, keepdims=True)
                       - ksum)
                dgc = dgc + jnp.where(_rows(dgc.shape) == CHUNK - 1, dgl_tot, 0.0)
                dqkv_ref[rs[i], h * HEAD_DIM:(h + 1) * HEAD_DIM] = dq
                dqkv_ref[rs[i], W + h * HEAD_DIM:W + (h + 1) * HEAD_DIM] = dk
                dqkv_ref[rs[i], 2 * W + h * HEAD_DIM:2 * W + (h + 1) * HEAD_DIM] = dvb * beta[i]
                dsc = jnp.where(lane == h, dbeta, jnp.where(lane == HEADS + h, dgc, dsc))
                dgr_ref[ci, h:h + 1, :] = jnp.sum(gmat, axis=0, keepdims=True)
            dsc_ref[ci * CHUNK:(ci + 1) * CHUNK, :] = dsc

    row = lambda c: (nsteps - 1 - c, 0)
    lead3 = lambda c: (nsteps - 1 - c, 0, 0)
    return _pcall(
        body, name="gdn_bwd", grid=(nsteps,),
        in_specs=[pl.BlockSpec((rows_per_step, 3 * W), row), pl.BlockSpec((rows_per_step, LANE), row),
                  pl.BlockSpec((cps, HEADS, CHUNK), lead3),
                  pl.BlockSpec((rows_per_step, W), row), pl.BlockSpec((rows_per_step, W), row),
                  pl.BlockSpec((rows_per_step, W), row),
                  pl.BlockSpec((cps, HEADS, CHUNK, CHUNK), lambda c: (nsteps - 1 - c, 0, 0, 0)),
                  pl.BlockSpec((cps, W, HEAD_DIM), lead3), pl.BlockSpec((rows_per_step, W), row)],
        out_specs=[pl.BlockSpec((rows_per_step, 3 * W), row), pl.BlockSpec((rows_per_step, LANE), row),
                   pl.BlockSpec((cps, HEADS, CHUNK), lead3)],
        out_shape=[jax.ShapeDtypeStruct((L, 3 * W), f32), jax.ShapeDtypeStruct((L, LANE), f32),
                   jax.ShapeDtypeStruct((nc, HEADS, CHUNK), f32)],
        scratch_shapes=[pltpu.VMEM((HEADS, HEAD_DIM, HEAD_DIM), f32)],
        compiler_params=_cparams("arbitrary"),
    )(qkv, sc, gr, u_all, w_all, vn_all, t_all, sp_all, do_all)


def _qkv_bwd(proj, cw, dn, dproj):
    L = proj.shape[0]

    def body(x_ref, cw_ref, dn_ref, dproj_in, dx_ref, gcw_ref):
        j = pl.program_id(0)
        steps = GDN_WIDTH // ELT_W
        scale = jnp.where(j < steps, HEAD_DIM ** -0.5, 1.0).astype(f32)
        for ls in HALVES:
            x, dn_v = x_ref[:, ls], dn_ref[:, ls]
            c = _conv4(x, cw_ref, ls)
            sg = _sigmoid(c)
            a = c * sg
            rn = lax.rsqrt(jnp.sum(a * a, axis=1, keepdims=True) + EPS)
            da_n = (scale * rn) * (dn_v - a * ((rn * rn) * jnp.sum(dn_v * a, axis=1, keepdims=True)))
            da = jnp.where(j < 2 * steps, da_n, dn_v)
            dc = da * (sg * (1.0 + c * (1.0 - sg)))
            dc1, dc2, dc3 = _shift_up(dc, 1), _shift_up(dc, 2), _shift_up(dc, 3)
            gcw_ref[3:4, ls] = jnp.sum(dc * x, axis=0, keepdims=True)
            gcw_ref[2:3, ls] = jnp.sum(dc1 * x, axis=0, keepdims=True)
            gcw_ref[1:2, ls] = jnp.sum(dc2 * x, axis=0, keepdims=True)
            gcw_ref[0:1, ls] = jnp.sum(dc3 * x, axis=0, keepdims=True)
            dx = cw_ref[3:4, ls] * dc + cw_ref[2:3, ls] * dc1 + cw_ref[1:2, ls] * dc2 + cw_ref[0:1, ls] * dc3
            dx_ref[:, ls] = dx.astype(bf16)

    col = pl.BlockSpec((L, ELT_W), lambda j: (0, j))
    wspec = pl.BlockSpec((4, ELT_W), lambda j: (0, j))
    return _pcall(
        body, name="qkv_bwd", grid=(3 * GDN_WIDTH // ELT_W,),
        in_specs=[col, wspec, col, ANY], out_specs=[col, wspec],
        out_shape=[jax.ShapeDtypeStruct(dproj.shape, dproj.dtype), jax.ShapeDtypeStruct((4, 3 * GDN_WIDTH), f32)],
        input_output_aliases={3: 0},
        compiler_params=_cparams("parallel"),
    )(proj, cw, dn, dproj)


def _scalars_bwd(proj, alog_p, dtb_p, dsc, dgr_col, dproj, after):
    L = proj.shape[0]

    def body(x_ref, al_ref, dt_ref, dsc_ref, dgr_ref, dproj_in, after_ref, dba_ref, gs_ref):
        x, dsc_v = x_ref[...], dsc_ref[...]
        lane = _lanes(x.shape)
        dec = (lane >= HEADS) & (lane < 2 * HEADS)
        dg = jnp.where(dec, dsc_v - dgr_ref[...], 0.0)
        rc = _rows(x.shape) & (CHUNK - 1)
        for s in (1, 2, 4, 8, 16, 32):
            dg = dg + jnp.where(rc + s < CHUNK, pltpu.roll(dg, L - s, 0), 0.0)
        xa = x + dt_ref[...]
        ea = jnp.exp(al_ref[...])
        g = -ea * _softplus(xa)
        da = dg * (-ea) * _sigmoid(xa)
        beta = _sigmoid(x)
        db = dsc_v * beta * (1.0 - beta)
        dba_ref[:, :LANE] = jnp.where(lane < HEADS, db, jnp.where(dec, da, 0.0)).astype(bf16)
        dba_ref[:, LANE:] = jnp.zeros((L, ELT_W - LANE), bf16)
        g_al = jnp.sum(jnp.where(dec, dg * g, 0.0), axis=0, keepdims=True)
        g_dt = jnp.sum(jnp.where(dec, da, 0.0), axis=0, keepdims=True)
        row8 = _rows(gs_ref.shape)
        gs = jnp.where(row8 == 0, g_al, jnp.where(row8 == 1, g_dt, 0.0))
        gs_ref[...] = pltpu.roll(gs, LANE - HEADS, 1)

    full = pl.BlockSpec((L, LANE), lambda i: (0, 0))
    vec = pl.BlockSpec((1, LANE), lambda i: (0, 0))
    return _pcall(
        body, name="scalars_bwd", grid=(1,),
        in_specs=[pl.BlockSpec((L, LANE), lambda i: (0, OFF_BA // LANE)), vec, vec, full, full, ANY, ANY],
        out_specs=[pl.BlockSpec((L, ELT_W), lambda i: (0, OFF_BA // ELT_W)), pl.BlockSpec((8, LANE), lambda i: (0, 0))],
        out_shape=[jax.ShapeDtypeStruct(dproj.shape, dproj.dtype), jax.ShapeDtypeStruct((8, LANE), f32)],
        input_output_aliases={5: 0},
        compiler_params=_cparams("arbitrary"),
    )(proj, alog_p, dtb_p, dsc, dgr_col, dproj, after)


def _input_grad(dproj, wpad, x, nw, dy, after):
    L = x.shape[0]
    tm = min(512, L)
    cuts = (0, 512, 1024, 2048, 3072, 5120, 7168, PROJ_PAD)
    nk = len(cuts) - 1

    def body(dp_ref, w_hbm, x_ref, nw_ref, dy_ref, after_ref, gx_ref, gnw_ref, w_vmem, sems):
        first = pl.program_id(0) == 0
        loads = [pltpu.make_async_copy(w_hbm.at[cuts[k]:cuts[k + 1], :], w_vmem.at[cuts[k]:cuts[k + 1], :], sems.at[k])
                 for k in range(nk)]

        @pl.when(first)
        def _():
            for cp in loads:
                cp.start()
            gnw_ref[...] = jnp.zeros_like(gnw_ref)
        dh = None
        for k in range(nk):
            pl.when(first)(loads[k].wait)
            part = jnp.dot(dp_ref[:, cuts[k]:cuts[k + 1]], w_vmem[cuts[k]:cuts[k + 1], :], preferred_element_type=f32)
            dh = part if dh is None else dh + part
        xv, nwv = x_ref[...], nw_ref[...]
        r = lax.rsqrt(jnp.mean(xv * xv, axis=-1, keepdims=True) + EPS)
        xh = xv * r
        gnw_ref[...] += jnp.sum(dh * xh, axis=0, keepdims=True)
        dxh = dh * nwv
        gx_ref[...] = dy_ref[...] + r * (dxh - xh * jnp.mean(dxh * xh, axis=-1, keepdims=True))

    row = lambda i: (i, 0)
    fix = lambda i: (0, 0)
    return _pcall(
        body, name="input_grad", grid=(L // tm,),
        in_specs=[pl.BlockSpec((tm, PROJ_PAD), row), ANY, pl.BlockSpec((tm, D_MODEL), row),
                  pl.BlockSpec((1, D_MODEL), fix), pl.BlockSpec((tm, D_MODEL), row), ANY],
        out_specs=[pl.BlockSpec((tm, D_MODEL), row), pl.BlockSpec((1, D_MODEL), fix)],
        out_shape=[jax.ShapeDtypeStruct((L, D_MODEL), f32), jax.ShapeDtypeStruct((1, D_MODEL), f32)],
        scratch_shapes=[pltpu.VMEM(wpad.shape, bf16), pltpu.SemaphoreType.DMA((nk,))],
        compiler_params=_cparams("arbitrary"),
    )(dproj, wpad, x, nw, dy, after)


def _adamw_reduce(parts, w, m, v, name, first_row=None):
    R, C = w.shape[0], w.shape[-1]
    n_parts = parts.shape[0]
    tr = 128 if R % 128 == 0 else R
    c1 = 1.0 - ADAM_B1 ** ADAM_STEP
    c2 = 1.0 - ADAM_B2 ** ADAM_STEP
    at = (slice(None), 0, slice(None)) if w.ndim == 3 else Ellipsis
    window = (slice(None), slice(None)) if first_row is None else (slice(first_row, first_row + R), slice(0, C))

    def body(p_ref, w_ref, m_ref, v_ref, g_ref, d_ref, nm_ref, nv_ref):
        g = p_ref[(0,) + window].astype(f32)
        for s in range(1, n_parts):
            g = g + p_ref[(s,) + window].astype(f32)
        nm = ADAM_B1 * m_ref[at] + (1.0 - ADAM_B1) * g
        nv = ADAM_B2 * v_ref[at] + (1.0 - ADAM_B2) * (g * g)
        g_ref[at] = g
        nm_ref[at] = nm
        nv_ref[at] = nv
        d_ref[at] = -ADAM_LR * ((nm / c1) / (jnp.sqrt(nv / c2) + ADAM_EPS) + ADAM_WD * w_ref[at])

    blk = pl.BlockSpec((tr, 1, C), lambda i: (i, 0, 0)) if w.ndim == 3 else pl.BlockSpec((tr, C), lambda i: (i, 0))
    out = jax.ShapeDtypeStruct(w.shape, f32)
    if first_row is None:
        p_spec = pl.BlockSpec((n_parts, tr, C), lambda i: (0, i, 0))
    else:
        assert tr == R
        p_spec = pl.BlockSpec(parts.shape, lambda i: (0, 0, 0))
    return _pcall(
        body, name=name, grid=(R // tr,),
        in_specs=[p_spec, blk, blk, blk],
        out_specs=[blk] * 4, out_shape=[out] * 4,
        compiler_params=_cparams("parallel"),
    )(parts, w, m, v)


SMALL_SLOTS = ((0, D_MODEL), (D_MODEL, D_MODEL), (2 * D_MODEL, D_MODEL), (3 * D_MODEL, LANE),
               (3 * D_MODEL + LANE, HEADS), (3 * D_MODEL + 2 * LANE, HEADS))
SMALL_LOSS = 3 * D_MODEL + 3 * LANE
SMALL_W = SMALL_LOSS + LANE


def _pack_small(gs, after):
    def body(nw_ref, cb_ref, fw_ref, gn_ref, sc_ref, ls_ref, after_ref, o_ref):
        for ref, (start, width) in zip((nw_ref, cb_ref, fw_ref, gn_ref), SMALL_SLOTS[:4]):
            o_ref[:, start:start + width] = ref[...]
        o_ref[:, SMALL_SLOTS[4][0]:SMALL_SLOTS[4][0] + LANE] = sc_ref[0:1, :]
        o_ref[:, SMALL_SLOTS[5][0]:SMALL_SLOTS[5][0] + LANE] = sc_ref[1:2, :]
        o_ref[:, SMALL_LOSS:SMALL_W] = ls_ref[...]

    vm = pl.BlockSpec(memory_space=pltpu.VMEM)
    return _pcall(body, name="pack_small_grads", out_shape=jax.ShapeDtypeStruct((1, SMALL_W), f32),
                  in_specs=[vm] * 6 + [ANY], out_specs=vm)(*gs, after)


def _adamw_small(parts, ws, ms, vs):
    c1 = 1.0 - ADAM_B1 ** ADAM_STEP
    c2 = 1.0 - ADAM_B2 ** ADAM_STEP
    np_ = len(ws)

    def body(*refs):
        p_ref = refs[0]
        w_refs, m_refs, v_refs = refs[1:1 + np_], refs[1 + np_:1 + 2 * np_], refs[1 + 2 * np_:1 + 3 * np_]
        outs = refs[1 + 3 * np_:]
        g_refs, d_refs, nm_refs, nv_refs = (outs[i * np_:(i + 1) * np_] for i in range(4))
        loss_ref = outs[4 * np_]

        def total(start, width):
            t = p_ref[0, :, start:start + width]
            for s in range(1, N_DEV):
                t = t + p_ref[s, :, start:start + width]
            return t

        for i, (start, width) in enumerate(SMALL_SLOTS):
            g = total(start, width)
            nm = ADAM_B1 * m_refs[i][...] + (1.0 - ADAM_B1) * g
            nv = ADAM_B2 * v_refs[i][...] + (1.0 - ADAM_B2) * (g * g)
            g_refs[i][...] = g
            nm_refs[i][...] = nm
            nv_refs[i][...] = nv
            d_refs[i][...] = -ADAM_LR * ((nm / c1) / (jnp.sqrt(nv / c2) + ADAM_EPS) + ADAM_WD * w_refs[i][...])
        loss_ref[...] = total(SMALL_LOSS, LANE)

    vm = pl.BlockSpec(memory_space=pltpu.VMEM)
    shapes = [jax.ShapeDtypeStruct(w.shape, f32) for w in ws]
    res = _pcall(body, name="adamw_small", out_shape=shapes * 4 + [jax.ShapeDtypeStruct((1, LANE), f32)],
                 in_specs=[vm] * (1 + 3 * np_), out_specs=[vm] * (4 * np_ + 1))(parts, *ws, *ms, *vs)
    return [res[i * np_:(i + 1) * np_] for i in range(4)], res[4 * np_]


def _adamw_w_in(part_a, part_b, w3, m3, v3, after):
    _, n, _ = part_a.shape
    c1 = 1.0 - ADAM_B1 ** ADAM_STEP
    c2 = 1.0 - ADAM_B2 ** ADAM_STEP

    def body(pa_ref, pb_ref, w_ref, m_ref, v_ref, after_ref, g_ref, d_ref, nm_ref, nv_ref):
        g = pa_ref[0].astype(f32) + pb_ref[0].astype(f32)
        nm = ADAM_B1 * m_ref[:, 0, :] + (1.0 - ADAM_B1) * g
        nv = ADAM_B2 * v_ref[:, 0, :] + (1.0 - ADAM_B2) * (g * g)
        g_ref[:, 0, :] = g
        nm_ref[:, 0, :] = nm
        nv_ref[:, 0, :] = nv
        d_ref[:, 0, :] = -ADAM_LR * ((nm / c1) / (jnp.sqrt(nv / c2) + ADAM_EPS) + ADAM_WD * w_ref[:, 0, :])

    tile = 2 * COL_TILE
    blk = pl.BlockSpec((n, 1, tile), lambda j: (0, 0, j))
    out = jax.ShapeDtypeStruct((n, 1, D_MODEL), f32)
    return _pcall(
        body, name="adamw_w_in", grid=(D_MODEL // tile,),
        in_specs=[pl.BlockSpec((1, n, tile), lambda j: (0, 0, j))] * 2 + [blk, blk, blk, ANY],
        out_specs=[blk] * 4, out_shape=[out] * 4,
        compiler_params=_cparams("parallel"),
    )(part_a, part_b, w3, m3, v3, after)


def _pad_lanes(vec8, start):
    return jnp.pad(vec8.reshape(1, -1), ((0, 0), (start, LANE - start - vec8.size)))


def kernel(x, norm_in_w, w_in, conv_qkv_w, A_log, dt_bias, gdn_norm_w, conv_w, conv_b, w_out, final_norm_w, loss_target, m_norm_in_w, m_w_in, m_conv_qkv_w, m_A_log, m_dt_bias, m_gdn_norm_w, m_conv_w, m_conv_b, m_w_out, m_final_norm_w, v_norm_in_w, v_w_in, v_conv_qkv_w, v_A_log, v_dt_bias, v_gdn_norm_w, v_conv_w, v_conv_b, v_w_out, v_final_norm_w):
    L = x.shape[1]
    nc = L // CHUNK
    xs = x[0]
    tgt = loss_target[0]
    fnw = final_norm_w.reshape(1, D_MODEL)

    as_rows = lambda a: jnp.transpose(a, (2, 0, 1))
    as_taps = lambda a: jnp.transpose(a, (1, 0, 2))
    win_blk, wo_blk = _cast_weights(as_rows(w_in), w_out[0])
    win_g, cqkv_g, cw_g = _all_gather([win_blk, conv_qkv_w[0], as_taps(conv_w)], "gather_weights",
                                      pieces=[8, 1, 1])
    wpad = _relayout_w_in(win_g)
    cqkv = jnp.concatenate([cqkv_g[d] for d in range(N_DEV)], axis=1)
    cw = jnp.concatenate([cw_g[d][:, 0, :] for d in range(N_DEV)], axis=1)
    alog_p = _pad_lanes(A_log, HEADS)
    dtb_p = _pad_lanes(dt_bias, HEADS)
    tok = lambda started: started[4]
    wo_started = _spread_start(wo_blk, wpad, "gather", "gather_w_out_start")

    proj, h = _in_proj(xs, norm_in_w, wpad, tok(wo_started))
    qkv = _qkv_act(proj, cqkv)
    sc, gr = _scalars(proj, alog_p, dtb_p)
    o, u_all, w_all, vn_all, t_all, sp_all = _gdn_fwd(qkv, sc, gr)
    mix = _conv_fwd(proj, cw, conv_b, _gdn_gate(o, proj, gdn_norm_w))
    wo = _spread_wait(wo_started, mix, "gather", "gather_w_out_wait")[1].reshape(-1, D_MODEL)
    dy, dyb, dmix, g_fnw, loss_v = _out_proj_loss(xs, mix, wo, fnw, tgt)

    g_wout = _tn_matmul(mix, dyb, "grad_w_out")
    gwo_started = _spread_start(g_wout.reshape(N_DEV, -1, D_MODEL), dyb, "scatter", "exchange_grad_w_out_start")
    do, dproj, g_gnw = _gdn_gate_bwd(o, proj, gdn_norm_w, dmix, tok(gwo_started))
    dproj, g_cw, g_cb = _conv_bwd(proj, cw, conv_b, dmix, dproj)
    dqkv_n, dsc, dgr = _gdn_bwd(qkv, sc, gr, u_all, w_all, vn_all, t_all, sp_all, do)
    dproj, g_cqkv = _qkv_bwd(proj, cqkv, dqkv_n, dproj)
    g_cqkv_blk = g_cqkv.reshape(4, N_DEV, -1).transpose(1, 0, 2)
    g_cw_blk = jnp.pad(g_cw.reshape(3, N_DEV, -1).transpose(1, 0, 2),
                       ((0, 0), (0, 1), (0, g_cqkv_blk.shape[2] - g_cw.shape[1] // N_DEV)))
    gsm_started = _spread_start(jnp.concatenate([g_cqkv_blk, g_cw_blk], axis=1), g_cqkv, "scatter",
                                "exchange_small_sharded_grads_start")
    dgr_col = jnp.pad(dgr.transpose(0, 2, 1).reshape(L, HEADS), ((0, 0), (HEADS, LANE - 2 * HEADS)))
    dproj, g_sc = _scalars_bwd(proj, alog_p, dtb_p, dsc, dgr_col, dproj, tok(gsm_started))
    g_win_blk = _grad_blocks(_tn_matmul(dproj, h, "grad_w_in"))

    (p_win,) = _pair_exchange([g_win_blk], "exchange_grads_pair")
    r_small = _spread_wait(gsm_started, p_win, "scatter", "exchange_small_sharded_grads_wait")[1]
    s_win = _pair_sum(g_win_blk, p_win, "pair_sum_w_in")
    gw1_started = _spread_start(s_win, r_small, "axis_a", "exchange_grads_axis1_start")
    grad_x, g_nw = _input_grad(dproj, wpad, xs, norm_in_w, dy, tok(gw1_started))
    s_thru, got1 = _spread_wait(gw1_started, grad_x, "axis_a", "exchange_grads_axis1_wait")
    t_win = _axis_sum(s_thru, got1, "axis_sum_w_in")
    gw2_started = _spread_start(t_win, got1, "axis_b", "exchange_grads_axis2_start")

    r_wout = _spread_wait(gwo_started, tok(gw2_started), "scatter", "exchange_grad_w_out_wait")[1]
    upd_wout =_adamw_reduce(r_wout, w_out[0], m_w_out[0], v_w_out[0], "adamw_w_out")
    upd_cqkv = _adamw_reduce(r_small, conv_qkv_w[0], m_conv_qkv_w[0], v_conv_qkv_w[0], "adamw_conv_qkv_w", first_row=0)
    upd_cw = _adamw_reduce(r_small, as_taps(conv_w), as_taps(m_conv_w), as_taps(v_conv_w), "adamw_conv_w", first_row=4)

    t_thru, got2 = _spread_wait(gw2_started, upd_cw[0], "axis_b", "exchange_grads_axis2_wait")

    small_g = _pack_small([g_nw, g_cb, g_fnw, g_gnw, g_sc, loss_v], got2)
    gsg_started = _spread_start(small_g, got2, "gather", "gather_small_grads_start")
    upd_win_t = _adamw_w_in(t_thru, got2, as_rows(w_in), as_rows(m_w_in), as_rows(v_w_in), tok(gsg_started))
    upd_win = [jnp.transpose(a, (1, 2, 0)) for a in upd_win_t]
    small_all = _spread_wait(gsg_started, upd_win_t[0], "gather", "gather_small_grads_wait")[1]
    fvec = lambda a: a.reshape(1, D_MODEL)
    upd_small, loss_sum = _adamw_small(
        small_all,
        [norm_in_w, conv_b, fvec(final_norm_w), gdn_norm_w, A_log, dt_bias],
        [m_norm_in_w, m_conv_b, fvec(m_final_norm_w), m_gdn_norm_w, m_A_log, m_dt_bias],
        [v_norm_in_w, v_conv_b, fvec(v_final_norm_w), v_gdn_norm_w, v_A_log, v_dt_bias])

    outs = [loss_sum[0, 0], grad_x[None]]
    for k in range(4):
        nw_k, cb_k, fw_k, gn_k, al_k, dt_k = upd_small[k]
        outs += [nw_k, upd_win[k], upd_cqkv[k][None], al_k, dt_k, gn_k,
                 as_taps(upd_cw[k]), cb_k, upd_wout[k][None], fw_k.reshape(D_MODEL)]
    return tuple(outs)
```

```python
import jax
import jax.numpy as jnp
from jax import lax
from jax.experimental import pallas as pl
from jax.experimental.pallas import tpu as pltpu

f32 = jnp.float32
bf16 = jnp.bfloat16

N_DEV = 8
D_MODEL = 1024
HEADS = 8
HEAD_DIM = 128
CHUNK = 64
GDN_CPS = 4
GDN_CPS_BWD = 1
GDN_WIDTH = HEADS * HEAD_DIM
CONV_WIDTH = 1024
PROJ_WIDTH = 8208
SHARD_W = PROJ_WIDTH // N_DEV
EPS = 1e-6

LANE = 128
ELT_W = 256

OFF_QKV, OFF_ZG, OFF_CONV, OFF_BA = 0, 3072, 4096, 8192
CONV_BLOCK = 4 * ELT_W
PROJ_PAD = 8448
NAT_BA, NAT_CONV = 4096, 4112


def _padded_col(n):
    if n < NAT_BA:
        return n
    if n < NAT_CONV:
        return OFF_BA + n - NAT_BA
    g, ch = divmod(n - NAT_CONV, CONV_WIDTH)
    j, r = divmod(ch, ELT_W)
    return OFF_CONV + CONV_BLOCK * j + ELT_W * g + r


def _layout_segments(n0, n1):
    cuts = [NAT_BA, NAT_CONV] + [NAT_CONV + ELT_W * k for k in range(1, 4 * CONV_WIDTH // ELT_W)]
    pts = [n0] + [c for c in cuts if n0 < c < n1] + [n1]
    return [(lo, hi - lo, _padded_col(lo)) for lo, hi in zip(pts, pts[1:])]

ADAM_LR, ADAM_B1, ADAM_B2, ADAM_EPS, ADAM_WD, ADAM_STEP = 0.001, 0.9, 0.999, 1e-08, 0.01, 10

V7X_VMEM_BYTES = 64 * 1024 * 1024
VMEM_LIMIT = V7X_VMEM_BYTES - 8 * 1024 * 1024

MESH = pl.DeviceIdType.MESH
ANY = pl.BlockSpec(memory_space=pl.ANY)


def _pcall(body, **kw):
    return pl.pallas_call(body, **kw)


def _cparams(*sem):
    return pltpu.CompilerParams(dimension_semantics=sem if sem else None, vmem_limit_bytes=VMEM_LIMIT)


def _mm(a, b):
    return jnp.dot(a.astype(bf16), b.astype(bf16), preferred_element_type=f32)


def _mm_nt(a, b):
    return lax.dot_general(a.astype(bf16), b.astype(bf16), (((1,), (1,)), ((), ())), preferred_element_type=f32)


def _cat16(parts, axis):
    return jnp.concatenate([p.astype(bf16) for p in parts], axis=axis)


def _mm_tn(a, b):
    return lax.dot_general(a.astype(bf16), b.astype(bf16), (((0,), (0,)), ((), ())), preferred_element_type=f32)


def _rows(shape):
    return lax.broadcasted_iota(jnp.int32, shape, 0)


def _lanes(shape):
    return lax.broadcasted_iota(jnp.int32, shape, 1)


def _shift_down(x, s):
    if s == 0:
        return x
    return jnp.where(_rows(x.shape) >= s, pltpu.roll(x, s, 0), 0.0)


def _shift_up(x, s):
    if s == 0:
        return x
    n = x.shape[0]
    return jnp.where(_rows(x.shape) < n - s, pltpu.roll(x, n - s, 0), 0.0)


def _sigmoid(x):
    return jax.nn.sigmoid(x)


def _softplus(x):
    e = jnp.exp(-jnp.abs(x))
    small = e * (1.0 - e * (0.5 - e * (1.0 / 3.0)))
    return jnp.maximum(x, 0.0) + jnp.where(e < 0.01, small, jnp.log(1.0 + e))


def _mesh_pos():
    return lax.axis_index("x"), lax.axis_index("y"), lax.axis_index("c")


def _flat(px, py, pc):
    return 4 * px + 2 * py + pc


def _all_gather(xs, name, pieces=None):
    n = len(xs)
    pieces = pieces or [1] * n
    items = [(a, q) for a in range(n) for q in range(pieces[a])]
    ni = len(items)

    def view(ref, i):
        a, q = items[i]
        if pieces[a] == 1:
            return ref
        wd = xs[a].shape[-1] // pieces[a]
        return ref.at[(slice(None),) * (xs[a].ndim - 1) + (pl.ds(q * wd, wd),)]

    def body(*refs):
        x_refs, o_refs = refs[:n], refs[n:2 * n]
        send_sems, recv_sems, local_sems = refs[2 * n:]
        x, y, c = _mesh_pos()
        me, sibling = (x, y, c), (x, y, 1 - c)
        flip = lambda v, bit: v + bit - 2 * v * bit
        nbr_a = (flip(x, 1 - c), flip(y, c))
        nbr_b = (flip(x, c), flip(y, 1 - c))
        diag = (1 - x, 1 - y)

        def copy(i, k, block, to, own=False):
            a = items[i][0]
            dst = view(o_refs[a].at[_flat(*block)], i)
            return pltpu.make_async_remote_copy(
                src_ref=view(x_refs[a], i) if own else dst, dst_ref=dst,
                send_sem=send_sems.at[i, k], recv_sem=recv_sems.at[i, k], device_id=to, device_id_type=MESH)

        mine, sent = [], []

        def go(cp):
            cp.start()
            sent.append(cp)

        for a in range(n):
            cp = pltpu.make_async_copy(x_refs[a], o_refs[a].at[_flat(*me)], local_sems.at[a])
            cp.start()
            mine.append(cp)
        for a in range(ni):
            go(copy(a, 1, me, (*nbr_a, c), own=True))
            go(copy(a, 2, me, (*nbr_b, c), own=True))
            go(copy(a, 0, me, sibling, own=True))
        for a in range(ni):
            copy(a, 1, (*nbr_a, c), me).wait_recv()
            go(copy(a, 3, (*nbr_a, c), (*nbr_b, c)))
            go(copy(a, 4, (*nbr_a, c), sibling))
        for a in range(ni):
            copy(a, 2, (*nbr_b, c), me).wait_recv()
            go(copy(a, 5, (*nbr_b, c), sibling))
        for a in range(ni):
            copy(a, 3, (*diag, c), me).wait_recv()
            go(copy(a, 6, (*diag, c), sibling))
        for a in range(ni):
            copy(a, 0, sibling, me).wait_recv()
            copy(a, 4, (*nbr_b, 1 - c), me).wait_recv()
            copy(a, 5, (*nbr_a, 1 - c), me).wait_recv()
            copy(a, 6, (*diag, 1 - c), me).wait_recv()
        for cp in sent:
            cp.wait_send()
        for cp in mine:
            cp.wait()

    outs = _pcall(
        body, name=name,
        out_shape=[jax.ShapeDtypeStruct((N_DEV,) + a.shape, a.dtype) for a in xs],
        in_specs=[ANY] * n, out_specs=[ANY] * n,
        scratch_shapes=[pltpu.SemaphoreType.DMA((ni, 7)), pltpu.SemaphoreType.DMA((ni, 7)), pltpu.SemaphoreType.DMA((n,))],
    )(*xs)
    return list(outs)


def _pair_exchange(gs, name):
    n = len(gs)
    chips = [(0, 0), (0, 1), (1, 0), (1, 1)]

    def body(*refs):
        g_refs, o_refs = refs[:n], refs[n:2 * n]
        send_sems, recv_sems = refs[2 * n:]
        x, y, c = _mesh_pos()
        sibling = (x, y, 1 - c)

        def copy(a, i):
            xp, yp = chips[i]
            return pltpu.make_async_remote_copy(
                src_ref=g_refs[a].at[_flat(xp, yp, 1 - c)], dst_ref=o_refs[a].at[i],
                send_sem=send_sems.at[a, i], recv_sem=recv_sems.at[a, i], device_id=sibling, device_id_type=MESH)

        cps = [copy(a, i) for a in range(n) for i in range(4)]
        for cp in cps:
            cp.start()
        for cp in cps:
            cp.wait()

    outs = _pcall(
        body, name=name,
        out_shape=[jax.ShapeDtypeStruct((4,) + a.shape[1:], a.dtype) for a in gs],
        in_specs=[ANY] * n, out_specs=[ANY] * n,
        scratch_shapes=[pltpu.SemaphoreType.DMA((n, 4)), pltpu.SemaphoreType.DMA((n, 4))],
    )(*gs)
    return list(outs)


def _pair_sum(g, p1, name):
    _, R, C = g.shape
    tr = 256 if R % 256 == 0 else R
    cidx = lax.axis_index("c").astype(jnp.int32).reshape(1)

    def body(c_ref, g_ref, p_ref, o_ref):
        o_ref[...] = (g_ref[...].astype(f32) + p_ref[...].astype(f32)).astype(o_ref.dtype)

    return _pcall(
        body, name=name,
        grid_spec=pltpu.PrefetchScalarGridSpec(
            num_scalar_prefetch=1, grid=(4, R // tr),
            in_specs=[pl.BlockSpec((1, tr, C), lambda i, r, c_ref: (2 * i + c_ref[0], r, 0)),
                      pl.BlockSpec((1, tr, C), lambda i, r, c_ref: (i, r, 0))],
            out_specs=pl.BlockSpec((1, tr, C), lambda i, r, c_ref: (i, r, 0))),
        out_shape=jax.ShapeDtypeStruct((4, R, C), g.dtype),
        compiler_params=_cparams("parallel", "parallel"),
    )(cidx, g, p1)


def _axis_sum(s, got, name):
    _, R, C = s.shape
    x, y, c = _mesh_pos()
    me, _, b, _ = _axis_chips(x, y, c)
    idx = jnp.stack([2 * me[0] + me[1], 2 * b[0] + b[1]]).astype(jnp.int32)

    def body(idx_ref, s_ref, g_ref, o_ref):
        o_ref[...] = (s_ref[...].astype(f32) + g_ref[...].astype(f32)).astype(o_ref.dtype)

    return _pcall(
        body, name=name,
        grid_spec=pltpu.PrefetchScalarGridSpec(
            num_scalar_prefetch=1, grid=(2,),
            in_specs=[pl.BlockSpec((1, R, C), lambda k, idx_ref: (idx_ref[k], 0, 0)),
                      pl.BlockSpec((1, R, C), lambda k, idx_ref: (k, 0, 0))],
            out_specs=pl.BlockSpec((1, R, C), lambda k, idx_ref: (k, 0, 0))),
        out_shape=jax.ShapeDtypeStruct((2, R, C), s.dtype),
        compiler_params=_cparams("parallel"),
    )(idx, s, got)


HBM = pl.BlockSpec(memory_space=pltpu.HBM)
SEM = pl.BlockSpec(memory_space=pltpu.SEMAPHORE)
EFFECT = pltpu.SideEffectType.DATAFLOW_SIDE_EFFECTING


def _peers(x, y, c):
    out = []
    for k in range(1, N_DEV):
        kx, ky, kc = (k >> 2) & 1, (k >> 1) & 1, k & 1
        out.append(((1 - x) if kx else x, (1 - y) if ky else y, (1 - c) if kc else c))
    return out


SPREAD_COPIES = {"gather": N_DEV - 1, "scatter": N_DEV - 1, "axis_a": 2, "axis_b": 1}
SPREAD_SLOTS = {"axis_a": 2, "axis_b": 1}


def _axis_chips(x, y, c):
    flip = lambda v, bit: v + bit - 2 * v * bit
    return (x, y), (flip(x, 1 - c), flip(y, c)), (flip(x, c), flip(y, 1 - c)), (1 - x, 1 - y)


def _spread_copy(src_ref, land_ref, send_sems, recv_sems, k, plan):
    x, y, c = _mesh_pos()
    if plan in ("axis_a", "axis_b"):
        _, a, b, d = _axis_chips(x, y, c)
        chip = lambda p: 2 * p[0] + p[1]
        peer = (*(a if plan == "axis_a" else b), c)
        src = src_ref.at[chip(a) if k == 0 else chip(d)] if plan == "axis_a" else src_ref.at[1]
        slot = k
    else:
        peer = _peers(x, y, c)[k]
        src, slot = (src_ref.at[_flat(*peer)] if plan == "scatter" else src_ref), _flat(x, y, c)
    return pltpu.make_async_remote_copy(
        src_ref=src, dst_ref=land_ref.at[slot], send_sem=send_sems.at[k], recv_sem=recv_sems.at[k],
        device_id=peer, device_id_type=MESH)


def _own_copy(src_ref, land_ref, send_sems, plan):
    me = _flat(*_mesh_pos())
    return pltpu.make_async_copy(src_ref.at[me] if plan == "scatter" else src_ref, land_ref.at[me],
                                 send_sems.at[SPREAD_COPIES[plan]])


def _spread_start(src, after, plan, name):
    land_shape = (N_DEV,) + src.shape if plan == "gather" else src.shape
    if plan in SPREAD_SLOTS:
        land_shape = (SPREAD_SLOTS[plan],) + src.shape[1:]
    n_copies = SPREAD_COPIES[plan]

    def body(src_ref, land_ref, after_ref, send_sems, recv_sems, src_thru, land_thru, token):
        for k in range(n_copies):
            _spread_copy(src_ref, land_ref, send_sems, recv_sems, k, plan).start()
        if plan not in SPREAD_SLOTS:
            _own_copy(src_ref, land_ref, send_sems, plan).start()
        token[...] = jnp.zeros_like(token)

    return _pcall(
        body, name=name,
        out_shape=(pltpu.SemaphoreType.DMA((n_copies + (plan not in SPREAD_SLOTS),)), pltpu.SemaphoreType.DMA((n_copies,)),
                   pltpu.HBM(src.shape, src.dtype), pltpu.HBM(land_shape, src.dtype), jax.ShapeDtypeStruct((8, LANE), f32)),
        in_specs=(HBM, HBM, ANY), out_specs=(SEM, SEM, HBM, HBM, pl.BlockSpec(memory_space=pltpu.VMEM)),
        input_output_aliases={0: 2, 1: 3},
        compiler_params=pltpu.CompilerParams(has_side_effects=EFFECT),
    )(pltpu.with_memory_space_constraint(src, pltpu.HBM),
      pltpu.with_memory_space_constraint(lax.empty(land_shape, src.dtype), pltpu.HBM), after)


def _spread_wait(started, after, plan, name):
    send_sems, recv_sems, src_thru, land_thru, _ = started

    def body(src_ref, land_ref, send_sems, recv_sems, after_ref, src_dead, got_ref):
        for k in range(SPREAD_COPIES[plan]):
            cp = _spread_copy(src_ref, land_ref, send_sems, recv_sems, k, plan)
            cp.wait_send()
            cp.wait_recv()
        if plan not in SPREAD_SLOTS:
            _own_copy(src_ref, land_ref, send_sems, plan).wait()

    return _pcall(
        body, name=name,
        out_shape=(pltpu.HBM(src_thru.shape, src_thru.dtype), pltpu.HBM(land_thru.shape, land_thru.dtype)),
        in_specs=(HBM, HBM, SEM, SEM, ANY), out_specs=(HBM, HBM), input_output_aliases={0: 0, 1: 1},
        compiler_params=pltpu.CompilerParams(has_side_effects=EFFECT),
    )(src_thru, land_thru, send_sems, recv_sems, after)


COL_TILE = 256


def _cast_weights(w3, wo):
    n = w3.shape[0]

    def body(w_ref, wo_ref, o_ref, oo_ref):
        o_ref[...] = w_ref[:, 0, :].astype(bf16)
        oo_ref[...] = wo_ref[...].astype(bf16)

    tile = 2 * COL_TILE
    return _pcall(
        body, name="cast_weights", grid=(D_MODEL // tile,),
        in_specs=[pl.BlockSpec((n, 1, tile), lambda j: (0, 0, j)), pl.BlockSpec((wo.shape[0], tile), lambda j: (0, j))],
        out_specs=[pl.BlockSpec((n, tile), lambda j: (0, j)), pl.BlockSpec((wo.shape[0], tile), lambda j: (0, j))],
        out_shape=[jax.ShapeDtypeStruct((n, D_MODEL), bf16), jax.ShapeDtypeStruct(wo.shape, bf16)],
        compiler_params=_cparams("parallel"),
    )(w3, wo)


def _relayout_w_in(win_g):
    def body(g_ref, o_ref):
        used = OFF_BA + NAT_CONV - NAT_BA
        o_ref[used:PROJ_PAD, :] = jnp.zeros((PROJ_PAD - used, COL_TILE), o_ref.dtype)
        for d in range(N_DEV):
            for lo, width, dst in _layout_segments(d * SHARD_W, (d + 1) * SHARD_W):
                src = lo - d * SHARD_W
                o_ref[dst:dst + width, :] = g_ref[d, src:src + width, :]

    return _pcall(
        body, name="relayout_w_in", grid=(D_MODEL // COL_TILE,),
        in_specs=[pl.BlockSpec((N_DEV, SHARD_W, COL_TILE), lambda j: (0, 0, j))],
        out_specs=pl.BlockSpec((PROJ_PAD, COL_TILE), lambda j: (0, j)),
        out_shape=jax.ShapeDtypeStruct((PROJ_PAD, D_MODEL), win_g.dtype),
        compiler_params=_cparams("parallel"),
    )(win_g)


def _grad_blocks(g_t):
    def body(p_ref, o_ref):
        for d in range(N_DEV):
            for lo, width, src in _layout_segments(d * SHARD_W, (d + 1) * SHARD_W):
                dst = lo - d * SHARD_W
                o_ref[d, dst:dst + width, :] = p_ref[src:src + width, :]

    return _pcall(
        body, name="grad_blocks", grid=(D_MODEL // COL_TILE,),
        in_specs=[pl.BlockSpec((PROJ_PAD, COL_TILE), lambda j: (0, j))],
        out_specs=pl.BlockSpec((N_DEV, SHARD_W, COL_TILE), lambda j: (0, 0, j)),
        out_shape=jax.ShapeDtypeStruct((N_DEV, SHARD_W, D_MODEL), bf16),
        compiler_params=_cparams("parallel"),
    )(g_t)


def _in_proj(x, nw, wpad_t, after):
    L = x.shape[0]
    tn = 768
    nj = wpad_t.shape[0] // tn

    def body(x_ref, nw_ref, w_ref, after_ref, proj_ref, h_ref):
        first = pl.program_id(0) == 0

        def project(r, n, hv):
            proj_ref[r:r + n, :] = lax.dot_general(hv, w_ref[...], (((1,), (1,)), ((), ())), preferred_element_type=f32)

        @pl.when(first)
        def _():
            for r in range(0, L, 256):
                xs = x_ref[r:r + 256, :]
                ms = jnp.mean(xs * xs, axis=-1, keepdims=True)
                hv = ((xs * lax.rsqrt(ms + EPS)) * nw_ref[...]).astype(bf16)
                h_ref[r:r + 256, :] = hv
                project(r, 256, hv)

        @pl.when(jnp.logical_not(first))
        def _():
            for r in range(0, L, 512):
                project(r, 512, h_ref[r:r + 512, :])

    return _pcall(
        body, name="in_proj", grid=(nj,),
        in_specs=[pl.BlockSpec((L, D_MODEL), lambda j: (0, 0)), pl.BlockSpec((1, D_MODEL), lambda j: (0, 0)),
                  pl.BlockSpec((tn, D_MODEL), lambda j: (j, 0)), ANY],
        out_specs=[pl.BlockSpec((L, tn), lambda j: (0, j)), pl.BlockSpec((L, D_MODEL), lambda j: (0, 0))],
        out_shape=[jax.ShapeDtypeStruct((L, wpad_t.shape[0]), f32), jax.ShapeDtypeStruct((L, D_MODEL), bf16)],
        compiler_params=_cparams("arbitrary"),
    )(x, nw, wpad_t, after)


HALVES = [slice(i * LANE, (i + 1) * LANE) for i in range(ELT_W // LANE)]
QKV_W = 512
QKV_HEADS = [slice(i * LANE, (i + 1) * LANE) for i in range(QKV_W // LANE)]
STEPS_PER_GROUP = GDN_WIDTH // QKV_W


def _conv4(x, cw_ref, ls):
    return (cw_ref[3:4, ls] * x + cw_ref[2:3, ls] * _shift_down(x, 1) + cw_ref[1:2, ls] * _shift_down(x, 2)
            + cw_ref[0:1, ls] * _shift_down(x, 3))


def _qkv_act(proj, cw):
    L = proj.shape[0]
    steps = 3 * STEPS_PER_GROUP
    slots = 3

    def body(x_hbm, cw_ref, o_ref, ring, sems):
        j = pl.program_id(0)

        def fetch(s):
            cols = pl.ds(pl.multiple_of(s * QKV_W, QKV_W), QKV_W)
            return pltpu.make_async_copy(x_hbm.at[:, cols], ring.at[s % slots], sems.at[s % slots])

        @pl.when(j == 0)
        def _():
            fetch(0).start()
            fetch(1).start()

        @pl.when(j + 2 < steps)
        def _():
            fetch(j + 2).start()
        fetch(j).wait()
        slot = j % slots
        scale = jnp.where(j < STEPS_PER_GROUP, HEAD_DIM ** -0.5, 1.0).astype(f32)
        for ls in QKV_HEADS:
            c = _conv4(ring[slot, :, ls], cw_ref, ls)
            a = c * _sigmoid(c)
            rn = lax.rsqrt(jnp.sum(a * a, axis=1, keepdims=True) + EPS)
            o_ref[:, ls] = jnp.where(j < 2 * STEPS_PER_GROUP, (a * rn) * scale, a)

    return _pcall(
        body, name="qkv_act", grid=(steps,),
        in_specs=[ANY, pl.BlockSpec((4, QKV_W), lambda j: (0, j))],
        out_specs=pl.BlockSpec((L, QKV_W), lambda j: (0, j)),
        out_shape=jax.ShapeDtypeStruct((L, 3 * GDN_WIDTH), f32),
        scratch_shapes=[pltpu.VMEM((slots, L, QKV_W), f32), pltpu.SemaphoreType.DMA((slots,))],
        compiler_params=_cparams("arbitrary"),
    )(proj, cw)


def _scalars(proj, alog_p, dtb_p):
    L = proj.shape[0]
    nc = L // CHUNK

    def body(x_ref, al_ref, dt_ref, sc_ref, gr_ref):
        x = x_ref[...]
        lane = _lanes(x.shape)
        beta = _sigmoid(x)
        g = -jnp.exp(al_ref[...]) * _softplus(x + dt_ref[...])
        gc = jnp.where((lane >= HEADS) & (lane < 2 * HEADS), g, 0.0)
        rc = _rows(x.shape) & (CHUNK - 1)
        for s in (1, 2, 4, 8, 16, 32):
            gc = gc + jnp.where(rc >= s, pltpu.roll(gc, s, 0), 0.0)
        sc_ref[...] = jnp.where(lane < HEADS, beta, gc)
        sel = (_lanes((HEADS, LANE)) == _rows((HEADS, LANE)) + HEADS).astype(f32)
        for c in range(nc):
            gr_ref[c] = lax.dot_general(sel, sc_ref[c * CHUNK:(c + 1) * CHUNK, :], (((1,), (1,)), ((), ())),
                                        preferred_element_type=f32, precision=lax.Precision.HIGHEST)

    return _pcall(
        body, name="scalars", grid=(1,),
        in_specs=[pl.BlockSpec((L, LANE), lambda i: (0, OFF_BA // LANE)), pl.BlockSpec((1, LANE), lambda i: (0, 0)),
                  pl.BlockSpec((1, LANE), lambda i: (0, 0))],
        out_specs=[pl.BlockSpec((L, LANE), lambda i: (0, 0)), pl.BlockSpec((nc, HEADS, CHUNK), lambda i: (0, 0, 0))],
        out_shape=[jax.ShapeDtypeStruct((L, LANE), f32), jax.ShapeDtypeStruct((nc, HEADS, CHUNK), f32)],
        compiler_params=_cparams("arbitrary"),
    )(proj, alog_p, dtb_p)


def _head_scalars(sc, gr_ref, h, ci=0):
    lane = _lanes(sc.shape)
    beta = jnp.sum(jnp.where(lane == h, sc, 0.0), axis=1, keepdims=True)
    gcc = jnp.sum(jnp.where(lane == HEADS + h, sc, 0.0), axis=1, keepdims=True)
    gcr = gr_ref[ci, h:h + 1, :]
    gl = jnp.sum(jnp.where(_lanes(gcr.shape) == CHUNK - 1, gcr, 0.0), axis=1, keepdims=True)
    ii, jj = _rows((CHUNK, CHUNK)), _lanes((CHUNK, CHUNK))
    dmat = jnp.where(ii >= jj, jnp.exp(jnp.minimum(gcc - gcr, 0.0)), 0.0)
    dmat_t = jnp.where(jj >= ii, jnp.exp(jnp.minimum(gcr - gcc, 0.0)), 0.0)
    return beta, gcc, gl, dmat, dmat_t, ii, jj


def _gdn_fwd(qkv, sc, gr):
    L = qkv.shape[0]
    nc = L // CHUNK
    W = GDN_WIDTH
    cps = GDN_CPS if nc % GDN_CPS == 0 else 1
    rows_per_step = cps * CHUNK

    def body(qkv_ref, sc_ref, gr_ref, o_ref, u_ref, w_ref, vn_ref, t_ref, sp_ref, s_scr):
        @pl.when(pl.program_id(0) == 0)
        def _():
            s_scr[...] = jnp.zeros_like(s_scr)
        HS = range(cps * HEADS)
        hd = [i % HEADS for i in HS]
        rs = [slice((i // HEADS) * CHUNK, (i // HEADS + 1) * CHUNK) for i in HS]
        cs = [slice(hd[i] * HEAD_DIM, (hd[i] + 1) * HEAD_DIM) for i in HS]
        q = [qkv_ref[rs[i], hd[i] * HEAD_DIM:(hd[i] + 1) * HEAD_DIM] for i in HS]
        k = [qkv_ref[rs[i], W + hd[i] * HEAD_DIM:W + (hd[i] + 1) * HEAD_DIM] for i in HS]
        v = [qkv_ref[rs[i], 2 * W + hd[i] * HEAD_DIM:2 * W + (hd[i] + 1) * HEAD_DIM] for i in HS]
        hsc = [_head_scalars(sc_ref[rs[i], :], gr_ref, hd[i], i // HEADS) for i in HS]
        beta, gcc, gl, dmat = ([x[i] for x in hsc] for i in range(4))
        ii, jj = hsc[0][5], hsc[0][6]
        eg = [jnp.exp(gcc[h]) for h in HS]
        kb = [k[h] * beta[h] for h in HS]
        kk = [_mm_nt(kb[h], k[h]) for h in HS]
        qk = [_mm_nt(q[h], k[h]) for h in HS]
        n0 = [-jnp.where(ii > jj, kk[h] * dmat[h], 0.0) for h in HS]
        n1 = [_mm(n0[h], n0[h]) for h in HS]
        n2 = [_mm(n1[h], n1[h]) for h in HS]
        p01 = [n0[h] + n1[h] + _mm(n0[h], n1[h]) for h in HS]
        n3 = [_mm(n2[h], n2[h]) for h in HS]
        n4 = [_mm(n3[h], n3[h]) for h in HS]
        p23 = [n2[h] + n3[h] + _mm(n2[h], n3[h]) for h in HS]
        n5 = [_mm(n4[h], n4[h]) for h in HS]
        p03 = [p01[h] + p23[h] + _mm(p01[h], p23[h]) for h in HS]
        p45 = [n4[h] + n5[h] + _mm(n4[h], n5[h]) for h in HS]
        t = [p03[h] + p45[h] + _mm(p03[h], p45[h]) for h in HS]
        vb = [v[h] * beta[h] for h in HS]
        kbg = [kb[h] * eg[h] for h in HS]
        uw = [_mm(t[h], _cat16([vb[h], kbg[h]], 1)) for h in HS]
        u = [vb[h] + uw[h][:, :HEAD_DIM] for h in HS]
        w = [kbg[h] + uw[h][:, HEAD_DIM:] for h in HS]
        wq = [_cat16([w[h], q[h] * eg[h]], 0) for h in HS]
        p = [jnp.where(ii >= jj, qk[h] * dmat[h], 0.0) for h in HS]
        ks = [k[h] * jnp.exp(gl[h] - gcc[h]) for h in HS]
        s = [s_scr[h] for h in range(HEADS)]
        for ci in range(cps):
            IS = range(ci * HEADS, (ci + 1) * HEADS)
            ws = [_mm(wq[i], s[hd[i]]) for i in IS]
            vn = [u[i] - ws[hd[i]][:CHUNK] for i in IS]
            pv = [_mm(p[i], vn[hd[i]]) for i in IS]
            kv = [_mm_tn(ks[i], vn[hd[i]]) for i in IS]
            for i in IS:
                h = hd[i]
                sp_ref[ci, cs[i], :] = s[h]
                o_ref[rs[i], cs[i]] = ws[h][CHUNK:] + pv[h]
                vn_ref[rs[i], cs[i]] = vn[h].astype(bf16)
            s = [jnp.exp(gl[i]) * s[hd[i]] + kv[hd[i]] for i in IS]
        for h in range(HEADS):
            s_scr[h] = s[h]
        for i in HS:
            u_ref[rs[i], cs[i]] = u[i].astype(bf16)
            w_ref[rs[i], cs[i]] = w[i].astype(bf16)
            t_ref[i // HEADS, hd[i]] = t[i].astype(bf16)

    row = lambda c: (c, 0)
    act, act16 = jax.ShapeDtypeStruct((L, W), f32), jax.ShapeDtypeStruct((L, W), bf16)
    return _pcall(
        body, name="gdn_fwd", grid=(nc // cps,),
        in_specs=[pl.BlockSpec((rows_per_step, 3 * W), row), pl.BlockSpec((rows_per_step, LANE), row),
                  pl.BlockSpec((cps, HEADS, CHUNK), lambda c: (c, 0, 0))],
        out_specs=[pl.BlockSpec((rows_per_step, W), row)] * 4 + [
            pl.BlockSpec((cps, HEADS, CHUNK, CHUNK), lambda c: (c, 0, 0, 0)),
            pl.BlockSpec((cps, W, HEAD_DIM), lambda c: (c, 0, 0))],
        out_shape=[act, act16, act16, act16, jax.ShapeDtypeStruct((nc, HEADS, CHUNK, CHUNK), bf16),
                   jax.ShapeDtypeStruct((nc, W, HEAD_DIM), f32)],
        scratch_shapes=[pltpu.VMEM((HEADS, HEAD_DIM, HEAD_DIM), f32)],
        compiler_params=_cparams("arbitrary"),
    )(qkv, sc, gr)


def _gdn_gate(o, proj, gnw):
    L = o.shape[0]

    def body(o_ref, z_ref, w_ref, m_ref):
        for ls in HALVES:
            ov, z = o_ref[:, ls], z_ref[:, ls]
            rms = lax.rsqrt(jnp.mean(ov * ov, axis=-1, keepdims=True) + EPS)
            m_ref[:, ls] = (((ov * rms) * w_ref[...]) * (z * _sigmoid(z))).astype(bf16)

    return _pcall(
        body, name="gdn_gate", grid=(GDN_WIDTH // ELT_W,),
        in_specs=[pl.BlockSpec((L, ELT_W), lambda j: (0, j)), pl.BlockSpec((L, ELT_W), lambda j: (0, OFF_ZG // ELT_W + j)),
                  pl.BlockSpec((1, LANE), lambda j: (0, 0))],
        out_specs=pl.BlockSpec((L, ELT_W), lambda j: (0, j)),
        out_shape=jax.ShapeDtypeStruct((L, GDN_WIDTH + CONV_WIDTH), bf16),
        compiler_params=_cparams("parallel"),
    )(o, proj, gnw)


def _conv3(u, cw_ref, ls):
    return cw_ref[2:3, ls] * u + cw_ref[1:2, ls] * _shift_down(u, 1) + cw_ref[0:1, ls] * _shift_down(u, 2)


def _conv_specs(L):
    return [pl.BlockSpec((L, CONV_BLOCK), lambda j: (0, OFF_CONV // CONV_BLOCK + j)),
            pl.BlockSpec((3, ELT_W), lambda j: (0, j)), pl.BlockSpec((1, ELT_W), lambda j: (0, j))]


def _conv_parts(ls):
    return [slice(g * ELT_W + ls.start, g * ELT_W + ls.stop) for g in range(4)]


def _conv_fwd(proj, cw, cb, mix):
    L = proj.shape[0]

    def body(p_ref, cw_ref, cb_ref, mix_in, m_ref):
        for ls in HALVES:
            sb, sc_, sh, sz = _conv_parts(ls)
            z = p_ref[:, sz]
            cv = _conv3(p_ref[:, sc_] * p_ref[:, sh], cw_ref, ls) + cb_ref[:, ls]
            m_ref[:, ls] = ((p_ref[:, sb] * cv) * (z * _sigmoid(z))).astype(bf16)

    return _pcall(
        body, name="conv_fwd", grid=(CONV_WIDTH // ELT_W,),
        in_specs=_conv_specs(L) + [ANY], out_specs=pl.BlockSpec((L, ELT_W), lambda j: (0, GDN_WIDTH // ELT_W + j)),
        out_shape=jax.ShapeDtypeStruct(mix.shape, mix.dtype), input_output_aliases={3: 0},
        compiler_params=_cparams("parallel"),
    )(proj, cw, cb, mix)


def _out_proj_loss(x, mix, wo, fw, tgt):
    L = x.shape[0]
    tm = min(512, L)
    MW = GDN_WIDTH + CONV_WIDTH

    def body(x_ref, m_ref, wo_ref, fw_ref, t_ref, dy_ref, dyb_ref, dm_ref, gfw_ref, loss_ref):
        @pl.when(pl.program_id(0) == 0)
        def _():
            gfw_ref[...] = jnp.zeros_like(gfw_ref)
            loss_ref[...] = jnp.zeros_like(loss_ref)
        y = x_ref[...] + jnp.dot(m_ref[...], wo_ref[...], preferred_element_type=f32)
        r = lax.rsqrt(jnp.mean(y * y, axis=-1, keepdims=True) + EPS)
        yh = y * r
        fwv = fw_ref[...]
        diff = yh * fwv - t_ref[...]
        loss_ref[...] += jnp.sum(jnp.sum(diff * diff, axis=-1, keepdims=True), axis=0, keepdims=True) * (0.5 / D_MODEL)
        dout = diff * (1.0 / D_MODEL)
        gfw_ref[...] += jnp.sum(dout * yh, axis=0, keepdims=True)
        dyh = dout * fwv
        dy = r * (dyh - yh * jnp.mean(dyh * yh, axis=-1, keepdims=True))
        dy_ref[...] = dy
        dyb = dy.astype(bf16)
        dyb_ref[...] = dyb
        dm_ref[...] = lax.dot_general(dyb, wo_ref[...], (((1,), (1,)), ((), ())), preferred_element_type=f32)

    row = lambda i: (i, 0)
    fix = lambda i: (0, 0)
    act = jax.ShapeDtypeStruct((L, D_MODEL), f32)
    return _pcall(
        body, name="out_proj_loss", grid=(L // tm,),
        in_specs=[pl.BlockSpec((tm, D_MODEL), row), pl.BlockSpec((tm, MW), row), pl.BlockSpec((MW, D_MODEL), fix),
                  pl.BlockSpec((1, D_MODEL), fix), pl.BlockSpec((tm, D_MODEL), row)],
        out_specs=[pl.BlockSpec((tm, D_MODEL), row), pl.BlockSpec((tm, D_MODEL), row), pl.BlockSpec((tm, MW), row),
                   pl.BlockSpec((1, D_MODEL), fix), pl.BlockSpec((1, LANE), fix)],
        out_shape=[act, jax.ShapeDtypeStruct((L, D_MODEL), bf16), jax.ShapeDtypeStruct((L, MW), f32),
                   jax.ShapeDtypeStruct((1, D_MODEL), f32), jax.ShapeDtypeStruct((1, LANE), f32)],
        compiler_params=_cparams("arbitrary"),
    )(x, mix, wo, fw, tgt)


def _tn_matmul(a, b, name):
    L, M = a.shape
    N = b.shape[1]
    tm = 512 if M % 512 == 0 else (768 if M % 768 == 0 else M)

    def body(a_ref, b_ref, o_ref):
        o_ref[...] = lax.dot_general(a_ref[...], b_ref[...], (((0,), (0,)), ((), ())),
                                     preferred_element_type=f32).astype(o_ref.dtype)

    return _pcall(
        body, name=name, grid=(M // tm,),
        in_specs=[pl.BlockSpec((L, tm), lambda i: (0, i)), pl.BlockSpec((L, N), lambda i: (0, 0))],
        out_specs=pl.BlockSpec((tm, N), lambda i: (i, 0)),
        out_shape=jax.ShapeDtypeStruct((M, N), bf16),
        compiler_params=_cparams("parallel"),
    )(a, b)


def _gdn_gate_bwd(o, proj, gnw, dmix_a, after):
    L = o.shape[0]

    def body(o_ref, z_ref, w_ref, dm_ref, after_ref, do_ref, dz_ref, gw_ref):
        @pl.when(pl.program_id(0) == 0)
        def _():
            gw_ref[...] = jnp.zeros_like(gw_ref)
        wv = w_ref[...]
        for ls in HALVES:
            ov, z, dm = o_ref[:, ls], z_ref[:, ls], dm_ref[:, ls]
            rms = lax.rsqrt(jnp.mean(ov * ov, axis=-1, keepdims=True) + EPS)
            xh = ov * rms
            sg = _sigmoid(z)
            d_on = dm * (z * sg)
            dz_ref[:, ls] = (dm * (xh * wv) * (sg * (1.0 + z * (1.0 - sg)))).astype(bf16)
            gw_ref[...] += jnp.sum(d_on * xh, axis=0, keepdims=True)
            dxh = d_on * wv
            do_ref[:, ls] = (rms * (dxh - xh * jnp.mean(dxh * xh, axis=-1, keepdims=True))).astype(bf16)

    wide = pl.BlockSpec((L, ELT_W), lambda j: (0, j))
    return _pcall(
        body, name="gdn_gate_bwd", grid=(GDN_WIDTH // ELT_W,),
        in_specs=[wide, pl.BlockSpec((L, ELT_W), lambda j: (0, OFF_ZG // ELT_W + j)),
                  pl.BlockSpec((1, LANE), lambda j: (0, 0)), wide, ANY],
        out_specs=[wide, pl.BlockSpec((L, ELT_W), lambda j: (0, OFF_ZG // ELT_W + j)),
                   pl.BlockSpec((1, LANE), lambda j: (0, 0))],
        out_shape=[jax.ShapeDtypeStruct((L, GDN_WIDTH), bf16), jax.ShapeDtypeStruct((L, PROJ_PAD), bf16),
                   jax.ShapeDtypeStruct((1, LANE), f32)],
        compiler_params=_cparams("arbitrary"),
    )(o, proj, gnw, dmix_a, after)


def _conv_bwd(proj, cw, cb, dmix_b, dproj):
    L = proj.shape[0]

    def body(p_ref, cw_ref, cb_ref, dm_ref, dproj_in, dp_ref, gcw_ref, gcb_ref):
        for ls in HALVES:
            sb, sc_, sh, sz_ = _conv_parts(ls)
            bv, cv_, hv, z, dm = p_ref[:, sb], p_ref[:, sc_], p_ref[:, sh], p_ref[:, sz_], dm_ref[:, ls]
            u = cv_ * hv
            cv = _conv3(u, cw_ref, ls) + cb_ref[:, ls]
            sg = _sigmoid(z)
            sz = z * sg
            dp_ref[:, sb] = (dm * cv * sz).astype(bf16)
            dp_ref[:, sz_] = (dm * (bv * cv) * (sg * (1.0 + z * (1.0 - sg)))).astype(bf16)
            dcv = dm * bv * sz
            gcb_ref[:, ls] = jnp.sum(dcv, axis=0, keepdims=True)
            dcv1, dcv2 = _shift_up(dcv, 1), _shift_up(dcv, 2)
            gcw_ref[2:3, ls] = jnp.sum(dcv * u, axis=0, keepdims=True)
            gcw_ref[1:2, ls] = jnp.sum(dcv1 * u, axis=0, keepdims=True)
            gcw_ref[0:1, ls] = jnp.sum(dcv2 * u, axis=0, keepdims=True)
            du = cw_ref[2:3, ls] * dcv + cw_ref[1:2, ls] * dcv1 + cw_ref[0:1, ls] * dcv2
            dp_ref[:, sc_] = (du * hv).astype(bf16)
            dp_ref[:, sh] = (du * cv_).astype(bf16)

    return _pcall(
        body, name="conv_bwd", grid=(CONV_WIDTH // ELT_W,),
        in_specs=_conv_specs(L) + [pl.BlockSpec((L, ELT_W), lambda j: (0, GDN_WIDTH // ELT_W + j)), ANY],
        out_specs=[pl.BlockSpec((L, CONV_BLOCK), lambda j: (0, OFF_CONV // CONV_BLOCK + j)),
                   pl.BlockSpec((3, ELT_W), lambda j: (0, j)), pl.BlockSpec((1, ELT_W), lambda j: (0, j))],
        out_shape=[jax.ShapeDtypeStruct(dproj.shape, dproj.dtype), jax.ShapeDtypeStruct((3, CONV_WIDTH), f32),
                   jax.ShapeDtypeStruct((1, CONV_WIDTH), f32)],
        input_output_aliases={4: 0},
        compiler_params=_cparams("parallel"),
    )(proj, cw, cb, dmix_b, dproj)


def _gdn_bwd(qkv, sc, gr, u_all, w_all, vn_all, t_all, sp_all, do_all):
    L = qkv.shape[0]
    nc = L // CHUNK
    W = GDN_WIDTH
    cps = GDN_CPS_BWD if nc % GDN_CPS_BWD == 0 else 1
    rows_per_step = cps * CHUNK
    nsteps = nc // cps

    def body(qkv_ref, sc_ref, gr_ref, u_ref, w_ref, vn_ref, t_ref, sp_ref, do_ref, dqkv_ref, dsc_ref, dgr_ref, ds_scr):
        @pl.when(pl.program_id(0) == 0)
        def _():
            ds_scr[...] = jnp.zeros_like(ds_scr)
        nh, base = HEADS, 0
        HS = range(cps * nh)
        hl = [i % nh for i in HS]
        hd = [base + hl[i] for i in HS]
        rs = [slice((i // nh) * CHUNK, (i // nh + 1) * CHUNK) for i in HS]
        cs = [slice(hd[i] * HEAD_DIM, (hd[i] + 1) * HEAD_DIM) for i in HS]
        q = [qkv_ref[rs[i], hd[i] * HEAD_DIM:(hd[i] + 1) * HEAD_DIM] for i in HS]
        k = [qkv_ref[rs[i], W + hd[i] * HEAD_DIM:W + (hd[i] + 1) * HEAD_DIM] for i in HS]
        v = [qkv_ref[rs[i], 2 * W + hd[i] * HEAD_DIM:2 * W + (hd[i] + 1) * HEAD_DIM] for i in HS]
        hsc = [_head_scalars(sc_ref[rs[i], :], gr_ref, hd[i], i // nh) for i in HS]
        beta, gcc, gl, dmat, dmat_t = ([x[i] for x in hsc] for i in range(5))
        ii, jj = hsc[0][5], hsc[0][6]
        eg = [jnp.exp(gcc[h]) for h in HS]
        ekl = [jnp.exp(gl[h] - gcc[h]) for h in HS]
        egl = [jnp.exp(gl[h]) for h in HS]
        kb = [k[h] * beta[h] for h in HS]
        ks = [k[h] * ekl[h] for h in HS]
        do = [do_ref[rs[h], cs[h]] for h in HS]
        vn = [vn_ref[rs[h], cs[h]] for h in HS]
        s = [sp_ref[h // nh, cs[h], :] for h in HS]
        w = [w_ref[rs[h], cs[h]] for h in HS]
        qd = [q[h] * eg[h] for h in HS]

        kq = [_mm_nt(k[h], q[h]) for h in HS]
        p_t = [jnp.where(jj >= ii, kq[h] * dmat_t[h], 0.0) for h in HS]
        ptd = [_mm(p_t[h], do[h]) for h in HS]
        qw = [_cat16([qd[h], -w[h]], 0) for h in HS]
        dsn, dvn, dodv = [None] * len(HS), [None] * len(HS), [None] * len(HS)
        ds_cur = [ds_scr[base + h] for h in range(nh)]
        for ci in reversed(range(cps)):
            IS = range(ci * nh, (ci + 1) * nh)
            ksd = [_mm(ks[i], ds_cur[hl[i]]) for i in IS]
            for i in IS:
                dsn[i] = ds_cur[hl[i]]
                dvn[i] = ptd[i] + ksd[hl[i]]
                dodv[i] = _cat16([do[i], dvn[i]], 0)
            dsq = [_mm_tn(qw[i], dodv[i]) for i in IS]
            ds_cur = [egl[i] * ds_cur[hl[i]] + dsq[hl[i]] for i in IS]
        for h in range(nh):
            ds_scr[base + h] = ds_cur[h]
        x1 = [_mm_nt(dodv[h], s[h]) for h in HS]
        dks = [_mm_nt(vn[h], dsn[h]) for h in HS]
        dov = [_mm_nt(do[h], vn[h]) for h in HS]
        vdo = [_mm_nt(vn[h], do[h]) for h in HS]
        kk = [_mm_nt(kb[h], k[h]) for h in HS]
        qk = [_mm_nt(q[h], k[h]) for h in HS]
        dgl = [egl[h] * jnp.sum(jnp.sum(s[h] * dsn[h], axis=1, keepdims=True), axis=0, keepdims=True) for h in HS]
        dqd = [x1[h][:CHUNK] for h in HS]
        duw = [jnp.concatenate([dvn[h], -x1[h][CHUNK:]], axis=1) for h in HS]
        tdu = [_mm_tn(t_ref[h // nh, hd[h]], duw[h]) for h in HS]
        dvk = [duw[h] + tdu[h] for h in HS]
        uw = [jnp.concatenate([u_ref[rs[h], cs[h]], w[h]], axis=1) for h in HS]
        da = [-jnp.where(ii > jj, _mm_nt(dvk[h], uw[h]), 0.0) for h in HS]
        da_t = [-jnp.where(jj > ii, _mm_nt(uw[h], dvk[h]), 0.0) for h in HS]
        dp = [jnp.where(ii >= jj, dov[h], 0.0) for h in HS]
        dp_t = [jnp.where(jj >= ii, vdo[h], 0.0) for h in HS]
        r1 = [_mm(_cat16([da[h] * dmat[h], dp[h] * dmat[h]], 0), k[h]) for h in HS]
        dk1 = [_mm(_cat16([da_t[h] * dmat_t[h], dp_t[h] * dmat_t[h]], 1), _cat16([kb[h], q[h]], 0)) for h in HS]
        lane = _lanes((CHUNK, LANE))
        for ci in range(cps):
            dsc = jnp.zeros((CHUNK, LANE), f32)
            for i in range(ci * nh, (ci + 1) * nh):
                h = hd[i]
                a = jnp.where(ii > jj, kk[i] * dmat[i], 0.0)
                p = jnp.where(ii >= jj, qk[i] * dmat[i], 0.0)
                gmat = da[i] * a + dp[i] * p
                dvb, dkbg = dvk[i][:, :HEAD_DIM], dvk[i][:, HEAD_DIM:]
                kbg = kb[i] * eg[i]
                dkb = r1[i][:CHUNK] + dkbg * eg[i]
                dq = r1[i][CHUNK:] + dqd[i] * eg[i]
                dk = dk1[i] + dks[i] * ekl[i] + dkb * beta[i]
                dbeta = jnp.sum(dkb * k[i] + dvb * v[i], axis=1, keepdims=True)
                ksum = jnp.sum(dks[i] * ks[i], axis=1, keepdims=True)
                dgl_tot = dgl[i] + jnp.sum(ksum, axis=0, keepdims=True)
                dgc = (jnp.sum(gmat, axis=1, keepdims=True) + jnp.sum(dqd[i] * qd[i] + dkbg * kbg, axis=1, keepdims=True)
                       - ksum)
                dgc = dgc + jnp.where(_rows(dgc.shape) == CHUNK - 1, dgl_tot, 0.0)
                dqkv_ref[rs[i], h * HEAD_DIM:(h + 1) * HEAD_DIM] = dq
                dqkv_ref[rs[i], W + h * HEAD_DIM:W + (h + 1) * HEAD_DIM] = dk
                dqkv_ref[rs[i], 2 * W + h * HEAD_DIM:2 * W + (h + 1) * HEAD_DIM] = dvb * beta[i]
                dsc = jnp.where(lane == h, dbeta, jnp.where(lane == HEADS + h, dgc, dsc))
                dgr_ref[ci, h:h + 1, :] = jnp.sum(gmat, axis=0, keepdims=True)
            dsc_ref[ci * CHUNK:(ci + 1) * CHUNK, :] = dsc

    row = lambda c: (nsteps - 1 - c, 0)
    lead3 = lambda c: (nsteps - 1 - c, 0, 0)
    return _pcall(
        body, name="gdn_bwd", grid=(nsteps,),
        in_specs=[pl.BlockSpec((rows_per_step, 3 * W), row), pl.BlockSpec((rows_per_step, LANE), row),
                  pl.BlockSpec((cps, HEADS, CHUNK), lead3),
                  pl.BlockSpec((rows_per_step, W), row), pl.BlockSpec((rows_per_step, W), row),
                  pl.BlockSpec((rows_per_step, W), row),
                  pl.BlockSpec((cps, HEADS, CHUNK, CHUNK), lambda c: (nsteps - 1 - c, 0, 0, 0)),
                  pl.BlockSpec((cps, W, HEAD_DIM), lead3), pl.BlockSpec((rows_per_step, W), row)],
        out_specs=[pl.BlockSpec((rows_per_step, 3 * W), row), pl.BlockSpec((rows_per_step, LANE), row),
                   pl.BlockSpec((cps, HEADS, CHUNK), lead3)],
        out_shape=[jax.ShapeDtypeStruct((L, 3 * W), f32), jax.ShapeDtypeStruct((L, LANE), f32),
                   jax.ShapeDtypeStruct((nc, HEADS, CHUNK), f32)],
        scratch_shapes=[pltpu.VMEM((HEADS, HEAD_DIM, HEAD_DIM), f32)],
        compiler_params=_cparams("arbitrary"),
    )(qkv, sc, gr, u_all, w_all, vn_all, t_all, sp_all, do_all)


def _qkv_bwd(proj, cw, dn, dproj):
    L = proj.shape[0]

    def body(x_ref, cw_ref, dn_ref, dproj_in, dx_ref, gcw_ref):
        j = pl.program_id(0)
        steps = GDN_WIDTH // ELT_W
        scale = jnp.where(j < steps, HEAD_DIM ** -0.5, 1.0).astype(f32)
        for ls in HALVES:
            x, dn_v = x_ref[:, ls], dn_ref[:, ls]
            c = _conv4(x, cw_ref, ls)
            sg = _sigmoid(c)
            a = c * sg
            rn = lax.rsqrt(jnp.sum(a * a, axis=1, keepdims=True) + EPS)
            da_n = (scale * rn) * (dn_v - a * ((rn * rn) * jnp.sum(dn_v * a, axis=1, keepdims=True)))
            da = jnp.where(j < 2 * steps, da_n, dn_v)
            dc = da * (sg * (1.0 + c * (1.0 - sg)))
            dc1, dc2, dc3 = _shift_up(dc, 1), _shift_up(dc, 2), _shift_up(dc, 3)
            gcw_ref[3:4, ls] = jnp.sum(dc * x, axis=0, keepdims=True)
            gcw_ref[2:3, ls] = jnp.sum(dc1 * x, axis=0, keepdims=True)
            gcw_ref[1:2, ls] = jnp.sum(dc2 * x, axis=0, keepdims=True)
            gcw_ref[0:1, ls] = jnp.sum(dc3 * x, axis=0, keepdims=True)
            dx = cw_ref[3:4, ls] * dc + cw_ref[2:3, ls] * dc1 + cw_ref[1:2, ls] * dc2 + cw_ref[0:1, ls] * dc3
            dx_ref[:, ls] = dx.astype(bf16)

    col = pl.BlockSpec((L, ELT_W), lambda j: (0, j))
    wspec = pl.BlockSpec((4, ELT_W), lambda j: (0, j))
    return _pcall(
        body, name="qkv_bwd", grid=(3 * GDN_WIDTH // ELT_W,),
        in_specs=[col, wspec, col, ANY], out_specs=[col, wspec],
        out_shape=[jax.ShapeDtypeStruct(dproj.shape, dproj.dtype), jax.ShapeDtypeStruct((4, 3 * GDN_WIDTH), f32)],
        input_output_aliases={3: 0},
        compiler_params=_cparams("parallel"),
    )(proj, cw, dn, dproj)


def _scalars_bwd(proj, alog_p, dtb_p, dsc, dgr_col, dproj, after):
    L = proj.shape[0]

    def body(x_ref, al_ref, dt_ref, dsc_ref, dgr_ref, dproj_in, after_ref, dba_ref, gs_ref):
        x, dsc_v = x_ref[...], dsc_ref[...]
        lane = _lanes(x.shape)
        dec = (lane >= HEADS) & (lane < 2 * HEADS)
        dg = jnp.where(dec, dsc_v - dgr_ref[...], 0.0)
        rc = _rows(x.shape) & (CHUNK - 1)
        for s in (1, 2, 4, 8, 16, 32):
            dg = dg + jnp.where(rc + s < CHUNK, pltpu.roll(dg, L - s, 0), 0.0)
        xa = x + dt_ref[...]
        ea = jnp.exp(al_ref[...])
        g = -ea * _softplus(xa)
        da = dg * (-ea) * _sigmoid(xa)
        beta = _sigmoid(x)
        db = dsc_v * beta * (1.0 - beta)
        dba_ref[:, :LANE] = jnp.where(lane < HEADS, db, jnp.where(dec, da, 0.0)).astype(bf16)
        dba_ref[:, LANE:] = jnp.zeros((L, ELT_W - LANE), bf16)
        g_al = jnp.sum(jnp.where(dec, dg * g, 0.0), axis=0, keepdims=True)
        g_dt = jnp.sum(jnp.where(dec, da, 0.0), axis=0, keepdims=True)
        row8 = _rows(gs_ref.shape)
        gs = jnp.where(row8 == 0, g_al, jnp.where(row8 == 1, g_dt, 0.0))
        gs_ref[...] = pltpu.roll(gs, LANE - HEADS, 1)

    full = pl.BlockSpec((L, LANE), lambda i: (0, 0))
    vec = pl.BlockSpec((1, LANE), lambda i: (0, 0))
    return _pcall(
        body, name="scalars_bwd", grid=(1,),
        in_specs=[pl.BlockSpec((L, LANE), lambda i: (0, OFF_BA // LANE)), vec, vec, full, full, ANY, ANY],
        out_specs=[pl.BlockSpec((L, ELT_W), lambda i: (0, OFF_BA // ELT_W)), pl.BlockSpec((8, LANE), lambda i: (0, 0))],
        out_shape=[jax.ShapeDtypeStruct(dproj.shape, dproj.dtype), jax.ShapeDtypeStruct((8, LANE), f32)],
        input_output_aliases={5: 0},
        compiler_params=_cparams("arbitrary"),
    )(proj, alog_p, dtb_p, dsc, dgr_col, dproj, after)


def _input_grad(dproj, wpad, x, nw, dy, after):
    L = x.shape[0]
    tm = min(512, L)
    cuts = (0, 512, 1024, 2048, 3072, 5120, 7168, PROJ_PAD)
    nk = len(cuts) - 1

    def body(dp_ref, w_hbm, x_ref, nw_ref, dy_ref, after_ref, gx_ref, gnw_ref, w_vmem, sems):
        first = pl.program_id(0) == 0
        loads = [pltpu.make_async_copy(w_hbm.at[cuts[k]:cuts[k + 1], :], w_vmem.at[cuts[k]:cuts[k + 1], :], sems.at[k])
                 for k in range(nk)]

        @pl.when(first)
        def _():
            for cp in loads:
                cp.start()
            gnw_ref[...] = jnp.zeros_like(gnw_ref)
        dh = None
        for k in range(nk):
            pl.when(first)(loads[k].wait)
            part = jnp.dot(dp_ref[:, cuts[k]:cuts[k + 1]], w_vmem[cuts[k]:cuts[k + 1], :], preferred_element_type=f32)
            dh = part if dh is None else dh + part
        xv, nwv = x_ref[...], nw_ref[...]
        r = lax.rsqrt(jnp.mean(xv * xv, axis=-1, keepdims=True) + EPS)
        xh = xv * r
        gnw_ref[...] += jnp.sum(dh * xh, axis=0, keepdims=True)
        dxh = dh * nwv
        gx_ref[...] = dy_ref[...] + r * (dxh - xh * jnp.mean(dxh * xh, axis=-1, keepdims=True))

    row = lambda i: (i, 0)
    fix = lambda i: (0, 0)
    return _pcall(
        body, name="input_grad", grid=(L // tm,),
        in_specs=[pl.BlockSpec((tm, PROJ_PAD), row), ANY, pl.BlockSpec((tm, D_MODEL), row),
                  pl.BlockSpec((1, D_MODEL), fix), pl.BlockSpec((tm, D_MODEL), row), ANY],
        out_specs=[pl.BlockSpec((tm, D_MODEL), row), pl.BlockSpec((1, D_MODEL), fix)],
        out_shape=[jax.ShapeDtypeStruct((L, D_MODEL), f32), jax.ShapeDtypeStruct((1, D_MODEL), f32)],
        scratch_shapes=[pltpu.VMEM(wpad.shape, bf16), pltpu.SemaphoreType.DMA((nk,))],
        compiler_params=_cparams("arbitrary"),
    )(dproj, wpad, x, nw, dy, after)


def _adamw_reduce(parts, w, m, v, name, first_row=None):
    R, C = w.shape[0], w.shape[-1]
    n_parts = parts.shape[0]
    tr = 128 if R % 128 == 0 else R
    c1 = 1.0 - ADAM_B1 ** ADAM_STEP
    c2 = 1.0 - ADAM_B2 ** ADAM_STEP
    at = (slice(None), 0, slice(None)) if w.ndim == 3 else Ellipsis
    window = (slice(None), slice(None)) if first_row is None else (slice(first_row, first_row + R), slice(0, C))

    def body(p_ref, w_ref, m_ref, v_ref, g_ref, d_ref, nm_ref, nv_ref):
        g = p_ref[(0,) + window].astype(f32)
        for s in range(1, n_parts):
            g = g + p_ref[(s,) + window].astype(f32)
        nm = ADAM_B1 * m_ref[at] + (1.0 - ADAM_B1) * g
        nv = ADAM_B2 * v_ref[at] + (1.0 - ADAM_B2) * (g * g)
        g_ref[at] = g
        nm_ref[at] = nm
        nv_ref[at] = nv
        d_ref[at] = -ADAM_LR * ((nm / c1) / (jnp.sqrt(nv / c2) + ADAM_EPS) + ADAM_WD * w_ref[at])

    blk = pl.BlockSpec((tr, 1, C), lambda i: (i, 0, 0)) if w.ndim == 3 else pl.BlockSpec((tr, C), lambda i: (i, 0))
    out = jax.ShapeDtypeStruct(w.shape, f32)
    if first_row is None:
        p_spec = pl.BlockSpec((n_parts, tr, C), lambda i: (0, i, 0))
    else:
        assert tr == R
        p_spec = pl.BlockSpec(parts.shape, lambda i: (0, 0, 0))
    return _pcall(
        body, name=name, grid=(R // tr,),
        in_specs=[p_spec, blk, blk, blk],
        out_specs=[blk] * 4, out_shape=[out] * 4,
        compiler_params=_cparams("parallel"),
    )(parts, w, m, v)


SMALL_SLOTS = ((0, D_MODEL), (D_MODEL, D_MODEL), (2 * D_MODEL, D_MODEL), (3 * D_MODEL, LANE),
               (3 * D_MODEL + LANE, HEADS), (3 * D_MODEL + 2 * LANE, HEADS))
SMALL_LOSS = 3 * D_MODEL + 3 * LANE
SMALL_W = SMALL_LOSS + LANE


def _pack_small(gs, after):
    def body(nw_ref, cb_ref, fw_ref, gn_ref, sc_ref, ls_ref, after_ref, o_ref):
        for ref, (start, width) in zip((nw_ref, cb_ref, fw_ref, gn_ref), SMALL_SLOTS[:4]):
            o_ref[:, start:start + width] = ref[...]
        o_ref[:, SMALL_SLOTS[4][0]:SMALL_SLOTS[4][0] + LANE] = sc_ref[0:1, :]
        o_ref[:, SMALL_SLOTS[5][0]:SMALL_SLOTS[5][0] + LANE] = sc_ref[1:2, :]
        o_ref[:, SMALL_LOSS:SMALL_W] = ls_ref[...]

    vm = pl.BlockSpec(memory_space=pltpu.VMEM)
    return _pcall(body, name="pack_small_grads", out_shape=jax.ShapeDtypeStruct((1, SMALL_W), f32),
                  in_specs=[vm] * 6 + [ANY], out_specs=vm)(*gs, after)


def _adamw_small(parts, ws, ms, vs):
    c1 = 1.0 - ADAM_B1 ** ADAM_STEP
    c2 = 1.0 - ADAM_B2 ** ADAM_STEP
    np_ = len(ws)

    def body(*refs):
        p_ref = refs[0]
        w_refs, m_refs, v_refs = refs[1:1 + np_], refs[1 + np_:1 + 2 * np_], refs[1 + 2 * np_:1 + 3 * np_]
        outs = refs[1 + 3 * np_:]
        g_refs, d_refs, nm_refs, nv_refs = (outs[i * np_:(i + 1) * np_] for i in range(4))
        loss_ref = outs[4 * np_]

        def total(start, width):
            t = p_ref[0, :, start:start + width]
            for s in range(1, N_DEV):
                t = t + p_ref[s, :, start:start + width]
            return t

        for i, (start, width) in enumerate(SMALL_SLOTS):
            g = total(start, width)
            nm = ADAM_B1 * m_refs[i][...] + (1.0 - ADAM_B1) * g
            nv = ADAM_B2 * v_refs[i][...] + (1.0 - ADAM_B2) * (g * g)
            g_refs[i][...] = g
            nm_refs[i][...] = nm
            nv_refs[i][...] = nv
            d_refs[i][...] = -ADAM_LR * ((nm / c1) / (jnp.sqrt(nv / c2) + ADAM_EPS) + ADAM_WD * w_refs[i][...])
        loss_ref[...] = total(SMALL_LOSS, LANE)

    vm = pl.BlockSpec(memory_space=pltpu.VMEM)
    shapes = [jax.ShapeDtypeStruct(w.shape, f32) for w in ws]
    res = _pcall(body, name="adamw_small", out_shape=shapes * 4 + [jax.ShapeDtypeStruct((1, LANE), f32)],
                 in_specs=[vm] * (1 + 3 * np_), out_specs=[vm] * (4 * np_ + 1))(parts, *ws, *ms, *vs)
    return [res[i * np_:(i + 1) * np_] for i in range(4)], res[4 * np_]


def _adamw_w_in(part_a, part_b, w3, m3, v3, after):
    _, n, _ = part_a.shape
    c1 = 1.0 - ADAM_B1 ** ADAM_STEP
    c2 = 1.0 - ADAM_B2 ** ADAM_STEP

    def body(pa_ref, pb_ref, w_ref, m_ref, v_ref, after_ref, g_ref, d_ref, nm_ref, nv_ref):
        g = pa_ref[0].astype(f32) + pb_ref[0].astype(f32)
        nm = ADAM_B1 * m_ref[:, 0, :] + (1.0 - ADAM_B1) * g
        nv = ADAM_B2 * v_ref[:, 0, :] + (1.0 - ADAM_B2) * (g * g)
        g_ref[:, 0, :] = g
        nm_ref[:, 0, :] = nm
        nv_ref[:, 0, :] = nv
        d_ref[:, 0, :] = -ADAM_LR * ((nm / c1) / (jnp.sqrt(nv / c2) + ADAM_EPS) + ADAM_WD * w_ref[:, 0, :])

    tile = 2 * COL_TILE
    blk = pl.BlockSpec((n, 1, tile), lambda j: (0, 0, j))
    out = jax.ShapeDtypeStruct((n, 1, D_MODEL), f32)
    return _pcall(
        body, name="adamw_w_in", grid=(D_MODEL // tile,),
        in_specs=[pl.BlockSpec((1, n, tile), lambda j: (0, 0, j))] * 2 + [blk, blk, blk, ANY],
        out_specs=[blk] * 4, out_shape=[out] * 4,
        compiler_params=_cparams("parallel"),
    )(part_a, part_b, w3, m3, v3, after)


def _pad_lanes(vec8, start):
    return jnp.pad(vec8.reshape(1, -1), ((0, 0), (start, LANE - start - vec8.size)))


def kernel(x, norm_in_w, w_in, conv_qkv_w, A_log, dt_bias, gdn_norm_w, conv_w, conv_b, w_out, final_norm_w, loss_target, m_norm_in_w, m_w_in, m_conv_qkv_w, m_A_log, m_dt_bias, m_gdn_norm_w, m_conv_w, m_conv_b, m_w_out, m_final_norm_w, v_norm_in_w, v_w_in, v_conv_qkv_w, v_A_log, v_dt_bias, v_gdn_norm_w, v_conv_w, v_conv_b, v_w_out, v_final_norm_w):
    L = x.shape[1]
    nc = L // CHUNK
    xs = x[0]
    tgt = loss_target[0]
    fnw = final_norm_w.reshape(1, D_MODEL)

    as_rows = lambda a: jnp.transpose(a, (2, 0, 1))
    as_taps = lambda a: jnp.transpose(a, (1, 0, 2))
    win_blk, wo_blk = _cast_weights(as_rows(w_in), w_out[0])
    win_g, cqkv_g, cw_g = _all_gather([win_blk, conv_qkv_w[0], as_taps(conv_w)], "gather_weights",
                                      pieces=[8, 1, 1])
    wpad = _relayout_w_in(win_g)
    cqkv = jnp.concatenate([cqkv_g[d] for d in range(N_DEV)], axis=1)
    cw = jnp.concatenate([cw_g[d][:, 0, :] for d in range(N_DEV)], axis=1)
    alog_p = _pad_lanes(A_log, HEADS)
    dtb_p = _pad_lanes(dt_bias, HEADS)
    tok = lambda started: started[4]
    wo_started = _spread_start(wo_blk, wpad, "gather", "gather_w_out_start")

    proj, h = _in_proj(xs, norm_in_w, wpad, tok(wo_started))
    qkv = _qkv_act(proj, cqkv)
    sc, gr = _scalars(proj, alog_p, dtb_p)
    o, u_all, w_all, vn_all, t_all, sp_all = _gdn_fwd(qkv, sc, gr)
    mix = _conv_fwd(proj, cw, conv_b, _gdn_gate(o, proj, gdn_norm_w))
    wo = _spread_wait(wo_started, mix, "gather", "gather_w_out_wait")[1].reshape(-1, D_MODEL)
    dy, dyb, dmix, g_fnw, loss_v = _out_proj_loss(xs, mix, wo, fnw, tgt)

    g_wout = _tn_matmul(mix, dyb, "grad_w_out")
    gwo_started = _spread_start(g_wout.reshape(N_DEV, -1, D_MODEL), dyb, "scatter", "exchange_grad_w_out_start")
    do, dproj, g_gnw = _gdn_gate_bwd(o, proj, gdn_norm_w, dmix, tok(gwo_started))
    dproj, g_cw, g_cb = _conv_bwd(proj, cw, conv_b, dmix, dproj)
    dqkv_n, dsc, dgr = _gdn_bwd(qkv, sc, gr, u_all, w_all, vn_all, t_all, sp_all, do)
    dproj, g_cqkv = _qkv_bwd(proj, cqkv, dqkv_n, dproj)
    g_cqkv_blk = g_cqkv.reshape(4, N_DEV, -1).transpose(1, 0, 2)
    g_cw_blk = jnp.pad(g_cw.reshape(3, N_DEV, -1).transpose(1, 0, 2),
                       ((0, 0), (0, 1), (0, g_cqkv_blk.shape[2] - g_cw.shape[1] // N_DEV)))
    gsm_started = _spread_start(jnp.concatenate([g_cqkv_blk, g_cw_blk], axis=1), g_cqkv, "scatter",
                                "exchange_small_sharded_grads_start")
    dgr_col = jnp.pad(dgr.transpose(0, 2, 1).reshape(L, HEADS), ((0, 0), (HEADS, LANE - 2 * HEADS)))
    dproj, g_sc = _scalars_bwd(proj, alog_p, dtb_p, dsc, dgr_col, dproj, tok(gsm_started))
    g_win_blk = _grad_blocks(_tn_matmul(dproj, h, "grad_w_in"))

    (p_win,) = _pair_exchange([g_win_blk], "exchange_grads_pair")
    r_small = _spread_wait(gsm_started, p_win, "scatter", "exchange_small_sharded_grads_wait")[1]
    s_win = _pair_sum(g_win_blk, p_win, "pair_sum_w_in")
    gw1_started = _spread_start(s_win, r_small, "axis_a", "exchange_grads_axis1_start")
    grad_x, g_nw = _input_grad(dproj, wpad, xs, norm_in_w, dy, tok(gw1_started))
    s_thru, got1 = _spread_wait(gw1_started, grad_x, "axis_a", "exchange_grads_axis1_wait")
    t_win = _axis_sum(s_thru, got1, "axis_sum_w_in")
    gw2_started = _spread_start(t_win, got1, "axis_b", "exchange_grads_axis2_start")

    r_wout = _spread_wait(gwo_started, tok(gw2_started), "scatter", "exchange_grad_w_out_wait")[1]
    upd_wout =_adamw_reduce(r_wout, w_out[0], m_w_out[0], v_w_out[0], "adamw_w_out")
    upd_cqkv = _adamw_reduce(r_small, conv_qkv_w[0], m_conv_qkv_w[0], v_conv_qkv_w[0], "adamw_conv_qkv_w", first_row=0)
    upd_cw = _adamw_reduce(r_small, as_taps(conv_w), as_taps(m_conv_w), as_taps(v_conv_w), "adamw_conv_w", first_row=4)

    t_thru, got2 = _spread_wait(gw2_started, upd_cw[0], "axis_b", "exchange_grads_axis2_wait")

    small_g = _pack_small([g_nw, g_cb, g_fnw, g_gnw, g_sc, loss_v], got2)
    gsg_started = _spread_start(small_g, got2, "gather", "gather_small_grads_start")
    upd_win_t = _adamw_w_in(t_thru, got2, as_rows(w_in), as_rows(m_w_in), as_rows(v_w_in), tok(gsg_started))
    upd_win = [jnp.transpose(a, (1, 2, 0)) for a in upd_win_t]
    small_all = _spread_wait(gsg_started, upd_win_t[0], "gather", "gather_small_grads_wait")[1]
    fvec = lambda a: a.reshape(1, D_MODEL)
    upd_small, loss_sum = _adamw_small(
        small_all,
        [norm_in_w, conv_b, fvec(final_norm_w), gdn_norm_w, A_log, dt_bias],
        [m_norm_in_w, m_conv_b, fvec(m_final_norm_w), m_gdn_norm_w, m_A_log, m_dt_bias],
        [v_norm_in_w, v_conv_b, fvec(v_final_norm_w), v_gdn_norm_w, v_A_log, v_dt_bias])

    outs = [loss_sum[0, 0], grad_x[None]]
    for k in range(4):
        nw_k, cb_k, fw_k, gn_k, al_k, dt_k = upd_small[k]
        outs += [nw_k, upd_win[k], upd_cqkv[k][None], al_k, dt_k, gn_k,
                 as_taps(upd_cw[k]), cb_k, upd_wout[k][None], fw_k.reshape(D_MODEL)]
    return tuple(outs)
```

```python
import jax
import jax.numpy as jnp
from jax import lax
from jax.experimental import pallas as pl
from jax.experimental.pallas import tpu as pltpu

f32 = jnp.float32
bf16 = jnp.bfloat16

N_DEV = 8
D_MODEL = 1024
HEADS = 8
HEAD_DIM = 128
CHUNK = 64
GDN_CPS = 4
GDN_CPS_BWD = 2
GDN_WIDTH = HEADS * HEAD_DIM
CONV_WIDTH = 1024
PROJ_WIDTH = 8208
SHARD_W = PROJ_WIDTH // N_DEV
EPS = 1e-6

LANE = 128
ELT_W = 256

OFF_QKV, OFF_ZG, OFF_CONV, OFF_BA = 0, 3072, 4096, 8192
CONV_BLOCK = 4 * ELT_W
PROJ_PAD = 8448
NAT_BA, NAT_CONV = 4096, 4112


def _padded_col(n):
    if n < NAT_BA:
        return n
    if n < NAT_CONV:
        return OFF_BA + n - NAT_BA
    g, ch = divmod(n - NAT_CONV, CONV_WIDTH)
    j, r = divmod(ch, ELT_W)
    return OFF_CONV + CONV_BLOCK * j + ELT_W * g + r


def _layout_segments(n0, n1):
    cuts = [NAT_BA, NAT_CONV] + [NAT_CONV + ELT_W * k for k in range(1, 4 * CONV_WIDTH // ELT_W)]
    pts = [n0] + [c for c in cuts if n0 < c < n1] + [n1]
    return [(lo, hi - lo, _padded_col(lo)) for lo, hi in zip(pts, pts[1:])]

ADAM_LR, ADAM_B1, ADAM_B2, ADAM_EPS, ADAM_WD, ADAM_STEP = 0.001, 0.9, 0.999, 1e-08, 0.01, 10

V7X_VMEM_BYTES = 64 * 1024 * 1024
VMEM_LIMIT = V7X_VMEM_BYTES - 8 * 1024 * 1024

MESH = pl.DeviceIdType.MESH
ANY = pl.BlockSpec(memory_space=pl.ANY)


def _pcall(body, **kw):
    return pl.pallas_call(body, **kw)


def _cparams(*sem):
    return pltpu.CompilerParams(dimension_semantics=sem if sem else None, vmem_limit_bytes=VMEM_LIMIT)


def _mm(a, b):
    return jnp.dot(a.astype(bf16), b.astype(bf16), preferred_element_type=f32)


def _mm_nt(a, b):
    return lax.dot_general(a.astype(bf16), b.astype(bf16), (((1,), (1,)), ((), ())), preferred_element_type=f32)


def _cat16(parts, axis):
    return jnp.concatenate([p.astype(bf16) for p in parts], axis=axis)


def _mm_tn(a, b):
    return lax.dot_general(a.astype(bf16), b.astype(bf16), (((0,), (0,)), ((), ())), preferred_element_type=f32)


def _rows(shape):
    return lax.broadcasted_iota(jnp.int32, shape, 0)


def _lanes(shape):
    return lax.broadcasted_iota(jnp.int32, shape, 1)


def _shift_down(x, s):
    if s == 0:
        return x
    return jnp.where(_rows(x.shape) >= s, pltpu.roll(x, s, 0), 0.0)


def _shift_up(x, s):
    if s == 0:
        return x
    n = x.shape[0]
    return jnp.where(_rows(x.shape) < n - s, pltpu.roll(x, n - s, 0), 0.0)


def _sigmoid(x):
    return jax.nn.sigmoid(x)


def _softplus(x):
    e = jnp.exp(-jnp.abs(x))
    small = e * (1.0 - e * (0.5 - e * (1.0 / 3.0)))
    return jnp.maximum(x, 0.0) + jnp.where(e < 0.01, small, jnp.log(1.0 + e))


def _mesh_pos():
    return lax.axis_index("x"), lax.axis_index("y"), lax.axis_index("c")


def _flat(px, py, pc):
    return 4 * px + 2 * py + pc


def _all_gather(xs, name, pieces=None):
    n = len(xs)
    pieces = pieces or [1] * n
    items = [(a, q) for a in range(n) for q in range(pieces[a])]
    ni = len(items)

    def view(ref, i):
        a, q = items[i]
        if pieces[a] == 1:
            return ref
        wd = xs[a].shape[-1] // pieces[a]
        return ref.at[(slice(None),) * (xs[a].ndim - 1) + (pl.ds(q * wd, wd),)]

    def body(*refs):
        x_refs, o_refs = refs[:n], refs[n:2 * n]
        send_sems, recv_sems, local_sems = refs[2 * n:]
        x, y, c = _mesh_pos()
        me, sibling = (x, y, c), (x, y, 1 - c)
        flip = lambda v, bit: v + bit - 2 * v * bit
        nbr_a = (flip(x, 1 - c), flip(y, c))
        nbr_b = (flip(x, c), flip(y, 1 - c))
        diag = (1 - x, 1 - y)

        def copy(i, k, block, to, own=False):
            a = items[i][0]
            dst = view(o_refs[a].at[_flat(*block)], i)
            return pltpu.make_async_remote_copy(
                src_ref=view(x_refs[a], i) if own else dst, dst_ref=dst,
                send_sem=send_sems.at[i, k], recv_sem=recv_sems.at[i, k], device_id=to, device_id_type=MESH)

        mine, sent = [], []

        def go(cp):
            cp.start()
            sent.append(cp)

        for a in range(n):
            cp = pltpu.make_async_copy(x_refs[a], o_refs[a].at[_flat(*me)], local_sems.at[a])
            cp.start()
            mine.append(cp)
        for a in range(ni):
            go(copy(a, 1, me, (*nbr_a, c), own=True))
            go(copy(a, 2, me, (*nbr_b, c), own=True))
            go(copy(a, 0, me, sibling, own=True))
        for a in range(ni):
            copy(a, 1, (*nbr_a, c), me).wait_recv()
            go(copy(a, 3, (*nbr_a, c), (*nbr_b, c)))
            go(copy(a, 4, (*nbr_a, c), sibling))
        for a in range(ni):
            copy(a, 2, (*nbr_b, c), me).wait_recv()
            go(copy(a, 5, (*nbr_b, c), sibling))
        for a in range(ni):
            copy(a, 3, (*diag, c), me).wait_recv()
            go(copy(a, 6, (*diag, c), sibling))
        for a in range(ni):
            copy(a, 0, sibling, me).wait_recv()
            copy(a, 4, (*nbr_b, 1 - c), me).wait_recv()
            copy(a, 5, (*nbr_a, 1 - c), me).wait_recv()
            copy(a, 6, (*diag, 1 - c), me).wait_recv()
        for cp in sent:
            cp.wait_send()
        for cp in mine:
            cp.wait()

    outs = _pcall(
        body, name=name,
        out_shape=[jax.ShapeDtypeStruct((N_DEV,) + a.shape, a.dtype) for a in xs],
        in_specs=[ANY] * n, out_specs=[ANY] * n,
        scratch_shapes=[pltpu.SemaphoreType.DMA((ni, 7)), pltpu.SemaphoreType.DMA((ni, 7)), pltpu.SemaphoreType.DMA((n,))],
    )(*xs)
    return list(outs)


def _pair_exchange(gs, name):
    n = len(gs)
    chips = [(0, 0), (0, 1), (1, 0), (1, 1)]

    def body(*refs):
        g_refs, o_refs = refs[:n], refs[n:2 * n]
        send_sems, recv_sems = refs[2 * n:]
        x, y, c = _mesh_pos()
        sibling = (x, y, 1 - c)

        def copy(a, i):
            xp, yp = chips[i]
            return pltpu.make_async_remote_copy(
                src_ref=g_refs[a].at[_flat(xp, yp, 1 - c)], dst_ref=o_refs[a].at[i],
                send_sem=send_sems.at[a, i], recv_sem=recv_sems.at[a, i], device_id=sibling, device_id_type=MESH)

        cps = [copy(a, i) for a in range(n) for i in range(4)]
        for cp in cps:
            cp.start()
        for cp in cps:
            cp.wait()

    outs = _pcall(
        body, name=name,
        out_shape=[jax.ShapeDtypeStruct((4,) + a.shape[1:], a.dtype) for a in gs],
        in_specs=[ANY] * n, out_specs=[ANY] * n,
        scratch_shapes=[pltpu.SemaphoreType.DMA((n, 4)), pltpu.SemaphoreType.DMA((n, 4))],
    )(*gs)
    return list(outs)


def _pair_sum(g, p1, name):
    _, R, C = g.shape
    tr = 256 if R % 256 == 0 else R
    cidx = lax.axis_index("c").astype(jnp.int32).reshape(1)

    def body(c_ref, g_ref, p_ref, o_ref):
        o_ref[...] = (g_ref[...].astype(f32) + p_ref[...].astype(f32)).astype(o_ref.dtype)

    return _pcall(
        body, name=name,
        grid_spec=pltpu.PrefetchScalarGridSpec(
            num_scalar_prefetch=1, grid=(4, R // tr),
            in_specs=[pl.BlockSpec((1, tr, C), lambda i, r, c_ref: (2 * i + c_ref[0], r, 0)),
                      pl.BlockSpec((1, tr, C), lambda i, r, c_ref: (i, r, 0))],
            out_specs=pl.BlockSpec((1, tr, C), lambda i, r, c_ref: (i, r, 0))),
        out_shape=jax.ShapeDtypeStruct((4, R, C), g.dtype),
        compiler_params=_cparams("parallel", "parallel"),
    )(cidx, g, p1)


def _axis_sum(s, got, name):
    _, R, C = s.shape
    x, y, c = _mesh_pos()
    me, _, b, _ = _axis_chips(x, y, c)
    idx = jnp.stack([2 * me[0] + me[1], 2 * b[0] + b[1]]).astype(jnp.int32)

    def body(idx_ref, s_ref, g_ref, o_ref):
        o_ref[...] = (s_ref[...].astype(f32) + g_ref[...].astype(f32)).astype(o_ref.dtype)

    return _pcall(
        body, name=name,
        grid_spec=pltpu.PrefetchScalarGridSpec(
            num_scalar_prefetch=1, grid=(2,),
            in_specs=[pl.BlockSpec((1, R, C), lambda k, idx_ref: (idx_ref[k], 0, 0)),
                      pl.BlockSpec((1, R, C), lambda k, idx_ref: (k, 0, 0))],
            out_specs=pl.BlockSpec((1, R, C), lambda k, idx_ref: (k, 0, 0))),
        out_shape=jax.ShapeDtypeStruct((2, R, C), s.dtype),
        compiler_params=_cparams("parallel"),
    )(idx, s, got)


HBM = pl.BlockSpec(memory_space=pltpu.HBM)
SEM = pl.BlockSpec(memory_space=pltpu.SEMAPHORE)
EFFECT = pltpu.SideEffectType.DATAFLOW_SIDE_EFFECTING


def _peers(x, y, c):
    out = []
    for k in range(1, N_DEV):
        kx, ky, kc = (k >> 2) & 1, (k >> 1) & 1, k & 1
        out.append(((1 - x) if kx else x, (1 - y) if ky else y, (1 - c) if kc else c))
    return out


SPREAD_COPIES = {"gather": N_DEV - 1, "scatter": N_DEV - 1, "axis_a": 2, "axis_b": 1}
SPREAD_SLOTS = {"axis_a": 2, "axis_b": 1}


def _axis_chips(x, y, c):
    flip = lambda v, bit: v + bit - 2 * v * bit
    return (x, y), (flip(x, 1 - c), flip(y, c)), (flip(x, c), flip(y, 1 - c)), (1 - x, 1 - y)


def _spread_copy(src_ref, land_ref, send_sems, recv_sems, k, plan):
    x, y, c = _mesh_pos()
    if plan in ("axis_a", "axis_b"):
        _, a, b, d = _axis_chips(x, y, c)
        chip = lambda p: 2 * p[0] + p[1]
        peer = (*(a if plan == "axis_a" else b), c)
        src = src_ref.at[chip(a) if k == 0 else chip(d)] if plan == "axis_a" else src_ref.at[1]
        slot = k
    else:
        peer = _peers(x, y, c)[k]
        src, slot = (src_ref.at[_flat(*peer)] if plan == "scatter" else src_ref), _flat(x, y, c)
    return pltpu.make_async_remote_copy(
        src_ref=src, dst_ref=land_ref.at[slot], send_sem=send_sems.at[k], recv_sem=recv_sems.at[k],
        device_id=peer, device_id_type=MESH)


def _own_copy(src_ref, land_ref, send_sems, plan):
    me = _flat(*_mesh_pos())
    return pltpu.make_async_copy(src_ref.at[me] if plan == "scatter" else src_ref, land_ref.at[me],
                                 send_sems.at[SPREAD_COPIES[plan]])


def _spread_start(src, after, plan, name):
    land_shape = (N_DEV,) + src.shape if plan == "gather" else src.shape
    if plan in SPREAD_SLOTS:
        land_shape = (SPREAD_SLOTS[plan],) + src.shape[1:]
    n_copies = SPREAD_COPIES[plan]

    def body(src_ref, land_ref, after_ref, send_sems, recv_sems, src_thru, land_thru, token):
        for k in range(n_copies):
            _spread_copy(src_ref, land_ref, send_sems, recv_sems, k, plan).start()
        if plan not in SPREAD_SLOTS:
            _own_copy(src_ref, land_ref, send_sems, plan).start()
        token[...] = jnp.zeros_like(token)

    return _pcall(
        body, name=name,
        out_shape=(pltpu.SemaphoreType.DMA((n_copies + (plan not in SPREAD_SLOTS),)), pltpu.SemaphoreType.DMA((n_copies,)),
                   pltpu.HBM(src.shape, src.dtype), pltpu.HBM(land_shape, src.dtype), jax.ShapeDtypeStruct((8, LANE), f32)),
        in_specs=(HBM, HBM, ANY), out_specs=(SEM, SEM, HBM, HBM, pl.BlockSpec(memory_space=pltpu.VMEM)),
        input_output_aliases={0: 2, 1: 3},
        compiler_params=pltpu.CompilerParams(has_side_effects=EFFECT),
    )(pltpu.with_memory_space_constraint(src, pltpu.HBM),
      pltpu.with_memory_space_constraint(lax.empty(land_shape, src.dtype), pltpu.HBM), after)


def _spread_wait(started, after, plan, name):
    send_sems, recv_sems, src_thru, land_thru, _ = started

    def body(src_ref, land_ref, send_sems, recv_sems, after_ref, src_dead, got_ref):
        for k in range(SPREAD_COPIES[plan]):
            cp = _spread_copy(src_ref, land_ref, send_sems, recv_sems, k, plan)
            cp.wait_send()
            cp.wait_recv()
        if plan not in SPREAD_SLOTS:
            _own_copy(src_ref, land_ref, send_sems, plan).wait()

    return _pcall(
        body, name=name,
        out_shape=(pltpu.HBM(src_thru.shape, src_thru.dtype), pltpu.HBM(land_thru.shape, land_thru.dtype)),
        in_specs=(HBM, HBM, SEM, SEM, ANY), out_specs=(HBM, HBM), input_output_aliases={0: 0, 1: 1},
        compiler_params=pltpu.CompilerParams(has_side_effects=EFFECT),
    )(src_thru, land_thru, send_sems, recv_sems, after)


COL_TILE = 256


def _cast_weights(w3, wo):
    n = w3.shape[0]

    def body(w_ref, wo_ref, o_ref, oo_ref):
        o_ref[...] = w_ref[:, 0, :].astype(bf16)
        oo_ref[...] = wo_ref[...].astype(bf16)

    tile = 2 * COL_TILE
    return _pcall(
        body, name="cast_weights", grid=(D_MODEL // tile,),
        in_specs=[pl.BlockSpec((n, 1, tile), lambda j: (0, 0, j)), pl.BlockSpec((wo.shape[0], tile), lambda j: (0, j))],
        out_specs=[pl.BlockSpec((n, tile), lambda j: (0, j)), pl.BlockSpec((wo.shape[0], tile), lambda j: (0, j))],
        out_shape=[jax.ShapeDtypeStruct((n, D_MODEL), bf16), jax.ShapeDtypeStruct(wo.shape, bf16)],
        compiler_params=_cparams("parallel"),
    )(w3, wo)


def _relayout_w_in(win_g):
    def body(g_ref, o_ref):
        used = OFF_BA + NAT_CONV - NAT_BA
        o_ref[used:PROJ_PAD, :] = jnp.zeros((PROJ_PAD - used, COL_TILE), o_ref.dtype)
        for d in range(N_DEV):
            for lo, width, dst in _layout_segments(d * SHARD_W, (d + 1) * SHARD_W):
                src = lo - d * SHARD_W
                o_ref[dst:dst + width, :] = g_ref[d, src:src + width, :]

    return _pcall(
        body, name="relayout_w_in", grid=(D_MODEL // COL_TILE,),
        in_specs=[pl.BlockSpec((N_DEV, SHARD_W, COL_TILE), lambda j: (0, 0, j))],
        out_specs=pl.BlockSpec((PROJ_PAD, COL_TILE), lambda j: (0, j)),
        out_shape=jax.ShapeDtypeStruct((PROJ_PAD, D_MODEL), win_g.dtype),
        compiler_params=_cparams("parallel"),
    )(win_g)


def _grad_blocks(g_t):
    def body(p_ref, o_ref):
        for d in range(N_DEV):
            for lo, width, src in _layout_segments(d * SHARD_W, (d + 1) * SHARD_W):
                dst = lo - d * SHARD_W
                o_ref[d, dst:dst + width, :] = p_ref[src:src + width, :]

    return _pcall(
        body, name="grad_blocks", grid=(D_MODEL // COL_TILE,),
        in_specs=[pl.BlockSpec((PROJ_PAD, COL_TILE), lambda j: (0, j))],
        out_specs=pl.BlockSpec((N_DEV, SHARD_W, COL_TILE), lambda j: (0, 0, j)),
        out_shape=jax.ShapeDtypeStruct((N_DEV, SHARD_W, D_MODEL), bf16),
        compiler_params=_cparams("parallel"),
    )(g_t)


def _in_proj(x, nw, wpad_t, after):
    L = x.shape[0]
    tn = 768
    nj = wpad_t.shape[0] // tn

    def body(x_ref, nw_ref, w_ref, after_ref, proj_ref, h_ref):
        first = pl.program_id(0) == 0

        def project(r, n, hv):
            proj_ref[r:r + n, :] = lax.dot_general(hv, w_ref[...], (((1,), (1,)), ((), ())), preferred_element_type=f32)

        @pl.when(first)
        def _():
            for r in range(0, L, 256):
                xs = x_ref[r:r + 256, :]
                ms = jnp.mean(xs * xs, axis=-1, keepdims=True)
                hv = ((xs * lax.rsqrt(ms + EPS)) * nw_ref[...]).astype(bf16)
                h_ref[r:r + 256, :] = hv
                project(r, 256, hv)

        @pl.when(jnp.logical_not(first))
        def _():
            for r in range(0, L, 512):
                project(r, 512, h_ref[r:r + 512, :])

    return _pcall(
        body, name="in_proj", grid=(nj,),
        in_specs=[pl.BlockSpec((L, D_MODEL), lambda j: (0, 0)), pl.BlockSpec((1, D_MODEL), lambda j: (0, 0)),
                  pl.BlockSpec((tn, D_MODEL), lambda j: (j, 0)), ANY],
        out_specs=[pl.BlockSpec((L, tn), lambda j: (0, j)), pl.BlockSpec((L, D_MODEL), lambda j: (0, 0))],
        out_shape=[jax.ShapeDtypeStruct((L, wpad_t.shape[0]), f32), jax.ShapeDtypeStruct((L, D_MODEL), bf16)],
        compiler_params=_cparams("arbitrary"),
    )(x, nw, wpad_t, after)


HALVES = [slice(i * LANE, (i + 1) * LANE) for i in range(ELT_W // LANE)]
QKV_W = 512
QKV_HEADS = [slice(i * LANE, (i + 1) * LANE) for i in range(QKV_W // LANE)]
STEPS_PER_GROUP = GDN_WIDTH // QKV_W


def _conv4(x, cw_ref, ls):
    return (cw_ref[3:4, ls] * x + cw_ref[2:3, ls] * _shift_down(x, 1) + cw_ref[1:2, ls] * _shift_down(x, 2)
            + cw_ref[0:1, ls] * _shift_down(x, 3))


def _qkv_act(proj, cw):
    L = proj.shape[0]

    def body(x_ref, cw_ref, o_ref):
        j = pl.program_id(0)
        scale = jnp.where(j < STEPS_PER_GROUP, HEAD_DIM ** -0.5, 1.0).astype(f32)
        for ls in QKV_HEADS:
            c = _conv4(x_ref[:, ls], cw_ref, ls)
            a = c * _sigmoid(c)
            rn = lax.rsqrt(jnp.sum(a * a, axis=1, keepdims=True) + EPS)
            o_ref[:, ls] = jnp.where(j < 2 * STEPS_PER_GROUP, (a * rn) * scale, a)

    return _pcall(
        body, name="qkv_act", grid=(3 * STEPS_PER_GROUP,),
        in_specs=[pl.BlockSpec((L, QKV_W), lambda j: (0, j)), pl.BlockSpec((4, QKV_W), lambda j: (0, j))],
        out_specs=pl.BlockSpec((L, QKV_W), lambda j: (0, j)),
        out_shape=jax.ShapeDtypeStruct((L, 3 * GDN_WIDTH), f32),
        compiler_params=_cparams("parallel"),
    )(proj, cw)


def _scalars(proj, alog_p, dtb_p):
    L = proj.shape[0]
    nc = L // CHUNK

    def body(x_ref, al_ref, dt_ref, sc_ref, gr_ref):
        x = x_ref[...]
        lane = _lanes(x.shape)
        beta = _sigmoid(x)
        g = -jnp.exp(al_ref[...]) * _softplus(x + dt_ref[...])
        gc = jnp.where((lane >= HEADS) & (lane < 2 * HEADS), g, 0.0)
        rc = _rows(x.shape) & (CHUNK - 1)
        for s in (1, 2, 4, 8, 16, 32):
            gc = gc + jnp.where(rc >= s, pltpu.roll(gc, s, 0), 0.0)
        sc_ref[...] = jnp.where(lane < HEADS, beta, gc)
        sel = (_lanes((HEADS, LANE)) == _rows((HEADS, LANE)) + HEADS).astype(f32)
        for c in range(nc):
            gr_ref[c] = lax.dot_general(sel, sc_ref[c * CHUNK:(c + 1) * CHUNK, :], (((1,), (1,)), ((), ())),
                                        preferred_element_type=f32, precision=lax.Precision.HIGHEST)

    return _pcall(
        body, name="scalars", grid=(1,),
        in_specs=[pl.BlockSpec((L, LANE), lambda i: (0, OFF_BA // LANE)), pl.BlockSpec((1, LANE), lambda i: (0, 0)),
                  pl.BlockSpec((1, LANE), lambda i: (0, 0))],
        out_specs=[pl.BlockSpec((L, LANE), lambda i: (0, 0)), pl.BlockSpec((nc, HEADS, CHUNK), lambda i: (0, 0, 0))],
        out_shape=[jax.ShapeDtypeStruct((L, LANE), f32), jax.ShapeDtypeStruct((nc, HEADS, CHUNK), f32)],
        compiler_params=_cparams("arbitrary"),
    )(proj, alog_p, dtb_p)


def _head_scalars(sc, gr_ref, h, ci=0):
    lane = _lanes(sc.shape)
    beta = jnp.sum(jnp.where(lane == h, sc, 0.0), axis=1, keepdims=True)
    gcc = jnp.sum(jnp.where(lane == HEADS + h, sc, 0.0), axis=1, keepdims=True)
    gcr = gr_ref[ci, h:h + 1, :]
    gl = jnp.sum(jnp.where(_lanes(gcr.shape) == CHUNK - 1, gcr, 0.0), axis=1, keepdims=True)
    ii, jj = _rows((CHUNK, CHUNK)), _lanes((CHUNK, CHUNK))
    dmat = jnp.where(ii >= jj, jnp.exp(jnp.minimum(gcc - gcr, 0.0)), 0.0)
    dmat_t = jnp.where(jj >= ii, jnp.exp(jnp.minimum(gcr - gcc, 0.0)), 0.0)
    return beta, gcc, gl, dmat, dmat_t, ii, jj


def _gdn_fwd(qkv, sc, gr):
    L = qkv.shape[0]
    nc = L // CHUNK
    W = GDN_WIDTH
    cps = GDN_CPS if nc % GDN_CPS == 0 else 1
    rows_per_step = cps * CHUNK

    def body(qkv_ref, sc_ref, gr_ref, o_ref, u_ref, w_ref, vn_ref, t_ref, sp_ref, s_scr):
        @pl.when(pl.program_id(0) == 0)
        def _():
            s_scr[...] = jnp.zeros_like(s_scr)
        HS = range(cps * HEADS)
        hd = [i % HEADS for i in HS]
        rs = [slice((i // HEADS) * CHUNK, (i // HEADS + 1) * CHUNK) for i in HS]
        cs = [slice(hd[i] * HEAD_DIM, (hd[i] + 1) * HEAD_DIM) for i in HS]
        q = [qkv_ref[rs[i], hd[i] * HEAD_DIM:(hd[i] + 1) * HEAD_DIM] for i in HS]
        k = [qkv_ref[rs[i], W + hd[i] * HEAD_DIM:W + (hd[i] + 1) * HEAD_DIM] for i in HS]
        v = [qkv_ref[rs[i], 2 * W + hd[i] * HEAD_DIM:2 * W + (hd[i] + 1) * HEAD_DIM] for i in HS]
        hsc = [_head_scalars(sc_ref[rs[i], :], gr_ref, hd[i], i // HEADS) for i in HS]
        beta, gcc, gl, dmat = ([x[i] for x in hsc] for i in range(4))
        ii, jj = hsc[0][5], hsc[0][6]
        eg = [jnp.exp(gcc[h]) for h in HS]
        kb = [k[h] * beta[h] for h in HS]
        kk = [_mm_nt(kb[h], k[h]) for h in HS]
        qk = [_mm_nt(q[h], k[h]) for h in HS]
        n0 = [-jnp.where(ii > jj, kk[h] * dmat[h], 0.0) for h in HS]
        n1 = [_mm(n0[h], n0[h]) for h in HS]
        n2 = [_mm(n1[h], n1[h]) for h in HS]
        p01 = [n0[h] + n1[h] + _mm(n0[h], n1[h]) for h in HS]
        n3 = [_mm(n2[h], n2[h]) for h in HS]
        n4 = [_mm(n3[h], n3[h]) for h in HS]
        p23 = [n2[h] + n3[h] + _mm(n2[h], n3[h]) for h in HS]
        n5 = [_mm(n4[h], n4[h]) for h in HS]
        p03 = [p01[h] + p23[h] + _mm(p01[h], p23[h]) for h in HS]
        p45 = [n4[h] + n5[h] + _mm(n4[h], n5[h]) for h in HS]
        t = [p03[h] + p45[h] + _mm(p03[h], p45[h]) for h in HS]
        vb = [v[h] * beta[h] for h in HS]
        kbg = [kb[h] * eg[h] for h in HS]
        uw = [_mm(t[h], _cat16([vb[h], kbg[h]], 1)) for h in HS]
        u = [vb[h] + uw[h][:, :HEAD_DIM] for h in HS]
        w = [kbg[h] + uw[h][:, HEAD_DIM:] for h in HS]
        wq = [_cat16([w[h], q[h] * eg[h]], 0) for h in HS]
        p = [jnp.where(ii >= jj, qk[h] * dmat[h], 0.0) for h in HS]
        ks = [k[h] * jnp.exp(gl[h] - gcc[h]) for h in HS]
        s = [s_scr[h] for h in range(HEADS)]
        for ci in range(cps):
            IS = range(ci * HEADS, (ci + 1) * HEADS)
            ws = [_mm(wq[i], s[hd[i]]) for i in IS]
            vn = [u[i] - ws[hd[i]][:CHUNK] for i in IS]
            pv = [_mm(p[i], vn[hd[i]]) for i in IS]
            kv = [_mm_tn(ks[i], vn[hd[i]]) for i in IS]
            for i in IS:
                h = hd[i]
                sp_ref[ci, cs[i], :] = s[h]
                o_ref[rs[i], cs[i]] = ws[h][CHUNK:] + pv[h]
                vn_ref[rs[i], cs[i]] = vn[h].astype(bf16)
            s = [jnp.exp(gl[i]) * s[hd[i]] + kv[hd[i]] for i in IS]
        for h in range(HEADS):
            s_scr[h] = s[h]
        for i in HS:
            u_ref[rs[i], cs[i]] = u[i].astype(bf16)
            w_ref[rs[i], cs[i]] = w[i].astype(bf16)
            t_ref[i // HEADS, hd[i]] = t[i].astype(bf16)

    row = lambda c: (c, 0)
    act, act16 = jax.ShapeDtypeStruct((L, W), f32), jax.ShapeDtypeStruct((L, W), bf16)
    return _pcall(
        body, name="gdn_fwd", grid=(nc // cps,),
        in_specs=[pl.BlockSpec((rows_per_step, 3 * W), row), pl.BlockSpec((rows_per_step, LANE), row),
                  pl.BlockSpec((cps, HEADS, CHUNK), lambda c: (c, 0, 0))],
        out_specs=[pl.BlockSpec((rows_per_step, W), row)] * 4 + [
            pl.BlockSpec((cps, HEADS, CHUNK, CHUNK), lambda c: (c, 0, 0, 0)),
            pl.BlockSpec((cps, W, HEAD_DIM), lambda c: (c, 0, 0))],
        out_shape=[act, act16, act16, act16, jax.ShapeDtypeStruct((nc, HEADS, CHUNK, CHUNK), bf16),
                   jax.ShapeDtypeStruct((nc, W, HEAD_DIM), f32)],
        scratch_shapes=[pltpu.VMEM((HEADS, HEAD_DIM, HEAD_DIM), f32)],
        compiler_params=_cparams("arbitrary"),
    )(qkv, sc, gr)


def _gdn_gate(o, proj, gnw):
    L = o.shape[0]

    def body(o_ref, z_ref, w_ref, m_ref):
        for ls in HALVES:
            ov, z = o_ref[:, ls], z_ref[:, ls]
            rms = lax.rsqrt(jnp.mean(ov * ov, axis=-1, keepdims=True) + EPS)
            m_ref[:, ls] = (((ov * rms) * w_ref[...]) * (z * _sigmoid(z))).astype(bf16)

    return _pcall(
        body, name="gdn_gate", grid=(GDN_WIDTH // ELT_W,),
        in_specs=[pl.BlockSpec((L, ELT_W), lambda j: (0, j)), pl.BlockSpec((L, ELT_W), lambda j: (0, OFF_ZG // ELT_W + j)),
                  pl.BlockSpec((1, LANE), lambda j: (0, 0))],
        out_specs=pl.BlockSpec((L, ELT_W), lambda j: (0, j)),
        out_shape=jax.ShapeDtypeStruct((L, GDN_WIDTH + CONV_WIDTH), bf16),
        compiler_params=_cparams("parallel"),
    )(o, proj, gnw)


def _conv3(u, cw_ref, ls):
    return cw_ref[2:3, ls] * u + cw_ref[1:2, ls] * _shift_down(u, 1) + cw_ref[0:1, ls] * _shift_down(u, 2)


def _conv_specs(L):
    return [pl.BlockSpec((L, CONV_BLOCK), lambda j: (0, OFF_CONV // CONV_BLOCK + j)),
            pl.BlockSpec((3, ELT_W), lambda j: (0, j)), pl.BlockSpec((1, ELT_W), lambda j: (0, j))]


def _conv_parts(ls):
    return [slice(g * ELT_W + ls.start, g * ELT_W + ls.stop) for g in range(4)]


def _conv_fwd(proj, cw, cb, mix):
    L = proj.shape[0]

    def body(p_ref, cw_ref, cb_ref, mix_in, m_ref):
        for ls in HALVES:
            sb, sc_, sh, sz = _conv_parts(ls)
            z = p_ref[:, sz]
            cv = _conv3(p_ref[:, sc_] * p_ref[:, sh], cw_ref, ls) + cb_ref[:, ls]
            m_ref[:, ls] = ((p_ref[:, sb] * cv) * (z * _sigmoid(z))).astype(bf16)

    return _pcall(
        body, name="conv_fwd", grid=(CONV_WIDTH // ELT_W,),
        in_specs=_conv_specs(L) + [ANY], out_specs=pl.BlockSpec((L, ELT_W), lambda j: (0, GDN_WIDTH // ELT_W + j)),
        out_shape=jax.ShapeDtypeStruct(mix.shape, mix.dtype), input_output_aliases={3: 0},
        compiler_params=_cparams("parallel"),
    )(proj, cw, cb, mix)


def _out_proj_loss(x, mix, wo, fw, tgt):
    L = x.shape[0]
    tm = min(512, L)
    MW = GDN_WIDTH + CONV_WIDTH

    def body(x_ref, m_ref, wo_ref, fw_ref, t_ref, dy_ref, dyb_ref, dm_ref, gfw_ref, loss_ref):
        @pl.when(pl.program_id(0) == 0)
        def _():
            gfw_ref[...] = jnp.zeros_like(gfw_ref)
            loss_ref[...] = jnp.zeros_like(loss_ref)
        y = x_ref[...] + jnp.dot(m_ref[...], wo_ref[...], preferred_element_type=f32)
        r = lax.rsqrt(jnp.mean(y * y, axis=-1, keepdims=True) + EPS)
        yh = y * r
        fwv = fw_ref[...]
        diff = yh * fwv - t_ref[...]
        loss_ref[...] += jnp.sum(jnp.sum(diff * diff, axis=-1, keepdims=True), axis=0, keepdims=True) * (0.5 / D_MODEL)
        dout = diff * (1.0 / D_MODEL)
        gfw_ref[...] += jnp.sum(dout * yh, axis=0, keepdims=True)
        dyh = dout * fwv
        dy = r * (dyh - yh * jnp.mean(dyh * yh, axis=-1, keepdims=True))
        dy_ref[...] = dy
        dyb = dy.astype(bf16)
        dyb_ref[...] = dyb
        dm_ref[...] = lax.dot_general(dyb, wo_ref[...], (((1,), (1,)), ((), ())), preferred_element_type=f32)

    row = lambda i: (i, 0)
    fix = lambda i: (0, 0)
    act = jax.ShapeDtypeStruct((L, D_MODEL), f32)
    return _pcall(
        body, name="out_proj_loss", grid=(L // tm,),
        in_specs=[pl.BlockSpec((tm, D_MODEL), row), pl.BlockSpec((tm, MW), row), pl.BlockSpec((MW, D_MODEL), fix),
                  pl.BlockSpec((1, D_MODEL), fix), pl.BlockSpec((tm, D_MODEL), row)],
        out_specs=[pl.BlockSpec((tm, D_MODEL), row), pl.BlockSpec((tm, D_MODEL), row), pl.BlockSpec((tm, MW), row),
                   pl.BlockSpec((1, D_MODEL), fix), pl.BlockSpec((1, LANE), fix)],
        out_shape=[act, jax.ShapeDtypeStruct((L, D_MODEL), bf16), jax.ShapeDtypeStruct((L, MW), f32),
                   jax.ShapeDtypeStruct((1, D_MODEL), f32), jax.ShapeDtypeStruct((1, LANE), f32)],
        compiler_params=_cparams("arbitrary"),
    )(x, mix, wo, fw, tgt)


def _tn_matmul(a, b, name):
    L, M = a.shape
    N = b.shape[1]
    tm = 512 if M % 512 == 0 else (768 if M % 768 == 0 else M)

    def body(a_ref, b_ref, o_ref):
        o_ref[...] = lax.dot_general(a_ref[...], b_ref[...], (((0,), (0,)), ((), ())),
                                     preferred_element_type=f32).astype(o_ref.dtype)

    return _pcall(
        body, name=name, grid=(M // tm,),
        in_specs=[pl.BlockSpec((L, tm), lambda i: (0, i)), pl.BlockSpec((L, N), lambda i: (0, 0))],
        out_specs=pl.BlockSpec((tm, N), lambda i: (i, 0)),
        out_shape=jax.ShapeDtypeStruct((M, N), bf16),
        compiler_params=_cparams("parallel"),
    )(a, b)


def _gdn_gate_bwd(o, proj, gnw, dmix_a, after):
    L = o.shape[0]

    def body(o_ref, z_ref, w_ref, dm_ref, after_ref, do_ref, dz_ref, gw_ref):
        @pl.when(pl.program_id(0) == 0)
        def _():
            gw_ref[...] = jnp.zeros_like(gw_ref)
        wv = w_ref[...]
        for ls in HALVES:
            ov, z, dm = o_ref[:, ls], z_ref[:, ls], dm_ref[:, ls]
            rms = lax.rsqrt(jnp.mean(ov * ov, axis=-1, keepdims=True) + EPS)
            xh = ov * rms
            sg = _sigmoid(z)
            d_on = dm * (z * sg)
            dz_ref[:, ls] = (dm * (xh * wv) * (sg * (1.0 + z * (1.0 - sg)))).astype(bf16)
            gw_ref[...] += jnp.sum(d_on * xh, axis=0, keepdims=True)
            dxh = d_on * wv
            do_ref[:, ls] = (rms * (dxh - xh * jnp.mean(dxh * xh, axis=-1, keepdims=True))).astype(bf16)

    wide = pl.BlockSpec((L, ELT_W), lambda j: (0, j))
    return _pcall(
        body, name="gdn_gate_bwd", grid=(GDN_WIDTH // ELT_W,),
        in_specs=[wide, pl.BlockSpec((L, ELT_W), lambda j: (0, OFF_ZG // ELT_W + j)),
                  pl.BlockSpec((1, LANE), lambda j: (0, 0)), wide, ANY],
        out_specs=[wide, pl.BlockSpec((L, ELT_W), lambda j: (0, OFF_ZG // ELT_W + j)),
                   pl.BlockSpec((1, LANE), lambda j: (0, 0))],
        out_shape=[jax.ShapeDtypeStruct((L, GDN_WIDTH), bf16), jax.ShapeDtypeStruct((L, PROJ_PAD), bf16),
                   jax.ShapeDtypeStruct((1, LANE), f32)],
        compiler_params=_cparams("arbitrary"),
    )(o, proj, gnw, dmix_a, after)


def _conv_bwd(proj, cw, cb, dmix_b, dproj):
    L = proj.shape[0]

    def body(p_ref, cw_ref, cb_ref, dm_ref, dproj_in, dp_ref, gcw_ref, gcb_ref):
        for ls in HALVES:
            sb, sc_, sh, sz_ = _conv_parts(ls)
            bv, cv_, hv, z, dm = p_ref[:, sb], p_ref[:, sc_], p_ref[:, sh], p_ref[:, sz_], dm_ref[:, ls]
            u = cv_ * hv
            cv = _conv3(u, cw_ref, ls) + cb_ref[:, ls]
            sg = _sigmoid(z)
            sz = z * sg
            dp_ref[:, sb] = (dm * cv * sz).astype(bf16)
            dp_ref[:, sz_] = (dm * (bv * cv) * (sg * (1.0 + z * (1.0 - sg)))).astype(bf16)
            dcv = dm * bv * sz
            gcb_ref[:, ls] = jnp.sum(dcv, axis=0, keepdims=True)
            dcv1, dcv2 = _shift_up(dcv, 1), _shift_up(dcv, 2)
            gcw_ref[2:3, ls] = jnp.sum(dcv * u, axis=0, keepdims=True)
            gcw_ref[1:2, ls] = jnp.sum(dcv1 * u, axis=0, keepdims=True)
            gcw_ref[0:1, ls] = jnp.sum(dcv2 * u, axis=0, keepdims=True)
            du = cw_ref[2:3, ls] * dcv + cw_ref[1:2, ls] * dcv1 + cw_ref[0:1, ls] * dcv2
            dp_ref[:, sc_] = (du * hv).astype(bf16)
            dp_ref[:, sh] = (du * cv_).astype(bf16)

    return _pcall(
        body, name="conv_bwd", grid=(CONV_WIDTH // ELT_W,),
        in_specs=_conv_specs(L) + [pl.BlockSpec((L, ELT_W), lambda j: (0, GDN_WIDTH // ELT_W + j)), ANY],
        out_specs=[pl.BlockSpec((L, CONV_BLOCK), lambda j: (0, OFF_CONV // CONV_BLOCK + j)),
                   pl.BlockSpec((3, ELT_W), lambda j: (0, j)), pl.BlockSpec((1, ELT_W), lambda j: (0, j))],
        out_shape=[jax.ShapeDtypeStruct(dproj.shape, dproj.dtype), jax.ShapeDtypeStruct((3, CONV_WIDTH), f32),
                   jax.ShapeDtypeStruct((1, CONV_WIDTH), f32)],
        input_output_aliases={4: 0},
        compiler_params=_cparams("parallel"),
    )(proj, cw, cb, dmix_b, dproj)


def _gdn_bwd(qkv, sc, gr, u_all, w_all, vn_all, t_all, sp_all, do_all):
    L = qkv.shape[0]
    nc = L // CHUNK
    W = GDN_WIDTH
    cps = GDN_CPS_BWD if nc % GDN_CPS_BWD == 0 else 1
    rows_per_step = cps * CHUNK
    nsteps = nc // cps

    def body(qkv_ref, sc_ref, gr_ref, u_ref, w_ref, vn_ref, t_ref, sp_ref, do_ref, dqkv_ref, dsc_ref, dgr_ref, ds_scr):
        @pl.when(pl.program_id(0) == 0)
        def _():
            ds_scr[...] = jnp.zeros_like(ds_scr)
        nh, base = HEADS, 0
        HS = range(cps * nh)
        hl = [i % nh for i in HS]
        hd = [base + hl[i] for i in HS]
        rs = [slice((i // nh) * CHUNK, (i // nh + 1) * CHUNK) for i in HS]
        cs = [slice(hd[i] * HEAD_DIM, (hd[i] + 1) * HEAD_DIM) for i in HS]
        q = [qkv_ref[rs[i], hd[i] * HEAD_DIM:(hd[i] + 1) * HEAD_DIM] for i in HS]
        k = [qkv_ref[rs[i], W + hd[i] * HEAD_DIM:W + (hd[i] + 1) * HEAD_DIM] for i in HS]
        v = [qkv_ref[rs[i], 2 * W + hd[i] * HEAD_DIM:2 * W + (hd[i] + 1) * HEAD_DIM] for i in HS]
        hsc = [_head_scalars(sc_ref[rs[i], :], gr_ref, hd[i], i // nh) for i in HS]
        beta, gcc, gl, dmat, dmat_t = ([x[i] for x in hsc] for i in range(5))
        ii, jj = hsc[0][5], hsc[0][6]
        eg = [jnp.exp(gcc[h]) for h in HS]
        ekl = [jnp.exp(gl[h] - gcc[h]) for h in HS]
        egl = [jnp.exp(gl[h]) for h in HS]
        kb = [k[h] * beta[h] for h in HS]
        ks = [k[h] * ekl[h] for h in HS]
        do = [do_ref[rs[h], cs[h]] for h in HS]
        vn = [vn_ref[rs[h], cs[h]] for h in HS]
        s = [sp_ref[h // nh, cs[h], :] for h in HS]
        w = [w_ref[rs[h], cs[h]] for h in HS]
        qd = [q[h] * eg[h] for h in HS]

        kq = [_mm_nt(k[h], q[h]) for h in HS]
        p_t = [jnp.where(jj >= ii, kq[h] * dmat_t[h], 0.0) for h in HS]
        ptd = [_mm(p_t[h], do[h]) for h in HS]
        qw = [_cat16([qd[h], -w[h]], 0) for h in HS]
        dsn, dvn, dodv = [None] * len(HS), [None] * len(HS), [None] * len(HS)
        ds_cur = [ds_scr[base + h] for h in range(nh)]
        for ci in reversed(range(cps)):
            IS = range(ci * nh, (ci + 1) * nh)
            ksd = [_mm(ks[i], ds_cur[hl[i]]) for i in IS]
            for i in IS:
                dsn[i] = ds_cur[hl[i]]
                dvn[i] = ptd[i] + ksd[hl[i]]
                dodv[i] = _cat16([do[i], dvn[i]], 0)
            dsq = [_mm_tn(qw[i], dodv[i]) for i in IS]
            ds_cur = [egl[i] * ds_cur[hl[i]] + dsq[hl[i]] for i in IS]
        for h in range(nh):
            ds_scr[base + h] = ds_cur[h]
        x1 = [_mm_nt(dodv[h], s[h]) for h in HS]
        dks = [_mm_nt(vn[h], dsn[h]) for h in HS]
        dov = [_mm_nt(do[h], vn[h]) for h in HS]
        vdo = [_mm_nt(vn[h], do[h]) for h in HS]
        kk = [_mm_nt(kb[h], k[h]) for h in HS]
        qk = [_mm_nt(q[h], k[h]) for h in HS]
        dgl = [egl[h] * jnp.sum(jnp.sum(s[h] * dsn[h], axis=1, keepdims=True), axis=0, keepdims=True) for h in HS]
        dqd = [x1[h][:CHUNK] for h in HS]
        duw = [jnp.concatenate([dvn[h], -x1[h][CHUNK:]], axis=1) for h in HS]
        tdu = [_mm_tn(t_ref[h // nh, hd[h]], duw[h]) for h in HS]
        dvk = [duw[h] + tdu[h] for h in HS]
        uw = [jnp.concatenate([u_ref[rs[h], cs[h]], w[h]], axis=1) for h in HS]
        da = [-jnp.where(ii > jj, _mm_nt(dvk[h], uw[h]), 0.0) for h in HS]
        da_t = [-jnp.where(jj > ii, _mm_nt(uw[h], dvk[h]), 0.0) for h in HS]
        dp = [jnp.where(ii >= jj, dov[h], 0.0) for h in HS]
        dp_t = [jnp.where(jj >= ii, vdo[h], 0.0) for h in HS]
        r1 = [_mm(_cat16([da[h] * dmat[h], dp[h] * dmat[h]], 0), k[h]) for h in HS]
        dk1 = [_mm(_cat16([da_t[h] * dmat_t[h], dp_t[h] * dmat_t[h]], 1), _cat16([kb[h], q[h]], 0)) for h in HS]
        lane = _lanes((CHUNK, LANE))
        for ci in range(cps):
            dsc = jnp.zeros((CHUNK, LANE), f32)
            for i in range(ci * nh, (ci + 1) * nh):
                h = hd[i]
                a = jnp.where(ii > jj, kk[i] * dmat[i], 0.0)
                p = jnp.where(ii >= jj, qk[i] * dmat[i], 0.0)
                gmat = da[i] * a + dp[i] * p
                dvb, dkbg = dvk[i][:, :HEAD_DIM], dvk[i][:, HEAD_DIM:]
                kbg = kb[i] * eg[i]
                dkb = r1[i][:CHUNK] + dkbg * eg[i]
                dq = r1[i][CHUNK:] + dqd[i] * eg[i]
                dk = dk1[i] + dks[i] * ekl[i] + dkb * beta[i]
                dbeta = jnp.sum(dkb * k[i] + dvb * v[i], axis=1, keepdims=True)
                ksum = jnp.sum(dks[i] * ks[i], axis=1, keepdims=True)
                dgl_tot = dgl[i] + jnp.sum(ksum, axis=0, keepdims=True)
                dgc = (jnp.sum(gmat, axis=1, keepdims=True) + jnp.sum(dqd[i] * qd[i] + dkbg * kbg, axis=1, keepdims=True)
                       - ksum)
                dgc = dgc + jnp.where(_rows(dgc.shape) == CHUNK - 1, dgl_tot, 0.0)
                dqkv_ref[rs[i], h * HEAD_DIM:(h + 1) * HEAD_DIM] = dq
                dqkv_ref[rs[i], W + h * HEAD_DIM:W + (h + 1) * HEAD_DIM] = dk
                dqkv_ref[rs[i], 2 * W + h * HEAD_DIM:2 * W + (h + 1) * HEAD_DIM] = dvb * beta[i]
                dsc = jnp.where(lane == h, dbeta, jnp.where(lane == HEADS + h, dgc, dsc))
                dgr_ref[ci, h:h + 1, :] = jnp.sum(gmat, axis=0, keepdims=True)
            dsc_ref[ci * CHUNK:(ci + 1) * CHUNK, :] = dsc

    row = lambda c: (nsteps - 1 - c, 0)
    lead3 = lambda c: (nsteps - 1 - c, 0, 0)
    return _pcall(
        body, name="gdn_bwd", grid=(nsteps,),
        in_specs=[pl.BlockSpec((rows_per_step, 3 * W), row), pl.BlockSpec((rows_per_step, LANE), row),
                  pl.BlockSpec((cps, HEADS, CHUNK), lead3),
                  pl.BlockSpec((rows_per_step, W), row), pl.BlockSpec((rows_per_step, W), row),
                  pl.BlockSpec((rows_per_step, W), row),
                  pl.BlockSpec((cps, HEADS, CHUNK, CHUNK), lambda c: (nsteps - 1 - c, 0, 0, 0)),
                  pl.BlockSpec((cps, W, HEAD_DIM), lead3), pl.BlockSpec((rows_per_step, W), row)],
        out_specs=[pl.BlockSpec((rows_per_step, 3 * W), row), pl.BlockSpec((rows_per_step, LANE), row),
                   pl.BlockSpec((cps, HEADS, CHUNK), lead3)],
        out_shape=[jax.ShapeDtypeStruct((L, 3 * W), f32), jax.ShapeDtypeStruct((L, LANE), f32),
                   jax.ShapeDtypeStruct((nc, HEADS, CHUNK), f32)],
        scratch_shapes=[pltpu.VMEM((HEADS, HEAD_DIM, HEAD_DIM), f32)],
        compiler_params=_cparams("arbitrary"),
    )(qkv, sc, gr, u_all, w_all, vn_all, t_all, sp_all, do_all)


def _qkv_bwd(proj, cw, dn, dproj):
    L = proj.shape[0]

    def body(x_ref, cw_ref, dn_ref, dproj_in, dx_ref, gcw_ref):
        j = pl.program_id(0)
        steps = GDN_WIDTH // ELT_W
        scale = jnp.where(j < steps, HEAD_DIM ** -0.5, 1.0).astype(f32)
        for ls in HALVES:
            x, dn_v = x_ref[:, ls], dn_ref[:, ls]
            c = _conv4(x, cw_ref, ls)
            sg = _sigmoid(c)
            a = c * sg
            rn = lax.rsqrt(jnp.sum(a * a, axis=1, keepdims=True) + EPS)
            da_n = (scale * rn) * (dn_v - a * ((rn * rn) * jnp.sum(dn_v * a, axis=1, keepdims=True)))
            da = jnp.where(j < 2 * steps, da_n, dn_v)
            dc = da * (sg * (1.0 + c * (1.0 - sg)))
            dc1, dc2, dc3 = _shift_up(dc, 1), _shift_up(dc, 2), _shift_up(dc, 3)
            gcw_ref[3:4, ls] = jnp.sum(dc * x, axis=0, keepdims=True)
            gcw_ref[2:3, ls] = jnp.sum(dc1 * x, axis=0, keepdims=True)
            gcw_ref[1:2, ls] = jnp.sum(dc2 * x, axis=0, keepdims=True)
            gcw_ref[0:1, ls] = jnp.sum(dc3 * x, axis=0, keepdims=True)
            dx = cw_ref[3:4, ls] * dc + cw_ref[2:3, ls] * dc1 + cw_ref[1:2, ls] * dc2 + cw_ref[0:1, ls] * dc3
            dx_ref[:, ls] = dx.astype(bf16)

    col = pl.BlockSpec((L, ELT_W), lambda j: (0, j))
    wspec = pl.BlockSpec((4, ELT_W), lambda j: (0, j))
    return _pcall(
        body, name="qkv_bwd", grid=(3 * GDN_WIDTH // ELT_W,),
        in_specs=[col, wspec, col, ANY], out_specs=[col, wspec],
        out_shape=[jax.ShapeDtypeStruct(dproj.shape, dproj.dtype), jax.ShapeDtypeStruct((4, 3 * GDN_WIDTH), f32)],
        input_output_aliases={3: 0},
        compiler_params=_cparams("parallel"),
    )(proj, cw, dn, dproj)


def _scalars_bwd(proj, alog_p, dtb_p, dsc, dgr_col, dproj, after):
    L = proj.shape[0]

    def body(x_ref, al_ref, dt_ref, dsc_ref, dgr_ref, dproj_in, after_ref, dba_ref, gs_ref):
        x, dsc_v = x_ref[...], dsc_ref[...]
        lane = _lanes(x.shape)
        dec = (lane >= HEADS) & (lane < 2 * HEADS)
        dg = jnp.where(dec, dsc_v - dgr_ref[...], 0.0)
        rc = _rows(x.shape) & (CHUNK - 1)
        for s in (1, 2, 4, 8, 16, 32):
            dg = dg + jnp.where(rc + s < CHUNK, pltpu.roll(dg, L - s, 0), 0.0)
        xa = x + dt_ref[...]
        ea = jnp.exp(al_ref[...])
        g = -ea * _softplus(xa)
        da = dg * (-ea) * _sigmoid(xa)
        beta = _sigmoid(x)
        db = dsc_v * beta * (1.0 - beta)
        dba_ref[:, :LANE] = jnp.where(lane < HEADS, db, jnp.where(dec, da, 0.0)).astype(bf16)
        dba_ref[:, LANE:] = jnp.zeros((L, ELT_W - LANE), bf16)
        g_al = jnp.sum(jnp.where(dec, dg * g, 0.0), axis=0, keepdims=True)
        g_dt = jnp.sum(jnp.where(dec, da, 0.0), axis=0, keepdims=True)
        row8 = _rows(gs_ref.shape)
        gs = jnp.where(row8 == 0, g_al, jnp.where(row8 == 1, g_dt, 0.0))
        gs_ref[...] = pltpu.roll(gs, LANE - HEADS, 1)

    full = pl.BlockSpec((L, LANE), lambda i: (0, 0))
    vec = pl.BlockSpec((1, LANE), lambda i: (0, 0))
    return _pcall(
        body, name="scalars_bwd", grid=(1,),
        in_specs=[pl.BlockSpec((L, LANE), lambda i: (0, OFF_BA // LANE)), vec, vec, full, full, ANY, ANY],
        out_specs=[pl.BlockSpec((L, ELT_W), lambda i: (0, OFF_BA // ELT_W)), pl.BlockSpec((8, LANE), lambda i: (0, 0))],
        out_shape=[jax.ShapeDtypeStruct(dproj.shape, dproj.dtype), jax.ShapeDtypeStruct((8, LANE), f32)],
        input_output_aliases={5: 0},
        compiler_params=_cparams("arbitrary"),
    )(proj, alog_p, dtb_p, dsc, dgr_col, dproj, after)


def _input_grad(dproj, wpad, x, nw, dy, after):
    L = x.shape[0]
    tm = min(512, L)
    cuts = (0, 512, 1024, 2048, 3072, 5120, 7168, PROJ_PAD)
    nk = len(cuts) - 1

    def body(dp_ref, w_hbm, x_ref, nw_ref, dy_ref, after_ref, gx_ref, gnw_ref, w_vmem, sems):
        first = pl.program_id(0) == 0
        loads = [pltpu.make_async_copy(w_hbm.at[cuts[k]:cuts[k + 1], :], w_vmem.at[cuts[k]:cuts[k + 1], :], sems.at[k])
                 for k in range(nk)]

        @pl.when(first)
        def _():
            for cp in loads:
                cp.start()
            gnw_ref[...] = jnp.zeros_like(gnw_ref)
        dh = None
        for k in range(nk):
            pl.when(first)(loads[k].wait)
            part = jnp.dot(dp_ref[:, cuts[k]:cuts[k + 1]], w_vmem[cuts[k]:cuts[k + 1], :], preferred_element_type=f32)
            dh = part if dh is None else dh + part
        xv, nwv = x_ref[...], nw_ref[...]
        r = lax.rsqrt(jnp.mean(xv * xv, axis=-1, keepdims=True) + EPS)
        xh = xv * r
        gnw_ref[...] += jnp.sum(dh * xh, axis=0, keepdims=True)
        dxh = dh * nwv
        gx_ref[...] = dy_ref[...] + r * (dxh - xh * jnp.mean(dxh * xh, axis=-1, keepdims=True))

    row = lambda i: (i, 0)
    fix = lambda i: (0, 0)
    return _pcall(
        body, name="input_grad", grid=(L // tm,),
        in_specs=[pl.BlockSpec((tm, PROJ_PAD), row), ANY, pl.BlockSpec((tm, D_MODEL), row),
                  pl.BlockSpec((1, D_MODEL), fix), pl.BlockSpec((tm, D_MODEL), row), ANY],
        out_specs=[pl.BlockSpec((tm, D_MODEL), row), pl.BlockSpec((1, D_MODEL), fix)],
        out_shape=[jax.ShapeDtypeStruct((L, D_MODEL), f32), jax.ShapeDtypeStruct((1, D_MODEL), f32)],
        scratch_shapes=[pltpu.VMEM(wpad.shape, bf16), pltpu.SemaphoreType.DMA((nk,))],
        compiler_params=_cparams("arbitrary"),
    )(dproj, wpad, x, nw, dy, after)


def _adamw_reduce(parts, w, m, v, name, first_row=None):
    R, C = w.shape[0], w.shape[-1]
    n_parts = parts.shape[0]
    tr = 128 if R % 128 == 0 else R
    c1 = 1.0 - ADAM_B1 ** ADAM_STEP
    c2 = 1.0 - ADAM_B2 ** ADAM_STEP
    at = (slice(None), 0, slice(None)) if w.ndim == 3 else Ellipsis
    window = (slice(None), slice(None)) if first_row is None else (slice(first_row, first_row + R), slice(0, C))

    def body(p_ref, w_ref, m_ref, v_ref, g_ref, d_ref, nm_ref, nv_ref):
        g = p_ref[(0,) + window].astype(f32)
        for s in range(1, n_parts):
            g = g + p_ref[(s,) + window].astype(f32)
        nm = ADAM_B1 * m_ref[at] + (1.0 - ADAM_B1) * g
        nv = ADAM_B2 * v_ref[at] + (1.0 - ADAM_B2) * (g * g)
        g_ref[at] = g
        nm_ref[at] = nm
        nv_ref[at] = nv
        d_ref[at] = -ADAM_LR * ((nm / c1) / (jnp.sqrt(nv / c2) + ADAM_EPS) + ADAM_WD * w_ref[at])

    blk = pl.BlockSpec((tr, 1, C), lambda i: (i, 0, 0)) if w.ndim == 3 else pl.BlockSpec((tr, C), lambda i: (i, 0))
    out = jax.ShapeDtypeStruct(w.shape, f32)
    if first_row is None:
        p_spec = pl.BlockSpec((n_parts, tr, C), lambda i: (0, i, 0))
    else:
        assert tr == R
        p_spec = pl.BlockSpec(parts.shape, lambda i: (0, 0, 0))
    return _pcall(
        body, name=name, grid=(R // tr,),
        in_specs=[p_spec, blk, blk, blk],
        out_specs=[blk] * 4, out_shape=[out] * 4,
        compiler_params=_cparams("parallel"),
    )(parts, w, m, v)


SMALL_SLOTS = ((0, D_MODEL), (D_MODEL, D_MODEL), (2 * D_MODEL, D_MODEL), (3 * D_MODEL, LANE),
               (3 * D_MODEL + LANE, HEADS), (3 * D_MODEL + 2 * LANE, HEADS))
SMALL_LOSS = 3 * D_MODEL + 3 * LANE
SMALL_W = SMALL_LOSS + LANE


def _pack_small(gs, after):
    def body(nw_ref, cb_ref, fw_ref, gn_ref, sc_ref, ls_ref, after_ref, o_ref):
        for ref, (start, width) in zip((nw_ref, cb_ref, fw_ref, gn_ref), SMALL_SLOTS[:4]):
            o_ref[:, start:start + width] = ref[...]
        o_ref[:, SMALL_SLOTS[4][0]:SMALL_SLOTS[4][0] + LANE] = sc_ref[0:1, :]
        o_ref[:, SMALL_SLOTS[5][0]:SMALL_SLOTS[5][0] + LANE] = sc_ref[1:2, :]
        o_ref[:, SMALL_LOSS:SMALL_W] = ls_ref[...]

    vm = pl.BlockSpec(memory_space=pltpu.VMEM)
    return _pcall(body, name="pack_small_grads", out_shape=jax.ShapeDtypeStruct((1, SMALL_W), f32),
                  in_specs=[vm] * 6 + [ANY], out_specs=vm)(*gs, after)


def _adamw_small(parts, ws, ms, vs):
    c1 = 1.0 - ADAM_B1 ** ADAM_STEP
    c2 = 1.0 - ADAM_B2 ** ADAM_STEP
    np_ = len(ws)

    def body(*refs):
        p_ref = refs[0]
        w_refs, m_refs, v_refs = refs[1:1 + np_], refs[1 + np_:1 + 2 * np_], refs[1 + 2 * np_:1 + 3 * np_]
        outs = refs[1 + 3 * np_:]
        g_refs, d_refs, nm_refs, nv_refs = (outs[i * np_:(i + 1) * np_] for i in range(4))
        loss_ref = outs[4 * np_]

        def total(start, width):
            t = p_ref[0, :, start:start + width]
            for s in range(1, N_DEV):
                t = t + p_ref[s, :, start:start + width]
            return t

        for i, (start, width) in enumerate(SMALL_SLOTS):
            g = total(start, width)
            nm = ADAM_B1 * m_refs[i][...] + (1.0 - ADAM_B1) * g
            nv = ADAM_B2 * v_refs[i][...] + (1.0 - ADAM_B2) * (g * g)
            g_refs[i][...] = g
            nm_refs[i][...] = nm
            nv_refs[i][...] = nv
            d_refs[i][...] = -ADAM_LR * ((nm / c1) / (jnp.sqrt(nv / c2) + ADAM_EPS) + ADAM_WD * w_refs[i][...])
        loss_ref[...] = total(SMALL_LOSS, LANE)

    vm = pl.BlockSpec(memory_space=pltpu.VMEM)
    shapes = [jax.ShapeDtypeStruct(w.shape, f32) for w in ws]
    res = _pcall(body, name="adamw_small", out_shape=shapes * 4 + [jax.ShapeDtypeStruct((1, LANE), f32)],
                 in_specs=[vm] * (1 + 3 * np_), out_specs=[vm] * (4 * np_ + 1))(parts, *ws, *ms, *vs)
    return [res[i * np_:(i + 1) * np_] for i in range(4)], res[4 * np_]


def _adamw_w_in(part_a, part_b, w3, m3, v3, after):
    _, n, _ = part_a.shape
    c1 = 1.0 - ADAM_B1 ** ADAM_STEP
    c2 = 1.0 - ADAM_B2 ** ADAM_STEP

    def body(pa_ref, pb_ref, w_ref, m_ref, v_ref, after_ref, g_ref, d_ref, nm_ref, nv_ref):
        g = pa_ref[0].astype(f32) + pb_ref[0].astype(f32)
        nm = ADAM_B1 * m_ref[:, 0, :] + (1.0 - ADAM_B1) * g
        nv = ADAM_B2 * v_ref[:, 0, :] + (1.0 - ADAM_B2) * (g * g)
        g_ref[:, 0, :] = g
        nm_ref[:, 0, :] = nm
        nv_ref[:, 0, :] = nv
        d_ref[:, 0, :] = -ADAM_LR * ((nm / c1) / (jnp.sqrt(nv / c2) + ADAM_EPS) + ADAM_WD * w_ref[:, 0, :])

    tile = 2 * COL_TILE
    blk = pl.BlockSpec((n, 1, tile), lambda j: (0, 0, j))
    out = jax.ShapeDtypeStruct((n, 1, D_MODEL), f32)
    return _pcall(
        body, name="adamw_w_in", grid=(D_MODEL // tile,),
        in_specs=[pl.BlockSpec((1, n, tile), lambda j: (0, 0, j))] * 2 + [blk, blk, blk, ANY],
        out_specs=[blk] * 4, out_shape=[out] * 4,
        compiler_params=_cparams("parallel"),
    )(part_a, part_b, w3, m3, v3, after)


def _pad_lanes(vec8, start):
    return jnp.pad(vec8.reshape(1, -1), ((0, 0), (start, LANE - start - vec8.size)))


def kernel(x, norm_in_w, w_in, conv_qkv_w, A_log, dt_bias, gdn_norm_w, conv_w, conv_b, w_out, final_norm_w, loss_target, m_norm_in_w, m_w_in, m_conv_qkv_w, m_A_log, m_dt_bias, m_gdn_norm_w, m_conv_w, m_conv_b, m_w_out, m_final_norm_w, v_norm_in_w, v_w_in, v_conv_qkv_w, v_A_log, v_dt_bias, v_gdn_norm_w, v_conv_w, v_conv_b, v_w_out, v_final_norm_w):
    L = x.shape[1]
    nc = L // CHUNK
    xs = x[0]
    tgt = loss_target[0]
    fnw = final_norm_w.reshape(1, D_MODEL)

    as_rows = lambda a: jnp.transpose(a, (2, 0, 1))
    as_taps = lambda a: jnp.transpose(a, (1, 0, 2))
    win_blk, wo_blk = _cast_weights(as_rows(w_in), w_out[0])
    win_g, cqkv_g, cw_g = _all_gather([win_blk, conv_qkv_w[0], as_taps(conv_w)], "gather_weights",
                                      pieces=[8, 1, 1])
    wpad = _relayout_w_in(win_g)
    cqkv = jnp.concatenate([cqkv_g[d] for d in range(N_DEV)], axis=1)
    cw = jnp.concatenate([cw_g[d][:, 0, :] for d in range(N_DEV)], axis=1)
    alog_p = _pad_lanes(A_log, HEADS)
    dtb_p = _pad_lanes(dt_bias, HEADS)
    tok = lambda started: started[4]
    wo_started = _spread_start(wo_blk, wpad, "gather", "gather_w_out_start")

    proj, h = _in_proj(xs, norm_in_w, wpad, tok(wo_started))
    qkv = _qkv_act(proj, cqkv)
    sc, gr = _scalars(proj, alog_p, dtb_p)
    o, u_all, w_all, vn_all, t_all, sp_all = _gdn_fwd(qkv, sc, gr)
    mix = _conv_fwd(proj, cw, conv_b, _gdn_gate(o, proj, gdn_norm_w))
    wo = _spread_wait(wo_started, mix, "gather", "gather_w_out_wait")[1].reshape(-1, D_MODEL)
    dy, dyb, dmix, g_fnw, loss_v = _out_proj_loss(xs, mix, wo, fnw, tgt)

    g_wout = _tn_matmul(mix, dyb, "grad_w_out")
    gwo_started = _spread_start(g_wout.reshape(N_DEV, -1, D_MODEL), dyb, "scatter", "exchange_grad_w_out_start")
    do, dproj, g_gnw = _gdn_gate_bwd(o, proj, gdn_norm_w, dmix, tok(gwo_started))
    dproj, g_cw, g_cb = _conv_bwd(proj, cw, conv_b, dmix, dproj)
    dqkv_n, dsc, dgr = _gdn_bwd(qkv, sc, gr, u_all, w_all, vn_all, t_all, sp_all, do)
    dproj, g_cqkv = _qkv_bwd(proj, cqkv, dqkv_n, dproj)
    g_cqkv_blk = g_cqkv.reshape(4, N_DEV, -1).transpose(1, 0, 2)
    g_cw_blk = jnp.pad(g_cw.reshape(3, N_DEV, -1).transpose(1, 0, 2),
                       ((0, 0), (0, 1), (0, g_cqkv_blk.shape[2] - g_cw.shape[1] // N_DEV)))
    gsm_started = _spread_start(jnp.concatenate([g_cqkv_blk, g_cw_blk], axis=1), g_cqkv, "scatter",
                                "exchange_small_sharded_grads_start")
    dgr_col = jnp.pad(dgr.transpose(0, 2, 1).reshape(L, HEADS), ((0, 0), (HEADS, LANE - 2 * HEADS)))
    dproj, g_sc = _scalars_bwd(proj, alog_p, dtb_p, dsc, dgr_col, dproj, tok(gsm_started))
    g_win_blk = _grad_blocks(_tn_matmul(dproj, h, "grad_w_in"))

    (p_win,) = _pair_exchange([g_win_blk], "exchange_grads_pair")
    r_small = _spread_wait(gsm_started, p_win, "scatter", "exchange_small_sharded_grads_wait")[1]
    s_win = _pair_sum(g_win_blk, p_win, "pair_sum_w_in")
    gw1_started = _spread_start(s_win, r_small, "axis_a", "exchange_grads_axis1_start")
    grad_x, g_nw = _input_grad(dproj, wpad, xs, norm_in_w, dy, tok(gw1_started))
    s_thru, got1 = _spread_wait(gw1_started, grad_x, "axis_a", "exchange_grads_axis1_wait")
    t_win = _axis_sum(s_thru, got1, "axis_sum_w_in")
    gw2_started = _spread_start(t_win, got1, "axis_b", "exchange_grads_axis2_start")

    r_wout = _spread_wait(gwo_started, tok(gw2_started), "scatter", "exchange_grad_w_out_wait")[1]
    upd_wout =_adamw_reduce(r_wout, w_out[0], m_w_out[0], v_w_out[0], "adamw_w_out")
    upd_cqkv = _adamw_reduce(r_small, conv_qkv_w[0], m_conv_qkv_w[0], v_conv_qkv_w[0], "adamw_conv_qkv_w", first_row=0)
    upd_cw = _adamw_reduce(r_small, as_taps(conv_w), as_taps(m_conv_w), as_taps(v_conv_w), "adamw_conv_w", first_row=4)

    t_thru, got2 = _spread_wait(gw2_started, upd_cw[0], "axis_b", "exchange_grads_axis2_wait")

    small_g = _pack_small([g_nw, g_cb, g_fnw, g_gnw, g_sc, loss_v], got2)
    gsg_started = _spread_start(small_g, got2, "gather", "gather_small_grads_start")
    upd_win_t = _adamw_w_in(t_thru, got2, as_rows(w_in), as_rows(m_w_in), as_rows(v_w_in), tok(gsg_started))
    upd_win = [jnp.transpose(a, (1, 2, 0)) for a in upd_win_t]
    small_all = _spread_wait(gsg_started, upd_win_t[0], "gather", "gather_small_grads_wait")[1]
    fvec = lambda a: a.reshape(1, D_MODEL)
    upd_small, loss_sum = _adamw_small(
        small_all,
        [norm_in_w, conv_b, fvec(final_norm_w), gdn_norm_w, A_log, dt_bias],
        [m_norm_in_w, m_conv_b, fvec(m_final_norm_w), m_gdn_norm_w, m_A_log, m_dt_bias],
        [v_norm_in_w, v_conv_b, fvec(v_final_norm_w), v_gdn_norm_w, v_A_log, v_dt_bias])

    outs = [loss_sum[0, 0], grad_x[None]]
    for k in range(4):
        nw_k, cb_k, fw_k, gn_k, al_k, dt_k = upd_small[k]
        outs += [nw_k, upd_win[k], upd_cqkv[k][None], al_k, dt_k, gn_k,
                 as_taps(upd_cw[k]), cb_k, upd_wout[k][None], fw_k.reshape(D_MODEL)]
    return tuple(outs)
```

```python
import jax
import jax.numpy as jnp
from jax import lax
from jax.experimental import pallas as pl
from jax.experimental.pallas import tpu as pltpu

f32 = jnp.float32
bf16 = jnp.bfloat16

N_DEV = 8
D_MODEL = 1024
HEADS = 8
HEAD_DIM = 128
CHUNK = 64
GDN_CPS = 4
GDN_CPS_BWD = 1
GDN_WIDTH = HEADS * HEAD_DIM
CONV_WIDTH = 1024
PROJ_WIDTH = 8208
SHARD_W = PROJ_WIDTH // N_DEV
EPS = 1e-6

LANE = 128
ELT_W = 256

OFF_QKV, OFF_ZG, OFF_CONV, OFF_BA = 0, 3072, 4096, 8192
CONV_BLOCK = 4 * ELT_W
PROJ_PAD = 8448
NAT_BA, NAT_CONV = 4096, 4112


def _padded_col(n):
    if n < NAT_BA:
        return n
    if n < NAT_CONV:
        return OFF_BA + n - NAT_BA
    g, ch = divmod(n - NAT_CONV, CONV_WIDTH)
    j, r = divmod(ch, ELT_W)
    return OFF_CONV + CONV_BLOCK * j + ELT_W * g + r


def _layout_segments(n0, n1):
    cuts = [NAT_BA, NAT_CONV] + [NAT_CONV + ELT_W * k for k in range(1, 4 * CONV_WIDTH // ELT_W)]
    pts = [n0] + [c for c in cuts if n0 < c < n1] + [n1]
    return [(lo, hi - lo, _padded_col(lo)) for lo, hi in zip(pts, pts[1:])]

ADAM_LR, ADAM_B1, ADAM_B2, ADAM_EPS, ADAM_WD, ADAM_STEP = 0.001, 0.9, 0.999, 1e-08, 0.01, 10

V7X_VMEM_BYTES = 64 * 1024 * 1024
VMEM_LIMIT = V7X_VMEM_BYTES - 8 * 1024 * 1024

MESH = pl.DeviceIdType.MESH
ANY = pl.BlockSpec(memory_space=pl.ANY)


def _pcall(body, **kw):
    return pl.pallas_call(body, **kw)


def _cparams(*sem):
    return pltpu.CompilerParams(dimension_semantics=sem if sem else None, vmem_limit_bytes=VMEM_LIMIT)


def _mm(a, b):
    return jnp.dot(a.astype(bf16), b.astype(bf16), preferred_element_type=f32)


def _mm_nt(a, b):
    return lax.dot_general(a.astype(bf16), b.astype(bf16), (((1,), (1,)), ((), ())), preferred_element_type=f32)


def _cat16(parts, axis):
    return jnp.concatenate([p.astype(bf16) for p in parts], axis=axis)


def _mm_tn(a, b):
    return lax.dot_general(a.astype(bf16), b.astype(bf16), (((0,), (0,)), ((), ())), preferred_element_type=f32)


def _rows(shape):
    return lax.broadcasted_iota(jnp.int32, shape, 0)


def _lanes(shape):
    return lax.broadcasted_iota(jnp.int32, shape, 1)


def _shift_down(x, s):
    if s == 0:
        return x
    return jnp.where(_rows(x.shape) >= s, pltpu.roll(x, s, 0), 0.0)


def _shift_up(x, s):
    if s == 0:
        return x
    n = x.shape[0]
    return jnp.where(_rows(x.shape) < n - s, pltpu.roll(x, n - s, 0), 0.0)


def _sigmoid(x):
    return jax.nn.sigmoid(x)


def _softplus(x):
    e = jnp.exp(-jnp.abs(x))
    small = e * (1.0 - e * (0.5 - e * (1.0 / 3.0)))
    return jnp.maximum(x, 0.0) + jnp.where(e < 0.01, small, jnp.log(1.0 + e))


def _mesh_pos():
    return lax.axis_index("x"), lax.axis_index("y"), lax.axis_index("c")


def _flat(px, py, pc):
    return 4 * px + 2 * py + pc


def _all_gather(xs, name, pieces=None):
    n = len(xs)
    pieces = pieces or [1] * n
    items = [(a, q) for a in range(n) for q in range(pieces[a])]
    ni = len(items)

    def view(ref, i):
        a, q = items[i]
        if pieces[a] == 1:
            return ref
        wd = xs[a].shape[-1] // pieces[a]
        return ref.at[(slice(None),) * (xs[a].ndim - 1) + (pl.ds(q * wd, wd),)]

    def body(*refs):
        x_refs, o_refs = refs[:n], refs[n:2 * n]
        send_sems, recv_sems, local_sems = refs[2 * n:]
        x, y, c = _mesh_pos()
        me, sibling = (x, y, c), (x, y, 1 - c)
        flip = lambda v, bit: v + bit - 2 * v * bit
        nbr_a = (flip(x, 1 - c), flip(y, c))
        nbr_b = (flip(x, c), flip(y, 1 - c))
        diag = (1 - x, 1 - y)

        def copy(i, k, block, to, own=False):
            a = items[i][0]
            dst = view(o_refs[a].at[_flat(*block)], i)
            return pltpu.make_async_remote_copy(
                src_ref=view(x_refs[a], i) if own else dst, dst_ref=dst,
                send_sem=send_sems.at[i, k], recv_sem=recv_sems.at[i, k], device_id=to, device_id_type=MESH)

        mine, sent = [], []

        def go(cp):
            cp.start()
            sent.append(cp)

        for a in range(n):
            cp = pltpu.make_async_copy(x_refs[a], o_refs[a].at[_flat(*me)], local_sems.at[a])
            cp.start()
            mine.append(cp)
        for a in range(ni):
            go(copy(a, 1, me, (*nbr_a, c), own=True))
            go(copy(a, 2, me, (*nbr_b, c), own=True))
            go(copy(a, 0, me, sibling, own=True))
        for a in range(ni):
            copy(a, 1, (*nbr_a, c), me).wait_recv()
            go(copy(a, 3, (*nbr_a, c), (*nbr_b, c)))
            go(copy(a, 4, (*nbr_a, c), sibling))
        for a in range(ni):
            copy(a, 2, (*nbr_b, c), me).wait_recv()
            go(copy(a, 5, (*nbr_b, c), sibling))
        for a in range(ni):
            copy(a, 3, (*diag, c), me).wait_recv()
            go(copy(a, 6, (*diag, c), sibling))
        for a in range(ni):
            copy(a, 0, sibling, me).wait_recv()
            copy(a, 4, (*nbr_b, 1 - c), me).wait_recv()
            copy(a, 5, (*nbr_a, 1 - c), me).wait_recv()
            copy(a, 6, (*diag, 1 - c), me).wait_recv()
        for cp in sent:
            cp.wait_send()
        for cp in mine:
            cp.wait()

    outs = _pcall(
        body, name=name,
        out_shape=[jax.ShapeDtypeStruct((N_DEV,) + a.shape, a.dtype) for a in xs],
        in_specs=[ANY] * n, out_specs=[ANY] * n,
        scratch_shapes=[pltpu.SemaphoreType.DMA((ni, 7)), pltpu.SemaphoreType.DMA((ni, 7)), pltpu.SemaphoreType.DMA((n,))],
    )(*xs)
    return list(outs)


def _pair_exchange(gs, name):
    n = len(gs)
    chips = [(0, 0), (0, 1), (1, 0), (1, 1)]

    def body(*refs):
        g_refs, o_refs = refs[:n], refs[n:2 * n]
        send_sems, recv_sems = refs[2 * n:]
        x, y, c = _mesh_pos()
        sibling = (x, y, 1 - c)

        def copy(a, i):
            xp, yp = chips[i]
            return pltpu.make_async_remote_copy(
                src_ref=g_refs[a].at[_flat(xp, yp, 1 - c)], dst_ref=o_refs[a].at[i],
                send_sem=send_sems.at[a, i], recv_sem=recv_sems.at[a, i], device_id=sibling, device_id_type=MESH)

        cps = [copy(a, i) for a in range(n) for i in range(4)]
        for cp in cps:
            cp.start()
        for cp in cps:
            cp.wait()

    outs = _pcall(
        body, name=name,
        out_shape=[jax.ShapeDtypeStruct((4,) + a.shape[1:], a.dtype) for a in gs],
        in_specs=[ANY] * n, out_specs=[ANY] * n,
        scratch_shapes=[pltpu.SemaphoreType.DMA((n, 4)), pltpu.SemaphoreType.DMA((n, 4))],
    )(*gs)
    return list(outs)


def _pair_sum(g, p1, name):
    _, R, C = g.shape
    tc = C // 2
    cidx = lax.axis_index("c").astype(jnp.int32).reshape(1)

    def body(c_ref, g_ref, p_ref, o_ref):
        o_ref[...] = (g_ref[...].astype(f32) + p_ref[...].astype(f32)).astype(o_ref.dtype)

    return _pcall(
        body, name=name,
        grid_spec=pltpu.PrefetchScalarGridSpec(
            num_scalar_prefetch=1, grid=(4, C // tc),
            in_specs=[pl.BlockSpec((1, R, tc), lambda i, r, c_ref: (2 * i + c_ref[0], 0, r)),
                      pl.BlockSpec((1, R, tc), lambda i, r, c_ref: (i, 0, r))],
            out_specs=pl.BlockSpec((1, R, tc), lambda i, r, c_ref: (i, 0, r))),
        out_shape=jax.ShapeDtypeStruct((4, R, C), g.dtype),
        compiler_params=_cparams("parallel", "parallel"),
    )(cidx, g, p1)


def _axis_sum(s, got, name):
    _, R, C = s.shape
    x, y, c = _mesh_pos()
    me, _, b, _ = _axis_chips(x, y, c)
    idx = jnp.stack([2 * me[0] + me[1], 2 * b[0] + b[1]]).astype(jnp.int32)

    def body(idx_ref, s_ref, g_ref, o_ref):
        o_ref[...] = (s_ref[...].astype(f32) + g_ref[...].astype(f32)).astype(o_ref.dtype)

    return _pcall(
        body, name=name,
        grid_spec=pltpu.PrefetchScalarGridSpec(
            num_scalar_prefetch=1, grid=(2, 2),
            in_specs=[pl.BlockSpec((1, R, C // 2), lambda k, h, idx_ref: (idx_ref[k], 0, h)),
                      pl.BlockSpec((1, R, C // 2), lambda k, h, idx_ref: (k, 0, h))],
            out_specs=pl.BlockSpec((1, R, C // 2), lambda k, h, idx_ref: (k, 0, h))),
        out_shape=jax.ShapeDtypeStruct((2, R, C), s.dtype),
        compiler_params=_cparams("parallel", "parallel"),
    )(idx, s, got)


HBM = pl.BlockSpec(memory_space=pltpu.HBM)
SEM = pl.BlockSpec(memory_space=pltpu.SEMAPHORE)
EFFECT = pltpu.SideEffectType.DATAFLOW_SIDE_EFFECTING


def _peers(x, y, c):
    out = []
    for k in range(1, N_DEV):
        kx, ky, kc = (k >> 2) & 1, (k >> 1) & 1, k & 1
        out.append(((1 - x) if kx else x, (1 - y) if ky else y, (1 - c) if kc else c))
    return out


SPREAD_COPIES = {"gather": N_DEV - 1, "scatter": N_DEV - 1, "axis_a": 2, "axis_b": 1}
SPREAD_SLOTS = {"axis_a": 2, "axis_b": 1}


def _axis_chips(x, y, c):
    flip = lambda v, bit: v + bit - 2 * v * bit
    return (x, y), (flip(x, 1 - c), flip(y, c)), (flip(x, c), flip(y, 1 - c)), (1 - x, 1 - y)


def _spread_copy(src_ref, land_ref, send_sems, recv_sems, k, plan):
    x, y, c = _mesh_pos()
    if plan in ("axis_a", "axis_b"):
        _, a, b, d = _axis_chips(x, y, c)
        chip = lambda p: 2 * p[0] + p[1]
        peer = (*(a if plan == "axis_a" else b), c)
        src = src_ref.at[chip(a) if k == 0 else chip(d)] if plan == "axis_a" else src_ref.at[1]
        slot = k
    else:
        peer = _peers(x, y, c)[k]
        src, slot = (src_ref.at[_flat(*peer)] if plan == "scatter" else src_ref), _flat(x, y, c)
    return pltpu.make_async_remote_copy(
        src_ref=src, dst_ref=land_ref.at[slot], send_sem=send_sems.at[k], recv_sem=recv_sems.at[k],
        device_id=peer, device_id_type=MESH)


def _own_copy(src_ref, land_ref, send_sems, plan):
    me = _flat(*_mesh_pos())
    return pltpu.make_async_copy(src_ref.at[me] if plan == "scatter" else src_ref, land_ref.at[me],
                                 send_sems.at[SPREAD_COPIES[plan]])


def _spread_start(src, after, plan, name):
    land_shape = (N_DEV,) + src.shape if plan == "gather" else src.shape
    if plan in SPREAD_SLOTS:
        land_shape = (SPREAD_SLOTS[plan],) + src.shape[1:]
    n_copies = SPREAD_COPIES[plan]

    def body(src_ref, land_ref, after_ref, send_sems, recv_sems, src_thru, land_thru, token):
        for k in range(n_copies):
            _spread_copy(src_ref, land_ref, send_sems, recv_sems, k, plan).start()
        if plan not in SPREAD_SLOTS:
            _own_copy(src_ref, land_ref, send_sems, plan).start()
        token[...] = jnp.zeros_like(token)

    return _pcall(
        body, name=name,
        out_shape=(pltpu.SemaphoreType.DMA((n_copies + (plan not in SPREAD_SLOTS),)), pltpu.SemaphoreType.DMA((n_copies,)),
                   pltpu.HBM(src.shape, src.dtype), pltpu.HBM(land_shape, src.dtype), jax.ShapeDtypeStruct((8, LANE), f32)),
        in_specs=(HBM, HBM, ANY), out_specs=(SEM, SEM, HBM, HBM, pl.BlockSpec(memory_space=pltpu.VMEM)),
        input_output_aliases={0: 2, 1: 3},
        compiler_params=pltpu.CompilerParams(has_side_effects=EFFECT),
    )(pltpu.with_memory_space_constraint(src, pltpu.HBM),
      pltpu.with_memory_space_constraint(lax.empty(land_shape, src.dtype), pltpu.HBM), after)


def _spread_wait(started, after, plan, name):
    send_sems, recv_sems, src_thru, land_thru, _ = started

    def body(src_ref, land_ref, send_sems, recv_sems, after_ref, src_dead, got_ref):
        for k in range(SPREAD_COPIES[plan]):
            cp = _spread_copy(src_ref, land_ref, send_sems, recv_sems, k, plan)
            cp.wait_send()
            cp.wait_recv()
        if plan not in SPREAD_SLOTS:
            _own_copy(src_ref, land_ref, send_sems, plan).wait()

    return _pcall(
        body, name=name,
        out_shape=(pltpu.HBM(src_thru.shape, src_thru.dtype), pltpu.HBM(land_thru.shape, land_thru.dtype)),
        in_specs=(HBM, HBM, SEM, SEM, ANY), out_specs=(HBM, HBM), input_output_aliases={0: 0, 1: 1},
        compiler_params=pltpu.CompilerParams(has_side_effects=EFFECT),
    )(src_thru, land_thru, send_sems, recv_sems, after)


COL_TILE = 256


def _cast_weights(w3, wo):
    n = w3.shape[0]

    def body(w_ref, wo_ref, o_ref, oo_ref):
        o_ref[...] = w_ref[:, 0, :].astype(bf16)
        oo_ref[...] = wo_ref[...].astype(bf16)

    tile = 2 * COL_TILE
    return _pcall(
        body, name="cast_weights", grid=(D_MODEL // tile,),
        in_specs=[pl.BlockSpec((n, 1, tile), lambda j: (0, 0, j)), pl.BlockSpec((wo.shape[0], tile), lambda j: (0, j))],
        out_specs=[pl.BlockSpec((n, tile), lambda j: (0, j)), pl.BlockSpec((wo.shape[0], tile), lambda j: (0, j))],
        out_shape=[jax.ShapeDtypeStruct((n, D_MODEL), bf16), jax.ShapeDtypeStruct(wo.shape, bf16)],
        compiler_params=_cparams("parallel"),
    )(w3, wo)


def _relayout_w_in(win_g):
    def body(g_ref, o_ref):
        used = OFF_BA + NAT_CONV - NAT_BA
        o_ref[used:PROJ_PAD, :] = jnp.zeros((PROJ_PAD - used, COL_TILE), o_ref.dtype)
        for d in range(N_DEV):
            for lo, width, dst in _layout_segments(d * SHARD_W, (d + 1) * SHARD_W):
                src = lo - d * SHARD_W
                o_ref[dst:dst + width, :] = g_ref[d, src:src + width, :]

    return _pcall(
        body, name="relayout_w_in", grid=(D_MODEL // COL_TILE,),
        in_specs=[pl.BlockSpec((N_DEV, SHARD_W, COL_TILE), lambda j: (0, 0, j))],
        out_specs=pl.BlockSpec((PROJ_PAD, COL_TILE), lambda j: (0, j)),
        out_shape=jax.ShapeDtypeStruct((PROJ_PAD, D_MODEL), win_g.dtype),
        compiler_params=_cparams("parallel"),
    )(win_g)


def _grad_blocks(g_t):
    def body(p_ref, o_ref):
        for d in range(N_DEV):
            for lo, width, src in _layout_segments(d * SHARD_W, (d + 1) * SHARD_W):
                dst = lo - d * SHARD_W
                o_ref[d, dst:dst + width, :] = p_ref[src:src + width, :]

    return _pcall(
        body, name="grad_blocks", grid=(D_MODEL // COL_TILE,),
        in_specs=[pl.BlockSpec((PROJ_PAD, COL_TILE), lambda j: (0, j))],
        out_specs=pl.BlockSpec((N_DEV, SHARD_W, COL_TILE), lambda j: (0, 0, j)),
        out_shape=jax.ShapeDtypeStruct((N_DEV, SHARD_W, D_MODEL), bf16),
        compiler_params=_cparams("parallel"),
    )(g_t)


def _in_proj(x, nw, wpad_t, after):
    L = x.shape[0]
    tn = 768
    nj = wpad_t.shape[0] // tn

    def body(x_ref, nw_ref, w_ref, after_ref, proj_ref, h_ref):
        first = pl.program_id(0) == 0

        def project(r, n, hv):
            proj_ref[r:r + n, :] = lax.dot_general(hv, w_ref[...], (((1,), (1,)), ((), ())), preferred_element_type=f32)

        @pl.when(first)
        def _():
            for r in range(0, L, 256):
                xs = x_ref[r:r + 256, :]
                ms = jnp.mean(xs * xs, axis=-1, keepdims=True)
                hv = ((xs * lax.rsqrt(ms + EPS)) * nw_ref[...]).astype(bf16)
                h_ref[r:r + 256, :] = hv
                project(r, 256, hv)

        @pl.when(jnp.logical_not(first))
        def _():
            for r in range(0, L, 512):
                project(r, 512, h_ref[r:r + 512, :])

    return _pcall(
        body, name="in_proj", grid=(nj,),
        in_specs=[pl.BlockSpec((L, D_MODEL), lambda j: (0, 0)), pl.BlockSpec((1, D_MODEL), lambda j: (0, 0)),
                  pl.BlockSpec((tn, D_MODEL), lambda j: (j, 0)), ANY],
        out_specs=[pl.BlockSpec((L, tn), lambda j: (0, j)), pl.BlockSpec((L, D_MODEL), lambda j: (0, 0))],
        out_shape=[jax.ShapeDtypeStruct((L, wpad_t.shape[0]), f32), jax.ShapeDtypeStruct((L, D_MODEL), bf16)],
        compiler_params=_cparams("arbitrary"),
    )(x, nw, wpad_t, after)


HALVES = [slice(i * LANE, (i + 1) * LANE) for i in range(ELT_W // LANE)]
QKV_W = 512
QKV_HEADS = [slice(i * LANE, (i + 1) * LANE) for i in range(QKV_W // LANE)]
STEPS_PER_GROUP = GDN_WIDTH // QKV_W


def _conv4(x, cw_ref, ls):
    return (cw_ref[3:4, ls] * x + cw_ref[2:3, ls] * _shift_down(x, 1) + cw_ref[1:2, ls] * _shift_down(x, 2)
            + cw_ref[0:1, ls] * _shift_down(x, 3))


def _qkv_act(proj, cw):
    L = proj.shape[0]

    def body(x_ref, cw_ref, o_ref):
        j = pl.program_id(0)
        scale = jnp.where(j < STEPS_PER_GROUP, HEAD_DIM ** -0.5, 1.0).astype(f32)
        for ls in QKV_HEADS:
            c = _conv4(x_ref[:, ls], cw_ref, ls)
            a = c * _sigmoid(c)
            rn = lax.rsqrt(jnp.sum(a * a, axis=1, keepdims=True) + EPS)
            o_ref[:, ls] = jnp.where(j < 2 * STEPS_PER_GROUP, (a * rn) * scale, a)

    return _pcall(
        body, name="qkv_act", grid=(3 * STEPS_PER_GROUP,),
        in_specs=[pl.BlockSpec((L, QKV_W), lambda j: (0, j)), pl.BlockSpec((4, QKV_W), lambda j: (0, j))],
        out_specs=pl.BlockSpec((L, QKV_W), lambda j: (0, j)),
        out_shape=jax.ShapeDtypeStruct((L, 3 * GDN_WIDTH), f32),
        compiler_params=_cparams("parallel"),
    )(proj, cw)


def _scalars(proj, alog_p, dtb_p):
    L = proj.shape[0]
    nc = L // CHUNK

    def body(x_ref, al_ref, dt_ref, sc_ref, gr_ref):
        x = x_ref[...]
        lane = _lanes(x.shape)
        beta = _sigmoid(x)
        g = -jnp.exp(al_ref[...]) * _softplus(x + dt_ref[...])
        gc = jnp.where((lane >= HEADS) & (lane < 2 * HEADS), g, 0.0)
        rc = _rows(x.shape) & (CHUNK - 1)
        for s in (1, 2, 4, 8, 16, 32):
            gc = gc + jnp.where(rc >= s, pltpu.roll(gc, s, 0), 0.0)
        sc_ref[...] = jnp.where(lane < HEADS, beta, gc)
        sel = (_lanes((HEADS, LANE)) == _rows((HEADS, LANE)) + HEADS).astype(f32)
        for c in range(nc):
            gr_ref[c] = lax.dot_general(sel, sc_ref[c * CHUNK:(c + 1) * CHUNK, :], (((1,), (1,)), ((), ())),
                                        preferred_element_type=f32, precision=lax.Precision.HIGHEST)

    return _pcall(
        body, name="scalars", grid=(1,),
        in_specs=[pl.BlockSpec((L, LANE), lambda i: (0, OFF_BA // LANE)), pl.BlockSpec((1, LANE), lambda i: (0, 0)),
                  pl.BlockSpec((1, LANE), lambda i: (0, 0))],
        out_specs=[pl.BlockSpec((L, LANE), lambda i: (0, 0)), pl.BlockSpec((nc, HEADS, CHUNK), lambda i: (0, 0, 0))],
        out_shape=[jax.ShapeDtypeStruct((L, LANE), f32), jax.ShapeDtypeStruct((nc, HEADS, CHUNK), f32)],
        compiler_params=_cparams("arbitrary"),
    )(proj, alog_p, dtb_p)


def _head_scalars(sc, gr_ref, h, ci=0):
    lane = _lanes(sc.shape)
    beta = jnp.sum(jnp.where(lane == h, sc, 0.0), axis=1, keepdims=True)
    gcc = jnp.sum(jnp.where(lane == HEADS + h, sc, 0.0), axis=1, keepdims=True)
    gcr = gr_ref[ci, h:h + 1, :]
    gl = jnp.sum(jnp.where(_lanes(gcr.shape) == CHUNK - 1, gcr, 0.0), axis=1, keepdims=True)
    ii, jj = _rows((CHUNK, CHUNK)), _lanes((CHUNK, CHUNK))
    dmat = jnp.where(ii >= jj, jnp.exp(jnp.minimum(gcc - gcr, 0.0)), 0.0)
    dmat_t = jnp.where(jj >= ii, jnp.exp(jnp.minimum(gcr - gcc, 0.0)), 0.0)
    return beta, gcc, gl, dmat, dmat_t, ii, jj


def _gdn_fwd(qkv, sc, gr):
    L = qkv.shape[0]
    nc = L // CHUNK
    W = GDN_WIDTH
    cps = GDN_CPS if nc % GDN_CPS == 0 else 1
    rows_per_step = cps * CHUNK

    def body(qkv_ref, sc_ref, gr_ref, o_ref, u_ref, w_ref, vn_ref, t_ref, sp_ref, s_scr):
        @pl.when(pl.program_id(0) == 0)
        def _():
            s_scr[...] = jnp.zeros_like(s_scr)
        HS = range(cps * HEADS)
        hd = [i % HEADS for i in HS]
        rs = [slice((i // HEADS) * CHUNK, (i // HEADS + 1) * CHUNK) for i in HS]
        cs = [slice(hd[i] * HEAD_DIM, (hd[i] + 1) * HEAD_DIM) for i in HS]
        q = [qkv_ref[rs[i], hd[i] * HEAD_DIM:(hd[i] + 1) * HEAD_DIM] for i in HS]
        k = [qkv_ref[rs[i], W + hd[i] * HEAD_DIM:W + (hd[i] + 1) * HEAD_DIM] for i in HS]
        v = [qkv_ref[rs[i], 2 * W + hd[i] * HEAD_DIM:2 * W + (hd[i] + 1) * HEAD_DIM] for i in HS]
        hsc = [_head_scalars(sc_ref[rs[i], :], gr_ref, hd[i], i // HEADS) for i in HS]
        beta, gcc, gl, dmat = ([x[i] for x in hsc] for i in range(4))
        ii, jj = hsc[0][5], hsc[0][6]
        eg = [jnp.exp(gcc[h]) for h in HS]
        kb = [k[h] * beta[h] for h in HS]
        kk = [_mm_nt(kb[h], k[h]) for h in HS]
        qk = [_mm_nt(q[h], k[h]) for h in HS]
        n0 = [-jnp.where(ii > jj, kk[h] * dmat[h], 0.0) for h in HS]
        n1 = [_mm(n0[h], n0[h]) for h in HS]
        n2 = [_mm(n1[h], n1[h]) for h in HS]
        p01 = [n0[h] + n1[h] + _mm(n0[h], n1[h]) for h in HS]
        n3 = [_mm(n2[h], n2[h]) for h in HS]
        n4 = [_mm(n3[h], n3[h]) for h in HS]
        p23 = [n2[h] + n3[h] + _mm(n2[h], n3[h]) for h in HS]
        n5 = [_mm(n4[h], n4[h]) for h in HS]
        p03 = [p01[h] + p23[h] + _mm(p01[h], p23[h]) for h in HS]
        p45 = [n4[h] + n5[h] + _mm(n4[h], n5[h]) for h in HS]
        t = [p03[h] + p45[h] + _mm(p03[h], p45[h]) for h in HS]
        vb = [v[h] * beta[h] for h in HS]
        kbg = [kb[h] * eg[h] for h in HS]
        uw = [_mm(t[h], _cat16([vb[h], kbg[h]], 1)) for h in HS]
        u = [vb[h] + uw[h][:, :HEAD_DIM] for h in HS]
        w = [kbg[h] + uw[h][:, HEAD_DIM:] for h in HS]
        wq = [_cat16([w[h], q[h] * eg[h]], 0) for h in HS]
        p = [jnp.where(ii >= jj, qk[h] * dmat[h], 0.0) for h in HS]
        ks = [k[h] * jnp.exp(gl[h] - gcc[h]) for h in HS]
        s = [s_scr[h] for h in range(HEADS)]
        for ci in range(cps):
            IS = range(ci * HEADS, (ci + 1) * HEADS)
            ws = [_mm(wq[i], s[hd[i]]) for i in IS]
            vn = [u[i] - ws[hd[i]][:CHUNK] for i in IS]
            pv = [_mm(p[i], vn[hd[i]]) for i in IS]
            kv = [_mm_tn(ks[i], vn[hd[i]]) for i in IS]
            for i in IS:
                h = hd[i]
                sp_ref[ci, cs[i], :] = s[h]
                o_ref[rs[i], cs[i]] = ws[h][CHUNK:] + pv[h]
                vn_ref[rs[i], cs[i]] = vn[h].astype(bf16)
            s = [jnp.exp(gl[i]) * s[hd[i]] + kv[hd[i]] for i in IS]
        for h in range(HEADS):
            s_scr[h] = s[h]
        for i in HS:
            u_ref[rs[i], cs[i]] = u[i].astype(bf16)
            w_ref[rs[i], cs[i]] = w[i].astype(bf16)
            t_ref[i // HEADS, hd[i]] = t[i].astype(bf16)

    row = lambda c: (c, 0)
    act, act16 = jax.ShapeDtypeStruct((L, W), f32), jax.ShapeDtypeStruct((L, W), bf16)
    return _pcall(
        body, name="gdn_fwd", grid=(nc // cps,),
        in_specs=[pl.BlockSpec((rows_per_step, 3 * W), row), pl.BlockSpec((rows_per_step, LANE), row),
                  pl.BlockSpec((cps, HEADS, CHUNK), lambda c: (c, 0, 0))],
        out_specs=[pl.BlockSpec((rows_per_step, W), row)] * 4 + [
            pl.BlockSpec((cps, HEADS, CHUNK, CHUNK), lambda c: (c, 0, 0, 0)),
            pl.BlockSpec((cps, W, HEAD_DIM), lambda c: (c, 0, 0))],
        out_shape=[act, act16, act16, act16, jax.ShapeDtypeStruct((nc, HEADS, CHUNK, CHUNK), bf16),
                   jax.ShapeDtypeStruct((nc, W, HEAD_DIM), f32)],
        scratch_shapes=[pltpu.VMEM((HEADS, HEAD_DIM, HEAD_DIM), f32)],
        compiler_params=_cparams("arbitrary"),
    )(qkv, sc, gr)


def _gdn_gate(o, proj, gnw):
    L = o.shape[0]

    def body(o_ref, z_ref, w_ref, m_ref):
        for ls in HALVES:
            ov, z = o_ref[:, ls], z_ref[:, ls]
            rms = lax.rsqrt(jnp.mean(ov * ov, axis=-1, keepdims=True) + EPS)
            m_ref[:, ls] = (((ov * rms) * w_ref[...]) * (z * _sigmoid(z))).astype(bf16)

    return _pcall(
        body, name="gdn_gate", grid=(GDN_WIDTH // ELT_W,),
        in_specs=[pl.BlockSpec((L, ELT_W), lambda j: (0, j)), pl.BlockSpec((L, ELT_W), lambda j: (0, OFF_ZG // ELT_W + j)),
                  pl.BlockSpec((1, LANE), lambda j: (0, 0))],
        out_specs=pl.BlockSpec((L, ELT_W), lambda j: (0, j)),
        out_shape=jax.ShapeDtypeStruct((L, GDN_WIDTH + CONV_WIDTH), bf16),
        compiler_params=_cparams("parallel"),
    )(o, proj, gnw)


def _conv3(u, cw_ref, ls):
    return cw_ref[2:3, ls] * u + cw_ref[1:2, ls] * _shift_down(u, 1) + cw_ref[0:1, ls] * _shift_down(u, 2)


def _conv_specs(L):
    return [pl.BlockSpec((L, CONV_BLOCK), lambda j: (0, OFF_CONV // CONV_BLOCK + j)),
            pl.BlockSpec((3, ELT_W), lambda j: (0, j)), pl.BlockSpec((1, ELT_W), lambda j: (0, j))]


def _conv_parts(ls):
    return [slice(g * ELT_W + ls.start, g * ELT_W + ls.stop) for g in range(4)]


def _conv_fwd(proj, cw, cb, mix):
    L = proj.shape[0]

    def body(p_ref, cw_ref, cb_ref, mix_in, m_ref):
        for ls in HALVES:
            sb, sc_, sh, sz = _conv_parts(ls)
            z = p_ref[:, sz]
            cv = _conv3(p_ref[:, sc_] * p_ref[:, sh], cw_ref, ls) + cb_ref[:, ls]
            m_ref[:, ls] = ((p_ref[:, sb] * cv) * (z * _sigmoid(z))).astype(bf16)

    return _pcall(
        body, name="conv_fwd", grid=(CONV_WIDTH // ELT_W,),
        in_specs=_conv_specs(L) + [ANY], out_specs=pl.BlockSpec((L, ELT_W), lambda j: (0, GDN_WIDTH // ELT_W + j)),
        out_shape=jax.ShapeDtypeStruct(mix.shape, mix.dtype), input_output_aliases={3: 0},
        compiler_params=_cparams("parallel"),
    )(proj, cw, cb, mix)


def _out_proj_loss(x, mix, wo, fw, tgt):
    L = x.shape[0]
    tm = min(512, L)
    MW = GDN_WIDTH + CONV_WIDTH

    def body(x_ref, m_ref, wo_ref, fw_ref, t_ref, dy_ref, dyb_ref, dm_ref, gfw_ref, loss_ref):
        @pl.when(pl.program_id(0) == 0)
        def _():
            gfw_ref[...] = jnp.zeros_like(gfw_ref)
            loss_ref[...] = jnp.zeros_like(loss_ref)
        y = x_ref[...] + jnp.dot(m_ref[...], wo_ref[...], preferred_element_type=f32)
        r = lax.rsqrt(jnp.mean(y * y, axis=-1, keepdims=True) + EPS)
        yh = y * r
        fwv = fw_ref[...]
        diff = yh * fwv - t_ref[...]
        loss_ref[...] += jnp.sum(jnp.sum(diff * diff, axis=-1, keepdims=True), axis=0, keepdims=True) * (0.5 / D_MODEL)
        dout = diff * (1.0 / D_MODEL)
        gfw_ref[...] += jnp.sum(dout * yh, axis=0, keepdims=True)
        dyh = dout * fwv
        dy = r * (dyh - yh * jnp.mean(dyh * yh, axis=-1, keepdims=True))
        dy_ref[...] = dy
        dyb = dy.astype(bf16)
        dyb_ref[...] = dyb
        dm_ref[...] = lax.dot_general(dyb, wo_ref[...], (((1,), (1,)), ((), ())), preferred_element_type=f32)

    row = lambda i: (i, 0)
    fix = lambda i: (0, 0)
    act = jax.ShapeDtypeStruct((L, D_MODEL), f32)
    return _pcall(
        body, name="out_proj_loss", grid=(L // tm,),
        in_specs=[pl.BlockSpec((tm, D_MODEL), row), pl.BlockSpec((tm, MW), row), pl.BlockSpec((MW, D_MODEL), fix),
                  pl.BlockSpec((1, D_MODEL), fix), pl.BlockSpec((tm, D_MODEL), row)],
        out_specs=[pl.BlockSpec((tm, D_MODEL), row), pl.BlockSpec((tm, D_MODEL), row), pl.BlockSpec((tm, MW), row),
                   pl.BlockSpec((1, D_MODEL), fix), pl.BlockSpec((1, LANE), fix)],
        out_shape=[act, jax.ShapeDtypeStruct((L, D_MODEL), bf16), jax.ShapeDtypeStruct((L, MW), f32),
                   jax.ShapeDtypeStruct((1, D_MODEL), f32), jax.ShapeDtypeStruct((1, LANE), f32)],
        compiler_params=_cparams("arbitrary"),
    )(x, mix, wo, fw, tgt)


def _tn_matmul(a, b, name):
    L, M = a.shape
    N = b.shape[1]
    tm = 512 if M % 512 == 0 else (768 if M % 768 == 0 else M)

    def body(a_ref, b_ref, o_ref):
        o_ref[...] = lax.dot_general(a_ref[...], b_ref[...], (((0,), (0,)), ((), ())),
                                     preferred_element_type=f32).astype(o_ref.dtype)

    return _pcall(
        body, name=name, grid=(M // tm,),
        in_specs=[pl.BlockSpec((L, tm), lambda i: (0, i)), pl.BlockSpec((L, N), lambda i: (0, 0))],
        out_specs=pl.BlockSpec((tm, N), lambda i: (i, 0)),
        out_shape=jax.ShapeDtypeStruct((M, N), bf16),
        compiler_params=_cparams("parallel"),
    )(a, b)


def _gdn_gate_bwd(o, proj, gnw, dmix_a, after):
    L = o.shape[0]

    def body(o_ref, z_ref, w_ref, dm_ref, after_ref, do_ref, dz_ref, gw_ref):
        @pl.when(pl.program_id(0) == 0)
        def _():
            gw_ref[...] = jnp.zeros_like(gw_ref)
        wv = w_ref[...]
        for ls in HALVES:
            ov, z, dm = o_ref[:, ls], z_ref[:, ls], dm_ref[:, ls]
            rms = lax.rsqrt(jnp.mean(ov * ov, axis=-1, keepdims=True) + EPS)
            xh = ov * rms
            sg = _sigmoid(z)
            d_on = dm * (z * sg)
            dz_ref[:, ls] = (dm * (xh * wv) * (sg * (1.0 + z * (1.0 - sg)))).astype(bf16)
            gw_ref[...] += jnp.sum(d_on * xh, axis=0, keepdims=True)
            dxh = d_on * wv
            do_ref[:, ls] = (rms * (dxh - xh * jnp.mean(dxh * xh, axis=-1, keepdims=True))).astype(bf16)

    wide = pl.BlockSpec((L, ELT_W), lambda j: (0, j))
    return _pcall(
        body, name="gdn_gate_bwd", grid=(GDN_WIDTH // ELT_W,),
        in_specs=[wide, pl.BlockSpec((L, ELT_W), lambda j: (0, OFF_ZG // ELT_W + j)),
                  pl.BlockSpec((1, LANE), lambda j: (0, 0)), wide, ANY],
        out_specs=[wide, pl.BlockSpec((L, ELT_W), lambda j: (0, OFF_ZG // ELT_W + j)),
                   pl.BlockSpec((1, LANE), lambda j: (0, 0))],
        out_shape=[jax.ShapeDtypeStruct((L, GDN_WIDTH), bf16), jax.ShapeDtypeStruct((L, PROJ_PAD), bf16),
                   jax.ShapeDtypeStruct((1, LANE), f32)],
        compiler_params=_cparams("arbitrary"),
    )(o, proj, gnw, dmix_a, after)


def _conv_bwd(proj, cw, cb, dmix_b, dproj):
    L = proj.shape[0]

    def body(p_ref, cw_ref, cb_ref, dm_ref, dproj_in, dp_ref, gcw_ref, gcb_ref):
        for ls in HALVES:
            sb, sc_, sh, sz_ = _conv_parts(ls)
            bv, cv_, hv, z, dm = p_ref[:, sb], p_ref[:, sc_], p_ref[:, sh], p_ref[:, sz_], dm_ref[:, ls]
            u = cv_ * hv
            cv = _conv3(u, cw_ref, ls) + cb_ref[:, ls]
            sg = _sigmoid(z)
            sz = z * sg
            dp_ref[:, sb] = (dm * cv * sz).astype(bf16)
            dp_ref[:, sz_] = (dm * (bv * cv) * (sg * (1.0 + z * (1.0 - sg)))).astype(bf16)
            dcv = dm * bv * sz
            gcb_ref[:, ls] = jnp.sum(dcv, axis=0, keepdims=True)
            dcv1, dcv2 = _shift_up(dcv, 1), _shift_up(dcv, 2)
            gcw_ref[2:3, ls] = jnp.sum(dcv * u, axis=0, keepdims=True)
            gcw_ref[1:2, ls] = jnp.sum(dcv1 * u, axis=0, keepdims=True)
            gcw_ref[0:1, ls] = jnp.sum(dcv2 * u, axis=0, keepdims=True)
            du = cw_ref[2:3, ls] * dcv + cw_ref[1:2, ls] * dcv1 + cw_ref[0:1, ls] * dcv2
            dp_ref[:, sc_] = (du * hv).astype(bf16)
            dp_ref[:, sh] = (du * cv_).astype(bf16)

    return _pcall(
        body, name="conv_bwd", grid=(CONV_WIDTH // ELT_W,),
        in_specs=_conv_specs(L) + [pl.BlockSpec((L, ELT_W), lambda j: (0, GDN_WIDTH // ELT_W + j)), ANY],
        out_specs=[pl.BlockSpec((L, CONV_BLOCK), lambda j: (0, OFF_CONV // CONV_BLOCK + j)),
                   pl.BlockSpec((3, ELT_W), lambda j: (0, j)), pl.BlockSpec((1, ELT_W), lambda j: (0, j))],
        out_shape=[jax.ShapeDtypeStruct(dproj.shape, dproj.dtype), jax.ShapeDtypeStruct((3, CONV_WIDTH), f32),
                   jax.ShapeDtypeStruct((1, CONV_WIDTH), f32)],
        input_output_aliases={4: 0},
        compiler_params=_cparams("parallel"),
    )(proj, cw, cb, dmix_b, dproj)


def _gdn_bwd(qkv, sc, gr, u_all, w_all, vn_all, t_all, sp_all, do_all):
    L = qkv.shape[0]
    nc = L // CHUNK
    W = GDN_WIDTH
    cps = GDN_CPS_BWD if nc % GDN_CPS_BWD == 0 else 1
    rows_per_step = cps * CHUNK
    nsteps = nc // cps

    def body(qkv_ref, sc_ref, gr_ref, u_ref, w_ref, vn_ref, t_ref, sp_ref, do_ref, dqkv_ref, dsc_ref, dgr_ref, ds_scr):
        @pl.when(pl.program_id(0) == 0)
        def _():
            ds_scr[...] = jnp.zeros_like(ds_scr)
        nh, base = HEADS, 0
        HS = range(cps * nh)
        hl = [i % nh for i in HS]
        hd = [base + hl[i] for i in HS]
        rs = [slice((i // nh) * CHUNK, (i // nh + 1) * CHUNK) for i in HS]
        cs = [slice(hd[i] * HEAD_DIM, (hd[i] + 1) * HEAD_DIM) for i in HS]
        q = [qkv_ref[rs[i], hd[i] * HEAD_DIM:(hd[i] + 1) * HEAD_DIM] for i in HS]
        k = [qkv_ref[rs[i], W + hd[i] * HEAD_DIM:W + (hd[i] + 1) * HEAD_DIM] for i in HS]
        v = [qkv_ref[rs[i], 2 * W + hd[i] * HEAD_DIM:2 * W + (hd[i] + 1) * HEAD_DIM] for i in HS]
        hsc = [_head_scalars(sc_ref[rs[i], :], gr_ref, hd[i], i // nh) for i in HS]
        beta, gcc, gl, dmat, dmat_t = ([x[i] for x in hsc] for i in range(5))
        ii, jj = hsc[0][5], hsc[0][6]
        eg = [jnp.exp(gcc[h]) for h in HS]
        ekl = [jnp.exp(gl[h] - gcc[h]) for h in HS]
        egl = [jnp.exp(gl[h]) for h in HS]
        kb = [k[h] * beta[h] for h in HS]
        ks = [k[h] * ekl[h] for h in HS]
        do = [do_ref[rs[h], cs[h]] for h in HS]
        vn = [vn_ref[rs[h], cs[h]] for h in HS]
        s = [sp_ref[h // nh, cs[h], :] for h in HS]
        w = [w_ref[rs[h], cs[h]] for h in HS]
        qd = [q[h] * eg[h] for h in HS]

        kq = [_mm_nt(k[h], q[h]) for h in HS]
        p_t = [jnp.where(jj >= ii, kq[h] * dmat_t[h], 0.0) for h in HS]
        ptd = [_mm(p_t[h], do[h]) for h in HS]
        qw = [_cat16([qd[h], -w[h]], 0) for h in HS]
        dsn, dvn, dodv = [None] * len(HS), [None] * len(HS), [None] * len(HS)
        ds_cur = [ds_scr[base + h] for h in range(nh)]
        for ci in reversed(range(cps)):
            IS = range(ci * nh, (ci + 1) * nh)
            ksd = [_mm(ks[i], ds_cur[hl[i]]) for i in IS]
            for i in IS:
                dsn[i] = ds_cur[hl[i]]
                dvn[i] = ptd[i] + ksd[hl[i]]
                dodv[i] = _cat16([do[i], dvn[i]], 0)
            dsq = [_mm_tn(qw[i], dodv[i]) for i in IS]
            ds_cur = [egl[i] * ds_cur[hl[i]] + dsq[hl[i]] for i in IS]
        for h in range(nh):
            ds_scr[base + h] = ds_cur[h]
        x1 = [_mm_nt(dodv[h], s[h]) for h in HS]
        dks = [_mm_nt(vn[h], dsn[h]) for h in HS]
        dov = [_mm_nt(do[h], vn[h]) for h in HS]
        vdo = [_mm_nt(vn[h], do[h]) for h in HS]
        kk = [_mm_nt(kb[h], k[h]) for h in HS]
        qk = [_mm_nt(q[h], k[h]) for h in HS]
        dgl = [egl[h] * jnp.sum(jnp.sum(s[h] * dsn[h], axis=1, keepdims=True), axis=0, keepdims=True) for h in HS]
        dqd = [x1[h][:CHUNK] for h in HS]
        duw = [jnp.concatenate([dvn[h], -x1[h][CHUNK:]], axis=1) for h in HS]
        tdu = [_mm_tn(t_ref[h // nh, hd[h]], duw[h]) for h in HS]
        dvk = [duw[h] + tdu[h] for h in HS]
        uw = [jnp.concatenate([u_ref[rs[h], cs[h]], w[h]], axis=1) for h in HS]
        da = [-jnp.where(ii > jj, _mm_nt(dvk[h], uw[h]), 0.0) for h in HS]
        da_t = [-jnp.where(jj > ii, _mm_nt(uw[h], dvk[h]), 0.0) for h in HS]
        dp = [jnp.where(ii >= jj, dov[h], 0.0) for h in HS]
        dp_t = [jnp.where(jj >= ii, vdo[h], 0.0) for h in HS]
        r1 = [_mm(_cat16([da[h] * dmat[h], dp[h] * dmat[h]], 0), k[h]) for h in HS]
        dk1 = [_mm(_cat16([da_t[h] * dmat_t[h], dp_t[h] * dmat_t[h]], 1), _cat16([kb[h], q[h]], 0)) for h in HS]
        lane = _lanes((CHUNK, LANE))
        for ci in range(cps):
            dsc = jnp.zeros((CHUNK, LANE), f32)
            for i in range(ci * nh, (ci + 1) * nh):
                h = hd[i]
                a = jnp.where(ii > jj, kk[i] * dmat[i], 0.0)
                p = jnp.where(ii >= jj, qk[i] * dmat[i], 0.0)
                gmat = da[i] * a + dp[i] * p
                dvb, dkbg = dvk[i][:, :HEAD_DIM], dvk[i][:, HEAD_DIM:]
                kbg = kb[i] * eg[i]
                dkb = r1[i][:CHUNK] + dkbg * eg[i]
                dq = r1[i][CHUNK:] + dqd[i] * eg[i]
                dk = dk1[i] + dks[i] * ekl[i] + dkb * beta[i]
                dbeta = jnp.sum(dkb * k[i] + dvb * v[i], axis=1, keepdims=True)
                ksum = jnp.sum(dks[i] * ks[i], axis=1, keepdims=True)
                dgl_tot = dgl[i] + jnp.sum(ksum, axis=0, keepdims=True)
                dgc = (jnp.sum(gmat, axis=1, keepdims=True) + jnp.sum(dqd[i] * qd[i] + dkbg * kbg, axis=1, keepdims=True)
                       - ksum)
                dgc = dgc + jnp.where(_rows(dgc.shape) == CHUNK - 1, dgl_tot, 0.0)
                dqkv_ref[rs[i], h * HEAD_DIM:(h + 1) * HEAD_DIM] = dq
                dqkv_ref[rs[i], W + h * HEAD_DIM:W + (h + 1) * HEAD_DIM] = dk
                dqkv_ref[rs[i], 2 * W + h * HEAD_DIM:2 * W + (h + 1) * HEAD_DIM] = dvb * beta[i]
                dsc = jnp.where(lane == h, dbeta, jnp.where(lane == HEADS + h, dgc, dsc))
                dgr_ref[ci, h:h + 1, :] = jnp.sum(gmat, axis=0, keepdims=True)
            dsc_ref[ci * CHUNK:(ci + 1) * CHUNK, :] = dsc

    row = lambda c: (nsteps - 1 - c, 0)
    lead3 = lambda c: (nsteps - 1 - c, 0, 0)
    return _pcall(
        body, name="gdn_bwd", grid=(nsteps,),
        in_specs=[pl.BlockSpec((rows_per_step, 3 * W), row), pl.BlockSpec((rows_per_step, LANE), row),
                  pl.BlockSpec((cps, HEADS, CHUNK), lead3),
                  pl.BlockSpec((rows_per_step, W), row), pl.BlockSpec((rows_per_step, W), row),
                  pl.BlockSpec((rows_per_step, W), row),
                  pl.BlockSpec((cps, HEADS, CHUNK, CHUNK), lambda c: (nsteps - 1 - c, 0, 0, 0)),
                  pl.BlockSpec((cps, W, HEAD_DIM), lead3), pl.BlockSpec((rows_per_step, W), row)],
        out_specs=[pl.BlockSpec((rows_per_step, 3 * W), row), pl.BlockSpec((rows_per_step, LANE), row),
                   pl.BlockSpec((cps, HEADS, CHUNK), lead3)],
        out_shape=[jax.ShapeDtypeStruct((L, 3 * W), f32), jax.ShapeDtypeStruct((L, LANE), f32),
                   jax.ShapeDtypeStruct((nc, HEADS, CHUNK), f32)],
        scratch_shapes=[pltpu.VMEM((HEADS, HEAD_DIM, HEAD_DIM), f32)],
        compiler_params=_cparams("arbitrary"),
    )(qkv, sc, gr, u_all, w_all, vn_all, t_all, sp_all, do_all)


def _qkv_bwd(proj, cw, dn, dproj):
    L = proj.shape[0]

    def body(x_ref, cw_ref, dn_ref, dproj_in, dx_ref, gcw_ref):
        j = pl.program_id(0)
        steps = GDN_WIDTH // ELT_W
        scale = jnp.where(j < steps, HEAD_DIM ** -0.5, 1.0).astype(f32)
        for ls in HALVES:
            x, dn_v = x_ref[:, ls], dn_ref[:, ls]
            c = _conv4(x, cw_ref, ls)
            sg = _sigmoid(c)
            a = c * sg
            rn = lax.rsqrt(jnp.sum(a * a, axis=1, keepdims=True) + EPS)
            da_n = (scale * rn) * (dn_v - a * ((rn * rn) * jnp.sum(dn_v * a, axis=1, keepdims=True)))
            da = jnp.where(j < 2 * steps, da_n, dn_v)
            dc = da * (sg * (1.0 + c * (1.0 - sg)))
            dc1, dc2, dc3 = _shift_up(dc, 1), _shift_up(dc, 2), _shift_up(dc, 3)
            gcw_ref[3:4, ls] = jnp.sum(dc * x, axis=0, keepdims=True)
            gcw_ref[2:3, ls] = jnp.sum(dc1 * x, axis=0, keepdims=True)
            gcw_ref[1:2, ls] = jnp.sum(dc2 * x, axis=0, keepdims=True)
            gcw_ref[0:1, ls] = jnp.sum(dc3 * x, axis=0, keepdims=True)
            dx = cw_ref[3:4, ls] * dc + cw_ref[2:3, ls] * dc1 + cw_ref[1:2, ls] * dc2 + cw_ref[0:1, ls] * dc3
            dx_ref[:, ls] = dx.astype(bf16)

    col = pl.BlockSpec((L, ELT_W), lambda j: (0, j))
    wspec = pl.BlockSpec((4, ELT_W), lambda j: (0, j))
    return _pcall(
        body, name="qkv_bwd", grid=(3 * GDN_WIDTH // ELT_W,),
        in_specs=[col, wspec, col, ANY], out_specs=[col, wspec],
        out_shape=[jax.ShapeDtypeStruct(dproj.shape, dproj.dtype), jax.ShapeDtypeStruct((4, 3 * GDN_WIDTH), f32)],
        input_output_aliases={3: 0},
        compiler_params=_cparams("parallel"),
    )(proj, cw, dn, dproj)


def _scalars_bwd(proj, alog_p, dtb_p, dsc, dgr_col, dproj, after):
    L = proj.shape[0]

    def body(x_ref, al_ref, dt_ref, dsc_ref, dgr_ref, dproj_in, after_ref, dba_ref, gs_ref):
        x, dsc_v = x_ref[...], dsc_ref[...]
        lane = _lanes(x.shape)
        dec = (lane >= HEADS) & (lane < 2 * HEADS)
        dg = jnp.where(dec, dsc_v - dgr_ref[...], 0.0)
        rc = _rows(x.shape) & (CHUNK - 1)
        for s in (1, 2, 4, 8, 16, 32):
            dg = dg + jnp.where(rc + s < CHUNK, pltpu.roll(dg, L - s, 0), 0.0)
        xa = x + dt_ref[...]
        ea = jnp.exp(al_ref[...])
        g = -ea * _softplus(xa)
        da = dg * (-ea) * _sigmoid(xa)
        beta = _sigmoid(x)
        db = dsc_v * beta * (1.0 - beta)
        dba_ref[:, :LANE] = jnp.where(lane < HEADS, db, jnp.where(dec, da, 0.0)).astype(bf16)
        dba_ref[:, LANE:] = jnp.zeros((L, ELT_W - LANE), bf16)
        g_al = jnp.sum(jnp.where(dec, dg * g, 0.0), axis=0, keepdims=True)
        g_dt = jnp.sum(jnp.where(dec, da, 0.0), axis=0, keepdims=True)
        row8 = _rows(gs_ref.shape)
        gs = jnp.where(row8 == 0, g_al, jnp.where(row8 == 1, g_dt, 0.0))
        gs_ref[...] = pltpu.roll(gs, LANE - HEADS, 1)

    full = pl.BlockSpec((L, LANE), lambda i: (0, 0))
    vec = pl.BlockSpec((1, LANE), lambda i: (0, 0))
    return _pcall(
        body, name="scalars_bwd", grid=(1,),
        in_specs=[pl.BlockSpec((L, LANE), lambda i: (0, OFF_BA // LANE)), vec, vec, full, full, ANY, ANY],
        out_specs=[pl.BlockSpec((L, ELT_W), lambda i: (0, OFF_BA // ELT_W)), pl.BlockSpec((8, LANE), lambda i: (0, 0))],
        out_shape=[jax.ShapeDtypeStruct(dproj.shape, dproj.dtype), jax.ShapeDtypeStruct((8, LANE), f32)],
        input_output_aliases={5: 0},
        compiler_params=_cparams("arbitrary"),
    )(proj, alog_p, dtb_p, dsc, dgr_col, dproj, after)


def _input_grad(dproj, wpad, x, nw, dy, after):
    L = x.shape[0]
    tm = min(512, L)
    cuts = (0, 512, 1024, 2048, 3072, 5120, 7168, PROJ_PAD)
    nk = len(cuts) - 1

    def body(dp_ref, w_hbm, x_ref, nw_ref, dy_ref, after_ref, gx_ref, gnw_ref, w_vmem, sems):
        first = pl.program_id(0) == 0
        loads = [pltpu.make_async_copy(w_hbm.at[cuts[k]:cuts[k + 1], :], w_vmem.at[cuts[k]:cuts[k + 1], :], sems.at[k])
                 for k in range(nk)]

        @pl.when(first)
        def _():
            for cp in loads:
                cp.start()
            gnw_ref[...] = jnp.zeros_like(gnw_ref)
        dh = None
        for k in range(nk):
            pl.when(first)(loads[k].wait)
            part = jnp.dot(dp_ref[:, cuts[k]:cuts[k + 1]], w_vmem[cuts[k]:cuts[k + 1], :], preferred_element_type=f32)
            dh = part if dh is None else dh + part
        xv, nwv = x_ref[...], nw_ref[...]
        r = lax.rsqrt(jnp.mean(xv * xv, axis=-1, keepdims=True) + EPS)
        xh = xv * r
        gnw_ref[...] += jnp.sum(dh * xh, axis=0, keepdims=True)
        dxh = dh * nwv
        gx_ref[...] = dy_ref[...] + r * (dxh - xh * jnp.mean(dxh * xh, axis=-1, keepdims=True))

    row = lambda i: (i, 0)
    fix = lambda i: (0, 0)
    return _pcall(
        body, name="input_grad", grid=(L // tm,),
        in_specs=[pl.BlockSpec((tm, PROJ_PAD), row), ANY, pl.BlockSpec((tm, D_MODEL), row),
                  pl.BlockSpec((1, D_MODEL), fix), pl.BlockSpec((tm, D_MODEL), row), ANY],
        out_specs=[pl.BlockSpec((tm, D_MODEL), row), pl.BlockSpec((1, D_MODEL), fix)],
        out_shape=[jax.ShapeDtypeStruct((L, D_MODEL), f32), jax.ShapeDtypeStruct((1, D_MODEL), f32)],
        scratch_shapes=[pltpu.VMEM(wpad.shape, bf16), pltpu.SemaphoreType.DMA((nk,))],
        compiler_params=_cparams("arbitrary"),
    )(dproj, wpad, x, nw, dy, after)


def _adamw_reduce(parts, w, m, v, name, first_row=None):
    R, C = w.shape[0], w.shape[-1]
    n_parts = parts.shape[0]
    tr = 128 if R % 128 == 0 else R
    c1 = 1.0 - ADAM_B1 ** ADAM_STEP
    c2 = 1.0 - ADAM_B2 ** ADAM_STEP
    at = (slice(None), 0, slice(None)) if w.ndim == 3 else Ellipsis
    window = (slice(None), slice(None)) if first_row is None else (slice(first_row, first_row + R), slice(0, C))

    def body(p_ref, w_ref, m_ref, v_ref, g_ref, d_ref, nm_ref, nv_ref):
        g = p_ref[(0,) + window].astype(f32)
        for s in range(1, n_parts):
            g = g + p_ref[(s,) + window].astype(f32)
        nm = ADAM_B1 * m_ref[at] + (1.0 - ADAM_B1) * g
        nv = ADAM_B2 * v_ref[at] + (1.0 - ADAM_B2) * (g * g)
        g_ref[at] = g
        nm_ref[at] = nm
        nv_ref[at] = nv
        d_ref[at] = -ADAM_LR * ((nm / c1) / (jnp.sqrt(nv / c2) + ADAM_EPS) + ADAM_WD * w_ref[at])

    blk = pl.BlockSpec((tr, 1, C), lambda i: (i, 0, 0)) if w.ndim == 3 else pl.BlockSpec((tr, C), lambda i: (i, 0))
    out = jax.ShapeDtypeStruct(w.shape, f32)
    if first_row is None:
        p_spec = pl.BlockSpec((n_parts, tr, C), lambda i: (0, i, 0))
    else:
        assert tr == R
        p_spec = pl.BlockSpec(parts.shape, lambda i: (0, 0, 0))
    return _pcall(
        body, name=name, grid=(R // tr,),
        in_specs=[p_spec, blk, blk, blk],
        out_specs=[blk] * 4, out_shape=[out] * 4,
        compiler_params=_cparams("parallel"),
    )(parts, w, m, v)


SMALL_SLOTS = ((0, D_MODEL), (D_MODEL, D_MODEL), (2 * D_MODEL, D_MODEL), (3 * D_MODEL, LANE),
               (3 * D_MODEL + LANE, HEADS), (3 * D_MODEL + 2 * LANE, HEADS))
SMALL_LOSS = 3 * D_MODEL + 3 * LANE
SMALL_W = SMALL_LOSS + LANE


def _pack_small(gs, after):
    def body(nw_ref, cb_ref, fw_ref, gn_ref, sc_ref, ls_ref, after_ref, o_ref):
        for ref, (start, width) in zip((nw_ref, cb_ref, fw_ref, gn_ref), SMALL_SLOTS[:4]):
            o_ref[:, start:start + width] = ref[...]
        o_ref[:, SMALL_SLOTS[4][0]:SMALL_SLOTS[4][0] + LANE] = sc_ref[0:1, :]
        o_ref[:, SMALL_SLOTS[5][0]:SMALL_SLOTS[5][0] + LANE] = sc_ref[1:2, :]
        o_ref[:, SMALL_LOSS:SMALL_W] = ls_ref[...]

    vm = pl.BlockSpec(memory_space=pltpu.VMEM)
    return _pcall(body, name="pack_small_grads", out_shape=jax.ShapeDtypeStruct((1, SMALL_W), f32),
                  in_specs=[vm] * 6 + [ANY], out_specs=vm)(*gs, after)


def _adamw_small(parts, ws, ms, vs):
    c1 = 1.0 - ADAM_B1 ** ADAM_STEP
    c2 = 1.0 - ADAM_B2 ** ADAM_STEP
    np_ = len(ws)

    def body(*refs):
        p_ref = refs[0]
        w_refs, m_refs, v_refs = refs[1:1 + np_], refs[1 + np_:1 + 2 * np_], refs[1 + 2 * np_:1 + 3 * np_]
        outs = refs[1 + 3 * np_:]
        g_refs, d_refs, nm_refs, nv_refs = (outs[i * np_:(i + 1) * np_] for i in range(4))
        loss_ref = outs[4 * np_]

        def total(start, width):
            t = p_ref[0, :, start:start + width]
            for s in range(1, N_DEV):
                t = t + p_ref[s, :, start:start + width]
            return t

        for i, (start, width) in enumerate(SMALL_SLOTS):
            g = total(start, width)
            nm = ADAM_B1 * m_refs[i][...] + (1.0 - ADAM_B1) * g
            nv = ADAM_B2 * v_refs[i][...] + (1.0 - ADAM_B2) * (g * g)
            g_refs[i][...] = g
            nm_refs[i][...] = nm
            nv_refs[i][...] = nv
            d_refs[i][...] = -ADAM_LR * ((nm / c1) / (jnp.sqrt(nv / c2) + ADAM_EPS) + ADAM_WD * w_refs[i][...])
        loss_ref[...] = total(SMALL_LOSS, LANE)

    vm = pl.BlockSpec(memory_space=pltpu.VMEM)
    shapes = [jax.ShapeDtypeStruct(w.shape, f32) for w in ws]
    res = _pcall(body, name="adamw_small", out_shape=shapes * 4 + [jax.ShapeDtypeStruct((1, LANE), f32)],
                 in_specs=[vm] * (1 + 3 * np_), out_specs=[vm] * (4 * np_ + 1))(parts, *ws, *ms, *vs)
    return [res[i * np_:(i + 1) * np_] for i in range(4)], res[4 * np_]


def _adamw_w_in(part_a, part_b, w3, m3, v3, after):
    _, n, _ = part_a.shape
    c1 = 1.0 - ADAM_B1 ** ADAM_STEP
    c2 = 1.0 - ADAM_B2 ** ADAM_STEP

    def body(pa_ref, pb_ref, w_ref, m_ref, v_ref, after_ref, g_ref, d_ref, nm_ref, nv_ref):
        g = pa_ref[0].astype(f32) + pb_ref[0].astype(f32)
        nm = ADAM_B1 * m_ref[:, 0, :] + (1.0 - ADAM_B1) * g
        nv = ADAM_B2 * v_ref[:, 0, :] + (1.0 - ADAM_B2) * (g * g)
        g_ref[:, 0, :] = g
        nm_ref[:, 0, :] = nm
        nv_ref[:, 0, :] = nv
        d_ref[:, 0, :] = -ADAM_LR * ((nm / c1) / (jnp.sqrt(nv / c2) + ADAM_EPS) + ADAM_WD * w_ref[:, 0, :])

    tile = 2 * COL_TILE
    blk = pl.BlockSpec((n, 1, tile), lambda j: (0, 0, j))
    out = jax.ShapeDtypeStruct((n, 1, D_MODEL), f32)
    return _pcall(
        body, name="adamw_w_in", grid=(D_MODEL // tile,),
        in_specs=[pl.BlockSpec((1, n, tile), lambda j: (0, 0, j))] * 2 + [blk, blk, blk, ANY],
        out_specs=[blk] * 4, out_shape=[out] * 4,
        compiler_params=_cparams("parallel"),
    )(part_a, part_b, w3, m3, v3, after)


def _pad_lanes(vec8, start):
    return jnp.pad(vec8.reshape(1, -1), ((0, 0), (start, LANE - start - vec8.size)))


def kernel(x, norm_in_w, w_in, conv_qkv_w, A_log, dt_bias, gdn_norm_w, conv_w, conv_b, w_out, final_norm_w, loss_target, m_norm_in_w, m_w_in, m_conv_qkv_w, m_A_log, m_dt_bias, m_gdn_norm_w, m_conv_w, m_conv_b, m_w_out, m_final_norm_w, v_norm_in_w, v_w_in, v_conv_qkv_w, v_A_log, v_dt_bias, v_gdn_norm_w, v_conv_w, v_conv_b, v_w_out, v_final_norm_w):
    L = x.shape[1]
    nc = L // CHUNK
    xs = x[0]
    tgt = loss_target[0]
    fnw = final_norm_w.reshape(1, D_MODEL)

    as_rows = lambda a: jnp.transpose(a, (2, 0, 1))
    as_taps = lambda a: jnp.transpose(a, (1, 0, 2))
    win_blk, wo_blk = _cast_weights(as_rows(w_in), w_out[0])
    win_g, cqkv_g, cw_g = _all_gather([win_blk, conv_qkv_w[0], as_taps(conv_w)], "gather_weights",
                                      pieces=[8, 1, 1])
    wpad = _relayout_w_in(win_g)
    cqkv = jnp.concatenate([cqkv_g[d] for d in range(N_DEV)], axis=1)
    cw = jnp.concatenate([cw_g[d][:, 0, :] for d in range(N_DEV)], axis=1)
    alog_p = _pad_lanes(A_log, HEADS)
    dtb_p = _pad_lanes(dt_bias, HEADS)
    tok = lambda started: started[4]
    wo_started = _spread_start(wo_blk, wpad, "gather", "gather_w_out_start")

    proj, h = _in_proj(xs, norm_in_w, wpad, tok(wo_started))
    qkv = _qkv_act(proj, cqkv)
    sc, gr = _scalars(proj, alog_p, dtb_p)
    o, u_all, w_all, vn_all, t_all, sp_all = _gdn_fwd(qkv, sc, gr)
    mix = _conv_fwd(proj, cw, conv_b, _gdn_gate(o, proj, gdn_norm_w))
    wo = _spread_wait(wo_started, mix, "gather", "gather_w_out_wait")[1].reshape(-1, D_MODEL)
    dy, dyb, dmix, g_fnw, loss_v = _out_proj_loss(xs, mix, wo, fnw, tgt)

    g_wout = _tn_matmul(mix, dyb, "grad_w_out")
    gwo_started = _spread_start(g_wout.reshape(N_DEV, -1, D_MODEL), dyb, "scatter", "exchange_grad_w_out_start")
    do, dproj, g_gnw = _gdn_gate_bwd(o, proj, gdn_norm_w, dmix, tok(gwo_started))
    dproj, g_cw, g_cb = _conv_bwd(proj, cw, conv_b, dmix, dproj)
    dqkv_n, dsc, dgr = _gdn_bwd(qkv, sc, gr, u_all, w_all, vn_all, t_all, sp_all, do)
    dproj, g_cqkv = _qkv_bwd(proj, cqkv, dqkv_n, dproj)
    g_cqkv_blk = g_cqkv.reshape(4, N_DEV, -1).transpose(1, 0, 2)
    g_cw_blk = jnp.pad(g_cw.reshape(3, N_DEV, -1).transpose(1, 0, 2),
                       ((0, 0), (0, 1), (0, g_cqkv_blk.shape[2] - g_cw.shape[1] // N_DEV)))
    gsm_started = _spread_start(jnp.concatenate([g_cqkv_blk, g_cw_blk], axis=1), g_cqkv, "scatter",
                                "exchange_small_sharded_grads_start")
    dgr_col = jnp.pad(dgr.transpose(0, 2, 1).reshape(L, HEADS), ((0, 0), (HEADS, LANE - 2 * HEADS)))
    dproj, g_sc = _scalars_bwd(proj, alog_p, dtb_p, dsc, dgr_col, dproj, tok(gsm_started))
    g_win_blk = _grad_blocks(_tn_matmul(dproj, h, "grad_w_in"))

    (p_win,) = _pair_exchange([g_win_blk], "exchange_grads_pair")
    r_small = _spread_wait(gsm_started, p_win, "scatter", "exchange_small_sharded_grads_wait")[1]
    s_win = _pair_sum(g_win_blk, p_win, "pair_sum_w_in")
    gw1_started = _spread_start(s_win, r_small, "axis_a", "exchange_grads_axis1_start")
    grad_x, g_nw = _input_grad(dproj, wpad, xs, norm_in_w, dy, tok(gw1_started))
    s_thru, got1 = _spread_wait(gw1_started, grad_x, "axis_a", "exchange_grads_axis1_wait")
    t_win = _axis_sum(s_thru, got1, "axis_sum_w_in")
    gw2_started = _spread_start(t_win, got1, "axis_b", "exchange_grads_axis2_start")

    r_wout = _spread_wait(gwo_started, tok(gw2_started), "scatter", "exchange_grad_w_out_wait")[1]
    upd_wout =_adamw_reduce(r_wout, w_out[0], m_w_out[0], v_w_out[0], "adamw_w_out")
    upd_cqkv = _adamw_reduce(r_small, conv_qkv_w[0], m_conv_qkv_w[0], v_conv_qkv_w[0], "adamw_conv_qkv_w", first_row=0)
    upd_cw = _adamw_reduce(r_small, as_taps(conv_w), as_taps(m_conv_w), as_taps(v_conv_w), "adamw_conv_w", first_row=4)

    t_thru, got2 = _spread_wait(gw2_started, upd_cw[0], "axis_b", "exchange_grads_axis2_wait")

    small_g = _pack_small([g_nw, g_cb, g_fnw, g_gnw, g_sc, loss_v], got2)
    gsg_started = _spread_start(small_g, got2, "gather", "gather_small_grads_start")
    upd_win_t = _adamw_w_in(t_thru, got2, as_rows(w_in), as_rows(m_w_in), as_rows(v_w_in), tok(gsg_started))
    upd_win = [jnp.transpose(a, (1, 2, 0)) for a in upd_win_t]
    small_all = _spread_wait(gsg_started, upd_win_t[0], "gather", "gather_small_grads_wait")[1]
    fvec = lambda a: a.reshape(1, D_MODEL)
    upd_small, loss_sum = _adamw_small(
        small_all,
        [norm_in_w, conv_b, fvec(final_norm_w), gdn_norm_w, A_log, dt_bias],
        [m_norm_in_w, m_conv_b, fvec(m_final_norm_w), m_gdn_norm_w, m_A_log, m_dt_bias],
        [v_norm_in_w, v_conv_b, fvec(v_final_norm_w), v_gdn_norm_w, v_A_log, v_dt_bias])

    outs = [loss_sum[0, 0], grad_x[None]]
    for k in range(4):
        nw_k, cb_k, fw_k, gn_k, al_k, dt_k = upd_small[k]
        outs += [nw_k, upd_win[k], upd_cqkv[k][None], al_k, dt_k, gn_k,
                 as_taps(upd_cw[k]), cb_k, upd_wout[k][None], fw_k.reshape(D_MODEL)]
    return tuple(outs)
```
